```python
import math
import jax, jax.numpy as jnp
from jax import lax
import numpy as np

D_MODEL = 1024
BATCH = 8
SEQ = 4096
DEPTH = 2

N_META = 16
HEAD_DIM = 64
N_Q_HEADS = 8
N_KV_HEADS = 2
Q_PER_KV = N_Q_HEADS // N_KV_HEADS
ATTN_WIDTH = N_Q_HEADS * HEAD_DIM
KV_WIDTH = N_KV_HEADS * HEAD_DIM
WINDOW = 128
BLOCK = 128
ROPE_THETA = 500000.0
ROT_DIM = HEAD_DIM // 4
SSM_WIDTH = D_MODEL // 2
SSM_GROUP = 16
SSM_GROUPS = SSM_WIDTH // SSM_GROUP
SSM_STATE = 64
N_BRANCH = 2
D_FF = 2816
IN_WIDTH = ATTN_WIDTH + 2 * KV_WIDTH + SSM_WIDTH + N_BRANCH * D_MODEL
EPS = 1e-6
NEG_INF = -1e30

kernel_name = "hybrid_swa_s5_gated_macaron"


def rmsnorm(x, g):
    x32 = x.astype(jnp.float32)
    r = x32 * lax.rsqrt(jnp.mean(x32 * x32, axis=-1, keepdims=True) + EPS)
    return (r * g.astype(jnp.float32)).astype(x.dtype)


def swiglu(h, w_gate, w_up, w_down):
    return (jax.nn.silu(h @ w_gate) * (h @ w_up)) @ w_down


def rope_tables(l):
    pos = jnp.arange(l, dtype=jnp.float32)
    inv_freq = ROPE_THETA ** (-jnp.arange(0, ROT_DIM, 2, dtype=jnp.float32) / ROT_DIM)
    ang = pos[:, None] * inv_freq[None, :]
    return jnp.cos(ang)[None, :, None, :], jnp.sin(ang)[None, :, None, :]


def apply_partial_rope(x, cos, sin):
    x32 = x.astype(jnp.float32)
    half = ROT_DIM // 2
    x1 = x32[..., :half]
    x2 = x32[..., half:ROT_DIM]
    rest = x32[..., ROT_DIM:]
    out = jnp.concatenate([x1 * cos - x2 * sin, x2 * cos + x1 * sin, rest], axis=-1)
    return out.astype(x.dtype)


def sliding_window_attention(q, k, v, sinks):
    b, l = q.shape[0], q.shape[1]
    pad_front = (-N_META) % BLOCK
    pad_back = (-(l + pad_front)) % BLOCK
    lp = l + pad_front + pad_back
    nb = lp // BLOCK
    padw = ((0, 0), (pad_front, pad_back), (0, 0), (0, 0))
    qb = jnp.pad(q, padw).reshape(b, nb, BLOCK, N_KV_HEADS, Q_PER_KV, HEAD_DIM)
    kb = jnp.pad(k, padw).reshape(b, nb, BLOCK, N_KV_HEADS, HEAD_DIM)
    vb = jnp.pad(v, padw).reshape(b, nb, BLOCK, N_KV_HEADS, HEAD_DIM)
    k_prev = jnp.pad(kb, ((0, 0), (1, 0), (0, 0), (0, 0), (0, 0)))[:, :-1]
    v_prev = jnp.pad(vb, ((0, 0), (1, 0), (0, 0), (0, 0), (0, 0)))[:, :-1]
    k_band = jnp.concatenate([k_prev, kb], axis=2)
    v_band = jnp.concatenate([v_prev, vb], axis=2)
    k_meta = k[:, :N_META]
    v_meta = v[:, :N_META]

    pos_q = (jnp.arange(lp) - pad_front).reshape(nb, BLOCK)
    pos_k = jnp.concatenate([pos_q - BLOCK, pos_q], axis=1)
    dist = pos_q[:, :, None] - pos_k[:, None, :]
    band_mask = (pos_k[:, None, :] >= N_META) & (dist >= 0) & (dist < WINDOW)
    meta_mask = jnp.arange(N_META)[None, None, :] <= pos_q[:, :, None]

    scale = HEAD_DIM ** -0.5
    s_band = jnp.einsum('bnqhgd,bnkhd->bnhgqk', qb, k_band).astype(jnp.float32) * scale
    s_meta = jnp.einsum('bnqhgd,bmhd->bnhgqm', qb, k_meta).astype(jnp.float32) * scale
    s_band = jnp.where(band_mask[None, :, None, None], s_band, NEG_INF)
    s_meta = jnp.where(meta_mask[None, :, None, None], s_meta, NEG_INF)
    sink = jnp.broadcast_to(
        sinks.astype(jnp.float32).reshape(N_KV_HEADS, Q_PER_KV)[None, None, :, :, None, None],
        s_band.shape[:-1] + (1,))
    probs = jax.nn.softmax(jnp.concatenate([s_band, s_meta, sink], axis=-1), axis=-1)
    p_band = probs[..., :2 * BLOCK].astype(v.dtype)
    p_meta = probs[..., 2 * BLOCK:2 * BLOCK + N_META].astype(v.dtype)
    out = (jnp.einsum('bnhgqk,bnkhd->bnqhgd', p_band, v_band)
           + jnp.einsum('bnhgqm,bmhd->bnqhgd', p_meta, v_meta))
    out = out.reshape(b, lp, ATTN_WIDTH)
    return out[:, pad_front:pad_front + l]


def s5_ssm(u, a_re, a_im, log_dt, b_re, b_im, c_re, c_im, d_skip):
    bsz, l = u.shape[0], u.shape[1]
    u32 = u.astype(jnp.float32).reshape(bsz, l, SSM_GROUPS, SSM_GROUP)
    dt = jnp.exp(log_dt.astype(jnp.float32))[:, None]
    ar = a_re.astype(jnp.float32)
    ai = a_im.astype(jnp.float32)
    mag = jnp.exp(ar * dt)
    lb_re = mag * jnp.cos(ai * dt)
    lb_im = mag * jnp.sin(ai * dt)
    den = ar * ar + ai * ai
    num_re = lb_re - 1.0
    coef_re = (num_re * ar + lb_im * ai) / den
    coef_im = (lb_im * ar - num_re * ai) / den
    br = b_re.astype(jnp.float32)
    bi = b_im.astype(jnp.float32)
    bb_re = coef_re[..., None] * br - coef_im[..., None] * bi
    bb_im = coef_re[..., None] * bi + coef_im[..., None] * br
    bu_re = jnp.einsum('blgc,gpc->blgp', u32, bb_re)
    bu_im = jnp.einsum('blgc,gpc->blgp', u32, bb_im)
    la_re = jnp.broadcast_to(lb_re, bu_re.shape)
    la_im = jnp.broadcast_to(lb_im, bu_im.shape)

    def combine(e1, e2):
        a1r, a1i, b1r, b1i = e1
        a2r, a2i, b2r, b2i = e2
        return (a1r * a2r - a1i * a2i,
                a1r * a2i + a1i * a2r,
                a2r * b1r - a2i * b1i + b2r,
                a2r * b1i + a2i * b1r + b2i)

    _, _, h_re, h_im = lax.associative_scan(combine, (la_re, la_im, bu_re, bu_im), axis=1)
    y = (jnp.einsum('blgp,gcp->blgc', h_re, c_re.astype(jnp.float32))
         - jnp.einsum('blgp,gcp->blgc', h_im, c_im.astype(jnp.float32)))
    y = y.reshape(bsz, l, SSM_WIDTH) + d_skip.astype(jnp.float32) * u32.reshape(bsz, l, SSM_WIDTH)
    return y.astype(u.dtype)


def hybrid_mixer(h, cos, sin, w_in, attn_sinks, ssm_a_re, ssm_a_im, ssm_log_dt,
                 ssm_b_re, ssm_b_im, ssm_c_re, ssm_c_im, ssm_d,
                 w_attn_proj, w_glu_v, w_glu_g, w_out):
    bsz, l = h.shape[0], h.shape[1]
    z = h @ w_in
    o1 = ATTN_WIDTH
    o2 = o1 + KV_WIDTH
    o3 = o2 + KV_WIDTH
    o4 = o3 + SSM_WIDTH
    o5 = o4 + D_MODEL
    q = z[..., :o1].reshape(bsz, l, N_Q_HEADS, HEAD_DIM)
    k = z[..., o1:o2].reshape(bsz, l, N_KV_HEADS, HEAD_DIM)
    v = z[..., o2:o3].reshape(bsz, l, N_KV_HEADS, HEAD_DIM)
    u = z[..., o3:o4]
    g_attn = z[..., o4:o5]
    g_ssm = z[..., o5:]
    q = apply_partial_rope(q, cos, sin)
    k = apply_partial_rope(k, cos, sin)
    attn = sliding_window_attention(q, k, v, attn_sinks) @ w_attn_proj
    y = jax.nn.gelu(s5_ssm(u, ssm_a_re, ssm_a_im, ssm_log_dt, ssm_b_re, ssm_b_im,
                           ssm_c_re, ssm_c_im, ssm_d))
    ssm = (y @ w_glu_v) * jax.nn.sigmoid(y @ w_glu_g)
    merged = jax.nn.sigmoid(g_attn) * attn + jax.nn.sigmoid(g_ssm) * ssm
    return merged @ w_out


def _fwd_setup_inputs(seed: int = 0) -> dict:
    key = jax.random.key(seed)
    ks = jax.random.split(key, 40)
    f32 = jnp.float32

    def nrm(k, shape, fan_in):
        return jax.random.normal(k, shape, f32) * (fan_in ** -0.5)

    def gain(k, shape):
        return 1.0 + 0.01 * jax.random.normal(k, shape, f32)

    L, G, P, C = DEPTH, SSM_GROUPS, SSM_STATE, SSM_GROUP
    a_re = -0.5 * jnp.exp(0.01 * jax.random.normal(ks[20], (L, G, P), f32))
    a_im = jnp.broadcast_to(jnp.pi * jnp.arange(P, dtype=f32), (L, G, P)) \
        + 0.001 * jax.random.normal(ks[21], (L, G, P), f32)
    log_dt = jax.random.uniform(ks[22], (L, G), f32, math.log(0.001), math.log(0.1))
    return {
        "x": jax.random.normal(ks[0], (BATCH, SEQ, D_MODEL), f32),
        "meta_tokens": jax.random.normal(ks[1], (N_META, D_MODEL), f32),
        "ffn1_norm": gain(ks[2], (L, D_MODEL)),
        "ffn1_w_gate": nrm(ks[3], (L, D_MODEL, D_FF), D_MODEL),
        "ffn1_w_up": nrm(ks[4], (L, D_MODEL, D_FF), D_MODEL),
        "ffn1_w_down": nrm(ks[5], (L, D_FF, D_MODEL), D_FF),
        "mix_norm": gain(ks[6], (L, D_MODEL)),
        "w_in": nrm(ks[7], (L, D_MODEL, IN_WIDTH), D_MODEL),
        "attn_sinks": 0.5 * jax.random.normal(ks[8], (L, N_Q_HEADS), f32),
        "ssm_a_re": a_re,
        "ssm_a_im": a_im,
        "ssm_log_dt": log_dt,
        "ssm_b_re": nrm(ks[9], (L, G, P, C), 2 * C),
        "ssm_b_im": nrm(ks[10], (L, G, P, C), 2 * C),
        "ssm_c_re": nrm(ks[11], (L, G, C, P), 2 * P),
        "ssm_c_im": nrm(ks[12], (L, G, C, P), 2 * P),
        "ssm_d": jax.random.normal(ks[13], (L, SSM_WIDTH), f32),
        "w_attn_proj": nrm(ks[14], (L, ATTN_WIDTH, D_MODEL), ATTN_WIDTH),
        "w_glu_v": nrm(ks[15], (L, SSM_WIDTH, D_MODEL), SSM_WIDTH),
        "w_glu_g": nrm(ks[16], (L, SSM_WIDTH, D_MODEL), SSM_WIDTH),
        "w_out": nrm(ks[17], (L, D_MODEL, D_MODEL), D_MODEL),
        "ffn2_norm": gain(ks[18], (L, D_MODEL)),
        "ffn2_w_gate": nrm(ks[19], (L, D_MODEL, D_FF), D_MODEL),
        "ffn2_w_up": nrm(ks[23], (L, D_MODEL, D_FF), D_MODEL),
        "ffn2_w_down": nrm(ks[24], (L, D_FF, D_MODEL), D_FF),
        "final_norm": gain(ks[25], (D_MODEL,)),
    }


def _fwd_reference(x, meta_tokens, ffn1_norm, ffn1_w_gate, ffn1_w_up, ffn1_w_down,
              mix_norm, w_in, attn_sinks, ssm_a_re, ssm_a_im, ssm_log_dt,
              ssm_b_re, ssm_b_im, ssm_c_re, ssm_c_im, ssm_d,
              w_attn_proj, w_glu_v, w_glu_g, w_out,
              ffn2_norm, ffn2_w_gate, ffn2_w_up, ffn2_w_down, final_norm):
    bsz, seq = x.shape[0], x.shape[1]
    meta = jnp.broadcast_to(meta_tokens.astype(x.dtype)[None], (bsz, N_META, D_MODEL))
    h = jnp.concatenate([meta, x], axis=1)
    cos, sin = rope_tables(seq + N_META)
    for i in range(DEPTH):
        h = h + 0.5 * swiglu(rmsnorm(h, ffn1_norm[i]), ffn1_w_gate[i], ffn1_w_up[i], ffn1_w_down[i])
        h = h + hybrid_mixer(rmsnorm(h, mix_norm[i]), cos, sin, w_in[i], attn_sinks[i],
                             ssm_a_re[i], ssm_a_im[i], ssm_log_dt[i],
                             ssm_b_re[i], ssm_b_im[i], ssm_c_re[i], ssm_c_im[i], ssm_d[i],
                             w_attn_proj[i], w_glu_v[i], w_glu_g[i], w_out[i])
        h = h + 0.5 * swiglu(rmsnorm(h, ffn2_norm[i]), ffn2_w_gate[i], ffn2_w_up[i], ffn2_w_down[i])
    h = rmsnorm(h, final_norm)
    return h[:, N_META:]


import jax as _jax
import jax.numpy as _jnp

TWIN_FORMAT = 'train_step'
FWD_PARAMS = ['x', 'meta_tokens', 'ffn1_norm', 'ffn1_w_gate', 'ffn1_w_up', 'ffn1_w_down', 'mix_norm', 'w_in', 'attn_sinks', 'ssm_a_re', 'ssm_a_im', 'ssm_log_dt', 'ssm_b_re', 'ssm_b_im', 'ssm_c_re', 'ssm_c_im', 'ssm_d', 'w_attn_proj', 'w_glu_v', 'w_glu_g', 'w_out', 'ffn2_norm', 'ffn2_w_gate', 'ffn2_w_up', 'ffn2_w_down', 'final_norm']
TWIN_WEIGHTS = ['meta_tokens', 'ffn1_norm', 'ffn1_w_gate', 'ffn1_w_up', 'ffn1_w_down', 'mix_norm', 'w_in', 'attn_sinks', 'ssm_a_re', 'ssm_a_im', 'ssm_log_dt', 'ssm_b_re', 'ssm_b_im', 'ssm_c_re', 'ssm_c_im', 'ssm_d', 'w_attn_proj', 'w_glu_v', 'w_glu_g', 'w_out', 'ffn2_norm', 'ffn2_w_gate', 'ffn2_w_up', 'ffn2_w_down', 'final_norm']
TWIN_DIFF_INPUT = 'x'
TWIN_INPUTS = ['x', 'meta_tokens', 'ffn1_norm', 'ffn1_w_gate', 'ffn1_w_up', 'ffn1_w_down', 'mix_norm', 'w_in', 'attn_sinks', 'ssm_a_re', 'ssm_a_im', 'ssm_log_dt', 'ssm_b_re', 'ssm_b_im', 'ssm_c_re', 'ssm_c_im', 'ssm_d', 'w_attn_proj', 'w_glu_v', 'w_glu_g', 'w_out', 'ffn2_norm', 'ffn2_w_gate', 'ffn2_w_up', 'ffn2_w_down', 'final_norm', 'loss_target', 'm_meta_tokens', 'm_ffn1_norm', 'm_ffn1_w_gate', 'm_ffn1_w_up', 'm_ffn1_w_down', 'm_mix_norm', 'm_w_in', 'm_attn_sinks', 'm_ssm_a_re', 'm_ssm_a_im', 'm_ssm_log_dt', 'm_ssm_b_re', 'm_ssm_b_im', 'm_ssm_c_re', 'm_ssm_c_im', 'm_ssm_d', 'm_w_attn_proj', 'm_w_glu_v', 'm_w_glu_g', 'm_w_out', 'm_ffn2_norm', 'm_ffn2_w_gate', 'm_ffn2_w_up', 'm_ffn2_w_down', 'm_final_norm', 'v_meta_tokens', 'v_ffn1_norm', 'v_ffn1_w_gate', 'v_ffn1_w_up', 'v_ffn1_w_down', 'v_mix_norm', 'v_w_in', 'v_attn_sinks', 'v_ssm_a_re', 'v_ssm_a_im', 'v_ssm_log_dt', 'v_ssm_b_re', 'v_ssm_b_im', 'v_ssm_c_re', 'v_ssm_c_im', 'v_ssm_d', 'v_w_attn_proj', 'v_w_glu_v', 'v_w_glu_g', 'v_w_out', 'v_ffn2_norm', 'v_ffn2_w_gate', 'v_ffn2_w_up', 'v_ffn2_w_down', 'v_final_norm']
TWIN_OUTPUTS = ['loss', 'grad_x', 'grad_meta_tokens', 'grad_ffn1_norm', 'grad_ffn1_w_gate', 'grad_ffn1_w_up', 'grad_ffn1_w_down', 'grad_mix_norm', 'grad_w_in', 'grad_attn_sinks', 'grad_ssm_a_re', 'grad_ssm_a_im', 'grad_ssm_log_dt', 'grad_ssm_b_re', 'grad_ssm_b_im', 'grad_ssm_c_re', 'grad_ssm_c_im', 'grad_ssm_d', 'grad_w_attn_proj', 'grad_w_glu_v', 'grad_w_glu_g', 'grad_w_out', 'grad_ffn2_norm', 'grad_ffn2_w_gate', 'grad_ffn2_w_up', 'grad_ffn2_w_down', 'grad_final_norm', 'delta_meta_tokens', 'delta_ffn1_norm', 'delta_ffn1_w_gate', 'delta_ffn1_w_up', 'delta_ffn1_w_down', 'delta_mix_norm', 'delta_w_in', 'delta_attn_sinks', 'delta_ssm_a_re', 'delta_ssm_a_im', 'delta_ssm_log_dt', 'delta_ssm_b_re', 'delta_ssm_b_im', 'delta_ssm_c_re', 'delta_ssm_c_im', 'delta_ssm_d', 'delta_w_attn_proj', 'delta_w_glu_v', 'delta_w_glu_g', 'delta_w_out', 'delta_ffn2_norm', 'delta_ffn2_w_gate', 'delta_ffn2_w_up', 'delta_ffn2_w_down', 'delta_final_norm', 'new_m_meta_tokens', 'new_m_ffn1_norm', 'new_m_ffn1_w_gate', 'new_m_ffn1_w_up', 'new_m_ffn1_w_down', 'new_m_mix_norm', 'new_m_w_in', 'new_m_attn_sinks', 'new_m_ssm_a_re', 'new_m_ssm_a_im', 'new_m_ssm_log_dt', 'new_m_ssm_b_re', 'new_m_ssm_b_im', 'new_m_ssm_c_re', 'new_m_ssm_c_im', 'new_m_ssm_d', 'new_m_w_attn_proj', 'new_m_w_glu_v', 'new_m_w_glu_g', 'new_m_w_out', 'new_m_ffn2_norm', 'new_m_ffn2_w_gate', 'new_m_ffn2_w_up', 'new_m_ffn2_w_down', 'new_m_final_norm', 'new_v_meta_tokens', 'new_v_ffn1_norm', 'new_v_ffn1_w_gate', 'new_v_ffn1_w_up', 'new_v_ffn1_w_down', 'new_v_mix_norm', 'new_v_w_in', 'new_v_attn_sinks', 'new_v_ssm_a_re', 'new_v_ssm_a_im', 'new_v_ssm_log_dt', 'new_v_ssm_b_re', 'new_v_ssm_b_im', 'new_v_ssm_c_re', 'new_v_ssm_c_im', 'new_v_ssm_d', 'new_v_w_attn_proj', 'new_v_w_glu_v', 'new_v_w_glu_g', 'new_v_w_out', 'new_v_ffn2_norm', 'new_v_ffn2_w_gate', 'new_v_ffn2_w_up', 'new_v_ffn2_w_down', 'new_v_final_norm']
TWIN_LEAF_KINDS = {'loss': 'loss', 'grad_x': 'grad_x', 'grad_meta_tokens': 'grad_w', 'grad_ffn1_norm': 'grad_w', 'grad_ffn1_w_gate': 'grad_w', 'grad_ffn1_w_up': 'grad_w', 'grad_ffn1_w_down': 'grad_w', 'grad_mix_norm': 'grad_w', 'grad_w_in': 'grad_w', 'grad_attn_sinks': 'grad_w', 'grad_ssm_a_re': 'grad_w', 'grad_ssm_a_im': 'grad_w', 'grad_ssm_log_dt': 'grad_w', 'grad_ssm_b_re': 'grad_w', 'grad_ssm_b_im': 'grad_w', 'grad_ssm_c_re': 'grad_w', 'grad_ssm_c_im': 'grad_w', 'grad_ssm_d': 'grad_w', 'grad_w_attn_proj': 'grad_w', 'grad_w_glu_v': 'grad_w', 'grad_w_glu_g': 'grad_w', 'grad_w_out': 'grad_w', 'grad_ffn2_norm': 'grad_w', 'grad_ffn2_w_gate': 'grad_w', 'grad_ffn2_w_up': 'grad_w', 'grad_ffn2_w_down': 'grad_w', 'grad_final_norm': 'grad_w', 'delta_meta_tokens': 'delta_w', 'delta_ffn1_norm': 'delta_w', 'delta_ffn1_w_gate': 'delta_w', 'delta_ffn1_w_up': 'delta_w', 'delta_ffn1_w_down': 'delta_w', 'delta_mix_norm': 'delta_w', 'delta_w_in': 'delta_w', 'delta_attn_sinks': 'delta_w', 'delta_ssm_a_re': 'delta_w', 'delta_ssm_a_im': 'delta_w', 'delta_ssm_log_dt': 'delta_w', 'delta_ssm_b_re': 'delta_w', 'delta_ssm_b_im': 'delta_w', 'delta_ssm_c_re': 'delta_w', 'delta_ssm_c_im': 'delta_w', 'delta_ssm_d': 'delta_w', 'delta_w_attn_proj': 'delta_w', 'delta_w_glu_v': 'delta_w', 'delta_w_glu_g': 'delta_w', 'delta_w_out': 'delta_w', 'delta_ffn2_norm': 'delta_w', 'delta_ffn2_w_gate': 'delta_w', 'delta_ffn2_w_up': 'delta_w', 'delta_ffn2_w_down': 'delta_w', 'delta_final_norm': 'delta_w', 'new_m_meta_tokens': 'new_m', 'new_m_ffn1_norm': 'new_m', 'new_m_ffn1_w_gate': 'new_m', 'new_m_ffn1_w_up': 'new_m', 'new_m_ffn1_w_down': 'new_m', 'new_m_mix_norm': 'new_m', 'new_m_w_in': 'new_m', 'new_m_attn_sinks': 'new_m', 'new_m_ssm_a_re': 'new_m', 'new_m_ssm_a_im': 'new_m', 'new_m_ssm_log_dt': 'new_m', 'new_m_ssm_b_re': 'new_m', 'new_m_ssm_b_im': 'new_m', 'new_m_ssm_c_re': 'new_m', 'new_m_ssm_c_im': 'new_m', 'new_m_ssm_d': 'new_m', 'new_m_w_attn_proj': 'new_m', 'new_m_w_glu_v': 'new_m', 'new_m_w_glu_g': 'new_m', 'new_m_w_out': 'new_m', 'new_m_ffn2_norm': 'new_m', 'new_m_ffn2_w_gate': 'new_m', 'new_m_ffn2_w_up': 'new_m', 'new_m_ffn2_w_down': 'new_m', 'new_m_final_norm': 'new_m', 'new_v_meta_tokens': 'new_v', 'new_v_ffn1_norm': 'new_v', 'new_v_ffn1_w_gate': 'new_v', 'new_v_ffn1_w_up': 'new_v', 'new_v_ffn1_w_down': 'new_v', 'new_v_mix_norm': 'new_v', 'new_v_w_in': 'new_v', 'new_v_attn_sinks': 'new_v', 'new_v_ssm_a_re': 'new_v', 'new_v_ssm_a_im': 'new_v', 'new_v_ssm_log_dt': 'new_v', 'new_v_ssm_b_re': 'new_v', 'new_v_ssm_b_im': 'new_v', 'new_v_ssm_c_re': 'new_v', 'new_v_ssm_c_im': 'new_v', 'new_v_ssm_d': 'new_v', 'new_v_w_attn_proj': 'new_v', 'new_v_w_glu_v': 'new_v', 'new_v_w_glu_g': 'new_v', 'new_v_w_out': 'new_v', 'new_v_ffn2_norm': 'new_v', 'new_v_ffn2_w_gate': 'new_v', 'new_v_ffn2_w_up': 'new_v', 'new_v_ffn2_w_down': 'new_v', 'new_v_final_norm': 'new_v'}


def _forward(args):
    return _fwd_reference(*[args[k] for k in FWD_PARAMS])


def _output_shape():
    out = _jax.eval_shape(lambda: _forward(_fwd_setup_inputs(0)))
    return out.shape, out.dtype

N_MICROBATCH = 1
ADAM_LR = 0.001
ADAM_B1 = 0.9
ADAM_B2 = 0.999
ADAM_EPS = 1e-08
ADAM_WD = 0.01
ADAM_STEP = 10
PER_EXAMPLE_BATCH_AXIS = {'x': 0, 'loss_target': 0}
SHARED_INPUTS = []
_WEIGHT_DTYPES = {'meta_tokens': _jnp.float32, 'ffn1_norm': _jnp.float32, 'ffn1_w_gate': _jnp.float32, 'ffn1_w_up': _jnp.float32, 'ffn1_w_down': _jnp.float32, 'mix_norm': _jnp.float32, 'w_in': _jnp.float32, 'attn_sinks': _jnp.float32, 'ssm_a_re': _jnp.float32, 'ssm_a_im': _jnp.float32, 'ssm_log_dt': _jnp.float32, 'ssm_b_re': _jnp.float32, 'ssm_b_im': _jnp.float32, 'ssm_c_re': _jnp.float32, 'ssm_c_im': _jnp.float32, 'ssm_d': _jnp.float32, 'w_attn_proj': _jnp.float32, 'w_glu_v': _jnp.float32, 'w_glu_g': _jnp.float32, 'w_out': _jnp.float32, 'ffn2_norm': _jnp.float32, 'ffn2_w_gate': _jnp.float32, 'ffn2_w_up': _jnp.float32, 'ffn2_w_down': _jnp.float32, 'final_norm': _jnp.float32}
MOMENT_SCALE = {'meta_tokens': 4.613650e-03, 'ffn1_norm': 7.635327e-02, 'ffn1_w_gate': 3.302127e-02, 'ffn1_w_up': 3.200908e-02, 'ffn1_w_down': 5.288118e-02, 'mix_norm': 4.717569e-02, 'w_in': 2.650798e-02, 'attn_sinks': 3.926438e-03, 'ssm_a_re': 2.460550e-03, 'ssm_a_im': 2.399164e-03, 'ssm_log_dt': 1.948702e+00, 'ssm_b_re': 1.586070e-03, 'ssm_b_im': 1.576880e-03, 'ssm_c_re': 3.135826e-03, 'ssm_c_im': 3.190225e-03, 'ssm_d': 5.474646e-02, 'w_attn_proj': 1.740582e-02, 'w_glu_v': 3.498771e-02, 'w_glu_g': 1.007316e-02, 'w_out': 3.796941e-02, 'ffn2_norm': 7.008397e-02, 'ffn2_w_gate': 2.977214e-02, 'ffn2_w_up': 2.883308e-02, 'ffn2_w_down': 4.780614e-02, 'final_norm': 3.194293e+01}


def _to_microbatches(a, axis):
    t = _jnp.moveaxis(a, axis, 0)
    t = t.reshape((N_MICROBATCH, t.shape[0] // N_MICROBATCH) + t.shape[1:])
    return _jnp.moveaxis(t, 1, axis + 1)


def setup_inputs(seed: int = 0) -> dict:
    inp = _fwd_setup_inputs(seed)
    key = _jax.random.fold_in(_jax.random.key(seed), 7919)
    shape, _ = _output_shape()
    out = dict(inp)
    out["loss_target"] = _jax.random.normal(_jax.random.fold_in(key, 0), shape, _jnp.float32)
    for i, name in enumerate(TWIN_WEIGHTS):
        w = inp[name].astype(_jnp.float32)
        if MOMENT_SCALE is None:
            s = _jnp.sqrt(_jnp.mean(_jnp.square(w)) + 1e-30)
        else:
            s = MOMENT_SCALE[name]
        km, kv = _jax.random.split(_jax.random.fold_in(key, i + 1))
        out[name] = w
        out["m_" + name] = s * _jax.random.normal(km, w.shape, _jnp.float32)
        out["v_" + name] = (s * s) * _jax.random.uniform(kv, w.shape, _jnp.float32, 0.5, 1.5)
    if N_MICROBATCH > 1:
        for name, axis in PER_EXAMPLE_BATCH_AXIS.items():
            out[name] = _to_microbatches(out[name], axis)
    return {'x': out['x'], 'meta_tokens': out['meta_tokens'], 'ffn1_norm': out['ffn1_norm'], 'ffn1_w_gate': out['ffn1_w_gate'], 'ffn1_w_up': out['ffn1_w_up'], 'ffn1_w_down': out['ffn1_w_down'], 'mix_norm': out['mix_norm'], 'w_in': out['w_in'], 'attn_sinks': out['attn_sinks'], 'ssm_a_re': out['ssm_a_re'], 'ssm_a_im': out['ssm_a_im'], 'ssm_log_dt': out['ssm_log_dt'], 'ssm_b_re': out['ssm_b_re'], 'ssm_b_im': out['ssm_b_im'], 'ssm_c_re': out['ssm_c_re'], 'ssm_c_im': out['ssm_c_im'], 'ssm_d': out['ssm_d'], 'w_attn_proj': out['w_attn_proj'], 'w_glu_v': out['w_glu_v'], 'w_glu_g': out['w_glu_g'], 'w_out': out['w_out'], 'ffn2_norm': out['ffn2_norm'], 'ffn2_w_gate': out['ffn2_w_gate'], 'ffn2_w_up': out['ffn2_w_up'], 'ffn2_w_down': out['ffn2_w_down'], 'final_norm': out['final_norm'], 'loss_target': out['loss_target'], 'm_meta_tokens': out['m_meta_tokens'], 'm_ffn1_norm': out['m_ffn1_norm'], 'm_ffn1_w_gate': out['m_ffn1_w_gate'], 'm_ffn1_w_up': out['m_ffn1_w_up'], 'm_ffn1_w_down': out['m_ffn1_w_down'], 'm_mix_norm': out['m_mix_norm'], 'm_w_in': out['m_w_in'], 'm_attn_sinks': out['m_attn_sinks'], 'm_ssm_a_re': out['m_ssm_a_re'], 'm_ssm_a_im': out['m_ssm_a_im'], 'm_ssm_log_dt': out['m_ssm_log_dt'], 'm_ssm_b_re': out['m_ssm_b_re'], 'm_ssm_b_im': out['m_ssm_b_im'], 'm_ssm_c_re': out['m_ssm_c_re'], 'm_ssm_c_im': out['m_ssm_c_im'], 'm_ssm_d': out['m_ssm_d'], 'm_w_attn_proj': out['m_w_attn_proj'], 'm_w_glu_v': out['m_w_glu_v'], 'm_w_glu_g': out['m_w_glu_g'], 'm_w_out': out['m_w_out'], 'm_ffn2_norm': out['m_ffn2_norm'], 'm_ffn2_w_gate': out['m_ffn2_w_gate'], 'm_ffn2_w_up': out['m_ffn2_w_up'], 'm_ffn2_w_down': out['m_ffn2_w_down'], 'm_final_norm': out['m_final_norm'], 'v_meta_tokens': out['v_meta_tokens'], 'v_ffn1_norm': out['v_ffn1_norm'], 'v_ffn1_w_gate': out['v_ffn1_w_gate'], 'v_ffn1_w_up': out['v_ffn1_w_up'], 'v_ffn1_w_down': out['v_ffn1_w_down'], 'v_mix_norm': out['v_mix_norm'], 'v_w_in': out['v_w_in'], 'v_attn_sinks': out['v_attn_sinks'], 'v_ssm_a_re': out['v_ssm_a_re'], 'v_ssm_a_im': out['v_ssm_a_im'], 'v_ssm_log_dt': out['v_ssm_log_dt'], 'v_ssm_b_re': out['v_ssm_b_re'], 'v_ssm_b_im': out['v_ssm_b_im'], 'v_ssm_c_re': out['v_ssm_c_re'], 'v_ssm_c_im': out['v_ssm_c_im'], 'v_ssm_d': out['v_ssm_d'], 'v_w_attn_proj': out['v_w_attn_proj'], 'v_w_glu_v': out['v_w_glu_v'], 'v_w_glu_g': out['v_w_glu_g'], 'v_w_out': out['v_w_out'], 'v_ffn2_norm': out['v_ffn2_norm'], 'v_ffn2_w_gate': out['v_ffn2_w_gate'], 'v_ffn2_w_up': out['v_ffn2_w_up'], 'v_ffn2_w_down': out['v_ffn2_w_down'], 'v_final_norm': out['v_final_norm']}


def _loss(weights, diff, rest, loss_target):
    with _jax.named_scope("forward"):
        args = {**rest, TWIN_DIFF_INPUT: diff, **{k: w.astype(_WEIGHT_DTYPES[k]) for k, w in weights.items()}}
        y = _forward(args)
    with _jax.named_scope("loss_head"):
        err = _jnp.square(y.astype(_jnp.float32) - loss_target)
        return 0.5 * _jnp.sum(_jnp.mean(err, axis=-1)) if err.ndim else 0.5 * err


def _adamw(w, g, m, v):
    m = ADAM_B1 * m + (1.0 - ADAM_B1) * g
    v = ADAM_B2 * v + (1.0 - ADAM_B2) * _jnp.square(g)
    m_hat = m / (1.0 - ADAM_B1 ** ADAM_STEP)
    v_hat = v / (1.0 - ADAM_B2 ** ADAM_STEP)
    delta = -ADAM_LR * (m_hat / (_jnp.sqrt(v_hat) + ADAM_EPS) + ADAM_WD * w)
    return delta, m, v


def reference(x, meta_tokens, ffn1_norm, ffn1_w_gate, ffn1_w_up, ffn1_w_down, mix_norm, w_in, attn_sinks, ssm_a_re, ssm_a_im, ssm_log_dt, ssm_b_re, ssm_b_im, ssm_c_re, ssm_c_im, ssm_d, w_attn_proj, w_glu_v, w_glu_g, w_out, ffn2_norm, ffn2_w_gate, ffn2_w_up, ffn2_w_down, final_norm, loss_target, m_meta_tokens, m_ffn1_norm, m_ffn1_w_gate, m_ffn1_w_up, m_ffn1_w_down, m_mix_norm, m_w_in, m_attn_sinks, m_ssm_a_re, m_ssm_a_im, m_ssm_log_dt, m_ssm_b_re, m_ssm_b_im, m_ssm_c_re, m_ssm_c_im, m_ssm_d, m_w_attn_proj, m_w_glu_v, m_w_glu_g, m_w_out, m_ffn2_norm, m_ffn2_w_gate, m_ffn2_w_up, m_ffn2_w_down, m_final_norm, v_meta_tokens, v_ffn1_norm, v_ffn1_w_gate, v_ffn1_w_up, v_ffn1_w_down, v_mix_norm, v_w_in, v_attn_sinks, v_ssm_a_re, v_ssm_a_im, v_ssm_log_dt, v_ssm_b_re, v_ssm_b_im, v_ssm_c_re, v_ssm_c_im, v_ssm_d, v_w_attn_proj, v_w_glu_v, v_w_glu_g, v_w_out, v_ffn2_norm, v_ffn2_w_gate, v_ffn2_w_up, v_ffn2_w_down, v_final_norm):
    given = dict(x=x, meta_tokens=meta_tokens, ffn1_norm=ffn1_norm, ffn1_w_gate=ffn1_w_gate, ffn1_w_up=ffn1_w_up, ffn1_w_down=ffn1_w_down, mix_norm=mix_norm, w_in=w_in, attn_sinks=attn_sinks, ssm_a_re=ssm_a_re, ssm_a_im=ssm_a_im, ssm_log_dt=ssm_log_dt, ssm_b_re=ssm_b_re, ssm_b_im=ssm_b_im, ssm_c_re=ssm_c_re, ssm_c_im=ssm_c_im, ssm_d=ssm_d, w_attn_proj=w_attn_proj, w_glu_v=w_glu_v, w_glu_g=w_glu_g, w_out=w_out, ffn2_norm=ffn2_norm, ffn2_w_gate=ffn2_w_gate, ffn2_w_up=ffn2_w_up, ffn2_w_down=ffn2_w_down, final_norm=final_norm, loss_target=loss_target, m_meta_tokens=m_meta_tokens, m_ffn1_norm=m_ffn1_norm, m_ffn1_w_gate=m_ffn1_w_gate, m_ffn1_w_up=m_ffn1_w_up, m_ffn1_w_down=m_ffn1_w_down, m_mix_norm=m_mix_norm, m_w_in=m_w_in, m_attn_sinks=m_attn_sinks, m_ssm_a_re=m_ssm_a_re, m_ssm_a_im=m_ssm_a_im, m_ssm_log_dt=m_ssm_log_dt, m_ssm_b_re=m_ssm_b_re, m_ssm_b_im=m_ssm_b_im, m_ssm_c_re=m_ssm_c_re, m_ssm_c_im=m_ssm_c_im, m_ssm_d=m_ssm_d, m_w_attn_proj=m_w_attn_proj, m_w_glu_v=m_w_glu_v, m_w_glu_g=m_w_glu_g, m_w_out=m_w_out, m_ffn2_norm=m_ffn2_norm, m_ffn2_w_gate=m_ffn2_w_gate, m_ffn2_w_up=m_ffn2_w_up, m_ffn2_w_down=m_ffn2_w_down, m_final_norm=m_final_norm, v_meta_tokens=v_meta_tokens, v_ffn1_norm=v_ffn1_norm, v_ffn1_w_gate=v_ffn1_w_gate, v_ffn1_w_up=v_ffn1_w_up, v_ffn1_w_down=v_ffn1_w_down, v_mix_norm=v_mix_norm, v_w_in=v_w_in, v_attn_sinks=v_attn_sinks, v_ssm_a_re=v_ssm_a_re, v_ssm_a_im=v_ssm_a_im, v_ssm_log_dt=v_ssm_log_dt, v_ssm_b_re=v_ssm_b_re, v_ssm_b_im=v_ssm_b_im, v_ssm_c_re=v_ssm_c_re, v_ssm_c_im=v_ssm_c_im, v_ssm_d=v_ssm_d, v_w_attn_proj=v_w_attn_proj, v_w_glu_v=v_w_glu_v, v_w_glu_g=v_w_glu_g, v_w_out=v_w_out, v_ffn2_norm=v_ffn2_norm, v_ffn2_w_gate=v_ffn2_w_gate, v_ffn2_w_up=v_ffn2_w_up, v_ffn2_w_down=v_ffn2_w_down, v_final_norm=v_final_norm)
    weights = {n: given[n] for n in TWIN_WEIGHTS}
    shared = {n: given[n] for n in SHARED_INPUTS}
    per_example = {n: given[n] for n in ['x']}
    grad_fn = _jax.value_and_grad(_loss, argnums=(0, 1))

    def one_microbatch(ex, loss_target):
        ex = dict(ex)
        diff = ex.pop(TWIN_DIFF_INPUT)
        return grad_fn(weights, diff, {**shared, **ex}, loss_target)

    if N_MICROBATCH == 1:
        loss, (grad_w, grad_x) = one_microbatch(per_example, given["loss_target"])
    else:
        def body(carry, xs):
            loss_sum, grad_sum = carry
            l_k, (gw_k, gx_k) = one_microbatch(xs[0], xs[1])
            with _jax.named_scope("update"):
                return (loss_sum + l_k, _jax.tree.map(_jnp.add, grad_sum, gw_k)), gx_k

        init = (_jnp.zeros((), _jnp.float32), _jax.tree.map(_jnp.zeros_like, weights))
        (loss, grad_w), grad_x = _jax.lax.scan(body, init, (per_example, given["loss_target"]))
    with _jax.named_scope("update"):
        delta_w, new_m, new_v = {}, {}, {}
        for n in TWIN_WEIGHTS:
            delta_w[n], new_m[n], new_v[n] = _adamw(weights[n], grad_w[n], given["m_" + n], given["v_" + n])
    return (loss, grad_x, *[grad_w[n] for n in TWIN_WEIGHTS], *[delta_w[n] for n in TWIN_WEIGHTS],
            *[new_m[n] for n in TWIN_WEIGHTS], *[new_v[n] for n in TWIN_WEIGHTS])
```

```python
import functools
import math

import jax
import jax.numpy as jnp
from jax import lax
from jax.experimental import pallas as pl
from jax.experimental.pallas import tpu as pltpu

F32 = jnp.float32
BF16 = jnp.bfloat16

N_META = 16
HEAD_DIM = 64
N_Q_HEADS = 8
N_KV_HEADS = 2
Q_PER_KV = N_Q_HEADS // N_KV_HEADS
ATTN_WIDTH = N_Q_HEADS * HEAD_DIM
KV_WIDTH = N_KV_HEADS * HEAD_DIM
BLOCK = 128
PAD_FRONT = BLOCK - N_META
ROPE_THETA = 500000.0
ROT_DIM = HEAD_DIM // 4
SSM_GROUP = 16
SSM_STATE = 64
GROUPS_PER_TILE = 4
TILE_STATES = GROUPS_PER_TILE * SSM_STATE
LANES = 128
SUBLANES = 8
EPS = 1e-6
NEG_INF = -1e30
N_CHIPS = 4

ADAM_LR = 0.001
ADAM_B1 = 0.9
ADAM_B2 = 0.999
ADAM_EPS = 1e-08
ADAM_WD = 0.01
ADAM_STEP = 10

VMEM_LIMIT = 56 * 1024 * 1024
MESH = pl.DeviceIdType.MESH


def _cparams(sem=None):
    return pltpu.CompilerParams(dimension_semantics=sem, vmem_limit_bytes=VMEM_LIMIT)


def _row_tile(rows, limit=512):
    best = None
    for t in range(128, limit + 1, 128):
        if rows % t == 0:
            best = t
    assert best is not None, rows
    return best


def _div_tile(rows, row_bytes, max_bytes=1 << 20, mult=8):
    best = None
    for t in range(mult, rows + 1, mult):
        if rows % t == 0 and t * row_bytes <= max_bytes:
            best = t
    if best is None:
        best = rows
    return best


def _dot(a, b, mode):
    if mode == "nn":
        dims = (((1,), (0,)), ((), ()))
    elif mode == "nt":
        dims = (((1,), (1,)), ((), ()))
    else:
        dims = (((0,), (0,)), ((), ()))
    return lax.dot_general(a.astype(BF16), b.astype(BF16), dims, preferred_element_type=F32)


def _sigmoid(x):
    return 1.0 / (1.0 + jnp.exp(-x))


_GELU_C = math.sqrt(2.0 / math.pi)


def _gelu(x):
    return 0.5 * x * (1.0 + jnp.tanh(_GELU_C * (x + 0.044715 * x * x * x)))


def _gelu_grad(x):
    t = jnp.tanh(_GELU_C * (x + 0.044715 * x * x * x))
    return 0.5 * (1.0 + t) + 0.5 * x * (1.0 - t * t) * _GELU_C * (1.0 + 3.0 * 0.044715 * x * x)


def _matmul(name, grid, k_axis, ins, in_specs, pairs, acc_shapes, epilogue, out_shapes, out_specs, sem):
    n_in, n_out, n_acc = len(ins), len(out_shapes), len(acc_shapes)

    def body(*refs):
        in_refs = refs[:n_in]
        out_refs = refs[n_in:n_in + n_out]
        acc_refs = refs[n_in + n_out:]
        if k_axis is None:
            accs = [None] * n_acc
            for ia, ib, mode, iacc in pairs:
                d = _dot(in_refs[ia][...], in_refs[ib][...], mode)
                accs[iacc] = d if accs[iacc] is None else accs[iacc] + d
            epilogue(accs, in_refs, out_refs)
            return
        k = pl.program_id(k_axis)

        @pl.when(k == 0)
        def _():
            for r in acc_refs:
                r[...] = jnp.zeros_like(r)

        for ia, ib, mode, iacc in pairs:
            acc_refs[iacc][...] += _dot(in_refs[ia][...], in_refs[ib][...], mode)

        @pl.when(k == pl.num_programs(k_axis) - 1)
        def _():
            epilogue([r[...] for r in acc_refs], in_refs, out_refs)

    scratch = [] if k_axis is None else [pltpu.VMEM(s, F32) for s in acc_shapes]
    return pl.pallas_call(
        body, out_shape=out_shapes, grid=grid, in_specs=in_specs, out_specs=out_specs,
        scratch_shapes=scratch, compiler_params=_cparams(sem), name=name)(*ins)


def _rms_fwd(h, g, name):
    lp, d = h.shape
    tm = _row_tile(lp)

    def body(h_ref, g_ref, n_ref):
        x = h_ref[...]
        r = lax.rsqrt(jnp.mean(x * x, axis=-1, keepdims=True) + EPS)
        n_ref[...] = (x * r * g_ref[...]).astype(BF16)

    return pl.pallas_call(
        body, out_shape=jax.ShapeDtypeStruct((lp, d), BF16), grid=(lp // tm,),
        in_specs=[pl.BlockSpec((tm, d), lambda i: (i, 0)), pl.BlockSpec((1, d), lambda i: (0, 0))],
        out_specs=pl.BlockSpec((tm, d), lambda i: (i, 0)),
        compiler_params=_cparams(("parallel",)), name=name)(h, g.reshape(1, d))


def _rms_bwd_math(dn, x, g):
    r = lax.rsqrt(jnp.mean(x * x, axis=-1, keepdims=True) + EPS)
    xh = x * r
    dxh = dn * g
    dx = r * (dxh - xh * jnp.mean(dxh * xh, axis=-1, keepdims=True))
    return dx, dn * xh


def _scale_cast(x, scale, name):
    lp, d = x.shape
    tm = _row_tile(lp)

    def body(x_ref, o_ref):
        o_ref[...] = (x_ref[...] * scale).astype(BF16)

    return pl.pallas_call(
        body, out_shape=jax.ShapeDtypeStruct((lp, d), BF16), grid=(lp // tm,),
        in_specs=[pl.BlockSpec((tm, d), lambda i: (i, 0))], out_specs=pl.BlockSpec((tm, d), lambda i: (i, 0)),
        compiler_params=_cparams(("parallel",)), name=name)(x)


def _ffn_fwd(h, gain, wg, wu, wd, tag):
    lp, d = h.shape
    f4 = wg.shape[2]
    tm = _row_tile(lp)
    ni = lp // tm
    n = _rms_fwd(h, gain, "rms_fwd_ffn")

    def up_epi(accs, in_refs, out_refs):
        a, b = accs
        out_refs[0][...] = a.astype(BF16)
        out_refs[1][...] = b.astype(BF16)
        out_refs[2][...] = (a * _sigmoid(a) * b).astype(BF16)

    slab = jax.ShapeDtypeStruct((N_CHIPS, lp, f4), BF16)
    a, b, s = _matmul(
        "ffn_up", (N_CHIPS, ni), None, [n, wg, wu],
        [pl.BlockSpec((tm, d), lambda j, i: (i, 0)),
         pl.BlockSpec((None, d, f4), lambda j, i: (j, 0, 0)),
         pl.BlockSpec((None, d, f4), lambda j, i: (j, 0, 0))],
        [(0, 1, "nn", 0), (0, 2, "nn", 1)], [(tm, f4)] * 2, up_epi,
        [slab, slab, slab], [pl.BlockSpec((None, tm, f4), lambda j, i: (j, i, 0))] * 3,
        ("parallel", "parallel"))

    def down_epi(accs, in_refs, out_refs):
        out_refs[0][...] = in_refs[2][...] + 0.5 * accs[0]

    (h_new,) = _matmul(
        "ffn_down", (ni, N_CHIPS), 1, [s, wd, h],
        [pl.BlockSpec((None, tm, f4), lambda i, j: (j, i, 0)),
         pl.BlockSpec((None, f4, d), lambda i, j: (j, 0, 0)),
         pl.BlockSpec((tm, d), lambda i, j: (i, 0))],
        [(0, 1, "nn", 0)], [(tm, d)], down_epi,
        [jax.ShapeDtypeStruct((lp, d), F32)], [pl.BlockSpec((tm, d), lambda i, j: (i, 0))],
        ("parallel", "arbitrary"))
    return h_new, (a, b, s)


def _tn_tiles(lp):
    return _row_tile(lp, 1408)


def _ffn_bwd(dh, h_in, gain, wg, wu, wd, saved):
    a, b, s = saved
    lp, d = h_in.shape
    f4 = wg.shape[2]
    tm = _row_tile(lp)
    ni = lp // tm
    tk = _tn_tiles(lp)
    nk = lp // tk
    n = _rms_fwd(h_in, gain, "rms_fwd_ffn")
    dhs = _scale_cast(dh, 0.5, "ffn_dh_half")

    def ds_epi(accs, in_refs, out_refs):
        ds = accs[0]
        av = in_refs[2][...].astype(F32)
        bv = in_refs[3][...].astype(F32)
        sg = _sigmoid(av)
        out_refs[0][...] = (ds * bv * sg * (1.0 + av * (1.0 - sg))).astype(BF16)
        out_refs[1][...] = (ds * av * sg).astype(BF16)

    slab = jax.ShapeDtypeStruct((N_CHIPS, lp, f4), BF16)
    slab_spec = pl.BlockSpec((None, tm, f4), lambda j, i: (j, i, 0))
    da, db = _matmul(
        "ffn_bwd_ds", (N_CHIPS, ni), None, [dhs, wd, a, b],
        [pl.BlockSpec((tm, d), lambda j, i: (i, 0)), pl.BlockSpec((None, f4, d), lambda j, i: (j, 0, 0)),
         slab_spec, slab_spec],
        [(0, 1, "nt", 0)], [(tm, f4)], ds_epi, [slab, slab], [slab_spec, slab_spec], ("parallel", "parallel"))

    def copy_epi(accs, in_refs, out_refs):
        for acc, o in zip(accs, out_refs):
            o[...] = acc

    (dwd,) = _matmul(
        "ffn_dwd", (N_CHIPS, nk), 1, [s, dhs],
        [pl.BlockSpec((None, tk, f4), lambda j, k: (j, k, 0)), pl.BlockSpec((tk, d), lambda j, k: (k, 0))],
        [(0, 1, "tn", 0)], [(f4, d)], copy_epi,
        [jax.ShapeDtypeStruct((N_CHIPS, f4, d), F32)], [pl.BlockSpec((None, f4, d), lambda j, k: (j, 0, 0))],
        ("parallel", "arbitrary"))

    dw_shape = jax.ShapeDtypeStruct((N_CHIPS, d, f4), F32)
    dw_spec = pl.BlockSpec((None, d, f4), lambda j, k: (j, 0, 0))
    in_slab = pl.BlockSpec((None, tk, f4), lambda j, k: (j, k, 0))
    dwg, dwu = _matmul(
        "ffn_dwgu", (N_CHIPS, nk), 1, [n, da, db],
        [pl.BlockSpec((tk, d), lambda j, k: (k, 0)), in_slab, in_slab],
        [(0, 1, "tn", 0), (0, 2, "tn", 1)], [(d, f4)] * 2, copy_epi,
        [dw_shape, dw_shape], [dw_spec, dw_spec], ("parallel", "arbitrary"))

    def dn_epi(accs, in_refs, out_refs):
        i, j = pl.program_id(0), pl.program_id(1)
        dx, dgrow = _rms_bwd_math(accs[0], in_refs[5][...], in_refs[6][...])
        out_refs[0][...] = in_refs[4][...] + dx

        @pl.when(i == 0)
        def _():
            out_refs[1][...] = jnp.zeros_like(out_refs[1])

        out_refs[1][...] += jnp.sum(dgrow, axis=0, keepdims=True)

    row_spec = pl.BlockSpec((tm, d), lambda i, j: (i, 0))
    in_slab2 = pl.BlockSpec((None, tm, f4), lambda i, j: (j, i, 0))
    w_spec = pl.BlockSpec((None, d, f4), lambda i, j: (j, 0, 0))
    dh_in, dgain = _matmul(
        "ffn_bwd_dn", (ni, N_CHIPS), 1, [da, wg, db, wu, dh, h_in, gain.reshape(1, d)],
        [in_slab2, w_spec, in_slab2, w_spec, row_spec, row_spec, pl.BlockSpec((1, d), lambda i, j: (0, 0))],
        [(0, 1, "nt", 0), (2, 3, "nt", 0)], [(tm, d)], dn_epi,
        [jax.ShapeDtypeStruct((lp, d), F32), jax.ShapeDtypeStruct((1, d), F32)],
        [row_spec, pl.BlockSpec((1, d), lambda i, j: (0, 0))], ("arbitrary", "arbitrary"))
    return dh_in, dgain, dwg, dwu, dwd


def _rope_tables(lp):
    pos = jnp.arange(lp, dtype=F32) - float(PAD_FRONT)
    inv_freq = ROPE_THETA ** (-jnp.arange(0, ROT_DIM, 2, dtype=F32) / ROT_DIM)
    ang = pos[:, None] * inv_freq[None, :]
    cos, sin = jnp.cos(ang), jnp.sin(ang)
    half = ROT_DIM // 2
    ones = jnp.ones((lp, HEAD_DIM - ROT_DIM), F32)
    zeros_h = jnp.zeros((lp, half), F32)
    zeros_r = jnp.zeros((lp, HEAD_DIM - ROT_DIM), F32)
    c = jnp.concatenate([cos, cos, ones], axis=1)
    s1 = jnp.concatenate([-sin, zeros_h, zeros_r], axis=1)
    s2 = jnp.concatenate([zeros_h, sin, zeros_r], axis=1)
    reps = LANES // HEAD_DIM
    return jnp.stack([jnp.tile(c, (1, reps)), jnp.tile(s1, (1, reps)), jnp.tile(s2, (1, reps))])


def _rope(x, c, s1, s2):
    half = ROT_DIM // 2
    outs = []
    for ch in range(x.shape[1] // LANES):
        xc = x[:, ch * LANES:(ch + 1) * LANES]
        outs.append(xc * c + pltpu.roll(xc, LANES - half, 1) * s1 + pltpu.roll(xc, half, 1) * s2)
    return outs[0] if len(outs) == 1 else jnp.concatenate(outs, axis=1)


def _rope_t(dy, c, s1, s2):
    half = ROT_DIM // 2
    outs = []
    for ch in range(dy.shape[1] // LANES):
        dc = dy[:, ch * LANES:(ch + 1) * LANES]
        outs.append(dc * c + pltpu.roll(dc * s1, half, 1) + pltpu.roll(dc * s2, LANES - half, 1))
    return outs[0] if len(outs) == 1 else jnp.concatenate(outs, axis=1)


def _in_proj(n, w_in, tabs, ssm_w):
    lp, d = n.shape
    inw = w_in.shape[1]
    tm = _row_tile(lp)
    o1 = ATTN_WIDTH
    o2 = o1 + KV_WIDTH
    o3 = o2 + KV_WIDTH
    o4 = o3 + ssm_w
    o5 = o4 + d

    def epi(accs, in_refs, out_refs):
        z = accs[0]
        c, s1, s2 = in_refs[2][0], in_refs[2][1], in_refs[2][2]
        out_refs[0][...] = _rope(z[:, :o1], c, s1, s2).astype(BF16)
        out_refs[1][...] = _rope(z[:, o1:o2], c, s1, s2).astype(BF16)
        out_refs[2][...] = z[:, o2:o3].astype(BF16)
        out_refs[3][...] = z[:, o3:o4]
        out_refs[4][...] = z[:, o4:o5]
        out_refs[5][...] = z[:, o5:]

    def rs(w, dt):
        return jax.ShapeDtypeStruct((lp, w), dt), pl.BlockSpec((tm, w), lambda i: (i, 0))

    shapes, specs = zip(rs(o1, BF16), rs(KV_WIDTH, BF16), rs(KV_WIDTH, BF16), rs(ssm_w, F32), rs(d, F32), rs(d, F32))
    return _matmul(
        "mix_in_proj", (lp // tm,), None, [n, w_in, tabs],
        [pl.BlockSpec((tm, d), lambda i: (i, 0)), pl.BlockSpec((d, inw), lambda i: (0, 0)),
         pl.BlockSpec((3, tm, LANES), lambda i: (0, i, 0))],
        [(0, 1, "nn", 0)], [(tm, inw)], epi, list(shapes), list(specs), ("parallel",))


def _attn_mask(b):
    rows = lax.broadcasted_iota(jnp.int32, (BLOCK, 3 * BLOCK), 0)
    cols = lax.broadcasted_iota(jnp.int32, (BLOCK, 3 * BLOCK), 1)
    qpos = b * BLOCK + rows - PAD_FRONT
    kpos = (b - 1) * BLOCK + cols - PAD_FRONT
    dist = qpos - kpos
    band = (cols < 2 * BLOCK) & (kpos >= N_META) & (dist >= 0) & (dist < BLOCK)
    mrow = cols - 2 * BLOCK
    meta = (mrow >= PAD_FRONT) & ((mrow - PAD_FRONT) <= qpos)
    return band | meta


def _attn_probs(qh, kk, mask, sink):
    s = _dot(qh, kk, "nt") * (HEAD_DIM ** -0.5)
    s = jnp.where(mask, s, NEG_INF)
    m = jnp.maximum(jnp.max(s, axis=-1, keepdims=True), sink)
    e = jnp.exp(s - m)
    es = jnp.exp(sink - m)
    z = jnp.sum(e, axis=-1, keepdims=True) + es
    inv = 1.0 / z
    return e * inv, es * inv


def _head(ref_or_val, h):
    return ref_or_val[:, h * HEAD_DIM:(h + 1) * HEAD_DIM]


def _attn_fwd(q, k, v, sinks):
    lp = q.shape[0]
    nb = lp // BLOCK

    def body(sink_ref, q_ref, kp_ref, kc_ref, km_ref, vp_ref, vc_ref, vm_ref, o_ref):
        b = pl.program_id(0)
        mask = _attn_mask(b)
        for hk in range(N_KV_HEADS):
            kk = jnp.concatenate([_head(kp_ref, hk), _head(kc_ref, hk), _head(km_ref, hk)], axis=0)
            vv = jnp.concatenate([_head(vp_ref, hk), _head(vc_ref, hk), _head(vm_ref, hk)], axis=0)
            for g in range(Q_PER_KV):
                h = hk * Q_PER_KV + g
                p, _ = _attn_probs(_head(q_ref, h), kk, mask, sink_ref[h])
                o_ref[:, h * HEAD_DIM:(h + 1) * HEAD_DIM] = _dot(p, vv, "nn").astype(BF16)

    cur = lambda b: (b, 0)
    prev = lambda b: (jnp.maximum(b - 1, 0), 0)
    first = lambda b: (0, 0)
    kvs = lambda f: pl.BlockSpec((BLOCK, KV_WIDTH), f)
    return pl.pallas_call(
        body, out_shape=jax.ShapeDtypeStruct((lp, ATTN_WIDTH), BF16), grid=(nb,),
        in_specs=[pl.BlockSpec(memory_space=pltpu.SMEM), pl.BlockSpec((BLOCK, ATTN_WIDTH), cur),
                  kvs(prev), kvs(cur), kvs(first), kvs(prev), kvs(cur), kvs(first)],
        out_specs=pl.BlockSpec((BLOCK, ATTN_WIDTH), cur),
        compiler_params=_cparams(("parallel",)), name="attn_fwd")(sinks, q, k, k, k, v, v, v)


def _attn_bwd(q, k, v, do, sinks, tabs):
    lp = q.shape[0]
    nb = lp // BLOCK
    scale = HEAD_DIM ** -0.5

    def body(sink_ref, q_ref, do_ref, kp_ref, kc_ref, km_ref, vp_ref, vc_ref, vm_ref, tq_ref, tk_ref, t0_ref,
             dq_ref, dk_ref, dv_ref, dkm_ref, dvm_ref, dsink_ref,
             dq_s, dkk_s, dvv_s, ck_s, cv_s, mk_s, mv_s):
        b = pl.program_id(0)

        @pl.when(b == 0)
        def _():
            for r in (ck_s, cv_s, mk_s, mv_s, dsink_ref):
                r[...] = jnp.zeros_like(r)

        @pl.when(b < nb)
        def _():
            mask = _attn_mask(b)
            for hk in range(N_KV_HEADS):
                kk = jnp.concatenate([_head(kp_ref, hk), _head(kc_ref, hk), _head(km_ref, hk)], axis=0)
                vv = jnp.concatenate([_head(vp_ref, hk), _head(vc_ref, hk), _head(vm_ref, hk)], axis=0)
                dkk = jnp.zeros((3 * BLOCK, HEAD_DIM), F32)
                dvv = jnp.zeros((3 * BLOCK, HEAD_DIM), F32)
                for g in range(Q_PER_KV):
                    h = hk * Q_PER_KV + g
                    qh = _head(q_ref, h)
                    doh = _head(do_ref, h)
                    p, ps = _attn_probs(qh, kk, mask, sink_ref[h])
                    dp = _dot(doh, vv, "nt")
                    delta = jnp.sum(p * dp, axis=-1, keepdims=True)
                    ds = (p * (dp - delta)).astype(BF16)
                    dsink_ref[h:h + 1, :] += jnp.zeros((1, LANES), F32) - jnp.sum(ps * delta)
                    dq_s[:, h * HEAD_DIM:(h + 1) * HEAD_DIM] = _dot(ds, kk, "nn") * scale
                    dkk = dkk + _dot(ds, qh, "tn") * scale
                    dvv = dvv + _dot(p, doh, "tn")
                dkk_s[:, hk * HEAD_DIM:(hk + 1) * HEAD_DIM] = dkk
                dvv_s[:, hk * HEAD_DIM:(hk + 1) * HEAD_DIM] = dvv
            dq_ref[...] = _rope_t(dq_s[...], tq_ref[0], tq_ref[1], tq_ref[2])
            dk_ref[...] = _rope_t(ck_s[...] + dkk_s[0:BLOCK, :], tk_ref[0], tk_ref[1], tk_ref[2])
            dv_ref[...] = cv_s[...] + dvv_s[0:BLOCK, :]
            ck_s[...] = dkk_s[BLOCK:2 * BLOCK, :]
            cv_s[...] = dvv_s[BLOCK:2 * BLOCK, :]
            mk_s[...] += dkk_s[2 * BLOCK:, :]
            mv_s[...] += dvv_s[2 * BLOCK:, :]

        @pl.when(b == nb)
        def _():
            dk_ref[...] = _rope_t(ck_s[...], tk_ref[0], tk_ref[1], tk_ref[2])
            dv_ref[...] = cv_s[...]
            dkm_ref[...] = _rope_t(mk_s[...], t0_ref[0], t0_ref[1], t0_ref[2])
            dvm_ref[...] = mv_s[...]

    cur = lambda b: (jnp.minimum(b, nb - 1), 0)
    prev = lambda b: (jnp.clip(b - 1, 0, nb - 1), 0)
    first = lambda b: (0, 0)
    kvs = lambda f: pl.BlockSpec((BLOCK, KV_WIDTH), f)
    tab = lambda f: pl.BlockSpec((3, BLOCK, LANES), lambda b: (0,) + f(b)[:1] + (0,))
    kv_out = lambda b: (jnp.maximum(b - 1, 0), 0)
    return pl.pallas_call(
        body,
        out_shape=[jax.ShapeDtypeStruct((lp, ATTN_WIDTH), F32), jax.ShapeDtypeStruct((lp, KV_WIDTH), F32),
                   jax.ShapeDtypeStruct((lp, KV_WIDTH), F32), jax.ShapeDtypeStruct((BLOCK, KV_WIDTH), F32),
                   jax.ShapeDtypeStruct((BLOCK, KV_WIDTH), F32), jax.ShapeDtypeStruct((N_Q_HEADS, LANES), F32)],
        grid=(nb + 1,),
        in_specs=[pl.BlockSpec(memory_space=pltpu.SMEM), pl.BlockSpec((BLOCK, ATTN_WIDTH), cur),
                  pl.BlockSpec((BLOCK, ATTN_WIDTH), cur),
                  kvs(prev), kvs(cur), kvs(first), kvs(prev), kvs(cur), kvs(first),
                  tab(cur), tab(kv_out), tab(first)],
        out_specs=[pl.BlockSpec((BLOCK, ATTN_WIDTH), cur), kvs(kv_out), kvs(kv_out), kvs(first), kvs(first),
                   pl.BlockSpec((N_Q_HEADS, LANES), first)],
        scratch_shapes=[pltpu.VMEM((BLOCK, ATTN_WIDTH), F32), pltpu.VMEM((3 * BLOCK, KV_WIDTH), F32),
                        pltpu.VMEM((3 * BLOCK, KV_WIDTH), F32), pltpu.VMEM((BLOCK, KV_WIDTH), F32),
                        pltpu.VMEM((BLOCK, KV_WIDTH), F32), pltpu.VMEM((BLOCK, KV_WIDTH), F32),
                        pltpu.VMEM((BLOCK, KV_WIDTH), F32)],
        compiler_params=_cparams(("arbitrary",)), name="attn_bwd")(sinks, q, do, k, k, k, v, v, v, tabs, tabs, tabs)


def _cmul(ar, ai, br, bi):
    return ar * br - ai * bi, ar * bi + ai * br


def _cpow(lr, li, n):
    rr = ri = None
    br, bi = lr, li
    while n:
        if n & 1:
            rr, ri = (br, bi) if rr is None else _cmul(rr, ri, br, bi)
        n >>= 1
        if n:
            br, bi = _cmul(br, bi, br, bi)
    return rr, ri


def _shift_rows(x, d, reverse):
    rows = lax.broadcasted_iota(jnp.int32, x.shape, 0)
    if not reverse:
        return jnp.where(rows >= d, pltpu.roll(x, d, 0), 0.0)
    return jnp.where(rows < SUBLANES - d, pltpu.roll(x, SUBLANES - d, 0), 0.0)


def _seg_carry(er, ei, mr, mi, reverse):
    ir, ii, pr, pi = er, ei, mr, mi
    for d in (1, 2, 4):
        tr, ti = _cmul(pr, pi, _shift_rows(ir, d, reverse), _shift_rows(ii, d, reverse))
        ir, ii = ir + tr, ii + ti
        if d < 4:
            pr, pi = _cmul(pr, pi, pr, pi)
    return _shift_rows(ir, 1, reverse), _shift_rows(ii, 1, reverse)


def _seg_scan(xr_ref, xi_ref, lam, seg, reverse, store, init, extra=None):
    nt = len(lam)
    acc0 = () if extra is None else extra[1]

    def step(i, carry):
        hs, acc = carry
        t = seg - 1 - i if reverse else i
        sl = pl.ds(t, SUBLANES, stride=seg)
        out = []
        for j in range(nt):
            lr, li = lam[j]
            hr, hi = hs[2 * j], hs[2 * j + 1]
            nr = lr * hr - li * hi + xr_ref[j, sl, :]
            ni = lr * hi + li * hr + xi_ref[j, sl, :]
            if store:
                xr_ref[j, sl, :] = nr
                xi_ref[j, sl, :] = ni
            if extra is not None:
                acc = extra[0](t, j, nr, ni, acc)
            out += [nr, ni]
        return tuple(out), acc

    return lax.fori_loop(0, seg, step, (tuple(init), acc0))


def _ssm_scan(xr_ref, xi_ref, lam, seg, reverse, extra=None):
    nt = len(lam)
    zero = [jnp.zeros((SUBLANES, LANES), F32)] * (2 * nt)
    ends, _ = _seg_scan(xr_ref, xi_ref, lam, seg, reverse, False, zero)
    init = []
    for j in range(nt):
        mr, mi = _cpow(lam[j][0], lam[j][1], seg)
        cr, ci = _seg_carry(ends[2 * j], ends[2 * j + 1], mr, mi, reverse)
        init += [cr, ci]
    _, acc = _seg_scan(xr_ref, xi_ref, lam, seg, reverse, True, init, extra)
    return acc


def _ssm_tables(bb_re, bb_im, c_re, c_im):
    g = bb_re.shape[0]
    nt = g // GROUPS_PER_TILE
    eye = jnp.eye(g, dtype=F32)

    def b_tab(bb):
        full = jnp.einsum('gpc,gh->gchp', bb, eye).reshape(g * SSM_GROUP, g * SSM_STATE)
        full = full.reshape(g * SSM_GROUP // LANES, LANES, nt, TILE_STATES)
        return jnp.stack([full[t // 2, :, t, :] for t in range(nt)])

    def c_tab(c):
        full = jnp.einsum('gcp,gh->gphc', c, eye).reshape(g * SSM_STATE, g * SSM_GROUP)
        full = full.reshape(nt, TILE_STATES, g * SSM_GROUP // LANES, LANES)
        return jnp.stack([full[t, :, t // 2, :] for t in range(nt)])

    return b_tab(bb_re), b_tab(bb_im), c_tab(c_re), c_tab(c_im)


def _ssm_untable_b(db, g):
    nt = g // GROUPS_PER_TILE
    per_blk = LANES // SSM_GROUP
    db = db.reshape(nt, GROUPS_PER_TILE, SSM_STATE, per_blk, SSM_GROUP)
    out = []
    for t in range(nt):
        for gl in range(GROUPS_PER_TILE):
            out.append(db[t, gl, :, GROUPS_PER_TILE * (t % 2) + gl, :])
    return jnp.stack(out)


def _ssm_untable_c(dc, g):
    nt = g // GROUPS_PER_TILE
    per_blk = LANES // SSM_GROUP
    dc = dc.reshape(nt, per_blk, SSM_GROUP, GROUPS_PER_TILE, SSM_STATE)
    out = []
    for t in range(nt):
        for gl in range(GROUPS_PER_TILE):
            out.append(dc[t, GROUPS_PER_TILE * (t % 2) + gl, :, gl, :])
    return jnp.stack(out)


def _lam_tiles(lam_ref):
    out = []
    for j in range(TILE_STATES // LANES):
        out.append(jnp.broadcast_to(lam_ref[:, j * LANES:(j + 1) * LANES], (SUBLANES, LANES)))
    return out


def _ssm_fwd(u, lam_re, lam_im, tb_re, tb_im, tc_re, tc_im, d_skip):
    lp, w = u.shape
    nt = tb_re.shape[0]
    seg = lp // SUBLANES
    njt = TILE_STATES // LANES

    def body(u_ref, lr_ref, li_ref, br_ref, bi_ref, cr_ref, ci_ref, d_ref, y_ref, xr, xi):
        t = pl.program_id(0)
        for s in range(SUBLANES):
            rs = pl.ds(s * seg, seg)
            ub = u_ref[rs, :].astype(BF16)
            for j in range(njt):
                cs = slice(j * LANES, (j + 1) * LANES)
                xr[j, rs, :] = _dot(ub, br_ref[:, cs], "nn")
                xi[j, rs, :] = _dot(ub, bi_ref[:, cs], "nn")
        lrs, lis = _lam_tiles(lr_ref), _lam_tiles(li_ref)
        _ssm_scan(xr, xi, list(zip(lrs, lis)), seg, False)
        for s in range(SUBLANES):
            rs = pl.ds(s * seg, seg)
            y = None
            for j in range(njt):
                cs = slice(j * LANES, (j + 1) * LANES)
                term = _dot(xr[j, rs, :], cr_ref[cs, :], "nn") - _dot(xi[j, rs, :], ci_ref[cs, :], "nn")
                y = term if y is None else y + term

            @pl.when(t % 2 == 0)
            def _():
                y_ref[rs, :] = y + d_ref[...] * u_ref[rs, :]

            @pl.when(t % 2 == 1)
            def _():
                y_ref[rs, :] += y

    blk = pl.BlockSpec((lp, LANES), lambda t: (0, t // 2))
    lam_spec = pl.BlockSpec((None, 1, TILE_STATES), lambda t: (t, 0, 0))
    b_spec = pl.BlockSpec((None, LANES, TILE_STATES), lambda t: (t, 0, 0))
    c_spec = pl.BlockSpec((None, TILE_STATES, LANES), lambda t: (t, 0, 0))
    return pl.pallas_call(
        body, out_shape=jax.ShapeDtypeStruct((lp, w), F32), grid=(nt,),
        in_specs=[blk, lam_spec, lam_spec, b_spec, b_spec, c_spec, c_spec,
                  pl.BlockSpec((1, LANES), lambda t: (0, t // 2))],
        out_specs=blk,
        scratch_shapes=[pltpu.VMEM((njt, lp, LANES), F32), pltpu.VMEM((njt, lp, LANES), F32)],
        compiler_params=_cparams(("arbitrary",)), name="ssm_fwd")(
            u, lam_re, lam_im, tb_re, tb_im, tc_re, tc_im, d_skip.reshape(1, w))


def _ssm_bwd(u, dy, lam_re, lam_im, tb_re, tb_im, tc_re, tc_im, d_skip):
    lp, w = u.shape
    nt = tb_re.shape[0]
    seg = lp // SUBLANES
    njt = TILE_STATES // LANES
    tbt_re, tbt_im = jnp.swapaxes(tb_re, 1, 2), jnp.swapaxes(tb_im, 1, 2)
    tct_re, tct_im = jnp.swapaxes(tc_re, 1, 2), jnp.swapaxes(tc_im, 1, 2)

    def body(u_ref, dy_ref, lr_ref, li_ref, br_ref, bi_ref, btr_ref, bti_ref, ctr_ref, cti_ref, d_ref,
             du_ref, dlr_ref, dli_ref, dbr_ref, dbi_ref, dcr_ref, dci_ref, dd_ref, hr, hi, ar, ai):
        t = pl.program_id(0)
        lrs, lis = _lam_tiles(lr_ref), _lam_tiles(li_ref)
        for s in range(SUBLANES):
            rs = pl.ds(s * seg, seg)
            ub = u_ref[rs, :].astype(BF16)
            dyb = dy_ref[rs, :].astype(BF16)
            for j in range(njt):
                cs = slice(j * LANES, (j + 1) * LANES)
                hr[j, rs, :] = _dot(ub, br_ref[:, cs], "nn")
                hi[j, rs, :] = _dot(ub, bi_ref[:, cs], "nn")
                ar[j, rs, :] = _dot(dyb, ctr_ref[:, cs], "nn")
                ai[j, rs, :] = -_dot(dyb, cti_ref[:, cs], "nn")
        _ssm_scan(hr, hi, list(zip(lrs, lis)), seg, False)

        def dlam_step(tt, j, a_r, a_i, acc):
            tp = jnp.maximum(tt - 1, 0)
            sl = pl.ds(tp, SUBLANES, stride=seg)
            p_r, p_i = hr[j, sl, :], hi[j, sl, :]
            acc = list(acc)
            acc[2 * j] = acc[2 * j] + jnp.where(tt > 0, a_r * p_r + a_i * p_i, 0.0)
            acc[2 * j + 1] = acc[2 * j + 1] + jnp.where(tt > 0, a_i * p_r - a_r * p_i, 0.0)
            return tuple(acc)

        zero = tuple([jnp.zeros((SUBLANES, LANES), F32)] * (2 * njt))
        conj = [(lr, -li) for lr, li in zip(lrs, lis)]
        acc = _ssm_scan(ar, ai, conj, seg, True, (dlam_step, zero))
        last = pl.ds(seg - 1, SUBLANES, stride=seg)
        first = pl.ds(0, SUBLANES, stride=seg)
        for j in range(njt):
            cs = slice(j * LANES, (j + 1) * LANES)
            p_r = _shift_rows(hr[j, last, :], 1, False)
            p_i = _shift_rows(hi[j, last, :], 1, False)
            a_r, a_i = ar[j, first, :], ai[j, first, :]
            dlr_ref[:, cs] = jnp.sum(acc[2 * j] + a_r * p_r + a_i * p_i, axis=0, keepdims=True)
            dli_ref[:, cs] = jnp.sum(acc[2 * j + 1] + a_i * p_r - a_r * p_i, axis=0, keepdims=True)

        dd = jnp.zeros((1, LANES), F32)
        for s in range(SUBLANES):
            rs = pl.ds(s * seg, seg)
            ub = u_ref[rs, :].astype(BF16)
            dyv = dy_ref[rs, :]
            dyb = dyv.astype(BF16)
            du = None
            for j in range(njt):
                cs = slice(j * LANES, (j + 1) * LANES)
                arb, aib = ar[j, rs, :].astype(BF16), ai[j, rs, :].astype(BF16)
                term = _dot(arb, btr_ref[cs, :], "nn") + _dot(aib, bti_ref[cs, :], "nn")
                du = term if du is None else du + term
                upd = [(dbr_ref, _dot(arb, ub, "tn")), (dbi_ref, _dot(aib, ub, "tn"))]
                for ref, val in upd:
                    if s == 0:
                        ref[cs, :] = val
                    else:
                        ref[cs, :] += val
                updc = [(dcr_ref, _dot(dyb, hr[j, rs, :], "tn")), (dci_ref, -_dot(dyb, hi[j, rs, :], "tn"))]
                for ref, val in updc:
                    if s == 0:
                        ref[:, cs] = val
                    else:
                        ref[:, cs] += val
            rows = lax.broadcasted_iota(jnp.int32, (seg, LANES), 0) + s * seg
            keep = rows >= PAD_FRONT
            dd = dd + jnp.sum(dyv * u_ref[rs, :], axis=0, keepdims=True)

            @pl.when(t % 2 == 0)
            def _():
                du_ref[rs, :] = jnp.where(keep, du + d_ref[...] * dyv, 0.0)

            @pl.when(t % 2 == 1)
            def _():
                du_ref[rs, :] += jnp.where(keep, du, 0.0)

        @pl.when(t % 2 == 0)
        def _():
            dd_ref[...] = dd

    blk = pl.BlockSpec((lp, LANES), lambda t: (0, t // 2))
    vec = pl.BlockSpec((1, LANES), lambda t: (0, t // 2))
    lam_spec = pl.BlockSpec((None, 1, TILE_STATES), lambda t: (t, 0, 0))
    b_spec = pl.BlockSpec((None, LANES, TILE_STATES), lambda t: (t, 0, 0))
    c_spec = pl.BlockSpec((None, TILE_STATES, LANES), lambda t: (t, 0, 0))
    lam_shape = jax.ShapeDtypeStruct((nt, 1, TILE_STATES), F32)
    bt_shape = jax.ShapeDtypeStruct((nt, TILE_STATES, LANES), F32)
    ct_shape = jax.ShapeDtypeStruct((nt, LANES, TILE_STATES), F32)
    st = pltpu.VMEM((njt, lp, LANES), F32)
    return pl.pallas_call(
        body,
        out_shape=[jax.ShapeDtypeStruct((lp, w), F32), lam_shape, lam_shape, bt_shape, bt_shape, ct_shape, ct_shape,
                   jax.ShapeDtypeStruct((1, w), F32)],
        grid=(nt,),
        in_specs=[blk, blk, lam_spec, lam_spec, b_spec, b_spec, c_spec, c_spec, b_spec, b_spec, vec],
        out_specs=[blk, lam_spec, lam_spec, c_spec, c_spec, b_spec, b_spec, vec],
        scratch_shapes=[st, st, st, st],
        compiler_params=_cparams(("arbitrary",)), name="ssm_bwd")(
            u, dy, lam_re, lam_im, tb_re, tb_im, tbt_re, tbt_im, tct_re, tct_im, d_skip.reshape(1, w))


def _ssm_params(a_re, a_im, log_dt, b_re, b_im):
    dt = jnp.exp(log_dt)[:, None]
    mag = jnp.exp(a_re * dt)
    lb_re = mag * jnp.cos(a_im * dt)
    lb_im = mag * jnp.sin(a_im * dt)
    den = a_re * a_re + a_im * a_im
    num_re = lb_re - 1.0
    coef_re = (num_re * a_re + lb_im * a_im) / den
    coef_im = (lb_im * a_re - num_re * a_im) / den
    bb_re = coef_re[..., None] * b_re - coef_im[..., None] * b_im
    bb_im = coef_re[..., None] * b_im + coef_im[..., None] * b_re
    return lb_re, lb_im, bb_re, bb_im


def _gelu_fwd(y):
    lp, w = y.shape
    tm = _row_tile(lp)

    def body(y_ref, o_ref):
        o_ref[...] = _gelu(y_ref[...]).astype(BF16)

    return pl.pallas_call(
        body, out_shape=jax.ShapeDtypeStruct((lp, w), BF16), grid=(lp // tm,),
        in_specs=[pl.BlockSpec((tm, w), lambda i: (i, 0))], out_specs=pl.BlockSpec((tm, w), lambda i: (i, 0)),
        compiler_params=_cparams(("parallel",)), name="gelu_fwd")(y)


def _merge_fwd(o, yg, ga, gs, w3):
    lp, d = ga.shape
    d4 = w3.shape[3]
    kw = w3.shape[2]
    tm = _row_tile(lp)

    def epi(accs, in_refs, out_refs):
        attn, vv, gg = accs
        out_refs[0][...] = (_sigmoid(in_refs[5][...]) * attn
                            + _sigmoid(in_refs[6][...]) * (vv * _sigmoid(gg))).astype(BF16)

    wspec = lambda which: pl.BlockSpec((None, None, kw, d4), lambda j, i: (j, which, 0, 0))
    colspec = pl.BlockSpec((tm, d4), lambda j, i: (i, j))
    aspec = pl.BlockSpec((tm, kw), lambda j, i: (i, 0))
    (merged,) = _matmul(
        "merge_fwd", (N_CHIPS, lp // tm), None, [o, yg, w3, w3, w3, ga, gs],
        [aspec, aspec, wspec(0), wspec(1), wspec(2), colspec, colspec],
        [(0, 2, "nn", 0), (1, 3, "nn", 1), (1, 4, "nn", 2)], [(tm, d4)] * 3, epi,
        [jax.ShapeDtypeStruct((lp, d), BF16)], [colspec], ("parallel", "parallel"))
    return merged


def _out_proj(merged, w_out, h):
    lp, d = h.shape
    d4 = w_out.shape[1]
    tm = _row_tile(lp)

    def epi(accs, in_refs, out_refs):
        out_refs[0][...] = in_refs[2][...] + accs[0]

    (h_new,) = _matmul(
        "mix_out_proj", (lp // tm, N_CHIPS), 1, [merged, w_out, h],
        [pl.BlockSpec((tm, d4), lambda i, j: (i, j)), pl.BlockSpec((None, d4, d), lambda i, j: (j, 0, 0)),
         pl.BlockSpec((tm, d), lambda i, j: (i, 0))],
        [(0, 1, "nn", 0)], [(tm, d)], epi,
        [jax.ShapeDtypeStruct((lp, d), F32)], [pl.BlockSpec((tm, d), lambda i, j: (i, 0))],
        ("parallel", "arbitrary"))
    return h_new


def _merge_bwd(dhb, w_out, o, yg, ga, gs, w3):
    lp, d = ga.shape
    d4 = w3.shape[3]
    kw = w3.shape[2]
    tm = _row_tile(lp)

    def epi(accs, in_refs, out_refs):
        dm, attn, vv, gg = accs
        sa = _sigmoid(in_refs[7][...])
        ss = _sigmoid(in_refs[8][...])
        sg = _sigmoid(gg)
        ssm = vv * sg
        dssm = dm * ss
        out_refs[0][...] = (dm * sa).astype(BF16)
        out_refs[1][...] = (dssm * sg).astype(BF16)
        out_refs[2][...] = (dssm * vv * sg * (1.0 - sg)).astype(BF16)
        out_refs[3][...] = (dm * attn * sa * (1.0 - sa)).astype(BF16)
        out_refs[4][...] = (dm * ssm * ss * (1.0 - ss)).astype(BF16)

    wspec = lambda which: pl.BlockSpec((None, None, kw, d4), lambda j, i: (j, which, 0, 0))
    colspec = pl.BlockSpec((tm, d4), lambda j, i: (i, j))
    aspec = pl.BlockSpec((tm, kw), lambda j, i: (i, 0))
    shp = jax.ShapeDtypeStruct((lp, d), BF16)
    return _matmul(
        "merge_bwd", (N_CHIPS, lp // tm), None, [dhb, w_out, o, yg, w3, w3, w3, ga, gs],
        [pl.BlockSpec((tm, d), lambda j, i: (i, 0)), pl.BlockSpec((None, d4, d), lambda j, i: (j, 0, 0)),
         aspec, aspec, wspec(0), wspec(1), wspec(2), colspec, colspec],
        [(0, 1, "nt", 0), (2, 4, "nn", 1), (3, 5, "nn", 2), (3, 6, "nn", 3)], [(tm, d4)] * 4, epi,
        [shp] * 5, [colspec] * 5, ("parallel", "parallel"))


def _branch_bwd(dattn, dv, dg, w3, y):
    lp, d = dattn.shape
    d4 = w3.shape[3]
    kw = w3.shape[2]
    tm = _row_tile(lp)

    def epi(accs, in_refs, out_refs):
        out_refs[0][...] = accs[0].astype(BF16)
        out_refs[1][...] = accs[1] * _gelu_grad(in_refs[6][...])

    wspec = lambda which: pl.BlockSpec((None, None, kw, d4), lambda i, j: (j, which, 0, 0))
    colspec = pl.BlockSpec((tm, d4), lambda i, j: (i, j))
    rowspec = pl.BlockSpec((tm, kw), lambda i, j: (i, 0))
    return _matmul(
        "branch_bwd", (lp // tm, N_CHIPS), 1, [dattn, dv, dg, w3, w3, w3, y],
        [colspec, colspec, colspec, wspec(0), wspec(1), wspec(2), rowspec],
        [(0, 3, "nt", 0), (1, 4, "nt", 1), (2, 5, "nt", 1)], [(tm, kw)] * 2, epi,
        [jax.ShapeDtypeStruct((lp, kw), BF16), jax.ShapeDtypeStruct((lp, kw), F32)], [rowspec, rowspec],
        ("parallel", "arbitrary"))


def _tn_cols(x, ys, name):
    lp, kx = x.shape
    n = ys[0].shape[1]
    n4 = n // N_CHIPS
    tk = _tn_tiles(lp)

    def epi(accs, in_refs, out_refs):
        for acc, o in zip(accs, out_refs):
            o[...] = acc

    shp = jax.ShapeDtypeStruct((N_CHIPS, kx, n4), F32)
    return _matmul(
        name, (N_CHIPS, lp // tk), 1, [x] + list(ys),
        [pl.BlockSpec((tk, kx), lambda j, k: (k, 0))] + [pl.BlockSpec((tk, n4), lambda j, k: (k, j))] * len(ys),
        [(0, 1 + i, "tn", i) for i in range(len(ys))], [(kx, n4)] * len(ys), epi,
        [shp] * len(ys), [pl.BlockSpec((None, kx, n4), lambda j, k: (j, 0, 0))] * len(ys),
        ("parallel", "arbitrary"))


def _tn_full(x, y, name, tn_cols=None):
    lp, kx = x.shape
    n = y.shape[1]
    tk = _tn_tiles(lp)
    tn = n if tn_cols is None else tn_cols

    def epi(accs, in_refs, out_refs):
        out_refs[0][...] = accs[0]

    (out,) = _matmul(
        name, (n // tn, lp // tk), 1, [x, y],
        [pl.BlockSpec((tk, kx), lambda j, k: (k, 0)), pl.BlockSpec((tk, tn), lambda j, k: (k, j))],
        [(0, 1, "tn", 0)], [(kx, tn)], epi,
        [jax.ShapeDtypeStruct((kx, n), F32)], [pl.BlockSpec((kx, tn), lambda j, k: (0, j))],
        ("parallel", "arbitrary"))
    return out


def _in_proj_bwd(dz, w_in, dh, h_in, gain):
    lp, d = h_in.shape
    inw = w_in.shape[1]
    tm = _row_tile(lp)

    def epi(accs, in_refs, out_refs):
        i = pl.program_id(0)
        dx, dgrow = _rms_bwd_math(accs[0], in_refs[3][...], in_refs[4][...])
        out_refs[0][...] = in_refs[2][...] + dx

        @pl.when(i == 0)
        def _():
            out_refs[1][...] = jnp.zeros_like(out_refs[1])

        out_refs[1][...] += jnp.sum(dgrow, axis=0, keepdims=True)

    row = pl.BlockSpec((tm, d), lambda i: (i, 0))
    return _matmul(
        "mix_in_proj_bwd", (lp // tm,), None, [dz, w_in, dh, h_in, gain.reshape(1, d)],
        [pl.BlockSpec((tm, inw), lambda i: (i, 0)), pl.BlockSpec((d, inw), lambda i: (0, 0)), row, row,
         pl.BlockSpec((1, d), lambda i: (0, 0))],
        [(0, 1, "nt", 0)], [(tm, d)], epi,
        [jax.ShapeDtypeStruct((lp, d), F32), jax.ShapeDtypeStruct((1, d), F32)],
        [row, pl.BlockSpec((1, d), lambda i: (0, 0))], ("arbitrary",))


def _loss_head(h, gain, target):
    lp, d = h.shape
    nb = lp // BLOCK

    def body(h_ref, g_ref, t_ref, dh_ref, dg_ref, loss_ref):
        i = pl.program_id(0)

        @pl.when(i == 0)
        def _():
            dg_ref[...] = jnp.zeros_like(dg_ref)
            loss_ref[...] = jnp.zeros_like(loss_ref)
            dh_ref[...] = jnp.zeros_like(dh_ref)

        @pl.when(i > 0)
        def _():
            x = h_ref[...]
            g = g_ref[...]
            r = lax.rsqrt(jnp.mean(x * x, axis=-1, keepdims=True) + EPS)
            err = x * r * g - t_ref[...]
            loss_ref[...] += jnp.zeros_like(loss_ref) + 0.5 * jnp.sum(jnp.sum(err * err, axis=-1, keepdims=True)) / d
            dx, dgrow = _rms_bwd_math(err * (1.0 / d), x, g)
            dh_ref[...] = dx
            dg_ref[...] += jnp.sum(dgrow, axis=0, keepdims=True)

    row = pl.BlockSpec((BLOCK, d), lambda i: (i, 0))
    one = pl.BlockSpec((1, d), lambda i: (0, 0))
    return pl.pallas_call(
        body,
        out_shape=[jax.ShapeDtypeStruct((lp, d), F32), jax.ShapeDtypeStruct((1, d), F32),
                   jax.ShapeDtypeStruct((SUBLANES, LANES), F32)],
        grid=(nb,),
        in_specs=[row, one, pl.BlockSpec((BLOCK, d), lambda i: (jnp.maximum(i - 1, 0), 0))],
        out_specs=[row, one, pl.BlockSpec((SUBLANES, LANES), lambda i: (0, 0))],
        compiler_params=_cparams(("arbitrary",)), name="loss_head")(h, gain.reshape(1, d), target)


def _adam_math(w, g, m, v):
    m = ADAM_B1 * m + (1.0 - ADAM_B1) * g
    v = ADAM_B2 * v + (1.0 - ADAM_B2) * (g * g)
    m_hat = m / (1.0 - ADAM_B1 ** ADAM_STEP)
    v_hat = v / (1.0 - ADAM_B2 ** ADAM_STEP)
    delta = -ADAM_LR * (m_hat / (jnp.sqrt(v_hat) + ADAM_EPS) + ADAM_WD * w)
    return delta, m, v


def _adamw_layers(w, m, v, grads, name):
    depth, r, c = w.shape
    tr = _div_tile(r, c * 4)
    nr = r // tr

    def body(*refs):
        w_ref, m_ref, v_ref = refs[:3]
        g_refs = refs[3:3 + depth]
        g_out, d_out, m_out, v_out = refs[3 + depth:]
        layer = pl.program_id(0)
        for l in range(depth):
            @pl.when(layer == l)
            def _(l=l):
                g = g_refs[l][...]
                delta, nm, nv = _adam_math(w_ref[...], g, m_ref[...], v_ref[...])
                g_out[...] = g
                d_out[...] = delta
                m_out[...] = nm
                v_out[...] = nv

    stacked = pl.BlockSpec((None, tr, c), lambda l, i: (l, i, 0))

    def gspec(layer):
        def imap(l, i):
            return (jnp.where(l == layer, i, jnp.where(l < layer, 0, nr - 1)), 0)
        return pl.BlockSpec((tr, c), imap)

    shp = jax.ShapeDtypeStruct((depth, r, c), F32)
    return pl.pallas_call(
        body, out_shape=[shp] * 4, grid=(depth, nr),
        in_specs=[stacked] * 3 + [gspec(l) for l in range(depth)], out_specs=[stacked] * 4,
        compiler_params=_cparams(("arbitrary", "arbitrary")), name=name)(w, m, v, *grads)


def _adamw_flat(w, g, m, v, name):
    r, c = w.shape
    tr = _div_tile(r, c * 4)

    def body(w_ref, g_ref, m_ref, v_ref, d_out, m_out, v_out):
        delta, nm, nv = _adam_math(w_ref[...], g_ref[...], m_ref[...], v_ref[...])
        d_out[...] = delta
        m_out[...] = nm
        v_out[...] = nv

    spec = pl.BlockSpec((tr, c), lambda i: (i, 0))
    shp = jax.ShapeDtypeStruct((r, c), F32)
    return pl.pallas_call(
        body, out_shape=[shp] * 3, grid=(r // tr,), in_specs=[spec] * 4, out_specs=[spec] * 3,
        compiler_params=_cparams(("parallel",)), name=name)(w, g, m, v)


def _mesh_pos():
    return lax.axis_index("x"), lax.axis_index("y"), lax.axis_index("c")


def _row_half(ref, which, lead):
    half = ref.shape[lead] // 2
    idx = (slice(None),) * lead + (pl.ds(which * half, half), slice(None))
    return ref.at[idx]


HBM_SPEC = pl.BlockSpec(memory_space=pltpu.HBM)


def _all_gather_chips(arrs, name):
    n = len(arrs)

    def body(*refs):
        ins, outs = refs[:n], refs[n:2 * n]
        send_sems, recv_sems, local_sems = refs[2 * n:]
        x, y, c = _mesh_pos()
        sibling = (x, y, 1 - c)
        chips = [(1 - x, y), (x, 1 - y), (1 - x, 1 - y)]
        mine = 2 * x + y

        def slot(k, chip, which):
            lead = len(ins[k].shape) - 2
            return _row_half(outs[k].at[2 * chip[0] + chip[1]], which, lead)

        def copy(k, j, src, dst, to):
            return pltpu.make_async_remote_copy(
                src_ref=src, dst_ref=dst, send_sem=send_sems.at[6 * k + j], recv_sem=recv_sems.at[6 * k + j],
                device_id=to, device_id_type=MESH)

        local, first, passed = [], [], []
        for k in range(n):
            lead = len(ins[k].shape) - 2
            lc = pltpu.make_async_copy(ins[k], outs[k].at[mine], local_sems.at[k])
            lc.start()
            local.append(lc)
            for j, chip in enumerate(chips):
                cp = copy(k, j, _row_half(ins[k], c, lead), slot(k, (x, y), c), (*chip, c))
                cp.start()
                first.append(cp)
        for j, chip in enumerate(chips):
            for k in range(n):
                copy(k, j, slot(k, chip, c), slot(k, chip, c), sibling).wait_recv()
                cp = copy(k, 3 + j, slot(k, chip, c), slot(k, chip, c), sibling)
                cp.start()
                passed.append(cp)
        for j, chip in enumerate(chips):
            for k in range(n):
                copy(k, 3 + j, slot(k, chip, 1 - c), slot(k, chip, 1 - c), sibling).wait_recv()
        for cp in first + passed:
            cp.wait_send()
        for lc in local:
            lc.wait()

    return pl.pallas_call(
        body, out_shape=[jax.ShapeDtypeStruct((N_CHIPS,) + a.shape, a.dtype) for a in arrs],
        in_specs=[HBM_SPEC] * n, out_specs=[HBM_SPEC] * n,
        scratch_shapes=[pltpu.SemaphoreType.DMA((6 * n,)), pltpu.SemaphoreType.DMA((6 * n,)),
                        pltpu.SemaphoreType.DMA((n,))],
        name=name)(*arrs)


def _all_gather_devices(x_shard, name):
    m_per, ncol = x_shard.shape

    def body(x_ref, out_ref, send_sems, recv_sems, local_sem):
        x, y, c = _mesh_pos()
        me, sibling = (x, y, c), (x, y, 1 - c)
        chips = [(1 - x, y), (x, 1 - y), (1 - x, 1 - y)]

        def rows(px, py, pc):
            return out_ref.at[4 * px + 2 * py + pc]

        def copy(k, block, to, src=None):
            return pltpu.make_async_remote_copy(
                src_ref=rows(*block) if src is None else src, dst_ref=rows(*block),
                send_sem=send_sems.at[k], recv_sem=recv_sems.at[k], device_id=to, device_id_type=MESH)

        mine = pltpu.make_async_copy(x_ref, rows(*me), local_sem)
        mine.start()
        first = [copy(0, me, sibling, src=x_ref)]
        first += [copy(1 + j, me, (*chip, c), src=x_ref) for j, chip in enumerate(chips)]
        for cp in first:
            cp.start()
        passed = [copy(4 + j, (*chip, c), sibling) for j, chip in enumerate(chips)]
        for j, chip in enumerate(chips):
            copy(1 + j, (*chip, c), me).wait_recv()
            passed[j].start()
        copy(0, sibling, me).wait_recv()
        for j, chip in enumerate(chips):
            copy(4 + j, (*chip, 1 - c), me).wait_recv()
        for cp in first + passed:
            cp.wait_send()
        mine.wait()

    return pl.pallas_call(
        body, out_shape=jax.ShapeDtypeStruct((8, m_per, ncol), x_shard.dtype),
        in_specs=[pl.BlockSpec(memory_space=pltpu.VMEM)], out_specs=pl.BlockSpec(memory_space=pltpu.VMEM),
        scratch_shapes=[pltpu.SemaphoreType.DMA((7,)), pltpu.SemaphoreType.DMA((7,)), pltpu.SemaphoreType.DMA],
        compiler_params=pltpu.CompilerParams(vmem_limit_bytes=VMEM_LIMIT), name=name)(x_shard)


def _sum_devices(g8, name):
    _, r, c = g8.shape
    tr = _div_tile(r, c * 4 * 8)

    def body(g_ref, o_ref):
        acc = g_ref[0]
        for dev in range(1, 8):
            acc = acc + g_ref[dev]
        o_ref[...] = acc

    return pl.pallas_call(
        body, out_shape=jax.ShapeDtypeStruct((r, c), F32), grid=(r // tr,),
        in_specs=[pl.BlockSpec((8, tr, c), lambda i: (0, i, 0))], out_specs=pl.BlockSpec((tr, c), lambda i: (i, 0)),
        compiler_params=_cparams(("parallel",)), name=name)(g8)


def _exchange_sibling_halves(arrs, name):
    n = len(arrs)

    def body(*refs):
        ins, outs = refs[:n], refs[n:2 * n]
        send_sems, recv_sems = refs[2 * n:]
        x, y, c = _mesh_pos()
        cps = []
        for k in range(n):
            cp = pltpu.make_async_remote_copy(
                src_ref=_row_half(ins[k], 1 - c, 1), dst_ref=outs[k], send_sem=send_sems.at[k],
                recv_sem=recv_sems.at[k], device_id=(x, y, 1 - c), device_id_type=MESH)
            cp.start()
            cps.append(cp)
        for cp in cps:
            cp.wait()

    return pl.pallas_call(
        body,
        out_shape=[jax.ShapeDtypeStruct((a.shape[0], a.shape[1] // 2, a.shape[2]), a.dtype) for a in arrs],
        in_specs=[HBM_SPEC] * n, out_specs=[HBM_SPEC] * n,
        scratch_shapes=[pltpu.SemaphoreType.DMA((n,)), pltpu.SemaphoreType.DMA((n,))], name=name)(*arrs)


def _chip_partial(arr, recv, pos, name):
    nslab, r, c = arr.shape
    half = r // 2

    def body(pos_ref, a_ref, b_ref, o_ref):
        o_ref[...] = (a_ref[...] + b_ref[...]).astype(BF16)

    grid_spec = pltpu.PrefetchScalarGridSpec(
        num_scalar_prefetch=1, grid=(nslab,),
        in_specs=[pl.BlockSpec((None, half, c), lambda j, p: (j, p[0], 0)),
                  pl.BlockSpec((None, half, c), lambda j, p: (j, 0, 0))],
        out_specs=pl.BlockSpec((None, half, c), lambda j, p: (j, 0, 0)))
    return pl.pallas_call(
        body, out_shape=jax.ShapeDtypeStruct((nslab, half, c), BF16), grid_spec=grid_spec,
        compiler_params=_cparams(("parallel",)), name=name)(pos, arr, recv)


def _exchange_chip_partials(parts, name):
    n = len(parts)

    def body(*refs):
        ins, outs = refs[:n], refs[n:2 * n]
        send_sems, recv_sems = refs[2 * n:]
        x, y, c = _mesh_pos()
        chips = [(1 - x, y), (x, 1 - y), (1 - x, 1 - y)]
        cps = []
        for k in range(n):
            for j, chip in enumerate(chips):
                cp = pltpu.make_async_remote_copy(
                    src_ref=ins[k].at[2 * chip[0] + chip[1]], dst_ref=outs[k].at[j],
                    send_sem=send_sems.at[3 * k + j], recv_sem=recv_sems.at[3 * k + j],
                    device_id=(*chip, c), device_id_type=MESH)
                cp.start()
                cps.append(cp)
        for cp in cps:
            cp.wait()

    return pl.pallas_call(
        body, out_shape=[jax.ShapeDtypeStruct((3,) + p.shape[1:], p.dtype) for p in parts],
        in_specs=[HBM_SPEC] * n, out_specs=[HBM_SPEC] * n,
        scratch_shapes=[pltpu.SemaphoreType.DMA((3 * n,)), pltpu.SemaphoreType.DMA((3 * n,))], name=name)(*parts)


def _reduce_half(arr, recv, got, pos, name):
    nslab, r, c = arr.shape
    half = r // 2

    def body(pos_ref, a_ref, b_ref, g_ref, o_ref):
        acc = a_ref[...] + b_ref[...]
        for j in range(3):
            acc = acc + g_ref[j].astype(F32)
        o_ref[...] = acc

    grid_spec = pltpu.PrefetchScalarGridSpec(
        num_scalar_prefetch=1, grid=(1,),
        in_specs=[pl.BlockSpec((None, half, c), lambda i, p: (p[1], p[0], 0)),
                  pl.BlockSpec((None, half, c), lambda i, p: (p[1], 0, 0)),
                  pl.BlockSpec((3, half, c), lambda i, p: (0, 0, 0))],
        out_specs=pl.BlockSpec((half, c), lambda i, p: (0, 0)))
    return pl.pallas_call(
        body, out_shape=jax.ShapeDtypeStruct((half, c), F32), grid_spec=grid_spec,
        compiler_params=_cparams(("arbitrary",)), name=name)(pos, arr, recv, got)


def _share_halves(halves, name):
    n = len(halves)

    def body(*refs):
        ins, outs = refs[:n], refs[n:2 * n]
        send_sems, recv_sems, local_sems = refs[2 * n:]
        x, y, c = _mesh_pos()
        cps, lcs = [], []
        for k in range(n):
            dst = _row_half(outs[k], c, 0)
            lc = pltpu.make_async_copy(ins[k], dst, local_sems.at[k])
            lc.start()
            lcs.append(lc)
            cp = pltpu.make_async_remote_copy(
                src_ref=ins[k], dst_ref=dst, send_sem=send_sems.at[k], recv_sem=recv_sems.at[k],
                device_id=(x, y, 1 - c), device_id_type=MESH)
            cp.start()
            cps.append(cp)
        for k in range(n):
            cps[k].wait_send()
            pltpu.make_async_remote_copy(
                src_ref=ins[k], dst_ref=_row_half(outs[k], 1 - c, 0), send_sem=send_sems.at[k],
                recv_sem=recv_sems.at[k], device_id=(x, y, 1 - c), device_id_type=MESH).wait_recv()
            lcs[k].wait()

    return pl.pallas_call(
        body, out_shape=[jax.ShapeDtypeStruct((2 * h.shape[0], h.shape[1]), h.dtype) for h in halves],
        in_specs=[HBM_SPEC] * n, out_specs=[HBM_SPEC] * n,
        scratch_shapes=[pltpu.SemaphoreType.DMA((n,)), pltpu.SemaphoreType.DMA((n,)),
                        pltpu.SemaphoreType.DMA((n,))], name=name)(*halves)


def _reduce_scatter(arrs, pos, tag):
    recv = _exchange_sibling_halves(arrs, "rs_sibling_" + tag)
    parts = [_chip_partial(a, r, pos, "rs_partial") for a, r in zip(arrs, recv)]
    got = _exchange_chip_partials(parts, "rs_chips_" + tag)
    halves = [_reduce_half(a, r, g, pos, "rs_reduce") for a, r, g in zip(arrs, recv, got)]
    return _share_halves(halves, "rs_share_" + tag)


def _layer_fwd(h, p, tabs):
    h1, ffn1_saved = _ffn_fwd(h, p["ffn1_norm"], p["wg1"], p["wu1"], p["wd1"], "1")
    n = _rms_fwd(h1, p["mix_norm"], "rms_fwd_mix")
    ssm_w = p["ssm_d"].shape[0]
    q, k, v, u, ga, gs = _in_proj(n, p["w_in"], tabs, ssm_w)
    o = _attn_fwd(q, k, v, p["attn_sinks"])
    y = _ssm_fwd(u, *p["ssm_tabs"], p["ssm_d"])
    yg = _gelu_fwd(y)
    merged = _merge_fwd(o, yg, ga, gs, p["w3"])
    h2 = _out_proj(merged, p["w_out"], h1)
    h3, ffn2_saved = _ffn_fwd(h2, p["ffn2_norm"], p["wg2"], p["wu2"], p["wd2"], "2")
    saved = dict(h0=h, h1=h1, h2=h2, ffn1=ffn1_saved, ffn2=ffn2_saved, q=q, k=k, v=v, u=u, ga=ga, gs=gs, o=o, y=y,
                 yg=yg, merged=merged)
    return h3, saved


def _layer_bwd(dh, p, s, tabs):
    g = {}
    dh2, g["ffn2_norm"], g["wg2"], g["wu2"], g["wd2"] = _ffn_bwd(
        dh, s["h2"], p["ffn2_norm"], p["wg2"], p["wu2"], p["wd2"], s["ffn2"])
    lp, d = dh2.shape
    d4 = d // N_CHIPS
    dhb = _scale_cast(dh2, 1.0, "mix_dh_cast")
    g["w_out"] = _tn_full(s["merged"], dhb, "mix_dw_out").reshape(N_CHIPS, d4, d)
    dattn, dv, dg, dga, dgs = _merge_bwd(dhb, p["w_out"], s["o"], s["yg"], s["ga"], s["gs"], p["w3"])
    (g["w_ap"],) = _tn_cols(s["o"], [dattn], "mix_dw_ap")
    g["w_gv"], g["w_gg"] = _tn_cols(s["yg"], [dv, dg], "mix_dw_glu")
    do, dy = _branch_bwd(dattn, dv, dg, p["w3"], s["y"])
    dq, dk, dvv, dkm, dvm, dsink = _attn_bwd(s["q"], s["k"], s["v"], do, p["attn_sinks"], tabs)
    g["attn_sinks"] = dsink[:, 0]
    du, dlr, dli, dbr, dbi, dcr, dci, dd = _ssm_bwd(s["u"], dy, *p["ssm_tabs"], p["ssm_d"])
    ngrp = p["ssm_d"].shape[0] // SSM_GROUP
    g["ssm_lam"] = (dlr.reshape(ngrp, SSM_STATE), dli.reshape(ngrp, SSM_STATE),
                    _ssm_untable_b(dbr, ngrp), _ssm_untable_b(dbi, ngrp))
    g["ssm_c_re"] = _ssm_untable_c(dcr, ngrp)
    g["ssm_c_im"] = _ssm_untable_c(dci, ngrp)
    g["ssm_d"] = dd[0]
    dk = dk.at[:BLOCK].add(dkm)
    dvv = dvv.at[:BLOCK].add(dvm)
    dz = jnp.concatenate([dq.astype(BF16), dk.astype(BF16), dvv.astype(BF16), du.astype(BF16), dga, dgs], axis=1)
    n = _rms_fwd(s["h1"], p["mix_norm"], "rms_fwd_mix")
    inw = p["w_in"].shape[1]
    tn_cols = inw // 2 if (inw // 2) % LANES == 0 else None
    g["w_in"] = _tn_full(n, dz, "mix_dw_in", tn_cols)
    dh1, g["mix_norm"] = _in_proj_bwd(dz, p["w_in"], dh2, s["h1"], p["mix_norm"])
    dh0, g["ffn1_norm"], g["wg1"], g["wu1"], g["wd1"] = _ffn_bwd(
        dh1, s["h0"], p["ffn1_norm"], p["wg1"], p["wu1"], p["wd1"], s["ffn1"])
    return dh0, g


BIG = ["ffn1_w_gate", "ffn1_w_up", "ffn1_w_down", "w_in", "w_attn_proj", "w_glu_v", "w_glu_g", "w_out",
       "ffn2_w_gate", "ffn2_w_up", "ffn2_w_down"]
SMALL = ["ffn1_norm", "mix_norm", "attn_sinks", "ssm_a_re", "ssm_a_im", "ssm_log_dt", "ssm_b_re", "ssm_b_im",
         "ssm_c_re", "ssm_c_im", "ssm_d", "ffn2_norm", "final_norm"]
WEIGHTS = ["meta_tokens", "ffn1_norm", "ffn1_w_gate", "ffn1_w_up", "ffn1_w_down", "mix_norm", "w_in", "attn_sinks",
           "ssm_a_re", "ssm_a_im", "ssm_log_dt", "ssm_b_re", "ssm_b_im", "ssm_c_re", "ssm_c_im", "ssm_d",
           "w_attn_proj", "w_glu_v", "w_glu_g", "w_out", "ffn2_norm", "ffn2_w_gate", "ffn2_w_up", "ffn2_w_down",
           "final_norm"]


def _pack_small(tree):
    flat = jnp.concatenate([tree[k].reshape(-1) for k in SMALL + ["meta_tokens"]])
    rows = -(-flat.shape[0] // (LANES * LANES)) * LANES
    return jnp.pad(flat, (0, rows * LANES - flat.shape[0])).reshape(rows, LANES)


def _unpack_small(packed, like):
    flat = packed.reshape(-1)
    out, off = {}, 0
    for k in SMALL + ["meta_tokens"]:
        size = math.prod(like[k].shape)
        out[k] = flat[off:off + size].reshape(like[k].shape)
        off += size
    return out


def kernel(x, meta_tokens, ffn1_norm, ffn1_w_gate, ffn1_w_up, ffn1_w_down, mix_norm, w_in, attn_sinks, ssm_a_re, ssm_a_im, ssm_log_dt, ssm_b_re, ssm_b_im, ssm_c_re, ssm_c_im, ssm_d, w_attn_proj, w_glu_v, w_glu_g, w_out, ffn2_norm, ffn2_w_gate, ffn2_w_up, ffn2_w_down, final_norm, loss_target, m_meta_tokens, m_ffn1_norm, m_ffn1_w_gate, m_ffn1_w_up, m_ffn1_w_down, m_mix_norm, m_w_in, m_attn_sinks, m_ssm_a_re, m_ssm_a_im, m_ssm_log_dt, m_ssm_b_re, m_ssm_b_im, m_ssm_c_re, m_ssm_c_im, m_ssm_d, m_w_attn_proj, m_w_glu_v, m_w_glu_g, m_w_out, m_ffn2_norm, m_ffn2_w_gate, m_ffn2_w_up, m_ffn2_w_down, m_final_norm, v_meta_tokens, v_ffn1_norm, v_ffn1_w_gate, v_ffn1_w_up, v_ffn1_w_down, v_mix_norm, v_w_in, v_attn_sinks, v_ssm_a_re, v_ssm_a_im, v_ssm_log_dt, v_ssm_b_re, v_ssm_b_im, v_ssm_c_re, v_ssm_c_im, v_ssm_d, v_w_attn_proj, v_w_glu_v, v_w_glu_g, v_w_out, v_ffn2_norm, v_ffn2_w_gate, v_ffn2_w_up, v_ffn2_w_down, v_final_norm):
    args = dict(locals())
    w = {k: args[k] for k in WEIGHTS}
    m = {k: args["m_" + k] for k in WEIGHTS}
    v = {k: args["v_" + k] for k in WEIGHTS}
    depth = ffn1_norm.shape[0]
    seq, d = x.shape[1], x.shape[2]
    lp = seq + BLOCK
    xi, yi, ci = _mesh_pos()
    pos = jnp.stack([ci, 2 * xi + yi]).astype(jnp.int32)

    tabs = _rope_tables(lp)
    (meta_all,) = _all_gather_chips([meta_tokens], "gather_meta")
    meta_full = jnp.concatenate([meta_all[j] for j in range(N_CHIPS)], axis=1)
    layers = []
    for l in range(depth):
        shards = [
            jnp.stack([ffn1_w_gate[l], ffn1_w_up[l]]).astype(BF16),
            ffn1_w_down[l].astype(BF16),
            w_in[l].astype(BF16),
            jnp.stack([w_attn_proj[l], w_glu_v[l], w_glu_g[l]]).astype(BF16),
            w_out[l].astype(BF16),
            jnp.stack([ffn2_w_gate[l], ffn2_w_up[l]]).astype(BF16),
            ffn2_w_down[l].astype(BF16),
        ]
        wgu1, wd1, win, w3, wout, wgu2, wd2 = _all_gather_chips(shards, "gather_weights")
        lb_re, lb_im, bb_re, bb_im = _ssm_params(ssm_a_re[l], ssm_a_im[l], ssm_log_dt[l], ssm_b_re[l], ssm_b_im[l])
        ngrp = lb_re.shape[0]
        nt = ngrp // GROUPS_PER_TILE
        ssm_tabs = (lb_re.reshape(nt, 1, TILE_STATES), lb_im.reshape(nt, 1, TILE_STATES),
                    *_ssm_tables(bb_re, bb_im, ssm_c_re[l], ssm_c_im[l]))
        layers.append(dict(
            wg1=wgu1[:, 0], wu1=wgu1[:, 1], wd1=wd1, wg2=wgu2[:, 0], wu2=wgu2[:, 1], wd2=wd2,
            w_in=jnp.concatenate([win[j] for j in range(N_CHIPS)], axis=1), w3=w3, w_out=wout,
            ffn1_norm=ffn1_norm[l], mix_norm=mix_norm[l], ffn2_norm=ffn2_norm[l], attn_sinks=attn_sinks[l],
            ssm_d=ssm_d[l], ssm_tabs=ssm_tabs))

    h = jnp.concatenate([jnp.zeros((PAD_FRONT, d), F32), meta_full, x[0]], axis=0)
    saved = []
    for l in range(depth):
        h, s = _layer_fwd(h, layers[l], tabs)
        saved.append(s)
    dh, g_final, loss_acc = _loss_head(h, final_norm, loss_target[0])
    loss = lax.psum(loss_acc[0, 0], ("x", "y", "c"))

    grads = [None] * depth
    for l in reversed(range(depth)):
        dh, grads[l] = _layer_bwd(dh, layers[l], saved[l], tabs)
    grad_x = dh[BLOCK:][None]
    dmeta_local = dh[PAD_FRONT:BLOCK]

    small = {k: [] for k in SMALL}
    for l in range(depth):
        gl = grads[l]
        _, vjp = jax.vjp(_ssm_params, ssm_a_re[l], ssm_a_im[l], ssm_log_dt[l], ssm_b_re[l], ssm_b_im[l])
        da_re, da_im, dlog_dt, db_re, db_im = vjp(gl["ssm_lam"])
        for k, val in (("ffn1_norm", gl["ffn1_norm"][0]), ("mix_norm", gl["mix_norm"][0]),
                       ("attn_sinks", gl["attn_sinks"]), ("ssm_a_re", da_re), ("ssm_a_im", da_im),
                       ("ssm_log_dt", dlog_dt), ("ssm_b_re", db_re), ("ssm_b_im", db_im),
                       ("ssm_c_re", gl["ssm_c_re"]), ("ssm_c_im", gl["ssm_c_im"]), ("ssm_d", gl["ssm_d"]),
                       ("ffn2_norm", gl["ffn2_norm"][0])):
            small[k].append(val)
    small_local = {k: jnp.stack(vals) for k, vals in small.items() if k != "final_norm"}
    small_local["final_norm"] = g_final[0]
    small_local["meta_tokens"] = dmeta_local
    like = dict(small_local)
    g_small = _sum_devices(_all_gather_devices(_pack_small(small_local), "gather_small_grads"), "sum_small_grads")
    g_small_tree = _unpack_small(g_small, like)
    d4 = d // N_CHIPS
    chip = 2 * xi + yi
    g_meta = lax.dynamic_slice_in_dim(g_small_tree["meta_tokens"], chip * d4, d4, axis=1)

    reduced = []
    for l in range(depth):
        gl = grads[l]
        inw4 = w_in.shape[2]
        dwin = jnp.stack([gl["w_in"][:, j * inw4:(j + 1) * inw4] for j in range(N_CHIPS)])
        arrs = [gl["wg1"], gl["wu1"], gl["wd1"], dwin, gl["w_ap"], gl["w_gv"], gl["w_gg"], gl["w_out"],
                gl["wg2"], gl["wu2"], gl["wd2"]]
        reduced.append(_reduce_scatter(arrs, pos, "grads"))

    g_out, delta, new_m, new_v = {}, {}, {}, {}
    for i, k in enumerate(BIG):
        g_out[k], delta[k], new_m[k], new_v[k] = _adamw_layers(
            w[k], m[k], v[k], [reduced[l][i] for l in range(depth)], "adamw_" + k)
    small_names = SMALL + ["meta_tokens"]
    w_small = {k: w[k] for k in small_names}
    m_small = {k: m[k] for k in small_names}
    v_small = {k: v[k] for k in small_names}
    g_small_local = dict(g_small_tree)
    g_small_local["meta_tokens"] = g_meta
    d_s, m_s, v_s = _adamw_flat(_pack_small(w_small), _pack_small(g_small_local), _pack_small(m_small),
                                _pack_small(v_small), "adamw_small")
    for tree, packed in ((delta, d_s), (new_m, m_s), (new_v, v_s)):
        tree.update(_unpack_small(packed, w_small))
    for k in small_names:
        g_out[k] = g_small_local[k]

    return (loss, grad_x, *[g_out[k] for k in WEIGHTS], *[delta[k] for k in WEIGHTS],
            *[new_m[k] for k in WEIGHTS], *[new_v[k] for k in WEIGHTS])
```

```python
import functools
import math

import jax
import jax.numpy as jnp
from jax import lax
from jax.experimental import pallas as pl
from jax.experimental.pallas import tpu as pltpu

F32 = jnp.float32
BF16 = jnp.bfloat16

N_META = 16
HEAD_DIM = 64
N_Q_HEADS = 8
N_KV_HEADS = 2
Q_PER_KV = N_Q_HEADS // N_KV_HEADS
ATTN_WIDTH = N_Q_HEADS * HEAD_DIM
KV_WIDTH = N_KV_HEADS * HEAD_DIM
BLOCK = 128
PAD_FRONT = BLOCK - N_META
ROPE_THETA = 500000.0
ROT_DIM = HEAD_DIM // 4
SSM_GROUP = 16
SSM_STATE = 64
GROUPS_PER_TILE = 4
TILE_STATES = GROUPS_PER_TILE * SSM_STATE
LANES = 128
SUBLANES = 8
EPS = 1e-6
NEG_INF = -1e30
N_CHIPS = 4

ADAM_LR = 0.001
ADAM_B1 = 0.9
ADAM_B2 = 0.999
ADAM_EPS = 1e-08
ADAM_WD = 0.01
ADAM_STEP = 10

VMEM_LIMIT = 56 * 1024 * 1024
MESH = pl.DeviceIdType.MESH


def _cparams(sem=None):
    return pltpu.CompilerParams(dimension_semantics=sem, vmem_limit_bytes=VMEM_LIMIT)


def _row_tile(rows, limit=512):
    best = None
    for t in range(128, limit + 1, 128):
        if rows % t == 0:
            best = t
    assert best is not None, rows
    return best


def _div_tile(rows, row_bytes, max_bytes=1 << 20, mult=8):
    best = None
    for t in range(mult, rows + 1, mult):
        if rows % t == 0 and t * row_bytes <= max_bytes:
            best = t
    if best is None:
        best = rows
    return best


def _dot(a, b, mode):
    if mode == "nn":
        dims = (((1,), (0,)), ((), ()))
    elif mode == "nt":
        dims = (((1,), (1,)), ((), ()))
    else:
        dims = (((0,), (0,)), ((), ()))
    return lax.dot_general(a.astype(BF16), b.astype(BF16), dims, preferred_element_type=F32)


def _sigmoid(x):
    return 1.0 / (1.0 + jnp.exp(-x))


_GELU_C = math.sqrt(2.0 / math.pi)


def _gelu(x):
    return 0.5 * x * (1.0 + jnp.tanh(_GELU_C * (x + 0.044715 * x * x * x)))


def _gelu_grad(x):
    t = jnp.tanh(_GELU_C * (x + 0.044715 * x * x * x))
    return 0.5 * (1.0 + t) + 0.5 * x * (1.0 - t * t) * _GELU_C * (1.0 + 3.0 * 0.044715 * x * x)


def _matmul(name, grid, k_axis, ins, in_specs, pairs, acc_shapes, epilogue, out_shapes, out_specs, sem):
    n_in, n_out, n_acc = len(ins), len(out_shapes), len(acc_shapes)

    def body(*refs):
        in_refs = refs[:n_in]
        out_refs = refs[n_in:n_in + n_out]
        acc_refs = refs[n_in + n_out:]
        if k_axis is None:
            accs = [None] * n_acc
            for ia, ib, mode, iacc in pairs:
                d = _dot(in_refs[ia][...], in_refs[ib][...], mode)
                accs[iacc] = d if accs[iacc] is None else accs[iacc] + d
            epilogue(accs, in_refs, out_refs)
            return
        k = pl.program_id(k_axis)

        @pl.when(k == 0)
        def _():
            for r in acc_refs:
                r[...] = jnp.zeros_like(r)

        for ia, ib, mode, iacc in pairs:
            acc_refs[iacc][...] += _dot(in_refs[ia][...], in_refs[ib][...], mode)

        @pl.when(k == pl.num_programs(k_axis) - 1)
        def _():
            epilogue([r[...] for r in acc_refs], in_refs, out_refs)

    scratch = [] if k_axis is None else [pltpu.VMEM(s, F32) for s in acc_shapes]
    return pl.pallas_call(
        body, out_shape=out_shapes, grid=grid, in_specs=in_specs, out_specs=out_specs,
        scratch_shapes=scratch, compiler_params=_cparams(sem), name=name)(*ins)


def _rms_fwd(h, g, name):
    lp, d = h.shape
    tm = _row_tile(lp)

    def body(h_ref, g_ref, n_ref):
        x = h_ref[...]
        r = lax.rsqrt(jnp.mean(x * x, axis=-1, keepdims=True) + EPS)
        n_ref[...] = (x * r * g_ref[...]).astype(BF16)

    return pl.pallas_call(
        body, out_shape=jax.ShapeDtypeStruct((lp, d), BF16), grid=(lp // tm,),
        in_specs=[pl.BlockSpec((tm, d), lambda i: (i, 0)), pl.BlockSpec((1, d), lambda i: (0, 0))],
        out_specs=pl.BlockSpec((tm, d), lambda i: (i, 0)),
        compiler_params=_cparams(("parallel",)), name=name)(h, g.reshape(1, d))


def _rms_bwd_math(dn, x, g):
    r = lax.rsqrt(jnp.mean(x * x, axis=-1, keepdims=True) + EPS)
    xh = x * r
    dxh = dn * g
    dx = r * (dxh - xh * jnp.mean(dxh * xh, axis=-1, keepdims=True))
    return dx, dn * xh


def _scale_cast(x, scale, name):
    lp, d = x.shape
    tm = _row_tile(lp)

    def body(x_ref, o_ref):
        o_ref[...] = (x_ref[...] * scale).astype(BF16)

    return pl.pallas_call(
        body, out_shape=jax.ShapeDtypeStruct((lp, d), BF16), grid=(lp // tm,),
        in_specs=[pl.BlockSpec((tm, d), lambda i: (i, 0))], out_specs=pl.BlockSpec((tm, d), lambda i: (i, 0)),
        compiler_params=_cparams(("parallel",)), name=name)(x)


def _ffn_fwd(h, gain, wgu, wd):
    lp, d = h.shape
    f4 = wgu.shape[3]
    tm = _row_tile(lp)
    ni = lp // tm
    n = _rms_fwd(h, gain, "rms_fwd_ffn")

    def up_epi(accs, in_refs, out_refs):
        a, b = accs
        out_refs[0][...] = a.astype(BF16)
        out_refs[1][...] = b.astype(BF16)
        out_refs[2][...] = (a * _sigmoid(a) * b).astype(BF16)

    slab = jax.ShapeDtypeStruct((N_CHIPS, lp, f4), BF16)
    a, b, s = _matmul(
        "ffn_up", (N_CHIPS, ni), None, [n, wgu, wgu],
        [pl.BlockSpec((tm, d), lambda j, i: (i, 0)),
         pl.BlockSpec((None, None, d, f4), lambda j, i: (j, 0, 0, 0)),
         pl.BlockSpec((None, None, d, f4), lambda j, i: (j, 1, 0, 0))],
        [(0, 1, "nn", 0), (0, 2, "nn", 1)], [(tm, f4)] * 2, up_epi,
        [slab, slab, slab], [pl.BlockSpec((None, tm, f4), lambda j, i: (j, i, 0))] * 3,
        ("parallel", "parallel"))

    def down_epi(accs, in_refs, out_refs):
        out_refs[0][...] = in_refs[2][...] + 0.5 * accs[0]

    (h_new,) = _matmul(
        "ffn_down", (ni, N_CHIPS), 1, [s, wd, h],
        [pl.BlockSpec((None, tm, f4), lambda i, j: (j, i, 0)),
         pl.BlockSpec((None, f4, d), lambda i, j: (j, 0, 0)),
         pl.BlockSpec((tm, d), lambda i, j: (i, 0))],
        [(0, 1, "nn", 0)], [(tm, d)], down_epi,
        [jax.ShapeDtypeStruct((lp, d), F32)], [pl.BlockSpec((tm, d), lambda i, j: (i, 0))],
        ("parallel", "arbitrary"))
    return h_new, (a, b, s)


def _tn_tiles(lp):
    return _row_tile(lp, 1408)


def _ffn_bwd(dh, h_in, gain, wgu, wd, saved):
    a, b, s = saved
    lp, d = h_in.shape
    f4 = wgu.shape[3]
    tm = _row_tile(lp)
    ni = lp // tm
    tk = _tn_tiles(lp)
    nk = lp // tk
    n = _rms_fwd(h_in, gain, "rms_fwd_ffn")
    dhs = _scale_cast(dh, 0.5, "ffn_dh_half")

    def ds_epi(accs, in_refs, out_refs):
        ds = accs[0]
        av = in_refs[2][...].astype(F32)
        bv = in_refs[3][...].astype(F32)
        sg = _sigmoid(av)
        out_refs[0][...] = (ds * bv * sg * (1.0 + av * (1.0 - sg))).astype(BF16)
        out_refs[1][...] = (ds * av * sg).astype(BF16)

    slab = jax.ShapeDtypeStruct((N_CHIPS, lp, f4), BF16)
    slab_spec = pl.BlockSpec((None, tm, f4), lambda j, i: (j, i, 0))
    da, db = _matmul(
        "ffn_bwd_ds", (N_CHIPS, ni), None, [dhs, wd, a, b],
        [pl.BlockSpec((tm, d), lambda j, i: (i, 0)), pl.BlockSpec((None, f4, d), lambda j, i: (j, 0, 0)),
         slab_spec, slab_spec],
        [(0, 1, "nt", 0)], [(tm, f4)], ds_epi, [slab, slab], [slab_spec, slab_spec], ("parallel", "parallel"))

    def copy_epi(accs, in_refs, out_refs):
        for acc, o in zip(accs, out_refs):
            o[...] = acc

    (dwd,) = _matmul(
        "ffn_dwd", (N_CHIPS, nk), 1, [s, dhs],
        [pl.BlockSpec((None, tk, f4), lambda j, k: (j, k, 0)), pl.BlockSpec((tk, d), lambda j, k: (k, 0))],
        [(0, 1, "tn", 0)], [(f4, d)], copy_epi,
        [jax.ShapeDtypeStruct((N_CHIPS, f4, d), F32)], [pl.BlockSpec((None, f4, d), lambda j, k: (j, 0, 0))],
        ("parallel", "arbitrary"))

    dw_shape = jax.ShapeDtypeStruct((N_CHIPS, d, f4), F32)
    dw_spec = pl.BlockSpec((None, d, f4), lambda j, k: (j, 0, 0))
    in_slab = pl.BlockSpec((None, tk, f4), lambda j, k: (j, k, 0))
    dwg, dwu = _matmul(
        "ffn_dwgu", (N_CHIPS, nk), 1, [n, da, db],
        [pl.BlockSpec((tk, d), lambda j, k: (k, 0)), in_slab, in_slab],
        [(0, 1, "tn", 0), (0, 2, "tn", 1)], [(d, f4)] * 2, copy_epi,
        [dw_shape, dw_shape], [dw_spec, dw_spec], ("parallel", "arbitrary"))

    def dn_epi(accs, in_refs, out_refs):
        i, j = pl.program_id(0), pl.program_id(1)
        dx, dgrow = _rms_bwd_math(accs[0], in_refs[5][...], in_refs[6][...])
        out_refs[0][...] = in_refs[4][...] + dx

        @pl.when(i == 0)
        def _():
            out_refs[1][...] = jnp.zeros_like(out_refs[1])

        out_refs[1][...] += jnp.sum(dgrow, axis=0, keepdims=True)

    row_spec = pl.BlockSpec((tm, d), lambda i, j: (i, 0))
    in_slab2 = pl.BlockSpec((None, tm, f4), lambda i, j: (j, i, 0))
    wg_spec = pl.BlockSpec((None, None, d, f4), lambda i, j: (j, 0, 0, 0))
    wu_spec = pl.BlockSpec((None, None, d, f4), lambda i, j: (j, 1, 0, 0))
    dh_in, dgain = _matmul(
        "ffn_bwd_dn", (ni, N_CHIPS), 1, [da, wgu, db, wgu, dh, h_in, gain.reshape(1, d)],
        [in_slab2, wg_spec, in_slab2, wu_spec, row_spec, row_spec, pl.BlockSpec((1, d), lambda i, j: (0, 0))],
        [(0, 1, "nt", 0), (2, 3, "nt", 0)], [(tm, d)], dn_epi,
        [jax.ShapeDtypeStruct((lp, d), F32), jax.ShapeDtypeStruct((1, d), F32)],
        [row_spec, pl.BlockSpec((1, d), lambda i, j: (0, 0))], ("arbitrary", "arbitrary"))
    return dh_in, dgain, dwg, dwu, dwd


def _rope_tables(lp):
    pos = jnp.arange(lp, dtype=F32) - float(PAD_FRONT)
    inv_freq = ROPE_THETA ** (-jnp.arange(0, ROT_DIM, 2, dtype=F32) / ROT_DIM)
    ang = pos[:, None] * inv_freq[None, :]
    cos, sin = jnp.cos(ang), jnp.sin(ang)
    half = ROT_DIM // 2
    ones = jnp.ones((lp, HEAD_DIM - ROT_DIM), F32)
    zeros_h = jnp.zeros((lp, half), F32)
    zeros_r = jnp.zeros((lp, HEAD_DIM - ROT_DIM), F32)
    c = jnp.concatenate([cos, cos, ones], axis=1)
    s1 = jnp.concatenate([-sin, zeros_h, zeros_r], axis=1)
    s2 = jnp.concatenate([zeros_h, sin, zeros_r], axis=1)
    reps = LANES // HEAD_DIM
    return jnp.stack([jnp.tile(c, (1, reps)), jnp.tile(s1, (1, reps)), jnp.tile(s2, (1, reps))])


def _rope(x, c, s1, s2):
    half = ROT_DIM // 2
    outs = []
    for ch in range(x.shape[1] // LANES):
        xc = x[:, ch * LANES:(ch + 1) * LANES]
        outs.append(xc * c + pltpu.roll(xc, LANES - half, 1) * s1 + pltpu.roll(xc, half, 1) * s2)
    return outs[0] if len(outs) == 1 else jnp.concatenate(outs, axis=1)


def _rope_t(dy, c, s1, s2):
    half = ROT_DIM // 2
    outs = []
    for ch in range(dy.shape[1] // LANES):
        dc = dy[:, ch * LANES:(ch + 1) * LANES]
        outs.append(dc * c + pltpu.roll(dc * s1, half, 1) + pltpu.roll(dc * s2, LANES - half, 1))
    return outs[0] if len(outs) == 1 else jnp.concatenate(outs, axis=1)


def _in_proj(n, w_in, tabs, ssm_w):
    lp, d = n.shape
    inw = w_in.shape[1]
    tm = _row_tile(lp)
    o1 = ATTN_WIDTH
    o2 = o1 + KV_WIDTH
    o3 = o2 + KV_WIDTH
    o4 = o3 + ssm_w
    o5 = o4 + d

    def epi(accs, in_refs, out_refs):
        z = accs[0]
        c, s1, s2 = in_refs[2][0], in_refs[2][1], in_refs[2][2]
        out_refs[0][...] = _rope(z[:, :o1], c, s1, s2).astype(BF16)
        out_refs[1][...] = _rope(z[:, o1:o2], c, s1, s2).astype(BF16)
        out_refs[2][...] = z[:, o2:o3].astype(BF16)
        out_refs[3][...] = z[:, o3:o4]
        out_refs[4][...] = z[:, o4:o5]
        out_refs[5][...] = z[:, o5:]

    def rs(w, dt):
        return jax.ShapeDtypeStruct((lp, w), dt), pl.BlockSpec((tm, w), lambda i: (i, 0))

    shapes, specs = zip(rs(o1, BF16), rs(KV_WIDTH, BF16), rs(KV_WIDTH, BF16), rs(ssm_w, F32), rs(d, F32), rs(d, F32))
    return _matmul(
        "mix_in_proj", (lp // tm,), None, [n, w_in, tabs],
        [pl.BlockSpec((tm, d), lambda i: (i, 0)), pl.BlockSpec((d, inw), lambda i: (0, 0)),
         pl.BlockSpec((3, tm, LANES), lambda i: (0, i, 0))],
        [(0, 1, "nn", 0)], [(tm, inw)], epi, list(shapes), list(specs), ("parallel",))


def _attn_mask(b):
    rows = lax.broadcasted_iota(jnp.int32, (BLOCK, 3 * BLOCK), 0)
    cols = lax.broadcasted_iota(jnp.int32, (BLOCK, 3 * BLOCK), 1)
    qpos = b * BLOCK + rows - PAD_FRONT
    kpos = (b - 1) * BLOCK + cols - PAD_FRONT
    dist = qpos - kpos
    band = (cols < 2 * BLOCK) & (kpos >= N_META) & (dist >= 0) & (dist < BLOCK)
    mrow = cols - 2 * BLOCK
    meta = (mrow >= PAD_FRONT) & ((mrow - PAD_FRONT) <= qpos)
    return band | meta


def _attn_probs(qh, kk, mask, sink):
    s = _dot(qh, kk, "nt") * (HEAD_DIM ** -0.5)
    s = jnp.where(mask, s, NEG_INF)
    m = jnp.maximum(jnp.max(s, axis=-1, keepdims=True), sink)
    e = jnp.exp(s - m)
    es = jnp.exp(sink - m)
    z = jnp.sum(e, axis=-1, keepdims=True) + es
    inv = 1.0 / z
    return e * inv, es * inv


def _head(ref_or_val, h):
    return ref_or_val[:, h * HEAD_DIM:(h + 1) * HEAD_DIM]


def _attn_fwd(q, k, v, sinks):
    lp = q.shape[0]
    nb = lp // BLOCK

    def body(sink_ref, q_ref, kp_ref, kc_ref, km_ref, vp_ref, vc_ref, vm_ref, o_ref):
        b = pl.program_id(0)
        mask = _attn_mask(b)
        for hk in range(N_KV_HEADS):
            kk = jnp.concatenate([_head(kp_ref, hk), _head(kc_ref, hk), _head(km_ref, hk)], axis=0)
            vv = jnp.concatenate([_head(vp_ref, hk), _head(vc_ref, hk), _head(vm_ref, hk)], axis=0)
            for g in range(Q_PER_KV):
                h = hk * Q_PER_KV + g
                p, _ = _attn_probs(_head(q_ref, h), kk, mask, sink_ref[h])
                o_ref[:, h * HEAD_DIM:(h + 1) * HEAD_DIM] = _dot(p, vv, "nn").astype(BF16)

    cur = lambda b: (b, 0)
    prev = lambda b: (jnp.maximum(b - 1, 0), 0)
    first = lambda b: (0, 0)
    kvs = lambda f: pl.BlockSpec((BLOCK, KV_WIDTH), f)
    return pl.pallas_call(
        body, out_shape=jax.ShapeDtypeStruct((lp, ATTN_WIDTH), BF16), grid=(nb,),
        in_specs=[pl.BlockSpec(memory_space=pltpu.SMEM), pl.BlockSpec((BLOCK, ATTN_WIDTH), cur),
                  kvs(prev), kvs(cur), kvs(first), kvs(prev), kvs(cur), kvs(first)],
        out_specs=pl.BlockSpec((BLOCK, ATTN_WIDTH), cur),
        compiler_params=_cparams(("parallel",)), name="attn_fwd")(sinks, q, k, k, k, v, v, v)


def _attn_bwd(q, k, v, do, sinks, tabs):
    lp = q.shape[0]
    nb = lp // BLOCK
    scale = HEAD_DIM ** -0.5

    def body(sink_ref, q_ref, do_ref, kp_ref, kc_ref, km_ref, vp_ref, vc_ref, vm_ref, tq_ref, tk_ref, t0_ref,
             dq_ref, dk_ref, dv_ref, dkm_ref, dvm_ref, dsink_ref,
             dq_s, dkk_s, dvv_s, ck_s, cv_s, mk_s, mv_s):
        b = pl.program_id(0)

        @pl.when(b == 0)
        def _():
            for r in (ck_s, cv_s, mk_s, mv_s, dsink_ref):
                r[...] = jnp.zeros_like(r)

        @pl.when(b < nb)
        def _():
            mask = _attn_mask(b)
            for hk in range(N_KV_HEADS):
                kk = jnp.concatenate([_head(kp_ref, hk), _head(kc_ref, hk), _head(km_ref, hk)], axis=0)
                vv = jnp.concatenate([_head(vp_ref, hk), _head(vc_ref, hk), _head(vm_ref, hk)], axis=0)
                dkk = jnp.zeros((3 * BLOCK, HEAD_DIM), F32)
                dvv = jnp.zeros((3 * BLOCK, HEAD_DIM), F32)
                for g in range(Q_PER_KV):
                    h = hk * Q_PER_KV + g
                    qh = _head(q_ref, h)
                    doh = _head(do_ref, h)
                    p, ps = _attn_probs(qh, kk, mask, sink_ref[h])
                    dp = _dot(doh, vv, "nt")
                    delta = jnp.sum(p * dp, axis=-1, keepdims=True)
                    ds = (p * (dp - delta)).astype(BF16)
                    dsink_ref[h:h + 1, :] += jnp.zeros((1, LANES), F32) - jnp.sum(ps * delta)
                    dq_s[:, h * HEAD_DIM:(h + 1) * HEAD_DIM] = _dot(ds, kk, "nn") * scale
                    dkk = dkk + _dot(ds, qh, "tn") * scale
                    dvv = dvv + _dot(p, doh, "tn")
                dkk_s[:, hk * HEAD_DIM:(hk + 1) * HEAD_DIM] = dkk
                dvv_s[:, hk * HEAD_DIM:(hk + 1) * HEAD_DIM] = dvv
            dq_ref[...] = _rope_t(dq_s[...], tq_ref[0], tq_ref[1], tq_ref[2])
            dk_ref[...] = _rope_t(ck_s[...] + dkk_s[0:BLOCK, :], tk_ref[0], tk_ref[1], tk_ref[2])
            dv_ref[...] = cv_s[...] + dvv_s[0:BLOCK, :]
            ck_s[...] = dkk_s[BLOCK:2 * BLOCK, :]
            cv_s[...] = dvv_s[BLOCK:2 * BLOCK, :]
            mk_s[...] += dkk_s[2 * BLOCK:, :]
            mv_s[...] += dvv_s[2 * BLOCK:, :]

        @pl.when(b == nb)
        def _():
            dk_ref[...] = _rope_t(ck_s[...], tk_ref[0], tk_ref[1], tk_ref[2])
            dv_ref[...] = cv_s[...]
            dkm_ref[...] = _rope_t(mk_s[...], t0_ref[0], t0_ref[1], t0_ref[2])
            dvm_ref[...] = mv_s[...]

    cur = lambda b: (jnp.minimum(b, nb - 1), 0)
    prev = lambda b: (jnp.clip(b - 1, 0, nb - 1), 0)
    first = lambda b: (0, 0)
    kvs = lambda f: pl.BlockSpec((BLOCK, KV_WIDTH), f)
    tab = lambda f: pl.BlockSpec((3, BLOCK, LANES), lambda b: (0,) + f(b)[:1] + (0,))
    kv_out = lambda b: (jnp.maximum(b - 1, 0), 0)
    return pl.pallas_call(
        body,
        out_shape=[jax.ShapeDtypeStruct((lp, ATTN_WIDTH), F32), jax.ShapeDtypeStruct((lp, KV_WIDTH), F32),
                   jax.ShapeDtypeStruct((lp, KV_WIDTH), F32), jax.ShapeDtypeStruct((BLOCK, KV_WIDTH), F32),
                   jax.ShapeDtypeStruct((BLOCK, KV_WIDTH), F32), jax.ShapeDtypeStruct((N_Q_HEADS, LANES), F32)],
        grid=(nb + 1,),
        in_specs=[pl.BlockSpec(memory_space=pltpu.SMEM), pl.BlockSpec((BLOCK, ATTN_WIDTH), cur),
                  pl.BlockSpec((BLOCK, ATTN_WIDTH), cur),
                  kvs(prev), kvs(cur), kvs(first), kvs(prev), kvs(cur), kvs(first),
                  tab(cur), tab(kv_out), tab(first)],
        out_specs=[pl.BlockSpec((BLOCK, ATTN_WIDTH), cur), kvs(kv_out), kvs(kv_out), kvs(first), kvs(first),
                   pl.BlockSpec((N_Q_HEADS, LANES), first)],
        scratch_shapes=[pltpu.VMEM((BLOCK, ATTN_WIDTH), F32), pltpu.VMEM((3 * BLOCK, KV_WIDTH), F32),
                        pltpu.VMEM((3 * BLOCK, KV_WIDTH), F32), pltpu.VMEM((BLOCK, KV_WIDTH), F32),
                        pltpu.VMEM((BLOCK, KV_WIDTH), F32), pltpu.VMEM((BLOCK, KV_WIDTH), F32),
                        pltpu.VMEM((BLOCK, KV_WIDTH), F32)],
        compiler_params=_cparams(("arbitrary",)), name="attn_bwd")(sinks, q, do, k, k, k, v, v, v, tabs, tabs, tabs)


def _cmul(ar, ai, br, bi):
    return ar * br - ai * bi, ar * bi + ai * br


def _cpow(lr, li, n):
    rr = ri = None
    br, bi = lr, li
    while n:
        if n & 1:
            rr, ri = (br, bi) if rr is None else _cmul(rr, ri, br, bi)
        n >>= 1
        if n:
            br, bi = _cmul(br, bi, br, bi)
    return rr, ri


def _shift_rows(x, d, reverse):
    rows = lax.broadcasted_iota(jnp.int32, x.shape, 0)
    if not reverse:
        return jnp.where(rows >= d, pltpu.roll(x, d, 0), 0.0)
    return jnp.where(rows < SUBLANES - d, pltpu.roll(x, SUBLANES - d, 0), 0.0)


def _sublane_powers(mr, mi, reverse):
    rows = lax.broadcasted_iota(jnp.int32, mr.shape, 0)
    e = SUBLANES - 1 - rows if reverse else rows
    pr, pi = jnp.ones_like(mr), jnp.zeros_like(mr)
    br, bi = mr, mi
    for d in (1, 2, 4):
        tr, ti = _cmul(pr, pi, br, bi)
        on = (e & d) != 0
        pr, pi = jnp.where(on, tr, pr), jnp.where(on, ti, pi)
        if d < 4:
            br, bi = _cmul(br, bi, br, bi)
    return pr, pi


def _inclusive_prefix(er, ei, mr, mi, reverse):
    ir, ii, pr, pi = er, ei, mr, mi
    for d in (1, 2, 4):
        tr, ti = _cmul(pr, pi, _shift_rows(ir, d, reverse), _shift_rows(ii, d, reverse))
        ir, ii = ir + tr, ii + ti
        if d < 4:
            pr, pi = _cmul(pr, pi, pr, pi)
    return ir, ii


def _chain_rows(a, t, seg):
    return pl.ds(a * SUBLANES * seg + t, SUBLANES, stride=seg)


def _seg_scan(xr_ref, xi_ref, lam, seg, nchain, reverse, store, init, extra=None):
    nt = len(lam)
    acc0 = () if extra is None else extra[1]

    def step(i, carry):
        hs, acc = carry
        t = seg - 1 - i if reverse else i
        out = []
        for a in range(nchain):
            sl = _chain_rows(a, t, seg)
            for j in range(nt):
                lr, li = lam[j]
                k = 2 * (a * nt + j)
                hr, hi = hs[k], hs[k + 1]
                nr = lr * hr - li * hi + xr_ref[j, sl, :]
                ni = lr * hi + li * hr + xi_ref[j, sl, :]
                if store:
                    xr_ref[j, sl, :] = nr
                    xi_ref[j, sl, :] = ni
                if extra is not None:
                    acc = extra[0](t, a, j, nr, ni, acc)
                out += [nr, ni]
        return tuple(out), acc

    return lax.fori_loop(0, seg, step, (tuple(init), acc0))


def _ssm_scan(xr_ref, xi_ref, lam, seg, nchain, reverse, extra=None):
    nt = len(lam)
    zero = [jnp.zeros((SUBLANES, LANES), F32)] * (2 * nt * nchain)
    ends, _ = _seg_scan(xr_ref, xi_ref, lam, seg, nchain, reverse, False, zero)
    init = [None] * (2 * nt * nchain)
    last = 0 if reverse else SUBLANES - 1
    for j in range(nt):
        mr, mi = _cpow(lam[j][0], lam[j][1], seg)
        m8r, m8i = _cpow(mr, mi, SUBLANES)
        pwr, pwi = _sublane_powers(mr, mi, reverse)
        gr = gi = jnp.zeros((SUBLANES, LANES), F32)
        for a in (reversed(range(nchain)) if reverse else range(nchain)):
            k = 2 * (a * nt + j)
            incr, inci = _inclusive_prefix(ends[k], ends[k + 1], mr, mi, reverse)
            tr, ti = _cmul(pwr, pwi, gr, gi)
            init[k] = _shift_rows(incr, 1, reverse) + tr
            init[k + 1] = _shift_rows(inci, 1, reverse) + ti
            g2r, g2i = _cmul(m8r, m8i, gr, gi)
            gr = g2r + jnp.broadcast_to(incr[last:last + 1, :], gr.shape)
            gi = g2i + jnp.broadcast_to(inci[last:last + 1, :], gi.shape)
    _, acc = _seg_scan(xr_ref, xi_ref, lam, seg, nchain, reverse, True, init, extra)
    return acc


def _ssm_tables(bb_re, bb_im, c_re, c_im):
    g = bb_re.shape[0]
    nt = g // GROUPS_PER_TILE
    eye = jnp.eye(g, dtype=F32)

    def b_tab(bb):
        full = jnp.einsum('gpc,gh->gchp', bb, eye).reshape(g * SSM_GROUP, g * SSM_STATE)
        full = full.reshape(g * SSM_GROUP // LANES, LANES, nt, TILE_STATES)
        return jnp.stack([full[t // 2, :, t, :] for t in range(nt)])

    def c_tab(c):
        full = jnp.einsum('gcp,gh->gphc', c, eye).reshape(g * SSM_STATE, g * SSM_GROUP)
        full = full.reshape(nt, TILE_STATES, g * SSM_GROUP // LANES, LANES)
        return jnp.stack([full[t, :, t // 2, :] for t in range(nt)])

    return b_tab(bb_re), b_tab(bb_im), c_tab(c_re), c_tab(c_im)


def _ssm_untable_b(db, g):
    nt = g // GROUPS_PER_TILE
    per_blk = LANES // SSM_GROUP
    db = db.reshape(nt, GROUPS_PER_TILE, SSM_STATE, per_blk, SSM_GROUP)
    out = []
    for t in range(nt):
        for gl in range(GROUPS_PER_TILE):
            out.append(db[t, gl, :, GROUPS_PER_TILE * (t % 2) + gl, :])
    return jnp.stack(out)


def _ssm_untable_c(dc, g):
    nt = g // GROUPS_PER_TILE
    per_blk = LANES // SSM_GROUP
    dc = dc.reshape(nt, per_blk, SSM_GROUP, GROUPS_PER_TILE, SSM_STATE)
    out = []
    for t in range(nt):
        for gl in range(GROUPS_PER_TILE):
            out.append(dc[t, GROUPS_PER_TILE * (t % 2) + gl, :, gl, :])
    return jnp.stack(out)


def _lam_tiles(lam_ref):
    out = []
    for j in range(TILE_STATES // LANES):
        out.append(jnp.broadcast_to(lam_ref[:, j * LANES:(j + 1) * LANES], (SUBLANES, LANES)))
    return out


def _scan_chains(lp):
    for n in (4, 2, 1):
        if lp % (SUBLANES * n) == 0 and (lp // SUBLANES) % 16 == 0:
            return n
    raise ValueError(lp)


def _split_tiles(dst_ref, rows, val):
    for j in range(val.shape[1] // LANES):
        dst_ref[j, rows, :] = val[:, j * LANES:(j + 1) * LANES]


def _cat_tiles(src_ref, rows):
    njt = src_ref.shape[0]
    return jnp.concatenate([src_ref[j, rows, :] for j in range(njt)], axis=1).astype(BF16)


def _ssm_fwd(u, lam_re, lam_im, tb_re, tb_im, tc_re, tc_im, d_skip):
    lp, w = u.shape
    nt = tb_re.shape[0]
    nchain = _scan_chains(lp)
    seg = lp // (SUBLANES * nchain)
    chunk = lp // SUBLANES
    njt = TILE_STATES // LANES

    def body(u_ref, lr_ref, li_ref, br_ref, bi_ref, cr_ref, ci_ref, d_ref, y_ref, xr, xi):
        t = pl.program_id(0)
        for s in range(SUBLANES):
            rs = pl.ds(s * chunk, chunk)
            ub = u_ref[rs, :].astype(BF16)
            _split_tiles(xr, rs, _dot(ub, br_ref[...], "nn"))
            _split_tiles(xi, rs, _dot(ub, bi_ref[...], "nn"))
        lrs, lis = _lam_tiles(lr_ref), _lam_tiles(li_ref)
        _ssm_scan(xr, xi, list(zip(lrs, lis)), seg, nchain, False)
        for s in range(SUBLANES):
            rs = pl.ds(s * chunk, chunk)
            y = _dot(_cat_tiles(xr, rs), cr_ref[...], "nn") - _dot(_cat_tiles(xi, rs), ci_ref[...], "nn")

            @pl.when(t % 2 == 0)
            def _():
                y_ref[rs, :] = y + d_ref[...] * u_ref[rs, :]

            @pl.when(t % 2 == 1)
            def _():
                y_ref[rs, :] += y

    blk = pl.BlockSpec((lp, LANES), lambda t: (0, t // 2))
    lam_spec = pl.BlockSpec((None, 1, TILE_STATES), lambda t: (t, 0, 0))
    b_spec = pl.BlockSpec((None, LANES, TILE_STATES), lambda t: (t, 0, 0))
    c_spec = pl.BlockSpec((None, TILE_STATES, LANES), lambda t: (t, 0, 0))
    return pl.pallas_call(
        body, out_shape=jax.ShapeDtypeStruct((lp, w), F32), grid=(nt,),
        in_specs=[blk, lam_spec, lam_spec, b_spec, b_spec, c_spec, c_spec,
                  pl.BlockSpec((1, LANES), lambda t: (0, t // 2))],
        out_specs=blk,
        scratch_shapes=[pltpu.VMEM((njt, lp, LANES), F32), pltpu.VMEM((njt, lp, LANES), F32)],
        compiler_params=_cparams(("arbitrary",)), name="ssm_fwd")(
            u, lam_re, lam_im, tb_re, tb_im, tc_re, tc_im, d_skip.reshape(1, w))


def _ssm_bwd(u, dy, lam_re, lam_im, tb_re, tb_im, tc_re, tc_im, d_skip):
    lp, w = u.shape
    nt = tb_re.shape[0]
    nchain = _scan_chains(lp)
    seg = lp // (SUBLANES * nchain)
    chunk = lp // SUBLANES
    njt = TILE_STATES // LANES
    tbt_re, tbt_im = jnp.swapaxes(tb_re, 1, 2), jnp.swapaxes(tb_im, 1, 2)
    tct_re, tct_im = jnp.swapaxes(tc_re, 1, 2), jnp.swapaxes(tc_im, 1, 2)

    def body(u_ref, dy_ref, lr_ref, li_ref, br_ref, bi_ref, btr_ref, bti_ref, ctr_ref, cti_ref, d_ref,
             du_ref, dlr_ref, dli_ref, dbr_ref, dbi_ref, dcr_ref, dci_ref, dd_ref, hr, hi, ar, ai):
        t = pl.program_id(0)
        lrs, lis = _lam_tiles(lr_ref), _lam_tiles(li_ref)
        for s in range(SUBLANES):
            rs = pl.ds(s * chunk, chunk)
            ub = u_ref[rs, :].astype(BF16)
            dyb = dy_ref[rs, :].astype(BF16)
            _split_tiles(hr, rs, _dot(ub, br_ref[...], "nn"))
            _split_tiles(hi, rs, _dot(ub, bi_ref[...], "nn"))
            _split_tiles(ar, rs, _dot(dyb, ctr_ref[...], "nn"))
            _split_tiles(ai, rs, -_dot(dyb, cti_ref[...], "nn"))
        _ssm_scan(hr, hi, list(zip(lrs, lis)), seg, nchain, False)

        def dlam_step(tt, a, j, a_r, a_i, acc):
            sl = _chain_rows(a, jnp.maximum(tt - 1, 0), seg)
            p_r, p_i = hr[j, sl, :], hi[j, sl, :]
            acc = list(acc)
            acc[2 * j] = acc[2 * j] + jnp.where(tt > 0, a_r * p_r + a_i * p_i, 0.0)
            acc[2 * j + 1] = acc[2 * j + 1] + jnp.where(tt > 0, a_i * p_r - a_r * p_i, 0.0)
            return tuple(acc)

        zero = tuple([jnp.zeros((SUBLANES, LANES), F32)] * (2 * njt))
        conj = [(lr, -li) for lr, li in zip(lrs, lis)]
        acc = list(_ssm_scan(ar, ai, conj, seg, nchain, True, (dlam_step, zero)))
        row0 = lax.broadcasted_iota(jnp.int32, (SUBLANES, LANES), 0) == 0
        for j in range(njt):
            cs = slice(j * LANES, (j + 1) * LANES)
            for a in range(nchain):
                p_r = _shift_rows(hr[j, _chain_rows(a, seg - 1, seg), :], 1, False)
                p_i = _shift_rows(hi[j, _chain_rows(a, seg - 1, seg), :], 1, False)
                if a > 0:
                    before = pl.ds(a * SUBLANES * seg - 1, 1)
                    p_r = jnp.where(row0, jnp.broadcast_to(hr[j, before, :], p_r.shape), p_r)
                    p_i = jnp.where(row0, jnp.broadcast_to(hi[j, before, :], p_i.shape), p_i)
                a_r, a_i = ar[j, _chain_rows(a, 0, seg), :], ai[j, _chain_rows(a, 0, seg), :]
                acc[2 * j] = acc[2 * j] + a_r * p_r + a_i * p_i
                acc[2 * j + 1] = acc[2 * j + 1] + a_i * p_r - a_r * p_i
            dlr_ref[:, cs] = jnp.sum(acc[2 * j], axis=0, keepdims=True)
            dli_ref[:, cs] = jnp.sum(acc[2 * j + 1], axis=0, keepdims=True)

        dd = jnp.zeros((1, LANES), F32)
        for s in range(SUBLANES):
            rs = pl.ds(s * chunk, chunk)
            ub = u_ref[rs, :].astype(BF16)
            dyv = dy_ref[rs, :]
            dyb = dyv.astype(BF16)
            arb, aib = _cat_tiles(ar, rs), _cat_tiles(ai, rs)
            hrb, hib = _cat_tiles(hr, rs), _cat_tiles(hi, rs)
            du = _dot(arb, btr_ref[...], "nn") + _dot(aib, bti_ref[...], "nn")
            upd = [(dbr_ref, _dot(arb, ub, "tn")), (dbi_ref, _dot(aib, ub, "tn")),
                   (dcr_ref, _dot(dyb, hrb, "tn")), (dci_ref, -_dot(dyb, hib, "tn"))]
            for ref, val in upd:
                if s == 0:
                    ref[...] = val
                else:
                    ref[...] += val
            rows = lax.broadcasted_iota(jnp.int32, (chunk, LANES), 0) + s * chunk
            keep = rows >= PAD_FRONT
            dd = dd + jnp.sum(dyv * u_ref[rs, :], axis=0, keepdims=True)

            @pl.when(t % 2 == 0)
            def _():
                du_ref[rs, :] = jnp.where(keep, du + d_ref[...] * dyv, 0.0)

            @pl.when(t % 2 == 1)
            def _():
                du_ref[rs, :] += jnp.where(keep, du, 0.0)

        @pl.when(t % 2 == 0)
        def _():
            dd_ref[...] = dd

    blk = pl.BlockSpec((lp, LANES), lambda t: (0, t // 2))
    vec = pl.BlockSpec((1, LANES), lambda t: (0, t // 2))
    lam_spec = pl.BlockSpec((None, 1, TILE_STATES), lambda t: (t, 0, 0))
    b_spec = pl.BlockSpec((None, LANES, TILE_STATES), lambda t: (t, 0, 0))
    c_spec = pl.BlockSpec((None, TILE_STATES, LANES), lambda t: (t, 0, 0))
    lam_shape = jax.ShapeDtypeStruct((nt, 1, TILE_STATES), F32)
    bt_shape = jax.ShapeDtypeStruct((nt, TILE_STATES, LANES), F32)
    ct_shape = jax.ShapeDtypeStruct((nt, LANES, TILE_STATES), F32)
    st = pltpu.VMEM((njt, lp, LANES), F32)
    return pl.pallas_call(
        body,
        out_shape=[jax.ShapeDtypeStruct((lp, w), F32), lam_shape, lam_shape, bt_shape, bt_shape, ct_shape, ct_shape,
                   jax.ShapeDtypeStruct((1, w), F32)],
        grid=(nt,),
        in_specs=[blk, blk, lam_spec, lam_spec, b_spec, b_spec, c_spec, c_spec, b_spec, b_spec, vec],
        out_specs=[blk, lam_spec, lam_spec, c_spec, c_spec, b_spec, b_spec, vec],
        scratch_shapes=[st, st, st, st],
        compiler_params=_cparams(("arbitrary",)), name="ssm_bwd")(
            u, dy, lam_re, lam_im, tb_re, tb_im, tbt_re, tbt_im, tct_re, tct_im, d_skip.reshape(1, w))


def _ssm_params(a_re, a_im, log_dt, b_re, b_im):
    dt = jnp.exp(log_dt)[:, None]
    mag = jnp.exp(a_re * dt)
    lb_re = mag * jnp.cos(a_im * dt)
    lb_im = mag * jnp.sin(a_im * dt)
    den = a_re * a_re + a_im * a_im
    num_re = lb_re - 1.0
    coef_re = (num_re * a_re + lb_im * a_im) / den
    coef_im = (lb_im * a_re - num_re * a_im) / den
    bb_re = coef_re[..., None] * b_re - coef_im[..., None] * b_im
    bb_im = coef_re[..., None] * b_im + coef_im[..., None] * b_re
    return lb_re, lb_im, bb_re, bb_im


def _gelu_fwd(y):
    lp, w = y.shape
    tm = _row_tile(lp)

    def body(y_ref, o_ref):
        o_ref[...] = _gelu(y_ref[...]).astype(BF16)

    return pl.pallas_call(
        body, out_shape=jax.ShapeDtypeStruct((lp, w), BF16), grid=(lp // tm,),
        in_specs=[pl.BlockSpec((tm, w), lambda i: (i, 0))], out_specs=pl.BlockSpec((tm, w), lambda i: (i, 0)),
        compiler_params=_cparams(("parallel",)), name="gelu_fwd")(y)


def _merge_fwd(o, yg, ga, gs, w3):
    lp, d = ga.shape
    d4 = w3.shape[3]
    kw = w3.shape[2]
    tm = _row_tile(lp)

    def epi(accs, in_refs, out_refs):
        attn, vv, gg = accs
        out_refs[0][...] = (_sigmoid(in_refs[5][...]) * attn
                            + _sigmoid(in_refs[6][...]) * (vv * _sigmoid(gg))).astype(BF16)

    wspec = lambda which: pl.BlockSpec((None, None, kw, d4), lambda j, i: (j, which, 0, 0))
    colspec = pl.BlockSpec((tm, d4), lambda j, i: (i, j))
    aspec = pl.BlockSpec((tm, kw), lambda j, i: (i, 0))
    (merged,) = _matmul(
        "merge_fwd", (N_CHIPS, lp // tm), None, [o, yg, w3, w3, w3, ga, gs],
        [aspec, aspec, wspec(0), wspec(1), wspec(2), colspec, colspec],
        [(0, 2, "nn", 0), (1, 3, "nn", 1), (1, 4, "nn", 2)], [(tm, d4)] * 3, epi,
        [jax.ShapeDtypeStruct((lp, d), BF16)], [colspec], ("parallel", "parallel"))
    return merged


def _out_proj(merged, w_out, h):
    lp, d = h.shape
    d4 = w_out.shape[1]
    tm = _row_tile(lp)

    def epi(accs, in_refs, out_refs):
        out_refs[0][...] = in_refs[2][...] + accs[0]

    (h_new,) = _matmul(
        "mix_out_proj", (lp // tm, N_CHIPS), 1, [merged, w_out, h],
        [pl.BlockSpec((tm, d4), lambda i, j: (i, j)), pl.BlockSpec((None, d4, d), lambda i, j: (j, 0, 0)),
         pl.BlockSpec((tm, d), lambda i, j: (i, 0))],
        [(0, 1, "nn", 0)], [(tm, d)], epi,
        [jax.ShapeDtypeStruct((lp, d), F32)], [pl.BlockSpec((tm, d), lambda i, j: (i, 0))],
        ("parallel", "arbitrary"))
    return h_new


def _merge_bwd(dhb, w_out, o, yg, ga, gs, w3):
    lp, d = ga.shape
    d4 = w3.shape[3]
    kw = w3.shape[2]
    tm = _row_tile(lp)

    def epi(accs, in_refs, out_refs):
        dm, attn, vv, gg = accs
        sa = _sigmoid(in_refs[7][...])
        ss = _sigmoid(in_refs[8][...])
        sg = _sigmoid(gg)
        ssm = vv * sg
        dssm = dm * ss
        out_refs[0][...] = (dm * sa).astype(BF16)
        out_refs[1][...] = (dssm * sg).astype(BF16)
        out_refs[2][...] = (dssm * vv * sg * (1.0 - sg)).astype(BF16)
        out_refs[3][...] = (dm * attn * sa * (1.0 - sa)).astype(BF16)
        out_refs[4][...] = (dm * ssm * ss * (1.0 - ss)).astype(BF16)

    wspec = lambda which: pl.BlockSpec((None, None, kw, d4), lambda j, i: (j, which, 0, 0))
    colspec = pl.BlockSpec((tm, d4), lambda j, i: (i, j))
    aspec = pl.BlockSpec((tm, kw), lambda j, i: (i, 0))
    shp = jax.ShapeDtypeStruct((lp, d), BF16)
    return _matmul(
        "merge_bwd", (N_CHIPS, lp // tm), None, [dhb, w_out, o, yg, w3, w3, w3, ga, gs],
        [pl.BlockSpec((tm, d), lambda j, i: (i, 0)), pl.BlockSpec((None, d4, d), lambda j, i: (j, 0, 0)),
         aspec, aspec, wspec(0), wspec(1), wspec(2), colspec, colspec],
        [(0, 1, "nt", 0), (2, 4, "nn", 1), (3, 5, "nn", 2), (3, 6, "nn", 3)], [(tm, d4)] * 4, epi,
        [shp] * 5, [colspec] * 5, ("parallel", "parallel"))


def _branch_bwd(dattn, dv, dg, w3, y):
    lp, d = dattn.shape
    d4 = w3.shape[3]
    kw = w3.shape[2]
    tm = _row_tile(lp)

    def epi(accs, in_refs, out_refs):
        out_refs[0][...] = accs[0].astype(BF16)
        out_refs[1][...] = accs[1] * _gelu_grad(in_refs[6][...])

    wspec = lambda which: pl.BlockSpec((None, None, kw, d4), lambda i, j: (j, which, 0, 0))
    colspec = pl.BlockSpec((tm, d4), lambda i, j: (i, j))
    rowspec = pl.BlockSpec((tm, kw), lambda i, j: (i, 0))
    return _matmul(
        "branch_bwd", (lp // tm, N_CHIPS), 1, [dattn, dv, dg, w3, w3, w3, y],
        [colspec, colspec, colspec, wspec(0), wspec(1), wspec(2), rowspec],
        [(0, 3, "nt", 0), (1, 4, "nt", 1), (2, 5, "nt", 1)], [(tm, kw)] * 2, epi,
        [jax.ShapeDtypeStruct((lp, kw), BF16), jax.ShapeDtypeStruct((lp, kw), F32)], [rowspec, rowspec],
        ("parallel", "arbitrary"))


def _tn_cols(x, ys, name):
    lp, kx = x.shape
    n = ys[0].shape[1]
    n4 = n // N_CHIPS
    tk = _tn_tiles(lp)

    def epi(accs, in_refs, out_refs):
        for acc, o in zip(accs, out_refs):
            o[...] = acc

    shp = jax.ShapeDtypeStruct((N_CHIPS, kx, n4), F32)
    return _matmul(
        name, (N_CHIPS, lp // tk), 1, [x] + list(ys),
        [pl.BlockSpec((tk, kx), lambda j, k: (k, 0))] + [pl.BlockSpec((tk, n4), lambda j, k: (k, j))] * len(ys),
        [(0, 1 + i, "tn", i) for i in range(len(ys))], [(kx, n4)] * len(ys), epi,
        [shp] * len(ys), [pl.BlockSpec((None, kx, n4), lambda j, k: (j, 0, 0))] * len(ys),
        ("parallel", "arbitrary"))


def _tn_full(x, y, name, tn_cols=None):
    lp, kx = x.shape
    n = y.shape[1]
    tk = _tn_tiles(lp)
    tn = n if tn_cols is None else tn_cols

    def epi(accs, in_refs, out_refs):
        out_refs[0][...] = accs[0]

    (out,) = _matmul(
        name, (n // tn, lp // tk), 1, [x, y],
        [pl.BlockSpec((tk, kx), lambda j, k: (k, 0)), pl.BlockSpec((tk, tn), lambda j, k: (k, j))],
        [(0, 1, "tn", 0)], [(kx, tn)], epi,
        [jax.ShapeDtypeStruct((kx, n), F32)], [pl.BlockSpec((kx, tn), lambda j, k: (0, j))],
        ("parallel", "arbitrary"))
    return out


def _in_proj_bwd(dz, w_in, dh, h_in, gain):
    lp, d = h_in.shape
    inw = w_in.shape[1]
    tm = _row_tile(lp)

    def epi(accs, in_refs, out_refs):
        i = pl.program_id(0)
        dx, dgrow = _rms_bwd_math(accs[0], in_refs[3][...], in_refs[4][...])
        out_refs[0][...] = in_refs[2][...] + dx

        @pl.when(i == 0)
        def _():
            out_refs[1][...] = jnp.zeros_like(out_refs[1])

        out_refs[1][...] += jnp.sum(dgrow, axis=0, keepdims=True)

    row = pl.BlockSpec((tm, d), lambda i: (i, 0))
    return _matmul(
        "mix_in_proj_bwd", (lp // tm,), None, [dz, w_in, dh, h_in, gain.reshape(1, d)],
        [pl.BlockSpec((tm, inw), lambda i: (i, 0)), pl.BlockSpec((d, inw), lambda i: (0, 0)), row, row,
         pl.BlockSpec((1, d), lambda i: (0, 0))],
        [(0, 1, "nt", 0)], [(tm, d)], epi,
        [jax.ShapeDtypeStruct((lp, d), F32), jax.ShapeDtypeStruct((1, d), F32)],
        [row, pl.BlockSpec((1, d), lambda i: (0, 0))], ("arbitrary",))


def _loss_head(h, gain, target):
    lp, d = h.shape
    nb = lp // BLOCK

    def body(h_ref, g_ref, t_ref, dh_ref, dg_ref, loss_ref):
        i = pl.program_id(0)

        @pl.when(i == 0)
        def _():
            dg_ref[...] = jnp.zeros_like(dg_ref)
            loss_ref[...] = jnp.zeros_like(loss_ref)
            dh_ref[...] = jnp.zeros_like(dh_ref)

        @pl.when(i > 0)
        def _():
            x = h_ref[...]
            g = g_ref[...]
            r = lax.rsqrt(jnp.mean(x * x, axis=-1, keepdims=True) + EPS)
            err = x * r * g - t_ref[...]
            loss_ref[...] += jnp.zeros_like(loss_ref) + 0.5 * jnp.sum(jnp.sum(err * err, axis=-1, keepdims=True)) / d
            dx, dgrow = _rms_bwd_math(err * (1.0 / d), x, g)
            dh_ref[...] = dx
            dg_ref[...] += jnp.sum(dgrow, axis=0, keepdims=True)

    row = pl.BlockSpec((BLOCK, d), lambda i: (i, 0))
    one = pl.BlockSpec((1, d), lambda i: (0, 0))
    return pl.pallas_call(
        body,
        out_shape=[jax.ShapeDtypeStruct((lp, d), F32), jax.ShapeDtypeStruct((1, d), F32),
                   jax.ShapeDtypeStruct((SUBLANES, LANES), F32)],
        grid=(nb,),
        in_specs=[row, one, pl.BlockSpec((BLOCK, d), lambda i: (jnp.maximum(i - 1, 0), 0))],
        out_specs=[row, one, pl.BlockSpec((SUBLANES, LANES), lambda i: (0, 0))],
        compiler_params=_cparams(("arbitrary",)), name="loss_head")(h, gain.reshape(1, d), target)


def _adam_math(w, g, m, v):
    m = ADAM_B1 * m + (1.0 - ADAM_B1) * g
    v = ADAM_B2 * v + (1.0 - ADAM_B2) * (g * g)
    m_hat = m / (1.0 - ADAM_B1 ** ADAM_STEP)
    v_hat = v / (1.0 - ADAM_B2 ** ADAM_STEP)
    delta = -ADAM_LR * (m_hat / (jnp.sqrt(v_hat) + ADAM_EPS) + ADAM_WD * w)
    return delta, m, v


def _adamw_layers(w, m, v, mine, other, pos, name):
    depth, r, c = w.shape
    half = r // 2
    tr = _div_tile(half, c * 4)
    nh = half // tr

    def body(*refs):
        pos_ref, w_ref, m_ref, v_ref = refs[:4]
        mine_refs = refs[4:4 + depth]
        other_refs = refs[4 + depth:4 + 2 * depth]
        g_out, d_out, m_out, v_out = refs[4 + 2 * depth:]
        layer, i = pl.program_id(0), pl.program_id(1)
        is_mine = (i // nh) == pos_ref[0]

        def update(g):
            delta, nm, nv = _adam_math(w_ref[...], g, m_ref[...], v_ref[...])
            g_out[...] = g
            d_out[...] = delta
            m_out[...] = nm
            v_out[...] = nv

        for l in range(depth):
            @pl.when((layer == l) & is_mine)
            def _(l=l):
                update(mine_refs[l][...])

            @pl.when((layer == l) & jnp.logical_not(is_mine))
            def _(l=l):
                update(other_refs[l][...])

    stacked = pl.BlockSpec((None, tr, c), lambda l, i, p: (l, i, 0))

    def gspec(layer, is_other):
        def imap(l, i, p):
            first = jnp.where(is_other, 1 - p[0], p[0]) * nh
            here = jnp.clip(i - first, 0, nh - 1)
            return (jnp.where(l == layer, here, jnp.where(l < layer, 0, nh - 1)), 0)
        return pl.BlockSpec((tr, c), imap)

    shp = jax.ShapeDtypeStruct((depth, r, c), F32)
    grid_spec = pltpu.PrefetchScalarGridSpec(
        num_scalar_prefetch=1, grid=(depth, 2 * nh),
        in_specs=[stacked] * 3 + [gspec(l, 0) for l in range(depth)] + [gspec(l, 1) for l in range(depth)],
        out_specs=[stacked] * 4)
    return pl.pallas_call(
        body, out_shape=[shp] * 4, grid_spec=grid_spec,
        compiler_params=_cparams(("arbitrary", "arbitrary")), name=name)(pos, w, m, v, *mine, *other)


def _adamw_flat(w, g, m, v, name):
    r, c = w.shape
    tr = _div_tile(r, c * 4)

    def body(w_ref, g_ref, m_ref, v_ref, d_out, m_out, v_out):
        delta, nm, nv = _adam_math(w_ref[...], g_ref[...], m_ref[...], v_ref[...])
        d_out[...] = delta
        m_out[...] = nm
        v_out[...] = nv

    spec = pl.BlockSpec((tr, c), lambda i: (i, 0))
    shp = jax.ShapeDtypeStruct((r, c), F32)
    return pl.pallas_call(
        body, out_shape=[shp] * 3, grid=(r // tr,), in_specs=[spec] * 4, out_specs=[spec] * 3,
        compiler_params=_cparams(("parallel",)), name=name)(w, g, m, v)


def _mesh_pos():
    return lax.axis_index("x"), lax.axis_index("y"), lax.axis_index("c")


def _row_half(ref, which, lead):
    half = ref.shape[lead] // 2
    idx = (slice(None),) * lead + (pl.ds(which * half, half), slice(None))
    return ref.at[idx]


HBM_SPEC = pl.BlockSpec(memory_space=pltpu.HBM)


def _all_gather_chips(arrs, name):
    n = len(arrs)

    def body(*refs):
        ins, outs = refs[:n], refs[n:2 * n]
        send_sems, recv_sems, local_sems = refs[2 * n:]
        x, y, c = _mesh_pos()
        sibling = (x, y, 1 - c)
        chips = [(1 - x, y), (x, 1 - y), (1 - x, 1 - y)]
        mine = 2 * x + y

        def slot(k, chip, which):
            lead = len(ins[k].shape) - 2
            return _row_half(outs[k].at[2 * chip[0] + chip[1]], which, lead)

        def copy(k, j, src, dst, to):
            return pltpu.make_async_remote_copy(
                src_ref=src, dst_ref=dst, send_sem=send_sems.at[6 * k + j], recv_sem=recv_sems.at[6 * k + j],
                device_id=to, device_id_type=MESH)

        local, first, passed = [], [], []
        for k in range(n):
            lead = len(ins[k].shape) - 2
            lc = pltpu.make_async_copy(ins[k], outs[k].at[mine], local_sems.at[k])
            lc.start()
            local.append(lc)
            for j, chip in enumerate(chips):
                cp = copy(k, j, _row_half(ins[k], c, lead), slot(k, (x, y), c), (*chip, c))
                cp.start()
                first.append(cp)
        for j, chip in enumerate(chips):
            for k in range(n):
                copy(k, j, slot(k, chip, c), slot(k, chip, c), sibling).wait_recv()
                cp = copy(k, 3 + j, slot(k, chip, c), slot(k, chip, c), sibling)
                cp.start()
                passed.append(cp)
        for j, chip in enumerate(chips):
            for k in range(n):
                copy(k, 3 + j, slot(k, chip, 1 - c), slot(k, chip, 1 - c), sibling).wait_recv()
        for cp in first + passed:
            cp.wait_send()
        for lc in local:
            lc.wait()

    return pl.pallas_call(
        body, out_shape=[jax.ShapeDtypeStruct((N_CHIPS,) + a.shape, a.dtype) for a in arrs],
        in_specs=[HBM_SPEC] * n, out_specs=[HBM_SPEC] * n,
        scratch_shapes=[pltpu.SemaphoreType.DMA((6 * n,)), pltpu.SemaphoreType.DMA((6 * n,)),
                        pltpu.SemaphoreType.DMA((n,))],
        name=name)(*arrs)


def _all_gather_devices(x_shard, name):
    m_per, ncol = x_shard.shape

    def body(x_ref, out_ref, send_sems, recv_sems, local_sem):
        x, y, c = _mesh_pos()
        me, sibling = (x, y, c), (x, y, 1 - c)
        chips = [(1 - x, y), (x, 1 - y), (1 - x, 1 - y)]

        def rows(px, py, pc):
            return out_ref.at[4 * px + 2 * py + pc]

        def copy(k, block, to, src=None):
            return pltpu.make_async_remote_copy(
                src_ref=rows(*block) if src is None else src, dst_ref=rows(*block),
                send_sem=send_sems.at[k], recv_sem=recv_sems.at[k], device_id=to, device_id_type=MESH)

        mine = pltpu.make_async_copy(x_ref, rows(*me), local_sem)
        mine.start()
        first = [copy(0, me, sibling, src=x_ref)]
        first += [copy(1 + j, me, (*chip, c), src=x_ref) for j, chip in enumerate(chips)]
        for cp in first:
            cp.start()
        passed = [copy(4 + j, (*chip, c), sibling) for j, chip in enumerate(chips)]
        for j, chip in enumerate(chips):
            copy(1 + j, (*chip, c), me).wait_recv()
            passed[j].start()
        copy(0, sibling, me).wait_recv()
        for j, chip in enumerate(chips):
            copy(4 + j, (*chip, 1 - c), me).wait_recv()
        for cp in first + passed:
            cp.wait_send()
        mine.wait()

    return pl.pallas_call(
        body, out_shape=jax.ShapeDtypeStruct((8, m_per, ncol), x_shard.dtype),
        in_specs=[pl.BlockSpec(memory_space=pltpu.VMEM)], out_specs=pl.BlockSpec(memory_space=pltpu.VMEM),
        scratch_shapes=[pltpu.SemaphoreType.DMA((7,)), pltpu.SemaphoreType.DMA((7,)), pltpu.SemaphoreType.DMA],
        compiler_params=pltpu.CompilerParams(vmem_limit_bytes=VMEM_LIMIT), name=name)(x_shard)


def _sum_devices(g8, name):
    _, r, c = g8.shape
    tr = _div_tile(r, c * 4 * 8)

    def body(g_ref, o_ref):
        acc = g_ref[0]
        for dev in range(1, 8):
            acc = acc + g_ref[dev]
        o_ref[...] = acc

    return pl.pallas_call(
        body, out_shape=jax.ShapeDtypeStruct((r, c), F32), grid=(r // tr,),
        in_specs=[pl.BlockSpec((8, tr, c), lambda i: (0, i, 0))], out_specs=pl.BlockSpec((tr, c), lambda i: (i, 0)),
        compiler_params=_cparams(("parallel",)), name=name)(g8)


def _exchange_sibling_halves(arrs, name):
    n = len(arrs)

    def body(*refs):
        ins, outs = refs[:n], refs[n:2 * n]
        send_sems, recv_sems = refs[2 * n:]
        x, y, c = _mesh_pos()
        cps = []
        for k in range(n):
            cp = pltpu.make_async_remote_copy(
                src_ref=_row_half(ins[k], 1 - c, 1), dst_ref=outs[k], send_sem=send_sems.at[k],
                recv_sem=recv_sems.at[k], device_id=(x, y, 1 - c), device_id_type=MESH)
            cp.start()
            cps.append(cp)
        for cp in cps:
            cp.wait()

    return pl.pallas_call(
        body,
        out_shape=[jax.ShapeDtypeStruct((a.shape[0], a.shape[1] // 2, a.shape[2]), a.dtype) for a in arrs],
        in_specs=[HBM_SPEC] * n, out_specs=[HBM_SPEC] * n,
        scratch_shapes=[pltpu.SemaphoreType.DMA((n,)), pltpu.SemaphoreType.DMA((n,))], name=name)(*arrs)


def _chip_partial(arr, recv, pos, name):
    nslab, r, c = arr.shape
    half = r // 2

    def body(pos_ref, a_ref, b_ref, o_ref):
        o_ref[...] = (a_ref[...] + b_ref[...]).astype(BF16)

    grid_spec = pltpu.PrefetchScalarGridSpec(
        num_scalar_prefetch=1, grid=(nslab,),
        in_specs=[pl.BlockSpec((None, half, c), lambda j, p: (j, p[0], 0)),
                  pl.BlockSpec((None, half, c), lambda j, p: (j, 0, 0))],
        out_specs=pl.BlockSpec((None, half, c), lambda j, p: (j, 0, 0)))
    return pl.pallas_call(
        body, out_shape=jax.ShapeDtypeStruct((nslab, half, c), BF16), grid_spec=grid_spec,
        compiler_params=_cparams(("parallel",)), name=name)(pos, arr, recv)


def _exchange_chip_partials(parts, name):
    n = len(parts)

    def body(*refs):
        ins, outs = refs[:n], refs[n:2 * n]
        send_sems, recv_sems = refs[2 * n:]
        x, y, c = _mesh_pos()
        chips = [(1 - x, y), (x, 1 - y), (1 - x, 1 - y)]
        cps = []
        for k in range(n):
            for j, chip in enumerate(chips):
                cp = pltpu.make_async_remote_copy(
                    src_ref=ins[k].at[2 * chip[0] + chip[1]], dst_ref=outs[k].at[j],
                    send_sem=send_sems.at[3 * k + j], recv_sem=recv_sems.at[3 * k + j],
                    device_id=(*chip, c), device_id_type=MESH)
                cp.start()
                cps.append(cp)
        for cp in cps:
            cp.wait()

    return pl.pallas_call(
        body, out_shape=[jax.ShapeDtypeStruct((3,) + p.shape[1:], p.dtype) for p in parts],
        in_specs=[HBM_SPEC] * n, out_specs=[HBM_SPEC] * n,
        scratch_shapes=[pltpu.SemaphoreType.DMA((3 * n,)), pltpu.SemaphoreType.DMA((3 * n,))], name=name)(*parts)


def _reduce_half(arr, recv, got, pos, name):
    nslab, r, c = arr.shape
    half = r // 2

    def body(pos_ref, a_ref, b_ref, g_ref, o_ref):
        acc = a_ref[...] + b_ref[...]
        for j in range(3):
            acc = acc + g_ref[j].astype(F32)
        o_ref[...] = acc

    grid_spec = pltpu.PrefetchScalarGridSpec(
        num_scalar_prefetch=1, grid=(1,),
        in_specs=[pl.BlockSpec((None, half, c), lambda i, p: (p[1], p[0], 0)),
                  pl.BlockSpec((None, half, c), lambda i, p: (p[1], 0, 0)),
                  pl.BlockSpec((3, half, c), lambda i, p: (0, 0, 0))],
        out_specs=pl.BlockSpec((half, c), lambda i, p: (0, 0)))
    return pl.pallas_call(
        body, out_shape=jax.ShapeDtypeStruct((half, c), F32), grid_spec=grid_spec,
        compiler_params=_cparams(("arbitrary",)), name=name)(pos, arr, recv, got)


def _share_halves(halves, name):
    n = len(halves)

    def body(*refs):
        ins, outs = refs[:n], refs[n:2 * n]
        send_sems, recv_sems = refs[2 * n:]
        x, y, c = _mesh_pos()
        cps = []
        for k in range(n):
            cp = pltpu.make_async_remote_copy(
                src_ref=ins[k], dst_ref=outs[k], send_sem=send_sems.at[k], recv_sem=recv_sems.at[k],
                device_id=(x, y, 1 - c), device_id_type=MESH)
            cp.start()
            cps.append(cp)
        for cp in cps:
            cp.wait()

    return pl.pallas_call(
        body, out_shape=[jax.ShapeDtypeStruct(h.shape, h.dtype) for h in halves],
        in_specs=[HBM_SPEC] * n, out_specs=[HBM_SPEC] * n,
        scratch_shapes=[pltpu.SemaphoreType.DMA((n,)), pltpu.SemaphoreType.DMA((n,))], name=name)(*halves)


def _reduce_scatter(arrs, pos, tag):
    recv = _exchange_sibling_halves(arrs, "rs_sibling_" + tag)
    parts = [_chip_partial(a, r, pos, "rs_partial") for a, r in zip(arrs, recv)]
    got = _exchange_chip_partials(parts, "rs_chips_" + tag)
    halves = [_reduce_half(a, r, g, pos, "rs_reduce") for a, r, g in zip(arrs, recv, got)]
    return halves, _share_halves(halves, "rs_share_" + tag)


def _layer_fwd(h, p, tabs):
    h1, ffn1_saved = _ffn_fwd(h, p["ffn1_norm"], p["wgu1"], p["wd1"])
    n = _rms_fwd(h1, p["mix_norm"], "rms_fwd_mix")
    ssm_w = p["ssm_d"].shape[0]
    q, k, v, u, ga, gs = _in_proj(n, p["w_in"], tabs, ssm_w)
    o = _attn_fwd(q, k, v, p["attn_sinks"])
    y = _ssm_fwd(u, *p["ssm_tabs"], p["ssm_d"])
    yg = _gelu_fwd(y)
    merged = _merge_fwd(o, yg, ga, gs, p["w3"])
    h2 = _out_proj(merged, p["w_out"], h1)
    h3, ffn2_saved = _ffn_fwd(h2, p["ffn2_norm"], p["wgu2"], p["wd2"])
    saved = dict(h0=h, h1=h1, h2=h2, ffn1=ffn1_saved, ffn2=ffn2_saved, q=q, k=k, v=v, u=u, ga=ga, gs=gs, o=o, y=y,
                 yg=yg, merged=merged)
    return h3, saved


def _layer_bwd(dh, p, s, tabs):
    g = {}
    dh2, g["ffn2_norm"], g["wg2"], g["wu2"], g["wd2"] = _ffn_bwd(
        dh, s["h2"], p["ffn2_norm"], p["wgu2"], p["wd2"], s["ffn2"])
    lp, d = dh2.shape
    d4 = d // N_CHIPS
    dhb = _scale_cast(dh2, 1.0, "mix_dh_cast")
    g["w_out"] = _tn_full(s["merged"], dhb, "mix_dw_out").reshape(N_CHIPS, d4, d)
    dattn, dv, dg, dga, dgs = _merge_bwd(dhb, p["w_out"], s["o"], s["yg"], s["ga"], s["gs"], p["w3"])
    (g["w_ap"],) = _tn_cols(s["o"], [dattn], "mix_dw_ap")
    g["w_gv"], g["w_gg"] = _tn_cols(s["yg"], [dv, dg], "mix_dw_glu")
    do, dy = _branch_bwd(dattn, dv, dg, p["w3"], s["y"])
    dq, dk, dvv, dkm, dvm, dsink = _attn_bwd(s["q"], s["k"], s["v"], do, p["attn_sinks"], tabs)
    g["attn_sinks"] = dsink[:, 0]
    du, dlr, dli, dbr, dbi, dcr, dci, dd = _ssm_bwd(s["u"], dy, *p["ssm_tabs"], p["ssm_d"])
    ngrp = p["ssm_d"].shape[0] // SSM_GROUP
    g["ssm_lam"] = (dlr.reshape(ngrp, SSM_STATE), dli.reshape(ngrp, SSM_STATE),
                    _ssm_untable_b(dbr, ngrp), _ssm_untable_b(dbi, ngrp))
    g["ssm_c_re"] = _ssm_untable_c(dcr, ngrp)
    g["ssm_c_im"] = _ssm_untable_c(dci, ngrp)
    g["ssm_d"] = dd[0]
    dk = dk.at[:BLOCK].add(dkm)
    dvv = dvv.at[:BLOCK].add(dvm)
    dz = jnp.concatenate([dq.astype(BF16), dk.astype(BF16), dvv.astype(BF16), du.astype(BF16), dga, dgs], axis=1)
    n = _rms_fwd(s["h1"], p["mix_norm"], "rms_fwd_mix")
    inw = p["w_in"].shape[1]
    tn_cols = inw // 2 if (inw // 2) % LANES == 0 else None
    g["w_in"] = _tn_full(n, dz, "mix_dw_in", tn_cols)
    dh1, g["mix_norm"] = _in_proj_bwd(dz, p["w_in"], dh2, s["h1"], p["mix_norm"])
    dh0, g["ffn1_norm"], g["wg1"], g["wu1"], g["wd1"] = _ffn_bwd(
        dh1, s["h0"], p["ffn1_norm"], p["wgu1"], p["wd1"], s["ffn1"])
    return dh0, g


BIG = ["ffn1_w_gate", "ffn1_w_up", "ffn1_w_down", "w_in", "w_attn_proj", "w_glu_v", "w_glu_g", "w_out",
       "ffn2_w_gate", "ffn2_w_up", "ffn2_w_down"]
SMALL = ["ffn1_norm", "mix_norm", "attn_sinks", "ssm_a_re", "ssm_a_im", "ssm_log_dt", "ssm_b_re", "ssm_b_im",
         "ssm_c_re", "ssm_c_im", "ssm_d", "ffn2_norm", "final_norm"]
WEIGHTS = ["meta_tokens", "ffn1_norm", "ffn1_w_gate", "ffn1_w_up", "ffn1_w_down", "mix_norm", "w_in", "attn_sinks",
           "ssm_a_re", "ssm_a_im", "ssm_log_dt", "ssm_b_re", "ssm_b_im", "ssm_c_re", "ssm_c_im", "ssm_d",
           "w_attn_proj", "w_glu_v", "w_glu_g", "w_out", "ffn2_norm", "ffn2_w_gate", "ffn2_w_up", "ffn2_w_down",
           "final_norm"]


def _pack_small(tree):
    flat = jnp.concatenate([tree[k].reshape(-1) for k in SMALL + ["meta_tokens"]])
    rows = -(-flat.shape[0] // (LANES * LANES)) * LANES
    return jnp.pad(flat, (0, rows * LANES - flat.shape[0])).reshape(rows, LANES)


def _unpack_small(packed, like):
    flat = packed.reshape(-1)
    out, off = {}, 0
    for k in SMALL + ["meta_tokens"]:
        size = math.prod(like[k].shape)
        out[k] = flat[off:off + size].reshape(like[k].shape)
        off += size
    return out


def kernel(x, meta_tokens, ffn1_norm, ffn1_w_gate, ffn1_w_up, ffn1_w_down, mix_norm, w_in, attn_sinks, ssm_a_re, ssm_a_im, ssm_log_dt, ssm_b_re, ssm_b_im, ssm_c_re, ssm_c_im, ssm_d, w_attn_proj, w_glu_v, w_glu_g, w_out, ffn2_norm, ffn2_w_gate, ffn2_w_up, ffn2_w_down, final_norm, loss_target, m_meta_tokens, m_ffn1_norm, m_ffn1_w_gate, m_ffn1_w_up, m_ffn1_w_down, m_mix_norm, m_w_in, m_attn_sinks, m_ssm_a_re, m_ssm_a_im, m_ssm_log_dt, m_ssm_b_re, m_ssm_b_im, m_ssm_c_re, m_ssm_c_im, m_ssm_d, m_w_attn_proj, m_w_glu_v, m_w_glu_g, m_w_out, m_ffn2_norm, m_ffn2_w_gate, m_ffn2_w_up, m_ffn2_w_down, m_final_norm, v_meta_tokens, v_ffn1_norm, v_ffn1_w_gate, v_ffn1_w_up, v_ffn1_w_down, v_mix_norm, v_w_in, v_attn_sinks, v_ssm_a_re, v_ssm_a_im, v_ssm_log_dt, v_ssm_b_re, v_ssm_b_im, v_ssm_c_re, v_ssm_c_im, v_ssm_d, v_w_attn_proj, v_w_glu_v, v_w_glu_g, v_w_out, v_ffn2_norm, v_ffn2_w_gate, v_ffn2_w_up, v_ffn2_w_down, v_final_norm):
    args = dict(locals())
    w = {k: args[k] for k in WEIGHTS}
    m = {k: args["m_" + k] for k in WEIGHTS}
    v = {k: args["v_" + k] for k in WEIGHTS}
    depth = ffn1_norm.shape[0]
    seq, d = x.shape[1], x.shape[2]
    lp = seq + BLOCK
    xi, yi, ci = _mesh_pos()
    pos = jnp.stack([ci, 2 * xi + yi]).astype(jnp.int32)

    tabs = _rope_tables(lp)
    (meta_all,) = _all_gather_chips([meta_tokens], "gather_meta")
    meta_full = jnp.concatenate([meta_all[j] for j in range(N_CHIPS)], axis=1)
    layers = []
    for l in range(depth):
        shards = [
            jnp.stack([ffn1_w_gate[l], ffn1_w_up[l]]).astype(BF16),
            ffn1_w_down[l].astype(BF16),
            w_in[l].astype(BF16),
            jnp.stack([w_attn_proj[l], w_glu_v[l], w_glu_g[l]]).astype(BF16),
            w_out[l].astype(BF16),
            jnp.stack([ffn2_w_gate[l], ffn2_w_up[l]]).astype(BF16),
            ffn2_w_down[l].astype(BF16),
        ]
        wgu1, wd1, win, w3, wout, wgu2, wd2 = _all_gather_chips(shards, "gather_weights")
        lb_re, lb_im, bb_re, bb_im = _ssm_params(ssm_a_re[l], ssm_a_im[l], ssm_log_dt[l], ssm_b_re[l], ssm_b_im[l])
        ngrp = lb_re.shape[0]
        nt = ngrp // GROUPS_PER_TILE
        ssm_tabs = (lb_re.reshape(nt, 1, TILE_STATES), lb_im.reshape(nt, 1, TILE_STATES),
                    *_ssm_tables(bb_re, bb_im, ssm_c_re[l], ssm_c_im[l]))
        layers.append(dict(
            wgu1=wgu1, wd1=wd1, wgu2=wgu2, wd2=wd2,
            w_in=jnp.concatenate([win[j] for j in range(N_CHIPS)], axis=1), w3=w3, w_out=wout,
            ffn1_norm=ffn1_norm[l], mix_norm=mix_norm[l], ffn2_norm=ffn2_norm[l], attn_sinks=attn_sinks[l],
            ssm_d=ssm_d[l], ssm_tabs=ssm_tabs))

    h = jnp.concatenate([jnp.zeros((PAD_FRONT, d), F32), meta_full, x[0]], axis=0)
    saved = []
    for l in range(depth):
        h, s = _layer_fwd(h, layers[l], tabs)
        saved.append(s)
    dh, g_final, loss_acc = _loss_head(h, final_norm, loss_target[0])
    loss = lax.psum(loss_acc[0, 0], ("x", "y", "c"))

    grads = [None] * depth
    for l in reversed(range(depth)):
        dh, grads[l] = _layer_bwd(dh, layers[l], saved[l], tabs)
    grad_x = dh[BLOCK:][None]
    dmeta_local = dh[PAD_FRONT:BLOCK]

    small = {k: [] for k in SMALL}
    for l in range(depth):
        gl = grads[l]
        _, vjp = jax.vjp(_ssm_params, ssm_a_re[l], ssm_a_im[l], ssm_log_dt[l], ssm_b_re[l], ssm_b_im[l])
        da_re, da_im, dlog_dt, db_re, db_im = vjp(gl["ssm_lam"])
        for k, val in (("ffn1_norm", gl["ffn1_norm"][0]), ("mix_norm", gl["mix_norm"][0]),
                       ("attn_sinks", gl["attn_sinks"]), ("ssm_a_re", da_re), ("ssm_a_im", da_im),
                       ("ssm_log_dt", dlog_dt), ("ssm_b_re", db_re), ("ssm_b_im", db_im),
                       ("ssm_c_re", gl["ssm_c_re"]), ("ssm_c_im", gl["ssm_c_im"]), ("ssm_d", gl["ssm_d"]),
                       ("ffn2_norm", gl["ffn2_norm"][0])):
            small[k].append(val)
    small_local = {k: jnp.stack(vals) for k, vals in small.items() if k != "final_norm"}
    small_local["final_norm"] = g_final[0]
    small_local["meta_tokens"] = dmeta_local
    like = dict(small_local)
    g_small = _sum_devices(_all_gather_devices(_pack_small(small_local), "gather_small_grads"), "sum_small_grads")
    g_small_tree = _unpack_small(g_small, like)
    d4 = d // N_CHIPS
    chip = 2 * xi + yi
    g_meta = lax.dynamic_slice_in_dim(g_small_tree["meta_tokens"], chip * d4, d4, axis=1)

    reduced = []
    for l in range(depth):
        gl = grads[l]
        inw4 = w_in.shape[2]
        dwin = jnp.stack([gl["w_in"][:, j * inw4:(j + 1) * inw4] for j in range(N_CHIPS)])
        arrs = [gl["wg1"], gl["wu1"], gl["wd1"], dwin, gl["w_ap"], gl["w_gv"], gl["w_gg"], gl["w_out"],
                gl["wg2"], gl["wu2"], gl["wd2"]]
        reduced.append(_reduce_scatter(arrs, pos, "grads"))

    g_out, delta, new_m, new_v = {}, {}, {}, {}
    for i, k in enumerate(BIG):
        g_out[k], delta[k], new_m[k], new_v[k] = _adamw_layers(
            w[k], m[k], v[k], [reduced[l][0][i] for l in range(depth)], [reduced[l][1][i] for l in range(depth)],
            pos, "adamw_" + k)
    small_names = SMALL + ["meta_tokens"]
    w_small = {k: w[k] for k in small_names}
    m_small = {k: m[k] for k in small_names}
    v_small = {k: v[k] for k in small_names}
    g_small_local = dict(g_small_tree)
    g_small_local["meta_tokens"] = g_meta
    d_s, m_s, v_s = _adamw_flat(_pack_small(w_small), _pack_small(g_small_local), _pack_small(m_small),
                                _pack_small(v_small), "adamw_small")
    for tree, packed in ((delta, d_s), (new_m, m_s), (new_v, v_s)):
        tree.update(_unpack_small(packed, w_small))
    for k in small_names:
        g_out[k] = g_small_local[k]

    return (loss, grad_x, *[g_out[k] for k in WEIGHTS], *[delta[k] for k in WEIGHTS],
            *[new_m[k] for k in WEIGHTS], *[new_v[k] for k in WEIGHTS])
```

```python
import functools
import math

import jax
import jax.numpy as jnp
from jax import lax
from jax.experimental import pallas as pl
from jax.experimental.pallas import tpu as pltpu

F32 = jnp.float32
BF16 = jnp.bfloat16

N_META = 16
HEAD_DIM = 64
N_Q_HEADS = 8
N_KV_HEADS = 2
Q_PER_KV = N_Q_HEADS // N_KV_HEADS
ATTN_WIDTH = N_Q_HEADS * HEAD_DIM
KV_WIDTH = N_KV_HEADS * HEAD_DIM
BLOCK = 128
PAD_FRONT = BLOCK - N_META
ROPE_THETA = 500000.0
ROT_DIM = HEAD_DIM // 4
SSM_GROUP = 16
SSM_STATE = 64
GROUPS_PER_TILE = 4
TILE_STATES = GROUPS_PER_TILE * SSM_STATE
LANES = 128
SUBLANES = 8
EPS = 1e-6
NEG_INF = -1e30
N_CHIPS = 4

ADAM_LR = 0.001
ADAM_B1 = 0.9
ADAM_B2 = 0.999
ADAM_EPS = 1e-08
ADAM_WD = 0.01
ADAM_STEP = 10

VMEM_LIMIT = 56 * 1024 * 1024
MESH = pl.DeviceIdType.MESH


def _cparams(sem=None):
    return pltpu.CompilerParams(dimension_semantics=sem, vmem_limit_bytes=VMEM_LIMIT)


def _row_tile(rows, limit=512):
    best = None
    for t in range(128, limit + 1, 128):
        if rows % t == 0:
            best = t
    assert best is not None, rows
    return best


def _div_tile(rows, row_bytes, max_bytes=1 << 20, mult=8):
    best = None
    for t in range(mult, rows + 1, mult):
        if rows % t == 0 and t * row_bytes <= max_bytes:
            best = t
    if best is None:
        best = rows
    return best


def _dot(a, b, mode):
    if mode == "nn":
        dims = (((1,), (0,)), ((), ()))
    elif mode == "nt":
        dims = (((1,), (1,)), ((), ()))
    else:
        dims = (((0,), (0,)), ((), ()))
    return lax.dot_general(a.astype(BF16), b.astype(BF16), dims, preferred_element_type=F32)


def _sigmoid(x):
    return 1.0 / (1.0 + jnp.exp(-x))


_GELU_C = math.sqrt(2.0 / math.pi)


def _gelu(x):
    return 0.5 * x * (1.0 + jnp.tanh(_GELU_C * (x + 0.044715 * x * x * x)))


def _gelu_grad(x):
    t = jnp.tanh(_GELU_C * (x + 0.044715 * x * x * x))
    return 0.5 * (1.0 + t) + 0.5 * x * (1.0 - t * t) * _GELU_C * (1.0 + 3.0 * 0.044715 * x * x)


class _Comm:
    def __init__(self, tag, ins, out_shapes, sems, start, mid, finish):
        self.tag, self.ins, self.out_shapes, self.sems = tag, list(ins), list(out_shapes), list(sems)
        self.start, self.mid, self.finish = start, mid, finish


HBM_SPEC = pl.BlockSpec(memory_space=pltpu.HBM)


def _hosted_call(body, comm, *, out_shape, grid, in_specs, out_specs, scratch_shapes, sem, name, args):
    out_shape, in_specs, out_specs = list(out_shape), list(in_specs), list(out_specs)
    scratch_shapes = list(scratch_shapes)
    if comm is None:
        res = pl.pallas_call(
            body, out_shape=out_shape, grid=grid, in_specs=in_specs, out_specs=out_specs,
            scratch_shapes=scratch_shapes, compiler_params=_cparams(sem), name=name)(*args)
        return list(res), []
    n_in, n_out, n_sc = len(args), len(out_shape), len(scratch_shapes)
    nci, nco = len(comm.ins), len(comm.out_shapes)
    total = math.prod(grid)

    def wrapped(*refs):
        in_refs, cin = refs[:n_in], refs[n_in:n_in + nci]
        o0 = n_in + nci
        out_refs, cout = refs[o0:o0 + n_out], refs[o0 + n_out:o0 + n_out + nco]
        s0 = o0 + n_out + nco
        sc, csem = refs[s0:s0 + n_sc], refs[s0 + n_sc:]
        lin = 0
        for dim, size in enumerate(grid):
            lin = lin * size + pl.program_id(dim)

        @pl.when(lin == 0)
        def _():
            comm.start(cin, cout, csem)

        if comm.mid is not None:
            @pl.when(lin == total // 2)
            def _():
                comm.mid(cin, cout, csem)

        body(*in_refs, *out_refs, *sc)

        @pl.when(lin == total - 1)
        def _():
            comm.finish(cin, cout, csem)

    res = pl.pallas_call(
        wrapped, out_shape=out_shape + comm.out_shapes, grid=grid,
        in_specs=in_specs + [HBM_SPEC] * nci, out_specs=out_specs + [HBM_SPEC] * nco,
        scratch_shapes=scratch_shapes + comm.sems,
        compiler_params=_cparams(("arbitrary",) * len(grid)), name=name + "_" + comm.tag)(*args, *comm.ins)
    return list(res[:n_out]), list(res[n_out:])


def _matmul(name, grid, k_axis, ins, in_specs, pairs, acc_shapes, epilogue, out_shapes, out_specs, sem, comm=None):
    n_in, n_out, n_acc = len(ins), len(out_shapes), len(acc_shapes)

    def body(*refs):
        in_refs = refs[:n_in]
        out_refs = refs[n_in:n_in + n_out]
        acc_refs = refs[n_in + n_out:]
        if k_axis is None:
            accs = [None] * n_acc
            for ia, ib, mode, iacc in pairs:
                d = _dot(in_refs[ia][...], in_refs[ib][...], mode)
                accs[iacc] = d if accs[iacc] is None else accs[iacc] + d
            epilogue(accs, in_refs, out_refs)
            return
        k = pl.program_id(k_axis)

        @pl.when(k == 0)
        def _():
            for r in acc_refs:
                r[...] = jnp.zeros_like(r)

        for ia, ib, mode, iacc in pairs:
            acc_refs[iacc][...] += _dot(in_refs[ia][...], in_refs[ib][...], mode)

        @pl.when(k == pl.num_programs(k_axis) - 1)
        def _():
            epilogue([r[...] for r in acc_refs], in_refs, out_refs)

    scratch = [] if k_axis is None else [pltpu.VMEM(s, F32) for s in acc_shapes]
    outs, couts = _hosted_call(
        body, comm, out_shape=out_shapes, grid=grid, in_specs=in_specs, out_specs=out_specs,
        scratch_shapes=scratch, sem=sem, name=name, args=ins)
    return outs if comm is None else (outs, couts)


def _rms_fwd(h, g, name):
    lp, d = h.shape
    tm = _row_tile(lp)

    def body(h_ref, g_ref, n_ref):
        x = h_ref[...]
        r = lax.rsqrt(jnp.mean(x * x, axis=-1, keepdims=True) + EPS)
        n_ref[...] = (x * r * g_ref[...]).astype(BF16)

    return pl.pallas_call(
        body, out_shape=jax.ShapeDtypeStruct((lp, d), BF16), grid=(lp // tm,),
        in_specs=[pl.BlockSpec((tm, d), lambda i: (i, 0)), pl.BlockSpec((1, d), lambda i: (0, 0))],
        out_specs=pl.BlockSpec((tm, d), lambda i: (i, 0)),
        compiler_params=_cparams(("parallel",)), name=name)(h, g.reshape(1, d))


def _rms_bwd_math(dn, x, g):
    r = lax.rsqrt(jnp.mean(x * x, axis=-1, keepdims=True) + EPS)
    xh = x * r
    dxh = dn * g
    dx = r * (dxh - xh * jnp.mean(dxh * xh, axis=-1, keepdims=True))
    return dx, dn * xh


def _scale_cast(x, scale, name):
    lp, d = x.shape
    tm = _row_tile(lp)

    def body(x_ref, o_ref):
        o_ref[...] = (x_ref[...] * scale).astype(BF16)

    return pl.pallas_call(
        body, out_shape=jax.ShapeDtypeStruct((lp, d), BF16), grid=(lp // tm,),
        in_specs=[pl.BlockSpec((tm, d), lambda i: (i, 0))], out_specs=pl.BlockSpec((tm, d), lambda i: (i, 0)),
        compiler_params=_cparams(("parallel",)), name=name)(x)


def _ffn_fwd(h, gain, wgu, wd, comm=None):
    lp, d = h.shape
    f4 = wgu.shape[3]
    tm = _row_tile(lp)
    ni = lp // tm
    n = _rms_fwd(h, gain, "rms_fwd_ffn")

    def up_epi(accs, in_refs, out_refs):
        a, b = accs
        out_refs[0][...] = a.astype(BF16)
        out_refs[1][...] = b.astype(BF16)
        out_refs[2][...] = (a * _sigmoid(a) * b).astype(BF16)

    slab = jax.ShapeDtypeStruct((N_CHIPS, lp, f4), BF16)
    res = _matmul(
        "ffn_up", (N_CHIPS, ni), None, [n, wgu, wgu],
        [pl.BlockSpec((tm, d), lambda j, i: (i, 0)),
         pl.BlockSpec((None, None, d, f4), lambda j, i: (j, 0, 0, 0)),
         pl.BlockSpec((None, None, d, f4), lambda j, i: (j, 1, 0, 0))],
        [(0, 1, "nn", 0), (0, 2, "nn", 1)], [(tm, f4)] * 2, up_epi,
        [slab, slab, slab], [pl.BlockSpec((None, tm, f4), lambda j, i: (j, i, 0))] * 3,
        ("parallel", "parallel"), comm)
    (a, b, s), couts = (res, []) if comm is None else res

    def down_epi(accs, in_refs, out_refs):
        out_refs[0][...] = in_refs[2][...] + 0.5 * accs[0]

    (h_new,) = _matmul(
        "ffn_down", (ni, N_CHIPS), 1, [s, wd, h],
        [pl.BlockSpec((None, tm, f4), lambda i, j: (j, i, 0)),
         pl.BlockSpec((None, f4, d), lambda i, j: (j, 0, 0)),
         pl.BlockSpec((tm, d), lambda i, j: (i, 0))],
        [(0, 1, "nn", 0)], [(tm, d)], down_epi,
        [jax.ShapeDtypeStruct((lp, d), F32)], [pl.BlockSpec((tm, d), lambda i, j: (i, 0))],
        ("parallel", "arbitrary"))
    return h_new, (a, b, s), couts


def _tn_tiles(lp):
    return _row_tile(lp, 1408)


def _ffn_bwd(dh, h_in, gain, wgu, wd, saved, comm=None):
    a, b, s = saved
    lp, d = h_in.shape
    f4 = wgu.shape[3]
    tm = _row_tile(lp)
    ni = lp // tm
    tk = _tn_tiles(lp)
    nk = lp // tk
    n = _rms_fwd(h_in, gain, "rms_fwd_ffn")
    dhs = _scale_cast(dh, 0.5, "ffn_dh_half")

    def ds_epi(accs, in_refs, out_refs):
        ds = accs[0]
        av = in_refs[2][...].astype(F32)
        bv = in_refs[3][...].astype(F32)
        sg = _sigmoid(av)
        out_refs[0][...] = (ds * bv * sg * (1.0 + av * (1.0 - sg))).astype(BF16)
        out_refs[1][...] = (ds * av * sg).astype(BF16)

    slab = jax.ShapeDtypeStruct((N_CHIPS, lp, f4), BF16)
    slab_spec = pl.BlockSpec((None, tm, f4), lambda j, i: (j, i, 0))
    res = _matmul(
        "ffn_bwd_ds", (N_CHIPS, ni), None, [dhs, wd, a, b],
        [pl.BlockSpec((tm, d), lambda j, i: (i, 0)), pl.BlockSpec((None, f4, d), lambda j, i: (j, 0, 0)),
         slab_spec, slab_spec],
        [(0, 1, "nt", 0)], [(tm, f4)], ds_epi, [slab, slab], [slab_spec, slab_spec], ("parallel", "parallel"),
        comm)
    (da, db), couts = (res, []) if comm is None else res

    def copy_epi(accs, in_refs, out_refs):
        for acc, o in zip(accs, out_refs):
            o[...] = acc

    (dwd,) = _matmul(
        "ffn_dwd", (N_CHIPS, nk), 1, [s, dhs],
        [pl.BlockSpec((None, tk, f4), lambda j, k: (j, k, 0)), pl.BlockSpec((tk, d), lambda j, k: (k, 0))],
        [(0, 1, "tn", 0)], [(f4, d)], copy_epi,
        [jax.ShapeDtypeStruct((N_CHIPS, f4, d), F32)], [pl.BlockSpec((None, f4, d), lambda j, k: (j, 0, 0))],
        ("parallel", "arbitrary"))

    dw_shape = jax.ShapeDtypeStruct((N_CHIPS, d, f4), F32)
    dw_spec = pl.BlockSpec((None, d, f4), lambda j, k: (j, 0, 0))
    in_slab = pl.BlockSpec((None, tk, f4), lambda j, k: (j, k, 0))
    dwg, dwu = _matmul(
        "ffn_dwgu", (N_CHIPS, nk), 1, [n, da, db],
        [pl.BlockSpec((tk, d), lambda j, k: (k, 0)), in_slab, in_slab],
        [(0, 1, "tn", 0), (0, 2, "tn", 1)], [(d, f4)] * 2, copy_epi,
        [dw_shape, dw_shape], [dw_spec, dw_spec], ("parallel", "arbitrary"))

    def dn_epi(accs, in_refs, out_refs):
        i, j = pl.program_id(0), pl.program_id(1)
        dx, dgrow = _rms_bwd_math(accs[0], in_refs[5][...], in_refs[6][...])
        out_refs[0][...] = in_refs[4][...] + dx

        @pl.when(i == 0)
        def _():
            out_refs[1][...] = jnp.zeros_like(out_refs[1])

        out_refs[1][...] += jnp.sum(dgrow, axis=0, keepdims=True)

    row_spec = pl.BlockSpec((tm, d), lambda i, j: (i, 0))
    in_slab2 = pl.BlockSpec((None, tm, f4), lambda i, j: (j, i, 0))
    wg_spec = pl.BlockSpec((None, None, d, f4), lambda i, j: (j, 0, 0, 0))
    wu_spec = pl.BlockSpec((None, None, d, f4), lambda i, j: (j, 1, 0, 0))
    dh_in, dgain = _matmul(
        "ffn_bwd_dn", (ni, N_CHIPS), 1, [da, wgu, db, wgu, dh, h_in, gain.reshape(1, d)],
        [in_slab2, wg_spec, in_slab2, wu_spec, row_spec, row_spec, pl.BlockSpec((1, d), lambda i, j: (0, 0))],
        [(0, 1, "nt", 0), (2, 3, "nt", 0)], [(tm, d)], dn_epi,
        [jax.ShapeDtypeStruct((lp, d), F32), jax.ShapeDtypeStruct((1, d), F32)],
        [row_spec, pl.BlockSpec((1, d), lambda i, j: (0, 0))], ("arbitrary", "arbitrary"))
    return dh_in, dgain, dwg, dwu, dwd, couts


def _rope_tables(lp):
    pos = jnp.arange(lp, dtype=F32) - float(PAD_FRONT)
    inv_freq = ROPE_THETA ** (-jnp.arange(0, ROT_DIM, 2, dtype=F32) / ROT_DIM)
    ang = pos[:, None] * inv_freq[None, :]
    cos, sin = jnp.cos(ang), jnp.sin(ang)
    half = ROT_DIM // 2
    ones = jnp.ones((lp, HEAD_DIM - ROT_DIM), F32)
    zeros_h = jnp.zeros((lp, half), F32)
    zeros_r = jnp.zeros((lp, HEAD_DIM - ROT_DIM), F32)
    c = jnp.concatenate([cos, cos, ones], axis=1)
    s1 = jnp.concatenate([-sin, zeros_h, zeros_r], axis=1)
    s2 = jnp.concatenate([zeros_h, sin, zeros_r], axis=1)
    reps = LANES // HEAD_DIM
    return jnp.stack([jnp.tile(c, (1, reps)), jnp.tile(s1, (1, reps)), jnp.tile(s2, (1, reps))])


def _rope(x, c, s1, s2):
    half = ROT_DIM // 2
    outs = []
    for ch in range(x.shape[1] // LANES):
        xc = x[:, ch * LANES:(ch + 1) * LANES]
        outs.append(xc * c + pltpu.roll(xc, LANES - half, 1) * s1 + pltpu.roll(xc, half, 1) * s2)
    return outs[0] if len(outs) == 1 else jnp.concatenate(outs, axis=1)


def _rope_t(dy, c, s1, s2):
    half = ROT_DIM // 2
    outs = []
    for ch in range(dy.shape[1] // LANES):
        dc = dy[:, ch * LANES:(ch + 1) * LANES]
        outs.append(dc * c + pltpu.roll(dc * s1, half, 1) + pltpu.roll(dc * s2, LANES - half, 1))
    return outs[0] if len(outs) == 1 else jnp.concatenate(outs, axis=1)


def _in_proj(n, w_in, tabs, ssm_w):
    lp, d = n.shape
    inw = w_in.shape[1]
    tm = _row_tile(lp)
    o1 = ATTN_WIDTH
    o2 = o1 + KV_WIDTH
    o3 = o2 + KV_WIDTH
    o4 = o3 + ssm_w
    o5 = o4 + d

    def epi(accs, in_refs, out_refs):
        z = accs[0]
        c, s1, s2 = in_refs[2][0], in_refs[2][1], in_refs[2][2]
        out_refs[0][...] = _rope(z[:, :o1], c, s1, s2).astype(BF16)
        out_refs[1][...] = _rope(z[:, o1:o2], c, s1, s2).astype(BF16)
        out_refs[2][...] = z[:, o2:o3].astype(BF16)
        out_refs[3][...] = z[:, o3:o4]
        out_refs[4][...] = z[:, o4:o5]
        out_refs[5][...] = z[:, o5:]

    def rs(w, dt):
        return jax.ShapeDtypeStruct((lp, w), dt), pl.BlockSpec((tm, w), lambda i: (i, 0))

    shapes, specs = zip(rs(o1, BF16), rs(KV_WIDTH, BF16), rs(KV_WIDTH, BF16), rs(ssm_w, F32), rs(d, F32), rs(d, F32))
    return _matmul(
        "mix_in_proj", (lp // tm,), None, [n, w_in, tabs],
        [pl.BlockSpec((tm, d), lambda i: (i, 0)), pl.BlockSpec((d, inw), lambda i: (0, 0)),
         pl.BlockSpec((3, tm, LANES), lambda i: (0, i, 0))],
        [(0, 1, "nn", 0)], [(tm, inw)], epi, list(shapes), list(specs), ("parallel",))


def _attn_mask(b):
    rows = lax.broadcasted_iota(jnp.int32, (BLOCK, 3 * BLOCK), 0)
    cols = lax.broadcasted_iota(jnp.int32, (BLOCK, 3 * BLOCK), 1)
    qpos = b * BLOCK + rows - PAD_FRONT
    kpos = (b - 1) * BLOCK + cols - PAD_FRONT
    dist = qpos - kpos
    band = (cols < 2 * BLOCK) & (kpos >= N_META) & (dist >= 0) & (dist < BLOCK)
    mrow = cols - 2 * BLOCK
    meta = (mrow >= PAD_FRONT) & ((mrow - PAD_FRONT) <= qpos)
    return band | meta


def _attn_probs(qh, kk, mask, sink):
    s = _dot(qh, kk, "nt") * (HEAD_DIM ** -0.5)
    s = jnp.where(mask, s, NEG_INF)
    m = jnp.maximum(jnp.max(s, axis=-1, keepdims=True), sink)
    e = jnp.exp(s - m)
    es = jnp.exp(sink - m)
    z = jnp.sum(e, axis=-1, keepdims=True) + es
    inv = 1.0 / z
    return e * inv, es * inv


def _head(ref_or_val, h):
    return ref_or_val[:, h * HEAD_DIM:(h + 1) * HEAD_DIM]


def _attn_fwd(q, k, v, sinks, comm=None):
    lp = q.shape[0]
    nb = lp // BLOCK

    def body(sink_ref, q_ref, kp_ref, kc_ref, km_ref, vp_ref, vc_ref, vm_ref, o_ref):
        b = pl.program_id(0)
        mask = _attn_mask(b)
        for hk in range(N_KV_HEADS):
            kk = jnp.concatenate([_head(kp_ref, hk), _head(kc_ref, hk), _head(km_ref, hk)], axis=0)
            vv = jnp.concatenate([_head(vp_ref, hk), _head(vc_ref, hk), _head(vm_ref, hk)], axis=0)
            for g in range(Q_PER_KV):
                h = hk * Q_PER_KV + g
                p, _ = _attn_probs(_head(q_ref, h), kk, mask, sink_ref[h])
                o_ref[:, h * HEAD_DIM:(h + 1) * HEAD_DIM] = _dot(p, vv, "nn").astype(BF16)

    cur = lambda b: (b, 0)
    prev = lambda b: (jnp.maximum(b - 1, 0), 0)
    first = lambda b: (0, 0)
    kvs = lambda f: pl.BlockSpec((BLOCK, KV_WIDTH), f)
    (o,), couts = _hosted_call(
        body, comm, out_shape=[jax.ShapeDtypeStruct((lp, ATTN_WIDTH), BF16)], grid=(nb,),
        in_specs=[pl.BlockSpec(memory_space=pltpu.SMEM), pl.BlockSpec((BLOCK, ATTN_WIDTH), cur),
                  kvs(prev), kvs(cur), kvs(first), kvs(prev), kvs(cur), kvs(first)],
        out_specs=[pl.BlockSpec((BLOCK, ATTN_WIDTH), cur)], scratch_shapes=[],
        sem=("parallel",), name="attn_fwd", args=(sinks, q, k, k, k, v, v, v))
    return o, couts


def _attn_bwd(q, k, v, do, sinks, tabs, comm=None):
    lp = q.shape[0]
    nb = lp // BLOCK
    scale = HEAD_DIM ** -0.5

    def body(sink_ref, q_ref, do_ref, kp_ref, kc_ref, km_ref, vp_ref, vc_ref, vm_ref, tq_ref, tk_ref, t0_ref,
             dq_ref, dk_ref, dv_ref, dkm_ref, dvm_ref, dsink_ref,
             dq_s, dkk_s, dvv_s, ck_s, cv_s, mk_s, mv_s):
        b = pl.program_id(0)

        @pl.when(b == 0)
        def _():
            for r in (ck_s, cv_s, mk_s, mv_s, dsink_ref):
                r[...] = jnp.zeros_like(r)

        @pl.when(b < nb)
        def _():
            mask = _attn_mask(b)
            for hk in range(N_KV_HEADS):
                kk = jnp.concatenate([_head(kp_ref, hk), _head(kc_ref, hk), _head(km_ref, hk)], axis=0)
                vv = jnp.concatenate([_head(vp_ref, hk), _head(vc_ref, hk), _head(vm_ref, hk)], axis=0)
                dkk = jnp.zeros((3 * BLOCK, HEAD_DIM), F32)
                dvv = jnp.zeros((3 * BLOCK, HEAD_DIM), F32)
                for g in range(Q_PER_KV):
                    h = hk * Q_PER_KV + g
                    qh = _head(q_ref, h)
                    doh = _head(do_ref, h)
                    p, ps = _attn_probs(qh, kk, mask, sink_ref[h])
                    dp = _dot(doh, vv, "nt")
                    delta = jnp.sum(p * dp, axis=-1, keepdims=True)
                    ds = (p * (dp - delta)).astype(BF16)
                    dsink_ref[h:h + 1, :] += jnp.zeros((1, LANES), F32) - jnp.sum(ps * delta)
                    dq_s[:, h * HEAD_DIM:(h + 1) * HEAD_DIM] = _dot(ds, kk, "nn") * scale
                    dkk = dkk + _dot(ds, qh, "tn") * scale
                    dvv = dvv + _dot(p, doh, "tn")
                dkk_s[:, hk * HEAD_DIM:(hk + 1) * HEAD_DIM] = dkk
                dvv_s[:, hk * HEAD_DIM:(hk + 1) * HEAD_DIM] = dvv
            dq_ref[...] = _rope_t(dq_s[...], tq_ref[0], tq_ref[1], tq_ref[2])
            dk_ref[...] = _rope_t(ck_s[...] + dkk_s[0:BLOCK, :], tk_ref[0], tk_ref[1], tk_ref[2])
            dv_ref[...] = cv_s[...] + dvv_s[0:BLOCK, :]
            ck_s[...] = dkk_s[BLOCK:2 * BLOCK, :]
            cv_s[...] = dvv_s[BLOCK:2 * BLOCK, :]
            mk_s[...] += dkk_s[2 * BLOCK:, :]
            mv_s[...] += dvv_s[2 * BLOCK:, :]

        @pl.when(b == nb)
        def _():
            dk_ref[...] = _rope_t(ck_s[...], tk_ref[0], tk_ref[1], tk_ref[2])
            dv_ref[...] = cv_s[...]
            dkm_ref[...] = _rope_t(mk_s[...], t0_ref[0], t0_ref[1], t0_ref[2])
            dvm_ref[...] = mv_s[...]

    cur = lambda b: (jnp.minimum(b, nb - 1), 0)
    prev = lambda b: (jnp.clip(b - 1, 0, nb - 1), 0)
    first = lambda b: (0, 0)
    kvs = lambda f: pl.BlockSpec((BLOCK, KV_WIDTH), f)
    tab = lambda f: pl.BlockSpec((3, BLOCK, LANES), lambda b: (0,) + f(b)[:1] + (0,))
    kv_out = lambda b: (jnp.maximum(b - 1, 0), 0)
    return _hosted_call(
        body, comm,
        out_shape=[jax.ShapeDtypeStruct((lp, ATTN_WIDTH), F32), jax.ShapeDtypeStruct((lp, KV_WIDTH), F32),
                   jax.ShapeDtypeStruct((lp, KV_WIDTH), F32), jax.ShapeDtypeStruct((BLOCK, KV_WIDTH), F32),
                   jax.ShapeDtypeStruct((BLOCK, KV_WIDTH), F32), jax.ShapeDtypeStruct((N_Q_HEADS, LANES), F32)],
        grid=(nb + 1,),
        in_specs=[pl.BlockSpec(memory_space=pltpu.SMEM), pl.BlockSpec((BLOCK, ATTN_WIDTH), cur),
                  pl.BlockSpec((BLOCK, ATTN_WIDTH), cur),
                  kvs(prev), kvs(cur), kvs(first), kvs(prev), kvs(cur), kvs(first),
                  tab(cur), tab(kv_out), tab(first)],
        out_specs=[pl.BlockSpec((BLOCK, ATTN_WIDTH), cur), kvs(kv_out), kvs(kv_out), kvs(first), kvs(first),
                   pl.BlockSpec((N_Q_HEADS, LANES), first)],
        scratch_shapes=[pltpu.VMEM((BLOCK, ATTN_WIDTH), F32), pltpu.VMEM((3 * BLOCK, KV_WIDTH), F32),
                        pltpu.VMEM((3 * BLOCK, KV_WIDTH), F32), pltpu.VMEM((BLOCK, KV_WIDTH), F32),
                        pltpu.VMEM((BLOCK, KV_WIDTH), F32), pltpu.VMEM((BLOCK, KV_WIDTH), F32),
                        pltpu.VMEM((BLOCK, KV_WIDTH), F32)],
        sem=("arbitrary",), name="attn_bwd", args=(sinks, q, do, k, k, k, v, v, v, tabs, tabs, tabs))


def _cmul(ar, ai, br, bi):
    return ar * br - ai * bi, ar * bi + ai * br


def _cpow(lr, li, n):
    rr = ri = None
    br, bi = lr, li
    while n:
        if n & 1:
            rr, ri = (br, bi) if rr is None else _cmul(rr, ri, br, bi)
        n >>= 1
        if n:
            br, bi = _cmul(br, bi, br, bi)
    return rr, ri


def _shift_rows(x, d, reverse):
    rows = lax.broadcasted_iota(jnp.int32, x.shape, 0)
    if not reverse:
        return jnp.where(rows >= d, pltpu.roll(x, d, 0), 0.0)
    return jnp.where(rows < SUBLANES - d, pltpu.roll(x, SUBLANES - d, 0), 0.0)


def _sublane_powers(mr, mi, reverse):
    rows = lax.broadcasted_iota(jnp.int32, mr.shape, 0)
    e = SUBLANES - 1 - rows if reverse else rows
    pr, pi = jnp.ones_like(mr), jnp.zeros_like(mr)
    br, bi = mr, mi
    for d in (1, 2, 4):
        tr, ti = _cmul(pr, pi, br, bi)
        on = (e & d) != 0
        pr, pi = jnp.where(on, tr, pr), jnp.where(on, ti, pi)
        if d < 4:
            br, bi = _cmul(br, bi, br, bi)
    return pr, pi


def _inclusive_prefix(er, ei, mr, mi, reverse):
    ir, ii, pr, pi = er, ei, mr, mi
    for d in (1, 2, 4):
        tr, ti = _cmul(pr, pi, _shift_rows(ir, d, reverse), _shift_rows(ii, d, reverse))
        ir, ii = ir + tr, ii + ti
        if d < 4:
            pr, pi = _cmul(pr, pi, pr, pi)
    return ir, ii


def _chain_rows(a, t, seg):
    return pl.ds(a * SUBLANES * seg + t, SUBLANES, stride=seg)


def _seg_scan(xr_ref, xi_ref, lam, seg, nchain, reverse, store, init, extra=None):
    nt = len(lam)
    acc0 = () if extra is None else extra[1]

    def step(i, carry):
        hs, acc = carry
        t = seg - 1 - i if reverse else i
        out = []
        for a in range(nchain):
            sl = _chain_rows(a, t, seg)
            for j in range(nt):
                lr, li = lam[j]
                k = 2 * (a * nt + j)
                hr, hi = hs[k], hs[k + 1]
                nr = lr * hr - li * hi + xr_ref[j, sl, :]
                ni = lr * hi + li * hr + xi_ref[j, sl, :]
                if store:
                    xr_ref[j, sl, :] = nr
                    xi_ref[j, sl, :] = ni
                if extra is not None:
                    acc = extra[0](t, a, j, nr, ni, acc)
                out += [nr, ni]
        return tuple(out), acc

    return lax.fori_loop(0, seg, step, (tuple(init), acc0))


def _ssm_scan(xr_ref, xi_ref, lam, seg, nchain, reverse, extra=None):
    nt = len(lam)
    zero = [jnp.zeros((SUBLANES, LANES), F32)] * (2 * nt * nchain)
    ends, _ = _seg_scan(xr_ref, xi_ref, lam, seg, nchain, reverse, False, zero)
    init = [None] * (2 * nt * nchain)
    last = 0 if reverse else SUBLANES - 1
    for j in range(nt):
        mr, mi = _cpow(lam[j][0], lam[j][1], seg)
        m8r, m8i = _cpow(mr, mi, SUBLANES)
        pwr, pwi = _sublane_powers(mr, mi, reverse)
        gr = gi = jnp.zeros((SUBLANES, LANES), F32)
        for a in (reversed(range(nchain)) if reverse else range(nchain)):
            k = 2 * (a * nt + j)
            incr, inci = _inclusive_prefix(ends[k], ends[k + 1], mr, mi, reverse)
            tr, ti = _cmul(pwr, pwi, gr, gi)
            init[k] = _shift_rows(incr, 1, reverse) + tr
            init[k + 1] = _shift_rows(inci, 1, reverse) + ti
            g2r, g2i = _cmul(m8r, m8i, gr, gi)
            gr = g2r + jnp.broadcast_to(incr[last:last + 1, :], gr.shape)
            gi = g2i + jnp.broadcast_to(inci[last:last + 1, :], gi.shape)
    _, acc = _seg_scan(xr_ref, xi_ref, lam, seg, nchain, reverse, True, init, extra)
    return acc


def _ssm_tables(bb_re, bb_im, c_re, c_im):
    g = bb_re.shape[0]
    nt = g // GROUPS_PER_TILE
    eye = jnp.eye(g, dtype=F32)

    def b_tab(bb):
        full = jnp.einsum('gpc,gh->gchp', bb, eye).reshape(g * SSM_GROUP, g * SSM_STATE)
        full = full.reshape(g * SSM_GROUP // LANES, LANES, nt, TILE_STATES)
        return jnp.stack([full[t // 2, :, t, :] for t in range(nt)])

    def c_tab(c):
        full = jnp.einsum('gcp,gh->gphc', c, eye).reshape(g * SSM_STATE, g * SSM_GROUP)
        full = full.reshape(nt, TILE_STATES, g * SSM_GROUP // LANES, LANES)
        return jnp.stack([full[t, :, t // 2, :] for t in range(nt)])

    return b_tab(bb_re), b_tab(bb_im), c_tab(c_re), c_tab(c_im)


def _ssm_untable_b(db, g):
    nt = g // GROUPS_PER_TILE
    per_blk = LANES // SSM_GROUP
    db = db.reshape(nt, GROUPS_PER_TILE, SSM_STATE, per_blk, SSM_GROUP)
    out = []
    for t in range(nt):
        for gl in range(GROUPS_PER_TILE):
            out.append(db[t, gl, :, GROUPS_PER_TILE * (t % 2) + gl, :])
    return jnp.stack(out)


def _ssm_untable_c(dc, g):
    nt = g // GROUPS_PER_TILE
    per_blk = LANES // SSM_GROUP
    dc = dc.reshape(nt, per_blk, SSM_GROUP, GROUPS_PER_TILE, SSM_STATE)
    out = []
    for t in range(nt):
        for gl in range(GROUPS_PER_TILE):
            out.append(dc[t, GROUPS_PER_TILE * (t % 2) + gl, :, gl, :])
    return jnp.stack(out)


def _lam_tiles(lam_ref):
    out = []
    for j in range(TILE_STATES // LANES):
        out.append(jnp.broadcast_to(lam_ref[:, j * LANES:(j + 1) * LANES], (SUBLANES, LANES)))
    return out


def _scan_chains(lp):
    for n in (4, 2, 1):
        if lp % (SUBLANES * n) == 0 and (lp // SUBLANES) % 16 == 0:
            return n
    raise ValueError(lp)


def _split_tiles(dst_ref, rows, val):
    for j in range(val.shape[1] // LANES):
        dst_ref[j, rows, :] = val[:, j * LANES:(j + 1) * LANES]


def _cat_tiles(src_ref, rows):
    njt = src_ref.shape[0]
    return jnp.concatenate([src_ref[j, rows, :] for j in range(njt)], axis=1).astype(BF16)


def _ssm_fwd(u, lam_re, lam_im, tb_re, tb_im, tc_re, tc_im, d_skip, comm=None):
    lp, w = u.shape
    nt = tb_re.shape[0]
    nchain = _scan_chains(lp)
    seg = lp // (SUBLANES * nchain)
    chunk = lp // SUBLANES
    njt = TILE_STATES // LANES

    def body(u_ref, lr_ref, li_ref, br_ref, bi_ref, cr_ref, ci_ref, d_ref, y_ref, xr, xi):
        t = pl.program_id(0)
        for s in range(SUBLANES):
            rs = pl.ds(s * chunk, chunk)
            ub = u_ref[rs, :].astype(BF16)
            _split_tiles(xr, rs, _dot(ub, br_ref[...], "nn"))
            _split_tiles(xi, rs, _dot(ub, bi_ref[...], "nn"))
        lrs, lis = _lam_tiles(lr_ref), _lam_tiles(li_ref)
        _ssm_scan(xr, xi, list(zip(lrs, lis)), seg, nchain, False)
        for s in range(SUBLANES):
            rs = pl.ds(s * chunk, chunk)
            y = _dot(_cat_tiles(xr, rs), cr_ref[...], "nn") - _dot(_cat_tiles(xi, rs), ci_ref[...], "nn")

            @pl.when(t % 2 == 0)
            def _():
                y_ref[rs, :] = y + d_ref[...] * u_ref[rs, :]

            @pl.when(t % 2 == 1)
            def _():
                y_ref[rs, :] += y

    blk = pl.BlockSpec((lp, LANES), lambda t: (0, t // 2))
    lam_spec = pl.BlockSpec((None, 1, TILE_STATES), lambda t: (t, 0, 0))
    b_spec = pl.BlockSpec((None, LANES, TILE_STATES), lambda t: (t, 0, 0))
    c_spec = pl.BlockSpec((None, TILE_STATES, LANES), lambda t: (t, 0, 0))
    (y,), couts = _hosted_call(
        body, comm, out_shape=[jax.ShapeDtypeStruct((lp, w), F32)], grid=(nt,),
        in_specs=[blk, lam_spec, lam_spec, b_spec, b_spec, c_spec, c_spec,
                  pl.BlockSpec((1, LANES), lambda t: (0, t // 2))],
        out_specs=[blk],
        scratch_shapes=[pltpu.VMEM((njt, lp, LANES), F32), pltpu.VMEM((njt, lp, LANES), F32)],
        sem=("arbitrary",), name="ssm_fwd",
        args=(u, lam_re, lam_im, tb_re, tb_im, tc_re, tc_im, d_skip.reshape(1, w)))
    return y, couts


def _ssm_bwd(u, dy, lam_re, lam_im, tb_re, tb_im, tc_re, tc_im, d_skip, comm=None):
    lp, w = u.shape
    nt = tb_re.shape[0]
    nchain = _scan_chains(lp)
    seg = lp // (SUBLANES * nchain)
    chunk = lp // SUBLANES
    njt = TILE_STATES // LANES
    tbt_re, tbt_im = jnp.swapaxes(tb_re, 1, 2), jnp.swapaxes(tb_im, 1, 2)
    tct_re, tct_im = jnp.swapaxes(tc_re, 1, 2), jnp.swapaxes(tc_im, 1, 2)

    def body(u_ref, dy_ref, lr_ref, li_ref, br_ref, bi_ref, btr_ref, bti_ref, ctr_ref, cti_ref, d_ref,
             du_ref, dlr_ref, dli_ref, dbr_ref, dbi_ref, dcr_ref, dci_ref, dd_ref, hr, hi, ar, ai):
        t = pl.program_id(0)
        lrs, lis = _lam_tiles(lr_ref), _lam_tiles(li_ref)
        for s in range(SUBLANES):
            rs = pl.ds(s * chunk, chunk)
            ub = u_ref[rs, :].astype(BF16)
            dyb = dy_ref[rs, :].astype(BF16)
            _split_tiles(hr, rs, _dot(ub, br_ref[...], "nn"))
            _split_tiles(hi, rs, _dot(ub, bi_ref[...], "nn"))
            _split_tiles(ar, rs, _dot(dyb, ctr_ref[...], "nn"))
            _split_tiles(ai, rs, -_dot(dyb, cti_ref[...], "nn"))
        _ssm_scan(hr, hi, list(zip(lrs, lis)), seg, nchain, False)

        def dlam_step(tt, a, j, a_r, a_i, acc):
            sl = _chain_rows(a, jnp.maximum(tt - 1, 0), seg)
            p_r, p_i = hr[j, sl, :], hi[j, sl, :]
            acc = list(acc)
            acc[2 * j] = acc[2 * j] + jnp.where(tt > 0, a_r * p_r + a_i * p_i, 0.0)
            acc[2 * j + 1] = acc[2 * j + 1] + jnp.where(tt > 0, a_i * p_r - a_r * p_i, 0.0)
            return tuple(acc)

        zero = tuple([jnp.zeros((SUBLANES, LANES), F32)] * (2 * njt))
        conj = [(lr, -li) for lr, li in zip(lrs, lis)]
        acc = list(_ssm_scan(ar, ai, conj, seg, nchain, True, (dlam_step, zero)))
        row0 = lax.broadcasted_iota(jnp.int32, (SUBLANES, LANES), 0) == 0
        for j in range(njt):
            cs = slice(j * LANES, (j + 1) * LANES)
            for a in range(nchain):
                p_r = _shift_rows(hr[j, _chain_rows(a, seg - 1, seg), :], 1, False)
                p_i = _shift_rows(hi[j, _chain_rows(a, seg - 1, seg), :], 1, False)
                if a > 0:
                    before = pl.ds(a * SUBLANES * seg - 1, 1)
                    p_r = jnp.where(row0, jnp.broadcast_to(hr[j, before, :], p_r.shape), p_r)
                    p_i = jnp.where(row0, jnp.broadcast_to(hi[j, before, :], p_i.shape), p_i)
                a_r, a_i = ar[j, _chain_rows(a, 0, seg), :], ai[j, _chain_rows(a, 0, seg), :]
                acc[2 * j] = acc[2 * j] + a_r * p_r + a_i * p_i
                acc[2 * j + 1] = acc[2 * j + 1] + a_i * p_r - a_r * p_i
            dlr_ref[:, cs] = jnp.sum(acc[2 * j], axis=0, keepdims=True)
            dli_ref[:, cs] = jnp.sum(acc[2 * j + 1], axis=0, keepdims=True)

        dd = jnp.zeros((1, LANES), F32)
        for s in range(SUBLANES):
            rs = pl.ds(s * chunk, chunk)
            ub = u_ref[rs, :].astype(BF16)
            dyv = dy_ref[rs, :]
            dyb = dyv.astype(BF16)
            arb, aib = _cat_tiles(ar, rs), _cat_tiles(ai, rs)
            hrb, hib = _cat_tiles(hr, rs), _cat_tiles(hi, rs)
            du = _dot(arb, btr_ref[...], "nn") + _dot(aib, bti_ref[...], "nn")
            upd = [(dbr_ref, _dot(arb, ub, "tn")), (dbi_ref, _dot(aib, ub, "tn")),
                   (dcr_ref, _dot(dyb, hrb, "tn")), (dci_ref, -_dot(dyb, hib, "tn"))]
            for ref, val in upd:
                if s == 0:
                    ref[...] = val
                else:
                    ref[...] += val
            rows = lax.broadcasted_iota(jnp.int32, (chunk, LANES), 0) + s * chunk
            keep = rows >= PAD_FRONT
            dd = dd + jnp.sum(dyv * u_ref[rs, :], axis=0, keepdims=True)

            @pl.when(t % 2 == 0)
            def _():
                du_ref[rs, :] = jnp.where(keep, du + d_ref[...] * dyv, 0.0)

            @pl.when(t % 2 == 1)
            def _():
                du_ref[rs, :] += jnp.where(keep, du, 0.0)

        @pl.when(t % 2 == 0)
        def _():
            dd_ref[...] = dd

    blk = pl.BlockSpec((lp, LANES), lambda t: (0, t // 2))
    vec = pl.BlockSpec((1, LANES), lambda t: (0, t // 2))
    lam_spec = pl.BlockSpec((None, 1, TILE_STATES), lambda t: (t, 0, 0))
    b_spec = pl.BlockSpec((None, LANES, TILE_STATES), lambda t: (t, 0, 0))
    c_spec = pl.BlockSpec((None, TILE_STATES, LANES), lambda t: (t, 0, 0))
    lam_shape = jax.ShapeDtypeStruct((nt, 1, TILE_STATES), F32)
    bt_shape = jax.ShapeDtypeStruct((nt, TILE_STATES, LANES), F32)
    ct_shape = jax.ShapeDtypeStruct((nt, LANES, TILE_STATES), F32)
    st = pltpu.VMEM((njt, lp, LANES), F32)
    return _hosted_call(
        body, comm,
        out_shape=[jax.ShapeDtypeStruct((lp, w), F32), lam_shape, lam_shape, bt_shape, bt_shape, ct_shape, ct_shape,
                   jax.ShapeDtypeStruct((1, w), F32)],
        grid=(nt,),
        in_specs=[blk, blk, lam_spec, lam_spec, b_spec, b_spec, c_spec, c_spec, b_spec, b_spec, vec],
        out_specs=[blk, lam_spec, lam_spec, c_spec, c_spec, b_spec, b_spec, vec],
        scratch_shapes=[st, st, st, st], sem=("arbitrary",), name="ssm_bwd",
        args=(u, dy, lam_re, lam_im, tb_re, tb_im, tbt_re, tbt_im, tct_re, tct_im, d_skip.reshape(1, w)))


def _ssm_params(a_re, a_im, log_dt, b_re, b_im):
    dt = jnp.exp(log_dt)[:, None]
    mag = jnp.exp(a_re * dt)
    lb_re = mag * jnp.cos(a_im * dt)
    lb_im = mag * jnp.sin(a_im * dt)
    den = a_re * a_re + a_im * a_im
    num_re = lb_re - 1.0
    coef_re = (num_re * a_re + lb_im * a_im) / den
    coef_im = (lb_im * a_re - num_re * a_im) / den
    bb_re = coef_re[..., None] * b_re - coef_im[..., None] * b_im
    bb_im = coef_re[..., None] * b_im + coef_im[..., None] * b_re
    return lb_re, lb_im, bb_re, bb_im


def _gelu_fwd(y):
    lp, w = y.shape
    tm = _row_tile(lp)

    def body(y_ref, o_ref):
        o_ref[...] = _gelu(y_ref[...]).astype(BF16)

    return pl.pallas_call(
        body, out_shape=jax.ShapeDtypeStruct((lp, w), BF16), grid=(lp // tm,),
        in_specs=[pl.BlockSpec((tm, w), lambda i: (i, 0))], out_specs=pl.BlockSpec((tm, w), lambda i: (i, 0)),
        compiler_params=_cparams(("parallel",)), name="gelu_fwd")(y)


def _merge_fwd(o, yg, ga, gs, w3):
    lp, d = ga.shape
    d4 = w3.shape[3]
    kw = w3.shape[2]
    tm = _row_tile(lp)

    def epi(accs, in_refs, out_refs):
        attn, vv, gg = accs
        out_refs[0][...] = (_sigmoid(in_refs[5][...]) * attn
                            + _sigmoid(in_refs[6][...]) * (vv * _sigmoid(gg))).astype(BF16)

    wspec = lambda which: pl.BlockSpec((None, None, kw, d4), lambda j, i: (j, which, 0, 0))
    colspec = pl.BlockSpec((tm, d4), lambda j, i: (i, j))
    aspec = pl.BlockSpec((tm, kw), lambda j, i: (i, 0))
    (merged,) = _matmul(
        "merge_fwd", (N_CHIPS, lp // tm), None, [o, yg, w3, w3, w3, ga, gs],
        [aspec, aspec, wspec(0), wspec(1), wspec(2), colspec, colspec],
        [(0, 2, "nn", 0), (1, 3, "nn", 1), (1, 4, "nn", 2)], [(tm, d4)] * 3, epi,
        [jax.ShapeDtypeStruct((lp, d), BF16)], [colspec], ("parallel", "parallel"))
    return merged


def _out_proj(merged, w_out, h):
    lp, d = h.shape
    d4 = w_out.shape[1]
    tm = _row_tile(lp)

    def epi(accs, in_refs, out_refs):
        out_refs[0][...] = in_refs[2][...] + accs[0]

    (h_new,) = _matmul(
        "mix_out_proj", (lp // tm, N_CHIPS), 1, [merged, w_out, h],
        [pl.BlockSpec((tm, d4), lambda i, j: (i, j)), pl.BlockSpec((None, d4, d), lambda i, j: (j, 0, 0)),
         pl.BlockSpec((tm, d), lambda i, j: (i, 0))],
        [(0, 1, "nn", 0)], [(tm, d)], epi,
        [jax.ShapeDtypeStruct((lp, d), F32)], [pl.BlockSpec((tm, d), lambda i, j: (i, 0))],
        ("parallel", "arbitrary"))
    return h_new


def _merge_bwd(dhb, w_out, o, yg, ga, gs, w3):
    lp, d = ga.shape
    d4 = w3.shape[3]
    kw = w3.shape[2]
    tm = _row_tile(lp)

    def epi(accs, in_refs, out_refs):
        dm, attn, vv, gg = accs
        sa = _sigmoid(in_refs[7][...])
        ss = _sigmoid(in_refs[8][...])
        sg = _sigmoid(gg)
        ssm = vv * sg
        dssm = dm * ss
        out_refs[0][...] = (dm * sa).astype(BF16)
        out_refs[1][...] = (dssm * sg).astype(BF16)
        out_refs[2][...] = (dssm * vv * sg * (1.0 - sg)).astype(BF16)
        out_refs[3][...] = (dm * attn * sa * (1.0 - sa)).astype(BF16)
        out_refs[4][...] = (dm * ssm * ss * (1.0 - ss)).astype(BF16)

    wspec = lambda which: pl.BlockSpec((None, None, kw, d4), lambda j, i: (j, which, 0, 0))
    colspec = pl.BlockSpec((tm, d4), lambda j, i: (i, j))
    aspec = pl.BlockSpec((tm, kw), lambda j, i: (i, 0))
    shp = jax.ShapeDtypeStruct((lp, d), BF16)
    return _matmul(
        "merge_bwd", (N_CHIPS, lp // tm), None, [dhb, w_out, o, yg, w3, w3, w3, ga, gs],
        [pl.BlockSpec((tm, d), lambda j, i: (i, 0)), pl.BlockSpec((None, d4, d), lambda j, i: (j, 0, 0)),
         aspec, aspec, wspec(0), wspec(1), wspec(2), colspec, colspec],
        [(0, 1, "nt", 0), (2, 4, "nn", 1), (3, 5, "nn", 2), (3, 6, "nn", 3)], [(tm, d4)] * 4, epi,
        [shp] * 5, [colspec] * 5, ("parallel", "parallel"))


def _branch_bwd(dattn, dv, dg, w3, y):
    lp, d = dattn.shape
    d4 = w3.shape[3]
    kw = w3.shape[2]
    tm = _row_tile(lp)

    def epi(accs, in_refs, out_refs):
        out_refs[0][...] = accs[0].astype(BF16)
        out_refs[1][...] = accs[1] * _gelu_grad(in_refs[6][...])

    wspec = lambda which: pl.BlockSpec((None, None, kw, d4), lambda i, j: (j, which, 0, 0))
    colspec = pl.BlockSpec((tm, d4), lambda i, j: (i, j))
    rowspec = pl.BlockSpec((tm, kw), lambda i, j: (i, 0))
    return _matmul(
        "branch_bwd", (lp // tm, N_CHIPS), 1, [dattn, dv, dg, w3, w3, w3, y],
        [colspec, colspec, colspec, wspec(0), wspec(1), wspec(2), rowspec],
        [(0, 3, "nt", 0), (1, 4, "nt", 1), (2, 5, "nt", 1)], [(tm, kw)] * 2, epi,
        [jax.ShapeDtypeStruct((lp, kw), BF16), jax.ShapeDtypeStruct((lp, kw), F32)], [rowspec, rowspec],
        ("parallel", "arbitrary"))


def _tn_cols(x, ys, name):
    lp, kx = x.shape
    n = ys[0].shape[1]
    n4 = n // N_CHIPS
    tk = _tn_tiles(lp)

    def epi(accs, in_refs, out_refs):
        for acc, o in zip(accs, out_refs):
            o[...] = acc

    shp = jax.ShapeDtypeStruct((N_CHIPS, kx, n4), F32)
    return _matmul(
        name, (N_CHIPS, lp // tk), 1, [x] + list(ys),
        [pl.BlockSpec((tk, kx), lambda j, k: (k, 0))] + [pl.BlockSpec((tk, n4), lambda j, k: (k, j))] * len(ys),
        [(0, 1 + i, "tn", i) for i in range(len(ys))], [(kx, n4)] * len(ys), epi,
        [shp] * len(ys), [pl.BlockSpec((None, kx, n4), lambda j, k: (j, 0, 0))] * len(ys),
        ("parallel", "arbitrary"))


def _tn_full(x, y, name, tn_cols=None):
    lp, kx = x.shape
    n = y.shape[1]
    tk = _tn_tiles(lp)
    tn = n if tn_cols is None else tn_cols

    def epi(accs, in_refs, out_refs):
        out_refs[0][...] = accs[0]

    (out,) = _matmul(
        name, (n // tn, lp // tk), 1, [x, y],
        [pl.BlockSpec((tk, kx), lambda j, k: (k, 0)), pl.BlockSpec((tk, tn), lambda j, k: (k, j))],
        [(0, 1, "tn", 0)], [(kx, tn)], epi,
        [jax.ShapeDtypeStruct((kx, n), F32)], [pl.BlockSpec((kx, tn), lambda j, k: (0, j))],
        ("parallel", "arbitrary"))
    return out


def _in_proj_bwd(dz, w_in, dh, h_in, gain):
    lp, d = h_in.shape
    inw = w_in.shape[1]
    tm = _row_tile(lp)

    def epi(accs, in_refs, out_refs):
        i = pl.program_id(0)
        dx, dgrow = _rms_bwd_math(accs[0], in_refs[3][...], in_refs[4][...])
        out_refs[0][...] = in_refs[2][...] + dx

        @pl.when(i == 0)
        def _():
            out_refs[1][...] = jnp.zeros_like(out_refs[1])

        out_refs[1][...] += jnp.sum(dgrow, axis=0, keepdims=True)

    row = pl.BlockSpec((tm, d), lambda i: (i, 0))
    return _matmul(
        "mix_in_proj_bwd", (lp // tm,), None, [dz, w_in, dh, h_in, gain.reshape(1, d)],
        [pl.BlockSpec((tm, inw), lambda i: (i, 0)), pl.BlockSpec((d, inw), lambda i: (0, 0)), row, row,
         pl.BlockSpec((1, d), lambda i: (0, 0))],
        [(0, 1, "nt", 0)], [(tm, d)], epi,
        [jax.ShapeDtypeStruct((lp, d), F32), jax.ShapeDtypeStruct((1, d), F32)],
        [row, pl.BlockSpec((1, d), lambda i: (0, 0))], ("arbitrary",))


def _loss_head(h, gain, target):
    lp, d = h.shape
    nb = lp // BLOCK

    def body(h_ref, g_ref, t_ref, dh_ref, dg_ref, loss_ref):
        i = pl.program_id(0)

        @pl.when(i == 0)
        def _():
            dg_ref[...] = jnp.zeros_like(dg_ref)
            loss_ref[...] = jnp.zeros_like(loss_ref)
            dh_ref[...] = jnp.zeros_like(dh_ref)

        @pl.when(i > 0)
        def _():
            x = h_ref[...]
            g = g_ref[...]
            r = lax.rsqrt(jnp.mean(x * x, axis=-1, keepdims=True) + EPS)
            err = x * r * g - t_ref[...]
            loss_ref[...] += jnp.zeros_like(loss_ref) + 0.5 * jnp.sum(jnp.sum(err * err, axis=-1, keepdims=True)) / d
            dx, dgrow = _rms_bwd_math(err * (1.0 / d), x, g)
            dh_ref[...] = dx
            dg_ref[...] += jnp.sum(dgrow, axis=0, keepdims=True)

    row = pl.BlockSpec((BLOCK, d), lambda i: (i, 0))
    one = pl.BlockSpec((1, d), lambda i: (0, 0))
    return pl.pallas_call(
        body,
        out_shape=[jax.ShapeDtypeStruct((lp, d), F32), jax.ShapeDtypeStruct((1, d), F32),
                   jax.ShapeDtypeStruct((SUBLANES, LANES), F32)],
        grid=(nb,),
        in_specs=[row, one, pl.BlockSpec((BLOCK, d), lambda i: (jnp.maximum(i - 1, 0), 0))],
        out_specs=[row, one, pl.BlockSpec((SUBLANES, LANES), lambda i: (0, 0))],
        compiler_params=_cparams(("arbitrary",)), name="loss_head")(h, gain.reshape(1, d), target)


def _adam_math(w, g, m, v):
    m = ADAM_B1 * m + (1.0 - ADAM_B1) * g
    v = ADAM_B2 * v + (1.0 - ADAM_B2) * (g * g)
    m_hat = m / (1.0 - ADAM_B1 ** ADAM_STEP)
    v_hat = v / (1.0 - ADAM_B2 ** ADAM_STEP)
    delta = -ADAM_LR * (m_hat / (jnp.sqrt(v_hat) + ADAM_EPS) + ADAM_WD * w)
    return delta, m, v


def _adamw_layers(w, m, v, mine, other, pos, name):
    depth, r, c = w.shape
    half = r // 2
    tr = _div_tile(half, c * 4)
    nh = half // tr

    def body(*refs):
        pos_ref, w_ref, m_ref, v_ref = refs[:4]
        mine_refs = refs[4:4 + depth]
        other_refs = refs[4 + depth:4 + 2 * depth]
        g_out, d_out, m_out, v_out = refs[4 + 2 * depth:]
        layer, i = pl.program_id(0), pl.program_id(1)
        is_mine = (i // nh) == pos_ref[0]

        def update(g):
            delta, nm, nv = _adam_math(w_ref[...], g, m_ref[...], v_ref[...])
            g_out[...] = g
            d_out[...] = delta
            m_out[...] = nm
            v_out[...] = nv

        for l in range(depth):
            @pl.when((layer == l) & is_mine)
            def _(l=l):
                update(mine_refs[l][...])

            @pl.when((layer == l) & jnp.logical_not(is_mine))
            def _(l=l):
                update(other_refs[l][...])

    stacked = pl.BlockSpec((None, tr, c), lambda l, i, p: (l, i, 0))

    def gspec(layer, is_other):
        def imap(l, i, p):
            first = jnp.where(is_other, 1 - p[0], p[0]) * nh
            here = jnp.clip(i - first, 0, nh - 1)
            return (jnp.where(l == layer, here, jnp.where(l < layer, 0, nh - 1)), 0)
        return pl.BlockSpec((tr, c), imap)

    shp = jax.ShapeDtypeStruct((depth, r, c), F32)
    grid_spec = pltpu.PrefetchScalarGridSpec(
        num_scalar_prefetch=1, grid=(depth, 2 * nh),
        in_specs=[stacked] * 3 + [gspec(l, 0) for l in range(depth)] + [gspec(l, 1) for l in range(depth)],
        out_specs=[stacked] * 4)
    return pl.pallas_call(
        body, out_shape=[shp] * 4, grid_spec=grid_spec,
        compiler_params=_cparams(("arbitrary", "arbitrary")), name=name)(pos, w, m, v, *mine, *other)


def _adamw_flat(w, g, m, v, name):
    r, c = w.shape
    tr = _div_tile(r, c * 4)

    def body(w_ref, g_ref, m_ref, v_ref, d_out, m_out, v_out):
        delta, nm, nv = _adam_math(w_ref[...], g_ref[...], m_ref[...], v_ref[...])
        d_out[...] = delta
        m_out[...] = nm
        v_out[...] = nv

    spec = pl.BlockSpec((tr, c), lambda i: (i, 0))
    shp = jax.ShapeDtypeStruct((r, c), F32)
    return pl.pallas_call(
        body, out_shape=[shp] * 3, grid=(r // tr,), in_specs=[spec] * 4, out_specs=[spec] * 3,
        compiler_params=_cparams(("parallel",)), name=name)(w, g, m, v)


def _mesh_pos():
    return lax.axis_index("x"), lax.axis_index("y"), lax.axis_index("c")


def _row_half(ref, which, lead):
    half = ref.shape[lead] // 2
    idx = (slice(None),) * lead + (pl.ds(which * half, half), slice(None))
    return ref.at[idx]


def _gather_comm(arrs, tag):
    n = len(arrs)

    def ctx(ins, outs, sems):
        send_sems, recv_sems, local_sems = sems
        x, y, c = _mesh_pos()
        chips = [(1 - x, y), (x, 1 - y), (1 - x, 1 - y)]

        def slot(k, chip, which):
            lead = len(ins[k].shape) - 2
            return _row_half(outs[k].at[2 * chip[0] + chip[1]], which, lead)

        def copy(k, j, src, dst, to):
            return pltpu.make_async_remote_copy(
                src_ref=src, dst_ref=dst, send_sem=send_sems.at[6 * k + j], recv_sem=recv_sems.at[6 * k + j],
                device_id=to, device_id_type=MESH)

        def local(k):
            return pltpu.make_async_copy(ins[k], outs[k].at[2 * x + y], local_sems.at[k])

        def first(k, j):
            lead = len(ins[k].shape) - 2
            return copy(k, j, _row_half(ins[k], c, lead), slot(k, (x, y), c), (*chips[j], c))

        def passed(k, j, which):
            return copy(k, 3 + j, slot(k, chips[j], which), slot(k, chips[j], which), (x, y, 1 - c))

        def landed(k, j):
            return copy(k, j, slot(k, chips[j], c), slot(k, chips[j], c), (x, y, 1 - c))

        return c, local, first, passed, landed

    def start(ins, outs, sems):
        c, local, first, passed, landed = ctx(ins, outs, sems)
        for k in range(n):
            local(k).start()
            for j in range(3):
                first(k, j).start()

    def mid(ins, outs, sems):
        c, local, first, passed, landed = ctx(ins, outs, sems)
        for j in range(3):
            for k in range(n):
                landed(k, j).wait_recv()
                passed(k, j, c).start()

    def finish(ins, outs, sems):
        c, local, first, passed, landed = ctx(ins, outs, sems)
        for j in range(3):
            for k in range(n):
                passed(k, j, 1 - c).wait_recv()
        for k in range(n):
            for j in range(3):
                first(k, j).wait_send()
                passed(k, j, c).wait_send()
            local(k).wait()

    return _Comm(
        tag, arrs, [jax.ShapeDtypeStruct((N_CHIPS,) + a.shape, a.dtype) for a in arrs],
        [pltpu.SemaphoreType.DMA((6 * n,)), pltpu.SemaphoreType.DMA((6 * n,)), pltpu.SemaphoreType.DMA((n,))],
        start, mid, finish)


def _run_comm(comm, name):
    n_in, n_out = len(comm.ins), len(comm.out_shapes)

    def body(*refs):
        ins, outs, sems = refs[:n_in], refs[n_in:n_in + n_out], refs[n_in + n_out:]
        comm.start(ins, outs, sems)
        if comm.mid is not None:
            comm.mid(ins, outs, sems)
        comm.finish(ins, outs, sems)

    return pl.pallas_call(
        body, out_shape=comm.out_shapes, in_specs=[HBM_SPEC] * n_in, out_specs=[HBM_SPEC] * n_out,
        scratch_shapes=comm.sems, name=name)(*comm.ins)


def _all_gather_chips(arrs, name):
    return _run_comm(_gather_comm(arrs, "gather"), name)


def _all_gather_devices(x_shard, name):
    m_per, ncol = x_shard.shape

    def body(x_ref, out_ref, send_sems, recv_sems, local_sem):
        x, y, c = _mesh_pos()
        me, sibling = (x, y, c), (x, y, 1 - c)
        chips = [(1 - x, y), (x, 1 - y), (1 - x, 1 - y)]

        def rows(px, py, pc):
            return out_ref.at[4 * px + 2 * py + pc]

        def copy(k, block, to, src=None):
            return pltpu.make_async_remote_copy(
                src_ref=rows(*block) if src is None else src, dst_ref=rows(*block),
                send_sem=send_sems.at[k], recv_sem=recv_sems.at[k], device_id=to, device_id_type=MESH)

        mine = pltpu.make_async_copy(x_ref, rows(*me), local_sem)
        mine.start()
        first = [copy(0, me, sibling, src=x_ref)]
        first += [copy(1 + j, me, (*chip, c), src=x_ref) for j, chip in enumerate(chips)]
        for cp in first:
            cp.start()
        passed = [copy(4 + j, (*chip, c), sibling) for j, chip in enumerate(chips)]
        for j, chip in enumerate(chips):
            copy(1 + j, (*chip, c), me).wait_recv()
            passed[j].start()
        copy(0, sibling, me).wait_recv()
        for j, chip in enumerate(chips):
            copy(4 + j, (*chip, 1 - c), me).wait_recv()
        for cp in first + passed:
            cp.wait_send()
        mine.wait()

    return pl.pallas_call(
        body, out_shape=jax.ShapeDtypeStruct((8, m_per, ncol), x_shard.dtype),
        in_specs=[pl.BlockSpec(memory_space=pltpu.VMEM)], out_specs=pl.BlockSpec(memory_space=pltpu.VMEM),
        scratch_shapes=[pltpu.SemaphoreType.DMA((7,)), pltpu.SemaphoreType.DMA((7,)), pltpu.SemaphoreType.DMA],
        compiler_params=pltpu.CompilerParams(vmem_limit_bytes=VMEM_LIMIT), name=name)(x_shard)


def _sum_devices(g8, name):
    _, r, c = g8.shape
    tr = _div_tile(r, c * 4 * 8)

    def body(g_ref, o_ref):
        acc = g_ref[0]
        for dev in range(1, 8):
            acc = acc + g_ref[dev]
        o_ref[...] = acc

    return pl.pallas_call(
        body, out_shape=jax.ShapeDtypeStruct((r, c), F32), grid=(r // tr,),
        in_specs=[pl.BlockSpec((8, tr, c), lambda i: (0, i, 0))], out_specs=pl.BlockSpec((tr, c), lambda i: (i, 0)),
        compiler_params=_cparams(("parallel",)), name=name)(g8)


def _exchange_sibling_halves(arrs, name):
    n = len(arrs)

    def body(*refs):
        ins, outs = refs[:n], refs[n:2 * n]
        send_sems, recv_sems = refs[2 * n:]
        x, y, c = _mesh_pos()
        cps = []
        for k in range(n):
            cp = pltpu.make_async_remote_copy(
                src_ref=_row_half(ins[k], 1 - c, 1), dst_ref=outs[k], send_sem=send_sems.at[k],
                recv_sem=recv_sems.at[k], device_id=(x, y, 1 - c), device_id_type=MESH)
            cp.start()
            cps.append(cp)
        for cp in cps:
            cp.wait()

    return pl.pallas_call(
        body,
        out_shape=[jax.ShapeDtypeStruct((a.shape[0], a.shape[1] // 2, a.shape[2]), a.dtype) for a in arrs],
        in_specs=[HBM_SPEC] * n, out_specs=[HBM_SPEC] * n,
        scratch_shapes=[pltpu.SemaphoreType.DMA((n,)), pltpu.SemaphoreType.DMA((n,))], name=name)(*arrs)


def _chip_partial(arr, recv, pos, name):
    nslab, r, c = arr.shape
    half = r // 2

    def body(pos_ref, a_ref, b_ref, o_ref):
        o_ref[...] = (a_ref[...] + b_ref[...]).astype(BF16)

    grid_spec = pltpu.PrefetchScalarGridSpec(
        num_scalar_prefetch=1, grid=(nslab,),
        in_specs=[pl.BlockSpec((None, half, c), lambda j, p: (j, p[0], 0)),
                  pl.BlockSpec((None, half, c), lambda j, p: (j, 0, 0))],
        out_specs=pl.BlockSpec((None, half, c), lambda j, p: (j, 0, 0)))
    return pl.pallas_call(
        body, out_shape=jax.ShapeDtypeStruct((nslab, half, c), BF16), grid_spec=grid_spec,
        compiler_params=_cparams(("parallel",)), name=name)(pos, arr, recv)


def _chip_exchange_comm(parts, tag):
    n = len(parts)

    def copies(ins, outs, sems):
        send_sems, recv_sems = sems
        x, y, c = _mesh_pos()
        chips = [(1 - x, y), (x, 1 - y), (1 - x, 1 - y)]
        return [pltpu.make_async_remote_copy(
            src_ref=ins[k].at[2 * chip[0] + chip[1]], dst_ref=outs[k].at[j],
            send_sem=send_sems.at[3 * k + j], recv_sem=recv_sems.at[3 * k + j],
            device_id=(*chip, c), device_id_type=MESH) for k in range(n) for j, chip in enumerate(chips)]

    def start(ins, outs, sems):
        for cp in copies(ins, outs, sems):
            cp.start()

    def finish(ins, outs, sems):
        for cp in copies(ins, outs, sems):
            cp.wait()

    return _Comm(
        tag, parts, [jax.ShapeDtypeStruct((3,) + p.shape[1:], p.dtype) for p in parts],
        [pltpu.SemaphoreType.DMA((3 * n,)), pltpu.SemaphoreType.DMA((3 * n,))], start, None, finish)


def _reduce_half(arr, recv, got, pos, name):
    nslab, r, c = arr.shape
    half = r // 2

    def body(pos_ref, a_ref, b_ref, g_ref, o_ref):
        acc = a_ref[...] + b_ref[...]
        for j in range(3):
            acc = acc + g_ref[j].astype(F32)
        o_ref[...] = acc

    grid_spec = pltpu.PrefetchScalarGridSpec(
        num_scalar_prefetch=1, grid=(1,),
        in_specs=[pl.BlockSpec((None, half, c), lambda i, p: (p[1], p[0], 0)),
                  pl.BlockSpec((None, half, c), lambda i, p: (p[1], 0, 0)),
                  pl.BlockSpec((3, half, c), lambda i, p: (0, 0, 0))],
        out_specs=pl.BlockSpec((half, c), lambda i, p: (0, 0)))
    return pl.pallas_call(
        body, out_shape=jax.ShapeDtypeStruct((half, c), F32), grid_spec=grid_spec,
        compiler_params=_cparams(("arbitrary",)), name=name)(pos, arr, recv, got)


def _share_halves(halves, name):
    n = len(halves)

    def body(*refs):
        ins, outs = refs[:n], refs[n:2 * n]
        send_sems, recv_sems = refs[2 * n:]
        x, y, c = _mesh_pos()
        cps = []
        for k in range(n):
            cp = pltpu.make_async_remote_copy(
                src_ref=ins[k], dst_ref=outs[k], send_sem=send_sems.at[k], recv_sem=recv_sems.at[k],
                device_id=(x, y, 1 - c), device_id_type=MESH)
            cp.start()
            cps.append(cp)
        for cp in cps:
            cp.wait()

    return pl.pallas_call(
        body, out_shape=[jax.ShapeDtypeStruct(h.shape, h.dtype) for h in halves],
        in_specs=[HBM_SPEC] * n, out_specs=[HBM_SPEC] * n,
        scratch_shapes=[pltpu.SemaphoreType.DMA((n,)), pltpu.SemaphoreType.DMA((n,))], name=name)(*halves)


class _Reduction:
    def __init__(self, arrs, pos, tag):
        self.arrs, self.pos, self.tag = arrs, pos, tag
        self.recv = _exchange_sibling_halves(arrs, "rs_sibling_" + tag)
        self.parts = [_chip_partial(a, r, pos, "rs_partial") for a, r in zip(arrs, self.recv)]
        self.got = None

    def comm(self):
        return _chip_exchange_comm(self.parts, "rs_" + self.tag)

    def end(self):
        if self.got is None:
            self.got = _run_comm(self.comm(), "rs_chips_" + self.tag)
        halves = [_reduce_half(a, r, g, self.pos, "rs_reduce") for a, r, g in zip(self.arrs, self.recv, self.got)]
        return halves, _share_halves(halves, "rs_share_" + self.tag)


def _put_mix_weights(p, gathered):
    win, p["w3"], p["w_out"] = gathered
    p["w_in"] = jnp.concatenate([win[j] for j in range(N_CHIPS)], axis=1)


def _layer_fwd(h, p, nxt, tabs, jobs):
    def comm(key, tag):
        return None if jobs.get(key) is None else _gather_comm(jobs[key], tag)

    h1, ffn1_saved, got = _ffn_fwd(h, p["ffn1_norm"], p["wgu1"], p["wd1"], comm("ffn1_up", "gather_mix"))
    if got:
        _put_mix_weights(p, got)
    n = _rms_fwd(h1, p["mix_norm"], "rms_fwd_mix")
    ssm_w = p["ssm_d"].shape[0]
    q, k, v, u, ga, gs = _in_proj(n, p["w_in"], tabs, ssm_w)
    o, got = _attn_fwd(q, k, v, p["attn_sinks"], comm("attn", "gather_ffn"))
    if got:
        p["wgu2"], p["wd2"] = got
    y, got = _ssm_fwd(u, *p["ssm_tabs"], p["ssm_d"], comm("ssm", "gather_ffn"))
    if got:
        nxt["wgu1"], nxt["wd1"] = got
    yg = _gelu_fwd(y)
    merged = _merge_fwd(o, yg, ga, gs, p["w3"])
    h2 = _out_proj(merged, p["w_out"], h1)
    h3, ffn2_saved, got = _ffn_fwd(h2, p["ffn2_norm"], p["wgu2"], p["wd2"], comm("ffn2_up", "gather_mix"))
    if got:
        _put_mix_weights(nxt, got)
    saved = dict(h0=h, h1=h1, h2=h2, ffn1=ffn1_saved, ffn2=ffn2_saved, q=q, k=k, v=v, u=u, ga=ga, gs=gs, o=o, y=y,
                 yg=yg, merged=merged)
    return h3, saved


def _layer_bwd(dh, p, s, tabs, pos, carried):
    g = {}
    dh2, g["ffn2_norm"], dwg2, dwu2, dwd2, _ = _ffn_bwd(
        dh, s["h2"], p["ffn2_norm"], p["wgu2"], p["wd2"], s["ffn2"])
    red_ffn2 = _Reduction([dwg2, dwu2, dwd2], pos, "ffn")
    lp, d = dh2.shape
    d4 = d // N_CHIPS
    dhb = _scale_cast(dh2, 1.0, "mix_dh_cast")
    dw_out = _tn_full(s["merged"], dhb, "mix_dw_out").reshape(N_CHIPS, d4, d)
    dattn, dv, dg, dga, dgs = _merge_bwd(dhb, p["w_out"], s["o"], s["yg"], s["ga"], s["gs"], p["w3"])
    (dw_ap,) = _tn_cols(s["o"], [dattn], "mix_dw_ap")
    dw_gv, dw_gg = _tn_cols(s["yg"], [dv, dg], "mix_dw_glu")
    do, dy = _branch_bwd(dattn, dv, dg, p["w3"], s["y"])
    (dq, dk, dvv, dkm, dvm, dsink), red_ffn2.got = _attn_bwd(
        s["q"], s["k"], s["v"], do, p["attn_sinks"], tabs, red_ffn2.comm())
    g["attn_sinks"] = dsink[:, 0]
    (du, dlr, dli, dbr, dbi, dcr, dci, dd), got = _ssm_bwd(
        s["u"], dy, *p["ssm_tabs"], p["ssm_d"], None if carried is None else carried.comm())
    if carried is not None:
        carried.got = got
    ngrp = p["ssm_d"].shape[0] // SSM_GROUP
    g["ssm_lam"] = (dlr.reshape(ngrp, SSM_STATE), dli.reshape(ngrp, SSM_STATE),
                    _ssm_untable_b(dbr, ngrp), _ssm_untable_b(dbi, ngrp))
    g["ssm_c_re"] = _ssm_untable_c(dcr, ngrp)
    g["ssm_c_im"] = _ssm_untable_c(dci, ngrp)
    g["ssm_d"] = dd[0]
    dk = dk.at[:BLOCK].add(dkm)
    dvv = dvv.at[:BLOCK].add(dvm)
    dz = jnp.concatenate([dq.astype(BF16), dk.astype(BF16), dvv.astype(BF16), du.astype(BF16), dga, dgs], axis=1)
    n = _rms_fwd(s["h1"], p["mix_norm"], "rms_fwd_mix")
    inw = p["w_in"].shape[1]
    tn_cols = inw // 2 if (inw // 2) % LANES == 0 else None
    dw_in = _tn_full(n, dz, "mix_dw_in", tn_cols)
    inw4 = inw // N_CHIPS
    dw_in = jnp.stack([dw_in[:, j * inw4:(j + 1) * inw4] for j in range(N_CHIPS)])
    red_mix = _Reduction([dw_in, dw_ap, dw_gv, dw_gg, dw_out], pos, "mix")
    dh1, g["mix_norm"] = _in_proj_bwd(dz, p["w_in"], dh2, s["h1"], p["mix_norm"])
    dh0, g["ffn1_norm"], dwg1, dwu1, dwd1, red_mix.got = _ffn_bwd(
        dh1, s["h0"], p["ffn1_norm"], p["wgu1"], p["wd1"], s["ffn1"], red_mix.comm())
    red_ffn1 = _Reduction([dwg1, dwu1, dwd1], pos, "ffn")
    return dh0, g, [red_ffn1, red_mix, red_ffn2]


BIG = ["ffn1_w_gate", "ffn1_w_up", "ffn1_w_down", "w_in", "w_attn_proj", "w_glu_v", "w_glu_g", "w_out",
       "ffn2_w_gate", "ffn2_w_up", "ffn2_w_down"]
SMALL = ["ffn1_norm", "mix_norm", "attn_sinks", "ssm_a_re", "ssm_a_im", "ssm_log_dt", "ssm_b_re", "ssm_b_im",
         "ssm_c_re", "ssm_c_im", "ssm_d", "ffn2_norm", "final_norm"]
WEIGHTS = ["meta_tokens", "ffn1_norm", "ffn1_w_gate", "ffn1_w_up", "ffn1_w_down", "mix_norm", "w_in", "attn_sinks",
           "ssm_a_re", "ssm_a_im", "ssm_log_dt", "ssm_b_re", "ssm_b_im", "ssm_c_re", "ssm_c_im", "ssm_d",
           "w_attn_proj", "w_glu_v", "w_glu_g", "w_out", "ffn2_norm", "ffn2_w_gate", "ffn2_w_up", "ffn2_w_down",
           "final_norm"]


def _pack_small(tree):
    flat = jnp.concatenate([tree[k].reshape(-1) for k in SMALL + ["meta_tokens"]])
    rows = -(-flat.shape[0] // (LANES * LANES)) * LANES
    return jnp.pad(flat, (0, rows * LANES - flat.shape[0])).reshape(rows, LANES)


def _unpack_small(packed, like):
    flat = packed.reshape(-1)
    out, off = {}, 0
    for k in SMALL + ["meta_tokens"]:
        size = math.prod(like[k].shape)
        out[k] = flat[off:off + size].reshape(like[k].shape)
        off += size
    return out


def kernel(x, meta_tokens, ffn1_norm, ffn1_w_gate, ffn1_w_up, ffn1_w_down, mix_norm, w_in, attn_sinks, ssm_a_re, ssm_a_im, ssm_log_dt, ssm_b_re, ssm_b_im, ssm_c_re, ssm_c_im, ssm_d, w_attn_proj, w_glu_v, w_glu_g, w_out, ffn2_norm, ffn2_w_gate, ffn2_w_up, ffn2_w_down, final_norm, loss_target, m_meta_tokens, m_ffn1_norm, m_ffn1_w_gate, m_ffn1_w_up, m_ffn1_w_down, m_mix_norm, m_w_in, m_attn_sinks, m_ssm_a_re, m_ssm_a_im, m_ssm_log_dt, m_ssm_b_re, m_ssm_b_im, m_ssm_c_re, m_ssm_c_im, m_ssm_d, m_w_attn_proj, m_w_glu_v, m_w_glu_g, m_w_out, m_ffn2_norm, m_ffn2_w_gate, m_ffn2_w_up, m_ffn2_w_down, m_final_norm, v_meta_tokens, v_ffn1_norm, v_ffn1_w_gate, v_ffn1_w_up, v_ffn1_w_down, v_mix_norm, v_w_in, v_attn_sinks, v_ssm_a_re, v_ssm_a_im, v_ssm_log_dt, v_ssm_b_re, v_ssm_b_im, v_ssm_c_re, v_ssm_c_im, v_ssm_d, v_w_attn_proj, v_w_glu_v, v_w_glu_g, v_w_out, v_ffn2_norm, v_ffn2_w_gate, v_ffn2_w_up, v_ffn2_w_down, v_final_norm):
    args = dict(locals())
    w = {k: args[k] for k in WEIGHTS}
    m = {k: args["m_" + k] for k in WEIGHTS}
    v = {k: args["v_" + k] for k in WEIGHTS}
    depth = ffn1_norm.shape[0]
    seq, d = x.shape[1], x.shape[2]
    lp = seq + BLOCK
    xi, yi, ci = _mesh_pos()
    pos = jnp.stack([ci, 2 * xi + yi]).astype(jnp.int32)

    tabs = _rope_tables(lp)
    (meta_all,) = _all_gather_chips([meta_tokens], "gather_meta")
    meta_full = jnp.concatenate([meta_all[j] for j in range(N_CHIPS)], axis=1)
    layers, shards = [], []
    for l in range(depth):
        shards.append(dict(
            ffn1=[jnp.stack([ffn1_w_gate[l], ffn1_w_up[l]]).astype(BF16), ffn1_w_down[l].astype(BF16)],
            mix=[w_in[l].astype(BF16), jnp.stack([w_attn_proj[l], w_glu_v[l], w_glu_g[l]]).astype(BF16),
                 w_out[l].astype(BF16)],
            ffn2=[jnp.stack([ffn2_w_gate[l], ffn2_w_up[l]]).astype(BF16), ffn2_w_down[l].astype(BF16)]))
        lb_re, lb_im, bb_re, bb_im = _ssm_params(ssm_a_re[l], ssm_a_im[l], ssm_log_dt[l], ssm_b_re[l], ssm_b_im[l])
        ngrp = lb_re.shape[0]
        nt = ngrp // GROUPS_PER_TILE
        ssm_tabs = (lb_re.reshape(nt, 1, TILE_STATES), lb_im.reshape(nt, 1, TILE_STATES),
                    *_ssm_tables(bb_re, bb_im, ssm_c_re[l], ssm_c_im[l]))
        layers.append(dict(
            ffn1_norm=ffn1_norm[l], mix_norm=mix_norm[l], ffn2_norm=ffn2_norm[l], attn_sinks=attn_sinks[l],
            ssm_d=ssm_d[l], ssm_tabs=ssm_tabs))
    layers[0]["wgu1"], layers[0]["wd1"] = _all_gather_chips(shards[0]["ffn1"], "gather_ffn")

    h = jnp.concatenate([jnp.zeros((PAD_FRONT, d), F32), meta_full, x[0]], axis=0)
    saved = []
    for l in range(depth):
        more = l + 1 < depth
        jobs = dict(ffn1_up=shards[l]["mix"] if l == 0 else None, attn=shards[l]["ffn2"],
                    ssm=shards[l + 1]["ffn1"] if more else None, ffn2_up=shards[l + 1]["mix"] if more else None)
        h, s = _layer_fwd(h, layers[l], layers[l + 1] if more else None, tabs, jobs)
        saved.append(s)
    dh, g_final, loss_acc = _loss_head(h, final_norm, loss_target[0])
    loss = lax.psum(loss_acc[0, 0], ("x", "y", "c"))

    grads, reds = [None] * depth, [None] * depth
    carried = None
    for l in reversed(range(depth)):
        dh, grads[l], reds[l] = _layer_bwd(dh, layers[l], saved[l], tabs, pos, carried)
        carried = reds[l][0]
    grad_x = dh[BLOCK:][None]
    dmeta_local = dh[PAD_FRONT:BLOCK]

    small = {k: [] for k in SMALL}
    for l in range(depth):
        gl = grads[l]
        _, vjp = jax.vjp(_ssm_params, ssm_a_re[l], ssm_a_im[l], ssm_log_dt[l], ssm_b_re[l], ssm_b_im[l])
        da_re, da_im, dlog_dt, db_re, db_im = vjp(gl["ssm_lam"])
        for k, val in (("ffn1_norm", gl["ffn1_norm"][0]), ("mix_norm", gl["mix_norm"][0]),
                       ("attn_sinks", gl["attn_sinks"]), ("ssm_a_re", da_re), ("ssm_a_im", da_im),
                       ("ssm_log_dt", dlog_dt), ("ssm_b_re", db_re), ("ssm_b_im", db_im),
                       ("ssm_c_re", gl["ssm_c_re"]), ("ssm_c_im", gl["ssm_c_im"]), ("ssm_d", gl["ssm_d"]),
                       ("ffn2_norm", gl["ffn2_norm"][0])):
            small[k].append(val)
    small_local = {k: jnp.stack(vals) for k, vals in small.items() if k != "final_norm"}
    small_local["final_norm"] = g_final[0]
    small_local["meta_tokens"] = dmeta_local
    like = dict(small_local)
    g_small = _sum_devices(_all_gather_devices(_pack_small(small_local), "gather_small_grads"), "sum_small_grads")
    g_small_tree = _unpack_small(g_small, like)
    d4 = d // N_CHIPS
    chip = 2 * xi + yi
    g_meta = lax.dynamic_slice_in_dim(g_small_tree["meta_tokens"], chip * d4, d4, axis=1)

    reduced = []
    for l in range(depth):
        mine, other = [], []
        for red in reds[l]:
            halves, sibling_halves = red.end()
            mine += halves
            other += sibling_halves
        reduced.append((mine, other))

    g_out, delta, new_m, new_v = {}, {}, {}, {}
    for i, k in enumerate(BIG):
        g_out[k], delta[k], new_m[k], new_v[k] = _adamw_layers(
            w[k], m[k], v[k], [reduced[l][0][i] for l in range(depth)], [reduced[l][1][i] for l in range(depth)],
            pos, "adamw_" + k)
    small_names = SMALL + ["meta_tokens"]
    w_small = {k: w[k] for k in small_names}
    m_small = {k: m[k] for k in small_names}
    v_small = {k: v[k] for k in small_names}
    g_small_local = dict(g_small_tree)
    g_small_local["meta_tokens"] = g_meta
    d_s, m_s, v_s = _adamw_flat(_pack_small(w_small), _pack_small(g_small_local), _pack_small(m_small),
                                _pack_small(v_small), "adamw_small")
    for tree, packed in ((delta, d_s), (new_m, m_s), (new_v, v_s)):
        tree.update(_unpack_small(packed, w_small))
    for k in small_names:
        g_out[k] = g_small_local[k]

    return (loss, grad_x, *[g_out[k] for k in WEIGHTS], *[delta[k] for k in WEIGHTS],
            *[new_m[k] for k in WEIGHTS], *[new_v[k] for k in WEIGHTS])
```

```python
import functools
import math

import jax
import jax.numpy as jnp
from jax import lax
from jax.experimental import pallas as pl
from jax.experimental.pallas import tpu as pltpu

F32 = jnp.float32
BF16 = jnp.bfloat16

N_META = 16
HEAD_DIM = 64
N_Q_HEADS = 8
N_KV_HEADS = 2
Q_PER_KV = N_Q_HEADS // N_KV_HEADS
ATTN_WIDTH = N_Q_HEADS * HEAD_DIM
KV_WIDTH = N_KV_HEADS * HEAD_DIM
BLOCK = 128
PAD_FRONT = BLOCK - N_META
ROPE_THETA = 500000.0
ROT_DIM = HEAD_DIM // 4
SSM_GROUP = 16
SSM_STATE = 64
GROUPS_PER_TILE = 4
TILE_STATES = GROUPS_PER_TILE * SSM_STATE
LANES = 128
SUBLANES = 8
EPS = 1e-6
NEG_INF = -1e30
N_CHIPS = 4

ADAM_LR = 0.001
ADAM_B1 = 0.9
ADAM_B2 = 0.999
ADAM_EPS = 1e-08
ADAM_WD = 0.01
ADAM_STEP = 10

VMEM_LIMIT = 56 * 1024 * 1024
MESH = pl.DeviceIdType.MESH


def _cparams(sem=None):
    return pltpu.CompilerParams(dimension_semantics=sem, vmem_limit_bytes=VMEM_LIMIT)


def _row_tile(rows, limit=512):
    best = None
    for t in range(128, limit + 1, 128):
        if rows % t == 0:
            best = t
    assert best is not None, rows
    return best


def _div_tile(rows, row_bytes, max_bytes=1 << 20, mult=8):
    best = None
    for t in range(mult, rows + 1, mult):
        if rows % t == 0 and t * row_bytes <= max_bytes:
            best = t
    if best is None:
        best = rows
    return best


def _dot(a, b, mode):
    if mode == "nn":
        dims = (((1,), (0,)), ((), ()))
    elif mode == "nt":
        dims = (((1,), (1,)), ((), ()))
    else:
        dims = (((0,), (0,)), ((), ()))
    return lax.dot_general(a.astype(BF16), b.astype(BF16), dims, preferred_element_type=F32)


def _sigmoid(x):
    return 1.0 / (1.0 + jnp.exp(-x))


_GELU_C = math.sqrt(2.0 / math.pi)


def _gelu(x):
    return 0.5 * x * (1.0 + jnp.tanh(_GELU_C * (x + 0.044715 * x * x * x)))


def _gelu_grad(x):
    t = jnp.tanh(_GELU_C * (x + 0.044715 * x * x * x))
    return 0.5 * (1.0 + t) + 0.5 * x * (1.0 - t * t) * _GELU_C * (1.0 + 3.0 * 0.044715 * x * x)


class _Comm:
    def __init__(self, tag, ins, out_shapes, sems, start, mid, finish):
        self.tag, self.ins, self.out_shapes, self.sems = tag, list(ins), list(out_shapes), list(sems)
        self.start, self.mid, self.finish = start, mid, finish


HBM_SPEC = pl.BlockSpec(memory_space=pltpu.HBM)


def _hosted_call(body, comm, *, out_shape, grid, in_specs, out_specs, scratch_shapes, sem, name, args):
    out_shape, in_specs, out_specs = list(out_shape), list(in_specs), list(out_specs)
    scratch_shapes = list(scratch_shapes)
    if comm is None:
        res = pl.pallas_call(
            body, out_shape=out_shape, grid=grid, in_specs=in_specs, out_specs=out_specs,
            scratch_shapes=scratch_shapes, compiler_params=_cparams(sem), name=name)(*args)
        return list(res), []
    n_in, n_out, n_sc = len(args), len(out_shape), len(scratch_shapes)
    nci, nco = len(comm.ins), len(comm.out_shapes)
    total = math.prod(grid)

    def wrapped(*refs):
        in_refs, cin = refs[:n_in], refs[n_in:n_in + nci]
        o0 = n_in + nci
        out_refs, cout = refs[o0:o0 + n_out], refs[o0 + n_out:o0 + n_out + nco]
        s0 = o0 + n_out + nco
        sc, csem = refs[s0:s0 + n_sc], refs[s0 + n_sc:]
        lin = 0
        for dim, size in enumerate(grid):
            lin = lin * size + pl.program_id(dim)

        @pl.when(lin == 0)
        def _():
            comm.start(cin, cout, csem)

        if comm.mid is not None:
            @pl.when(lin == total // 2)
            def _():
                comm.mid(cin, cout, csem)

        body(*in_refs, *out_refs, *sc)

        @pl.when(lin == total - 1)
        def _():
            comm.finish(cin, cout, csem)

    res = pl.pallas_call(
        wrapped, out_shape=out_shape + comm.out_shapes, grid=grid,
        in_specs=in_specs + [HBM_SPEC] * nci, out_specs=out_specs + [HBM_SPEC] * nco,
        scratch_shapes=scratch_shapes + comm.sems,
        compiler_params=_cparams(("arbitrary",) * len(grid)), name=name + "_" + comm.tag)(*args, *comm.ins)
    return list(res[:n_out]), list(res[n_out:])


def _matmul(name, grid, k_axis, ins, in_specs, pairs, acc_shapes, epilogue, out_shapes, out_specs, sem, comm=None):
    n_in, n_out, n_acc = len(ins), len(out_shapes), len(acc_shapes)

    def body(*refs):
        in_refs = refs[:n_in]
        out_refs = refs[n_in:n_in + n_out]
        acc_refs = refs[n_in + n_out:]
        if k_axis is None:
            accs = [None] * n_acc
            for ia, ib, mode, iacc in pairs:
                d = _dot(in_refs[ia][...], in_refs[ib][...], mode)
                accs[iacc] = d if accs[iacc] is None else accs[iacc] + d
            epilogue(accs, in_refs, out_refs)
            return
        k = pl.program_id(k_axis)

        @pl.when(k == 0)
        def _():
            for r in acc_refs:
                r[...] = jnp.zeros_like(r)

        for ia, ib, mode, iacc in pairs:
            acc_refs[iacc][...] += _dot(in_refs[ia][...], in_refs[ib][...], mode)

        @pl.when(k == pl.num_programs(k_axis) - 1)
        def _():
            epilogue([r[...] for r in acc_refs], in_refs, out_refs)

    scratch = [] if k_axis is None else [pltpu.VMEM(s, F32) for s in acc_shapes]
    outs, couts = _hosted_call(
        body, comm, out_shape=out_shapes, grid=grid, in_specs=in_specs, out_specs=out_specs,
        scratch_shapes=scratch, sem=sem, name=name, args=ins)
    return outs if comm is None else (outs, couts)


def _rms_fwd(h, g, name):
    lp, d = h.shape
    tm = _row_tile(lp)

    def body(h_ref, g_ref, n_ref):
        x = h_ref[...]
        r = lax.rsqrt(jnp.mean(x * x, axis=-1, keepdims=True) + EPS)
        n_ref[...] = (x * r * g_ref[...]).astype(BF16)

    return pl.pallas_call(
        body, out_shape=jax.ShapeDtypeStruct((lp, d), BF16), grid=(lp // tm,),
        in_specs=[pl.BlockSpec((tm, d), lambda i: (i, 0)), pl.BlockSpec((1, d), lambda i: (0, 0))],
        out_specs=pl.BlockSpec((tm, d), lambda i: (i, 0)),
        compiler_params=_cparams(("parallel",)), name=name)(h, g.reshape(1, d))


def _rms_bwd_math(dn, x, g):
    r = lax.rsqrt(jnp.mean(x * x, axis=-1, keepdims=True) + EPS)
    xh = x * r
    dxh = dn * g
    dx = r * (dxh - xh * jnp.mean(dxh * xh, axis=-1, keepdims=True))
    return dx, dn * xh


def _scale_cast(x, scale, name):
    lp, d = x.shape
    tm = _row_tile(lp)

    def body(x_ref, o_ref):
        o_ref[...] = (x_ref[...] * scale).astype(BF16)

    return pl.pallas_call(
        body, out_shape=jax.ShapeDtypeStruct((lp, d), BF16), grid=(lp // tm,),
        in_specs=[pl.BlockSpec((tm, d), lambda i: (i, 0))], out_specs=pl.BlockSpec((tm, d), lambda i: (i, 0)),
        compiler_params=_cparams(("parallel",)), name=name)(x)


def _ffn_up(h, gain, wg, wu, comm=None):
    lp, d = h.shape
    f4 = wg.shape[2]
    tm = _row_tile(lp)
    ni = lp // tm
    n = _rms_fwd(h, gain, "rms_fwd_ffn")

    def up_epi(accs, in_refs, out_refs):
        a, b = accs
        out_refs[0][...] = a.astype(BF16)
        out_refs[1][...] = b.astype(BF16)
        out_refs[2][...] = (a * _sigmoid(a) * b).astype(BF16)

    slab = jax.ShapeDtypeStruct((N_CHIPS, lp, f4), BF16)
    w_spec = pl.BlockSpec((None, d, f4), lambda j, i: (j, 0, 0))
    res = _matmul(
        "ffn_up", (N_CHIPS, ni), None, [n, wg, wu],
        [pl.BlockSpec((tm, d), lambda j, i: (i, 0)), w_spec, w_spec],
        [(0, 1, "nn", 0), (0, 2, "nn", 1)], [(tm, f4)] * 2, up_epi,
        [slab, slab, slab], [pl.BlockSpec((None, tm, f4), lambda j, i: (j, i, 0))] * 3,
        ("parallel", "parallel"), comm)
    return (tuple(res), []) if comm is None else (tuple(res[0]), res[1])


def _ffn_down(s, wd, h, comm=None):
    lp, d = h.shape
    f4 = wd.shape[1]
    tm = _row_tile(lp)

    def down_epi(accs, in_refs, out_refs):
        out_refs[0][...] = in_refs[2][...] + 0.5 * accs[0]

    res = _matmul(
        "ffn_down", (lp // tm, N_CHIPS), 1, [s, wd, h],
        [pl.BlockSpec((None, tm, f4), lambda i, j: (j, i, 0)),
         pl.BlockSpec((None, f4, d), lambda i, j: (j, 0, 0)),
         pl.BlockSpec((tm, d), lambda i, j: (i, 0))],
        [(0, 1, "nn", 0)], [(tm, d)], down_epi,
        [jax.ShapeDtypeStruct((lp, d), F32)], [pl.BlockSpec((tm, d), lambda i, j: (i, 0))],
        ("parallel", "arbitrary"), comm)
    return (res[0], []) if comm is None else (res[0][0], res[1])


def _tn_tiles(lp):
    return _row_tile(lp, 1408)


def _ffn_bwd(dh, h_in, gain, wg, wu, wd, saved, comm=None):
    a, b, s = saved
    lp, d = h_in.shape
    f4 = wg.shape[2]
    tm = _row_tile(lp)
    ni = lp // tm
    tk = _tn_tiles(lp)
    nk = lp // tk
    n = _rms_fwd(h_in, gain, "rms_fwd_ffn")
    dhs = _scale_cast(dh, 0.5, "ffn_dh_half")

    def ds_epi(accs, in_refs, out_refs):
        ds = accs[0]
        av = in_refs[2][...].astype(F32)
        bv = in_refs[3][...].astype(F32)
        sg = _sigmoid(av)
        out_refs[0][...] = (ds * bv * sg * (1.0 + av * (1.0 - sg))).astype(BF16)
        out_refs[1][...] = (ds * av * sg).astype(BF16)

    slab = jax.ShapeDtypeStruct((N_CHIPS, lp, f4), BF16)
    slab_spec = pl.BlockSpec((None, tm, f4), lambda j, i: (j, i, 0))
    res = _matmul(
        "ffn_bwd_ds", (N_CHIPS, ni), None, [dhs, wd, a, b],
        [pl.BlockSpec((tm, d), lambda j, i: (i, 0)), pl.BlockSpec((None, f4, d), lambda j, i: (j, 0, 0)),
         slab_spec, slab_spec],
        [(0, 1, "nt", 0)], [(tm, f4)], ds_epi, [slab, slab], [slab_spec, slab_spec], ("parallel", "parallel"),
        comm)
    (da, db), couts = (res, []) if comm is None else res

    def copy_epi(accs, in_refs, out_refs):
        for acc, o in zip(accs, out_refs):
            o[...] = acc

    (dwd,) = _matmul(
        "ffn_dwd", (N_CHIPS, nk), 1, [s, dhs],
        [pl.BlockSpec((None, tk, f4), lambda j, k: (j, k, 0)), pl.BlockSpec((tk, d), lambda j, k: (k, 0))],
        [(0, 1, "tn", 0)], [(f4, d)], copy_epi,
        [jax.ShapeDtypeStruct((N_CHIPS, f4, d), F32)], [pl.BlockSpec((None, f4, d), lambda j, k: (j, 0, 0))],
        ("parallel", "arbitrary"))

    dw_shape = jax.ShapeDtypeStruct((N_CHIPS, d, f4), F32)
    dw_spec = pl.BlockSpec((None, d, f4), lambda j, k: (j, 0, 0))
    in_slab = pl.BlockSpec((None, tk, f4), lambda j, k: (j, k, 0))
    dwg, dwu = _matmul(
        "ffn_dwgu", (N_CHIPS, nk), 1, [n, da, db],
        [pl.BlockSpec((tk, d), lambda j, k: (k, 0)), in_slab, in_slab],
        [(0, 1, "tn", 0), (0, 2, "tn", 1)], [(d, f4)] * 2, copy_epi,
        [dw_shape, dw_shape], [dw_spec, dw_spec], ("parallel", "arbitrary"))

    def dn_epi(accs, in_refs, out_refs):
        i, j = pl.program_id(0), pl.program_id(1)
        dx, dgrow = _rms_bwd_math(accs[0], in_refs[5][...], in_refs[6][...])
        out_refs[0][...] = in_refs[4][...] + dx

        @pl.when(i == 0)
        def _():
            out_refs[1][...] = jnp.zeros_like(out_refs[1])

        out_refs[1][...] += jnp.sum(dgrow, axis=0, keepdims=True)

    row_spec = pl.BlockSpec((tm, d), lambda i, j: (i, 0))
    in_slab2 = pl.BlockSpec((None, tm, f4), lambda i, j: (j, i, 0))
    w_spec = pl.BlockSpec((None, d, f4), lambda i, j: (j, 0, 0))
    dh_in, dgain = _matmul(
        "ffn_bwd_dn", (ni, N_CHIPS), 1, [da, wg, db, wu, dh, h_in, gain.reshape(1, d)],
        [in_slab2, w_spec, in_slab2, w_spec, row_spec, row_spec, pl.BlockSpec((1, d), lambda i, j: (0, 0))],
        [(0, 1, "nt", 0), (2, 3, "nt", 0)], [(tm, d)], dn_epi,
        [jax.ShapeDtypeStruct((lp, d), F32), jax.ShapeDtypeStruct((1, d), F32)],
        [row_spec, pl.BlockSpec((1, d), lambda i, j: (0, 0))], ("arbitrary", "arbitrary"))
    return dh_in, dgain, dwg, dwu, dwd, couts


def _rope_tables(lp):
    pos = jnp.arange(lp, dtype=F32) - float(PAD_FRONT)
    inv_freq = ROPE_THETA ** (-jnp.arange(0, ROT_DIM, 2, dtype=F32) / ROT_DIM)
    ang = pos[:, None] * inv_freq[None, :]
    cos, sin = jnp.cos(ang), jnp.sin(ang)
    half = ROT_DIM // 2
    ones = jnp.ones((lp, HEAD_DIM - ROT_DIM), F32)
    zeros_h = jnp.zeros((lp, half), F32)
    zeros_r = jnp.zeros((lp, HEAD_DIM - ROT_DIM), F32)
    c = jnp.concatenate([cos, cos, ones], axis=1)
    s1 = jnp.concatenate([-sin, zeros_h, zeros_r], axis=1)
    s2 = jnp.concatenate([zeros_h, sin, zeros_r], axis=1)
    reps = LANES // HEAD_DIM
    return jnp.stack([jnp.tile(c, (1, reps)), jnp.tile(s1, (1, reps)), jnp.tile(s2, (1, reps))])


def _rope(x, c, s1, s2):
    half = ROT_DIM // 2
    outs = []
    for ch in range(x.shape[1] // LANES):
        xc = x[:, ch * LANES:(ch + 1) * LANES]
        outs.append(xc * c + pltpu.roll(xc, LANES - half, 1) * s1 + pltpu.roll(xc, half, 1) * s2)
    return outs[0] if len(outs) == 1 else jnp.concatenate(outs, axis=1)


def _rope_t(dy, c, s1, s2):
    half = ROT_DIM // 2
    outs = []
    for ch in range(dy.shape[1] // LANES):
        dc = dy[:, ch * LANES:(ch + 1) * LANES]
        outs.append(dc * c + pltpu.roll(dc * s1, half, 1) + pltpu.roll(dc * s2, LANES - half, 1))
    return outs[0] if len(outs) == 1 else jnp.concatenate(outs, axis=1)


def _in_proj(n, w_in, tabs, ssm_w, comm=None):
    lp, d = n.shape
    inw = w_in.shape[1]
    tm = _row_tile(lp)
    o1 = ATTN_WIDTH
    o2 = o1 + KV_WIDTH
    o3 = o2 + KV_WIDTH
    o4 = o3 + ssm_w
    o5 = o4 + d

    def epi(accs, in_refs, out_refs):
        z = accs[0]
        c, s1, s2 = in_refs[2][0], in_refs[2][1], in_refs[2][2]
        out_refs[0][...] = _rope(z[:, :o1], c, s1, s2).astype(BF16)
        out_refs[1][...] = _rope(z[:, o1:o2], c, s1, s2).astype(BF16)
        out_refs[2][...] = z[:, o2:o3].astype(BF16)
        out_refs[3][...] = z[:, o3:o4]
        out_refs[4][...] = z[:, o4:o5]
        out_refs[5][...] = z[:, o5:]

    def rs(w, dt):
        return jax.ShapeDtypeStruct((lp, w), dt), pl.BlockSpec((tm, w), lambda i: (i, 0))

    shapes, specs = zip(rs(o1, BF16), rs(KV_WIDTH, BF16), rs(KV_WIDTH, BF16), rs(ssm_w, F32), rs(d, F32), rs(d, F32))
    res = _matmul(
        "mix_in_proj", (lp // tm,), None, [n, w_in, tabs],
        [pl.BlockSpec((tm, d), lambda i: (i, 0)), pl.BlockSpec((d, inw), lambda i: (0, 0)),
         pl.BlockSpec((3, tm, LANES), lambda i: (0, i, 0))],
        [(0, 1, "nn", 0)], [(tm, inw)], epi, list(shapes), list(specs), ("parallel",), comm)
    return (res, []) if comm is None else res


def _attn_mask(b):
    rows = lax.broadcasted_iota(jnp.int32, (BLOCK, 3 * BLOCK), 0)
    cols = lax.broadcasted_iota(jnp.int32, (BLOCK, 3 * BLOCK), 1)
    qpos = b * BLOCK + rows - PAD_FRONT
    kpos = (b - 1) * BLOCK + cols - PAD_FRONT
    dist = qpos - kpos
    band = (cols < 2 * BLOCK) & (kpos >= N_META) & (dist >= 0) & (dist < BLOCK)
    mrow = cols - 2 * BLOCK
    meta = (mrow >= PAD_FRONT) & ((mrow - PAD_FRONT) <= qpos)
    return band | meta


def _attn_probs(qh, kk, mask, sink):
    s = _dot(qh, kk, "nt") * (HEAD_DIM ** -0.5)
    s = jnp.where(mask, s, NEG_INF)
    m = jnp.maximum(jnp.max(s, axis=-1, keepdims=True), sink)
    e = jnp.exp(s - m)
    es = jnp.exp(sink - m)
    z = jnp.sum(e, axis=-1, keepdims=True) + es
    inv = 1.0 / z
    return e * inv, es * inv


def _head(ref_or_val, h):
    return ref_or_val[:, h * HEAD_DIM:(h + 1) * HEAD_DIM]


def _attn_fwd(q, k, v, sinks, comm=None):
    lp = q.shape[0]
    nb = lp // BLOCK

    def body(sink_ref, q_ref, kp_ref, kc_ref, km_ref, vp_ref, vc_ref, vm_ref, o_ref):
        b = pl.program_id(0)
        mask = _attn_mask(b)
        for hk in range(N_KV_HEADS):
            kk = jnp.concatenate([_head(kp_ref, hk), _head(kc_ref, hk), _head(km_ref, hk)], axis=0)
            vv = jnp.concatenate([_head(vp_ref, hk), _head(vc_ref, hk), _head(vm_ref, hk)], axis=0)
            for g in range(Q_PER_KV):
                h = hk * Q_PER_KV + g
                p, _ = _attn_probs(_head(q_ref, h), kk, mask, sink_ref[h])
                o_ref[:, h * HEAD_DIM:(h + 1) * HEAD_DIM] = _dot(p, vv, "nn").astype(BF16)

    cur = lambda b: (b, 0)
    prev = lambda b: (jnp.maximum(b - 1, 0), 0)
    first = lambda b: (0, 0)
    kvs = lambda f: pl.BlockSpec((BLOCK, KV_WIDTH), f)
    (o,), couts = _hosted_call(
        body, comm, out_shape=[jax.ShapeDtypeStruct((lp, ATTN_WIDTH), BF16)], grid=(nb,),
        in_specs=[pl.BlockSpec(memory_space=pltpu.SMEM), pl.BlockSpec((BLOCK, ATTN_WIDTH), cur),
                  kvs(prev), kvs(cur), kvs(first), kvs(prev), kvs(cur), kvs(first)],
        out_specs=[pl.BlockSpec((BLOCK, ATTN_WIDTH), cur)], scratch_shapes=[],
        sem=("parallel",), name="attn_fwd", args=(sinks, q, k, k, k, v, v, v))
    return o, couts


def _attn_bwd(q, k, v, do, sinks, tabs, comm=None):
    lp = q.shape[0]
    nb = lp // BLOCK
    scale = HEAD_DIM ** -0.5

    def body(sink_ref, q_ref, do_ref, kp_ref, kc_ref, km_ref, vp_ref, vc_ref, vm_ref, tq_ref, tk_ref, t0_ref,
             dq_ref, dk_ref, dv_ref, dkm_ref, dvm_ref, dsink_ref,
             dq_s, dkk_s, dvv_s, ck_s, cv_s, mk_s, mv_s):
        b = pl.program_id(0)

        @pl.when(b == 0)
        def _():
            for r in (ck_s, cv_s, mk_s, mv_s, dsink_ref):
                r[...] = jnp.zeros_like(r)

        @pl.when(b < nb)
        def _():
            mask = _attn_mask(b)
            for hk in range(N_KV_HEADS):
                kk = jnp.concatenate([_head(kp_ref, hk), _head(kc_ref, hk), _head(km_ref, hk)], axis=0)
                vv = jnp.concatenate([_head(vp_ref, hk), _head(vc_ref, hk), _head(vm_ref, hk)], axis=0)
                dkk = jnp.zeros((3 * BLOCK, HEAD_DIM), F32)
                dvv = jnp.zeros((3 * BLOCK, HEAD_DIM), F32)
                for g in range(Q_PER_KV):
                    h = hk * Q_PER_KV + g
                    qh = _head(q_ref, h)
                    doh = _head(do_ref, h)
                    p, ps = _attn_probs(qh, kk, mask, sink_ref[h])
                    dp = _dot(doh, vv, "nt")
                    delta = jnp.sum(p * dp, axis=-1, keepdims=True)
                    ds = (p * (dp - delta)).astype(BF16)
                    dsink_ref[h:h + 1, :] += jnp.zeros((1, LANES), F32) - jnp.sum(ps * delta)
                    dq_s[:, h * HEAD_DIM:(h + 1) * HEAD_DIM] = _dot(ds, kk, "nn") * scale
                    dkk = dkk + _dot(ds, qh, "tn") * scale
                    dvv = dvv + _dot(p, doh, "tn")
                dkk_s[:, hk * HEAD_DIM:(hk + 1) * HEAD_DIM] = dkk
                dvv_s[:, hk * HEAD_DIM:(hk + 1) * HEAD_DIM] = dvv
            dq_ref[...] = _rope_t(dq_s[...], tq_ref[0], tq_ref[1], tq_ref[2])
            dk_ref[...] = _rope_t(ck_s[...] + dkk_s[0:BLOCK, :], tk_ref[0], tk_ref[1], tk_ref[2])
            dv_ref[...] = cv_s[...] + dvv_s[0:BLOCK, :]
            ck_s[...] = dkk_s[BLOCK:2 * BLOCK, :]
            cv_s[...] = dvv_s[BLOCK:2 * BLOCK, :]
            mk_s[...] += dkk_s[2 * BLOCK:, :]
            mv_s[...] += dvv_s[2 * BLOCK:, :]

        @pl.when(b == nb)
        def _():
            dk_ref[...] = _rope_t(ck_s[...], tk_ref[0], tk_ref[1], tk_ref[2])
            dv_ref[...] = cv_s[...]
            dkm_ref[...] = _rope_t(mk_s[...], t0_ref[0], t0_ref[1], t0_ref[2])
            dvm_ref[...] = mv_s[...]

    cur = lambda b: (jnp.minimum(b, nb - 1), 0)
    prev = lambda b: (jnp.clip(b - 1, 0, nb - 1), 0)
    first = lambda b: (0, 0)
    kvs = lambda f: pl.BlockSpec((BLOCK, KV_WIDTH), f)
    tab = lambda f: pl.BlockSpec((3, BLOCK, LANES), lambda b: (0,) + f(b)[:1] + (0,))
    kv_out = lambda b: (jnp.maximum(b - 1, 0), 0)
    return _hosted_call(
        body, comm,
        out_shape=[jax.ShapeDtypeStruct((lp, ATTN_WIDTH), F32), jax.ShapeDtypeStruct((lp, KV_WIDTH), F32),
                   jax.ShapeDtypeStruct((lp, KV_WIDTH), F32), jax.ShapeDtypeStruct((BLOCK, KV_WIDTH), F32),
                   jax.ShapeDtypeStruct((BLOCK, KV_WIDTH), F32), jax.ShapeDtypeStruct((N_Q_HEADS, LANES), F32)],
        grid=(nb + 1,),
        in_specs=[pl.BlockSpec(memory_space=pltpu.SMEM), pl.BlockSpec((BLOCK, ATTN_WIDTH), cur),
                  pl.BlockSpec((BLOCK, ATTN_WIDTH), cur),
                  kvs(prev), kvs(cur), kvs(first), kvs(prev), kvs(cur), kvs(first),
                  tab(cur), tab(kv_out), tab(first)],
        out_specs=[pl.BlockSpec((BLOCK, ATTN_WIDTH), cur), kvs(kv_out), kvs(kv_out), kvs(first), kvs(first),
                   pl.BlockSpec((N_Q_HEADS, LANES), first)],
        scratch_shapes=[pltpu.VMEM((BLOCK, ATTN_WIDTH), F32), pltpu.VMEM((3 * BLOCK, KV_WIDTH), F32),
                        pltpu.VMEM((3 * BLOCK, KV_WIDTH), F32), pltpu.VMEM((BLOCK, KV_WIDTH), F32),
                        pltpu.VMEM((BLOCK, KV_WIDTH), F32), pltpu.VMEM((BLOCK, KV_WIDTH), F32),
                        pltpu.VMEM((BLOCK, KV_WIDTH), F32)],
        sem=("arbitrary",), name="attn_bwd", args=(sinks, q, do, k, k, k, v, v, v, tabs, tabs, tabs))


def _cmul(ar, ai, br, bi):
    return ar * br - ai * bi, ar * bi + ai * br


def _cpow(lr, li, n):
    rr = ri = None
    br, bi = lr, li
    while n:
        if n & 1:
            rr, ri = (br, bi) if rr is None else _cmul(rr, ri, br, bi)
        n >>= 1
        if n:
            br, bi = _cmul(br, bi, br, bi)
    return rr, ri


def _shift_rows(x, d, reverse):
    rows = lax.broadcasted_iota(jnp.int32, x.shape, 0)
    if not reverse:
        return jnp.where(rows >= d, pltpu.roll(x, d, 0), 0.0)
    return jnp.where(rows < SUBLANES - d, pltpu.roll(x, SUBLANES - d, 0), 0.0)


def _sublane_powers(mr, mi, reverse):
    rows = lax.broadcasted_iota(jnp.int32, mr.shape, 0)
    e = SUBLANES - 1 - rows if reverse else rows
    pr, pi = jnp.ones_like(mr), jnp.zeros_like(mr)
    br, bi = mr, mi
    for d in (1, 2, 4):
        tr, ti = _cmul(pr, pi, br, bi)
        on = (e & d) != 0
        pr, pi = jnp.where(on, tr, pr), jnp.where(on, ti, pi)
        if d < 4:
            br, bi = _cmul(br, bi, br, bi)
    return pr, pi


def _inclusive_prefix(er, ei, mr, mi, reverse):
    ir, ii, pr, pi = er, ei, mr, mi
    for d in (1, 2, 4):
        tr, ti = _cmul(pr, pi, _shift_rows(ir, d, reverse), _shift_rows(ii, d, reverse))
        ir, ii = ir + tr, ii + ti
        if d < 4:
            pr, pi = _cmul(pr, pi, pr, pi)
    return ir, ii


def _chain_rows(a, t, seg):
    return pl.ds(a * SUBLANES * seg + t, SUBLANES, stride=seg)


def _seg_scan(xr_ref, xi_ref, lam, seg, nchain, reverse, store, init, extra=None):
    nt = len(lam)
    acc0 = () if extra is None else extra[1]

    def step(i, carry):
        hs, acc = carry
        t = seg - 1 - i if reverse else i
        out = []
        for a in range(nchain):
            sl = _chain_rows(a, t, seg)
            for j in range(nt):
                lr, li = lam[j]
                k = 2 * (a * nt + j)
                hr, hi = hs[k], hs[k + 1]
                nr = lr * hr - li * hi + xr_ref[j, sl, :]
                ni = lr * hi + li * hr + xi_ref[j, sl, :]
                if store:
                    xr_ref[j, sl, :] = nr
                    xi_ref[j, sl, :] = ni
                if extra is not None:
                    acc = extra[0](t, a, j, nr, ni, acc)
                out += [nr, ni]
        return tuple(out), acc

    return lax.fori_loop(0, seg, step, (tuple(init), acc0))


def _ssm_scan(xr_ref, xi_ref, lam, seg, nchain, reverse, extra=None):
    nt = len(lam)
    zero = [jnp.zeros((SUBLANES, LANES), F32)] * (2 * nt * nchain)
    ends, _ = _seg_scan(xr_ref, xi_ref, lam, seg, nchain, reverse, False, zero)
    init = [None] * (2 * nt * nchain)
    last = 0 if reverse else SUBLANES - 1
    for j in range(nt):
        mr, mi = _cpow(lam[j][0], lam[j][1], seg)
        m8r, m8i = _cpow(mr, mi, SUBLANES)
        pwr, pwi = _sublane_powers(mr, mi, reverse)
        gr = gi = jnp.zeros((SUBLANES, LANES), F32)
        for a in (reversed(range(nchain)) if reverse else range(nchain)):
            k = 2 * (a * nt + j)
            incr, inci = _inclusive_prefix(ends[k], ends[k + 1], mr, mi, reverse)
            tr, ti = _cmul(pwr, pwi, gr, gi)
            init[k] = _shift_rows(incr, 1, reverse) + tr
            init[k + 1] = _shift_rows(inci, 1, reverse) + ti
            g2r, g2i = _cmul(m8r, m8i, gr, gi)
            gr = g2r + jnp.broadcast_to(incr[last:last + 1, :], gr.shape)
            gi = g2i + jnp.broadcast_to(inci[last:last + 1, :], gi.shape)
    _, acc = _seg_scan(xr_ref, xi_ref, lam, seg, nchain, reverse, True, init, extra)
    return acc


def _ssm_tables(bb_re, bb_im, c_re, c_im):
    g = bb_re.shape[0]
    nt = g // GROUPS_PER_TILE
    eye = jnp.eye(g, dtype=F32)

    def b_tab(bb):
        full = jnp.einsum('gpc,gh->gchp', bb, eye).reshape(g * SSM_GROUP, g * SSM_STATE)
        full = full.reshape(g * SSM_GROUP // LANES, LANES, nt, TILE_STATES)
        return jnp.stack([full[t // 2, :, t, :] for t in range(nt)])

    def c_tab(c):
        full = jnp.einsum('gcp,gh->gphc', c, eye).reshape(g * SSM_STATE, g * SSM_GROUP)
        full = full.reshape(nt, TILE_STATES, g * SSM_GROUP // LANES, LANES)
        return jnp.stack([full[t, :, t // 2, :] for t in range(nt)])

    return b_tab(bb_re), b_tab(bb_im), c_tab(c_re), c_tab(c_im)


def _ssm_untable_b(db, g):
    nt = g // GROUPS_PER_TILE
    per_blk = LANES // SSM_GROUP
    db = db.reshape(nt, GROUPS_PER_TILE, SSM_STATE, per_blk, SSM_GROUP)
    out = []
    for t in range(nt):
        for gl in range(GROUPS_PER_TILE):
            out.append(db[t, gl, :, GROUPS_PER_TILE * (t % 2) + gl, :])
    return jnp.stack(out)


def _ssm_untable_c(dc, g):
    nt = g // GROUPS_PER_TILE
    per_blk = LANES // SSM_GROUP
    dc = dc.reshape(nt, per_blk, SSM_GROUP, GROUPS_PER_TILE, SSM_STATE)
    out = []
    for t in range(nt):
        for gl in range(GROUPS_PER_TILE):
            out.append(dc[t, GROUPS_PER_TILE * (t % 2) + gl, :, gl, :])
    return jnp.stack(out)


def _lam_tiles(lam_ref):
    out = []
    for j in range(TILE_STATES // LANES):
        out.append(jnp.broadcast_to(lam_ref[:, j * LANES:(j + 1) * LANES], (SUBLANES, LANES)))
    return out


def _scan_chains(lp):
    for n in (4, 2, 1):
        if lp % (SUBLANES * n) == 0 and (lp // SUBLANES) % 16 == 0:
            return n
    raise ValueError(lp)


def _split_tiles(dst_ref, rows, val):
    for j in range(val.shape[1] // LANES):
        dst_ref[j, rows, :] = val[:, j * LANES:(j + 1) * LANES]


def _cat_tiles(src_ref, rows):
    njt = src_ref.shape[0]
    return jnp.concatenate([src_ref[j, rows, :] for j in range(njt)], axis=1).astype(BF16)


def _ssm_fwd(u, lam_re, lam_im, tb_re, tb_im, tc_re, tc_im, d_skip, comm=None):
    lp, w = u.shape
    nt = tb_re.shape[0]
    nchain = _scan_chains(lp)
    seg = lp // (SUBLANES * nchain)
    chunk = lp // SUBLANES
    njt = TILE_STATES // LANES

    def body(u_ref, lr_ref, li_ref, br_ref, bi_ref, cr_ref, ci_ref, d_ref, y_ref, xr, xi):
        t = pl.program_id(0)
        for s in range(SUBLANES):
            rs = pl.ds(s * chunk, chunk)
            ub = u_ref[rs, :].astype(BF16)
            _split_tiles(xr, rs, _dot(ub, br_ref[...], "nn"))
            _split_tiles(xi, rs, _dot(ub, bi_ref[...], "nn"))
        lrs, lis = _lam_tiles(lr_ref), _lam_tiles(li_ref)
        _ssm_scan(xr, xi, list(zip(lrs, lis)), seg, nchain, False)
        for s in range(SUBLANES):
            rs = pl.ds(s * chunk, chunk)
            y = _dot(_cat_tiles(xr, rs), cr_ref[...], "nn") - _dot(_cat_tiles(xi, rs), ci_ref[...], "nn")

            @pl.when(t % 2 == 0)
            def _():
                y_ref[rs, :] = y + d_ref[...] * u_ref[rs, :]

            @pl.when(t % 2 == 1)
            def _():
                y_ref[rs, :] += y

    blk = pl.BlockSpec((lp, LANES), lambda t: (0, t // 2))
    lam_spec = pl.BlockSpec((None, 1, TILE_STATES), lambda t: (t, 0, 0))
    b_spec = pl.BlockSpec((None, LANES, TILE_STATES), lambda t: (t, 0, 0))
    c_spec = pl.BlockSpec((None, TILE_STATES, LANES), lambda t: (t, 0, 0))
    (y,), couts = _hosted_call(
        body, comm, out_shape=[jax.ShapeDtypeStruct((lp, w), F32)], grid=(nt,),
        in_specs=[blk, lam_spec, lam_spec, b_spec, b_spec, c_spec, c_spec,
                  pl.BlockSpec((1, LANES), lambda t: (0, t // 2))],
        out_specs=[blk],
        scratch_shapes=[pltpu.VMEM((njt, lp, LANES), F32), pltpu.VMEM((njt, lp, LANES), F32)],
        sem=("arbitrary",), name="ssm_fwd",
        args=(u, lam_re, lam_im, tb_re, tb_im, tc_re, tc_im, d_skip.reshape(1, w)))
    return y, couts


def _ssm_bwd(u, dy, lam_re, lam_im, tb_re, tb_im, tc_re, tc_im, d_skip, comm=None):
    lp, w = u.shape
    nt = tb_re.shape[0]
    nchain = _scan_chains(lp)
    seg = lp // (SUBLANES * nchain)
    chunk = lp // SUBLANES
    njt = TILE_STATES // LANES
    tbt_re, tbt_im = jnp.swapaxes(tb_re, 1, 2), jnp.swapaxes(tb_im, 1, 2)
    tct_re, tct_im = jnp.swapaxes(tc_re, 1, 2), jnp.swapaxes(tc_im, 1, 2)

    def body(u_ref, dy_ref, lr_ref, li_ref, br_ref, bi_ref, btr_ref, bti_ref, ctr_ref, cti_ref, d_ref,
             du_ref, dlr_ref, dli_ref, dbr_ref, dbi_ref, dcr_ref, dci_ref, dd_ref, hr, hi, ar, ai):
        t = pl.program_id(0)
        lrs, lis = _lam_tiles(lr_ref), _lam_tiles(li_ref)
        for s in range(SUBLANES):
            rs = pl.ds(s * chunk, chunk)
            ub = u_ref[rs, :].astype(BF16)
            dyb = dy_ref[rs, :].astype(BF16)
            _split_tiles(hr, rs, _dot(ub, br_ref[...], "nn"))
            _split_tiles(hi, rs, _dot(ub, bi_ref[...], "nn"))
            _split_tiles(ar, rs, _dot(dyb, ctr_ref[...], "nn"))
            _split_tiles(ai, rs, -_dot(dyb, cti_ref[...], "nn"))
        _ssm_scan(hr, hi, list(zip(lrs, lis)), seg, nchain, False)

        def dlam_step(tt, a, j, a_r, a_i, acc):
            sl = _chain_rows(a, jnp.maximum(tt - 1, 0), seg)
            p_r, p_i = hr[j, sl, :], hi[j, sl, :]
            acc = list(acc)
            acc[2 * j] = acc[2 * j] + jnp.where(tt > 0, a_r * p_r + a_i * p_i, 0.0)
            acc[2 * j + 1] = acc[2 * j + 1] + jnp.where(tt > 0, a_i * p_r - a_r * p_i, 0.0)
            return tuple(acc)

        zero = tuple([jnp.zeros((SUBLANES, LANES), F32)] * (2 * njt))
        conj = [(lr, -li) for lr, li in zip(lrs, lis)]
        acc = list(_ssm_scan(ar, ai, conj, seg, nchain, True, (dlam_step, zero)))
        row0 = lax.broadcasted_iota(jnp.int32, (SUBLANES, LANES), 0) == 0
        for j in range(njt):
            cs = slice(j * LANES, (j + 1) * LANES)
            for a in range(nchain):
                p_r = _shift_rows(hr[j, _chain_rows(a, seg - 1, seg), :], 1, False)
                p_i = _shift_rows(hi[j, _chain_rows(a, seg - 1, seg), :], 1, False)
                if a > 0:
                    before = pl.ds(a * SUBLANES * seg - 1, 1)
                    p_r = jnp.where(row0, jnp.broadcast_to(hr[j, before, :], p_r.shape), p_r)
                    p_i = jnp.where(row0, jnp.broadcast_to(hi[j, before, :], p_i.shape), p_i)
                a_r, a_i = ar[j, _chain_rows(a, 0, seg), :], ai[j, _chain_rows(a, 0, seg), :]
                acc[2 * j] = acc[2 * j] + a_r * p_r + a_i * p_i
                acc[2 * j + 1] = acc[2 * j + 1] + a_i * p_r - a_r * p_i
            dlr_ref[:, cs] = jnp.sum(acc[2 * j], axis=0, keepdims=True)
            dli_ref[:, cs] = jnp.sum(acc[2 * j + 1], axis=0, keepdims=True)

        dd = jnp.zeros((1, LANES), F32)
        for s in range(SUBLANES):
            rs = pl.ds(s * chunk, chunk)
            ub = u_ref[rs, :].astype(BF16)
            dyv = dy_ref[rs, :]
            dyb = dyv.astype(BF16)
            arb, aib = _cat_tiles(ar, rs), _cat_tiles(ai, rs)
            hrb, hib = _cat_tiles(hr, rs), _cat_tiles(hi, rs)
            du = _dot(arb, btr_ref[...], "nn") + _dot(aib, bti_ref[...], "nn")
            upd = [(dbr_ref, _dot(arb, ub, "tn")), (dbi_ref, _dot(aib, ub, "tn")),
                   (dcr_ref, _dot(dyb, hrb, "tn")), (dci_ref, -_dot(dyb, hib, "tn"))]
            for ref, val in upd:
                if s == 0:
                    ref[...] = val
                else:
                    ref[...] += val
            rows = lax.broadcasted_iota(jnp.int32, (chunk, LANES), 0) + s * chunk
            keep = rows >= PAD_FRONT
            dd = dd + jnp.sum(dyv * u_ref[rs, :], axis=0, keepdims=True)

            @pl.when(t % 2 == 0)
            def _():
                du_ref[rs, :] = jnp.where(keep, du + d_ref[...] * dyv, 0.0)

            @pl.when(t % 2 == 1)
            def _():
                du_ref[rs, :] += jnp.where(keep, du, 0.0)

        @pl.when(t % 2 == 0)
        def _():
            dd_ref[...] = dd

    blk = pl.BlockSpec((lp, LANES), lambda t: (0, t // 2))
    vec = pl.BlockSpec((1, LANES), lambda t: (0, t // 2))
    lam_spec = pl.BlockSpec((None, 1, TILE_STATES), lambda t: (t, 0, 0))
    b_spec = pl.BlockSpec((None, LANES, TILE_STATES), lambda t: (t, 0, 0))
    c_spec = pl.BlockSpec((None, TILE_STATES, LANES), lambda t: (t, 0, 0))
    lam_shape = jax.ShapeDtypeStruct((nt, 1, TILE_STATES), F32)
    bt_shape = jax.ShapeDtypeStruct((nt, TILE_STATES, LANES), F32)
    ct_shape = jax.ShapeDtypeStruct((nt, LANES, TILE_STATES), F32)
    st = pltpu.VMEM((njt, lp, LANES), F32)
    return _hosted_call(
        body, comm,
        out_shape=[jax.ShapeDtypeStruct((lp, w), F32), lam_shape, lam_shape, bt_shape, bt_shape, ct_shape, ct_shape,
                   jax.ShapeDtypeStruct((1, w), F32)],
        grid=(nt,),
        in_specs=[blk, blk, lam_spec, lam_spec, b_spec, b_spec, c_spec, c_spec, b_spec, b_spec, vec],
        out_specs=[blk, lam_spec, lam_spec, c_spec, c_spec, b_spec, b_spec, vec],
        scratch_shapes=[st, st, st, st], sem=("arbitrary",), name="ssm_bwd",
        args=(u, dy, lam_re, lam_im, tb_re, tb_im, tbt_re, tbt_im, tct_re, tct_im, d_skip.reshape(1, w)))


def _ssm_params(a_re, a_im, log_dt, b_re, b_im):
    dt = jnp.exp(log_dt)[:, None]
    mag = jnp.exp(a_re * dt)
    lb_re = mag * jnp.cos(a_im * dt)
    lb_im = mag * jnp.sin(a_im * dt)
    den = a_re * a_re + a_im * a_im
    num_re = lb_re - 1.0
    coef_re = (num_re * a_re + lb_im * a_im) / den
    coef_im = (lb_im * a_re - num_re * a_im) / den
    bb_re = coef_re[..., None] * b_re - coef_im[..., None] * b_im
    bb_im = coef_re[..., None] * b_im + coef_im[..., None] * b_re
    return lb_re, lb_im, bb_re, bb_im


def _gelu_fwd(y):
    lp, w = y.shape
    tm = _row_tile(lp)

    def body(y_ref, o_ref):
        o_ref[...] = _gelu(y_ref[...]).astype(BF16)

    return pl.pallas_call(
        body, out_shape=jax.ShapeDtypeStruct((lp, w), BF16), grid=(lp // tm,),
        in_specs=[pl.BlockSpec((tm, w), lambda i: (i, 0))], out_specs=pl.BlockSpec((tm, w), lambda i: (i, 0)),
        compiler_params=_cparams(("parallel",)), name="gelu_fwd")(y)


def _merge_fwd(o, yg, ga, gs, w3, comm=None):
    lp, d = ga.shape
    d4 = w3.shape[3]
    kw = w3.shape[2]
    tm = _row_tile(lp)

    def epi(accs, in_refs, out_refs):
        attn, vv, gg = accs
        out_refs[0][...] = (_sigmoid(in_refs[5][...]) * attn
                            + _sigmoid(in_refs[6][...]) * (vv * _sigmoid(gg))).astype(BF16)

    wspec = lambda which: pl.BlockSpec((None, None, kw, d4), lambda j, i: (j, which, 0, 0))
    colspec = pl.BlockSpec((tm, d4), lambda j, i: (i, j))
    aspec = pl.BlockSpec((tm, kw), lambda j, i: (i, 0))
    res = _matmul(
        "merge_fwd", (N_CHIPS, lp // tm), None, [o, yg, w3, w3, w3, ga, gs],
        [aspec, aspec, wspec(0), wspec(1), wspec(2), colspec, colspec],
        [(0, 2, "nn", 0), (1, 3, "nn", 1), (1, 4, "nn", 2)], [(tm, d4)] * 3, epi,
        [jax.ShapeDtypeStruct((lp, d), BF16)], [colspec], ("parallel", "parallel"), comm)
    return (res[0], []) if comm is None else (res[0][0], res[1])


def _out_proj(merged, w_out, h, comm=None):
    lp, d = h.shape
    d4 = w_out.shape[1]
    tm = _row_tile(lp)

    def epi(accs, in_refs, out_refs):
        out_refs[0][...] = in_refs[2][...] + accs[0]

    res = _matmul(
        "mix_out_proj", (lp // tm, N_CHIPS), 1, [merged, w_out, h],
        [pl.BlockSpec((tm, d4), lambda i, j: (i, j)), pl.BlockSpec((None, d4, d), lambda i, j: (j, 0, 0)),
         pl.BlockSpec((tm, d), lambda i, j: (i, 0))],
        [(0, 1, "nn", 0)], [(tm, d)], epi,
        [jax.ShapeDtypeStruct((lp, d), F32)], [pl.BlockSpec((tm, d), lambda i, j: (i, 0))],
        ("parallel", "arbitrary"), comm)
    return (res[0], []) if comm is None else (res[0][0], res[1])


def _merge_bwd(dhb, w_out, o, yg, ga, gs, w3):
    lp, d = ga.shape
    d4 = w3.shape[3]
    kw = w3.shape[2]
    tm = _row_tile(lp)

    def epi(accs, in_refs, out_refs):
        dm, attn, vv, gg = accs
        sa = _sigmoid(in_refs[7][...])
        ss = _sigmoid(in_refs[8][...])
        sg = _sigmoid(gg)
        ssm = vv * sg
        dssm = dm * ss
        out_refs[0][...] = (dm * sa).astype(BF16)
        out_refs[1][...] = (dssm * sg).astype(BF16)
        out_refs[2][...] = (dssm * vv * sg * (1.0 - sg)).astype(BF16)
        out_refs[3][...] = (dm * attn * sa * (1.0 - sa)).astype(BF16)
        out_refs[4][...] = (dm * ssm * ss * (1.0 - ss)).astype(BF16)

    wspec = lambda which: pl.BlockSpec((None, None, kw, d4), lambda j, i: (j, which, 0, 0))
    colspec = pl.BlockSpec((tm, d4), lambda j, i: (i, j))
    aspec = pl.BlockSpec((tm, kw), lambda j, i: (i, 0))
    shp = jax.ShapeDtypeStruct((lp, d), BF16)
    return _matmul(
        "merge_bwd", (N_CHIPS, lp // tm), None, [dhb, w_out, o, yg, w3, w3, w3, ga, gs],
        [pl.BlockSpec((tm, d), lambda j, i: (i, 0)), pl.BlockSpec((None, d4, d), lambda j, i: (j, 0, 0)),
         aspec, aspec, wspec(0), wspec(1), wspec(2), colspec, colspec],
        [(0, 1, "nt", 0), (2, 4, "nn", 1), (3, 5, "nn", 2), (3, 6, "nn", 3)], [(tm, d4)] * 4, epi,
        [shp] * 5, [colspec] * 5, ("parallel", "parallel"))


def _branch_bwd(dattn, dv, dg, w3, y):
    lp, d = dattn.shape
    d4 = w3.shape[3]
    kw = w3.shape[2]
    tm = _row_tile(lp)

    def epi(accs, in_refs, out_refs):
        out_refs[0][...] = accs[0].astype(BF16)
        out_refs[1][...] = accs[1] * _gelu_grad(in_refs[6][...])

    wspec = lambda which: pl.BlockSpec((None, None, kw, d4), lambda i, j: (j, which, 0, 0))
    colspec = pl.BlockSpec((tm, d4), lambda i, j: (i, j))
    rowspec = pl.BlockSpec((tm, kw), lambda i, j: (i, 0))
    return _matmul(
        "branch_bwd", (lp // tm, N_CHIPS), 1, [dattn, dv, dg, w3, w3, w3, y],
        [colspec, colspec, colspec, wspec(0), wspec(1), wspec(2), rowspec],
        [(0, 3, "nt", 0), (1, 4, "nt", 1), (2, 5, "nt", 1)], [(tm, kw)] * 2, epi,
        [jax.ShapeDtypeStruct((lp, kw), BF16), jax.ShapeDtypeStruct((lp, kw), F32)], [rowspec, rowspec],
        ("parallel", "arbitrary"))


def _tn_cols(x, ys, name):
    lp, kx = x.shape
    n = ys[0].shape[1]
    n4 = n // N_CHIPS
    tk = _tn_tiles(lp)

    def epi(accs, in_refs, out_refs):
        for acc, o in zip(accs, out_refs):
            o[...] = acc

    shp = jax.ShapeDtypeStruct((N_CHIPS, kx, n4), F32)
    return _matmul(
        name, (N_CHIPS, lp // tk), 1, [x] + list(ys),
        [pl.BlockSpec((tk, kx), lambda j, k: (k, 0))] + [pl.BlockSpec((tk, n4), lambda j, k: (k, j))] * len(ys),
        [(0, 1 + i, "tn", i) for i in range(len(ys))], [(kx, n4)] * len(ys), epi,
        [shp] * len(ys), [pl.BlockSpec((None, kx, n4), lambda j, k: (j, 0, 0))] * len(ys),
        ("parallel", "arbitrary"))


def _tn_full(x, y, name, tn_cols=None):
    lp, kx = x.shape
    n = y.shape[1]
    tk = _tn_tiles(lp)
    tn = n if tn_cols is None else tn_cols

    def epi(accs, in_refs, out_refs):
        out_refs[0][...] = accs[0]

    (out,) = _matmul(
        name, (n // tn, lp // tk), 1, [x, y],
        [pl.BlockSpec((tk, kx), lambda j, k: (k, 0)), pl.BlockSpec((tk, tn), lambda j, k: (k, j))],
        [(0, 1, "tn", 0)], [(kx, tn)], epi,
        [jax.ShapeDtypeStruct((kx, n), F32)], [pl.BlockSpec((kx, tn), lambda j, k: (0, j))],
        ("parallel", "arbitrary"))
    return out


def _in_proj_bwd(dz, w_in, dh, h_in, gain):
    lp, d = h_in.shape
    inw = w_in.shape[1]
    tm = _row_tile(lp)

    def epi(accs, in_refs, out_refs):
        i = pl.program_id(0)
        dx, dgrow = _rms_bwd_math(accs[0], in_refs[3][...], in_refs[4][...])
        out_refs[0][...] = in_refs[2][...] + dx

        @pl.when(i == 0)
        def _():
            out_refs[1][...] = jnp.zeros_like(out_refs[1])

        out_refs[1][...] += jnp.sum(dgrow, axis=0, keepdims=True)

    row = pl.BlockSpec((tm, d), lambda i: (i, 0))
    return _matmul(
        "mix_in_proj_bwd", (lp // tm,), None, [dz, w_in, dh, h_in, gain.reshape(1, d)],
        [pl.BlockSpec((tm, inw), lambda i: (i, 0)), pl.BlockSpec((d, inw), lambda i: (0, 0)), row, row,
         pl.BlockSpec((1, d), lambda i: (0, 0))],
        [(0, 1, "nt", 0)], [(tm, d)], epi,
        [jax.ShapeDtypeStruct((lp, d), F32), jax.ShapeDtypeStruct((1, d), F32)],
        [row, pl.BlockSpec((1, d), lambda i: (0, 0))], ("arbitrary",))


def _loss_head(h, gain, target):
    lp, d = h.shape
    nb = lp // BLOCK

    def body(h_ref, g_ref, t_ref, dh_ref, dg_ref, loss_ref):
        i = pl.program_id(0)

        @pl.when(i == 0)
        def _():
            dg_ref[...] = jnp.zeros_like(dg_ref)
            loss_ref[...] = jnp.zeros_like(loss_ref)
            dh_ref[...] = jnp.zeros_like(dh_ref)

        @pl.when(i > 0)
        def _():
            x = h_ref[...]
            g = g_ref[...]
            r = lax.rsqrt(jnp.mean(x * x, axis=-1, keepdims=True) + EPS)
            err = x * r * g - t_ref[...]
            loss_ref[...] += jnp.zeros_like(loss_ref) + 0.5 * jnp.sum(jnp.sum(err * err, axis=-1, keepdims=True)) / d
            dx, dgrow = _rms_bwd_math(err * (1.0 / d), x, g)
            dh_ref[...] = dx
            dg_ref[...] += jnp.sum(dgrow, axis=0, keepdims=True)

    row = pl.BlockSpec((BLOCK, d), lambda i: (i, 0))
    one = pl.BlockSpec((1, d), lambda i: (0, 0))
    return pl.pallas_call(
        body,
        out_shape=[jax.ShapeDtypeStruct((lp, d), F32), jax.ShapeDtypeStruct((1, d), F32),
                   jax.ShapeDtypeStruct((SUBLANES, LANES), F32)],
        grid=(nb,),
        in_specs=[row, one, pl.BlockSpec((BLOCK, d), lambda i: (jnp.maximum(i - 1, 0), 0))],
        out_specs=[row, one, pl.BlockSpec((SUBLANES, LANES), lambda i: (0, 0))],
        compiler_params=_cparams(("arbitrary",)), name="loss_head")(h, gain.reshape(1, d), target)


def _adam_math(w, g, m, v):
    m = ADAM_B1 * m + (1.0 - ADAM_B1) * g
    v = ADAM_B2 * v + (1.0 - ADAM_B2) * (g * g)
    m_hat = m / (1.0 - ADAM_B1 ** ADAM_STEP)
    v_hat = v / (1.0 - ADAM_B2 ** ADAM_STEP)
    delta = -ADAM_LR * (m_hat / (jnp.sqrt(v_hat) + ADAM_EPS) + ADAM_WD * w)
    return delta, m, v


def _adamw_layers(w, m, v, mine, other, pos, name):
    depth, r, c = w.shape
    half = r // 2
    tr = _div_tile(half, c * 4)
    nh = half // tr

    def body(*refs):
        pos_ref, w_ref, m_ref, v_ref = refs[:4]
        mine_refs = refs[4:4 + depth]
        other_refs = refs[4 + depth:4 + 2 * depth]
        g_out, d_out, m_out, v_out = refs[4 + 2 * depth:]
        layer, i = pl.program_id(0), pl.program_id(1)
        is_mine = (i // nh) == pos_ref[0]

        def update(g):
            delta, nm, nv = _adam_math(w_ref[...], g, m_ref[...], v_ref[...])
            g_out[...] = g
            d_out[...] = delta
            m_out[...] = nm
            v_out[...] = nv

        for l in range(depth):
            @pl.when((layer == l) & is_mine)
            def _(l=l):
                update(mine_refs[l][...])

            @pl.when((layer == l) & jnp.logical_not(is_mine))
            def _(l=l):
                update(other_refs[l][...])

    stacked = pl.BlockSpec((None, tr, c), lambda l, i, p: (l, i, 0))

    def gspec(layer, is_other):
        def imap(l, i, p):
            first = jnp.where(is_other, 1 - p[0], p[0]) * nh
            here = jnp.clip(i - first, 0, nh - 1)
            return (jnp.where(l == layer, here, jnp.where(l < layer, 0, nh - 1)), 0)
        return pl.BlockSpec((tr, c), imap)

    shp = jax.ShapeDtypeStruct((depth, r, c), F32)
    grid_spec = pltpu.PrefetchScalarGridSpec(
        num_scalar_prefetch=1, grid=(depth, 2 * nh),
        in_specs=[stacked] * 3 + [gspec(l, 0) for l in range(depth)] + [gspec(l, 1) for l in range(depth)],
        out_specs=[stacked] * 4)
    return pl.pallas_call(
        body, out_shape=[shp] * 4, grid_spec=grid_spec,
        compiler_params=_cparams(("arbitrary", "arbitrary")), name=name)(pos, w, m, v, *mine, *other)


def _adamw_flat(w, g, m, v, name):
    r, c = w.shape
    tr = _div_tile(r, c * 4)

    def body(w_ref, g_ref, m_ref, v_ref, d_out, m_out, v_out):
        delta, nm, nv = _adam_math(w_ref[...], g_ref[...], m_ref[...], v_ref[...])
        d_out[...] = delta
        m_out[...] = nm
        v_out[...] = nv

    spec = pl.BlockSpec((tr, c), lambda i: (i, 0))
    shp = jax.ShapeDtypeStruct((r, c), F32)
    return pl.pallas_call(
        body, out_shape=[shp] * 3, grid=(r // tr,), in_specs=[spec] * 4, out_specs=[spec] * 3,
        compiler_params=_cparams(("parallel",)), name=name)(w, g, m, v)


def _mesh_pos():
    return lax.axis_index("x"), lax.axis_index("y"), lax.axis_index("c")


def _row_half(ref, which, lead):
    half = ref.shape[lead] // 2
    idx = (slice(None),) * lead + (pl.ds(which * half, half), slice(None))
    return ref.at[idx]


def _gather_comm(arrs, tag):
    n = len(arrs)

    def ctx(ins, outs, sems):
        send_sems, recv_sems, local_sems = sems
        x, y, c = _mesh_pos()
        chips = [(1 - x, y), (x, 1 - y), (1 - x, 1 - y)]

        def slot(k, chip, which):
            lead = len(ins[k].shape) - 2
            return _row_half(outs[k].at[2 * chip[0] + chip[1]], which, lead)

        def copy(k, j, src, dst, to):
            return pltpu.make_async_remote_copy(
                src_ref=src, dst_ref=dst, send_sem=send_sems.at[6 * k + j], recv_sem=recv_sems.at[6 * k + j],
                device_id=to, device_id_type=MESH)

        def local(k):
            return pltpu.make_async_copy(ins[k], outs[k].at[2 * x + y], local_sems.at[k])

        def first(k, j):
            lead = len(ins[k].shape) - 2
            return copy(k, j, _row_half(ins[k], c, lead), slot(k, (x, y), c), (*chips[j], c))

        def passed(k, j, which):
            return copy(k, 3 + j, slot(k, chips[j], which), slot(k, chips[j], which), (x, y, 1 - c))

        def landed(k, j):
            return copy(k, j, slot(k, chips[j], c), slot(k, chips[j], c), (x, y, 1 - c))

        return c, local, first, passed, landed

    def start(ins, outs, sems):
        c, local, first, passed, landed = ctx(ins, outs, sems)
        for k in range(n):
            local(k).start()
            for j in range(3):
                first(k, j).start()

    def mid(ins, outs, sems):
        c, local, first, passed, landed = ctx(ins, outs, sems)
        for j in range(3):
            for k in range(n):
                landed(k, j).wait_recv()
                passed(k, j, c).start()

    def finish(ins, outs, sems):
        c, local, first, passed, landed = ctx(ins, outs, sems)
        for j in range(3):
            for k in range(n):
                passed(k, j, 1 - c).wait_recv()
        for k in range(n):
            for j in range(3):
                first(k, j).wait_send()
                passed(k, j, c).wait_send()
            local(k).wait()

    return _Comm(
        tag, arrs, [jax.ShapeDtypeStruct((N_CHIPS,) + a.shape, a.dtype) for a in arrs],
        [pltpu.SemaphoreType.DMA((6 * n,)), pltpu.SemaphoreType.DMA((6 * n,)), pltpu.SemaphoreType.DMA((n,))],
        start, mid, finish)


def _run_comm(comm, name):
    n_in, n_out = len(comm.ins), len(comm.out_shapes)

    def body(*refs):
        ins, outs, sems = refs[:n_in], refs[n_in:n_in + n_out], refs[n_in + n_out:]
        comm.start(ins, outs, sems)
        if comm.mid is not None:
            comm.mid(ins, outs, sems)
        comm.finish(ins, outs, sems)

    return pl.pallas_call(
        body, out_shape=comm.out_shapes, in_specs=[HBM_SPEC] * n_in, out_specs=[HBM_SPEC] * n_out,
        scratch_shapes=comm.sems, name=name)(*comm.ins)


def _all_gather_chips(arrs, name):
    return _run_comm(_gather_comm(arrs, "gather"), name)


GATHER_US_PER_BYTE = 380.0 / 11.65e6
HOST_US = dict(ffn_up=78.0, ffn_down=65.0, in_proj=38.0, attn_fwd=103.0, ssm_fwd=67.0, merge_fwd=50.0,
               out_proj=45.0)
HOST_SLACK_US = 10.0


class _WeightStream:
    def __init__(self, pieces):
        self.keys = [k for k, _ in pieces]
        self.shards = dict(pieces)
        self.next = 0
        self.full = {}
        self.pending = []

    def comm_for(self, host):
        budget = HOST_US[host] + HOST_SLACK_US
        taken, cost = [], 0.0
        while self.next < len(self.keys):
            key = self.keys[self.next]
            c = self.shards[key].size * self.shards[key].dtype.itemsize * GATHER_US_PER_BYTE
            if cost + c > budget:
                break
            taken.append(key)
            cost += c
            self.next += 1
        self.pending = taken
        if not taken:
            return None
        return _gather_comm([self.shards[k] for k in taken], "g_" + "_".join(k[1] for k in taken))

    def deposit(self, gathered):
        for key, arr in zip(self.pending, gathered):
            self.full[key] = arr
        self.pending = []

    def get(self, key):
        if key not in self.full:
            upto = self.keys.index(key) + 1
            keys = self.keys[self.next:upto]
            self.next = upto
            for k, arr in zip(keys, _all_gather_chips([self.shards[k] for k in keys], "gather_now")):
                self.full[k] = arr
        return self.full[key]


def _all_gather_devices(x_shard, name):
    m_per, ncol = x_shard.shape

    def body(x_ref, out_ref, send_sems, recv_sems, local_sem):
        x, y, c = _mesh_pos()
        me, sibling = (x, y, c), (x, y, 1 - c)
        chips = [(1 - x, y), (x, 1 - y), (1 - x, 1 - y)]

        def rows(px, py, pc):
            return out_ref.at[4 * px + 2 * py + pc]

        def copy(k, block, to, src=None):
            return pltpu.make_async_remote_copy(
                src_ref=rows(*block) if src is None else src, dst_ref=rows(*block),
                send_sem=send_sems.at[k], recv_sem=recv_sems.at[k], device_id=to, device_id_type=MESH)

        mine = pltpu.make_async_copy(x_ref, rows(*me), local_sem)
        mine.start()
        first = [copy(0, me, sibling, src=x_ref)]
        first += [copy(1 + j, me, (*chip, c), src=x_ref) for j, chip in enumerate(chips)]
        for cp in first:
            cp.start()
        passed = [copy(4 + j, (*chip, c), sibling) for j, chip in enumerate(chips)]
        for j, chip in enumerate(chips):
            copy(1 + j, (*chip, c), me).wait_recv()
            passed[j].start()
        copy(0, sibling, me).wait_recv()
        for j, chip in enumerate(chips):
            copy(4 + j, (*chip, 1 - c), me).wait_recv()
        for cp in first + passed:
            cp.wait_send()
        mine.wait()

    return pl.pallas_call(
        body, out_shape=jax.ShapeDtypeStruct((8, m_per, ncol), x_shard.dtype),
        in_specs=[pl.BlockSpec(memory_space=pltpu.VMEM)], out_specs=pl.BlockSpec(memory_space=pltpu.VMEM),
        scratch_shapes=[pltpu.SemaphoreType.DMA((7,)), pltpu.SemaphoreType.DMA((7,)), pltpu.SemaphoreType.DMA],
        compiler_params=pltpu.CompilerParams(vmem_limit_bytes=VMEM_LIMIT), name=name)(x_shard)


def _sum_devices(g8, name):
    _, r, c = g8.shape
    tr = _div_tile(r, c * 4 * 8)

    def body(g_ref, o_ref):
        acc = g_ref[0]
        for dev in range(1, 8):
            acc = acc + g_ref[dev]
        o_ref[...] = acc

    return pl.pallas_call(
        body, out_shape=jax.ShapeDtypeStruct((r, c), F32), grid=(r // tr,),
        in_specs=[pl.BlockSpec((8, tr, c), lambda i: (0, i, 0))], out_specs=pl.BlockSpec((tr, c), lambda i: (i, 0)),
        compiler_params=_cparams(("parallel",)), name=name)(g8)


def _exchange_sibling_halves(arrs, name):
    n = len(arrs)

    def body(*refs):
        ins, outs = refs[:n], refs[n:2 * n]
        send_sems, recv_sems = refs[2 * n:]
        x, y, c = _mesh_pos()
        cps = []
        for k in range(n):
            cp = pltpu.make_async_remote_copy(
                src_ref=_row_half(ins[k], 1 - c, 1), dst_ref=outs[k], send_sem=send_sems.at[k],
                recv_sem=recv_sems.at[k], device_id=(x, y, 1 - c), device_id_type=MESH)
            cp.start()
            cps.append(cp)
        for cp in cps:
            cp.wait()

    return pl.pallas_call(
        body,
        out_shape=[jax.ShapeDtypeStruct((a.shape[0], a.shape[1] // 2, a.shape[2]), a.dtype) for a in arrs],
        in_specs=[HBM_SPEC] * n, out_specs=[HBM_SPEC] * n,
        scratch_shapes=[pltpu.SemaphoreType.DMA((n,)), pltpu.SemaphoreType.DMA((n,))], name=name)(*arrs)


def _chip_partials(arrs, recvs, pos, name):
    n = len(arrs)

    def body(pos_ref, *refs):
        for a_ref, b_ref, o_ref in zip(refs[:n], refs[n:2 * n], refs[2 * n:]):
            o_ref[...] = (a_ref[...] + b_ref[...]).astype(BF16)

    own_specs, recv_specs, shapes = [], [], []
    for arr in arrs:
        nslab, r, c = arr.shape
        own_specs.append(pl.BlockSpec((None, r // 2, c), lambda j, p: (j, p[0], 0)))
        recv_specs.append(pl.BlockSpec((None, r // 2, c), lambda j, p: (j, 0, 0)))
        shapes.append(jax.ShapeDtypeStruct((nslab, r // 2, c), BF16))
    grid_spec = pltpu.PrefetchScalarGridSpec(
        num_scalar_prefetch=1, grid=(N_CHIPS,), in_specs=own_specs + recv_specs, out_specs=recv_specs)
    return pl.pallas_call(
        body, out_shape=shapes, grid_spec=grid_spec,
        compiler_params=_cparams(("parallel",)), name=name)(pos, *arrs, *recvs)


def _chip_exchange_comm(parts, tag):
    n = len(parts)

    def copies(ins, outs, sems):
        send_sems, recv_sems = sems
        x, y, c = _mesh_pos()
        chips = [(1 - x, y), (x, 1 - y), (1 - x, 1 - y)]
        return [pltpu.make_async_remote_copy(
            src_ref=ins[k].at[2 * chip[0] + chip[1]], dst_ref=outs[k].at[j],
            send_sem=send_sems.at[3 * k + j], recv_sem=recv_sems.at[3 * k + j],
            device_id=(*chip, c), device_id_type=MESH) for k in range(n) for j, chip in enumerate(chips)]

    def start(ins, outs, sems):
        for cp in copies(ins, outs, sems):
            cp.start()

    def finish(ins, outs, sems):
        for cp in copies(ins, outs, sems):
            cp.wait()

    return _Comm(
        tag, parts, [jax.ShapeDtypeStruct((3,) + p.shape[1:], p.dtype) for p in parts],
        [pltpu.SemaphoreType.DMA((3 * n,)), pltpu.SemaphoreType.DMA((3 * n,))], start, None, finish)


def _reduce_halves(arrs, recvs, gots, pos, name):
    n = len(arrs)

    def body(pos_ref, *refs):
        for a_ref, b_ref, g_ref, o_ref in zip(refs[:n], refs[n:2 * n], refs[2 * n:3 * n], refs[3 * n:]):
            acc = a_ref[...] + b_ref[...]
            for j in range(3):
                acc = acc + g_ref[j].astype(F32)
            o_ref[...] = acc

    own_specs, recv_specs, got_specs, out_specs, shapes = [], [], [], [], []
    for arr in arrs:
        _, r, c = arr.shape
        own_specs.append(pl.BlockSpec((None, r // 2, c), lambda i, p: (p[1], p[0], 0)))
        recv_specs.append(pl.BlockSpec((None, r // 2, c), lambda i, p: (p[1], 0, 0)))
        got_specs.append(pl.BlockSpec((3, r // 2, c), lambda i, p: (0, 0, 0)))
        out_specs.append(pl.BlockSpec((r // 2, c), lambda i, p: (0, 0)))
        shapes.append(jax.ShapeDtypeStruct((r // 2, c), F32))
    grid_spec = pltpu.PrefetchScalarGridSpec(
        num_scalar_prefetch=1, grid=(1,), in_specs=own_specs + recv_specs + got_specs, out_specs=out_specs)
    return pl.pallas_call(
        body, out_shape=shapes, grid_spec=grid_spec,
        compiler_params=_cparams(("arbitrary",)), name=name)(pos, *arrs, *recvs, *gots)


def _share_halves(halves, name):
    n = len(halves)

    def body(*refs):
        ins, outs = refs[:n], refs[n:2 * n]
        send_sems, recv_sems = refs[2 * n:]
        x, y, c = _mesh_pos()
        cps = []
        for k in range(n):
            cp = pltpu.make_async_remote_copy(
                src_ref=ins[k], dst_ref=outs[k], send_sem=send_sems.at[k], recv_sem=recv_sems.at[k],
                device_id=(x, y, 1 - c), device_id_type=MESH)
            cp.start()
            cps.append(cp)
        for cp in cps:
            cp.wait()

    return pl.pallas_call(
        body, out_shape=[jax.ShapeDtypeStruct(h.shape, h.dtype) for h in halves],
        in_specs=[HBM_SPEC] * n, out_specs=[HBM_SPEC] * n,
        scratch_shapes=[pltpu.SemaphoreType.DMA((n,)), pltpu.SemaphoreType.DMA((n,))], name=name)(*halves)


class _Reduction:
    def __init__(self, arrs, pos, tag):
        self.arrs, self.pos, self.tag = arrs, pos, tag
        self.recv = _exchange_sibling_halves(arrs, "rs_sibling_" + tag)
        self.parts = _chip_partials(arrs, self.recv, pos, "rs_partial_" + tag)
        self.got = None

    def comm(self):
        return _chip_exchange_comm(self.parts, "rs_" + self.tag)

    def end(self):
        if self.got is None:
            self.got = _run_comm(self.comm(), "rs_chips_" + self.tag)
        halves = _reduce_halves(self.arrs, self.recv, self.got, self.pos, "rs_reduce_" + self.tag)
        return halves, _share_halves(halves, "rs_share_" + self.tag)


def _w_in_full(p, l, ws):
    if "w_in" not in p:
        slabs = ws.get((l, "w_in"))
        p["w_in"] = jnp.concatenate([slabs[j] for j in range(N_CHIPS)], axis=1)
    return p["w_in"]


def _layer_fwd(h, l, p, ws, tabs):
    def hosted(host, fn, *args):
        out, got = fn(*args, ws.comm_for(host))
        ws.deposit(got)
        return out

    a, b, sact = hosted("ffn_up", _ffn_up, h, p["ffn1_norm"], ws.get((l, "wg1")), ws.get((l, "wu1")))
    h1 = hosted("ffn_down", _ffn_down, sact, ws.get((l, "wd1")), h)
    ffn1_saved = (a, b, sact)
    n = _rms_fwd(h1, p["mix_norm"], "rms_fwd_mix")
    ssm_w = p["ssm_d"].shape[0]
    q, k, v, u, ga, gs = hosted("in_proj", _in_proj, n, _w_in_full(p, l, ws), tabs, ssm_w)
    o = hosted("attn_fwd", _attn_fwd, q, k, v, p["attn_sinks"])
    y = hosted("ssm_fwd", _ssm_fwd, u, *p["ssm_tabs"], p["ssm_d"])
    yg = _gelu_fwd(y)
    merged = hosted("merge_fwd", _merge_fwd, o, yg, ga, gs, ws.get((l, "w3")))
    h2 = hosted("out_proj", _out_proj, merged, ws.get((l, "w_out")), h1)
    a, b, sact = hosted("ffn_up", _ffn_up, h2, p["ffn2_norm"], ws.get((l, "wg2")), ws.get((l, "wu2")))
    h3 = hosted("ffn_down", _ffn_down, sact, ws.get((l, "wd2")), h2)
    saved = dict(h0=h, h1=h1, h2=h2, ffn1=ffn1_saved, ffn2=(a, b, sact), q=q, k=k, v=v, u=u, ga=ga, gs=gs, o=o, y=y,
                 yg=yg, merged=merged)
    return h3, saved


def _layer_bwd(dh, l, p, ws, s, tabs, pos, carried):
    g = {}
    dh2, g["ffn2_norm"], dwg2, dwu2, dwd2, _ = _ffn_bwd(
        dh, s["h2"], p["ffn2_norm"], ws.get((l, "wg2")), ws.get((l, "wu2")), ws.get((l, "wd2")), s["ffn2"])
    w3, w_out_w = ws.get((l, "w3")), ws.get((l, "w_out"))
    red_ffn2 = _Reduction([dwg2, dwu2, dwd2], pos, "ffn")
    lp, d = dh2.shape
    d4 = d // N_CHIPS
    dhb = _scale_cast(dh2, 1.0, "mix_dh_cast")
    dw_out = _tn_full(s["merged"], dhb, "mix_dw_out").reshape(N_CHIPS, d4, d)
    dattn, dv, dg, dga, dgs = _merge_bwd(dhb, w_out_w, s["o"], s["yg"], s["ga"], s["gs"], w3)
    (dw_ap,) = _tn_cols(s["o"], [dattn], "mix_dw_ap")
    dw_gv, dw_gg = _tn_cols(s["yg"], [dv, dg], "mix_dw_glu")
    do, dy = _branch_bwd(dattn, dv, dg, w3, s["y"])
    (dq, dk, dvv, dkm, dvm, dsink), red_ffn2.got = _attn_bwd(
        s["q"], s["k"], s["v"], do, p["attn_sinks"], tabs, red_ffn2.comm())
    g["attn_sinks"] = dsink[:, 0]
    (du, dlr, dli, dbr, dbi, dcr, dci, dd), got = _ssm_bwd(
        s["u"], dy, *p["ssm_tabs"], p["ssm_d"], None if carried is None else carried.comm())
    if carried is not None:
        carried.got = got
    ngrp = p["ssm_d"].shape[0] // SSM_GROUP
    g["ssm_lam"] = (dlr.reshape(ngrp, SSM_STATE), dli.reshape(ngrp, SSM_STATE),
                    _ssm_untable_b(dbr, ngrp), _ssm_untable_b(dbi, ngrp))
    g["ssm_c_re"] = _ssm_untable_c(dcr, ngrp)
    g["ssm_c_im"] = _ssm_untable_c(dci, ngrp)
    g["ssm_d"] = dd[0]
    dk = dk.at[:BLOCK].add(dkm)
    dvv = dvv.at[:BLOCK].add(dvm)
    dz = jnp.concatenate([dq.astype(BF16), dk.astype(BF16), dvv.astype(BF16), du.astype(BF16), dga, dgs], axis=1)
    n = _rms_fwd(s["h1"], p["mix_norm"], "rms_fwd_mix")
    w_in = _w_in_full(p, l, ws)
    inw = w_in.shape[1]
    tn_cols = inw // 2 if (inw // 2) % LANES == 0 else None
    dw_in = _tn_full(n, dz, "mix_dw_in", tn_cols)
    inw4 = inw // N_CHIPS
    dw_in = jnp.stack([dw_in[:, j * inw4:(j + 1) * inw4] for j in range(N_CHIPS)])
    red_mix = _Reduction([dw_in, dw_ap, dw_gv, dw_gg, dw_out], pos, "mix")
    dh1, g["mix_norm"] = _in_proj_bwd(dz, w_in, dh2, s["h1"], p["mix_norm"])
    dh0, g["ffn1_norm"], dwg1, dwu1, dwd1, red_mix.got = _ffn_bwd(
        dh1, s["h0"], p["ffn1_norm"], ws.get((l, "wg1")), ws.get((l, "wu1")), ws.get((l, "wd1")), s["ffn1"],
        red_mix.comm())
    red_ffn1 = _Reduction([dwg1, dwu1, dwd1], pos, "ffn")
    return dh0, g, [red_ffn1, red_mix, red_ffn2]


BIG = ["ffn1_w_gate", "ffn1_w_up", "ffn1_w_down", "w_in", "w_attn_proj", "w_glu_v", "w_glu_g", "w_out",
       "ffn2_w_gate", "ffn2_w_up", "ffn2_w_down"]
SMALL = ["ffn1_norm", "mix_norm", "attn_sinks", "ssm_a_re", "ssm_a_im", "ssm_log_dt", "ssm_b_re", "ssm_b_im",
         "ssm_c_re", "ssm_c_im", "ssm_d", "ffn2_norm", "final_norm"]
WEIGHTS = ["meta_tokens", "ffn1_norm", "ffn1_w_gate", "ffn1_w_up", "ffn1_w_down", "mix_norm", "w_in", "attn_sinks",
           "ssm_a_re", "ssm_a_im", "ssm_log_dt", "ssm_b_re", "ssm_b_im", "ssm_c_re", "ssm_c_im", "ssm_d",
           "w_attn_proj", "w_glu_v", "w_glu_g", "w_out", "ffn2_norm", "ffn2_w_gate", "ffn2_w_up", "ffn2_w_down",
           "final_norm"]


def _pack_small(tree):
    flat = jnp.concatenate([tree[k].reshape(-1) for k in SMALL + ["meta_tokens"]])
    rows = -(-flat.shape[0] // (LANES * LANES)) * LANES
    return jnp.pad(flat, (0, rows * LANES - flat.shape[0])).reshape(rows, LANES)


def _unpack_small(packed, like):
    flat = packed.reshape(-1)
    out, off = {}, 0
    for k in SMALL + ["meta_tokens"]:
        size = math.prod(like[k].shape)
        out[k] = flat[off:off + size].reshape(like[k].shape)
        off += size
    return out


def kernel(x, meta_tokens, ffn1_norm, ffn1_w_gate, ffn1_w_up, ffn1_w_down, mix_norm, w_in, attn_sinks, ssm_a_re, ssm_a_im, ssm_log_dt, ssm_b_re, ssm_b_im, ssm_c_re, ssm_c_im, ssm_d, w_attn_proj, w_glu_v, w_glu_g, w_out, ffn2_norm, ffn2_w_gate, ffn2_w_up, ffn2_w_down, final_norm, loss_target, m_meta_tokens, m_ffn1_norm, m_ffn1_w_gate, m_ffn1_w_up, m_ffn1_w_down, m_mix_norm, m_w_in, m_attn_sinks, m_ssm_a_re, m_ssm_a_im, m_ssm_log_dt, m_ssm_b_re, m_ssm_b_im, m_ssm_c_re, m_ssm_c_im, m_ssm_d, m_w_attn_proj, m_w_glu_v, m_w_glu_g, m_w_out, m_ffn2_norm, m_ffn2_w_gate, m_ffn2_w_up, m_ffn2_w_down, m_final_norm, v_meta_tokens, v_ffn1_norm, v_ffn1_w_gate, v_ffn1_w_up, v_ffn1_w_down, v_mix_norm, v_w_in, v_attn_sinks, v_ssm_a_re, v_ssm_a_im, v_ssm_log_dt, v_ssm_b_re, v_ssm_b_im, v_ssm_c_re, v_ssm_c_im, v_ssm_d, v_w_attn_proj, v_w_glu_v, v_w_glu_g, v_w_out, v_ffn2_norm, v_ffn2_w_gate, v_ffn2_w_up, v_ffn2_w_down, v_final_norm):
    args = dict(locals())
    w = {k: args[k] for k in WEIGHTS}
    m = {k: args["m_" + k] for k in WEIGHTS}
    v = {k: args["v_" + k] for k in WEIGHTS}
    depth = ffn1_norm.shape[0]
    seq, d = x.shape[1], x.shape[2]
    lp = seq + BLOCK
    xi, yi, ci = _mesh_pos()
    pos = jnp.stack([ci, 2 * xi + yi]).astype(jnp.int32)

    tabs = _rope_tables(lp)
    (meta_all,) = _all_gather_chips([meta_tokens], "gather_meta")
    meta_full = jnp.concatenate([meta_all[j] for j in range(N_CHIPS)], axis=1)
    layers, pieces = [], []
    for l in range(depth):
        pieces += [
            ((l, "wg1"), ffn1_w_gate[l].astype(BF16)), ((l, "wu1"), ffn1_w_up[l].astype(BF16)),
            ((l, "wd1"), ffn1_w_down[l].astype(BF16)), ((l, "w_in"), w_in[l].astype(BF16)),
            ((l, "w3"), jnp.stack([w_attn_proj[l], w_glu_v[l], w_glu_g[l]]).astype(BF16)),
            ((l, "w_out"), w_out[l].astype(BF16)),
            ((l, "wg2"), ffn2_w_gate[l].astype(BF16)), ((l, "wu2"), ffn2_w_up[l].astype(BF16)),
            ((l, "wd2"), ffn2_w_down[l].astype(BF16))]
        lb_re, lb_im, bb_re, bb_im = _ssm_params(ssm_a_re[l], ssm_a_im[l], ssm_log_dt[l], ssm_b_re[l], ssm_b_im[l])
        ngrp = lb_re.shape[0]
        nt = ngrp // GROUPS_PER_TILE
        ssm_tabs = (lb_re.reshape(nt, 1, TILE_STATES), lb_im.reshape(nt, 1, TILE_STATES),
                    *_ssm_tables(bb_re, bb_im, ssm_c_re[l], ssm_c_im[l]))
        layers.append(dict(
            ffn1_norm=ffn1_norm[l], mix_norm=mix_norm[l], ffn2_norm=ffn2_norm[l], attn_sinks=attn_sinks[l],
            ssm_d=ssm_d[l], ssm_tabs=ssm_tabs))
    ws = _WeightStream(pieces)
    ws.get((0, "wu1"))

    h = jnp.concatenate([jnp.zeros((PAD_FRONT, d), F32), meta_full, x[0]], axis=0)
    saved = []
    for l in range(depth):
        h, s = _layer_fwd(h, l, layers[l], ws, tabs)
        saved.append(s)
    dh, g_final, loss_acc = _loss_head(h, final_norm, loss_target[0])
    loss = lax.psum(loss_acc[0, 0], ("x", "y", "c"))

    grads, reds = [None] * depth, [None] * depth
    carried = None
    for l in reversed(range(depth)):
        dh, grads[l], reds[l] = _layer_bwd(dh, l, layers[l], ws, saved[l], tabs, pos, carried)
        carried = reds[l][0]
    grad_x = dh[BLOCK:][None]
    dmeta_local = dh[PAD_FRONT:BLOCK]

    small = {k: [] for k in SMALL}
    for l in range(depth):
        gl = grads[l]
        _, vjp = jax.vjp(_ssm_params, ssm_a_re[l], ssm_a_im[l], ssm_log_dt[l], ssm_b_re[l], ssm_b_im[l])
        da_re, da_im, dlog_dt, db_re, db_im = vjp(gl["ssm_lam"])
        for k, val in (("ffn1_norm", gl["ffn1_norm"][0]), ("mix_norm", gl["mix_norm"][0]),
                       ("attn_sinks", gl["attn_sinks"]), ("ssm_a_re", da_re), ("ssm_a_im", da_im),
                       ("ssm_log_dt", dlog_dt), ("ssm_b_re", db_re), ("ssm_b_im", db_im),
                       ("ssm_c_re", gl["ssm_c_re"]), ("ssm_c_im", gl["ssm_c_im"]), ("ssm_d", gl["ssm_d"]),
                       ("ffn2_norm", gl["ffn2_norm"][0])):
            small[k].append(val)
    small_local = {k: jnp.stack(vals) for k, vals in small.items() if k != "final_norm"}
    small_local["final_norm"] = g_final[0]
    small_local["meta_tokens"] = dmeta_local
    like = dict(small_local)
    g_small = _sum_devices(_all_gather_devices(_pack_small(small_local), "gather_small_grads"), "sum_small_grads")
    g_small_tree = _unpack_small(g_small, like)
    d4 = d // N_CHIPS
    chip = 2 * xi + yi
    g_meta = lax.dynamic_slice_in_dim(g_small_tree["meta_tokens"], chip * d4, d4, axis=1)

    reduced = []
    for l in range(depth):
        mine, other = [], []
        for red in reds[l]:
            halves, sibling_halves = red.end()
            mine += halves
            other += sibling_halves
        reduced.append((mine, other))

    g_out, delta, new_m, new_v = {}, {}, {}, {}
    for i, k in enumerate(BIG):
        g_out[k], delta[k], new_m[k], new_v[k] = _adamw_layers(
            w[k], m[k], v[k], [reduced[l][0][i] for l in range(depth)], [reduced[l][1][i] for l in range(depth)],
            pos, "adamw_" + k)
    small_names = SMALL + ["meta_tokens"]
    w_small = {k: w[k] for k in small_names}
    m_small = {k: m[k] for k in small_names}
    v_small = {k: v[k] for k in small_names}
    g_small_local = dict(g_small_tree)
    g_small_local["meta_tokens"] = g_meta
    d_s, m_s, v_s = _adamw_flat(_pack_small(w_small), _pack_small(g_small_local), _pack_small(m_small),
                                _pack_small(v_small), "adamw_small")
    for tree, packed in ((delta, d_s), (new_m, m_s), (new_v, v_s)):
        tree.update(_unpack_small(packed, w_small))
    for k in small_names:
        g_out[k] = g_small_local[k]

    return (loss, grad_x, *[g_out[k] for k in WEIGHTS], *[delta[k] for k in WEIGHTS],
            *[new_m[k] for k in WEIGHTS], *[new_v[k] for k in WEIGHTS])
```

```python
import functools
import math

import jax
import jax.numpy as jnp
from jax import lax
from jax.experimental import pallas as pl
from jax.experimental.pallas import tpu as pltpu

F32 = jnp.float32
BF16 = jnp.bfloat16

N_META = 16
HEAD_DIM = 64
N_Q_HEADS = 8
N_KV_HEADS = 2
Q_PER_KV = N_Q_HEADS // N_KV_HEADS
ATTN_WIDTH = N_Q_HEADS * HEAD_DIM
KV_WIDTH = N_KV_HEADS * HEAD_DIM
BLOCK = 128
PAD_FRONT = BLOCK - N_META
ROPE_THETA = 500000.0
ROT_DIM = HEAD_DIM // 4
SSM_GROUP = 16
SSM_STATE = 64
GROUPS_PER_TILE = 4
TILE_STATES = GROUPS_PER_TILE * SSM_STATE
LANES = 128
SUBLANES = 8
EPS = 1e-6
NEG_INF = -1e30
N_CHIPS = 4

ADAM_LR = 0.001
ADAM_B1 = 0.9
ADAM_B2 = 0.999
ADAM_EPS = 1e-08
ADAM_WD = 0.01
ADAM_STEP = 10

VMEM_LIMIT = 56 * 1024 * 1024
MESH = pl.DeviceIdType.MESH


def _cparams(sem=None):
    return pltpu.CompilerParams(dimension_semantics=sem, vmem_limit_bytes=VMEM_LIMIT)


def _row_tile(rows, limit=512):
    best = None
    for t in range(128, limit + 1, 128):
        if rows % t == 0:
            best = t
    assert best is not None, rows
    return best


def _div_tile(rows, row_bytes, max_bytes=1 << 20, mult=8):
    best = None
    for t in range(mult, rows + 1, mult):
        if rows % t == 0 and t * row_bytes <= max_bytes:
            best = t
    if best is None:
        best = rows
    return best


def _dot(a, b, mode):
    if mode == "nn":
        dims = (((1,), (0,)), ((), ()))
    elif mode == "nt":
        dims = (((1,), (1,)), ((), ()))
    else:
        dims = (((0,), (0,)), ((), ()))
    return lax.dot_general(a.astype(BF16), b.astype(BF16), dims, preferred_element_type=F32)


def _sigmoid(x):
    return 1.0 / (1.0 + jnp.exp(-x))


_GELU_C = math.sqrt(2.0 / math.pi)


def _gelu(x):
    return 0.5 * x * (1.0 + jnp.tanh(_GELU_C * (x + 0.044715 * x * x * x)))


def _gelu_grad(x):
    t = jnp.tanh(_GELU_C * (x + 0.044715 * x * x * x))
    return 0.5 * (1.0 + t) + 0.5 * x * (1.0 - t * t) * _GELU_C * (1.0 + 3.0 * 0.044715 * x * x)


class _Comm:
    def __init__(self, tag, ins, out_shapes, sems, start, mid, finish):
        self.tag, self.ins, self.out_shapes, self.sems = tag, list(ins), list(out_shapes), list(sems)
        self.start, self.mid, self.finish = start, mid, finish


HBM_SPEC = pl.BlockSpec(memory_space=pltpu.HBM)


def _hosted_call(body, comm, *, out_shape, grid, in_specs, out_specs, scratch_shapes, sem, name, args):
    out_shape, in_specs, out_specs = list(out_shape), list(in_specs), list(out_specs)
    scratch_shapes = list(scratch_shapes)
    if comm is None:
        res = pl.pallas_call(
            body, out_shape=out_shape, grid=grid, in_specs=in_specs, out_specs=out_specs,
            scratch_shapes=scratch_shapes, compiler_params=_cparams(sem), name=name)(*args)
        return list(res), []
    n_in, n_out, n_sc = len(args), len(out_shape), len(scratch_shapes)
    nci, nco = len(comm.ins), len(comm.out_shapes)
    total = math.prod(grid)

    def wrapped(*refs):
        in_refs, cin = refs[:n_in], refs[n_in:n_in + nci]
        o0 = n_in + nci
        out_refs, cout = refs[o0:o0 + n_out], refs[o0 + n_out:o0 + n_out + nco]
        s0 = o0 + n_out + nco
        sc, csem = refs[s0:s0 + n_sc], refs[s0 + n_sc:]
        lin = 0
        for dim, size in enumerate(grid):
            lin = lin * size + pl.program_id(dim)

        @pl.when(lin == 0)
        def _():
            comm.start(cin, cout, csem)

        if comm.mid is not None:
            @pl.when(lin == total // 2)
            def _():
                comm.mid(cin, cout, csem)

        body(*in_refs, *out_refs, *sc)

        @pl.when(lin == total - 1)
        def _():
            comm.finish(cin, cout, csem)

    res = pl.pallas_call(
        wrapped, out_shape=out_shape + comm.out_shapes, grid=grid,
        in_specs=in_specs + [HBM_SPEC] * nci, out_specs=out_specs + [HBM_SPEC] * nco,
        scratch_shapes=scratch_shapes + comm.sems,
        compiler_params=_cparams(("arbitrary",) * len(grid)), name=name + "_" + comm.tag)(*args, *comm.ins)
    return list(res[:n_out]), list(res[n_out:])


def _matmul(name, grid, k_axis, ins, in_specs, pairs, acc_shapes, epilogue, out_shapes, out_specs, sem, comm=None):
    n_in, n_out, n_acc = len(ins), len(out_shapes), len(acc_shapes)

    def body(*refs):
        in_refs = refs[:n_in]
        out_refs = refs[n_in:n_in + n_out]
        acc_refs = refs[n_in + n_out:]
        if k_axis is None:
            accs = [None] * n_acc
            for ia, ib, mode, iacc in pairs:
                d = _dot(in_refs[ia][...], in_refs[ib][...], mode)
                accs[iacc] = d if accs[iacc] is None else accs[iacc] + d
            epilogue(accs, in_refs, out_refs)
            return
        k = pl.program_id(k_axis)

        @pl.when(k == 0)
        def _():
            for r in acc_refs:
                r[...] = jnp.zeros_like(r)

        for ia, ib, mode, iacc in pairs:
            acc_refs[iacc][...] += _dot(in_refs[ia][...], in_refs[ib][...], mode)

        @pl.when(k == pl.num_programs(k_axis) - 1)
        def _():
            epilogue([r[...] for r in acc_refs], in_refs, out_refs)

    scratch = [] if k_axis is None else [pltpu.VMEM(s, F32) for s in acc_shapes]
    outs, couts = _hosted_call(
        body, comm, out_shape=out_shapes, grid=grid, in_specs=in_specs, out_specs=out_specs,
        scratch_shapes=scratch, sem=sem, name=name, args=ins)
    return outs if comm is None else (outs, couts)


def _rms_fwd(h, g, name):
    lp, d = h.shape
    tm = _row_tile(lp)

    def body(h_ref, g_ref, n_ref):
        x = h_ref[...]
        r = lax.rsqrt(jnp.mean(x * x, axis=-1, keepdims=True) + EPS)
        n_ref[...] = (x * r * g_ref[...]).astype(BF16)

    return pl.pallas_call(
        body, out_shape=jax.ShapeDtypeStruct((lp, d), BF16), grid=(lp // tm,),
        in_specs=[pl.BlockSpec((tm, d), lambda i: (i, 0)), pl.BlockSpec((1, d), lambda i: (0, 0))],
        out_specs=pl.BlockSpec((tm, d), lambda i: (i, 0)),
        compiler_params=_cparams(("parallel",)), name=name)(h, g.reshape(1, d))


def _rms_bwd_math(dn, x, g):
    r = lax.rsqrt(jnp.mean(x * x, axis=-1, keepdims=True) + EPS)
    xh = x * r
    dxh = dn * g
    dx = r * (dxh - xh * jnp.mean(dxh * xh, axis=-1, keepdims=True))
    return dx, dn * xh


def _scale_cast(x, scale, name):
    lp, d = x.shape
    tm = _row_tile(lp)

    def body(x_ref, o_ref):
        o_ref[...] = (x_ref[...] * scale).astype(BF16)

    return pl.pallas_call(
        body, out_shape=jax.ShapeDtypeStruct((lp, d), BF16), grid=(lp // tm,),
        in_specs=[pl.BlockSpec((tm, d), lambda i: (i, 0))], out_specs=pl.BlockSpec((tm, d), lambda i: (i, 0)),
        compiler_params=_cparams(("parallel",)), name=name)(x)


def _ffn_up(h, gain, wg, wu, comm=None):
    lp, d = h.shape
    f4 = wg.shape[2]
    tm = _row_tile(lp)
    ni = lp // tm
    n = _rms_fwd(h, gain, "rms_fwd_ffn")

    def up_epi(accs, in_refs, out_refs):
        a, b = accs
        out_refs[0][...] = a.astype(BF16)
        out_refs[1][...] = b.astype(BF16)
        out_refs[2][...] = (a * _sigmoid(a) * b).astype(BF16)

    slab = jax.ShapeDtypeStruct((N_CHIPS, lp, f4), BF16)
    w_spec = pl.BlockSpec((None, d, f4), lambda j, i: (j, 0, 0))
    res = _matmul(
        "ffn_up", (N_CHIPS, ni), None, [n, wg, wu],
        [pl.BlockSpec((tm, d), lambda j, i: (i, 0)), w_spec, w_spec],
        [(0, 1, "nn", 0), (0, 2, "nn", 1)], [(tm, f4)] * 2, up_epi,
        [slab, slab, slab], [pl.BlockSpec((None, tm, f4), lambda j, i: (j, i, 0))] * 3,
        ("parallel", "parallel"), comm)
    return (tuple(res), []) if comm is None else (tuple(res[0]), res[1])


def _ffn_down(s, wd, h, comm=None):
    lp, d = h.shape
    f4 = wd.shape[1]
    tm = _row_tile(lp)

    def down_epi(accs, in_refs, out_refs):
        out_refs[0][...] = in_refs[2][...] + 0.5 * accs[0]

    res = _matmul(
        "ffn_down", (lp // tm, N_CHIPS), 1, [s, wd, h],
        [pl.BlockSpec((None, tm, f4), lambda i, j: (j, i, 0)),
         pl.BlockSpec((None, f4, d), lambda i, j: (j, 0, 0)),
         pl.BlockSpec((tm, d), lambda i, j: (i, 0))],
        [(0, 1, "nn", 0)], [(tm, d)], down_epi,
        [jax.ShapeDtypeStruct((lp, d), F32)], [pl.BlockSpec((tm, d), lambda i, j: (i, 0))],
        ("parallel", "arbitrary"), comm)
    return (res[0], []) if comm is None else (res[0][0], res[1])


def _tn_tiles(lp):
    return _row_tile(lp, 1408)


def _ffn_bwd(dh, h_in, gain, wg, wu, wd, saved, pos, comm=None):
    a, b, s = saved
    lp, d = h_in.shape
    f4 = wg.shape[2]
    tm = _row_tile(lp)
    ni = lp // tm
    tk = _tn_tiles(lp)
    nk = lp // tk
    n = _rms_fwd(h_in, gain, "rms_fwd_ffn")
    dhs = _scale_cast(dh, 0.5, "ffn_dh_half")

    def ds_epi(accs, in_refs, out_refs):
        ds = accs[0]
        av = in_refs[2][...].astype(F32)
        bv = in_refs[3][...].astype(F32)
        sg = _sigmoid(av)
        out_refs[0][...] = (ds * bv * sg * (1.0 + av * (1.0 - sg))).astype(BF16)
        out_refs[1][...] = (ds * av * sg).astype(BF16)

    slab = jax.ShapeDtypeStruct((N_CHIPS, lp, f4), BF16)
    slab_spec = pl.BlockSpec((None, tm, f4), lambda j, i: (j, i, 0))
    res = _matmul(
        "ffn_bwd_ds", (N_CHIPS, ni), None, [dhs, wd, a, b],
        [pl.BlockSpec((tm, d), lambda j, i: (i, 0)), pl.BlockSpec((None, f4, d), lambda j, i: (j, 0, 0)),
         slab_spec, slab_spec],
        [(0, 1, "nt", 0)], [(tm, f4)], ds_epi, [slab, slab], [slab_spec, slab_spec], ("parallel", "parallel"),
        comm)
    (da, db), couts = (res, []) if comm is None else res

    def copy_epi(accs, in_refs, out_refs):
        for acc, o in zip(accs, out_refs):
            o[...] = acc

    (dwd,) = _matmul(
        "ffn_dwd", (N_CHIPS, nk), 1, [s, dhs],
        [pl.BlockSpec((None, tk, f4), lambda j, k: (j, k, 0)), pl.BlockSpec((tk, d), lambda j, k: (k, 0))],
        [(0, 1, "tn", 0)], [(f4, d)], copy_epi,
        [jax.ShapeDtypeStruct((N_CHIPS, f4, d), F32)], [pl.BlockSpec((None, f4, d), lambda j, k: (j, 0, 0))],
        ("parallel", "arbitrary"))

    dw_shape = jax.ShapeDtypeStruct((N_CHIPS, f4, d), F32)
    dw_spec = pl.BlockSpec((None, f4, d), lambda j, k: (j, 0, 0))
    in_slab = pl.BlockSpec((None, tk, f4), lambda j, k: (j, k, 0))
    dwg, dwu = _matmul(
        "ffn_dwgu", (N_CHIPS, nk), 1, [n, da, db],
        [pl.BlockSpec((tk, d), lambda j, k: (k, 0)), in_slab, in_slab],
        [(1, 0, "tn", 0), (2, 0, "tn", 1)], [(f4, d)] * 2, copy_epi,
        [dw_shape, dw_shape], [dw_spec, dw_spec], ("parallel", "arbitrary"))

    def dn_epi(accs, in_refs, out_refs):
        i, j = pl.program_id(0), pl.program_id(1)
        dx, dgrow = _rms_bwd_math(accs[0], in_refs[5][...], in_refs[6][...])
        out_refs[0][...] = in_refs[4][...] + dx

        @pl.when(i == 0)
        def _():
            out_refs[1][...] = jnp.zeros_like(out_refs[1])

        out_refs[1][...] += jnp.sum(dgrow, axis=0, keepdims=True)

    row_spec = pl.BlockSpec((tm, d), lambda i, j: (i, 0))
    in_slab2 = pl.BlockSpec((None, tm, f4), lambda i, j: (j, i, 0))
    w_spec = pl.BlockSpec((None, d, f4), lambda i, j: (j, 0, 0))
    red = _Reduction([dwg, dwu, dwd], pos, "ffn")
    (dh_in, dgain), red.got = _matmul(
        "ffn_bwd_dn", (ni, N_CHIPS), 1, [da, wg, db, wu, dh, h_in, gain.reshape(1, d)],
        [in_slab2, w_spec, in_slab2, w_spec, row_spec, row_spec, pl.BlockSpec((1, d), lambda i, j: (0, 0))],
        [(0, 1, "nt", 0), (2, 3, "nt", 0)], [(tm, d)], dn_epi,
        [jax.ShapeDtypeStruct((lp, d), F32), jax.ShapeDtypeStruct((1, d), F32)],
        [row_spec, pl.BlockSpec((1, d), lambda i, j: (0, 0))], ("arbitrary", "arbitrary"), red.comm())
    return dh_in, dgain, red, couts


def _rope_tables(lp):
    pos = jnp.arange(lp, dtype=F32) - float(PAD_FRONT)
    inv_freq = ROPE_THETA ** (-jnp.arange(0, ROT_DIM, 2, dtype=F32) / ROT_DIM)
    ang = pos[:, None] * inv_freq[None, :]
    cos, sin = jnp.cos(ang), jnp.sin(ang)
    half = ROT_DIM // 2
    ones = jnp.ones((lp, HEAD_DIM - ROT_DIM), F32)
    zeros_h = jnp.zeros((lp, half), F32)
    zeros_r = jnp.zeros((lp, HEAD_DIM - ROT_DIM), F32)
    c = jnp.concatenate([cos, cos, ones], axis=1)
    s1 = jnp.concatenate([-sin, zeros_h, zeros_r], axis=1)
    s2 = jnp.concatenate([zeros_h, sin, zeros_r], axis=1)
    reps = LANES // HEAD_DIM
    return jnp.stack([jnp.tile(c, (1, reps)), jnp.tile(s1, (1, reps)), jnp.tile(s2, (1, reps))])


def _rope(x, c, s1, s2):
    half = ROT_DIM // 2
    outs = []
    for ch in range(x.shape[1] // LANES):
        xc = x[:, ch * LANES:(ch + 1) * LANES]
        outs.append(xc * c + pltpu.roll(xc, LANES - half, 1) * s1 + pltpu.roll(xc, half, 1) * s2)
    return outs[0] if len(outs) == 1 else jnp.concatenate(outs, axis=1)


def _rope_t(dy, c, s1, s2):
    half = ROT_DIM // 2
    outs = []
    for ch in range(dy.shape[1] // LANES):
        dc = dy[:, ch * LANES:(ch + 1) * LANES]
        outs.append(dc * c + pltpu.roll(dc * s1, half, 1) + pltpu.roll(dc * s2, LANES - half, 1))
    return outs[0] if len(outs) == 1 else jnp.concatenate(outs, axis=1)


def _in_proj(n, w_in, tabs, ssm_w, comm=None):
    lp, d = n.shape
    inw = w_in.shape[1]
    tm = _row_tile(lp)
    o1 = ATTN_WIDTH
    o2 = o1 + KV_WIDTH
    o3 = o2 + KV_WIDTH
    o4 = o3 + ssm_w
    o5 = o4 + d

    def epi(accs, in_refs, out_refs):
        z = accs[0]
        c, s1, s2 = in_refs[2][0], in_refs[2][1], in_refs[2][2]
        out_refs[0][...] = _rope(z[:, :o1], c, s1, s2).astype(BF16)
        out_refs[1][...] = _rope(z[:, o1:o2], c, s1, s2).astype(BF16)
        out_refs[2][...] = z[:, o2:o3].astype(BF16)
        out_refs[3][...] = z[:, o3:o4]
        out_refs[4][...] = z[:, o4:o5]
        out_refs[5][...] = z[:, o5:]

    def rs(w, dt):
        return jax.ShapeDtypeStruct((lp, w), dt), pl.BlockSpec((tm, w), lambda i: (i, 0))

    shapes, specs = zip(rs(o1, BF16), rs(KV_WIDTH, BF16), rs(KV_WIDTH, BF16), rs(ssm_w, F32), rs(d, F32), rs(d, F32))
    res = _matmul(
        "mix_in_proj", (lp // tm,), None, [n, w_in, tabs],
        [pl.BlockSpec((tm, d), lambda i: (i, 0)), pl.BlockSpec((d, inw), lambda i: (0, 0)),
         pl.BlockSpec((3, tm, LANES), lambda i: (0, i, 0))],
        [(0, 1, "nn", 0)], [(tm, inw)], epi, list(shapes), list(specs), ("parallel",), comm)
    return (res, []) if comm is None else res


def _attn_mask(b):
    rows = lax.broadcasted_iota(jnp.int32, (BLOCK, 3 * BLOCK), 0)
    cols = lax.broadcasted_iota(jnp.int32, (BLOCK, 3 * BLOCK), 1)
    qpos = b * BLOCK + rows - PAD_FRONT
    kpos = (b - 1) * BLOCK + cols - PAD_FRONT
    dist = qpos - kpos
    band = (cols < 2 * BLOCK) & (kpos >= N_META) & (dist >= 0) & (dist < BLOCK)
    mrow = cols - 2 * BLOCK
    meta = (mrow >= PAD_FRONT) & ((mrow - PAD_FRONT) <= qpos)
    return band | meta


def _attn_probs(qh, kk, mask, sink):
    s = _dot(qh, kk, "nt") * (HEAD_DIM ** -0.5)
    s = jnp.where(mask, s, NEG_INF)
    m = jnp.maximum(jnp.max(s, axis=-1, keepdims=True), sink)
    e = jnp.exp(s - m)
    es = jnp.exp(sink - m)
    z = jnp.sum(e, axis=-1, keepdims=True) + es
    inv = 1.0 / z
    return e * inv, es * inv


def _head(ref_or_val, h):
    return ref_or_val[:, h * HEAD_DIM:(h + 1) * HEAD_DIM]


def _attn_fwd(q, k, v, sinks, comm=None):
    lp = q.shape[0]
    nb = lp // BLOCK

    def body(sink_ref, q_ref, kp_ref, kc_ref, km_ref, vp_ref, vc_ref, vm_ref, o_ref):
        b = pl.program_id(0)
        mask = _attn_mask(b)
        for hk in range(N_KV_HEADS):
            kk = jnp.concatenate([_head(kp_ref, hk), _head(kc_ref, hk), _head(km_ref, hk)], axis=0)
            vv = jnp.concatenate([_head(vp_ref, hk), _head(vc_ref, hk), _head(vm_ref, hk)], axis=0)
            for g in range(Q_PER_KV):
                h = hk * Q_PER_KV + g
                p, _ = _attn_probs(_head(q_ref, h), kk, mask, sink_ref[h])
                o_ref[:, h * HEAD_DIM:(h + 1) * HEAD_DIM] = _dot(p, vv, "nn").astype(BF16)

    cur = lambda b: (b, 0)
    prev = lambda b: (jnp.maximum(b - 1, 0), 0)
    first = lambda b: (0, 0)
    kvs = lambda f: pl.BlockSpec((BLOCK, KV_WIDTH), f)
    (o,), couts = _hosted_call(
        body, comm, out_shape=[jax.ShapeDtypeStruct((lp, ATTN_WIDTH), BF16)], grid=(nb,),
        in_specs=[pl.BlockSpec(memory_space=pltpu.SMEM), pl.BlockSpec((BLOCK, ATTN_WIDTH), cur),
                  kvs(prev), kvs(cur), kvs(first), kvs(prev), kvs(cur), kvs(first)],
        out_specs=[pl.BlockSpec((BLOCK, ATTN_WIDTH), cur)], scratch_shapes=[],
        sem=("parallel",), name="attn_fwd", args=(sinks, q, k, k, k, v, v, v))
    return o, couts


def _attn_bwd(q, k, v, do, sinks, tabs, comm=None):
    lp = q.shape[0]
    nb = lp // BLOCK
    scale = HEAD_DIM ** -0.5

    def body(sink_ref, q_ref, do_ref, kp_ref, kc_ref, km_ref, vp_ref, vc_ref, vm_ref, tq_ref, tk_ref, t0_ref,
             dq_ref, dk_ref, dv_ref, dkm_ref, dvm_ref, dsink_ref,
             dq_s, dkk_s, dvv_s, ck_s, cv_s, mk_s, mv_s):
        b = pl.program_id(0)

        @pl.when(b == 0)
        def _():
            for r in (ck_s, cv_s, mk_s, mv_s, dsink_ref):
                r[...] = jnp.zeros_like(r)

        @pl.when(b < nb)
        def _():
            mask = _attn_mask(b)
            for hk in range(N_KV_HEADS):
                kk = jnp.concatenate([_head(kp_ref, hk), _head(kc_ref, hk), _head(km_ref, hk)], axis=0)
                vv = jnp.concatenate([_head(vp_ref, hk), _head(vc_ref, hk), _head(vm_ref, hk)], axis=0)
                dkk = jnp.zeros((3 * BLOCK, HEAD_DIM), F32)
                dvv = jnp.zeros((3 * BLOCK, HEAD_DIM), F32)
                for g in range(Q_PER_KV):
                    h = hk * Q_PER_KV + g
                    qh = _head(q_ref, h)
                    doh = _head(do_ref, h)
                    p, ps = _attn_probs(qh, kk, mask, sink_ref[h])
                    dp = _dot(doh, vv, "nt")
                    delta = jnp.sum(p * dp, axis=-1, keepdims=True)
                    ds = (p * (dp - delta)).astype(BF16)
                    dsink_ref[h:h + 1, :] += jnp.zeros((1, LANES), F32) - jnp.sum(ps * delta)
                    dq_s[:, h * HEAD_DIM:(h + 1) * HEAD_DIM] = _dot(ds, kk, "nn") * scale
                    dkk = dkk + _dot(ds, qh, "tn") * scale
                    dvv = dvv + _dot(p, doh, "tn")
                dkk_s[:, hk * HEAD_DIM:(hk + 1) * HEAD_DIM] = dkk
                dvv_s[:, hk * HEAD_DIM:(hk + 1) * HEAD_DIM] = dvv
            dq_ref[...] = _rope_t(dq_s[...], tq_ref[0], tq_ref[1], tq_ref[2])
            dk_ref[...] = _rope_t(ck_s[...] + dkk_s[0:BLOCK, :], tk_ref[0], tk_ref[1], tk_ref[2])
            dv_ref[...] = cv_s[...] + dvv_s[0:BLOCK, :]
            ck_s[...] = dkk_s[BLOCK:2 * BLOCK, :]
            cv_s[...] = dvv_s[BLOCK:2 * BLOCK, :]
            mk_s[...] += dkk_s[2 * BLOCK:, :]
            mv_s[...] += dvv_s[2 * BLOCK:, :]

        @pl.when(b == nb)
        def _():
            dk_ref[...] = _rope_t(ck_s[...], tk_ref[0], tk_ref[1], tk_ref[2])
            dv_ref[...] = cv_s[...]
            dkm_ref[...] = _rope_t(mk_s[...], t0_ref[0], t0_ref[1], t0_ref[2])
            dvm_ref[...] = mv_s[...]

    cur = lambda b: (jnp.minimum(b, nb - 1), 0)
    prev = lambda b: (jnp.clip(b - 1, 0, nb - 1), 0)
    first = lambda b: (0, 0)
    kvs = lambda f: pl.BlockSpec((BLOCK, KV_WIDTH), f)
    tab = lambda f: pl.BlockSpec((3, BLOCK, LANES), lambda b: (0,) + f(b)[:1] + (0,))
    kv_out = lambda b: (jnp.maximum(b - 1, 0), 0)
    return _hosted_call(
        body, comm,
        out_shape=[jax.ShapeDtypeStruct((lp, ATTN_WIDTH), F32), jax.ShapeDtypeStruct((lp, KV_WIDTH), F32),
                   jax.ShapeDtypeStruct((lp, KV_WIDTH), F32), jax.ShapeDtypeStruct((BLOCK, KV_WIDTH), F32),
                   jax.ShapeDtypeStruct((BLOCK, KV_WIDTH), F32), jax.ShapeDtypeStruct((N_Q_HEADS, LANES), F32)],
        grid=(nb + 1,),
        in_specs=[pl.BlockSpec(memory_space=pltpu.SMEM), pl.BlockSpec((BLOCK, ATTN_WIDTH), cur),
                  pl.BlockSpec((BLOCK, ATTN_WIDTH), cur),
                  kvs(prev), kvs(cur), kvs(first), kvs(prev), kvs(cur), kvs(first),
                  tab(cur), tab(kv_out), tab(first)],
        out_specs=[pl.BlockSpec((BLOCK, ATTN_WIDTH), cur), kvs(kv_out), kvs(kv_out), kvs(first), kvs(first),
                   pl.BlockSpec((N_Q_HEADS, LANES), first)],
        scratch_shapes=[pltpu.VMEM((BLOCK, ATTN_WIDTH), F32), pltpu.VMEM((3 * BLOCK, KV_WIDTH), F32),
                        pltpu.VMEM((3 * BLOCK, KV_WIDTH), F32), pltpu.VMEM((BLOCK, KV_WIDTH), F32),
                        pltpu.VMEM((BLOCK, KV_WIDTH), F32), pltpu.VMEM((BLOCK, KV_WIDTH), F32),
                        pltpu.VMEM((BLOCK, KV_WIDTH), F32)],
        sem=("arbitrary",), name="attn_bwd", args=(sinks, q, do, k, k, k, v, v, v, tabs, tabs, tabs))


def _cmul(ar, ai, br, bi):
    return ar * br - ai * bi, ar * bi + ai * br


def _cpow(lr, li, n):
    rr = ri = None
    br, bi = lr, li
    while n:
        if n & 1:
            rr, ri = (br, bi) if rr is None else _cmul(rr, ri, br, bi)
        n >>= 1
        if n:
            br, bi = _cmul(br, bi, br, bi)
    return rr, ri


def _shift_rows(x, d, reverse):
    rows = lax.broadcasted_iota(jnp.int32, x.shape, 0)
    if not reverse:
        return jnp.where(rows >= d, pltpu.roll(x, d, 0), 0.0)
    return jnp.where(rows < SUBLANES - d, pltpu.roll(x, SUBLANES - d, 0), 0.0)


def _sublane_powers(mr, mi, reverse):
    rows = lax.broadcasted_iota(jnp.int32, mr.shape, 0)
    e = SUBLANES - 1 - rows if reverse else rows
    pr, pi = jnp.ones_like(mr), jnp.zeros_like(mr)
    br, bi = mr, mi
    for d in (1, 2, 4):
        tr, ti = _cmul(pr, pi, br, bi)
        on = (e & d) != 0
        pr, pi = jnp.where(on, tr, pr), jnp.where(on, ti, pi)
        if d < 4:
            br, bi = _cmul(br, bi, br, bi)
    return pr, pi


def _inclusive_prefix(er, ei, mr, mi, reverse):
    ir, ii, pr, pi = er, ei, mr, mi
    for d in (1, 2, 4):
        tr, ti = _cmul(pr, pi, _shift_rows(ir, d, reverse), _shift_rows(ii, d, reverse))
        ir, ii = ir + tr, ii + ti
        if d < 4:
            pr, pi = _cmul(pr, pi, pr, pi)
    return ir, ii


def _chain_rows(a, t, seg):
    return pl.ds(a * SUBLANES * seg + t, SUBLANES, stride=seg)


def _seg_scan(xr_ref, xi_ref, lam, seg, nchain, reverse, store, init, extra=None):
    nt = len(lam)
    acc0 = () if extra is None else extra[1]

    def step(i, carry):
        hs, acc = carry
        t = seg - 1 - i if reverse else i
        out = []
        for a in range(nchain):
            sl = _chain_rows(a, t, seg)
            for j in range(nt):
                lr, li = lam[j]
                k = 2 * (a * nt + j)
                hr, hi = hs[k], hs[k + 1]
                nr = lr * hr - li * hi + xr_ref[j, sl, :]
                ni = lr * hi + li * hr + xi_ref[j, sl, :]
                if store:
                    xr_ref[j, sl, :] = nr
                    xi_ref[j, sl, :] = ni
                if extra is not None:
                    acc = extra[0](t, a, j, nr, ni, acc)
                out += [nr, ni]
        return tuple(out), acc

    return lax.fori_loop(0, seg, step, (tuple(init), acc0))


def _ssm_scan(xr_ref, xi_ref, lam, seg, nchain, reverse, extra=None):
    nt = len(lam)
    zero = [jnp.zeros((SUBLANES, LANES), F32)] * (2 * nt * nchain)
    ends, _ = _seg_scan(xr_ref, xi_ref, lam, seg, nchain, reverse, False, zero)
    init = [None] * (2 * nt * nchain)
    last = 0 if reverse else SUBLANES - 1
    for j in range(nt):
        mr, mi = _cpow(lam[j][0], lam[j][1], seg)
        m8r, m8i = _cpow(mr, mi, SUBLANES)
        pwr, pwi = _sublane_powers(mr, mi, reverse)
        gr = gi = jnp.zeros((SUBLANES, LANES), F32)
        for a in (reversed(range(nchain)) if reverse else range(nchain)):
            k = 2 * (a * nt + j)
            incr, inci = _inclusive_prefix(ends[k], ends[k + 1], mr, mi, reverse)
            tr, ti = _cmul(pwr, pwi, gr, gi)
            init[k] = _shift_rows(incr, 1, reverse) + tr
            init[k + 1] = _shift_rows(inci, 1, reverse) + ti
            g2r, g2i = _cmul(m8r, m8i, gr, gi)
            gr = g2r + jnp.broadcast_to(incr[last:last + 1, :], gr.shape)
            gi = g2i + jnp.broadcast_to(inci[last:last + 1, :], gi.shape)
    _, acc = _seg_scan(xr_ref, xi_ref, lam, seg, nchain, reverse, True, init, extra)
    return acc


def _diag_mask():
    steps = LANES // SSM_GROUP // GROUPS_PER_TILE
    return (jnp.eye(steps, dtype=F32)[:, None, :, None] * jnp.eye(GROUPS_PER_TILE, dtype=F32)[None, :, None, :])


def _ssm_tables(bb_re, bb_im, c_re, c_im):
    g = bb_re.shape[0]
    nt = g // GROUPS_PER_TILE
    steps = LANES // SSM_GROUP // GROUPS_PER_TILE
    mask = _diag_mask()

    def b_tab(bb):
        x = bb.reshape(nt // steps, steps, GROUPS_PER_TILE, SSM_STATE, SSM_GROUP)
        x = jnp.transpose(x, (0, 1, 4, 2, 3))[:, :, None, None]
        m = jnp.transpose(mask, (0, 2, 3, 1))[None, :, :, :, None, :, None]
        return (x * m).reshape(nt, LANES, TILE_STATES)

    def c_tab(c):
        x = c.reshape(nt // steps, steps, GROUPS_PER_TILE, SSM_GROUP, SSM_STATE)
        x = jnp.transpose(x, (0, 1, 2, 4, 3))[:, :, :, :, None, None]
        m = mask[None, :, :, None, :, :, None]
        return (x * m).reshape(nt, TILE_STATES, LANES)

    return b_tab(bb_re), b_tab(bb_im), c_tab(c_re), c_tab(c_im)


def _ssm_untable_b(db, g):
    nt = g // GROUPS_PER_TILE
    steps = LANES // SSM_GROUP // GROUPS_PER_TILE
    x = db.reshape(nt // steps, steps, GROUPS_PER_TILE, SSM_STATE, steps, GROUPS_PER_TILE, SSM_GROUP)
    m = _diag_mask()[None, :, :, None, :, :, None]
    return jnp.sum(x * m, axis=(4, 5)).reshape(g, SSM_STATE, SSM_GROUP)


def _ssm_untable_c(dc, g):
    nt = g // GROUPS_PER_TILE
    steps = LANES // SSM_GROUP // GROUPS_PER_TILE
    x = dc.reshape(nt // steps, steps, steps, GROUPS_PER_TILE, SSM_GROUP, GROUPS_PER_TILE, SSM_STATE)
    m = jnp.transpose(_diag_mask(), (0, 2, 3, 1))[None, :, :, :, None, :, None]
    out = jnp.sum(x * m, axis=(2, 3))
    return jnp.transpose(out, (0, 1, 3, 2, 4)).reshape(g, SSM_GROUP, SSM_STATE)


def _lam_tiles(lam_ref):
    out = []
    for j in range(TILE_STATES // LANES):
        out.append(jnp.broadcast_to(lam_ref[:, j * LANES:(j + 1) * LANES], (SUBLANES, LANES)))
    return out


def _scan_chains(lp):
    for n in (4, 2, 1):
        if lp % (SUBLANES * n) == 0 and (lp // SUBLANES) % 16 == 0:
            return n
    raise ValueError(lp)


def _split_tiles(dst_ref, rows, val):
    for j in range(val.shape[1] // LANES):
        dst_ref[j, rows, :] = val[:, j * LANES:(j + 1) * LANES]


def _cat_tiles(src_ref, rows):
    njt = src_ref.shape[0]
    return jnp.concatenate([src_ref[j, rows, :] for j in range(njt)], axis=1).astype(BF16)


def _ssm_fwd(u, lam_re, lam_im, tb_re, tb_im, tc_re, tc_im, d_skip, comm=None):
    lp, w = u.shape
    nt = tb_re.shape[0]
    nchain = _scan_chains(lp)
    seg = lp // (SUBLANES * nchain)
    chunk = lp // SUBLANES
    njt = TILE_STATES // LANES

    def body(u_ref, lr_ref, li_ref, br_ref, bi_ref, cr_ref, ci_ref, d_ref, y_ref, xr, xi):
        t = pl.program_id(0)
        for s in range(SUBLANES):
            rs = pl.ds(s * chunk, chunk)
            ub = u_ref[rs, :].astype(BF16)
            _split_tiles(xr, rs, _dot(ub, br_ref[...], "nn"))
            _split_tiles(xi, rs, _dot(ub, bi_ref[...], "nn"))
        lrs, lis = _lam_tiles(lr_ref), _lam_tiles(li_ref)
        _ssm_scan(xr, xi, list(zip(lrs, lis)), seg, nchain, False)
        for s in range(SUBLANES):
            rs = pl.ds(s * chunk, chunk)
            y = _dot(_cat_tiles(xr, rs), cr_ref[...], "nn") - _dot(_cat_tiles(xi, rs), ci_ref[...], "nn")

            @pl.when(t % 2 == 0)
            def _():
                y_ref[rs, :] = y + d_ref[...] * u_ref[rs, :]

            @pl.when(t % 2 == 1)
            def _():
                y_ref[rs, :] += y

    blk = pl.BlockSpec((lp, LANES), lambda t: (0, t // 2))
    lam_spec = pl.BlockSpec((None, 1, TILE_STATES), lambda t: (t, 0, 0))
    b_spec = pl.BlockSpec((None, LANES, TILE_STATES), lambda t: (t, 0, 0))
    c_spec = pl.BlockSpec((None, TILE_STATES, LANES), lambda t: (t, 0, 0))
    (y,), couts = _hosted_call(
        body, comm, out_shape=[jax.ShapeDtypeStruct((lp, w), F32)], grid=(nt,),
        in_specs=[blk, lam_spec, lam_spec, b_spec, b_spec, c_spec, c_spec,
                  pl.BlockSpec((1, LANES), lambda t: (0, t // 2))],
        out_specs=[blk],
        scratch_shapes=[pltpu.VMEM((njt, lp, LANES), F32), pltpu.VMEM((njt, lp, LANES), F32)],
        sem=("arbitrary",), name="ssm_fwd",
        args=(u, lam_re, lam_im, tb_re, tb_im, tc_re, tc_im, d_skip.reshape(1, w)))
    return y, couts


def _ssm_bwd(u, dy, lam_re, lam_im, tb_re, tb_im, tc_re, tc_im, d_skip, comm=None):
    lp, w = u.shape
    nt = tb_re.shape[0]
    nchain = _scan_chains(lp)
    seg = lp // (SUBLANES * nchain)
    chunk = lp // SUBLANES
    njt = TILE_STATES // LANES
    tbt_re, tbt_im = jnp.swapaxes(tb_re, 1, 2), jnp.swapaxes(tb_im, 1, 2)
    tct_re, tct_im = jnp.swapaxes(tc_re, 1, 2), jnp.swapaxes(tc_im, 1, 2)

    def body(u_ref, dy_ref, lr_ref, li_ref, br_ref, bi_ref, btr_ref, bti_ref, ctr_ref, cti_ref, d_ref,
             du_ref, dlr_ref, dli_ref, dbr_ref, dbi_ref, dcr_ref, dci_ref, dd_ref, hr, hi, ar, ai):
        t = pl.program_id(0)
        lrs, lis = _lam_tiles(lr_ref), _lam_tiles(li_ref)
        for s in range(SUBLANES):
            rs = pl.ds(s * chunk, chunk)
            ub = u_ref[rs, :].astype(BF16)
            dyb = dy_ref[rs, :].astype(BF16)
            _split_tiles(hr, rs, _dot(ub, br_ref[...], "nn"))
            _split_tiles(hi, rs, _dot(ub, bi_ref[...], "nn"))
            _split_tiles(ar, rs, _dot(dyb, ctr_ref[...], "nn"))
            _split_tiles(ai, rs, -_dot(dyb, cti_ref[...], "nn"))
        _ssm_scan(hr, hi, list(zip(lrs, lis)), seg, nchain, False)

        def dlam_step(tt, a, j, a_r, a_i, acc):
            sl = _chain_rows(a, jnp.maximum(tt - 1, 0), seg)
            p_r, p_i = hr[j, sl, :], hi[j, sl, :]
            acc = list(acc)
            acc[2 * j] = acc[2 * j] + jnp.where(tt > 0, a_r * p_r + a_i * p_i, 0.0)
            acc[2 * j + 1] = acc[2 * j + 1] + jnp.where(tt > 0, a_i * p_r - a_r * p_i, 0.0)
            return tuple(acc)

        zero = tuple([jnp.zeros((SUBLANES, LANES), F32)] * (2 * njt))
        conj = [(lr, -li) for lr, li in zip(lrs, lis)]
        acc = list(_ssm_scan(ar, ai, conj, seg, nchain, True, (dlam_step, zero)))
        row0 = lax.broadcasted_iota(jnp.int32, (SUBLANES, LANES), 0) == 0
        for j in range(njt):
            cs = slice(j * LANES, (j + 1) * LANES)
            for a in range(nchain):
                p_r = _shift_rows(hr[j, _chain_rows(a, seg - 1, seg), :], 1, False)
                p_i = _shift_rows(hi[j, _chain_rows(a, seg - 1, seg), :], 1, False)
                if a > 0:
                    before = pl.ds(a * SUBLANES * seg - 1, 1)
                    p_r = jnp.where(row0, jnp.broadcast_to(hr[j, before, :], p_r.shape), p_r)
                    p_i = jnp.where(row0, jnp.broadcast_to(hi[j, before, :], p_i.shape), p_i)
                a_r, a_i = ar[j, _chain_rows(a, 0, seg), :], ai[j, _chain_rows(a, 0, seg), :]
                acc[2 * j] = acc[2 * j] + a_r * p_r + a_i * p_i
                acc[2 * j + 1] = acc[2 * j + 1] + a_i * p_r - a_r * p_i
            dlr_ref[:, cs] = jnp.sum(acc[2 * j], axis=0, keepdims=True)
            dli_ref[:, cs] = jnp.sum(acc[2 * j + 1], axis=0, keepdims=True)

        dd = jnp.zeros((1, LANES), F32)
        for s in range(SUBLANES):
            rs = pl.ds(s * chunk, chunk)
            ub = u_ref[rs, :].astype(BF16)
            dyv = dy_ref[rs, :]
            dyb = dyv.astype(BF16)
            arb, aib = _cat_tiles(ar, rs), _cat_tiles(ai, rs)
            hrb, hib = _cat_tiles(hr, rs), _cat_tiles(hi, rs)
            du = _dot(arb, btr_ref[...], "nn") + _dot(aib, bti_ref[...], "nn")
            upd = [(dbr_ref, _dot(arb, ub, "tn")), (dbi_ref, _dot(aib, ub, "tn")),
                   (dcr_ref, _dot(dyb, hrb, "tn")), (dci_ref, -_dot(dyb, hib, "tn"))]
            for ref, val in upd:
                if s == 0:
                    ref[...] = val
                else:
                    ref[...] += val
            rows = lax.broadcasted_iota(jnp.int32, (chunk, LANES), 0) + s * chunk
            keep = rows >= PAD_FRONT
            dd = dd + jnp.sum(dyv * u_ref[rs, :], axis=0, keepdims=True)

            @pl.when(t % 2 == 0)
            def _():
                du_ref[rs, :] = jnp.where(keep, du + d_ref[...] * dyv, 0.0)

            @pl.when(t % 2 == 1)
            def _():
                du_ref[rs, :] += jnp.where(keep, du, 0.0)

        @pl.when(t % 2 == 0)
        def _():
            dd_ref[...] = dd

    blk = pl.BlockSpec((lp, LANES), lambda t: (0, t // 2))
    vec = pl.BlockSpec((1, LANES), lambda t: (0, t // 2))
    lam_spec = pl.BlockSpec((None, 1, TILE_STATES), lambda t: (t, 0, 0))
    b_spec = pl.BlockSpec((None, LANES, TILE_STATES), lambda t: (t, 0, 0))
    c_spec = pl.BlockSpec((None, TILE_STATES, LANES), lambda t: (t, 0, 0))
    lam_shape = jax.ShapeDtypeStruct((nt, 1, TILE_STATES), F32)
    bt_shape = jax.ShapeDtypeStruct((nt, TILE_STATES, LANES), F32)
    ct_shape = jax.ShapeDtypeStruct((nt, LANES, TILE_STATES), F32)
    st = pltpu.VMEM((njt, lp, LANES), F32)
    return _hosted_call(
        body, comm,
        out_shape=[jax.ShapeDtypeStruct((lp, w), F32), lam_shape, lam_shape, bt_shape, bt_shape, ct_shape, ct_shape,
                   jax.ShapeDtypeStruct((1, w), F32)],
        grid=(nt,),
        in_specs=[blk, blk, lam_spec, lam_spec, b_spec, b_spec, c_spec, c_spec, b_spec, b_spec, vec],
        out_specs=[blk, lam_spec, lam_spec, c_spec, c_spec, b_spec, b_spec, vec],
        scratch_shapes=[st, st, st, st], sem=("arbitrary",), name="ssm_bwd",
        args=(u, dy, lam_re, lam_im, tb_re, tb_im, tbt_re, tbt_im, tct_re, tct_im, d_skip.reshape(1, w)))


def _ssm_params(a_re, a_im, log_dt, b_re, b_im):
    dt = jnp.exp(log_dt)[:, None]
    mag = jnp.exp(a_re * dt)
    lb_re = mag * jnp.cos(a_im * dt)
    lb_im = mag * jnp.sin(a_im * dt)
    den = a_re * a_re + a_im * a_im
    num_re = lb_re - 1.0
    coef_re = (num_re * a_re + lb_im * a_im) / den
    coef_im = (lb_im * a_re - num_re * a_im) / den
    bb_re = coef_re[..., None] * b_re - coef_im[..., None] * b_im
    bb_im = coef_re[..., None] * b_im + coef_im[..., None] * b_re
    return lb_re, lb_im, bb_re, bb_im


def _gelu_fwd(y):
    lp, w = y.shape
    tm = _row_tile(lp)

    def body(y_ref, o_ref):
        o_ref[...] = _gelu(y_ref[...]).astype(BF16)

    return pl.pallas_call(
        body, out_shape=jax.ShapeDtypeStruct((lp, w), BF16), grid=(lp // tm,),
        in_specs=[pl.BlockSpec((tm, w), lambda i: (i, 0))], out_specs=pl.BlockSpec((tm, w), lambda i: (i, 0)),
        compiler_params=_cparams(("parallel",)), name="gelu_fwd")(y)


def _merge_fwd(o, yg, ga, gs, w3, comm=None):
    lp, d = ga.shape
    d4 = w3.shape[3]
    kw = w3.shape[2]
    tm = _row_tile(lp)

    def epi(accs, in_refs, out_refs):
        attn, vv, gg = accs
        out_refs[0][...] = (_sigmoid(in_refs[5][...]) * attn
                            + _sigmoid(in_refs[6][...]) * (vv * _sigmoid(gg))).astype(BF16)

    wspec = lambda which: pl.BlockSpec((None, None, kw, d4), lambda j, i: (j, which, 0, 0))
    colspec = pl.BlockSpec((tm, d4), lambda j, i: (i, j))
    aspec = pl.BlockSpec((tm, kw), lambda j, i: (i, 0))
    res = _matmul(
        "merge_fwd", (N_CHIPS, lp // tm), None, [o, yg, w3, w3, w3, ga, gs],
        [aspec, aspec, wspec(0), wspec(1), wspec(2), colspec, colspec],
        [(0, 2, "nn", 0), (1, 3, "nn", 1), (1, 4, "nn", 2)], [(tm, d4)] * 3, epi,
        [jax.ShapeDtypeStruct((lp, d), BF16)], [colspec], ("parallel", "parallel"), comm)
    return (res[0], []) if comm is None else (res[0][0], res[1])


def _out_proj(merged, w_out, h, comm=None):
    lp, d = h.shape
    d4 = w_out.shape[1]
    tm = _row_tile(lp)

    def epi(accs, in_refs, out_refs):
        out_refs[0][...] = in_refs[2][...] + accs[0]

    res = _matmul(
        "mix_out_proj", (lp // tm, N_CHIPS), 1, [merged, w_out, h],
        [pl.BlockSpec((tm, d4), lambda i, j: (i, j)), pl.BlockSpec((None, d4, d), lambda i, j: (j, 0, 0)),
         pl.BlockSpec((tm, d), lambda i, j: (i, 0))],
        [(0, 1, "nn", 0)], [(tm, d)], epi,
        [jax.ShapeDtypeStruct((lp, d), F32)], [pl.BlockSpec((tm, d), lambda i, j: (i, 0))],
        ("parallel", "arbitrary"), comm)
    return (res[0], []) if comm is None else (res[0][0], res[1])


def _merge_bwd(dhb, w_out, o, yg, ga, gs, w3):
    lp, d = ga.shape
    d4 = w3.shape[3]
    kw = w3.shape[2]
    tm = _row_tile(lp)

    def epi(accs, in_refs, out_refs):
        dm, attn, vv, gg = accs
        sa = _sigmoid(in_refs[7][...])
        ss = _sigmoid(in_refs[8][...])
        sg = _sigmoid(gg)
        ssm = vv * sg
        dssm = dm * ss
        out_refs[0][...] = (dm * sa).astype(BF16)
        out_refs[1][...] = (dssm * sg).astype(BF16)
        out_refs[2][...] = (dssm * vv * sg * (1.0 - sg)).astype(BF16)
        out_refs[3][...] = (dm * attn * sa * (1.0 - sa)).astype(BF16)
        out_refs[4][...] = (dm * ssm * ss * (1.0 - ss)).astype(BF16)

    wspec = lambda which: pl.BlockSpec((None, None, kw, d4), lambda j, i: (j, which, 0, 0))
    colspec = pl.BlockSpec((tm, d4), lambda j, i: (i, j))
    aspec = pl.BlockSpec((tm, kw), lambda j, i: (i, 0))
    shp = jax.ShapeDtypeStruct((lp, d), BF16)
    return _matmul(
        "merge_bwd", (N_CHIPS, lp // tm), None, [dhb, w_out, o, yg, w3, w3, w3, ga, gs],
        [pl.BlockSpec((tm, d), lambda j, i: (i, 0)), pl.BlockSpec((None, d4, d), lambda j, i: (j, 0, 0)),
         aspec, aspec, wspec(0), wspec(1), wspec(2), colspec, colspec],
        [(0, 1, "nt", 0), (2, 4, "nn", 1), (3, 5, "nn", 2), (3, 6, "nn", 3)], [(tm, d4)] * 4, epi,
        [shp] * 5, [colspec] * 5, ("parallel", "parallel"))


def _branch_bwd(dattn, dv, dg, w3, y):
    lp, d = dattn.shape
    d4 = w3.shape[3]
    kw = w3.shape[2]
    tm = _row_tile(lp)

    def epi(accs, in_refs, out_refs):
        out_refs[0][...] = accs[0].astype(BF16)
        out_refs[1][...] = accs[1] * _gelu_grad(in_refs[6][...])

    wspec = lambda which: pl.BlockSpec((None, None, kw, d4), lambda i, j: (j, which, 0, 0))
    colspec = pl.BlockSpec((tm, d4), lambda i, j: (i, j))
    rowspec = pl.BlockSpec((tm, kw), lambda i, j: (i, 0))
    return _matmul(
        "branch_bwd", (lp // tm, N_CHIPS), 1, [dattn, dv, dg, w3, w3, w3, y],
        [colspec, colspec, colspec, wspec(0), wspec(1), wspec(2), rowspec],
        [(0, 3, "nt", 0), (1, 4, "nt", 1), (2, 5, "nt", 1)], [(tm, kw)] * 2, epi,
        [jax.ShapeDtypeStruct((lp, kw), BF16), jax.ShapeDtypeStruct((lp, kw), F32)], [rowspec, rowspec],
        ("parallel", "arbitrary"))


def _tn_cols(x, ys, name):
    lp, kx = x.shape
    n = ys[0].shape[1]
    n4 = n // N_CHIPS
    tk = _tn_tiles(lp)

    def epi(accs, in_refs, out_refs):
        for acc, o in zip(accs, out_refs):
            o[...] = acc

    shp = jax.ShapeDtypeStruct((N_CHIPS, kx, n4), F32)
    return _matmul(
        name, (N_CHIPS, lp // tk), 1, [x] + list(ys),
        [pl.BlockSpec((tk, kx), lambda j, k: (k, 0))] + [pl.BlockSpec((tk, n4), lambda j, k: (k, j))] * len(ys),
        [(0, 1 + i, "tn", i) for i in range(len(ys))], [(kx, n4)] * len(ys), epi,
        [shp] * len(ys), [pl.BlockSpec((None, kx, n4), lambda j, k: (j, 0, 0))] * len(ys),
        ("parallel", "arbitrary"))


def _tn_full(x, y, name, tn_cols=None):
    lp, kx = x.shape
    n = y.shape[1]
    tk = _tn_tiles(lp)
    tn = n if tn_cols is None else tn_cols

    def epi(accs, in_refs, out_refs):
        out_refs[0][...] = accs[0]

    (out,) = _matmul(
        name, (n // tn, lp // tk), 1, [x, y],
        [pl.BlockSpec((tk, kx), lambda j, k: (k, 0)), pl.BlockSpec((tk, tn), lambda j, k: (k, j))],
        [(0, 1, "tn", 0)], [(kx, tn)], epi,
        [jax.ShapeDtypeStruct((kx, n), F32)], [pl.BlockSpec((kx, tn), lambda j, k: (0, j))],
        ("parallel", "arbitrary"))
    return out


def _in_proj_bwd(dz, w_in, dh, h_in, gain):
    lp, d = h_in.shape
    inw = w_in.shape[1]
    tm = _row_tile(lp)

    def epi(accs, in_refs, out_refs):
        i = pl.program_id(0)
        dx, dgrow = _rms_bwd_math(accs[0], in_refs[3][...], in_refs[4][...])
        out_refs[0][...] = in_refs[2][...] + dx

        @pl.when(i == 0)
        def _():
            out_refs[1][...] = jnp.zeros_like(out_refs[1])

        out_refs[1][...] += jnp.sum(dgrow, axis=0, keepdims=True)

    row = pl.BlockSpec((tm, d), lambda i: (i, 0))
    return _matmul(
        "mix_in_proj_bwd", (lp // tm,), None, [dz, w_in, dh, h_in, gain.reshape(1, d)],
        [pl.BlockSpec((tm, inw), lambda i: (i, 0)), pl.BlockSpec((d, inw), lambda i: (0, 0)), row, row,
         pl.BlockSpec((1, d), lambda i: (0, 0))],
        [(0, 1, "nt", 0)], [(tm, d)], epi,
        [jax.ShapeDtypeStruct((lp, d), F32), jax.ShapeDtypeStruct((1, d), F32)],
        [row, pl.BlockSpec((1, d), lambda i: (0, 0))], ("arbitrary",))


def _loss_head(h, gain, target):
    lp, d = h.shape
    nb = lp // BLOCK

    def body(h_ref, g_ref, t_ref, dh_ref, dg_ref, loss_ref):
        i = pl.program_id(0)

        @pl.when(i == 0)
        def _():
            dg_ref[...] = jnp.zeros_like(dg_ref)
            loss_ref[...] = jnp.zeros_like(loss_ref)
            dh_ref[...] = jnp.zeros_like(dh_ref)

        @pl.when(i > 0)
        def _():
            x = h_ref[...]
            g = g_ref[...]
            r = lax.rsqrt(jnp.mean(x * x, axis=-1, keepdims=True) + EPS)
            err = x * r * g - t_ref[...]
            loss_ref[...] += jnp.zeros_like(loss_ref) + 0.5 * jnp.sum(jnp.sum(err * err, axis=-1, keepdims=True)) / d
            dx, dgrow = _rms_bwd_math(err * (1.0 / d), x, g)
            dh_ref[...] = dx
            dg_ref[...] += jnp.sum(dgrow, axis=0, keepdims=True)

    row = pl.BlockSpec((BLOCK, d), lambda i: (i, 0))
    one = pl.BlockSpec((1, d), lambda i: (0, 0))
    return pl.pallas_call(
        body,
        out_shape=[jax.ShapeDtypeStruct((lp, d), F32), jax.ShapeDtypeStruct((1, d), F32),
                   jax.ShapeDtypeStruct((SUBLANES, LANES), F32)],
        grid=(nb,),
        in_specs=[row, one, pl.BlockSpec((BLOCK, d), lambda i: (jnp.maximum(i - 1, 0), 0))],
        out_specs=[row, one, pl.BlockSpec((SUBLANES, LANES), lambda i: (0, 0))],
        compiler_params=_cparams(("arbitrary",)), name="loss_head")(h, gain.reshape(1, d), target)


def _adam_math(w, g, m, v):
    m = ADAM_B1 * m + (1.0 - ADAM_B1) * g
    v = ADAM_B2 * v + (1.0 - ADAM_B2) * (g * g)
    m_hat = m / (1.0 - ADAM_B1 ** ADAM_STEP)
    v_hat = v / (1.0 - ADAM_B2 ** ADAM_STEP)
    delta = -ADAM_LR * (m_hat / (jnp.sqrt(v_hat) + ADAM_EPS) + ADAM_WD * w)
    return delta, m, v


def _adamw_layers(w, m, v, mine, other, pos, name):
    depth, r, c = w.shape
    half = r // 2
    tr = _div_tile(half, c * 4)
    nh = half // tr

    def body(*refs):
        pos_ref, w_ref, m_ref, v_ref = refs[:4]
        mine_refs = refs[4:4 + depth]
        other_refs = refs[4 + depth:4 + 2 * depth]
        g_out, d_out, m_out, v_out = refs[4 + 2 * depth:]
        layer, i = pl.program_id(0), pl.program_id(1)
        is_mine = (i // nh) == pos_ref[0]

        def update(g):
            delta, nm, nv = _adam_math(w_ref[...], g, m_ref[...], v_ref[...])
            g_out[...] = g
            d_out[...] = delta
            m_out[...] = nm
            v_out[...] = nv

        for l in range(depth):
            @pl.when((layer == l) & is_mine)
            def _(l=l):
                update(mine_refs[l][...])

            @pl.when((layer == l) & jnp.logical_not(is_mine))
            def _(l=l):
                update(other_refs[l][...])

    stacked = pl.BlockSpec((None, tr, c), lambda l, i, p: (l, i, 0))

    def gspec(layer, is_other):
        def imap(l, i, p):
            first = jnp.where(is_other, 1 - p[0], p[0]) * nh
            here = jnp.clip(i - first, 0, nh - 1)
            return (jnp.where(l == layer, here, jnp.where(l < layer, 0, nh - 1)), 0)
        return pl.BlockSpec((tr, c), imap)

    shp = jax.ShapeDtypeStruct((depth, r, c), F32)
    grid_spec = pltpu.PrefetchScalarGridSpec(
        num_scalar_prefetch=1, grid=(depth, 2 * nh),
        in_specs=[stacked] * 3 + [gspec(l, 0) for l in range(depth)] + [gspec(l, 1) for l in range(depth)],
        out_specs=[stacked] * 4)
    return pl.pallas_call(
        body, out_shape=[shp] * 4, grid_spec=grid_spec,
        compiler_params=_cparams(("arbitrary", "arbitrary")), name=name)(pos, w, m, v, *mine, *other)


def _adamw_flat(w, g, m, v, name):
    r, c = w.shape
    tr = _div_tile(r, c * 4)

    def body(w_ref, g_ref, m_ref, v_ref, d_out, m_out, v_out):
        delta, nm, nv = _adam_math(w_ref[...], g_ref[...], m_ref[...], v_ref[...])
        d_out[...] = delta
        m_out[...] = nm
        v_out[...] = nv

    spec = pl.BlockSpec((tr, c), lambda i: (i, 0))
    shp = jax.ShapeDtypeStruct((r, c), F32)
    return pl.pallas_call(
        body, out_shape=[shp] * 3, grid=(r // tr,), in_specs=[spec] * 4, out_specs=[spec] * 3,
        compiler_params=_cparams(("parallel",)), name=name)(w, g, m, v)


def _mesh_pos():
    return lax.axis_index("x"), lax.axis_index("y"), lax.axis_index("c")


def _row_half(ref, which, lead):
    half = ref.shape[lead] // 2
    idx = (slice(None),) * lead + (pl.ds(which * half, half), slice(None))
    return ref.at[idx]


def _gather_comm(arrs, tag):
    n = len(arrs)

    def ctx(ins, outs, sems):
        send_sems, recv_sems, local_sems = sems
        x, y, c = _mesh_pos()
        chips = [(1 - x, y), (x, 1 - y), (1 - x, 1 - y)]

        def slot(k, chip, which):
            lead = len(ins[k].shape) - 2
            return _row_half(outs[k].at[2 * chip[0] + chip[1]], which, lead)

        def copy(k, j, src, dst, to):
            return pltpu.make_async_remote_copy(
                src_ref=src, dst_ref=dst, send_sem=send_sems.at[6 * k + j], recv_sem=recv_sems.at[6 * k + j],
                device_id=to, device_id_type=MESH)

        def local(k):
            return pltpu.make_async_copy(ins[k], outs[k].at[2 * x + y], local_sems.at[k])

        def first(k, j):
            lead = len(ins[k].shape) - 2
            return copy(k, j, _row_half(ins[k], c, lead), slot(k, (x, y), c), (*chips[j], c))

        def passed(k, j, which):
            return copy(k, 3 + j, slot(k, chips[j], which), slot(k, chips[j], which), (x, y, 1 - c))

        def landed(k, j):
            return copy(k, j, slot(k, chips[j], c), slot(k, chips[j], c), (x, y, 1 - c))

        return c, local, first, passed, landed

    def start(ins, outs, sems):
        c, local, first, passed, landed = ctx(ins, outs, sems)
        for k in range(n):
            local(k).start()
            for j in range(3):
                first(k, j).start()

    def mid(ins, outs, sems):
        c, local, first, passed, landed = ctx(ins, outs, sems)
        for j in range(3):
            for k in range(n):
                landed(k, j).wait_recv()
                passed(k, j, c).start()

    def finish(ins, outs, sems):
        c, local, first, passed, landed = ctx(ins, outs, sems)
        for j in range(3):
            for k in range(n):
                passed(k, j, 1 - c).wait_recv()
        for k in range(n):
            for j in range(3):
                first(k, j).wait_send()
                passed(k, j, c).wait_send()
            local(k).wait()

    return _Comm(
        tag, arrs, [jax.ShapeDtypeStruct((N_CHIPS,) + a.shape, a.dtype) for a in arrs],
        [pltpu.SemaphoreType.DMA((6 * n,)), pltpu.SemaphoreType.DMA((6 * n,)), pltpu.SemaphoreType.DMA((n,))],
        start, mid, finish)


def _run_comm(comm, name):
    n_in, n_out = len(comm.ins), len(comm.out_shapes)

    def body(*refs):
        ins, outs, sems = refs[:n_in], refs[n_in:n_in + n_out], refs[n_in + n_out:]
        comm.start(ins, outs, sems)
        if comm.mid is not None:
            comm.mid(ins, outs, sems)
        comm.finish(ins, outs, sems)

    return pl.pallas_call(
        body, out_shape=comm.out_shapes, in_specs=[HBM_SPEC] * n_in, out_specs=[HBM_SPEC] * n_out,
        scratch_shapes=comm.sems, name=name)(*comm.ins)


def _all_gather_chips(arrs, name):
    return _run_comm(_gather_comm(arrs, "gather"), name)


GATHER_US_PER_BYTE = 380.0 / 11.65e6
HOST_US = dict(ffn_up=78.0, ffn_down=65.0, in_proj=38.0, attn_fwd=103.0, ssm_fwd=67.0, merge_fwd=50.0,
               out_proj=45.0)
HOST_SLACK_US = 10.0


class _WeightStream:
    def __init__(self, pieces):
        self.keys = [k for k, _ in pieces]
        self.shards = dict(pieces)
        self.next = 0
        self.full = {}
        self.pending = []

    def comm_for(self, host):
        budget = HOST_US[host] + HOST_SLACK_US
        taken, cost = [], 0.0
        while self.next < len(self.keys):
            key = self.keys[self.next]
            c = self.shards[key].size * self.shards[key].dtype.itemsize * GATHER_US_PER_BYTE
            if cost + c > budget:
                break
            taken.append(key)
            cost += c
            self.next += 1
        self.pending = taken
        if not taken:
            return None
        return _gather_comm([self.shards[k] for k in taken], "g_" + "_".join(k[1] for k in taken))

    def deposit(self, gathered):
        for key, arr in zip(self.pending, gathered):
            self.full[key] = arr
        self.pending = []

    def get(self, key):
        if key not in self.full:
            upto = self.keys.index(key) + 1
            keys = self.keys[self.next:upto]
            self.next = upto
            for k, arr in zip(keys, _all_gather_chips([self.shards[k] for k in keys], "gather_now")):
                self.full[k] = arr
        return self.full[key]


def _all_gather_devices(x_shard, name):
    m_per, ncol = x_shard.shape

    def body(x_ref, out_ref, send_sems, recv_sems, local_sem):
        x, y, c = _mesh_pos()
        me, sibling = (x, y, c), (x, y, 1 - c)
        chips = [(1 - x, y), (x, 1 - y), (1 - x, 1 - y)]

        def rows(px, py, pc):
            return out_ref.at[4 * px + 2 * py + pc]

        def copy(k, block, to, src=None):
            return pltpu.make_async_remote_copy(
                src_ref=rows(*block) if src is None else src, dst_ref=rows(*block),
                send_sem=send_sems.at[k], recv_sem=recv_sems.at[k], device_id=to, device_id_type=MESH)

        mine = pltpu.make_async_copy(x_ref, rows(*me), local_sem)
        mine.start()
        first = [copy(0, me, sibling, src=x_ref)]
        first += [copy(1 + j, me, (*chip, c), src=x_ref) for j, chip in enumerate(chips)]
        for cp in first:
            cp.start()
        passed = [copy(4 + j, (*chip, c), sibling) for j, chip in enumerate(chips)]
        for j, chip in enumerate(chips):
            copy(1 + j, (*chip, c), me).wait_recv()
            passed[j].start()
        copy(0, sibling, me).wait_recv()
        for j, chip in enumerate(chips):
            copy(4 + j, (*chip, 1 - c), me).wait_recv()
        for cp in first + passed:
            cp.wait_send()
        mine.wait()

    return pl.pallas_call(
        body, out_shape=jax.ShapeDtypeStruct((8, m_per, ncol), x_shard.dtype),
        in_specs=[pl.BlockSpec(memory_space=pltpu.VMEM)], out_specs=pl.BlockSpec(memory_space=pltpu.VMEM),
        scratch_shapes=[pltpu.SemaphoreType.DMA((7,)), pltpu.SemaphoreType.DMA((7,)), pltpu.SemaphoreType.DMA],
        compiler_params=pltpu.CompilerParams(vmem_limit_bytes=VMEM_LIMIT), name=name)(x_shard)


def _sum_devices(g8, name):
    _, r, c = g8.shape
    tr = _div_tile(r, c * 4 * 8)

    def body(g_ref, o_ref):
        acc = g_ref[0]
        for dev in range(1, 8):
            acc = acc + g_ref[dev]
        o_ref[...] = acc

    return pl.pallas_call(
        body, out_shape=jax.ShapeDtypeStruct((r, c), F32), grid=(r // tr,),
        in_specs=[pl.BlockSpec((8, tr, c), lambda i: (0, i, 0))], out_specs=pl.BlockSpec((tr, c), lambda i: (i, 0)),
        compiler_params=_cparams(("parallel",)), name=name)(g8)


def _exchange_sibling_halves(arrs, name):
    n = len(arrs)

    def body(*refs):
        ins, outs = refs[:n], refs[n:2 * n]
        send_sems, recv_sems = refs[2 * n:]
        x, y, c = _mesh_pos()
        cps = []
        for k in range(n):
            cp = pltpu.make_async_remote_copy(
                src_ref=_row_half(ins[k], 1 - c, 1), dst_ref=outs[k], send_sem=send_sems.at[k],
                recv_sem=recv_sems.at[k], device_id=(x, y, 1 - c), device_id_type=MESH)
            cp.start()
            cps.append(cp)
        for cp in cps:
            cp.wait()

    return pl.pallas_call(
        body,
        out_shape=[jax.ShapeDtypeStruct((a.shape[0], a.shape[1] // 2, a.shape[2]), a.dtype) for a in arrs],
        in_specs=[HBM_SPEC] * n, out_specs=[HBM_SPEC] * n,
        scratch_shapes=[pltpu.SemaphoreType.DMA((n,)), pltpu.SemaphoreType.DMA((n,))], name=name)(*arrs)


def _chip_partials(arrs, recvs, pos, name):
    n = len(arrs)

    def body(pos_ref, *refs):
        for a_ref, b_ref, o_ref in zip(refs[:n], refs[n:2 * n], refs[2 * n:]):
            o_ref[...] = (a_ref[...] + b_ref[...]).astype(BF16)

    own_specs, recv_specs, shapes = [], [], []
    for arr in arrs:
        nslab, r, c = arr.shape
        own_specs.append(pl.BlockSpec((None, r // 2, c), lambda j, p: (j, p[0], 0)))
        recv_specs.append(pl.BlockSpec((None, r // 2, c), lambda j, p: (j, 0, 0)))
        shapes.append(jax.ShapeDtypeStruct((nslab, r // 2, c), BF16))
    grid_spec = pltpu.PrefetchScalarGridSpec(
        num_scalar_prefetch=1, grid=(N_CHIPS,), in_specs=own_specs + recv_specs, out_specs=recv_specs)
    return pl.pallas_call(
        body, out_shape=shapes, grid_spec=grid_spec,
        compiler_params=_cparams(("parallel",)), name=name)(pos, *arrs, *recvs)


def _chip_exchange_comm(parts, tag):
    n = len(parts)

    def copies(ins, outs, sems):
        send_sems, recv_sems = sems
        x, y, c = _mesh_pos()
        chips = [(1 - x, y), (x, 1 - y), (1 - x, 1 - y)]
        return [pltpu.make_async_remote_copy(
            src_ref=ins[k].at[2 * chip[0] + chip[1]], dst_ref=outs[k].at[j],
            send_sem=send_sems.at[3 * k + j], recv_sem=recv_sems.at[3 * k + j],
            device_id=(*chip, c), device_id_type=MESH) for k in range(n) for j, chip in enumerate(chips)]

    def start(ins, outs, sems):
        for cp in copies(ins, outs, sems):
            cp.start()

    def finish(ins, outs, sems):
        for cp in copies(ins, outs, sems):
            cp.wait()

    return _Comm(
        tag, parts, [jax.ShapeDtypeStruct((3,) + p.shape[1:], p.dtype) for p in parts],
        [pltpu.SemaphoreType.DMA((3 * n,)), pltpu.SemaphoreType.DMA((3 * n,))], start, None, finish)


def _reduce_halves(arrs, recvs, gots, pos, name):
    n = len(arrs)

    def body(pos_ref, *refs):
        for a_ref, b_ref, g_ref, o_ref in zip(refs[:n], refs[n:2 * n], refs[2 * n:3 * n], refs[3 * n:]):
            acc = a_ref[...] + b_ref[...]
            for j in range(3):
                acc = acc + g_ref[j].astype(F32)
            o_ref[...] = acc

    own_specs, recv_specs, got_specs, out_specs, shapes = [], [], [], [], []
    for arr in arrs:
        _, r, c = arr.shape
        own_specs.append(pl.BlockSpec((None, r // 2, c), lambda i, p: (p[1], p[0], 0)))
        recv_specs.append(pl.BlockSpec((None, r // 2, c), lambda i, p: (p[1], 0, 0)))
        got_specs.append(pl.BlockSpec((3, r // 2, c), lambda i, p: (0, 0, 0)))
        out_specs.append(pl.BlockSpec((r // 2, c), lambda i, p: (0, 0)))
        shapes.append(jax.ShapeDtypeStruct((r // 2, c), F32))
    grid_spec = pltpu.PrefetchScalarGridSpec(
        num_scalar_prefetch=1, grid=(1,), in_specs=own_specs + recv_specs + got_specs, out_specs=out_specs)
    return pl.pallas_call(
        body, out_shape=shapes, grid_spec=grid_spec,
        compiler_params=_cparams(("arbitrary",)), name=name)(pos, *arrs, *recvs, *gots)


def _share_halves(halves, name):
    n = len(halves)

    def body(*refs):
        ins, outs = refs[:n], refs[n:2 * n]
        send_sems, recv_sems = refs[2 * n:]
        x, y, c = _mesh_pos()
        cps = []
        for k in range(n):
            cp = pltpu.make_async_remote_copy(
                src_ref=ins[k], dst_ref=outs[k], send_sem=send_sems.at[k], recv_sem=recv_sems.at[k],
                device_id=(x, y, 1 - c), device_id_type=MESH)
            cp.start()
            cps.append(cp)
        for cp in cps:
            cp.wait()

    return pl.pallas_call(
        body, out_shape=[jax.ShapeDtypeStruct(h.shape, h.dtype) for h in halves],
        in_specs=[HBM_SPEC] * n, out_specs=[HBM_SPEC] * n,
        scratch_shapes=[pltpu.SemaphoreType.DMA((n,)), pltpu.SemaphoreType.DMA((n,))], name=name)(*halves)


class _Reduction:
    def __init__(self, arrs, pos, tag):
        self.arrs, self.pos, self.tag = arrs, pos, tag
        self.recv = _exchange_sibling_halves(arrs, "rs_sibling_" + tag)
        self.parts = _chip_partials(arrs, self.recv, pos, "rs_partial_" + tag)
        self.got = None

    def comm(self):
        return _chip_exchange_comm(self.parts, "rs_" + self.tag)

    def end(self):
        if self.got is None:
            self.got = _run_comm(self.comm(), "rs_chips_" + self.tag)
        halves = _reduce_halves(self.arrs, self.recv, self.got, self.pos, "rs_reduce_" + self.tag)
        return halves, _share_halves(halves, "rs_share_" + self.tag)


def _w_in_full(p, l, ws):
    if "w_in" not in p:
        slabs = ws.get((l, "w_in"))
        p["w_in"] = jnp.concatenate([slabs[j] for j in range(N_CHIPS)], axis=1)
    return p["w_in"]


def _layer_fwd(h, l, p, ws, tabs):
    def hosted(host, fn, *args):
        out, got = fn(*args, ws.comm_for(host))
        ws.deposit(got)
        return out

    a, b, sact = hosted("ffn_up", _ffn_up, h, p["ffn1_norm"], ws.get((l, "wg1")), ws.get((l, "wu1")))
    h1 = hosted("ffn_down", _ffn_down, sact, ws.get((l, "wd1")), h)
    ffn1_saved = (a, b, sact)
    n = _rms_fwd(h1, p["mix_norm"], "rms_fwd_mix")
    ssm_w = p["ssm_d"].shape[0]
    q, k, v, u, ga, gs = hosted("in_proj", _in_proj, n, _w_in_full(p, l, ws), tabs, ssm_w)
    o = hosted("attn_fwd", _attn_fwd, q, k, v, p["attn_sinks"])
    y = hosted("ssm_fwd", _ssm_fwd, u, *p["ssm_tabs"], p["ssm_d"])
    yg = _gelu_fwd(y)
    merged = hosted("merge_fwd", _merge_fwd, o, yg, ga, gs, ws.get((l, "w3")))
    h2 = hosted("out_proj", _out_proj, merged, ws.get((l, "w_out")), h1)
    a, b, sact = hosted("ffn_up", _ffn_up, h2, p["ffn2_norm"], ws.get((l, "wg2")), ws.get((l, "wu2")))
    h3 = hosted("ffn_down", _ffn_down, sact, ws.get((l, "wd2")), h2)
    saved = dict(h0=h, h1=h1, h2=h2, ffn1=ffn1_saved, ffn2=(a, b, sact), q=q, k=k, v=v, u=u, ga=ga, gs=gs, o=o, y=y,
                 yg=yg, merged=merged)
    return h3, saved


def _layer_bwd(dh, l, p, ws, s, tabs, pos):
    g = {}
    dh2, g["ffn2_norm"], red_ffn2, _ = _ffn_bwd(
        dh, s["h2"], p["ffn2_norm"], ws.get((l, "wg2")), ws.get((l, "wu2")), ws.get((l, "wd2")), s["ffn2"], pos)
    w3, w_out_w = ws.get((l, "w3")), ws.get((l, "w_out"))
    lp, d = dh2.shape
    d4 = d // N_CHIPS
    dhb = _scale_cast(dh2, 1.0, "mix_dh_cast")
    dw_out = _tn_full(s["merged"], dhb, "mix_dw_out").reshape(N_CHIPS, d4, d)
    dattn, dv, dg, dga, dgs = _merge_bwd(dhb, w_out_w, s["o"], s["yg"], s["ga"], s["gs"], w3)
    (dw_ap,) = _tn_cols(s["o"], [dattn], "mix_dw_ap")
    dw_gv, dw_gg = _tn_cols(s["yg"], [dv, dg], "mix_dw_glu")
    do, dy = _branch_bwd(dattn, dv, dg, w3, s["y"])
    (dq, dk, dvv, dkm, dvm, dsink), _ = _attn_bwd(s["q"], s["k"], s["v"], do, p["attn_sinks"], tabs)
    g["attn_sinks"] = dsink[:, 0]
    (du, dlr, dli, dbr, dbi, dcr, dci, dd), _ = _ssm_bwd(s["u"], dy, *p["ssm_tabs"], p["ssm_d"])
    ngrp = p["ssm_d"].shape[0] // SSM_GROUP
    g["ssm_lam"] = (dlr.reshape(ngrp, SSM_STATE), dli.reshape(ngrp, SSM_STATE),
                    _ssm_untable_b(dbr, ngrp), _ssm_untable_b(dbi, ngrp))
    g["ssm_c_re"] = _ssm_untable_c(dcr, ngrp)
    g["ssm_c_im"] = _ssm_untable_c(dci, ngrp)
    g["ssm_d"] = dd[0]
    dk = dk.at[:BLOCK].add(dkm)
    dvv = dvv.at[:BLOCK].add(dvm)
    dz = jnp.concatenate([dq.astype(BF16), dk.astype(BF16), dvv.astype(BF16), du.astype(BF16), dga, dgs], axis=1)
    n = _rms_fwd(s["h1"], p["mix_norm"], "rms_fwd_mix")
    w_in = _w_in_full(p, l, ws)
    inw = w_in.shape[1]
    dw_in = _tn_full(dz, n, "mix_dw_in", d // 2).reshape(N_CHIPS, inw // N_CHIPS, d)
    red_mix = _Reduction([dw_in, dw_ap, dw_gv, dw_gg, dw_out], pos, "mix")
    dh1, g["mix_norm"] = _in_proj_bwd(dz, w_in, dh2, s["h1"], p["mix_norm"])
    dh0, g["ffn1_norm"], red_ffn1, red_mix.got = _ffn_bwd(
        dh1, s["h0"], p["ffn1_norm"], ws.get((l, "wg1")), ws.get((l, "wu1")), ws.get((l, "wd1")), s["ffn1"],
        pos, red_mix.comm())
    return dh0, g, [red_ffn1, red_mix, red_ffn2]


BIG = ["ffn1_w_gate", "ffn1_w_up", "ffn1_w_down", "w_in", "w_attn_proj", "w_glu_v", "w_glu_g", "w_out",
       "ffn2_w_gate", "ffn2_w_up", "ffn2_w_down"]
TRANSPOSED = ["ffn1_w_gate", "ffn1_w_up", "w_in", "ffn2_w_gate", "ffn2_w_up"]
SMALL = ["ffn1_norm", "mix_norm", "attn_sinks", "ssm_a_re", "ssm_a_im", "ssm_log_dt", "ssm_b_re", "ssm_b_im",
         "ssm_c_re", "ssm_c_im", "ssm_d", "ffn2_norm", "final_norm"]
WEIGHTS = ["meta_tokens", "ffn1_norm", "ffn1_w_gate", "ffn1_w_up", "ffn1_w_down", "mix_norm", "w_in", "attn_sinks",
           "ssm_a_re", "ssm_a_im", "ssm_log_dt", "ssm_b_re", "ssm_b_im", "ssm_c_re", "ssm_c_im", "ssm_d",
           "w_attn_proj", "w_glu_v", "w_glu_g", "w_out", "ffn2_norm", "ffn2_w_gate", "ffn2_w_up", "ffn2_w_down",
           "final_norm"]


def _small_rows(shape):
    rows = -(-math.prod(shape) // LANES)
    return -(-rows // SUBLANES) * SUBLANES


def _pack_small(tree):
    parts = []
    for k in SMALL + ["meta_tokens"]:
        size, rows = math.prod(tree[k].shape), _small_rows(tree[k].shape)
        if size % LANES == 0:
            part = tree[k].reshape(size // LANES, LANES)
        else:
            part = jnp.pad(tree[k].reshape(1, size), ((0, 0), (0, LANES - size)))
        parts.append(jnp.pad(part, ((0, rows - part.shape[0]), (0, 0))))
    return jnp.concatenate(parts, axis=0)


def _unpack_small(packed, like):
    out, off = {}, 0
    for k in SMALL + ["meta_tokens"]:
        size, rows = math.prod(like[k].shape), _small_rows(like[k].shape)
        if size % LANES == 0:
            out[k] = packed[off:off + size // LANES].reshape(like[k].shape)
        else:
            out[k] = packed[off, :size].reshape(like[k].shape)
        off += rows
    return out


def kernel(x, meta_tokens, ffn1_norm, ffn1_w_gate, ffn1_w_up, ffn1_w_down, mix_norm, w_in, attn_sinks, ssm_a_re, ssm_a_im, ssm_log_dt, ssm_b_re, ssm_b_im, ssm_c_re, ssm_c_im, ssm_d, w_attn_proj, w_glu_v, w_glu_g, w_out, ffn2_norm, ffn2_w_gate, ffn2_w_up, ffn2_w_down, final_norm, loss_target, m_meta_tokens, m_ffn1_norm, m_ffn1_w_gate, m_ffn1_w_up, m_ffn1_w_down, m_mix_norm, m_w_in, m_attn_sinks, m_ssm_a_re, m_ssm_a_im, m_ssm_log_dt, m_ssm_b_re, m_ssm_b_im, m_ssm_c_re, m_ssm_c_im, m_ssm_d, m_w_attn_proj, m_w_glu_v, m_w_glu_g, m_w_out, m_ffn2_norm, m_ffn2_w_gate, m_ffn2_w_up, m_ffn2_w_down, m_final_norm, v_meta_tokens, v_ffn1_norm, v_ffn1_w_gate, v_ffn1_w_up, v_ffn1_w_down, v_mix_norm, v_w_in, v_attn_sinks, v_ssm_a_re, v_ssm_a_im, v_ssm_log_dt, v_ssm_b_re, v_ssm_b_im, v_ssm_c_re, v_ssm_c_im, v_ssm_d, v_w_attn_proj, v_w_glu_v, v_w_glu_g, v_w_out, v_ffn2_norm, v_ffn2_w_gate, v_ffn2_w_up, v_ffn2_w_down, v_final_norm):
    args = dict(locals())
    w = {k: args[k] for k in WEIGHTS}
    m = {k: args["m_" + k] for k in WEIGHTS}
    v = {k: args["v_" + k] for k in WEIGHTS}
    depth = ffn1_norm.shape[0]
    seq, d = x.shape[1], x.shape[2]
    lp = seq + BLOCK
    xi, yi, ci = _mesh_pos()
    pos = jnp.stack([ci, 2 * xi + yi]).astype(jnp.int32)

    tabs = _rope_tables(lp)
    (meta_all,) = _all_gather_chips([meta_tokens], "gather_meta")
    meta_full = jnp.concatenate([meta_all[j] for j in range(N_CHIPS)], axis=1)
    layers, pieces = [], []
    for l in range(depth):
        pieces += [
            ((l, "wg1"), ffn1_w_gate[l].astype(BF16)), ((l, "wu1"), ffn1_w_up[l].astype(BF16)),
            ((l, "wd1"), ffn1_w_down[l].astype(BF16)), ((l, "w_in"), w_in[l].astype(BF16)),
            ((l, "w3"), jnp.stack([w_attn_proj[l], w_glu_v[l], w_glu_g[l]]).astype(BF16)),
            ((l, "w_out"), w_out[l].astype(BF16)),
            ((l, "wg2"), ffn2_w_gate[l].astype(BF16)), ((l, "wu2"), ffn2_w_up[l].astype(BF16)),
            ((l, "wd2"), ffn2_w_down[l].astype(BF16))]
        lb_re, lb_im, bb_re, bb_im = _ssm_params(ssm_a_re[l], ssm_a_im[l], ssm_log_dt[l], ssm_b_re[l], ssm_b_im[l])
        ngrp = lb_re.shape[0]
        nt = ngrp // GROUPS_PER_TILE
        ssm_tabs = (lb_re.reshape(nt, 1, TILE_STATES), lb_im.reshape(nt, 1, TILE_STATES),
                    *_ssm_tables(bb_re, bb_im, ssm_c_re[l], ssm_c_im[l]))
        layers.append(dict(
            ffn1_norm=ffn1_norm[l], mix_norm=mix_norm[l], ffn2_norm=ffn2_norm[l], attn_sinks=attn_sinks[l],
            ssm_d=ssm_d[l], ssm_tabs=ssm_tabs))
    ws = _WeightStream(pieces)
    ws.get((0, "wu1"))

    h = jnp.concatenate([jnp.zeros((PAD_FRONT, d), F32), meta_full, x[0]], axis=0)
    saved = []
    for l in range(depth):
        h, s = _layer_fwd(h, l, layers[l], ws, tabs)
        saved.append(s)
    dh, g_final, loss_acc = _loss_head(h, final_norm, loss_target[0])
    loss = lax.psum(loss_acc[0, 0], ("x", "y", "c"))

    grads, reds = [None] * depth, [None] * depth
    for l in reversed(range(depth)):
        dh, grads[l], reds[l] = _layer_bwd(dh, l, layers[l], ws, saved[l], tabs, pos)
    grad_x = dh[BLOCK:][None]
    dmeta_local = dh[PAD_FRONT:BLOCK]

    small = {k: [] for k in SMALL}
    for l in range(depth):
        gl = grads[l]
        _, vjp = jax.vjp(_ssm_params, ssm_a_re[l], ssm_a_im[l], ssm_log_dt[l], ssm_b_re[l], ssm_b_im[l])
        da_re, da_im, dlog_dt, db_re, db_im = vjp(gl["ssm_lam"])
        for k, val in (("ffn1_norm", gl["ffn1_norm"][0]), ("mix_norm", gl["mix_norm"][0]),
                       ("attn_sinks", gl["attn_sinks"]), ("ssm_a_re", da_re), ("ssm_a_im", da_im),
                       ("ssm_log_dt", dlog_dt), ("ssm_b_re", db_re), ("ssm_b_im", db_im),
                       ("ssm_c_re", gl["ssm_c_re"]), ("ssm_c_im", gl["ssm_c_im"]), ("ssm_d", gl["ssm_d"]),
                       ("ffn2_norm", gl["ffn2_norm"][0])):
            small[k].append(val)
    small_local = {k: jnp.stack(vals) for k, vals in small.items() if k != "final_norm"}
    small_local["final_norm"] = g_final[0]
    small_local["meta_tokens"] = dmeta_local
    like = dict(small_local)
    g_small = _sum_devices(_all_gather_devices(_pack_small(small_local), "gather_small_grads"), "sum_small_grads")
    g_small_tree = _unpack_small(g_small, like)
    d4 = d // N_CHIPS
    chip = 2 * xi + yi
    g_meta = lax.dynamic_slice_in_dim(g_small_tree["meta_tokens"], chip * d4, d4, axis=1)

    reduced = []
    for l in range(depth):
        mine, other = [], []
        for red in reds[l]:
            halves, sibling_halves = red.end()
            mine += halves
            other += sibling_halves
        reduced.append((mine, other))

    g_out, delta, new_m, new_v = {}, {}, {}, {}
    for i, k in enumerate(BIG):
        flip = (lambda t: jnp.swapaxes(t, 1, 2)) if k in TRANSPOSED else (lambda t: t)
        outs = _adamw_layers(
            flip(w[k]), flip(m[k]), flip(v[k]), [reduced[l][0][i] for l in range(depth)],
            [reduced[l][1][i] for l in range(depth)], pos, "adamw_" + k)
        g_out[k], delta[k], new_m[k], new_v[k] = [flip(t) for t in outs]
    small_names = SMALL + ["meta_tokens"]
    w_small = {k: w[k] for k in small_names}
    m_small = {k: m[k] for k in small_names}
    v_small = {k: v[k] for k in small_names}
    g_small_local = dict(g_small_tree)
    g_small_local["meta_tokens"] = g_meta
    d_s, m_s, v_s = _adamw_flat(_pack_small(w_small), _pack_small(g_small_local), _pack_small(m_small),
                                _pack_small(v_small), "adamw_small")
    for tree, packed in ((delta, d_s), (new_m, m_s), (new_v, v_s)):
        tree.update(_unpack_small(packed, w_small))
    for k in small_names:
        g_out[k] = g_small_local[k]

    return (loss, grad_x, *[g_out[k] for k in WEIGHTS], *[delta[k] for k in WEIGHTS],
            *[new_m[k] for k in WEIGHTS], *[new_v[k] for k in WEIGHTS])
```

```python
import functools
import math

import jax
import jax.numpy as jnp
from jax import lax
from jax.experimental import pallas as pl
from jax.experimental.pallas import tpu as pltpu

F32 = jnp.float32
BF16 = jnp.bfloat16

N_META = 16
HEAD_DIM = 64
N_Q_HEADS = 8
N_KV_HEADS = 2
Q_PER_KV = N_Q_HEADS // N_KV_HEADS
ATTN_WIDTH = N_Q_HEADS * HEAD_DIM
KV_WIDTH = N_KV_HEADS * HEAD_DIM
BLOCK = 128
PAD_FRONT = BLOCK - N_META
ROPE_THETA = 500000.0
ROT_DIM = HEAD_DIM // 4
SSM_GROUP = 16
SSM_STATE = 64
GROUPS_PER_TILE = 4
TILE_STATES = GROUPS_PER_TILE * SSM_STATE
LANES = 128
SUBLANES = 8
MXU_DIM = 256
EPS = 1e-6
NEG_INF = -1e30
N_CHIPS = 4

ADAM_LR = 0.001
ADAM_B1 = 0.9
ADAM_B2 = 0.999
ADAM_EPS = 1e-08
ADAM_WD = 0.01
ADAM_STEP = 10

VMEM_LIMIT = 56 * 1024 * 1024
MESH = pl.DeviceIdType.MESH


def _cparams(sem=None):
    return pltpu.CompilerParams(dimension_semantics=sem, vmem_limit_bytes=VMEM_LIMIT)


def _row_tile(rows, limit=512):
    best = None
    for t in range(128, limit + 1, 128):
        if rows % t == 0:
            best = t
    assert best is not None, rows
    return best


def _div_tile(rows, row_bytes, max_bytes=1 << 20, mult=8):
    best = None
    for t in range(mult, rows + 1, mult):
        if rows % t == 0 and t * row_bytes <= max_bytes:
            best = t
    if best is None:
        best = rows
    return best


def _dot(a, b, mode):
    if mode == "nn":
        dims = (((1,), (0,)), ((), ()))
    elif mode == "nt":
        dims = (((1,), (1,)), ((), ()))
    else:
        dims = (((0,), (0,)), ((), ()))
    return lax.dot_general(a.astype(BF16), b.astype(BF16), dims, preferred_element_type=F32)


def _sigmoid(x):
    return 1.0 / (1.0 + jnp.exp(-x))


_GELU_C = math.sqrt(2.0 / math.pi)


def _gelu(x):
    return 0.5 * x * (1.0 + jnp.tanh(_GELU_C * (x + 0.044715 * x * x * x)))


def _gelu_grad(x):
    t = jnp.tanh(_GELU_C * (x + 0.044715 * x * x * x))
    return 0.5 * (1.0 + t) + 0.5 * x * (1.0 - t * t) * _GELU_C * (1.0 + 3.0 * 0.044715 * x * x)


class _Comm:
    def __init__(self, tag, ins, out_shapes, sems, start, mid, finish):
        self.tag, self.ins, self.out_shapes, self.sems = tag, list(ins), list(out_shapes), list(sems)
        self.start, self.mid, self.finish = start, mid, finish


HBM_SPEC = pl.BlockSpec(memory_space=pltpu.HBM)


def _hosted_call(body, comm, *, out_shape, grid, in_specs, out_specs, scratch_shapes, sem, name, args):
    out_shape, in_specs, out_specs = list(out_shape), list(in_specs), list(out_specs)
    scratch_shapes = list(scratch_shapes)
    if comm is None:
        res = pl.pallas_call(
            body, out_shape=out_shape, grid=grid, in_specs=in_specs, out_specs=out_specs,
            scratch_shapes=scratch_shapes, compiler_params=_cparams(sem), name=name)(*args)
        return list(res), []
    n_in, n_out, n_sc = len(args), len(out_shape), len(scratch_shapes)
    nci, nco = len(comm.ins), len(comm.out_shapes)
    total = math.prod(grid)

    def wrapped(*refs):
        in_refs, cin = refs[:n_in], refs[n_in:n_in + nci]
        o0 = n_in + nci
        out_refs, cout = refs[o0:o0 + n_out], refs[o0 + n_out:o0 + n_out + nco]
        s0 = o0 + n_out + nco
        sc, csem = refs[s0:s0 + n_sc], refs[s0 + n_sc:]
        lin = 0
        for dim, size in enumerate(grid):
            lin = lin * size + pl.program_id(dim)

        @pl.when(lin == 0)
        def _():
            comm.start(cin, cout, csem)

        if comm.mid is not None:
            @pl.when(lin == total // 2)
            def _():
                comm.mid(cin, cout, csem)

        body(*in_refs, *out_refs, *sc)

        @pl.when(lin == total - 1)
        def _():
            comm.finish(cin, cout, csem)

    res = pl.pallas_call(
        wrapped, out_shape=out_shape + comm.out_shapes, grid=grid,
        in_specs=in_specs + [HBM_SPEC] * nci, out_specs=out_specs + [HBM_SPEC] * nco,
        scratch_shapes=scratch_shapes + comm.sems,
        compiler_params=_cparams(("arbitrary",) * len(grid)), name=name + "_" + comm.tag)(*args, *comm.ins)
    return list(res[:n_out]), list(res[n_out:])


def _matmul(name, grid, k_axis, ins, in_specs, pairs, acc_shapes, epilogue, out_shapes, out_specs, sem, comm=None):
    n_in, n_out, n_acc = len(ins), len(out_shapes), len(acc_shapes)

    def body(*refs):
        in_refs = refs[:n_in]
        out_refs = refs[n_in:n_in + n_out]
        acc_refs = refs[n_in + n_out:]
        if k_axis is None:
            accs = [None] * n_acc
            for ia, ib, mode, iacc in pairs:
                d = _dot(in_refs[ia][...], in_refs[ib][...], mode)
                accs[iacc] = d if accs[iacc] is None else accs[iacc] + d
            epilogue(accs, in_refs, out_refs)
            return
        k = pl.program_id(k_axis)

        @pl.when(k == 0)
        def _():
            for r in acc_refs:
                r[...] = jnp.zeros_like(r)

        for ia, ib, mode, iacc in pairs:
            acc_refs[iacc][...] += _dot(in_refs[ia][...], in_refs[ib][...], mode)

        @pl.when(k == pl.num_programs(k_axis) - 1)
        def _():
            epilogue([r[...] for r in acc_refs], in_refs, out_refs)

    scratch = [] if k_axis is None else [pltpu.VMEM(s, F32) for s in acc_shapes]
    outs, couts = _hosted_call(
        body, comm, out_shape=out_shapes, grid=grid, in_specs=in_specs, out_specs=out_specs,
        scratch_shapes=scratch, sem=sem, name=name, args=ins)
    return outs if comm is None else (outs, couts)


def _rms_fwd(h, g, name):
    lp, d = h.shape
    tm = _row_tile(lp)

    def body(h_ref, g_ref, n_ref):
        x = h_ref[...]
        r = lax.rsqrt(jnp.mean(x * x, axis=-1, keepdims=True) + EPS)
        n_ref[...] = (x * r * g_ref[...]).astype(BF16)

    return pl.pallas_call(
        body, out_shape=jax.ShapeDtypeStruct((lp, d), BF16), grid=(lp // tm,),
        in_specs=[pl.BlockSpec((tm, d), lambda i: (i, 0)), pl.BlockSpec((1, d), lambda i: (0, 0))],
        out_specs=pl.BlockSpec((tm, d), lambda i: (i, 0)),
        compiler_params=_cparams(("parallel",)), name=name)(h, g.reshape(1, d))


def _rms_bwd_math(dn, x, g):
    r = lax.rsqrt(jnp.mean(x * x, axis=-1, keepdims=True) + EPS)
    xh = x * r
    dxh = dn * g
    dx = r * (dxh - xh * jnp.mean(dxh * xh, axis=-1, keepdims=True))
    return dx, dn * xh


def _scale_cast(x, scale, name):
    lp, d = x.shape
    tm = _row_tile(lp)

    def body(x_ref, o_ref):
        o_ref[...] = (x_ref[...] * scale).astype(BF16)

    return pl.pallas_call(
        body, out_shape=jax.ShapeDtypeStruct((lp, d), BF16), grid=(lp // tm,),
        in_specs=[pl.BlockSpec((tm, d), lambda i: (i, 0))], out_specs=pl.BlockSpec((tm, d), lambda i: (i, 0)),
        compiler_params=_cparams(("parallel",)), name=name)(x)


def _ffn_up(h, gain, wgt, wut, comm=None):
    lp, d = h.shape
    fp = wgt.shape[1]
    tm = _row_tile(lp)
    n = _rms_fwd(h, gain, "rms_fwd_ffn")

    def up_epi(accs, in_refs, out_refs):
        a, b = accs
        out_refs[0][...] = a.astype(BF16)
        out_refs[1][...] = b.astype(BF16)
        out_refs[2][...] = (a * _sigmoid(a) * b).astype(BF16)

    act = jax.ShapeDtypeStruct((lp, N_CHIPS * fp), BF16)
    w_spec = pl.BlockSpec((None, fp, d), lambda j, i: (j, 0, 0))
    res = _matmul(
        "ffn_up", (N_CHIPS, lp // tm), None, [n, wgt, wut],
        [pl.BlockSpec((tm, d), lambda j, i: (i, 0)), w_spec, w_spec],
        [(0, 1, "nt", 0), (0, 2, "nt", 1)], [(tm, fp)] * 2, up_epi,
        [act, act, act], [pl.BlockSpec((tm, fp), lambda j, i: (i, j))] * 3,
        ("parallel", "parallel"), comm)
    return (tuple(res), []) if comm is None else (tuple(res[0]), res[1])


def _ffn_down(s, wd, h, comm=None):
    lp, d = h.shape
    ff = s.shape[1]
    tm = _row_tile(lp)

    def down_epi(accs, in_refs, out_refs):
        out_refs[0][...] = in_refs[2][...] + 0.5 * accs[0]

    res = _matmul(
        "ffn_down", (lp // tm,), None, [s, wd.reshape(ff, d), h],
        [pl.BlockSpec((tm, ff), lambda i: (i, 0)), pl.BlockSpec((ff, d), lambda i: (0, 0)),
         pl.BlockSpec((tm, d), lambda i: (i, 0))],
        [(0, 1, "nn", 0)], [(tm, d)], down_epi,
        [jax.ShapeDtypeStruct((lp, d), F32)], [pl.BlockSpec((tm, d), lambda i: (i, 0))],
        ("parallel",), comm)
    return (res[0], []) if comm is None else (res[0][0], res[1])


def _tn_tiles(lp):
    return _row_tile(lp, 1408)


def _ffn_bwd(dh, h_in, gain, wgt, wut, wd, f4, saved, pos, comm=None):
    a, b, s = saved
    lp, d = h_in.shape
    fp = wgt.shape[1]
    ff = N_CHIPS * fp
    tm = _row_tile(lp)
    ni = lp // tm
    tk = _tn_tiles(lp)
    nk = lp // tk
    n = _rms_fwd(h_in, gain, "rms_fwd_ffn")

    def ds_epi(accs, in_refs, out_refs):
        ds = 0.5 * accs[0]
        av = in_refs[2][...].astype(F32)
        bv = in_refs[3][...].astype(F32)
        sg = _sigmoid(av)
        out_refs[0][...] = (ds * bv * sg * (1.0 + av * (1.0 - sg))).astype(BF16)
        out_refs[1][...] = (ds * av * sg).astype(BF16)

    act = jax.ShapeDtypeStruct((lp, ff), BF16)
    col_spec = pl.BlockSpec((tm, fp), lambda j, i: (i, j))
    res = _matmul(
        "ffn_bwd_ds", (N_CHIPS, ni), None, [dh, wd, a, b],
        [pl.BlockSpec((tm, d), lambda j, i: (i, 0)), pl.BlockSpec((None, fp, d), lambda j, i: (j, 0, 0)),
         col_spec, col_spec],
        [(0, 1, "nt", 0)], [(tm, fp)], ds_epi, [act, act], [col_spec, col_spec], ("parallel", "parallel"),
        comm)
    (da, db), couts = (res, []) if comm is None else res

    dw_shape = jax.ShapeDtypeStruct((N_CHIPS, f4, d), F32)
    dw_spec = pl.BlockSpec((None, f4, d), lambda j, k: (j, 0, 0))
    in_col = pl.BlockSpec((tk, fp), lambda j, k: (k, j))
    in_row = pl.BlockSpec((tk, d), lambda j, k: (k, 0))

    def dwd_epi(accs, in_refs, out_refs):
        out_refs[0][...] = 0.5 * accs[0][:f4]

    (dwd,) = _matmul(
        "ffn_dwd", (N_CHIPS, nk), 1, [s, dh], [in_col, in_row],
        [(0, 1, "tn", 0)], [(fp, d)], dwd_epi, [dw_shape], [dw_spec], ("parallel", "arbitrary"))

    def dwgu_epi(accs, in_refs, out_refs):
        for acc, o in zip(accs, out_refs):
            o[...] = acc[:f4]

    dwg, dwu = _matmul(
        "ffn_dwgu", (N_CHIPS, nk), 1, [n, da, db], [in_row, in_col, in_col],
        [(1, 0, "tn", 0), (2, 0, "tn", 1)], [(fp, d)] * 2, dwgu_epi,
        [dw_shape, dw_shape], [dw_spec, dw_spec], ("parallel", "arbitrary"))

    def dn_epi(accs, in_refs, out_refs):
        i = pl.program_id(0)
        dx, dgrow = _rms_bwd_math(accs[0], in_refs[5][...], in_refs[6][...])
        out_refs[0][...] = in_refs[4][...] + dx

        @pl.when(i == 0)
        def _():
            out_refs[1][...] = jnp.zeros_like(out_refs[1])

        out_refs[1][...] += jnp.sum(dgrow, axis=0, keepdims=True)

    red = _Reduction([dwg, dwu, dwd], pos, "ffn")
    row_spec = pl.BlockSpec((tm, d), lambda i: (i, 0))
    act_spec = pl.BlockSpec((tm, ff), lambda i: (i, 0))
    w_spec = pl.BlockSpec((ff, d), lambda i: (0, 0))
    one_spec = pl.BlockSpec((1, d), lambda i: (0, 0))
    (dh_in, dgain), red.got = _matmul(
        "ffn_bwd_dn", (ni,), None, [da, wgt.reshape(ff, d), db, wut.reshape(ff, d), dh, h_in, gain.reshape(1, d)],
        [act_spec, w_spec, act_spec, w_spec, row_spec, row_spec, one_spec],
        [(0, 1, "nn", 0), (2, 3, "nn", 0)], [(tm, d)], dn_epi,
        [jax.ShapeDtypeStruct((lp, d), F32), jax.ShapeDtypeStruct((1, d), F32)],
        [row_spec, one_spec], ("arbitrary",), red.comm())
    return dh_in, dgain, red, couts


def _rope_tables(lp):
    pos = jnp.arange(lp, dtype=F32) - float(PAD_FRONT)
    inv_freq = ROPE_THETA ** (-jnp.arange(0, ROT_DIM, 2, dtype=F32) / ROT_DIM)
    ang = pos[:, None] * inv_freq[None, :]
    cos, sin = jnp.cos(ang), jnp.sin(ang)
    half = ROT_DIM // 2
    ones = jnp.ones((lp, HEAD_DIM - ROT_DIM), F32)
    zeros_h = jnp.zeros((lp, half), F32)
    zeros_r = jnp.zeros((lp, HEAD_DIM - ROT_DIM), F32)
    c = jnp.concatenate([cos, cos, ones], axis=1)
    s1 = jnp.concatenate([-sin, zeros_h, zeros_r], axis=1)
    s2 = jnp.concatenate([zeros_h, sin, zeros_r], axis=1)
    reps = LANES // HEAD_DIM
    return jnp.stack([jnp.tile(c, (1, reps)), jnp.tile(s1, (1, reps)), jnp.tile(s2, (1, reps))])


def _rope(x, c, s1, s2):
    half = ROT_DIM // 2
    outs = []
    for ch in range(x.shape[1] // LANES):
        xc = x[:, ch * LANES:(ch + 1) * LANES]
        outs.append(xc * c + pltpu.roll(xc, LANES - half, 1) * s1 + pltpu.roll(xc, half, 1) * s2)
    return outs[0] if len(outs) == 1 else jnp.concatenate(outs, axis=1)


def _rope_t(dy, c, s1, s2):
    half = ROT_DIM // 2
    outs = []
    for ch in range(dy.shape[1] // LANES):
        dc = dy[:, ch * LANES:(ch + 1) * LANES]
        outs.append(dc * c + pltpu.roll(dc * s1, half, 1) + pltpu.roll(dc * s2, LANES - half, 1))
    return outs[0] if len(outs) == 1 else jnp.concatenate(outs, axis=1)


def _in_proj(n, w_in, tabs, ssm_w, comm=None):
    lp, d = n.shape
    inw = w_in.shape[1]
    tm = _row_tile(lp)
    o1 = ATTN_WIDTH
    o2 = o1 + KV_WIDTH
    o3 = o2 + KV_WIDTH
    o4 = o3 + ssm_w
    o5 = o4 + d

    def epi(accs, in_refs, out_refs):
        z = accs[0]
        c, s1, s2 = in_refs[2][0], in_refs[2][1], in_refs[2][2]
        out_refs[0][...] = _rope(z[:, :o1], c, s1, s2).astype(BF16)
        out_refs[1][...] = _rope(z[:, o1:o2], c, s1, s2).astype(BF16)
        out_refs[2][...] = z[:, o2:o3].astype(BF16)
        out_refs[3][...] = z[:, o3:o4]
        out_refs[4][...] = z[:, o4:o5]
        out_refs[5][...] = z[:, o5:]

    def rs(w, dt):
        return jax.ShapeDtypeStruct((lp, w), dt), pl.BlockSpec((tm, w), lambda i: (i, 0))

    shapes, specs = zip(rs(o1, BF16), rs(KV_WIDTH, BF16), rs(KV_WIDTH, BF16), rs(ssm_w, F32), rs(d, F32), rs(d, F32))
    res = _matmul(
        "mix_in_proj", (lp // tm,), None, [n, w_in, tabs],
        [pl.BlockSpec((tm, d), lambda i: (i, 0)), pl.BlockSpec((d, inw), lambda i: (0, 0)),
         pl.BlockSpec((3, tm, LANES), lambda i: (0, i, 0))],
        [(0, 1, "nn", 0)], [(tm, inw)], epi, list(shapes), list(specs), ("parallel",), comm)
    return (res, []) if comm is None else res


def _attn_mask(b):
    rows = lax.broadcasted_iota(jnp.int32, (BLOCK, 3 * BLOCK), 0)
    cols = lax.broadcasted_iota(jnp.int32, (BLOCK, 3 * BLOCK), 1)
    qpos = b * BLOCK + rows - PAD_FRONT
    kpos = (b - 1) * BLOCK + cols - PAD_FRONT
    dist = qpos - kpos
    band = (cols < 2 * BLOCK) & (kpos >= N_META) & (dist >= 0) & (dist < BLOCK)
    mrow = cols - 2 * BLOCK
    meta = (mrow >= PAD_FRONT) & ((mrow - PAD_FRONT) <= qpos)
    return band | meta


def _attn_probs(qh, kk, mask, sink):
    s = _dot(qh, kk, "nt") * (HEAD_DIM ** -0.5)
    s = jnp.where(mask, s, NEG_INF)
    m = jnp.maximum(jnp.max(s, axis=-1, keepdims=True), sink)
    e = jnp.exp(s - m)
    es = jnp.exp(sink - m)
    z = jnp.sum(e, axis=-1, keepdims=True) + es
    inv = 1.0 / z
    return e * inv, es * inv


def _head(ref_or_val, h):
    return ref_or_val[:, h * HEAD_DIM:(h + 1) * HEAD_DIM]


def _attn_fwd(q, k, v, sinks, comm=None):
    lp = q.shape[0]
    nb = lp // BLOCK

    def body(sink_ref, q_ref, kp_ref, kc_ref, km_ref, vp_ref, vc_ref, vm_ref, o_ref):
        b = pl.program_id(0)
        mask = _attn_mask(b)
        for hk in range(N_KV_HEADS):
            kk = jnp.concatenate([_head(kp_ref, hk), _head(kc_ref, hk), _head(km_ref, hk)], axis=0)
            vv = jnp.concatenate([_head(vp_ref, hk), _head(vc_ref, hk), _head(vm_ref, hk)], axis=0)
            for g in range(Q_PER_KV):
                h = hk * Q_PER_KV + g
                p, _ = _attn_probs(_head(q_ref, h), kk, mask, sink_ref[h])
                o_ref[:, h * HEAD_DIM:(h + 1) * HEAD_DIM] = _dot(p, vv, "nn").astype(BF16)

    cur = lambda b: (b, 0)
    prev = lambda b: (jnp.maximum(b - 1, 0), 0)
    first = lambda b: (0, 0)
    kvs = lambda f: pl.BlockSpec((BLOCK, KV_WIDTH), f)
    (o,), couts = _hosted_call(
        body, comm, out_shape=[jax.ShapeDtypeStruct((lp, ATTN_WIDTH), BF16)], grid=(nb,),
        in_specs=[pl.BlockSpec(memory_space=pltpu.SMEM), pl.BlockSpec((BLOCK, ATTN_WIDTH), cur),
                  kvs(prev), kvs(cur), kvs(first), kvs(prev), kvs(cur), kvs(first)],
        out_specs=[pl.BlockSpec((BLOCK, ATTN_WIDTH), cur)], scratch_shapes=[],
        sem=("parallel",), name="attn_fwd", args=(sinks, q, k, k, k, v, v, v))
    return o, couts


def _attn_bwd(q, k, v, do, sinks, tabs, comm=None):
    lp = q.shape[0]
    nb = lp // BLOCK
    scale = HEAD_DIM ** -0.5

    def body(sink_ref, q_ref, do_ref, kp_ref, kc_ref, km_ref, vp_ref, vc_ref, vm_ref, tq_ref, tk_ref, t0_ref,
             dq_ref, dk_ref, dv_ref, dkm_ref, dvm_ref, dsink_ref,
             dq_s, dkk_s, dvv_s, ck_s, cv_s, mk_s, mv_s):
        b = pl.program_id(0)

        @pl.when(b == 0)
        def _():
            for r in (ck_s, cv_s, mk_s, mv_s, dsink_ref):
                r[...] = jnp.zeros_like(r)

        @pl.when(b < nb)
        def _():
            mask = _attn_mask(b)
            for hk in range(N_KV_HEADS):
                kk = jnp.concatenate([_head(kp_ref, hk), _head(kc_ref, hk), _head(km_ref, hk)], axis=0)
                vv = jnp.concatenate([_head(vp_ref, hk), _head(vc_ref, hk), _head(vm_ref, hk)], axis=0)
                dkk = jnp.zeros((3 * BLOCK, HEAD_DIM), F32)
                dvv = jnp.zeros((3 * BLOCK, HEAD_DIM), F32)
                for g in range(Q_PER_KV):
                    h = hk * Q_PER_KV + g
                    qh = _head(q_ref, h)
                    doh = _head(do_ref, h)
                    p, ps = _attn_probs(qh, kk, mask, sink_ref[h])
                    dp = _dot(doh, vv, "nt")
                    delta = jnp.sum(p * dp, axis=-1, keepdims=True)
                    ds = (p * (dp - delta)).astype(BF16)
                    dsink_ref[h:h + 1, :] += jnp.zeros((1, LANES), F32) - jnp.sum(ps * delta)
                    dq_s[:, h * HEAD_DIM:(h + 1) * HEAD_DIM] = _dot(ds, kk, "nn") * scale
                    dkk = dkk + _dot(ds, qh, "tn") * scale
                    dvv = dvv + _dot(p, doh, "tn")
                dkk_s[:, hk * HEAD_DIM:(hk + 1) * HEAD_DIM] = dkk
                dvv_s[:, hk * HEAD_DIM:(hk + 1) * HEAD_DIM] = dvv
            dq_ref[...] = _rope_t(dq_s[...], tq_ref[0], tq_ref[1], tq_ref[2])
            dk_ref[...] = _rope_t(ck_s[...] + dkk_s[0:BLOCK, :], tk_ref[0], tk_ref[1], tk_ref[2])
            dv_ref[...] = cv_s[...] + dvv_s[0:BLOCK, :]
            ck_s[...] = dkk_s[BLOCK:2 * BLOCK, :]
            cv_s[...] = dvv_s[BLOCK:2 * BLOCK, :]
            mk_s[...] += dkk_s[2 * BLOCK:, :]
            mv_s[...] += dvv_s[2 * BLOCK:, :]

        @pl.when(b == nb)
        def _():
            dk_ref[...] = _rope_t(ck_s[...], tk_ref[0], tk_ref[1], tk_ref[2])
            dv_ref[...] = cv_s[...]
            dkm_ref[...] = _rope_t(mk_s[...], t0_ref[0], t0_ref[1], t0_ref[2])
            dvm_ref[...] = mv_s[...]

    cur = lambda b: (jnp.minimum(b, nb - 1), 0)
    prev = lambda b: (jnp.clip(b - 1, 0, nb - 1), 0)
    first = lambda b: (0, 0)
    kvs = lambda f: pl.BlockSpec((BLOCK, KV_WIDTH), f)
    tab = lambda f: pl.BlockSpec((3, BLOCK, LANES), lambda b: (0,) + f(b)[:1] + (0,))
    kv_out = lambda b: (jnp.maximum(b - 1, 0), 0)
    return _hosted_call(
        body, comm,
        out_shape=[jax.ShapeDtypeStruct((lp, ATTN_WIDTH), F32), jax.ShapeDtypeStruct((lp, KV_WIDTH), F32),
                   jax.ShapeDtypeStruct((lp, KV_WIDTH), F32), jax.ShapeDtypeStruct((BLOCK, KV_WIDTH), F32),
                   jax.ShapeDtypeStruct((BLOCK, KV_WIDTH), F32), jax.ShapeDtypeStruct((N_Q_HEADS, LANES), F32)],
        grid=(nb + 1,),
        in_specs=[pl.BlockSpec(memory_space=pltpu.SMEM), pl.BlockSpec((BLOCK, ATTN_WIDTH), cur),
                  pl.BlockSpec((BLOCK, ATTN_WIDTH), cur),
                  kvs(prev), kvs(cur), kvs(first), kvs(prev), kvs(cur), kvs(first),
                  tab(cur), tab(kv_out), tab(first)],
        out_specs=[pl.BlockSpec((BLOCK, ATTN_WIDTH), cur), kvs(kv_out), kvs(kv_out), kvs(first), kvs(first),
                   pl.BlockSpec((N_Q_HEADS, LANES), first)],
        scratch_shapes=[pltpu.VMEM((BLOCK, ATTN_WIDTH), F32), pltpu.VMEM((3 * BLOCK, KV_WIDTH), F32),
                        pltpu.VMEM((3 * BLOCK, KV_WIDTH), F32), pltpu.VMEM((BLOCK, KV_WIDTH), F32),
                        pltpu.VMEM((BLOCK, KV_WIDTH), F32), pltpu.VMEM((BLOCK, KV_WIDTH), F32),
                        pltpu.VMEM((BLOCK, KV_WIDTH), F32)],
        sem=("arbitrary",), name="attn_bwd", args=(sinks, q, do, k, k, k, v, v, v, tabs, tabs, tabs))


def _cmul(ar, ai, br, bi):
    return ar * br - ai * bi, ar * bi + ai * br


def _cpow(lr, li, n):
    rr = ri = None
    br, bi = lr, li
    while n:
        if n & 1:
            rr, ri = (br, bi) if rr is None else _cmul(rr, ri, br, bi)
        n >>= 1
        if n:
            br, bi = _cmul(br, bi, br, bi)
    return rr, ri


def _shift_rows(x, d, reverse):
    rows = lax.broadcasted_iota(jnp.int32, x.shape, 0)
    if not reverse:
        return jnp.where(rows >= d, pltpu.roll(x, d, 0), 0.0)
    return jnp.where(rows < SUBLANES - d, pltpu.roll(x, SUBLANES - d, 0), 0.0)


def _sublane_powers(mr, mi, reverse):
    rows = lax.broadcasted_iota(jnp.int32, mr.shape, 0)
    e = SUBLANES - 1 - rows if reverse else rows
    pr, pi = jnp.ones_like(mr), jnp.zeros_like(mr)
    br, bi = mr, mi
    for d in (1, 2, 4):
        tr, ti = _cmul(pr, pi, br, bi)
        on = (e & d) != 0
        pr, pi = jnp.where(on, tr, pr), jnp.where(on, ti, pi)
        if d < 4:
            br, bi = _cmul(br, bi, br, bi)
    return pr, pi


def _inclusive_prefix(er, ei, mr, mi, reverse):
    ir, ii, pr, pi = er, ei, mr, mi
    for d in (1, 2, 4):
        tr, ti = _cmul(pr, pi, _shift_rows(ir, d, reverse), _shift_rows(ii, d, reverse))
        ir, ii = ir + tr, ii + ti
        if d < 4:
            pr, pi = _cmul(pr, pi, pr, pi)
    return ir, ii


def _chain_rows(a, t, seg):
    return pl.ds(a * SUBLANES * seg + t, SUBLANES, stride=seg)


def _seg_scan(xr_ref, xi_ref, lam, seg, nchain, reverse, store, init, extra=None):
    nt = len(lam)
    acc0 = () if extra is None else extra[1]

    def step(i, carry):
        hs, acc = carry
        t = seg - 1 - i if reverse else i
        out = []
        for a in range(nchain):
            sl = _chain_rows(a, t, seg)
            for j in range(nt):
                lr, li = lam[j]
                k = 2 * (a * nt + j)
                hr, hi = hs[k], hs[k + 1]
                nr = lr * hr - li * hi + xr_ref[j, sl, :]
                ni = lr * hi + li * hr + xi_ref[j, sl, :]
                if store:
                    xr_ref[j, sl, :] = nr
                    xi_ref[j, sl, :] = ni
                if extra is not None:
                    acc = extra[0](t, a, j, nr, ni, acc)
                out += [nr, ni]
        return tuple(out), acc

    return lax.fori_loop(0, seg, step, (tuple(init), acc0))


def _ssm_scan(xr_ref, xi_ref, lam, seg, nchain, reverse, extra=None):
    nt = len(lam)
    zero = [jnp.zeros((SUBLANES, LANES), F32)] * (2 * nt * nchain)
    ends, _ = _seg_scan(xr_ref, xi_ref, lam, seg, nchain, reverse, False, zero)
    init = [None] * (2 * nt * nchain)
    last = 0 if reverse else SUBLANES - 1
    for j in range(nt):
        mr, mi = _cpow(lam[j][0], lam[j][1], seg)
        m8r, m8i = _cpow(mr, mi, SUBLANES)
        pwr, pwi = _sublane_powers(mr, mi, reverse)
        gr = gi = jnp.zeros((SUBLANES, LANES), F32)
        for a in (reversed(range(nchain)) if reverse else range(nchain)):
            k = 2 * (a * nt + j)
            incr, inci = _inclusive_prefix(ends[k], ends[k + 1], mr, mi, reverse)
            tr, ti = _cmul(pwr, pwi, gr, gi)
            init[k] = _shift_rows(incr, 1, reverse) + tr
            init[k + 1] = _shift_rows(inci, 1, reverse) + ti
            g2r, g2i = _cmul(m8r, m8i, gr, gi)
            gr = g2r + jnp.broadcast_to(incr[last:last + 1, :], gr.shape)
            gi = g2i + jnp.broadcast_to(inci[last:last + 1, :], gi.shape)
    _, acc = _seg_scan(xr_ref, xi_ref, lam, seg, nchain, reverse, True, init, extra)
    return acc


def _diag_mask():
    steps = LANES // SSM_GROUP // GROUPS_PER_TILE
    return (jnp.eye(steps, dtype=F32)[:, None, :, None] * jnp.eye(GROUPS_PER_TILE, dtype=F32)[None, :, None, :])


def _ssm_tables(bb_re, bb_im, c_re, c_im):
    g = bb_re.shape[0]
    nt = g // GROUPS_PER_TILE
    steps = LANES // SSM_GROUP // GROUPS_PER_TILE
    mask = _diag_mask()

    def b_tab(bb):
        x = bb.reshape(nt // steps, steps, GROUPS_PER_TILE, SSM_STATE, SSM_GROUP)
        x = jnp.transpose(x, (0, 1, 4, 2, 3))[:, :, None, None]
        m = jnp.transpose(mask, (0, 2, 3, 1))[None, :, :, :, None, :, None]
        return (x * m).reshape(nt, LANES, TILE_STATES)

    def c_tab(c):
        x = c.reshape(nt // steps, steps, GROUPS_PER_TILE, SSM_GROUP, SSM_STATE)
        x = jnp.transpose(x, (0, 1, 2, 4, 3))[:, :, :, :, None, None]
        m = mask[None, :, :, None, :, :, None]
        return (x * m).reshape(nt, TILE_STATES, LANES)

    return b_tab(bb_re), b_tab(bb_im), c_tab(c_re), c_tab(c_im)


def _ssm_untable_b(db, g):
    nt = g // GROUPS_PER_TILE
    steps = LANES // SSM_GROUP // GROUPS_PER_TILE
    x = db.reshape(nt // steps, steps, GROUPS_PER_TILE, SSM_STATE, steps, GROUPS_PER_TILE, SSM_GROUP)
    m = _diag_mask()[None, :, :, None, :, :, None]
    return jnp.sum(x * m, axis=(4, 5)).reshape(g, SSM_STATE, SSM_GROUP)


def _ssm_untable_c(dc, g):
    nt = g // GROUPS_PER_TILE
    steps = LANES // SSM_GROUP // GROUPS_PER_TILE
    x = dc.reshape(nt // steps, steps, steps, GROUPS_PER_TILE, SSM_GROUP, GROUPS_PER_TILE, SSM_STATE)
    m = jnp.transpose(_diag_mask(), (0, 2, 3, 1))[None, :, :, :, None, :, None]
    out = jnp.sum(x * m, axis=(2, 3))
    return jnp.transpose(out, (0, 1, 3, 2, 4)).reshape(g, SSM_GROUP, SSM_STATE)


def _lam_tiles(lam_ref):
    out = []
    for j in range(TILE_STATES // LANES):
        out.append(jnp.broadcast_to(lam_ref[:, j * LANES:(j + 1) * LANES], (SUBLANES, LANES)))
    return out


def _scan_chains(lp):
    for n in (4, 2, 1):
        if lp % (SUBLANES * n) == 0 and (lp // SUBLANES) % 16 == 0:
            return n
    raise ValueError(lp)


def _split_tiles(dst_ref, rows, val):
    for j in range(val.shape[1] // LANES):
        dst_ref[j, rows, :] = val[:, j * LANES:(j + 1) * LANES]


def _cat_tiles(src_ref, rows):
    njt = src_ref.shape[0]
    return jnp.concatenate([src_ref[j, rows, :] for j in range(njt)], axis=1).astype(BF16)


def _ssm_fwd(u, lam_re, lam_im, tb_re, tb_im, tc_re, tc_im, d_skip, comm=None):
    lp, w = u.shape
    nt = tb_re.shape[0]
    nchain = _scan_chains(lp)
    seg = lp // (SUBLANES * nchain)
    chunk = lp // SUBLANES
    njt = TILE_STATES // LANES

    def body(u_ref, lr_ref, li_ref, br_ref, bi_ref, cr_ref, ci_ref, d_ref, y_ref, xr, xi):
        t = pl.program_id(0)
        for s in range(SUBLANES):
            rs = pl.ds(s * chunk, chunk)
            ub = u_ref[rs, :].astype(BF16)
            _split_tiles(xr, rs, _dot(ub, br_ref[...], "nn"))
            _split_tiles(xi, rs, _dot(ub, bi_ref[...], "nn"))
        lrs, lis = _lam_tiles(lr_ref), _lam_tiles(li_ref)
        _ssm_scan(xr, xi, list(zip(lrs, lis)), seg, nchain, False)
        for s in range(SUBLANES):
            rs = pl.ds(s * chunk, chunk)
            y = _dot(_cat_tiles(xr, rs), cr_ref[...], "nn") - _dot(_cat_tiles(xi, rs), ci_ref[...], "nn")

            @pl.when(t % 2 == 0)
            def _():
                y_ref[rs, :] = y + d_ref[...] * u_ref[rs, :]

            @pl.when(t % 2 == 1)
            def _():
                y_ref[rs, :] += y

    blk = pl.BlockSpec((lp, LANES), lambda t: (0, t // 2))
    lam_spec = pl.BlockSpec((None, 1, TILE_STATES), lambda t: (t, 0, 0))
    b_spec = pl.BlockSpec((None, LANES, TILE_STATES), lambda t: (t, 0, 0))
    c_spec = pl.BlockSpec((None, TILE_STATES, LANES), lambda t: (t, 0, 0))
    (y,), couts = _hosted_call(
        body, comm, out_shape=[jax.ShapeDtypeStruct((lp, w), F32)], grid=(nt,),
        in_specs=[blk, lam_spec, lam_spec, b_spec, b_spec, c_spec, c_spec,
                  pl.BlockSpec((1, LANES), lambda t: (0, t // 2))],
        out_specs=[blk],
        scratch_shapes=[pltpu.VMEM((njt, lp, LANES), F32), pltpu.VMEM((njt, lp, LANES), F32)],
        sem=("arbitrary",), name="ssm_fwd",
        args=(u, lam_re, lam_im, tb_re, tb_im, tc_re, tc_im, d_skip.reshape(1, w)))
    return y, couts


def _ssm_bwd(u, dy, lam_re, lam_im, tb_re, tb_im, tc_re, tc_im, d_skip, comm=None):
    lp, w = u.shape
    nt = tb_re.shape[0]
    nchain = _scan_chains(lp)
    seg = lp // (SUBLANES * nchain)
    chunk = lp // SUBLANES
    njt = TILE_STATES // LANES
    tbt_re, tbt_im = jnp.swapaxes(tb_re, 1, 2), jnp.swapaxes(tb_im, 1, 2)
    tct_re, tct_im = jnp.swapaxes(tc_re, 1, 2), jnp.swapaxes(tc_im, 1, 2)

    def body(u_ref, dy_ref, lr_ref, li_ref, br_ref, bi_ref, btr_ref, bti_ref, ctr_ref, cti_ref, d_ref,
             du_ref, dlr_ref, dli_ref, dbr_ref, dbi_ref, dcr_ref, dci_ref, dd_ref, hr, hi, ar, ai):
        t = pl.program_id(0)
        lrs, lis = _lam_tiles(lr_ref), _lam_tiles(li_ref)
        for s in range(SUBLANES):
            rs = pl.ds(s * chunk, chunk)
            ub = u_ref[rs, :].astype(BF16)
            dyb = dy_ref[rs, :].astype(BF16)
            _split_tiles(hr, rs, _dot(ub, br_ref[...], "nn"))
            _split_tiles(hi, rs, _dot(ub, bi_ref[...], "nn"))
            _split_tiles(ar, rs, _dot(dyb, ctr_ref[...], "nn"))
            _split_tiles(ai, rs, -_dot(dyb, cti_ref[...], "nn"))
        _ssm_scan(hr, hi, list(zip(lrs, lis)), seg, nchain, False)

        def dlam_step(tt, a, j, a_r, a_i, acc):
            sl = _chain_rows(a, jnp.maximum(tt - 1, 0), seg)
            p_r, p_i = hr[j, sl, :], hi[j, sl, :]
            acc = list(acc)
            acc[2 * j] = acc[2 * j] + jnp.where(tt > 0, a_r * p_r + a_i * p_i, 0.0)
            acc[2 * j + 1] = acc[2 * j + 1] + jnp.where(tt > 0, a_i * p_r - a_r * p_i, 0.0)
            return tuple(acc)

        zero = tuple([jnp.zeros((SUBLANES, LANES), F32)] * (2 * njt))
        conj = [(lr, -li) for lr, li in zip(lrs, lis)]
        acc = list(_ssm_scan(ar, ai, conj, seg, nchain, True, (dlam_step, zero)))
        row0 = lax.broadcasted_iota(jnp.int32, (SUBLANES, LANES), 0) == 0
        for j in range(njt):
            cs = slice(j * LANES, (j + 1) * LANES)
            for a in range(nchain):
                p_r = _shift_rows(hr[j, _chain_rows(a, seg - 1, seg), :], 1, False)
                p_i = _shift_rows(hi[j, _chain_rows(a, seg - 1, seg), :], 1, False)
                if a > 0:
                    before = pl.ds(a * SUBLANES * seg - 1, 1)
                    p_r = jnp.where(row0, jnp.broadcast_to(hr[j, before, :], p_r.shape), p_r)
                    p_i = jnp.where(row0, jnp.broadcast_to(hi[j, before, :], p_i.shape), p_i)
                a_r, a_i = ar[j, _chain_rows(a, 0, seg), :], ai[j, _chain_rows(a, 0, seg), :]
                acc[2 * j] = acc[2 * j] + a_r * p_r + a_i * p_i
                acc[2 * j + 1] = acc[2 * j + 1] + a_i * p_r - a_r * p_i
            dlr_ref[:, cs] = jnp.sum(acc[2 * j], axis=0, keepdims=True)
            dli_ref[:, cs] = jnp.sum(acc[2 * j + 1], axis=0, keepdims=True)

        dd = jnp.zeros((1, LANES), F32)
        for s in range(SUBLANES):
            rs = pl.ds(s * chunk, chunk)
            ub = u_ref[rs, :].astype(BF16)
            dyv = dy_ref[rs, :]
            dyb = dyv.astype(BF16)
            arb, aib = _cat_tiles(ar, rs), _cat_tiles(ai, rs)
            hrb, hib = _cat_tiles(hr, rs), _cat_tiles(hi, rs)
            du = _dot(arb, btr_ref[...], "nn") + _dot(aib, bti_ref[...], "nn")
            upd = [(dbr_ref, _dot(arb, ub, "tn")), (dbi_ref, _dot(aib, ub, "tn")),
                   (dcr_ref, _dot(dyb, hrb, "tn")), (dci_ref, -_dot(dyb, hib, "tn"))]
            for ref, val in upd:
                if s == 0:
                    ref[...] = val
                else:
                    ref[...] += val
            rows = lax.broadcasted_iota(jnp.int32, (chunk, LANES), 0) + s * chunk
            keep = rows >= PAD_FRONT
            dd = dd + jnp.sum(dyv * u_ref[rs, :], axis=0, keepdims=True)

            @pl.when(t % 2 == 0)
            def _():
                du_ref[rs, :] = jnp.where(keep, du + d_ref[...] * dyv, 0.0)

            @pl.when(t % 2 == 1)
            def _():
                du_ref[rs, :] += jnp.where(keep, du, 0.0)

        @pl.when(t % 2 == 0)
        def _():
            dd_ref[...] = dd

    blk = pl.BlockSpec((lp, LANES), lambda t: (0, t // 2))
    vec = pl.BlockSpec((1, LANES), lambda t: (0, t // 2))
    lam_spec = pl.BlockSpec((None, 1, TILE_STATES), lambda t: (t, 0, 0))
    b_spec = pl.BlockSpec((None, LANES, TILE_STATES), lambda t: (t, 0, 0))
    c_spec = pl.BlockSpec((None, TILE_STATES, LANES), lambda t: (t, 0, 0))
    lam_shape = jax.ShapeDtypeStruct((nt, 1, TILE_STATES), F32)
    bt_shape = jax.ShapeDtypeStruct((nt, TILE_STATES, LANES), F32)
    ct_shape = jax.ShapeDtypeStruct((nt, LANES, TILE_STATES), F32)
    st = pltpu.VMEM((njt, lp, LANES), F32)
    return _hosted_call(
        body, comm,
        out_shape=[jax.ShapeDtypeStruct((lp, w), F32), lam_shape, lam_shape, bt_shape, bt_shape, ct_shape, ct_shape,
                   jax.ShapeDtypeStruct((1, w), F32)],
        grid=(nt,),
        in_specs=[blk, blk, lam_spec, lam_spec, b_spec, b_spec, c_spec, c_spec, b_spec, b_spec, vec],
        out_specs=[blk, lam_spec, lam_spec, c_spec, c_spec, b_spec, b_spec, vec],
        scratch_shapes=[st, st, st, st], sem=("arbitrary",), name="ssm_bwd",
        args=(u, dy, lam_re, lam_im, tb_re, tb_im, tbt_re, tbt_im, tct_re, tct_im, d_skip.reshape(1, w)))


def _ssm_params(a_re, a_im, log_dt, b_re, b_im):
    dt = jnp.exp(log_dt)[:, None]
    mag = jnp.exp(a_re * dt)
    lb_re = mag * jnp.cos(a_im * dt)
    lb_im = mag * jnp.sin(a_im * dt)
    den = a_re * a_re + a_im * a_im
    num_re = lb_re - 1.0
    coef_re = (num_re * a_re + lb_im * a_im) / den
    coef_im = (lb_im * a_re - num_re * a_im) / den
    bb_re = coef_re[..., None] * b_re - coef_im[..., None] * b_im
    bb_im = coef_re[..., None] * b_im + coef_im[..., None] * b_re
    return lb_re, lb_im, bb_re, bb_im


def _gelu_fwd(y):
    lp, w = y.shape
    tm = _row_tile(lp)

    def body(y_ref, o_ref):
        o_ref[...] = _gelu(y_ref[...]).astype(BF16)

    return pl.pallas_call(
        body, out_shape=jax.ShapeDtypeStruct((lp, w), BF16), grid=(lp // tm,),
        in_specs=[pl.BlockSpec((tm, w), lambda i: (i, 0))], out_specs=pl.BlockSpec((tm, w), lambda i: (i, 0)),
        compiler_params=_cparams(("parallel",)), name="gelu_fwd")(y)


def _merge_fwd(o, yg, ga, gs, w3, comm=None):
    lp, d = ga.shape
    d4 = w3.shape[3]
    kw = w3.shape[2]
    tm = _row_tile(lp)

    def epi(accs, in_refs, out_refs):
        attn, vv, gg = accs
        out_refs[0][...] = (_sigmoid(in_refs[5][...]) * attn
                            + _sigmoid(in_refs[6][...]) * (vv * _sigmoid(gg))).astype(BF16)

    wspec = lambda which: pl.BlockSpec((None, None, kw, d4), lambda j, i: (j, which, 0, 0))
    colspec = pl.BlockSpec((tm, d4), lambda j, i: (i, j))
    aspec = pl.BlockSpec((tm, kw), lambda j, i: (i, 0))
    res = _matmul(
        "merge_fwd", (N_CHIPS, lp // tm), None, [o, yg, w3, w3, w3, ga, gs],
        [aspec, aspec, wspec(0), wspec(1), wspec(2), colspec, colspec],
        [(0, 2, "nn", 0), (1, 3, "nn", 1), (1, 4, "nn", 2)], [(tm, d4)] * 3, epi,
        [jax.ShapeDtypeStruct((lp, d), BF16)], [colspec], ("parallel", "parallel"), comm)
    return (res[0], []) if comm is None else (res[0][0], res[1])


def _out_proj(merged, w_out, h, comm=None):
    lp, d = h.shape
    d4 = w_out.shape[1]
    tm = _row_tile(lp)

    def epi(accs, in_refs, out_refs):
        out_refs[0][...] = in_refs[2][...] + accs[0]

    res = _matmul(
        "mix_out_proj", (lp // tm, N_CHIPS), 1, [merged, w_out, h],
        [pl.BlockSpec((tm, d4), lambda i, j: (i, j)), pl.BlockSpec((None, d4, d), lambda i, j: (j, 0, 0)),
         pl.BlockSpec((tm, d), lambda i, j: (i, 0))],
        [(0, 1, "nn", 0)], [(tm, d)], epi,
        [jax.ShapeDtypeStruct((lp, d), F32)], [pl.BlockSpec((tm, d), lambda i, j: (i, 0))],
        ("parallel", "arbitrary"), comm)
    return (res[0], []) if comm is None else (res[0][0], res[1])


def _merge_bwd(dhb, w_out, o, yg, ga, gs, w3):
    lp, d = ga.shape
    d4 = w3.shape[3]
    kw = w3.shape[2]
    tm = _row_tile(lp)

    def epi(accs, in_refs, out_refs):
        dm, attn, vv, gg = accs
        sa = _sigmoid(in_refs[7][...])
        ss = _sigmoid(in_refs[8][...])
        sg = _sigmoid(gg)
        ssm = vv * sg
        dssm = dm * ss
        out_refs[0][...] = (dm * sa).astype(BF16)
        out_refs[1][...] = (dssm * sg).astype(BF16)
        out_refs[2][...] = (dssm * vv * sg * (1.0 - sg)).astype(BF16)
        out_refs[3][...] = (dm * attn * sa * (1.0 - sa)).astype(BF16)
        out_refs[4][...] = (dm * ssm * ss * (1.0 - ss)).astype(BF16)

    wspec = lambda which: pl.BlockSpec((None, None, kw, d4), lambda j, i: (j, which, 0, 0))
    colspec = pl.BlockSpec((tm, d4), lambda j, i: (i, j))
    aspec = pl.BlockSpec((tm, kw), lambda j, i: (i, 0))
    shp = jax.ShapeDtypeStruct((lp, d), BF16)
    return _matmul(
        "merge_bwd", (N_CHIPS, lp // tm), None, [dhb, w_out, o, yg, w3, w3, w3, ga, gs],
        [pl.BlockSpec((tm, d), lambda j, i: (i, 0)), pl.BlockSpec((None, d4, d), lambda j, i: (j, 0, 0)),
         aspec, aspec, wspec(0), wspec(1), wspec(2), colspec, colspec],
        [(0, 1, "nt", 0), (2, 4, "nn", 1), (3, 5, "nn", 2), (3, 6, "nn", 3)], [(tm, d4)] * 4, epi,
        [shp] * 5, [colspec] * 5, ("parallel", "parallel"))


def _branch_bwd(dattn, dv, dg, w3, y):
    lp, d = dattn.shape
    d4 = w3.shape[3]
    kw = w3.shape[2]
    tm = _row_tile(lp)

    def epi(accs, in_refs, out_refs):
        out_refs[0][...] = accs[0].astype(BF16)
        out_refs[1][...] = accs[1] * _gelu_grad(in_refs[6][...])

    wspec = lambda which: pl.BlockSpec((None, None, kw, d4), lambda i, j: (j, which, 0, 0))
    colspec = pl.BlockSpec((tm, d4), lambda i, j: (i, j))
    rowspec = pl.BlockSpec((tm, kw), lambda i, j: (i, 0))
    return _matmul(
        "branch_bwd", (lp // tm, N_CHIPS), 1, [dattn, dv, dg, w3, w3, w3, y],
        [colspec, colspec, colspec, wspec(0), wspec(1), wspec(2), rowspec],
        [(0, 3, "nt", 0), (1, 4, "nt", 1), (2, 5, "nt", 1)], [(tm, kw)] * 2, epi,
        [jax.ShapeDtypeStruct((lp, kw), BF16), jax.ShapeDtypeStruct((lp, kw), F32)], [rowspec, rowspec],
        ("parallel", "arbitrary"))


def _tn_cols(x, ys, name):
    lp, kx = x.shape
    n = ys[0].shape[1]
    n4 = n // N_CHIPS
    tk = _tn_tiles(lp)

    def epi(accs, in_refs, out_refs):
        for acc, o in zip(accs, out_refs):
            o[...] = acc

    shp = jax.ShapeDtypeStruct((N_CHIPS, kx, n4), F32)
    return _matmul(
        name, (N_CHIPS, lp // tk), 1, [x] + list(ys),
        [pl.BlockSpec((tk, kx), lambda j, k: (k, 0))] + [pl.BlockSpec((tk, n4), lambda j, k: (k, j))] * len(ys),
        [(0, 1 + i, "tn", i) for i in range(len(ys))], [(kx, n4)] * len(ys), epi,
        [shp] * len(ys), [pl.BlockSpec((None, kx, n4), lambda j, k: (j, 0, 0))] * len(ys),
        ("parallel", "arbitrary"))


def _tn_full(x, y, name, tn_cols=None):
    lp, kx = x.shape
    n = y.shape[1]
    tk = _tn_tiles(lp)
    tn = n if tn_cols is None else tn_cols

    def epi(accs, in_refs, out_refs):
        out_refs[0][...] = accs[0]

    (out,) = _matmul(
        name, (n // tn, lp // tk), 1, [x, y],
        [pl.BlockSpec((tk, kx), lambda j, k: (k, 0)), pl.BlockSpec((tk, tn), lambda j, k: (k, j))],
        [(0, 1, "tn", 0)], [(kx, tn)], epi,
        [jax.ShapeDtypeStruct((kx, n), F32)], [pl.BlockSpec((kx, tn), lambda j, k: (0, j))],
        ("parallel", "arbitrary"))
    return out


def _in_proj_bwd(dz, w_in, dh, h_in, gain):
    lp, d = h_in.shape
    inw = w_in.shape[1]
    tm = _row_tile(lp)

    def epi(accs, in_refs, out_refs):
        i = pl.program_id(0)
        dx, dgrow = _rms_bwd_math(accs[0], in_refs[3][...], in_refs[4][...])
        out_refs[0][...] = in_refs[2][...] + dx

        @pl.when(i == 0)
        def _():
            out_refs[1][...] = jnp.zeros_like(out_refs[1])

        out_refs[1][...] += jnp.sum(dgrow, axis=0, keepdims=True)

    row = pl.BlockSpec((tm, d), lambda i: (i, 0))
    return _matmul(
        "mix_in_proj_bwd", (lp // tm,), None, [dz, w_in, dh, h_in, gain.reshape(1, d)],
        [pl.BlockSpec((tm, inw), lambda i: (i, 0)), pl.BlockSpec((d, inw), lambda i: (0, 0)), row, row,
         pl.BlockSpec((1, d), lambda i: (0, 0))],
        [(0, 1, "nt", 0)], [(tm, d)], epi,
        [jax.ShapeDtypeStruct((lp, d), F32), jax.ShapeDtypeStruct((1, d), F32)],
        [row, pl.BlockSpec((1, d), lambda i: (0, 0))], ("arbitrary",))


def _loss_head(h, gain, target):
    lp, d = h.shape
    nb = lp // BLOCK

    def body(h_ref, g_ref, t_ref, dh_ref, dg_ref, loss_ref):
        i = pl.program_id(0)

        @pl.when(i == 0)
        def _():
            dg_ref[...] = jnp.zeros_like(dg_ref)
            loss_ref[...] = jnp.zeros_like(loss_ref)
            dh_ref[...] = jnp.zeros_like(dh_ref)

        @pl.when(i > 0)
        def _():
            x = h_ref[...]
            g = g_ref[...]
            r = lax.rsqrt(jnp.mean(x * x, axis=-1, keepdims=True) + EPS)
            err = x * r * g - t_ref[...]
            loss_ref[...] += jnp.zeros_like(loss_ref) + 0.5 * jnp.sum(jnp.sum(err * err, axis=-1, keepdims=True)) / d
            dx, dgrow = _rms_bwd_math(err * (1.0 / d), x, g)
            dh_ref[...] = dx
            dg_ref[...] += jnp.sum(dgrow, axis=0, keepdims=True)

    row = pl.BlockSpec((BLOCK, d), lambda i: (i, 0))
    one = pl.BlockSpec((1, d), lambda i: (0, 0))
    return pl.pallas_call(
        body,
        out_shape=[jax.ShapeDtypeStruct((lp, d), F32), jax.ShapeDtypeStruct((1, d), F32),
                   jax.ShapeDtypeStruct((SUBLANES, LANES), F32)],
        grid=(nb,),
        in_specs=[row, one, pl.BlockSpec((BLOCK, d), lambda i: (jnp.maximum(i - 1, 0), 0))],
        out_specs=[row, one, pl.BlockSpec((SUBLANES, LANES), lambda i: (0, 0))],
        compiler_params=_cparams(("arbitrary",)), name="loss_head")(h, gain.reshape(1, d), target)


def _adam_math(w, g, m, v):
    m = ADAM_B1 * m + (1.0 - ADAM_B1) * g
    v = ADAM_B2 * v + (1.0 - ADAM_B2) * (g * g)
    m_hat = m / (1.0 - ADAM_B1 ** ADAM_STEP)
    v_hat = v / (1.0 - ADAM_B2 ** ADAM_STEP)
    delta = -ADAM_LR * (m_hat / (jnp.sqrt(v_hat) + ADAM_EPS) + ADAM_WD * w)
    return delta, m, v


def _adamw_layers(w, m, v, mine, other, pos, name):
    depth, r, c = w.shape
    half = r // 2
    tr = _div_tile(half, c * 4)
    nh = half // tr

    def body(*refs):
        pos_ref, w_ref, m_ref, v_ref = refs[:4]
        mine_refs = refs[4:4 + depth]
        other_refs = refs[4 + depth:4 + 2 * depth]
        g_out, d_out, m_out, v_out = refs[4 + 2 * depth:]
        layer, i = pl.program_id(0), pl.program_id(1)
        is_mine = (i // nh) == pos_ref[0]

        def update(g):
            delta, nm, nv = _adam_math(w_ref[...], g, m_ref[...], v_ref[...])
            g_out[...] = g
            d_out[...] = delta
            m_out[...] = nm
            v_out[...] = nv

        for l in range(depth):
            @pl.when((layer == l) & is_mine)
            def _(l=l):
                update(mine_refs[l][...])

            @pl.when((layer == l) & jnp.logical_not(is_mine))
            def _(l=l):
                update(other_refs[l][...])

    stacked = pl.BlockSpec((None, tr, c), lambda l, i, p: (l, i, 0))

    def gspec(layer, is_other):
        def imap(l, i, p):
            first = jnp.where(is_other, 1 - p[0], p[0]) * nh
            here = jnp.clip(i - first, 0, nh - 1)
            return (jnp.where(l == layer, here, jnp.where(l < layer, 0, nh - 1)), 0)
        return pl.BlockSpec((tr, c), imap)

    shp = jax.ShapeDtypeStruct((depth, r, c), F32)
    grid_spec = pltpu.PrefetchScalarGridSpec(
        num_scalar_prefetch=1, grid=(depth, 2 * nh),
        in_specs=[stacked] * 3 + [gspec(l, 0) for l in range(depth)] + [gspec(l, 1) for l in range(depth)],
        out_specs=[stacked] * 4)
    return pl.pallas_call(
        body, out_shape=[shp] * 4, grid_spec=grid_spec,
        compiler_params=_cparams(("arbitrary", "arbitrary")), name=name)(pos, w, m, v, *mine, *other)


def _adamw_flat(w, g, m, v, name):
    r, c = w.shape
    tr = _div_tile(r, c * 4)

    def body(w_ref, g_ref, m_ref, v_ref, d_out, m_out, v_out):
        delta, nm, nv = _adam_math(w_ref[...], g_ref[...], m_ref[...], v_ref[...])
        d_out[...] = delta
        m_out[...] = nm
        v_out[...] = nv

    spec = pl.BlockSpec((tr, c), lambda i: (i, 0))
    shp = jax.ShapeDtypeStruct((r, c), F32)
    return pl.pallas_call(
        body, out_shape=[shp] * 3, grid=(r // tr,), in_specs=[spec] * 4, out_specs=[spec] * 3,
        compiler_params=_cparams(("parallel",)), name=name)(w, g, m, v)


def _mesh_pos():
    return lax.axis_index("x"), lax.axis_index("y"), lax.axis_index("c")


def _row_half(ref, which, lead):
    half = ref.shape[lead] // 2
    idx = (slice(None),) * lead + (pl.ds(which * half, half), slice(None))
    return ref.at[idx]


def _gather_comm(arrs, tag):
    n = len(arrs)

    def ctx(ins, outs, sems):
        send_sems, recv_sems, local_sems = sems
        x, y, c = _mesh_pos()
        chips = [(1 - x, y), (x, 1 - y), (1 - x, 1 - y)]

        def slot(k, chip, which):
            lead = len(ins[k].shape) - 2
            return _row_half(outs[k].at[2 * chip[0] + chip[1]], which, lead)

        def copy(k, j, src, dst, to):
            return pltpu.make_async_remote_copy(
                src_ref=src, dst_ref=dst, send_sem=send_sems.at[6 * k + j], recv_sem=recv_sems.at[6 * k + j],
                device_id=to, device_id_type=MESH)

        def local(k):
            return pltpu.make_async_copy(ins[k], outs[k].at[2 * x + y], local_sems.at[k])

        def first(k, j):
            lead = len(ins[k].shape) - 2
            return copy(k, j, _row_half(ins[k], c, lead), slot(k, (x, y), c), (*chips[j], c))

        def passed(k, j, which):
            return copy(k, 3 + j, slot(k, chips[j], which), slot(k, chips[j], which), (x, y, 1 - c))

        def landed(k, j):
            return copy(k, j, slot(k, chips[j], c), slot(k, chips[j], c), (x, y, 1 - c))

        return c, local, first, passed, landed

    def start(ins, outs, sems):
        c, local, first, passed, landed = ctx(ins, outs, sems)
        for k in range(n):
            local(k).start()
            for j in range(3):
                first(k, j).start()

    def mid(ins, outs, sems):
        c, local, first, passed, landed = ctx(ins, outs, sems)
        for j in range(3):
            for k in range(n):
                landed(k, j).wait_recv()
                passed(k, j, c).start()

    def finish(ins, outs, sems):
        c, local, first, passed, landed = ctx(ins, outs, sems)
        for j in range(3):
            for k in range(n):
                passed(k, j, 1 - c).wait_recv()
        for k in range(n):
            for j in range(3):
                first(k, j).wait_send()
                passed(k, j, c).wait_send()
            local(k).wait()

    return _Comm(
        tag, arrs, [jax.ShapeDtypeStruct((N_CHIPS,) + a.shape, a.dtype) for a in arrs],
        [pltpu.SemaphoreType.DMA((6 * n,)), pltpu.SemaphoreType.DMA((6 * n,)), pltpu.SemaphoreType.DMA((n,))],
        start, mid, finish)


def _run_comm(comm, name):
    n_in, n_out = len(comm.ins), len(comm.out_shapes)

    def body(*refs):
        ins, outs, sems = refs[:n_in], refs[n_in:n_in + n_out], refs[n_in + n_out:]
        comm.start(ins, outs, sems)
        if comm.mid is not None:
            comm.mid(ins, outs, sems)
        comm.finish(ins, outs, sems)

    return pl.pallas_call(
        body, out_shape=comm.out_shapes, in_specs=[HBM_SPEC] * n_in, out_specs=[HBM_SPEC] * n_out,
        scratch_shapes=comm.sems, name=name)(*comm.ins)


def _all_gather_chips(arrs, name):
    return _run_comm(_gather_comm(arrs, "gather"), name)


GATHER_US_PER_BYTE = 380.0 / 11.65e6
HOST_US = dict(ffn_up=78.0, ffn_down=65.0, in_proj=38.0, attn_fwd=103.0, ssm_fwd=67.0, merge_fwd=50.0,
               out_proj=45.0)
HOST_SLACK_US = 10.0


class _WeightStream:
    def __init__(self, pieces):
        self.keys = [k for k, _ in pieces]
        self.shards = dict(pieces)
        self.next = 0
        self.full = {}
        self.pending = []

    def comm_for(self, host):
        budget = HOST_US[host] + HOST_SLACK_US
        taken, cost = [], 0.0
        while self.next < len(self.keys):
            key = self.keys[self.next]
            c = self.shards[key].size * self.shards[key].dtype.itemsize * GATHER_US_PER_BYTE
            if cost + c > budget:
                break
            taken.append(key)
            cost += c
            self.next += 1
        self.pending = taken
        if not taken:
            return None
        return _gather_comm([self.shards[k] for k in taken], "g_" + "_".join(k[1] for k in taken))

    def deposit(self, gathered):
        for key, arr in zip(self.pending, gathered):
            self.full[key] = arr
        self.pending = []

    def get(self, key):
        if key not in self.full:
            upto = self.keys.index(key) + 1
            keys = self.keys[self.next:upto]
            self.next = upto
            for k, arr in zip(keys, _all_gather_chips([self.shards[k] for k in keys], "gather_now")):
                self.full[k] = arr
        return self.full[key]


def _all_gather_devices(x_shard, name):
    m_per, ncol = x_shard.shape

    def body(x_ref, out_ref, send_sems, recv_sems, local_sem):
        x, y, c = _mesh_pos()
        me, sibling = (x, y, c), (x, y, 1 - c)
        chips = [(1 - x, y), (x, 1 - y), (1 - x, 1 - y)]

        def rows(px, py, pc):
            return out_ref.at[4 * px + 2 * py + pc]

        def copy(k, block, to, src=None):
            return pltpu.make_async_remote_copy(
                src_ref=rows(*block) if src is None else src, dst_ref=rows(*block),
                send_sem=send_sems.at[k], recv_sem=recv_sems.at[k], device_id=to, device_id_type=MESH)

        mine = pltpu.make_async_copy(x_ref, rows(*me), local_sem)
        mine.start()
        first = [copy(0, me, sibling, src=x_ref)]
        first += [copy(1 + j, me, (*chip, c), src=x_ref) for j, chip in enumerate(chips)]
        for cp in first:
            cp.start()
        passed = [copy(4 + j, (*chip, c), sibling) for j, chip in enumerate(chips)]
        for j, chip in enumerate(chips):
            copy(1 + j, (*chip, c), me).wait_recv()
            passed[j].start()
        copy(0, sibling, me).wait_recv()
        for j, chip in enumerate(chips):
            copy(4 + j, (*chip, 1 - c), me).wait_recv()
        for cp in first + passed:
            cp.wait_send()
        mine.wait()

    return pl.pallas_call(
        body, out_shape=jax.ShapeDtypeStruct((8, m_per, ncol), x_shard.dtype),
        in_specs=[pl.BlockSpec(memory_space=pltpu.VMEM)], out_specs=pl.BlockSpec(memory_space=pltpu.VMEM),
        scratch_shapes=[pltpu.SemaphoreType.DMA((7,)), pltpu.SemaphoreType.DMA((7,)), pltpu.SemaphoreType.DMA],
        compiler_params=pltpu.CompilerParams(vmem_limit_bytes=VMEM_LIMIT), name=name)(x_shard)


def _sum_devices(g8, name):
    _, r, c = g8.shape
    tr = _div_tile(r, c * 4 * 8)

    def body(g_ref, o_ref):
        acc = g_ref[0]
        for dev in range(1, 8):
            acc = acc + g_ref[dev]
        o_ref[...] = acc

    return pl.pallas_call(
        body, out_shape=jax.ShapeDtypeStruct((r, c), F32), grid=(r // tr,),
        in_specs=[pl.BlockSpec((8, tr, c), lambda i: (0, i, 0))], out_specs=pl.BlockSpec((tr, c), lambda i: (i, 0)),
        compiler_params=_cparams(("parallel",)), name=name)(g8)


def _exchange_sibling_halves(arrs, name):
    n = len(arrs)

    def body(*refs):
        ins, outs = refs[:n], refs[n:2 * n]
        send_sems, recv_sems = refs[2 * n:]
        x, y, c = _mesh_pos()
        cps = []
        for k in range(n):
            cp = pltpu.make_async_remote_copy(
                src_ref=_row_half(ins[k], 1 - c, 1), dst_ref=outs[k], send_sem=send_sems.at[k],
                recv_sem=recv_sems.at[k], device_id=(x, y, 1 - c), device_id_type=MESH)
            cp.start()
            cps.append(cp)
        for cp in cps:
            cp.wait()

    return pl.pallas_call(
        body,
        out_shape=[jax.ShapeDtypeStruct((a.shape[0], a.shape[1] // 2, a.shape[2]), a.dtype) for a in arrs],
        in_specs=[HBM_SPEC] * n, out_specs=[HBM_SPEC] * n,
        scratch_shapes=[pltpu.SemaphoreType.DMA((n,)), pltpu.SemaphoreType.DMA((n,))], name=name)(*arrs)


def _chip_partials(arrs, recvs, pos, name):
    n = len(arrs)

    def body(pos_ref, *refs):
        for a_ref, b_ref, o_ref in zip(refs[:n], refs[n:2 * n], refs[2 * n:]):
            o_ref[...] = (a_ref[...] + b_ref[...]).astype(BF16)

    own_specs, recv_specs, shapes = [], [], []
    for arr in arrs:
        nslab, r, c = arr.shape
        own_specs.append(pl.BlockSpec((None, r // 2, c), lambda j, p: (j, p[0], 0)))
        recv_specs.append(pl.BlockSpec((None, r // 2, c), lambda j, p: (j, 0, 0)))
        shapes.append(jax.ShapeDtypeStruct((nslab, r // 2, c), BF16))
    grid_spec = pltpu.PrefetchScalarGridSpec(
        num_scalar_prefetch=1, grid=(N_CHIPS,), in_specs=own_specs + recv_specs, out_specs=recv_specs)
    return pl.pallas_call(
        body, out_shape=shapes, grid_spec=grid_spec,
        compiler_params=_cparams(("parallel",)), name=name)(pos, *arrs, *recvs)


def _chip_exchange_comm(parts, tag):
    n = len(parts)

    def copies(ins, outs, sems):
        send_sems, recv_sems = sems
        x, y, c = _mesh_pos()
        chips = [(1 - x, y), (x, 1 - y), (1 - x, 1 - y)]
        return [pltpu.make_async_remote_copy(
            src_ref=ins[k].at[2 * chip[0] + chip[1]], dst_ref=outs[k].at[j],
            send_sem=send_sems.at[3 * k + j], recv_sem=recv_sems.at[3 * k + j],
            device_id=(*chip, c), device_id_type=MESH) for k in range(n) for j, chip in enumerate(chips)]

    def start(ins, outs, sems):
        for cp in copies(ins, outs, sems):
            cp.start()

    def finish(ins, outs, sems):
        for cp in copies(ins, outs, sems):
            cp.wait()

    return _Comm(
        tag, parts, [jax.ShapeDtypeStruct((3,) + p.shape[1:], p.dtype) for p in parts],
        [pltpu.SemaphoreType.DMA((3 * n,)), pltpu.SemaphoreType.DMA((3 * n,))], start, None, finish)


def _reduce_halves(arrs, recvs, gots, pos, name):
    n = len(arrs)

    def body(pos_ref, *refs):
        for a_ref, b_ref, g_ref, o_ref in zip(refs[:n], refs[n:2 * n], refs[2 * n:3 * n], refs[3 * n:]):
            acc = a_ref[...] + b_ref[...]
            for j in range(3):
                acc = acc + g_ref[j].astype(F32)
            o_ref[...] = acc

    own_specs, recv_specs, got_specs, out_specs, shapes = [], [], [], [], []
    for arr in arrs:
        _, r, c = arr.shape
        own_specs.append(pl.BlockSpec((None, r // 2, c), lambda i, p: (p[1], p[0], 0)))
        recv_specs.append(pl.BlockSpec((None, r // 2, c), lambda i, p: (p[1], 0, 0)))
        got_specs.append(pl.BlockSpec((3, r // 2, c), lambda i, p: (0, 0, 0)))
        out_specs.append(pl.BlockSpec((r // 2, c), lambda i, p: (0, 0)))
        shapes.append(jax.ShapeDtypeStruct((r // 2, c), F32))
    grid_spec = pltpu.PrefetchScalarGridSpec(
        num_scalar_prefetch=1, grid=(1,), in_specs=own_specs + recv_specs + got_specs, out_specs=out_specs)
    return pl.pallas_call(
        body, out_shape=shapes, grid_spec=grid_spec,
        compiler_params=_cparams(("arbitrary",)), name=name)(pos, *arrs, *recvs, *gots)


def _share_halves(halves, name):
    n = len(halves)

    def body(*refs):
        ins, outs = refs[:n], refs[n:2 * n]
        send_sems, recv_sems = refs[2 * n:]
        x, y, c = _mesh_pos()
        cps = []
        for k in range(n):
            cp = pltpu.make_async_remote_copy(
                src_ref=ins[k], dst_ref=outs[k], send_sem=send_sems.at[k], recv_sem=recv_sems.at[k],
                device_id=(x, y, 1 - c), device_id_type=MESH)
            cp.start()
            cps.append(cp)
        for cp in cps:
            cp.wait()

    return pl.pallas_call(
        body, out_shape=[jax.ShapeDtypeStruct(h.shape, h.dtype) for h in halves],
        in_specs=[HBM_SPEC] * n, out_specs=[HBM_SPEC] * n,
        scratch_shapes=[pltpu.SemaphoreType.DMA((n,)), pltpu.SemaphoreType.DMA((n,))], name=name)(*halves)


class _Reduction:
    def __init__(self, arrs, pos, tag):
        self.arrs, self.pos, self.tag = arrs, pos, tag
        self.recv = _exchange_sibling_halves(arrs, "rs_sibling_" + tag)
        self.parts = _chip_partials(arrs, self.recv, pos, "rs_partial_" + tag)
        self.got = None

    def comm(self):
        return _chip_exchange_comm(self.parts, "rs_" + self.tag)

    def end(self):
        if self.got is None:
            self.got = _run_comm(self.comm(), "rs_chips_" + self.tag)
        halves = _reduce_halves(self.arrs, self.recv, self.got, self.pos, "rs_reduce_" + self.tag)
        return halves, _share_halves(halves, "rs_share_" + self.tag)


def _w_in_full(p, l, ws):
    if "w_in" not in p:
        slabs = ws.get((l, "w_in"))
        p["w_in"] = jnp.concatenate([slabs[j] for j in range(N_CHIPS)], axis=1)
    return p["w_in"]


def _layer_fwd(h, l, p, ws, tabs):
    def hosted(host, fn, *args):
        out, got = fn(*args, ws.comm_for(host))
        ws.deposit(got)
        return out

    a, b, sact = hosted("ffn_up", _ffn_up, h, p["ffn1_norm"], ws.get((l, "wg1")), ws.get((l, "wu1")))
    h1 = hosted("ffn_down", _ffn_down, sact, ws.get((l, "wd1")), h)
    ffn1_saved = (a, b, sact)
    n = _rms_fwd(h1, p["mix_norm"], "rms_fwd_mix")
    ssm_w = p["ssm_d"].shape[0]
    q, k, v, u, ga, gs = hosted("in_proj", _in_proj, n, _w_in_full(p, l, ws), tabs, ssm_w)
    o = hosted("attn_fwd", _attn_fwd, q, k, v, p["attn_sinks"])
    y = hosted("ssm_fwd", _ssm_fwd, u, *p["ssm_tabs"], p["ssm_d"])
    yg = _gelu_fwd(y)
    merged = hosted("merge_fwd", _merge_fwd, o, yg, ga, gs, ws.get((l, "w3")))
    h2 = hosted("out_proj", _out_proj, merged, ws.get((l, "w_out")), h1)
    a, b, sact = hosted("ffn_up", _ffn_up, h2, p["ffn2_norm"], ws.get((l, "wg2")), ws.get((l, "wu2")))
    h3 = hosted("ffn_down", _ffn_down, sact, ws.get((l, "wd2")), h2)
    saved = dict(h0=h, h1=h1, h2=h2, ffn1=ffn1_saved, ffn2=(a, b, sact), q=q, k=k, v=v, u=u, ga=ga, gs=gs, o=o, y=y,
                 yg=yg, merged=merged)
    return h3, saved


def _layer_bwd(dh, l, p, ws, s, tabs, pos):
    g = {}
    dh2, g["ffn2_norm"], red_ffn2, _ = _ffn_bwd(
        dh, s["h2"], p["ffn2_norm"], ws.get((l, "wg2")), ws.get((l, "wu2")), ws.get((l, "wd2")), p["f4"],
        s["ffn2"], pos)
    w3, w_out_w = ws.get((l, "w3")), ws.get((l, "w_out"))
    lp, d = dh2.shape
    d4 = d // N_CHIPS
    dhb = _scale_cast(dh2, 1.0, "mix_dh_cast")
    dw_out = _tn_full(s["merged"], dhb, "mix_dw_out").reshape(N_CHIPS, d4, d)
    dattn, dv, dg, dga, dgs = _merge_bwd(dhb, w_out_w, s["o"], s["yg"], s["ga"], s["gs"], w3)
    (dw_ap,) = _tn_cols(s["o"], [dattn], "mix_dw_ap")
    dw_gv, dw_gg = _tn_cols(s["yg"], [dv, dg], "mix_dw_glu")
    do, dy = _branch_bwd(dattn, dv, dg, w3, s["y"])
    (dq, dk, dvv, dkm, dvm, dsink), _ = _attn_bwd(s["q"], s["k"], s["v"], do, p["attn_sinks"], tabs)
    g["attn_sinks"] = dsink[:, 0]
    (du, dlr, dli, dbr, dbi, dcr, dci, dd), _ = _ssm_bwd(s["u"], dy, *p["ssm_tabs"], p["ssm_d"])
    ngrp = p["ssm_d"].shape[0] // SSM_GROUP
    g["ssm_lam"] = (dlr.reshape(ngrp, SSM_STATE), dli.reshape(ngrp, SSM_STATE),
                    _ssm_untable_b(dbr, ngrp), _ssm_untable_b(dbi, ngrp))
    g["ssm_c_re"] = _ssm_untable_c(dcr, ngrp)
    g["ssm_c_im"] = _ssm_untable_c(dci, ngrp)
    g["ssm_d"] = dd[0]
    dk = dk.at[:BLOCK].add(dkm)
    dvv = dvv.at[:BLOCK].add(dvm)
    dz = jnp.concatenate([dq.astype(BF16), dk.astype(BF16), dvv.astype(BF16), du.astype(BF16), dga, dgs], axis=1)
    n = _rms_fwd(s["h1"], p["mix_norm"], "rms_fwd_mix")
    w_in = _w_in_full(p, l, ws)
    inw = w_in.shape[1]
    dw_in = _tn_full(dz, n, "mix_dw_in", d // 2).reshape(N_CHIPS, inw // N_CHIPS, d)
    red_mix = _Reduction([dw_in, dw_ap, dw_gv, dw_gg, dw_out], pos, "mix")
    dh1, g["mix_norm"] = _in_proj_bwd(dz, w_in, dh2, s["h1"], p["mix_norm"])
    dh0, g["ffn1_norm"], red_ffn1, red_mix.got = _ffn_bwd(
        dh1, s["h0"], p["ffn1_norm"], ws.get((l, "wg1")), ws.get((l, "wu1")), ws.get((l, "wd1")), p["f4"],
        s["ffn1"], pos, red_mix.comm())
    return dh0, g, [red_ffn1, red_mix, red_ffn2]


BIG = ["ffn1_w_gate", "ffn1_w_up", "ffn1_w_down", "w_in", "w_attn_proj", "w_glu_v", "w_glu_g", "w_out",
       "ffn2_w_gate", "ffn2_w_up", "ffn2_w_down"]
TRANSPOSED = ["ffn1_w_gate", "ffn1_w_up", "w_in", "ffn2_w_gate", "ffn2_w_up"]
SMALL = ["ffn1_norm", "mix_norm", "attn_sinks", "ssm_a_re", "ssm_a_im", "ssm_log_dt", "ssm_b_re", "ssm_b_im",
         "ssm_c_re", "ssm_c_im", "ssm_d", "ffn2_norm", "final_norm"]
WEIGHTS = ["meta_tokens", "ffn1_norm", "ffn1_w_gate", "ffn1_w_up", "ffn1_w_down", "mix_norm", "w_in", "attn_sinks",
           "ssm_a_re", "ssm_a_im", "ssm_log_dt", "ssm_b_re", "ssm_b_im", "ssm_c_re", "ssm_c_im", "ssm_d",
           "w_attn_proj", "w_glu_v", "w_glu_g", "w_out", "ffn2_norm", "ffn2_w_gate", "ffn2_w_up", "ffn2_w_down",
           "final_norm"]


def _small_rows(shape):
    rows = -(-math.prod(shape) // LANES)
    return -(-rows // SUBLANES) * SUBLANES


def _pack_small(tree):
    parts = []
    for k in SMALL + ["meta_tokens"]:
        size, rows = math.prod(tree[k].shape), _small_rows(tree[k].shape)
        if size % LANES == 0:
            part = tree[k].reshape(size // LANES, LANES)
        else:
            part = jnp.pad(tree[k].reshape(1, size), ((0, 0), (0, LANES - size)))
        parts.append(jnp.pad(part, ((0, rows - part.shape[0]), (0, 0))))
    return jnp.concatenate(parts, axis=0)


def _unpack_small(packed, like):
    out, off = {}, 0
    for k in SMALL + ["meta_tokens"]:
        size, rows = math.prod(like[k].shape), _small_rows(like[k].shape)
        if size % LANES == 0:
            out[k] = packed[off:off + size // LANES].reshape(like[k].shape)
        else:
            out[k] = packed[off, :size].reshape(like[k].shape)
        off += rows
    return out


def kernel(x, meta_tokens, ffn1_norm, ffn1_w_gate, ffn1_w_up, ffn1_w_down, mix_norm, w_in, attn_sinks, ssm_a_re, ssm_a_im, ssm_log_dt, ssm_b_re, ssm_b_im, ssm_c_re, ssm_c_im, ssm_d, w_attn_proj, w_glu_v, w_glu_g, w_out, ffn2_norm, ffn2_w_gate, ffn2_w_up, ffn2_w_down, final_norm, loss_target, m_meta_tokens, m_ffn1_norm, m_ffn1_w_gate, m_ffn1_w_up, m_ffn1_w_down, m_mix_norm, m_w_in, m_attn_sinks, m_ssm_a_re, m_ssm_a_im, m_ssm_log_dt, m_ssm_b_re, m_ssm_b_im, m_ssm_c_re, m_ssm_c_im, m_ssm_d, m_w_attn_proj, m_w_glu_v, m_w_glu_g, m_w_out, m_ffn2_norm, m_ffn2_w_gate, m_ffn2_w_up, m_ffn2_w_down, m_final_norm, v_meta_tokens, v_ffn1_norm, v_ffn1_w_gate, v_ffn1_w_up, v_ffn1_w_down, v_mix_norm, v_w_in, v_attn_sinks, v_ssm_a_re, v_ssm_a_im, v_ssm_log_dt, v_ssm_b_re, v_ssm_b_im, v_ssm_c_re, v_ssm_c_im, v_ssm_d, v_w_attn_proj, v_w_glu_v, v_w_glu_g, v_w_out, v_ffn2_norm, v_ffn2_w_gate, v_ffn2_w_up, v_ffn2_w_down, v_final_norm):
    args = dict(locals())
    w = {k: args[k] for k in WEIGHTS}
    m = {k: args["m_" + k] for k in WEIGHTS}
    v = {k: args["v_" + k] for k in WEIGHTS}
    depth = ffn1_norm.shape[0]
    seq, d = x.shape[1], x.shape[2]
    lp = seq + BLOCK
    xi, yi, ci = _mesh_pos()
    pos = jnp.stack([ci, 2 * xi + yi]).astype(jnp.int32)

    tabs = _rope_tables(lp)
    (meta_all,) = _all_gather_chips([meta_tokens], "gather_meta")
    meta_full = jnp.concatenate([meta_all[j] for j in range(N_CHIPS)], axis=1)
    layers, pieces = [], []
    f4 = ffn1_w_gate.shape[2]
    fp = -(-f4 // MXU_DIM) * MXU_DIM

    def ffn_rows(wt):
        return jnp.pad(wt, ((0, fp - f4), (0, 0))).astype(BF16)

    for l in range(depth):
        pieces += [
            ((l, "wg1"), ffn_rows(ffn1_w_gate[l].T)), ((l, "wu1"), ffn_rows(ffn1_w_up[l].T)),
            ((l, "wd1"), ffn_rows(ffn1_w_down[l])), ((l, "w_in"), w_in[l].astype(BF16)),
            ((l, "w3"), jnp.stack([w_attn_proj[l], w_glu_v[l], w_glu_g[l]]).astype(BF16)),
            ((l, "w_out"), w_out[l].astype(BF16)),
            ((l, "wg2"), ffn_rows(ffn2_w_gate[l].T)), ((l, "wu2"), ffn_rows(ffn2_w_up[l].T)),
            ((l, "wd2"), ffn_rows(ffn2_w_down[l]))]
        lb_re, lb_im, bb_re, bb_im = _ssm_params(ssm_a_re[l], ssm_a_im[l], ssm_log_dt[l], ssm_b_re[l], ssm_b_im[l])
        ngrp = lb_re.shape[0]
        nt = ngrp // GROUPS_PER_TILE
        ssm_tabs = (lb_re.reshape(nt, 1, TILE_STATES), lb_im.reshape(nt, 1, TILE_STATES),
                    *_ssm_tables(bb_re, bb_im, ssm_c_re[l], ssm_c_im[l]))
        layers.append(dict(
            ffn1_norm=ffn1_norm[l], mix_norm=mix_norm[l], ffn2_norm=ffn2_norm[l], attn_sinks=attn_sinks[l],
            ssm_d=ssm_d[l], ssm_tabs=ssm_tabs, f4=f4))
    ws = _WeightStream(pieces)
    ws.get((0, "wu1"))

    h = jnp.concatenate([jnp.zeros((PAD_FRONT, d), F32), meta_full, x[0]], axis=0)
    saved = []
    for l in range(depth):
        h, s = _layer_fwd(h, l, layers[l], ws, tabs)
        saved.append(s)
    dh, g_final, loss_acc = _loss_head(h, final_norm, loss_target[0])
    loss = lax.psum(loss_acc[0, 0], ("x", "y", "c"))

    grads, reds = [None] * depth, [None] * depth
    for l in reversed(range(depth)):
        dh, grads[l], reds[l] = _layer_bwd(dh, l, layers[l], ws, saved[l], tabs, pos)
    grad_x = dh[BLOCK:][None]
    dmeta_local = dh[PAD_FRONT:BLOCK]

    small = {k: [] for k in SMALL}
    for l in range(depth):
        gl = grads[l]
        _, vjp = jax.vjp(_ssm_params, ssm_a_re[l], ssm_a_im[l], ssm_log_dt[l], ssm_b_re[l], ssm_b_im[l])
        da_re, da_im, dlog_dt, db_re, db_im = vjp(gl["ssm_lam"])
        for k, val in (("ffn1_norm", gl["ffn1_norm"][0]), ("mix_norm", gl["mix_norm"][0]),
                       ("attn_sinks", gl["attn_sinks"]), ("ssm_a_re", da_re), ("ssm_a_im", da_im),
                       ("ssm_log_dt", dlog_dt), ("ssm_b_re", db_re), ("ssm_b_im", db_im),
                       ("ssm_c_re", gl["ssm_c_re"]), ("ssm_c_im", gl["ssm_c_im"]), ("ssm_d", gl["ssm_d"]),
                       ("ffn2_norm", gl["ffn2_norm"][0])):
            small[k].append(val)
    small_local = {k: jnp.stack(vals) for k, vals in small.items() if k != "final_norm"}
    small_local["final_norm"] = g_final[0]
    small_local["meta_tokens"] = dmeta_local
    like = dict(small_local)
    g_small = _sum_devices(_all_gather_devices(_pack_small(small_local), "gather_small_grads"), "sum_small_grads")
    g_small_tree = _unpack_small(g_small, like)
    d4 = d // N_CHIPS
    chip = 2 * xi + yi
    g_meta = lax.dynamic_slice_in_dim(g_small_tree["meta_tokens"], chip * d4, d4, axis=1)

    reduced = []
    for l in range(depth):
        mine, other = [], []
        for red in reds[l]:
            halves, sibling_halves = red.end()
            mine += halves
            other += sibling_halves
        reduced.append((mine, other))

    g_out, delta, new_m, new_v = {}, {}, {}, {}
    for i, k in enumerate(BIG):
        flip = (lambda t: jnp.swapaxes(t, 1, 2)) if k in TRANSPOSED else (lambda t: t)
        outs = _adamw_layers(
            flip(w[k]), flip(m[k]), flip(v[k]), [reduced[l][0][i] for l in range(depth)],
            [reduced[l][1][i] for l in range(depth)], pos, "adamw_" + k)
        g_out[k], delta[k], new_m[k], new_v[k] = [flip(t) for t in outs]
    small_names = SMALL + ["meta_tokens"]
    w_small = {k: w[k] for k in small_names}
    m_small = {k: m[k] for k in small_names}
    v_small = {k: v[k] for k in small_names}
    g_small_local = dict(g_small_tree)
    g_small_local["meta_tokens"] = g_meta
    d_s, m_s, v_s = _adamw_flat(_pack_small(w_small), _pack_small(g_small_local), _pack_small(m_small),
                                _pack_small(v_small), "adamw_small")
    for tree, packed in ((delta, d_s), (new_m, m_s), (new_v, v_s)):
        tree.update(_unpack_small(packed, w_small))
    for k in small_names:
        g_out[k] = g_small_local[k]

    return (loss, grad_x, *[g_out[k] for k in WEIGHTS], *[delta[k] for k in WEIGHTS],
            *[new_m[k] for k in WEIGHTS], *[new_v[k] for k in WEIGHTS])
```

```python
import functools
import math

import jax
import jax.numpy as jnp
from jax import lax
from jax.experimental import pallas as pl
from jax.experimental.pallas import tpu as pltpu

F32 = jnp.float32
BF16 = jnp.bfloat16

N_META = 16
HEAD_DIM = 64
N_Q_HEADS = 8
N_KV_HEADS = 2
Q_PER_KV = N_Q_HEADS // N_KV_HEADS
ATTN_WIDTH = N_Q_HEADS * HEAD_DIM
KV_WIDTH = N_KV_HEADS * HEAD_DIM
BLOCK = 128
PAD_FRONT = BLOCK - N_META
ROPE_THETA = 500000.0
ROT_DIM = HEAD_DIM // 4
SSM_GROUP = 16
SSM_STATE = 64
GROUPS_PER_TILE = 4
TILE_STATES = GROUPS_PER_TILE * SSM_STATE
LANES = 128
SUBLANES = 8
MXU_DIM = 256
EPS = 1e-6
NEG_INF = -1e30
N_CHIPS = 4

ADAM_LR = 0.001
ADAM_B1 = 0.9
ADAM_B2 = 0.999
ADAM_EPS = 1e-08
ADAM_WD = 0.01
ADAM_STEP = 10

VMEM_LIMIT = 56 * 1024 * 1024
MESH = pl.DeviceIdType.MESH


def _cparams(sem=None):
    return pltpu.CompilerParams(dimension_semantics=sem, vmem_limit_bytes=VMEM_LIMIT)


def _row_tile(rows, limit=512):
    best = None
    for t in range(128, limit + 1, 128):
        if rows % t == 0:
            best = t
    assert best is not None, rows
    return best


def _div_tile(rows, row_bytes, max_bytes=1 << 20, mult=8):
    best = None
    for t in range(mult, rows + 1, mult):
        if rows % t == 0 and t * row_bytes <= max_bytes:
            best = t
    if best is None:
        best = rows
    return best


def _dot(a, b, mode):
    if mode == "nn":
        dims = (((1,), (0,)), ((), ()))
    elif mode == "nt":
        dims = (((1,), (1,)), ((), ()))
    else:
        dims = (((0,), (0,)), ((), ()))
    return lax.dot_general(a.astype(BF16), b.astype(BF16), dims, preferred_element_type=F32)


def _sigmoid(x):
    return 1.0 / (1.0 + jnp.exp(-x))


_GELU_C = math.sqrt(2.0 / math.pi)


def _gelu(x):
    return 0.5 * x * (1.0 + jnp.tanh(_GELU_C * (x + 0.044715 * x * x * x)))


def _gelu_grad(x):
    t = jnp.tanh(_GELU_C * (x + 0.044715 * x * x * x))
    return 0.5 * (1.0 + t) + 0.5 * x * (1.0 - t * t) * _GELU_C * (1.0 + 3.0 * 0.044715 * x * x)


class _Comm:
    def __init__(self, tag, ins, out_shapes, sems, start, mid, finish):
        self.tag, self.ins, self.out_shapes, self.sems = tag, list(ins), list(out_shapes), list(sems)
        self.start, self.mid, self.finish = start, mid, finish


HBM_SPEC = pl.BlockSpec(memory_space=pltpu.HBM)


def _hosted_call(body, comm, *, out_shape, grid, in_specs, out_specs, scratch_shapes, sem, name, args):
    out_shape, in_specs, out_specs = list(out_shape), list(in_specs), list(out_specs)
    scratch_shapes = list(scratch_shapes)
    if comm is None:
        res = pl.pallas_call(
            body, out_shape=out_shape, grid=grid, in_specs=in_specs, out_specs=out_specs,
            scratch_shapes=scratch_shapes, compiler_params=_cparams(sem), name=name)(*args)
        return list(res), []
    n_in, n_out, n_sc = len(args), len(out_shape), len(scratch_shapes)
    nci, nco = len(comm.ins), len(comm.out_shapes)
    total = math.prod(grid)

    def wrapped(*refs):
        in_refs, cin = refs[:n_in], refs[n_in:n_in + nci]
        o0 = n_in + nci
        out_refs, cout = refs[o0:o0 + n_out], refs[o0 + n_out:o0 + n_out + nco]
        s0 = o0 + n_out + nco
        sc, csem = refs[s0:s0 + n_sc], refs[s0 + n_sc:]
        lin = 0
        for dim, size in enumerate(grid):
            lin = lin * size + pl.program_id(dim)

        @pl.when(lin == 0)
        def _():
            comm.start(cin, cout, csem)

        if comm.mid is not None:
            @pl.when(lin == total // 2)
            def _():
                comm.mid(cin, cout, csem)

        body(*in_refs, *out_refs, *sc)

        @pl.when(lin == total - 1)
        def _():
            comm.finish(cin, cout, csem)

    res = pl.pallas_call(
        wrapped, out_shape=out_shape + comm.out_shapes, grid=grid,
        in_specs=in_specs + [HBM_SPEC] * nci, out_specs=out_specs + [HBM_SPEC] * nco,
        scratch_shapes=scratch_shapes + comm.sems,
        compiler_params=_cparams(("arbitrary",) * len(grid)), name=name + "_" + comm.tag)(*args, *comm.ins)
    return list(res[:n_out]), list(res[n_out:])


def _matmul(name, grid, k_axis, ins, in_specs, pairs, acc_shapes, epilogue, out_shapes, out_specs, sem, comm=None):
    n_in, n_out, n_acc = len(ins), len(out_shapes), len(acc_shapes)

    def body(*refs):
        in_refs = refs[:n_in]
        out_refs = refs[n_in:n_in + n_out]
        acc_refs = refs[n_in + n_out:]
        if k_axis is None:
            accs = [None] * n_acc
            for ia, ib, mode, iacc in pairs:
                d = _dot(in_refs[ia][...], in_refs[ib][...], mode)
                accs[iacc] = d if accs[iacc] is None else accs[iacc] + d
            epilogue(accs, in_refs, out_refs)
            return
        k = pl.program_id(k_axis)

        @pl.when(k == 0)
        def _():
            for r in acc_refs:
                r[...] = jnp.zeros_like(r)

        for ia, ib, mode, iacc in pairs:
            acc_refs[iacc][...] += _dot(in_refs[ia][...], in_refs[ib][...], mode)

        @pl.when(k == pl.num_programs(k_axis) - 1)
        def _():
            epilogue([r[...] for r in acc_refs], in_refs, out_refs)

    scratch = [] if k_axis is None else [pltpu.VMEM(s, F32) for s in acc_shapes]
    outs, couts = _hosted_call(
        body, comm, out_shape=out_shapes, grid=grid, in_specs=in_specs, out_specs=out_specs,
        scratch_shapes=scratch, sem=sem, name=name, args=ins)
    return outs if comm is None else (outs, couts)


def _rms_fwd(h, g, name):
    lp, d = h.shape
    tm = _row_tile(lp)

    def body(h_ref, g_ref, n_ref):
        x = h_ref[...]
        r = lax.rsqrt(jnp.mean(x * x, axis=-1, keepdims=True) + EPS)
        n_ref[...] = (x * r * g_ref[...]).astype(BF16)

    return pl.pallas_call(
        body, out_shape=jax.ShapeDtypeStruct((lp, d), BF16), grid=(lp // tm,),
        in_specs=[pl.BlockSpec((tm, d), lambda i: (i, 0)), pl.BlockSpec((1, d), lambda i: (0, 0))],
        out_specs=pl.BlockSpec((tm, d), lambda i: (i, 0)),
        compiler_params=_cparams(("parallel",)), name=name)(h, g.reshape(1, d))


def _rms_bwd_math(dn, x, g):
    r = lax.rsqrt(jnp.mean(x * x, axis=-1, keepdims=True) + EPS)
    xh = x * r
    dxh = dn * g
    dx = r * (dxh - xh * jnp.mean(dxh * xh, axis=-1, keepdims=True))
    return dx, dn * xh


def _scale_cast(x, scale, name):
    lp, d = x.shape
    tm = _row_tile(lp)

    def body(x_ref, o_ref):
        o_ref[...] = (x_ref[...] * scale).astype(BF16)

    return pl.pallas_call(
        body, out_shape=jax.ShapeDtypeStruct((lp, d), BF16), grid=(lp // tm,),
        in_specs=[pl.BlockSpec((tm, d), lambda i: (i, 0))], out_specs=pl.BlockSpec((tm, d), lambda i: (i, 0)),
        compiler_params=_cparams(("parallel",)), name=name)(x)


def _ffn_up(h, gain, wgt, wut, comm=None):
    lp, d = h.shape
    fp = wgt.shape[1]
    tm = _row_tile(lp)
    n = _rms_fwd(h, gain, "rms_fwd_ffn")

    def up_epi(accs, in_refs, out_refs):
        a, b = accs
        sg = _sigmoid(a)
        silu = a * sg
        out_refs[0][...] = (b * sg * (1.0 + a * (1.0 - sg))).astype(BF16)
        out_refs[1][...] = silu.astype(BF16)
        out_refs[2][...] = (silu * b).astype(BF16)

    act = jax.ShapeDtypeStruct((lp, N_CHIPS * fp), BF16)
    w_spec = pl.BlockSpec((None, fp, d), lambda j, i: (j, 0, 0))
    res = _matmul(
        "ffn_up", (N_CHIPS, lp // tm), None, [n, wgt, wut],
        [pl.BlockSpec((tm, d), lambda j, i: (i, 0)), w_spec, w_spec],
        [(0, 1, "nt", 0), (0, 2, "nt", 1)], [(tm, fp)] * 2, up_epi,
        [act, act, act], [pl.BlockSpec((tm, fp), lambda j, i: (i, j))] * 3,
        ("parallel", "parallel"), comm)
    return (tuple(res), []) if comm is None else (tuple(res[0]), res[1])


def _ffn_down(s, wd, h, comm=None):
    lp, d = h.shape
    ff = s.shape[1]
    tm = _row_tile(lp)

    def down_epi(accs, in_refs, out_refs):
        out_refs[0][...] = in_refs[2][...] + 0.5 * accs[0]

    res = _matmul(
        "ffn_down", (lp // tm,), None, [s, wd.reshape(ff, d), h],
        [pl.BlockSpec((tm, ff), lambda i: (i, 0)), pl.BlockSpec((ff, d), lambda i: (0, 0)),
         pl.BlockSpec((tm, d), lambda i: (i, 0))],
        [(0, 1, "nn", 0)], [(tm, d)], down_epi,
        [jax.ShapeDtypeStruct((lp, d), F32)], [pl.BlockSpec((tm, d), lambda i: (i, 0))],
        ("parallel",), comm)
    return (res[0], []) if comm is None else (res[0][0], res[1])


def _tn_tiles(lp):
    return _row_tile(lp, 1408)


def _ffn_bwd(dh, h_in, gain, wgt, wut, wd, f4, saved, pos, comm=None):
    a, b, s = saved
    lp, d = h_in.shape
    fp = wgt.shape[1]
    ff = N_CHIPS * fp
    tm = _row_tile(lp)
    ni = lp // tm
    tk = _tn_tiles(lp)
    nk = lp // tk
    n = _rms_fwd(h_in, gain, "rms_fwd_ffn")

    def ds_epi(accs, in_refs, out_refs):
        ds = 0.5 * accs[0]
        out_refs[0][...] = (ds * in_refs[2][...].astype(F32)).astype(BF16)
        out_refs[1][...] = (ds * in_refs[3][...].astype(F32)).astype(BF16)

    act = jax.ShapeDtypeStruct((lp, ff), BF16)
    col_spec = pl.BlockSpec((tm, fp), lambda j, i: (i, j))
    res = _matmul(
        "ffn_bwd_ds", (N_CHIPS, ni), None, [dh, wd, a, b],
        [pl.BlockSpec((tm, d), lambda j, i: (i, 0)), pl.BlockSpec((None, fp, d), lambda j, i: (j, 0, 0)),
         col_spec, col_spec],
        [(0, 1, "nt", 0)], [(tm, fp)], ds_epi, [act, act], [col_spec, col_spec], ("parallel", "parallel"),
        comm)
    (da, db), couts = (res, []) if comm is None else res

    dw_shape = jax.ShapeDtypeStruct((N_CHIPS, f4, d), F32)
    dw_spec = pl.BlockSpec((None, f4, d), lambda j, k: (j, 0, 0))
    in_col = pl.BlockSpec((tk, fp), lambda j, k: (k, j))
    in_row = pl.BlockSpec((tk, d), lambda j, k: (k, 0))

    def dwd_epi(accs, in_refs, out_refs):
        out_refs[0][...] = 0.5 * accs[0][:f4]

    (dwd,) = _matmul(
        "ffn_dwd", (N_CHIPS, nk), 1, [s, dh], [in_col, in_row],
        [(0, 1, "tn", 0)], [(fp, d)], dwd_epi, [dw_shape], [dw_spec], ("parallel", "arbitrary"))

    def dwgu_epi(accs, in_refs, out_refs):
        for acc, o in zip(accs, out_refs):
            o[...] = acc[:f4]

    red_down = _Reduction([dwd], pos, "ffn_d")
    (dwg, dwu), red_down.got = _matmul(
        "ffn_dwgu", (N_CHIPS, nk), 1, [n, da, db], [in_row, in_col, in_col],
        [(1, 0, "tn", 0), (2, 0, "tn", 1)], [(fp, d)] * 2, dwgu_epi,
        [dw_shape, dw_shape], [dw_spec, dw_spec], ("parallel", "arbitrary"), red_down.comm())

    def dn_epi(accs, in_refs, out_refs):
        i = pl.program_id(0)
        dx, dgrow = _rms_bwd_math(accs[0], in_refs[5][...], in_refs[6][...])
        out_refs[0][...] = in_refs[4][...] + dx

        @pl.when(i == 0)
        def _():
            out_refs[1][...] = jnp.zeros_like(out_refs[1])

        out_refs[1][...] += jnp.sum(dgrow, axis=0, keepdims=True)

    red = _Reduction([dwg, dwu], pos, "ffn_gu")
    row_spec = pl.BlockSpec((tm, d), lambda i: (i, 0))
    act_spec = pl.BlockSpec((tm, ff), lambda i: (i, 0))
    w_spec = pl.BlockSpec((ff, d), lambda i: (0, 0))
    one_spec = pl.BlockSpec((1, d), lambda i: (0, 0))
    (dh_in, dgain), red.got = _matmul(
        "ffn_bwd_dn", (ni,), None, [da, wgt.reshape(ff, d), db, wut.reshape(ff, d), dh, h_in, gain.reshape(1, d)],
        [act_spec, w_spec, act_spec, w_spec, row_spec, row_spec, one_spec],
        [(0, 1, "nn", 0), (2, 3, "nn", 0)], [(tm, d)], dn_epi,
        [jax.ShapeDtypeStruct((lp, d), F32), jax.ShapeDtypeStruct((1, d), F32)],
        [row_spec, one_spec], ("arbitrary",), red.comm())
    return dh_in, dgain, [red, red_down], couts


def _rope_tables(lp):
    pos = jnp.arange(lp, dtype=F32) - float(PAD_FRONT)
    inv_freq = ROPE_THETA ** (-jnp.arange(0, ROT_DIM, 2, dtype=F32) / ROT_DIM)
    ang = pos[:, None] * inv_freq[None, :]
    cos, sin = jnp.cos(ang), jnp.sin(ang)
    half = ROT_DIM // 2
    ones = jnp.ones((lp, HEAD_DIM - ROT_DIM), F32)
    zeros_h = jnp.zeros((lp, half), F32)
    zeros_r = jnp.zeros((lp, HEAD_DIM - ROT_DIM), F32)
    c = jnp.concatenate([cos, cos, ones], axis=1)
    s1 = jnp.concatenate([-sin, zeros_h, zeros_r], axis=1)
    s2 = jnp.concatenate([zeros_h, sin, zeros_r], axis=1)
    reps = LANES // HEAD_DIM
    return jnp.stack([jnp.tile(c, (1, reps)), jnp.tile(s1, (1, reps)), jnp.tile(s2, (1, reps))])


def _rope(x, c, s1, s2):
    half = ROT_DIM // 2
    outs = []
    for ch in range(x.shape[1] // LANES):
        xc = x[:, ch * LANES:(ch + 1) * LANES]
        outs.append(xc * c + pltpu.roll(xc, LANES - half, 1) * s1 + pltpu.roll(xc, half, 1) * s2)
    return outs[0] if len(outs) == 1 else jnp.concatenate(outs, axis=1)


def _rope_t(dy, c, s1, s2):
    half = ROT_DIM // 2
    outs = []
    for ch in range(dy.shape[1] // LANES):
        dc = dy[:, ch * LANES:(ch + 1) * LANES]
        outs.append(dc * c + pltpu.roll(dc * s1, half, 1) + pltpu.roll(dc * s2, LANES - half, 1))
    return outs[0] if len(outs) == 1 else jnp.concatenate(outs, axis=1)


def _in_proj(n, w_in, tabs, ssm_w, comm=None):
    lp, d = n.shape
    inw = w_in.shape[1]
    tm = _row_tile(lp)
    o1 = ATTN_WIDTH
    o2 = o1 + KV_WIDTH
    o3 = o2 + KV_WIDTH
    o4 = o3 + ssm_w
    o5 = o4 + d

    def epi(accs, in_refs, out_refs):
        z = accs[0]
        c, s1, s2 = in_refs[2][0], in_refs[2][1], in_refs[2][2]
        out_refs[0][...] = _rope(z[:, :o1], c, s1, s2).astype(BF16)
        out_refs[1][...] = _rope(z[:, o1:o2], c, s1, s2).astype(BF16)
        out_refs[2][...] = z[:, o2:o3].astype(BF16)
        out_refs[3][...] = z[:, o3:o4]
        out_refs[4][...] = z[:, o4:o5]
        out_refs[5][...] = z[:, o5:]

    def rs(w, dt):
        return jax.ShapeDtypeStruct((lp, w), dt), pl.BlockSpec((tm, w), lambda i: (i, 0))

    shapes, specs = zip(rs(o1, BF16), rs(KV_WIDTH, BF16), rs(KV_WIDTH, BF16), rs(ssm_w, F32), rs(d, F32), rs(d, F32))
    res = _matmul(
        "mix_in_proj", (lp // tm,), None, [n, w_in, tabs],
        [pl.BlockSpec((tm, d), lambda i: (i, 0)), pl.BlockSpec((d, inw), lambda i: (0, 0)),
         pl.BlockSpec((3, tm, LANES), lambda i: (0, i, 0))],
        [(0, 1, "nn", 0)], [(tm, inw)], epi, list(shapes), list(specs), ("parallel",), comm)
    return (res, []) if comm is None else res


def _attn_mask(b):
    rows = lax.broadcasted_iota(jnp.int32, (BLOCK, 3 * BLOCK), 0)
    cols = lax.broadcasted_iota(jnp.int32, (BLOCK, 3 * BLOCK), 1)
    qpos = b * BLOCK + rows - PAD_FRONT
    kpos = (b - 1) * BLOCK + cols - PAD_FRONT
    dist = qpos - kpos
    band = (cols < 2 * BLOCK) & (kpos >= N_META) & (dist >= 0) & (dist < BLOCK)
    mrow = cols - 2 * BLOCK
    meta = (mrow >= PAD_FRONT) & ((mrow - PAD_FRONT) <= qpos)
    return band | meta


def _attn_probs(qh, kk, mask, sink):
    s = _dot(qh, kk, "nt") * (HEAD_DIM ** -0.5)
    s = jnp.where(mask, s, NEG_INF)
    m = jnp.maximum(jnp.max(s, axis=-1, keepdims=True), sink)
    e = jnp.exp(s - m)
    es = jnp.exp(sink - m)
    z = jnp.sum(e, axis=-1, keepdims=True) + es
    inv = 1.0 / z
    return e * inv, es * inv


def _head(ref_or_val, h):
    return ref_or_val[:, h * HEAD_DIM:(h + 1) * HEAD_DIM]


def _attn_fwd(q, k, v, sinks, comm=None):
    lp = q.shape[0]
    nb = lp // BLOCK

    def body(sink_ref, q_ref, kp_ref, kc_ref, km_ref, vp_ref, vc_ref, vm_ref, o_ref):
        b = pl.program_id(0)
        mask = _attn_mask(b)
        for hk in range(N_KV_HEADS):
            kk = jnp.concatenate([_head(kp_ref, hk), _head(kc_ref, hk), _head(km_ref, hk)], axis=0)
            vv = jnp.concatenate([_head(vp_ref, hk), _head(vc_ref, hk), _head(vm_ref, hk)], axis=0)
            for g in range(Q_PER_KV):
                h = hk * Q_PER_KV + g
                p, _ = _attn_probs(_head(q_ref, h), kk, mask, sink_ref[h])
                o_ref[:, h * HEAD_DIM:(h + 1) * HEAD_DIM] = _dot(p, vv, "nn").astype(BF16)

    cur = lambda b: (b, 0)
    prev = lambda b: (jnp.maximum(b - 1, 0), 0)
    first = lambda b: (0, 0)
    kvs = lambda f: pl.BlockSpec((BLOCK, KV_WIDTH), f)
    (o,), couts = _hosted_call(
        body, comm, out_shape=[jax.ShapeDtypeStruct((lp, ATTN_WIDTH), BF16)], grid=(nb,),
        in_specs=[pl.BlockSpec(memory_space=pltpu.SMEM), pl.BlockSpec((BLOCK, ATTN_WIDTH), cur),
                  kvs(prev), kvs(cur), kvs(first), kvs(prev), kvs(cur), kvs(first)],
        out_specs=[pl.BlockSpec((BLOCK, ATTN_WIDTH), cur)], scratch_shapes=[],
        sem=("parallel",), name="attn_fwd", args=(sinks, q, k, k, k, v, v, v))
    return o, couts


def _attn_bwd(q, k, v, do, sinks, tabs, comm=None):
    lp = q.shape[0]
    nb = lp // BLOCK
    scale = HEAD_DIM ** -0.5

    def body(sink_ref, q_ref, do_ref, kp_ref, kc_ref, km_ref, vp_ref, vc_ref, vm_ref, tq_ref, tk_ref, t0_ref,
             dq_ref, dk_ref, dv_ref, dkm_ref, dvm_ref, dsink_ref,
             dq_s, dkk_s, dvv_s, ck_s, cv_s, mk_s, mv_s):
        b = pl.program_id(0)

        @pl.when(b == 0)
        def _():
            for r in (ck_s, cv_s, mk_s, mv_s, dsink_ref):
                r[...] = jnp.zeros_like(r)

        @pl.when(b < nb)
        def _():
            mask = _attn_mask(b)
            for hk in range(N_KV_HEADS):
                kk = jnp.concatenate([_head(kp_ref, hk), _head(kc_ref, hk), _head(km_ref, hk)], axis=0)
                vv = jnp.concatenate([_head(vp_ref, hk), _head(vc_ref, hk), _head(vm_ref, hk)], axis=0)
                dkk = jnp.zeros((3 * BLOCK, HEAD_DIM), F32)
                dvv = jnp.zeros((3 * BLOCK, HEAD_DIM), F32)
                for g in range(Q_PER_KV):
                    h = hk * Q_PER_KV + g
                    qh = _head(q_ref, h)
                    doh = _head(do_ref, h)
                    p, ps = _attn_probs(qh, kk, mask, sink_ref[h])
                    dp = _dot(doh, vv, "nt")
                    delta = jnp.sum(p * dp, axis=-1, keepdims=True)
                    ds = (p * (dp - delta)).astype(BF16)
                    dsink_ref[h:h + 1, :] += jnp.zeros((1, LANES), F32) - jnp.sum(ps * delta)
                    dq_s[:, h * HEAD_DIM:(h + 1) * HEAD_DIM] = _dot(ds, kk, "nn") * scale
                    dkk = dkk + _dot(ds, qh, "tn") * scale
                    dvv = dvv + _dot(p, doh, "tn")
                dkk_s[:, hk * HEAD_DIM:(hk + 1) * HEAD_DIM] = dkk
                dvv_s[:, hk * HEAD_DIM:(hk + 1) * HEAD_DIM] = dvv
            dq_ref[...] = _rope_t(dq_s[...], tq_ref[0], tq_ref[1], tq_ref[2])
            dk_ref[...] = _rope_t(ck_s[...] + dkk_s[0:BLOCK, :], tk_ref[0], tk_ref[1], tk_ref[2])
            dv_ref[...] = cv_s[...] + dvv_s[0:BLOCK, :]
            ck_s[...] = dkk_s[BLOCK:2 * BLOCK, :]
            cv_s[...] = dvv_s[BLOCK:2 * BLOCK, :]
            mk_s[...] += dkk_s[2 * BLOCK:, :]
            mv_s[...] += dvv_s[2 * BLOCK:, :]

        @pl.when(b == nb)
        def _():
            dk_ref[...] = _rope_t(ck_s[...], tk_ref[0], tk_ref[1], tk_ref[2])
            dv_ref[...] = cv_s[...]
            dkm_ref[...] = _rope_t(mk_s[...], t0_ref[0], t0_ref[1], t0_ref[2])
            dvm_ref[...] = mv_s[...]

    cur = lambda b: (jnp.minimum(b, nb - 1), 0)
    prev = lambda b: (jnp.clip(b - 1, 0, nb - 1), 0)
    first = lambda b: (0, 0)
    kvs = lambda f: pl.BlockSpec((BLOCK, KV_WIDTH), f)
    tab = lambda f: pl.BlockSpec((3, BLOCK, LANES), lambda b: (0,) + f(b)[:1] + (0,))
    kv_out = lambda b: (jnp.maximum(b - 1, 0), 0)
    return _hosted_call(
        body, comm,
        out_shape=[jax.ShapeDtypeStruct((lp, ATTN_WIDTH), F32), jax.ShapeDtypeStruct((lp, KV_WIDTH), F32),
                   jax.ShapeDtypeStruct((lp, KV_WIDTH), F32), jax.ShapeDtypeStruct((BLOCK, KV_WIDTH), F32),
                   jax.ShapeDtypeStruct((BLOCK, KV_WIDTH), F32), jax.ShapeDtypeStruct((N_Q_HEADS, LANES), F32)],
        grid=(nb + 1,),
        in_specs=[pl.BlockSpec(memory_space=pltpu.SMEM), pl.BlockSpec((BLOCK, ATTN_WIDTH), cur),
                  pl.BlockSpec((BLOCK, ATTN_WIDTH), cur),
                  kvs(prev), kvs(cur), kvs(first), kvs(prev), kvs(cur), kvs(first),
                  tab(cur), tab(kv_out), tab(first)],
        out_specs=[pl.BlockSpec((BLOCK, ATTN_WIDTH), cur), kvs(kv_out), kvs(kv_out), kvs(first), kvs(first),
                   pl.BlockSpec((N_Q_HEADS, LANES), first)],
        scratch_shapes=[pltpu.VMEM((BLOCK, ATTN_WIDTH), F32), pltpu.VMEM((3 * BLOCK, KV_WIDTH), F32),
                        pltpu.VMEM((3 * BLOCK, KV_WIDTH), F32), pltpu.VMEM((BLOCK, KV_WIDTH), F32),
                        pltpu.VMEM((BLOCK, KV_WIDTH), F32), pltpu.VMEM((BLOCK, KV_WIDTH), F32),
                        pltpu.VMEM((BLOCK, KV_WIDTH), F32)],
        sem=("arbitrary",), name="attn_bwd", args=(sinks, q, do, k, k, k, v, v, v, tabs, tabs, tabs))


def _cmul(ar, ai, br, bi):
    return ar * br - ai * bi, ar * bi + ai * br


def _cpow(lr, li, n):
    rr = ri = None
    br, bi = lr, li
    while n:
        if n & 1:
            rr, ri = (br, bi) if rr is None else _cmul(rr, ri, br, bi)
        n >>= 1
        if n:
            br, bi = _cmul(br, bi, br, bi)
    return rr, ri


def _shift_rows(x, d, reverse):
    rows = lax.broadcasted_iota(jnp.int32, x.shape, 0)
    if not reverse:
        return jnp.where(rows >= d, pltpu.roll(x, d, 0), 0.0)
    return jnp.where(rows < SUBLANES - d, pltpu.roll(x, SUBLANES - d, 0), 0.0)


def _sublane_powers(mr, mi, reverse):
    rows = lax.broadcasted_iota(jnp.int32, mr.shape, 0)
    e = SUBLANES - 1 - rows if reverse else rows
    pr, pi = jnp.ones_like(mr), jnp.zeros_like(mr)
    br, bi = mr, mi
    for d in (1, 2, 4):
        tr, ti = _cmul(pr, pi, br, bi)
        on = (e & d) != 0
        pr, pi = jnp.where(on, tr, pr), jnp.where(on, ti, pi)
        if d < 4:
            br, bi = _cmul(br, bi, br, bi)
    return pr, pi


def _inclusive_prefix(er, ei, mr, mi, reverse):
    ir, ii, pr, pi = er, ei, mr, mi
    for d in (1, 2, 4):
        tr, ti = _cmul(pr, pi, _shift_rows(ir, d, reverse), _shift_rows(ii, d, reverse))
        ir, ii = ir + tr, ii + ti
        if d < 4:
            pr, pi = _cmul(pr, pi, pr, pi)
    return ir, ii


def _chain_rows(a, t, seg):
    return pl.ds(a * SUBLANES * seg + t, SUBLANES, stride=seg)


def _seg_scan(xr_ref, xi_ref, lam, seg, nchain, reverse, store, init, extra=None):
    nt = len(lam)
    acc0 = () if extra is None else extra[1]

    def step(i, carry):
        hs, acc = carry
        t = seg - 1 - i if reverse else i
        out = []
        for a in range(nchain):
            sl = _chain_rows(a, t, seg)
            for j in range(nt):
                lr, li = lam[j]
                k = 2 * (a * nt + j)
                hr, hi = hs[k], hs[k + 1]
                nr = lr * hr - li * hi + xr_ref[j, sl, :]
                ni = lr * hi + li * hr + xi_ref[j, sl, :]
                if store:
                    xr_ref[j, sl, :] = nr
                    xi_ref[j, sl, :] = ni
                if extra is not None:
                    acc = extra[0](t, a, j, nr, ni, acc)
                out += [nr, ni]
        return tuple(out), acc

    return lax.fori_loop(0, seg, step, (tuple(init), acc0))


def _ssm_scan(xr_ref, xi_ref, lam, seg, nchain, reverse, extra=None):
    nt = len(lam)
    zero = [jnp.zeros((SUBLANES, LANES), F32)] * (2 * nt * nchain)
    ends, _ = _seg_scan(xr_ref, xi_ref, lam, seg, nchain, reverse, False, zero)
    init = [None] * (2 * nt * nchain)
    last = 0 if reverse else SUBLANES - 1
    for j in range(nt):
        mr, mi = _cpow(lam[j][0], lam[j][1], seg)
        m8r, m8i = _cpow(mr, mi, SUBLANES)
        pwr, pwi = _sublane_powers(mr, mi, reverse)
        gr = gi = jnp.zeros((SUBLANES, LANES), F32)
        for a in (reversed(range(nchain)) if reverse else range(nchain)):
            k = 2 * (a * nt + j)
            incr, inci = _inclusive_prefix(ends[k], ends[k + 1], mr, mi, reverse)
            tr, ti = _cmul(pwr, pwi, gr, gi)
            init[k] = _shift_rows(incr, 1, reverse) + tr
            init[k + 1] = _shift_rows(inci, 1, reverse) + ti
            g2r, g2i = _cmul(m8r, m8i, gr, gi)
            gr = g2r + jnp.broadcast_to(incr[last:last + 1, :], gr.shape)
            gi = g2i + jnp.broadcast_to(inci[last:last + 1, :], gi.shape)
    _, acc = _seg_scan(xr_ref, xi_ref, lam, seg, nchain, reverse, True, init, extra)
    return acc


def _diag_mask():
    steps = LANES // SSM_GROUP // GROUPS_PER_TILE
    return (jnp.eye(steps, dtype=F32)[:, None, :, None] * jnp.eye(GROUPS_PER_TILE, dtype=F32)[None, :, None, :])


def _ssm_tables(bb_re, bb_im, c_re, c_im):
    g = bb_re.shape[0]
    nt = g // GROUPS_PER_TILE
    steps = LANES // SSM_GROUP // GROUPS_PER_TILE
    mask = _diag_mask()

    def b_tab(bb):
        x = bb.reshape(nt // steps, steps, GROUPS_PER_TILE, SSM_STATE, SSM_GROUP)
        x = jnp.transpose(x, (0, 1, 4, 2, 3))[:, :, None, None]
        m = jnp.transpose(mask, (0, 2, 3, 1))[None, :, :, :, None, :, None]
        return (x * m).reshape(nt, LANES, TILE_STATES)

    def c_tab(c):
        x = c.reshape(nt // steps, steps, GROUPS_PER_TILE, SSM_GROUP, SSM_STATE)
        x = jnp.transpose(x, (0, 1, 2, 4, 3))[:, :, :, :, None, None]
        m = mask[None, :, :, None, :, :, None]
        return (x * m).reshape(nt, TILE_STATES, LANES)

    return b_tab(bb_re), b_tab(bb_im), c_tab(c_re), c_tab(c_im)


def _ssm_untable_b(db, g):
    nt = g // GROUPS_PER_TILE
    steps = LANES // SSM_GROUP // GROUPS_PER_TILE
    x = db.reshape(nt // steps, steps, GROUPS_PER_TILE, SSM_STATE, steps, GROUPS_PER_TILE, SSM_GROUP)
    m = _diag_mask()[None, :, :, None, :, :, None]
    return jnp.sum(x * m, axis=(4, 5)).reshape(g, SSM_STATE, SSM_GROUP)


def _ssm_untable_c(dc, g):
    nt = g // GROUPS_PER_TILE
    steps = LANES // SSM_GROUP // GROUPS_PER_TILE
    x = dc.reshape(nt // steps, steps, steps, GROUPS_PER_TILE, SSM_GROUP, GROUPS_PER_TILE, SSM_STATE)
    m = jnp.transpose(_diag_mask(), (0, 2, 3, 1))[None, :, :, :, None, :, None]
    out = jnp.sum(x * m, axis=(2, 3))
    return jnp.transpose(out, (0, 1, 3, 2, 4)).reshape(g, SSM_GROUP, SSM_STATE)


def _lam_tiles(lam_ref):
    out = []
    for j in range(TILE_STATES // LANES):
        out.append(jnp.broadcast_to(lam_ref[:, j * LANES:(j + 1) * LANES], (SUBLANES, LANES)))
    return out


def _scan_chains(lp):
    for n in (4, 2, 1):
        if lp % (SUBLANES * n) == 0 and (lp // SUBLANES) % 16 == 0:
            return n
    raise ValueError(lp)


def _split_tiles(dst_ref, rows, val):
    for j in range(val.shape[1] // LANES):
        dst_ref[j, rows, :] = val[:, j * LANES:(j + 1) * LANES]


def _cat_tiles(src_ref, rows):
    njt = src_ref.shape[0]
    return jnp.concatenate([src_ref[j, rows, :] for j in range(njt)], axis=1).astype(BF16)


def _ssm_fwd(u, lam_re, lam_im, tb_re, tb_im, tc_re, tc_im, d_skip, comm=None):
    lp, w = u.shape
    nt = tb_re.shape[0]
    nchain = _scan_chains(lp)
    seg = lp // (SUBLANES * nchain)
    chunk = lp // SUBLANES
    njt = TILE_STATES // LANES

    def body(u_ref, lr_ref, li_ref, br_ref, bi_ref, cr_ref, ci_ref, d_ref, y_ref, xr, xi):
        t = pl.program_id(0)
        for s in range(SUBLANES):
            rs = pl.ds(s * chunk, chunk)
            ub = u_ref[rs, :].astype(BF16)
            _split_tiles(xr, rs, _dot(ub, br_ref[...], "nn"))
            _split_tiles(xi, rs, _dot(ub, bi_ref[...], "nn"))
        lrs, lis = _lam_tiles(lr_ref), _lam_tiles(li_ref)
        _ssm_scan(xr, xi, list(zip(lrs, lis)), seg, nchain, False)
        for s in range(SUBLANES):
            rs = pl.ds(s * chunk, chunk)
            y = _dot(_cat_tiles(xr, rs), cr_ref[...], "nn") - _dot(_cat_tiles(xi, rs), ci_ref[...], "nn")

            @pl.when(t % 2 == 0)
            def _():
                y_ref[rs, :] = y + d_ref[...] * u_ref[rs, :]

            @pl.when(t % 2 == 1)
            def _():
                y_ref[rs, :] += y

    blk = pl.BlockSpec((lp, LANES), lambda t: (0, t // 2))
    lam_spec = pl.BlockSpec((None, 1, TILE_STATES), lambda t: (t, 0, 0))
    b_spec = pl.BlockSpec((None, LANES, TILE_STATES), lambda t: (t, 0, 0))
    c_spec = pl.BlockSpec((None, TILE_STATES, LANES), lambda t: (t, 0, 0))
    (y,), couts = _hosted_call(
        body, comm, out_shape=[jax.ShapeDtypeStruct((lp, w), F32)], grid=(nt,),
        in_specs=[blk, lam_spec, lam_spec, b_spec, b_spec, c_spec, c_spec,
                  pl.BlockSpec((1, LANES), lambda t: (0, t // 2))],
        out_specs=[blk],
        scratch_shapes=[pltpu.VMEM((njt, lp, LANES), F32), pltpu.VMEM((njt, lp, LANES), F32)],
        sem=("arbitrary",), name="ssm_fwd",
        args=(u, lam_re, lam_im, tb_re, tb_im, tc_re, tc_im, d_skip.reshape(1, w)))
    return y, couts


def _ssm_bwd(u, dy, lam_re, lam_im, tb_re, tb_im, tc_re, tc_im, d_skip, comm=None):
    lp, w = u.shape
    nt = tb_re.shape[0]
    nchain = _scan_chains(lp)
    seg = lp // (SUBLANES * nchain)
    chunk = lp // SUBLANES
    njt = TILE_STATES // LANES
    tbt_re, tbt_im = jnp.swapaxes(tb_re, 1, 2), jnp.swapaxes(tb_im, 1, 2)
    tct_re, tct_im = jnp.swapaxes(tc_re, 1, 2), jnp.swapaxes(tc_im, 1, 2)

    def body(u_ref, dy_ref, lr_ref, li_ref, br_ref, bi_ref, btr_ref, bti_ref, ctr_ref, cti_ref, d_ref,
             du_ref, dlr_ref, dli_ref, dbr_ref, dbi_ref, dcr_ref, dci_ref, dd_ref, hr, hi, ar, ai):
        t = pl.program_id(0)
        lrs, lis = _lam_tiles(lr_ref), _lam_tiles(li_ref)
        for s in range(SUBLANES):
            rs = pl.ds(s * chunk, chunk)
            ub = u_ref[rs, :].astype(BF16)
            dyb = dy_ref[rs, :].astype(BF16)
            _split_tiles(hr, rs, _dot(ub, br_ref[...], "nn"))
            _split_tiles(hi, rs, _dot(ub, bi_ref[...], "nn"))
            _split_tiles(ar, rs, _dot(dyb, ctr_ref[...], "nn"))
            _split_tiles(ai, rs, -_dot(dyb, cti_ref[...], "nn"))
        _ssm_scan(hr, hi, list(zip(lrs, lis)), seg, nchain, False)

        def dlam_step(tt, a, j, a_r, a_i, acc):
            sl = _chain_rows(a, jnp.maximum(tt - 1, 0), seg)
            p_r, p_i = hr[j, sl, :], hi[j, sl, :]
            acc = list(acc)
            acc[2 * j] = acc[2 * j] + jnp.where(tt > 0, a_r * p_r + a_i * p_i, 0.0)
            acc[2 * j + 1] = acc[2 * j + 1] + jnp.where(tt > 0, a_i * p_r - a_r * p_i, 0.0)
            return tuple(acc)

        zero = tuple([jnp.zeros((SUBLANES, LANES), F32)] * (2 * njt))
        conj = [(lr, -li) for lr, li in zip(lrs, lis)]
        acc = list(_ssm_scan(ar, ai, conj, seg, nchain, True, (dlam_step, zero)))
        row0 = lax.broadcasted_iota(jnp.int32, (SUBLANES, LANES), 0) == 0
        for j in range(njt):
            cs = slice(j * LANES, (j + 1) * LANES)
            for a in range(nchain):
                p_r = _shift_rows(hr[j, _chain_rows(a, seg - 1, seg), :], 1, False)
                p_i = _shift_rows(hi[j, _chain_rows(a, seg - 1, seg), :], 1, False)
                if a > 0:
                    before = pl.ds(a * SUBLANES * seg - 1, 1)
                    p_r = jnp.where(row0, jnp.broadcast_to(hr[j, before, :], p_r.shape), p_r)
                    p_i = jnp.where(row0, jnp.broadcast_to(hi[j, before, :], p_i.shape), p_i)
                a_r, a_i = ar[j, _chain_rows(a, 0, seg), :], ai[j, _chain_rows(a, 0, seg), :]
                acc[2 * j] = acc[2 * j] + a_r * p_r + a_i * p_i
                acc[2 * j + 1] = acc[2 * j + 1] + a_i * p_r - a_r * p_i
            dlr_ref[:, cs] = jnp.sum(acc[2 * j], axis=0, keepdims=True)
            dli_ref[:, cs] = jnp.sum(acc[2 * j + 1], axis=0, keepdims=True)

        dd = jnp.zeros((1, LANES), F32)
        for s in range(SUBLANES):
            rs = pl.ds(s * chunk, chunk)
            ub = u_ref[rs, :].astype(BF16)
            dyv = dy_ref[rs, :]
            dyb = dyv.astype(BF16)
            arb, aib = _cat_tiles(ar, rs), _cat_tiles(ai, rs)
            hrb, hib = _cat_tiles(hr, rs), _cat_tiles(hi, rs)
            du = _dot(arb, btr_ref[...], "nn") + _dot(aib, bti_ref[...], "nn")
            upd = [(dbr_ref, _dot(arb, ub, "tn")), (dbi_ref, _dot(aib, ub, "tn")),
                   (dcr_ref, _dot(dyb, hrb, "tn")), (dci_ref, -_dot(dyb, hib, "tn"))]
            for ref, val in upd:
                if s == 0:
                    ref[...] = val
                else:
                    ref[...] += val
            rows = lax.broadcasted_iota(jnp.int32, (chunk, LANES), 0) + s * chunk
            keep = rows >= PAD_FRONT
            dd = dd + jnp.sum(dyv * u_ref[rs, :], axis=0, keepdims=True)

            @pl.when(t % 2 == 0)
            def _():
                du_ref[rs, :] = jnp.where(keep, du + d_ref[...] * dyv, 0.0)

            @pl.when(t % 2 == 1)
            def _():
                du_ref[rs, :] += jnp.where(keep, du, 0.0)

        @pl.when(t % 2 == 0)
        def _():
            dd_ref[...] = dd

    blk = pl.BlockSpec((lp, LANES), lambda t: (0, t // 2))
    vec = pl.BlockSpec((1, LANES), lambda t: (0, t // 2))
    lam_spec = pl.BlockSpec((None, 1, TILE_STATES), lambda t: (t, 0, 0))
    b_spec = pl.BlockSpec((None, LANES, TILE_STATES), lambda t: (t, 0, 0))
    c_spec = pl.BlockSpec((None, TILE_STATES, LANES), lambda t: (t, 0, 0))
    lam_shape = jax.ShapeDtypeStruct((nt, 1, TILE_STATES), F32)
    bt_shape = jax.ShapeDtypeStruct((nt, TILE_STATES, LANES), F32)
    ct_shape = jax.ShapeDtypeStruct((nt, LANES, TILE_STATES), F32)
    st = pltpu.VMEM((njt, lp, LANES), F32)
    return _hosted_call(
        body, comm,
        out_shape=[jax.ShapeDtypeStruct((lp, w), F32), lam_shape, lam_shape, bt_shape, bt_shape, ct_shape, ct_shape,
                   jax.ShapeDtypeStruct((1, w), F32)],
        grid=(nt,),
        in_specs=[blk, blk, lam_spec, lam_spec, b_spec, b_spec, c_spec, c_spec, b_spec, b_spec, vec],
        out_specs=[blk, lam_spec, lam_spec, c_spec, c_spec, b_spec, b_spec, vec],
        scratch_shapes=[st, st, st, st], sem=("arbitrary",), name="ssm_bwd",
        args=(u, dy, lam_re, lam_im, tb_re, tb_im, tbt_re, tbt_im, tct_re, tct_im, d_skip.reshape(1, w)))


def _ssm_params(a_re, a_im, log_dt, b_re, b_im):
    dt = jnp.exp(log_dt)[:, None]
    mag = jnp.exp(a_re * dt)
    lb_re = mag * jnp.cos(a_im * dt)
    lb_im = mag * jnp.sin(a_im * dt)
    den = a_re * a_re + a_im * a_im
    num_re = lb_re - 1.0
    coef_re = (num_re * a_re + lb_im * a_im) / den
    coef_im = (lb_im * a_re - num_re * a_im) / den
    bb_re = coef_re[..., None] * b_re - coef_im[..., None] * b_im
    bb_im = coef_re[..., None] * b_im + coef_im[..., None] * b_re
    return lb_re, lb_im, bb_re, bb_im


def _gelu_fwd(y):
    lp, w = y.shape
    tm = _row_tile(lp)

    def body(y_ref, o_ref):
        o_ref[...] = _gelu(y_ref[...]).astype(BF16)

    return pl.pallas_call(
        body, out_shape=jax.ShapeDtypeStruct((lp, w), BF16), grid=(lp // tm,),
        in_specs=[pl.BlockSpec((tm, w), lambda i: (i, 0))], out_specs=pl.BlockSpec((tm, w), lambda i: (i, 0)),
        compiler_params=_cparams(("parallel",)), name="gelu_fwd")(y)


def _merge_fwd(o, yg, ga, gs, w3, comm=None):
    lp, d = ga.shape
    d4 = w3.shape[3]
    kw = w3.shape[2]
    tm = _row_tile(lp)

    def epi(accs, in_refs, out_refs):
        attn, vv, gg = accs
        out_refs[0][...] = (_sigmoid(in_refs[5][...]) * attn
                            + _sigmoid(in_refs[6][...]) * (vv * _sigmoid(gg))).astype(BF16)

    wspec = lambda which: pl.BlockSpec((None, None, kw, d4), lambda j, i: (j, which, 0, 0))
    colspec = pl.BlockSpec((tm, d4), lambda j, i: (i, j))
    aspec = pl.BlockSpec((tm, kw), lambda j, i: (i, 0))
    res = _matmul(
        "merge_fwd", (N_CHIPS, lp // tm), None, [o, yg, w3, w3, w3, ga, gs],
        [aspec, aspec, wspec(0), wspec(1), wspec(2), colspec, colspec],
        [(0, 2, "nn", 0), (1, 3, "nn", 1), (1, 4, "nn", 2)], [(tm, d4)] * 3, epi,
        [jax.ShapeDtypeStruct((lp, d), BF16)], [colspec], ("parallel", "parallel"), comm)
    return (res[0], []) if comm is None else (res[0][0], res[1])


def _out_proj(merged, w_out, h, comm=None):
    lp, d = h.shape
    d4 = w_out.shape[1]
    tm = _row_tile(lp)

    def epi(accs, in_refs, out_refs):
        out_refs[0][...] = in_refs[2][...] + accs[0]

    res = _matmul(
        "mix_out_proj", (lp // tm, N_CHIPS), 1, [merged, w_out, h],
        [pl.BlockSpec((tm, d4), lambda i, j: (i, j)), pl.BlockSpec((None, d4, d), lambda i, j: (j, 0, 0)),
         pl.BlockSpec((tm, d), lambda i, j: (i, 0))],
        [(0, 1, "nn", 0)], [(tm, d)], epi,
        [jax.ShapeDtypeStruct((lp, d), F32)], [pl.BlockSpec((tm, d), lambda i, j: (i, 0))],
        ("parallel", "arbitrary"), comm)
    return (res[0], []) if comm is None else (res[0][0], res[1])


def _merge_bwd(dhb, w_out, o, yg, ga, gs, w3):
    lp, d = ga.shape
    d4 = w3.shape[3]
    kw = w3.shape[2]
    tm = _row_tile(lp)

    def epi(accs, in_refs, out_refs):
        dm, attn, vv, gg = accs
        sa = _sigmoid(in_refs[7][...])
        ss = _sigmoid(in_refs[8][...])
        sg = _sigmoid(gg)
        ssm = vv * sg
        dssm = dm * ss
        out_refs[0][...] = (dm * sa).astype(BF16)
        out_refs[1][...] = (dssm * sg).astype(BF16)
        out_refs[2][...] = (dssm * vv * sg * (1.0 - sg)).astype(BF16)
        out_refs[3][...] = (dm * attn * sa * (1.0 - sa)).astype(BF16)
        out_refs[4][...] = (dm * ssm * ss * (1.0 - ss)).astype(BF16)

    wspec = lambda which: pl.BlockSpec((None, None, kw, d4), lambda j, i: (j, which, 0, 0))
    colspec = pl.BlockSpec((tm, d4), lambda j, i: (i, j))
    aspec = pl.BlockSpec((tm, kw), lambda j, i: (i, 0))
    shp = jax.ShapeDtypeStruct((lp, d), BF16)
    return _matmul(
        "merge_bwd", (N_CHIPS, lp // tm), None, [dhb, w_out, o, yg, w3, w3, w3, ga, gs],
        [pl.BlockSpec((tm, d), lambda j, i: (i, 0)), pl.BlockSpec((None, d4, d), lambda j, i: (j, 0, 0)),
         aspec, aspec, wspec(0), wspec(1), wspec(2), colspec, colspec],
        [(0, 1, "nt", 0), (2, 4, "nn", 1), (3, 5, "nn", 2), (3, 6, "nn", 3)], [(tm, d4)] * 4, epi,
        [shp] * 5, [colspec] * 5, ("parallel", "parallel"))


def _branch_bwd(dattn, dv, dg, w3, y):
    lp, d = dattn.shape
    d4 = w3.shape[3]
    kw = w3.shape[2]
    tm = _row_tile(lp)

    def epi(accs, in_refs, out_refs):
        out_refs[0][...] = accs[0].astype(BF16)
        out_refs[1][...] = accs[1] * _gelu_grad(in_refs[6][...])

    wspec = lambda which: pl.BlockSpec((None, None, kw, d4), lambda i, j: (j, which, 0, 0))
    colspec = pl.BlockSpec((tm, d4), lambda i, j: (i, j))
    rowspec = pl.BlockSpec((tm, kw), lambda i, j: (i, 0))
    return _matmul(
        "branch_bwd", (lp // tm, N_CHIPS), 1, [dattn, dv, dg, w3, w3, w3, y],
        [colspec, colspec, colspec, wspec(0), wspec(1), wspec(2), rowspec],
        [(0, 3, "nt", 0), (1, 4, "nt", 1), (2, 5, "nt", 1)], [(tm, kw)] * 2, epi,
        [jax.ShapeDtypeStruct((lp, kw), BF16), jax.ShapeDtypeStruct((lp, kw), F32)], [rowspec, rowspec],
        ("parallel", "arbitrary"))


def _tn_cols(x, ys, name):
    lp, kx = x.shape
    n = ys[0].shape[1]
    n4 = n // N_CHIPS
    tk = _tn_tiles(lp)

    def epi(accs, in_refs, out_refs):
        for acc, o in zip(accs, out_refs):
            o[...] = acc

    shp = jax.ShapeDtypeStruct((N_CHIPS, kx, n4), F32)
    return _matmul(
        name, (N_CHIPS, lp // tk), 1, [x] + list(ys),
        [pl.BlockSpec((tk, kx), lambda j, k: (k, 0))] + [pl.BlockSpec((tk, n4), lambda j, k: (k, j))] * len(ys),
        [(0, 1 + i, "tn", i) for i in range(len(ys))], [(kx, n4)] * len(ys), epi,
        [shp] * len(ys), [pl.BlockSpec((None, kx, n4), lambda j, k: (j, 0, 0))] * len(ys),
        ("parallel", "arbitrary"))


def _tn_full(x, y, name, tn_cols=None):
    lp, kx = x.shape
    n = y.shape[1]
    tk = _tn_tiles(lp)
    tn = n if tn_cols is None else tn_cols

    def epi(accs, in_refs, out_refs):
        out_refs[0][...] = accs[0]

    (out,) = _matmul(
        name, (n // tn, lp // tk), 1, [x, y],
        [pl.BlockSpec((tk, kx), lambda j, k: (k, 0)), pl.BlockSpec((tk, tn), lambda j, k: (k, j))],
        [(0, 1, "tn", 0)], [(kx, tn)], epi,
        [jax.ShapeDtypeStruct((kx, n), F32)], [pl.BlockSpec((kx, tn), lambda j, k: (0, j))],
        ("parallel", "arbitrary"))
    return out


def _in_proj_bwd(dz, w_in, dh, h_in, gain):
    lp, d = h_in.shape
    inw = w_in.shape[1]
    tm = _row_tile(lp)

    def epi(accs, in_refs, out_refs):
        i = pl.program_id(0)
        dx, dgrow = _rms_bwd_math(accs[0], in_refs[3][...], in_refs[4][...])
        out_refs[0][...] = in_refs[2][...] + dx

        @pl.when(i == 0)
        def _():
            out_refs[1][...] = jnp.zeros_like(out_refs[1])

        out_refs[1][...] += jnp.sum(dgrow, axis=0, keepdims=True)

    row = pl.BlockSpec((tm, d), lambda i: (i, 0))
    return _matmul(
        "mix_in_proj_bwd", (lp // tm,), None, [dz, w_in, dh, h_in, gain.reshape(1, d)],
        [pl.BlockSpec((tm, inw), lambda i: (i, 0)), pl.BlockSpec((d, inw), lambda i: (0, 0)), row, row,
         pl.BlockSpec((1, d), lambda i: (0, 0))],
        [(0, 1, "nt", 0)], [(tm, d)], epi,
        [jax.ShapeDtypeStruct((lp, d), F32), jax.ShapeDtypeStruct((1, d), F32)],
        [row, pl.BlockSpec((1, d), lambda i: (0, 0))], ("arbitrary",))


def _loss_head(h, gain, target):
    lp, d = h.shape
    nb = lp // BLOCK

    def body(h_ref, g_ref, t_ref, dh_ref, dg_ref, loss_ref):
        i = pl.program_id(0)

        @pl.when(i == 0)
        def _():
            dg_ref[...] = jnp.zeros_like(dg_ref)
            loss_ref[...] = jnp.zeros_like(loss_ref)
            dh_ref[...] = jnp.zeros_like(dh_ref)

        @pl.when(i > 0)
        def _():
            x = h_ref[...]
            g = g_ref[...]
            r = lax.rsqrt(jnp.mean(x * x, axis=-1, keepdims=True) + EPS)
            err = x * r * g - t_ref[...]
            loss_ref[...] += jnp.zeros_like(loss_ref) + 0.5 * jnp.sum(jnp.sum(err * err, axis=-1, keepdims=True)) / d
            dx, dgrow = _rms_bwd_math(err * (1.0 / d), x, g)
            dh_ref[...] = dx
            dg_ref[...] += jnp.sum(dgrow, axis=0, keepdims=True)

    row = pl.BlockSpec((BLOCK, d), lambda i: (i, 0))
    one = pl.BlockSpec((1, d), lambda i: (0, 0))
    return pl.pallas_call(
        body,
        out_shape=[jax.ShapeDtypeStruct((lp, d), F32), jax.ShapeDtypeStruct((1, d), F32),
                   jax.ShapeDtypeStruct((SUBLANES, LANES), F32)],
        grid=(nb,),
        in_specs=[row, one, pl.BlockSpec((BLOCK, d), lambda i: (jnp.maximum(i - 1, 0), 0))],
        out_specs=[row, one, pl.BlockSpec((SUBLANES, LANES), lambda i: (0, 0))],
        compiler_params=_cparams(("arbitrary",)), name="loss_head")(h, gain.reshape(1, d), target)


def _adam_math(w, g, m, v):
    m = ADAM_B1 * m + (1.0 - ADAM_B1) * g
    v = ADAM_B2 * v + (1.0 - ADAM_B2) * (g * g)
    m_hat = m / (1.0 - ADAM_B1 ** ADAM_STEP)
    v_hat = v / (1.0 - ADAM_B2 ** ADAM_STEP)
    delta = -ADAM_LR * (m_hat / (jnp.sqrt(v_hat) + ADAM_EPS) + ADAM_WD * w)
    return delta, m, v


def _adamw_layers(w, m, v, mine, other, pos, name):
    depth, r, c = w.shape
    half = r // 2
    tr = _div_tile(half, c * 4)
    nh = half // tr

    def body(*refs):
        pos_ref, w_ref, m_ref, v_ref = refs[:4]
        mine_refs = refs[4:4 + depth]
        other_refs = refs[4 + depth:4 + 2 * depth]
        g_out, d_out, m_out, v_out = refs[4 + 2 * depth:]
        layer, i = pl.program_id(0), pl.program_id(1)
        is_mine = (i // nh) == pos_ref[0]

        def update(g):
            delta, nm, nv = _adam_math(w_ref[...], g, m_ref[...], v_ref[...])
            g_out[...] = g
            d_out[...] = delta
            m_out[...] = nm
            v_out[...] = nv

        for l in range(depth):
            @pl.when((layer == l) & is_mine)
            def _(l=l):
                update(mine_refs[l][...])

            @pl.when((layer == l) & jnp.logical_not(is_mine))
            def _(l=l):
                update(other_refs[l][...])

    stacked = pl.BlockSpec((None, tr, c), lambda l, i, p: (l, i, 0))

    def gspec(layer, is_other):
        def imap(l, i, p):
            first = jnp.where(is_other, 1 - p[0], p[0]) * nh
            here = jnp.clip(i - first, 0, nh - 1)
            return (jnp.where(l == layer, here, jnp.where(l < layer, 0, nh - 1)), 0)
        return pl.BlockSpec((tr, c), imap)

    shp = jax.ShapeDtypeStruct((depth, r, c), F32)
    grid_spec = pltpu.PrefetchScalarGridSpec(
        num_scalar_prefetch=1, grid=(depth, 2 * nh),
        in_specs=[stacked] * 3 + [gspec(l, 0) for l in range(depth)] + [gspec(l, 1) for l in range(depth)],
        out_specs=[stacked] * 4)
    return pl.pallas_call(
        body, out_shape=[shp] * 4, grid_spec=grid_spec,
        compiler_params=_cparams(("arbitrary", "arbitrary")), name=name)(pos, w, m, v, *mine, *other)


def _adamw_flat(w, g, m, v, name):
    r, c = w.shape
    tr = _div_tile(r, c * 4)

    def body(w_ref, g_ref, m_ref, v_ref, d_out, m_out, v_out):
        delta, nm, nv = _adam_math(w_ref[...], g_ref[...], m_ref[...], v_ref[...])
        d_out[...] = delta
        m_out[...] = nm
        v_out[...] = nv

    spec = pl.BlockSpec((tr, c), lambda i: (i, 0))
    shp = jax.ShapeDtypeStruct((r, c), F32)
    return pl.pallas_call(
        body, out_shape=[shp] * 3, grid=(r // tr,), in_specs=[spec] * 4, out_specs=[spec] * 3,
        compiler_params=_cparams(("parallel",)), name=name)(w, g, m, v)


def _mesh_pos():
    return lax.axis_index("x"), lax.axis_index("y"), lax.axis_index("c")


def _row_half(ref, which, lead):
    half = ref.shape[lead] // 2
    idx = (slice(None),) * lead + (pl.ds(which * half, half), slice(None))
    return ref.at[idx]


def _gather_comm(arrs, tag):
    n = len(arrs)

    def ctx(ins, outs, sems):
        send_sems, recv_sems, local_sems = sems
        x, y, c = _mesh_pos()
        chips = [(1 - x, y), (x, 1 - y), (1 - x, 1 - y)]

        def slot(k, chip, which):
            lead = len(ins[k].shape) - 2
            return _row_half(outs[k].at[2 * chip[0] + chip[1]], which, lead)

        def copy(k, j, src, dst, to):
            return pltpu.make_async_remote_copy(
                src_ref=src, dst_ref=dst, send_sem=send_sems.at[6 * k + j], recv_sem=recv_sems.at[6 * k + j],
                device_id=to, device_id_type=MESH)

        def local(k):
            return pltpu.make_async_copy(ins[k], outs[k].at[2 * x + y], local_sems.at[k])

        def first(k, j):
            lead = len(ins[k].shape) - 2
            return copy(k, j, _row_half(ins[k], c, lead), slot(k, (x, y), c), (*chips[j], c))

        def passed(k, j, which):
            return copy(k, 3 + j, slot(k, chips[j], which), slot(k, chips[j], which), (x, y, 1 - c))

        def landed(k, j):
            return copy(k, j, slot(k, chips[j], c), slot(k, chips[j], c), (x, y, 1 - c))

        return c, local, first, passed, landed

    def start(ins, outs, sems):
        c, local, first, passed, landed = ctx(ins, outs, sems)
        for k in range(n):
            local(k).start()
            for j in range(3):
                first(k, j).start()

    def mid(ins, outs, sems):
        c, local, first, passed, landed = ctx(ins, outs, sems)
        for j in range(3):
            for k in range(n):
                landed(k, j).wait_recv()
                passed(k, j, c).start()

    def finish(ins, outs, sems):
        c, local, first, passed, landed = ctx(ins, outs, sems)
        for j in range(3):
            for k in range(n):
                passed(k, j, 1 - c).wait_recv()
        for k in range(n):
            for j in range(3):
                first(k, j).wait_send()
                passed(k, j, c).wait_send()
            local(k).wait()

    return _Comm(
        tag, arrs, [jax.ShapeDtypeStruct((N_CHIPS,) + a.shape, a.dtype) for a in arrs],
        [pltpu.SemaphoreType.DMA((6 * n,)), pltpu.SemaphoreType.DMA((6 * n,)), pltpu.SemaphoreType.DMA((n,))],
        start, mid, finish)


def _run_comm(comm, name):
    n_in, n_out = len(comm.ins), len(comm.out_shapes)

    def body(*refs):
        ins, outs, sems = refs[:n_in], refs[n_in:n_in + n_out], refs[n_in + n_out:]
        comm.start(ins, outs, sems)
        if comm.mid is not None:
            comm.mid(ins, outs, sems)
        comm.finish(ins, outs, sems)

    return pl.pallas_call(
        body, out_shape=comm.out_shapes, in_specs=[HBM_SPEC] * n_in, out_specs=[HBM_SPEC] * n_out,
        scratch_shapes=comm.sems, name=name)(*comm.ins)


def _all_gather_chips(arrs, name):
    return _run_comm(_gather_comm(arrs, "gather"), name)


GATHER_US_PER_BYTE = 380.0 / 11.65e6
HOST_US = dict(ffn_up=78.0, ffn_down=65.0, in_proj=38.0, attn_fwd=103.0, ssm_fwd=67.0, merge_fwd=50.0,
               out_proj=45.0)
HOST_SLACK_US = 10.0


class _WeightStream:
    def __init__(self, pieces):
        self.keys = [k for k, _ in pieces]
        self.shards = dict(pieces)
        self.next = 0
        self.full = {}
        self.pending = []

    def comm_for(self, host):
        budget = HOST_US[host] + HOST_SLACK_US
        taken, cost = [], 0.0
        while self.next < len(self.keys):
            key = self.keys[self.next]
            c = self.shards[key].size * self.shards[key].dtype.itemsize * GATHER_US_PER_BYTE
            if cost + c > budget:
                break
            taken.append(key)
            cost += c
            self.next += 1
        self.pending = taken
        if not taken:
            return None
        return _gather_comm([self.shards[k] for k in taken], "g_" + "_".join(k[1] for k in taken))

    def deposit(self, gathered):
        for key, arr in zip(self.pending, gathered):
            self.full[key] = arr
        self.pending = []

    def get(self, key):
        if key not in self.full:
            upto = self.keys.index(key) + 1
            keys = self.keys[self.next:upto]
            self.next = upto
            for k, arr in zip(keys, _all_gather_chips([self.shards[k] for k in keys], "gather_now")):
                self.full[k] = arr
        return self.full[key]


def _all_gather_devices(x_shard, name):
    m_per, ncol = x_shard.shape

    def body(x_ref, out_ref, send_sems, recv_sems, local_sem):
        x, y, c = _mesh_pos()
        me, sibling = (x, y, c), (x, y, 1 - c)
        chips = [(1 - x, y), (x, 1 - y), (1 - x, 1 - y)]

        def rows(px, py, pc):
            return out_ref.at[4 * px + 2 * py + pc]

        def copy(k, block, to, src=None):
            return pltpu.make_async_remote_copy(
                src_ref=rows(*block) if src is None else src, dst_ref=rows(*block),
                send_sem=send_sems.at[k], recv_sem=recv_sems.at[k], device_id=to, device_id_type=MESH)

        mine = pltpu.make_async_copy(x_ref, rows(*me), local_sem)
        mine.start()
        first = [copy(0, me, sibling, src=x_ref)]
        first += [copy(1 + j, me, (*chip, c), src=x_ref) for j, chip in enumerate(chips)]
        for cp in first:
            cp.start()
        passed = [copy(4 + j, (*chip, c), sibling) for j, chip in enumerate(chips)]
        for j, chip in enumerate(chips):
            copy(1 + j, (*chip, c), me).wait_recv()
            passed[j].start()
        copy(0, sibling, me).wait_recv()
        for j, chip in enumerate(chips):
            copy(4 + j, (*chip, 1 - c), me).wait_recv()
        for cp in first + passed:
            cp.wait_send()
        mine.wait()

    return pl.pallas_call(
        body, out_shape=jax.ShapeDtypeStruct((8, m_per, ncol), x_shard.dtype),
        in_specs=[pl.BlockSpec(memory_space=pltpu.VMEM)], out_specs=pl.BlockSpec(memory_space=pltpu.VMEM),
        scratch_shapes=[pltpu.SemaphoreType.DMA((7,)), pltpu.SemaphoreType.DMA((7,)), pltpu.SemaphoreType.DMA],
        compiler_params=pltpu.CompilerParams(vmem_limit_bytes=VMEM_LIMIT), name=name)(x_shard)


def _sum_devices(g8, name):
    _, r, c = g8.shape
    tr = _div_tile(r, c * 4 * 8)

    def body(g_ref, o_ref):
        acc = g_ref[0]
        for dev in range(1, 8):
            acc = acc + g_ref[dev]
        o_ref[...] = acc

    return pl.pallas_call(
        body, out_shape=jax.ShapeDtypeStruct((r, c), F32), grid=(r // tr,),
        in_specs=[pl.BlockSpec((8, tr, c), lambda i: (0, i, 0))], out_specs=pl.BlockSpec((tr, c), lambda i: (i, 0)),
        compiler_params=_cparams(("parallel",)), name=name)(g8)


def _exchange_sibling_halves(arrs, name):
    n = len(arrs)

    def body(*refs):
        ins, outs = refs[:n], refs[n:2 * n]
        send_sems, recv_sems = refs[2 * n:]
        x, y, c = _mesh_pos()
        cps = []
        for k in range(n):
            cp = pltpu.make_async_remote_copy(
                src_ref=_row_half(ins[k], 1 - c, 1), dst_ref=outs[k], send_sem=send_sems.at[k],
                recv_sem=recv_sems.at[k], device_id=(x, y, 1 - c), device_id_type=MESH)
            cp.start()
            cps.append(cp)
        for cp in cps:
            cp.wait()

    return pl.pallas_call(
        body,
        out_shape=[jax.ShapeDtypeStruct((a.shape[0], a.shape[1] // 2, a.shape[2]), a.dtype) for a in arrs],
        in_specs=[HBM_SPEC] * n, out_specs=[HBM_SPEC] * n,
        scratch_shapes=[pltpu.SemaphoreType.DMA((n,)), pltpu.SemaphoreType.DMA((n,))], name=name)(*arrs)


def _chip_partials(arrs, recvs, pos, name):
    n = len(arrs)

    def body(pos_ref, *refs):
        for a_ref, b_ref, o_ref in zip(refs[:n], refs[n:2 * n], refs[2 * n:]):
            o_ref[...] = (a_ref[...] + b_ref[...]).astype(BF16)

    own_specs, recv_specs, shapes = [], [], []
    for arr in arrs:
        nslab, r, c = arr.shape
        own_specs.append(pl.BlockSpec((None, r // 2, c), lambda j, p: (j, p[0], 0)))
        recv_specs.append(pl.BlockSpec((None, r // 2, c), lambda j, p: (j, 0, 0)))
        shapes.append(jax.ShapeDtypeStruct((nslab, r // 2, c), BF16))
    grid_spec = pltpu.PrefetchScalarGridSpec(
        num_scalar_prefetch=1, grid=(N_CHIPS,), in_specs=own_specs + recv_specs, out_specs=recv_specs)
    return pl.pallas_call(
        body, out_shape=shapes, grid_spec=grid_spec,
        compiler_params=_cparams(("parallel",)), name=name)(pos, *arrs, *recvs)


def _chip_exchange_comm(parts, tag):
    n = len(parts)

    def copies(ins, outs, sems):
        send_sems, recv_sems = sems
        x, y, c = _mesh_pos()
        chips = [(1 - x, y), (x, 1 - y), (1 - x, 1 - y)]
        return [pltpu.make_async_remote_copy(
            src_ref=ins[k].at[2 * chip[0] + chip[1]], dst_ref=outs[k].at[j],
            send_sem=send_sems.at[3 * k + j], recv_sem=recv_sems.at[3 * k + j],
            device_id=(*chip, c), device_id_type=MESH) for k in range(n) for j, chip in enumerate(chips)]

    def start(ins, outs, sems):
        for cp in copies(ins, outs, sems):
            cp.start()

    def finish(ins, outs, sems):
        for cp in copies(ins, outs, sems):
            cp.wait()

    return _Comm(
        tag, parts, [jax.ShapeDtypeStruct((3,) + p.shape[1:], p.dtype) for p in parts],
        [pltpu.SemaphoreType.DMA((3 * n,)), pltpu.SemaphoreType.DMA((3 * n,))], start, None, finish)


def _reduce_halves(arrs, recvs, gots, pos, name):
    n = len(arrs)

    def body(pos_ref, *refs):
        for a_ref, b_ref, g_ref, o_ref in zip(refs[:n], refs[n:2 * n], refs[2 * n:3 * n], refs[3 * n:]):
            acc = a_ref[...] + b_ref[...]
            for j in range(3):
                acc = acc + g_ref[j].astype(F32)
            o_ref[...] = acc

    own_specs, recv_specs, got_specs, out_specs, shapes = [], [], [], [], []
    for arr in arrs:
        _, r, c = arr.shape
        own_specs.append(pl.BlockSpec((None, r // 2, c), lambda i, p: (p[1], p[0], 0)))
        recv_specs.append(pl.BlockSpec((None, r // 2, c), lambda i, p: (p[1], 0, 0)))
        got_specs.append(pl.BlockSpec((3, r // 2, c), lambda i, p: (0, 0, 0)))
        out_specs.append(pl.BlockSpec((r // 2, c), lambda i, p: (0, 0)))
        shapes.append(jax.ShapeDtypeStruct((r // 2, c), F32))
    grid_spec = pltpu.PrefetchScalarGridSpec(
        num_scalar_prefetch=1, grid=(1,), in_specs=own_specs + recv_specs + got_specs, out_specs=out_specs)
    return pl.pallas_call(
        body, out_shape=shapes, grid_spec=grid_spec,
        compiler_params=_cparams(("arbitrary",)), name=name)(pos, *arrs, *recvs, *gots)


def _share_halves(halves, name):
    n = len(halves)

    def body(*refs):
        ins, outs = refs[:n], refs[n:2 * n]
        send_sems, recv_sems = refs[2 * n:]
        x, y, c = _mesh_pos()
        cps = []
        for k in range(n):
            cp = pltpu.make_async_remote_copy(
                src_ref=ins[k], dst_ref=outs[k], send_sem=send_sems.at[k], recv_sem=recv_sems.at[k],
                device_id=(x, y, 1 - c), device_id_type=MESH)
            cp.start()
            cps.append(cp)
        for cp in cps:
            cp.wait()

    return pl.pallas_call(
        body, out_shape=[jax.ShapeDtypeStruct(h.shape, h.dtype) for h in halves],
        in_specs=[HBM_SPEC] * n, out_specs=[HBM_SPEC] * n,
        scratch_shapes=[pltpu.SemaphoreType.DMA((n,)), pltpu.SemaphoreType.DMA((n,))], name=name)(*halves)


class _Reduction:
    def __init__(self, arrs, pos, tag):
        self.arrs, self.pos, self.tag = arrs, pos, tag
        self.recv = _exchange_sibling_halves(arrs, "rs_sibling_" + tag)
        self.parts = _chip_partials(arrs, self.recv, pos, "rs_partial_" + tag)
        self.got = None

    def comm(self):
        return _chip_exchange_comm(self.parts, "rs_" + self.tag)

    def end(self):
        if self.got is None:
            self.got = _run_comm(self.comm(), "rs_chips_" + self.tag)
        halves = _reduce_halves(self.arrs, self.recv, self.got, self.pos, "rs_reduce_" + self.tag)
        return halves, _share_halves(halves, "rs_share_" + self.tag)


def _w_in_full(p, l, ws):
    if "w_in" not in p:
        slabs = ws.get((l, "w_in"))
        p["w_in"] = jnp.concatenate([slabs[j] for j in range(N_CHIPS)], axis=1)
    return p["w_in"]


def _layer_fwd(h, l, p, ws, tabs):
    def hosted(host, fn, *args):
        out, got = fn(*args, ws.comm_for(host))
        ws.deposit(got)
        return out

    a, b, sact = hosted("ffn_up", _ffn_up, h, p["ffn1_norm"], ws.get((l, "wg1")), ws.get((l, "wu1")))
    h1 = hosted("ffn_down", _ffn_down, sact, ws.get((l, "wd1")), h)
    ffn1_saved = (a, b, sact)
    n = _rms_fwd(h1, p["mix_norm"], "rms_fwd_mix")
    ssm_w = p["ssm_d"].shape[0]
    q, k, v, u, ga, gs = hosted("in_proj", _in_proj, n, _w_in_full(p, l, ws), tabs, ssm_w)
    o = hosted("attn_fwd", _attn_fwd, q, k, v, p["attn_sinks"])
    y = hosted("ssm_fwd", _ssm_fwd, u, *p["ssm_tabs"], p["ssm_d"])
    yg = _gelu_fwd(y)
    merged = hosted("merge_fwd", _merge_fwd, o, yg, ga, gs, ws.get((l, "w3")))
    h2 = hosted("out_proj", _out_proj, merged, ws.get((l, "w_out")), h1)
    a, b, sact = hosted("ffn_up", _ffn_up, h2, p["ffn2_norm"], ws.get((l, "wg2")), ws.get((l, "wu2")))
    h3 = hosted("ffn_down", _ffn_down, sact, ws.get((l, "wd2")), h2)
    saved = dict(h0=h, h1=h1, h2=h2, ffn1=ffn1_saved, ffn2=(a, b, sact), q=q, k=k, v=v, u=u, ga=ga, gs=gs, o=o, y=y,
                 yg=yg, merged=merged)
    return h3, saved


def _layer_bwd(dh, l, p, ws, s, tabs, pos):
    g = {}
    dh2, g["ffn2_norm"], red_ffn2, _ = _ffn_bwd(
        dh, s["h2"], p["ffn2_norm"], ws.get((l, "wg2")), ws.get((l, "wu2")), ws.get((l, "wd2")), p["f4"],
        s["ffn2"], pos)
    w3, w_out_w = ws.get((l, "w3")), ws.get((l, "w_out"))
    lp, d = dh2.shape
    d4 = d // N_CHIPS
    dhb = _scale_cast(dh2, 1.0, "mix_dh_cast")
    dw_out = _tn_full(s["merged"], dhb, "mix_dw_out").reshape(N_CHIPS, d4, d)
    dattn, dv, dg, dga, dgs = _merge_bwd(dhb, w_out_w, s["o"], s["yg"], s["ga"], s["gs"], w3)
    (dw_ap,) = _tn_cols(s["o"], [dattn], "mix_dw_ap")
    dw_gv, dw_gg = _tn_cols(s["yg"], [dv, dg], "mix_dw_glu")
    do, dy = _branch_bwd(dattn, dv, dg, w3, s["y"])
    (dq, dk, dvv, dkm, dvm, dsink), _ = _attn_bwd(s["q"], s["k"], s["v"], do, p["attn_sinks"], tabs)
    g["attn_sinks"] = dsink[:, 0]
    (du, dlr, dli, dbr, dbi, dcr, dci, dd), _ = _ssm_bwd(s["u"], dy, *p["ssm_tabs"], p["ssm_d"])
    ngrp = p["ssm_d"].shape[0] // SSM_GROUP
    g["ssm_lam"] = (dlr.reshape(ngrp, SSM_STATE), dli.reshape(ngrp, SSM_STATE),
                    _ssm_untable_b(dbr, ngrp), _ssm_untable_b(dbi, ngrp))
    g["ssm_c_re"] = _ssm_untable_c(dcr, ngrp)
    g["ssm_c_im"] = _ssm_untable_c(dci, ngrp)
    g["ssm_d"] = dd[0]
    dk = dk.at[:BLOCK].add(dkm)
    dvv = dvv.at[:BLOCK].add(dvm)
    dz = jnp.concatenate([dq.astype(BF16), dk.astype(BF16), dvv.astype(BF16), du.astype(BF16), dga, dgs], axis=1)
    n = _rms_fwd(s["h1"], p["mix_norm"], "rms_fwd_mix")
    w_in = _w_in_full(p, l, ws)
    inw = w_in.shape[1]
    dw_in = _tn_full(dz, n, "mix_dw_in", d // 2).reshape(N_CHIPS, inw // N_CHIPS, d)
    red_mix = _Reduction([dw_in, dw_ap, dw_gv, dw_gg, dw_out], pos, "mix")
    dh1, g["mix_norm"] = _in_proj_bwd(dz, w_in, dh2, s["h1"], p["mix_norm"])
    dh0, g["ffn1_norm"], red_ffn1, red_mix.got = _ffn_bwd(
        dh1, s["h0"], p["ffn1_norm"], ws.get((l, "wg1")), ws.get((l, "wu1")), ws.get((l, "wd1")), p["f4"],
        s["ffn1"], pos, red_mix.comm())
    return dh0, g, [*red_ffn1, red_mix, *red_ffn2]


BIG = ["ffn1_w_gate", "ffn1_w_up", "ffn1_w_down", "w_in", "w_attn_proj", "w_glu_v", "w_glu_g", "w_out",
       "ffn2_w_gate", "ffn2_w_up", "ffn2_w_down"]
TRANSPOSED = ["ffn1_w_gate", "ffn1_w_up", "w_in", "ffn2_w_gate", "ffn2_w_up"]
SMALL = ["ffn1_norm", "mix_norm", "attn_sinks", "ssm_a_re", "ssm_a_im", "ssm_log_dt", "ssm_b_re", "ssm_b_im",
         "ssm_c_re", "ssm_c_im", "ssm_d", "ffn2_norm", "final_norm"]
WEIGHTS = ["meta_tokens", "ffn1_norm", "ffn1_w_gate", "ffn1_w_up", "ffn1_w_down", "mix_norm", "w_in", "attn_sinks",
           "ssm_a_re", "ssm_a_im", "ssm_log_dt", "ssm_b_re", "ssm_b_im", "ssm_c_re", "ssm_c_im", "ssm_d",
           "w_attn_proj", "w_glu_v", "w_glu_g", "w_out", "ffn2_norm", "ffn2_w_gate", "ffn2_w_up", "ffn2_w_down",
           "final_norm"]


def _small_rows(shape):
    rows = -(-math.prod(shape) // LANES)
    return -(-rows // SUBLANES) * SUBLANES


def _pack_small(tree):
    parts = []
    for k in SMALL + ["meta_tokens"]:
        size, rows = math.prod(tree[k].shape), _small_rows(tree[k].shape)
        if size % LANES == 0:
            part = tree[k].reshape(size // LANES, LANES)
        else:
            part = jnp.pad(tree[k].reshape(1, size), ((0, 0), (0, LANES - size)))
        parts.append(jnp.pad(part, ((0, rows - part.shape[0]), (0, 0))))
    return jnp.concatenate(parts, axis=0)


def _unpack_small(packed, like):
    out, off = {}, 0
    for k in SMALL + ["meta_tokens"]:
        size, rows = math.prod(like[k].shape), _small_rows(like[k].shape)
        if size % LANES == 0:
            out[k] = packed[off:off + size // LANES].reshape(like[k].shape)
        else:
            out[k] = packed[off, :size].reshape(like[k].shape)
        off += rows
    return out


def kernel(x, meta_tokens, ffn1_norm, ffn1_w_gate, ffn1_w_up, ffn1_w_down, mix_norm, w_in, attn_sinks, ssm_a_re, ssm_a_im, ssm_log_dt, ssm_b_re, ssm_b_im, ssm_c_re, ssm_c_im, ssm_d, w_attn_proj, w_glu_v, w_glu_g, w_out, ffn2_norm, ffn2_w_gate, ffn2_w_up, ffn2_w_down, final_norm, loss_target, m_meta_tokens, m_ffn1_norm, m_ffn1_w_gate, m_ffn1_w_up, m_ffn1_w_down, m_mix_norm, m_w_in, m_attn_sinks, m_ssm_a_re, m_ssm_a_im, m_ssm_log_dt, m_ssm_b_re, m_ssm_b_im, m_ssm_c_re, m_ssm_c_im, m_ssm_d, m_w_attn_proj, m_w_glu_v, m_w_glu_g, m_w_out, m_ffn2_norm, m_ffn2_w_gate, m_ffn2_w_up, m_ffn2_w_down, m_final_norm, v_meta_tokens, v_ffn1_norm, v_ffn1_w_gate, v_ffn1_w_up, v_ffn1_w_down, v_mix_norm, v_w_in, v_attn_sinks, v_ssm_a_re, v_ssm_a_im, v_ssm_log_dt, v_ssm_b_re, v_ssm_b_im, v_ssm_c_re, v_ssm_c_im, v_ssm_d, v_w_attn_proj, v_w_glu_v, v_w_glu_g, v_w_out, v_ffn2_norm, v_ffn2_w_gate, v_ffn2_w_up, v_ffn2_w_down, v_final_norm):
    args = dict(locals())
    w = {k: args[k] for k in WEIGHTS}
    m = {k: args["m_" + k] for k in WEIGHTS}
    v = {k: args["v_" + k] for k in WEIGHTS}
    depth = ffn1_norm.shape[0]
    seq, d = x.shape[1], x.shape[2]
    lp = seq + BLOCK
    xi, yi, ci = _mesh_pos()
    pos = jnp.stack([ci, 2 * xi + yi]).astype(jnp.int32)

    tabs = _rope_tables(lp)
    (meta_all,) = _all_gather_chips([meta_tokens], "gather_meta")
    meta_full = jnp.concatenate([meta_all[j] for j in range(N_CHIPS)], axis=1)
    layers, pieces = [], []
    f4 = ffn1_w_gate.shape[2]
    fp = -(-f4 // MXU_DIM) * MXU_DIM

    def ffn_rows(wt):
        return jnp.pad(wt, ((0, fp - f4), (0, 0))).astype(BF16)

    for l in range(depth):
        pieces += [
            ((l, "wg1"), ffn_rows(ffn1_w_gate[l].T)), ((l, "wu1"), ffn_rows(ffn1_w_up[l].T)),
            ((l, "wd1"), ffn_rows(ffn1_w_down[l])), ((l, "w_in"), w_in[l].astype(BF16)),
            ((l, "w3"), jnp.stack([w_attn_proj[l], w_glu_v[l], w_glu_g[l]]).astype(BF16)),
            ((l, "w_out"), w_out[l].astype(BF16)),
            ((l, "wg2"), ffn_rows(ffn2_w_gate[l].T)), ((l, "wu2"), ffn_rows(ffn2_w_up[l].T)),
            ((l, "wd2"), ffn_rows(ffn2_w_down[l]))]
        lb_re, lb_im, bb_re, bb_im = _ssm_params(ssm_a_re[l], ssm_a_im[l], ssm_log_dt[l], ssm_b_re[l], ssm_b_im[l])
        ngrp = lb_re.shape[0]
        nt = ngrp // GROUPS_PER_TILE
        ssm_tabs = (lb_re.reshape(nt, 1, TILE_STATES), lb_im.reshape(nt, 1, TILE_STATES),
                    *_ssm_tables(bb_re, bb_im, ssm_c_re[l], ssm_c_im[l]))
        layers.append(dict(
            ffn1_norm=ffn1_norm[l], mix_norm=mix_norm[l], ffn2_norm=ffn2_norm[l], attn_sinks=attn_sinks[l],
            ssm_d=ssm_d[l], ssm_tabs=ssm_tabs, f4=f4))
    ws = _WeightStream(pieces)
    ws.get((0, "wu1"))

    h = jnp.concatenate([jnp.zeros((PAD_FRONT, d), F32), meta_full, x[0]], axis=0)
    saved = []
    for l in range(depth):
        h, s = _layer_fwd(h, l, layers[l], ws, tabs)
        saved.append(s)
    dh, g_final, loss_acc = _loss_head(h, final_norm, loss_target[0])
    loss = lax.psum(loss_acc[0, 0], ("x", "y", "c"))

    grads, reds = [None] * depth, [None] * depth
    for l in reversed(range(depth)):
        dh, grads[l], reds[l] = _layer_bwd(dh, l, layers[l], ws, saved[l], tabs, pos)
    grad_x = dh[BLOCK:][None]
    dmeta_local = dh[PAD_FRONT:BLOCK]

    small = {k: [] for k in SMALL}
    for l in range(depth):
        gl = grads[l]
        _, vjp = jax.vjp(_ssm_params, ssm_a_re[l], ssm_a_im[l], ssm_log_dt[l], ssm_b_re[l], ssm_b_im[l])
        da_re, da_im, dlog_dt, db_re, db_im = vjp(gl["ssm_lam"])
        for k, val in (("ffn1_norm", gl["ffn1_norm"][0]), ("mix_norm", gl["mix_norm"][0]),
                       ("attn_sinks", gl["attn_sinks"]), ("ssm_a_re", da_re), ("ssm_a_im", da_im),
                       ("ssm_log_dt", dlog_dt), ("ssm_b_re", db_re), ("ssm_b_im", db_im),
                       ("ssm_c_re", gl["ssm_c_re"]), ("ssm_c_im", gl["ssm_c_im"]), ("ssm_d", gl["ssm_d"]),
                       ("ffn2_norm", gl["ffn2_norm"][0])):
            small[k].append(val)
    small_local = {k: jnp.stack(vals) for k, vals in small.items() if k != "final_norm"}
    small_local["final_norm"] = g_final[0]
    small_local["meta_tokens"] = dmeta_local
    like = dict(small_local)
    g_small = _sum_devices(_all_gather_devices(_pack_small(small_local), "gather_small_grads"), "sum_small_grads")
    g_small_tree = _unpack_small(g_small, like)
    d4 = d // N_CHIPS
    chip = 2 * xi + yi
    g_meta = lax.dynamic_slice_in_dim(g_small_tree["meta_tokens"], chip * d4, d4, axis=1)

    reduced = []
    for l in range(depth):
        mine, other = [], []
        for red in reds[l]:
            halves, sibling_halves = red.end()
            mine += halves
            other += sibling_halves
        reduced.append((mine, other))

    g_out, delta, new_m, new_v = {}, {}, {}, {}
    for i, k in enumerate(BIG):
        flip = (lambda t: jnp.swapaxes(t, 1, 2)) if k in TRANSPOSED else (lambda t: t)
        outs = _adamw_layers(
            flip(w[k]), flip(m[k]), flip(v[k]), [reduced[l][0][i] for l in range(depth)],
            [reduced[l][1][i] for l in range(depth)], pos, "adamw_" + k)
        g_out[k], delta[k], new_m[k], new_v[k] = [flip(t) for t in outs]
    small_names = SMALL + ["meta_tokens"]
    w_small = {k: w[k] for k in small_names}
    m_small = {k: m[k] for k in small_names}
    v_small = {k: v[k] for k in small_names}
    g_small_local = dict(g_small_tree)
    g_small_local["meta_tokens"] = g_meta
    d_s, m_s, v_s = _adamw_flat(_pack_small(w_small), _pack_small(g_small_local), _pack_small(m_small),
                                _pack_small(v_small), "adamw_small")
    for tree, packed in ((delta, d_s), (new_m, m_s), (new_v, v_s)):
        tree.update(_unpack_small(packed, w_small))
    for k in small_names:
        g_out[k] = g_small_local[k]

    return (loss, grad_x, *[g_out[k] for k in WEIGHTS], *[delta[k] for k in WEIGHTS],
            *[new_m[k] for k in WEIGHTS], *[new_v[k] for k in WEIGHTS])
```

```python
import functools
import math

import jax
import jax.numpy as jnp
from jax import lax
from jax.experimental import pallas as pl
from jax.experimental.pallas import tpu as pltpu

F32 = jnp.float32
BF16 = jnp.bfloat16

N_META = 16
HEAD_DIM = 64
N_Q_HEADS = 8
N_KV_HEADS = 2
Q_PER_KV = N_Q_HEADS // N_KV_HEADS
ATTN_WIDTH = N_Q_HEADS * HEAD_DIM
KV_WIDTH = N_KV_HEADS * HEAD_DIM
BLOCK = 128
PAD_FRONT = BLOCK - N_META
ROPE_THETA = 500000.0
ROT_DIM = HEAD_DIM // 4
SSM_GROUP = 16
SSM_STATE = 64
GROUPS_PER_TILE = 4
TILE_STATES = GROUPS_PER_TILE * SSM_STATE
LANES = 128
SUBLANES = 8
MXU_DIM = 256
EPS = 1e-6
NEG_INF = -1e30
N_CHIPS = 4

ADAM_LR = 0.001
ADAM_B1 = 0.9
ADAM_B2 = 0.999
ADAM_EPS = 1e-08
ADAM_WD = 0.01
ADAM_STEP = 10

VMEM_LIMIT = 56 * 1024 * 1024
MESH = pl.DeviceIdType.MESH


def _cparams(sem=None):
    return pltpu.CompilerParams(dimension_semantics=sem, vmem_limit_bytes=VMEM_LIMIT)


def _row_tile(rows, limit=512):
    best = None
    for t in range(128, limit + 1, 128):
        if rows % t == 0:
            best = t
    assert best is not None, rows
    return best


def _div_tile(rows, row_bytes, max_bytes=1 << 20, mult=8):
    best = None
    for t in range(mult, rows + 1, mult):
        if rows % t == 0 and t * row_bytes <= max_bytes:
            best = t
    if best is None:
        best = rows
    return best


def _dot(a, b, mode):
    if mode == "nn":
        dims = (((1,), (0,)), ((), ()))
    elif mode == "nt":
        dims = (((1,), (1,)), ((), ()))
    else:
        dims = (((0,), (0,)), ((), ()))
    return lax.dot_general(a.astype(BF16), b.astype(BF16), dims, preferred_element_type=F32)


def _sigmoid(x):
    return 1.0 / (1.0 + jnp.exp(-x))


_GELU_C = math.sqrt(2.0 / math.pi)


def _gelu(x):
    return 0.5 * x * (1.0 + jnp.tanh(_GELU_C * (x + 0.044715 * x * x * x)))


def _gelu_grad(x):
    t = jnp.tanh(_GELU_C * (x + 0.044715 * x * x * x))
    return 0.5 * (1.0 + t) + 0.5 * x * (1.0 - t * t) * _GELU_C * (1.0 + 3.0 * 0.044715 * x * x)


class _Comm:
    def __init__(self, tag, ins, out_shapes, sems, start, mid, finish):
        self.tag, self.ins, self.out_shapes, self.sems = tag, list(ins), list(out_shapes), list(sems)
        self.start, self.mid, self.finish = start, mid, finish


HBM_SPEC = pl.BlockSpec(memory_space=pltpu.HBM)


def _hosted_call(body, comm, *, out_shape, grid, in_specs, out_specs, scratch_shapes, sem, name, args):
    out_shape, in_specs, out_specs = list(out_shape), list(in_specs), list(out_specs)
    scratch_shapes = list(scratch_shapes)
    if comm is None:
        res = pl.pallas_call(
            body, out_shape=out_shape, grid=grid, in_specs=in_specs, out_specs=out_specs,
            scratch_shapes=scratch_shapes, compiler_params=_cparams(sem), name=name)(*args)
        return list(res), []
    n_in, n_out, n_sc = len(args), len(out_shape), len(scratch_shapes)
    nci, nco = len(comm.ins), len(comm.out_shapes)
    total = math.prod(grid)

    def wrapped(*refs):
        in_refs, cin = refs[:n_in], refs[n_in:n_in + nci]
        o0 = n_in + nci
        out_refs, cout = refs[o0:o0 + n_out], refs[o0 + n_out:o0 + n_out + nco]
        s0 = o0 + n_out + nco
        sc, csem = refs[s0:s0 + n_sc], refs[s0 + n_sc:]
        lin = 0
        for dim, size in enumerate(grid):
            lin = lin * size + pl.program_id(dim)

        @pl.when(lin == 0)
        def _():
            comm.start(cin, cout, csem)

        if comm.mid is not None:
            @pl.when(lin == total // 2)
            def _():
                comm.mid(cin, cout, csem)

        body(*in_refs, *out_refs, *sc)

        @pl.when(lin == total - 1)
        def _():
            comm.finish(cin, cout, csem)

    res = pl.pallas_call(
        wrapped, out_shape=out_shape + comm.out_shapes, grid=grid,
        in_specs=in_specs + [HBM_SPEC] * nci, out_specs=out_specs + [HBM_SPEC] * nco,
        scratch_shapes=scratch_shapes + comm.sems,
        compiler_params=_cparams(("arbitrary",) * len(grid)), name=name + "_" + comm.tag)(*args, *comm.ins)
    return list(res[:n_out]), list(res[n_out:])


def _matmul(name, grid, k_axis, ins, in_specs, pairs, acc_shapes, epilogue, out_shapes, out_specs, sem, comm=None):
    n_in, n_out, n_acc = len(ins), len(out_shapes), len(acc_shapes)

    def body(*refs):
        in_refs = refs[:n_in]
        out_refs = refs[n_in:n_in + n_out]
        acc_refs = refs[n_in + n_out:]
        if k_axis is None:
            accs = [None] * n_acc
            for ia, ib, mode, iacc in pairs:
                d = _dot(in_refs[ia][...], in_refs[ib][...], mode)
                accs[iacc] = d if accs[iacc] is None else accs[iacc] + d
            epilogue(accs, in_refs, out_refs)
            return
        k = pl.program_id(k_axis)

        @pl.when(k == 0)
        def _():
            for r in acc_refs:
                r[...] = jnp.zeros_like(r)

        for ia, ib, mode, iacc in pairs:
            acc_refs[iacc][...] += _dot(in_refs[ia][...], in_refs[ib][...], mode)

        @pl.when(k == pl.num_programs(k_axis) - 1)
        def _():
            epilogue([r[...] for r in acc_refs], in_refs, out_refs)

    scratch = [] if k_axis is None else [pltpu.VMEM(s, F32) for s in acc_shapes]
    outs, couts = _hosted_call(
        body, comm, out_shape=out_shapes, grid=grid, in_specs=in_specs, out_specs=out_specs,
        scratch_shapes=scratch, sem=sem, name=name, args=ins)
    return outs if comm is None else (outs, couts)


def _rms_fwd(h, g, name):
    lp, d = h.shape
    tm = _row_tile(lp)

    def body(h_ref, g_ref, n_ref):
        x = h_ref[...]
        r = lax.rsqrt(jnp.mean(x * x, axis=-1, keepdims=True) + EPS)
        n_ref[...] = (x * r * g_ref[...]).astype(BF16)

    return pl.pallas_call(
        body, out_shape=jax.ShapeDtypeStruct((lp, d), BF16), grid=(lp // tm,),
        in_specs=[pl.BlockSpec((tm, d), lambda i: (i, 0)), pl.BlockSpec((1, d), lambda i: (0, 0))],
        out_specs=pl.BlockSpec((tm, d), lambda i: (i, 0)),
        compiler_params=_cparams(("parallel",)), name=name)(h, g.reshape(1, d))


def _rms_bwd_math(dn, x, g):
    r = lax.rsqrt(jnp.mean(x * x, axis=-1, keepdims=True) + EPS)
    xh = x * r
    dxh = dn * g
    dx = r * (dxh - xh * jnp.mean(dxh * xh, axis=-1, keepdims=True))
    return dx, dn * xh


def _scale_cast(x, scale, name):
    lp, d = x.shape
    tm = _row_tile(lp)

    def body(x_ref, o_ref):
        o_ref[...] = (x_ref[...] * scale).astype(BF16)

    return pl.pallas_call(
        body, out_shape=jax.ShapeDtypeStruct((lp, d), BF16), grid=(lp // tm,),
        in_specs=[pl.BlockSpec((tm, d), lambda i: (i, 0))], out_specs=pl.BlockSpec((tm, d), lambda i: (i, 0)),
        compiler_params=_cparams(("parallel",)), name=name)(x)


def _ffn_up(h, gain, wgt, wut, comm=None):
    lp, d = h.shape
    fp = wgt.shape[1]
    tm = _row_tile(lp)
    n = _rms_fwd(h, gain, "rms_fwd_ffn")

    def up_epi(accs, in_refs, out_refs):
        a, b = accs
        out_refs[0][...] = a.astype(BF16)
        out_refs[1][...] = b.astype(BF16)
        out_refs[2][...] = (a * _sigmoid(a) * b).astype(BF16)

    act = jax.ShapeDtypeStruct((lp, N_CHIPS * fp), BF16)
    w_spec = pl.BlockSpec((None, fp, d), lambda j, i: (j, 0, 0))
    res = _matmul(
        "ffn_up", (N_CHIPS, lp // tm), None, [n, wgt, wut],
        [pl.BlockSpec((tm, d), lambda j, i: (i, 0)), w_spec, w_spec],
        [(0, 1, "nt", 0), (0, 2, "nt", 1)], [(tm, fp)] * 2, up_epi,
        [act, act, act], [pl.BlockSpec((tm, fp), lambda j, i: (i, j))] * 3,
        ("parallel", "parallel"), comm)
    return (tuple(res), []) if comm is None else (tuple(res[0]), res[1])


def _ffn_down(s, wd, h, comm=None):
    lp, d = h.shape
    ff = s.shape[1]
    tm = _row_tile(lp)

    def down_epi(accs, in_refs, out_refs):
        out_refs[0][...] = in_refs[2][...] + 0.5 * accs[0]

    res = _matmul(
        "ffn_down", (lp // tm,), None, [s, wd.reshape(ff, d), h],
        [pl.BlockSpec((tm, ff), lambda i: (i, 0)), pl.BlockSpec((ff, d), lambda i: (0, 0)),
         pl.BlockSpec((tm, d), lambda i: (i, 0))],
        [(0, 1, "nn", 0)], [(tm, d)], down_epi,
        [jax.ShapeDtypeStruct((lp, d), F32)], [pl.BlockSpec((tm, d), lambda i: (i, 0))],
        ("parallel",), comm)
    return (res[0], []) if comm is None else (res[0][0], res[1])


def _tn_tiles(lp):
    return _row_tile(lp, 1408)


def _ffn_bwd(dh, h_in, gain, wgt, wut, wd, f4, saved, pos, comm=None):
    a, b, s = saved
    lp, d = h_in.shape
    fp = wgt.shape[1]
    ff = N_CHIPS * fp
    tm = _row_tile(lp)
    ni = lp // tm
    tk = _tn_tiles(lp)
    nk = lp // tk
    n = _rms_fwd(h_in, gain, "rms_fwd_ffn")

    def ds_epi(accs, in_refs, out_refs):
        ds = 0.5 * accs[0]
        av = in_refs[2][...].astype(F32)
        bv = in_refs[3][...].astype(F32)
        sg = _sigmoid(av)
        out_refs[0][...] = (ds * bv * sg * (1.0 + av * (1.0 - sg))).astype(BF16)
        out_refs[1][...] = (ds * av * sg).astype(BF16)

    act = jax.ShapeDtypeStruct((lp, ff), BF16)
    col_spec = pl.BlockSpec((tm, fp), lambda j, i: (i, j))
    res = _matmul(
        "ffn_bwd_ds", (N_CHIPS, ni), None, [dh, wd, a, b],
        [pl.BlockSpec((tm, d), lambda j, i: (i, 0)), pl.BlockSpec((None, fp, d), lambda j, i: (j, 0, 0)),
         col_spec, col_spec],
        [(0, 1, "nt", 0)], [(tm, fp)], ds_epi, [act, act], [col_spec, col_spec], ("parallel", "parallel"),
        comm)
    (da, db), couts = (res, []) if comm is None else res

    dw_shape = jax.ShapeDtypeStruct((N_CHIPS, f4, d), F32)
    dw_spec = pl.BlockSpec((None, f4, d), lambda j, k: (j, 0, 0))
    in_col = pl.BlockSpec((tk, fp), lambda j, k: (k, j))
    in_row = pl.BlockSpec((tk, d), lambda j, k: (k, 0))

    def dwd_epi(accs, in_refs, out_refs):
        out_refs[0][...] = 0.5 * accs[0][:f4]

    (dwd,) = _matmul(
        "ffn_dwd", (N_CHIPS, nk), 1, [s, dh], [in_col, in_row],
        [(0, 1, "tn", 0)], [(fp, d)], dwd_epi, [dw_shape], [dw_spec], ("parallel", "arbitrary"))

    def dwgu_epi(accs, in_refs, out_refs):
        for acc, o in zip(accs, out_refs):
            o[...] = acc[:f4]

    red_down = _Reduction([dwd], pos, "ffn_d")
    (dwg, dwu), red_down.got = _matmul(
        "ffn_dwgu", (N_CHIPS, nk), 1, [n, da, db], [in_row, in_col, in_col],
        [(1, 0, "tn", 0), (2, 0, "tn", 1)], [(fp, d)] * 2, dwgu_epi,
        [dw_shape, dw_shape], [dw_spec, dw_spec], ("parallel", "arbitrary"), red_down.comm())

    def dn_epi(accs, in_refs, out_refs):
        i = pl.program_id(0)
        dx, dgrow = _rms_bwd_math(accs[0], in_refs[5][...], in_refs[6][...])
        out_refs[0][...] = in_refs[4][...] + dx

        @pl.when(i == 0)
        def _():
            out_refs[1][...] = jnp.zeros_like(out_refs[1])

        out_refs[1][...] += jnp.sum(dgrow, axis=0, keepdims=True)

    red = _Reduction([dwg, dwu], pos, "ffn_gu")
    row_spec = pl.BlockSpec((tm, d), lambda i: (i, 0))
    act_spec = pl.BlockSpec((tm, ff), lambda i: (i, 0))
    w_spec = pl.BlockSpec((ff, d), lambda i: (0, 0))
    one_spec = pl.BlockSpec((1, d), lambda i: (0, 0))
    (dh_in, dgain), red.got = _matmul(
        "ffn_bwd_dn", (ni,), None, [da, wgt.reshape(ff, d), db, wut.reshape(ff, d), dh, h_in, gain.reshape(1, d)],
        [act_spec, w_spec, act_spec, w_spec, row_spec, row_spec, one_spec],
        [(0, 1, "nn", 0), (2, 3, "nn", 0)], [(tm, d)], dn_epi,
        [jax.ShapeDtypeStruct((lp, d), F32), jax.ShapeDtypeStruct((1, d), F32)],
        [row_spec, one_spec], ("arbitrary",), red.comm())
    return dh_in, dgain, [red, red_down], couts


def _rope_tables(lp):
    pos = jnp.arange(lp, dtype=F32) - float(PAD_FRONT)
    inv_freq = ROPE_THETA ** (-jnp.arange(0, ROT_DIM, 2, dtype=F32) / ROT_DIM)
    ang = pos[:, None] * inv_freq[None, :]
    cos, sin = jnp.cos(ang), jnp.sin(ang)
    half = ROT_DIM // 2
    ones = jnp.ones((lp, HEAD_DIM - ROT_DIM), F32)
    zeros_h = jnp.zeros((lp, half), F32)
    zeros_r = jnp.zeros((lp, HEAD_DIM - ROT_DIM), F32)
    c = jnp.concatenate([cos, cos, ones], axis=1)
    s1 = jnp.concatenate([-sin, zeros_h, zeros_r], axis=1)
    s2 = jnp.concatenate([zeros_h, sin, zeros_r], axis=1)
    reps = LANES // HEAD_DIM
    return jnp.stack([jnp.tile(c, (1, reps)), jnp.tile(s1, (1, reps)), jnp.tile(s2, (1, reps))])


def _rope(x, c, s1, s2):
    half = ROT_DIM // 2
    outs = []
    for ch in range(x.shape[1] // LANES):
        xc = x[:, ch * LANES:(ch + 1) * LANES]
        outs.append(xc * c + pltpu.roll(xc, LANES - half, 1) * s1 + pltpu.roll(xc, half, 1) * s2)
    return outs[0] if len(outs) == 1 else jnp.concatenate(outs, axis=1)


def _rope_t(dy, c, s1, s2):
    half = ROT_DIM // 2
    outs = []
    for ch in range(dy.shape[1] // LANES):
        dc = dy[:, ch * LANES:(ch + 1) * LANES]
        outs.append(dc * c + pltpu.roll(dc * s1, half, 1) + pltpu.roll(dc * s2, LANES - half, 1))
    return outs[0] if len(outs) == 1 else jnp.concatenate(outs, axis=1)


def _in_proj(n, w_in, tabs, ssm_w, comm=None):
    lp, d = n.shape
    inw = w_in.shape[1]
    tm = _row_tile(lp)
    o1 = ATTN_WIDTH
    o2 = o1 + KV_WIDTH
    o3 = o2 + KV_WIDTH
    o4 = o3 + ssm_w
    o5 = o4 + d

    def epi(accs, in_refs, out_refs):
        z = accs[0]
        c, s1, s2 = in_refs[2][0], in_refs[2][1], in_refs[2][2]
        out_refs[0][...] = _rope(z[:, :o1], c, s1, s2).astype(BF16)
        out_refs[1][...] = _rope(z[:, o1:o2], c, s1, s2).astype(BF16)
        out_refs[2][...] = z[:, o2:o3].astype(BF16)
        out_refs[3][...] = z[:, o3:o4]
        out_refs[4][...] = z[:, o4:o5]
        out_refs[5][...] = z[:, o5:]

    def rs(w, dt):
        return jax.ShapeDtypeStruct((lp, w), dt), pl.BlockSpec((tm, w), lambda i: (i, 0))

    shapes, specs = zip(rs(o1, BF16), rs(KV_WIDTH, BF16), rs(KV_WIDTH, BF16), rs(ssm_w, F32), rs(d, F32), rs(d, F32))
    res = _matmul(
        "mix_in_proj", (lp // tm,), None, [n, w_in, tabs],
        [pl.BlockSpec((tm, d), lambda i: (i, 0)), pl.BlockSpec((d, inw), lambda i: (0, 0)),
         pl.BlockSpec((3, tm, LANES), lambda i: (0, i, 0))],
        [(0, 1, "nn", 0)], [(tm, inw)], epi, list(shapes), list(specs), ("parallel",), comm)
    return (res, []) if comm is None else res


def _attn_mask(b):
    rows = lax.broadcasted_iota(jnp.int32, (Q_PER_KV * BLOCK, 3 * BLOCK), 0) & (BLOCK - 1)
    cols = lax.broadcasted_iota(jnp.int32, (Q_PER_KV * BLOCK, 3 * BLOCK), 1)
    qpos = b * BLOCK + rows - PAD_FRONT
    kpos = (b - 1) * BLOCK + cols - PAD_FRONT
    dist = qpos - kpos
    band = (cols < 2 * BLOCK) & (kpos >= N_META) & (dist >= 0) & (dist < BLOCK)
    mrow = cols - 2 * BLOCK
    meta = (mrow >= PAD_FRONT) & ((mrow - PAD_FRONT) <= qpos)
    return band | meta


def _attn_probs(qh, kk, mask, sink):
    s = _dot(qh, kk, "nt") * (HEAD_DIM ** -0.5)
    s = jnp.where(mask, s, NEG_INF)
    m = jnp.maximum(jnp.max(s, axis=-1, keepdims=True), sink)
    e = jnp.exp(s - m)
    es = jnp.exp(sink - m)
    z = jnp.sum(e, axis=-1, keepdims=True) + es
    inv = 1.0 / z
    return e * inv, es * inv


def _head(ref_or_val, h):
    return ref_or_val[:, h * HEAD_DIM:(h + 1) * HEAD_DIM]


def _group_rows(ref, hk):
    return jnp.concatenate([_head(ref, hk * Q_PER_KV + g) for g in range(Q_PER_KV)], axis=0)


def _group_sinks(sink_ref, hk):
    return jnp.concatenate([jnp.full((BLOCK, 1), sink_ref[hk * Q_PER_KV + g], F32) for g in range(Q_PER_KV)], axis=0)


def _attn_fwd(q, k, v, sinks, comm=None):
    lp = q.shape[0]
    nb = lp // BLOCK

    def body(sink_ref, q_ref, kp_ref, kc_ref, km_ref, vp_ref, vc_ref, vm_ref, o_ref):
        b = pl.program_id(0)
        mask = _attn_mask(b)
        for hk in range(N_KV_HEADS):
            kk = jnp.concatenate([_head(kp_ref, hk), _head(kc_ref, hk), _head(km_ref, hk)], axis=0)
            vv = jnp.concatenate([_head(vp_ref, hk), _head(vc_ref, hk), _head(vm_ref, hk)], axis=0)
            p, _ = _attn_probs(_group_rows(q_ref, hk), kk, mask, _group_sinks(sink_ref, hk))
            o4 = _dot(p, vv, "nn").astype(BF16)
            for g in range(Q_PER_KV):
                h = hk * Q_PER_KV + g
                o_ref[:, h * HEAD_DIM:(h + 1) * HEAD_DIM] = o4[g * BLOCK:(g + 1) * BLOCK]

    cur = lambda b: (b, 0)
    prev = lambda b: (jnp.maximum(b - 1, 0), 0)
    first = lambda b: (0, 0)
    kvs = lambda f: pl.BlockSpec((BLOCK, KV_WIDTH), f)
    (o,), couts = _hosted_call(
        body, comm, out_shape=[jax.ShapeDtypeStruct((lp, ATTN_WIDTH), BF16)], grid=(nb,),
        in_specs=[pl.BlockSpec(memory_space=pltpu.SMEM), pl.BlockSpec((BLOCK, ATTN_WIDTH), cur),
                  kvs(prev), kvs(cur), kvs(first), kvs(prev), kvs(cur), kvs(first)],
        out_specs=[pl.BlockSpec((BLOCK, ATTN_WIDTH), cur)], scratch_shapes=[],
        sem=("parallel",), name="attn_fwd", args=(sinks, q, k, k, k, v, v, v))
    return o, couts


def _attn_bwd(q, k, v, do, sinks, tabs, comm=None):
    lp = q.shape[0]
    nb = lp // BLOCK
    scale = HEAD_DIM ** -0.5

    def body(sink_ref, q_ref, do_ref, kp_ref, kc_ref, km_ref, vp_ref, vc_ref, vm_ref, tq_ref, tk_ref, t0_ref,
             dq_ref, dk_ref, dv_ref, dkm_ref, dvm_ref, dsink_ref,
             dq_s, dkk_s, dvv_s, ck_s, cv_s, mk_s, mv_s):
        b = pl.program_id(0)

        @pl.when(b == 0)
        def _():
            for r in (ck_s, cv_s, mk_s, mv_s, dsink_ref):
                r[...] = jnp.zeros_like(r)

        @pl.when(b < nb)
        def _():
            mask = _attn_mask(b)
            for hk in range(N_KV_HEADS):
                kk = jnp.concatenate([_head(kp_ref, hk), _head(kc_ref, hk), _head(km_ref, hk)], axis=0)
                vv = jnp.concatenate([_head(vp_ref, hk), _head(vc_ref, hk), _head(vm_ref, hk)], axis=0)
                q4, do4 = _group_rows(q_ref, hk), _group_rows(do_ref, hk)
                p, ps = _attn_probs(q4, kk, mask, _group_sinks(sink_ref, hk))
                dp = _dot(do4, vv, "nt")
                delta = jnp.sum(p * dp, axis=-1, keepdims=True)
                ds = (p * (dp - delta)).astype(BF16)
                dsk = ps * delta
                dq4 = _dot(ds, kk, "nn") * scale
                for g in range(Q_PER_KV):
                    h = hk * Q_PER_KV + g
                    rows = slice(g * BLOCK, (g + 1) * BLOCK)
                    dsink_ref[h:h + 1, :] += jnp.zeros((1, LANES), F32) - jnp.sum(dsk[rows])
                    dq_s[:, h * HEAD_DIM:(h + 1) * HEAD_DIM] = dq4[rows]
                dkk_s[:, hk * HEAD_DIM:(hk + 1) * HEAD_DIM] = _dot(ds, q4, "tn") * scale
                dvv_s[:, hk * HEAD_DIM:(hk + 1) * HEAD_DIM] = _dot(p, do4, "tn")
            dq_ref[...] = _rope_t(dq_s[...], tq_ref[0], tq_ref[1], tq_ref[2])
            dk_ref[...] = _rope_t(ck_s[...] + dkk_s[0:BLOCK, :], tk_ref[0], tk_ref[1], tk_ref[2])
            dv_ref[...] = cv_s[...] + dvv_s[0:BLOCK, :]
            ck_s[...] = dkk_s[BLOCK:2 * BLOCK, :]
            cv_s[...] = dvv_s[BLOCK:2 * BLOCK, :]
            mk_s[...] += dkk_s[2 * BLOCK:, :]
            mv_s[...] += dvv_s[2 * BLOCK:, :]

        @pl.when(b == nb)
        def _():
            dk_ref[...] = _rope_t(ck_s[...], tk_ref[0], tk_ref[1], tk_ref[2])
            dv_ref[...] = cv_s[...]
            dkm_ref[...] = _rope_t(mk_s[...], t0_ref[0], t0_ref[1], t0_ref[2])
            dvm_ref[...] = mv_s[...]

    cur = lambda b: (jnp.minimum(b, nb - 1), 0)
    prev = lambda b: (jnp.clip(b - 1, 0, nb - 1), 0)
    first = lambda b: (0, 0)
    kvs = lambda f: pl.BlockSpec((BLOCK, KV_WIDTH), f)
    tab = lambda f: pl.BlockSpec((3, BLOCK, LANES), lambda b: (0,) + f(b)[:1] + (0,))
    kv_out = lambda b: (jnp.maximum(b - 1, 0), 0)
    return _hosted_call(
        body, comm,
        out_shape=[jax.ShapeDtypeStruct((lp, ATTN_WIDTH), F32), jax.ShapeDtypeStruct((lp, KV_WIDTH), F32),
                   jax.ShapeDtypeStruct((lp, KV_WIDTH), F32), jax.ShapeDtypeStruct((BLOCK, KV_WIDTH), F32),
                   jax.ShapeDtypeStruct((BLOCK, KV_WIDTH), F32), jax.ShapeDtypeStruct((N_Q_HEADS, LANES), F32)],
        grid=(nb + 1,),
        in_specs=[pl.BlockSpec(memory_space=pltpu.SMEM), pl.BlockSpec((BLOCK, ATTN_WIDTH), cur),
                  pl.BlockSpec((BLOCK, ATTN_WIDTH), cur),
                  kvs(prev), kvs(cur), kvs(first), kvs(prev), kvs(cur), kvs(first),
                  tab(cur), tab(kv_out), tab(first)],
        out_specs=[pl.BlockSpec((BLOCK, ATTN_WIDTH), cur), kvs(kv_out), kvs(kv_out), kvs(first), kvs(first),
                   pl.BlockSpec((N_Q_HEADS, LANES), first)],
        scratch_shapes=[pltpu.VMEM((BLOCK, ATTN_WIDTH), F32), pltpu.VMEM((3 * BLOCK, KV_WIDTH), F32),
                        pltpu.VMEM((3 * BLOCK, KV_WIDTH), F32), pltpu.VMEM((BLOCK, KV_WIDTH), F32),
                        pltpu.VMEM((BLOCK, KV_WIDTH), F32), pltpu.VMEM((BLOCK, KV_WIDTH), F32),
                        pltpu.VMEM((BLOCK, KV_WIDTH), F32)],
        sem=("arbitrary",), name="attn_bwd", args=(sinks, q, do, k, k, k, v, v, v, tabs, tabs, tabs))


def _cmul(ar, ai, br, bi):
    return ar * br - ai * bi, ar * bi + ai * br


def _cpow(lr, li, n):
    rr = ri = None
    br, bi = lr, li
    while n:
        if n & 1:
            rr, ri = (br, bi) if rr is None else _cmul(rr, ri, br, bi)
        n >>= 1
        if n:
            br, bi = _cmul(br, bi, br, bi)
    return rr, ri


def _shift_rows(x, d, reverse):
    rows = lax.broadcasted_iota(jnp.int32, x.shape, 0)
    if not reverse:
        return jnp.where(rows >= d, pltpu.roll(x, d, 0), 0.0)
    return jnp.where(rows < SUBLANES - d, pltpu.roll(x, SUBLANES - d, 0), 0.0)


def _sublane_powers(mr, mi, reverse):
    rows = lax.broadcasted_iota(jnp.int32, mr.shape, 0)
    e = SUBLANES - 1 - rows if reverse else rows
    pr, pi = jnp.ones_like(mr), jnp.zeros_like(mr)
    br, bi = mr, mi
    for d in (1, 2, 4):
        tr, ti = _cmul(pr, pi, br, bi)
        on = (e & d) != 0
        pr, pi = jnp.where(on, tr, pr), jnp.where(on, ti, pi)
        if d < 4:
            br, bi = _cmul(br, bi, br, bi)
    return pr, pi


def _inclusive_prefix(er, ei, mr, mi, reverse):
    ir, ii, pr, pi = er, ei, mr, mi
    for d in (1, 2, 4):
        tr, ti = _cmul(pr, pi, _shift_rows(ir, d, reverse), _shift_rows(ii, d, reverse))
        ir, ii = ir + tr, ii + ti
        if d < 4:
            pr, pi = _cmul(pr, pi, pr, pi)
    return ir, ii


def _chain_rows(a, t, seg):
    return pl.ds(a * SUBLANES * seg + t, SUBLANES, stride=seg)


def _seg_scan(xr_ref, xi_ref, lam, seg, nchain, reverse, store, init, extra=None):
    nt = len(lam)
    acc0 = () if extra is None else extra[1]

    def step(i, carry):
        hs, acc = carry
        t = seg - 1 - i if reverse else i
        out = []
        for a in range(nchain):
            sl = _chain_rows(a, t, seg)
            for j in range(nt):
                lr, li = lam[j]
                k = 2 * (a * nt + j)
                hr, hi = hs[k], hs[k + 1]
                nr = lr * hr - li * hi + xr_ref[j, sl, :]
                ni = lr * hi + li * hr + xi_ref[j, sl, :]
                if store:
                    xr_ref[j, sl, :] = nr
                    xi_ref[j, sl, :] = ni
                if extra is not None:
                    acc = extra[0](t, a, j, nr, ni, acc)
                out += [nr, ni]
        return tuple(out), acc

    return lax.fori_loop(0, seg, step, (tuple(init), acc0))


def _ssm_scan(xr_ref, xi_ref, lam, seg, nchain, reverse, extra=None):
    nt = len(lam)
    zero = [jnp.zeros((SUBLANES, LANES), F32)] * (2 * nt * nchain)
    ends, _ = _seg_scan(xr_ref, xi_ref, lam, seg, nchain, reverse, False, zero)
    init = [None] * (2 * nt * nchain)
    last = 0 if reverse else SUBLANES - 1
    for j in range(nt):
        mr, mi = _cpow(lam[j][0], lam[j][1], seg)
        m8r, m8i = _cpow(mr, mi, SUBLANES)
        pwr, pwi = _sublane_powers(mr, mi, reverse)
        gr = gi = jnp.zeros((SUBLANES, LANES), F32)
        for a in (reversed(range(nchain)) if reverse else range(nchain)):
            k = 2 * (a * nt + j)
            incr, inci = _inclusive_prefix(ends[k], ends[k + 1], mr, mi, reverse)
            tr, ti = _cmul(pwr, pwi, gr, gi)
            init[k] = _shift_rows(incr, 1, reverse) + tr
            init[k + 1] = _shift_rows(inci, 1, reverse) + ti
            g2r, g2i = _cmul(m8r, m8i, gr, gi)
            gr = g2r + jnp.broadcast_to(incr[last:last + 1, :], gr.shape)
            gi = g2i + jnp.broadcast_to(inci[last:last + 1, :], gi.shape)
    _, acc = _seg_scan(xr_ref, xi_ref, lam, seg, nchain, reverse, True, init, extra)
    return acc


def _diag_mask():
    steps = LANES // SSM_GROUP // GROUPS_PER_TILE
    return (jnp.eye(steps, dtype=F32)[:, None, :, None] * jnp.eye(GROUPS_PER_TILE, dtype=F32)[None, :, None, :])


def _ssm_tables(bb_re, bb_im, c_re, c_im):
    g = bb_re.shape[0]
    nt = g // GROUPS_PER_TILE
    steps = LANES // SSM_GROUP // GROUPS_PER_TILE
    mask = _diag_mask()

    def b_tab(bb):
        x = bb.reshape(nt // steps, steps, GROUPS_PER_TILE, SSM_STATE, SSM_GROUP)
        x = jnp.transpose(x, (0, 1, 4, 2, 3))[:, :, None, None]
        m = jnp.transpose(mask, (0, 2, 3, 1))[None, :, :, :, None, :, None]
        return (x * m).reshape(nt, LANES, TILE_STATES)

    def c_tab(c):
        x = c.reshape(nt // steps, steps, GROUPS_PER_TILE, SSM_GROUP, SSM_STATE)
        x = jnp.transpose(x, (0, 1, 2, 4, 3))[:, :, :, :, None, None]
        m = mask[None, :, :, None, :, :, None]
        return (x * m).reshape(nt, TILE_STATES, LANES)

    return b_tab(bb_re), b_tab(bb_im), c_tab(c_re), c_tab(c_im)


def _ssm_untable_b(db, g):
    nt = g // GROUPS_PER_TILE
    steps = LANES // SSM_GROUP // GROUPS_PER_TILE
    x = db.reshape(nt // steps, steps, GROUPS_PER_TILE, SSM_STATE, steps, GROUPS_PER_TILE, SSM_GROUP)
    m = _diag_mask()[None, :, :, None, :, :, None]
    return jnp.sum(x * m, axis=(4, 5)).reshape(g, SSM_STATE, SSM_GROUP)


def _ssm_untable_c(dc, g):
    nt = g // GROUPS_PER_TILE
    steps = LANES // SSM_GROUP // GROUPS_PER_TILE
    x = dc.reshape(nt // steps, steps, steps, GROUPS_PER_TILE, SSM_GROUP, GROUPS_PER_TILE, SSM_STATE)
    m = jnp.transpose(_diag_mask(), (0, 2, 3, 1))[None, :, :, :, None, :, None]
    out = jnp.sum(x * m, axis=(2, 3))
    return jnp.transpose(out, (0, 1, 3, 2, 4)).reshape(g, SSM_GROUP, SSM_STATE)


def _lam_tiles(lam_ref):
    out = []
    for j in range(TILE_STATES // LANES):
        out.append(jnp.broadcast_to(lam_ref[:, j * LANES:(j + 1) * LANES], (SUBLANES, LANES)))
    return out


def _scan_chains(lp):
    for n in (4, 2, 1):
        if lp % (SUBLANES * n) == 0 and (lp // SUBLANES) % 16 == 0:
            return n
    raise ValueError(lp)


def _split_tiles(dst_ref, rows, val):
    for j in range(val.shape[1] // LANES):
        dst_ref[j, rows, :] = val[:, j * LANES:(j + 1) * LANES]


def _cat_tiles(src_ref, rows):
    njt = src_ref.shape[0]
    return jnp.concatenate([src_ref[j, rows, :] for j in range(njt)], axis=1).astype(BF16)


def _ssm_fwd(u, lam_re, lam_im, tb_re, tb_im, tc_re, tc_im, d_skip, comm=None):
    lp, w = u.shape
    nt = tb_re.shape[0]
    nchain = _scan_chains(lp)
    seg = lp // (SUBLANES * nchain)
    chunk = lp // SUBLANES
    njt = TILE_STATES // LANES

    def body(u_ref, lr_ref, li_ref, br_ref, bi_ref, cr_ref, ci_ref, d_ref, y_ref, xr, xi):
        t = pl.program_id(0)
        for s in range(SUBLANES):
            rs = pl.ds(s * chunk, chunk)
            ub = u_ref[rs, :].astype(BF16)
            _split_tiles(xr, rs, _dot(ub, br_ref[...], "nn"))
            _split_tiles(xi, rs, _dot(ub, bi_ref[...], "nn"))
        lrs, lis = _lam_tiles(lr_ref), _lam_tiles(li_ref)
        _ssm_scan(xr, xi, list(zip(lrs, lis)), seg, nchain, False)
        for s in range(SUBLANES):
            rs = pl.ds(s * chunk, chunk)
            y = _dot(_cat_tiles(xr, rs), cr_ref[...], "nn") - _dot(_cat_tiles(xi, rs), ci_ref[...], "nn")

            @pl.when(t % 2 == 0)
            def _():
                y_ref[rs, :] = y + d_ref[...] * u_ref[rs, :]

            @pl.when(t % 2 == 1)
            def _():
                y_ref[rs, :] += y

    blk = pl.BlockSpec((lp, LANES), lambda t: (0, t // 2))
    lam_spec = pl.BlockSpec((None, 1, TILE_STATES), lambda t: (t, 0, 0))
    b_spec = pl.BlockSpec((None, LANES, TILE_STATES), lambda t: (t, 0, 0))
    c_spec = pl.BlockSpec((None, TILE_STATES, LANES), lambda t: (t, 0, 0))
    (y,), couts = _hosted_call(
        body, comm, out_shape=[jax.ShapeDtypeStruct((lp, w), F32)], grid=(nt,),
        in_specs=[blk, lam_spec, lam_spec, b_spec, b_spec, c_spec, c_spec,
                  pl.BlockSpec((1, LANES), lambda t: (0, t // 2))],
        out_specs=[blk],
        scratch_shapes=[pltpu.VMEM((njt, lp, LANES), F32), pltpu.VMEM((njt, lp, LANES), F32)],
        sem=("arbitrary",), name="ssm_fwd",
        args=(u, lam_re, lam_im, tb_re, tb_im, tc_re, tc_im, d_skip.reshape(1, w)))
    return y, couts


def _ssm_bwd(u, dy, lam_re, lam_im, tb_re, tb_im, tc_re, tc_im, d_skip, comm=None):
    lp, w = u.shape
    nt = tb_re.shape[0]
    nchain = _scan_chains(lp)
    seg = lp // (SUBLANES * nchain)
    chunk = lp // SUBLANES
    njt = TILE_STATES // LANES
    tbt_re, tbt_im = jnp.swapaxes(tb_re, 1, 2), jnp.swapaxes(tb_im, 1, 2)
    tct_re, tct_im = jnp.swapaxes(tc_re, 1, 2), jnp.swapaxes(tc_im, 1, 2)

    def body(u_ref, dy_ref, lr_ref, li_ref, br_ref, bi_ref, btr_ref, bti_ref, ctr_ref, cti_ref, d_ref,
             du_ref, dlr_ref, dli_ref, dbr_ref, dbi_ref, dcr_ref, dci_ref, dd_ref, hr, hi, ar, ai):
        t = pl.program_id(0)
        lrs, lis = _lam_tiles(lr_ref), _lam_tiles(li_ref)
        for s in range(SUBLANES):
            rs = pl.ds(s * chunk, chunk)
            ub = u_ref[rs, :].astype(BF16)
            dyb = dy_ref[rs, :].astype(BF16)
            _split_tiles(hr, rs, _dot(ub, br_ref[...], "nn"))
            _split_tiles(hi, rs, _dot(ub, bi_ref[...], "nn"))
            _split_tiles(ar, rs, _dot(dyb, ctr_ref[...], "nn"))
            _split_tiles(ai, rs, -_dot(dyb, cti_ref[...], "nn"))
        _ssm_scan(hr, hi, list(zip(lrs, lis)), seg, nchain, False)

        def dlam_step(tt, a, j, a_r, a_i, acc):
            sl = _chain_rows(a, jnp.maximum(tt - 1, 0), seg)
            p_r, p_i = hr[j, sl, :], hi[j, sl, :]
            acc = list(acc)
            acc[2 * j] = acc[2 * j] + jnp.where(tt > 0, a_r * p_r + a_i * p_i, 0.0)
            acc[2 * j + 1] = acc[2 * j + 1] + jnp.where(tt > 0, a_i * p_r - a_r * p_i, 0.0)
            return tuple(acc)

        zero = tuple([jnp.zeros((SUBLANES, LANES), F32)] * (2 * njt))
        conj = [(lr, -li) for lr, li in zip(lrs, lis)]
        acc = list(_ssm_scan(ar, ai, conj, seg, nchain, True, (dlam_step, zero)))
        row0 = lax.broadcasted_iota(jnp.int32, (SUBLANES, LANES), 0) == 0
        for j in range(njt):
            cs = slice(j * LANES, (j + 1) * LANES)
            for a in range(nchain):
                p_r = _shift_rows(hr[j, _chain_rows(a, seg - 1, seg), :], 1, False)
                p_i = _shift_rows(hi[j, _chain_rows(a, seg - 1, seg), :], 1, False)
                if a > 0:
                    before = pl.ds(a * SUBLANES * seg - 1, 1)
                    p_r = jnp.where(row0, jnp.broadcast_to(hr[j, before, :], p_r.shape), p_r)
                    p_i = jnp.where(row0, jnp.broadcast_to(hi[j, before, :], p_i.shape), p_i)
                a_r, a_i = ar[j, _chain_rows(a, 0, seg), :], ai[j, _chain_rows(a, 0, seg), :]
                acc[2 * j] = acc[2 * j] + a_r * p_r + a_i * p_i
                acc[2 * j + 1] = acc[2 * j + 1] + a_i * p_r - a_r * p_i
            dlr_ref[:, cs] = jnp.sum(acc[2 * j], axis=0, keepdims=True)
            dli_ref[:, cs] = jnp.sum(acc[2 * j + 1], axis=0, keepdims=True)

        dd = jnp.zeros((1, LANES), F32)
        for s in range(SUBLANES):
            rs = pl.ds(s * chunk, chunk)
            ub = u_ref[rs, :].astype(BF16)
            dyv = dy_ref[rs, :]
            dyb = dyv.astype(BF16)
            arb, aib = _cat_tiles(ar, rs), _cat_tiles(ai, rs)
            hrb, hib = _cat_tiles(hr, rs), _cat_tiles(hi, rs)
            du = _dot(arb, btr_ref[...], "nn") + _dot(aib, bti_ref[...], "nn")
            upd = [(dbr_ref, _dot(arb, ub, "tn")), (dbi_ref, _dot(aib, ub, "tn")),
                   (dcr_ref, _dot(dyb, hrb, "tn")), (dci_ref, -_dot(dyb, hib, "tn"))]
            for ref, val in upd:
                if s == 0:
                    ref[...] = val
                else:
                    ref[...] += val
            rows = lax.broadcasted_iota(jnp.int32, (chunk, LANES), 0) + s * chunk
            keep = rows >= PAD_FRONT
            dd = dd + jnp.sum(dyv * u_ref[rs, :], axis=0, keepdims=True)

            @pl.when(t % 2 == 0)
            def _():
                du_ref[rs, :] = jnp.where(keep, du + d_ref[...] * dyv, 0.0)

            @pl.when(t % 2 == 1)
            def _():
                du_ref[rs, :] += jnp.where(keep, du, 0.0)

        @pl.when(t % 2 == 0)
        def _():
            dd_ref[...] = dd

    blk = pl.BlockSpec((lp, LANES), lambda t: (0, t // 2))
    vec = pl.BlockSpec((1, LANES), lambda t: (0, t // 2))
    lam_spec = pl.BlockSpec((None, 1, TILE_STATES), lambda t: (t, 0, 0))
    b_spec = pl.BlockSpec((None, LANES, TILE_STATES), lambda t: (t, 0, 0))
    c_spec = pl.BlockSpec((None, TILE_STATES, LANES), lambda t: (t, 0, 0))
    lam_shape = jax.ShapeDtypeStruct((nt, 1, TILE_STATES), F32)
    bt_shape = jax.ShapeDtypeStruct((nt, TILE_STATES, LANES), F32)
    ct_shape = jax.ShapeDtypeStruct((nt, LANES, TILE_STATES), F32)
    st = pltpu.VMEM((njt, lp, LANES), F32)
    return _hosted_call(
        body, comm,
        out_shape=[jax.ShapeDtypeStruct((lp, w), F32), lam_shape, lam_shape, bt_shape, bt_shape, ct_shape, ct_shape,
                   jax.ShapeDtypeStruct((1, w), F32)],
        grid=(nt,),
        in_specs=[blk, blk, lam_spec, lam_spec, b_spec, b_spec, c_spec, c_spec, b_spec, b_spec, vec],
        out_specs=[blk, lam_spec, lam_spec, c_spec, c_spec, b_spec, b_spec, vec],
        scratch_shapes=[st, st, st, st], sem=("arbitrary",), name="ssm_bwd",
        args=(u, dy, lam_re, lam_im, tb_re, tb_im, tbt_re, tbt_im, tct_re, tct_im, d_skip.reshape(1, w)))


def _ssm_params(a_re, a_im, log_dt, b_re, b_im):
    dt = jnp.exp(log_dt)[:, None]
    mag = jnp.exp(a_re * dt)
    lb_re = mag * jnp.cos(a_im * dt)
    lb_im = mag * jnp.sin(a_im * dt)
    den = a_re * a_re + a_im * a_im
    num_re = lb_re - 1.0
    coef_re = (num_re * a_re + lb_im * a_im) / den
    coef_im = (lb_im * a_re - num_re * a_im) / den
    bb_re = coef_re[..., None] * b_re - coef_im[..., None] * b_im
    bb_im = coef_re[..., None] * b_im + coef_im[..., None] * b_re
    return lb_re, lb_im, bb_re, bb_im


def _gelu_fwd(y):
    lp, w = y.shape
    tm = _row_tile(lp)

    def body(y_ref, o_ref):
        o_ref[...] = _gelu(y_ref[...]).astype(BF16)

    return pl.pallas_call(
        body, out_shape=jax.ShapeDtypeStruct((lp, w), BF16), grid=(lp // tm,),
        in_specs=[pl.BlockSpec((tm, w), lambda i: (i, 0))], out_specs=pl.BlockSpec((tm, w), lambda i: (i, 0)),
        compiler_params=_cparams(("parallel",)), name="gelu_fwd")(y)


def _merge_fwd(o, yg, ga, gs, w3, comm=None):
    lp, d = ga.shape
    d4 = w3.shape[3]
    kw = w3.shape[2]
    tm = _row_tile(lp)

    def epi(accs, in_refs, out_refs):
        attn, vv, gg = accs
        out_refs[0][...] = (_sigmoid(in_refs[5][...]) * attn
                            + _sigmoid(in_refs[6][...]) * (vv * _sigmoid(gg))).astype(BF16)

    wspec = lambda which: pl.BlockSpec((None, None, kw, d4), lambda j, i: (j, which, 0, 0))
    colspec = pl.BlockSpec((tm, d4), lambda j, i: (i, j))
    aspec = pl.BlockSpec((tm, kw), lambda j, i: (i, 0))
    res = _matmul(
        "merge_fwd", (N_CHIPS, lp // tm), None, [o, yg, w3, w3, w3, ga, gs],
        [aspec, aspec, wspec(0), wspec(1), wspec(2), colspec, colspec],
        [(0, 2, "nn", 0), (1, 3, "nn", 1), (1, 4, "nn", 2)], [(tm, d4)] * 3, epi,
        [jax.ShapeDtypeStruct((lp, d), BF16)], [colspec], ("parallel", "parallel"), comm)
    return (res[0], []) if comm is None else (res[0][0], res[1])


def _out_proj(merged, w_out, h, comm=None):
    lp, d = h.shape
    d4 = w_out.shape[1]
    tm = _row_tile(lp)

    def epi(accs, in_refs, out_refs):
        out_refs[0][...] = in_refs[2][...] + accs[0]

    res = _matmul(
        "mix_out_proj", (lp // tm, N_CHIPS), 1, [merged, w_out, h],
        [pl.BlockSpec((tm, d4), lambda i, j: (i, j)), pl.BlockSpec((None, d4, d), lambda i, j: (j, 0, 0)),
         pl.BlockSpec((tm, d), lambda i, j: (i, 0))],
        [(0, 1, "nn", 0)], [(tm, d)], epi,
        [jax.ShapeDtypeStruct((lp, d), F32)], [pl.BlockSpec((tm, d), lambda i, j: (i, 0))],
        ("parallel", "arbitrary"), comm)
    return (res[0], []) if comm is None else (res[0][0], res[1])


def _merge_bwd(dhb, w_out, o, yg, ga, gs, w3):
    lp, d = ga.shape
    d4 = w3.shape[3]
    kw = w3.shape[2]
    tm = _row_tile(lp)

    def epi(accs, in_refs, out_refs):
        dm, attn, vv, gg = accs
        sa = _sigmoid(in_refs[7][...])
        ss = _sigmoid(in_refs[8][...])
        sg = _sigmoid(gg)
        ssm = vv * sg
        dssm = dm * ss
        out_refs[0][...] = (dm * sa).astype(BF16)
        out_refs[1][...] = (dssm * sg).astype(BF16)
        out_refs[2][...] = (dssm * vv * sg * (1.0 - sg)).astype(BF16)
        out_refs[3][...] = (dm * attn * sa * (1.0 - sa)).astype(BF16)
        out_refs[4][...] = (dm * ssm * ss * (1.0 - ss)).astype(BF16)

    wspec = lambda which: pl.BlockSpec((None, None, kw, d4), lambda j, i: (j, which, 0, 0))
    colspec = pl.BlockSpec((tm, d4), lambda j, i: (i, j))
    aspec = pl.BlockSpec((tm, kw), lambda j, i: (i, 0))
    shp = jax.ShapeDtypeStruct((lp, d), BF16)
    return _matmul(
        "merge_bwd", (N_CHIPS, lp // tm), None, [dhb, w_out, o, yg, w3, w3, w3, ga, gs],
        [pl.BlockSpec((tm, d), lambda j, i: (i, 0)), pl.BlockSpec((None, d4, d), lambda j, i: (j, 0, 0)),
         aspec, aspec, wspec(0), wspec(1), wspec(2), colspec, colspec],
        [(0, 1, "nt", 0), (2, 4, "nn", 1), (3, 5, "nn", 2), (3, 6, "nn", 3)], [(tm, d4)] * 4, epi,
        [shp] * 5, [colspec] * 5, ("parallel", "parallel"))


def _branch_bwd(dattn, dv, dg, w3, y):
    lp, d = dattn.shape
    d4 = w3.shape[3]
    kw = w3.shape[2]
    tm = _row_tile(lp)

    def epi(accs, in_refs, out_refs):
        out_refs[0][...] = accs[0].astype(BF16)
        out_refs[1][...] = accs[1] * _gelu_grad(in_refs[6][...])

    wspec = lambda which: pl.BlockSpec((None, None, kw, d4), lambda i, j: (j, which, 0, 0))
    colspec = pl.BlockSpec((tm, d4), lambda i, j: (i, j))
    rowspec = pl.BlockSpec((tm, kw), lambda i, j: (i, 0))
    return _matmul(
        "branch_bwd", (lp // tm, N_CHIPS), 1, [dattn, dv, dg, w3, w3, w3, y],
        [colspec, colspec, colspec, wspec(0), wspec(1), wspec(2), rowspec],
        [(0, 3, "nt", 0), (1, 4, "nt", 1), (2, 5, "nt", 1)], [(tm, kw)] * 2, epi,
        [jax.ShapeDtypeStruct((lp, kw), BF16), jax.ShapeDtypeStruct((lp, kw), F32)], [rowspec, rowspec],
        ("parallel", "arbitrary"))


def _tn_cols(x, ys, name):
    lp, kx = x.shape
    n = ys[0].shape[1]
    n4 = n // N_CHIPS
    tk = _tn_tiles(lp)

    def epi(accs, in_refs, out_refs):
        for acc, o in zip(accs, out_refs):
            o[...] = acc

    shp = jax.ShapeDtypeStruct((N_CHIPS, kx, n4), F32)
    return _matmul(
        name, (N_CHIPS, lp // tk), 1, [x] + list(ys),
        [pl.BlockSpec((tk, kx), lambda j, k: (k, 0))] + [pl.BlockSpec((tk, n4), lambda j, k: (k, j))] * len(ys),
        [(0, 1 + i, "tn", i) for i in range(len(ys))], [(kx, n4)] * len(ys), epi,
        [shp] * len(ys), [pl.BlockSpec((None, kx, n4), lambda j, k: (j, 0, 0))] * len(ys),
        ("parallel", "arbitrary"))


def _tn_full(x, y, name, tn_cols=None):
    lp, kx = x.shape
    n = y.shape[1]
    tk = _tn_tiles(lp)
    tn = n if tn_cols is None else tn_cols

    def epi(accs, in_refs, out_refs):
        out_refs[0][...] = accs[0]

    (out,) = _matmul(
        name, (n // tn, lp // tk), 1, [x, y],
        [pl.BlockSpec((tk, kx), lambda j, k: (k, 0)), pl.BlockSpec((tk, tn), lambda j, k: (k, j))],
        [(0, 1, "tn", 0)], [(kx, tn)], epi,
        [jax.ShapeDtypeStruct((kx, n), F32)], [pl.BlockSpec((kx, tn), lambda j, k: (0, j))],
        ("parallel", "arbitrary"))
    return out


def _in_proj_bwd(dz, w_in, dh, h_in, gain):
    lp, d = h_in.shape
    inw = w_in.shape[1]
    tm = _row_tile(lp)

    def epi(accs, in_refs, out_refs):
        i = pl.program_id(0)
        dx, dgrow = _rms_bwd_math(accs[0], in_refs[3][...], in_refs[4][...])
        out_refs[0][...] = in_refs[2][...] + dx

        @pl.when(i == 0)
        def _():
            out_refs[1][...] = jnp.zeros_like(out_refs[1])

        out_refs[1][...] += jnp.sum(dgrow, axis=0, keepdims=True)

    row = pl.BlockSpec((tm, d), lambda i: (i, 0))
    return _matmul(
        "mix_in_proj_bwd", (lp // tm,), None, [dz, w_in, dh, h_in, gain.reshape(1, d)],
        [pl.BlockSpec((tm, inw), lambda i: (i, 0)), pl.BlockSpec((d, inw), lambda i: (0, 0)), row, row,
         pl.BlockSpec((1, d), lambda i: (0, 0))],
        [(0, 1, "nt", 0)], [(tm, d)], epi,
        [jax.ShapeDtypeStruct((lp, d), F32), jax.ShapeDtypeStruct((1, d), F32)],
        [row, pl.BlockSpec((1, d), lambda i: (0, 0))], ("arbitrary",))


def _loss_head(h, gain, target):
    lp, d = h.shape
    nb = lp // BLOCK

    def body(h_ref, g_ref, t_ref, dh_ref, dg_ref, loss_ref):
        i = pl.program_id(0)

        @pl.when(i == 0)
        def _():
            dg_ref[...] = jnp.zeros_like(dg_ref)
            loss_ref[...] = jnp.zeros_like(loss_ref)
            dh_ref[...] = jnp.zeros_like(dh_ref)

        @pl.when(i > 0)
        def _():
            x = h_ref[...]
            g = g_ref[...]
            r = lax.rsqrt(jnp.mean(x * x, axis=-1, keepdims=True) + EPS)
            err = x * r * g - t_ref[...]
            loss_ref[...] += jnp.zeros_like(loss_ref) + 0.5 * jnp.sum(jnp.sum(err * err, axis=-1, keepdims=True)) / d
            dx, dgrow = _rms_bwd_math(err * (1.0 / d), x, g)
            dh_ref[...] = dx
            dg_ref[...] += jnp.sum(dgrow, axis=0, keepdims=True)

    row = pl.BlockSpec((BLOCK, d), lambda i: (i, 0))
    one = pl.BlockSpec((1, d), lambda i: (0, 0))
    return pl.pallas_call(
        body,
        out_shape=[jax.ShapeDtypeStruct((lp, d), F32), jax.ShapeDtypeStruct((1, d), F32),
                   jax.ShapeDtypeStruct((SUBLANES, LANES), F32)],
        grid=(nb,),
        in_specs=[row, one, pl.BlockSpec((BLOCK, d), lambda i: (jnp.maximum(i - 1, 0), 0))],
        out_specs=[row, one, pl.BlockSpec((SUBLANES, LANES), lambda i: (0, 0))],
        compiler_params=_cparams(("arbitrary",)), name="loss_head")(h, gain.reshape(1, d), target)


def _adam_math(w, g, m, v):
    m = ADAM_B1 * m + (1.0 - ADAM_B1) * g
    v = ADAM_B2 * v + (1.0 - ADAM_B2) * (g * g)
    m_hat = m / (1.0 - ADAM_B1 ** ADAM_STEP)
    v_hat = v / (1.0 - ADAM_B2 ** ADAM_STEP)
    delta = -ADAM_LR * (m_hat / (jnp.sqrt(v_hat) + ADAM_EPS) + ADAM_WD * w)
    return delta, m, v


def _adamw_layers(w, m, v, mine, other, pos, name):
    depth, r, c = w.shape
    half = r // 2
    tr = _div_tile(half, c * 4)
    nh = half // tr

    def body(*refs):
        pos_ref, w_ref, m_ref, v_ref = refs[:4]
        mine_refs = refs[4:4 + depth]
        other_refs = refs[4 + depth:4 + 2 * depth]
        g_out, d_out, m_out, v_out = refs[4 + 2 * depth:]
        layer, i = pl.program_id(0), pl.program_id(1)
        is_mine = (i // nh) == pos_ref[0]

        def update(g):
            delta, nm, nv = _adam_math(w_ref[...], g, m_ref[...], v_ref[...])
            g_out[...] = g
            d_out[...] = delta
            m_out[...] = nm
            v_out[...] = nv

        for l in range(depth):
            @pl.when((layer == l) & is_mine)
            def _(l=l):
                update(mine_refs[l][...])

            @pl.when((layer == l) & jnp.logical_not(is_mine))
            def _(l=l):
                update(other_refs[l][...])

    stacked = pl.BlockSpec((None, tr, c), lambda l, i, p: (l, i, 0))

    def gspec(layer, is_other):
        def imap(l, i, p):
            first = jnp.where(is_other, 1 - p[0], p[0]) * nh
            here = jnp.clip(i - first, 0, nh - 1)
            return (jnp.where(l == layer, here, jnp.where(l < layer, 0, nh - 1)), 0)
        return pl.BlockSpec((tr, c), imap)

    shp = jax.ShapeDtypeStruct((depth, r, c), F32)
    grid_spec = pltpu.PrefetchScalarGridSpec(
        num_scalar_prefetch=1, grid=(depth, 2 * nh),
        in_specs=[stacked] * 3 + [gspec(l, 0) for l in range(depth)] + [gspec(l, 1) for l in range(depth)],
        out_specs=[stacked] * 4)
    return pl.pallas_call(
        body, out_shape=[shp] * 4, grid_spec=grid_spec,
        compiler_params=_cparams(("arbitrary", "arbitrary")), name=name)(pos, w, m, v, *mine, *other)


def _adamw_flat(w, g, m, v, name):
    r, c = w.shape
    tr = _div_tile(r, c * 4)

    def body(w_ref, g_ref, m_ref, v_ref, d_out, m_out, v_out):
        delta, nm, nv = _adam_math(w_ref[...], g_ref[...], m_ref[...], v_ref[...])
        d_out[...] = delta
        m_out[...] = nm
        v_out[...] = nv

    spec = pl.BlockSpec((tr, c), lambda i: (i, 0))
    shp = jax.ShapeDtypeStruct((r, c), F32)
    return pl.pallas_call(
        body, out_shape=[shp] * 3, grid=(r // tr,), in_specs=[spec] * 4, out_specs=[spec] * 3,
        compiler_params=_cparams(("parallel",)), name=name)(w, g, m, v)


def _mesh_pos():
    return lax.axis_index("x"), lax.axis_index("y"), lax.axis_index("c")


def _row_half(ref, which, lead):
    half = ref.shape[lead] // 2
    idx = (slice(None),) * lead + (pl.ds(which * half, half), slice(None))
    return ref.at[idx]


def _gather_comm(arrs, tag):
    n = len(arrs)

    def ctx(ins, outs, sems):
        send_sems, recv_sems, local_sems = sems
        x, y, c = _mesh_pos()
        chips = [(1 - x, y), (x, 1 - y), (1 - x, 1 - y)]

        def slot(k, chip, which):
            lead = len(ins[k].shape) - 2
            return _row_half(outs[k].at[2 * chip[0] + chip[1]], which, lead)

        def copy(k, j, src, dst, to):
            return pltpu.make_async_remote_copy(
                src_ref=src, dst_ref=dst, send_sem=send_sems.at[6 * k + j], recv_sem=recv_sems.at[6 * k + j],
                device_id=to, device_id_type=MESH)

        def local(k):
            return pltpu.make_async_copy(ins[k], outs[k].at[2 * x + y], local_sems.at[k])

        def first(k, j):
            lead = len(ins[k].shape) - 2
            return copy(k, j, _row_half(ins[k], c, lead), slot(k, (x, y), c), (*chips[j], c))

        def passed(k, j, which):
            return copy(k, 3 + j, slot(k, chips[j], which), slot(k, chips[j], which), (x, y, 1 - c))

        def landed(k, j):
            return copy(k, j, slot(k, chips[j], c), slot(k, chips[j], c), (x, y, 1 - c))

        return c, local, first, passed, landed

    def start(ins, outs, sems):
        c, local, first, passed, landed = ctx(ins, outs, sems)
        for k in range(n):
            local(k).start()
            for j in range(3):
                first(k, j).start()

    def mid(ins, outs, sems):
        c, local, first, passed, landed = ctx(ins, outs, sems)
        for j in range(3):
            for k in range(n):
                landed(k, j).wait_recv()
                passed(k, j, c).start()

    def finish(ins, outs, sems):
        c, local, first, passed, landed = ctx(ins, outs, sems)
        for j in range(3):
            for k in range(n):
                passed(k, j, 1 - c).wait_recv()
        for k in range(n):
            for j in range(3):
                first(k, j).wait_send()
                passed(k, j, c).wait_send()
            local(k).wait()

    return _Comm(
        tag, arrs, [jax.ShapeDtypeStruct((N_CHIPS,) + a.shape, a.dtype) for a in arrs],
        [pltpu.SemaphoreType.DMA((6 * n,)), pltpu.SemaphoreType.DMA((6 * n,)), pltpu.SemaphoreType.DMA((n,))],
        start, mid, finish)


def _run_comm(comm, name):
    n_in, n_out = len(comm.ins), len(comm.out_shapes)

    def body(*refs):
        ins, outs, sems = refs[:n_in], refs[n_in:n_in + n_out], refs[n_in + n_out:]
        comm.start(ins, outs, sems)
        if comm.mid is not None:
            comm.mid(ins, outs, sems)
        comm.finish(ins, outs, sems)

    return pl.pallas_call(
        body, out_shape=comm.out_shapes, in_specs=[HBM_SPEC] * n_in, out_specs=[HBM_SPEC] * n_out,
        scratch_shapes=comm.sems, name=name)(*comm.ins)


def _all_gather_chips(arrs, name):
    return _run_comm(_gather_comm(arrs, "gather"), name)


GATHER_US_PER_BYTE = 380.0 / 11.65e6
HOST_US = dict(ffn_up=78.0, ffn_down=65.0, in_proj=38.0, attn_fwd=103.0, ssm_fwd=67.0, merge_fwd=50.0,
               out_proj=45.0)
HOST_SLACK_US = 10.0


class _WeightStream:
    def __init__(self, pieces):
        self.keys = [k for k, _ in pieces]
        self.shards = dict(pieces)
        self.next = 0
        self.full = {}
        self.pending = []

    def comm_for(self, host):
        budget = HOST_US[host] + HOST_SLACK_US
        taken, cost = [], 0.0
        while self.next < len(self.keys):
            key = self.keys[self.next]
            c = self.shards[key].size * self.shards[key].dtype.itemsize * GATHER_US_PER_BYTE
            if cost + c > budget:
                break
            taken.append(key)
            cost += c
            self.next += 1
        self.pending = taken
        if not taken:
            return None
        return _gather_comm([self.shards[k] for k in taken], "g_" + "_".join(k[1] for k in taken))

    def deposit(self, gathered):
        for key, arr in zip(self.pending, gathered):
            self.full[key] = arr
        self.pending = []

    def get(self, key):
        if key not in self.full:
            upto = self.keys.index(key) + 1
            keys = self.keys[self.next:upto]
            self.next = upto
            for k, arr in zip(keys, _all_gather_chips([self.shards[k] for k in keys], "gather_now")):
                self.full[k] = arr
        return self.full[key]


def _all_gather_devices(x_shard, name):
    m_per, ncol = x_shard.shape

    def body(x_ref, out_ref, send_sems, recv_sems, local_sem):
        x, y, c = _mesh_pos()
        me, sibling = (x, y, c), (x, y, 1 - c)
        chips = [(1 - x, y), (x, 1 - y), (1 - x, 1 - y)]

        def rows(px, py, pc):
            return out_ref.at[4 * px + 2 * py + pc]

        def copy(k, block, to, src=None):
            return pltpu.make_async_remote_copy(
                src_ref=rows(*block) if src is None else src, dst_ref=rows(*block),
                send_sem=send_sems.at[k], recv_sem=recv_sems.at[k], device_id=to, device_id_type=MESH)

        mine = pltpu.make_async_copy(x_ref, rows(*me), local_sem)
        mine.start()
        first = [copy(0, me, sibling, src=x_ref)]
        first += [copy(1 + j, me, (*chip, c), src=x_ref) for j, chip in enumerate(chips)]
        for cp in first:
            cp.start()
        passed = [copy(4 + j, (*chip, c), sibling) for j, chip in enumerate(chips)]
        for j, chip in enumerate(chips):
            copy(1 + j, (*chip, c), me).wait_recv()
            passed[j].start()
        copy(0, sibling, me).wait_recv()
        for j, chip in enumerate(chips):
            copy(4 + j, (*chip, 1 - c), me).wait_recv()
        for cp in first + passed:
            cp.wait_send()
        mine.wait()

    return pl.pallas_call(
        body, out_shape=jax.ShapeDtypeStruct((8, m_per, ncol), x_shard.dtype),
        in_specs=[pl.BlockSpec(memory_space=pltpu.VMEM)], out_specs=pl.BlockSpec(memory_space=pltpu.VMEM),
        scratch_shapes=[pltpu.SemaphoreType.DMA((7,)), pltpu.SemaphoreType.DMA((7,)), pltpu.SemaphoreType.DMA],
        compiler_params=pltpu.CompilerParams(vmem_limit_bytes=VMEM_LIMIT), name=name)(x_shard)


def _sum_devices(g8, name):
    _, r, c = g8.shape
    tr = _div_tile(r, c * 4 * 8)

    def body(g_ref, o_ref):
        acc = g_ref[0]
        for dev in range(1, 8):
            acc = acc + g_ref[dev]
        o_ref[...] = acc

    return pl.pallas_call(
        body, out_shape=jax.ShapeDtypeStruct((r, c), F32), grid=(r // tr,),
        in_specs=[pl.BlockSpec((8, tr, c), lambda i: (0, i, 0))], out_specs=pl.BlockSpec((tr, c), lambda i: (i, 0)),
        compiler_params=_cparams(("parallel",)), name=name)(g8)


def _exchange_sibling_halves(arrs, name):
    n = len(arrs)

    def body(*refs):
        ins, outs = refs[:n], refs[n:2 * n]
        send_sems, recv_sems = refs[2 * n:]
        x, y, c = _mesh_pos()
        cps = []
        for k in range(n):
            cp = pltpu.make_async_remote_copy(
                src_ref=_row_half(ins[k], 1 - c, 1), dst_ref=outs[k], send_sem=send_sems.at[k],
                recv_sem=recv_sems.at[k], device_id=(x, y, 1 - c), device_id_type=MESH)
            cp.start()
            cps.append(cp)
        for cp in cps:
            cp.wait()

    return pl.pallas_call(
        body,
        out_shape=[jax.ShapeDtypeStruct((a.shape[0], a.shape[1] // 2, a.shape[2]), a.dtype) for a in arrs],
        in_specs=[HBM_SPEC] * n, out_specs=[HBM_SPEC] * n,
        scratch_shapes=[pltpu.SemaphoreType.DMA((n,)), pltpu.SemaphoreType.DMA((n,))], name=name)(*arrs)


def _chip_partials(arrs, recvs, pos, name):
    n = len(arrs)

    def body(pos_ref, *refs):
        for a_ref, b_ref, o_ref in zip(refs[:n], refs[n:2 * n], refs[2 * n:]):
            o_ref[...] = (a_ref[...] + b_ref[...]).astype(BF16)

    own_specs, recv_specs, shapes = [], [], []
    for arr in arrs:
        nslab, r, c = arr.shape
        own_specs.append(pl.BlockSpec((None, r // 2, c), lambda j, p: (j, p[0], 0)))
        recv_specs.append(pl.BlockSpec((None, r // 2, c), lambda j, p: (j, 0, 0)))
        shapes.append(jax.ShapeDtypeStruct((nslab, r // 2, c), BF16))
    grid_spec = pltpu.PrefetchScalarGridSpec(
        num_scalar_prefetch=1, grid=(N_CHIPS,), in_specs=own_specs + recv_specs, out_specs=recv_specs)
    return pl.pallas_call(
        body, out_shape=shapes, grid_spec=grid_spec,
        compiler_params=_cparams(("parallel",)), name=name)(pos, *arrs, *recvs)


def _chip_exchange_comm(parts, tag):
    n = len(parts)

    def copies(ins, outs, sems):
        send_sems, recv_sems = sems
        x, y, c = _mesh_pos()
        chips = [(1 - x, y), (x, 1 - y), (1 - x, 1 - y)]
        return [pltpu.make_async_remote_copy(
            src_ref=ins[k].at[2 * chip[0] + chip[1]], dst_ref=outs[k].at[j],
            send_sem=send_sems.at[3 * k + j], recv_sem=recv_sems.at[3 * k + j],
            device_id=(*chip, c), device_id_type=MESH) for k in range(n) for j, chip in enumerate(chips)]

    def start(ins, outs, sems):
        for cp in copies(ins, outs, sems):
            cp.start()

    def finish(ins, outs, sems):
        for cp in copies(ins, outs, sems):
            cp.wait()

    return _Comm(
        tag, parts, [jax.ShapeDtypeStruct((3,) + p.shape[1:], p.dtype) for p in parts],
        [pltpu.SemaphoreType.DMA((3 * n,)), pltpu.SemaphoreType.DMA((3 * n,))], start, None, finish)


def _reduce_halves(arrs, recvs, gots, pos, name):
    n = len(arrs)

    def body(pos_ref, *refs):
        for a_ref, b_ref, g_ref, o_ref in zip(refs[:n], refs[n:2 * n], refs[2 * n:3 * n], refs[3 * n:]):
            acc = a_ref[...] + b_ref[...]
            for j in range(3):
                acc = acc + g_ref[j].astype(F32)
            o_ref[...] = acc

    own_specs, recv_specs, got_specs, out_specs, shapes = [], [], [], [], []
    for arr in arrs:
        _, r, c = arr.shape
        own_specs.append(pl.BlockSpec((None, r // 2, c), lambda i, p: (p[1], p[0], 0)))
        recv_specs.append(pl.BlockSpec((None, r // 2, c), lambda i, p: (p[1], 0, 0)))
        got_specs.append(pl.BlockSpec((3, r // 2, c), lambda i, p: (0, 0, 0)))
        out_specs.append(pl.BlockSpec((r // 2, c), lambda i, p: (0, 0)))
        shapes.append(jax.ShapeDtypeStruct((r // 2, c), F32))
    grid_spec = pltpu.PrefetchScalarGridSpec(
        num_scalar_prefetch=1, grid=(1,), in_specs=own_specs + recv_specs + got_specs, out_specs=out_specs)
    return pl.pallas_call(
        body, out_shape=shapes, grid_spec=grid_spec,
        compiler_params=_cparams(("arbitrary",)), name=name)(pos, *arrs, *recvs, *gots)


def _share_halves(halves, name):
    n = len(halves)

    def body(*refs):
        ins, outs = refs[:n], refs[n:2 * n]
        send_sems, recv_sems = refs[2 * n:]
        x, y, c = _mesh_pos()
        cps = []
        for k in range(n):
            cp = pltpu.make_async_remote_copy(
                src_ref=ins[k], dst_ref=outs[k], send_sem=send_sems.at[k], recv_sem=recv_sems.at[k],
                device_id=(x, y, 1 - c), device_id_type=MESH)
            cp.start()
            cps.append(cp)
        for cp in cps:
            cp.wait()

    return pl.pallas_call(
        body, out_shape=[jax.ShapeDtypeStruct(h.shape, h.dtype) for h in halves],
        in_specs=[HBM_SPEC] * n, out_specs=[HBM_SPEC] * n,
        scratch_shapes=[pltpu.SemaphoreType.DMA((n,)), pltpu.SemaphoreType.DMA((n,))], name=name)(*halves)


class _Reduction:
    def __init__(self, arrs, pos, tag):
        self.arrs, self.pos, self.tag = arrs, pos, tag
        self.recv = _exchange_sibling_halves(arrs, "rs_sibling_" + tag)
        self.parts = _chip_partials(arrs, self.recv, pos, "rs_partial_" + tag)
        self.got = None

    def comm(self):
        return _chip_exchange_comm(self.parts, "rs_" + self.tag)

    def end(self):
        if self.got is None:
            self.got = _run_comm(self.comm(), "rs_chips_" + self.tag)
        halves = _reduce_halves(self.arrs, self.recv, self.got, self.pos, "rs_reduce_" + self.tag)
        return halves, _share_halves(halves, "rs_share_" + self.tag)


def _w_in_full(p, l, ws):
    if "w_in" not in p:
        slabs = ws.get((l, "w_in"))
        p["w_in"] = jnp.concatenate([slabs[j] for j in range(N_CHIPS)], axis=1)
    return p["w_in"]


def _layer_fwd(h, l, p, ws, tabs):
    def hosted(host, fn, *args):
        out, got = fn(*args, ws.comm_for(host))
        ws.deposit(got)
        return out

    a, b, sact = hosted("ffn_up", _ffn_up, h, p["ffn1_norm"], ws.get((l, "wg1")), ws.get((l, "wu1")))
    h1 = hosted("ffn_down", _ffn_down, sact, ws.get((l, "wd1")), h)
    ffn1_saved = (a, b, sact)
    n = _rms_fwd(h1, p["mix_norm"], "rms_fwd_mix")
    ssm_w = p["ssm_d"].shape[0]
    q, k, v, u, ga, gs = hosted("in_proj", _in_proj, n, _w_in_full(p, l, ws), tabs, ssm_w)
    o = hosted("attn_fwd", _attn_fwd, q, k, v, p["attn_sinks"])
    y = hosted("ssm_fwd", _ssm_fwd, u, *p["ssm_tabs"], p["ssm_d"])
    yg = _gelu_fwd(y)
    merged = hosted("merge_fwd", _merge_fwd, o, yg, ga, gs, ws.get((l, "w3")))
    h2 = hosted("out_proj", _out_proj, merged, ws.get((l, "w_out")), h1)
    a, b, sact = hosted("ffn_up", _ffn_up, h2, p["ffn2_norm"], ws.get((l, "wg2")), ws.get((l, "wu2")))
    h3 = hosted("ffn_down", _ffn_down, sact, ws.get((l, "wd2")), h2)
    saved = dict(h0=h, h1=h1, h2=h2, ffn1=ffn1_saved, ffn2=(a, b, sact), q=q, k=k, v=v, u=u, ga=ga, gs=gs, o=o, y=y,
                 yg=yg, merged=merged)
    return h3, saved


def _layer_bwd(dh, l, p, ws, s, tabs, pos):
    g = {}
    dh2, g["ffn2_norm"], red_ffn2, _ = _ffn_bwd(
        dh, s["h2"], p["ffn2_norm"], ws.get((l, "wg2")), ws.get((l, "wu2")), ws.get((l, "wd2")), p["f4"],
        s["ffn2"], pos)
    w3, w_out_w = ws.get((l, "w3")), ws.get((l, "w_out"))
    lp, d = dh2.shape
    d4 = d // N_CHIPS
    dhb = _scale_cast(dh2, 1.0, "mix_dh_cast")
    dw_out = _tn_full(s["merged"], dhb, "mix_dw_out").reshape(N_CHIPS, d4, d)
    dattn, dv, dg, dga, dgs = _merge_bwd(dhb, w_out_w, s["o"], s["yg"], s["ga"], s["gs"], w3)
    (dw_ap,) = _tn_cols(s["o"], [dattn], "mix_dw_ap")
    dw_gv, dw_gg = _tn_cols(s["yg"], [dv, dg], "mix_dw_glu")
    do, dy = _branch_bwd(dattn, dv, dg, w3, s["y"])
    (dq, dk, dvv, dkm, dvm, dsink), _ = _attn_bwd(s["q"], s["k"], s["v"], do, p["attn_sinks"], tabs)
    g["attn_sinks"] = dsink[:, 0]
    (du, dlr, dli, dbr, dbi, dcr, dci, dd), _ = _ssm_bwd(s["u"], dy, *p["ssm_tabs"], p["ssm_d"])
    ngrp = p["ssm_d"].shape[0] // SSM_GROUP
    g["ssm_lam"] = (dlr.reshape(ngrp, SSM_STATE), dli.reshape(ngrp, SSM_STATE),
                    _ssm_untable_b(dbr, ngrp), _ssm_untable_b(dbi, ngrp))
    g["ssm_c_re"] = _ssm_untable_c(dcr, ngrp)
    g["ssm_c_im"] = _ssm_untable_c(dci, ngrp)
    g["ssm_d"] = dd[0]
    dk = dk.at[:BLOCK].add(dkm)
    dvv = dvv.at[:BLOCK].add(dvm)
    dz = jnp.concatenate([dq.astype(BF16), dk.astype(BF16), dvv.astype(BF16), du.astype(BF16), dga, dgs], axis=1)
    n = _rms_fwd(s["h1"], p["mix_norm"], "rms_fwd_mix")
    w_in = _w_in_full(p, l, ws)
    inw = w_in.shape[1]
    dw_in = _tn_full(dz, n, "mix_dw_in", d // 2).reshape(N_CHIPS, inw // N_CHIPS, d)
    red_mix = _Reduction([dw_in, dw_ap, dw_gv, dw_gg, dw_out], pos, "mix")
    dh1, g["mix_norm"] = _in_proj_bwd(dz, w_in, dh2, s["h1"], p["mix_norm"])
    dh0, g["ffn1_norm"], red_ffn1, red_mix.got = _ffn_bwd(
        dh1, s["h0"], p["ffn1_norm"], ws.get((l, "wg1")), ws.get((l, "wu1")), ws.get((l, "wd1")), p["f4"],
        s["ffn1"], pos, red_mix.comm())
    return dh0, g, [*red_ffn1, red_mix, *red_ffn2]


BIG = ["ffn1_w_gate", "ffn1_w_up", "ffn1_w_down", "w_in", "w_attn_proj", "w_glu_v", "w_glu_g", "w_out",
       "ffn2_w_gate", "ffn2_w_up", "ffn2_w_down"]
TRANSPOSED = ["ffn1_w_gate", "ffn1_w_up", "w_in", "ffn2_w_gate", "ffn2_w_up"]
SMALL = ["ffn1_norm", "mix_norm", "attn_sinks", "ssm_a_re", "ssm_a_im", "ssm_log_dt", "ssm_b_re", "ssm_b_im",
         "ssm_c_re", "ssm_c_im", "ssm_d", "ffn2_norm", "final_norm"]
WEIGHTS = ["meta_tokens", "ffn1_norm", "ffn1_w_gate", "ffn1_w_up", "ffn1_w_down", "mix_norm", "w_in", "attn_sinks",
           "ssm_a_re", "ssm_a_im", "ssm_log_dt", "ssm_b_re", "ssm_b_im", "ssm_c_re", "ssm_c_im", "ssm_d",
           "w_attn_proj", "w_glu_v", "w_glu_g", "w_out", "ffn2_norm", "ffn2_w_gate", "ffn2_w_up", "ffn2_w_down",
           "final_norm"]


def _small_rows(shape):
    rows = -(-math.prod(shape) // LANES)
    return -(-rows // SUBLANES) * SUBLANES


def _pack_small(tree):
    parts = []
    for k in SMALL + ["meta_tokens"]:
        size, rows = math.prod(tree[k].shape), _small_rows(tree[k].shape)
        if size % LANES == 0:
            part = tree[k].reshape(size // LANES, LANES)
        else:
            part = jnp.pad(tree[k].reshape(1, size), ((0, 0), (0, LANES - size)))
        parts.append(jnp.pad(part, ((0, rows - part.shape[0]), (0, 0))))
    return jnp.concatenate(parts, axis=0)


def _unpack_small(packed, like):
    out, off = {}, 0
    for k in SMALL + ["meta_tokens"]:
        size, rows = math.prod(like[k].shape), _small_rows(like[k].shape)
        if size % LANES == 0:
            out[k] = packed[off:off + size // LANES].reshape(like[k].shape)
        else:
            out[k] = packed[off, :size].reshape(like[k].shape)
        off += rows
    return out


def kernel(x, meta_tokens, ffn1_norm, ffn1_w_gate, ffn1_w_up, ffn1_w_down, mix_norm, w_in, attn_sinks, ssm_a_re, ssm_a_im, ssm_log_dt, ssm_b_re, ssm_b_im, ssm_c_re, ssm_c_im, ssm_d, w_attn_proj, w_glu_v, w_glu_g, w_out, ffn2_norm, ffn2_w_gate, ffn2_w_up, ffn2_w_down, final_norm, loss_target, m_meta_tokens, m_ffn1_norm, m_ffn1_w_gate, m_ffn1_w_up, m_ffn1_w_down, m_mix_norm, m_w_in, m_attn_sinks, m_ssm_a_re, m_ssm_a_im, m_ssm_log_dt, m_ssm_b_re, m_ssm_b_im, m_ssm_c_re, m_ssm_c_im, m_ssm_d, m_w_attn_proj, m_w_glu_v, m_w_glu_g, m_w_out, m_ffn2_norm, m_ffn2_w_gate, m_ffn2_w_up, m_ffn2_w_down, m_final_norm, v_meta_tokens, v_ffn1_norm, v_ffn1_w_gate, v_ffn1_w_up, v_ffn1_w_down, v_mix_norm, v_w_in, v_attn_sinks, v_ssm_a_re, v_ssm_a_im, v_ssm_log_dt, v_ssm_b_re, v_ssm_b_im, v_ssm_c_re, v_ssm_c_im, v_ssm_d, v_w_attn_proj, v_w_glu_v, v_w_glu_g, v_w_out, v_ffn2_norm, v_ffn2_w_gate, v_ffn2_w_up, v_ffn2_w_down, v_final_norm):
    args = dict(locals())
    w = {k: args[k] for k in WEIGHTS}
    m = {k: args["m_" + k] for k in WEIGHTS}
    v = {k: args["v_" + k] for k in WEIGHTS}
    depth = ffn1_norm.shape[0]
    seq, d = x.shape[1], x.shape[2]
    lp = seq + BLOCK
    xi, yi, ci = _mesh_pos()
    pos = jnp.stack([ci, 2 * xi + yi]).astype(jnp.int32)

    tabs = _rope_tables(lp)
    (meta_all,) = _all_gather_chips([meta_tokens], "gather_meta")
    meta_full = jnp.concatenate([meta_all[j] for j in range(N_CHIPS)], axis=1)
    layers, pieces = [], []
    f4 = ffn1_w_gate.shape[2]
    fp = -(-f4 // MXU_DIM) * MXU_DIM

    def ffn_rows(wt):
        return jnp.pad(wt, ((0, fp - f4), (0, 0))).astype(BF16)

    for l in range(depth):
        pieces += [
            ((l, "wg1"), ffn_rows(ffn1_w_gate[l].T)), ((l, "wu1"), ffn_rows(ffn1_w_up[l].T)),
            ((l, "wd1"), ffn_rows(ffn1_w_down[l])), ((l, "w_in"), w_in[l].astype(BF16)),
            ((l, "w3"), jnp.stack([w_attn_proj[l], w_glu_v[l], w_glu_g[l]]).astype(BF16)),
            ((l, "w_out"), w_out[l].astype(BF16)),
            ((l, "wg2"), ffn_rows(ffn2_w_gate[l].T)), ((l, "wu2"), ffn_rows(ffn2_w_up[l].T)),
            ((l, "wd2"), ffn_rows(ffn2_w_down[l]))]
        lb_re, lb_im, bb_re, bb_im = _ssm_params(ssm_a_re[l], ssm_a_im[l], ssm_log_dt[l], ssm_b_re[l], ssm_b_im[l])
        ngrp = lb_re.shape[0]
        nt = ngrp // GROUPS_PER_TILE
        ssm_tabs = (lb_re.reshape(nt, 1, TILE_STATES), lb_im.reshape(nt, 1, TILE_STATES),
                    *_ssm_tables(bb_re, bb_im, ssm_c_re[l], ssm_c_im[l]))
        layers.append(dict(
            ffn1_norm=ffn1_norm[l], mix_norm=mix_norm[l], ffn2_norm=ffn2_norm[l], attn_sinks=attn_sinks[l],
            ssm_d=ssm_d[l], ssm_tabs=ssm_tabs, f4=f4))
    ws = _WeightStream(pieces)
    ws.get((0, "wu1"))

    h = jnp.concatenate([jnp.zeros((PAD_FRONT, d), F32), meta_full, x[0]], axis=0)
    saved = []
    for l in range(depth):
        h, s = _layer_fwd(h, l, layers[l], ws, tabs)
        saved.append(s)
    dh, g_final, loss_acc = _loss_head(h, final_norm, loss_target[0])
    loss = lax.psum(loss_acc[0, 0], ("x", "y", "c"))

    grads, reds = [None] * depth, [None] * depth
    for l in reversed(range(depth)):
        dh, grads[l], reds[l] = _layer_bwd(dh, l, layers[l], ws, saved[l], tabs, pos)
    grad_x = dh[BLOCK:][None]
    dmeta_local = dh[PAD_FRONT:BLOCK]

    small = {k: [] for k in SMALL}
    for l in range(depth):
        gl = grads[l]
        _, vjp = jax.vjp(_ssm_params, ssm_a_re[l], ssm_a_im[l], ssm_log_dt[l], ssm_b_re[l], ssm_b_im[l])
        da_re, da_im, dlog_dt, db_re, db_im = vjp(gl["ssm_lam"])
        for k, val in (("ffn1_norm", gl["ffn1_norm"][0]), ("mix_norm", gl["mix_norm"][0]),
                       ("attn_sinks", gl["attn_sinks"]), ("ssm_a_re", da_re), ("ssm_a_im", da_im),
                       ("ssm_log_dt", dlog_dt), ("ssm_b_re", db_re), ("ssm_b_im", db_im),
                       ("ssm_c_re", gl["ssm_c_re"]), ("ssm_c_im", gl["ssm_c_im"]), ("ssm_d", gl["ssm_d"]),
                       ("ffn2_norm", gl["ffn2_norm"][0])):
            small[k].append(val)
    small_local = {k: jnp.stack(vals) for k, vals in small.items() if k != "final_norm"}
    small_local["final_norm"] = g_final[0]
    small_local["meta_tokens"] = dmeta_local
    like = dict(small_local)
    g_small = _sum_devices(_all_gather_devices(_pack_small(small_local), "gather_small_grads"), "sum_small_grads")
    g_small_tree = _unpack_small(g_small, like)
    d4 = d // N_CHIPS
    chip = 2 * xi + yi
    g_meta = lax.dynamic_slice_in_dim(g_small_tree["meta_tokens"], chip * d4, d4, axis=1)

    reduced = []
    for l in range(depth):
        mine, other = [], []
        for red in reds[l]:
            halves, sibling_halves = red.end()
            mine += halves
            other += sibling_halves
        reduced.append((mine, other))

    g_out, delta, new_m, new_v = {}, {}, {}, {}
    for i, k in enumerate(BIG):
        flip = (lambda t: jnp.swapaxes(t, 1, 2)) if k in TRANSPOSED else (lambda t: t)
        outs = _adamw_layers(
            flip(w[k]), flip(m[k]), flip(v[k]), [reduced[l][0][i] for l in range(depth)],
            [reduced[l][1][i] for l in range(depth)], pos, "adamw_" + k)
        g_out[k], delta[k], new_m[k], new_v[k] = [flip(t) for t in outs]
    small_names = SMALL + ["meta_tokens"]
    w_small = {k: w[k] for k in small_names}
    m_small = {k: m[k] for k in small_names}
    v_small = {k: v[k] for k in small_names}
    g_small_local = dict(g_small_tree)
    g_small_local["meta_tokens"] = g_meta
    d_s, m_s, v_s = _adamw_flat(_pack_small(w_small), _pack_small(g_small_local), _pack_small(m_small),
                                _pack_small(v_small), "adamw_small")
    for tree, packed in ((delta, d_s), (new_m, m_s), (new_v, v_s)):
        tree.update(_unpack_small(packed, w_small))
    for k in small_names:
        g_out[k] = g_small_local[k]

    return (loss, grad_x, *[g_out[k] for k in WEIGHTS], *[delta[k] for k in WEIGHTS],
            *[new_m[k] for k in WEIGHTS], *[new_v[k] for k in WEIGHTS])
```

```python
import functools
import math

import jax
import jax.numpy as jnp
from jax import lax
from jax.experimental import pallas as pl
from jax.experimental.pallas import tpu as pltpu

F32 = jnp.float32
BF16 = jnp.bfloat16

N_META = 16
HEAD_DIM = 64
N_Q_HEADS = 8
N_KV_HEADS = 2
Q_PER_KV = N_Q_HEADS // N_KV_HEADS
ATTN_WIDTH = N_Q_HEADS * HEAD_DIM
KV_WIDTH = N_KV_HEADS * HEAD_DIM
BLOCK = 128
PAD_FRONT = BLOCK - N_META
ROPE_THETA = 500000.0
ROT_DIM = HEAD_DIM // 4
SSM_GROUP = 16
SSM_STATE = 64
GROUPS_PER_TILE = 4
TILE_STATES = GROUPS_PER_TILE * SSM_STATE
LANES = 128
SUBLANES = 8
MXU_DIM = 256
EPS = 1e-6
NEG_INF = -1e30
N_CHIPS = 4

ADAM_LR = 0.001
ADAM_B1 = 0.9
ADAM_B2 = 0.999
ADAM_EPS = 1e-08
ADAM_WD = 0.01
ADAM_STEP = 10

VMEM_LIMIT = 56 * 1024 * 1024
MESH = pl.DeviceIdType.MESH


def _cparams(sem=None):
    return pltpu.CompilerParams(dimension_semantics=sem, vmem_limit_bytes=VMEM_LIMIT)


def _row_tile(rows, limit=512):
    best = None
    for t in range(128, limit + 1, 128):
        if rows % t == 0:
            best = t
    assert best is not None, rows
    return best


def _div_tile(rows, row_bytes, max_bytes=1 << 20, mult=8):
    best = None
    for t in range(mult, rows + 1, mult):
        if rows % t == 0 and t * row_bytes <= max_bytes:
            best = t
    if best is None:
        best = rows
    return best


def _dot(a, b, mode):
    if mode == "nn":
        dims = (((1,), (0,)), ((), ()))
    elif mode == "nt":
        dims = (((1,), (1,)), ((), ()))
    else:
        dims = (((0,), (0,)), ((), ()))
    return lax.dot_general(a.astype(BF16), b.astype(BF16), dims, preferred_element_type=F32)


def _sigmoid(x):
    return 1.0 / (1.0 + jnp.exp(-x))


_GELU_C = math.sqrt(2.0 / math.pi)


def _gelu(x):
    return 0.5 * x * (1.0 + jnp.tanh(_GELU_C * (x + 0.044715 * x * x * x)))


def _gelu_grad(x):
    t = jnp.tanh(_GELU_C * (x + 0.044715 * x * x * x))
    return 0.5 * (1.0 + t) + 0.5 * x * (1.0 - t * t) * _GELU_C * (1.0 + 3.0 * 0.044715 * x * x)


class _Comm:
    def __init__(self, tag, ins, out_shapes, sems, start, mid, finish):
        self.tag, self.ins, self.out_shapes, self.sems = tag, list(ins), list(out_shapes), list(sems)
        self.start, self.mid, self.finish = start, mid, finish


HBM_SPEC = pl.BlockSpec(memory_space=pltpu.HBM)


def _hosted_call(body, comm, *, out_shape, grid, in_specs, out_specs, scratch_shapes, sem, name, args):
    out_shape, in_specs, out_specs = list(out_shape), list(in_specs), list(out_specs)
    scratch_shapes = list(scratch_shapes)
    if comm is None:
        res = pl.pallas_call(
            body, out_shape=out_shape, grid=grid, in_specs=in_specs, out_specs=out_specs,
            scratch_shapes=scratch_shapes, compiler_params=_cparams(sem), name=name)(*args)
        return list(res), []
    n_in, n_out, n_sc = len(args), len(out_shape), len(scratch_shapes)
    nci, nco = len(comm.ins), len(comm.out_shapes)
    total = math.prod(grid)

    def wrapped(*refs):
        in_refs, cin = refs[:n_in], refs[n_in:n_in + nci]
        o0 = n_in + nci
        out_refs, cout = refs[o0:o0 + n_out], refs[o0 + n_out:o0 + n_out + nco]
        s0 = o0 + n_out + nco
        sc, csem = refs[s0:s0 + n_sc], refs[s0 + n_sc:]
        lin = 0
        for dim, size in enumerate(grid):
            lin = lin * size + pl.program_id(dim)

        @pl.when(lin == 0)
        def _():
            comm.start(cin, cout, csem)

        if comm.mid is not None:
            @pl.when(lin == total // 2)
            def _():
                comm.mid(cin, cout, csem)

        body(*in_refs, *out_refs, *sc)

        @pl.when(lin == total - 1)
        def _():
            comm.finish(cin, cout, csem)

    res = pl.pallas_call(
        wrapped, out_shape=out_shape + comm.out_shapes, grid=grid,
        in_specs=in_specs + [HBM_SPEC] * nci, out_specs=out_specs + [HBM_SPEC] * nco,
        scratch_shapes=scratch_shapes + comm.sems,
        compiler_params=_cparams(("arbitrary",) * len(grid)), name=name + "_" + comm.tag)(*args, *comm.ins)
    return list(res[:n_out]), list(res[n_out:])


def _matmul(name, grid, k_axis, ins, in_specs, pairs, acc_shapes, epilogue, out_shapes, out_specs, sem, comm=None):
    n_in, n_out, n_acc = len(ins), len(out_shapes), len(acc_shapes)

    def body(*refs):
        in_refs = refs[:n_in]
        out_refs = refs[n_in:n_in + n_out]
        acc_refs = refs[n_in + n_out:]
        if k_axis is None:
            accs = [None] * n_acc
            for ia, ib, mode, iacc in pairs:
                d = _dot(in_refs[ia][...], in_refs[ib][...], mode)
                accs[iacc] = d if accs[iacc] is None else accs[iacc] + d
            epilogue(accs, in_refs, out_refs)
            return
        k = pl.program_id(k_axis)

        @pl.when(k == 0)
        def _():
            for r in acc_refs:
                r[...] = jnp.zeros_like(r)

        for ia, ib, mode, iacc in pairs:
            acc_refs[iacc][...] += _dot(in_refs[ia][...], in_refs[ib][...], mode)

        @pl.when(k == pl.num_programs(k_axis) - 1)
        def _():
            epilogue([r[...] for r in acc_refs], in_refs, out_refs)

    scratch = [] if k_axis is None else [pltpu.VMEM(s, F32) for s in acc_shapes]
    outs, couts = _hosted_call(
        body, comm, out_shape=out_shapes, grid=grid, in_specs=in_specs, out_specs=out_specs,
        scratch_shapes=scratch, sem=sem, name=name, args=ins)
    return outs if comm is None else (outs, couts)


def _rms_fwd(h, g, name):
    lp, d = h.shape
    tm = _row_tile(lp)

    def body(h_ref, g_ref, n_ref):
        x = h_ref[...]
        r = lax.rsqrt(jnp.mean(x * x, axis=-1, keepdims=True) + EPS)
        n_ref[...] = (x * r * g_ref[...]).astype(BF16)

    return pl.pallas_call(
        body, out_shape=jax.ShapeDtypeStruct((lp, d), BF16), grid=(lp // tm,),
        in_specs=[pl.BlockSpec((tm, d), lambda i: (i, 0)), pl.BlockSpec((1, d), lambda i: (0, 0))],
        out_specs=pl.BlockSpec((tm, d), lambda i: (i, 0)),
        compiler_params=_cparams(("parallel",)), name=name)(h, g.reshape(1, d))


def _rms_bwd_math(dn, x, g):
    r = lax.rsqrt(jnp.mean(x * x, axis=-1, keepdims=True) + EPS)
    xh = x * r
    dxh = dn * g
    dx = r * (dxh - xh * jnp.mean(dxh * xh, axis=-1, keepdims=True))
    return dx, dn * xh


def _scale_cast(x, scale, name):
    lp, d = x.shape
    tm = _row_tile(lp)

    def body(x_ref, o_ref):
        o_ref[...] = (x_ref[...] * scale).astype(BF16)

    return pl.pallas_call(
        body, out_shape=jax.ShapeDtypeStruct((lp, d), BF16), grid=(lp // tm,),
        in_specs=[pl.BlockSpec((tm, d), lambda i: (i, 0))], out_specs=pl.BlockSpec((tm, d), lambda i: (i, 0)),
        compiler_params=_cparams(("parallel",)), name=name)(x)


def _ffn_up(h, gain, wgt, wut, comm=None):
    lp, d = h.shape
    fp = wgt.shape[1]
    tm = _row_tile(lp)
    n = _rms_fwd(h, gain, "rms_fwd_ffn")

    def up_epi(accs, in_refs, out_refs):
        a, b = accs
        out_refs[0][...] = a.astype(BF16)
        out_refs[1][...] = b.astype(BF16)
        out_refs[2][...] = (a * _sigmoid(a) * b).astype(BF16)

    act = jax.ShapeDtypeStruct((lp, N_CHIPS * fp), BF16)
    w_spec = pl.BlockSpec((None, fp, d), lambda j, i: (j, 0, 0))
    res = _matmul(
        "ffn_up", (N_CHIPS, lp // tm), None, [n, wgt, wut],
        [pl.BlockSpec((tm, d), lambda j, i: (i, 0)), w_spec, w_spec],
        [(0, 1, "nt", 0), (0, 2, "nt", 1)], [(tm, fp)] * 2, up_epi,
        [act, act, act], [pl.BlockSpec((tm, fp), lambda j, i: (i, j))] * 3,
        ("parallel", "parallel"), comm)
    outs, couts = (res, []) if comm is None else res
    return (*outs, n), couts


def _ffn_down(s, wd, h, comm=None):
    lp, d = h.shape
    ff = s.shape[1]
    tm = _row_tile(lp)

    def down_epi(accs, in_refs, out_refs):
        out_refs[0][...] = in_refs[2][...] + 0.5 * accs[0]

    res = _matmul(
        "ffn_down", (lp // tm,), None, [s, wd.reshape(ff, d), h],
        [pl.BlockSpec((tm, ff), lambda i: (i, 0)), pl.BlockSpec((ff, d), lambda i: (0, 0)),
         pl.BlockSpec((tm, d), lambda i: (i, 0))],
        [(0, 1, "nn", 0)], [(tm, d)], down_epi,
        [jax.ShapeDtypeStruct((lp, d), F32)], [pl.BlockSpec((tm, d), lambda i: (i, 0))],
        ("parallel",), comm)
    return (res[0], []) if comm is None else (res[0][0], res[1])


def _tn_tiles(lp):
    return _row_tile(lp, 1408)


def _ffn_bwd(dh, h_in, gain, wgt, wut, wd, f4, saved, pos, comm=None):
    a, b, s, n = saved
    lp, d = h_in.shape
    fp = wgt.shape[1]
    ff = N_CHIPS * fp
    tm = _row_tile(lp)
    ni = lp // tm
    tk = _tn_tiles(lp)
    nk = lp // tk

    def ds_epi(accs, in_refs, out_refs):
        ds = 0.5 * accs[0]
        av = in_refs[2][...].astype(F32)
        bv = in_refs[3][...].astype(F32)
        sg = _sigmoid(av)
        out_refs[0][...] = (ds * bv * sg * (1.0 + av * (1.0 - sg))).astype(BF16)
        out_refs[1][...] = (ds * av * sg).astype(BF16)

    act = jax.ShapeDtypeStruct((lp, ff), BF16)
    col_spec = pl.BlockSpec((tm, fp), lambda j, i: (i, j))
    res = _matmul(
        "ffn_bwd_ds", (N_CHIPS, ni), None, [dh, wd, a, b],
        [pl.BlockSpec((tm, d), lambda j, i: (i, 0)), pl.BlockSpec((None, fp, d), lambda j, i: (j, 0, 0)),
         col_spec, col_spec],
        [(0, 1, "nt", 0)], [(tm, fp)], ds_epi, [act, act], [col_spec, col_spec], ("parallel", "parallel"),
        comm)
    (da, db), couts = (res, []) if comm is None else res

    dw_shape = jax.ShapeDtypeStruct((N_CHIPS, f4, d), F32)
    dw_spec = pl.BlockSpec((None, f4, d), lambda j, k: (j, 0, 0))
    in_col = pl.BlockSpec((tk, fp), lambda j, k: (k, j))
    in_row = pl.BlockSpec((tk, d), lambda j, k: (k, 0))

    def dwd_epi(accs, in_refs, out_refs):
        out_refs[0][...] = 0.5 * accs[0][:f4]

    (dwd,) = _matmul(
        "ffn_dwd", (N_CHIPS, nk), 1, [s, dh], [in_col, in_row],
        [(0, 1, "tn", 0)], [(fp, d)], dwd_epi, [dw_shape], [dw_spec], ("parallel", "arbitrary"))

    def dwgu_epi(accs, in_refs, out_refs):
        for acc, o in zip(accs, out_refs):
            o[...] = acc[:f4]

    red_down = _Reduction([dwd], pos, "ffn_d")
    (dwg, dwu), red_down.got = _matmul(
        "ffn_dwgu", (N_CHIPS, nk), 1, [n, da, db], [in_row, in_col, in_col],
        [(1, 0, "tn", 0), (2, 0, "tn", 1)], [(fp, d)] * 2, dwgu_epi,
        [dw_shape, dw_shape], [dw_spec, dw_spec], ("parallel", "arbitrary"), red_down.comm())

    def dn_epi(accs, in_refs, out_refs):
        i = pl.program_id(0)
        dx, dgrow = _rms_bwd_math(accs[0], in_refs[5][...], in_refs[6][...])
        out_refs[0][...] = in_refs[4][...] + dx

        @pl.when(i == 0)
        def _():
            out_refs[1][...] = jnp.zeros_like(out_refs[1])

        out_refs[1][...] += jnp.sum(dgrow, axis=0, keepdims=True)

    red = _Reduction([dwg, dwu], pos, "ffn_gu")
    row_spec = pl.BlockSpec((tm, d), lambda i: (i, 0))
    act_spec = pl.BlockSpec((tm, ff), lambda i: (i, 0))
    w_spec = pl.BlockSpec((ff, d), lambda i: (0, 0))
    one_spec = pl.BlockSpec((1, d), lambda i: (0, 0))
    (dh_in, dgain), red.got = _matmul(
        "ffn_bwd_dn", (ni,), None, [da, wgt.reshape(ff, d), db, wut.reshape(ff, d), dh, h_in, gain.reshape(1, d)],
        [act_spec, w_spec, act_spec, w_spec, row_spec, row_spec, one_spec],
        [(0, 1, "nn", 0), (2, 3, "nn", 0)], [(tm, d)], dn_epi,
        [jax.ShapeDtypeStruct((lp, d), F32), jax.ShapeDtypeStruct((1, d), F32)],
        [row_spec, one_spec], ("arbitrary",), red.comm())
    return dh_in, dgain, [red, red_down], couts


def _rope_tables(lp):
    pos = jnp.arange(lp, dtype=F32) - float(PAD_FRONT)
    inv_freq = ROPE_THETA ** (-jnp.arange(0, ROT_DIM, 2, dtype=F32) / ROT_DIM)
    ang = pos[:, None] * inv_freq[None, :]
    cos, sin = jnp.cos(ang), jnp.sin(ang)
    half = ROT_DIM // 2
    ones = jnp.ones((lp, HEAD_DIM - ROT_DIM), F32)
    zeros_h = jnp.zeros((lp, half), F32)
    zeros_r = jnp.zeros((lp, HEAD_DIM - ROT_DIM), F32)
    c = jnp.concatenate([cos, cos, ones], axis=1)
    s1 = jnp.concatenate([-sin, zeros_h, zeros_r], axis=1)
    s2 = jnp.concatenate([zeros_h, sin, zeros_r], axis=1)
    reps = LANES // HEAD_DIM
    return jnp.stack([jnp.tile(c, (1, reps)), jnp.tile(s1, (1, reps)), jnp.tile(s2, (1, reps))])


def _rope(x, c, s1, s2):
    half = ROT_DIM // 2
    outs = []
    for ch in range(x.shape[1] // LANES):
        xc = x[:, ch * LANES:(ch + 1) * LANES]
        outs.append(xc * c + pltpu.roll(xc, LANES - half, 1) * s1 + pltpu.roll(xc, half, 1) * s2)
    return outs[0] if len(outs) == 1 else jnp.concatenate(outs, axis=1)


def _rope_t(dy, c, s1, s2):
    half = ROT_DIM // 2
    outs = []
    for ch in range(dy.shape[1] // LANES):
        dc = dy[:, ch * LANES:(ch + 1) * LANES]
        outs.append(dc * c + pltpu.roll(dc * s1, half, 1) + pltpu.roll(dc * s2, LANES - half, 1))
    return outs[0] if len(outs) == 1 else jnp.concatenate(outs, axis=1)


def _in_proj(n, w_in, tabs, ssm_w, comm=None):
    lp, d = n.shape
    inw = w_in.shape[1]
    tm = _row_tile(lp)
    o1 = ATTN_WIDTH
    o2 = o1 + KV_WIDTH
    o3 = o2 + KV_WIDTH
    o4 = o3 + ssm_w
    o5 = o4 + d

    def epi(accs, in_refs, out_refs):
        z = accs[0]
        c, s1, s2 = in_refs[2][0], in_refs[2][1], in_refs[2][2]
        out_refs[0][...] = _rope(z[:, :o1], c, s1, s2).astype(BF16)
        out_refs[1][...] = _rope(z[:, o1:o2], c, s1, s2).astype(BF16)
        out_refs[2][...] = z[:, o2:o3].astype(BF16)
        out_refs[3][...] = z[:, o3:o4]
        out_refs[4][...] = z[:, o4:o5]
        out_refs[5][...] = z[:, o5:]

    def rs(w, dt):
        return jax.ShapeDtypeStruct((lp, w), dt), pl.BlockSpec((tm, w), lambda i: (i, 0))

    shapes, specs = zip(rs(o1, BF16), rs(KV_WIDTH, BF16), rs(KV_WIDTH, BF16), rs(ssm_w, F32), rs(d, F32), rs(d, F32))
    res = _matmul(
        "mix_in_proj", (lp // tm,), None, [n, w_in, tabs],
        [pl.BlockSpec((tm, d), lambda i: (i, 0)), pl.BlockSpec((d, inw), lambda i: (0, 0)),
         pl.BlockSpec((3, tm, LANES), lambda i: (0, i, 0))],
        [(0, 1, "nn", 0)], [(tm, inw)], epi, list(shapes), list(specs), ("parallel",), comm)
    return (res, []) if comm is None else res


def _attn_mask(b):
    rows = lax.broadcasted_iota(jnp.int32, (BLOCK, 3 * BLOCK), 0)
    cols = lax.broadcasted_iota(jnp.int32, (BLOCK, 3 * BLOCK), 1)
    qpos = b * BLOCK + rows - PAD_FRONT
    kpos = (b - 1) * BLOCK + cols - PAD_FRONT
    dist = qpos - kpos
    band = (cols < 2 * BLOCK) & (kpos >= N_META) & (dist >= 0) & (dist < BLOCK)
    mrow = cols - 2 * BLOCK
    meta = (mrow >= PAD_FRONT) & ((mrow - PAD_FRONT) <= qpos)
    return band | meta


def _attn_probs(qh, kk, mask, sink):
    s = _dot(qh, kk, "nt") * (HEAD_DIM ** -0.5)
    s = jnp.where(mask, s, NEG_INF)
    m = jnp.maximum(jnp.max(s, axis=-1, keepdims=True), sink)
    e = jnp.exp(s - m)
    es = jnp.exp(sink - m)
    z = jnp.sum(e, axis=-1, keepdims=True) + es
    inv = 1.0 / z
    return e * inv, es * inv


def _head(ref_or_val, h):
    return ref_or_val[:, h * HEAD_DIM:(h + 1) * HEAD_DIM]


def _attn_fwd(q, k, v, sinks, comm=None):
    lp = q.shape[0]
    nb = lp // BLOCK

    def body(sink_ref, q_ref, kp_ref, kc_ref, km_ref, vp_ref, vc_ref, vm_ref, o_ref):
        b = pl.program_id(0)
        mask = _attn_mask(b)
        for hk in range(N_KV_HEADS):
            kk = jnp.concatenate([_head(kp_ref, hk), _head(kc_ref, hk), _head(km_ref, hk)], axis=0)
            vv = jnp.concatenate([_head(vp_ref, hk), _head(vc_ref, hk), _head(vm_ref, hk)], axis=0)
            for g in range(Q_PER_KV):
                h = hk * Q_PER_KV + g
                p, _ = _attn_probs(_head(q_ref, h), kk, mask, sink_ref[h])
                o_ref[:, h * HEAD_DIM:(h + 1) * HEAD_DIM] = _dot(p, vv, "nn").astype(BF16)

    cur = lambda b: (b, 0)
    prev = lambda b: (jnp.maximum(b - 1, 0), 0)
    first = lambda b: (0, 0)
    kvs = lambda f: pl.BlockSpec((BLOCK, KV_WIDTH), f)
    (o,), couts = _hosted_call(
        body, comm, out_shape=[jax.ShapeDtypeStruct((lp, ATTN_WIDTH), BF16)], grid=(nb,),
        in_specs=[pl.BlockSpec(memory_space=pltpu.SMEM), pl.BlockSpec((BLOCK, ATTN_WIDTH), cur),
                  kvs(prev), kvs(cur), kvs(first), kvs(prev), kvs(cur), kvs(first)],
        out_specs=[pl.BlockSpec((BLOCK, ATTN_WIDTH), cur)], scratch_shapes=[],
        sem=("parallel",), name="attn_fwd", args=(sinks, q, k, k, k, v, v, v))
    return o, couts


def _attn_bwd(q, k, v, do, sinks, tabs, comm=None):
    lp = q.shape[0]
    nb = lp // BLOCK
    scale = HEAD_DIM ** -0.5

    def body(sink_ref, q_ref, do_ref, kp_ref, kc_ref, km_ref, vp_ref, vc_ref, vm_ref, tq_ref, tk_ref, t0_ref,
             dq_ref, dk_ref, dv_ref, dkm_ref, dvm_ref, dsink_ref,
             dq_s, dkk_s, dvv_s, ck_s, cv_s, mk_s, mv_s):
        b = pl.program_id(0)

        @pl.when(b == 0)
        def _():
            for r in (ck_s, cv_s, mk_s, mv_s, dsink_ref):
                r[...] = jnp.zeros_like(r)

        @pl.when(b < nb)
        def _():
            mask = _attn_mask(b)
            for hk in range(N_KV_HEADS):
                kk = jnp.concatenate([_head(kp_ref, hk), _head(kc_ref, hk), _head(km_ref, hk)], axis=0)
                vv = jnp.concatenate([_head(vp_ref, hk), _head(vc_ref, hk), _head(vm_ref, hk)], axis=0)
                dkk = jnp.zeros((3 * BLOCK, HEAD_DIM), F32)
                dvv = jnp.zeros((3 * BLOCK, HEAD_DIM), F32)
                for g in range(Q_PER_KV):
                    h = hk * Q_PER_KV + g
                    qh = _head(q_ref, h)
                    doh = _head(do_ref, h)
                    p, ps = _attn_probs(qh, kk, mask, sink_ref[h])
                    dp = _dot(doh, vv, "nt")
                    delta = jnp.sum(p * dp, axis=-1, keepdims=True)
                    ds = (p * (dp - delta)).astype(BF16)
                    dsink_ref[h:h + 1, :] += jnp.zeros((1, LANES), F32) - jnp.sum(ps * delta)
                    dq_s[:, h * HEAD_DIM:(h + 1) * HEAD_DIM] = _dot(ds, kk, "nn") * scale
                    dkk = dkk + _dot(ds, qh, "tn") * scale
                    dvv = dvv + _dot(p, doh, "tn")
                dkk_s[:, hk * HEAD_DIM:(hk + 1) * HEAD_DIM] = dkk
                dvv_s[:, hk * HEAD_DIM:(hk + 1) * HEAD_DIM] = dvv
            dq_ref[...] = _rope_t(dq_s[...], tq_ref[0], tq_ref[1], tq_ref[2])
            dk_ref[...] = _rope_t(ck_s[...] + dkk_s[0:BLOCK, :], tk_ref[0], tk_ref[1], tk_ref[2])
            dv_ref[...] = cv_s[...] + dvv_s[0:BLOCK, :]
            ck_s[...] = dkk_s[BLOCK:2 * BLOCK, :]
            cv_s[...] = dvv_s[BLOCK:2 * BLOCK, :]
            mk_s[...] += dkk_s[2 * BLOCK:, :]
            mv_s[...] += dvv_s[2 * BLOCK:, :]

        @pl.when(b == nb)
        def _():
            dk_ref[...] = _rope_t(ck_s[...], tk_ref[0], tk_ref[1], tk_ref[2])
            dv_ref[...] = cv_s[...]
            dkm_ref[...] = _rope_t(mk_s[...], t0_ref[0], t0_ref[1], t0_ref[2])
            dvm_ref[...] = mv_s[...]

    cur = lambda b: (jnp.minimum(b, nb - 1), 0)
    prev = lambda b: (jnp.clip(b - 1, 0, nb - 1), 0)
    first = lambda b: (0, 0)
    kvs = lambda f: pl.BlockSpec((BLOCK, KV_WIDTH), f)
    tab = lambda f: pl.BlockSpec((3, BLOCK, LANES), lambda b: (0,) + f(b)[:1] + (0,))
    kv_out = lambda b: (jnp.maximum(b - 1, 0), 0)
    return _hosted_call(
        body, comm,
        out_shape=[jax.ShapeDtypeStruct((lp, ATTN_WIDTH), F32), jax.ShapeDtypeStruct((lp, KV_WIDTH), F32),
                   jax.ShapeDtypeStruct((lp, KV_WIDTH), F32), jax.ShapeDtypeStruct((BLOCK, KV_WIDTH), F32),
                   jax.ShapeDtypeStruct((BLOCK, KV_WIDTH), F32), jax.ShapeDtypeStruct((N_Q_HEADS, LANES), F32)],
        grid=(nb + 1,),
        in_specs=[pl.BlockSpec(memory_space=pltpu.SMEM), pl.BlockSpec((BLOCK, ATTN_WIDTH), cur),
                  pl.BlockSpec((BLOCK, ATTN_WIDTH), cur),
                  kvs(prev), kvs(cur), kvs(first), kvs(prev), kvs(cur), kvs(first),
                  tab(cur), tab(kv_out), tab(first)],
        out_specs=[pl.BlockSpec((BLOCK, ATTN_WIDTH), cur), kvs(kv_out), kvs(kv_out), kvs(first), kvs(first),
                   pl.BlockSpec((N_Q_HEADS, LANES), first)],
        scratch_shapes=[pltpu.VMEM((BLOCK, ATTN_WIDTH), F32), pltpu.VMEM((3 * BLOCK, KV_WIDTH), F32),
                        pltpu.VMEM((3 * BLOCK, KV_WIDTH), F32), pltpu.VMEM((BLOCK, KV_WIDTH), F32),
                        pltpu.VMEM((BLOCK, KV_WIDTH), F32), pltpu.VMEM((BLOCK, KV_WIDTH), F32),
                        pltpu.VMEM((BLOCK, KV_WIDTH), F32)],
        sem=("arbitrary",), name="attn_bwd", args=(sinks, q, do, k, k, k, v, v, v, tabs, tabs, tabs))


def _cmul(ar, ai, br, bi):
    return ar * br - ai * bi, ar * bi + ai * br


def _cpow(lr, li, n):
    rr = ri = None
    br, bi = lr, li
    while n:
        if n & 1:
            rr, ri = (br, bi) if rr is None else _cmul(rr, ri, br, bi)
        n >>= 1
        if n:
            br, bi = _cmul(br, bi, br, bi)
    return rr, ri


def _shift_rows(x, d, reverse):
    rows = lax.broadcasted_iota(jnp.int32, x.shape, 0)
    if not reverse:
        return jnp.where(rows >= d, pltpu.roll(x, d, 0), 0.0)
    return jnp.where(rows < SUBLANES - d, pltpu.roll(x, SUBLANES - d, 0), 0.0)


def _sublane_powers(mr, mi, reverse):
    rows = lax.broadcasted_iota(jnp.int32, mr.shape, 0)
    e = SUBLANES - 1 - rows if reverse else rows
    pr, pi = jnp.ones_like(mr), jnp.zeros_like(mr)
    br, bi = mr, mi
    for d in (1, 2, 4):
        tr, ti = _cmul(pr, pi, br, bi)
        on = (e & d) != 0
        pr, pi = jnp.where(on, tr, pr), jnp.where(on, ti, pi)
        if d < 4:
            br, bi = _cmul(br, bi, br, bi)
    return pr, pi


def _inclusive_prefix(er, ei, mr, mi, reverse):
    ir, ii, pr, pi = er, ei, mr, mi
    for d in (1, 2, 4):
        tr, ti = _cmul(pr, pi, _shift_rows(ir, d, reverse), _shift_rows(ii, d, reverse))
        ir, ii = ir + tr, ii + ti
        if d < 4:
            pr, pi = _cmul(pr, pi, pr, pi)
    return ir, ii


def _chain_rows(a, t, seg):
    return pl.ds(a * SUBLANES * seg + t, SUBLANES, stride=seg)


def _seg_scan(xr_ref, xi_ref, lam, seg, nchain, reverse, store, init, extra=None):
    nt = len(lam)
    acc0 = () if extra is None else extra[1]

    def step(i, carry):
        hs, acc = carry
        t = seg - 1 - i if reverse else i
        out = []
        for a in range(nchain):
            sl = _chain_rows(a, t, seg)
            for j in range(nt):
                lr, li = lam[j]
                k = 2 * (a * nt + j)
                hr, hi = hs[k], hs[k + 1]
                nr = lr * hr - li * hi + xr_ref[j, sl, :]
                ni = lr * hi + li * hr + xi_ref[j, sl, :]
                if store:
                    xr_ref[j, sl, :] = nr
                    xi_ref[j, sl, :] = ni
                if extra is not None:
                    acc = extra[0](t, a, j, nr, ni, acc)
                out += [nr, ni]
        return tuple(out), acc

    return lax.fori_loop(0, seg, step, (tuple(init), acc0))


def _ssm_scan(xr_ref, xi_ref, lam, seg, nchain, reverse, extra=None):
    nt = len(lam)
    zero = [jnp.zeros((SUBLANES, LANES), F32)] * (2 * nt * nchain)
    ends, _ = _seg_scan(xr_ref, xi_ref, lam, seg, nchain, reverse, False, zero)
    init = [None] * (2 * nt * nchain)
    last = 0 if reverse else SUBLANES - 1
    for j in range(nt):
        mr, mi = _cpow(lam[j][0], lam[j][1], seg)
        m8r, m8i = _cpow(mr, mi, SUBLANES)
        pwr, pwi = _sublane_powers(mr, mi, reverse)
        gr = gi = jnp.zeros((SUBLANES, LANES), F32)
        for a in (reversed(range(nchain)) if reverse else range(nchain)):
            k = 2 * (a * nt + j)
            incr, inci = _inclusive_prefix(ends[k], ends[k + 1], mr, mi, reverse)
            tr, ti = _cmul(pwr, pwi, gr, gi)
            init[k] = _shift_rows(incr, 1, reverse) + tr
            init[k + 1] = _shift_rows(inci, 1, reverse) + ti
            g2r, g2i = _cmul(m8r, m8i, gr, gi)
            gr = g2r + jnp.broadcast_to(incr[last:last + 1, :], gr.shape)
            gi = g2i + jnp.broadcast_to(inci[last:last + 1, :], gi.shape)
    _, acc = _seg_scan(xr_ref, xi_ref, lam, seg, nchain, reverse, True, init, extra)
    return acc


def _diag_mask():
    steps = LANES // SSM_GROUP // GROUPS_PER_TILE
    return (jnp.eye(steps, dtype=F32)[:, None, :, None] * jnp.eye(GROUPS_PER_TILE, dtype=F32)[None, :, None, :])


def _ssm_tables(bb_re, bb_im, c_re, c_im):
    g = bb_re.shape[0]
    nt = g // GROUPS_PER_TILE
    steps = LANES // SSM_GROUP // GROUPS_PER_TILE
    mask = _diag_mask()

    def b_tab(bb):
        x = bb.reshape(nt // steps, steps, GROUPS_PER_TILE, SSM_STATE, SSM_GROUP)
        x = jnp.transpose(x, (0, 1, 4, 2, 3))[:, :, None, None]
        m = jnp.transpose(mask, (0, 2, 3, 1))[None, :, :, :, None, :, None]
        return (x * m).reshape(nt, LANES, TILE_STATES)

    def c_tab(c):
        x = c.reshape(nt // steps, steps, GROUPS_PER_TILE, SSM_GROUP, SSM_STATE)
        x = jnp.transpose(x, (0, 1, 2, 4, 3))[:, :, :, :, None, None]
        m = mask[None, :, :, None, :, :, None]
        return (x * m).reshape(nt, TILE_STATES, LANES)

    return b_tab(bb_re), b_tab(bb_im), c_tab(c_re), c_tab(c_im)


def _ssm_untable_b(db, g):
    nt = g // GROUPS_PER_TILE
    steps = LANES // SSM_GROUP // GROUPS_PER_TILE
    x = db.reshape(nt // steps, steps, GROUPS_PER_TILE, SSM_STATE, steps, GROUPS_PER_TILE, SSM_GROUP)
    m = _diag_mask()[None, :, :, None, :, :, None]
    return jnp.sum(x * m, axis=(4, 5)).reshape(g, SSM_STATE, SSM_GROUP)


def _ssm_untable_c(dc, g):
    nt = g // GROUPS_PER_TILE
    steps = LANES // SSM_GROUP // GROUPS_PER_TILE
    x = dc.reshape(nt // steps, steps, steps, GROUPS_PER_TILE, SSM_GROUP, GROUPS_PER_TILE, SSM_STATE)
    m = jnp.transpose(_diag_mask(), (0, 2, 3, 1))[None, :, :, :, None, :, None]
    out = jnp.sum(x * m, axis=(2, 3))
    return jnp.transpose(out, (0, 1, 3, 2, 4)).reshape(g, SSM_GROUP, SSM_STATE)


def _lam_tiles(lam_ref):
    out = []
    for j in range(TILE_STATES // LANES):
        out.append(jnp.broadcast_to(lam_ref[:, j * LANES:(j + 1) * LANES], (SUBLANES, LANES)))
    return out


def _scan_chains(lp):
    for n in (4, 2, 1):
        if lp % (SUBLANES * n) == 0 and (lp // SUBLANES) % 16 == 0:
            return n
    raise ValueError(lp)


def _split_tiles(dst_ref, rows, val):
    for j in range(val.shape[1] // LANES):
        dst_ref[j, rows, :] = val[:, j * LANES:(j + 1) * LANES]


def _cat_tiles(src_ref, rows):
    njt = src_ref.shape[0]
    return jnp.concatenate([src_ref[j, rows, :] for j in range(njt)], axis=1).astype(BF16)


def _ssm_fwd(u, lam_re, lam_im, tb_re, tb_im, tc_re, tc_im, d_skip, comm=None):
    lp, w = u.shape
    nt = tb_re.shape[0]
    nchain = _scan_chains(lp)
    seg = lp // (SUBLANES * nchain)
    chunk = lp // SUBLANES
    njt = TILE_STATES // LANES

    def body(u_ref, lr_ref, li_ref, br_ref, bi_ref, cr_ref, ci_ref, d_ref, y_ref, yg_ref, xr, xi):
        t = pl.program_id(0)
        for s in range(SUBLANES):
            rs = pl.ds(s * chunk, chunk)
            ub = u_ref[rs, :].astype(BF16)
            _split_tiles(xr, rs, _dot(ub, br_ref[...], "nn"))
            _split_tiles(xi, rs, _dot(ub, bi_ref[...], "nn"))
        lrs, lis = _lam_tiles(lr_ref), _lam_tiles(li_ref)
        _ssm_scan(xr, xi, list(zip(lrs, lis)), seg, nchain, False)
        for s in range(SUBLANES):
            rs = pl.ds(s * chunk, chunk)
            y = _dot(_cat_tiles(xr, rs), cr_ref[...], "nn") - _dot(_cat_tiles(xi, rs), ci_ref[...], "nn")

            @pl.when(t % 2 == 0)
            def _():
                y_ref[rs, :] = y + d_ref[...] * u_ref[rs, :]

            @pl.when(t % 2 == 1)
            def _():
                total = y_ref[rs, :] + y
                y_ref[rs, :] = total
                yg_ref[rs, :] = _gelu(total).astype(BF16)

    blk = pl.BlockSpec((lp, LANES), lambda t: (0, t // 2))
    lam_spec = pl.BlockSpec((None, 1, TILE_STATES), lambda t: (t, 0, 0))
    b_spec = pl.BlockSpec((None, LANES, TILE_STATES), lambda t: (t, 0, 0))
    c_spec = pl.BlockSpec((None, TILE_STATES, LANES), lambda t: (t, 0, 0))
    (y, yg), couts = _hosted_call(
        body, comm, out_shape=[jax.ShapeDtypeStruct((lp, w), F32), jax.ShapeDtypeStruct((lp, w), BF16)], grid=(nt,),
        in_specs=[blk, lam_spec, lam_spec, b_spec, b_spec, c_spec, c_spec,
                  pl.BlockSpec((1, LANES), lambda t: (0, t // 2))],
        out_specs=[blk, blk],
        scratch_shapes=[pltpu.VMEM((njt, lp, LANES), F32), pltpu.VMEM((njt, lp, LANES), F32)],
        sem=("arbitrary",), name="ssm_fwd",
        args=(u, lam_re, lam_im, tb_re, tb_im, tc_re, tc_im, d_skip.reshape(1, w)))
    return (y, yg), couts


def _ssm_bwd(u, dy, lam_re, lam_im, tb_re, tb_im, tc_re, tc_im, d_skip, comm=None):
    lp, w = u.shape
    nt = tb_re.shape[0]
    nchain = _scan_chains(lp)
    seg = lp // (SUBLANES * nchain)
    chunk = lp // SUBLANES
    njt = TILE_STATES // LANES
    tbt_re, tbt_im = jnp.swapaxes(tb_re, 1, 2), jnp.swapaxes(tb_im, 1, 2)
    tct_re, tct_im = jnp.swapaxes(tc_re, 1, 2), jnp.swapaxes(tc_im, 1, 2)

    def body(u_ref, dy_ref, lr_ref, li_ref, br_ref, bi_ref, btr_ref, bti_ref, ctr_ref, cti_ref, d_ref,
             du_ref, dlr_ref, dli_ref, dbr_ref, dbi_ref, dcr_ref, dci_ref, dd_ref, hr, hi, ar, ai):
        t = pl.program_id(0)
        lrs, lis = _lam_tiles(lr_ref), _lam_tiles(li_ref)
        for s in range(SUBLANES):
            rs = pl.ds(s * chunk, chunk)
            ub = u_ref[rs, :].astype(BF16)
            dyb = dy_ref[rs, :].astype(BF16)
            _split_tiles(hr, rs, _dot(ub, br_ref[...], "nn"))
            _split_tiles(hi, rs, _dot(ub, bi_ref[...], "nn"))
            _split_tiles(ar, rs, _dot(dyb, ctr_ref[...], "nn"))
            _split_tiles(ai, rs, -_dot(dyb, cti_ref[...], "nn"))
        _ssm_scan(hr, hi, list(zip(lrs, lis)), seg, nchain, False)

        def dlam_step(tt, a, j, a_r, a_i, acc):
            sl = _chain_rows(a, jnp.maximum(tt - 1, 0), seg)
            p_r, p_i = hr[j, sl, :], hi[j, sl, :]
            acc = list(acc)
            acc[2 * j] = acc[2 * j] + jnp.where(tt > 0, a_r * p_r + a_i * p_i, 0.0)
            acc[2 * j + 1] = acc[2 * j + 1] + jnp.where(tt > 0, a_i * p_r - a_r * p_i, 0.0)
            return tuple(acc)

        zero = tuple([jnp.zeros((SUBLANES, LANES), F32)] * (2 * njt))
        conj = [(lr, -li) for lr, li in zip(lrs, lis)]
        acc = list(_ssm_scan(ar, ai, conj, seg, nchain, True, (dlam_step, zero)))
        row0 = lax.broadcasted_iota(jnp.int32, (SUBLANES, LANES), 0) == 0
        for j in range(njt):
            cs = slice(j * LANES, (j + 1) * LANES)
            for a in range(nchain):
                p_r = _shift_rows(hr[j, _chain_rows(a, seg - 1, seg), :], 1, False)
                p_i = _shift_rows(hi[j, _chain_rows(a, seg - 1, seg), :], 1, False)
                if a > 0:
                    before = pl.ds(a * SUBLANES * seg - 1, 1)
                    p_r = jnp.where(row0, jnp.broadcast_to(hr[j, before, :], p_r.shape), p_r)
                    p_i = jnp.where(row0, jnp.broadcast_to(hi[j, before, :], p_i.shape), p_i)
                a_r, a_i = ar[j, _chain_rows(a, 0, seg), :], ai[j, _chain_rows(a, 0, seg), :]
                acc[2 * j] = acc[2 * j] + a_r * p_r + a_i * p_i
                acc[2 * j + 1] = acc[2 * j + 1] + a_i * p_r - a_r * p_i
            dlr_ref[:, cs] = jnp.sum(acc[2 * j], axis=0, keepdims=True)
            dli_ref[:, cs] = jnp.sum(acc[2 * j + 1], axis=0, keepdims=True)

        dd = jnp.zeros((1, LANES), F32)
        for s in range(SUBLANES):
            rs = pl.ds(s * chunk, chunk)
            ub = u_ref[rs, :].astype(BF16)
            dyv = dy_ref[rs, :]
            dyb = dyv.astype(BF16)
            arb, aib = _cat_tiles(ar, rs), _cat_tiles(ai, rs)
            hrb, hib = _cat_tiles(hr, rs), _cat_tiles(hi, rs)
            du = _dot(arb, btr_ref[...], "nn") + _dot(aib, bti_ref[...], "nn")
            upd = [(dbr_ref, _dot(arb, ub, "tn")), (dbi_ref, _dot(aib, ub, "tn")),
                   (dcr_ref, _dot(dyb, hrb, "tn")), (dci_ref, -_dot(dyb, hib, "tn"))]
            for ref, val in upd:
                if s == 0:
                    ref[...] = val
                else:
                    ref[...] += val
            rows = lax.broadcasted_iota(jnp.int32, (chunk, LANES), 0) + s * chunk
            keep = rows >= PAD_FRONT
            dd = dd + jnp.sum(dyv * u_ref[rs, :], axis=0, keepdims=True)

            @pl.when(t % 2 == 0)
            def _():
                du_ref[rs, :] = jnp.where(keep, du + d_ref[...] * dyv, 0.0)

            @pl.when(t % 2 == 1)
            def _():
                du_ref[rs, :] += jnp.where(keep, du, 0.0)

        @pl.when(t % 2 == 0)
        def _():
            dd_ref[...] = dd

    blk = pl.BlockSpec((lp, LANES), lambda t: (0, t // 2))
    vec = pl.BlockSpec((1, LANES), lambda t: (0, t // 2))
    lam_spec = pl.BlockSpec((None, 1, TILE_STATES), lambda t: (t, 0, 0))
    b_spec = pl.BlockSpec((None, LANES, TILE_STATES), lambda t: (t, 0, 0))
    c_spec = pl.BlockSpec((None, TILE_STATES, LANES), lambda t: (t, 0, 0))
    lam_shape = jax.ShapeDtypeStruct((nt, 1, TILE_STATES), F32)
    bt_shape = jax.ShapeDtypeStruct((nt, TILE_STATES, LANES), F32)
    ct_shape = jax.ShapeDtypeStruct((nt, LANES, TILE_STATES), F32)
    st = pltpu.VMEM((njt, lp, LANES), F32)
    return _hosted_call(
        body, comm,
        out_shape=[jax.ShapeDtypeStruct((lp, w), F32), lam_shape, lam_shape, bt_shape, bt_shape, ct_shape, ct_shape,
                   jax.ShapeDtypeStruct((1, w), F32)],
        grid=(nt,),
        in_specs=[blk, blk, lam_spec, lam_spec, b_spec, b_spec, c_spec, c_spec, b_spec, b_spec, vec],
        out_specs=[blk, lam_spec, lam_spec, c_spec, c_spec, b_spec, b_spec, vec],
        scratch_shapes=[st, st, st, st], sem=("arbitrary",), name="ssm_bwd",
        args=(u, dy, lam_re, lam_im, tb_re, tb_im, tbt_re, tbt_im, tct_re, tct_im, d_skip.reshape(1, w)))


def _ssm_params(a_re, a_im, log_dt, b_re, b_im):
    dt = jnp.exp(log_dt)[:, None]
    mag = jnp.exp(a_re * dt)
    lb_re = mag * jnp.cos(a_im * dt)
    lb_im = mag * jnp.sin(a_im * dt)
    den = a_re * a_re + a_im * a_im
    num_re = lb_re - 1.0
    coef_re = (num_re * a_re + lb_im * a_im) / den
    coef_im = (lb_im * a_re - num_re * a_im) / den
    bb_re = coef_re[..., None] * b_re - coef_im[..., None] * b_im
    bb_im = coef_re[..., None] * b_im + coef_im[..., None] * b_re
    return lb_re, lb_im, bb_re, bb_im


def _merge_fwd(o, yg, ga, gs, w3t, comm=None):
    lp, d = ga.shape
    kw = w3t.shape[2]
    tm = _row_tile(lp)

    def epi(accs, in_refs, out_refs):
        attn, vv, gg = accs
        out_refs[0][...] = (_sigmoid(in_refs[5][...]) * attn
                            + _sigmoid(in_refs[6][...]) * (vv * _sigmoid(gg))).astype(BF16)

    wspec = lambda which: pl.BlockSpec((None, d, kw), lambda i: (which, 0, 0))
    rowspec = pl.BlockSpec((tm, d), lambda i: (i, 0))
    aspec = pl.BlockSpec((tm, kw), lambda i: (i, 0))
    res = _matmul(
        "merge_fwd", (lp // tm,), None, [o, yg, w3t, w3t, w3t, ga, gs],
        [aspec, aspec, wspec(0), wspec(1), wspec(2), rowspec, rowspec],
        [(0, 2, "nt", 0), (1, 3, "nt", 1), (1, 4, "nt", 2)], [(tm, d)] * 3, epi,
        [jax.ShapeDtypeStruct((lp, d), BF16)], [rowspec], ("parallel",), comm)
    return (res[0], []) if comm is None else (res[0][0], res[1])


def _out_proj(merged, w_out, h, comm=None):
    lp, d = h.shape
    tm = _row_tile(lp)

    def epi(accs, in_refs, out_refs):
        out_refs[0][...] = in_refs[2][...] + accs[0]

    rowspec = pl.BlockSpec((tm, d), lambda i: (i, 0))
    res = _matmul(
        "mix_out_proj", (lp // tm,), None, [merged, w_out, h],
        [rowspec, pl.BlockSpec((d, d), lambda i: (0, 0)), rowspec],
        [(0, 1, "nn", 0)], [(tm, d)], epi, [jax.ShapeDtypeStruct((lp, d), F32)], [rowspec], ("parallel",), comm)
    return (res[0], []) if comm is None else (res[0][0], res[1])


def _merge_bwd(dhb, w_out, o, yg, ga, gs, w3t):
    lp, d = ga.shape
    kw = w3t.shape[2]
    tm = _row_tile(lp)

    def epi(accs, in_refs, out_refs):
        dm, attn, vv, gg = accs
        sa = _sigmoid(in_refs[7][...])
        ss = _sigmoid(in_refs[8][...])
        sg = _sigmoid(gg)
        ssm = vv * sg
        dssm = dm * ss
        out_refs[0][...] = (dm * sa).astype(BF16)
        out_refs[1][...] = (dssm * sg).astype(BF16)
        out_refs[2][...] = (dssm * vv * sg * (1.0 - sg)).astype(BF16)
        out_refs[3][...] = (dm * attn * sa * (1.0 - sa)).astype(BF16)
        out_refs[4][...] = (dm * ssm * ss * (1.0 - ss)).astype(BF16)

    wspec = lambda which: pl.BlockSpec((None, d, kw), lambda i: (which, 0, 0))
    rowspec = pl.BlockSpec((tm, d), lambda i: (i, 0))
    aspec = pl.BlockSpec((tm, kw), lambda i: (i, 0))
    shp = jax.ShapeDtypeStruct((lp, d), BF16)
    return _matmul(
        "merge_bwd", (lp // tm,), None, [dhb, w_out, o, yg, w3t, w3t, w3t, ga, gs],
        [rowspec, pl.BlockSpec((d, d), lambda i: (0, 0)), aspec, aspec, wspec(0), wspec(1), wspec(2), rowspec,
         rowspec],
        [(0, 1, "nt", 0), (2, 4, "nt", 1), (3, 5, "nt", 2), (3, 6, "nt", 3)], [(tm, d)] * 4, epi,
        [shp] * 5, [rowspec] * 5, ("parallel",))


def _branch_bwd(dattn, dv, dg, w3t, y):
    lp, d = dattn.shape
    kw = w3t.shape[2]
    tm = _row_tile(lp)

    def epi(accs, in_refs, out_refs):
        out_refs[0][...] = accs[0].astype(BF16)
        out_refs[1][...] = accs[1] * _gelu_grad(in_refs[6][...])

    wspec = lambda which: pl.BlockSpec((None, d, kw), lambda i: (which, 0, 0))
    rowspec = pl.BlockSpec((tm, d), lambda i: (i, 0))
    aspec = pl.BlockSpec((tm, kw), lambda i: (i, 0))
    return _matmul(
        "branch_bwd", (lp // tm,), None, [dattn, dv, dg, w3t, w3t, w3t, y],
        [rowspec, rowspec, rowspec, wspec(0), wspec(1), wspec(2), aspec],
        [(0, 3, "nn", 0), (1, 4, "nn", 1), (2, 5, "nn", 1)], [(tm, kw)] * 2, epi,
        [jax.ShapeDtypeStruct((lp, kw), BF16), jax.ShapeDtypeStruct((lp, kw), F32)], [aspec, aspec],
        ("parallel",))


def _tn_cols(x, ys, name):
    lp, kx = x.shape
    n = ys[0].shape[1]
    n4 = n // N_CHIPS
    tk = _tn_tiles(lp)

    def epi(accs, in_refs, out_refs):
        for acc, o in zip(accs, out_refs):
            o[...] = acc

    shp = jax.ShapeDtypeStruct((N_CHIPS, kx, n4), F32)
    return _matmul(
        name, (N_CHIPS, lp // tk), 1, [x] + list(ys),
        [pl.BlockSpec((tk, kx), lambda j, k: (k, 0))] + [pl.BlockSpec((tk, n4), lambda j, k: (k, j))] * len(ys),
        [(0, 1 + i, "tn", i) for i in range(len(ys))], [(kx, n4)] * len(ys), epi,
        [shp] * len(ys), [pl.BlockSpec((None, kx, n4), lambda j, k: (j, 0, 0))] * len(ys),
        ("parallel", "arbitrary"))


def _tn_full(x, y, name, tn_cols=None):
    lp, kx = x.shape
    n = y.shape[1]
    tk = _tn_tiles(lp)
    tn = n if tn_cols is None else tn_cols

    def epi(accs, in_refs, out_refs):
        out_refs[0][...] = accs[0]

    (out,) = _matmul(
        name, (n // tn, lp // tk), 1, [x, y],
        [pl.BlockSpec((tk, kx), lambda j, k: (k, 0)), pl.BlockSpec((tk, tn), lambda j, k: (k, j))],
        [(0, 1, "tn", 0)], [(kx, tn)], epi,
        [jax.ShapeDtypeStruct((kx, n), F32)], [pl.BlockSpec((kx, tn), lambda j, k: (0, j))],
        ("parallel", "arbitrary"))
    return out


def _in_proj_bwd(dz, w_in, dh, h_in, gain):
    lp, d = h_in.shape
    inw = w_in.shape[1]
    tm = _row_tile(lp)

    def epi(accs, in_refs, out_refs):
        i = pl.program_id(0)
        dx, dgrow = _rms_bwd_math(accs[0], in_refs[3][...], in_refs[4][...])
        out_refs[0][...] = in_refs[2][...] + dx

        @pl.when(i == 0)
        def _():
            out_refs[1][...] = jnp.zeros_like(out_refs[1])

        out_refs[1][...] += jnp.sum(dgrow, axis=0, keepdims=True)

    row = pl.BlockSpec((tm, d), lambda i: (i, 0))
    return _matmul(
        "mix_in_proj_bwd", (lp // tm,), None, [dz, w_in, dh, h_in, gain.reshape(1, d)],
        [pl.BlockSpec((tm, inw), lambda i: (i, 0)), pl.BlockSpec((d, inw), lambda i: (0, 0)), row, row,
         pl.BlockSpec((1, d), lambda i: (0, 0))],
        [(0, 1, "nt", 0)], [(tm, d)], epi,
        [jax.ShapeDtypeStruct((lp, d), F32), jax.ShapeDtypeStruct((1, d), F32)],
        [row, pl.BlockSpec((1, d), lambda i: (0, 0))], ("arbitrary",))


def _loss_head(h, gain, target):
    lp, d = h.shape
    nb = lp // BLOCK

    def body(h_ref, g_ref, t_ref, dh_ref, dg_ref, loss_ref):
        i = pl.program_id(0)

        @pl.when(i == 0)
        def _():
            dg_ref[...] = jnp.zeros_like(dg_ref)
            loss_ref[...] = jnp.zeros_like(loss_ref)
            dh_ref[...] = jnp.zeros_like(dh_ref)

        @pl.when(i > 0)
        def _():
            x = h_ref[...]
            g = g_ref[...]
            r = lax.rsqrt(jnp.mean(x * x, axis=-1, keepdims=True) + EPS)
            err = x * r * g - t_ref[...]
            loss_ref[...] += jnp.zeros_like(loss_ref) + 0.5 * jnp.sum(jnp.sum(err * err, axis=-1, keepdims=True)) / d
            dx, dgrow = _rms_bwd_math(err * (1.0 / d), x, g)
            dh_ref[...] = dx
            dg_ref[...] += jnp.sum(dgrow, axis=0, keepdims=True)

    row = pl.BlockSpec((BLOCK, d), lambda i: (i, 0))
    one = pl.BlockSpec((1, d), lambda i: (0, 0))
    return pl.pallas_call(
        body,
        out_shape=[jax.ShapeDtypeStruct((lp, d), F32), jax.ShapeDtypeStruct((1, d), F32),
                   jax.ShapeDtypeStruct((SUBLANES, LANES), F32)],
        grid=(nb,),
        in_specs=[row, one, pl.BlockSpec((BLOCK, d), lambda i: (jnp.maximum(i - 1, 0), 0))],
        out_specs=[row, one, pl.BlockSpec((SUBLANES, LANES), lambda i: (0, 0))],
        compiler_params=_cparams(("arbitrary",)), name="loss_head")(h, gain.reshape(1, d), target)


def _adam_math(w, g, m, v):
    m = ADAM_B1 * m + (1.0 - ADAM_B1) * g
    v = ADAM_B2 * v + (1.0 - ADAM_B2) * (g * g)
    m_hat = m / (1.0 - ADAM_B1 ** ADAM_STEP)
    v_hat = v / (1.0 - ADAM_B2 ** ADAM_STEP)
    delta = -ADAM_LR * (m_hat / (jnp.sqrt(v_hat) + ADAM_EPS) + ADAM_WD * w)
    return delta, m, v


def _adamw_layers(w, m, v, mine, other, pos, name):
    depth, r, c = w.shape
    half = r // 2
    tr = _div_tile(half, c * 4)
    nh = half // tr

    def body(*refs):
        pos_ref, w_ref, m_ref, v_ref = refs[:4]
        mine_refs = refs[4:4 + depth]
        other_refs = refs[4 + depth:4 + 2 * depth]
        g_out, d_out, m_out, v_out = refs[4 + 2 * depth:]
        layer, i = pl.program_id(0), pl.program_id(1)
        is_mine = (i // nh) == pos_ref[0]

        def update(g):
            delta, nm, nv = _adam_math(w_ref[...], g, m_ref[...], v_ref[...])
            g_out[...] = g
            d_out[...] = delta
            m_out[...] = nm
            v_out[...] = nv

        for l in range(depth):
            @pl.when((layer == l) & is_mine)
            def _(l=l):
                update(mine_refs[l][...])

            @pl.when((layer == l) & jnp.logical_not(is_mine))
            def _(l=l):
                update(other_refs[l][...])

    stacked = pl.BlockSpec((None, tr, c), lambda l, i, p: (l, i, 0))

    def gspec(layer, is_other):
        def imap(l, i, p):
            first = jnp.where(is_other, 1 - p[0], p[0]) * nh
            here = jnp.clip(i - first, 0, nh - 1)
            return (jnp.where(l == layer, here, jnp.where(l < layer, 0, nh - 1)), 0)
        return pl.BlockSpec((tr, c), imap)

    shp = jax.ShapeDtypeStruct((depth, r, c), F32)
    grid_spec = pltpu.PrefetchScalarGridSpec(
        num_scalar_prefetch=1, grid=(depth, 2 * nh),
        in_specs=[stacked] * 3 + [gspec(l, 0) for l in range(depth)] + [gspec(l, 1) for l in range(depth)],
        out_specs=[stacked] * 4)
    return pl.pallas_call(
        body, out_shape=[shp] * 4, grid_spec=grid_spec,
        compiler_params=_cparams(("arbitrary", "arbitrary")), name=name)(pos, w, m, v, *mine, *other)


def _adamw_flat(w, g, m, v, name):
    r, c = w.shape
    tr = _div_tile(r, c * 4)

    def body(w_ref, g_ref, m_ref, v_ref, d_out, m_out, v_out):
        delta, nm, nv = _adam_math(w_ref[...], g_ref[...], m_ref[...], v_ref[...])
        d_out[...] = delta
        m_out[...] = nm
        v_out[...] = nv

    spec = pl.BlockSpec((tr, c), lambda i: (i, 0))
    shp = jax.ShapeDtypeStruct((r, c), F32)
    return pl.pallas_call(
        body, out_shape=[shp] * 3, grid=(r // tr,), in_specs=[spec] * 4, out_specs=[spec] * 3,
        compiler_params=_cparams(("parallel",)), name=name)(w, g, m, v)


def _mesh_pos():
    return lax.axis_index("x"), lax.axis_index("y"), lax.axis_index("c")


def _row_half(ref, which, lead):
    half = ref.shape[lead] // 2
    idx = (slice(None),) * lead + (pl.ds(which * half, half), slice(None))
    return ref.at[idx]


def _gather_comm(arrs, tag):
    n = len(arrs)

    def ctx(ins, outs, sems):
        send_sems, recv_sems, local_sems = sems
        x, y, c = _mesh_pos()
        chips = [(1 - x, y), (x, 1 - y), (1 - x, 1 - y)]

        def slot(k, chip, which):
            lead = len(ins[k].shape) - 2
            return _row_half(outs[k].at[2 * chip[0] + chip[1]], which, lead)

        def copy(k, j, src, dst, to):
            return pltpu.make_async_remote_copy(
                src_ref=src, dst_ref=dst, send_sem=send_sems.at[6 * k + j], recv_sem=recv_sems.at[6 * k + j],
                device_id=to, device_id_type=MESH)

        def local(k):
            return pltpu.make_async_copy(ins[k], outs[k].at[2 * x + y], local_sems.at[k])

        def first(k, j):
            lead = len(ins[k].shape) - 2
            return copy(k, j, _row_half(ins[k], c, lead), slot(k, (x, y), c), (*chips[j], c))

        def passed(k, j, which):
            return copy(k, 3 + j, slot(k, chips[j], which), slot(k, chips[j], which), (x, y, 1 - c))

        def landed(k, j):
            return copy(k, j, slot(k, chips[j], c), slot(k, chips[j], c), (x, y, 1 - c))

        return c, local, first, passed, landed

    def start(ins, outs, sems):
        c, local, first, passed, landed = ctx(ins, outs, sems)
        for k in range(n):
            local(k).start()
            for j in range(3):
                first(k, j).start()

    def mid(ins, outs, sems):
        c, local, first, passed, landed = ctx(ins, outs, sems)
        for j in range(3):
            for k in range(n):
                landed(k, j).wait_recv()
                passed(k, j, c).start()

    def finish(ins, outs, sems):
        c, local, first, passed, landed = ctx(ins, outs, sems)
        for j in range(3):
            for k in range(n):
                passed(k, j, 1 - c).wait_recv()
        for k in range(n):
            for j in range(3):
                first(k, j).wait_send()
                passed(k, j, c).wait_send()
            local(k).wait()

    return _Comm(
        tag, arrs, [jax.ShapeDtypeStruct((N_CHIPS,) + a.shape, a.dtype) for a in arrs],
        [pltpu.SemaphoreType.DMA((6 * n,)), pltpu.SemaphoreType.DMA((6 * n,)), pltpu.SemaphoreType.DMA((n,))],
        start, mid, finish)


def _run_comm(comm, name):
    n_in, n_out = len(comm.ins), len(comm.out_shapes)

    def body(*refs):
        ins, outs, sems = refs[:n_in], refs[n_in:n_in + n_out], refs[n_in + n_out:]
        comm.start(ins, outs, sems)
        if comm.mid is not None:
            comm.mid(ins, outs, sems)
        comm.finish(ins, outs, sems)

    return pl.pallas_call(
        body, out_shape=comm.out_shapes, in_specs=[HBM_SPEC] * n_in, out_specs=[HBM_SPEC] * n_out,
        scratch_shapes=comm.sems, name=name)(*comm.ins)


def _all_gather_chips(arrs, name):
    return _run_comm(_gather_comm(arrs, "gather"), name)


GATHER_US_PER_BYTE = 380.0 / 11.65e6
HOST_US = dict(ffn_up=78.0, ffn_down=65.0, in_proj=38.0, attn_fwd=103.0, ssm_fwd=67.0, merge_fwd=50.0,
               out_proj=45.0)
HOST_SLACK_US = 10.0


class _WeightStream:
    def __init__(self, pieces):
        self.keys = [k for k, _ in pieces]
        self.shards = dict(pieces)
        self.next = 0
        self.full = {}
        self.pending = []

    def comm_for(self, host):
        budget = HOST_US[host] + HOST_SLACK_US
        taken, cost = [], 0.0
        while self.next < len(self.keys):
            key = self.keys[self.next]
            c = self.shards[key].size * self.shards[key].dtype.itemsize * GATHER_US_PER_BYTE
            if cost + c > budget:
                break
            taken.append(key)
            cost += c
            self.next += 1
        self.pending = taken
        if not taken:
            return None
        return _gather_comm([self.shards[k] for k in taken], "g_" + "_".join(k[1] for k in taken))

    def deposit(self, gathered):
        for key, arr in zip(self.pending, gathered):
            self.full[key] = arr
        self.pending = []

    def get(self, key):
        if key not in self.full:
            upto = self.keys.index(key) + 1
            keys = self.keys[self.next:upto]
            self.next = upto
            for k, arr in zip(keys, _all_gather_chips([self.shards[k] for k in keys], "gather_now")):
                self.full[k] = arr
        return self.full[key]


def _all_gather_devices(x_shard, name):
    m_per, ncol = x_shard.shape

    def body(x_ref, out_ref, send_sems, recv_sems, local_sem):
        x, y, c = _mesh_pos()
        me, sibling = (x, y, c), (x, y, 1 - c)
        chips = [(1 - x, y), (x, 1 - y), (1 - x, 1 - y)]

        def rows(px, py, pc):
            return out_ref.at[4 * px + 2 * py + pc]

        def copy(k, block, to, src=None):
            return pltpu.make_async_remote_copy(
                src_ref=rows(*block) if src is None else src, dst_ref=rows(*block),
                send_sem=send_sems.at[k], recv_sem=recv_sems.at[k], device_id=to, device_id_type=MESH)

        mine = pltpu.make_async_copy(x_ref, rows(*me), local_sem)
        mine.start()
        first = [copy(0, me, sibling, src=x_ref)]
        first += [copy(1 + j, me, (*chip, c), src=x_ref) for j, chip in enumerate(chips)]
        for cp in first:
            cp.start()
        passed = [copy(4 + j, (*chip, c), sibling) for j, chip in enumerate(chips)]
        for j, chip in enumerate(chips):
            copy(1 + j, (*chip, c), me).wait_recv()
            passed[j].start()
        copy(0, sibling, me).wait_recv()
        for j, chip in enumerate(chips):
            copy(4 + j, (*chip, 1 - c), me).wait_recv()
        for cp in first + passed:
            cp.wait_send()
        mine.wait()

    return pl.pallas_call(
        body, out_shape=jax.ShapeDtypeStruct((8, m_per, ncol), x_shard.dtype),
        in_specs=[pl.BlockSpec(memory_space=pltpu.VMEM)], out_specs=pl.BlockSpec(memory_space=pltpu.VMEM),
        scratch_shapes=[pltpu.SemaphoreType.DMA((7,)), pltpu.SemaphoreType.DMA((7,)), pltpu.SemaphoreType.DMA],
        compiler_params=pltpu.CompilerParams(vmem_limit_bytes=VMEM_LIMIT), name=name)(x_shard)


def _sum_devices(g8, name):
    _, r, c = g8.shape
    tr = _div_tile(r, c * 4 * 8)

    def body(g_ref, o_ref):
        acc = g_ref[0]
        for dev in range(1, 8):
            acc = acc + g_ref[dev]
        o_ref[...] = acc

    return pl.pallas_call(
        body, out_shape=jax.ShapeDtypeStruct((r, c), F32), grid=(r // tr,),
        in_specs=[pl.BlockSpec((8, tr, c), lambda i: (0, i, 0))], out_specs=pl.BlockSpec((tr, c), lambda i: (i, 0)),
        compiler_params=_cparams(("parallel",)), name=name)(g8)


def _exchange_sibling_halves(arrs, name):
    n = len(arrs)

    def body(*refs):
        ins, outs = refs[:n], refs[n:2 * n]
        send_sems, recv_sems = refs[2 * n:]
        x, y, c = _mesh_pos()
        cps = []
        for k in range(n):
            cp = pltpu.make_async_remote_copy(
                src_ref=_row_half(ins[k], 1 - c, 1), dst_ref=outs[k], send_sem=send_sems.at[k],
                recv_sem=recv_sems.at[k], device_id=(x, y, 1 - c), device_id_type=MESH)
            cp.start()
            cps.append(cp)
        for cp in cps:
            cp.wait()

    return pl.pallas_call(
        body,
        out_shape=[jax.ShapeDtypeStruct((a.shape[0], a.shape[1] // 2, a.shape[2]), a.dtype) for a in arrs],
        in_specs=[HBM_SPEC] * n, out_specs=[HBM_SPEC] * n,
        scratch_shapes=[pltpu.SemaphoreType.DMA((n,)), pltpu.SemaphoreType.DMA((n,))], name=name)(*arrs)


def _chip_partials(arrs, recvs, pos, name):
    n = len(arrs)

    def body(pos_ref, *refs):
        for a_ref, b_ref, o_ref in zip(refs[:n], refs[n:2 * n], refs[2 * n:]):
            o_ref[...] = (a_ref[...] + b_ref[...]).astype(BF16)

    own_specs, recv_specs, shapes = [], [], []
    for arr in arrs:
        nslab, r, c = arr.shape
        own_specs.append(pl.BlockSpec((None, r // 2, c), lambda j, p: (j, p[0], 0)))
        recv_specs.append(pl.BlockSpec((None, r // 2, c), lambda j, p: (j, 0, 0)))
        shapes.append(jax.ShapeDtypeStruct((nslab, r // 2, c), BF16))
    grid_spec = pltpu.PrefetchScalarGridSpec(
        num_scalar_prefetch=1, grid=(N_CHIPS,), in_specs=own_specs + recv_specs, out_specs=recv_specs)
    return pl.pallas_call(
        body, out_shape=shapes, grid_spec=grid_spec,
        compiler_params=_cparams(("parallel",)), name=name)(pos, *arrs, *recvs)


def _chip_exchange_comm(parts, tag):
    n = len(parts)

    def copies(ins, outs, sems):
        send_sems, recv_sems = sems
        x, y, c = _mesh_pos()
        chips = [(1 - x, y), (x, 1 - y), (1 - x, 1 - y)]
        return [pltpu.make_async_remote_copy(
            src_ref=ins[k].at[2 * chip[0] + chip[1]], dst_ref=outs[k].at[j],
            send_sem=send_sems.at[3 * k + j], recv_sem=recv_sems.at[3 * k + j],
            device_id=(*chip, c), device_id_type=MESH) for k in range(n) for j, chip in enumerate(chips)]

    def start(ins, outs, sems):
        for cp in copies(ins, outs, sems):
            cp.start()

    def finish(ins, outs, sems):
        for cp in copies(ins, outs, sems):
            cp.wait()

    return _Comm(
        tag, parts, [jax.ShapeDtypeStruct((3,) + p.shape[1:], p.dtype) for p in parts],
        [pltpu.SemaphoreType.DMA((3 * n,)), pltpu.SemaphoreType.DMA((3 * n,))], start, None, finish)


def _reduce_halves(arrs, recvs, gots, pos, name):
    n = len(arrs)

    def body(pos_ref, *refs):
        for a_ref, b_ref, g_ref, o_ref in zip(refs[:n], refs[n:2 * n], refs[2 * n:3 * n], refs[3 * n:]):
            acc = a_ref[...] + b_ref[...]
            for j in range(3):
                acc = acc + g_ref[j].astype(F32)
            o_ref[...] = acc

    own_specs, recv_specs, got_specs, out_specs, shapes = [], [], [], [], []
    for arr in arrs:
        _, r, c = arr.shape
        own_specs.append(pl.BlockSpec((None, r // 2, c), lambda i, p: (p[1], p[0], 0)))
        recv_specs.append(pl.BlockSpec((None, r // 2, c), lambda i, p: (p[1], 0, 0)))
        got_specs.append(pl.BlockSpec((3, r // 2, c), lambda i, p: (0, 0, 0)))
        out_specs.append(pl.BlockSpec((r // 2, c), lambda i, p: (0, 0)))
        shapes.append(jax.ShapeDtypeStruct((r // 2, c), F32))
    grid_spec = pltpu.PrefetchScalarGridSpec(
        num_scalar_prefetch=1, grid=(1,), in_specs=own_specs + recv_specs + got_specs, out_specs=out_specs)
    return pl.pallas_call(
        body, out_shape=shapes, grid_spec=grid_spec,
        compiler_params=_cparams(("arbitrary",)), name=name)(pos, *arrs, *recvs, *gots)


def _share_halves(halves, name):
    n = len(halves)

    def body(*refs):
        ins, outs = refs[:n], refs[n:2 * n]
        send_sems, recv_sems = refs[2 * n:]
        x, y, c = _mesh_pos()
        cps = []
        for k in range(n):
            cp = pltpu.make_async_remote_copy(
                src_ref=ins[k], dst_ref=outs[k], send_sem=send_sems.at[k], recv_sem=recv_sems.at[k],
                device_id=(x, y, 1 - c), device_id_type=MESH)
            cp.start()
            cps.append(cp)
        for cp in cps:
            cp.wait()

    return pl.pallas_call(
        body, out_shape=[jax.ShapeDtypeStruct(h.shape, h.dtype) for h in halves],
        in_specs=[HBM_SPEC] * n, out_specs=[HBM_SPEC] * n,
        scratch_shapes=[pltpu.SemaphoreType.DMA((n,)), pltpu.SemaphoreType.DMA((n,))], name=name)(*halves)


class _Reduction:
    def __init__(self, arrs, pos, tag):
        self.arrs, self.pos, self.tag = arrs, pos, tag
        self.recv = _exchange_sibling_halves(arrs, "rs_sibling_" + tag)
        self.parts = _chip_partials(arrs, self.recv, pos, "rs_partial_" + tag)
        self.got = None

    def comm(self):
        return _chip_exchange_comm(self.parts, "rs_" + self.tag)

    def end(self):
        if self.got is None:
            self.got = _run_comm(self.comm(), "rs_chips_" + self.tag)
        halves = _reduce_halves(self.arrs, self.recv, self.got, self.pos, "rs_reduce_" + self.tag)
        return halves, _share_halves(halves, "rs_share_" + self.tag)


def _w_in_full(p, l, ws):
    if "w_in" not in p:
        slabs = ws.get((l, "w_in"))
        p["w_in"] = jnp.concatenate([slabs[j] for j in range(N_CHIPS)], axis=1)
    return p["w_in"]


def _w3t_full(p, l, ws):
    if "w3t" not in p:
        slabs = ws.get((l, "w3"))
        p["w3t"] = jnp.swapaxes(slabs, 0, 1).reshape(slabs.shape[1], -1, slabs.shape[3])
    return p["w3t"]


def _w_out_full(l, ws):
    slabs = ws.get((l, "w_out"))
    return slabs.reshape(-1, slabs.shape[2])


def _layer_fwd(h, l, p, ws, tabs):
    def hosted(host, fn, *args):
        out, got = fn(*args, ws.comm_for(host))
        ws.deposit(got)
        return out

    ffn1_saved = hosted("ffn_up", _ffn_up, h, p["ffn1_norm"], ws.get((l, "wg1")), ws.get((l, "wu1")))
    h1 = hosted("ffn_down", _ffn_down, ffn1_saved[2], ws.get((l, "wd1")), h)
    n = _rms_fwd(h1, p["mix_norm"], "rms_fwd_mix")
    ssm_w = p["ssm_d"].shape[0]
    q, k, v, u, ga, gs = hosted("in_proj", _in_proj, n, _w_in_full(p, l, ws), tabs, ssm_w)
    o = hosted("attn_fwd", _attn_fwd, q, k, v, p["attn_sinks"])
    y, yg = hosted("ssm_fwd", _ssm_fwd, u, *p["ssm_tabs"], p["ssm_d"])
    merged = hosted("merge_fwd", _merge_fwd, o, yg, ga, gs, _w3t_full(p, l, ws))
    h2 = hosted("out_proj", _out_proj, merged, _w_out_full(l, ws), h1)
    ffn2_saved = hosted("ffn_up", _ffn_up, h2, p["ffn2_norm"], ws.get((l, "wg2")), ws.get((l, "wu2")))
    h3 = hosted("ffn_down", _ffn_down, ffn2_saved[2], ws.get((l, "wd2")), h2)
    saved = dict(h0=h, h1=h1, h2=h2, ffn1=ffn1_saved, ffn2=ffn2_saved, n_mix=n, q=q, k=k, v=v, u=u, ga=ga, gs=gs,
                 o=o, y=y, yg=yg, merged=merged)
    return h3, saved


def _layer_bwd(dh, l, p, ws, s, tabs, pos):
    g = {}
    dh2, g["ffn2_norm"], red_ffn2, _ = _ffn_bwd(
        dh, s["h2"], p["ffn2_norm"], ws.get((l, "wg2")), ws.get((l, "wu2")), ws.get((l, "wd2")), p["f4"],
        s["ffn2"], pos)
    w3, w_out_w = _w3t_full(p, l, ws), _w_out_full(l, ws)
    lp, d = dh2.shape
    d4 = d // N_CHIPS
    dhb = _scale_cast(dh2, 1.0, "mix_dh_cast")
    dw_out = _tn_full(s["merged"], dhb, "mix_dw_out").reshape(N_CHIPS, d4, d)
    dattn, dv, dg, dga, dgs = _merge_bwd(dhb, w_out_w, s["o"], s["yg"], s["ga"], s["gs"], w3)
    (dw_ap,) = _tn_cols(s["o"], [dattn], "mix_dw_ap")
    dw_gv, dw_gg = _tn_cols(s["yg"], [dv, dg], "mix_dw_glu")
    do, dy = _branch_bwd(dattn, dv, dg, w3, s["y"])
    (dq, dk, dvv, dkm, dvm, dsink), _ = _attn_bwd(s["q"], s["k"], s["v"], do, p["attn_sinks"], tabs)
    g["attn_sinks"] = dsink[:, 0]
    (du, dlr, dli, dbr, dbi, dcr, dci, dd), _ = _ssm_bwd(s["u"], dy, *p["ssm_tabs"], p["ssm_d"])
    ngrp = p["ssm_d"].shape[0] // SSM_GROUP
    g["ssm_lam"] = (dlr.reshape(ngrp, SSM_STATE), dli.reshape(ngrp, SSM_STATE),
                    _ssm_untable_b(dbr, ngrp), _ssm_untable_b(dbi, ngrp))
    g["ssm_c_re"] = _ssm_untable_c(dcr, ngrp)
    g["ssm_c_im"] = _ssm_untable_c(dci, ngrp)
    g["ssm_d"] = dd[0]
    dk = dk.at[:BLOCK].add(dkm)
    dvv = dvv.at[:BLOCK].add(dvm)
    dz = jnp.concatenate([dq.astype(BF16), dk.astype(BF16), dvv.astype(BF16), du.astype(BF16), dga, dgs], axis=1)
    n = s["n_mix"]
    w_in = _w_in_full(p, l, ws)
    inw = w_in.shape[1]
    dw_in = _tn_full(dz, n, "mix_dw_in", d // 2).reshape(N_CHIPS, inw // N_CHIPS, d)
    red_mix = _Reduction([dw_in, dw_ap, dw_gv, dw_gg, dw_out], pos, "mix")
    dh1, g["mix_norm"] = _in_proj_bwd(dz, w_in, dh2, s["h1"], p["mix_norm"])
    dh0, g["ffn1_norm"], red_ffn1, red_mix.got = _ffn_bwd(
        dh1, s["h0"], p["ffn1_norm"], ws.get((l, "wg1")), ws.get((l, "wu1")), ws.get((l, "wd1")), p["f4"],
        s["ffn1"], pos, red_mix.comm())
    return dh0, g, [*red_ffn1, red_mix, *red_ffn2]


BIG = ["ffn1_w_gate", "ffn1_w_up", "ffn1_w_down", "w_in", "w_attn_proj", "w_glu_v", "w_glu_g", "w_out",
       "ffn2_w_gate", "ffn2_w_up", "ffn2_w_down"]
TRANSPOSED = ["ffn1_w_gate", "ffn1_w_up", "w_in", "ffn2_w_gate", "ffn2_w_up"]
SMALL = ["ffn1_norm", "mix_norm", "attn_sinks", "ssm_a_re", "ssm_a_im", "ssm_log_dt", "ssm_b_re", "ssm_b_im",
         "ssm_c_re", "ssm_c_im", "ssm_d", "ffn2_norm", "final_norm"]
WEIGHTS = ["meta_tokens", "ffn1_norm", "ffn1_w_gate", "ffn1_w_up", "ffn1_w_down", "mix_norm", "w_in", "attn_sinks",
           "ssm_a_re", "ssm_a_im", "ssm_log_dt", "ssm_b_re", "ssm_b_im", "ssm_c_re", "ssm_c_im", "ssm_d",
           "w_attn_proj", "w_glu_v", "w_glu_g", "w_out", "ffn2_norm", "ffn2_w_gate", "ffn2_w_up", "ffn2_w_down",
           "final_norm"]


def _small_rows(shape):
    rows = -(-math.prod(shape) // LANES)
    return -(-rows // SUBLANES) * SUBLANES


def _pack_small(tree):
    parts = []
    for k in SMALL + ["meta_tokens"]:
        size, rows = math.prod(tree[k].shape), _small_rows(tree[k].shape)
        if size % LANES == 0:
            part = tree[k].reshape(size // LANES, LANES)
        else:
            part = jnp.pad(tree[k].reshape(1, size), ((0, 0), (0, LANES - size)))
        parts.append(jnp.pad(part, ((0, rows - part.shape[0]), (0, 0))))
    return jnp.concatenate(parts, axis=0)


def _unpack_small(packed, like):
    out, off = {}, 0
    for k in SMALL + ["meta_tokens"]:
        size, rows = math.prod(like[k].shape), _small_rows(like[k].shape)
        if size % LANES == 0:
            out[k] = packed[off:off + size // LANES].reshape(like[k].shape)
        else:
            out[k] = packed[off, :size].reshape(like[k].shape)
        off += rows
    return out


def kernel(x, meta_tokens, ffn1_norm, ffn1_w_gate, ffn1_w_up, ffn1_w_down, mix_norm, w_in, attn_sinks, ssm_a_re, ssm_a_im, ssm_log_dt, ssm_b_re, ssm_b_im, ssm_c_re, ssm_c_im, ssm_d, w_attn_proj, w_glu_v, w_glu_g, w_out, ffn2_norm, ffn2_w_gate, ffn2_w_up, ffn2_w_down, final_norm, loss_target, m_meta_tokens, m_ffn1_norm, m_ffn1_w_gate, m_ffn1_w_up, m_ffn1_w_down, m_mix_norm, m_w_in, m_attn_sinks, m_ssm_a_re, m_ssm_a_im, m_ssm_log_dt, m_ssm_b_re, m_ssm_b_im, m_ssm_c_re, m_ssm_c_im, m_ssm_d, m_w_attn_proj, m_w_glu_v, m_w_glu_g, m_w_out, m_ffn2_norm, m_ffn2_w_gate, m_ffn2_w_up, m_ffn2_w_down, m_final_norm, v_meta_tokens, v_ffn1_norm, v_ffn1_w_gate, v_ffn1_w_up, v_ffn1_w_down, v_mix_norm, v_w_in, v_attn_sinks, v_ssm_a_re, v_ssm_a_im, v_ssm_log_dt, v_ssm_b_re, v_ssm_b_im, v_ssm_c_re, v_ssm_c_im, v_ssm_d, v_w_attn_proj, v_w_glu_v, v_w_glu_g, v_w_out, v_ffn2_norm, v_ffn2_w_gate, v_ffn2_w_up, v_ffn2_w_down, v_final_norm):
    args = dict(locals())
    w = {k: args[k] for k in WEIGHTS}
    m = {k: args["m_" + k] for k in WEIGHTS}
    v = {k: args["v_" + k] for k in WEIGHTS}
    depth = ffn1_norm.shape[0]
    seq, d = x.shape[1], x.shape[2]
    lp = seq + BLOCK
    xi, yi, ci = _mesh_pos()
    pos = jnp.stack([ci, 2 * xi + yi]).astype(jnp.int32)

    tabs = _rope_tables(lp)
    (meta_all,) = _all_gather_chips([meta_tokens], "gather_meta")
    meta_full = jnp.concatenate([meta_all[j] for j in range(N_CHIPS)], axis=1)
    layers, pieces = [], []
    f4 = ffn1_w_gate.shape[2]
    fp = -(-f4 // MXU_DIM) * MXU_DIM

    def ffn_rows(wt):
        return jnp.pad(wt, ((0, fp - f4), (0, 0))).astype(BF16)

    for l in range(depth):
        pieces += [
            ((l, "wg1"), ffn_rows(ffn1_w_gate[l].T)), ((l, "wu1"), ffn_rows(ffn1_w_up[l].T)),
            ((l, "wd1"), ffn_rows(ffn1_w_down[l])), ((l, "w_in"), w_in[l].astype(BF16)),
            ((l, "w3"), jnp.stack([w_attn_proj[l].T, w_glu_v[l].T, w_glu_g[l].T]).astype(BF16)),
            ((l, "w_out"), w_out[l].astype(BF16)),
            ((l, "wg2"), ffn_rows(ffn2_w_gate[l].T)), ((l, "wu2"), ffn_rows(ffn2_w_up[l].T)),
            ((l, "wd2"), ffn_rows(ffn2_w_down[l]))]
        lb_re, lb_im, bb_re, bb_im = _ssm_params(ssm_a_re[l], ssm_a_im[l], ssm_log_dt[l], ssm_b_re[l], ssm_b_im[l])
        ngrp = lb_re.shape[0]
        nt = ngrp // GROUPS_PER_TILE
        ssm_tabs = (lb_re.reshape(nt, 1, TILE_STATES), lb_im.reshape(nt, 1, TILE_STATES),
                    *_ssm_tables(bb_re, bb_im, ssm_c_re[l], ssm_c_im[l]))
        layers.append(dict(
            ffn1_norm=ffn1_norm[l], mix_norm=mix_norm[l], ffn2_norm=ffn2_norm[l], attn_sinks=attn_sinks[l],
            ssm_d=ssm_d[l], ssm_tabs=ssm_tabs, f4=f4))
    ws = _WeightStream(pieces)
    ws.get((0, "wu1"))

    h = jnp.concatenate([jnp.zeros((PAD_FRONT, d), F32), meta_full, x[0]], axis=0)
    saved = []
    for l in range(depth):
        h, s = _layer_fwd(h, l, layers[l], ws, tabs)
        saved.append(s)
    dh, g_final, loss_acc = _loss_head(h, final_norm, loss_target[0])
    loss = lax.psum(loss_acc[0, 0], ("x", "y", "c"))

    grads, reds = [None] * depth, [None] * depth
    for l in reversed(range(depth)):
        dh, grads[l], reds[l] = _layer_bwd(dh, l, layers[l], ws, saved[l], tabs, pos)
    grad_x = dh[BLOCK:][None]
    dmeta_local = dh[PAD_FRONT:BLOCK]

    small = {k: [] for k in SMALL}
    for l in range(depth):
        gl = grads[l]
        _, vjp = jax.vjp(_ssm_params, ssm_a_re[l], ssm_a_im[l], ssm_log_dt[l], ssm_b_re[l], ssm_b_im[l])
        da_re, da_im, dlog_dt, db_re, db_im = vjp(gl["ssm_lam"])
        for k, val in (("ffn1_norm", gl["ffn1_norm"][0]), ("mix_norm", gl["mix_norm"][0]),
                       ("attn_sinks", gl["attn_sinks"]), ("ssm_a_re", da_re), ("ssm_a_im", da_im),
                       ("ssm_log_dt", dlog_dt), ("ssm_b_re", db_re), ("ssm_b_im", db_im),
                       ("ssm_c_re", gl["ssm_c_re"]), ("ssm_c_im", gl["ssm_c_im"]), ("ssm_d", gl["ssm_d"]),
                       ("ffn2_norm", gl["ffn2_norm"][0])):
            small[k].append(val)
    small_local = {k: jnp.stack(vals) for k, vals in small.items() if k != "final_norm"}
    small_local["final_norm"] = g_final[0]
    small_local["meta_tokens"] = dmeta_local
    like = dict(small_local)
    g_small = _sum_devices(_all_gather_devices(_pack_small(small_local), "gather_small_grads"), "sum_small_grads")
    g_small_tree = _unpack_small(g_small, like)
    d4 = d // N_CHIPS
    chip = 2 * xi + yi
    g_meta = lax.dynamic_slice_in_dim(g_small_tree["meta_tokens"], chip * d4, d4, axis=1)

    reduced = []
    for l in range(depth):
        mine, other = [], []
        for red in reds[l]:
            halves, sibling_halves = red.end()
            mine += halves
            other += sibling_halves
        reduced.append((mine, other))

    g_out, delta, new_m, new_v = {}, {}, {}, {}
    for i, k in enumerate(BIG):
        flip = (lambda t: jnp.swapaxes(t, 1, 2)) if k in TRANSPOSED else (lambda t: t)
        outs = _adamw_layers(
            flip(w[k]), flip(m[k]), flip(v[k]), [reduced[l][0][i] for l in range(depth)],
            [reduced[l][1][i] for l in range(depth)], pos, "adamw_" + k)
        g_out[k], delta[k], new_m[k], new_v[k] = [flip(t) for t in outs]
    small_names = SMALL + ["meta_tokens"]
    w_small = {k: w[k] for k in small_names}
    m_small = {k: m[k] for k in small_names}
    v_small = {k: v[k] for k in small_names}
    g_small_local = dict(g_small_tree)
    g_small_local["meta_tokens"] = g_meta
    d_s, m_s, v_s = _adamw_flat(_pack_small(w_small), _pack_small(g_small_local), _pack_small(m_small),
                                _pack_small(v_small), "adamw_small")
    for tree, packed in ((delta, d_s), (new_m, m_s), (new_v, v_s)):
        tree.update(_unpack_small(packed, w_small))
    for k in small_names:
        g_out[k] = g_small_local[k]

    return (loss, grad_x, *[g_out[k] for k in WEIGHTS], *[delta[k] for k in WEIGHTS],
            *[new_m[k] for k in WEIGHTS], *[new_v[k] for k in WEIGHTS])
```

```python
import functools
import math

import jax
import jax.numpy as jnp
from jax import lax
from jax.experimental import pallas as pl
from jax.experimental.pallas import tpu as pltpu

F32 = jnp.float32
BF16 = jnp.bfloat16

N_META = 16
HEAD_DIM = 64
N_Q_HEADS = 8
N_KV_HEADS = 2
Q_PER_KV = N_Q_HEADS // N_KV_HEADS
ATTN_WIDTH = N_Q_HEADS * HEAD_DIM
KV_WIDTH = N_KV_HEADS * HEAD_DIM
BLOCK = 128
PAD_FRONT = BLOCK - N_META
ROPE_THETA = 500000.0
ROT_DIM = HEAD_DIM // 4
SSM_GROUP = 16
SSM_STATE = 64
GROUPS_PER_TILE = 4
TILE_STATES = GROUPS_PER_TILE * SSM_STATE
LANES = 128
SUBLANES = 8
MXU_DIM = 256
EPS = 1e-6
NEG_INF = -1e30
N_CHIPS = 4

ADAM_LR = 0.001
ADAM_B1 = 0.9
ADAM_B2 = 0.999
ADAM_EPS = 1e-08
ADAM_WD = 0.01
ADAM_STEP = 10

VMEM_LIMIT = 56 * 1024 * 1024
MESH = pl.DeviceIdType.MESH


def _cparams(sem=None):
    return pltpu.CompilerParams(dimension_semantics=sem, vmem_limit_bytes=VMEM_LIMIT)


def _row_tile(rows, limit=512):
    best = None
    for t in range(128, limit + 1, 128):
        if rows % t == 0:
            best = t
    assert best is not None, rows
    return best


def _div_tile(rows, row_bytes, max_bytes=1 << 20, mult=8):
    best = None
    for t in range(mult, rows + 1, mult):
        if rows % t == 0 and t * row_bytes <= max_bytes:
            best = t
    if best is None:
        best = rows
    return best


def _dot(a, b, mode):
    if mode == "nn":
        dims = (((1,), (0,)), ((), ()))
    elif mode == "nt":
        dims = (((1,), (1,)), ((), ()))
    else:
        dims = (((0,), (0,)), ((), ()))
    return lax.dot_general(a.astype(BF16), b.astype(BF16), dims, preferred_element_type=F32)


def _sigmoid(x):
    return 1.0 / (1.0 + jnp.exp(-x))


_GELU_C = math.sqrt(2.0 / math.pi)


def _gelu(x):
    return 0.5 * x * (1.0 + jnp.tanh(_GELU_C * (x + 0.044715 * x * x * x)))


def _gelu_grad(x):
    t = jnp.tanh(_GELU_C * (x + 0.044715 * x * x * x))
    return 0.5 * (1.0 + t) + 0.5 * x * (1.0 - t * t) * _GELU_C * (1.0 + 3.0 * 0.044715 * x * x)


class _Comm:
    def __init__(self, tag, ins, out_shapes, sems, start, mid, finish):
        self.tag, self.ins, self.out_shapes, self.sems = tag, list(ins), list(out_shapes), list(sems)
        self.start, self.mid, self.finish = start, mid, finish


HBM_SPEC = pl.BlockSpec(memory_space=pltpu.HBM)


def _hosted_call(body, comm, *, out_shape, grid, in_specs, out_specs, scratch_shapes, sem, name, args):
    out_shape, in_specs, out_specs = list(out_shape), list(in_specs), list(out_specs)
    scratch_shapes = list(scratch_shapes)
    if comm is None:
        res = pl.pallas_call(
            body, out_shape=out_shape, grid=grid, in_specs=in_specs, out_specs=out_specs,
            scratch_shapes=scratch_shapes, compiler_params=_cparams(sem), name=name)(*args)
        return list(res), []
    n_in, n_out, n_sc = len(args), len(out_shape), len(scratch_shapes)
    nci, nco = len(comm.ins), len(comm.out_shapes)
    total = math.prod(grid)

    def wrapped(*refs):
        in_refs, cin = refs[:n_in], refs[n_in:n_in + nci]
        o0 = n_in + nci
        out_refs, cout = refs[o0:o0 + n_out], refs[o0 + n_out:o0 + n_out + nco]
        s0 = o0 + n_out + nco
        sc, csem = refs[s0:s0 + n_sc], refs[s0 + n_sc:]
        lin = 0
        for dim, size in enumerate(grid):
            lin = lin * size + pl.program_id(dim)

        @pl.when(lin == 0)
        def _():
            comm.start(cin, cout, csem)

        if comm.mid is not None:
            @pl.when(lin == total // 2)
            def _():
                comm.mid(cin, cout, csem)

        body(*in_refs, *out_refs, *sc)

        @pl.when(lin == total - 1)
        def _():
            comm.finish(cin, cout, csem)

    res = pl.pallas_call(
        wrapped, out_shape=out_shape + comm.out_shapes, grid=grid,
        in_specs=in_specs + [HBM_SPEC] * nci, out_specs=out_specs + [HBM_SPEC] * nco,
        scratch_shapes=scratch_shapes + comm.sems,
        compiler_params=_cparams(("arbitrary",) * len(grid)), name=name + "_" + comm.tag)(*args, *comm.ins)
    return list(res[:n_out]), list(res[n_out:])


def _matmul(name, grid, k_axis, ins, in_specs, pairs, acc_shapes, epilogue, out_shapes, out_specs, sem, comm=None):
    n_in, n_out, n_acc = len(ins), len(out_shapes), len(acc_shapes)

    def body(*refs):
        in_refs = refs[:n_in]
        out_refs = refs[n_in:n_in + n_out]
        acc_refs = refs[n_in + n_out:]
        if k_axis is None:
            accs = [None] * n_acc
            for ia, ib, mode, iacc in pairs:
                d = _dot(in_refs[ia][...], in_refs[ib][...], mode)
                accs[iacc] = d if accs[iacc] is None else accs[iacc] + d
            epilogue(accs, in_refs, out_refs)
            return
        k = pl.program_id(k_axis)

        @pl.when(k == 0)
        def _():
            for r in acc_refs:
                r[...] = jnp.zeros_like(r)

        for ia, ib, mode, iacc in pairs:
            acc_refs[iacc][...] += _dot(in_refs[ia][...], in_refs[ib][...], mode)

        @pl.when(k == pl.num_programs(k_axis) - 1)
        def _():
            epilogue([r[...] for r in acc_refs], in_refs, out_refs)

    scratch = [] if k_axis is None else [pltpu.VMEM(s, F32) for s in acc_shapes]
    outs, couts = _hosted_call(
        body, comm, out_shape=out_shapes, grid=grid, in_specs=in_specs, out_specs=out_specs,
        scratch_shapes=scratch, sem=sem, name=name, args=ins)
    return outs if comm is None else (outs, couts)


def _rms_fwd(h, g, name):
    lp, d = h.shape
    tm = _row_tile(lp)

    def body(h_ref, g_ref, n_ref):
        x = h_ref[...]
        r = lax.rsqrt(jnp.mean(x * x, axis=-1, keepdims=True) + EPS)
        n_ref[...] = (x * r * g_ref[...]).astype(BF16)

    return pl.pallas_call(
        body, out_shape=jax.ShapeDtypeStruct((lp, d), BF16), grid=(lp // tm,),
        in_specs=[pl.BlockSpec((tm, d), lambda i: (i, 0)), pl.BlockSpec((1, d), lambda i: (0, 0))],
        out_specs=pl.BlockSpec((tm, d), lambda i: (i, 0)),
        compiler_params=_cparams(("parallel",)), name=name)(h, g.reshape(1, d))


def _rms_bwd_math(dn, x, g):
    r = lax.rsqrt(jnp.mean(x * x, axis=-1, keepdims=True) + EPS)
    xh = x * r
    dxh = dn * g
    dx = r * (dxh - xh * jnp.mean(dxh * xh, axis=-1, keepdims=True))
    return dx, dn * xh


def _scale_cast(x, scale, name):
    lp, d = x.shape
    tm = _row_tile(lp)

    def body(x_ref, o_ref):
        o_ref[...] = (x_ref[...] * scale).astype(BF16)

    return pl.pallas_call(
        body, out_shape=jax.ShapeDtypeStruct((lp, d), BF16), grid=(lp // tm,),
        in_specs=[pl.BlockSpec((tm, d), lambda i: (i, 0))], out_specs=pl.BlockSpec((tm, d), lambda i: (i, 0)),
        compiler_params=_cparams(("parallel",)), name=name)(x)


def _ffn_up(h, gain, wgt, wut, comm=None):
    lp, d = h.shape
    fp = wgt.shape[1]
    tm = _row_tile(lp)
    n = _rms_fwd(h, gain, "rms_fwd_ffn")

    def up_epi(accs, in_refs, out_refs):
        a, b = accs
        out_refs[0][...] = a.astype(BF16)
        out_refs[1][...] = b.astype(BF16)
        out_refs[2][...] = (a * _sigmoid(a) * b).astype(BF16)

    act = jax.ShapeDtypeStruct((lp, N_CHIPS * fp), BF16)
    w_spec = pl.BlockSpec((None, fp, d), lambda j, i: (j, 0, 0))
    res = _matmul(
        "ffn_up", (N_CHIPS, lp // tm), None, [n, wgt, wut],
        [pl.BlockSpec((tm, d), lambda j, i: (i, 0)), w_spec, w_spec],
        [(0, 1, "nt", 0), (0, 2, "nt", 1)], [(tm, fp)] * 2, up_epi,
        [act, act, act], [pl.BlockSpec((tm, fp), lambda j, i: (i, j))] * 3,
        ("parallel", "parallel"), comm)
    outs, couts = (res, []) if comm is None else res
    return (*outs, n), couts


def _ffn_down(s, wd, h, comm=None):
    lp, d = h.shape
    ff = s.shape[1]
    tm = _row_tile(lp)

    def down_epi(accs, in_refs, out_refs):
        out_refs[0][...] = in_refs[2][...] + 0.5 * accs[0]

    res = _matmul(
        "ffn_down", (lp // tm,), None, [s, wd.reshape(ff, d), h],
        [pl.BlockSpec((tm, ff), lambda i: (i, 0)), pl.BlockSpec((ff, d), lambda i: (0, 0)),
         pl.BlockSpec((tm, d), lambda i: (i, 0))],
        [(0, 1, "nn", 0)], [(tm, d)], down_epi,
        [jax.ShapeDtypeStruct((lp, d), F32)], [pl.BlockSpec((tm, d), lambda i: (i, 0))],
        ("parallel",), comm)
    return (res[0], []) if comm is None else (res[0][0], res[1])


def _tn_tiles(lp):
    return _row_tile(lp, 1408)


def _ffn_bwd(dh_pair, h_in, gain, wgt, wut, wd, f4, saved, pos, comm=None):
    dh, dhb = dh_pair
    a, b, s, n = saved
    lp, d = h_in.shape
    fp = wgt.shape[1]
    ff = N_CHIPS * fp
    tm = _row_tile(lp)
    ni = lp // tm
    tk = _tn_tiles(lp)
    nk = lp // tk

    def ds_epi(accs, in_refs, out_refs):
        ds = 0.5 * accs[0]
        av = in_refs[2][...].astype(F32)
        bv = in_refs[3][...].astype(F32)
        sg = _sigmoid(av)
        out_refs[0][...] = (ds * bv * sg * (1.0 + av * (1.0 - sg))).astype(BF16)
        out_refs[1][...] = (ds * av * sg).astype(BF16)

    act = jax.ShapeDtypeStruct((lp, ff), BF16)
    col_spec = pl.BlockSpec((tm, fp), lambda j, i: (i, j))
    res = _matmul(
        "ffn_bwd_ds", (N_CHIPS, ni), None, [dhb, wd, a, b],
        [pl.BlockSpec((tm, d), lambda j, i: (i, 0)), pl.BlockSpec((None, fp, d), lambda j, i: (j, 0, 0)),
         col_spec, col_spec],
        [(0, 1, "nt", 0)], [(tm, fp)], ds_epi, [act, act], [col_spec, col_spec], ("parallel", "parallel"),
        comm)
    (da, db), couts = (res, []) if comm is None else res

    dw_shape = jax.ShapeDtypeStruct((N_CHIPS, f4, d), F32)
    dw_spec = pl.BlockSpec((None, f4, d), lambda j, k: (j, 0, 0))
    in_col = pl.BlockSpec((tk, fp), lambda j, k: (k, j))
    in_row = pl.BlockSpec((tk, d), lambda j, k: (k, 0))

    def dwd_epi(accs, in_refs, out_refs):
        out_refs[0][...] = 0.5 * accs[0][:f4]

    (dwd,) = _matmul(
        "ffn_dwd", (N_CHIPS, nk), 1, [s, dhb], [in_col, in_row],
        [(0, 1, "tn", 0)], [(fp, d)], dwd_epi, [dw_shape], [dw_spec], ("parallel", "arbitrary"))

    def dwgu_epi(accs, in_refs, out_refs):
        for acc, o in zip(accs, out_refs):
            o[...] = acc[:f4]

    red_down = _Reduction([dwd], pos, "ffn_d")
    (dwg, dwu), red_down.got = _matmul(
        "ffn_dwgu", (N_CHIPS, nk), 1, [n, da, db], [in_row, in_col, in_col],
        [(1, 0, "tn", 0), (2, 0, "tn", 1)], [(fp, d)] * 2, dwgu_epi,
        [dw_shape, dw_shape], [dw_spec, dw_spec], ("parallel", "arbitrary"), red_down.comm())

    def dn_epi(accs, in_refs, out_refs):
        i = pl.program_id(0)
        dx, dgrow = _rms_bwd_math(accs[0], in_refs[5][...], in_refs[6][...])
        dh_new = in_refs[4][...] + dx
        out_refs[0][...] = dh_new
        out_refs[2][...] = dh_new.astype(BF16)

        @pl.when(i == 0)
        def _():
            out_refs[1][...] = jnp.zeros_like(out_refs[1])

        out_refs[1][...] += jnp.sum(dgrow, axis=0, keepdims=True)

    red = _Reduction([dwg, dwu], pos, "ffn_gu")
    row_spec = pl.BlockSpec((tm, d), lambda i: (i, 0))
    act_spec = pl.BlockSpec((tm, ff), lambda i: (i, 0))
    w_spec = pl.BlockSpec((ff, d), lambda i: (0, 0))
    one_spec = pl.BlockSpec((1, d), lambda i: (0, 0))
    (dh_in, dgain, dh_in_b), red.got = _matmul(
        "ffn_bwd_dn", (ni,), None, [da, wgt.reshape(ff, d), db, wut.reshape(ff, d), dh, h_in, gain.reshape(1, d)],
        [act_spec, w_spec, act_spec, w_spec, row_spec, row_spec, one_spec],
        [(0, 1, "nn", 0), (2, 3, "nn", 0)], [(tm, d)], dn_epi,
        [jax.ShapeDtypeStruct((lp, d), F32), jax.ShapeDtypeStruct((1, d), F32), jax.ShapeDtypeStruct((lp, d), BF16)],
        [row_spec, one_spec, row_spec], ("arbitrary",), red.comm())
    return (dh_in, dh_in_b), dgain, [red, red_down], couts


def _rope_tables(lp):
    pos = jnp.arange(lp, dtype=F32) - float(PAD_FRONT)
    inv_freq = ROPE_THETA ** (-jnp.arange(0, ROT_DIM, 2, dtype=F32) / ROT_DIM)
    ang = pos[:, None] * inv_freq[None, :]
    cos, sin = jnp.cos(ang), jnp.sin(ang)
    half = ROT_DIM // 2
    ones = jnp.ones((lp, HEAD_DIM - ROT_DIM), F32)
    zeros_h = jnp.zeros((lp, half), F32)
    zeros_r = jnp.zeros((lp, HEAD_DIM - ROT_DIM), F32)
    c = jnp.concatenate([cos, cos, ones], axis=1)
    s1 = jnp.concatenate([-sin, zeros_h, zeros_r], axis=1)
    s2 = jnp.concatenate([zeros_h, sin, zeros_r], axis=1)
    reps = LANES // HEAD_DIM
    return jnp.stack([jnp.tile(c, (1, reps)), jnp.tile(s1, (1, reps)), jnp.tile(s2, (1, reps))])


def _rope(x, c, s1, s2):
    half = ROT_DIM // 2
    outs = []
    for ch in range(x.shape[1] // LANES):
        xc = x[:, ch * LANES:(ch + 1) * LANES]
        outs.append(xc * c + pltpu.roll(xc, LANES - half, 1) * s1 + pltpu.roll(xc, half, 1) * s2)
    return outs[0] if len(outs) == 1 else jnp.concatenate(outs, axis=1)


def _rope_t(dy, c, s1, s2):
    half = ROT_DIM // 2
    outs = []
    for ch in range(dy.shape[1] // LANES):
        dc = dy[:, ch * LANES:(ch + 1) * LANES]
        outs.append(dc * c + pltpu.roll(dc * s1, half, 1) + pltpu.roll(dc * s2, LANES - half, 1))
    return outs[0] if len(outs) == 1 else jnp.concatenate(outs, axis=1)


def _in_proj(n, w_in, tabs, ssm_w, comm=None):
    lp, d = n.shape
    inw = w_in.shape[1]
    tm = _row_tile(lp)
    o1 = ATTN_WIDTH
    o2 = o1 + KV_WIDTH
    o3 = o2 + KV_WIDTH
    o4 = o3 + ssm_w
    o5 = o4 + d

    def epi(accs, in_refs, out_refs):
        z = accs[0]
        c, s1, s2 = in_refs[2][0], in_refs[2][1], in_refs[2][2]
        out_refs[0][...] = _rope(z[:, :o1], c, s1, s2).astype(BF16)
        out_refs[1][...] = _rope(z[:, o1:o2], c, s1, s2).astype(BF16)
        out_refs[2][...] = z[:, o2:o3].astype(BF16)
        out_refs[3][...] = z[:, o3:o4]
        out_refs[4][...] = z[:, o4:o5]
        out_refs[5][...] = z[:, o5:]

    def rs(w, dt):
        return jax.ShapeDtypeStruct((lp, w), dt), pl.BlockSpec((tm, w), lambda i: (i, 0))

    shapes, specs = zip(rs(o1, BF16), rs(KV_WIDTH, BF16), rs(KV_WIDTH, BF16), rs(ssm_w, F32), rs(d, F32), rs(d, F32))
    res = _matmul(
        "mix_in_proj", (lp // tm,), None, [n, w_in, tabs],
        [pl.BlockSpec((tm, d), lambda i: (i, 0)), pl.BlockSpec((d, inw), lambda i: (0, 0)),
         pl.BlockSpec((3, tm, LANES), lambda i: (0, i, 0))],
        [(0, 1, "nn", 0)], [(tm, inw)], epi, list(shapes), list(specs), ("parallel",), comm)
    return (res, []) if comm is None else res


def _attn_mask(b):
    rows = lax.broadcasted_iota(jnp.int32, (BLOCK, 3 * BLOCK), 0)
    cols = lax.broadcasted_iota(jnp.int32, (BLOCK, 3 * BLOCK), 1)
    qpos = b * BLOCK + rows - PAD_FRONT
    kpos = (b - 1) * BLOCK + cols - PAD_FRONT
    dist = qpos - kpos
    band = (cols < 2 * BLOCK) & (kpos >= N_META) & (dist >= 0) & (dist < BLOCK)
    mrow = cols - 2 * BLOCK
    meta = (mrow >= PAD_FRONT) & ((mrow - PAD_FRONT) <= qpos)
    return band | meta


def _attn_probs(qh, kk, mask, sink):
    s = _dot(qh, kk, "nt") * (HEAD_DIM ** -0.5)
    s = jnp.where(mask, s, NEG_INF)
    m = jnp.maximum(jnp.max(s, axis=-1, keepdims=True), sink)
    e = jnp.exp(s - m)
    es = jnp.exp(sink - m)
    z = jnp.sum(e, axis=-1, keepdims=True) + es
    inv = 1.0 / z
    return e * inv, es * inv


def _head(ref_or_val, h):
    return ref_or_val[:, h * HEAD_DIM:(h + 1) * HEAD_DIM]


def _attn_fwd(q, k, v, sinks, comm=None):
    lp = q.shape[0]
    nb = lp // BLOCK

    def body(sink_ref, q_ref, kp_ref, kc_ref, km_ref, vp_ref, vc_ref, vm_ref, o_ref):
        b = pl.program_id(0)
        mask = _attn_mask(b)
        for hk in range(N_KV_HEADS):
            kk = jnp.concatenate([_head(kp_ref, hk), _head(kc_ref, hk), _head(km_ref, hk)], axis=0)
            vv = jnp.concatenate([_head(vp_ref, hk), _head(vc_ref, hk), _head(vm_ref, hk)], axis=0)
            for g in range(Q_PER_KV):
                h = hk * Q_PER_KV + g
                p, _ = _attn_probs(_head(q_ref, h), kk, mask, sink_ref[h])
                o_ref[:, h * HEAD_DIM:(h + 1) * HEAD_DIM] = _dot(p, vv, "nn").astype(BF16)

    cur = lambda b: (b, 0)
    prev = lambda b: (jnp.maximum(b - 1, 0), 0)
    first = lambda b: (0, 0)
    kvs = lambda f: pl.BlockSpec((BLOCK, KV_WIDTH), f)
    (o,), couts = _hosted_call(
        body, comm, out_shape=[jax.ShapeDtypeStruct((lp, ATTN_WIDTH), BF16)], grid=(nb,),
        in_specs=[pl.BlockSpec(memory_space=pltpu.SMEM), pl.BlockSpec((BLOCK, ATTN_WIDTH), cur),
                  kvs(prev), kvs(cur), kvs(first), kvs(prev), kvs(cur), kvs(first)],
        out_specs=[pl.BlockSpec((BLOCK, ATTN_WIDTH), cur)], scratch_shapes=[],
        sem=("parallel",), name="attn_fwd", args=(sinks, q, k, k, k, v, v, v))
    return o, couts


def _attn_bwd(q, k, v, do, sinks, tabs, comm=None):
    lp = q.shape[0]
    nb = lp // BLOCK
    scale = HEAD_DIM ** -0.5

    def body(sink_ref, q_ref, do_ref, kp_ref, kc_ref, km_ref, vp_ref, vc_ref, vm_ref, tq_ref, tk_ref, t0_ref,
             dq_ref, dk_ref, dv_ref, dkm_ref, dvm_ref, dsink_ref,
             dq_s, dkk_s, dvv_s, ck_s, cv_s, mk_s, mv_s):
        b = pl.program_id(0)

        @pl.when(b == 0)
        def _():
            for r in (ck_s, cv_s, mk_s, mv_s, dsink_ref):
                r[...] = jnp.zeros_like(r)

        @pl.when(b < nb)
        def _():
            mask = _attn_mask(b)
            for hk in range(N_KV_HEADS):
                kk = jnp.concatenate([_head(kp_ref, hk), _head(kc_ref, hk), _head(km_ref, hk)], axis=0)
                vv = jnp.concatenate([_head(vp_ref, hk), _head(vc_ref, hk), _head(vm_ref, hk)], axis=0)
                dkk = jnp.zeros((3 * BLOCK, HEAD_DIM), F32)
                dvv = jnp.zeros((3 * BLOCK, HEAD_DIM), F32)
                for g in range(Q_PER_KV):
                    h = hk * Q_PER_KV + g
                    qh = _head(q_ref, h)
                    doh = _head(do_ref, h)
                    p, ps = _attn_probs(qh, kk, mask, sink_ref[h])
                    dp = _dot(doh, vv, "nt")
                    delta = jnp.sum(p * dp, axis=-1, keepdims=True)
                    ds = (p * (dp - delta)).astype(BF16)
                    dsink_ref[h:h + 1, :] += jnp.zeros((1, LANES), F32) - jnp.sum(ps * delta)
                    dq_s[:, h * HEAD_DIM:(h + 1) * HEAD_DIM] = _dot(ds, kk, "nn") * scale
                    dkk = dkk + _dot(ds, qh, "tn") * scale
                    dvv = dvv + _dot(p, doh, "tn")
                dkk_s[:, hk * HEAD_DIM:(hk + 1) * HEAD_DIM] = dkk
                dvv_s[:, hk * HEAD_DIM:(hk + 1) * HEAD_DIM] = dvv
            dq_ref[...] = _rope_t(dq_s[...], tq_ref[0], tq_ref[1], tq_ref[2])
            dk_ref[...] = _rope_t(ck_s[...] + dkk_s[0:BLOCK, :], tk_ref[0], tk_ref[1], tk_ref[2])
            dv_ref[...] = cv_s[...] + dvv_s[0:BLOCK, :]
            ck_s[...] = dkk_s[BLOCK:2 * BLOCK, :]
            cv_s[...] = dvv_s[BLOCK:2 * BLOCK, :]
            mk_s[...] += dkk_s[2 * BLOCK:, :]
            mv_s[...] += dvv_s[2 * BLOCK:, :]

        @pl.when(b == nb)
        def _():
            dk_ref[...] = _rope_t(ck_s[...], tk_ref[0], tk_ref[1], tk_ref[2])
            dv_ref[...] = cv_s[...]
            dkm_ref[...] = _rope_t(mk_s[...], t0_ref[0], t0_ref[1], t0_ref[2])
            dvm_ref[...] = mv_s[...]

    cur = lambda b: (jnp.minimum(b, nb - 1), 0)
    prev = lambda b: (jnp.clip(b - 1, 0, nb - 1), 0)
    first = lambda b: (0, 0)
    kvs = lambda f: pl.BlockSpec((BLOCK, KV_WIDTH), f)
    tab = lambda f: pl.BlockSpec((3, BLOCK, LANES), lambda b: (0,) + f(b)[:1] + (0,))
    kv_out = lambda b: (jnp.maximum(b - 1, 0), 0)
    return _hosted_call(
        body, comm,
        out_shape=[jax.ShapeDtypeStruct((lp, ATTN_WIDTH), F32), jax.ShapeDtypeStruct((lp, KV_WIDTH), F32),
                   jax.ShapeDtypeStruct((lp, KV_WIDTH), F32), jax.ShapeDtypeStruct((BLOCK, KV_WIDTH), F32),
                   jax.ShapeDtypeStruct((BLOCK, KV_WIDTH), F32), jax.ShapeDtypeStruct((N_Q_HEADS, LANES), F32)],
        grid=(nb + 1,),
        in_specs=[pl.BlockSpec(memory_space=pltpu.SMEM), pl.BlockSpec((BLOCK, ATTN_WIDTH), cur),
                  pl.BlockSpec((BLOCK, ATTN_WIDTH), cur),
                  kvs(prev), kvs(cur), kvs(first), kvs(prev), kvs(cur), kvs(first),
                  tab(cur), tab(kv_out), tab(first)],
        out_specs=[pl.BlockSpec((BLOCK, ATTN_WIDTH), cur), kvs(kv_out), kvs(kv_out), kvs(first), kvs(first),
                   pl.BlockSpec((N_Q_HEADS, LANES), first)],
        scratch_shapes=[pltpu.VMEM((BLOCK, ATTN_WIDTH), F32), pltpu.VMEM((3 * BLOCK, KV_WIDTH), F32),
                        pltpu.VMEM((3 * BLOCK, KV_WIDTH), F32), pltpu.VMEM((BLOCK, KV_WIDTH), F32),
                        pltpu.VMEM((BLOCK, KV_WIDTH), F32), pltpu.VMEM((BLOCK, KV_WIDTH), F32),
                        pltpu.VMEM((BLOCK, KV_WIDTH), F32)],
        sem=("arbitrary",), name="attn_bwd", args=(sinks, q, do, k, k, k, v, v, v, tabs, tabs, tabs))


def _cmul(ar, ai, br, bi):
    return ar * br - ai * bi, ar * bi + ai * br


def _cpow(lr, li, n):
    rr = ri = None
    br, bi = lr, li
    while n:
        if n & 1:
            rr, ri = (br, bi) if rr is None else _cmul(rr, ri, br, bi)
        n >>= 1
        if n:
            br, bi = _cmul(br, bi, br, bi)
    return rr, ri


def _shift_rows(x, d, reverse):
    rows = lax.broadcasted_iota(jnp.int32, x.shape, 0)
    if not reverse:
        return jnp.where(rows >= d, pltpu.roll(x, d, 0), 0.0)
    return jnp.where(rows < SUBLANES - d, pltpu.roll(x, SUBLANES - d, 0), 0.0)


def _sublane_powers(mr, mi, reverse):
    rows = lax.broadcasted_iota(jnp.int32, mr.shape, 0)
    e = SUBLANES - 1 - rows if reverse else rows
    pr, pi = jnp.ones_like(mr), jnp.zeros_like(mr)
    br, bi = mr, mi
    for d in (1, 2, 4):
        tr, ti = _cmul(pr, pi, br, bi)
        on = (e & d) != 0
        pr, pi = jnp.where(on, tr, pr), jnp.where(on, ti, pi)
        if d < 4:
            br, bi = _cmul(br, bi, br, bi)
    return pr, pi


def _inclusive_prefix(er, ei, mr, mi, reverse):
    ir, ii, pr, pi = er, ei, mr, mi
    for d in (1, 2, 4):
        tr, ti = _cmul(pr, pi, _shift_rows(ir, d, reverse), _shift_rows(ii, d, reverse))
        ir, ii = ir + tr, ii + ti
        if d < 4:
            pr, pi = _cmul(pr, pi, pr, pi)
    return ir, ii


def _chain_rows(a, t, seg):
    return pl.ds(a * SUBLANES * seg + t, SUBLANES, stride=seg)


def _seg_scan(xr_ref, xi_ref, lam, seg, nchain, reverse, store, init, extra=None):
    nt = len(lam)
    acc0 = () if extra is None else extra[1]

    def step(i, carry):
        hs, acc = carry
        t = seg - 1 - i if reverse else i
        out = []
        for a in range(nchain):
            sl = _chain_rows(a, t, seg)
            for j in range(nt):
                lr, li = lam[j]
                k = 2 * (a * nt + j)
                hr, hi = hs[k], hs[k + 1]
                nr = lr * hr - li * hi + xr_ref[j, sl, :]
                ni = lr * hi + li * hr + xi_ref[j, sl, :]
                if store:
                    xr_ref[j, sl, :] = nr
                    xi_ref[j, sl, :] = ni
                if extra is not None:
                    acc = extra[0](t, a, j, nr, ni, acc)
                out += [nr, ni]
        return tuple(out), acc

    return lax.fori_loop(0, seg, step, (tuple(init), acc0))


def _ssm_scan(xr_ref, xi_ref, lam, seg, nchain, reverse, extra=None):
    nt = len(lam)
    zero = [jnp.zeros((SUBLANES, LANES), F32)] * (2 * nt * nchain)
    ends, _ = _seg_scan(xr_ref, xi_ref, lam, seg, nchain, reverse, False, zero)
    init = [None] * (2 * nt * nchain)
    last = 0 if reverse else SUBLANES - 1
    for j in range(nt):
        mr, mi = _cpow(lam[j][0], lam[j][1], seg)
        m8r, m8i = _cpow(mr, mi, SUBLANES)
        pwr, pwi = _sublane_powers(mr, mi, reverse)
        gr = gi = jnp.zeros((SUBLANES, LANES), F32)
        for a in (reversed(range(nchain)) if reverse else range(nchain)):
            k = 2 * (a * nt + j)
            incr, inci = _inclusive_prefix(ends[k], ends[k + 1], mr, mi, reverse)
            tr, ti = _cmul(pwr, pwi, gr, gi)
            init[k] = _shift_rows(incr, 1, reverse) + tr
            init[k + 1] = _shift_rows(inci, 1, reverse) + ti
            g2r, g2i = _cmul(m8r, m8i, gr, gi)
            gr = g2r + jnp.broadcast_to(incr[last:last + 1, :], gr.shape)
            gi = g2i + jnp.broadcast_to(inci[last:last + 1, :], gi.shape)
    _, acc = _seg_scan(xr_ref, xi_ref, lam, seg, nchain, reverse, True, init, extra)
    return acc


def _diag_mask():
    steps = LANES // SSM_GROUP // GROUPS_PER_TILE
    return (jnp.eye(steps, dtype=F32)[:, None, :, None] * jnp.eye(GROUPS_PER_TILE, dtype=F32)[None, :, None, :])


def _ssm_tables(bb_re, bb_im, c_re, c_im):
    g = bb_re.shape[0]
    nt = g // GROUPS_PER_TILE
    steps = LANES // SSM_GROUP // GROUPS_PER_TILE
    mask = _diag_mask()

    def b_tab(bb):
        x = bb.reshape(nt // steps, steps, GROUPS_PER_TILE, SSM_STATE, SSM_GROUP)
        x = jnp.transpose(x, (0, 1, 4, 2, 3))[:, :, None, None]
        m = jnp.transpose(mask, (0, 2, 3, 1))[None, :, :, :, None, :, None]
        return (x * m).reshape(nt, LANES, TILE_STATES)

    def c_tab(c):
        x = c.reshape(nt // steps, steps, GROUPS_PER_TILE, SSM_GROUP, SSM_STATE)
        x = jnp.transpose(x, (0, 1, 2, 4, 3))[:, :, :, :, None, None]
        m = mask[None, :, :, None, :, :, None]
        return (x * m).reshape(nt, TILE_STATES, LANES)

    return b_tab(bb_re), b_tab(bb_im), c_tab(c_re), c_tab(c_im)


def _ssm_untable_b(db, g):
    nt = g // GROUPS_PER_TILE
    steps = LANES // SSM_GROUP // GROUPS_PER_TILE
    x = db.reshape(nt // steps, steps, GROUPS_PER_TILE, SSM_STATE, steps, GROUPS_PER_TILE, SSM_GROUP)
    m = _diag_mask()[None, :, :, None, :, :, None]
    return jnp.sum(x * m, axis=(4, 5)).reshape(g, SSM_STATE, SSM_GROUP)


def _ssm_untable_c(dc, g):
    nt = g // GROUPS_PER_TILE
    steps = LANES // SSM_GROUP // GROUPS_PER_TILE
    x = dc.reshape(nt // steps, steps, steps, GROUPS_PER_TILE, SSM_GROUP, GROUPS_PER_TILE, SSM_STATE)
    m = jnp.transpose(_diag_mask(), (0, 2, 3, 1))[None, :, :, :, None, :, None]
    out = jnp.sum(x * m, axis=(2, 3))
    return jnp.transpose(out, (0, 1, 3, 2, 4)).reshape(g, SSM_GROUP, SSM_STATE)


def _lam_tiles(lam_ref):
    out = []
    for j in range(TILE_STATES // LANES):
        out.append(jnp.broadcast_to(lam_ref[:, j * LANES:(j + 1) * LANES], (SUBLANES, LANES)))
    return out


def _scan_chains(lp):
    for n in (4, 2, 1):
        if lp % (SUBLANES * n) == 0 and (lp // SUBLANES) % 16 == 0:
            return n
    raise ValueError(lp)


def _split_tiles(dst_ref, rows, val):
    for j in range(val.shape[1] // LANES):
        dst_ref[j, rows, :] = val[:, j * LANES:(j + 1) * LANES]


def _cat_tiles(src_ref, rows):
    njt = src_ref.shape[0]
    return jnp.concatenate([src_ref[j, rows, :] for j in range(njt)], axis=1).astype(BF16)


def _ssm_fwd(u, lam_re, lam_im, tb_re, tb_im, tc_re, tc_im, d_skip, comm=None):
    lp, w = u.shape
    nt = tb_re.shape[0]
    nchain = _scan_chains(lp)
    seg = lp // (SUBLANES * nchain)
    chunk = lp // SUBLANES
    njt = TILE_STATES // LANES

    def body(u_ref, lr_ref, li_ref, br_ref, bi_ref, cr_ref, ci_ref, d_ref, y_ref, yg_ref, xr, xi):
        t = pl.program_id(0)
        for s in range(SUBLANES):
            rs = pl.ds(s * chunk, chunk)
            ub = u_ref[rs, :].astype(BF16)
            _split_tiles(xr, rs, _dot(ub, br_ref[...], "nn"))
            _split_tiles(xi, rs, _dot(ub, bi_ref[...], "nn"))
        lrs, lis = _lam_tiles(lr_ref), _lam_tiles(li_ref)
        _ssm_scan(xr, xi, list(zip(lrs, lis)), seg, nchain, False)
        for s in range(SUBLANES):
            rs = pl.ds(s * chunk, chunk)
            y = _dot(_cat_tiles(xr, rs), cr_ref[...], "nn") - _dot(_cat_tiles(xi, rs), ci_ref[...], "nn")

            @pl.when(t % 2 == 0)
            def _():
                y_ref[rs, :] = y + d_ref[...] * u_ref[rs, :]

            @pl.when(t % 2 == 1)
            def _():
                total = y_ref[rs, :] + y
                y_ref[rs, :] = total
                yg_ref[rs, :] = _gelu(total).astype(BF16)

    blk = pl.BlockSpec((lp, LANES), lambda t: (0, t // 2))
    lam_spec = pl.BlockSpec((None, 1, TILE_STATES), lambda t: (t, 0, 0))
    b_spec = pl.BlockSpec((None, LANES, TILE_STATES), lambda t: (t, 0, 0))
    c_spec = pl.BlockSpec((None, TILE_STATES, LANES), lambda t: (t, 0, 0))
    (y, yg), couts = _hosted_call(
        body, comm, out_shape=[jax.ShapeDtypeStruct((lp, w), F32), jax.ShapeDtypeStruct((lp, w), BF16)], grid=(nt,),
        in_specs=[blk, lam_spec, lam_spec, b_spec, b_spec, c_spec, c_spec,
                  pl.BlockSpec((1, LANES), lambda t: (0, t // 2))],
        out_specs=[blk, blk],
        scratch_shapes=[pltpu.VMEM((njt, lp, LANES), F32), pltpu.VMEM((njt, lp, LANES), F32)],
        sem=("arbitrary",), name="ssm_fwd",
        args=(u, lam_re, lam_im, tb_re, tb_im, tc_re, tc_im, d_skip.reshape(1, w)))
    return (y, yg), couts


def _ssm_bwd(u, dy, lam_re, lam_im, tb_re, tb_im, tc_re, tc_im, d_skip, comm=None):
    lp, w = u.shape
    nt = tb_re.shape[0]
    nchain = _scan_chains(lp)
    seg = lp // (SUBLANES * nchain)
    chunk = lp // SUBLANES
    njt = TILE_STATES // LANES
    tbt_re, tbt_im = jnp.swapaxes(tb_re, 1, 2), jnp.swapaxes(tb_im, 1, 2)
    tct_re, tct_im = jnp.swapaxes(tc_re, 1, 2), jnp.swapaxes(tc_im, 1, 2)

    def body(u_ref, dy_ref, lr_ref, li_ref, br_ref, bi_ref, btr_ref, bti_ref, ctr_ref, cti_ref, d_ref,
             du_ref, dlr_ref, dli_ref, dbr_ref, dbi_ref, dcr_ref, dci_ref, dd_ref, hr, hi, ar, ai):
        t = pl.program_id(0)
        lrs, lis = _lam_tiles(lr_ref), _lam_tiles(li_ref)
        for s in range(SUBLANES):
            rs = pl.ds(s * chunk, chunk)
            ub = u_ref[rs, :].astype(BF16)
            dyb = dy_ref[rs, :].astype(BF16)
            _split_tiles(hr, rs, _dot(ub, br_ref[...], "nn"))
            _split_tiles(hi, rs, _dot(ub, bi_ref[...], "nn"))
            _split_tiles(ar, rs, _dot(dyb, ctr_ref[...], "nn"))
            _split_tiles(ai, rs, -_dot(dyb, cti_ref[...], "nn"))
        _ssm_scan(hr, hi, list(zip(lrs, lis)), seg, nchain, False)

        def dlam_step(tt, a, j, a_r, a_i, acc):
            sl = _chain_rows(a, jnp.maximum(tt - 1, 0), seg)
            p_r, p_i = hr[j, sl, :], hi[j, sl, :]
            acc = list(acc)
            acc[2 * j] = acc[2 * j] + jnp.where(tt > 0, a_r * p_r + a_i * p_i, 0.0)
            acc[2 * j + 1] = acc[2 * j + 1] + jnp.where(tt > 0, a_i * p_r - a_r * p_i, 0.0)
            return tuple(acc)

        zero = tuple([jnp.zeros((SUBLANES, LANES), F32)] * (2 * njt))
        conj = [(lr, -li) for lr, li in zip(lrs, lis)]
        acc = list(_ssm_scan(ar, ai, conj, seg, nchain, True, (dlam_step, zero)))
        row0 = lax.broadcasted_iota(jnp.int32, (SUBLANES, LANES), 0) == 0
        for j in range(njt):
            cs = slice(j * LANES, (j + 1) * LANES)
            for a in range(nchain):
                p_r = _shift_rows(hr[j, _chain_rows(a, seg - 1, seg), :], 1, False)
                p_i = _shift_rows(hi[j, _chain_rows(a, seg - 1, seg), :], 1, False)
                if a > 0:
                    before = pl.ds(a * SUBLANES * seg - 1, 1)
                    p_r = jnp.where(row0, jnp.broadcast_to(hr[j, before, :], p_r.shape), p_r)
                    p_i = jnp.where(row0, jnp.broadcast_to(hi[j, before, :], p_i.shape), p_i)
                a_r, a_i = ar[j, _chain_rows(a, 0, seg), :], ai[j, _chain_rows(a, 0, seg), :]
                acc[2 * j] = acc[2 * j] + a_r * p_r + a_i * p_i
                acc[2 * j + 1] = acc[2 * j + 1] + a_i * p_r - a_r * p_i
            dlr_ref[:, cs] = jnp.sum(acc[2 * j], axis=0, keepdims=True)
            dli_ref[:, cs] = jnp.sum(acc[2 * j + 1], axis=0, keepdims=True)

        dd = jnp.zeros((1, LANES), F32)
        for s in range(SUBLANES):
            rs = pl.ds(s * chunk, chunk)
            ub = u_ref[rs, :].astype(BF16)
            dyv = dy_ref[rs, :]
            dyb = dyv.astype(BF16)
            arb, aib = _cat_tiles(ar, rs), _cat_tiles(ai, rs)
            hrb, hib = _cat_tiles(hr, rs), _cat_tiles(hi, rs)
            du = _dot(arb, btr_ref[...], "nn") + _dot(aib, bti_ref[...], "nn")
            upd = [(dbr_ref, _dot(arb, ub, "tn")), (dbi_ref, _dot(aib, ub, "tn")),
                   (dcr_ref, _dot(dyb, hrb, "tn")), (dci_ref, -_dot(dyb, hib, "tn"))]
            for ref, val in upd:
                if s == 0:
                    ref[...] = val
                else:
                    ref[...] += val
            rows = lax.broadcasted_iota(jnp.int32, (chunk, LANES), 0) + s * chunk
            keep = rows >= PAD_FRONT
            dd = dd + jnp.sum(dyv * u_ref[rs, :], axis=0, keepdims=True)

            @pl.when(t % 2 == 0)
            def _():
                du_ref[rs, :] = jnp.where(keep, du + d_ref[...] * dyv, 0.0)

            @pl.when(t % 2 == 1)
            def _():
                du_ref[rs, :] += jnp.where(keep, du, 0.0)

        @pl.when(t % 2 == 0)
        def _():
            dd_ref[...] = dd

    blk = pl.BlockSpec((lp, LANES), lambda t: (0, t // 2))
    vec = pl.BlockSpec((1, LANES), lambda t: (0, t // 2))
    lam_spec = pl.BlockSpec((None, 1, TILE_STATES), lambda t: (t, 0, 0))
    b_spec = pl.BlockSpec((None, LANES, TILE_STATES), lambda t: (t, 0, 0))
    c_spec = pl.BlockSpec((None, TILE_STATES, LANES), lambda t: (t, 0, 0))
    lam_shape = jax.ShapeDtypeStruct((nt, 1, TILE_STATES), F32)
    bt_shape = jax.ShapeDtypeStruct((nt, TILE_STATES, LANES), F32)
    ct_shape = jax.ShapeDtypeStruct((nt, LANES, TILE_STATES), F32)
    st = pltpu.VMEM((njt, lp, LANES), F32)
    return _hosted_call(
        body, comm,
        out_shape=[jax.ShapeDtypeStruct((lp, w), F32), lam_shape, lam_shape, bt_shape, bt_shape, ct_shape, ct_shape,
                   jax.ShapeDtypeStruct((1, w), F32)],
        grid=(nt,),
        in_specs=[blk, blk, lam_spec, lam_spec, b_spec, b_spec, c_spec, c_spec, b_spec, b_spec, vec],
        out_specs=[blk, lam_spec, lam_spec, c_spec, c_spec, b_spec, b_spec, vec],
        scratch_shapes=[st, st, st, st], sem=("arbitrary",), name="ssm_bwd",
        args=(u, dy, lam_re, lam_im, tb_re, tb_im, tbt_re, tbt_im, tct_re, tct_im, d_skip.reshape(1, w)))


def _ssm_params(a_re, a_im, log_dt, b_re, b_im):
    dt = jnp.exp(log_dt)[:, None]
    mag = jnp.exp(a_re * dt)
    lb_re = mag * jnp.cos(a_im * dt)
    lb_im = mag * jnp.sin(a_im * dt)
    den = a_re * a_re + a_im * a_im
    num_re = lb_re - 1.0
    coef_re = (num_re * a_re + lb_im * a_im) / den
    coef_im = (lb_im * a_re - num_re * a_im) / den
    bb_re = coef_re[..., None] * b_re - coef_im[..., None] * b_im
    bb_im = coef_re[..., None] * b_im + coef_im[..., None] * b_re
    return lb_re, lb_im, bb_re, bb_im


def _merge_fwd(o, yg, ga, gs, w3t, comm=None):
    lp, d = ga.shape
    kw = w3t.shape[2]
    tm = _row_tile(lp)

    def epi(accs, in_refs, out_refs):
        attn, vv, gg = accs
        out_refs[0][...] = (_sigmoid(in_refs[5][...]) * attn
                            + _sigmoid(in_refs[6][...]) * (vv * _sigmoid(gg))).astype(BF16)

    wspec = lambda which: pl.BlockSpec((None, d, kw), lambda i: (which, 0, 0))
    rowspec = pl.BlockSpec((tm, d), lambda i: (i, 0))
    aspec = pl.BlockSpec((tm, kw), lambda i: (i, 0))
    res = _matmul(
        "merge_fwd", (lp // tm,), None, [o, yg, w3t, w3t, w3t, ga, gs],
        [aspec, aspec, wspec(0), wspec(1), wspec(2), rowspec, rowspec],
        [(0, 2, "nt", 0), (1, 3, "nt", 1), (1, 4, "nt", 2)], [(tm, d)] * 3, epi,
        [jax.ShapeDtypeStruct((lp, d), BF16)], [rowspec], ("parallel",), comm)
    return (res[0], []) if comm is None else (res[0][0], res[1])


def _out_proj(merged, w_out, h, comm=None):
    lp, d = h.shape
    tm = _row_tile(lp)

    def epi(accs, in_refs, out_refs):
        out_refs[0][...] = in_refs[2][...] + accs[0]

    rowspec = pl.BlockSpec((tm, d), lambda i: (i, 0))
    res = _matmul(
        "mix_out_proj", (lp // tm,), None, [merged, w_out, h],
        [rowspec, pl.BlockSpec((d, d), lambda i: (0, 0)), rowspec],
        [(0, 1, "nn", 0)], [(tm, d)], epi, [jax.ShapeDtypeStruct((lp, d), F32)], [rowspec], ("parallel",), comm)
    return (res[0], []) if comm is None else (res[0][0], res[1])


def _merge_bwd(dhb, w_out, o, yg, ga, gs, w3t):
    lp, d = ga.shape
    kw = w3t.shape[2]
    tm = _row_tile(lp)

    def epi(accs, in_refs, out_refs):
        dm, attn, vv, gg = accs
        sa = _sigmoid(in_refs[7][...])
        ss = _sigmoid(in_refs[8][...])
        sg = _sigmoid(gg)
        ssm = vv * sg
        dssm = dm * ss
        out_refs[0][...] = (dm * sa).astype(BF16)
        out_refs[1][...] = (dssm * sg).astype(BF16)
        out_refs[2][...] = (dssm * vv * sg * (1.0 - sg)).astype(BF16)
        out_refs[3][...] = (dm * attn * sa * (1.0 - sa)).astype(BF16)
        out_refs[4][...] = (dm * ssm * ss * (1.0 - ss)).astype(BF16)

    wspec = lambda which: pl.BlockSpec((None, d, kw), lambda i: (which, 0, 0))
    rowspec = pl.BlockSpec((tm, d), lambda i: (i, 0))
    aspec = pl.BlockSpec((tm, kw), lambda i: (i, 0))
    shp = jax.ShapeDtypeStruct((lp, d), BF16)
    return _matmul(
        "merge_bwd", (lp // tm,), None, [dhb, w_out, o, yg, w3t, w3t, w3t, ga, gs],
        [rowspec, pl.BlockSpec((d, d), lambda i: (0, 0)), aspec, aspec, wspec(0), wspec(1), wspec(2), rowspec,
         rowspec],
        [(0, 1, "nt", 0), (2, 4, "nt", 1), (3, 5, "nt", 2), (3, 6, "nt", 3)], [(tm, d)] * 4, epi,
        [shp] * 5, [rowspec] * 5, ("parallel",))


def _branch_bwd(dattn, dv, dg, w3t, y):
    lp, d = dattn.shape
    kw = w3t.shape[2]
    tm = _row_tile(lp)

    def epi(accs, in_refs, out_refs):
        out_refs[0][...] = accs[0].astype(BF16)
        out_refs[1][...] = accs[1] * _gelu_grad(in_refs[6][...])

    wspec = lambda which: pl.BlockSpec((None, d, kw), lambda i: (which, 0, 0))
    rowspec = pl.BlockSpec((tm, d), lambda i: (i, 0))
    aspec = pl.BlockSpec((tm, kw), lambda i: (i, 0))
    return _matmul(
        "branch_bwd", (lp // tm,), None, [dattn, dv, dg, w3t, w3t, w3t, y],
        [rowspec, rowspec, rowspec, wspec(0), wspec(1), wspec(2), aspec],
        [(0, 3, "nn", 0), (1, 4, "nn", 1), (2, 5, "nn", 1)], [(tm, kw)] * 2, epi,
        [jax.ShapeDtypeStruct((lp, kw), BF16), jax.ShapeDtypeStruct((lp, kw), F32)], [aspec, aspec],
        ("parallel",))


def _tn_cols(x, ys, name):
    lp, kx = x.shape
    n = ys[0].shape[1]
    n4 = n // N_CHIPS
    tk = _tn_tiles(lp)

    def epi(accs, in_refs, out_refs):
        for acc, o in zip(accs, out_refs):
            o[...] = acc

    shp = jax.ShapeDtypeStruct((N_CHIPS, kx, n4), F32)
    return _matmul(
        name, (N_CHIPS, lp // tk), 1, [x] + list(ys),
        [pl.BlockSpec((tk, kx), lambda j, k: (k, 0))] + [pl.BlockSpec((tk, n4), lambda j, k: (k, j))] * len(ys),
        [(0, 1 + i, "tn", i) for i in range(len(ys))], [(kx, n4)] * len(ys), epi,
        [shp] * len(ys), [pl.BlockSpec((None, kx, n4), lambda j, k: (j, 0, 0))] * len(ys),
        ("parallel", "arbitrary"))


def _tn_full(x, y, name, tn_cols=None):
    lp, kx = x.shape
    n = y.shape[1]
    tk = _tn_tiles(lp)
    tn = n if tn_cols is None else tn_cols

    def epi(accs, in_refs, out_refs):
        out_refs[0][...] = accs[0]

    (out,) = _matmul(
        name, (n // tn, lp // tk), 1, [x, y],
        [pl.BlockSpec((tk, kx), lambda j, k: (k, 0)), pl.BlockSpec((tk, tn), lambda j, k: (k, j))],
        [(0, 1, "tn", 0)], [(kx, tn)], epi,
        [jax.ShapeDtypeStruct((kx, n), F32)], [pl.BlockSpec((kx, tn), lambda j, k: (0, j))],
        ("parallel", "arbitrary"))
    return out


def _in_proj_bwd(dz, w_in, dh, h_in, gain):
    lp, d = h_in.shape
    inw = w_in.shape[1]
    tm = _row_tile(lp)

    def epi(accs, in_refs, out_refs):
        i = pl.program_id(0)
        dx, dgrow = _rms_bwd_math(accs[0], in_refs[3][...], in_refs[4][...])
        dh_new = in_refs[2][...] + dx
        out_refs[0][...] = dh_new
        out_refs[2][...] = dh_new.astype(BF16)

        @pl.when(i == 0)
        def _():
            out_refs[1][...] = jnp.zeros_like(out_refs[1])

        out_refs[1][...] += jnp.sum(dgrow, axis=0, keepdims=True)

    row = pl.BlockSpec((tm, d), lambda i: (i, 0))
    return _matmul(
        "mix_in_proj_bwd", (lp // tm,), None, [dz, w_in, dh, h_in, gain.reshape(1, d)],
        [pl.BlockSpec((tm, inw), lambda i: (i, 0)), pl.BlockSpec((d, inw), lambda i: (0, 0)), row, row,
         pl.BlockSpec((1, d), lambda i: (0, 0))],
        [(0, 1, "nt", 0)], [(tm, d)], epi,
        [jax.ShapeDtypeStruct((lp, d), F32), jax.ShapeDtypeStruct((1, d), F32), jax.ShapeDtypeStruct((lp, d), BF16)],
        [row, pl.BlockSpec((1, d), lambda i: (0, 0)), row], ("arbitrary",))


def _loss_head(h, gain, target):
    lp, d = h.shape
    nb = lp // BLOCK

    def body(h_ref, g_ref, t_ref, dh_ref, dg_ref, loss_ref, dhb_ref):
        i = pl.program_id(0)

        @pl.when(i == 0)
        def _():
            dg_ref[...] = jnp.zeros_like(dg_ref)
            loss_ref[...] = jnp.zeros_like(loss_ref)
            dh_ref[...] = jnp.zeros_like(dh_ref)
            dhb_ref[...] = jnp.zeros_like(dhb_ref)

        @pl.when(i > 0)
        def _():
            x = h_ref[...]
            g = g_ref[...]
            r = lax.rsqrt(jnp.mean(x * x, axis=-1, keepdims=True) + EPS)
            err = x * r * g - t_ref[...]
            loss_ref[...] += jnp.zeros_like(loss_ref) + 0.5 * jnp.sum(jnp.sum(err * err, axis=-1, keepdims=True)) / d
            dx, dgrow = _rms_bwd_math(err * (1.0 / d), x, g)
            dh_ref[...] = dx
            dhb_ref[...] = dx.astype(BF16)
            dg_ref[...] += jnp.sum(dgrow, axis=0, keepdims=True)

    row = pl.BlockSpec((BLOCK, d), lambda i: (i, 0))
    one = pl.BlockSpec((1, d), lambda i: (0, 0))
    return pl.pallas_call(
        body,
        out_shape=[jax.ShapeDtypeStruct((lp, d), F32), jax.ShapeDtypeStruct((1, d), F32),
                   jax.ShapeDtypeStruct((SUBLANES, LANES), F32), jax.ShapeDtypeStruct((lp, d), BF16)],
        grid=(nb,),
        in_specs=[row, one, pl.BlockSpec((BLOCK, d), lambda i: (jnp.maximum(i - 1, 0), 0))],
        out_specs=[row, one, pl.BlockSpec((SUBLANES, LANES), lambda i: (0, 0)), row],
        compiler_params=_cparams(("arbitrary",)), name="loss_head")(h, gain.reshape(1, d), target)


def _adam_math(w, g, m, v):
    m = ADAM_B1 * m + (1.0 - ADAM_B1) * g
    v = ADAM_B2 * v + (1.0 - ADAM_B2) * (g * g)
    m_hat = m / (1.0 - ADAM_B1 ** ADAM_STEP)
    v_hat = v / (1.0 - ADAM_B2 ** ADAM_STEP)
    delta = -ADAM_LR * (m_hat / (jnp.sqrt(v_hat) + ADAM_EPS) + ADAM_WD * w)
    return delta, m, v


def _adamw_layers(w, m, v, mine, other, pos, name):
    depth, r, c = w.shape
    half = r // 2
    tr = _div_tile(half, c * 4)
    nh = half // tr

    def body(*refs):
        pos_ref, w_ref, m_ref, v_ref = refs[:4]
        mine_refs = refs[4:4 + depth]
        other_refs = refs[4 + depth:4 + 2 * depth]
        g_out, d_out, m_out, v_out = refs[4 + 2 * depth:]
        layer, i = pl.program_id(0), pl.program_id(1)
        is_mine = (i // nh) == pos_ref[0]

        def update(g):
            delta, nm, nv = _adam_math(w_ref[...], g, m_ref[...], v_ref[...])
            g_out[...] = g
            d_out[...] = delta
            m_out[...] = nm
            v_out[...] = nv

        for l in range(depth):
            @pl.when((layer == l) & is_mine)
            def _(l=l):
                update(mine_refs[l][...])

            @pl.when((layer == l) & jnp.logical_not(is_mine))
            def _(l=l):
                update(other_refs[l][...])

    stacked = pl.BlockSpec((None, tr, c), lambda l, i, p: (l, i, 0))

    def gspec(layer, is_other):
        def imap(l, i, p):
            first = jnp.where(is_other, 1 - p[0], p[0]) * nh
            here = jnp.clip(i - first, 0, nh - 1)
            return (jnp.where(l == layer, here, jnp.where(l < layer, 0, nh - 1)), 0)
        return pl.BlockSpec((tr, c), imap)

    shp = jax.ShapeDtypeStruct((depth, r, c), F32)
    grid_spec = pltpu.PrefetchScalarGridSpec(
        num_scalar_prefetch=1, grid=(depth, 2 * nh),
        in_specs=[stacked] * 3 + [gspec(l, 0) for l in range(depth)] + [gspec(l, 1) for l in range(depth)],
        out_specs=[stacked] * 4)
    return pl.pallas_call(
        body, out_shape=[shp] * 4, grid_spec=grid_spec,
        compiler_params=_cparams(("arbitrary", "arbitrary")), name=name)(pos, w, m, v, *mine, *other)


def _adamw_flat(w, g, m, v, name):
    r, c = w.shape
    tr = _div_tile(r, c * 4)

    def body(w_ref, g_ref, m_ref, v_ref, d_out, m_out, v_out):
        delta, nm, nv = _adam_math(w_ref[...], g_ref[...], m_ref[...], v_ref[...])
        d_out[...] = delta
        m_out[...] = nm
        v_out[...] = nv

    spec = pl.BlockSpec((tr, c), lambda i: (i, 0))
    shp = jax.ShapeDtypeStruct((r, c), F32)
    return pl.pallas_call(
        body, out_shape=[shp] * 3, grid=(r // tr,), in_specs=[spec] * 4, out_specs=[spec] * 3,
        compiler_params=_cparams(("parallel",)), name=name)(w, g, m, v)


def _mesh_pos():
    return lax.axis_index("x"), lax.axis_index("y"), lax.axis_index("c")


def _row_half(ref, which, lead):
    half = ref.shape[lead] // 2
    idx = (slice(None),) * lead + (pl.ds(which * half, half), slice(None))
    return ref.at[idx]


def _gather_comm(arrs, tag):
    n = len(arrs)

    def ctx(ins, outs, sems):
        send_sems, recv_sems, local_sems = sems
        x, y, c = _mesh_pos()
        chips = [(1 - x, y), (x, 1 - y), (1 - x, 1 - y)]

        def slot(k, chip, which):
            lead = len(ins[k].shape) - 2
            return _row_half(outs[k].at[2 * chip[0] + chip[1]], which, lead)

        def copy(k, j, src, dst, to):
            return pltpu.make_async_remote_copy(
                src_ref=src, dst_ref=dst, send_sem=send_sems.at[6 * k + j], recv_sem=recv_sems.at[6 * k + j],
                device_id=to, device_id_type=MESH)

        def local(k):
            return pltpu.make_async_copy(ins[k], outs[k].at[2 * x + y], local_sems.at[k])

        def first(k, j):
            lead = len(ins[k].shape) - 2
            return copy(k, j, _row_half(ins[k], c, lead), slot(k, (x, y), c), (*chips[j], c))

        def passed(k, j, which):
            return copy(k, 3 + j, slot(k, chips[j], which), slot(k, chips[j], which), (x, y, 1 - c))

        def landed(k, j):
            return copy(k, j, slot(k, chips[j], c), slot(k, chips[j], c), (x, y, 1 - c))

        return c, local, first, passed, landed

    def start(ins, outs, sems):
        c, local, first, passed, landed = ctx(ins, outs, sems)
        for k in range(n):
            local(k).start()
            for j in range(3):
                first(k, j).start()

    def mid(ins, outs, sems):
        c, local, first, passed, landed = ctx(ins, outs, sems)
        for j in range(3):
            for k in range(n):
                landed(k, j).wait_recv()
                passed(k, j, c).start()

    def finish(ins, outs, sems):
        c, local, first, passed, landed = ctx(ins, outs, sems)
        for j in range(3):
            for k in range(n):
                passed(k, j, 1 - c).wait_recv()
        for k in range(n):
            for j in range(3):
                first(k, j).wait_send()
                passed(k, j, c).wait_send()
            local(k).wait()

    return _Comm(
        tag, arrs, [jax.ShapeDtypeStruct((N_CHIPS,) + a.shape, a.dtype) for a in arrs],
        [pltpu.SemaphoreType.DMA((6 * n,)), pltpu.SemaphoreType.DMA((6 * n,)), pltpu.SemaphoreType.DMA((n,))],
        start, mid, finish)


def _run_comm(comm, name):
    n_in, n_out = len(comm.ins), len(comm.out_shapes)

    def body(*refs):
        ins, outs, sems = refs[:n_in], refs[n_in:n_in + n_out], refs[n_in + n_out:]
        comm.start(ins, outs, sems)
        if comm.mid is not None:
            comm.mid(ins, outs, sems)
        comm.finish(ins, outs, sems)

    return pl.pallas_call(
        body, out_shape=comm.out_shapes, in_specs=[HBM_SPEC] * n_in, out_specs=[HBM_SPEC] * n_out,
        scratch_shapes=comm.sems, name=name)(*comm.ins)


def _all_gather_chips(arrs, name):
    return _run_comm(_gather_comm(arrs, "gather"), name)


GATHER_US_PER_BYTE = 380.0 / 11.65e6
HOST_US = dict(ffn_up=78.0, ffn_down=65.0, in_proj=38.0, attn_fwd=103.0, ssm_fwd=67.0, merge_fwd=50.0,
               out_proj=45.0)
HOST_SLACK_US = 10.0


class _WeightStream:
    def __init__(self, pieces):
        self.keys = [k for k, _ in pieces]
        self.shards = dict(pieces)
        self.next = 0
        self.full = {}
        self.pending = []

    def comm_for(self, host):
        budget = HOST_US[host] + HOST_SLACK_US
        taken, cost = [], 0.0
        while self.next < len(self.keys):
            key = self.keys[self.next]
            c = self.shards[key].size * self.shards[key].dtype.itemsize * GATHER_US_PER_BYTE
            if cost + c > budget:
                break
            taken.append(key)
            cost += c
            self.next += 1
        self.pending = taken
        if not taken:
            return None
        return _gather_comm([self.shards[k] for k in taken], "g_" + "_".join(k[1] for k in taken))

    def deposit(self, gathered):
        for key, arr in zip(self.pending, gathered):
            self.full[key] = arr
        self.pending = []

    def get(self, key):
        if key not in self.full:
            upto = self.keys.index(key) + 1
            keys = self.keys[self.next:upto]
            self.next = upto
            for k, arr in zip(keys, _all_gather_chips([self.shards[k] for k in keys], "gather_now")):
                self.full[k] = arr
        return self.full[key]


def _all_gather_devices(x_shard, name):
    m_per, ncol = x_shard.shape

    def body(x_ref, out_ref, send_sems, recv_sems, local_sem):
        x, y, c = _mesh_pos()
        me, sibling = (x, y, c), (x, y, 1 - c)
        chips = [(1 - x, y), (x, 1 - y), (1 - x, 1 - y)]

        def rows(px, py, pc):
            return out_ref.at[4 * px + 2 * py + pc]

        def copy(k, block, to, src=None):
            return pltpu.make_async_remote_copy(
                src_ref=rows(*block) if src is None else src, dst_ref=rows(*block),
                send_sem=send_sems.at[k], recv_sem=recv_sems.at[k], device_id=to, device_id_type=MESH)

        mine = pltpu.make_async_copy(x_ref, rows(*me), local_sem)
        mine.start()
        first = [copy(0, me, sibling, src=x_ref)]
        first += [copy(1 + j, me, (*chip, c), src=x_ref) for j, chip in enumerate(chips)]
        for cp in first:
            cp.start()
        passed = [copy(4 + j, (*chip, c), sibling) for j, chip in enumerate(chips)]
        for j, chip in enumerate(chips):
            copy(1 + j, (*chip, c), me).wait_recv()
            passed[j].start()
        copy(0, sibling, me).wait_recv()
        for j, chip in enumerate(chips):
            copy(4 + j, (*chip, 1 - c), me).wait_recv()
        for cp in first + passed:
            cp.wait_send()
        mine.wait()

    return pl.pallas_call(
        body, out_shape=jax.ShapeDtypeStruct((8, m_per, ncol), x_shard.dtype),
        in_specs=[pl.BlockSpec(memory_space=pltpu.VMEM)], out_specs=pl.BlockSpec(memory_space=pltpu.VMEM),
        scratch_shapes=[pltpu.SemaphoreType.DMA((7,)), pltpu.SemaphoreType.DMA((7,)), pltpu.SemaphoreType.DMA],
        compiler_params=pltpu.CompilerParams(vmem_limit_bytes=VMEM_LIMIT), name=name)(x_shard)


def _sum_devices(g8, name):
    _, r, c = g8.shape
    tr = _div_tile(r, c * 4 * 8)

    def body(g_ref, o_ref):
        acc = g_ref[0]
        for dev in range(1, 8):
            acc = acc + g_ref[dev]
        o_ref[...] = acc

    return pl.pallas_call(
        body, out_shape=jax.ShapeDtypeStruct((r, c), F32), grid=(r // tr,),
        in_specs=[pl.BlockSpec((8, tr, c), lambda i: (0, i, 0))], out_specs=pl.BlockSpec((tr, c), lambda i: (i, 0)),
        compiler_params=_cparams(("parallel",)), name=name)(g8)


def _exchange_sibling_halves(arrs, name):
    n = len(arrs)

    def body(*refs):
        ins, outs = refs[:n], refs[n:2 * n]
        send_sems, recv_sems = refs[2 * n:]
        x, y, c = _mesh_pos()
        cps = []
        for k in range(n):
            cp = pltpu.make_async_remote_copy(
                src_ref=_row_half(ins[k], 1 - c, 1), dst_ref=outs[k], send_sem=send_sems.at[k],
                recv_sem=recv_sems.at[k], device_id=(x, y, 1 - c), device_id_type=MESH)
            cp.start()
            cps.append(cp)
        for cp in cps:
            cp.wait()

    return pl.pallas_call(
        body,
        out_shape=[jax.ShapeDtypeStruct((a.shape[0], a.shape[1] // 2, a.shape[2]), a.dtype) for a in arrs],
        in_specs=[HBM_SPEC] * n, out_specs=[HBM_SPEC] * n,
        scratch_shapes=[pltpu.SemaphoreType.DMA((n,)), pltpu.SemaphoreType.DMA((n,))], name=name)(*arrs)


def _chip_partials(arrs, recvs, pos, name):
    n = len(arrs)

    def body(pos_ref, *refs):
        for a_ref, b_ref, o_ref in zip(refs[:n], refs[n:2 * n], refs[2 * n:]):
            o_ref[...] = (a_ref[...] + b_ref[...]).astype(BF16)

    own_specs, recv_specs, shapes = [], [], []
    for arr in arrs:
        nslab, r, c = arr.shape
        own_specs.append(pl.BlockSpec((None, r // 2, c), lambda j, p: (j, p[0], 0)))
        recv_specs.append(pl.BlockSpec((None, r // 2, c), lambda j, p: (j, 0, 0)))
        shapes.append(jax.ShapeDtypeStruct((nslab, r // 2, c), BF16))
    grid_spec = pltpu.PrefetchScalarGridSpec(
        num_scalar_prefetch=1, grid=(N_CHIPS,), in_specs=own_specs + recv_specs, out_specs=recv_specs)
    return pl.pallas_call(
        body, out_shape=shapes, grid_spec=grid_spec,
        compiler_params=_cparams(("parallel",)), name=name)(pos, *arrs, *recvs)


def _chip_exchange_comm(parts, tag):
    n = len(parts)

    def copies(ins, outs, sems):
        send_sems, recv_sems = sems
        x, y, c = _mesh_pos()
        chips = [(1 - x, y), (x, 1 - y), (1 - x, 1 - y)]
        return [pltpu.make_async_remote_copy(
            src_ref=ins[k].at[2 * chip[0] + chip[1]], dst_ref=outs[k].at[j],
            send_sem=send_sems.at[3 * k + j], recv_sem=recv_sems.at[3 * k + j],
            device_id=(*chip, c), device_id_type=MESH) for k in range(n) for j, chip in enumerate(chips)]

    def start(ins, outs, sems):
        for cp in copies(ins, outs, sems):
            cp.start()

    def finish(ins, outs, sems):
        for cp in copies(ins, outs, sems):
            cp.wait()

    return _Comm(
        tag, parts, [jax.ShapeDtypeStruct((3,) + p.shape[1:], p.dtype) for p in parts],
        [pltpu.SemaphoreType.DMA((3 * n,)), pltpu.SemaphoreType.DMA((3 * n,))], start, None, finish)


def _reduce_halves(arrs, recvs, gots, pos, name):
    n = len(arrs)

    def body(pos_ref, *refs):
        for a_ref, b_ref, g_ref, o_ref in zip(refs[:n], refs[n:2 * n], refs[2 * n:3 * n], refs[3 * n:]):
            acc = a_ref[...] + b_ref[...]
            for j in range(3):
                acc = acc + g_ref[j].astype(F32)
            o_ref[...] = acc

    own_specs, recv_specs, got_specs, out_specs, shapes = [], [], [], [], []
    for arr in arrs:
        _, r, c = arr.shape
        own_specs.append(pl.BlockSpec((None, r // 2, c), lambda i, p: (p[1], p[0], 0)))
        recv_specs.append(pl.BlockSpec((None, r // 2, c), lambda i, p: (p[1], 0, 0)))
        got_specs.append(pl.BlockSpec((3, r // 2, c), lambda i, p: (0, 0, 0)))
        out_specs.append(pl.BlockSpec((r // 2, c), lambda i, p: (0, 0)))
        shapes.append(jax.ShapeDtypeStruct((r // 2, c), F32))
    grid_spec = pltpu.PrefetchScalarGridSpec(
        num_scalar_prefetch=1, grid=(1,), in_specs=own_specs + recv_specs + got_specs, out_specs=out_specs)
    return pl.pallas_call(
        body, out_shape=shapes, grid_spec=grid_spec,
        compiler_params=_cparams(("arbitrary",)), name=name)(pos, *arrs, *recvs, *gots)


def _share_halves(halves, name):
    n = len(halves)

    def body(*refs):
        ins, outs = refs[:n], refs[n:2 * n]
        send_sems, recv_sems = refs[2 * n:]
        x, y, c = _mesh_pos()
        cps = []
        for k in range(n):
            cp = pltpu.make_async_remote_copy(
                src_ref=ins[k], dst_ref=outs[k], send_sem=send_sems.at[k], recv_sem=recv_sems.at[k],
                device_id=(x, y, 1 - c), device_id_type=MESH)
            cp.start()
            cps.append(cp)
        for cp in cps:
            cp.wait()

    return pl.pallas_call(
        body, out_shape=[jax.ShapeDtypeStruct(h.shape, h.dtype) for h in halves],
        in_specs=[HBM_SPEC] * n, out_specs=[HBM_SPEC] * n,
        scratch_shapes=[pltpu.SemaphoreType.DMA((n,)), pltpu.SemaphoreType.DMA((n,))], name=name)(*halves)


class _Reduction:
    def __init__(self, arrs, pos, tag):
        self.arrs, self.pos, self.tag = arrs, pos, tag
        self.recv = _exchange_sibling_halves(arrs, "rs_sibling_" + tag)
        self.parts = _chip_partials(arrs, self.recv, pos, "rs_partial_" + tag)
        self.got = None

    def comm(self):
        return _chip_exchange_comm(self.parts, "rs_" + self.tag)

    def end(self):
        if self.got is None:
            self.got = _run_comm(self.comm(), "rs_chips_" + self.tag)
        halves = _reduce_halves(self.arrs, self.recv, self.got, self.pos, "rs_reduce_" + self.tag)
        return halves, _share_halves(halves, "rs_share_" + self.tag)


def _w_in_full(p, l, ws):
    if "w_in" not in p:
        slabs = ws.get((l, "w_in"))
        p["w_in"] = jnp.concatenate([slabs[j] for j in range(N_CHIPS)], axis=1)
    return p["w_in"]


def _w3t_full(p, l, ws):
    if "w3t" not in p:
        slabs = ws.get((l, "w3"))
        p["w3t"] = jnp.swapaxes(slabs, 0, 1).reshape(slabs.shape[1], -1, slabs.shape[3])
    return p["w3t"]


def _w_out_full(l, ws):
    slabs = ws.get((l, "w_out"))
    return slabs.reshape(-1, slabs.shape[2])


def _layer_fwd(h, l, p, ws, tabs):
    def hosted(host, fn, *args):
        out, got = fn(*args, ws.comm_for(host))
        ws.deposit(got)
        return out

    ffn1_saved = hosted("ffn_up", _ffn_up, h, p["ffn1_norm"], ws.get((l, "wg1")), ws.get((l, "wu1")))
    h1 = hosted("ffn_down", _ffn_down, ffn1_saved[2], ws.get((l, "wd1")), h)
    n = _rms_fwd(h1, p["mix_norm"], "rms_fwd_mix")
    ssm_w = p["ssm_d"].shape[0]
    q, k, v, u, ga, gs = hosted("in_proj", _in_proj, n, _w_in_full(p, l, ws), tabs, ssm_w)
    o = hosted("attn_fwd", _attn_fwd, q, k, v, p["attn_sinks"])
    y, yg = hosted("ssm_fwd", _ssm_fwd, u, *p["ssm_tabs"], p["ssm_d"])
    merged = hosted("merge_fwd", _merge_fwd, o, yg, ga, gs, _w3t_full(p, l, ws))
    h2 = hosted("out_proj", _out_proj, merged, _w_out_full(l, ws), h1)
    ffn2_saved = hosted("ffn_up", _ffn_up, h2, p["ffn2_norm"], ws.get((l, "wg2")), ws.get((l, "wu2")))
    h3 = hosted("ffn_down", _ffn_down, ffn2_saved[2], ws.get((l, "wd2")), h2)
    saved = dict(h0=h, h1=h1, h2=h2, ffn1=ffn1_saved, ffn2=ffn2_saved, n_mix=n, q=q, k=k, v=v, u=u, ga=ga, gs=gs,
                 o=o, y=y, yg=yg, merged=merged)
    return h3, saved


def _layer_bwd(dh_pair, l, p, ws, s, tabs, pos):
    g = {}
    (dh2, dhb), g["ffn2_norm"], red_ffn2, _ = _ffn_bwd(
        dh_pair, s["h2"], p["ffn2_norm"], ws.get((l, "wg2")), ws.get((l, "wu2")), ws.get((l, "wd2")), p["f4"],
        s["ffn2"], pos)
    w3, w_out_w = _w3t_full(p, l, ws), _w_out_full(l, ws)
    lp, d = dh2.shape
    d4 = d // N_CHIPS
    dw_out = _tn_full(s["merged"], dhb, "mix_dw_out").reshape(N_CHIPS, d4, d)
    dattn, dv, dg, dga, dgs = _merge_bwd(dhb, w_out_w, s["o"], s["yg"], s["ga"], s["gs"], w3)
    (dw_ap,) = _tn_cols(s["o"], [dattn], "mix_dw_ap")
    dw_gv, dw_gg = _tn_cols(s["yg"], [dv, dg], "mix_dw_glu")
    do, dy = _branch_bwd(dattn, dv, dg, w3, s["y"])
    (dq, dk, dvv, dkm, dvm, dsink), _ = _attn_bwd(s["q"], s["k"], s["v"], do, p["attn_sinks"], tabs)
    g["attn_sinks"] = dsink[:, 0]
    (du, dlr, dli, dbr, dbi, dcr, dci, dd), _ = _ssm_bwd(s["u"], dy, *p["ssm_tabs"], p["ssm_d"])
    ngrp = p["ssm_d"].shape[0] // SSM_GROUP
    g["ssm_lam"] = (dlr.reshape(ngrp, SSM_STATE), dli.reshape(ngrp, SSM_STATE),
                    _ssm_untable_b(dbr, ngrp), _ssm_untable_b(dbi, ngrp))
    g["ssm_c_re"] = _ssm_untable_c(dcr, ngrp)
    g["ssm_c_im"] = _ssm_untable_c(dci, ngrp)
    g["ssm_d"] = dd[0]
    dk = dk.at[:BLOCK].add(dkm)
    dvv = dvv.at[:BLOCK].add(dvm)
    dz = jnp.concatenate([dq.astype(BF16), dk.astype(BF16), dvv.astype(BF16), du.astype(BF16), dga, dgs], axis=1)
    n = s["n_mix"]
    w_in = _w_in_full(p, l, ws)
    inw = w_in.shape[1]
    dw_in = _tn_full(dz, n, "mix_dw_in", d // 2).reshape(N_CHIPS, inw // N_CHIPS, d)
    red_mix = _Reduction([dw_in, dw_ap, dw_gv, dw_gg, dw_out], pos, "mix")
    dh1, g["mix_norm"], dh1b = _in_proj_bwd(dz, w_in, dh2, s["h1"], p["mix_norm"])
    dh0_pair, g["ffn1_norm"], red_ffn1, red_mix.got = _ffn_bwd(
        (dh1, dh1b), s["h0"], p["ffn1_norm"], ws.get((l, "wg1")), ws.get((l, "wu1")), ws.get((l, "wd1")), p["f4"],
        s["ffn1"], pos, red_mix.comm())
    return dh0_pair, g, [*red_ffn1, red_mix, *red_ffn2]


BIG = ["ffn1_w_gate", "ffn1_w_up", "ffn1_w_down", "w_in", "w_attn_proj", "w_glu_v", "w_glu_g", "w_out",
       "ffn2_w_gate", "ffn2_w_up", "ffn2_w_down"]
TRANSPOSED = ["ffn1_w_gate", "ffn1_w_up", "w_in", "ffn2_w_gate", "ffn2_w_up"]
SMALL = ["ffn1_norm", "mix_norm", "attn_sinks", "ssm_a_re", "ssm_a_im", "ssm_log_dt", "ssm_b_re", "ssm_b_im",
         "ssm_c_re", "ssm_c_im", "ssm_d", "ffn2_norm", "final_norm"]
WEIGHTS = ["meta_tokens", "ffn1_norm", "ffn1_w_gate", "ffn1_w_up", "ffn1_w_down", "mix_norm", "w_in", "attn_sinks",
           "ssm_a_re", "ssm_a_im", "ssm_log_dt", "ssm_b_re", "ssm_b_im", "ssm_c_re", "ssm_c_im", "ssm_d",
           "w_attn_proj", "w_glu_v", "w_glu_g", "w_out", "ffn2_norm", "ffn2_w_gate", "ffn2_w_up", "ffn2_w_down",
           "final_norm"]


def _small_rows(shape):
    rows = -(-math.prod(shape) // LANES)
    return -(-rows // SUBLANES) * SUBLANES


def _pack_small(tree):
    parts = []
    for k in SMALL + ["meta_tokens"]:
        size, rows = math.prod(tree[k].shape), _small_rows(tree[k].shape)
        if size % LANES == 0:
            part = tree[k].reshape(size // LANES, LANES)
        else:
            part = jnp.pad(tree[k].reshape(1, size), ((0, 0), (0, LANES - size)))
        parts.append(jnp.pad(part, ((0, rows - part.shape[0]), (0, 0))))
    return jnp.concatenate(parts, axis=0)


def _unpack_small(packed, like):
    out, off = {}, 0
    for k in SMALL + ["meta_tokens"]:
        size, rows = math.prod(like[k].shape), _small_rows(like[k].shape)
        if size % LANES == 0:
            out[k] = packed[off:off + size // LANES].reshape(like[k].shape)
        else:
            out[k] = packed[off, :size].reshape(like[k].shape)
        off += rows
    return out


def kernel(x, meta_tokens, ffn1_norm, ffn1_w_gate, ffn1_w_up, ffn1_w_down, mix_norm, w_in, attn_sinks, ssm_a_re, ssm_a_im, ssm_log_dt, ssm_b_re, ssm_b_im, ssm_c_re, ssm_c_im, ssm_d, w_attn_proj, w_glu_v, w_glu_g, w_out, ffn2_norm, ffn2_w_gate, ffn2_w_up, ffn2_w_down, final_norm, loss_target, m_meta_tokens, m_ffn1_norm, m_ffn1_w_gate, m_ffn1_w_up, m_ffn1_w_down, m_mix_norm, m_w_in, m_attn_sinks, m_ssm_a_re, m_ssm_a_im, m_ssm_log_dt, m_ssm_b_re, m_ssm_b_im, m_ssm_c_re, m_ssm_c_im, m_ssm_d, m_w_attn_proj, m_w_glu_v, m_w_glu_g, m_w_out, m_ffn2_norm, m_ffn2_w_gate, m_ffn2_w_up, m_ffn2_w_down, m_final_norm, v_meta_tokens, v_ffn1_norm, v_ffn1_w_gate, v_ffn1_w_up, v_ffn1_w_down, v_mix_norm, v_w_in, v_attn_sinks, v_ssm_a_re, v_ssm_a_im, v_ssm_log_dt, v_ssm_b_re, v_ssm_b_im, v_ssm_c_re, v_ssm_c_im, v_ssm_d, v_w_attn_proj, v_w_glu_v, v_w_glu_g, v_w_out, v_ffn2_norm, v_ffn2_w_gate, v_ffn2_w_up, v_ffn2_w_down, v_final_norm):
    args = dict(locals())
    w = {k: args[k] for k in WEIGHTS}
    m = {k: args["m_" + k] for k in WEIGHTS}
    v = {k: args["v_" + k] for k in WEIGHTS}
    depth = ffn1_norm.shape[0]
    seq, d = x.shape[1], x.shape[2]
    lp = seq + BLOCK
    xi, yi, ci = _mesh_pos()
    pos = jnp.stack([ci, 2 * xi + yi]).astype(jnp.int32)

    tabs = _rope_tables(lp)
    (meta_all,) = _all_gather_chips([meta_tokens], "gather_meta")
    meta_full = jnp.concatenate([meta_all[j] for j in range(N_CHIPS)], axis=1)
    layers, pieces = [], []
    f4 = ffn1_w_gate.shape[2]
    fp = -(-f4 // MXU_DIM) * MXU_DIM

    def ffn_rows(wt):
        return jnp.pad(wt, ((0, fp - f4), (0, 0))).astype(BF16)

    for l in range(depth):
        pieces += [
            ((l, "wg1"), ffn_rows(ffn1_w_gate[l].T)), ((l, "wu1"), ffn_rows(ffn1_w_up[l].T)),
            ((l, "wd1"), ffn_rows(ffn1_w_down[l])), ((l, "w_in"), w_in[l].astype(BF16)),
            ((l, "w3"), jnp.stack([w_attn_proj[l].T, w_glu_v[l].T, w_glu_g[l].T]).astype(BF16)),
            ((l, "w_out"), w_out[l].astype(BF16)),
            ((l, "wg2"), ffn_rows(ffn2_w_gate[l].T)), ((l, "wu2"), ffn_rows(ffn2_w_up[l].T)),
            ((l, "wd2"), ffn_rows(ffn2_w_down[l]))]
        lb_re, lb_im, bb_re, bb_im = _ssm_params(ssm_a_re[l], ssm_a_im[l], ssm_log_dt[l], ssm_b_re[l], ssm_b_im[l])
        ngrp = lb_re.shape[0]
        nt = ngrp // GROUPS_PER_TILE
        ssm_tabs = (lb_re.reshape(nt, 1, TILE_STATES), lb_im.reshape(nt, 1, TILE_STATES),
                    *_ssm_tables(bb_re, bb_im, ssm_c_re[l], ssm_c_im[l]))
        layers.append(dict(
            ffn1_norm=ffn1_norm[l], mix_norm=mix_norm[l], ffn2_norm=ffn2_norm[l], attn_sinks=attn_sinks[l],
            ssm_d=ssm_d[l], ssm_tabs=ssm_tabs, f4=f4))
    ws = _WeightStream(pieces)
    ws.get((0, "wu1"))

    h = jnp.concatenate([jnp.zeros((PAD_FRONT, d), F32), meta_full, x[0]], axis=0)
    saved = []
    for l in range(depth):
        h, s = _layer_fwd(h, l, layers[l], ws, tabs)
        saved.append(s)
    dh, g_final, loss_acc, dhb = _loss_head(h, final_norm, loss_target[0])
    dh_pair = (dh, dhb)
    loss = lax.psum(loss_acc[0, 0], ("x", "y", "c"))

    grads, reds = [None] * depth, [None] * depth
    for l in reversed(range(depth)):
        dh_pair, grads[l], reds[l] = _layer_bwd(dh_pair, l, layers[l], ws, saved[l], tabs, pos)
    dh = dh_pair[0]
    grad_x = dh[BLOCK:][None]
    dmeta_local = dh[PAD_FRONT:BLOCK]

    small = {k: [] for k in SMALL}
    for l in range(depth):
        gl = grads[l]
        _, vjp = jax.vjp(_ssm_params, ssm_a_re[l], ssm_a_im[l], ssm_log_dt[l], ssm_b_re[l], ssm_b_im[l])
        da_re, da_im, dlog_dt, db_re, db_im = vjp(gl["ssm_lam"])
        for k, val in (("ffn1_norm", gl["ffn1_norm"][0]), ("mix_norm", gl["mix_norm"][0]),
                       ("attn_sinks", gl["attn_sinks"]), ("ssm_a_re", da_re), ("ssm_a_im", da_im),
                       ("ssm_log_dt", dlog_dt), ("ssm_b_re", db_re), ("ssm_b_im", db_im),
                       ("ssm_c_re", gl["ssm_c_re"]), ("ssm_c_im", gl["ssm_c_im"]), ("ssm_d", gl["ssm_d"]),
                       ("ffn2_norm", gl["ffn2_norm"][0])):
            small[k].append(val)
    small_local = {k: jnp.stack(vals) for k, vals in small.items() if k != "final_norm"}
    small_local["final_norm"] = g_final[0]
    small_local["meta_tokens"] = dmeta_local
    like = dict(small_local)
    g_small = _sum_devices(_all_gather_devices(_pack_small(small_local), "gather_small_grads"), "sum_small_grads")
    g_small_tree = _unpack_small(g_small, like)
    d4 = d // N_CHIPS
    chip = 2 * xi + yi
    g_meta = lax.dynamic_slice_in_dim(g_small_tree["meta_tokens"], chip * d4, d4, axis=1)

    reduced = []
    for l in range(depth):
        mine, other = [], []
        for red in reds[l]:
            halves, sibling_halves = red.end()
            mine += halves
            other += sibling_halves
        reduced.append((mine, other))

    g_out, delta, new_m, new_v = {}, {}, {}, {}
    for i, k in enumerate(BIG):
        flip = (lambda t: jnp.swapaxes(t, 1, 2)) if k in TRANSPOSED else (lambda t: t)
        outs = _adamw_layers(
            flip(w[k]), flip(m[k]), flip(v[k]), [reduced[l][0][i] for l in range(depth)],
            [reduced[l][1][i] for l in range(depth)], pos, "adamw_" + k)
        g_out[k], delta[k], new_m[k], new_v[k] = [flip(t) for t in outs]
    small_names = SMALL + ["meta_tokens"]
    w_small = {k: w[k] for k in small_names}
    m_small = {k: m[k] for k in small_names}
    v_small = {k: v[k] for k in small_names}
    g_small_local = dict(g_small_tree)
    g_small_local["meta_tokens"] = g_meta
    d_s, m_s, v_s = _adamw_flat(_pack_small(w_small), _pack_small(g_small_local), _pack_small(m_small),
                                _pack_small(v_small), "adamw_small")
    for tree, packed in ((delta, d_s), (new_m, m_s), (new_v, v_s)):
        tree.update(_unpack_small(packed, w_small))
    for k in small_names:
        g_out[k] = g_small_local[k]

    return (loss, grad_x, *[g_out[k] for k in WEIGHTS], *[delta[k] for k in WEIGHTS],
            *[new_m[k] for k in WEIGHTS], *[new_v[k] for k in WEIGHTS])
```

```python
import functools
import math

import jax
import jax.numpy as jnp
from jax import lax
from jax.experimental import pallas as pl
from jax.experimental.pallas import tpu as pltpu

F32 = jnp.float32
BF16 = jnp.bfloat16

N_META = 16
HEAD_DIM = 64
N_Q_HEADS = 8
N_KV_HEADS = 2
Q_PER_KV = N_Q_HEADS // N_KV_HEADS
ATTN_WIDTH = N_Q_HEADS * HEAD_DIM
KV_WIDTH = N_KV_HEADS * HEAD_DIM
BLOCK = 128
PAD_FRONT = BLOCK - N_META
ROPE_THETA = 500000.0
ROT_DIM = HEAD_DIM // 4
SSM_GROUP = 16
SSM_STATE = 64
GROUPS_PER_TILE = 4
TILE_STATES = GROUPS_PER_TILE * SSM_STATE
LANES = 128
SUBLANES = 8
MXU_DIM = 256
EPS = 1e-6
NEG_INF = -1e30
N_CHIPS = 4

ADAM_LR = 0.001
ADAM_B1 = 0.9
ADAM_B2 = 0.999
ADAM_EPS = 1e-08
ADAM_WD = 0.01
ADAM_STEP = 10

VMEM_LIMIT = 56 * 1024 * 1024
MESH = pl.DeviceIdType.MESH


def _cparams(sem=None):
    return pltpu.CompilerParams(dimension_semantics=sem, vmem_limit_bytes=VMEM_LIMIT)


def _row_tile(rows, limit=512):
    best = None
    for t in range(128, limit + 1, 128):
        if rows % t == 0:
            best = t
    assert best is not None, rows
    return best


def _div_tile(rows, row_bytes, max_bytes=1 << 20, mult=8):
    best = None
    for t in range(mult, rows + 1, mult):
        if rows % t == 0 and t * row_bytes <= max_bytes:
            best = t
    if best is None:
        best = rows
    return best


def _dot(a, b, mode):
    if mode == "nn":
        dims = (((1,), (0,)), ((), ()))
    elif mode == "nt":
        dims = (((1,), (1,)), ((), ()))
    else:
        dims = (((0,), (0,)), ((), ()))
    return lax.dot_general(a.astype(BF16), b.astype(BF16), dims, preferred_element_type=F32)


def _sigmoid(x):
    return 1.0 / (1.0 + jnp.exp(-x))


_GELU_C = math.sqrt(2.0 / math.pi)


def _gelu(x):
    return 0.5 * x * (1.0 + jnp.tanh(_GELU_C * (x + 0.044715 * x * x * x)))


def _gelu_grad(x):
    t = jnp.tanh(_GELU_C * (x + 0.044715 * x * x * x))
    return 0.5 * (1.0 + t) + 0.5 * x * (1.0 - t * t) * _GELU_C * (1.0 + 3.0 * 0.044715 * x * x)


class _Comm:
    def __init__(self, tag, ins, out_shapes, sems, start, mid, finish):
        self.tag, self.ins, self.out_shapes, self.sems = tag, list(ins), list(out_shapes), list(sems)
        self.start, self.mid, self.finish = start, mid, finish


HBM_SPEC = pl.BlockSpec(memory_space=pltpu.HBM)


def _hosted_call(body, comm, *, out_shape, grid, in_specs, out_specs, scratch_shapes, sem, name, args):
    out_shape, in_specs, out_specs = list(out_shape), list(in_specs), list(out_specs)
    scratch_shapes = list(scratch_shapes)
    if comm is None:
        res = pl.pallas_call(
            body, out_shape=out_shape, grid=grid, in_specs=in_specs, out_specs=out_specs,
            scratch_shapes=scratch_shapes, compiler_params=_cparams(sem), name=name)(*args)
        return list(res), []
    n_in, n_out, n_sc = len(args), len(out_shape), len(scratch_shapes)
    nci, nco = len(comm.ins), len(comm.out_shapes)
    total = math.prod(grid)

    def wrapped(*refs):
        in_refs, cin = refs[:n_in], refs[n_in:n_in + nci]
        o0 = n_in + nci
        out_refs, cout = refs[o0:o0 + n_out], refs[o0 + n_out:o0 + n_out + nco]
        s0 = o0 + n_out + nco
        sc, csem = refs[s0:s0 + n_sc], refs[s0 + n_sc:]
        lin = 0
        for dim, size in enumerate(grid):
            lin = lin * size + pl.program_id(dim)

        @pl.when(lin == 0)
        def _():
            comm.start(cin, cout, csem)

        if comm.mid is not None:
            @pl.when(lin == total // 2)
            def _():
                comm.mid(cin, cout, csem)

        body(*in_refs, *out_refs, *sc)

        @pl.when(lin == total - 1)
        def _():
            comm.finish(cin, cout, csem)

    res = pl.pallas_call(
        wrapped, out_shape=out_shape + comm.out_shapes, grid=grid,
        in_specs=in_specs + [HBM_SPEC] * nci, out_specs=out_specs + [HBM_SPEC] * nco,
        scratch_shapes=scratch_shapes + comm.sems,
        compiler_params=_cparams(("arbitrary",) * len(grid)), name=name + "_" + comm.tag)(*args, *comm.ins)
    return list(res[:n_out]), list(res[n_out:])


def _matmul(name, grid, k_axis, ins, in_specs, pairs, acc_shapes, epilogue, out_shapes, out_specs, sem, comm=None):
    n_in, n_out, n_acc = len(ins), len(out_shapes), len(acc_shapes)

    def body(*refs):
        in_refs = refs[:n_in]
        out_refs = refs[n_in:n_in + n_out]
        acc_refs = refs[n_in + n_out:]
        if k_axis is None:
            accs = [None] * n_acc
            for ia, ib, mode, iacc in pairs:
                d = _dot(in_refs[ia][...], in_refs[ib][...], mode)
                accs[iacc] = d if accs[iacc] is None else accs[iacc] + d
            epilogue(accs, in_refs, out_refs)
            return
        k = pl.program_id(k_axis)

        @pl.when(k == 0)
        def _():
            for r in acc_refs:
                r[...] = jnp.zeros_like(r)

        for ia, ib, mode, iacc in pairs:
            acc_refs[iacc][...] += _dot(in_refs[ia][...], in_refs[ib][...], mode)

        @pl.when(k == pl.num_programs(k_axis) - 1)
        def _():
            epilogue([r[...] for r in acc_refs], in_refs, out_refs)

    scratch = [] if k_axis is None else [pltpu.VMEM(s, F32) for s in acc_shapes]
    outs, couts = _hosted_call(
        body, comm, out_shape=out_shapes, grid=grid, in_specs=in_specs, out_specs=out_specs,
        scratch_shapes=scratch, sem=sem, name=name, args=ins)
    return outs if comm is None else (outs, couts)


def _rms_fwd(h, g, name):
    lp, d = h.shape
    tm = _row_tile(lp)

    def body(h_ref, g_ref, n_ref):
        x = h_ref[...]
        r = lax.rsqrt(jnp.mean(x * x, axis=-1, keepdims=True) + EPS)
        n_ref[...] = (x * r * g_ref[...]).astype(BF16)

    return pl.pallas_call(
        body, out_shape=jax.ShapeDtypeStruct((lp, d), BF16), grid=(lp // tm,),
        in_specs=[pl.BlockSpec((tm, d), lambda i: (i, 0)), pl.BlockSpec((1, d), lambda i: (0, 0))],
        out_specs=pl.BlockSpec((tm, d), lambda i: (i, 0)),
        compiler_params=_cparams(("parallel",)), name=name)(h, g.reshape(1, d))


def _rms_bwd_math(dn, x, g):
    r = lax.rsqrt(jnp.mean(x * x, axis=-1, keepdims=True) + EPS)
    xh = x * r
    dxh = dn * g
    dx = r * (dxh - xh * jnp.mean(dxh * xh, axis=-1, keepdims=True))
    return dx, dn * xh


def _scale_cast(x, scale, name):
    lp, d = x.shape
    tm = _row_tile(lp)

    def body(x_ref, o_ref):
        o_ref[...] = (x_ref[...] * scale).astype(BF16)

    return pl.pallas_call(
        body, out_shape=jax.ShapeDtypeStruct((lp, d), BF16), grid=(lp // tm,),
        in_specs=[pl.BlockSpec((tm, d), lambda i: (i, 0))], out_specs=pl.BlockSpec((tm, d), lambda i: (i, 0)),
        compiler_params=_cparams(("parallel",)), name=name)(x)


def _ffn_up(h, gain, wgt, wut, comm=None):
    lp, d = h.shape
    fp = wgt.shape[1]
    tm = _row_tile(lp)
    n = _rms_fwd(h, gain, "rms_fwd_ffn")

    def up_epi(accs, in_refs, out_refs):
        a, b = accs
        out_refs[0][...] = a.astype(BF16)
        out_refs[1][...] = b.astype(BF16)
        out_refs[2][...] = (a * _sigmoid(a) * b).astype(BF16)

    act = jax.ShapeDtypeStruct((lp, N_CHIPS * fp), BF16)
    w_spec = pl.BlockSpec((None, fp, d), lambda j, i: (j, 0, 0))
    res = _matmul(
        "ffn_up", (N_CHIPS, lp // tm), None, [n, wgt, wut],
        [pl.BlockSpec((tm, d), lambda j, i: (i, 0)), w_spec, w_spec],
        [(0, 1, "nt", 0), (0, 2, "nt", 1)], [(tm, fp)] * 2, up_epi,
        [act, act, act], [pl.BlockSpec((tm, fp), lambda j, i: (i, j))] * 3,
        ("parallel", "parallel"), comm)
    outs, couts = (res, []) if comm is None else res
    return (*outs, n), couts


def _ffn_down(s, wd, h, comm=None):
    lp, d = h.shape
    ff = s.shape[1]
    tm = _row_tile(lp)

    def down_epi(accs, in_refs, out_refs):
        out_refs[0][...] = in_refs[2][...] + 0.5 * accs[0]

    res = _matmul(
        "ffn_down", (lp // tm,), None, [s, wd.reshape(ff, d), h],
        [pl.BlockSpec((tm, ff), lambda i: (i, 0)), pl.BlockSpec((ff, d), lambda i: (0, 0)),
         pl.BlockSpec((tm, d), lambda i: (i, 0))],
        [(0, 1, "nn", 0)], [(tm, d)], down_epi,
        [jax.ShapeDtypeStruct((lp, d), F32)], [pl.BlockSpec((tm, d), lambda i: (i, 0))],
        ("parallel",), comm)
    return (res[0], []) if comm is None else (res[0][0], res[1])


def _tn_tiles(lp):
    return _row_tile(lp, 1408)


def _ffn_bwd(dh_pair, h_in, gain, wgt, wut, wd, f4, saved, pos, comm=None):
    dh, dhb = dh_pair
    a, b, s, n = saved
    lp, d = h_in.shape
    fp = wgt.shape[1]
    ff = N_CHIPS * fp
    tm = _row_tile(lp)
    ni = lp // tm
    tk = _tn_tiles(lp)
    nk = lp // tk

    def ds_epi(accs, in_refs, out_refs):
        ds = 0.5 * accs[0]
        av = in_refs[2][...].astype(F32)
        bv = in_refs[3][...].astype(F32)
        sg = _sigmoid(av)
        out_refs[0][...] = (ds * bv * sg * (1.0 + av * (1.0 - sg))).astype(BF16)
        out_refs[1][...] = (ds * av * sg).astype(BF16)

    act = jax.ShapeDtypeStruct((lp, ff), BF16)
    col_spec = pl.BlockSpec((tm, fp), lambda j, i: (i, j))
    res = _matmul(
        "ffn_bwd_ds", (N_CHIPS, ni), None, [dhb, wd, a, b],
        [pl.BlockSpec((tm, d), lambda j, i: (i, 0)), pl.BlockSpec((None, fp, d), lambda j, i: (j, 0, 0)),
         col_spec, col_spec],
        [(0, 1, "nt", 0)], [(tm, fp)], ds_epi, [act, act], [col_spec, col_spec], ("parallel", "parallel"),
        comm)
    (da, db), couts = (res, []) if comm is None else res

    dw_shape = jax.ShapeDtypeStruct((N_CHIPS, f4, d), F32)
    dw_spec = pl.BlockSpec((None, f4, d), lambda j, k: (j, 0, 0))
    in_col = pl.BlockSpec((tk, fp), lambda j, k: (k, j))
    in_row = pl.BlockSpec((tk, d), lambda j, k: (k, 0))

    def dwd_epi(accs, in_refs, out_refs):
        out_refs[0][...] = 0.5 * accs[0][:f4]

    (dwd,) = _matmul(
        "ffn_dwd", (N_CHIPS, nk), 1, [s, dhb], [in_col, in_row],
        [(0, 1, "tn", 0)], [(fp, d)], dwd_epi, [dw_shape], [dw_spec], ("parallel", "arbitrary"))

    def dwgu_epi(accs, in_refs, out_refs):
        for acc, o in zip(accs, out_refs):
            o[...] = acc[:f4]

    red_down = _Reduction([dwd], pos, "ffn_d")
    (dwg, dwu), red_down.got = _matmul(
        "ffn_dwgu", (N_CHIPS, nk), 1, [n, da, db], [in_row, in_col, in_col],
        [(1, 0, "tn", 0), (2, 0, "tn", 1)], [(fp, d)] * 2, dwgu_epi,
        [dw_shape, dw_shape], [dw_spec, dw_spec], ("parallel", "arbitrary"), red_down.comm())

    def dn_epi(accs, in_refs, out_refs):
        i = pl.program_id(0)
        dx, dgrow = _rms_bwd_math(accs[0], in_refs[5][...], in_refs[6][...])
        dh_new = in_refs[4][...] + dx
        out_refs[0][...] = dh_new
        out_refs[2][...] = dh_new.astype(BF16)

        @pl.when(i == 0)
        def _():
            out_refs[1][...] = jnp.zeros_like(out_refs[1])

        out_refs[1][...] += jnp.sum(dgrow, axis=0, keepdims=True)

    red = _Reduction([dwg, dwu], pos, "ffn_gu")
    row_spec = pl.BlockSpec((tm, d), lambda i: (i, 0))
    act_spec = pl.BlockSpec((tm, ff), lambda i: (i, 0))
    w_spec = pl.BlockSpec((ff, d), lambda i: (0, 0))
    one_spec = pl.BlockSpec((1, d), lambda i: (0, 0))
    (dh_in, dgain, dh_in_b), red.got = _matmul(
        "ffn_bwd_dn", (ni,), None, [da, wgt.reshape(ff, d), db, wut.reshape(ff, d), dh, h_in, gain.reshape(1, d)],
        [act_spec, w_spec, act_spec, w_spec, row_spec, row_spec, one_spec],
        [(0, 1, "nn", 0), (2, 3, "nn", 0)], [(tm, d)], dn_epi,
        [jax.ShapeDtypeStruct((lp, d), F32), jax.ShapeDtypeStruct((1, d), F32), jax.ShapeDtypeStruct((lp, d), BF16)],
        [row_spec, one_spec, row_spec], ("arbitrary",), red.comm())
    return (dh_in, dh_in_b), dgain, [red, red_down], couts


def _rope_tables(lp):
    pos = jnp.arange(lp, dtype=F32) - float(PAD_FRONT)
    inv_freq = ROPE_THETA ** (-jnp.arange(0, ROT_DIM, 2, dtype=F32) / ROT_DIM)
    ang = pos[:, None] * inv_freq[None, :]
    cos, sin = jnp.cos(ang), jnp.sin(ang)
    half = ROT_DIM // 2
    ones = jnp.ones((lp, HEAD_DIM - ROT_DIM), F32)
    zeros_h = jnp.zeros((lp, half), F32)
    zeros_r = jnp.zeros((lp, HEAD_DIM - ROT_DIM), F32)
    c = jnp.concatenate([cos, cos, ones], axis=1)
    s1 = jnp.concatenate([-sin, zeros_h, zeros_r], axis=1)
    s2 = jnp.concatenate([zeros_h, sin, zeros_r], axis=1)
    reps = LANES // HEAD_DIM
    return jnp.stack([jnp.tile(c, (1, reps)), jnp.tile(s1, (1, reps)), jnp.tile(s2, (1, reps))])


def _rope(x, c, s1, s2):
    half = ROT_DIM // 2
    outs = []
    for ch in range(x.shape[1] // LANES):
        xc = x[:, ch * LANES:(ch + 1) * LANES]
        outs.append(xc * c + pltpu.roll(xc, LANES - half, 1) * s1 + pltpu.roll(xc, half, 1) * s2)
    return outs[0] if len(outs) == 1 else jnp.concatenate(outs, axis=1)


def _rope_t(dy, c, s1, s2):
    half = ROT_DIM // 2
    outs = []
    for ch in range(dy.shape[1] // LANES):
        dc = dy[:, ch * LANES:(ch + 1) * LANES]
        outs.append(dc * c + pltpu.roll(dc * s1, half, 1) + pltpu.roll(dc * s2, LANES - half, 1))
    return outs[0] if len(outs) == 1 else jnp.concatenate(outs, axis=1)


def _in_proj(n, w_in, tabs, ssm_w, comm=None):
    lp, d = n.shape
    inw = w_in.shape[1]
    tm = _row_tile(lp)
    o1 = ATTN_WIDTH
    o2 = o1 + KV_WIDTH
    o3 = o2 + KV_WIDTH
    o4 = o3 + ssm_w
    o5 = o4 + d

    def epi(accs, in_refs, out_refs):
        z = accs[0]
        c, s1, s2 = in_refs[2][0], in_refs[2][1], in_refs[2][2]
        out_refs[0][...] = _rope(z[:, :o1], c, s1, s2).astype(BF16)
        out_refs[1][...] = _rope(z[:, o1:o2], c, s1, s2).astype(BF16)
        out_refs[2][...] = z[:, o2:o3].astype(BF16)
        out_refs[3][...] = z[:, o3:o4]
        out_refs[4][...] = z[:, o4:o5]
        out_refs[5][...] = z[:, o5:]

    def rs(w, dt):
        return jax.ShapeDtypeStruct((lp, w), dt), pl.BlockSpec((tm, w), lambda i: (i, 0))

    shapes, specs = zip(rs(o1, BF16), rs(KV_WIDTH, BF16), rs(KV_WIDTH, BF16), rs(ssm_w, F32), rs(d, F32), rs(d, F32))
    res = _matmul(
        "mix_in_proj", (lp // tm,), None, [n, w_in, tabs],
        [pl.BlockSpec((tm, d), lambda i: (i, 0)), pl.BlockSpec((d, inw), lambda i: (0, 0)),
         pl.BlockSpec((3, tm, LANES), lambda i: (0, i, 0))],
        [(0, 1, "nn", 0)], [(tm, inw)], epi, list(shapes), list(specs), ("parallel",), comm)
    return (res, []) if comm is None else res


def _attn_mask(b):
    rows = lax.broadcasted_iota(jnp.int32, (BLOCK, 3 * BLOCK), 0)
    cols = lax.broadcasted_iota(jnp.int32, (BLOCK, 3 * BLOCK), 1)
    qpos = b * BLOCK + rows - PAD_FRONT
    kpos = (b - 1) * BLOCK + cols - PAD_FRONT
    dist = qpos - kpos
    band = (cols < 2 * BLOCK) & (kpos >= N_META) & (dist >= 0) & (dist < BLOCK)
    mrow = cols - 2 * BLOCK
    meta = (mrow >= PAD_FRONT) & ((mrow - PAD_FRONT) <= qpos)
    return band | meta


def _attn_probs(qh, kk, mask, sink):
    s = _dot(qh, kk, "nt") * (HEAD_DIM ** -0.5)
    s = jnp.where(mask, s, NEG_INF)
    m = jnp.maximum(jnp.max(s, axis=-1, keepdims=True), sink)
    e = jnp.exp(s - m)
    es = jnp.exp(sink - m)
    z = jnp.sum(e, axis=-1, keepdims=True) + es
    inv = 1.0 / z
    return e * inv, es * inv


def _head(ref_or_val, h):
    return ref_or_val[:, h * HEAD_DIM:(h + 1) * HEAD_DIM]


def _attn_fwd(q, k, v, sinks, comm=None):
    lp = q.shape[0]
    nb = lp // BLOCK

    def body(sink_ref, q_ref, kp_ref, kc_ref, km_ref, vp_ref, vc_ref, vm_ref, o_ref):
        b = pl.program_id(0)
        mask = _attn_mask(b)
        for hk in range(N_KV_HEADS):
            kk = jnp.concatenate([_head(kp_ref, hk), _head(kc_ref, hk), _head(km_ref, hk)], axis=0)
            vv = jnp.concatenate([_head(vp_ref, hk), _head(vc_ref, hk), _head(vm_ref, hk)], axis=0)
            for g in range(Q_PER_KV):
                h = hk * Q_PER_KV + g
                p, _ = _attn_probs(_head(q_ref, h), kk, mask, sink_ref[h])
                o_ref[:, h * HEAD_DIM:(h + 1) * HEAD_DIM] = _dot(p, vv, "nn").astype(BF16)

    cur = lambda b: (b, 0)
    prev = lambda b: (jnp.maximum(b - 1, 0), 0)
    first = lambda b: (0, 0)
    kvs = lambda f: pl.BlockSpec((BLOCK, KV_WIDTH), f)
    (o,), couts = _hosted_call(
        body, comm, out_shape=[jax.ShapeDtypeStruct((lp, ATTN_WIDTH), BF16)], grid=(nb,),
        in_specs=[pl.BlockSpec(memory_space=pltpu.SMEM), pl.BlockSpec((BLOCK, ATTN_WIDTH), cur),
                  kvs(prev), kvs(cur), kvs(first), kvs(prev), kvs(cur), kvs(first)],
        out_specs=[pl.BlockSpec((BLOCK, ATTN_WIDTH), cur)], scratch_shapes=[],
        sem=("parallel",), name="attn_fwd", args=(sinks, q, k, k, k, v, v, v))
    return o, couts


def _attn_bwd(q, k, v, do, sinks, tabs, comm=None):
    lp = q.shape[0]
    nb = lp // BLOCK
    scale = HEAD_DIM ** -0.5

    def body(sink_ref, q_ref, do_ref, kp_ref, kc_ref, km_ref, vp_ref, vc_ref, vm_ref, tq_ref, tk_ref, t0_ref,
             dq_ref, dk_ref, dv_ref, dkm_ref, dvm_ref, dsink_ref,
             dq_s, dkk_s, dvv_s, ck_s, cv_s, mk_s, mv_s):
        b = pl.program_id(0)

        @pl.when(b == 0)
        def _():
            for r in (ck_s, cv_s, mk_s, mv_s, dsink_ref):
                r[...] = jnp.zeros_like(r)

        @pl.when(b < nb)
        def _():
            mask = _attn_mask(b)
            for hk in range(N_KV_HEADS):
                kk = jnp.concatenate([_head(kp_ref, hk), _head(kc_ref, hk), _head(km_ref, hk)], axis=0)
                vv = jnp.concatenate([_head(vp_ref, hk), _head(vc_ref, hk), _head(vm_ref, hk)], axis=0)
                dkk = jnp.zeros((3 * BLOCK, HEAD_DIM), F32)
                dvv = jnp.zeros((3 * BLOCK, HEAD_DIM), F32)
                for g in range(Q_PER_KV):
                    h = hk * Q_PER_KV + g
                    qh = _head(q_ref, h)
                    doh = _head(do_ref, h)
                    p, ps = _attn_probs(qh, kk, mask, sink_ref[h])
                    dp = _dot(doh, vv, "nt")
                    delta = jnp.sum(p * dp, axis=-1, keepdims=True)
                    ds = (p * (dp - delta)).astype(BF16)
                    dsink_ref[h:h + 1, :] += jnp.zeros((1, LANES), F32) - jnp.sum(ps * delta)
                    dq_s[:, h * HEAD_DIM:(h + 1) * HEAD_DIM] = _dot(ds, kk, "nn") * scale
                    dkk = dkk + _dot(ds, qh, "tn") * scale
                    dvv = dvv + _dot(p, doh, "tn")
                dkk_s[:, hk * HEAD_DIM:(hk + 1) * HEAD_DIM] = dkk
                dvv_s[:, hk * HEAD_DIM:(hk + 1) * HEAD_DIM] = dvv
            dq_ref[...] = _rope_t(dq_s[...], tq_ref[0], tq_ref[1], tq_ref[2])
            dk_ref[...] = _rope_t(ck_s[...] + dkk_s[0:BLOCK, :], tk_ref[0], tk_ref[1], tk_ref[2])
            dv_ref[...] = cv_s[...] + dvv_s[0:BLOCK, :]
            ck_s[...] = dkk_s[BLOCK:2 * BLOCK, :]
            cv_s[...] = dvv_s[BLOCK:2 * BLOCK, :]
            mk_s[...] += dkk_s[2 * BLOCK:, :]
            mv_s[...] += dvv_s[2 * BLOCK:, :]

        @pl.when(b == nb)
        def _():
            dk_ref[...] = _rope_t(ck_s[...], tk_ref[0], tk_ref[1], tk_ref[2])
            dv_ref[...] = cv_s[...]
            dkm_ref[...] = _rope_t(mk_s[...], t0_ref[0], t0_ref[1], t0_ref[2])
            dvm_ref[...] = mv_s[...]

    cur = lambda b: (jnp.minimum(b, nb - 1), 0)
    prev = lambda b: (jnp.clip(b - 1, 0, nb - 1), 0)
    first = lambda b: (0, 0)
    kvs = lambda f: pl.BlockSpec((BLOCK, KV_WIDTH), f)
    tab = lambda f: pl.BlockSpec((3, BLOCK, LANES), lambda b: (0,) + f(b)[:1] + (0,))
    kv_out = lambda b: (jnp.maximum(b - 1, 0), 0)
    return _hosted_call(
        body, comm,
        out_shape=[jax.ShapeDtypeStruct((lp, ATTN_WIDTH), F32), jax.ShapeDtypeStruct((lp, KV_WIDTH), F32),
                   jax.ShapeDtypeStruct((lp, KV_WIDTH), F32), jax.ShapeDtypeStruct((BLOCK, KV_WIDTH), F32),
                   jax.ShapeDtypeStruct((BLOCK, KV_WIDTH), F32), jax.ShapeDtypeStruct((N_Q_HEADS, LANES), F32)],
        grid=(nb + 1,),
        in_specs=[pl.BlockSpec(memory_space=pltpu.SMEM), pl.BlockSpec((BLOCK, ATTN_WIDTH), cur),
                  pl.BlockSpec((BLOCK, ATTN_WIDTH), cur),
                  kvs(prev), kvs(cur), kvs(first), kvs(prev), kvs(cur), kvs(first),
                  tab(cur), tab(kv_out), tab(first)],
        out_specs=[pl.BlockSpec((BLOCK, ATTN_WIDTH), cur), kvs(kv_out), kvs(kv_out), kvs(first), kvs(first),
                   pl.BlockSpec((N_Q_HEADS, LANES), first)],
        scratch_shapes=[pltpu.VMEM((BLOCK, ATTN_WIDTH), F32), pltpu.VMEM((3 * BLOCK, KV_WIDTH), F32),
                        pltpu.VMEM((3 * BLOCK, KV_WIDTH), F32), pltpu.VMEM((BLOCK, KV_WIDTH), F32),
                        pltpu.VMEM((BLOCK, KV_WIDTH), F32), pltpu.VMEM((BLOCK, KV_WIDTH), F32),
                        pltpu.VMEM((BLOCK, KV_WIDTH), F32)],
        sem=("arbitrary",), name="attn_bwd", args=(sinks, q, do, k, k, k, v, v, v, tabs, tabs, tabs))


def _cmul(ar, ai, br, bi):
    return ar * br - ai * bi, ar * bi + ai * br


def _cpow(lr, li, n):
    rr = ri = None
    br, bi = lr, li
    while n:
        if n & 1:
            rr, ri = (br, bi) if rr is None else _cmul(rr, ri, br, bi)
        n >>= 1
        if n:
            br, bi = _cmul(br, bi, br, bi)
    return rr, ri


def _shift_rows(x, d, reverse):
    rows = lax.broadcasted_iota(jnp.int32, x.shape, 0)
    if not reverse:
        return jnp.where(rows >= d, pltpu.roll(x, d, 0), 0.0)
    return jnp.where(rows < SUBLANES - d, pltpu.roll(x, SUBLANES - d, 0), 0.0)


def _sublane_powers(mr, mi, reverse):
    rows = lax.broadcasted_iota(jnp.int32, mr.shape, 0)
    e = SUBLANES - 1 - rows if reverse else rows
    pr, pi = jnp.ones_like(mr), jnp.zeros_like(mr)
    br, bi = mr, mi
    for d in (1, 2, 4):
        tr, ti = _cmul(pr, pi, br, bi)
        on = (e & d) != 0
        pr, pi = jnp.where(on, tr, pr), jnp.where(on, ti, pi)
        if d < 4:
            br, bi = _cmul(br, bi, br, bi)
    return pr, pi


def _inclusive_prefix(er, ei, mr, mi, reverse):
    ir, ii, pr, pi = er, ei, mr, mi
    for d in (1, 2, 4):
        tr, ti = _cmul(pr, pi, _shift_rows(ir, d, reverse), _shift_rows(ii, d, reverse))
        ir, ii = ir + tr, ii + ti
        if d < 4:
            pr, pi = _cmul(pr, pi, pr, pi)
    return ir, ii


def _chain_rows(a, t, seg):
    return pl.ds(a * SUBLANES * seg + t, SUBLANES, stride=seg)


def _seg_scan(xr_ref, xi_ref, lam, seg, nchain, reverse, store, init, extra=None):
    nt = len(lam)
    acc0 = () if extra is None else extra[1]

    def step(i, carry):
        hs, acc = carry
        t = seg - 1 - i if reverse else i
        out = []
        for a in range(nchain):
            sl = _chain_rows(a, t, seg)
            for j in range(nt):
                lr, li = lam[j]
                k = 2 * (a * nt + j)
                hr, hi = hs[k], hs[k + 1]
                nr = lr * hr - li * hi + xr_ref[j, sl, :]
                ni = lr * hi + li * hr + xi_ref[j, sl, :]
                if store:
                    xr_ref[j, sl, :] = nr
                    xi_ref[j, sl, :] = ni
                if extra is not None:
                    acc = extra[0](t, a, j, nr, ni, acc)
                out += [nr, ni]
        return tuple(out), acc

    return lax.fori_loop(0, seg, step, (tuple(init), acc0))


def _ssm_scan(xr_ref, xi_ref, lam, seg, nchain, reverse, extra=None):
    nt = len(lam)
    zero = [jnp.zeros((SUBLANES, LANES), F32)] * (2 * nt * nchain)
    ends, _ = _seg_scan(xr_ref, xi_ref, lam, seg, nchain, reverse, False, zero)
    init = [None] * (2 * nt * nchain)
    last = 0 if reverse else SUBLANES - 1
    for j in range(nt):
        mr, mi = _cpow(lam[j][0], lam[j][1], seg)
        m8r, m8i = _cpow(mr, mi, SUBLANES)
        pwr, pwi = _sublane_powers(mr, mi, reverse)
        gr = gi = jnp.zeros((SUBLANES, LANES), F32)
        for a in (reversed(range(nchain)) if reverse else range(nchain)):
            k = 2 * (a * nt + j)
            incr, inci = _inclusive_prefix(ends[k], ends[k + 1], mr, mi, reverse)
            tr, ti = _cmul(pwr, pwi, gr, gi)
            init[k] = _shift_rows(incr, 1, reverse) + tr
            init[k + 1] = _shift_rows(inci, 1, reverse) + ti
            g2r, g2i = _cmul(m8r, m8i, gr, gi)
            gr = g2r + jnp.broadcast_to(incr[last:last + 1, :], gr.shape)
            gi = g2i + jnp.broadcast_to(inci[last:last + 1, :], gi.shape)
    _, acc = _seg_scan(xr_ref, xi_ref, lam, seg, nchain, reverse, True, init, extra)
    return acc


def _diag_mask():
    steps = LANES // SSM_GROUP // GROUPS_PER_TILE
    return (jnp.eye(steps, dtype=F32)[:, None, :, None] * jnp.eye(GROUPS_PER_TILE, dtype=F32)[None, :, None, :])


def _ssm_tables(bb_re, bb_im, c_re, c_im):
    g = bb_re.shape[0]
    nt = g // GROUPS_PER_TILE
    steps = LANES // SSM_GROUP // GROUPS_PER_TILE
    mask = _diag_mask()

    def b_tab(bb):
        x = bb.reshape(nt // steps, steps, GROUPS_PER_TILE, SSM_STATE, SSM_GROUP)
        x = jnp.transpose(x, (0, 1, 4, 2, 3))[:, :, None, None]
        m = jnp.transpose(mask, (0, 2, 3, 1))[None, :, :, :, None, :, None]
        return (x * m).reshape(nt, LANES, TILE_STATES)

    def c_tab(c):
        x = c.reshape(nt // steps, steps, GROUPS_PER_TILE, SSM_GROUP, SSM_STATE)
        x = jnp.transpose(x, (0, 1, 2, 4, 3))[:, :, :, :, None, None]
        m = mask[None, :, :, None, :, :, None]
        return (x * m).reshape(nt, TILE_STATES, LANES)

    return b_tab(bb_re), b_tab(bb_im), c_tab(c_re), c_tab(c_im)


def _ssm_untable_b(db, g):
    nt = g // GROUPS_PER_TILE
    steps = LANES // SSM_GROUP // GROUPS_PER_TILE
    x = db.reshape(nt // steps, steps, GROUPS_PER_TILE, SSM_STATE, steps, GROUPS_PER_TILE, SSM_GROUP)
    m = _diag_mask()[None, :, :, None, :, :, None]
    return jnp.sum(x * m, axis=(4, 5)).reshape(g, SSM_STATE, SSM_GROUP)


def _ssm_untable_c(dc, g):
    nt = g // GROUPS_PER_TILE
    steps = LANES // SSM_GROUP // GROUPS_PER_TILE
    x = dc.reshape(nt // steps, steps, steps, GROUPS_PER_TILE, SSM_GROUP, GROUPS_PER_TILE, SSM_STATE)
    m = jnp.transpose(_diag_mask(), (0, 2, 3, 1))[None, :, :, :, None, :, None]
    out = jnp.sum(x * m, axis=(2, 3))
    return jnp.transpose(out, (0, 1, 3, 2, 4)).reshape(g, SSM_GROUP, SSM_STATE)


def _lam_tiles(lam_ref):
    out = []
    for j in range(TILE_STATES // LANES):
        out.append(jnp.broadcast_to(lam_ref[:, j * LANES:(j + 1) * LANES], (SUBLANES, LANES)))
    return out


def _scan_chains(lp):
    for n in (4, 2, 1):
        if lp % (SUBLANES * n) == 0 and (lp // SUBLANES) % 16 == 0:
            return n
    raise ValueError(lp)


def _split_tiles(dst_ref, rows, val):
    for j in range(val.shape[1] // LANES):
        dst_ref[j, rows, :] = val[:, j * LANES:(j + 1) * LANES]


def _cat_tiles(src_ref, rows):
    njt = src_ref.shape[0]
    return jnp.concatenate([src_ref[j, rows, :] for j in range(njt)], axis=1).astype(BF16)


def _ssm_fwd(u, lam_re, lam_im, tb_re, tb_im, tc_re, tc_im, d_skip, comm=None):
    lp, w = u.shape
    nt = tb_re.shape[0]
    nchain = _scan_chains(lp)
    seg = lp // (SUBLANES * nchain)
    chunk = lp // SUBLANES
    njt = TILE_STATES // LANES

    def body(u_ref, lr_ref, li_ref, br_ref, bi_ref, cr_ref, ci_ref, d_ref, y_ref, yg_ref, xr, xi):
        t = pl.program_id(0)
        for s in range(SUBLANES):
            rs = pl.ds(s * chunk, chunk)
            ub = u_ref[rs, :].astype(BF16)
            _split_tiles(xr, rs, _dot(ub, br_ref[...], "nn"))
            _split_tiles(xi, rs, _dot(ub, bi_ref[...], "nn"))
        lrs, lis = _lam_tiles(lr_ref), _lam_tiles(li_ref)
        _ssm_scan(xr, xi, list(zip(lrs, lis)), seg, nchain, False)
        for s in range(SUBLANES):
            rs = pl.ds(s * chunk, chunk)
            y = _dot(_cat_tiles(xr, rs), cr_ref[...], "nn") - _dot(_cat_tiles(xi, rs), ci_ref[...], "nn")

            @pl.when(t % 2 == 0)
            def _():
                y_ref[rs, :] = y + d_ref[...] * u_ref[rs, :]

            @pl.when(t % 2 == 1)
            def _():
                total = y_ref[rs, :] + y
                y_ref[rs, :] = total
                yg_ref[rs, :] = _gelu(total).astype(BF16)

    blk = pl.BlockSpec((lp, LANES), lambda t: (0, t // 2))
    lam_spec = pl.BlockSpec((None, 1, TILE_STATES), lambda t: (t, 0, 0))
    b_spec = pl.BlockSpec((None, LANES, TILE_STATES), lambda t: (t, 0, 0))
    c_spec = pl.BlockSpec((None, TILE_STATES, LANES), lambda t: (t, 0, 0))
    (y, yg), couts = _hosted_call(
        body, comm, out_shape=[jax.ShapeDtypeStruct((lp, w), F32), jax.ShapeDtypeStruct((lp, w), BF16)], grid=(nt,),
        in_specs=[blk, lam_spec, lam_spec, b_spec, b_spec, c_spec, c_spec,
                  pl.BlockSpec((1, LANES), lambda t: (0, t // 2))],
        out_specs=[blk, blk],
        scratch_shapes=[pltpu.VMEM((njt, lp, LANES), F32), pltpu.VMEM((njt, lp, LANES), F32)],
        sem=("arbitrary",), name="ssm_fwd",
        args=(u, lam_re, lam_im, tb_re, tb_im, tc_re, tc_im, d_skip.reshape(1, w)))
    return (y, yg), couts


def _ssm_bwd(u, dy, lam_re, lam_im, tb_re, tb_im, tc_re, tc_im, d_skip, comm=None):
    lp, w = u.shape
    nt = tb_re.shape[0]
    nchain = _scan_chains(lp)
    seg = lp // (SUBLANES * nchain)
    chunk = lp // SUBLANES
    njt = TILE_STATES // LANES
    tbt_re, tbt_im = jnp.swapaxes(tb_re, 1, 2), jnp.swapaxes(tb_im, 1, 2)
    tct_re, tct_im = jnp.swapaxes(tc_re, 1, 2), jnp.swapaxes(tc_im, 1, 2)

    def body(u_ref, dy_ref, lr_ref, li_ref, br_ref, bi_ref, btr_ref, bti_ref, ctr_ref, cti_ref, d_ref,
             du_ref, dlr_ref, dli_ref, dbr_ref, dbi_ref, dcr_ref, dci_ref, dd_ref, hr, hi, ar, ai):
        t = pl.program_id(0)
        lrs, lis = _lam_tiles(lr_ref), _lam_tiles(li_ref)
        for s in range(SUBLANES):
            rs = pl.ds(s * chunk, chunk)
            ub = u_ref[rs, :].astype(BF16)
            dyb = dy_ref[rs, :].astype(BF16)
            _split_tiles(hr, rs, _dot(ub, br_ref[...], "nn"))
            _split_tiles(hi, rs, _dot(ub, bi_ref[...], "nn"))
            _split_tiles(ar, rs, _dot(dyb, ctr_ref[...], "nn"))
            _split_tiles(ai, rs, -_dot(dyb, cti_ref[...], "nn"))
        _ssm_scan(hr, hi, list(zip(lrs, lis)), seg, nchain, False)

        def dlam_step(tt, a, j, a_r, a_i, acc):
            sl = _chain_rows(a, jnp.maximum(tt - 1, 0), seg)
            p_r, p_i = hr[j, sl, :], hi[j, sl, :]
            acc = list(acc)
            acc[2 * j] = acc[2 * j] + jnp.where(tt > 0, a_r * p_r + a_i * p_i, 0.0)
            acc[2 * j + 1] = acc[2 * j + 1] + jnp.where(tt > 0, a_i * p_r - a_r * p_i, 0.0)
            return tuple(acc)

        zero = tuple([jnp.zeros((SUBLANES, LANES), F32)] * (2 * njt))
        conj = [(lr, -li) for lr, li in zip(lrs, lis)]
        acc = list(_ssm_scan(ar, ai, conj, seg, nchain, True, (dlam_step, zero)))
        row0 = lax.broadcasted_iota(jnp.int32, (SUBLANES, LANES), 0) == 0
        for j in range(njt):
            cs = slice(j * LANES, (j + 1) * LANES)
            for a in range(nchain):
                p_r = _shift_rows(hr[j, _chain_rows(a, seg - 1, seg), :], 1, False)
                p_i = _shift_rows(hi[j, _chain_rows(a, seg - 1, seg), :], 1, False)
                if a > 0:
                    before = pl.ds(a * SUBLANES * seg - 1, 1)
                    p_r = jnp.where(row0, jnp.broadcast_to(hr[j, before, :], p_r.shape), p_r)
                    p_i = jnp.where(row0, jnp.broadcast_to(hi[j, before, :], p_i.shape), p_i)
                a_r, a_i = ar[j, _chain_rows(a, 0, seg), :], ai[j, _chain_rows(a, 0, seg), :]
                acc[2 * j] = acc[2 * j] + a_r * p_r + a_i * p_i
                acc[2 * j + 1] = acc[2 * j + 1] + a_i * p_r - a_r * p_i
            dlr_ref[:, cs] = jnp.sum(acc[2 * j], axis=0, keepdims=True)
            dli_ref[:, cs] = jnp.sum(acc[2 * j + 1], axis=0, keepdims=True)

        dd = jnp.zeros((1, LANES), F32)
        for s in range(SUBLANES):
            rs = pl.ds(s * chunk, chunk)
            ub = u_ref[rs, :].astype(BF16)
            dyv = dy_ref[rs, :]
            dyb = dyv.astype(BF16)
            arb, aib = _cat_tiles(ar, rs), _cat_tiles(ai, rs)
            hrb, hib = _cat_tiles(hr, rs), _cat_tiles(hi, rs)
            du = _dot(arb, btr_ref[...], "nn") + _dot(aib, bti_ref[...], "nn")
            upd = [(dbr_ref, _dot(arb, ub, "tn")), (dbi_ref, _dot(aib, ub, "tn")),
                   (dcr_ref, _dot(dyb, hrb, "tn")), (dci_ref, -_dot(dyb, hib, "tn"))]
            for ref, val in upd:
                if s == 0:
                    ref[...] = val
                else:
                    ref[...] += val
            rows = lax.broadcasted_iota(jnp.int32, (chunk, LANES), 0) + s * chunk
            keep = rows >= PAD_FRONT
            dd = dd + jnp.sum(dyv * u_ref[rs, :], axis=0, keepdims=True)

            @pl.when(t % 2 == 0)
            def _():
                du_ref[rs, :] = jnp.where(keep, du + d_ref[...] * dyv, 0.0)

            @pl.when(t % 2 == 1)
            def _():
                du_ref[rs, :] += jnp.where(keep, du, 0.0)

        @pl.when(t % 2 == 0)
        def _():
            dd_ref[...] = dd

    blk = pl.BlockSpec((lp, LANES), lambda t: (0, t // 2))
    vec = pl.BlockSpec((1, LANES), lambda t: (0, t // 2))
    lam_spec = pl.BlockSpec((None, 1, TILE_STATES), lambda t: (t, 0, 0))
    b_spec = pl.BlockSpec((None, LANES, TILE_STATES), lambda t: (t, 0, 0))
    c_spec = pl.BlockSpec((None, TILE_STATES, LANES), lambda t: (t, 0, 0))
    lam_shape = jax.ShapeDtypeStruct((nt, 1, TILE_STATES), F32)
    bt_shape = jax.ShapeDtypeStruct((nt, TILE_STATES, LANES), F32)
    ct_shape = jax.ShapeDtypeStruct((nt, LANES, TILE_STATES), F32)
    st = pltpu.VMEM((njt, lp, LANES), F32)
    return _hosted_call(
        body, comm,
        out_shape=[jax.ShapeDtypeStruct((lp, w), F32), lam_shape, lam_shape, bt_shape, bt_shape, ct_shape, ct_shape,
                   jax.ShapeDtypeStruct((1, w), F32)],
        grid=(nt,),
        in_specs=[blk, blk, lam_spec, lam_spec, b_spec, b_spec, c_spec, c_spec, b_spec, b_spec, vec],
        out_specs=[blk, lam_spec, lam_spec, c_spec, c_spec, b_spec, b_spec, vec],
        scratch_shapes=[st, st, st, st], sem=("arbitrary",), name="ssm_bwd",
        args=(u, dy, lam_re, lam_im, tb_re, tb_im, tbt_re, tbt_im, tct_re, tct_im, d_skip.reshape(1, w)))


def _ssm_params(a_re, a_im, log_dt, b_re, b_im):
    dt = jnp.exp(log_dt)[:, None]
    mag = jnp.exp(a_re * dt)
    lb_re = mag * jnp.cos(a_im * dt)
    lb_im = mag * jnp.sin(a_im * dt)
    den = a_re * a_re + a_im * a_im
    num_re = lb_re - 1.0
    coef_re = (num_re * a_re + lb_im * a_im) / den
    coef_im = (lb_im * a_re - num_re * a_im) / den
    bb_re = coef_re[..., None] * b_re - coef_im[..., None] * b_im
    bb_im = coef_re[..., None] * b_im + coef_im[..., None] * b_re
    return lb_re, lb_im, bb_re, bb_im


def _merge_fwd(o, yg, ga, gs, w3t, comm=None):
    lp, d = ga.shape
    kw = w3t.shape[2]
    tm = _row_tile(lp)

    def epi(accs, in_refs, out_refs):
        attn, vv, gg = accs
        out_refs[0][...] = (_sigmoid(in_refs[5][...]) * attn
                            + _sigmoid(in_refs[6][...]) * (vv * _sigmoid(gg))).astype(BF16)

    wspec = lambda which: pl.BlockSpec((None, d, kw), lambda i: (which, 0, 0))
    rowspec = pl.BlockSpec((tm, d), lambda i: (i, 0))
    aspec = pl.BlockSpec((tm, kw), lambda i: (i, 0))
    res = _matmul(
        "merge_fwd", (lp // tm,), None, [o, yg, w3t, w3t, w3t, ga, gs],
        [aspec, aspec, wspec(0), wspec(1), wspec(2), rowspec, rowspec],
        [(0, 2, "nt", 0), (1, 3, "nt", 1), (1, 4, "nt", 2)], [(tm, d)] * 3, epi,
        [jax.ShapeDtypeStruct((lp, d), BF16)], [rowspec], ("parallel",), comm)
    return (res[0], []) if comm is None else (res[0][0], res[1])


def _out_proj(merged, w_out, h, comm=None):
    lp, d = h.shape
    tm = _row_tile(lp)

    def epi(accs, in_refs, out_refs):
        out_refs[0][...] = in_refs[2][...] + accs[0]

    rowspec = pl.BlockSpec((tm, d), lambda i: (i, 0))
    res = _matmul(
        "mix_out_proj", (lp // tm,), None, [merged, w_out, h],
        [rowspec, pl.BlockSpec((d, d), lambda i: (0, 0)), rowspec],
        [(0, 1, "nn", 0)], [(tm, d)], epi, [jax.ShapeDtypeStruct((lp, d), F32)], [rowspec], ("parallel",), comm)
    return (res[0], []) if comm is None else (res[0][0], res[1])


def _merge_bwd(dhb, w_out, o, yg, ga, gs, w3t):
    lp, d = ga.shape
    kw = w3t.shape[2]
    tm = _row_tile(lp)

    def epi(accs, in_refs, out_refs):
        dm, attn, vv, gg = accs
        sa = _sigmoid(in_refs[7][...])
        ss = _sigmoid(in_refs[8][...])
        sg = _sigmoid(gg)
        ssm = vv * sg
        dssm = dm * ss
        out_refs[0][...] = (dm * sa).astype(BF16)
        out_refs[1][...] = (dssm * sg).astype(BF16)
        out_refs[2][...] = (dssm * vv * sg * (1.0 - sg)).astype(BF16)
        out_refs[3][...] = (dm * attn * sa * (1.0 - sa)).astype(BF16)
        out_refs[4][...] = (dm * ssm * ss * (1.0 - ss)).astype(BF16)

    wspec = lambda which: pl.BlockSpec((None, d, kw), lambda i: (which, 0, 0))
    rowspec = pl.BlockSpec((tm, d), lambda i: (i, 0))
    aspec = pl.BlockSpec((tm, kw), lambda i: (i, 0))
    shp = jax.ShapeDtypeStruct((lp, d), BF16)
    return _matmul(
        "merge_bwd", (lp // tm,), None, [dhb, w_out, o, yg, w3t, w3t, w3t, ga, gs],
        [rowspec, pl.BlockSpec((d, d), lambda i: (0, 0)), aspec, aspec, wspec(0), wspec(1), wspec(2), rowspec,
         rowspec],
        [(0, 1, "nt", 0), (2, 4, "nt", 1), (3, 5, "nt", 2), (3, 6, "nt", 3)], [(tm, d)] * 4, epi,
        [shp] * 5, [rowspec] * 5, ("parallel",))


def _branch_bwd(dattn, dv, dg, w3t, y):
    lp, d = dattn.shape
    kw = w3t.shape[2]
    tm = _row_tile(lp)

    def epi(accs, in_refs, out_refs):
        out_refs[0][...] = accs[0].astype(BF16)
        out_refs[1][...] = accs[1] * _gelu_grad(in_refs[6][...])

    wspec = lambda which: pl.BlockSpec((None, d, kw), lambda i: (which, 0, 0))
    rowspec = pl.BlockSpec((tm, d), lambda i: (i, 0))
    aspec = pl.BlockSpec((tm, kw), lambda i: (i, 0))
    return _matmul(
        "branch_bwd", (lp // tm,), None, [dattn, dv, dg, w3t, w3t, w3t, y],
        [rowspec, rowspec, rowspec, wspec(0), wspec(1), wspec(2), aspec],
        [(0, 3, "nn", 0), (1, 4, "nn", 1), (2, 5, "nn", 1)], [(tm, kw)] * 2, epi,
        [jax.ShapeDtypeStruct((lp, kw), BF16), jax.ShapeDtypeStruct((lp, kw), F32)], [aspec, aspec],
        ("parallel",))


def _tn_cols(x, ys, name):
    lp, kx = x.shape
    n = ys[0].shape[1]
    n4 = n // N_CHIPS
    tk = _tn_tiles(lp)

    def epi(accs, in_refs, out_refs):
        for acc, o in zip(accs, out_refs):
            o[...] = acc

    shp = jax.ShapeDtypeStruct((N_CHIPS, kx, n4), F32)
    return _matmul(
        name, (N_CHIPS, lp // tk), 1, [x] + list(ys),
        [pl.BlockSpec((tk, kx), lambda j, k: (k, 0))] + [pl.BlockSpec((tk, n4), lambda j, k: (k, j))] * len(ys),
        [(0, 1 + i, "tn", i) for i in range(len(ys))], [(kx, n4)] * len(ys), epi,
        [shp] * len(ys), [pl.BlockSpec((None, kx, n4), lambda j, k: (j, 0, 0))] * len(ys),
        ("parallel", "arbitrary"))


def _tn_full(x, y, name, tn_cols=None):
    lp, kx = x.shape
    n = y.shape[1]
    tk = _tn_tiles(lp)
    tn = n if tn_cols is None else tn_cols

    def epi(accs, in_refs, out_refs):
        out_refs[0][...] = accs[0]

    (out,) = _matmul(
        name, (n // tn, lp // tk), 1, [x, y],
        [pl.BlockSpec((tk, kx), lambda j, k: (k, 0)), pl.BlockSpec((tk, tn), lambda j, k: (k, j))],
        [(0, 1, "tn", 0)], [(kx, tn)], epi,
        [jax.ShapeDtypeStruct((kx, n), F32)], [pl.BlockSpec((kx, tn), lambda j, k: (0, j))],
        ("parallel", "arbitrary"))
    return out


def _in_proj_bwd(dz, w_in, dh, h_in, gain):
    lp, d = h_in.shape
    inw = w_in.shape[1]
    tm = _row_tile(lp)

    def epi(accs, in_refs, out_refs):
        i = pl.program_id(0)
        dx, dgrow = _rms_bwd_math(accs[0], in_refs[3][...], in_refs[4][...])
        dh_new = in_refs[2][...] + dx
        out_refs[0][...] = dh_new
        out_refs[2][...] = dh_new.astype(BF16)

        @pl.when(i == 0)
        def _():
            out_refs[1][...] = jnp.zeros_like(out_refs[1])

        out_refs[1][...] += jnp.sum(dgrow, axis=0, keepdims=True)

    row = pl.BlockSpec((tm, d), lambda i: (i, 0))
    return _matmul(
        "mix_in_proj_bwd", (lp // tm,), None, [dz, w_in, dh, h_in, gain.reshape(1, d)],
        [pl.BlockSpec((tm, inw), lambda i: (i, 0)), pl.BlockSpec((d, inw), lambda i: (0, 0)), row, row,
         pl.BlockSpec((1, d), lambda i: (0, 0))],
        [(0, 1, "nt", 0)], [(tm, d)], epi,
        [jax.ShapeDtypeStruct((lp, d), F32), jax.ShapeDtypeStruct((1, d), F32), jax.ShapeDtypeStruct((lp, d), BF16)],
        [row, pl.BlockSpec((1, d), lambda i: (0, 0)), row], ("arbitrary",))


def _loss_head(h, gain, target):
    lp, d = h.shape
    nb = lp // BLOCK

    def body(h_ref, g_ref, t_ref, dh_ref, dg_ref, loss_ref, dhb_ref):
        i = pl.program_id(0)

        @pl.when(i == 0)
        def _():
            dg_ref[...] = jnp.zeros_like(dg_ref)
            loss_ref[...] = jnp.zeros_like(loss_ref)
            dh_ref[...] = jnp.zeros_like(dh_ref)
            dhb_ref[...] = jnp.zeros_like(dhb_ref)

        @pl.when(i > 0)
        def _():
            x = h_ref[...]
            g = g_ref[...]
            r = lax.rsqrt(jnp.mean(x * x, axis=-1, keepdims=True) + EPS)
            err = x * r * g - t_ref[...]
            loss_ref[...] += jnp.zeros_like(loss_ref) + 0.5 * jnp.sum(jnp.sum(err * err, axis=-1, keepdims=True)) / d
            dx, dgrow = _rms_bwd_math(err * (1.0 / d), x, g)
            dh_ref[...] = dx
            dhb_ref[...] = dx.astype(BF16)
            dg_ref[...] += jnp.sum(dgrow, axis=0, keepdims=True)

    row = pl.BlockSpec((BLOCK, d), lambda i: (i, 0))
    one = pl.BlockSpec((1, d), lambda i: (0, 0))
    return pl.pallas_call(
        body,
        out_shape=[jax.ShapeDtypeStruct((lp, d), F32), jax.ShapeDtypeStruct((1, d), F32),
                   jax.ShapeDtypeStruct((SUBLANES, LANES), F32), jax.ShapeDtypeStruct((lp, d), BF16)],
        grid=(nb,),
        in_specs=[row, one, pl.BlockSpec((BLOCK, d), lambda i: (jnp.maximum(i - 1, 0), 0))],
        out_specs=[row, one, pl.BlockSpec((SUBLANES, LANES), lambda i: (0, 0)), row],
        compiler_params=_cparams(("arbitrary",)), name="loss_head")(h, gain.reshape(1, d), target)


def _adam_math(w, g, m, v):
    m = ADAM_B1 * m + (1.0 - ADAM_B1) * g
    v = ADAM_B2 * v + (1.0 - ADAM_B2) * (g * g)
    m_hat = m / (1.0 - ADAM_B1 ** ADAM_STEP)
    v_hat = v / (1.0 - ADAM_B2 ** ADAM_STEP)
    delta = -ADAM_LR * (m_hat / (jnp.sqrt(v_hat) + ADAM_EPS) + ADAM_WD * w)
    return delta, m, v


def _adamw_layers(w, m, v, mine, other, pos, name):
    depth, r, c = w.shape
    half = r // 2
    tr = _div_tile(half, c * 4)
    nh = half // tr

    def body(*refs):
        pos_ref, w_ref, m_ref, v_ref = refs[:4]
        mine_refs = refs[4:4 + depth]
        other_refs = refs[4 + depth:4 + 2 * depth]
        g_out, d_out, m_out, v_out = refs[4 + 2 * depth:]
        layer, i = pl.program_id(0), pl.program_id(1)
        is_mine = (i // nh) == pos_ref[0]

        def update(g):
            delta, nm, nv = _adam_math(w_ref[...], g, m_ref[...], v_ref[...])
            g_out[...] = g
            d_out[...] = delta
            m_out[...] = nm
            v_out[...] = nv

        for l in range(depth):
            @pl.when((layer == l) & is_mine)
            def _(l=l):
                update(mine_refs[l][...])

            @pl.when((layer == l) & jnp.logical_not(is_mine))
            def _(l=l):
                update(other_refs[l][...])

    stacked = pl.BlockSpec((None, tr, c), lambda l, i, p: (l, i, 0))

    def gspec(layer, is_other):
        def imap(l, i, p):
            first = jnp.where(is_other, 1 - p[0], p[0]) * nh
            here = jnp.clip(i - first, 0, nh - 1)
            return (jnp.where(l == layer, here, jnp.where(l < layer, 0, nh - 1)), 0)
        return pl.BlockSpec((tr, c), imap)

    shp = jax.ShapeDtypeStruct((depth, r, c), F32)
    grid_spec = pltpu.PrefetchScalarGridSpec(
        num_scalar_prefetch=1, grid=(depth, 2 * nh),
        in_specs=[stacked] * 3 + [gspec(l, 0) for l in range(depth)] + [gspec(l, 1) for l in range(depth)],
        out_specs=[stacked] * 4)
    return pl.pallas_call(
        body, out_shape=[shp] * 4, grid_spec=grid_spec,
        compiler_params=_cparams(("arbitrary", "arbitrary")), name=name)(pos, w, m, v, *mine, *other)


def _adamw_whole(w, g, m, v, name):
    def body(w_ref, g_ref, m_ref, v_ref, d_out, m_out, v_out):
        delta, nm, nv = _adam_math(w_ref[...], g_ref[...], m_ref[...], v_ref[...])
        d_out[...] = delta
        m_out[...] = nm
        v_out[...] = nv

    shp = jax.ShapeDtypeStruct(w.shape, F32)
    return pl.pallas_call(body, out_shape=[shp] * 3, compiler_params=_cparams(), name=name)(w, g, m, v)


def _mesh_pos():
    return lax.axis_index("x"), lax.axis_index("y"), lax.axis_index("c")


def _row_half(ref, which, lead):
    half = ref.shape[lead] // 2
    idx = (slice(None),) * lead + (pl.ds(which * half, half), slice(None))
    return ref.at[idx]


def _gather_comm(arrs, tag):
    n = len(arrs)

    def ctx(ins, outs, sems):
        send_sems, recv_sems, local_sems = sems
        x, y, c = _mesh_pos()
        chips = [(1 - x, y), (x, 1 - y), (1 - x, 1 - y)]

        def slot(k, chip, which):
            lead = len(ins[k].shape) - 2
            return _row_half(outs[k].at[2 * chip[0] + chip[1]], which, lead)

        def copy(k, j, src, dst, to):
            return pltpu.make_async_remote_copy(
                src_ref=src, dst_ref=dst, send_sem=send_sems.at[6 * k + j], recv_sem=recv_sems.at[6 * k + j],
                device_id=to, device_id_type=MESH)

        def local(k):
            return pltpu.make_async_copy(ins[k], outs[k].at[2 * x + y], local_sems.at[k])

        def first(k, j):
            lead = len(ins[k].shape) - 2
            return copy(k, j, _row_half(ins[k], c, lead), slot(k, (x, y), c), (*chips[j], c))

        def passed(k, j, which):
            return copy(k, 3 + j, slot(k, chips[j], which), slot(k, chips[j], which), (x, y, 1 - c))

        def landed(k, j):
            return copy(k, j, slot(k, chips[j], c), slot(k, chips[j], c), (x, y, 1 - c))

        return c, local, first, passed, landed

    def start(ins, outs, sems):
        c, local, first, passed, landed = ctx(ins, outs, sems)
        for k in range(n):
            local(k).start()
            for j in range(3):
                first(k, j).start()

    def mid(ins, outs, sems):
        c, local, first, passed, landed = ctx(ins, outs, sems)
        for j in range(3):
            for k in range(n):
                landed(k, j).wait_recv()
                passed(k, j, c).start()

    def finish(ins, outs, sems):
        c, local, first, passed, landed = ctx(ins, outs, sems)
        for j in range(3):
            for k in range(n):
                passed(k, j, 1 - c).wait_recv()
        for k in range(n):
            for j in range(3):
                first(k, j).wait_send()
                passed(k, j, c).wait_send()
            local(k).wait()

    return _Comm(
        tag, arrs, [jax.ShapeDtypeStruct((N_CHIPS,) + a.shape, a.dtype) for a in arrs],
        [pltpu.SemaphoreType.DMA((6 * n,)), pltpu.SemaphoreType.DMA((6 * n,)), pltpu.SemaphoreType.DMA((n,))],
        start, mid, finish)


def _run_comm(comm, name):
    n_in, n_out = len(comm.ins), len(comm.out_shapes)

    def body(*refs):
        ins, outs, sems = refs[:n_in], refs[n_in:n_in + n_out], refs[n_in + n_out:]
        comm.start(ins, outs, sems)
        if comm.mid is not None:
            comm.mid(ins, outs, sems)
        comm.finish(ins, outs, sems)

    return pl.pallas_call(
        body, out_shape=comm.out_shapes, in_specs=[HBM_SPEC] * n_in, out_specs=[HBM_SPEC] * n_out,
        scratch_shapes=comm.sems, name=name)(*comm.ins)


def _all_gather_chips(arrs, name):
    return _run_comm(_gather_comm(arrs, "gather"), name)


GATHER_US_PER_BYTE = 380.0 / 11.65e6
HOST_US = dict(ffn_up=78.0, ffn_down=65.0, in_proj=38.0, attn_fwd=103.0, ssm_fwd=67.0, merge_fwd=50.0,
               out_proj=45.0)
HOST_SLACK_US = 10.0


class _WeightStream:
    def __init__(self, pieces):
        self.keys = [k for k, _ in pieces]
        self.shards = dict(pieces)
        self.next = 0
        self.full = {}
        self.pending = []

    def comm_for(self, host):
        budget = HOST_US[host] + HOST_SLACK_US
        taken, cost = [], 0.0
        while self.next < len(self.keys):
            key = self.keys[self.next]
            c = self.shards[key].size * self.shards[key].dtype.itemsize * GATHER_US_PER_BYTE
            if cost + c > budget:
                break
            taken.append(key)
            cost += c
            self.next += 1
        self.pending = taken
        if not taken:
            return None
        return _gather_comm([self.shards[k] for k in taken], "g_" + "_".join(k[1] for k in taken))

    def deposit(self, gathered):
        for key, arr in zip(self.pending, gathered):
            self.full[key] = arr
        self.pending = []

    def get(self, key):
        if key not in self.full:
            upto = self.keys.index(key) + 1
            keys = self.keys[self.next:upto]
            self.next = upto
            for k, arr in zip(keys, _all_gather_chips([self.shards[k] for k in keys], "gather_now")):
                self.full[k] = arr
        return self.full[key]


def _all_gather_devices(x_shard, name):
    m_per, ncol = x_shard.shape

    def body(x_ref, out_ref, send_sems, recv_sems, local_sem):
        x, y, c = _mesh_pos()
        me, sibling = (x, y, c), (x, y, 1 - c)
        chips = [(1 - x, y), (x, 1 - y), (1 - x, 1 - y)]

        def rows(px, py, pc):
            return out_ref.at[4 * px + 2 * py + pc]

        def copy(k, block, to, src=None):
            return pltpu.make_async_remote_copy(
                src_ref=rows(*block) if src is None else src, dst_ref=rows(*block),
                send_sem=send_sems.at[k], recv_sem=recv_sems.at[k], device_id=to, device_id_type=MESH)

        mine = pltpu.make_async_copy(x_ref, rows(*me), local_sem)
        mine.start()
        first = [copy(0, me, sibling, src=x_ref)]
        first += [copy(1 + j, me, (*chip, c), src=x_ref) for j, chip in enumerate(chips)]
        for cp in first:
            cp.start()
        passed = [copy(4 + j, (*chip, c), sibling) for j, chip in enumerate(chips)]
        for j, chip in enumerate(chips):
            copy(1 + j, (*chip, c), me).wait_recv()
            passed[j].start()
        copy(0, sibling, me).wait_recv()
        for j, chip in enumerate(chips):
            copy(4 + j, (*chip, 1 - c), me).wait_recv()
        for cp in first + passed:
            cp.wait_send()
        mine.wait()

    return pl.pallas_call(
        body, out_shape=jax.ShapeDtypeStruct((8, m_per, ncol), x_shard.dtype),
        in_specs=[pl.BlockSpec(memory_space=pltpu.VMEM)], out_specs=pl.BlockSpec(memory_space=pltpu.VMEM),
        scratch_shapes=[pltpu.SemaphoreType.DMA((7,)), pltpu.SemaphoreType.DMA((7,)), pltpu.SemaphoreType.DMA],
        compiler_params=pltpu.CompilerParams(vmem_limit_bytes=VMEM_LIMIT), name=name)(x_shard)


def _sum_devices(g8, name):
    _, r, c = g8.shape
    tr = _div_tile(r, c * 4 * 8)

    def body(g_ref, o_ref):
        acc = g_ref[0]
        for dev in range(1, 8):
            acc = acc + g_ref[dev]
        o_ref[...] = acc

    return pl.pallas_call(
        body, out_shape=jax.ShapeDtypeStruct((r, c), F32), grid=(r // tr,),
        in_specs=[pl.BlockSpec((8, tr, c), lambda i: (0, i, 0))], out_specs=pl.BlockSpec((tr, c), lambda i: (i, 0)),
        compiler_params=_cparams(("parallel",)), name=name)(g8)


def _other_halves_bf16(arrs, pos, name):
    n = len(arrs)

    def body(pos_ref, *refs):
        for a_ref, o_ref in zip(refs[:n], refs[n:]):
            o_ref[...] = a_ref[...].astype(BF16)

    in_specs, out_specs, shapes = [], [], []
    for arr in arrs:
        nslab, r, c = arr.shape
        in_specs.append(pl.BlockSpec((None, r // 2, c), lambda j, p: (j, 1 - p[0], 0)))
        out_specs.append(pl.BlockSpec((None, r // 2, c), lambda j, p: (j, 0, 0)))
        shapes.append(jax.ShapeDtypeStruct((nslab, r // 2, c), BF16))
    grid_spec = pltpu.PrefetchScalarGridSpec(
        num_scalar_prefetch=1, grid=(N_CHIPS,), in_specs=in_specs, out_specs=out_specs)
    return pl.pallas_call(
        body, out_shape=shapes, grid_spec=grid_spec, compiler_params=_cparams(("parallel",)), name=name)(pos, *arrs)


def _chip_partials(arrs, recvs, pos, name):
    n = len(arrs)

    def body(pos_ref, *refs):
        for a_ref, b_ref, o_ref in zip(refs[:n], refs[n:2 * n], refs[2 * n:]):
            o_ref[...] = (a_ref[...] + b_ref[...]).astype(BF16)

    own_specs, recv_specs, shapes = [], [], []
    for arr in arrs:
        nslab, r, c = arr.shape
        own_specs.append(pl.BlockSpec((None, r // 2, c), lambda j, p: (j, p[0], 0)))
        recv_specs.append(pl.BlockSpec((None, r // 2, c), lambda j, p: (j, 0, 0)))
        shapes.append(jax.ShapeDtypeStruct((nslab, r // 2, c), BF16))
    grid_spec = pltpu.PrefetchScalarGridSpec(
        num_scalar_prefetch=1, grid=(N_CHIPS,), in_specs=own_specs + recv_specs, out_specs=recv_specs)
    return pl.pallas_call(
        body, out_shape=shapes, grid_spec=grid_spec,
        compiler_params=_cparams(("parallel",)), name=name)(pos, *arrs, *recvs)


def _chip_exchange_comm(parts, tag):
    n = len(parts)

    def copies(ins, outs, sems):
        send_sems, recv_sems = sems
        x, y, c = _mesh_pos()
        chips = [(1 - x, y), (x, 1 - y), (1 - x, 1 - y)]
        return [pltpu.make_async_remote_copy(
            src_ref=ins[k].at[2 * chip[0] + chip[1]], dst_ref=outs[k].at[j],
            send_sem=send_sems.at[3 * k + j], recv_sem=recv_sems.at[3 * k + j],
            device_id=(*chip, c), device_id_type=MESH) for k in range(n) for j, chip in enumerate(chips)]

    def start(ins, outs, sems):
        for cp in copies(ins, outs, sems):
            cp.start()

    def finish(ins, outs, sems):
        for cp in copies(ins, outs, sems):
            cp.wait()

    return _Comm(
        tag, parts, [jax.ShapeDtypeStruct((3,) + p.shape[1:], p.dtype) for p in parts],
        [pltpu.SemaphoreType.DMA((3 * n,)), pltpu.SemaphoreType.DMA((3 * n,))], start, None, finish)


def _reduce_halves(arrs, recvs, gots, pos, name):
    n = len(arrs)

    def body(pos_ref, *refs):
        for a_ref, b_ref, g_ref, o_ref in zip(refs[:n], refs[n:2 * n], refs[2 * n:3 * n], refs[3 * n:]):
            acc = a_ref[...] + b_ref[...]
            for j in range(3):
                acc = acc + g_ref[j].astype(F32)
            o_ref[...] = acc

    own_specs, recv_specs, got_specs, out_specs, shapes = [], [], [], [], []
    for arr in arrs:
        _, r, c = arr.shape
        own_specs.append(pl.BlockSpec((None, r // 2, c), lambda i, p: (p[1], p[0], 0)))
        recv_specs.append(pl.BlockSpec((None, r // 2, c), lambda i, p: (p[1], 0, 0)))
        got_specs.append(pl.BlockSpec((3, r // 2, c), lambda i, p: (0, 0, 0)))
        out_specs.append(pl.BlockSpec((r // 2, c), lambda i, p: (0, 0)))
        shapes.append(jax.ShapeDtypeStruct((r // 2, c), F32))
    grid_spec = pltpu.PrefetchScalarGridSpec(
        num_scalar_prefetch=1, grid=(1,), in_specs=own_specs + recv_specs + got_specs, out_specs=out_specs)
    return pl.pallas_call(
        body, out_shape=shapes, grid_spec=grid_spec,
        compiler_params=_cparams(("arbitrary",)), name=name)(pos, *arrs, *recvs, *gots)


def _share_halves(halves, name):
    n = len(halves)

    def body(*refs):
        ins, outs = refs[:n], refs[n:2 * n]
        send_sems, recv_sems = refs[2 * n:]
        x, y, c = _mesh_pos()
        cps = []
        for k in range(n):
            cp = pltpu.make_async_remote_copy(
                src_ref=ins[k], dst_ref=outs[k], send_sem=send_sems.at[k], recv_sem=recv_sems.at[k],
                device_id=(x, y, 1 - c), device_id_type=MESH)
            cp.start()
            cps.append(cp)
        for cp in cps:
            cp.wait()

    return pl.pallas_call(
        body, out_shape=[jax.ShapeDtypeStruct(h.shape, h.dtype) for h in halves],
        in_specs=[HBM_SPEC] * n, out_specs=[HBM_SPEC] * n,
        scratch_shapes=[pltpu.SemaphoreType.DMA((n,)), pltpu.SemaphoreType.DMA((n,))], name=name)(*halves)


class _Reduction:
    def __init__(self, arrs, pos, tag):
        self.arrs, self.pos, self.tag = arrs, pos, tag
        self.recv = _share_halves(_other_halves_bf16(arrs, pos, "rs_other_" + tag), "rs_sibling_" + tag)
        self.parts = _chip_partials(arrs, self.recv, pos, "rs_partial_" + tag)
        self.got = None

    def comm(self):
        return _chip_exchange_comm(self.parts, "rs_" + self.tag)

    def end(self):
        if self.got is None:
            self.got = _run_comm(self.comm(), "rs_chips_" + self.tag)
        halves = _reduce_halves(self.arrs, self.recv, self.got, self.pos, "rs_reduce_" + self.tag)
        return halves, _share_halves(halves, "rs_share_" + self.tag)


def _w_in_full(p, l, ws):
    if "w_in" not in p:
        slabs = ws.get((l, "w_in"))
        p["w_in"] = jnp.concatenate([slabs[j] for j in range(N_CHIPS)], axis=1)
    return p["w_in"]


def _w3t_full(p, l, ws):
    if "w3t" not in p:
        slabs = ws.get((l, "w3"))
        p["w3t"] = jnp.swapaxes(slabs, 0, 1).reshape(slabs.shape[1], -1, slabs.shape[3])
    return p["w3t"]


def _w_out_full(l, ws):
    slabs = ws.get((l, "w_out"))
    return slabs.reshape(-1, slabs.shape[2])


def _layer_fwd(h, l, p, ws, tabs):
    def hosted(host, fn, *args):
        out, got = fn(*args, ws.comm_for(host))
        ws.deposit(got)
        return out

    ffn1_saved = hosted("ffn_up", _ffn_up, h, p["ffn1_norm"], ws.get((l, "wg1")), ws.get((l, "wu1")))
    h1 = hosted("ffn_down", _ffn_down, ffn1_saved[2], ws.get((l, "wd1")), h)
    n = _rms_fwd(h1, p["mix_norm"], "rms_fwd_mix")
    ssm_w = p["ssm_d"].shape[0]
    q, k, v, u, ga, gs = hosted("in_proj", _in_proj, n, _w_in_full(p, l, ws), tabs, ssm_w)
    o = hosted("attn_fwd", _attn_fwd, q, k, v, p["attn_sinks"])
    y, yg = hosted("ssm_fwd", _ssm_fwd, u, *p["ssm_tabs"], p["ssm_d"])
    merged = hosted("merge_fwd", _merge_fwd, o, yg, ga, gs, _w3t_full(p, l, ws))
    h2 = hosted("out_proj", _out_proj, merged, _w_out_full(l, ws), h1)
    ffn2_saved = hosted("ffn_up", _ffn_up, h2, p["ffn2_norm"], ws.get((l, "wg2")), ws.get((l, "wu2")))
    h3 = hosted("ffn_down", _ffn_down, ffn2_saved[2], ws.get((l, "wd2")), h2)
    saved = dict(h0=h, h1=h1, h2=h2, ffn1=ffn1_saved, ffn2=ffn2_saved, n_mix=n, q=q, k=k, v=v, u=u, ga=ga, gs=gs,
                 o=o, y=y, yg=yg, merged=merged)
    return h3, saved


def _layer_bwd(dh_pair, l, p, ws, s, tabs, pos):
    g = {}
    (dh2, dhb), g["ffn2_norm"], red_ffn2, _ = _ffn_bwd(
        dh_pair, s["h2"], p["ffn2_norm"], ws.get((l, "wg2")), ws.get((l, "wu2")), ws.get((l, "wd2")), p["f4"],
        s["ffn2"], pos)
    w3, w_out_w = _w3t_full(p, l, ws), _w_out_full(l, ws)
    lp, d = dh2.shape
    d4 = d // N_CHIPS
    dw_out = _tn_full(s["merged"], dhb, "mix_dw_out").reshape(N_CHIPS, d4, d)
    dattn, dv, dg, dga, dgs = _merge_bwd(dhb, w_out_w, s["o"], s["yg"], s["ga"], s["gs"], w3)
    (dw_ap,) = _tn_cols(s["o"], [dattn], "mix_dw_ap")
    dw_gv, dw_gg = _tn_cols(s["yg"], [dv, dg], "mix_dw_glu")
    do, dy = _branch_bwd(dattn, dv, dg, w3, s["y"])
    (dq, dk, dvv, dkm, dvm, dsink), _ = _attn_bwd(s["q"], s["k"], s["v"], do, p["attn_sinks"], tabs)
    g["attn_sinks"] = dsink[:, 0]
    (du, dlr, dli, dbr, dbi, dcr, dci, dd), _ = _ssm_bwd(s["u"], dy, *p["ssm_tabs"], p["ssm_d"])
    ngrp = p["ssm_d"].shape[0] // SSM_GROUP
    g["ssm_lam"] = (dlr.reshape(ngrp, SSM_STATE), dli.reshape(ngrp, SSM_STATE),
                    _ssm_untable_b(dbr, ngrp), _ssm_untable_b(dbi, ngrp))
    g["ssm_c_re"] = _ssm_untable_c(dcr, ngrp)
    g["ssm_c_im"] = _ssm_untable_c(dci, ngrp)
    g["ssm_d"] = dd[0]
    dk = dk.at[:BLOCK].add(dkm)
    dvv = dvv.at[:BLOCK].add(dvm)
    dz = jnp.concatenate([dq.astype(BF16), dk.astype(BF16), dvv.astype(BF16), du.astype(BF16), dga, dgs], axis=1)
    n = s["n_mix"]
    w_in = _w_in_full(p, l, ws)
    inw = w_in.shape[1]
    dw_in = _tn_full(dz, n, "mix_dw_in", d // 2).reshape(N_CHIPS, inw // N_CHIPS, d)
    red_mix = _Reduction([dw_in, dw_ap, dw_gv, dw_gg, dw_out], pos, "mix")
    dh1, g["mix_norm"], dh1b = _in_proj_bwd(dz, w_in, dh2, s["h1"], p["mix_norm"])
    dh0_pair, g["ffn1_norm"], red_ffn1, red_mix.got = _ffn_bwd(
        (dh1, dh1b), s["h0"], p["ffn1_norm"], ws.get((l, "wg1")), ws.get((l, "wu1")), ws.get((l, "wd1")), p["f4"],
        s["ffn1"], pos, red_mix.comm())
    return dh0_pair, g, [*red_ffn1, red_mix, *red_ffn2]


BIG = ["ffn1_w_gate", "ffn1_w_up", "ffn1_w_down", "w_in", "w_attn_proj", "w_glu_v", "w_glu_g", "w_out",
       "ffn2_w_gate", "ffn2_w_up", "ffn2_w_down"]
TRANSPOSED = ["ffn1_w_gate", "ffn1_w_up", "w_in", "ffn2_w_gate", "ffn2_w_up"]
SMALL = ["ffn1_norm", "mix_norm", "attn_sinks", "ssm_a_re", "ssm_a_im", "ssm_log_dt", "ssm_b_re", "ssm_b_im",
         "ssm_c_re", "ssm_c_im", "ssm_d", "ffn2_norm", "final_norm"]
WEIGHTS = ["meta_tokens", "ffn1_norm", "ffn1_w_gate", "ffn1_w_up", "ffn1_w_down", "mix_norm", "w_in", "attn_sinks",
           "ssm_a_re", "ssm_a_im", "ssm_log_dt", "ssm_b_re", "ssm_b_im", "ssm_c_re", "ssm_c_im", "ssm_d",
           "w_attn_proj", "w_glu_v", "w_glu_g", "w_out", "ffn2_norm", "ffn2_w_gate", "ffn2_w_up", "ffn2_w_down",
           "final_norm"]


def _small_rows(shape):
    rows = -(-math.prod(shape) // LANES)
    return -(-rows // SUBLANES) * SUBLANES


def _pack_small(tree):
    parts = []
    for k in SMALL + ["meta_tokens"]:
        size, rows = math.prod(tree[k].shape), _small_rows(tree[k].shape)
        if size % LANES == 0:
            part = tree[k].reshape(size // LANES, LANES)
        else:
            part = jnp.pad(tree[k].reshape(1, size), ((0, 0), (0, LANES - size)))
        parts.append(jnp.pad(part, ((0, rows - part.shape[0]), (0, 0))))
    return jnp.concatenate(parts, axis=0)


def _unpack_small(packed, like):
    out, off = {}, 0
    for k in SMALL + ["meta_tokens"]:
        size, rows = math.prod(like[k].shape), _small_rows(like[k].shape)
        if size % LANES == 0:
            out[k] = packed[off:off + size // LANES].reshape(like[k].shape)
        else:
            out[k] = packed[off, :size].reshape(like[k].shape)
        off += rows
    return out


def kernel(x, meta_tokens, ffn1_norm, ffn1_w_gate, ffn1_w_up, ffn1_w_down, mix_norm, w_in, attn_sinks, ssm_a_re, ssm_a_im, ssm_log_dt, ssm_b_re, ssm_b_im, ssm_c_re, ssm_c_im, ssm_d, w_attn_proj, w_glu_v, w_glu_g, w_out, ffn2_norm, ffn2_w_gate, ffn2_w_up, ffn2_w_down, final_norm, loss_target, m_meta_tokens, m_ffn1_norm, m_ffn1_w_gate, m_ffn1_w_up, m_ffn1_w_down, m_mix_norm, m_w_in, m_attn_sinks, m_ssm_a_re, m_ssm_a_im, m_ssm_log_dt, m_ssm_b_re, m_ssm_b_im, m_ssm_c_re, m_ssm_c_im, m_ssm_d, m_w_attn_proj, m_w_glu_v, m_w_glu_g, m_w_out, m_ffn2_norm, m_ffn2_w_gate, m_ffn2_w_up, m_ffn2_w_down, m_final_norm, v_meta_tokens, v_ffn1_norm, v_ffn1_w_gate, v_ffn1_w_up, v_ffn1_w_down, v_mix_norm, v_w_in, v_attn_sinks, v_ssm_a_re, v_ssm_a_im, v_ssm_log_dt, v_ssm_b_re, v_ssm_b_im, v_ssm_c_re, v_ssm_c_im, v_ssm_d, v_w_attn_proj, v_w_glu_v, v_w_glu_g, v_w_out, v_ffn2_norm, v_ffn2_w_gate, v_ffn2_w_up, v_ffn2_w_down, v_final_norm):
    args = dict(locals())
    w = {k: args[k] for k in WEIGHTS}
    m = {k: args["m_" + k] for k in WEIGHTS}
    v = {k: args["v_" + k] for k in WEIGHTS}
    depth = ffn1_norm.shape[0]
    seq, d = x.shape[1], x.shape[2]
    lp = seq + BLOCK
    xi, yi, ci = _mesh_pos()
    pos = jnp.stack([ci, 2 * xi + yi]).astype(jnp.int32)

    tabs = _rope_tables(lp)
    (meta_all,) = _all_gather_chips([meta_tokens], "gather_meta")
    meta_full = jnp.concatenate([meta_all[j] for j in range(N_CHIPS)], axis=1)
    layers, pieces = [], []
    f4 = ffn1_w_gate.shape[2]
    fp = -(-f4 // MXU_DIM) * MXU_DIM

    def ffn_rows(wt):
        return jnp.pad(wt, ((0, fp - f4), (0, 0))).astype(BF16)

    for l in range(depth):
        pieces += [
            ((l, "wg1"), ffn_rows(ffn1_w_gate[l].T)), ((l, "wu1"), ffn_rows(ffn1_w_up[l].T)),
            ((l, "wd1"), ffn_rows(ffn1_w_down[l])), ((l, "w_in"), w_in[l].astype(BF16)),
            ((l, "w3"), jnp.stack([w_attn_proj[l].T, w_glu_v[l].T, w_glu_g[l].T]).astype(BF16)),
            ((l, "w_out"), w_out[l].astype(BF16)),
            ((l, "wg2"), ffn_rows(ffn2_w_gate[l].T)), ((l, "wu2"), ffn_rows(ffn2_w_up[l].T)),
            ((l, "wd2"), ffn_rows(ffn2_w_down[l]))]
        lb_re, lb_im, bb_re, bb_im = _ssm_params(ssm_a_re[l], ssm_a_im[l], ssm_log_dt[l], ssm_b_re[l], ssm_b_im[l])
        ngrp = lb_re.shape[0]
        nt = ngrp // GROUPS_PER_TILE
        ssm_tabs = (lb_re.reshape(nt, 1, TILE_STATES), lb_im.reshape(nt, 1, TILE_STATES),
                    *_ssm_tables(bb_re, bb_im, ssm_c_re[l], ssm_c_im[l]))
        layers.append(dict(
            ffn1_norm=ffn1_norm[l], mix_norm=mix_norm[l], ffn2_norm=ffn2_norm[l], attn_sinks=attn_sinks[l],
            ssm_d=ssm_d[l], ssm_tabs=ssm_tabs, f4=f4))
    ws = _WeightStream(pieces)
    ws.get((0, "wu1"))

    h = jnp.concatenate([jnp.zeros((PAD_FRONT, d), F32), meta_full, x[0]], axis=0)
    saved = []
    for l in range(depth):
        h, s = _layer_fwd(h, l, layers[l], ws, tabs)
        saved.append(s)
    dh, g_final, loss_acc, dhb = _loss_head(h, final_norm, loss_target[0])
    dh_pair = (dh, dhb)
    loss = lax.psum(loss_acc[0, 0], ("x", "y", "c"))

    grads, reds = [None] * depth, [None] * depth
    for l in reversed(range(depth)):
        dh_pair, grads[l], reds[l] = _layer_bwd(dh_pair, l, layers[l], ws, saved[l], tabs, pos)
    dh = dh_pair[0]
    grad_x = dh[BLOCK:][None]
    dmeta_local = dh[PAD_FRONT:BLOCK]

    small = {k: [] for k in SMALL}
    for l in range(depth):
        gl = grads[l]
        _, vjp = jax.vjp(_ssm_params, ssm_a_re[l], ssm_a_im[l], ssm_log_dt[l], ssm_b_re[l], ssm_b_im[l])
        da_re, da_im, dlog_dt, db_re, db_im = vjp(gl["ssm_lam"])
        for k, val in (("ffn1_norm", gl["ffn1_norm"][0]), ("mix_norm", gl["mix_norm"][0]),
                       ("attn_sinks", gl["attn_sinks"]), ("ssm_a_re", da_re), ("ssm_a_im", da_im),
                       ("ssm_log_dt", dlog_dt), ("ssm_b_re", db_re), ("ssm_b_im", db_im),
                       ("ssm_c_re", gl["ssm_c_re"]), ("ssm_c_im", gl["ssm_c_im"]), ("ssm_d", gl["ssm_d"]),
                       ("ffn2_norm", gl["ffn2_norm"][0])):
            small[k].append(val)
    small_local = {k: jnp.stack(vals) for k, vals in small.items() if k != "final_norm"}
    small_local["final_norm"] = g_final[0]
    small_local["meta_tokens"] = dmeta_local
    like = dict(small_local)
    g_small = _sum_devices(_all_gather_devices(_pack_small(small_local), "gather_small_grads"), "sum_small_grads")
    g_small_tree = _unpack_small(g_small, like)
    d4 = d // N_CHIPS
    chip = 2 * xi + yi
    g_meta = lax.dynamic_slice_in_dim(g_small_tree["meta_tokens"], chip * d4, d4, axis=1)

    reduced = []
    for l in range(depth):
        mine, other = [], []
        for red in reds[l]:
            halves, sibling_halves = red.end()
            mine += halves
            other += sibling_halves
        reduced.append((mine, other))

    g_out, delta, new_m, new_v = {}, {}, {}, {}
    for i, k in enumerate(BIG):
        flip = (lambda t: jnp.swapaxes(t, 1, 2)) if k in TRANSPOSED else (lambda t: t)
        outs = _adamw_layers(
            flip(w[k]), flip(m[k]), flip(v[k]), [reduced[l][0][i] for l in range(depth)],
            [reduced[l][1][i] for l in range(depth)], pos, "adamw_" + k)
        g_out[k], delta[k], new_m[k], new_v[k] = [flip(t) for t in outs]
    g_small_tree["meta_tokens"] = g_meta
    for k in SMALL + ["meta_tokens"]:
        shape = w[k].shape if w[k].ndim > 1 else (1,) + w[k].shape
        outs = _adamw_whole(w[k].reshape(shape), g_small_tree[k].reshape(shape), m[k].reshape(shape),
                            v[k].reshape(shape), "adamw_" + k)
        g_out[k] = g_small_tree[k]
        delta[k], new_m[k], new_v[k] = [t.reshape(w[k].shape) for t in outs]

    return (loss, grad_x, *[g_out[k] for k in WEIGHTS], *[delta[k] for k in WEIGHTS],
            *[new_m[k] for k in WEIGHTS], *[new_v[k] for k in WEIGHTS])
```

```python
import functools
import math

import jax
import jax.numpy as jnp
from jax import lax
from jax.experimental import pallas as pl
from jax.experimental.pallas import tpu as pltpu

F32 = jnp.float32
BF16 = jnp.bfloat16

N_META = 16
HEAD_DIM = 64
N_Q_HEADS = 8
N_KV_HEADS = 2
Q_PER_KV = N_Q_HEADS // N_KV_HEADS
ATTN_WIDTH = N_Q_HEADS * HEAD_DIM
KV_WIDTH = N_KV_HEADS * HEAD_DIM
BLOCK = 128
PAD_FRONT = BLOCK - N_META
ROPE_THETA = 500000.0
ROT_DIM = HEAD_DIM // 4
SSM_GROUP = 16
SSM_STATE = 64
GROUPS_PER_TILE = 4
TILE_STATES = GROUPS_PER_TILE * SSM_STATE
LANES = 128
SUBLANES = 8
MXU_DIM = 256
EPS = 1e-6
NEG_INF = -1e30
N_CHIPS = 4

ADAM_LR = 0.001
ADAM_B1 = 0.9
ADAM_B2 = 0.999
ADAM_EPS = 1e-08
ADAM_WD = 0.01
ADAM_STEP = 10

VMEM_LIMIT = 56 * 1024 * 1024
MESH = pl.DeviceIdType.MESH


def _cparams(sem=None):
    return pltpu.CompilerParams(dimension_semantics=sem, vmem_limit_bytes=VMEM_LIMIT)


def _row_tile(rows, limit=512):
    best = None
    for t in range(128, limit + 1, 128):
        if rows % t == 0:
            best = t
    assert best is not None, rows
    return best


def _div_tile(rows, row_bytes, max_bytes=1 << 20, mult=8):
    best = None
    for t in range(mult, rows + 1, mult):
        if rows % t == 0 and t * row_bytes <= max_bytes:
            best = t
    if best is None:
        best = rows
    return best


def _dot(a, b, mode):
    if mode == "nn":
        dims = (((1,), (0,)), ((), ()))
    elif mode == "nt":
        dims = (((1,), (1,)), ((), ()))
    else:
        dims = (((0,), (0,)), ((), ()))
    return lax.dot_general(a.astype(BF16), b.astype(BF16), dims, preferred_element_type=F32)


def _sigmoid(x):
    return 1.0 / (1.0 + jnp.exp(-x))


_GELU_C = math.sqrt(2.0 / math.pi)


def _gelu(x):
    return 0.5 * x * (1.0 + jnp.tanh(_GELU_C * (x + 0.044715 * x * x * x)))


def _gelu_grad(x):
    t = jnp.tanh(_GELU_C * (x + 0.044715 * x * x * x))
    return 0.5 * (1.0 + t) + 0.5 * x * (1.0 - t * t) * _GELU_C * (1.0 + 3.0 * 0.044715 * x * x)


class _Comm:
    def __init__(self, tag, ins, out_shapes, sems, start, mid, finish):
        self.tag, self.ins, self.out_shapes, self.sems = tag, list(ins), list(out_shapes), list(sems)
        self.start, self.mid, self.finish = start, mid, finish


HBM_SPEC = pl.BlockSpec(memory_space=pltpu.HBM)


def _hosted_call(body, comm, *, out_shape, grid, in_specs, out_specs, scratch_shapes, sem, name, args):
    out_shape, in_specs, out_specs = list(out_shape), list(in_specs), list(out_specs)
    scratch_shapes = list(scratch_shapes)
    if comm is None:
        res = pl.pallas_call(
            body, out_shape=out_shape, grid=grid, in_specs=in_specs, out_specs=out_specs,
            scratch_shapes=scratch_shapes, compiler_params=_cparams(sem), name=name)(*args)
        return list(res), []
    n_in, n_out, n_sc = len(args), len(out_shape), len(scratch_shapes)
    nci, nco = len(comm.ins), len(comm.out_shapes)
    total = math.prod(grid)

    def wrapped(*refs):
        in_refs, cin = refs[:n_in], refs[n_in:n_in + nci]
        o0 = n_in + nci
        out_refs, cout = refs[o0:o0 + n_out], refs[o0 + n_out:o0 + n_out + nco]
        s0 = o0 + n_out + nco
        sc, csem = refs[s0:s0 + n_sc], refs[s0 + n_sc:]
        lin = 0
        for dim, size in enumerate(grid):
            lin = lin * size + pl.program_id(dim)

        @pl.when(lin == 0)
        def _():
            comm.start(cin, cout, csem)

        if comm.mid is not None:
            @pl.when(lin == total // 2)
            def _():
                comm.mid(cin, cout, csem)

        body(*in_refs, *out_refs, *sc)

        @pl.when(lin == total - 1)
        def _():
            comm.finish(cin, cout, csem)

    res = pl.pallas_call(
        wrapped, out_shape=out_shape + comm.out_shapes, grid=grid,
        in_specs=in_specs + [HBM_SPEC] * nci, out_specs=out_specs + [HBM_SPEC] * nco,
        scratch_shapes=scratch_shapes + comm.sems,
        compiler_params=_cparams(("arbitrary",) * len(grid)), name=name + "_" + comm.tag)(*args, *comm.ins)
    return list(res[:n_out]), list(res[n_out:])


def _matmul(name, grid, k_axis, ins, in_specs, pairs, acc_shapes, epilogue, out_shapes, out_specs, sem, comm=None):
    n_in, n_out, n_acc = len(ins), len(out_shapes), len(acc_shapes)

    def body(*refs):
        in_refs = refs[:n_in]
        out_refs = refs[n_in:n_in + n_out]
        acc_refs = refs[n_in + n_out:]
        if k_axis is None:
            accs = [None] * n_acc
            for ia, ib, mode, iacc in pairs:
                d = _dot(in_refs[ia][...], in_refs[ib][...], mode)
                accs[iacc] = d if accs[iacc] is None else accs[iacc] + d
            epilogue(accs, in_refs, out_refs)
            return
        k = pl.program_id(k_axis)

        @pl.when(k == 0)
        def _():
            for r in acc_refs:
                r[...] = jnp.zeros_like(r)

        for ia, ib, mode, iacc in pairs:
            acc_refs[iacc][...] += _dot(in_refs[ia][...], in_refs[ib][...], mode)

        @pl.when(k == pl.num_programs(k_axis) - 1)
        def _():
            epilogue([r[...] for r in acc_refs], in_refs, out_refs)

    scratch = [] if k_axis is None else [pltpu.VMEM(s, F32) for s in acc_shapes]
    outs, couts = _hosted_call(
        body, comm, out_shape=out_shapes, grid=grid, in_specs=in_specs, out_specs=out_specs,
        scratch_shapes=scratch, sem=sem, name=name, args=ins)
    return outs if comm is None else (outs, couts)


def _rms_fwd(h, g, name):
    lp, d = h.shape
    tm = _row_tile(lp)

    def body(h_ref, g_ref, n_ref):
        x = h_ref[...]
        r = lax.rsqrt(jnp.mean(x * x, axis=-1, keepdims=True) + EPS)
        n_ref[...] = (x * r * g_ref[...]).astype(BF16)

    return pl.pallas_call(
        body, out_shape=jax.ShapeDtypeStruct((lp, d), BF16), grid=(lp // tm,),
        in_specs=[pl.BlockSpec((tm, d), lambda i: (i, 0)), pl.BlockSpec((1, d), lambda i: (0, 0))],
        out_specs=pl.BlockSpec((tm, d), lambda i: (i, 0)),
        compiler_params=_cparams(("parallel",)), name=name)(h, g.reshape(1, d))


def _rms_bwd_math(dn, x, g):
    r = lax.rsqrt(jnp.mean(x * x, axis=-1, keepdims=True) + EPS)
    xh = x * r
    dxh = dn * g
    dx = r * (dxh - xh * jnp.mean(dxh * xh, axis=-1, keepdims=True))
    return dx, dn * xh


def _scale_cast(x, scale, name):
    lp, d = x.shape
    tm = _row_tile(lp)

    def body(x_ref, o_ref):
        o_ref[...] = (x_ref[...] * scale).astype(BF16)

    return pl.pallas_call(
        body, out_shape=jax.ShapeDtypeStruct((lp, d), BF16), grid=(lp // tm,),
        in_specs=[pl.BlockSpec((tm, d), lambda i: (i, 0))], out_specs=pl.BlockSpec((tm, d), lambda i: (i, 0)),
        compiler_params=_cparams(("parallel",)), name=name)(x)


def _ffn_up(h, gain, wgt, wut, comm=None):
    lp, d = h.shape
    fp = wgt.shape[1]
    tm = _row_tile(lp)
    n = _rms_fwd(h, gain, "rms_fwd_ffn")

    def up_body(n_ref, wg_ref, wu_ref, a_ref, b_ref, s_ref):
        x = n_ref[...]
        for jc in range(N_CHIPS):
            cols = slice(jc * fp, (jc + 1) * fp)
            a = _dot(x, wg_ref[jc], "nt")
            b = _dot(x, wu_ref[jc], "nt")
            a_ref[:, cols] = a.astype(BF16)
            b_ref[:, cols] = b.astype(BF16)
            s_ref[:, cols] = (a * _sigmoid(a) * b).astype(BF16)

    ff = N_CHIPS * fp
    act = jax.ShapeDtypeStruct((lp, ff), BF16)
    act_tile = pl.BlockSpec((tm, ff), lambda i: (i, 0))
    w_spec = pl.BlockSpec((N_CHIPS, fp, d), lambda i: (0, 0, 0))
    outs, couts = _hosted_call(
        up_body, comm, out_shape=[act, act, act], grid=(lp // tm,),
        in_specs=[pl.BlockSpec((tm, d), lambda i: (i, 0)), w_spec, w_spec],
        out_specs=[act_tile] * 3, scratch_shapes=[], sem=("parallel",), name="ffn_up", args=(n, wgt, wut))
    return (*outs, n), couts


def _ffn_down(s, wd, h, comm=None):
    lp, d = h.shape
    ff = s.shape[1]
    tm = _row_tile(lp)

    def down_epi(accs, in_refs, out_refs):
        out_refs[0][...] = in_refs[2][...] + 0.5 * accs[0]

    res = _matmul(
        "ffn_down", (lp // tm,), None, [s, wd.reshape(ff, d), h],
        [pl.BlockSpec((tm, ff), lambda i: (i, 0)), pl.BlockSpec((ff, d), lambda i: (0, 0)),
         pl.BlockSpec((tm, d), lambda i: (i, 0))],
        [(0, 1, "nn", 0)], [(tm, d)], down_epi,
        [jax.ShapeDtypeStruct((lp, d), F32)], [pl.BlockSpec((tm, d), lambda i: (i, 0))],
        ("parallel",), comm)
    return (res[0], []) if comm is None else (res[0][0], res[1])


def _tn_tiles(lp):
    return _row_tile(lp, 1408)


def _ffn_bwd(dh_pair, h_in, gain, wgt, wut, wd, f4, saved, pos, comm=None):
    dh, dhb = dh_pair
    a, b, s, n = saved
    lp, d = h_in.shape
    fp = wgt.shape[1]
    ff = N_CHIPS * fp
    tm = _row_tile(lp)
    ni = lp // tm
    tk = _tn_tiles(lp)
    nk = lp // tk

    def ds_body(dh_ref, wd_ref, a_ref, b_ref, da_ref, db_ref):
        x = dh_ref[...]
        for jc in range(N_CHIPS):
            cols = slice(jc * fp, (jc + 1) * fp)
            ds = 0.5 * _dot(x, wd_ref[jc], "nt")
            av = a_ref[:, cols].astype(F32)
            bv = b_ref[:, cols].astype(F32)
            sg = _sigmoid(av)
            da_ref[:, cols] = (ds * bv * sg * (1.0 + av * (1.0 - sg))).astype(BF16)
            db_ref[:, cols] = (ds * av * sg).astype(BF16)

    act = jax.ShapeDtypeStruct((lp, ff), BF16)
    act_tile = pl.BlockSpec((tm, ff), lambda i: (i, 0))
    (da, db), couts = _hosted_call(
        ds_body, comm, out_shape=[act, act], grid=(ni,),
        in_specs=[pl.BlockSpec((tm, d), lambda i: (i, 0)), pl.BlockSpec((N_CHIPS, fp, d), lambda i: (0, 0, 0)),
                  act_tile, act_tile],
        out_specs=[act_tile, act_tile], scratch_shapes=[], sem=("parallel",), name="ffn_bwd_ds",
        args=(dhb, wd, a, b))

    dw_shape = jax.ShapeDtypeStruct((N_CHIPS, f4, d), F32)
    dw_spec = pl.BlockSpec((None, f4, d), lambda j, k: (j, 0, 0))
    in_col = pl.BlockSpec((tk, fp), lambda j, k: (k, j))
    in_row = pl.BlockSpec((tk, d), lambda j, k: (k, 0))

    def dwd_epi(accs, in_refs, out_refs):
        out_refs[0][...] = 0.5 * accs[0][:f4]

    (dwd,) = _matmul(
        "ffn_dwd", (N_CHIPS, nk), 1, [s, dhb], [in_col, in_row],
        [(0, 1, "tn", 0)], [(fp, d)], dwd_epi, [dw_shape], [dw_spec], ("parallel", "arbitrary"))

    def dwgu_epi(accs, in_refs, out_refs):
        for acc, o in zip(accs, out_refs):
            o[...] = acc[:f4]

    red_down = _Reduction([dwd], pos, "ffn_d")
    (dwg, dwu), red_down.got = _matmul(
        "ffn_dwgu", (N_CHIPS, nk), 1, [n, da, db], [in_row, in_col, in_col],
        [(1, 0, "tn", 0), (2, 0, "tn", 1)], [(fp, d)] * 2, dwgu_epi,
        [dw_shape, dw_shape], [dw_spec, dw_spec], ("parallel", "arbitrary"), red_down.comm())

    def dn_epi(accs, in_refs, out_refs):
        i = pl.program_id(0)
        dx, dgrow = _rms_bwd_math(accs[0], in_refs[5][...], in_refs[6][...])
        dh_new = in_refs[4][...] + dx
        out_refs[0][...] = dh_new
        out_refs[2][...] = dh_new.astype(BF16)

        @pl.when(i == 0)
        def _():
            out_refs[1][...] = jnp.zeros_like(out_refs[1])

        out_refs[1][...] += jnp.sum(dgrow, axis=0, keepdims=True)

    red = _Reduction([dwg, dwu], pos, "ffn_gu")
    row_spec = pl.BlockSpec((tm, d), lambda i: (i, 0))
    act_spec = pl.BlockSpec((tm, ff), lambda i: (i, 0))
    w_spec = pl.BlockSpec((ff, d), lambda i: (0, 0))
    one_spec = pl.BlockSpec((1, d), lambda i: (0, 0))
    (dh_in, dgain, dh_in_b), red.got = _matmul(
        "ffn_bwd_dn", (ni,), None, [da, wgt.reshape(ff, d), db, wut.reshape(ff, d), dh, h_in, gain.reshape(1, d)],
        [act_spec, w_spec, act_spec, w_spec, row_spec, row_spec, one_spec],
        [(0, 1, "nn", 0), (2, 3, "nn", 0)], [(tm, d)], dn_epi,
        [jax.ShapeDtypeStruct((lp, d), F32), jax.ShapeDtypeStruct((1, d), F32), jax.ShapeDtypeStruct((lp, d), BF16)],
        [row_spec, one_spec, row_spec], ("arbitrary",), red.comm())
    return (dh_in, dh_in_b), dgain, [red, red_down], couts


def _rope_tables(lp):
    pos = jnp.arange(lp, dtype=F32) - float(PAD_FRONT)
    inv_freq = ROPE_THETA ** (-jnp.arange(0, ROT_DIM, 2, dtype=F32) / ROT_DIM)
    ang = pos[:, None] * inv_freq[None, :]
    cos, sin = jnp.cos(ang), jnp.sin(ang)
    half = ROT_DIM // 2
    ones = jnp.ones((lp, HEAD_DIM - ROT_DIM), F32)
    zeros_h = jnp.zeros((lp, half), F32)
    zeros_r = jnp.zeros((lp, HEAD_DIM - ROT_DIM), F32)
    c = jnp.concatenate([cos, cos, ones], axis=1)
    s1 = jnp.concatenate([-sin, zeros_h, zeros_r], axis=1)
    s2 = jnp.concatenate([zeros_h, sin, zeros_r], axis=1)
    reps = LANES // HEAD_DIM
    return jnp.stack([jnp.tile(c, (1, reps)), jnp.tile(s1, (1, reps)), jnp.tile(s2, (1, reps))])


def _rope(x, c, s1, s2):
    half = ROT_DIM // 2
    outs = []
    for ch in range(x.shape[1] // LANES):
        xc = x[:, ch * LANES:(ch + 1) * LANES]
        outs.append(xc * c + pltpu.roll(xc, LANES - half, 1) * s1 + pltpu.roll(xc, half, 1) * s2)
    return outs[0] if len(outs) == 1 else jnp.concatenate(outs, axis=1)


def _rope_t(dy, c, s1, s2):
    half = ROT_DIM // 2
    outs = []
    for ch in range(dy.shape[1] // LANES):
        dc = dy[:, ch * LANES:(ch + 1) * LANES]
        outs.append(dc * c + pltpu.roll(dc * s1, half, 1) + pltpu.roll(dc * s2, LANES - half, 1))
    return outs[0] if len(outs) == 1 else jnp.concatenate(outs, axis=1)


def _in_proj(n, w_in, tabs, ssm_w, comm=None):
    lp, d = n.shape
    inw = w_in.shape[1]
    tm = _row_tile(lp)
    o1 = ATTN_WIDTH
    o2 = o1 + KV_WIDTH
    o3 = o2 + KV_WIDTH
    o4 = o3 + ssm_w
    o5 = o4 + d

    def epi(accs, in_refs, out_refs):
        z = accs[0]
        c, s1, s2 = in_refs[2][0], in_refs[2][1], in_refs[2][2]
        out_refs[0][...] = _rope(z[:, :o1], c, s1, s2).astype(BF16)
        out_refs[1][...] = _rope(z[:, o1:o2], c, s1, s2).astype(BF16)
        out_refs[2][...] = z[:, o2:o3].astype(BF16)
        out_refs[3][...] = z[:, o3:o4]
        out_refs[4][...] = z[:, o4:o5]
        out_refs[5][...] = z[:, o5:]

    def rs(w, dt):
        return jax.ShapeDtypeStruct((lp, w), dt), pl.BlockSpec((tm, w), lambda i: (i, 0))

    shapes, specs = zip(rs(o1, BF16), rs(KV_WIDTH, BF16), rs(KV_WIDTH, BF16), rs(ssm_w, F32), rs(d, F32), rs(d, F32))
    res = _matmul(
        "mix_in_proj", (lp // tm,), None, [n, w_in, tabs],
        [pl.BlockSpec((tm, d), lambda i: (i, 0)), pl.BlockSpec((d, inw), lambda i: (0, 0)),
         pl.BlockSpec((3, tm, LANES), lambda i: (0, i, 0))],
        [(0, 1, "nn", 0)], [(tm, inw)], epi, list(shapes), list(specs), ("parallel",), comm)
    return (res, []) if comm is None else res


def _attn_mask(b):
    rows = lax.broadcasted_iota(jnp.int32, (BLOCK, 3 * BLOCK), 0)
    cols = lax.broadcasted_iota(jnp.int32, (BLOCK, 3 * BLOCK), 1)
    qpos = b * BLOCK + rows - PAD_FRONT
    kpos = (b - 1) * BLOCK + cols - PAD_FRONT
    dist = qpos - kpos
    band = (cols < 2 * BLOCK) & (kpos >= N_META) & (dist >= 0) & (dist < BLOCK)
    mrow = cols - 2 * BLOCK
    meta = (mrow >= PAD_FRONT) & ((mrow - PAD_FRONT) <= qpos)
    return band | meta


def _attn_probs(qh, kk, mask, sink):
    s = _dot(qh, kk, "nt") * (HEAD_DIM ** -0.5)
    s = jnp.where(mask, s, NEG_INF)
    m = jnp.maximum(jnp.max(s, axis=-1, keepdims=True), sink)
    e = jnp.exp(s - m)
    es = jnp.exp(sink - m)
    z = jnp.sum(e, axis=-1, keepdims=True) + es
    inv = 1.0 / z
    return e * inv, es * inv


def _head(ref_or_val, h):
    return ref_or_val[:, h * HEAD_DIM:(h + 1) * HEAD_DIM]


def _attn_fwd(q, k, v, sinks, comm=None):
    lp = q.shape[0]
    nb = lp // BLOCK

    def body(sink_ref, q_ref, kp_ref, kc_ref, km_ref, vp_ref, vc_ref, vm_ref, o_ref):
        b = pl.program_id(0)
        mask = _attn_mask(b)
        for hk in range(N_KV_HEADS):
            kk = jnp.concatenate([_head(kp_ref, hk), _head(kc_ref, hk), _head(km_ref, hk)], axis=0)
            vv = jnp.concatenate([_head(vp_ref, hk), _head(vc_ref, hk), _head(vm_ref, hk)], axis=0)
            for g in range(Q_PER_KV):
                h = hk * Q_PER_KV + g
                p, _ = _attn_probs(_head(q_ref, h), kk, mask, sink_ref[h])
                o_ref[:, h * HEAD_DIM:(h + 1) * HEAD_DIM] = _dot(p, vv, "nn").astype(BF16)

    cur = lambda b: (b, 0)
    prev = lambda b: (jnp.maximum(b - 1, 0), 0)
    first = lambda b: (0, 0)
    kvs = lambda f: pl.BlockSpec((BLOCK, KV_WIDTH), f)
    (o,), couts = _hosted_call(
        body, comm, out_shape=[jax.ShapeDtypeStruct((lp, ATTN_WIDTH), BF16)], grid=(nb,),
        in_specs=[pl.BlockSpec(memory_space=pltpu.SMEM), pl.BlockSpec((BLOCK, ATTN_WIDTH), cur),
                  kvs(prev), kvs(cur), kvs(first), kvs(prev), kvs(cur), kvs(first)],
        out_specs=[pl.BlockSpec((BLOCK, ATTN_WIDTH), cur)], scratch_shapes=[],
        sem=("parallel",), name="attn_fwd", args=(sinks, q, k, k, k, v, v, v))
    return o, couts


def _attn_bwd(q, k, v, do, sinks, tabs, comm=None):
    lp = q.shape[0]
    nb = lp // BLOCK
    scale = HEAD_DIM ** -0.5

    def body(sink_ref, q_ref, do_ref, kp_ref, kc_ref, km_ref, vp_ref, vc_ref, vm_ref, tq_ref, tk_ref, t0_ref,
             dq_ref, dk_ref, dv_ref, dkm_ref, dvm_ref, dsink_ref,
             dq_s, dkk_s, dvv_s, ck_s, cv_s, mk_s, mv_s):
        b = pl.program_id(0)

        @pl.when(b == 0)
        def _():
            for r in (ck_s, cv_s, mk_s, mv_s, dsink_ref):
                r[...] = jnp.zeros_like(r)

        @pl.when(b < nb)
        def _():
            mask = _attn_mask(b)
            for hk in range(N_KV_HEADS):
                kk = jnp.concatenate([_head(kp_ref, hk), _head(kc_ref, hk), _head(km_ref, hk)], axis=0)
                vv = jnp.concatenate([_head(vp_ref, hk), _head(vc_ref, hk), _head(vm_ref, hk)], axis=0)
                dkk = jnp.zeros((3 * BLOCK, HEAD_DIM), F32)
                dvv = jnp.zeros((3 * BLOCK, HEAD_DIM), F32)
                for g in range(Q_PER_KV):
                    h = hk * Q_PER_KV + g
                    qh = _head(q_ref, h)
                    doh = _head(do_ref, h)
                    p, ps = _attn_probs(qh, kk, mask, sink_ref[h])
                    dp = _dot(doh, vv, "nt")
                    delta = jnp.sum(p * dp, axis=-1, keepdims=True)
                    ds = (p * (dp - delta)).astype(BF16)
                    dsink_ref[h:h + 1, :] += jnp.zeros((1, LANES), F32) - jnp.sum(ps * delta)
                    dq_s[:, h * HEAD_DIM:(h + 1) * HEAD_DIM] = _dot(ds, kk, "nn") * scale
                    dkk = dkk + _dot(ds, qh, "tn") * scale
                    dvv = dvv + _dot(p, doh, "tn")
                dkk_s[:, hk * HEAD_DIM:(hk + 1) * HEAD_DIM] = dkk
                dvv_s[:, hk * HEAD_DIM:(hk + 1) * HEAD_DIM] = dvv
            dq_ref[...] = _rope_t(dq_s[...], tq_ref[0], tq_ref[1], tq_ref[2])
            dk_ref[...] = _rope_t(ck_s[...] + dkk_s[0:BLOCK, :], tk_ref[0], tk_ref[1], tk_ref[2])
            dv_ref[...] = cv_s[...] + dvv_s[0:BLOCK, :]
            ck_s[...] = dkk_s[BLOCK:2 * BLOCK, :]
            cv_s[...] = dvv_s[BLOCK:2 * BLOCK, :]
            mk_s[...] += dkk_s[2 * BLOCK:, :]
            mv_s[...] += dvv_s[2 * BLOCK:, :]

        @pl.when(b == nb)
        def _():
            dk_ref[...] = _rope_t(ck_s[...], tk_ref[0], tk_ref[1], tk_ref[2])
            dv_ref[...] = cv_s[...]
            dkm_ref[...] = _rope_t(mk_s[...], t0_ref[0], t0_ref[1], t0_ref[2])
            dvm_ref[...] = mv_s[...]

    cur = lambda b: (jnp.minimum(b, nb - 1), 0)
    prev = lambda b: (jnp.clip(b - 1, 0, nb - 1), 0)
    first = lambda b: (0, 0)
    kvs = lambda f: pl.BlockSpec((BLOCK, KV_WIDTH), f)
    tab = lambda f: pl.BlockSpec((3, BLOCK, LANES), lambda b: (0,) + f(b)[:1] + (0,))
    kv_out = lambda b: (jnp.maximum(b - 1, 0), 0)
    return _hosted_call(
        body, comm,
        out_shape=[jax.ShapeDtypeStruct((lp, ATTN_WIDTH), F32), jax.ShapeDtypeStruct((lp, KV_WIDTH), F32),
                   jax.ShapeDtypeStruct((lp, KV_WIDTH), F32), jax.ShapeDtypeStruct((BLOCK, KV_WIDTH), F32),
                   jax.ShapeDtypeStruct((BLOCK, KV_WIDTH), F32), jax.ShapeDtypeStruct((N_Q_HEADS, LANES), F32)],
        grid=(nb + 1,),
        in_specs=[pl.BlockSpec(memory_space=pltpu.SMEM), pl.BlockSpec((BLOCK, ATTN_WIDTH), cur),
                  pl.BlockSpec((BLOCK, ATTN_WIDTH), cur),
                  kvs(prev), kvs(cur), kvs(first), kvs(prev), kvs(cur), kvs(first),
                  tab(cur), tab(kv_out), tab(first)],
        out_specs=[pl.BlockSpec((BLOCK, ATTN_WIDTH), cur), kvs(kv_out), kvs(kv_out), kvs(first), kvs(first),
                   pl.BlockSpec((N_Q_HEADS, LANES), first)],
        scratch_shapes=[pltpu.VMEM((BLOCK, ATTN_WIDTH), F32), pltpu.VMEM((3 * BLOCK, KV_WIDTH), F32),
                        pltpu.VMEM((3 * BLOCK, KV_WIDTH), F32), pltpu.VMEM((BLOCK, KV_WIDTH), F32),
                        pltpu.VMEM((BLOCK, KV_WIDTH), F32), pltpu.VMEM((BLOCK, KV_WIDTH), F32),
                        pltpu.VMEM((BLOCK, KV_WIDTH), F32)],
        sem=("arbitrary",), name="attn_bwd", args=(sinks, q, do, k, k, k, v, v, v, tabs, tabs, tabs))


def _cmul(ar, ai, br, bi):
    return ar * br - ai * bi, ar * bi + ai * br


def _cpow(lr, li, n):
    rr = ri = None
    br, bi = lr, li
    while n:
        if n & 1:
            rr, ri = (br, bi) if rr is None else _cmul(rr, ri, br, bi)
        n >>= 1
        if n:
            br, bi = _cmul(br, bi, br, bi)
    return rr, ri


def _shift_rows(x, d, reverse):
    rows = lax.broadcasted_iota(jnp.int32, x.shape, 0)
    if not reverse:
        return jnp.where(rows >= d, pltpu.roll(x, d, 0), 0.0)
    return jnp.where(rows < SUBLANES - d, pltpu.roll(x, SUBLANES - d, 0), 0.0)


def _sublane_powers(mr, mi, reverse):
    rows = lax.broadcasted_iota(jnp.int32, mr.shape, 0)
    e = SUBLANES - 1 - rows if reverse else rows
    pr, pi = jnp.ones_like(mr), jnp.zeros_like(mr)
    br, bi = mr, mi
    for d in (1, 2, 4):
        tr, ti = _cmul(pr, pi, br, bi)
        on = (e & d) != 0
        pr, pi = jnp.where(on, tr, pr), jnp.where(on, ti, pi)
        if d < 4:
            br, bi = _cmul(br, bi, br, bi)
    return pr, pi


def _inclusive_prefix(er, ei, mr, mi, reverse):
    ir, ii, pr, pi = er, ei, mr, mi
    for d in (1, 2, 4):
        tr, ti = _cmul(pr, pi, _shift_rows(ir, d, reverse), _shift_rows(ii, d, reverse))
        ir, ii = ir + tr, ii + ti
        if d < 4:
            pr, pi = _cmul(pr, pi, pr, pi)
    return ir, ii


def _chain_rows(a, t, seg):
    return pl.ds(a * SUBLANES * seg + t, SUBLANES, stride=seg)


def _seg_scan(xr_ref, xi_ref, lam, seg, nchain, reverse, store, init, extra=None):
    nt = len(lam)
    acc0 = () if extra is None else extra[1]

    def step(i, carry):
        hs, acc = carry
        t = seg - 1 - i if reverse else i
        out = []
        for a in range(nchain):
            sl = _chain_rows(a, t, seg)
            for j in range(nt):
                lr, li = lam[j]
                k = 2 * (a * nt + j)
                hr, hi = hs[k], hs[k + 1]
                nr = lr * hr - li * hi + xr_ref[j, sl, :]
                ni = lr * hi + li * hr + xi_ref[j, sl, :]
                if store:
                    xr_ref[j, sl, :] = nr
                    xi_ref[j, sl, :] = ni
                if extra is not None:
                    acc = extra[0](t, a, j, nr, ni, acc)
                out += [nr, ni]
        return tuple(out), acc

    return lax.fori_loop(0, seg, step, (tuple(init), acc0))


def _ssm_scan(xr_ref, xi_ref, lam, seg, nchain, reverse, extra=None):
    nt = len(lam)
    zero = [jnp.zeros((SUBLANES, LANES), F32)] * (2 * nt * nchain)
    ends, _ = _seg_scan(xr_ref, xi_ref, lam, seg, nchain, reverse, False, zero)
    init = [None] * (2 * nt * nchain)
    last = 0 if reverse else SUBLANES - 1
    for j in range(nt):
        mr, mi = _cpow(lam[j][0], lam[j][1], seg)
        m8r, m8i = _cpow(mr, mi, SUBLANES)
        pwr, pwi = _sublane_powers(mr, mi, reverse)
        gr = gi = jnp.zeros((SUBLANES, LANES), F32)
        for a in (reversed(range(nchain)) if reverse else range(nchain)):
            k = 2 * (a * nt + j)
            incr, inci = _inclusive_prefix(ends[k], ends[k + 1], mr, mi, reverse)
            tr, ti = _cmul(pwr, pwi, gr, gi)
            init[k] = _shift_rows(incr, 1, reverse) + tr
            init[k + 1] = _shift_rows(inci, 1, reverse) + ti
            g2r, g2i = _cmul(m8r, m8i, gr, gi)
            gr = g2r + jnp.broadcast_to(incr[last:last + 1, :], gr.shape)
            gi = g2i + jnp.broadcast_to(inci[last:last + 1, :], gi.shape)
    _, acc = _seg_scan(xr_ref, xi_ref, lam, seg, nchain, reverse, True, init, extra)
    return acc


def _diag_mask():
    steps = LANES // SSM_GROUP // GROUPS_PER_TILE
    return (jnp.eye(steps, dtype=F32)[:, None, :, None] * jnp.eye(GROUPS_PER_TILE, dtype=F32)[None, :, None, :])


def _ssm_tables(bb_re, bb_im, c_re, c_im):
    g = bb_re.shape[0]
    nt = g // GROUPS_PER_TILE
    steps = LANES // SSM_GROUP // GROUPS_PER_TILE
    mask = _diag_mask()

    def b_tab(bb):
        x = bb.reshape(nt // steps, steps, GROUPS_PER_TILE, SSM_STATE, SSM_GROUP)
        x = jnp.transpose(x, (0, 1, 4, 2, 3))[:, :, None, None]
        m = jnp.transpose(mask, (0, 2, 3, 1))[None, :, :, :, None, :, None]
        return (x * m).reshape(nt, LANES, TILE_STATES)

    def c_tab(c):
        x = c.reshape(nt // steps, steps, GROUPS_PER_TILE, SSM_GROUP, SSM_STATE)
        x = jnp.transpose(x, (0, 1, 2, 4, 3))[:, :, :, :, None, None]
        m = mask[None, :, :, None, :, :, None]
        return (x * m).reshape(nt, TILE_STATES, LANES)

    return b_tab(bb_re), b_tab(bb_im), c_tab(c_re), c_tab(c_im)


def _ssm_untable_b(db, g):
    nt = g // GROUPS_PER_TILE
    steps = LANES // SSM_GROUP // GROUPS_PER_TILE
    x = db.reshape(nt // steps, steps, GROUPS_PER_TILE, SSM_STATE, steps, GROUPS_PER_TILE, SSM_GROUP)
    m = _diag_mask()[None, :, :, None, :, :, None]
    return jnp.sum(x * m, axis=(4, 5)).reshape(g, SSM_STATE, SSM_GROUP)


def _ssm_untable_c(dc, g):
    nt = g // GROUPS_PER_TILE
    steps = LANES // SSM_GROUP // GROUPS_PER_TILE
    x = dc.reshape(nt // steps, steps, steps, GROUPS_PER_TILE, SSM_GROUP, GROUPS_PER_TILE, SSM_STATE)
    m = jnp.transpose(_diag_mask(), (0, 2, 3, 1))[None, :, :, :, None, :, None]
    out = jnp.sum(x * m, axis=(2, 3))
    return jnp.transpose(out, (0, 1, 3, 2, 4)).reshape(g, SSM_GROUP, SSM_STATE)


def _lam_tiles(lam_ref):
    out = []
    for j in range(TILE_STATES // LANES):
        out.append(jnp.broadcast_to(lam_ref[:, j * LANES:(j + 1) * LANES], (SUBLANES, LANES)))
    return out


def _scan_chains(lp):
    for n in (4, 2, 1):
        if lp % (SUBLANES * n) == 0 and (lp // SUBLANES) % 16 == 0:
            return n
    raise ValueError(lp)


def _split_tiles(dst_ref, rows, val):
    for j in range(val.shape[1] // LANES):
        dst_ref[j, rows, :] = val[:, j * LANES:(j + 1) * LANES]


def _cat_tiles(src_ref, rows):
    njt = src_ref.shape[0]
    return jnp.concatenate([src_ref[j, rows, :] for j in range(njt)], axis=1).astype(BF16)


def _ssm_fwd(u, lam_re, lam_im, tb_re, tb_im, tc_re, tc_im, d_skip, comm=None):
    lp, w = u.shape
    nt = tb_re.shape[0]
    nchain = _scan_chains(lp)
    seg = lp // (SUBLANES * nchain)
    chunk = lp // SUBLANES
    njt = TILE_STATES // LANES

    def body(u_ref, lr_ref, li_ref, br_ref, bi_ref, cr_ref, ci_ref, d_ref, y_ref, yg_ref, xr, xi):
        t = pl.program_id(0)
        for s in range(SUBLANES):
            rs = pl.ds(s * chunk, chunk)
            ub = u_ref[rs, :].astype(BF16)
            _split_tiles(xr, rs, _dot(ub, br_ref[...], "nn"))
            _split_tiles(xi, rs, _dot(ub, bi_ref[...], "nn"))
        lrs, lis = _lam_tiles(lr_ref), _lam_tiles(li_ref)
        _ssm_scan(xr, xi, list(zip(lrs, lis)), seg, nchain, False)
        for s in range(SUBLANES):
            rs = pl.ds(s * chunk, chunk)
            y = _dot(_cat_tiles(xr, rs), cr_ref[...], "nn") - _dot(_cat_tiles(xi, rs), ci_ref[...], "nn")

            @pl.when(t % 2 == 0)
            def _():
                y_ref[rs, :] = y + d_ref[...] * u_ref[rs, :]

            @pl.when(t % 2 == 1)
            def _():
                total = y_ref[rs, :] + y
                y_ref[rs, :] = total
                yg_ref[rs, :] = _gelu(total).astype(BF16)

    blk = pl.BlockSpec((lp, LANES), lambda t: (0, t // 2))
    lam_spec = pl.BlockSpec((None, 1, TILE_STATES), lambda t: (t, 0, 0))
    b_spec = pl.BlockSpec((None, LANES, TILE_STATES), lambda t: (t, 0, 0))
    c_spec = pl.BlockSpec((None, TILE_STATES, LANES), lambda t: (t, 0, 0))
    (y, yg), couts = _hosted_call(
        body, comm, out_shape=[jax.ShapeDtypeStruct((lp, w), F32), jax.ShapeDtypeStruct((lp, w), BF16)], grid=(nt,),
        in_specs=[blk, lam_spec, lam_spec, b_spec, b_spec, c_spec, c_spec,
                  pl.BlockSpec((1, LANES), lambda t: (0, t // 2))],
        out_specs=[blk, blk],
        scratch_shapes=[pltpu.VMEM((njt, lp, LANES), F32), pltpu.VMEM((njt, lp, LANES), F32)],
        sem=("arbitrary",), name="ssm_fwd",
        args=(u, lam_re, lam_im, tb_re, tb_im, tc_re, tc_im, d_skip.reshape(1, w)))
    return (y, yg), couts


def _ssm_bwd(u, dy, lam_re, lam_im, tb_re, tb_im, tc_re, tc_im, d_skip, comm=None):
    lp, w = u.shape
    nt = tb_re.shape[0]
    nchain = _scan_chains(lp)
    seg = lp // (SUBLANES * nchain)
    chunk = lp // SUBLANES
    njt = TILE_STATES // LANES
    tbt_re, tbt_im = jnp.swapaxes(tb_re, 1, 2), jnp.swapaxes(tb_im, 1, 2)
    tct_re, tct_im = jnp.swapaxes(tc_re, 1, 2), jnp.swapaxes(tc_im, 1, 2)

    def body(u_ref, dy_ref, lr_ref, li_ref, br_ref, bi_ref, btr_ref, bti_ref, ctr_ref, cti_ref, d_ref,
             du_ref, dlr_ref, dli_ref, dbr_ref, dbi_ref, dcr_ref, dci_ref, dd_ref, hr, hi, ar, ai):
        t = pl.program_id(0)
        lrs, lis = _lam_tiles(lr_ref), _lam_tiles(li_ref)
        for s in range(SUBLANES):
            rs = pl.ds(s * chunk, chunk)
            ub = u_ref[rs, :].astype(BF16)
            dyb = dy_ref[rs, :].astype(BF16)
            _split_tiles(hr, rs, _dot(ub, br_ref[...], "nn"))
            _split_tiles(hi, rs, _dot(ub, bi_ref[...], "nn"))
            _split_tiles(ar, rs, _dot(dyb, ctr_ref[...], "nn"))
            _split_tiles(ai, rs, -_dot(dyb, cti_ref[...], "nn"))
        _ssm_scan(hr, hi, list(zip(lrs, lis)), seg, nchain, False)

        def dlam_step(tt, a, j, a_r, a_i, acc):
            sl = _chain_rows(a, jnp.maximum(tt - 1, 0), seg)
            p_r, p_i = hr[j, sl, :], hi[j, sl, :]
            acc = list(acc)
            acc[2 * j] = acc[2 * j] + jnp.where(tt > 0, a_r * p_r + a_i * p_i, 0.0)
            acc[2 * j + 1] = acc[2 * j + 1] + jnp.where(tt > 0, a_i * p_r - a_r * p_i, 0.0)
            return tuple(acc)

        zero = tuple([jnp.zeros((SUBLANES, LANES), F32)] * (2 * njt))
        conj = [(lr, -li) for lr, li in zip(lrs, lis)]
        acc = list(_ssm_scan(ar, ai, conj, seg, nchain, True, (dlam_step, zero)))
        row0 = lax.broadcasted_iota(jnp.int32, (SUBLANES, LANES), 0) == 0
        for j in range(njt):
            cs = slice(j * LANES, (j + 1) * LANES)
            for a in range(nchain):
                p_r = _shift_rows(hr[j, _chain_rows(a, seg - 1, seg), :], 1, False)
                p_i = _shift_rows(hi[j, _chain_rows(a, seg - 1, seg), :], 1, False)
                if a > 0:
                    before = pl.ds(a * SUBLANES * seg - 1, 1)
                    p_r = jnp.where(row0, jnp.broadcast_to(hr[j, before, :], p_r.shape), p_r)
                    p_i = jnp.where(row0, jnp.broadcast_to(hi[j, before, :], p_i.shape), p_i)
                a_r, a_i = ar[j, _chain_rows(a, 0, seg), :], ai[j, _chain_rows(a, 0, seg), :]
                acc[2 * j] = acc[2 * j] + a_r * p_r + a_i * p_i
                acc[2 * j + 1] = acc[2 * j + 1] + a_i * p_r - a_r * p_i
            dlr_ref[:, cs] = jnp.sum(acc[2 * j], axis=0, keepdims=True)
            dli_ref[:, cs] = jnp.sum(acc[2 * j + 1], axis=0, keepdims=True)

        dd = jnp.zeros((1, LANES), F32)
        for s in range(SUBLANES):
            rs = pl.ds(s * chunk, chunk)
            ub = u_ref[rs, :].astype(BF16)
            dyv = dy_ref[rs, :]
            dyb = dyv.astype(BF16)
            arb, aib = _cat_tiles(ar, rs), _cat_tiles(ai, rs)
            hrb, hib = _cat_tiles(hr, rs), _cat_tiles(hi, rs)
            du = _dot(arb, btr_ref[...], "nn") + _dot(aib, bti_ref[...], "nn")
            upd = [(dbr_ref, _dot(arb, ub, "tn")), (dbi_ref, _dot(aib, ub, "tn")),
                   (dcr_ref, _dot(dyb, hrb, "tn")), (dci_ref, -_dot(dyb, hib, "tn"))]
            for ref, val in upd:
                if s == 0:
                    ref[...] = val
                else:
                    ref[...] += val
            rows = lax.broadcasted_iota(jnp.int32, (chunk, LANES), 0) + s * chunk
            keep = rows >= PAD_FRONT
            dd = dd + jnp.sum(dyv * u_ref[rs, :], axis=0, keepdims=True)

            @pl.when(t % 2 == 0)
            def _():
                du_ref[rs, :] = jnp.where(keep, du + d_ref[...] * dyv, 0.0)

            @pl.when(t % 2 == 1)
            def _():
                du_ref[rs, :] += jnp.where(keep, du, 0.0)

        @pl.when(t % 2 == 0)
        def _():
            dd_ref[...] = dd

    blk = pl.BlockSpec((lp, LANES), lambda t: (0, t // 2))
    vec = pl.BlockSpec((1, LANES), lambda t: (0, t // 2))
    lam_spec = pl.BlockSpec((None, 1, TILE_STATES), lambda t: (t, 0, 0))
    b_spec = pl.BlockSpec((None, LANES, TILE_STATES), lambda t: (t, 0, 0))
    c_spec = pl.BlockSpec((None, TILE_STATES, LANES), lambda t: (t, 0, 0))
    lam_shape = jax.ShapeDtypeStruct((nt, 1, TILE_STATES), F32)
    bt_shape = jax.ShapeDtypeStruct((nt, TILE_STATES, LANES), F32)
    ct_shape = jax.ShapeDtypeStruct((nt, LANES, TILE_STATES), F32)
    st = pltpu.VMEM((njt, lp, LANES), F32)
    return _hosted_call(
        body, comm,
        out_shape=[jax.ShapeDtypeStruct((lp, w), F32), lam_shape, lam_shape, bt_shape, bt_shape, ct_shape, ct_shape,
                   jax.ShapeDtypeStruct((1, w), F32)],
        grid=(nt,),
        in_specs=[blk, blk, lam_spec, lam_spec, b_spec, b_spec, c_spec, c_spec, b_spec, b_spec, vec],
        out_specs=[blk, lam_spec, lam_spec, c_spec, c_spec, b_spec, b_spec, vec],
        scratch_shapes=[st, st, st, st], sem=("arbitrary",), name="ssm_bwd",
        args=(u, dy, lam_re, lam_im, tb_re, tb_im, tbt_re, tbt_im, tct_re, tct_im, d_skip.reshape(1, w)))


def _ssm_params(a_re, a_im, log_dt, b_re, b_im):
    dt = jnp.exp(log_dt)[:, None]
    mag = jnp.exp(a_re * dt)
    lb_re = mag * jnp.cos(a_im * dt)
    lb_im = mag * jnp.sin(a_im * dt)
    den = a_re * a_re + a_im * a_im
    num_re = lb_re - 1.0
    coef_re = (num_re * a_re + lb_im * a_im) / den
    coef_im = (lb_im * a_re - num_re * a_im) / den
    bb_re = coef_re[..., None] * b_re - coef_im[..., None] * b_im
    bb_im = coef_re[..., None] * b_im + coef_im[..., None] * b_re
    return lb_re, lb_im, bb_re, bb_im


def _merge_fwd(o, yg, ga, gs, w3t, comm=None):
    lp, d = ga.shape
    kw = w3t.shape[2]
    tm = _row_tile(lp)

    def epi(accs, in_refs, out_refs):
        attn, vv, gg = accs
        out_refs[0][...] = (_sigmoid(in_refs[5][...]) * attn
                            + _sigmoid(in_refs[6][...]) * (vv * _sigmoid(gg))).astype(BF16)

    wspec = lambda which: pl.BlockSpec((None, d, kw), lambda i: (which, 0, 0))
    rowspec = pl.BlockSpec((tm, d), lambda i: (i, 0))
    aspec = pl.BlockSpec((tm, kw), lambda i: (i, 0))
    res = _matmul(
        "merge_fwd", (lp // tm,), None, [o, yg, w3t, w3t, w3t, ga, gs],
        [aspec, aspec, wspec(0), wspec(1), wspec(2), rowspec, rowspec],
        [(0, 2, "nt", 0), (1, 3, "nt", 1), (1, 4, "nt", 2)], [(tm, d)] * 3, epi,
        [jax.ShapeDtypeStruct((lp, d), BF16)], [rowspec], ("parallel",), comm)
    return (res[0], []) if comm is None else (res[0][0], res[1])


def _out_proj(merged, w_out, h, comm=None):
    lp, d = h.shape
    tm = _row_tile(lp)

    def epi(accs, in_refs, out_refs):
        out_refs[0][...] = in_refs[2][...] + accs[0]

    rowspec = pl.BlockSpec((tm, d), lambda i: (i, 0))
    res = _matmul(
        "mix_out_proj", (lp // tm,), None, [merged, w_out, h],
        [rowspec, pl.BlockSpec((d, d), lambda i: (0, 0)), rowspec],
        [(0, 1, "nn", 0)], [(tm, d)], epi, [jax.ShapeDtypeStruct((lp, d), F32)], [rowspec], ("parallel",), comm)
    return (res[0], []) if comm is None else (res[0][0], res[1])


def _merge_bwd(dhb, w_out, o, yg, ga, gs, w3t):
    lp, d = ga.shape
    kw = w3t.shape[2]
    tm = _row_tile(lp)

    def epi(accs, in_refs, out_refs):
        dm, attn, vv, gg = accs
        sa = _sigmoid(in_refs[7][...])
        ss = _sigmoid(in_refs[8][...])
        sg = _sigmoid(gg)
        ssm = vv * sg
        dssm = dm * ss
        out_refs[0][...] = (dm * sa).astype(BF16)
        out_refs[1][...] = (dssm * sg).astype(BF16)
        out_refs[2][...] = (dssm * vv * sg * (1.0 - sg)).astype(BF16)
        out_refs[3][...] = (dm * attn * sa * (1.0 - sa)).astype(BF16)
        out_refs[4][...] = (dm * ssm * ss * (1.0 - ss)).astype(BF16)

    wspec = lambda which: pl.BlockSpec((None, d, kw), lambda i: (which, 0, 0))
    rowspec = pl.BlockSpec((tm, d), lambda i: (i, 0))
    aspec = pl.BlockSpec((tm, kw), lambda i: (i, 0))
    shp = jax.ShapeDtypeStruct((lp, d), BF16)
    return _matmul(
        "merge_bwd", (lp // tm,), None, [dhb, w_out, o, yg, w3t, w3t, w3t, ga, gs],
        [rowspec, pl.BlockSpec((d, d), lambda i: (0, 0)), aspec, aspec, wspec(0), wspec(1), wspec(2), rowspec,
         rowspec],
        [(0, 1, "nt", 0), (2, 4, "nt", 1), (3, 5, "nt", 2), (3, 6, "nt", 3)], [(tm, d)] * 4, epi,
        [shp] * 5, [rowspec] * 5, ("parallel",))


def _branch_bwd(dattn, dv, dg, w3t, y):
    lp, d = dattn.shape
    kw = w3t.shape[2]
    tm = _row_tile(lp)

    def epi(accs, in_refs, out_refs):
        out_refs[0][...] = accs[0].astype(BF16)
        out_refs[1][...] = accs[1] * _gelu_grad(in_refs[6][...])

    wspec = lambda which: pl.BlockSpec((None, d, kw), lambda i: (which, 0, 0))
    rowspec = pl.BlockSpec((tm, d), lambda i: (i, 0))
    aspec = pl.BlockSpec((tm, kw), lambda i: (i, 0))
    return _matmul(
        "branch_bwd", (lp // tm,), None, [dattn, dv, dg, w3t, w3t, w3t, y],
        [rowspec, rowspec, rowspec, wspec(0), wspec(1), wspec(2), aspec],
        [(0, 3, "nn", 0), (1, 4, "nn", 1), (2, 5, "nn", 1)], [(tm, kw)] * 2, epi,
        [jax.ShapeDtypeStruct((lp, kw), BF16), jax.ShapeDtypeStruct((lp, kw), F32)], [aspec, aspec],
        ("parallel",))


def _tn_cols(x, ys, name):
    lp, kx = x.shape
    n = ys[0].shape[1]
    n4 = n // N_CHIPS
    tk = _tn_tiles(lp)

    def epi(accs, in_refs, out_refs):
        for acc, o in zip(accs, out_refs):
            o[...] = acc

    shp = jax.ShapeDtypeStruct((N_CHIPS, kx, n4), F32)
    return _matmul(
        name, (N_CHIPS, lp // tk), 1, [x] + list(ys),
        [pl.BlockSpec((tk, kx), lambda j, k: (k, 0))] + [pl.BlockSpec((tk, n4), lambda j, k: (k, j))] * len(ys),
        [(0, 1 + i, "tn", i) for i in range(len(ys))], [(kx, n4)] * len(ys), epi,
        [shp] * len(ys), [pl.BlockSpec((None, kx, n4), lambda j, k: (j, 0, 0))] * len(ys),
        ("parallel", "arbitrary"))


def _tn_full(x, y, name, tn_cols=None):
    lp, kx = x.shape
    n = y.shape[1]
    tk = _tn_tiles(lp)
    tn = n if tn_cols is None else tn_cols

    def epi(accs, in_refs, out_refs):
        out_refs[0][...] = accs[0]

    (out,) = _matmul(
        name, (n // tn, lp // tk), 1, [x, y],
        [pl.BlockSpec((tk, kx), lambda j, k: (k, 0)), pl.BlockSpec((tk, tn), lambda j, k: (k, j))],
        [(0, 1, "tn", 0)], [(kx, tn)], epi,
        [jax.ShapeDtypeStruct((kx, n), F32)], [pl.BlockSpec((kx, tn), lambda j, k: (0, j))],
        ("parallel", "arbitrary"))
    return out


def _in_proj_bwd(dz, w_in, dh, h_in, gain):
    lp, d = h_in.shape
    inw = w_in.shape[1]
    tm = _row_tile(lp)

    def epi(accs, in_refs, out_refs):
        i = pl.program_id(0)
        dx, dgrow = _rms_bwd_math(accs[0], in_refs[3][...], in_refs[4][...])
        dh_new = in_refs[2][...] + dx
        out_refs[0][...] = dh_new
        out_refs[2][...] = dh_new.astype(BF16)

        @pl.when(i == 0)
        def _():
            out_refs[1][...] = jnp.zeros_like(out_refs[1])

        out_refs[1][...] += jnp.sum(dgrow, axis=0, keepdims=True)

    row = pl.BlockSpec((tm, d), lambda i: (i, 0))
    return _matmul(
        "mix_in_proj_bwd", (lp // tm,), None, [dz, w_in, dh, h_in, gain.reshape(1, d)],
        [pl.BlockSpec((tm, inw), lambda i: (i, 0)), pl.BlockSpec((d, inw), lambda i: (0, 0)), row, row,
         pl.BlockSpec((1, d), lambda i: (0, 0))],
        [(0, 1, "nt", 0)], [(tm, d)], epi,
        [jax.ShapeDtypeStruct((lp, d), F32), jax.ShapeDtypeStruct((1, d), F32), jax.ShapeDtypeStruct((lp, d), BF16)],
        [row, pl.BlockSpec((1, d), lambda i: (0, 0)), row], ("arbitrary",))


def _loss_head(h, gain, target):
    lp, d = h.shape
    nb = lp // BLOCK

    def body(h_ref, g_ref, t_ref, dh_ref, dg_ref, loss_ref, dhb_ref):
        i = pl.program_id(0)

        @pl.when(i == 0)
        def _():
            dg_ref[...] = jnp.zeros_like(dg_ref)
            loss_ref[...] = jnp.zeros_like(loss_ref)
            dh_ref[...] = jnp.zeros_like(dh_ref)
            dhb_ref[...] = jnp.zeros_like(dhb_ref)

        @pl.when(i > 0)
        def _():
            x = h_ref[...]
            g = g_ref[...]
            r = lax.rsqrt(jnp.mean(x * x, axis=-1, keepdims=True) + EPS)
            err = x * r * g - t_ref[...]
            loss_ref[...] += jnp.zeros_like(loss_ref) + 0.5 * jnp.sum(jnp.sum(err * err, axis=-1, keepdims=True)) / d
            dx, dgrow = _rms_bwd_math(err * (1.0 / d), x, g)
            dh_ref[...] = dx
            dhb_ref[...] = dx.astype(BF16)
            dg_ref[...] += jnp.sum(dgrow, axis=0, keepdims=True)

    row = pl.BlockSpec((BLOCK, d), lambda i: (i, 0))
    one = pl.BlockSpec((1, d), lambda i: (0, 0))
    return pl.pallas_call(
        body,
        out_shape=[jax.ShapeDtypeStruct((lp, d), F32), jax.ShapeDtypeStruct((1, d), F32),
                   jax.ShapeDtypeStruct((SUBLANES, LANES), F32), jax.ShapeDtypeStruct((lp, d), BF16)],
        grid=(nb,),
        in_specs=[row, one, pl.BlockSpec((BLOCK, d), lambda i: (jnp.maximum(i - 1, 0), 0))],
        out_specs=[row, one, pl.BlockSpec((SUBLANES, LANES), lambda i: (0, 0)), row],
        compiler_params=_cparams(("arbitrary",)), name="loss_head")(h, gain.reshape(1, d), target)


def _adam_math(w, g, m, v):
    m = ADAM_B1 * m + (1.0 - ADAM_B1) * g
    v = ADAM_B2 * v + (1.0 - ADAM_B2) * (g * g)
    m_hat = m / (1.0 - ADAM_B1 ** ADAM_STEP)
    v_hat = v / (1.0 - ADAM_B2 ** ADAM_STEP)
    delta = -ADAM_LR * (m_hat / (jnp.sqrt(v_hat) + ADAM_EPS) + ADAM_WD * w)
    return delta, m, v


def _adamw_layers(w, m, v, mine, other, pos, name):
    depth, r, c = w.shape
    half = r // 2
    tr = _div_tile(half, c * 4)
    nh = half // tr

    def body(*refs):
        pos_ref, w_ref, m_ref, v_ref = refs[:4]
        mine_refs = refs[4:4 + depth]
        other_refs = refs[4 + depth:4 + 2 * depth]
        g_out, d_out, m_out, v_out = refs[4 + 2 * depth:]
        layer, i = pl.program_id(0), pl.program_id(1)
        is_mine = (i // nh) == pos_ref[0]

        def update(g):
            delta, nm, nv = _adam_math(w_ref[...], g, m_ref[...], v_ref[...])
            g_out[...] = g
            d_out[...] = delta
            m_out[...] = nm
            v_out[...] = nv

        for l in range(depth):
            @pl.when((layer == l) & is_mine)
            def _(l=l):
                update(mine_refs[l][...])

            @pl.when((layer == l) & jnp.logical_not(is_mine))
            def _(l=l):
                update(other_refs[l][...])

    stacked = pl.BlockSpec((None, tr, c), lambda l, i, p: (l, i, 0))

    def gspec(layer, is_other):
        def imap(l, i, p):
            first = jnp.where(is_other, 1 - p[0], p[0]) * nh
            here = jnp.clip(i - first, 0, nh - 1)
            return (jnp.where(l == layer, here, jnp.where(l < layer, 0, nh - 1)), 0)
        return pl.BlockSpec((tr, c), imap)

    shp = jax.ShapeDtypeStruct((depth, r, c), F32)
    grid_spec = pltpu.PrefetchScalarGridSpec(
        num_scalar_prefetch=1, grid=(depth, 2 * nh),
        in_specs=[stacked] * 3 + [gspec(l, 0) for l in range(depth)] + [gspec(l, 1) for l in range(depth)],
        out_specs=[stacked] * 4)
    return pl.pallas_call(
        body, out_shape=[shp] * 4, grid_spec=grid_spec,
        compiler_params=_cparams(("arbitrary", "arbitrary")), name=name)(pos, w, m, v, *mine, *other)


def _adamw_whole(w, g, m, v, name):
    def body(w_ref, g_ref, m_ref, v_ref, d_out, m_out, v_out):
        delta, nm, nv = _adam_math(w_ref[...], g_ref[...], m_ref[...], v_ref[...])
        d_out[...] = delta
        m_out[...] = nm
        v_out[...] = nv

    shp = jax.ShapeDtypeStruct(w.shape, F32)
    return pl.pallas_call(body, out_shape=[shp] * 3, compiler_params=_cparams(), name=name)(w, g, m, v)


def _mesh_pos():
    return lax.axis_index("x"), lax.axis_index("y"), lax.axis_index("c")


def _row_half(ref, which, lead):
    half = ref.shape[lead] // 2
    idx = (slice(None),) * lead + (pl.ds(which * half, half), slice(None))
    return ref.at[idx]


def _gather_comm(arrs, tag):
    n = len(arrs)

    def ctx(ins, outs, sems):
        send_sems, recv_sems, local_sems = sems
        x, y, c = _mesh_pos()
        chips = [(1 - x, y), (x, 1 - y), (1 - x, 1 - y)]

        def slot(k, chip, which):
            lead = len(ins[k].shape) - 2
            return _row_half(outs[k].at[2 * chip[0] + chip[1]], which, lead)

        def copy(k, j, src, dst, to):
            return pltpu.make_async_remote_copy(
                src_ref=src, dst_ref=dst, send_sem=send_sems.at[6 * k + j], recv_sem=recv_sems.at[6 * k + j],
                device_id=to, device_id_type=MESH)

        def local(k):
            return pltpu.make_async_copy(ins[k], outs[k].at[2 * x + y], local_sems.at[k])

        def first(k, j):
            lead = len(ins[k].shape) - 2
            return copy(k, j, _row_half(ins[k], c, lead), slot(k, (x, y), c), (*chips[j], c))

        def passed(k, j, which):
            return copy(k, 3 + j, slot(k, chips[j], which), slot(k, chips[j], which), (x, y, 1 - c))

        def landed(k, j):
            return copy(k, j, slot(k, chips[j], c), slot(k, chips[j], c), (x, y, 1 - c))

        return c, local, first, passed, landed

    def start(ins, outs, sems):
        c, local, first, passed, landed = ctx(ins, outs, sems)
        for k in range(n):
            local(k).start()
            for j in range(3):
                first(k, j).start()

    def mid(ins, outs, sems):
        c, local, first, passed, landed = ctx(ins, outs, sems)
        for j in range(3):
            for k in range(n):
                landed(k, j).wait_recv()
                passed(k, j, c).start()

    def finish(ins, outs, sems):
        c, local, first, passed, landed = ctx(ins, outs, sems)
        for j in range(3):
            for k in range(n):
                passed(k, j, 1 - c).wait_recv()
        for k in range(n):
            for j in range(3):
                first(k, j).wait_send()
                passed(k, j, c).wait_send()
            local(k).wait()

    return _Comm(
        tag, arrs, [jax.ShapeDtypeStruct((N_CHIPS,) + a.shape, a.dtype) for a in arrs],
        [pltpu.SemaphoreType.DMA((6 * n,)), pltpu.SemaphoreType.DMA((6 * n,)), pltpu.SemaphoreType.DMA((n,))],
        start, mid, finish)


def _run_comm(comm, name):
    n_in, n_out = len(comm.ins), len(comm.out_shapes)

    def body(*refs):
        ins, outs, sems = refs[:n_in], refs[n_in:n_in + n_out], refs[n_in + n_out:]
        comm.start(ins, outs, sems)
        if comm.mid is not None:
            comm.mid(ins, outs, sems)
        comm.finish(ins, outs, sems)

    return pl.pallas_call(
        body, out_shape=comm.out_shapes, in_specs=[HBM_SPEC] * n_in, out_specs=[HBM_SPEC] * n_out,
        scratch_shapes=comm.sems, name=name)(*comm.ins)


def _all_gather_chips(arrs, name):
    return _run_comm(_gather_comm(arrs, "gather"), name)


GATHER_US_PER_BYTE = 380.0 / 11.65e6
HOST_US = dict(ffn_up=78.0, ffn_down=65.0, in_proj=38.0, attn_fwd=103.0, ssm_fwd=67.0, merge_fwd=50.0,
               out_proj=45.0)
HOST_SLACK_US = 10.0


class _WeightStream:
    def __init__(self, pieces):
        self.keys = [k for k, _ in pieces]
        self.shards = dict(pieces)
        self.next = 0
        self.full = {}
        self.pending = []

    def comm_for(self, host):
        budget = HOST_US[host] + HOST_SLACK_US
        taken, cost = [], 0.0
        while self.next < len(self.keys):
            key = self.keys[self.next]
            c = self.shards[key].size * self.shards[key].dtype.itemsize * GATHER_US_PER_BYTE
            if cost + c > budget:
                break
            taken.append(key)
            cost += c
            self.next += 1
        self.pending = taken
        if not taken:
            return None
        return _gather_comm([self.shards[k] for k in taken], "g_" + "_".join(k[1] for k in taken))

    def deposit(self, gathered):
        for key, arr in zip(self.pending, gathered):
            self.full[key] = arr
        self.pending = []

    def get(self, key):
        if key not in self.full:
            upto = self.keys.index(key) + 1
            keys = self.keys[self.next:upto]
            self.next = upto
            for k, arr in zip(keys, _all_gather_chips([self.shards[k] for k in keys], "gather_now")):
                self.full[k] = arr
        return self.full[key]


def _all_gather_devices(x_shard, name):
    m_per, ncol = x_shard.shape

    def body(x_ref, out_ref, send_sems, recv_sems, local_sem):
        x, y, c = _mesh_pos()
        me, sibling = (x, y, c), (x, y, 1 - c)
        chips = [(1 - x, y), (x, 1 - y), (1 - x, 1 - y)]

        def rows(px, py, pc):
            return out_ref.at[4 * px + 2 * py + pc]

        def copy(k, block, to, src=None):
            return pltpu.make_async_remote_copy(
                src_ref=rows(*block) if src is None else src, dst_ref=rows(*block),
                send_sem=send_sems.at[k], recv_sem=recv_sems.at[k], device_id=to, device_id_type=MESH)

        mine = pltpu.make_async_copy(x_ref, rows(*me), local_sem)
        mine.start()
        first = [copy(0, me, sibling, src=x_ref)]
        first += [copy(1 + j, me, (*chip, c), src=x_ref) for j, chip in enumerate(chips)]
        for cp in first:
            cp.start()
        passed = [copy(4 + j, (*chip, c), sibling) for j, chip in enumerate(chips)]
        for j, chip in enumerate(chips):
            copy(1 + j, (*chip, c), me).wait_recv()
            passed[j].start()
        copy(0, sibling, me).wait_recv()
        for j, chip in enumerate(chips):
            copy(4 + j, (*chip, 1 - c), me).wait_recv()
        for cp in first + passed:
            cp.wait_send()
        mine.wait()

    return pl.pallas_call(
        body, out_shape=jax.ShapeDtypeStruct((8, m_per, ncol), x_shard.dtype),
        in_specs=[pl.BlockSpec(memory_space=pltpu.VMEM)], out_specs=pl.BlockSpec(memory_space=pltpu.VMEM),
        scratch_shapes=[pltpu.SemaphoreType.DMA((7,)), pltpu.SemaphoreType.DMA((7,)), pltpu.SemaphoreType.DMA],
        compiler_params=pltpu.CompilerParams(vmem_limit_bytes=VMEM_LIMIT), name=name)(x_shard)


def _sum_devices(g8, name):
    _, r, c = g8.shape
    tr = _div_tile(r, c * 4 * 8)

    def body(g_ref, o_ref):
        acc = g_ref[0]
        for dev in range(1, 8):
            acc = acc + g_ref[dev]
        o_ref[...] = acc

    return pl.pallas_call(
        body, out_shape=jax.ShapeDtypeStruct((r, c), F32), grid=(r // tr,),
        in_specs=[pl.BlockSpec((8, tr, c), lambda i: (0, i, 0))], out_specs=pl.BlockSpec((tr, c), lambda i: (i, 0)),
        compiler_params=_cparams(("parallel",)), name=name)(g8)


def _other_halves_bf16(arrs, pos, name):
    n = len(arrs)

    def body(pos_ref, *refs):
        for a_ref, o_ref in zip(refs[:n], refs[n:]):
            o_ref[...] = a_ref[...].astype(BF16)

    in_specs, out_specs, shapes = [], [], []
    for arr in arrs:
        nslab, r, c = arr.shape
        in_specs.append(pl.BlockSpec((None, r // 2, c), lambda j, p: (j, 1 - p[0], 0)))
        out_specs.append(pl.BlockSpec((None, r // 2, c), lambda j, p: (j, 0, 0)))
        shapes.append(jax.ShapeDtypeStruct((nslab, r // 2, c), BF16))
    grid_spec = pltpu.PrefetchScalarGridSpec(
        num_scalar_prefetch=1, grid=(N_CHIPS,), in_specs=in_specs, out_specs=out_specs)
    return pl.pallas_call(
        body, out_shape=shapes, grid_spec=grid_spec, compiler_params=_cparams(("parallel",)), name=name)(pos, *arrs)


def _chip_partials(arrs, recvs, pos, name):
    n = len(arrs)

    def body(pos_ref, *refs):
        for a_ref, b_ref, o_ref in zip(refs[:n], refs[n:2 * n], refs[2 * n:]):
            o_ref[...] = (a_ref[...] + b_ref[...]).astype(BF16)

    own_specs, recv_specs, shapes = [], [], []
    for arr in arrs:
        nslab, r, c = arr.shape
        own_specs.append(pl.BlockSpec((None, r // 2, c), lambda j, p: (j, p[0], 0)))
        recv_specs.append(pl.BlockSpec((None, r // 2, c), lambda j, p: (j, 0, 0)))
        shapes.append(jax.ShapeDtypeStruct((nslab, r // 2, c), BF16))
    grid_spec = pltpu.PrefetchScalarGridSpec(
        num_scalar_prefetch=1, grid=(N_CHIPS,), in_specs=own_specs + recv_specs, out_specs=recv_specs)
    return pl.pallas_call(
        body, out_shape=shapes, grid_spec=grid_spec,
        compiler_params=_cparams(("parallel",)), name=name)(pos, *arrs, *recvs)


def _chip_exchange_comm(parts, tag):
    n = len(parts)

    def copies(ins, outs, sems):
        send_sems, recv_sems = sems
        x, y, c = _mesh_pos()
        chips = [(1 - x, y), (x, 1 - y), (1 - x, 1 - y)]
        return [pltpu.make_async_remote_copy(
            src_ref=ins[k].at[2 * chip[0] + chip[1]], dst_ref=outs[k].at[j],
            send_sem=send_sems.at[3 * k + j], recv_sem=recv_sems.at[3 * k + j],
            device_id=(*chip, c), device_id_type=MESH) for k in range(n) for j, chip in enumerate(chips)]

    def start(ins, outs, sems):
        for cp in copies(ins, outs, sems):
            cp.start()

    def finish(ins, outs, sems):
        for cp in copies(ins, outs, sems):
            cp.wait()

    return _Comm(
        tag, parts, [jax.ShapeDtypeStruct((3,) + p.shape[1:], p.dtype) for p in parts],
        [pltpu.SemaphoreType.DMA((3 * n,)), pltpu.SemaphoreType.DMA((3 * n,))], start, None, finish)


def _reduce_halves(arrs, recvs, gots, pos, name):
    n = len(arrs)

    def body(pos_ref, *refs):
        for a_ref, b_ref, g_ref, o_ref in zip(refs[:n], refs[n:2 * n], refs[2 * n:3 * n], refs[3 * n:]):
            acc = a_ref[...] + b_ref[...]
            for j in range(3):
                acc = acc + g_ref[j].astype(F32)
            o_ref[...] = acc

    own_specs, recv_specs, got_specs, out_specs, shapes = [], [], [], [], []
    for arr in arrs:
        _, r, c = arr.shape
        own_specs.append(pl.BlockSpec((None, r // 2, c), lambda i, p: (p[1], p[0], 0)))
        recv_specs.append(pl.BlockSpec((None, r // 2, c), lambda i, p: (p[1], 0, 0)))
        got_specs.append(pl.BlockSpec((3, r // 2, c), lambda i, p: (0, 0, 0)))
        out_specs.append(pl.BlockSpec((r // 2, c), lambda i, p: (0, 0)))
        shapes.append(jax.ShapeDtypeStruct((r // 2, c), F32))
    grid_spec = pltpu.PrefetchScalarGridSpec(
        num_scalar_prefetch=1, grid=(1,), in_specs=own_specs + recv_specs + got_specs, out_specs=out_specs)
    return pl.pallas_call(
        body, out_shape=shapes, grid_spec=grid_spec,
        compiler_params=_cparams(("arbitrary",)), name=name)(pos, *arrs, *recvs, *gots)


def _share_halves(halves, name):
    n = len(halves)

    def body(*refs):
        ins, outs = refs[:n], refs[n:2 * n]
        send_sems, recv_sems = refs[2 * n:]
        x, y, c = _mesh_pos()
        cps = []
        for k in range(n):
            cp = pltpu.make_async_remote_copy(
                src_ref=ins[k], dst_ref=outs[k], send_sem=send_sems.at[k], recv_sem=recv_sems.at[k],
                device_id=(x, y, 1 - c), device_id_type=MESH)
            cp.start()
            cps.append(cp)
        for cp in cps:
            cp.wait()

    return pl.pallas_call(
        body, out_shape=[jax.ShapeDtypeStruct(h.shape, h.dtype) for h in halves],
        in_specs=[HBM_SPEC] * n, out_specs=[HBM_SPEC] * n,
        scratch_shapes=[pltpu.SemaphoreType.DMA((n,)), pltpu.SemaphoreType.DMA((n,))], name=name)(*halves)


class _Reduction:
    def __init__(self, arrs, pos, tag):
        self.arrs, self.pos, self.tag = arrs, pos, tag
        self.recv = _share_halves(_other_halves_bf16(arrs, pos, "rs_other_" + tag), "rs_sibling_" + tag)
        self.parts = _chip_partials(arrs, self.recv, pos, "rs_partial_" + tag)
        self.got = None

    def comm(self):
        return _chip_exchange_comm(self.parts, "rs_" + self.tag)

    def end(self):
        if self.got is None:
            self.got = _run_comm(self.comm(), "rs_chips_" + self.tag)
        halves = _reduce_halves(self.arrs, self.recv, self.got, self.pos, "rs_reduce_" + self.tag)
        return halves, _share_halves(halves, "rs_share_" + self.tag)


def _w_in_full(p, l, ws):
    if "w_in" not in p:
        slabs = ws.get((l, "w_in"))
        p["w_in"] = jnp.concatenate([slabs[j] for j in range(N_CHIPS)], axis=1)
    return p["w_in"]


def _w3t_full(p, l, ws):
    if "w3t" not in p:
        slabs = ws.get((l, "w3"))
        p["w3t"] = jnp.swapaxes(slabs, 0, 1).reshape(slabs.shape[1], -1, slabs.shape[3])
    return p["w3t"]


def _w_out_full(l, ws):
    slabs = ws.get((l, "w_out"))
    return slabs.reshape(-1, slabs.shape[2])


def _layer_fwd(h, l, p, ws, tabs):
    def hosted(host, fn, *args):
        out, got = fn(*args, ws.comm_for(host))
        ws.deposit(got)
        return out

    ffn1_saved = hosted("ffn_up", _ffn_up, h, p["ffn1_norm"], ws.get((l, "wg1")), ws.get((l, "wu1")))
    h1 = hosted("ffn_down", _ffn_down, ffn1_saved[2], ws.get((l, "wd1")), h)
    n = _rms_fwd(h1, p["mix_norm"], "rms_fwd_mix")
    ssm_w = p["ssm_d"].shape[0]
    q, k, v, u, ga, gs = hosted("in_proj", _in_proj, n, _w_in_full(p, l, ws), tabs, ssm_w)
    o = hosted("attn_fwd", _attn_fwd, q, k, v, p["attn_sinks"])
    y, yg = hosted("ssm_fwd", _ssm_fwd, u, *p["ssm_tabs"], p["ssm_d"])
    merged = hosted("merge_fwd", _merge_fwd, o, yg, ga, gs, _w3t_full(p, l, ws))
    h2 = hosted("out_proj", _out_proj, merged, _w_out_full(l, ws), h1)
    ffn2_saved = hosted("ffn_up", _ffn_up, h2, p["ffn2_norm"], ws.get((l, "wg2")), ws.get((l, "wu2")))
    h3 = hosted("ffn_down", _ffn_down, ffn2_saved[2], ws.get((l, "wd2")), h2)
    saved = dict(h0=h, h1=h1, h2=h2, ffn1=ffn1_saved, ffn2=ffn2_saved, n_mix=n, q=q, k=k, v=v, u=u, ga=ga, gs=gs,
                 o=o, y=y, yg=yg, merged=merged)
    return h3, saved


def _layer_bwd(dh_pair, l, p, ws, s, tabs, pos):
    g = {}
    (dh2, dhb), g["ffn2_norm"], red_ffn2, _ = _ffn_bwd(
        dh_pair, s["h2"], p["ffn2_norm"], ws.get((l, "wg2")), ws.get((l, "wu2")), ws.get((l, "wd2")), p["f4"],
        s["ffn2"], pos)
    w3, w_out_w = _w3t_full(p, l, ws), _w_out_full(l, ws)
    lp, d = dh2.shape
    d4 = d // N_CHIPS
    dw_out = _tn_full(s["merged"], dhb, "mix_dw_out").reshape(N_CHIPS, d4, d)
    dattn, dv, dg, dga, dgs = _merge_bwd(dhb, w_out_w, s["o"], s["yg"], s["ga"], s["gs"], w3)
    (dw_ap,) = _tn_cols(s["o"], [dattn], "mix_dw_ap")
    dw_gv, dw_gg = _tn_cols(s["yg"], [dv, dg], "mix_dw_glu")
    do, dy = _branch_bwd(dattn, dv, dg, w3, s["y"])
    (dq, dk, dvv, dkm, dvm, dsink), _ = _attn_bwd(s["q"], s["k"], s["v"], do, p["attn_sinks"], tabs)
    g["attn_sinks"] = dsink[:, 0]
    (du, dlr, dli, dbr, dbi, dcr, dci, dd), _ = _ssm_bwd(s["u"], dy, *p["ssm_tabs"], p["ssm_d"])
    ngrp = p["ssm_d"].shape[0] // SSM_GROUP
    g["ssm_lam"] = (dlr.reshape(ngrp, SSM_STATE), dli.reshape(ngrp, SSM_STATE),
                    _ssm_untable_b(dbr, ngrp), _ssm_untable_b(dbi, ngrp))
    g["ssm_c_re"] = _ssm_untable_c(dcr, ngrp)
    g["ssm_c_im"] = _ssm_untable_c(dci, ngrp)
    g["ssm_d"] = dd[0]
    dk = dk.at[:BLOCK].add(dkm)
    dvv = dvv.at[:BLOCK].add(dvm)
    dz = jnp.concatenate([dq.astype(BF16), dk.astype(BF16), dvv.astype(BF16), du.astype(BF16), dga, dgs], axis=1)
    n = s["n_mix"]
    w_in = _w_in_full(p, l, ws)
    inw = w_in.shape[1]
    dw_in = _tn_full(dz, n, "mix_dw_in", d // 2).reshape(N_CHIPS, inw // N_CHIPS, d)
    red_mix = _Reduction([dw_in, dw_ap, dw_gv, dw_gg, dw_out], pos, "mix")
    dh1, g["mix_norm"], dh1b = _in_proj_bwd(dz, w_in, dh2, s["h1"], p["mix_norm"])
    dh0_pair, g["ffn1_norm"], red_ffn1, red_mix.got = _ffn_bwd(
        (dh1, dh1b), s["h0"], p["ffn1_norm"], ws.get((l, "wg1")), ws.get((l, "wu1")), ws.get((l, "wd1")), p["f4"],
        s["ffn1"], pos, red_mix.comm())
    return dh0_pair, g, [*red_ffn1, red_mix, *red_ffn2]


BIG = ["ffn1_w_gate", "ffn1_w_up", "ffn1_w_down", "w_in", "w_attn_proj", "w_glu_v", "w_glu_g", "w_out",
       "ffn2_w_gate", "ffn2_w_up", "ffn2_w_down"]
TRANSPOSED = ["ffn1_w_gate", "ffn1_w_up", "w_in", "ffn2_w_gate", "ffn2_w_up"]
SMALL = ["ffn1_norm", "mix_norm", "attn_sinks", "ssm_a_re", "ssm_a_im", "ssm_log_dt", "ssm_b_re", "ssm_b_im",
         "ssm_c_re", "ssm_c_im", "ssm_d", "ffn2_norm", "final_norm"]
WEIGHTS = ["meta_tokens", "ffn1_norm", "ffn1_w_gate", "ffn1_w_up", "ffn1_w_down", "mix_norm", "w_in", "attn_sinks",
           "ssm_a_re", "ssm_a_im", "ssm_log_dt", "ssm_b_re", "ssm_b_im", "ssm_c_re", "ssm_c_im", "ssm_d",
           "w_attn_proj", "w_glu_v", "w_glu_g", "w_out", "ffn2_norm", "ffn2_w_gate", "ffn2_w_up", "ffn2_w_down",
           "final_norm"]


def _small_rows(shape):
    rows = -(-math.prod(shape) // LANES)
    return -(-rows // SUBLANES) * SUBLANES


def _pack_small(tree):
    parts = []
    for k in SMALL + ["meta_tokens"]:
        size, rows = math.prod(tree[k].shape), _small_rows(tree[k].shape)
        if size % LANES == 0:
            part = tree[k].reshape(size // LANES, LANES)
        else:
            part = jnp.pad(tree[k].reshape(1, size), ((0, 0), (0, LANES - size)))
        parts.append(jnp.pad(part, ((0, rows - part.shape[0]), (0, 0))))
    return jnp.concatenate(parts, axis=0)


def _unpack_small(packed, like):
    out, off = {}, 0
    for k in SMALL + ["meta_tokens"]:
        size, rows = math.prod(like[k].shape), _small_rows(like[k].shape)
        if size % LANES == 0:
            out[k] = packed[off:off + size // LANES].reshape(like[k].shape)
        else:
            out[k] = packed[off, :size].reshape(like[k].shape)
        off += rows
    return out


def kernel(x, meta_tokens, ffn1_norm, ffn1_w_gate, ffn1_w_up, ffn1_w_down, mix_norm, w_in, attn_sinks, ssm_a_re, ssm_a_im, ssm_log_dt, ssm_b_re, ssm_b_im, ssm_c_re, ssm_c_im, ssm_d, w_attn_proj, w_glu_v, w_glu_g, w_out, ffn2_norm, ffn2_w_gate, ffn2_w_up, ffn2_w_down, final_norm, loss_target, m_meta_tokens, m_ffn1_norm, m_ffn1_w_gate, m_ffn1_w_up, m_ffn1_w_down, m_mix_norm, m_w_in, m_attn_sinks, m_ssm_a_re, m_ssm_a_im, m_ssm_log_dt, m_ssm_b_re, m_ssm_b_im, m_ssm_c_re, m_ssm_c_im, m_ssm_d, m_w_attn_proj, m_w_glu_v, m_w_glu_g, m_w_out, m_ffn2_norm, m_ffn2_w_gate, m_ffn2_w_up, m_ffn2_w_down, m_final_norm, v_meta_tokens, v_ffn1_norm, v_ffn1_w_gate, v_ffn1_w_up, v_ffn1_w_down, v_mix_norm, v_w_in, v_attn_sinks, v_ssm_a_re, v_ssm_a_im, v_ssm_log_dt, v_ssm_b_re, v_ssm_b_im, v_ssm_c_re, v_ssm_c_im, v_ssm_d, v_w_attn_proj, v_w_glu_v, v_w_glu_g, v_w_out, v_ffn2_norm, v_ffn2_w_gate, v_ffn2_w_up, v_ffn2_w_down, v_final_norm):
    args = dict(locals())
    w = {k: args[k] for k in WEIGHTS}
    m = {k: args["m_" + k] for k in WEIGHTS}
    v = {k: args["v_" + k] for k in WEIGHTS}
    depth = ffn1_norm.shape[0]
    seq, d = x.shape[1], x.shape[2]
    lp = seq + BLOCK
    xi, yi, ci = _mesh_pos()
    pos = jnp.stack([ci, 2 * xi + yi]).astype(jnp.int32)

    tabs = _rope_tables(lp)
    (meta_all,) = _all_gather_chips([meta_tokens], "gather_meta")
    meta_full = jnp.concatenate([meta_all[j] for j in range(N_CHIPS)], axis=1)
    layers, pieces = [], []
    f4 = ffn1_w_gate.shape[2]
    fp = -(-f4 // MXU_DIM) * MXU_DIM

    def ffn_rows(wt):
        return jnp.pad(wt, ((0, fp - f4), (0, 0))).astype(BF16)

    for l in range(depth):
        pieces += [
            ((l, "wg1"), ffn_rows(ffn1_w_gate[l].T)), ((l, "wu1"), ffn_rows(ffn1_w_up[l].T)),
            ((l, "wd1"), ffn_rows(ffn1_w_down[l])), ((l, "w_in"), w_in[l].astype(BF16)),
            ((l, "w3"), jnp.stack([w_attn_proj[l].T, w_glu_v[l].T, w_glu_g[l].T]).astype(BF16)),
            ((l, "w_out"), w_out[l].astype(BF16)),
            ((l, "wg2"), ffn_rows(ffn2_w_gate[l].T)), ((l, "wu2"), ffn_rows(ffn2_w_up[l].T)),
            ((l, "wd2"), ffn_rows(ffn2_w_down[l]))]
        lb_re, lb_im, bb_re, bb_im = _ssm_params(ssm_a_re[l], ssm_a_im[l], ssm_log_dt[l], ssm_b_re[l], ssm_b_im[l])
        ngrp = lb_re.shape[0]
        nt = ngrp // GROUPS_PER_TILE
        ssm_tabs = (lb_re.reshape(nt, 1, TILE_STATES), lb_im.reshape(nt, 1, TILE_STATES),
                    *_ssm_tables(bb_re, bb_im, ssm_c_re[l], ssm_c_im[l]))
        layers.append(dict(
            ffn1_norm=ffn1_norm[l], mix_norm=mix_norm[l], ffn2_norm=ffn2_norm[l], attn_sinks=attn_sinks[l],
            ssm_d=ssm_d[l], ssm_tabs=ssm_tabs, f4=f4))
    ws = _WeightStream(pieces)
    ws.get((0, "wu1"))

    h = jnp.concatenate([jnp.zeros((PAD_FRONT, d), F32), meta_full, x[0]], axis=0)
    saved = []
    for l in range(depth):
        h, s = _layer_fwd(h, l, layers[l], ws, tabs)
        saved.append(s)
    dh, g_final, loss_acc, dhb = _loss_head(h, final_norm, loss_target[0])
    dh_pair = (dh, dhb)
    loss = lax.psum(loss_acc[0, 0], ("x", "y", "c"))

    grads, reds = [None] * depth, [None] * depth
    for l in reversed(range(depth)):
        dh_pair, grads[l], reds[l] = _layer_bwd(dh_pair, l, layers[l], ws, saved[l], tabs, pos)
    dh = dh_pair[0]
    grad_x = dh[BLOCK:][None]
    dmeta_local = dh[PAD_FRONT:BLOCK]

    small = {k: [] for k in SMALL}
    for l in range(depth):
        gl = grads[l]
        _, vjp = jax.vjp(_ssm_params, ssm_a_re[l], ssm_a_im[l], ssm_log_dt[l], ssm_b_re[l], ssm_b_im[l])
        da_re, da_im, dlog_dt, db_re, db_im = vjp(gl["ssm_lam"])
        for k, val in (("ffn1_norm", gl["ffn1_norm"][0]), ("mix_norm", gl["mix_norm"][0]),
                       ("attn_sinks", gl["attn_sinks"]), ("ssm_a_re", da_re), ("ssm_a_im", da_im),
                       ("ssm_log_dt", dlog_dt), ("ssm_b_re", db_re), ("ssm_b_im", db_im),
                       ("ssm_c_re", gl["ssm_c_re"]), ("ssm_c_im", gl["ssm_c_im"]), ("ssm_d", gl["ssm_d"]),
                       ("ffn2_norm", gl["ffn2_norm"][0])):
            small[k].append(val)
    small_local = {k: jnp.stack(vals) for k, vals in small.items() if k != "final_norm"}
    small_local["final_norm"] = g_final[0]
    small_local["meta_tokens"] = dmeta_local
    like = dict(small_local)
    g_small = _sum_devices(_all_gather_devices(_pack_small(small_local), "gather_small_grads"), "sum_small_grads")
    g_small_tree = _unpack_small(g_small, like)
    d4 = d // N_CHIPS
    chip = 2 * xi + yi
    g_meta = lax.dynamic_slice_in_dim(g_small_tree["meta_tokens"], chip * d4, d4, axis=1)

    reduced = []
    for l in range(depth):
        mine, other = [], []
        for red in reds[l]:
            halves, sibling_halves = red.end()
            mine += halves
            other += sibling_halves
        reduced.append((mine, other))

    g_out, delta, new_m, new_v = {}, {}, {}, {}
    for i, k in enumerate(BIG):
        flip = (lambda t: jnp.swapaxes(t, 1, 2)) if k in TRANSPOSED else (lambda t: t)
        outs = _adamw_layers(
            flip(w[k]), flip(m[k]), flip(v[k]), [reduced[l][0][i] for l in range(depth)],
            [reduced[l][1][i] for l in range(depth)], pos, "adamw_" + k)
        g_out[k], delta[k], new_m[k], new_v[k] = [flip(t) for t in outs]
    g_small_tree["meta_tokens"] = g_meta
    for k in SMALL + ["meta_tokens"]:
        shape = w[k].shape if w[k].ndim > 1 else (1,) + w[k].shape
        outs = _adamw_whole(w[k].reshape(shape), g_small_tree[k].reshape(shape), m[k].reshape(shape),
                            v[k].reshape(shape), "adamw_" + k)
        g_out[k] = g_small_tree[k]
        delta[k], new_m[k], new_v[k] = [t.reshape(w[k].shape) for t in outs]

    return (loss, grad_x, *[g_out[k] for k in WEIGHTS], *[delta[k] for k in WEIGHTS],
            *[new_m[k] for k in WEIGHTS], *[new_v[k] for k in WEIGHTS])
```

```python
import functools
import math

import jax
import jax.numpy as jnp
from jax import lax
from jax.experimental import pallas as pl
from jax.experimental.pallas import tpu as pltpu

F32 = jnp.float32
BF16 = jnp.bfloat16

N_META = 16
HEAD_DIM = 64
N_Q_HEADS = 8
N_KV_HEADS = 2
Q_PER_KV = N_Q_HEADS // N_KV_HEADS
ATTN_WIDTH = N_Q_HEADS * HEAD_DIM
KV_WIDTH = N_KV_HEADS * HEAD_DIM
BLOCK = 128
PAD_FRONT = BLOCK - N_META
ROPE_THETA = 500000.0
ROT_DIM = HEAD_DIM // 4
SSM_GROUP = 16
SSM_STATE = 64
GROUPS_PER_TILE = 4
TILE_STATES = GROUPS_PER_TILE * SSM_STATE
LANES = 128
SUBLANES = 8
MXU_DIM = 256
EPS = 1e-6
NEG_INF = -1e30
N_CHIPS = 4

ADAM_LR = 0.001
ADAM_B1 = 0.9
ADAM_B2 = 0.999
ADAM_EPS = 1e-08
ADAM_WD = 0.01
ADAM_STEP = 10

VMEM_LIMIT = 56 * 1024 * 1024
MESH = pl.DeviceIdType.MESH


def _cparams(sem=None):
    return pltpu.CompilerParams(dimension_semantics=sem, vmem_limit_bytes=VMEM_LIMIT)


def _row_tile(rows, limit=512):
    best = None
    for t in range(128, limit + 1, 128):
        if rows % t == 0:
            best = t
    assert best is not None, rows
    return best


def _div_tile(rows, row_bytes, max_bytes=1 << 20, mult=8):
    best = None
    for t in range(mult, rows + 1, mult):
        if rows % t == 0 and t * row_bytes <= max_bytes:
            best = t
    if best is None:
        best = rows
    return best


def _dot(a, b, mode):
    if mode == "nn":
        dims = (((1,), (0,)), ((), ()))
    elif mode == "nt":
        dims = (((1,), (1,)), ((), ()))
    else:
        dims = (((0,), (0,)), ((), ()))
    return lax.dot_general(a.astype(BF16), b.astype(BF16), dims, preferred_element_type=F32)


def _sigmoid(x):
    return 1.0 / (1.0 + jnp.exp(-x))


_GELU_C = math.sqrt(2.0 / math.pi)


def _gelu(x):
    return 0.5 * x * (1.0 + jnp.tanh(_GELU_C * (x + 0.044715 * x * x * x)))


def _gelu_grad(x):
    t = jnp.tanh(_GELU_C * (x + 0.044715 * x * x * x))
    return 0.5 * (1.0 + t) + 0.5 * x * (1.0 - t * t) * _GELU_C * (1.0 + 3.0 * 0.044715 * x * x)


class _Comm:
    def __init__(self, tag, ins, out_shapes, sems, start, mid, finish):
        self.tag, self.ins, self.out_shapes, self.sems = tag, list(ins), list(out_shapes), list(sems)
        self.start, self.mid, self.finish = start, mid, finish


HBM_SPEC = pl.BlockSpec(memory_space=pltpu.HBM)


def _hosted_call(body, comm, *, out_shape, grid, in_specs, out_specs, scratch_shapes, sem, name, args):
    out_shape, in_specs, out_specs = list(out_shape), list(in_specs), list(out_specs)
    scratch_shapes = list(scratch_shapes)
    if comm is None:
        res = pl.pallas_call(
            body, out_shape=out_shape, grid=grid, in_specs=in_specs, out_specs=out_specs,
            scratch_shapes=scratch_shapes, compiler_params=_cparams(sem), name=name)(*args)
        return list(res), []
    n_in, n_out, n_sc = len(args), len(out_shape), len(scratch_shapes)
    nci, nco = len(comm.ins), len(comm.out_shapes)
    total = math.prod(grid)

    def wrapped(*refs):
        in_refs, cin = refs[:n_in], refs[n_in:n_in + nci]
        o0 = n_in + nci
        out_refs, cout = refs[o0:o0 + n_out], refs[o0 + n_out:o0 + n_out + nco]
        s0 = o0 + n_out + nco
        sc, csem = refs[s0:s0 + n_sc], refs[s0 + n_sc:]
        lin = 0
        for dim, size in enumerate(grid):
            lin = lin * size + pl.program_id(dim)

        @pl.when(lin == 0)
        def _():
            comm.start(cin, cout, csem)

        if comm.mid is not None:
            @pl.when(lin == total // 2)
            def _():
                comm.mid(cin, cout, csem)

        body(*in_refs, *out_refs, *sc)

        @pl.when(lin == total - 1)
        def _():
            comm.finish(cin, cout, csem)

    res = pl.pallas_call(
        wrapped, out_shape=out_shape + comm.out_shapes, grid=grid,
        in_specs=in_specs + [HBM_SPEC] * nci, out_specs=out_specs + [HBM_SPEC] * nco,
        scratch_shapes=scratch_shapes + comm.sems,
        compiler_params=_cparams(("arbitrary",) * len(grid)), name=name + "_" + comm.tag)(*args, *comm.ins)
    return list(res[:n_out]), list(res[n_out:])


def _matmul(name, grid, k_axis, ins, in_specs, pairs, acc_shapes, epilogue, out_shapes, out_specs, sem, comm=None):
    n_in, n_out, n_acc = len(ins), len(out_shapes), len(acc_shapes)

    def body(*refs):
        in_refs = refs[:n_in]
        out_refs = refs[n_in:n_in + n_out]
        acc_refs = refs[n_in + n_out:]
        if k_axis is None:
            accs = [None] * n_acc
            for ia, ib, mode, iacc in pairs:
                d = _dot(in_refs[ia][...], in_refs[ib][...], mode)
                accs[iacc] = d if accs[iacc] is None else accs[iacc] + d
            epilogue(accs, in_refs, out_refs)
            return
        k = pl.program_id(k_axis)

        @pl.when(k == 0)
        def _():
            for r in acc_refs:
                r[...] = jnp.zeros_like(r)

        for ia, ib, mode, iacc in pairs:
            acc_refs[iacc][...] += _dot(in_refs[ia][...], in_refs[ib][...], mode)

        @pl.when(k == pl.num_programs(k_axis) - 1)
        def _():
            epilogue([r[...] for r in acc_refs], in_refs, out_refs)

    scratch = [] if k_axis is None else [pltpu.VMEM(s, F32) for s in acc_shapes]
    outs, couts = _hosted_call(
        body, comm, out_shape=out_shapes, grid=grid, in_specs=in_specs, out_specs=out_specs,
        scratch_shapes=scratch, sem=sem, name=name, args=ins)
    return outs if comm is None else (outs, couts)


def _rms_fwd(h, g, name):
    lp, d = h.shape
    tm = _row_tile(lp)

    def body(h_ref, g_ref, n_ref):
        x = h_ref[...]
        r = lax.rsqrt(jnp.mean(x * x, axis=-1, keepdims=True) + EPS)
        n_ref[...] = (x * r * g_ref[...]).astype(BF16)

    return pl.pallas_call(
        body, out_shape=jax.ShapeDtypeStruct((lp, d), BF16), grid=(lp // tm,),
        in_specs=[pl.BlockSpec((tm, d), lambda i: (i, 0)), pl.BlockSpec((1, d), lambda i: (0, 0))],
        out_specs=pl.BlockSpec((tm, d), lambda i: (i, 0)),
        compiler_params=_cparams(("parallel",)), name=name)(h, g.reshape(1, d))


def _rms_bwd_math(dn, x, g):
    r = lax.rsqrt(jnp.mean(x * x, axis=-1, keepdims=True) + EPS)
    xh = x * r
    dxh = dn * g
    dx = r * (dxh - xh * jnp.mean(dxh * xh, axis=-1, keepdims=True))
    return dx, dn * xh


def _ffn_up(h, gain, wgt, wut, comm=None):
    lp, d = h.shape
    fp = wgt.shape[1]
    tm = _row_tile(lp)
    n = _rms_fwd(h, gain, "rms_fwd_ffn")

    def up_body(n_ref, wg_ref, wu_ref, a_ref, b_ref, s_ref):
        x = n_ref[...]
        for jc in range(N_CHIPS):
            cols = slice(jc * fp, (jc + 1) * fp)
            a = _dot(x, wg_ref[jc], "nt")
            b = _dot(x, wu_ref[jc], "nt")
            a_ref[:, cols] = a.astype(BF16)
            b_ref[:, cols] = b.astype(BF16)
            s_ref[:, cols] = (a * _sigmoid(a) * b).astype(BF16)

    ff = N_CHIPS * fp
    act = jax.ShapeDtypeStruct((lp, ff), BF16)
    act_tile = pl.BlockSpec((tm, ff), lambda i: (i, 0))
    w_spec = pl.BlockSpec((N_CHIPS, fp, d), lambda i: (0, 0, 0))
    outs, couts = _hosted_call(
        up_body, comm, out_shape=[act, act, act], grid=(lp // tm,),
        in_specs=[pl.BlockSpec((tm, d), lambda i: (i, 0)), w_spec, w_spec],
        out_specs=[act_tile] * 3, scratch_shapes=[], sem=("parallel",), name="ffn_up", args=(n, wgt, wut))
    return (*outs, n), couts


def _ffn_down(s, wd, h, comm=None):
    lp, d = h.shape
    ff = s.shape[1]
    tm = _row_tile(lp)

    def down_epi(accs, in_refs, out_refs):
        out_refs[0][...] = in_refs[2][...] + 0.5 * accs[0]

    res = _matmul(
        "ffn_down", (lp // tm,), None, [s, wd.reshape(ff, d), h],
        [pl.BlockSpec((tm, ff), lambda i: (i, 0)), pl.BlockSpec((ff, d), lambda i: (0, 0)),
         pl.BlockSpec((tm, d), lambda i: (i, 0))],
        [(0, 1, "nn", 0)], [(tm, d)], down_epi,
        [jax.ShapeDtypeStruct((lp, d), F32)], [pl.BlockSpec((tm, d), lambda i: (i, 0))],
        ("parallel",), comm)
    return (res[0], []) if comm is None else (res[0][0], res[1])


def _tn_tiles(lp):
    return _row_tile(lp, 1408)


def _ffn_bwd(dh_pair, h_in, gain, wgt, wut, wd, f4, saved, pos, comm=None):
    dh, dhb = dh_pair
    a, b, s, n = saved
    lp, d = h_in.shape
    fp = wgt.shape[1]
    ff = N_CHIPS * fp
    tm = _row_tile(lp)
    ni = lp // tm
    tk = _tn_tiles(lp)
    nk = lp // tk

    def ds_body(dh_ref, wd_ref, a_ref, b_ref, da_ref, db_ref):
        x = dh_ref[...]
        for jc in range(N_CHIPS):
            cols = slice(jc * fp, (jc + 1) * fp)
            ds = 0.5 * _dot(x, wd_ref[jc], "nt")
            av = a_ref[:, cols].astype(F32)
            bv = b_ref[:, cols].astype(F32)
            sg = _sigmoid(av)
            da_ref[:, cols] = (ds * bv * sg * (1.0 + av * (1.0 - sg))).astype(BF16)
            db_ref[:, cols] = (ds * av * sg).astype(BF16)

    act = jax.ShapeDtypeStruct((lp, ff), BF16)
    act_tile = pl.BlockSpec((tm, ff), lambda i: (i, 0))
    (da, db), couts = _hosted_call(
        ds_body, comm, out_shape=[act, act], grid=(ni,),
        in_specs=[pl.BlockSpec((tm, d), lambda i: (i, 0)), pl.BlockSpec((N_CHIPS, fp, d), lambda i: (0, 0, 0)),
                  act_tile, act_tile],
        out_specs=[act_tile, act_tile], scratch_shapes=[], sem=("parallel",), name="ffn_bwd_ds",
        args=(dhb, wd, a, b))

    dw_shape = jax.ShapeDtypeStruct((N_CHIPS, f4, d), F32)
    dw_spec = pl.BlockSpec((None, f4, d), lambda j, k: (j, 0, 0))
    in_col = pl.BlockSpec((tk, fp), lambda j, k: (k, j))
    in_row = pl.BlockSpec((tk, d), lambda j, k: (k, 0))

    half_shape = jax.ShapeDtypeStruct((N_CHIPS, f4 // 2, d), BF16)
    half_spec = pl.BlockSpec((None, f4 // 2, d), lambda j, k: (j, 0, 0))

    def dwd_epi(accs, in_refs, out_refs):
        dw = 0.5 * accs[0]
        out_refs[0][...] = dw[:f4]
        out_refs[1][...] = _sibling_half(dw, f4)

    dwd, dwd_other = _matmul(
        "ffn_dwd", (N_CHIPS, nk), 1, [s, dhb], [in_col, in_row],
        [(0, 1, "tn", 0)], [(fp, d)], dwd_epi, [dw_shape, half_shape], [dw_spec, half_spec],
        ("arbitrary", "arbitrary"))

    def dwgu_epi(accs, in_refs, out_refs):
        for i, acc in enumerate(accs):
            out_refs[i][...] = acc[:f4]
            out_refs[2 + i][...] = _sibling_half(acc, f4)

    red_down = _Reduction([dwd], [dwd_other], pos, "ffn_d")
    (dwg, dwu, dwg_other, dwu_other), red_down.got = _matmul(
        "ffn_dwgu", (N_CHIPS, nk), 1, [n, da, db], [in_row, in_col, in_col],
        [(1, 0, "tn", 0), (2, 0, "tn", 1)], [(fp, d)] * 2, dwgu_epi,
        [dw_shape, dw_shape, half_shape, half_shape], [dw_spec, dw_spec, half_spec, half_spec],
        ("arbitrary", "arbitrary"), red_down.comm())

    def dn_epi(accs, in_refs, out_refs):
        i = pl.program_id(0)
        dx, dgrow = _rms_bwd_math(accs[0], in_refs[5][...], in_refs[6][...])
        dh_new = in_refs[4][...] + dx
        out_refs[0][...] = dh_new
        out_refs[2][...] = dh_new.astype(BF16)

        @pl.when(i == 0)
        def _():
            out_refs[1][...] = jnp.zeros_like(out_refs[1])

        out_refs[1][...] += jnp.sum(dgrow, axis=0, keepdims=True)

    red = _Reduction([dwg, dwu], [dwg_other, dwu_other], pos, "ffn_gu")
    row_spec = pl.BlockSpec((tm, d), lambda i: (i, 0))
    act_spec = pl.BlockSpec((tm, ff), lambda i: (i, 0))
    w_spec = pl.BlockSpec((ff, d), lambda i: (0, 0))
    one_spec = pl.BlockSpec((1, d), lambda i: (0, 0))
    (dh_in, dgain, dh_in_b), red.got = _matmul(
        "ffn_bwd_dn", (ni,), None, [da, wgt.reshape(ff, d), db, wut.reshape(ff, d), dh, h_in, gain.reshape(1, d)],
        [act_spec, w_spec, act_spec, w_spec, row_spec, row_spec, one_spec],
        [(0, 1, "nn", 0), (2, 3, "nn", 0)], [(tm, d)], dn_epi,
        [jax.ShapeDtypeStruct((lp, d), F32), jax.ShapeDtypeStruct((1, d), F32), jax.ShapeDtypeStruct((lp, d), BF16)],
        [row_spec, one_spec, row_spec], ("arbitrary",), red.comm())
    return (dh_in, dh_in_b), dgain, [red, red_down], couts


def _rope_tables(lp):
    pos = jnp.arange(lp, dtype=F32) - float(PAD_FRONT)
    inv_freq = ROPE_THETA ** (-jnp.arange(0, ROT_DIM, 2, dtype=F32) / ROT_DIM)
    ang = pos[:, None] * inv_freq[None, :]
    cos, sin = jnp.cos(ang), jnp.sin(ang)
    half = ROT_DIM // 2
    ones = jnp.ones((lp, HEAD_DIM - ROT_DIM), F32)
    zeros_h = jnp.zeros((lp, half), F32)
    zeros_r = jnp.zeros((lp, HEAD_DIM - ROT_DIM), F32)
    c = jnp.concatenate([cos, cos, ones], axis=1)
    s1 = jnp.concatenate([-sin, zeros_h, zeros_r], axis=1)
    s2 = jnp.concatenate([zeros_h, sin, zeros_r], axis=1)
    reps = LANES // HEAD_DIM
    return jnp.stack([jnp.tile(c, (1, reps)), jnp.tile(s1, (1, reps)), jnp.tile(s2, (1, reps))])


def _rope(x, c, s1, s2):
    half = ROT_DIM // 2
    outs = []
    for ch in range(x.shape[1] // LANES):
        xc = x[:, ch * LANES:(ch + 1) * LANES]
        outs.append(xc * c + pltpu.roll(xc, LANES - half, 1) * s1 + pltpu.roll(xc, half, 1) * s2)
    return outs[0] if len(outs) == 1 else jnp.concatenate(outs, axis=1)


def _rope_t(dy, c, s1, s2):
    half = ROT_DIM // 2
    outs = []
    for ch in range(dy.shape[1] // LANES):
        dc = dy[:, ch * LANES:(ch + 1) * LANES]
        outs.append(dc * c + pltpu.roll(dc * s1, half, 1) + pltpu.roll(dc * s2, LANES - half, 1))
    return outs[0] if len(outs) == 1 else jnp.concatenate(outs, axis=1)


def _in_proj(n, w_in, tabs, ssm_w, comm=None):
    lp, d = n.shape
    inw = w_in.shape[1]
    tm = _row_tile(lp)
    o1 = ATTN_WIDTH
    o2 = o1 + KV_WIDTH
    o3 = o2 + KV_WIDTH
    o4 = o3 + ssm_w
    o5 = o4 + d

    def epi(accs, in_refs, out_refs):
        z = accs[0]
        c, s1, s2 = in_refs[2][0], in_refs[2][1], in_refs[2][2]
        out_refs[0][...] = _rope(z[:, :o1], c, s1, s2).astype(BF16)
        out_refs[1][...] = _rope(z[:, o1:o2], c, s1, s2).astype(BF16)
        out_refs[2][...] = z[:, o2:o3].astype(BF16)
        out_refs[3][...] = z[:, o3:o4]
        out_refs[4][...] = z[:, o4:o5]
        out_refs[5][...] = z[:, o5:]

    def rs(w, dt):
        return jax.ShapeDtypeStruct((lp, w), dt), pl.BlockSpec((tm, w), lambda i: (i, 0))

    shapes, specs = zip(rs(o1, BF16), rs(KV_WIDTH, BF16), rs(KV_WIDTH, BF16), rs(ssm_w, F32), rs(d, F32), rs(d, F32))
    res = _matmul(
        "mix_in_proj", (lp // tm,), None, [n, w_in, tabs],
        [pl.BlockSpec((tm, d), lambda i: (i, 0)), pl.BlockSpec((d, inw), lambda i: (0, 0)),
         pl.BlockSpec((3, tm, LANES), lambda i: (0, i, 0))],
        [(0, 1, "nn", 0)], [(tm, inw)], epi, list(shapes), list(specs), ("parallel",), comm)
    return (res, []) if comm is None else res


def _attn_mask(b):
    rows = lax.broadcasted_iota(jnp.int32, (BLOCK, 3 * BLOCK), 0)
    cols = lax.broadcasted_iota(jnp.int32, (BLOCK, 3 * BLOCK), 1)
    qpos = b * BLOCK + rows - PAD_FRONT
    kpos = (b - 1) * BLOCK + cols - PAD_FRONT
    dist = qpos - kpos
    band = (cols < 2 * BLOCK) & (kpos >= N_META) & (dist >= 0) & (dist < BLOCK)
    mrow = cols - 2 * BLOCK
    meta = (mrow >= PAD_FRONT) & ((mrow - PAD_FRONT) <= qpos)
    return band | meta


def _attn_probs(qh, kk, mask, sink):
    s = _dot(qh, kk, "nt") * (HEAD_DIM ** -0.5)
    s = jnp.where(mask, s, NEG_INF)
    m = jnp.maximum(jnp.max(s, axis=-1, keepdims=True), sink)
    e = jnp.exp(s - m)
    es = jnp.exp(sink - m)
    z = jnp.sum(e, axis=-1, keepdims=True) + es
    inv = 1.0 / z
    return e * inv, es * inv


def _head(ref_or_val, h):
    return ref_or_val[:, h * HEAD_DIM:(h + 1) * HEAD_DIM]


def _attn_fwd(q, k, v, sinks, comm=None):
    lp = q.shape[0]
    nb = lp // BLOCK

    def body(sink_ref, q_ref, kp_ref, kc_ref, km_ref, vp_ref, vc_ref, vm_ref, o_ref):
        b = pl.program_id(0)
        mask = _attn_mask(b)
        for hk in range(N_KV_HEADS):
            kk = jnp.concatenate([_head(kp_ref, hk), _head(kc_ref, hk), _head(km_ref, hk)], axis=0)
            vv = jnp.concatenate([_head(vp_ref, hk), _head(vc_ref, hk), _head(vm_ref, hk)], axis=0)
            for g in range(Q_PER_KV):
                h = hk * Q_PER_KV + g
                p, _ = _attn_probs(_head(q_ref, h), kk, mask, sink_ref[h])
                o_ref[:, h * HEAD_DIM:(h + 1) * HEAD_DIM] = _dot(p, vv, "nn").astype(BF16)

    cur = lambda b: (b, 0)
    prev = lambda b: (jnp.maximum(b - 1, 0), 0)
    first = lambda b: (0, 0)
    kvs = lambda f: pl.BlockSpec((BLOCK, KV_WIDTH), f)
    (o,), couts = _hosted_call(
        body, comm, out_shape=[jax.ShapeDtypeStruct((lp, ATTN_WIDTH), BF16)], grid=(nb,),
        in_specs=[pl.BlockSpec(memory_space=pltpu.SMEM), pl.BlockSpec((BLOCK, ATTN_WIDTH), cur),
                  kvs(prev), kvs(cur), kvs(first), kvs(prev), kvs(cur), kvs(first)],
        out_specs=[pl.BlockSpec((BLOCK, ATTN_WIDTH), cur)], scratch_shapes=[],
        sem=("parallel",), name="attn_fwd", args=(sinks, q, k, k, k, v, v, v))
    return o, couts


def _attn_bwd(q, k, v, do, sinks, tabs, comm=None):
    lp = q.shape[0]
    nb = lp // BLOCK
    scale = HEAD_DIM ** -0.5

    def body(sink_ref, q_ref, do_ref, kp_ref, kc_ref, km_ref, vp_ref, vc_ref, vm_ref, tq_ref, tk_ref, t0_ref,
             dq_ref, dk_ref, dv_ref, dkm_ref, dvm_ref, dsink_ref,
             dq_s, dkk_s, dvv_s, ck_s, cv_s, mk_s, mv_s):
        b = pl.program_id(0)

        @pl.when(b == 0)
        def _():
            for r in (ck_s, cv_s, mk_s, mv_s, dsink_ref):
                r[...] = jnp.zeros_like(r)

        @pl.when(b < nb)
        def _():
            mask = _attn_mask(b)
            for hk in range(N_KV_HEADS):
                kk = jnp.concatenate([_head(kp_ref, hk), _head(kc_ref, hk), _head(km_ref, hk)], axis=0)
                vv = jnp.concatenate([_head(vp_ref, hk), _head(vc_ref, hk), _head(vm_ref, hk)], axis=0)
                dkk = jnp.zeros((3 * BLOCK, HEAD_DIM), F32)
                dvv = jnp.zeros((3 * BLOCK, HEAD_DIM), F32)
                for g in range(Q_PER_KV):
                    h = hk * Q_PER_KV + g
                    qh = _head(q_ref, h)
                    doh = _head(do_ref, h)
                    p, ps = _attn_probs(qh, kk, mask, sink_ref[h])
                    dp = _dot(doh, vv, "nt")
                    delta = jnp.sum(p * dp, axis=-1, keepdims=True)
                    ds = (p * (dp - delta)).astype(BF16)
                    dsink_ref[h:h + 1, :] += jnp.zeros((1, LANES), F32) - jnp.sum(ps * delta)
                    dq_s[:, h * HEAD_DIM:(h + 1) * HEAD_DIM] = _dot(ds, kk, "nn") * scale
                    dkk = dkk + _dot(ds, qh, "tn") * scale
                    dvv = dvv + _dot(p, doh, "tn")
                dkk_s[:, hk * HEAD_DIM:(hk + 1) * HEAD_DIM] = dkk
                dvv_s[:, hk * HEAD_DIM:(hk + 1) * HEAD_DIM] = dvv
            dq_ref[...] = _rope_t(dq_s[...], tq_ref[0], tq_ref[1], tq_ref[2])
            dk_ref[...] = _rope_t(ck_s[...] + dkk_s[0:BLOCK, :], tk_ref[0], tk_ref[1], tk_ref[2])
            dv_ref[...] = cv_s[...] + dvv_s[0:BLOCK, :]
            ck_s[...] = dkk_s[BLOCK:2 * BLOCK, :]
            cv_s[...] = dvv_s[BLOCK:2 * BLOCK, :]
            mk_s[...] += dkk_s[2 * BLOCK:, :]
            mv_s[...] += dvv_s[2 * BLOCK:, :]

        @pl.when(b == nb)
        def _():
            dk_ref[...] = _rope_t(ck_s[...], tk_ref[0], tk_ref[1], tk_ref[2])
            dv_ref[...] = cv_s[...]
            dkm_ref[...] = _rope_t(mk_s[...], t0_ref[0], t0_ref[1], t0_ref[2])
            dvm_ref[...] = mv_s[...]

    cur = lambda b: (jnp.minimum(b, nb - 1), 0)
    prev = lambda b: (jnp.clip(b - 1, 0, nb - 1), 0)
    first = lambda b: (0, 0)
    kvs = lambda f: pl.BlockSpec((BLOCK, KV_WIDTH), f)
    tab = lambda f: pl.BlockSpec((3, BLOCK, LANES), lambda b: (0,) + f(b)[:1] + (0,))
    kv_out = lambda b: (jnp.maximum(b - 1, 0), 0)
    return _hosted_call(
        body, comm,
        out_shape=[jax.ShapeDtypeStruct((lp, ATTN_WIDTH), F32), jax.ShapeDtypeStruct((lp, KV_WIDTH), F32),
                   jax.ShapeDtypeStruct((lp, KV_WIDTH), F32), jax.ShapeDtypeStruct((BLOCK, KV_WIDTH), F32),
                   jax.ShapeDtypeStruct((BLOCK, KV_WIDTH), F32), jax.ShapeDtypeStruct((N_Q_HEADS, LANES), F32)],
        grid=(nb + 1,),
        in_specs=[pl.BlockSpec(memory_space=pltpu.SMEM), pl.BlockSpec((BLOCK, ATTN_WIDTH), cur),
                  pl.BlockSpec((BLOCK, ATTN_WIDTH), cur),
                  kvs(prev), kvs(cur), kvs(first), kvs(prev), kvs(cur), kvs(first),
                  tab(cur), tab(kv_out), tab(first)],
        out_specs=[pl.BlockSpec((BLOCK, ATTN_WIDTH), cur), kvs(kv_out), kvs(kv_out), kvs(first), kvs(first),
                   pl.BlockSpec((N_Q_HEADS, LANES), first)],
        scratch_shapes=[pltpu.VMEM((BLOCK, ATTN_WIDTH), F32), pltpu.VMEM((3 * BLOCK, KV_WIDTH), F32),
                        pltpu.VMEM((3 * BLOCK, KV_WIDTH), F32), pltpu.VMEM((BLOCK, KV_WIDTH), F32),
                        pltpu.VMEM((BLOCK, KV_WIDTH), F32), pltpu.VMEM((BLOCK, KV_WIDTH), F32),
                        pltpu.VMEM((BLOCK, KV_WIDTH), F32)],
        sem=("arbitrary",), name="attn_bwd", args=(sinks, q, do, k, k, k, v, v, v, tabs, tabs, tabs))


def _cmul(ar, ai, br, bi):
    return ar * br - ai * bi, ar * bi + ai * br


def _cpow(lr, li, n):
    rr = ri = None
    br, bi = lr, li
    while n:
        if n & 1:
            rr, ri = (br, bi) if rr is None else _cmul(rr, ri, br, bi)
        n >>= 1
        if n:
            br, bi = _cmul(br, bi, br, bi)
    return rr, ri


def _shift_rows(x, d, reverse):
    rows = lax.broadcasted_iota(jnp.int32, x.shape, 0)
    if not reverse:
        return jnp.where(rows >= d, pltpu.roll(x, d, 0), 0.0)
    return jnp.where(rows < SUBLANES - d, pltpu.roll(x, SUBLANES - d, 0), 0.0)


def _sublane_powers(mr, mi, reverse):
    rows = lax.broadcasted_iota(jnp.int32, mr.shape, 0)
    e = SUBLANES - 1 - rows if reverse else rows
    pr, pi = jnp.ones_like(mr), jnp.zeros_like(mr)
    br, bi = mr, mi
    for d in (1, 2, 4):
        tr, ti = _cmul(pr, pi, br, bi)
        on = (e & d) != 0
        pr, pi = jnp.where(on, tr, pr), jnp.where(on, ti, pi)
        if d < 4:
            br, bi = _cmul(br, bi, br, bi)
    return pr, pi


def _inclusive_prefix(er, ei, mr, mi, reverse):
    ir, ii, pr, pi = er, ei, mr, mi
    for d in (1, 2, 4):
        tr, ti = _cmul(pr, pi, _shift_rows(ir, d, reverse), _shift_rows(ii, d, reverse))
        ir, ii = ir + tr, ii + ti
        if d < 4:
            pr, pi = _cmul(pr, pi, pr, pi)
    return ir, ii


def _chain_rows(a, t, seg):
    return pl.ds(a * SUBLANES * seg + t, SUBLANES, stride=seg)


def _seg_scan(xr_ref, xi_ref, lam, seg, nchain, reverse, store, init, extra=None):
    nt = len(lam)
    acc0 = () if extra is None else extra[1]

    def step(i, carry):
        hs, acc = carry
        t = seg - 1 - i if reverse else i
        out = []
        for a in range(nchain):
            sl = _chain_rows(a, t, seg)
            for j in range(nt):
                lr, li = lam[j]
                k = 2 * (a * nt + j)
                hr, hi = hs[k], hs[k + 1]
                nr = lr * hr - li * hi + xr_ref[j, sl, :]
                ni = lr * hi + li * hr + xi_ref[j, sl, :]
                if store:
                    xr_ref[j, sl, :] = nr
                    xi_ref[j, sl, :] = ni
                if extra is not None:
                    acc = extra[0](t, a, j, nr, ni, acc)
                out += [nr, ni]
        return tuple(out), acc

    return lax.fori_loop(0, seg, step, (tuple(init), acc0))


def _ssm_scan(xr_ref, xi_ref, lam, seg, nchain, reverse, extra=None):
    nt = len(lam)
    zero = [jnp.zeros((SUBLANES, LANES), F32)] * (2 * nt * nchain)
    ends, _ = _seg_scan(xr_ref, xi_ref, lam, seg, nchain, reverse, False, zero)
    init = [None] * (2 * nt * nchain)
    last = 0 if reverse else SUBLANES - 1
    for j in range(nt):
        mr, mi = _cpow(lam[j][0], lam[j][1], seg)
        m8r, m8i = _cpow(mr, mi, SUBLANES)
        pwr, pwi = _sublane_powers(mr, mi, reverse)
        gr = gi = jnp.zeros((SUBLANES, LANES), F32)
        for a in (reversed(range(nchain)) if reverse else range(nchain)):
            k = 2 * (a * nt + j)
            incr, inci = _inclusive_prefix(ends[k], ends[k + 1], mr, mi, reverse)
            tr, ti = _cmul(pwr, pwi, gr, gi)
            init[k] = _shift_rows(incr, 1, reverse) + tr
            init[k + 1] = _shift_rows(inci, 1, reverse) + ti
            g2r, g2i = _cmul(m8r, m8i, gr, gi)
            gr = g2r + jnp.broadcast_to(incr[last:last + 1, :], gr.shape)
            gi = g2i + jnp.broadcast_to(inci[last:last + 1, :], gi.shape)
    _, acc = _seg_scan(xr_ref, xi_ref, lam, seg, nchain, reverse, True, init, extra)
    return acc


def _diag_mask():
    steps = LANES // SSM_GROUP // GROUPS_PER_TILE
    return (jnp.eye(steps, dtype=F32)[:, None, :, None] * jnp.eye(GROUPS_PER_TILE, dtype=F32)[None, :, None, :])


def _ssm_tables(bb_re, bb_im, c_re, c_im):
    g = bb_re.shape[0]
    nt = g // GROUPS_PER_TILE
    steps = LANES // SSM_GROUP // GROUPS_PER_TILE
    mask = _diag_mask()

    def b_tab(bb):
        x = bb.reshape(nt // steps, steps, GROUPS_PER_TILE, SSM_STATE, SSM_GROUP)
        x = jnp.transpose(x, (0, 1, 4, 2, 3))[:, :, None, None]
        m = jnp.transpose(mask, (0, 2, 3, 1))[None, :, :, :, None, :, None]
        return (x * m).reshape(nt, LANES, TILE_STATES)

    def c_tab(c):
        x = c.reshape(nt // steps, steps, GROUPS_PER_TILE, SSM_GROUP, SSM_STATE)
        x = jnp.transpose(x, (0, 1, 2, 4, 3))[:, :, :, :, None, None]
        m = mask[None, :, :, None, :, :, None]
        return (x * m).reshape(nt, TILE_STATES, LANES)

    return b_tab(bb_re), b_tab(bb_im), c_tab(c_re), c_tab(c_im)


def _ssm_untable_b(db, g):
    nt = g // GROUPS_PER_TILE
    steps = LANES // SSM_GROUP // GROUPS_PER_TILE
    x = db.reshape(nt // steps, steps, GROUPS_PER_TILE, SSM_STATE, steps, GROUPS_PER_TILE, SSM_GROUP)
    m = _diag_mask()[None, :, :, None, :, :, None]
    return jnp.sum(x * m, axis=(4, 5)).reshape(g, SSM_STATE, SSM_GROUP)


def _ssm_untable_c(dc, g):
    nt = g // GROUPS_PER_TILE
    steps = LANES // SSM_GROUP // GROUPS_PER_TILE
    x = dc.reshape(nt // steps, steps, steps, GROUPS_PER_TILE, SSM_GROUP, GROUPS_PER_TILE, SSM_STATE)
    m = jnp.transpose(_diag_mask(), (0, 2, 3, 1))[None, :, :, :, None, :, None]
    out = jnp.sum(x * m, axis=(2, 3))
    return jnp.transpose(out, (0, 1, 3, 2, 4)).reshape(g, SSM_GROUP, SSM_STATE)


def _lam_tiles(lam_ref):
    out = []
    for j in range(TILE_STATES // LANES):
        out.append(jnp.broadcast_to(lam_ref[:, j * LANES:(j + 1) * LANES], (SUBLANES, LANES)))
    return out


def _scan_chains(lp):
    for n in (4, 2, 1):
        if lp % (SUBLANES * n) == 0 and (lp // SUBLANES) % 16 == 0:
            return n
    raise ValueError(lp)


def _split_tiles(dst_ref, rows, val):
    for j in range(val.shape[1] // LANES):
        dst_ref[j, rows, :] = val[:, j * LANES:(j + 1) * LANES]


def _cat_tiles(src_ref, rows):
    njt = src_ref.shape[0]
    return jnp.concatenate([src_ref[j, rows, :] for j in range(njt)], axis=1).astype(BF16)


def _ssm_fwd(u, lam_re, lam_im, tb_re, tb_im, tc_re, tc_im, d_skip, comm=None):
    lp, w = u.shape
    nt = tb_re.shape[0]
    nchain = _scan_chains(lp)
    seg = lp // (SUBLANES * nchain)
    chunk = lp // SUBLANES
    njt = TILE_STATES // LANES

    def body(u_ref, lr_ref, li_ref, br_ref, bi_ref, cr_ref, ci_ref, d_ref, y_ref, yg_ref, xr, xi):
        t = pl.program_id(0)
        for s in range(SUBLANES):
            rs = pl.ds(s * chunk, chunk)
            ub = u_ref[rs, :].astype(BF16)
            _split_tiles(xr, rs, _dot(ub, br_ref[...], "nn"))
            _split_tiles(xi, rs, _dot(ub, bi_ref[...], "nn"))
        lrs, lis = _lam_tiles(lr_ref), _lam_tiles(li_ref)
        _ssm_scan(xr, xi, list(zip(lrs, lis)), seg, nchain, False)
        for s in range(SUBLANES):
            rs = pl.ds(s * chunk, chunk)
            y = _dot(_cat_tiles(xr, rs), cr_ref[...], "nn") - _dot(_cat_tiles(xi, rs), ci_ref[...], "nn")

            @pl.when(t % 2 == 0)
            def _():
                y_ref[rs, :] = y + d_ref[...] * u_ref[rs, :]

            @pl.when(t % 2 == 1)
            def _():
                total = y_ref[rs, :] + y
                y_ref[rs, :] = total
                yg_ref[rs, :] = _gelu(total).astype(BF16)

    blk = pl.BlockSpec((lp, LANES), lambda t: (0, t // 2))
    lam_spec = pl.BlockSpec((None, 1, TILE_STATES), lambda t: (t, 0, 0))
    b_spec = pl.BlockSpec((None, LANES, TILE_STATES), lambda t: (t, 0, 0))
    c_spec = pl.BlockSpec((None, TILE_STATES, LANES), lambda t: (t, 0, 0))
    (y, yg), couts = _hosted_call(
        body, comm, out_shape=[jax.ShapeDtypeStruct((lp, w), F32), jax.ShapeDtypeStruct((lp, w), BF16)], grid=(nt,),
        in_specs=[blk, lam_spec, lam_spec, b_spec, b_spec, c_spec, c_spec,
                  pl.BlockSpec((1, LANES), lambda t: (0, t // 2))],
        out_specs=[blk, blk],
        scratch_shapes=[pltpu.VMEM((njt, lp, LANES), F32), pltpu.VMEM((njt, lp, LANES), F32)],
        sem=("arbitrary",), name="ssm_fwd",
        args=(u, lam_re, lam_im, tb_re, tb_im, tc_re, tc_im, d_skip.reshape(1, w)))
    return (y, yg), couts


def _ssm_bwd(u, dy, lam_re, lam_im, tb_re, tb_im, tc_re, tc_im, d_skip, comm=None):
    lp, w = u.shape
    nt = tb_re.shape[0]
    nchain = _scan_chains(lp)
    seg = lp // (SUBLANES * nchain)
    chunk = lp // SUBLANES
    njt = TILE_STATES // LANES
    tbt_re, tbt_im = jnp.swapaxes(tb_re, 1, 2), jnp.swapaxes(tb_im, 1, 2)
    tct_re, tct_im = jnp.swapaxes(tc_re, 1, 2), jnp.swapaxes(tc_im, 1, 2)

    def body(u_ref, dy_ref, lr_ref, li_ref, br_ref, bi_ref, btr_ref, bti_ref, ctr_ref, cti_ref, d_ref,
             du_ref, dlr_ref, dli_ref, dbr_ref, dbi_ref, dcr_ref, dci_ref, dd_ref, hr, hi, ar, ai):
        t = pl.program_id(0)
        lrs, lis = _lam_tiles(lr_ref), _lam_tiles(li_ref)
        for s in range(SUBLANES):
            rs = pl.ds(s * chunk, chunk)
            ub = u_ref[rs, :].astype(BF16)
            dyb = dy_ref[rs, :].astype(BF16)
            _split_tiles(hr, rs, _dot(ub, br_ref[...], "nn"))
            _split_tiles(hi, rs, _dot(ub, bi_ref[...], "nn"))
            _split_tiles(ar, rs, _dot(dyb, ctr_ref[...], "nn"))
            _split_tiles(ai, rs, -_dot(dyb, cti_ref[...], "nn"))
        _ssm_scan(hr, hi, list(zip(lrs, lis)), seg, nchain, False)

        def dlam_step(tt, a, j, a_r, a_i, acc):
            sl = _chain_rows(a, jnp.maximum(tt - 1, 0), seg)
            p_r, p_i = hr[j, sl, :], hi[j, sl, :]
            acc = list(acc)
            acc[2 * j] = acc[2 * j] + jnp.where(tt > 0, a_r * p_r + a_i * p_i, 0.0)
            acc[2 * j + 1] = acc[2 * j + 1] + jnp.where(tt > 0, a_i * p_r - a_r * p_i, 0.0)
            return tuple(acc)

        zero = tuple([jnp.zeros((SUBLANES, LANES), F32)] * (2 * njt))
        conj = [(lr, -li) for lr, li in zip(lrs, lis)]
        acc = list(_ssm_scan(ar, ai, conj, seg, nchain, True, (dlam_step, zero)))
        row0 = lax.broadcasted_iota(jnp.int32, (SUBLANES, LANES), 0) == 0
        for j in range(njt):
            cs = slice(j * LANES, (j + 1) * LANES)
            for a in range(nchain):
                p_r = _shift_rows(hr[j, _chain_rows(a, seg - 1, seg), :], 1, False)
                p_i = _shift_rows(hi[j, _chain_rows(a, seg - 1, seg), :], 1, False)
                if a > 0:
                    before = pl.ds(a * SUBLANES * seg - 1, 1)
                    p_r = jnp.where(row0, jnp.broadcast_to(hr[j, before, :], p_r.shape), p_r)
                    p_i = jnp.where(row0, jnp.broadcast_to(hi[j, before, :], p_i.shape), p_i)
                a_r, a_i = ar[j, _chain_rows(a, 0, seg), :], ai[j, _chain_rows(a, 0, seg), :]
                acc[2 * j] = acc[2 * j] + a_r * p_r + a_i * p_i
                acc[2 * j + 1] = acc[2 * j + 1] + a_i * p_r - a_r * p_i
            dlr_ref[:, cs] = jnp.sum(acc[2 * j], axis=0, keepdims=True)
            dli_ref[:, cs] = jnp.sum(acc[2 * j + 1], axis=0, keepdims=True)

        dd = jnp.zeros((1, LANES), F32)
        for s in range(SUBLANES):
            rs = pl.ds(s * chunk, chunk)
            ub = u_ref[rs, :].astype(BF16)
            dyv = dy_ref[rs, :]
            dyb = dyv.astype(BF16)
            arb, aib = _cat_tiles(ar, rs), _cat_tiles(ai, rs)
            hrb, hib = _cat_tiles(hr, rs), _cat_tiles(hi, rs)
            du = _dot(arb, btr_ref[...], "nn") + _dot(aib, bti_ref[...], "nn")
            upd = [(dbr_ref, _dot(arb, ub, "tn")), (dbi_ref, _dot(aib, ub, "tn")),
                   (dcr_ref, _dot(dyb, hrb, "tn")), (dci_ref, -_dot(dyb, hib, "tn"))]
            for ref, val in upd:
                if s == 0:
                    ref[...] = val
                else:
                    ref[...] += val
            rows = lax.broadcasted_iota(jnp.int32, (chunk, LANES), 0) + s * chunk
            keep = rows >= PAD_FRONT
            dd = dd + jnp.sum(dyv * u_ref[rs, :], axis=0, keepdims=True)

            @pl.when(t % 2 == 0)
            def _():
                du_ref[rs, :] = jnp.where(keep, du + d_ref[...] * dyv, 0.0)

            @pl.when(t % 2 == 1)
            def _():
                du_ref[rs, :] += jnp.where(keep, du, 0.0)

        @pl.when(t % 2 == 0)
        def _():
            dd_ref[...] = dd

    blk = pl.BlockSpec((lp, LANES), lambda t: (0, t // 2))
    vec = pl.BlockSpec((1, LANES), lambda t: (0, t // 2))
    lam_spec = pl.BlockSpec((None, 1, TILE_STATES), lambda t: (t, 0, 0))
    b_spec = pl.BlockSpec((None, LANES, TILE_STATES), lambda t: (t, 0, 0))
    c_spec = pl.BlockSpec((None, TILE_STATES, LANES), lambda t: (t, 0, 0))
    lam_shape = jax.ShapeDtypeStruct((nt, 1, TILE_STATES), F32)
    bt_shape = jax.ShapeDtypeStruct((nt, TILE_STATES, LANES), F32)
    ct_shape = jax.ShapeDtypeStruct((nt, LANES, TILE_STATES), F32)
    st = pltpu.VMEM((njt, lp, LANES), F32)
    return _hosted_call(
        body, comm,
        out_shape=[jax.ShapeDtypeStruct((lp, w), F32), lam_shape, lam_shape, bt_shape, bt_shape, ct_shape, ct_shape,
                   jax.ShapeDtypeStruct((1, w), F32)],
        grid=(nt,),
        in_specs=[blk, blk, lam_spec, lam_spec, b_spec, b_spec, c_spec, c_spec, b_spec, b_spec, vec],
        out_specs=[blk, lam_spec, lam_spec, c_spec, c_spec, b_spec, b_spec, vec],
        scratch_shapes=[st, st, st, st], sem=("arbitrary",), name="ssm_bwd",
        args=(u, dy, lam_re, lam_im, tb_re, tb_im, tbt_re, tbt_im, tct_re, tct_im, d_skip.reshape(1, w)))


def _ssm_params(a_re, a_im, log_dt, b_re, b_im):
    dt = jnp.exp(log_dt)[:, None]
    mag = jnp.exp(a_re * dt)
    lb_re = mag * jnp.cos(a_im * dt)
    lb_im = mag * jnp.sin(a_im * dt)
    den = a_re * a_re + a_im * a_im
    num_re = lb_re - 1.0
    coef_re = (num_re * a_re + lb_im * a_im) / den
    coef_im = (lb_im * a_re - num_re * a_im) / den
    bb_re = coef_re[..., None] * b_re - coef_im[..., None] * b_im
    bb_im = coef_re[..., None] * b_im + coef_im[..., None] * b_re
    return lb_re, lb_im, bb_re, bb_im


def _merge_fwd(o, yg, ga, gs, w3t, comm=None):
    lp, d = ga.shape
    kw = w3t.shape[2]
    tm = _row_tile(lp)

    def epi(accs, in_refs, out_refs):
        attn, vv, gg = accs
        out_refs[0][...] = (_sigmoid(in_refs[5][...]) * attn
                            + _sigmoid(in_refs[6][...]) * (vv * _sigmoid(gg))).astype(BF16)

    wspec = lambda which: pl.BlockSpec((None, d, kw), lambda i: (which, 0, 0))
    rowspec = pl.BlockSpec((tm, d), lambda i: (i, 0))
    aspec = pl.BlockSpec((tm, kw), lambda i: (i, 0))
    res = _matmul(
        "merge_fwd", (lp // tm,), None, [o, yg, w3t, w3t, w3t, ga, gs],
        [aspec, aspec, wspec(0), wspec(1), wspec(2), rowspec, rowspec],
        [(0, 2, "nt", 0), (1, 3, "nt", 1), (1, 4, "nt", 2)], [(tm, d)] * 3, epi,
        [jax.ShapeDtypeStruct((lp, d), BF16)], [rowspec], ("parallel",), comm)
    return (res[0], []) if comm is None else (res[0][0], res[1])


def _out_proj(merged, w_out, h, comm=None):
    lp, d = h.shape
    tm = _row_tile(lp)

    def epi(accs, in_refs, out_refs):
        out_refs[0][...] = in_refs[2][...] + accs[0]

    rowspec = pl.BlockSpec((tm, d), lambda i: (i, 0))
    res = _matmul(
        "mix_out_proj", (lp // tm,), None, [merged, w_out, h],
        [rowspec, pl.BlockSpec((d, d), lambda i: (0, 0)), rowspec],
        [(0, 1, "nn", 0)], [(tm, d)], epi, [jax.ShapeDtypeStruct((lp, d), F32)], [rowspec], ("parallel",), comm)
    return (res[0], []) if comm is None else (res[0][0], res[1])


def _merge_bwd(dhb, w_out, o, yg, ga, gs, w3t):
    lp, d = ga.shape
    kw = w3t.shape[2]
    tm = _row_tile(lp)

    def epi(accs, in_refs, out_refs):
        dm, attn, vv, gg = accs
        sa = _sigmoid(in_refs[7][...])
        ss = _sigmoid(in_refs[8][...])
        sg = _sigmoid(gg)
        ssm = vv * sg
        dssm = dm * ss
        out_refs[0][...] = (dm * sa).astype(BF16)
        out_refs[1][...] = (dssm * sg).astype(BF16)
        out_refs[2][...] = (dssm * vv * sg * (1.0 - sg)).astype(BF16)
        out_refs[3][...] = (dm * attn * sa * (1.0 - sa)).astype(BF16)
        out_refs[4][...] = (dm * ssm * ss * (1.0 - ss)).astype(BF16)

    wspec = lambda which: pl.BlockSpec((None, d, kw), lambda i: (which, 0, 0))
    rowspec = pl.BlockSpec((tm, d), lambda i: (i, 0))
    aspec = pl.BlockSpec((tm, kw), lambda i: (i, 0))
    shp = jax.ShapeDtypeStruct((lp, d), BF16)
    return _matmul(
        "merge_bwd", (lp // tm,), None, [dhb, w_out, o, yg, w3t, w3t, w3t, ga, gs],
        [rowspec, pl.BlockSpec((d, d), lambda i: (0, 0)), aspec, aspec, wspec(0), wspec(1), wspec(2), rowspec,
         rowspec],
        [(0, 1, "nt", 0), (2, 4, "nt", 1), (3, 5, "nt", 2), (3, 6, "nt", 3)], [(tm, d)] * 4, epi,
        [shp] * 5, [rowspec] * 5, ("parallel",))


def _branch_bwd(dattn, dv, dg, w3t, y):
    lp, d = dattn.shape
    kw = w3t.shape[2]
    tm = _row_tile(lp)

    def epi(accs, in_refs, out_refs):
        out_refs[0][...] = accs[0].astype(BF16)
        out_refs[1][...] = accs[1] * _gelu_grad(in_refs[6][...])

    wspec = lambda which: pl.BlockSpec((None, d, kw), lambda i: (which, 0, 0))
    rowspec = pl.BlockSpec((tm, d), lambda i: (i, 0))
    aspec = pl.BlockSpec((tm, kw), lambda i: (i, 0))
    return _matmul(
        "branch_bwd", (lp // tm,), None, [dattn, dv, dg, w3t, w3t, w3t, y],
        [rowspec, rowspec, rowspec, wspec(0), wspec(1), wspec(2), aspec],
        [(0, 3, "nn", 0), (1, 4, "nn", 1), (2, 5, "nn", 1)], [(tm, kw)] * 2, epi,
        [jax.ShapeDtypeStruct((lp, kw), BF16), jax.ShapeDtypeStruct((lp, kw), F32)], [aspec, aspec],
        ("parallel",))


def _sibling_half(acc, rows):
    half = rows // 2
    return jnp.where(lax.axis_index("c") == 0, acc[half:rows], acc[:half]).astype(BF16)


def _tn_cols(x, ys, name):
    lp, kx = x.shape
    n = ys[0].shape[1]
    n4 = n // N_CHIPS
    tk = _tn_tiles(lp)
    ny = len(ys)

    def epi(accs, in_refs, out_refs):
        for i, acc in enumerate(accs):
            out_refs[i][...] = acc
            out_refs[ny + i][...] = _sibling_half(acc, kx)

    shp = jax.ShapeDtypeStruct((N_CHIPS, kx, n4), F32)
    shp_half = jax.ShapeDtypeStruct((N_CHIPS, kx // 2, n4), BF16)
    res = _matmul(
        name, (N_CHIPS, lp // tk), 1, [x] + list(ys),
        [pl.BlockSpec((tk, kx), lambda j, k: (k, 0))] + [pl.BlockSpec((tk, n4), lambda j, k: (k, j))] * ny,
        [(0, 1 + i, "tn", i) for i in range(ny)], [(kx, n4)] * ny, epi,
        [shp] * ny + [shp_half] * ny,
        [pl.BlockSpec((None, kx, n4), lambda j, k: (j, 0, 0))] * ny
        + [pl.BlockSpec((None, kx // 2, n4), lambda j, k: (j, 0, 0))] * ny,
        ("arbitrary", "arbitrary"))
    return res[:ny], res[ny:]


def _tn_full(x, y, name, tn_cols=None):
    lp, kx = x.shape
    n = y.shape[1]
    tk = _tn_tiles(lp)
    tn = n if tn_cols is None else tn_cols
    k4 = kx // N_CHIPS

    def epi(accs, in_refs, out_refs):
        for j in range(N_CHIPS):
            slab = accs[0][j * k4:(j + 1) * k4]
            out_refs[0][j] = slab
            out_refs[1][j] = _sibling_half(slab, k4)

    return _matmul(
        name, (n // tn, lp // tk), 1, [x, y],
        [pl.BlockSpec((tk, kx), lambda j, k: (k, 0)), pl.BlockSpec((tk, tn), lambda j, k: (k, j))],
        [(0, 1, "tn", 0)], [(kx, tn)], epi,
        [jax.ShapeDtypeStruct((N_CHIPS, k4, n), F32), jax.ShapeDtypeStruct((N_CHIPS, k4 // 2, n), BF16)],
        [pl.BlockSpec((N_CHIPS, k4, tn), lambda j, k: (0, 0, j)),
         pl.BlockSpec((N_CHIPS, k4 // 2, tn), lambda j, k: (0, 0, j))],
        ("arbitrary", "arbitrary"))


def _in_proj_bwd(dz, w_in, dh, h_in, gain):
    lp, d = h_in.shape
    inw = w_in.shape[1]
    tm = _row_tile(lp)

    def epi(accs, in_refs, out_refs):
        i = pl.program_id(0)
        dx, dgrow = _rms_bwd_math(accs[0], in_refs[3][...], in_refs[4][...])
        dh_new = in_refs[2][...] + dx
        out_refs[0][...] = dh_new
        out_refs[2][...] = dh_new.astype(BF16)

        @pl.when(i == 0)
        def _():
            out_refs[1][...] = jnp.zeros_like(out_refs[1])

        out_refs[1][...] += jnp.sum(dgrow, axis=0, keepdims=True)

    row = pl.BlockSpec((tm, d), lambda i: (i, 0))
    return _matmul(
        "mix_in_proj_bwd", (lp // tm,), None, [dz, w_in, dh, h_in, gain.reshape(1, d)],
        [pl.BlockSpec((tm, inw), lambda i: (i, 0)), pl.BlockSpec((d, inw), lambda i: (0, 0)), row, row,
         pl.BlockSpec((1, d), lambda i: (0, 0))],
        [(0, 1, "nt", 0)], [(tm, d)], epi,
        [jax.ShapeDtypeStruct((lp, d), F32), jax.ShapeDtypeStruct((1, d), F32), jax.ShapeDtypeStruct((lp, d), BF16)],
        [row, pl.BlockSpec((1, d), lambda i: (0, 0)), row], ("arbitrary",))


def _loss_head(h, gain, target):
    lp, d = h.shape
    nb = lp // BLOCK

    def body(h_ref, g_ref, t_ref, dh_ref, dg_ref, loss_ref, dhb_ref):
        i = pl.program_id(0)

        @pl.when(i == 0)
        def _():
            dg_ref[...] = jnp.zeros_like(dg_ref)
            loss_ref[...] = jnp.zeros_like(loss_ref)
            dh_ref[...] = jnp.zeros_like(dh_ref)
            dhb_ref[...] = jnp.zeros_like(dhb_ref)

        @pl.when(i > 0)
        def _():
            x = h_ref[...]
            g = g_ref[...]
            r = lax.rsqrt(jnp.mean(x * x, axis=-1, keepdims=True) + EPS)
            err = x * r * g - t_ref[...]
            loss_ref[...] += jnp.zeros_like(loss_ref) + 0.5 * jnp.sum(jnp.sum(err * err, axis=-1, keepdims=True)) / d
            dx, dgrow = _rms_bwd_math(err * (1.0 / d), x, g)
            dh_ref[...] = dx
            dhb_ref[...] = dx.astype(BF16)
            dg_ref[...] += jnp.sum(dgrow, axis=0, keepdims=True)

    row = pl.BlockSpec((BLOCK, d), lambda i: (i, 0))
    one = pl.BlockSpec((1, d), lambda i: (0, 0))
    return pl.pallas_call(
        body,
        out_shape=[jax.ShapeDtypeStruct((lp, d), F32), jax.ShapeDtypeStruct((1, d), F32),
                   jax.ShapeDtypeStruct((SUBLANES, LANES), F32), jax.ShapeDtypeStruct((lp, d), BF16)],
        grid=(nb,),
        in_specs=[row, one, pl.BlockSpec((BLOCK, d), lambda i: (jnp.maximum(i - 1, 0), 0))],
        out_specs=[row, one, pl.BlockSpec((SUBLANES, LANES), lambda i: (0, 0)), row],
        compiler_params=_cparams(("arbitrary",)), name="loss_head")(h, gain.reshape(1, d), target)


def _adam_math(w, g, m, v):
    m = ADAM_B1 * m + (1.0 - ADAM_B1) * g
    v = ADAM_B2 * v + (1.0 - ADAM_B2) * (g * g)
    m_hat = m / (1.0 - ADAM_B1 ** ADAM_STEP)
    v_hat = v / (1.0 - ADAM_B2 ** ADAM_STEP)
    delta = -ADAM_LR * (m_hat / (jnp.sqrt(v_hat) + ADAM_EPS) + ADAM_WD * w)
    return delta, m, v


def _adamw_layers(w, m, v, mine, other, pos, name):
    depth, r, c = w.shape
    half = r // 2
    tr = _div_tile(half, c * 4)
    nh = half // tr

    def body(*refs):
        pos_ref, w_ref, m_ref, v_ref = refs[:4]
        mine_refs = refs[4:4 + depth]
        other_refs = refs[4 + depth:4 + 2 * depth]
        g_out, d_out, m_out, v_out = refs[4 + 2 * depth:]
        layer, i = pl.program_id(0), pl.program_id(1)
        is_mine = (i // nh) == pos_ref[0]

        def update(g):
            delta, nm, nv = _adam_math(w_ref[...], g, m_ref[...], v_ref[...])
            g_out[...] = g
            d_out[...] = delta
            m_out[...] = nm
            v_out[...] = nv

        for l in range(depth):
            @pl.when((layer == l) & is_mine)
            def _(l=l):
                update(mine_refs[l][...])

            @pl.when((layer == l) & jnp.logical_not(is_mine))
            def _(l=l):
                update(other_refs[l][...])

    stacked = pl.BlockSpec((None, tr, c), lambda l, i, p: (l, i, 0))

    def gspec(layer, is_other):
        def imap(l, i, p):
            first = jnp.where(is_other, 1 - p[0], p[0]) * nh
            here = jnp.clip(i - first, 0, nh - 1)
            return (jnp.where(l == layer, here, jnp.where(l < layer, 0, nh - 1)), 0)
        return pl.BlockSpec((tr, c), imap)

    shp = jax.ShapeDtypeStruct((depth, r, c), F32)
    grid_spec = pltpu.PrefetchScalarGridSpec(
        num_scalar_prefetch=1, grid=(depth, 2 * nh),
        in_specs=[stacked] * 3 + [gspec(l, 0) for l in range(depth)] + [gspec(l, 1) for l in range(depth)],
        out_specs=[stacked] * 4)
    return pl.pallas_call(
        body, out_shape=[shp] * 4, grid_spec=grid_spec,
        compiler_params=_cparams(("arbitrary", "arbitrary")), name=name)(pos, w, m, v, *mine, *other)


def _adamw_whole(w, g, m, v, name):
    def body(w_ref, g_ref, m_ref, v_ref, d_out, m_out, v_out):
        delta, nm, nv = _adam_math(w_ref[...], g_ref[...], m_ref[...], v_ref[...])
        d_out[...] = delta
        m_out[...] = nm
        v_out[...] = nv

    shp = jax.ShapeDtypeStruct(w.shape, F32)
    return pl.pallas_call(body, out_shape=[shp] * 3, compiler_params=_cparams(), name=name)(w, g, m, v)


def _mesh_pos():
    return lax.axis_index("x"), lax.axis_index("y"), lax.axis_index("c")


def _row_half(ref, which, lead):
    half = ref.shape[lead] // 2
    idx = (slice(None),) * lead + (pl.ds(which * half, half), slice(None))
    return ref.at[idx]


def _gather_comm(arrs, tag):
    n = len(arrs)

    def ctx(ins, outs, sems):
        send_sems, recv_sems, local_sems = sems
        x, y, c = _mesh_pos()
        chips = [(1 - x, y), (x, 1 - y), (1 - x, 1 - y)]

        def slot(k, chip, which):
            lead = len(ins[k].shape) - 2
            return _row_half(outs[k].at[2 * chip[0] + chip[1]], which, lead)

        def copy(k, j, src, dst, to):
            return pltpu.make_async_remote_copy(
                src_ref=src, dst_ref=dst, send_sem=send_sems.at[6 * k + j], recv_sem=recv_sems.at[6 * k + j],
                device_id=to, device_id_type=MESH)

        def local(k):
            return pltpu.make_async_copy(ins[k], outs[k].at[2 * x + y], local_sems.at[k])

        def first(k, j):
            lead = len(ins[k].shape) - 2
            return copy(k, j, _row_half(ins[k], c, lead), slot(k, (x, y), c), (*chips[j], c))

        def passed(k, j, which):
            return copy(k, 3 + j, slot(k, chips[j], which), slot(k, chips[j], which), (x, y, 1 - c))

        def landed(k, j):
            return copy(k, j, slot(k, chips[j], c), slot(k, chips[j], c), (x, y, 1 - c))

        return c, local, first, passed, landed

    def start(ins, outs, sems):
        c, local, first, passed, landed = ctx(ins, outs, sems)
        for k in range(n):
            local(k).start()
            for j in range(3):
                first(k, j).start()

    def mid(ins, outs, sems):
        c, local, first, passed, landed = ctx(ins, outs, sems)
        for j in range(3):
            for k in range(n):
                landed(k, j).wait_recv()
                passed(k, j, c).start()

    def finish(ins, outs, sems):
        c, local, first, passed, landed = ctx(ins, outs, sems)
        for j in range(3):
            for k in range(n):
                passed(k, j, 1 - c).wait_recv()
        for k in range(n):
            for j in range(3):
                first(k, j).wait_send()
                passed(k, j, c).wait_send()
            local(k).wait()

    return _Comm(
        tag, arrs, [jax.ShapeDtypeStruct((N_CHIPS,) + a.shape, a.dtype) for a in arrs],
        [pltpu.SemaphoreType.DMA((6 * n,)), pltpu.SemaphoreType.DMA((6 * n,)), pltpu.SemaphoreType.DMA((n,))],
        start, mid, finish)


def _run_comm(comm, name):
    n_in, n_out = len(comm.ins), len(comm.out_shapes)

    def body(*refs):
        ins, outs, sems = refs[:n_in], refs[n_in:n_in + n_out], refs[n_in + n_out:]
        comm.start(ins, outs, sems)
        if comm.mid is not None:
            comm.mid(ins, outs, sems)
        comm.finish(ins, outs, sems)

    return pl.pallas_call(
        body, out_shape=comm.out_shapes, in_specs=[HBM_SPEC] * n_in, out_specs=[HBM_SPEC] * n_out,
        scratch_shapes=comm.sems, name=name)(*comm.ins)


def _all_gather_chips(arrs, name):
    return _run_comm(_gather_comm(arrs, "gather"), name)


GATHER_US_PER_BYTE = 380.0 / 11.65e6
HOST_US = dict(ffn_up=78.0, ffn_down=65.0, in_proj=38.0, attn_fwd=103.0, ssm_fwd=67.0, merge_fwd=50.0,
               out_proj=45.0)
HOST_SLACK_US = 10.0


class _WeightStream:
    def __init__(self, pieces):
        self.keys = [k for k, _ in pieces]
        self.shards = dict(pieces)
        self.next = 0
        self.full = {}
        self.pending = []

    def comm_for(self, host):
        budget = HOST_US[host] + HOST_SLACK_US
        taken, cost = [], 0.0
        while self.next < len(self.keys):
            key = self.keys[self.next]
            c = self.shards[key].size * self.shards[key].dtype.itemsize * GATHER_US_PER_BYTE
            if cost + c > budget:
                break
            taken.append(key)
            cost += c
            self.next += 1
        self.pending = taken
        if not taken:
            return None
        return _gather_comm([self.shards[k] for k in taken], "g_" + "_".join(k[1] for k in taken))

    def deposit(self, gathered):
        for key, arr in zip(self.pending, gathered):
            self.full[key] = arr
        self.pending = []

    def get(self, key):
        if key not in self.full:
            upto = self.keys.index(key) + 1
            keys = self.keys[self.next:upto]
            self.next = upto
            for k, arr in zip(keys, _all_gather_chips([self.shards[k] for k in keys], "gather_now")):
                self.full[k] = arr
        return self.full[key]


def _all_gather_devices(x_shard, name):
    m_per, ncol = x_shard.shape

    def body(x_ref, out_ref, send_sems, recv_sems, local_sem):
        x, y, c = _mesh_pos()
        me, sibling = (x, y, c), (x, y, 1 - c)
        chips = [(1 - x, y), (x, 1 - y), (1 - x, 1 - y)]

        def rows(px, py, pc):
            return out_ref.at[4 * px + 2 * py + pc]

        def copy(k, block, to, src=None):
            return pltpu.make_async_remote_copy(
                src_ref=rows(*block) if src is None else src, dst_ref=rows(*block),
                send_sem=send_sems.at[k], recv_sem=recv_sems.at[k], device_id=to, device_id_type=MESH)

        mine = pltpu.make_async_copy(x_ref, rows(*me), local_sem)
        mine.start()
        first = [copy(0, me, sibling, src=x_ref)]
        first += [copy(1 + j, me, (*chip, c), src=x_ref) for j, chip in enumerate(chips)]
        for cp in first:
            cp.start()
        passed = [copy(4 + j, (*chip, c), sibling) for j, chip in enumerate(chips)]
        for j, chip in enumerate(chips):
            copy(1 + j, (*chip, c), me).wait_recv()
            passed[j].start()
        copy(0, sibling, me).wait_recv()
        for j, chip in enumerate(chips):
            copy(4 + j, (*chip, 1 - c), me).wait_recv()
        for cp in first + passed:
            cp.wait_send()
        mine.wait()

    return pl.pallas_call(
        body, out_shape=jax.ShapeDtypeStruct((8, m_per, ncol), x_shard.dtype),
        in_specs=[pl.BlockSpec(memory_space=pltpu.VMEM)], out_specs=pl.BlockSpec(memory_space=pltpu.VMEM),
        scratch_shapes=[pltpu.SemaphoreType.DMA((7,)), pltpu.SemaphoreType.DMA((7,)), pltpu.SemaphoreType.DMA],
        compiler_params=pltpu.CompilerParams(vmem_limit_bytes=VMEM_LIMIT), name=name)(x_shard)


def _sum_devices(g8, name):
    _, r, c = g8.shape
    tr = _div_tile(r, c * 4 * 8)

    def body(g_ref, o_ref):
        acc = g_ref[0]
        for dev in range(1, 8):
            acc = acc + g_ref[dev]
        o_ref[...] = acc

    return pl.pallas_call(
        body, out_shape=jax.ShapeDtypeStruct((r, c), F32), grid=(r // tr,),
        in_specs=[pl.BlockSpec((8, tr, c), lambda i: (0, i, 0))], out_specs=pl.BlockSpec((tr, c), lambda i: (i, 0)),
        compiler_params=_cparams(("parallel",)), name=name)(g8)


def _chip_partials(arrs, recvs, pos, name):
    n = len(arrs)

    def body(pos_ref, *refs):
        for a_ref, b_ref, o_ref in zip(refs[:n], refs[n:2 * n], refs[2 * n:]):
            o_ref[...] = (a_ref[...] + b_ref[...]).astype(BF16)

    own_specs, recv_specs, shapes = [], [], []
    for arr in arrs:
        nslab, r, c = arr.shape
        own_specs.append(pl.BlockSpec((None, r // 2, c), lambda j, p: (j, p[0], 0)))
        recv_specs.append(pl.BlockSpec((None, r // 2, c), lambda j, p: (j, 0, 0)))
        shapes.append(jax.ShapeDtypeStruct((nslab, r // 2, c), BF16))
    grid_spec = pltpu.PrefetchScalarGridSpec(
        num_scalar_prefetch=1, grid=(N_CHIPS,), in_specs=own_specs + recv_specs, out_specs=recv_specs)
    return pl.pallas_call(
        body, out_shape=shapes, grid_spec=grid_spec,
        compiler_params=_cparams(("parallel",)), name=name)(pos, *arrs, *recvs)


def _chip_exchange_comm(parts, tag):
    n = len(parts)

    def copies(ins, outs, sems):
        send_sems, recv_sems = sems
        x, y, c = _mesh_pos()
        chips = [(1 - x, y), (x, 1 - y), (1 - x, 1 - y)]
        return [pltpu.make_async_remote_copy(
            src_ref=ins[k].at[2 * chip[0] + chip[1]], dst_ref=outs[k].at[j],
            send_sem=send_sems.at[3 * k + j], recv_sem=recv_sems.at[3 * k + j],
            device_id=(*chip, c), device_id_type=MESH) for k in range(n) for j, chip in enumerate(chips)]

    def start(ins, outs, sems):
        for cp in copies(ins, outs, sems):
            cp.start()

    def finish(ins, outs, sems):
        for cp in copies(ins, outs, sems):
            cp.wait()

    return _Comm(
        tag, parts, [jax.ShapeDtypeStruct((3,) + p.shape[1:], p.dtype) for p in parts],
        [pltpu.SemaphoreType.DMA((3 * n,)), pltpu.SemaphoreType.DMA((3 * n,))], start, None, finish)


def _reduce_halves(arrs, recvs, gots, pos, name):
    n = len(arrs)

    def body(pos_ref, *refs):
        for a_ref, b_ref, g_ref, o_ref in zip(refs[:n], refs[n:2 * n], refs[2 * n:3 * n], refs[3 * n:]):
            acc = a_ref[...] + b_ref[...]
            for j in range(3):
                acc = acc + g_ref[j].astype(F32)
            o_ref[...] = acc

    own_specs, recv_specs, got_specs, out_specs, shapes = [], [], [], [], []
    for arr in arrs:
        _, r, c = arr.shape
        own_specs.append(pl.BlockSpec((None, r // 2, c), lambda i, p: (p[1], p[0], 0)))
        recv_specs.append(pl.BlockSpec((None, r // 2, c), lambda i, p: (p[1], 0, 0)))
        got_specs.append(pl.BlockSpec((3, r // 2, c), lambda i, p: (0, 0, 0)))
        out_specs.append(pl.BlockSpec((r // 2, c), lambda i, p: (0, 0)))
        shapes.append(jax.ShapeDtypeStruct((r // 2, c), F32))
    grid_spec = pltpu.PrefetchScalarGridSpec(
        num_scalar_prefetch=1, grid=(1,), in_specs=own_specs + recv_specs + got_specs, out_specs=out_specs)
    return pl.pallas_call(
        body, out_shape=shapes, grid_spec=grid_spec,
        compiler_params=_cparams(("arbitrary",)), name=name)(pos, *arrs, *recvs, *gots)


def _share_halves(halves, name):
    n = len(halves)

    def body(*refs):
        ins, outs = refs[:n], refs[n:2 * n]
        send_sems, recv_sems = refs[2 * n:]
        x, y, c = _mesh_pos()
        cps = []
        for k in range(n):
            cp = pltpu.make_async_remote_copy(
                src_ref=ins[k], dst_ref=outs[k], send_sem=send_sems.at[k], recv_sem=recv_sems.at[k],
                device_id=(x, y, 1 - c), device_id_type=MESH)
            cp.start()
            cps.append(cp)
        for cp in cps:
            cp.wait()

    return pl.pallas_call(
        body, out_shape=[jax.ShapeDtypeStruct(h.shape, h.dtype) for h in halves],
        in_specs=[HBM_SPEC] * n, out_specs=[HBM_SPEC] * n,
        scratch_shapes=[pltpu.SemaphoreType.DMA((n,)), pltpu.SemaphoreType.DMA((n,))], name=name)(*halves)


class _Reduction:
    def __init__(self, arrs, others, pos, tag):
        self.arrs, self.pos, self.tag = arrs, pos, tag
        self.recv = _share_halves(others, "rs_sibling_" + tag)
        self.parts = _chip_partials(arrs, self.recv, pos, "rs_partial_" + tag)
        self.got = None

    def comm(self):
        return _chip_exchange_comm(self.parts, "rs_" + self.tag)

    def end(self):
        if self.got is None:
            self.got = _run_comm(self.comm(), "rs_chips_" + self.tag)
        halves = _reduce_halves(self.arrs, self.recv, self.got, self.pos, "rs_reduce_" + self.tag)
        return halves, _share_halves(halves, "rs_share_" + self.tag)


def _w_in_full(p, l, ws):
    if "w_in" not in p:
        slabs = ws.get((l, "w_in"))
        p["w_in"] = jnp.concatenate([slabs[j] for j in range(N_CHIPS)], axis=1)
    return p["w_in"]


def _w3t_full(p, l, ws):
    if "w3t" not in p:
        slabs = ws.get((l, "w3"))
        p["w3t"] = jnp.swapaxes(slabs, 0, 1).reshape(slabs.shape[1], -1, slabs.shape[3])
    return p["w3t"]


def _w_out_full(l, ws):
    slabs = ws.get((l, "w_out"))
    return slabs.reshape(-1, slabs.shape[2])


def _layer_fwd(h, l, p, ws, tabs):
    def hosted(host, fn, *args):
        out, got = fn(*args, ws.comm_for(host))
        ws.deposit(got)
        return out

    ffn1_saved = hosted("ffn_up", _ffn_up, h, p["ffn1_norm"], ws.get((l, "wg1")), ws.get((l, "wu1")))
    h1 = hosted("ffn_down", _ffn_down, ffn1_saved[2], ws.get((l, "wd1")), h)
    n = _rms_fwd(h1, p["mix_norm"], "rms_fwd_mix")
    ssm_w = p["ssm_d"].shape[0]
    q, k, v, u, ga, gs = hosted("in_proj", _in_proj, n, _w_in_full(p, l, ws), tabs, ssm_w)
    o = hosted("attn_fwd", _attn_fwd, q, k, v, p["attn_sinks"])
    y, yg = hosted("ssm_fwd", _ssm_fwd, u, *p["ssm_tabs"], p["ssm_d"])
    merged = hosted("merge_fwd", _merge_fwd, o, yg, ga, gs, _w3t_full(p, l, ws))
    h2 = hosted("out_proj", _out_proj, merged, _w_out_full(l, ws), h1)
    ffn2_saved = hosted("ffn_up", _ffn_up, h2, p["ffn2_norm"], ws.get((l, "wg2")), ws.get((l, "wu2")))
    h3 = hosted("ffn_down", _ffn_down, ffn2_saved[2], ws.get((l, "wd2")), h2)
    saved = dict(h0=h, h1=h1, h2=h2, ffn1=ffn1_saved, ffn2=ffn2_saved, n_mix=n, q=q, k=k, v=v, u=u, ga=ga, gs=gs,
                 o=o, y=y, yg=yg, merged=merged)
    return h3, saved


def _layer_bwd(dh_pair, l, p, ws, s, tabs, pos):
    g = {}
    (dh2, dhb), g["ffn2_norm"], red_ffn2, _ = _ffn_bwd(
        dh_pair, s["h2"], p["ffn2_norm"], ws.get((l, "wg2")), ws.get((l, "wu2")), ws.get((l, "wd2")), p["f4"],
        s["ffn2"], pos)
    w3, w_out_w = _w3t_full(p, l, ws), _w_out_full(l, ws)
    lp, d = dh2.shape
    d4 = d // N_CHIPS
    dw_out, dw_out_other = _tn_full(s["merged"], dhb, "mix_dw_out")
    dattn, dv, dg, dga, dgs = _merge_bwd(dhb, w_out_w, s["o"], s["yg"], s["ga"], s["gs"], w3)
    (dw_ap,), (dw_ap_other,) = _tn_cols(s["o"], [dattn], "mix_dw_ap")
    (dw_gv, dw_gg), (dw_gv_other, dw_gg_other) = _tn_cols(s["yg"], [dv, dg], "mix_dw_glu")
    do, dy = _branch_bwd(dattn, dv, dg, w3, s["y"])
    (dq, dk, dvv, dkm, dvm, dsink), _ = _attn_bwd(s["q"], s["k"], s["v"], do, p["attn_sinks"], tabs)
    g["attn_sinks"] = dsink[:, 0]
    (du, dlr, dli, dbr, dbi, dcr, dci, dd), _ = _ssm_bwd(s["u"], dy, *p["ssm_tabs"], p["ssm_d"])
    ngrp = p["ssm_d"].shape[0] // SSM_GROUP
    g["ssm_lam"] = (dlr.reshape(ngrp, SSM_STATE), dli.reshape(ngrp, SSM_STATE),
                    _ssm_untable_b(dbr, ngrp), _ssm_untable_b(dbi, ngrp))
    g["ssm_c_re"] = _ssm_untable_c(dcr, ngrp)
    g["ssm_c_im"] = _ssm_untable_c(dci, ngrp)
    g["ssm_d"] = dd[0]
    dk = dk.at[:BLOCK].add(dkm)
    dvv = dvv.at[:BLOCK].add(dvm)
    dz = jnp.concatenate([dq.astype(BF16), dk.astype(BF16), dvv.astype(BF16), du.astype(BF16), dga, dgs], axis=1)
    n = s["n_mix"]
    w_in = _w_in_full(p, l, ws)
    inw = w_in.shape[1]
    dw_in, dw_in_other = _tn_full(dz, n, "mix_dw_in", d // 2)
    red_mix = _Reduction([dw_in, dw_ap, dw_gv, dw_gg, dw_out],
                         [dw_in_other, dw_ap_other, dw_gv_other, dw_gg_other, dw_out_other], pos, "mix")
    dh1, g["mix_norm"], dh1b = _in_proj_bwd(dz, w_in, dh2, s["h1"], p["mix_norm"])
    dh0_pair, g["ffn1_norm"], red_ffn1, red_mix.got = _ffn_bwd(
        (dh1, dh1b), s["h0"], p["ffn1_norm"], ws.get((l, "wg1")), ws.get((l, "wu1")), ws.get((l, "wd1")), p["f4"],
        s["ffn1"], pos, red_mix.comm())
    return dh0_pair, g, [*red_ffn1, red_mix, *red_ffn2]


BIG = ["ffn1_w_gate", "ffn1_w_up", "ffn1_w_down", "w_in", "w_attn_proj", "w_glu_v", "w_glu_g", "w_out",
       "ffn2_w_gate", "ffn2_w_up", "ffn2_w_down"]
TRANSPOSED = ["ffn1_w_gate", "ffn1_w_up", "w_in", "ffn2_w_gate", "ffn2_w_up"]
SMALL = ["ffn1_norm", "mix_norm", "attn_sinks", "ssm_a_re", "ssm_a_im", "ssm_log_dt", "ssm_b_re", "ssm_b_im",
         "ssm_c_re", "ssm_c_im", "ssm_d", "ffn2_norm", "final_norm"]
WEIGHTS = ["meta_tokens", "ffn1_norm", "ffn1_w_gate", "ffn1_w_up", "ffn1_w_down", "mix_norm", "w_in", "attn_sinks",
           "ssm_a_re", "ssm_a_im", "ssm_log_dt", "ssm_b_re", "ssm_b_im", "ssm_c_re", "ssm_c_im", "ssm_d",
           "w_attn_proj", "w_glu_v", "w_glu_g", "w_out", "ffn2_norm", "ffn2_w_gate", "ffn2_w_up", "ffn2_w_down",
           "final_norm"]


def _small_rows(shape):
    rows = -(-math.prod(shape) // LANES)
    return -(-rows // SUBLANES) * SUBLANES


def _pack_small(tree):
    parts = []
    for k in SMALL + ["meta_tokens"]:
        size, rows = math.prod(tree[k].shape), _small_rows(tree[k].shape)
        if size % LANES == 0:
            part = tree[k].reshape(size // LANES, LANES)
        else:
            part = jnp.pad(tree[k].reshape(1, size), ((0, 0), (0, LANES - size)))
        parts.append(jnp.pad(part, ((0, rows - part.shape[0]), (0, 0))))
    return jnp.concatenate(parts, axis=0)


def _unpack_small(packed, like):
    out, off = {}, 0
    for k in SMALL + ["meta_tokens"]:
        size, rows = math.prod(like[k].shape), _small_rows(like[k].shape)
        if size % LANES == 0:
            out[k] = packed[off:off + size // LANES].reshape(like[k].shape)
        else:
            out[k] = packed[off, :size].reshape(like[k].shape)
        off += rows
    return out


def kernel(x, meta_tokens, ffn1_norm, ffn1_w_gate, ffn1_w_up, ffn1_w_down, mix_norm, w_in, attn_sinks, ssm_a_re, ssm_a_im, ssm_log_dt, ssm_b_re, ssm_b_im, ssm_c_re, ssm_c_im, ssm_d, w_attn_proj, w_glu_v, w_glu_g, w_out, ffn2_norm, ffn2_w_gate, ffn2_w_up, ffn2_w_down, final_norm, loss_target, m_meta_tokens, m_ffn1_norm, m_ffn1_w_gate, m_ffn1_w_up, m_ffn1_w_down, m_mix_norm, m_w_in, m_attn_sinks, m_ssm_a_re, m_ssm_a_im, m_ssm_log_dt, m_ssm_b_re, m_ssm_b_im, m_ssm_c_re, m_ssm_c_im, m_ssm_d, m_w_attn_proj, m_w_glu_v, m_w_glu_g, m_w_out, m_ffn2_norm, m_ffn2_w_gate, m_ffn2_w_up, m_ffn2_w_down, m_final_norm, v_meta_tokens, v_ffn1_norm, v_ffn1_w_gate, v_ffn1_w_up, v_ffn1_w_down, v_mix_norm, v_w_in, v_attn_sinks, v_ssm_a_re, v_ssm_a_im, v_ssm_log_dt, v_ssm_b_re, v_ssm_b_im, v_ssm_c_re, v_ssm_c_im, v_ssm_d, v_w_attn_proj, v_w_glu_v, v_w_glu_g, v_w_out, v_ffn2_norm, v_ffn2_w_gate, v_ffn2_w_up, v_ffn2_w_down, v_final_norm):
    args = dict(locals())
    w = {k: args[k] for k in WEIGHTS}
    m = {k: args["m_" + k] for k in WEIGHTS}
    v = {k: args["v_" + k] for k in WEIGHTS}
    depth = ffn1_norm.shape[0]
    seq, d = x.shape[1], x.shape[2]
    lp = seq + BLOCK
    xi, yi, ci = _mesh_pos()
    pos = jnp.stack([ci, 2 * xi + yi]).astype(jnp.int32)

    tabs = _rope_tables(lp)
    (meta_all,) = _all_gather_chips([meta_tokens], "gather_meta")
    meta_full = jnp.concatenate([meta_all[j] for j in range(N_CHIPS)], axis=1)
    layers, pieces = [], []
    f4 = ffn1_w_gate.shape[2]
    fp = -(-f4 // MXU_DIM) * MXU_DIM

    def ffn_rows(wt):
        return jnp.pad(wt, ((0, fp - f4), (0, 0))).astype(BF16)

    for l in range(depth):
        pieces += [
            ((l, "wg1"), ffn_rows(ffn1_w_gate[l].T)), ((l, "wu1"), ffn_rows(ffn1_w_up[l].T)),
            ((l, "wd1"), ffn_rows(ffn1_w_down[l])), ((l, "w_in"), w_in[l].astype(BF16)),
            ((l, "w3"), jnp.stack([w_attn_proj[l].T, w_glu_v[l].T, w_glu_g[l].T]).astype(BF16)),
            ((l, "w_out"), w_out[l].astype(BF16)),
            ((l, "wg2"), ffn_rows(ffn2_w_gate[l].T)), ((l, "wu2"), ffn_rows(ffn2_w_up[l].T)),
            ((l, "wd2"), ffn_rows(ffn2_w_down[l]))]
        lb_re, lb_im, bb_re, bb_im = _ssm_params(ssm_a_re[l], ssm_a_im[l], ssm_log_dt[l], ssm_b_re[l], ssm_b_im[l])
        ngrp = lb_re.shape[0]
        nt = ngrp // GROUPS_PER_TILE
        ssm_tabs = (lb_re.reshape(nt, 1, TILE_STATES), lb_im.reshape(nt, 1, TILE_STATES),
                    *_ssm_tables(bb_re, bb_im, ssm_c_re[l], ssm_c_im[l]))
        layers.append(dict(
            ffn1_norm=ffn1_norm[l], mix_norm=mix_norm[l], ffn2_norm=ffn2_norm[l], attn_sinks=attn_sinks[l],
            ssm_d=ssm_d[l], ssm_tabs=ssm_tabs, f4=f4))
    ws = _WeightStream(pieces)
    ws.get((0, "wu1"))

    h = jnp.concatenate([jnp.zeros((PAD_FRONT, d), F32), meta_full, x[0]], axis=0)
    saved = []
    for l in range(depth):
        h, s = _layer_fwd(h, l, layers[l], ws, tabs)
        saved.append(s)
    dh, g_final, loss_acc, dhb = _loss_head(h, final_norm, loss_target[0])
    dh_pair = (dh, dhb)
    loss = lax.psum(loss_acc[0, 0], ("x", "y", "c"))

    grads, reds = [None] * depth, [None] * depth
    for l in reversed(range(depth)):
        dh_pair, grads[l], reds[l] = _layer_bwd(dh_pair, l, layers[l], ws, saved[l], tabs, pos)
    dh = dh_pair[0]
    grad_x = dh[BLOCK:][None]
    dmeta_local = dh[PAD_FRONT:BLOCK]

    small = {k: [] for k in SMALL}
    for l in range(depth):
        gl = grads[l]
        _, vjp = jax.vjp(_ssm_params, ssm_a_re[l], ssm_a_im[l], ssm_log_dt[l], ssm_b_re[l], ssm_b_im[l])
        da_re, da_im, dlog_dt, db_re, db_im = vjp(gl["ssm_lam"])
        for k, val in (("ffn1_norm", gl["ffn1_norm"][0]), ("mix_norm", gl["mix_norm"][0]),
                       ("attn_sinks", gl["attn_sinks"]), ("ssm_a_re", da_re), ("ssm_a_im", da_im),
                       ("ssm_log_dt", dlog_dt), ("ssm_b_re", db_re), ("ssm_b_im", db_im),
                       ("ssm_c_re", gl["ssm_c_re"]), ("ssm_c_im", gl["ssm_c_im"]), ("ssm_d", gl["ssm_d"]),
                       ("ffn2_norm", gl["ffn2_norm"][0])):
            small[k].append(val)
    small_local = {k: jnp.stack(vals) for k, vals in small.items() if k != "final_norm"}
    small_local["final_norm"] = g_final[0]
    small_local["meta_tokens"] = dmeta_local
    like = dict(small_local)
    g_small = _sum_devices(_all_gather_devices(_pack_small(small_local), "gather_small_grads"), "sum_small_grads")
    g_small_tree = _unpack_small(g_small, like)
    d4 = d // N_CHIPS
    chip = 2 * xi + yi
    g_meta = lax.dynamic_slice_in_dim(g_small_tree["meta_tokens"], chip * d4, d4, axis=1)

    reduced = []
    for l in range(depth):
        mine, other = [], []
        for red in reds[l]:
            halves, sibling_halves = red.end()
            mine += halves
            other += sibling_halves
        reduced.append((mine, other))

    g_out, delta, new_m, new_v = {}, {}, {}, {}
    for i, k in enumerate(BIG):
        flip = (lambda t: jnp.swapaxes(t, 1, 2)) if k in TRANSPOSED else (lambda t: t)
        outs = _adamw_layers(
            flip(w[k]), flip(m[k]), flip(v[k]), [reduced[l][0][i] for l in range(depth)],
            [reduced[l][1][i] for l in range(depth)], pos, "adamw_" + k)
        g_out[k], delta[k], new_m[k], new_v[k] = [flip(t) for t in outs]
    g_small_tree["meta_tokens"] = g_meta
    for k in SMALL + ["meta_tokens"]:
        shape = w[k].shape if w[k].ndim > 1 else (1,) + w[k].shape
        outs = _adamw_whole(w[k].reshape(shape), g_small_tree[k].reshape(shape), m[k].reshape(shape),
                            v[k].reshape(shape), "adamw_" + k)
        g_out[k] = g_small_tree[k]
        delta[k], new_m[k], new_v[k] = [t.reshape(w[k].shape) for t in outs]

    return (loss, grad_x, *[g_out[k] for k in WEIGHTS], *[delta[k] for k in WEIGHTS],
            *[new_m[k] for k in WEIGHTS], *[new_v[k] for k in WEIGHTS])
```

```python
import functools
import math

import jax
import jax.numpy as jnp
from jax import lax
from jax.experimental import pallas as pl
from jax.experimental.pallas import tpu as pltpu

F32 = jnp.float32
BF16 = jnp.bfloat16

N_META = 16
HEAD_DIM = 64
N_Q_HEADS = 8
N_KV_HEADS = 2
Q_PER_KV = N_Q_HEADS // N_KV_HEADS
ATTN_WIDTH = N_Q_HEADS * HEAD_DIM
KV_WIDTH = N_KV_HEADS * HEAD_DIM
BLOCK = 128
PAD_FRONT = BLOCK - N_META
ROPE_THETA = 500000.0
ROT_DIM = HEAD_DIM // 4
SSM_GROUP = 16
SSM_STATE = 64
GROUPS_PER_TILE = 4
TILE_STATES = GROUPS_PER_TILE * SSM_STATE
LANES = 128
SUBLANES = 8
MXU_DIM = 256
EPS = 1e-6
NEG_INF = -1e30
N_CHIPS = 4

ADAM_LR = 0.001
ADAM_B1 = 0.9
ADAM_B2 = 0.999
ADAM_EPS = 1e-08
ADAM_WD = 0.01
ADAM_STEP = 10

VMEM_LIMIT = 56 * 1024 * 1024
MESH = pl.DeviceIdType.MESH


def _cparams(sem=None):
    return pltpu.CompilerParams(dimension_semantics=sem, vmem_limit_bytes=VMEM_LIMIT)


def _row_tile(rows, limit=512):
    best = None
    for t in range(128, limit + 1, 128):
        if rows % t == 0:
            best = t
    assert best is not None, rows
    return best


def _div_tile(rows, row_bytes, max_bytes=1 << 20, mult=8):
    best = None
    for t in range(mult, rows + 1, mult):
        if rows % t == 0 and t * row_bytes <= max_bytes:
            best = t
    if best is None:
        best = rows
    return best


def _dot(a, b, mode):
    if mode == "nn":
        dims = (((1,), (0,)), ((), ()))
    elif mode == "nt":
        dims = (((1,), (1,)), ((), ()))
    else:
        dims = (((0,), (0,)), ((), ()))
    return lax.dot_general(a.astype(BF16), b.astype(BF16), dims, preferred_element_type=F32)


def _sigmoid(x):
    return 1.0 / (1.0 + jnp.exp(-x))


_GELU_C = math.sqrt(2.0 / math.pi)


def _gelu(x):
    return 0.5 * x * (1.0 + jnp.tanh(_GELU_C * (x + 0.044715 * x * x * x)))


def _gelu_grad(x):
    t = jnp.tanh(_GELU_C * (x + 0.044715 * x * x * x))
    return 0.5 * (1.0 + t) + 0.5 * x * (1.0 - t * t) * _GELU_C * (1.0 + 3.0 * 0.044715 * x * x)


class _Comm:
    def __init__(self, tag, ins, out_shapes, sems, start, mid, finish):
        self.tag, self.ins, self.out_shapes, self.sems = tag, list(ins), list(out_shapes), list(sems)
        self.start, self.mid, self.finish = start, mid, finish


HBM_SPEC = pl.BlockSpec(memory_space=pltpu.HBM)


def _hosted_call(body, comm, *, out_shape, grid, in_specs, out_specs, scratch_shapes, sem, name, args):
    out_shape, in_specs, out_specs = list(out_shape), list(in_specs), list(out_specs)
    scratch_shapes = list(scratch_shapes)
    if comm is None:
        res = pl.pallas_call(
            body, out_shape=out_shape, grid=grid, in_specs=in_specs, out_specs=out_specs,
            scratch_shapes=scratch_shapes, compiler_params=_cparams(sem), name=name)(*args)
        return list(res), []
    n_in, n_out, n_sc = len(args), len(out_shape), len(scratch_shapes)
    nci, nco = len(comm.ins), len(comm.out_shapes)
    total = math.prod(grid)

    def wrapped(*refs):
        in_refs, cin = refs[:n_in], refs[n_in:n_in + nci]
        o0 = n_in + nci
        out_refs, cout = refs[o0:o0 + n_out], refs[o0 + n_out:o0 + n_out + nco]
        s0 = o0 + n_out + nco
        sc, csem = refs[s0:s0 + n_sc], refs[s0 + n_sc:]
        lin = 0
        for dim, size in enumerate(grid):
            lin = lin * size + pl.program_id(dim)

        @pl.when(lin == 0)
        def _():
            comm.start(cin, cout, csem)

        if comm.mid is not None:
            @pl.when(lin == total // 2)
            def _():
                comm.mid(cin, cout, csem)

        body(*in_refs, *out_refs, *sc)

        @pl.when(lin == total - 1)
        def _():
            comm.finish(cin, cout, csem)

    res = pl.pallas_call(
        wrapped, out_shape=out_shape + comm.out_shapes, grid=grid,
        in_specs=in_specs + [HBM_SPEC] * nci, out_specs=out_specs + [HBM_SPEC] * nco,
        scratch_shapes=scratch_shapes + comm.sems,
        compiler_params=_cparams(("arbitrary",) * len(grid)), name=name + "_" + comm.tag)(*args, *comm.ins)
    return list(res[:n_out]), list(res[n_out:])


def _matmul(name, grid, k_axis, ins, in_specs, pairs, acc_shapes, epilogue, out_shapes, out_specs, sem, comm=None):
    n_in, n_out, n_acc = len(ins), len(out_shapes), len(acc_shapes)

    def body(*refs):
        in_refs = refs[:n_in]
        out_refs = refs[n_in:n_in + n_out]
        acc_refs = refs[n_in + n_out:]
        if k_axis is None:
            accs = [None] * n_acc
            for ia, ib, mode, iacc in pairs:
                d = _dot(in_refs[ia][...], in_refs[ib][...], mode)
                accs[iacc] = d if accs[iacc] is None else accs[iacc] + d
            epilogue(accs, in_refs, out_refs)
            return
        k = pl.program_id(k_axis)

        @pl.when(k == 0)
        def _():
            for r in acc_refs:
                r[...] = jnp.zeros_like(r)

        for ia, ib, mode, iacc in pairs:
            acc_refs[iacc][...] += _dot(in_refs[ia][...], in_refs[ib][...], mode)

        @pl.when(k == pl.num_programs(k_axis) - 1)
        def _():
            epilogue([r[...] for r in acc_refs], in_refs, out_refs)

    scratch = [] if k_axis is None else [pltpu.VMEM(s, F32) for s in acc_shapes]
    outs, couts = _hosted_call(
        body, comm, out_shape=out_shapes, grid=grid, in_specs=in_specs, out_specs=out_specs,
        scratch_shapes=scratch, sem=sem, name=name, args=ins)
    return outs if comm is None else (outs, couts)


def _rms_math(x, g):
    r = lax.rsqrt(jnp.mean(x * x, axis=-1, keepdims=True) + EPS)
    return (x * r * g).astype(BF16)


def _rms_fwd(h, g, name):
    lp, d = h.shape
    tm = _row_tile(lp)

    def body(h_ref, g_ref, n_ref):
        n_ref[...] = _rms_math(h_ref[...], g_ref[...])

    return pl.pallas_call(
        body, out_shape=jax.ShapeDtypeStruct((lp, d), BF16), grid=(lp // tm,),
        in_specs=[pl.BlockSpec((tm, d), lambda i: (i, 0)), pl.BlockSpec((1, d), lambda i: (0, 0))],
        out_specs=pl.BlockSpec((tm, d), lambda i: (i, 0)),
        compiler_params=_cparams(("parallel",)), name=name)(h, g.reshape(1, d))


def _rms_bwd_math(dn, x, g):
    r = lax.rsqrt(jnp.mean(x * x, axis=-1, keepdims=True) + EPS)
    xh = x * r
    dxh = dn * g
    dx = r * (dxh - xh * jnp.mean(dxh * xh, axis=-1, keepdims=True))
    return dx, dn * xh


def _ffn_up(n, wgt, wut, comm=None):
    lp, d = n.shape
    fp = wgt.shape[1]
    tm = _row_tile(lp)

    def up_body(n_ref, wg_ref, wu_ref, a_ref, b_ref, s_ref):
        x = n_ref[...]
        for jc in range(N_CHIPS):
            cols = slice(jc * fp, (jc + 1) * fp)
            a = _dot(x, wg_ref[jc], "nt")
            b = _dot(x, wu_ref[jc], "nt")
            a_ref[:, cols] = a.astype(BF16)
            b_ref[:, cols] = b.astype(BF16)
            s_ref[:, cols] = (a * _sigmoid(a) * b).astype(BF16)

    ff = N_CHIPS * fp
    act = jax.ShapeDtypeStruct((lp, ff), BF16)
    act_tile = pl.BlockSpec((tm, ff), lambda i: (i, 0))
    w_spec = pl.BlockSpec((N_CHIPS, fp, d), lambda i: (0, 0, 0))
    outs, couts = _hosted_call(
        up_body, comm, out_shape=[act, act, act], grid=(lp // tm,),
        in_specs=[pl.BlockSpec((tm, d), lambda i: (i, 0)), w_spec, w_spec],
        out_specs=[act_tile] * 3, scratch_shapes=[], sem=("parallel",), name="ffn_up", args=(n, wgt, wut))
    return (*outs, n), couts


def _residual_outputs(h_new, in_refs, out_refs, gain_at):
    out_refs[0][...] = h_new
    if gain_at is not None:
        out_refs[1][...] = _rms_math(h_new, in_refs[gain_at][...])


def _residual_specs(lp, d, tm, next_gain):
    row = pl.BlockSpec((tm, d), lambda i: (i, 0))
    shapes, specs = [jax.ShapeDtypeStruct((lp, d), F32)], [row]
    extra_in, extra_specs = [], []
    if next_gain is not None:
        shapes.append(jax.ShapeDtypeStruct((lp, d), BF16))
        specs.append(row)
        extra_in, extra_specs = [next_gain.reshape(1, d)], [pl.BlockSpec((1, d), lambda i: (0, 0))]
    return shapes, specs, extra_in, extra_specs


def _ffn_down(s, wd, h, next_gain, comm=None):
    lp, d = h.shape
    ff = s.shape[1]
    tm = _row_tile(lp)
    shapes, specs, extra_in, extra_specs = _residual_specs(lp, d, tm, next_gain)

    def down_epi(accs, in_refs, out_refs):
        _residual_outputs(in_refs[2][...] + 0.5 * accs[0], in_refs, out_refs, 3 if extra_in else None)

    res = _matmul(
        "ffn_down", (lp // tm,), None, [s, wd.reshape(ff, d), h] + extra_in,
        [pl.BlockSpec((tm, ff), lambda i: (i, 0)), pl.BlockSpec((ff, d), lambda i: (0, 0)),
         pl.BlockSpec((tm, d), lambda i: (i, 0))] + extra_specs,
        [(0, 1, "nn", 0)], [(tm, d)], down_epi, shapes, specs, ("parallel",), comm)
    return (res, []) if comm is None else res


def _tn_tiles(lp):
    return _row_tile(lp, 1408)


def _ffn_bwd(dh_pair, h_in, gain, wgt, wut, wd, f4, saved, pos, comm=None):
    dh, dhb = dh_pair
    a, b, s, n = saved
    lp, d = h_in.shape
    fp = wgt.shape[1]
    ff = N_CHIPS * fp
    tm = _row_tile(lp)
    ni = lp // tm
    tk = _tn_tiles(lp)
    nk = lp // tk

    def ds_body(dh_ref, wd_ref, a_ref, b_ref, da_ref, db_ref):
        x = dh_ref[...]
        for jc in range(N_CHIPS):
            cols = slice(jc * fp, (jc + 1) * fp)
            ds = 0.5 * _dot(x, wd_ref[jc], "nt")
            av = a_ref[:, cols].astype(F32)
            bv = b_ref[:, cols].astype(F32)
            sg = _sigmoid(av)
            da_ref[:, cols] = (ds * bv * sg * (1.0 + av * (1.0 - sg))).astype(BF16)
            db_ref[:, cols] = (ds * av * sg).astype(BF16)

    act = jax.ShapeDtypeStruct((lp, ff), BF16)
    act_tile = pl.BlockSpec((tm, ff), lambda i: (i, 0))
    (da, db), couts = _hosted_call(
        ds_body, comm, out_shape=[act, act], grid=(ni,),
        in_specs=[pl.BlockSpec((tm, d), lambda i: (i, 0)), pl.BlockSpec((N_CHIPS, fp, d), lambda i: (0, 0, 0)),
                  act_tile, act_tile],
        out_specs=[act_tile, act_tile], scratch_shapes=[], sem=("parallel",), name="ffn_bwd_ds",
        args=(dhb, wd, a, b))

    dw_shape = jax.ShapeDtypeStruct((N_CHIPS, f4, d), F32)
    dw_spec = pl.BlockSpec((None, f4, d), lambda j, k: (j, 0, 0))
    in_col = pl.BlockSpec((tk, fp), lambda j, k: (k, j))
    in_row = pl.BlockSpec((tk, d), lambda j, k: (k, 0))

    half_shape = jax.ShapeDtypeStruct((N_CHIPS, f4 // 2, d), BF16)
    half_spec = pl.BlockSpec((None, f4 // 2, d), lambda j, k: (j, 0, 0))

    def dwd_epi(accs, in_refs, out_refs):
        dw = 0.5 * accs[0]
        out_refs[0][...] = dw[:f4]
        out_refs[1][...] = _sibling_half(dw, f4)

    dwd, dwd_other = _matmul(
        "ffn_dwd", (N_CHIPS, nk), 1, [s, dhb], [in_col, in_row],
        [(0, 1, "tn", 0)], [(fp, d)], dwd_epi, [dw_shape, half_shape], [dw_spec, half_spec],
        ("arbitrary", "arbitrary"))

    def dwgu_epi(accs, in_refs, out_refs):
        for i, acc in enumerate(accs):
            out_refs[i][...] = acc[:f4]
            out_refs[2 + i][...] = _sibling_half(acc, f4)

    red_down = _Reduction([dwd], [dwd_other], pos, "ffn_d")
    (dwg, dwu, dwg_other, dwu_other), red_down.got = _matmul(
        "ffn_dwgu", (N_CHIPS, nk), 1, [n, da, db], [in_row, in_col, in_col],
        [(1, 0, "tn", 0), (2, 0, "tn", 1)], [(fp, d)] * 2, dwgu_epi,
        [dw_shape, dw_shape, half_shape, half_shape], [dw_spec, dw_spec, half_spec, half_spec],
        ("arbitrary", "arbitrary"), red_down.comm())

    def dn_epi(accs, in_refs, out_refs):
        i = pl.program_id(0)
        dx, dgrow = _rms_bwd_math(accs[0], in_refs[5][...], in_refs[6][...])
        dh_new = in_refs[4][...] + dx
        out_refs[0][...] = dh_new
        out_refs[2][...] = dh_new.astype(BF16)

        @pl.when(i == 0)
        def _():
            out_refs[1][...] = jnp.zeros_like(out_refs[1])

        out_refs[1][...] += jnp.sum(dgrow, axis=0, keepdims=True)

    red = _Reduction([dwg, dwu], [dwg_other, dwu_other], pos, "ffn_gu")
    row_spec = pl.BlockSpec((tm, d), lambda i: (i, 0))
    act_spec = pl.BlockSpec((tm, ff), lambda i: (i, 0))
    w_spec = pl.BlockSpec((ff, d), lambda i: (0, 0))
    one_spec = pl.BlockSpec((1, d), lambda i: (0, 0))
    (dh_in, dgain, dh_in_b), red.got = _matmul(
        "ffn_bwd_dn", (ni,), None, [da, wgt.reshape(ff, d), db, wut.reshape(ff, d), dh, h_in, gain.reshape(1, d)],
        [act_spec, w_spec, act_spec, w_spec, row_spec, row_spec, one_spec],
        [(0, 1, "nn", 0), (2, 3, "nn", 0)], [(tm, d)], dn_epi,
        [jax.ShapeDtypeStruct((lp, d), F32), jax.ShapeDtypeStruct((1, d), F32), jax.ShapeDtypeStruct((lp, d), BF16)],
        [row_spec, one_spec, row_spec], ("arbitrary",), red.comm())
    return (dh_in, dh_in_b), dgain, [red, red_down], couts


def _rope_tables(lp):
    pos = jnp.arange(lp, dtype=F32) - float(PAD_FRONT)
    inv_freq = ROPE_THETA ** (-jnp.arange(0, ROT_DIM, 2, dtype=F32) / ROT_DIM)
    ang = pos[:, None] * inv_freq[None, :]
    cos, sin = jnp.cos(ang), jnp.sin(ang)
    half = ROT_DIM // 2
    ones = jnp.ones((lp, HEAD_DIM - ROT_DIM), F32)
    zeros_h = jnp.zeros((lp, half), F32)
    zeros_r = jnp.zeros((lp, HEAD_DIM - ROT_DIM), F32)
    c = jnp.concatenate([cos, cos, ones], axis=1)
    s1 = jnp.concatenate([-sin, zeros_h, zeros_r], axis=1)
    s2 = jnp.concatenate([zeros_h, sin, zeros_r], axis=1)
    reps = LANES // HEAD_DIM
    return jnp.stack([jnp.tile(c, (1, reps)), jnp.tile(s1, (1, reps)), jnp.tile(s2, (1, reps))])


def _rope(x, c, s1, s2):
    half = ROT_DIM // 2
    outs = []
    for ch in range(x.shape[1] // LANES):
        xc = x[:, ch * LANES:(ch + 1) * LANES]
        outs.append(xc * c + pltpu.roll(xc, LANES - half, 1) * s1 + pltpu.roll(xc, half, 1) * s2)
    return outs[0] if len(outs) == 1 else jnp.concatenate(outs, axis=1)


def _rope_t(dy, c, s1, s2):
    half = ROT_DIM // 2
    outs = []
    for ch in range(dy.shape[1] // LANES):
        dc = dy[:, ch * LANES:(ch + 1) * LANES]
        outs.append(dc * c + pltpu.roll(dc * s1, half, 1) + pltpu.roll(dc * s2, LANES - half, 1))
    return outs[0] if len(outs) == 1 else jnp.concatenate(outs, axis=1)


def _in_proj(n, w_in, tabs, ssm_w, comm=None):
    lp, d = n.shape
    inw = w_in.shape[0]
    tm = _row_tile(lp)
    o1 = ATTN_WIDTH
    o2 = o1 + KV_WIDTH
    o3 = o2 + KV_WIDTH
    o4 = o3 + ssm_w
    o5 = o4 + d

    def epi(accs, in_refs, out_refs):
        z = accs[0]
        c, s1, s2 = in_refs[2][0], in_refs[2][1], in_refs[2][2]
        out_refs[0][...] = _rope(z[:, :o1], c, s1, s2).astype(BF16)
        out_refs[1][...] = _rope(z[:, o1:o2], c, s1, s2).astype(BF16)
        out_refs[2][...] = z[:, o2:o3].astype(BF16)
        out_refs[3][...] = z[:, o3:o4]
        out_refs[4][...] = z[:, o4:o5]
        out_refs[5][...] = z[:, o5:]

    def rs(w, dt):
        return jax.ShapeDtypeStruct((lp, w), dt), pl.BlockSpec((tm, w), lambda i: (i, 0))

    shapes, specs = zip(rs(o1, BF16), rs(KV_WIDTH, BF16), rs(KV_WIDTH, BF16), rs(ssm_w, F32), rs(d, F32), rs(d, F32))
    res = _matmul(
        "mix_in_proj", (lp // tm,), None, [n, w_in, tabs],
        [pl.BlockSpec((tm, d), lambda i: (i, 0)), pl.BlockSpec((inw, d), lambda i: (0, 0)),
         pl.BlockSpec((3, tm, LANES), lambda i: (0, i, 0))],
        [(0, 1, "nt", 0)], [(tm, inw)], epi, list(shapes), list(specs), ("parallel",), comm)
    return (res, []) if comm is None else res


def _attn_mask(b):
    rows = lax.broadcasted_iota(jnp.int32, (BLOCK, 3 * BLOCK), 0)
    cols = lax.broadcasted_iota(jnp.int32, (BLOCK, 3 * BLOCK), 1)
    qpos = b * BLOCK + rows - PAD_FRONT
    kpos = (b - 1) * BLOCK + cols - PAD_FRONT
    dist = qpos - kpos
    band = (cols < 2 * BLOCK) & (kpos >= N_META) & (dist >= 0) & (dist < BLOCK)
    mrow = cols - 2 * BLOCK
    meta = (mrow >= PAD_FRONT) & ((mrow - PAD_FRONT) <= qpos)
    return band | meta


def _attn_probs(qh, kk, mask, sink):
    s = _dot(qh, kk, "nt") * (HEAD_DIM ** -0.5)
    s = jnp.where(mask, s, NEG_INF)
    m = jnp.maximum(jnp.max(s, axis=-1, keepdims=True), sink)
    e = jnp.exp(s - m)
    es = jnp.exp(sink - m)
    z = jnp.sum(e, axis=-1, keepdims=True) + es
    inv = 1.0 / z
    return e * inv, es * inv


def _head(ref_or_val, h):
    return ref_or_val[:, h * HEAD_DIM:(h + 1) * HEAD_DIM]


def _attn_fwd(q, k, v, sinks, comm=None):
    lp = q.shape[0]
    nb = lp // BLOCK

    def body(sink_ref, q_ref, kp_ref, kc_ref, km_ref, vp_ref, vc_ref, vm_ref, o_ref):
        b = pl.program_id(0)
        mask = _attn_mask(b)
        for hk in range(N_KV_HEADS):
            kk = jnp.concatenate([_head(kp_ref, hk), _head(kc_ref, hk), _head(km_ref, hk)], axis=0)
            vv = jnp.concatenate([_head(vp_ref, hk), _head(vc_ref, hk), _head(vm_ref, hk)], axis=0)
            for g in range(Q_PER_KV):
                h = hk * Q_PER_KV + g
                p, _ = _attn_probs(_head(q_ref, h), kk, mask, sink_ref[h])
                o_ref[:, h * HEAD_DIM:(h + 1) * HEAD_DIM] = _dot(p, vv, "nn").astype(BF16)

    cur = lambda b: (b, 0)
    prev = lambda b: (jnp.maximum(b - 1, 0), 0)
    first = lambda b: (0, 0)
    kvs = lambda f: pl.BlockSpec((BLOCK, KV_WIDTH), f)
    (o,), couts = _hosted_call(
        body, comm, out_shape=[jax.ShapeDtypeStruct((lp, ATTN_WIDTH), BF16)], grid=(nb,),
        in_specs=[pl.BlockSpec(memory_space=pltpu.SMEM), pl.BlockSpec((BLOCK, ATTN_WIDTH), cur),
                  kvs(prev), kvs(cur), kvs(first), kvs(prev), kvs(cur), kvs(first)],
        out_specs=[pl.BlockSpec((BLOCK, ATTN_WIDTH), cur)], scratch_shapes=[],
        sem=("parallel",), name="attn_fwd", args=(sinks, q, k, k, k, v, v, v))
    return o, couts


def _attn_bwd(q, k, v, do, sinks, tabs, comm=None):
    lp = q.shape[0]
    nb = lp // BLOCK
    scale = HEAD_DIM ** -0.5

    def body(sink_ref, q_ref, do_ref, kp_ref, kc_ref, km_ref, vp_ref, vc_ref, vm_ref, tq_ref, tk_ref, t0_ref,
             dq_ref, dk_ref, dv_ref, dkm_ref, dvm_ref, dsink_ref,
             dq_s, dkk_s, dvv_s, ck_s, cv_s, mk_s, mv_s):
        b = pl.program_id(0)

        @pl.when(b == 0)
        def _():
            for r in (ck_s, cv_s, mk_s, mv_s, dsink_ref):
                r[...] = jnp.zeros_like(r)

        @pl.when(b < nb)
        def _():
            mask = _attn_mask(b)
            for hk in range(N_KV_HEADS):
                kk = jnp.concatenate([_head(kp_ref, hk), _head(kc_ref, hk), _head(km_ref, hk)], axis=0)
                vv = jnp.concatenate([_head(vp_ref, hk), _head(vc_ref, hk), _head(vm_ref, hk)], axis=0)
                dkk = jnp.zeros((3 * BLOCK, HEAD_DIM), F32)
                dvv = jnp.zeros((3 * BLOCK, HEAD_DIM), F32)
                for g in range(Q_PER_KV):
                    h = hk * Q_PER_KV + g
                    qh = _head(q_ref, h)
                    doh = _head(do_ref, h)
                    p, ps = _attn_probs(qh, kk, mask, sink_ref[h])
                    dp = _dot(doh, vv, "nt")
                    delta = jnp.sum(p * dp, axis=-1, keepdims=True)
                    ds = (p * (dp - delta)).astype(BF16)
                    dsink_ref[h:h + 1, :] += jnp.zeros((1, LANES), F32) - jnp.sum(ps * delta)
                    dq_s[:, h * HEAD_DIM:(h + 1) * HEAD_DIM] = _dot(ds, kk, "nn") * scale
                    dkk = dkk + _dot(ds, qh, "tn") * scale
                    dvv = dvv + _dot(p, doh, "tn")
                dkk_s[:, hk * HEAD_DIM:(hk + 1) * HEAD_DIM] = dkk
                dvv_s[:, hk * HEAD_DIM:(hk + 1) * HEAD_DIM] = dvv
            dq_ref[...] = _rope_t(dq_s[...], tq_ref[0], tq_ref[1], tq_ref[2])
            dk_ref[...] = _rope_t(ck_s[...] + dkk_s[0:BLOCK, :], tk_ref[0], tk_ref[1], tk_ref[2])
            dv_ref[...] = cv_s[...] + dvv_s[0:BLOCK, :]
            ck_s[...] = dkk_s[BLOCK:2 * BLOCK, :]
            cv_s[...] = dvv_s[BLOCK:2 * BLOCK, :]
            mk_s[...] += dkk_s[2 * BLOCK:, :]
            mv_s[...] += dvv_s[2 * BLOCK:, :]

        @pl.when(b == nb)
        def _():
            dk_ref[...] = _rope_t(ck_s[...], tk_ref[0], tk_ref[1], tk_ref[2])
            dv_ref[...] = cv_s[...]
            dkm_ref[...] = _rope_t(mk_s[...], t0_ref[0], t0_ref[1], t0_ref[2])
            dvm_ref[...] = mv_s[...]

    cur = lambda b: (jnp.minimum(b, nb - 1), 0)
    prev = lambda b: (jnp.clip(b - 1, 0, nb - 1), 0)
    first = lambda b: (0, 0)
    kvs = lambda f: pl.BlockSpec((BLOCK, KV_WIDTH), f)
    tab = lambda f: pl.BlockSpec((3, BLOCK, LANES), lambda b: (0,) + f(b)[:1] + (0,))
    kv_out = lambda b: (jnp.maximum(b - 1, 0), 0)
    return _hosted_call(
        body, comm,
        out_shape=[jax.ShapeDtypeStruct((lp, ATTN_WIDTH), F32), jax.ShapeDtypeStruct((lp, KV_WIDTH), F32),
                   jax.ShapeDtypeStruct((lp, KV_WIDTH), F32), jax.ShapeDtypeStruct((BLOCK, KV_WIDTH), F32),
                   jax.ShapeDtypeStruct((BLOCK, KV_WIDTH), F32), jax.ShapeDtypeStruct((N_Q_HEADS, LANES), F32)],
        grid=(nb + 1,),
        in_specs=[pl.BlockSpec(memory_space=pltpu.SMEM), pl.BlockSpec((BLOCK, ATTN_WIDTH), cur),
                  pl.BlockSpec((BLOCK, ATTN_WIDTH), cur),
                  kvs(prev), kvs(cur), kvs(first), kvs(prev), kvs(cur), kvs(first),
                  tab(cur), tab(kv_out), tab(first)],
        out_specs=[pl.BlockSpec((BLOCK, ATTN_WIDTH), cur), kvs(kv_out), kvs(kv_out), kvs(first), kvs(first),
                   pl.BlockSpec((N_Q_HEADS, LANES), first)],
        scratch_shapes=[pltpu.VMEM((BLOCK, ATTN_WIDTH), F32), pltpu.VMEM((3 * BLOCK, KV_WIDTH), F32),
                        pltpu.VMEM((3 * BLOCK, KV_WIDTH), F32), pltpu.VMEM((BLOCK, KV_WIDTH), F32),
                        pltpu.VMEM((BLOCK, KV_WIDTH), F32), pltpu.VMEM((BLOCK, KV_WIDTH), F32),
                        pltpu.VMEM((BLOCK, KV_WIDTH), F32)],
        sem=("arbitrary",), name="attn_bwd", args=(sinks, q, do, k, k, k, v, v, v, tabs, tabs, tabs))


def _cmul(ar, ai, br, bi):
    return ar * br - ai * bi, ar * bi + ai * br


def _cpow(lr, li, n):
    rr = ri = None
    br, bi = lr, li
    while n:
        if n & 1:
            rr, ri = (br, bi) if rr is None else _cmul(rr, ri, br, bi)
        n >>= 1
        if n:
            br, bi = _cmul(br, bi, br, bi)
    return rr, ri


def _shift_rows(x, d, reverse):
    rows = lax.broadcasted_iota(jnp.int32, x.shape, 0)
    if not reverse:
        return jnp.where(rows >= d, pltpu.roll(x, d, 0), 0.0)
    return jnp.where(rows < SUBLANES - d, pltpu.roll(x, SUBLANES - d, 0), 0.0)


def _sublane_powers(mr, mi, reverse):
    rows = lax.broadcasted_iota(jnp.int32, mr.shape, 0)
    e = SUBLANES - 1 - rows if reverse else rows
    pr, pi = jnp.ones_like(mr), jnp.zeros_like(mr)
    br, bi = mr, mi
    for d in (1, 2, 4):
        tr, ti = _cmul(pr, pi, br, bi)
        on = (e & d) != 0
        pr, pi = jnp.where(on, tr, pr), jnp.where(on, ti, pi)
        if d < 4:
            br, bi = _cmul(br, bi, br, bi)
    return pr, pi


def _inclusive_prefix(er, ei, mr, mi, reverse):
    ir, ii, pr, pi = er, ei, mr, mi
    for d in (1, 2, 4):
        tr, ti = _cmul(pr, pi, _shift_rows(ir, d, reverse), _shift_rows(ii, d, reverse))
        ir, ii = ir + tr, ii + ti
        if d < 4:
            pr, pi = _cmul(pr, pi, pr, pi)
    return ir, ii


def _chain_rows(a, t, seg):
    return pl.ds(a * SUBLANES * seg + t, SUBLANES, stride=seg)


def _seg_scan(xr_ref, xi_ref, lam, seg, nchain, reverse, store, init, extra=None):
    nt = len(lam)
    acc0 = () if extra is None else extra[1]

    def step(i, carry):
        hs, acc = carry
        t = seg - 1 - i if reverse else i
        out = []
        for a in range(nchain):
            sl = _chain_rows(a, t, seg)
            for j in range(nt):
                lr, li = lam[j]
                k = 2 * (a * nt + j)
                hr, hi = hs[k], hs[k + 1]
                nr = lr * hr - li * hi + xr_ref[j, sl, :]
                ni = lr * hi + li * hr + xi_ref[j, sl, :]
                if store:
                    xr_ref[j, sl, :] = nr
                    xi_ref[j, sl, :] = ni
                if extra is not None:
                    acc = extra[0](t, a, j, nr, ni, acc)
                out += [nr, ni]
        return tuple(out), acc

    return lax.fori_loop(0, seg, step, (tuple(init), acc0))


def _ssm_scan(xr_ref, xi_ref, lam, seg, nchain, reverse, extra=None):
    nt = len(lam)
    zero = [jnp.zeros((SUBLANES, LANES), F32)] * (2 * nt * nchain)
    ends, _ = _seg_scan(xr_ref, xi_ref, lam, seg, nchain, reverse, False, zero)
    init = [None] * (2 * nt * nchain)
    last = 0 if reverse else SUBLANES - 1
    for j in range(nt):
        mr, mi = _cpow(lam[j][0], lam[j][1], seg)
        m8r, m8i = _cpow(mr, mi, SUBLANES)
        pwr, pwi = _sublane_powers(mr, mi, reverse)
        gr = gi = jnp.zeros((SUBLANES, LANES), F32)
        for a in (reversed(range(nchain)) if reverse else range(nchain)):
            k = 2 * (a * nt + j)
            incr, inci = _inclusive_prefix(ends[k], ends[k + 1], mr, mi, reverse)
            tr, ti = _cmul(pwr, pwi, gr, gi)
            init[k] = _shift_rows(incr, 1, reverse) + tr
            init[k + 1] = _shift_rows(inci, 1, reverse) + ti
            g2r, g2i = _cmul(m8r, m8i, gr, gi)
            gr = g2r + jnp.broadcast_to(incr[last:last + 1, :], gr.shape)
            gi = g2i + jnp.broadcast_to(inci[last:last + 1, :], gi.shape)
    _, acc = _seg_scan(xr_ref, xi_ref, lam, seg, nchain, reverse, True, init, extra)
    return acc


def _diag_mask():
    steps = LANES // SSM_GROUP // GROUPS_PER_TILE
    return (jnp.eye(steps, dtype=F32)[:, None, :, None] * jnp.eye(GROUPS_PER_TILE, dtype=F32)[None, :, None, :])


def _ssm_tables(bb_re, bb_im, c_re, c_im):
    g = bb_re.shape[0]
    nt = g // GROUPS_PER_TILE
    steps = LANES // SSM_GROUP // GROUPS_PER_TILE
    mask = _diag_mask()

    def b_tab(bb):
        x = bb.reshape(nt // steps, steps, GROUPS_PER_TILE, SSM_STATE, SSM_GROUP)
        x = jnp.transpose(x, (0, 1, 4, 2, 3))[:, :, None, None]
        m = jnp.transpose(mask, (0, 2, 3, 1))[None, :, :, :, None, :, None]
        return (x * m).reshape(nt, LANES, TILE_STATES)

    def c_tab(c):
        x = c.reshape(nt // steps, steps, GROUPS_PER_TILE, SSM_GROUP, SSM_STATE)
        x = jnp.transpose(x, (0, 1, 2, 4, 3))[:, :, :, :, None, None]
        m = mask[None, :, :, None, :, :, None]
        return (x * m).reshape(nt, TILE_STATES, LANES)

    return b_tab(bb_re), b_tab(bb_im), c_tab(c_re), c_tab(c_im)


def _ssm_untable_b(db, g):
    nt = g // GROUPS_PER_TILE
    steps = LANES // SSM_GROUP // GROUPS_PER_TILE
    x = db.reshape(nt // steps, steps, GROUPS_PER_TILE, SSM_STATE, steps, GROUPS_PER_TILE, SSM_GROUP)
    m = _diag_mask()[None, :, :, None, :, :, None]
    return jnp.sum(x * m, axis=(4, 5)).reshape(g, SSM_STATE, SSM_GROUP)


def _ssm_untable_c(dc, g):
    nt = g // GROUPS_PER_TILE
    steps = LANES // SSM_GROUP // GROUPS_PER_TILE
    x = dc.reshape(nt // steps, steps, steps, GROUPS_PER_TILE, SSM_GROUP, GROUPS_PER_TILE, SSM_STATE)
    m = jnp.transpose(_diag_mask(), (0, 2, 3, 1))[None, :, :, :, None, :, None]
    out = jnp.sum(x * m, axis=(2, 3))
    return jnp.transpose(out, (0, 1, 3, 2, 4)).reshape(g, SSM_GROUP, SSM_STATE)


def _lam_tiles(lam_ref):
    out = []
    for j in range(TILE_STATES // LANES):
        out.append(jnp.broadcast_to(lam_ref[:, j * LANES:(j + 1) * LANES], (SUBLANES, LANES)))
    return out


def _scan_chains(lp):
    for n in (4, 2, 1):
        if lp % (SUBLANES * n) == 0 and (lp // SUBLANES) % 16 == 0:
            return n
    raise ValueError(lp)


def _split_tiles(dst_ref, rows, val):
    for j in range(val.shape[1] // LANES):
        dst_ref[j, rows, :] = val[:, j * LANES:(j + 1) * LANES]


def _cat_tiles(src_ref, rows):
    njt = src_ref.shape[0]
    return jnp.concatenate([src_ref[j, rows, :] for j in range(njt)], axis=1).astype(BF16)


def _ssm_fwd(u, lam_re, lam_im, tb_re, tb_im, tc_re, tc_im, d_skip, comm=None):
    lp, w = u.shape
    nt = tb_re.shape[0]
    nchain = _scan_chains(lp)
    seg = lp // (SUBLANES * nchain)
    chunk = lp // SUBLANES
    njt = TILE_STATES // LANES

    def body(u_ref, lr_ref, li_ref, br_ref, bi_ref, cr_ref, ci_ref, d_ref, y_ref, yg_ref, xr, xi):
        t = pl.program_id(0)
        for s in range(SUBLANES):
            rs = pl.ds(s * chunk, chunk)
            ub = u_ref[rs, :].astype(BF16)
            _split_tiles(xr, rs, _dot(ub, br_ref[...], "nn"))
            _split_tiles(xi, rs, _dot(ub, bi_ref[...], "nn"))
        lrs, lis = _lam_tiles(lr_ref), _lam_tiles(li_ref)
        _ssm_scan(xr, xi, list(zip(lrs, lis)), seg, nchain, False)
        for s in range(SUBLANES):
            rs = pl.ds(s * chunk, chunk)
            y = _dot(_cat_tiles(xr, rs), cr_ref[...], "nn") - _dot(_cat_tiles(xi, rs), ci_ref[...], "nn")

            @pl.when(t % 2 == 0)
            def _():
                y_ref[rs, :] = y + d_ref[...] * u_ref[rs, :]

            @pl.when(t % 2 == 1)
            def _():
                total = y_ref[rs, :] + y
                y_ref[rs, :] = total
                yg_ref[rs, :] = _gelu(total).astype(BF16)

    blk = pl.BlockSpec((lp, LANES), lambda t: (0, t // 2))
    lam_spec = pl.BlockSpec((None, 1, TILE_STATES), lambda t: (t, 0, 0))
    b_spec = pl.BlockSpec((None, LANES, TILE_STATES), lambda t: (t, 0, 0))
    c_spec = pl.BlockSpec((None, TILE_STATES, LANES), lambda t: (t, 0, 0))
    (y, yg), couts = _hosted_call(
        body, comm, out_shape=[jax.ShapeDtypeStruct((lp, w), F32), jax.ShapeDtypeStruct((lp, w), BF16)], grid=(nt,),
        in_specs=[blk, lam_spec, lam_spec, b_spec, b_spec, c_spec, c_spec,
                  pl.BlockSpec((1, LANES), lambda t: (0, t // 2))],
        out_specs=[blk, blk],
        scratch_shapes=[pltpu.VMEM((njt, lp, LANES), F32), pltpu.VMEM((njt, lp, LANES), F32)],
        sem=("arbitrary",), name="ssm_fwd",
        args=(u, lam_re, lam_im, tb_re, tb_im, tc_re, tc_im, d_skip.reshape(1, w)))
    return (y, yg), couts


def _ssm_bwd(u, dy, lam_re, lam_im, tb_re, tb_im, tc_re, tc_im, d_skip, comm=None):
    lp, w = u.shape
    nt = tb_re.shape[0]
    nchain = _scan_chains(lp)
    seg = lp // (SUBLANES * nchain)
    chunk = lp // SUBLANES
    njt = TILE_STATES // LANES
    tbt_re, tbt_im = jnp.swapaxes(tb_re, 1, 2), jnp.swapaxes(tb_im, 1, 2)
    tct_re, tct_im = jnp.swapaxes(tc_re, 1, 2), jnp.swapaxes(tc_im, 1, 2)

    def body(u_ref, dy_ref, lr_ref, li_ref, br_ref, bi_ref, btr_ref, bti_ref, ctr_ref, cti_ref, d_ref,
             du_ref, dlr_ref, dli_ref, dbr_ref, dbi_ref, dcr_ref, dci_ref, dd_ref, hr, hi, ar, ai):
        t = pl.program_id(0)
        lrs, lis = _lam_tiles(lr_ref), _lam_tiles(li_ref)
        for s in range(SUBLANES):
            rs = pl.ds(s * chunk, chunk)
            ub = u_ref[rs, :].astype(BF16)
            dyb = dy_ref[rs, :].astype(BF16)
            _split_tiles(hr, rs, _dot(ub, br_ref[...], "nn"))
            _split_tiles(hi, rs, _dot(ub, bi_ref[...], "nn"))
            _split_tiles(ar, rs, _dot(dyb, ctr_ref[...], "nn"))
            _split_tiles(ai, rs, -_dot(dyb, cti_ref[...], "nn"))
        _ssm_scan(hr, hi, list(zip(lrs, lis)), seg, nchain, False)

        def dlam_step(tt, a, j, a_r, a_i, acc):
            sl = _chain_rows(a, jnp.maximum(tt - 1, 0), seg)
            p_r, p_i = hr[j, sl, :], hi[j, sl, :]
            acc = list(acc)
            acc[2 * j] = acc[2 * j] + jnp.where(tt > 0, a_r * p_r + a_i * p_i, 0.0)
            acc[2 * j + 1] = acc[2 * j + 1] + jnp.where(tt > 0, a_i * p_r - a_r * p_i, 0.0)
            return tuple(acc)

        zero = tuple([jnp.zeros((SUBLANES, LANES), F32)] * (2 * njt))
        conj = [(lr, -li) for lr, li in zip(lrs, lis)]
        acc = list(_ssm_scan(ar, ai, conj, seg, nchain, True, (dlam_step, zero)))
        row0 = lax.broadcasted_iota(jnp.int32, (SUBLANES, LANES), 0) == 0
        for j in range(njt):
            cs = slice(j * LANES, (j + 1) * LANES)
            for a in range(nchain):
                p_r = _shift_rows(hr[j, _chain_rows(a, seg - 1, seg), :], 1, False)
                p_i = _shift_rows(hi[j, _chain_rows(a, seg - 1, seg), :], 1, False)
                if a > 0:
                    before = pl.ds(a * SUBLANES * seg - 1, 1)
                    p_r = jnp.where(row0, jnp.broadcast_to(hr[j, before, :], p_r.shape), p_r)
                    p_i = jnp.where(row0, jnp.broadcast_to(hi[j, before, :], p_i.shape), p_i)
                a_r, a_i = ar[j, _chain_rows(a, 0, seg), :], ai[j, _chain_rows(a, 0, seg), :]
                acc[2 * j] = acc[2 * j] + a_r * p_r + a_i * p_i
                acc[2 * j + 1] = acc[2 * j + 1] + a_i * p_r - a_r * p_i
            dlr_ref[:, cs] = jnp.sum(acc[2 * j], axis=0, keepdims=True)
            dli_ref[:, cs] = jnp.sum(acc[2 * j + 1], axis=0, keepdims=True)

        dd = jnp.zeros((1, LANES), F32)
        for s in range(SUBLANES):
            rs = pl.ds(s * chunk, chunk)
            ub = u_ref[rs, :].astype(BF16)
            dyv = dy_ref[rs, :]
            dyb = dyv.astype(BF16)
            arb, aib = _cat_tiles(ar, rs), _cat_tiles(ai, rs)
            hrb, hib = _cat_tiles(hr, rs), _cat_tiles(hi, rs)
            du = _dot(arb, btr_ref[...], "nn") + _dot(aib, bti_ref[...], "nn")
            upd = [(dbr_ref, _dot(arb, ub, "tn")), (dbi_ref, _dot(aib, ub, "tn")),
                   (dcr_ref, _dot(dyb, hrb, "tn")), (dci_ref, -_dot(dyb, hib, "tn"))]
            for ref, val in upd:
                if s == 0:
                    ref[...] = val
                else:
                    ref[...] += val
            rows = lax.broadcasted_iota(jnp.int32, (chunk, LANES), 0) + s * chunk
            keep = rows >= PAD_FRONT
            dd = dd + jnp.sum(dyv * u_ref[rs, :], axis=0, keepdims=True)

            @pl.when(t % 2 == 0)
            def _():
                du_ref[rs, :] = jnp.where(keep, du + d_ref[...] * dyv, 0.0)

            @pl.when(t % 2 == 1)
            def _():
                du_ref[rs, :] += jnp.where(keep, du, 0.0)

        @pl.when(t % 2 == 0)
        def _():
            dd_ref[...] = dd

    blk = pl.BlockSpec((lp, LANES), lambda t: (0, t // 2))
    vec = pl.BlockSpec((1, LANES), lambda t: (0, t // 2))
    lam_spec = pl.BlockSpec((None, 1, TILE_STATES), lambda t: (t, 0, 0))
    b_spec = pl.BlockSpec((None, LANES, TILE_STATES), lambda t: (t, 0, 0))
    c_spec = pl.BlockSpec((None, TILE_STATES, LANES), lambda t: (t, 0, 0))
    lam_shape = jax.ShapeDtypeStruct((nt, 1, TILE_STATES), F32)
    bt_shape = jax.ShapeDtypeStruct((nt, TILE_STATES, LANES), F32)
    ct_shape = jax.ShapeDtypeStruct((nt, LANES, TILE_STATES), F32)
    st = pltpu.VMEM((njt, lp, LANES), F32)
    return _hosted_call(
        body, comm,
        out_shape=[jax.ShapeDtypeStruct((lp, w), F32), lam_shape, lam_shape, bt_shape, bt_shape, ct_shape, ct_shape,
                   jax.ShapeDtypeStruct((1, w), F32)],
        grid=(nt,),
        in_specs=[blk, blk, lam_spec, lam_spec, b_spec, b_spec, c_spec, c_spec, b_spec, b_spec, vec],
        out_specs=[blk, lam_spec, lam_spec, c_spec, c_spec, b_spec, b_spec, vec],
        scratch_shapes=[st, st, st, st], sem=("arbitrary",), name="ssm_bwd",
        args=(u, dy, lam_re, lam_im, tb_re, tb_im, tbt_re, tbt_im, tct_re, tct_im, d_skip.reshape(1, w)))


def _ssm_params(a_re, a_im, log_dt, b_re, b_im):
    dt = jnp.exp(log_dt)[:, None]
    mag = jnp.exp(a_re * dt)
    lb_re = mag * jnp.cos(a_im * dt)
    lb_im = mag * jnp.sin(a_im * dt)
    den = a_re * a_re + a_im * a_im
    num_re = lb_re - 1.0
    coef_re = (num_re * a_re + lb_im * a_im) / den
    coef_im = (lb_im * a_re - num_re * a_im) / den
    bb_re = coef_re[..., None] * b_re - coef_im[..., None] * b_im
    bb_im = coef_re[..., None] * b_im + coef_im[..., None] * b_re
    return lb_re, lb_im, bb_re, bb_im


def _merge_fwd(o, yg, ga, gs, w3t, comm=None):
    lp, d = ga.shape
    kw = w3t.shape[2]
    tm = _row_tile(lp)

    def epi(accs, in_refs, out_refs):
        attn, vv, gg = accs
        out_refs[0][...] = (_sigmoid(in_refs[5][...]) * attn
                            + _sigmoid(in_refs[6][...]) * (vv * _sigmoid(gg))).astype(BF16)

    wspec = lambda which: pl.BlockSpec((None, d, kw), lambda i: (which, 0, 0))
    rowspec = pl.BlockSpec((tm, d), lambda i: (i, 0))
    aspec = pl.BlockSpec((tm, kw), lambda i: (i, 0))
    res = _matmul(
        "merge_fwd", (lp // tm,), None, [o, yg, w3t, w3t, w3t, ga, gs],
        [aspec, aspec, wspec(0), wspec(1), wspec(2), rowspec, rowspec],
        [(0, 2, "nt", 0), (1, 3, "nt", 1), (1, 4, "nt", 2)], [(tm, d)] * 3, epi,
        [jax.ShapeDtypeStruct((lp, d), BF16)], [rowspec], ("parallel",), comm)
    return (res[0], []) if comm is None else (res[0][0], res[1])


def _out_proj(merged, w_out, h, next_gain, comm=None):
    lp, d = h.shape
    tm = _row_tile(lp)
    shapes, specs, extra_in, extra_specs = _residual_specs(lp, d, tm, next_gain)

    def epi(accs, in_refs, out_refs):
        _residual_outputs(in_refs[2][...] + accs[0], in_refs, out_refs, 3 if extra_in else None)

    rowspec = pl.BlockSpec((tm, d), lambda i: (i, 0))
    res = _matmul(
        "mix_out_proj", (lp // tm,), None, [merged, w_out, h] + extra_in,
        [rowspec, pl.BlockSpec((d, d), lambda i: (0, 0)), rowspec] + extra_specs,
        [(0, 1, "nn", 0)], [(tm, d)], epi, shapes, specs, ("parallel",), comm)
    return (res, []) if comm is None else res


def _merge_bwd(dhb, w_out, o, yg, ga, gs, w3t):
    lp, d = ga.shape
    kw = w3t.shape[2]
    tm = _row_tile(lp)

    def epi(accs, in_refs, out_refs):
        dm, attn, vv, gg = accs
        sa = _sigmoid(in_refs[7][...])
        ss = _sigmoid(in_refs[8][...])
        sg = _sigmoid(gg)
        ssm = vv * sg
        dssm = dm * ss
        out_refs[0][...] = (dm * sa).astype(BF16)
        out_refs[1][...] = (dssm * sg).astype(BF16)
        out_refs[2][...] = (dssm * vv * sg * (1.0 - sg)).astype(BF16)
        out_refs[3][...] = (dm * attn * sa * (1.0 - sa)).astype(BF16)
        out_refs[4][...] = (dm * ssm * ss * (1.0 - ss)).astype(BF16)

    wspec = lambda which: pl.BlockSpec((None, d, kw), lambda i: (which, 0, 0))
    rowspec = pl.BlockSpec((tm, d), lambda i: (i, 0))
    aspec = pl.BlockSpec((tm, kw), lambda i: (i, 0))
    shp = jax.ShapeDtypeStruct((lp, d), BF16)
    return _matmul(
        "merge_bwd", (lp // tm,), None, [dhb, w_out, o, yg, w3t, w3t, w3t, ga, gs],
        [rowspec, pl.BlockSpec((d, d), lambda i: (0, 0)), aspec, aspec, wspec(0), wspec(1), wspec(2), rowspec,
         rowspec],
        [(0, 1, "nt", 0), (2, 4, "nt", 1), (3, 5, "nt", 2), (3, 6, "nt", 3)], [(tm, d)] * 4, epi,
        [shp] * 5, [rowspec] * 5, ("parallel",))


def _branch_bwd(dattn, dv, dg, w3t, y):
    lp, d = dattn.shape
    kw = w3t.shape[2]
    tm = _row_tile(lp)

    def epi(accs, in_refs, out_refs):
        out_refs[0][...] = accs[0].astype(BF16)
        out_refs[1][...] = accs[1] * _gelu_grad(in_refs[6][...])

    wspec = lambda which: pl.BlockSpec((None, d, kw), lambda i: (which, 0, 0))
    rowspec = pl.BlockSpec((tm, d), lambda i: (i, 0))
    aspec = pl.BlockSpec((tm, kw), lambda i: (i, 0))
    return _matmul(
        "branch_bwd", (lp // tm,), None, [dattn, dv, dg, w3t, w3t, w3t, y],
        [rowspec, rowspec, rowspec, wspec(0), wspec(1), wspec(2), aspec],
        [(0, 3, "nn", 0), (1, 4, "nn", 1), (2, 5, "nn", 1)], [(tm, kw)] * 2, epi,
        [jax.ShapeDtypeStruct((lp, kw), BF16), jax.ShapeDtypeStruct((lp, kw), F32)], [aspec, aspec],
        ("parallel",))


def _sibling_half(acc, rows):
    half = rows // 2
    return jnp.where(lax.axis_index("c") == 0, acc[half:rows], acc[:half]).astype(BF16)


def _tn_cols(x, ys, name):
    lp, kx = x.shape
    n = ys[0].shape[1]
    n4 = n // N_CHIPS
    tk = _tn_tiles(lp)
    ny = len(ys)

    def epi(accs, in_refs, out_refs):
        for i, acc in enumerate(accs):
            out_refs[i][...] = acc
            out_refs[ny + i][...] = _sibling_half(acc, kx)

    shp = jax.ShapeDtypeStruct((N_CHIPS, kx, n4), F32)
    shp_half = jax.ShapeDtypeStruct((N_CHIPS, kx // 2, n4), BF16)
    res = _matmul(
        name, (N_CHIPS, lp // tk), 1, [x] + list(ys),
        [pl.BlockSpec((tk, kx), lambda j, k: (k, 0))] + [pl.BlockSpec((tk, n4), lambda j, k: (k, j))] * ny,
        [(0, 1 + i, "tn", i) for i in range(ny)], [(kx, n4)] * ny, epi,
        [shp] * ny + [shp_half] * ny,
        [pl.BlockSpec((None, kx, n4), lambda j, k: (j, 0, 0))] * ny
        + [pl.BlockSpec((None, kx // 2, n4), lambda j, k: (j, 0, 0))] * ny,
        ("arbitrary", "arbitrary"))
    return res[:ny], res[ny:]


def _tn_full(x, y, name, tn_cols=None):
    lp, kx = x.shape
    n = y.shape[1]
    tk = _tn_tiles(lp)
    tn = n if tn_cols is None else tn_cols
    k4 = kx // N_CHIPS

    def epi(accs, in_refs, out_refs):
        for j in range(N_CHIPS):
            slab = accs[0][j * k4:(j + 1) * k4]
            out_refs[0][j] = slab
            out_refs[1][j] = _sibling_half(slab, k4)

    return _matmul(
        name, (n // tn, lp // tk), 1, [x, y],
        [pl.BlockSpec((tk, kx), lambda j, k: (k, 0)), pl.BlockSpec((tk, tn), lambda j, k: (k, j))],
        [(0, 1, "tn", 0)], [(kx, tn)], epi,
        [jax.ShapeDtypeStruct((N_CHIPS, k4, n), F32), jax.ShapeDtypeStruct((N_CHIPS, k4 // 2, n), BF16)],
        [pl.BlockSpec((N_CHIPS, k4, tn), lambda j, k: (0, 0, j)),
         pl.BlockSpec((N_CHIPS, k4 // 2, tn), lambda j, k: (0, 0, j))],
        ("arbitrary", "arbitrary"))


def _in_proj_bwd(dz, w_in, dh, h_in, gain):
    lp, d = h_in.shape
    inw = w_in.shape[0]
    tm = _row_tile(lp)

    def epi(accs, in_refs, out_refs):
        i = pl.program_id(0)
        dx, dgrow = _rms_bwd_math(accs[0], in_refs[3][...], in_refs[4][...])
        dh_new = in_refs[2][...] + dx
        out_refs[0][...] = dh_new
        out_refs[2][...] = dh_new.astype(BF16)

        @pl.when(i == 0)
        def _():
            out_refs[1][...] = jnp.zeros_like(out_refs[1])

        out_refs[1][...] += jnp.sum(dgrow, axis=0, keepdims=True)

    row = pl.BlockSpec((tm, d), lambda i: (i, 0))
    return _matmul(
        "mix_in_proj_bwd", (lp // tm,), None, [dz, w_in, dh, h_in, gain.reshape(1, d)],
        [pl.BlockSpec((tm, inw), lambda i: (i, 0)), pl.BlockSpec((inw, d), lambda i: (0, 0)), row, row,
         pl.BlockSpec((1, d), lambda i: (0, 0))],
        [(0, 1, "nn", 0)], [(tm, d)], epi,
        [jax.ShapeDtypeStruct((lp, d), F32), jax.ShapeDtypeStruct((1, d), F32), jax.ShapeDtypeStruct((lp, d), BF16)],
        [row, pl.BlockSpec((1, d), lambda i: (0, 0)), row], ("arbitrary",))


def _loss_head(h, gain, target):
    lp, d = h.shape
    nb = lp // BLOCK

    def body(h_ref, g_ref, t_ref, dh_ref, dg_ref, loss_ref, dhb_ref):
        i = pl.program_id(0)

        @pl.when(i == 0)
        def _():
            dg_ref[...] = jnp.zeros_like(dg_ref)
            loss_ref[...] = jnp.zeros_like(loss_ref)
            dh_ref[...] = jnp.zeros_like(dh_ref)
            dhb_ref[...] = jnp.zeros_like(dhb_ref)

        @pl.when(i > 0)
        def _():
            x = h_ref[...]
            g = g_ref[...]
            r = lax.rsqrt(jnp.mean(x * x, axis=-1, keepdims=True) + EPS)
            err = x * r * g - t_ref[...]
            loss_ref[...] += jnp.zeros_like(loss_ref) + 0.5 * jnp.sum(jnp.sum(err * err, axis=-1, keepdims=True)) / d
            dx, dgrow = _rms_bwd_math(err * (1.0 / d), x, g)
            dh_ref[...] = dx
            dhb_ref[...] = dx.astype(BF16)
            dg_ref[...] += jnp.sum(dgrow, axis=0, keepdims=True)

    row = pl.BlockSpec((BLOCK, d), lambda i: (i, 0))
    one = pl.BlockSpec((1, d), lambda i: (0, 0))
    return pl.pallas_call(
        body,
        out_shape=[jax.ShapeDtypeStruct((lp, d), F32), jax.ShapeDtypeStruct((1, d), F32),
                   jax.ShapeDtypeStruct((SUBLANES, LANES), F32), jax.ShapeDtypeStruct((lp, d), BF16)],
        grid=(nb,),
        in_specs=[row, one, pl.BlockSpec((BLOCK, d), lambda i: (jnp.maximum(i - 1, 0), 0))],
        out_specs=[row, one, pl.BlockSpec((SUBLANES, LANES), lambda i: (0, 0)), row],
        compiler_params=_cparams(("arbitrary",)), name="loss_head")(h, gain.reshape(1, d), target)


def _adam_math(w, g, m, v):
    m = ADAM_B1 * m + (1.0 - ADAM_B1) * g
    v = ADAM_B2 * v + (1.0 - ADAM_B2) * (g * g)
    m_hat = m / (1.0 - ADAM_B1 ** ADAM_STEP)
    v_hat = v / (1.0 - ADAM_B2 ** ADAM_STEP)
    delta = -ADAM_LR * (m_hat / (jnp.sqrt(v_hat) + ADAM_EPS) + ADAM_WD * w)
    return delta, m, v


def _adamw_layers(w, m, v, mine, other, pos, name):
    depth, r, c = w.shape
    half = r // 2
    tr = _div_tile(half, c * 4)
    nh = half // tr

    def body(*refs):
        pos_ref, w_ref, m_ref, v_ref = refs[:4]
        mine_refs = refs[4:4 + depth]
        other_refs = refs[4 + depth:4 + 2 * depth]
        g_out, d_out, m_out, v_out = refs[4 + 2 * depth:]
        layer, i = pl.program_id(0), pl.program_id(1)
        is_mine = (i // nh) == pos_ref[0]

        def update(g):
            delta, nm, nv = _adam_math(w_ref[...], g, m_ref[...], v_ref[...])
            g_out[...] = g
            d_out[...] = delta
            m_out[...] = nm
            v_out[...] = nv

        for l in range(depth):
            @pl.when((layer == l) & is_mine)
            def _(l=l):
                update(mine_refs[l][...])

            @pl.when((layer == l) & jnp.logical_not(is_mine))
            def _(l=l):
                update(other_refs[l][...])

    stacked = pl.BlockSpec((None, tr, c), lambda l, i, p: (l, i, 0))

    def gspec(layer, is_other):
        def imap(l, i, p):
            first = jnp.where(is_other, 1 - p[0], p[0]) * nh
            here = jnp.clip(i - first, 0, nh - 1)
            return (jnp.where(l == layer, here, jnp.where(l < layer, 0, nh - 1)), 0)
        return pl.BlockSpec((tr, c), imap)

    shp = jax.ShapeDtypeStruct((depth, r, c), F32)
    grid_spec = pltpu.PrefetchScalarGridSpec(
        num_scalar_prefetch=1, grid=(depth, 2 * nh),
        in_specs=[stacked] * 3 + [gspec(l, 0) for l in range(depth)] + [gspec(l, 1) for l in range(depth)],
        out_specs=[stacked] * 4)
    return pl.pallas_call(
        body, out_shape=[shp] * 4, grid_spec=grid_spec,
        compiler_params=_cparams(("arbitrary", "arbitrary")), name=name)(pos, w, m, v, *mine, *other)


def _adamw_whole(w, g, m, v, name):
    def body(w_ref, g_ref, m_ref, v_ref, d_out, m_out, v_out):
        delta, nm, nv = _adam_math(w_ref[...], g_ref[...], m_ref[...], v_ref[...])
        d_out[...] = delta
        m_out[...] = nm
        v_out[...] = nv

    shp = jax.ShapeDtypeStruct(w.shape, F32)
    return pl.pallas_call(body, out_shape=[shp] * 3, compiler_params=_cparams(), name=name)(w, g, m, v)


def _mesh_pos():
    return lax.axis_index("x"), lax.axis_index("y"), lax.axis_index("c")


def _row_half(ref, which, lead):
    half = ref.shape[lead] // 2
    idx = (slice(None),) * lead + (pl.ds(which * half, half), slice(None))
    return ref.at[idx]


def _gather_comm(arrs, tag):
    n = len(arrs)

    def ctx(ins, outs, sems):
        send_sems, recv_sems, local_sems = sems
        x, y, c = _mesh_pos()
        chips = [(1 - x, y), (x, 1 - y), (1 - x, 1 - y)]

        def slot(k, chip, which):
            lead = len(ins[k].shape) - 2
            return _row_half(outs[k].at[2 * chip[0] + chip[1]], which, lead)

        def copy(k, j, src, dst, to):
            return pltpu.make_async_remote_copy(
                src_ref=src, dst_ref=dst, send_sem=send_sems.at[6 * k + j], recv_sem=recv_sems.at[6 * k + j],
                device_id=to, device_id_type=MESH)

        def local(k):
            return pltpu.make_async_copy(ins[k], outs[k].at[2 * x + y], local_sems.at[k])

        def first(k, j):
            lead = len(ins[k].shape) - 2
            return copy(k, j, _row_half(ins[k], c, lead), slot(k, (x, y), c), (*chips[j], c))

        def passed(k, j, which):
            return copy(k, 3 + j, slot(k, chips[j], which), slot(k, chips[j], which), (x, y, 1 - c))

        def landed(k, j):
            return copy(k, j, slot(k, chips[j], c), slot(k, chips[j], c), (x, y, 1 - c))

        return c, local, first, passed, landed

    def start(ins, outs, sems):
        c, local, first, passed, landed = ctx(ins, outs, sems)
        for k in range(n):
            local(k).start()
            for j in range(3):
                first(k, j).start()

    def mid(ins, outs, sems):
        c, local, first, passed, landed = ctx(ins, outs, sems)
        for j in range(3):
            for k in range(n):
                landed(k, j).wait_recv()
                passed(k, j, c).start()

    def finish(ins, outs, sems):
        c, local, first, passed, landed = ctx(ins, outs, sems)
        for j in range(3):
            for k in range(n):
                passed(k, j, 1 - c).wait_recv()
        for k in range(n):
            for j in range(3):
                first(k, j).wait_send()
                passed(k, j, c).wait_send()
            local(k).wait()

    return _Comm(
        tag, arrs, [jax.ShapeDtypeStruct((N_CHIPS,) + a.shape, a.dtype) for a in arrs],
        [pltpu.SemaphoreType.DMA((6 * n,)), pltpu.SemaphoreType.DMA((6 * n,)), pltpu.SemaphoreType.DMA((n,))],
        start, mid, finish)


def _run_comm(comm, name):
    n_in, n_out = len(comm.ins), len(comm.out_shapes)

    def body(*refs):
        ins, outs, sems = refs[:n_in], refs[n_in:n_in + n_out], refs[n_in + n_out:]
        comm.start(ins, outs, sems)
        if comm.mid is not None:
            comm.mid(ins, outs, sems)
        comm.finish(ins, outs, sems)

    return pl.pallas_call(
        body, out_shape=comm.out_shapes, in_specs=[HBM_SPEC] * n_in, out_specs=[HBM_SPEC] * n_out,
        scratch_shapes=comm.sems, name=name)(*comm.ins)


def _all_gather_chips(arrs, name):
    return _run_comm(_gather_comm(arrs, "gather"), name)


GATHER_US_PER_BYTE = 380.0 / 11.65e6
HOST_US = dict(ffn_up=68.0, ffn_down=37.0, in_proj=38.0, attn_fwd=103.0, ssm_fwd=70.0, merge_fwd=30.0,
               out_proj=23.0)
HOST_SLACK_US = 10.0


class _WeightStream:
    def __init__(self, pieces):
        self.keys = [k for k, _ in pieces]
        self.shards = dict(pieces)
        self.next = 0
        self.full = {}
        self.pending = []

    def comm_for(self, host):
        budget = HOST_US[host] + HOST_SLACK_US
        taken, cost = [], 0.0
        while self.next < len(self.keys):
            key = self.keys[self.next]
            c = self.shards[key].size * self.shards[key].dtype.itemsize * GATHER_US_PER_BYTE
            if cost + c > budget and taken:
                break
            taken.append(key)
            cost += c
            self.next += 1
        self.pending = taken
        if not taken:
            return None
        return _gather_comm([self.shards[k] for k in taken], "g_" + "_".join(k[1] for k in taken))

    def deposit(self, gathered):
        for key, arr in zip(self.pending, gathered):
            self.full[key] = arr
        self.pending = []

    def get(self, key):
        if key not in self.full:
            upto = self.keys.index(key) + 1
            keys = self.keys[self.next:upto]
            self.next = upto
            for k, arr in zip(keys, _all_gather_chips([self.shards[k] for k in keys], "gather_now")):
                self.full[k] = arr
        return self.full[key]


def _all_gather_devices(x_shard, name):
    m_per, ncol = x_shard.shape

    def body(x_ref, out_ref, send_sems, recv_sems, local_sem):
        x, y, c = _mesh_pos()
        me, sibling = (x, y, c), (x, y, 1 - c)
        chips = [(1 - x, y), (x, 1 - y), (1 - x, 1 - y)]

        def rows(px, py, pc):
            return out_ref.at[4 * px + 2 * py + pc]

        def copy(k, block, to, src=None):
            return pltpu.make_async_remote_copy(
                src_ref=rows(*block) if src is None else src, dst_ref=rows(*block),
                send_sem=send_sems.at[k], recv_sem=recv_sems.at[k], device_id=to, device_id_type=MESH)

        mine = pltpu.make_async_copy(x_ref, rows(*me), local_sem)
        mine.start()
        first = [copy(0, me, sibling, src=x_ref)]
        first += [copy(1 + j, me, (*chip, c), src=x_ref) for j, chip in enumerate(chips)]
        for cp in first:
            cp.start()
        passed = [copy(4 + j, (*chip, c), sibling) for j, chip in enumerate(chips)]
        for j, chip in enumerate(chips):
            copy(1 + j, (*chip, c), me).wait_recv()
            passed[j].start()
        copy(0, sibling, me).wait_recv()
        for j, chip in enumerate(chips):
            copy(4 + j, (*chip, 1 - c), me).wait_recv()
        for cp in first + passed:
            cp.wait_send()
        mine.wait()

    return pl.pallas_call(
        body, out_shape=jax.ShapeDtypeStruct((8, m_per, ncol), x_shard.dtype),
        in_specs=[pl.BlockSpec(memory_space=pltpu.VMEM)], out_specs=pl.BlockSpec(memory_space=pltpu.VMEM),
        scratch_shapes=[pltpu.SemaphoreType.DMA((7,)), pltpu.SemaphoreType.DMA((7,)), pltpu.SemaphoreType.DMA],
        compiler_params=pltpu.CompilerParams(vmem_limit_bytes=VMEM_LIMIT), name=name)(x_shard)


def _sum_devices(g8, name):
    _, r, c = g8.shape
    tr = _div_tile(r, c * 4 * 8)

    def body(g_ref, o_ref):
        acc = g_ref[0]
        for dev in range(1, 8):
            acc = acc + g_ref[dev]
        o_ref[...] = acc

    return pl.pallas_call(
        body, out_shape=jax.ShapeDtypeStruct((r, c), F32), grid=(r // tr,),
        in_specs=[pl.BlockSpec((8, tr, c), lambda i: (0, i, 0))], out_specs=pl.BlockSpec((tr, c), lambda i: (i, 0)),
        compiler_params=_cparams(("parallel",)), name=name)(g8)


def _chip_partials(arrs, recvs, pos, name):
    n = len(arrs)

    def body(pos_ref, *refs):
        for a_ref, b_ref, o_ref in zip(refs[:n], refs[n:2 * n], refs[2 * n:]):
            o_ref[...] = (a_ref[...] + b_ref[...]).astype(BF16)

    own_specs, recv_specs, shapes = [], [], []
    for arr in arrs:
        nslab, r, c = arr.shape
        own_specs.append(pl.BlockSpec((None, r // 2, c), lambda j, p: (j, p[0], 0)))
        recv_specs.append(pl.BlockSpec((None, r // 2, c), lambda j, p: (j, 0, 0)))
        shapes.append(jax.ShapeDtypeStruct((nslab, r // 2, c), BF16))
    grid_spec = pltpu.PrefetchScalarGridSpec(
        num_scalar_prefetch=1, grid=(N_CHIPS,), in_specs=own_specs + recv_specs, out_specs=recv_specs)
    return pl.pallas_call(
        body, out_shape=shapes, grid_spec=grid_spec,
        compiler_params=_cparams(("parallel",)), name=name)(pos, *arrs, *recvs)


def _chip_exchange_comm(parts, tag):
    n = len(parts)

    def copies(ins, outs, sems):
        send_sems, recv_sems = sems
        x, y, c = _mesh_pos()
        chips = [(1 - x, y), (x, 1 - y), (1 - x, 1 - y)]
        return [pltpu.make_async_remote_copy(
            src_ref=ins[k].at[2 * chip[0] + chip[1]], dst_ref=outs[k].at[j],
            send_sem=send_sems.at[3 * k + j], recv_sem=recv_sems.at[3 * k + j],
            device_id=(*chip, c), device_id_type=MESH) for k in range(n) for j, chip in enumerate(chips)]

    def start(ins, outs, sems):
        for cp in copies(ins, outs, sems):
            cp.start()

    def finish(ins, outs, sems):
        for cp in copies(ins, outs, sems):
            cp.wait()

    return _Comm(
        tag, parts, [jax.ShapeDtypeStruct((3,) + p.shape[1:], p.dtype) for p in parts],
        [pltpu.SemaphoreType.DMA((3 * n,)), pltpu.SemaphoreType.DMA((3 * n,))], start, None, finish)


def _reduce_halves(arrs, recvs, gots, pos, name):
    n = len(arrs)

    def body(pos_ref, *refs):
        for a_ref, b_ref, g_ref, o_ref in zip(refs[:n], refs[n:2 * n], refs[2 * n:3 * n], refs[3 * n:]):
            acc = a_ref[...] + b_ref[...]
            for j in range(3):
                acc = acc + g_ref[j].astype(F32)
            o_ref[...] = acc

    own_specs, recv_specs, got_specs, out_specs, shapes = [], [], [], [], []
    for arr in arrs:
        _, r, c = arr.shape
        own_specs.append(pl.BlockSpec((None, r // 2, c), lambda i, p: (p[1], p[0], 0)))
        recv_specs.append(pl.BlockSpec((None, r // 2, c), lambda i, p: (p[1], 0, 0)))
        got_specs.append(pl.BlockSpec((3, r // 2, c), lambda i, p: (0, 0, 0)))
        out_specs.append(pl.BlockSpec((r // 2, c), lambda i, p: (0, 0)))
        shapes.append(jax.ShapeDtypeStruct((r // 2, c), F32))
    grid_spec = pltpu.PrefetchScalarGridSpec(
        num_scalar_prefetch=1, grid=(1,), in_specs=own_specs + recv_specs + got_specs, out_specs=out_specs)
    return pl.pallas_call(
        body, out_shape=shapes, grid_spec=grid_spec,
        compiler_params=_cparams(("arbitrary",)), name=name)(pos, *arrs, *recvs, *gots)


def _share_halves(halves, name):
    n = len(halves)

    def body(*refs):
        ins, outs = refs[:n], refs[n:2 * n]
        send_sems, recv_sems = refs[2 * n:]
        x, y, c = _mesh_pos()
        cps = []
        for k in range(n):
            cp = pltpu.make_async_remote_copy(
                src_ref=ins[k], dst_ref=outs[k], send_sem=send_sems.at[k], recv_sem=recv_sems.at[k],
                device_id=(x, y, 1 - c), device_id_type=MESH)
            cp.start()
            cps.append(cp)
        for cp in cps:
            cp.wait()

    return pl.pallas_call(
        body, out_shape=[jax.ShapeDtypeStruct(h.shape, h.dtype) for h in halves],
        in_specs=[HBM_SPEC] * n, out_specs=[HBM_SPEC] * n,
        scratch_shapes=[pltpu.SemaphoreType.DMA((n,)), pltpu.SemaphoreType.DMA((n,))], name=name)(*halves)


class _Reduction:
    def __init__(self, arrs, others, pos, tag):
        self.arrs, self.pos, self.tag = arrs, pos, tag
        self.recv = _share_halves(others, "rs_sibling_" + tag)
        self.parts = _chip_partials(arrs, self.recv, pos, "rs_partial_" + tag)
        self.got = None

    def comm(self):
        return _chip_exchange_comm(self.parts, "rs_" + self.tag)

    def end(self):
        if self.got is None:
            self.got = _run_comm(self.comm(), "rs_chips_" + self.tag)
        halves = _reduce_halves(self.arrs, self.recv, self.got, self.pos, "rs_reduce_" + self.tag)
        return halves, _share_halves(halves, "rs_share_" + self.tag)


def _w_in_full(p, l, ws):
    slabs = ws.get((l, "w_in"))
    return slabs.reshape(-1, slabs.shape[2])


def _w3t_full(p, l, ws):
    if "w3t" not in p:
        slabs = ws.get((l, "w3"))
        p["w3t"] = jnp.swapaxes(slabs, 0, 1).reshape(slabs.shape[1], -1, slabs.shape[3])
    return p["w3t"]


def _w_out_full(l, ws):
    slabs = ws.get((l, "w_out"))
    return slabs.reshape(-1, slabs.shape[2])


def _layer_fwd(h, n0, l, p, next_gain, ws, tabs):
    def hosted(host, fn, *args):
        out, got = fn(*args, ws.comm_for(host))
        ws.deposit(got)
        return out

    ffn1_saved = hosted("ffn_up", _ffn_up, n0, ws.get((l, "wg1")), ws.get((l, "wu1")))
    h1, n = hosted("ffn_down", _ffn_down, ffn1_saved[2], ws.get((l, "wd1")), h, p["mix_norm"])
    ssm_w = p["ssm_d"].shape[0]
    q, k, v, u, ga, gs = hosted("in_proj", _in_proj, n, _w_in_full(p, l, ws), tabs, ssm_w)
    o = hosted("attn_fwd", _attn_fwd, q, k, v, p["attn_sinks"])
    y, yg = hosted("ssm_fwd", _ssm_fwd, u, *p["ssm_tabs"], p["ssm_d"])
    merged = hosted("merge_fwd", _merge_fwd, o, yg, ga, gs, _w3t_full(p, l, ws))
    h2, n2 = hosted("out_proj", _out_proj, merged, _w_out_full(l, ws), h1, p["ffn2_norm"])
    ffn2_saved = hosted("ffn_up", _ffn_up, n2, ws.get((l, "wg2")), ws.get((l, "wu2")))
    h3, *n3 = hosted("ffn_down", _ffn_down, ffn2_saved[2], ws.get((l, "wd2")), h2, next_gain)
    saved = dict(h0=h, h1=h1, h2=h2, ffn1=ffn1_saved, ffn2=ffn2_saved, n_mix=n, q=q, k=k, v=v, u=u, ga=ga, gs=gs,
                 o=o, y=y, yg=yg, merged=merged)
    return h3, (n3[0] if n3 else None), saved


def _layer_bwd(dh_pair, l, p, ws, s, tabs, pos):
    g = {}
    (dh2, dhb), g["ffn2_norm"], red_ffn2, _ = _ffn_bwd(
        dh_pair, s["h2"], p["ffn2_norm"], ws.get((l, "wg2")), ws.get((l, "wu2")), ws.get((l, "wd2")), p["f4"],
        s["ffn2"], pos)
    w3, w_out_w = _w3t_full(p, l, ws), _w_out_full(l, ws)
    lp, d = dh2.shape
    d4 = d // N_CHIPS
    dw_out, dw_out_other = _tn_full(s["merged"], dhb, "mix_dw_out")
    dattn, dv, dg, dga, dgs = _merge_bwd(dhb, w_out_w, s["o"], s["yg"], s["ga"], s["gs"], w3)
    (dw_ap,), (dw_ap_other,) = _tn_cols(s["o"], [dattn], "mix_dw_ap")
    (dw_gv, dw_gg), (dw_gv_other, dw_gg_other) = _tn_cols(s["yg"], [dv, dg], "mix_dw_glu")
    do, dy = _branch_bwd(dattn, dv, dg, w3, s["y"])
    (dq, dk, dvv, dkm, dvm, dsink), _ = _attn_bwd(s["q"], s["k"], s["v"], do, p["attn_sinks"], tabs)
    g["attn_sinks"] = dsink[:, 0]
    (du, dlr, dli, dbr, dbi, dcr, dci, dd), _ = _ssm_bwd(s["u"], dy, *p["ssm_tabs"], p["ssm_d"])
    ngrp = p["ssm_d"].shape[0] // SSM_GROUP
    g["ssm_lam"] = (dlr.reshape(ngrp, SSM_STATE), dli.reshape(ngrp, SSM_STATE),
                    _ssm_untable_b(dbr, ngrp), _ssm_untable_b(dbi, ngrp))
    g["ssm_c_re"] = _ssm_untable_c(dcr, ngrp)
    g["ssm_c_im"] = _ssm_untable_c(dci, ngrp)
    g["ssm_d"] = dd[0]
    dk = dk.at[:BLOCK].add(dkm)
    dvv = dvv.at[:BLOCK].add(dvm)
    dz = jnp.concatenate([dq.astype(BF16), dk.astype(BF16), dvv.astype(BF16), du.astype(BF16), dga, dgs], axis=1)
    n = s["n_mix"]
    w_in = _w_in_full(p, l, ws)
    dw_in, dw_in_other = _tn_full(dz, n, "mix_dw_in", d // 2)
    red_mix = _Reduction([dw_in, dw_ap, dw_gv, dw_gg, dw_out],
                         [dw_in_other, dw_ap_other, dw_gv_other, dw_gg_other, dw_out_other], pos, "mix")
    dh1, g["mix_norm"], dh1b = _in_proj_bwd(dz, w_in, dh2, s["h1"], p["mix_norm"])
    dh0_pair, g["ffn1_norm"], red_ffn1, red_mix.got = _ffn_bwd(
        (dh1, dh1b), s["h0"], p["ffn1_norm"], ws.get((l, "wg1")), ws.get((l, "wu1")), ws.get((l, "wd1")), p["f4"],
        s["ffn1"], pos, red_mix.comm())
    return dh0_pair, g, [*red_ffn1, red_mix, *red_ffn2]


BIG = ["ffn1_w_gate", "ffn1_w_up", "ffn1_w_down", "w_in", "w_attn_proj", "w_glu_v", "w_glu_g", "w_out",
       "ffn2_w_gate", "ffn2_w_up", "ffn2_w_down"]
TRANSPOSED = ["ffn1_w_gate", "ffn1_w_up", "w_in", "ffn2_w_gate", "ffn2_w_up"]
SMALL = ["ffn1_norm", "mix_norm", "attn_sinks", "ssm_a_re", "ssm_a_im", "ssm_log_dt", "ssm_b_re", "ssm_b_im",
         "ssm_c_re", "ssm_c_im", "ssm_d", "ffn2_norm", "final_norm"]
WEIGHTS = ["meta_tokens", "ffn1_norm", "ffn1_w_gate", "ffn1_w_up", "ffn1_w_down", "mix_norm", "w_in", "attn_sinks",
           "ssm_a_re", "ssm_a_im", "ssm_log_dt", "ssm_b_re", "ssm_b_im", "ssm_c_re", "ssm_c_im", "ssm_d",
           "w_attn_proj", "w_glu_v", "w_glu_g", "w_out", "ffn2_norm", "ffn2_w_gate", "ffn2_w_up", "ffn2_w_down",
           "final_norm"]


def _small_rows(shape):
    rows = -(-math.prod(shape) // LANES)
    return -(-rows // SUBLANES) * SUBLANES


def _pack_small(tree):
    parts = []
    for k in SMALL + ["meta_tokens"]:
        size, rows = math.prod(tree[k].shape), _small_rows(tree[k].shape)
        if size % LANES == 0:
            part = tree[k].reshape(size // LANES, LANES)
        else:
            part = jnp.pad(tree[k].reshape(1, size), ((0, 0), (0, LANES - size)))
        parts.append(jnp.pad(part, ((0, rows - part.shape[0]), (0, 0))))
    return jnp.concatenate(parts, axis=0)


def _unpack_small(packed, like):
    out, off = {}, 0
    for k in SMALL + ["meta_tokens"]:
        size, rows = math.prod(like[k].shape), _small_rows(like[k].shape)
        if size % LANES == 0:
            out[k] = packed[off:off + size // LANES].reshape(like[k].shape)
        else:
            out[k] = packed[off, :size].reshape(like[k].shape)
        off += rows
    return out


def kernel(x, meta_tokens, ffn1_norm, ffn1_w_gate, ffn1_w_up, ffn1_w_down, mix_norm, w_in, attn_sinks, ssm_a_re, ssm_a_im, ssm_log_dt, ssm_b_re, ssm_b_im, ssm_c_re, ssm_c_im, ssm_d, w_attn_proj, w_glu_v, w_glu_g, w_out, ffn2_norm, ffn2_w_gate, ffn2_w_up, ffn2_w_down, final_norm, loss_target, m_meta_tokens, m_ffn1_norm, m_ffn1_w_gate, m_ffn1_w_up, m_ffn1_w_down, m_mix_norm, m_w_in, m_attn_sinks, m_ssm_a_re, m_ssm_a_im, m_ssm_log_dt, m_ssm_b_re, m_ssm_b_im, m_ssm_c_re, m_ssm_c_im, m_ssm_d, m_w_attn_proj, m_w_glu_v, m_w_glu_g, m_w_out, m_ffn2_norm, m_ffn2_w_gate, m_ffn2_w_up, m_ffn2_w_down, m_final_norm, v_meta_tokens, v_ffn1_norm, v_ffn1_w_gate, v_ffn1_w_up, v_ffn1_w_down, v_mix_norm, v_w_in, v_attn_sinks, v_ssm_a_re, v_ssm_a_im, v_ssm_log_dt, v_ssm_b_re, v_ssm_b_im, v_ssm_c_re, v_ssm_c_im, v_ssm_d, v_w_attn_proj, v_w_glu_v, v_w_glu_g, v_w_out, v_ffn2_norm, v_ffn2_w_gate, v_ffn2_w_up, v_ffn2_w_down, v_final_norm):
    args = dict(locals())
    w = {k: args[k] for k in WEIGHTS}
    m = {k: args["m_" + k] for k in WEIGHTS}
    v = {k: args["v_" + k] for k in WEIGHTS}
    depth = ffn1_norm.shape[0]
    seq, d = x.shape[1], x.shape[2]
    lp = seq + BLOCK
    xi, yi, ci = _mesh_pos()
    pos = jnp.stack([ci, 2 * xi + yi]).astype(jnp.int32)

    tabs = _rope_tables(lp)
    (meta_all,) = _all_gather_chips([meta_tokens], "gather_meta")
    meta_full = jnp.concatenate([meta_all[j] for j in range(N_CHIPS)], axis=1)
    layers, pieces = [], []
    f4 = ffn1_w_gate.shape[2]
    fp = -(-f4 // MXU_DIM) * MXU_DIM

    def ffn_rows(wt):
        return jnp.pad(wt, ((0, fp - f4), (0, 0))).astype(BF16)

    for l in range(depth):
        pieces += [
            ((l, "wg1"), ffn_rows(ffn1_w_gate[l].T)), ((l, "wu1"), ffn_rows(ffn1_w_up[l].T)),
            ((l, "wd1"), ffn_rows(ffn1_w_down[l])), ((l, "w_in"), w_in[l].T.astype(BF16)),
            ((l, "w3"), jnp.stack([w_attn_proj[l].T, w_glu_v[l].T, w_glu_g[l].T]).astype(BF16)),
            ((l, "w_out"), w_out[l].astype(BF16)),
            ((l, "wg2"), ffn_rows(ffn2_w_gate[l].T)), ((l, "wu2"), ffn_rows(ffn2_w_up[l].T)),
            ((l, "wd2"), ffn_rows(ffn2_w_down[l]))]
        lb_re, lb_im, bb_re, bb_im = _ssm_params(ssm_a_re[l], ssm_a_im[l], ssm_log_dt[l], ssm_b_re[l], ssm_b_im[l])
        ngrp = lb_re.shape[0]
        nt = ngrp // GROUPS_PER_TILE
        ssm_tabs = (lb_re.reshape(nt, 1, TILE_STATES), lb_im.reshape(nt, 1, TILE_STATES),
                    *_ssm_tables(bb_re, bb_im, ssm_c_re[l], ssm_c_im[l]))
        layers.append(dict(
            ffn1_norm=ffn1_norm[l], mix_norm=mix_norm[l], ffn2_norm=ffn2_norm[l], attn_sinks=attn_sinks[l],
            ssm_d=ssm_d[l], ssm_tabs=ssm_tabs, f4=f4))
    ws = _WeightStream(pieces)
    ws.get((0, "wu1"))

    h = jnp.concatenate([jnp.zeros((PAD_FRONT, d), F32), meta_full, x[0]], axis=0)
    saved = []
    n0 = _rms_fwd(h, ffn1_norm[0], "rms_fwd_first")
    for l in range(depth):
        next_gain = ffn1_norm[l + 1] if l + 1 < depth else None
        h, n0, s = _layer_fwd(h, n0, l, layers[l], next_gain, ws, tabs)
        saved.append(s)
    dh, g_final, loss_acc, dhb = _loss_head(h, final_norm, loss_target[0])
    dh_pair = (dh, dhb)
    loss = lax.psum(loss_acc[0, 0], ("x", "y", "c"))

    grads, reds = [None] * depth, [None] * depth
    for l in reversed(range(depth)):
        dh_pair, grads[l], reds[l] = _layer_bwd(dh_pair, l, layers[l], ws, saved[l], tabs, pos)
    dh = dh_pair[0]
    grad_x = dh[BLOCK:][None]
    dmeta_local = dh[PAD_FRONT:BLOCK]

    small = {k: [] for k in SMALL}
    for l in range(depth):
        gl = grads[l]
        _, vjp = jax.vjp(_ssm_params, ssm_a_re[l], ssm_a_im[l], ssm_log_dt[l], ssm_b_re[l], ssm_b_im[l])
        da_re, da_im, dlog_dt, db_re, db_im = vjp(gl["ssm_lam"])
        for k, val in (("ffn1_norm", gl["ffn1_norm"][0]), ("mix_norm", gl["mix_norm"][0]),
                       ("attn_sinks", gl["attn_sinks"]), ("ssm_a_re", da_re), ("ssm_a_im", da_im),
                       ("ssm_log_dt", dlog_dt), ("ssm_b_re", db_re), ("ssm_b_im", db_im),
                       ("ssm_c_re", gl["ssm_c_re"]), ("ssm_c_im", gl["ssm_c_im"]), ("ssm_d", gl["ssm_d"]),
                       ("ffn2_norm", gl["ffn2_norm"][0])):
            small[k].append(val)
    small_local = {k: jnp.stack(vals) for k, vals in small.items() if k != "final_norm"}
    small_local["final_norm"] = g_final[0]
    small_local["meta_tokens"] = dmeta_local
    like = dict(small_local)
    g_small = _sum_devices(_all_gather_devices(_pack_small(small_local), "gather_small_grads"), "sum_small_grads")
    g_small_tree = _unpack_small(g_small, like)
    d4 = d // N_CHIPS
    chip = 2 * xi + yi
    g_meta = lax.dynamic_slice_in_dim(g_small_tree["meta_tokens"], chip * d4, d4, axis=1)

    reduced = []
    for l in range(depth):
        mine, other = [], []
        for red in reds[l]:
            halves, sibling_halves = red.end()
            mine += halves
            other += sibling_halves
        reduced.append((mine, other))

    g_out, delta, new_m, new_v = {}, {}, {}, {}
    for i, k in enumerate(BIG):
        flip = (lambda t: jnp.swapaxes(t, 1, 2)) if k in TRANSPOSED else (lambda t: t)
        outs = _adamw_layers(
            flip(w[k]), flip(m[k]), flip(v[k]), [reduced[l][0][i] for l in range(depth)],
            [reduced[l][1][i] for l in range(depth)], pos, "adamw_" + k)
        g_out[k], delta[k], new_m[k], new_v[k] = [flip(t) for t in outs]
    g_small_tree["meta_tokens"] = g_meta
    for k in SMALL + ["meta_tokens"]:
        shape = w[k].shape if w[k].ndim > 1 else (1,) + w[k].shape
        outs = _adamw_whole(w[k].reshape(shape), g_small_tree[k].reshape(shape), m[k].reshape(shape),
                            v[k].reshape(shape), "adamw_" + k)
        g_out[k] = g_small_tree[k]
        delta[k], new_m[k], new_v[k] = [t.reshape(w[k].shape) for t in outs]

    return (loss, grad_x, *[g_out[k] for k in WEIGHTS], *[delta[k] for k in WEIGHTS],
            *[new_m[k] for k in WEIGHTS], *[new_v[k] for k in WEIGHTS])
```

```python
import functools
import math

import jax
import jax.numpy as jnp
from jax import lax
from jax.experimental import pallas as pl
from jax.experimental.pallas import tpu as pltpu

F32 = jnp.float32
BF16 = jnp.bfloat16

N_META = 16
HEAD_DIM = 64
N_Q_HEADS = 8
N_KV_HEADS = 2
Q_PER_KV = N_Q_HEADS // N_KV_HEADS
ATTN_WIDTH = N_Q_HEADS * HEAD_DIM
KV_WIDTH = N_KV_HEADS * HEAD_DIM
BLOCK = 128
PAD_FRONT = BLOCK - N_META
ROPE_THETA = 500000.0
ROT_DIM = HEAD_DIM // 4
SSM_GROUP = 16
SSM_STATE = 64
GROUPS_PER_TILE = 4
TILE_STATES = GROUPS_PER_TILE * SSM_STATE
LANES = 128
SUBLANES = 8
MXU_DIM = 256
EPS = 1e-6
NEG_INF = -1e30
N_CHIPS = 4

ADAM_LR = 0.001
ADAM_B1 = 0.9
ADAM_B2 = 0.999
ADAM_EPS = 1e-08
ADAM_WD = 0.01
ADAM_STEP = 10

VMEM_LIMIT = 56 * 1024 * 1024
MESH = pl.DeviceIdType.MESH


def _cparams(sem=None):
    return pltpu.CompilerParams(dimension_semantics=sem, vmem_limit_bytes=VMEM_LIMIT)


def _row_tile(rows, limit=512):
    best = None
    for t in range(128, limit + 1, 128):
        if rows % t == 0:
            best = t
    assert best is not None, rows
    return best


def _div_tile(rows, row_bytes, max_bytes=1 << 20, mult=8):
    best = None
    for t in range(mult, rows + 1, mult):
        if rows % t == 0 and t * row_bytes <= max_bytes:
            best = t
    if best is None:
        best = rows
    return best


def _dot(a, b, mode):
    if mode == "nn":
        dims = (((1,), (0,)), ((), ()))
    elif mode == "nt":
        dims = (((1,), (1,)), ((), ()))
    else:
        dims = (((0,), (0,)), ((), ()))
    return lax.dot_general(a.astype(BF16), b.astype(BF16), dims, preferred_element_type=F32)


def _sigmoid(x):
    return 1.0 / (1.0 + jnp.exp(-x))


_GELU_C = math.sqrt(2.0 / math.pi)


def _gelu(x):
    return 0.5 * x * (1.0 + jnp.tanh(_GELU_C * (x + 0.044715 * x * x * x)))


def _gelu_grad(x):
    t = jnp.tanh(_GELU_C * (x + 0.044715 * x * x * x))
    return 0.5 * (1.0 + t) + 0.5 * x * (1.0 - t * t) * _GELU_C * (1.0 + 3.0 * 0.044715 * x * x)


class _Comm:
    def __init__(self, tag, ins, out_shapes, sems, start, mid, finish):
        self.tag, self.ins, self.out_shapes, self.sems = tag, list(ins), list(out_shapes), list(sems)
        self.start, self.mid, self.finish = start, mid, finish


HBM_SPEC = pl.BlockSpec(memory_space=pltpu.HBM)
MID_NUM, MID_DEN = 4, 5


def _hosted_call(body, comm, *, out_shape, grid, in_specs, out_specs, scratch_shapes, sem, name, args):
    out_shape, in_specs, out_specs = list(out_shape), list(in_specs), list(out_specs)
    scratch_shapes = list(scratch_shapes)
    if comm is None:
        res = pl.pallas_call(
            body, out_shape=out_shape, grid=grid, in_specs=in_specs, out_specs=out_specs,
            scratch_shapes=scratch_shapes, compiler_params=_cparams(sem), name=name)(*args)
        return list(res), []
    n_in, n_out, n_sc = len(args), len(out_shape), len(scratch_shapes)
    nci, nco = len(comm.ins), len(comm.out_shapes)
    total = math.prod(grid)

    def wrapped(*refs):
        in_refs, cin = refs[:n_in], refs[n_in:n_in + nci]
        o0 = n_in + nci
        out_refs, cout = refs[o0:o0 + n_out], refs[o0 + n_out:o0 + n_out + nco]
        s0 = o0 + n_out + nco
        sc, csem = refs[s0:s0 + n_sc], refs[s0 + n_sc:]
        lin = 0
        for dim, size in enumerate(grid):
            lin = lin * size + pl.program_id(dim)

        @pl.when(lin == 0)
        def _():
            comm.start(cin, cout, csem)

        if comm.mid is not None:
            @pl.when(lin == (total * MID_NUM) // MID_DEN)
            def _():
                comm.mid(cin, cout, csem)

        body(*in_refs, *out_refs, *sc)

        @pl.when(lin == total - 1)
        def _():
            comm.finish(cin, cout, csem)

    res = pl.pallas_call(
        wrapped, out_shape=out_shape + comm.out_shapes, grid=grid,
        in_specs=in_specs + [HBM_SPEC] * nci, out_specs=out_specs + [HBM_SPEC] * nco,
        scratch_shapes=scratch_shapes + comm.sems,
        compiler_params=_cparams(("arbitrary",) * len(grid)), name=name + "_" + comm.tag)(*args, *comm.ins)
    return list(res[:n_out]), list(res[n_out:])


def _matmul(name, grid, k_axis, ins, in_specs, pairs, acc_shapes, epilogue, out_shapes, out_specs, sem, comm=None):
    n_in, n_out, n_acc = len(ins), len(out_shapes), len(acc_shapes)

    def body(*refs):
        in_refs = refs[:n_in]
        out_refs = refs[n_in:n_in + n_out]
        acc_refs = refs[n_in + n_out:]
        if k_axis is None:
            accs = [None] * n_acc
            for ia, ib, mode, iacc in pairs:
                d = _dot(in_refs[ia][...], in_refs[ib][...], mode)
                accs[iacc] = d if accs[iacc] is None else accs[iacc] + d
            epilogue(accs, in_refs, out_refs)
            return
        k = pl.program_id(k_axis)

        @pl.when(k == 0)
        def _():
            for r in acc_refs:
                r[...] = jnp.zeros_like(r)

        for ia, ib, mode, iacc in pairs:
            acc_refs[iacc][...] += _dot(in_refs[ia][...], in_refs[ib][...], mode)

        @pl.when(k == pl.num_programs(k_axis) - 1)
        def _():
            epilogue([r[...] for r in acc_refs], in_refs, out_refs)

    scratch = [] if k_axis is None else [pltpu.VMEM(s, F32) for s in acc_shapes]
    outs, couts = _hosted_call(
        body, comm, out_shape=out_shapes, grid=grid, in_specs=in_specs, out_specs=out_specs,
        scratch_shapes=scratch, sem=sem, name=name, args=ins)
    return outs if comm is None else (outs, couts)


def _rms_math(x, g):
    r = lax.rsqrt(jnp.mean(x * x, axis=-1, keepdims=True) + EPS)
    return (x * r * g).astype(BF16)


def _rms_fwd(h, g, name):
    lp, d = h.shape
    tm = _row_tile(lp)

    def body(h_ref, g_ref, n_ref):
        n_ref[...] = _rms_math(h_ref[...], g_ref[...])

    return pl.pallas_call(
        body, out_shape=jax.ShapeDtypeStruct((lp, d), BF16), grid=(lp // tm,),
        in_specs=[pl.BlockSpec((tm, d), lambda i: (i, 0)), pl.BlockSpec((1, d), lambda i: (0, 0))],
        out_specs=pl.BlockSpec((tm, d), lambda i: (i, 0)),
        compiler_params=_cparams(("parallel",)), name=name)(h, g.reshape(1, d))


def _rms_bwd_math(dn, x, g):
    r = lax.rsqrt(jnp.mean(x * x, axis=-1, keepdims=True) + EPS)
    xh = x * r
    dxh = dn * g
    dx = r * (dxh - xh * jnp.mean(dxh * xh, axis=-1, keepdims=True))
    return dx, dn * xh


def _ffn_up(n, wgt, wut, comm=None):
    lp, d = n.shape
    fp = wgt.shape[1]
    tm = _row_tile(lp)

    def up_body(n_ref, wg_ref, wu_ref, a_ref, b_ref, s_ref):
        x = n_ref[...]
        for jc in range(N_CHIPS):
            cols = slice(jc * fp, (jc + 1) * fp)
            a = _dot(x, wg_ref[jc], "nt")
            b = _dot(x, wu_ref[jc], "nt")
            a_ref[:, cols] = a.astype(BF16)
            b_ref[:, cols] = b.astype(BF16)
            s_ref[:, cols] = (a * _sigmoid(a) * b).astype(BF16)

    ff = N_CHIPS * fp
    act = jax.ShapeDtypeStruct((lp, ff), BF16)
    act_tile = pl.BlockSpec((tm, ff), lambda i: (i, 0))
    w_spec = pl.BlockSpec((N_CHIPS, fp, d), lambda i: (0, 0, 0))
    outs, couts = _hosted_call(
        up_body, comm, out_shape=[act, act, act], grid=(lp // tm,),
        in_specs=[pl.BlockSpec((tm, d), lambda i: (i, 0)), w_spec, w_spec],
        out_specs=[act_tile] * 3, scratch_shapes=[], sem=("parallel",), name="ffn_up", args=(n, wgt, wut))
    return (*outs, n), couts


def _residual_outputs(h_new, in_refs, out_refs, gain_at):
    out_refs[0][...] = h_new
    if gain_at is not None:
        out_refs[1][...] = _rms_math(h_new, in_refs[gain_at][...])


def _residual_specs(lp, d, tm, next_gain):
    row = pl.BlockSpec((tm, d), lambda i: (i, 0))
    shapes, specs = [jax.ShapeDtypeStruct((lp, d), F32)], [row]
    extra_in, extra_specs = [], []
    if next_gain is not None:
        shapes.append(jax.ShapeDtypeStruct((lp, d), BF16))
        specs.append(row)
        extra_in, extra_specs = [next_gain.reshape(1, d)], [pl.BlockSpec((1, d), lambda i: (0, 0))]
    return shapes, specs, extra_in, extra_specs


def _ffn_down(s, wd, h, next_gain, comm=None):
    lp, d = h.shape
    ff = s.shape[1]
    tm = _row_tile(lp)
    shapes, specs, extra_in, extra_specs = _residual_specs(lp, d, tm, next_gain)

    def down_epi(accs, in_refs, out_refs):
        _residual_outputs(in_refs[2][...] + 0.5 * accs[0], in_refs, out_refs, 3 if extra_in else None)

    res = _matmul(
        "ffn_down", (lp // tm,), None, [s, wd.reshape(ff, d), h] + extra_in,
        [pl.BlockSpec((tm, ff), lambda i: (i, 0)), pl.BlockSpec((ff, d), lambda i: (0, 0)),
         pl.BlockSpec((tm, d), lambda i: (i, 0))] + extra_specs,
        [(0, 1, "nn", 0)], [(tm, d)], down_epi, shapes, specs, ("parallel",), comm)
    return (res, []) if comm is None else res


def _tn_tiles(lp):
    return _row_tile(lp, 1408)


def _ffn_bwd(dh_pair, h_in, gain, wgt, wut, wd, f4, saved, pos, comm=None):
    dh, dhb = dh_pair
    a, b, s, n = saved
    lp, d = h_in.shape
    fp = wgt.shape[1]
    ff = N_CHIPS * fp
    tm = _row_tile(lp)
    ni = lp // tm
    tk = _tn_tiles(lp)
    nk = lp // tk

    def ds_body(dh_ref, wd_ref, a_ref, b_ref, da_ref, db_ref):
        x = dh_ref[...]
        for jc in range(N_CHIPS):
            cols = slice(jc * fp, (jc + 1) * fp)
            ds = 0.5 * _dot(x, wd_ref[jc], "nt")
            av = a_ref[:, cols].astype(F32)
            bv = b_ref[:, cols].astype(F32)
            sg = _sigmoid(av)
            da_ref[:, cols] = (ds * bv * sg * (1.0 + av * (1.0 - sg))).astype(BF16)
            db_ref[:, cols] = (ds * av * sg).astype(BF16)

    act = jax.ShapeDtypeStruct((lp, ff), BF16)
    act_tile = pl.BlockSpec((tm, ff), lambda i: (i, 0))
    (da, db), couts = _hosted_call(
        ds_body, comm, out_shape=[act, act], grid=(ni,),
        in_specs=[pl.BlockSpec((tm, d), lambda i: (i, 0)), pl.BlockSpec((N_CHIPS, fp, d), lambda i: (0, 0, 0)),
                  act_tile, act_tile],
        out_specs=[act_tile, act_tile], scratch_shapes=[], sem=("parallel",), name="ffn_bwd_ds",
        args=(dhb, wd, a, b))

    dw_shape = jax.ShapeDtypeStruct((N_CHIPS, f4, d), F32)
    dw_spec = pl.BlockSpec((None, f4, d), lambda j, k: (j, 0, 0))
    in_col = pl.BlockSpec((tk, fp), lambda j, k: (k, j))
    in_row = pl.BlockSpec((tk, d), lambda j, k: (k, 0))

    half_shape = jax.ShapeDtypeStruct((N_CHIPS, f4 // 2, d), BF16)
    half_spec = pl.BlockSpec((None, f4 // 2, d), lambda j, k: (j, 0, 0))

    def dwd_epi(accs, in_refs, out_refs):
        dw = 0.5 * accs[0]
        out_refs[0][...] = dw[:f4]
        out_refs[1][...] = _sibling_half(dw, f4)

    dwd, dwd_other = _matmul(
        "ffn_dwd", (N_CHIPS, nk), 1, [s, dhb], [in_col, in_row],
        [(0, 1, "tn", 0)], [(fp, d)], dwd_epi, [dw_shape, half_shape], [dw_spec, half_spec],
        ("arbitrary", "arbitrary"))

    def dwgu_epi(accs, in_refs, out_refs):
        for i, acc in enumerate(accs):
            out_refs[i][...] = acc[:f4]
            out_refs[2 + i][...] = _sibling_half(acc, f4)

    red_down = _Reduction([dwd], [dwd_other], pos, "ffn_d")
    (dwg, dwu, dwg_other, dwu_other), red_down.got = _matmul(
        "ffn_dwgu", (N_CHIPS, nk), 1, [n, da, db], [in_row, in_col, in_col],
        [(1, 0, "tn", 0), (2, 0, "tn", 1)], [(fp, d)] * 2, dwgu_epi,
        [dw_shape, dw_shape, half_shape, half_shape], [dw_spec, dw_spec, half_spec, half_spec],
        ("arbitrary", "arbitrary"), red_down.comm())

    def dn_epi(accs, in_refs, out_refs):
        i = pl.program_id(0)
        dx, dgrow = _rms_bwd_math(accs[0], in_refs[5][...], in_refs[6][...])
        dh_new = in_refs[4][...] + dx
        out_refs[0][...] = dh_new
        out_refs[2][...] = dh_new.astype(BF16)

        @pl.when(i == 0)
        def _():
            out_refs[1][...] = jnp.zeros_like(out_refs[1])

        out_refs[1][...] += jnp.sum(dgrow, axis=0, keepdims=True)

    red = _Reduction([dwg, dwu], [dwg_other, dwu_other], pos, "ffn_gu")
    row_spec = pl.BlockSpec((tm, d), lambda i: (i, 0))
    act_spec = pl.BlockSpec((tm, ff), lambda i: (i, 0))
    w_spec = pl.BlockSpec((ff, d), lambda i: (0, 0))
    one_spec = pl.BlockSpec((1, d), lambda i: (0, 0))
    (dh_in, dgain, dh_in_b), red.got = _matmul(
        "ffn_bwd_dn", (ni,), None, [da, wgt.reshape(ff, d), db, wut.reshape(ff, d), dh, h_in, gain.reshape(1, d)],
        [act_spec, w_spec, act_spec, w_spec, row_spec, row_spec, one_spec],
        [(0, 1, "nn", 0), (2, 3, "nn", 0)], [(tm, d)], dn_epi,
        [jax.ShapeDtypeStruct((lp, d), F32), jax.ShapeDtypeStruct((1, d), F32), jax.ShapeDtypeStruct((lp, d), BF16)],
        [row_spec, one_spec, row_spec], ("arbitrary",), red.comm())
    return (dh_in, dh_in_b), dgain, [red, red_down], couts


def _rope_tables(lp):
    pos = jnp.arange(lp, dtype=F32) - float(PAD_FRONT)
    inv_freq = ROPE_THETA ** (-jnp.arange(0, ROT_DIM, 2, dtype=F32) / ROT_DIM)
    ang = pos[:, None] * inv_freq[None, :]
    cos, sin = jnp.cos(ang), jnp.sin(ang)
    half = ROT_DIM // 2
    ones = jnp.ones((lp, HEAD_DIM - ROT_DIM), F32)
    zeros_h = jnp.zeros((lp, half), F32)
    zeros_r = jnp.zeros((lp, HEAD_DIM - ROT_DIM), F32)
    c = jnp.concatenate([cos, cos, ones], axis=1)
    s1 = jnp.concatenate([-sin, zeros_h, zeros_r], axis=1)
    s2 = jnp.concatenate([zeros_h, sin, zeros_r], axis=1)
    reps = LANES // HEAD_DIM
    return jnp.stack([jnp.tile(c, (1, reps)), jnp.tile(s1, (1, reps)), jnp.tile(s2, (1, reps))])


def _rope(x, c, s1, s2):
    half = ROT_DIM // 2
    outs = []
    for ch in range(x.shape[1] // LANES):
        xc = x[:, ch * LANES:(ch + 1) * LANES]
        outs.append(xc * c + pltpu.roll(xc, LANES - half, 1) * s1 + pltpu.roll(xc, half, 1) * s2)
    return outs[0] if len(outs) == 1 else jnp.concatenate(outs, axis=1)


def _rope_t(dy, c, s1, s2):
    half = ROT_DIM // 2
    outs = []
    for ch in range(dy.shape[1] // LANES):
        dc = dy[:, ch * LANES:(ch + 1) * LANES]
        outs.append(dc * c + pltpu.roll(dc * s1, half, 1) + pltpu.roll(dc * s2, LANES - half, 1))
    return outs[0] if len(outs) == 1 else jnp.concatenate(outs, axis=1)


def _in_proj(n, w_in, tabs, ssm_w, comm=None):
    lp, d = n.shape
    inw = w_in.shape[0]
    tm = _row_tile(lp)
    o1 = ATTN_WIDTH
    o2 = o1 + KV_WIDTH
    o3 = o2 + KV_WIDTH
    o4 = o3 + ssm_w
    o5 = o4 + d

    def epi(accs, in_refs, out_refs):
        z = accs[0]
        c, s1, s2 = in_refs[2][0], in_refs[2][1], in_refs[2][2]
        out_refs[0][...] = _rope(z[:, :o1], c, s1, s2).astype(BF16)
        out_refs[1][...] = _rope(z[:, o1:o2], c, s1, s2).astype(BF16)
        out_refs[2][...] = z[:, o2:o3].astype(BF16)
        out_refs[3][...] = z[:, o3:o4]
        out_refs[4][...] = z[:, o4:o5]
        out_refs[5][...] = z[:, o5:]

    def rs(w, dt):
        return jax.ShapeDtypeStruct((lp, w), dt), pl.BlockSpec((tm, w), lambda i: (i, 0))

    shapes, specs = zip(rs(o1, BF16), rs(KV_WIDTH, BF16), rs(KV_WIDTH, BF16), rs(ssm_w, F32), rs(d, F32), rs(d, F32))
    res = _matmul(
        "mix_in_proj", (lp // tm,), None, [n, w_in, tabs],
        [pl.BlockSpec((tm, d), lambda i: (i, 0)), pl.BlockSpec((inw, d), lambda i: (0, 0)),
         pl.BlockSpec((3, tm, LANES), lambda i: (0, i, 0))],
        [(0, 1, "nt", 0)], [(tm, inw)], epi, list(shapes), list(specs), ("parallel",), comm)
    return (res, []) if comm is None else res


def _attn_mask(b):
    rows = lax.broadcasted_iota(jnp.int32, (BLOCK, 3 * BLOCK), 0)
    cols = lax.broadcasted_iota(jnp.int32, (BLOCK, 3 * BLOCK), 1)
    qpos = b * BLOCK + rows - PAD_FRONT
    kpos = (b - 1) * BLOCK + cols - PAD_FRONT
    dist = qpos - kpos
    band = (cols < 2 * BLOCK) & (kpos >= N_META) & (dist >= 0) & (dist < BLOCK)
    mrow = cols - 2 * BLOCK
    meta = (mrow >= PAD_FRONT) & ((mrow - PAD_FRONT) <= qpos)
    return band | meta


def _attn_probs(qh, kk, mask, sink):
    s = _dot(qh, kk, "nt") * (HEAD_DIM ** -0.5)
    s = jnp.where(mask, s, NEG_INF)
    m = jnp.maximum(jnp.max(s, axis=-1, keepdims=True), sink)
    e = jnp.exp(s - m)
    es = jnp.exp(sink - m)
    z = jnp.sum(e, axis=-1, keepdims=True) + es
    inv = 1.0 / z
    return e * inv, es * inv


def _head(ref_or_val, h):
    return ref_or_val[:, h * HEAD_DIM:(h + 1) * HEAD_DIM]


def _attn_fwd(q, k, v, sinks, comm=None):
    lp = q.shape[0]
    nb = lp // BLOCK

    def body(sink_ref, q_ref, kp_ref, kc_ref, km_ref, vp_ref, vc_ref, vm_ref, o_ref):
        b = pl.program_id(0)
        mask = _attn_mask(b)
        for hk in range(N_KV_HEADS):
            kk = jnp.concatenate([_head(kp_ref, hk), _head(kc_ref, hk), _head(km_ref, hk)], axis=0)
            vv = jnp.concatenate([_head(vp_ref, hk), _head(vc_ref, hk), _head(vm_ref, hk)], axis=0)
            for g in range(Q_PER_KV):
                h = hk * Q_PER_KV + g
                p, _ = _attn_probs(_head(q_ref, h), kk, mask, sink_ref[h])
                o_ref[:, h * HEAD_DIM:(h + 1) * HEAD_DIM] = _dot(p, vv, "nn").astype(BF16)

    cur = lambda b: (b, 0)
    prev = lambda b: (jnp.maximum(b - 1, 0), 0)
    first = lambda b: (0, 0)
    kvs = lambda f: pl.BlockSpec((BLOCK, KV_WIDTH), f)
    (o,), couts = _hosted_call(
        body, comm, out_shape=[jax.ShapeDtypeStruct((lp, ATTN_WIDTH), BF16)], grid=(nb,),
        in_specs=[pl.BlockSpec(memory_space=pltpu.SMEM), pl.BlockSpec((BLOCK, ATTN_WIDTH), cur),
                  kvs(prev), kvs(cur), kvs(first), kvs(prev), kvs(cur), kvs(first)],
        out_specs=[pl.BlockSpec((BLOCK, ATTN_WIDTH), cur)], scratch_shapes=[],
        sem=("parallel",), name="attn_fwd", args=(sinks, q, k, k, k, v, v, v))
    return o, couts


def _attn_bwd(q, k, v, do, sinks, tabs, comm=None):
    lp = q.shape[0]
    nb = lp // BLOCK
    scale = HEAD_DIM ** -0.5

    def body(sink_ref, q_ref, do_ref, kp_ref, kc_ref, km_ref, vp_ref, vc_ref, vm_ref, tq_ref, tk_ref, t0_ref,
             dq_ref, dk_ref, dv_ref, dkm_ref, dvm_ref, dsink_ref,
             dq_s, dkk_s, dvv_s, ck_s, cv_s, mk_s, mv_s):
        b = pl.program_id(0)

        @pl.when(b == 0)
        def _():
            for r in (ck_s, cv_s, mk_s, mv_s, dsink_ref):
                r[...] = jnp.zeros_like(r)

        @pl.when(b < nb)
        def _():
            mask = _attn_mask(b)
            for hk in range(N_KV_HEADS):
                kk = jnp.concatenate([_head(kp_ref, hk), _head(kc_ref, hk), _head(km_ref, hk)], axis=0)
                vv = jnp.concatenate([_head(vp_ref, hk), _head(vc_ref, hk), _head(vm_ref, hk)], axis=0)
                dkk = jnp.zeros((3 * BLOCK, HEAD_DIM), F32)
                dvv = jnp.zeros((3 * BLOCK, HEAD_DIM), F32)
                for g in range(Q_PER_KV):
                    h = hk * Q_PER_KV + g
                    qh = _head(q_ref, h)
                    doh = _head(do_ref, h)
                    p, ps = _attn_probs(qh, kk, mask, sink_ref[h])
                    dp = _dot(doh, vv, "nt")
                    delta = jnp.sum(p * dp, axis=-1, keepdims=True)
                    ds = (p * (dp - delta)).astype(BF16)
                    dsink_ref[h:h + 1, :] += jnp.zeros((1, LANES), F32) - jnp.sum(ps * delta)
                    dq_s[:, h * HEAD_DIM:(h + 1) * HEAD_DIM] = _dot(ds, kk, "nn") * scale
                    dkk = dkk + _dot(ds, qh, "tn") * scale
                    dvv = dvv + _dot(p, doh, "tn")
                dkk_s[:, hk * HEAD_DIM:(hk + 1) * HEAD_DIM] = dkk
                dvv_s[:, hk * HEAD_DIM:(hk + 1) * HEAD_DIM] = dvv
            dq_ref[...] = _rope_t(dq_s[...], tq_ref[0], tq_ref[1], tq_ref[2])
            dk_ref[...] = _rope_t(ck_s[...] + dkk_s[0:BLOCK, :], tk_ref[0], tk_ref[1], tk_ref[2])
            dv_ref[...] = cv_s[...] + dvv_s[0:BLOCK, :]
            ck_s[...] = dkk_s[BLOCK:2 * BLOCK, :]
            cv_s[...] = dvv_s[BLOCK:2 * BLOCK, :]
            mk_s[...] += dkk_s[2 * BLOCK:, :]
            mv_s[...] += dvv_s[2 * BLOCK:, :]

        @pl.when(b == nb)
        def _():
            dk_ref[...] = _rope_t(ck_s[...], tk_ref[0], tk_ref[1], tk_ref[2])
            dv_ref[...] = cv_s[...]
            dkm_ref[...] = _rope_t(mk_s[...], t0_ref[0], t0_ref[1], t0_ref[2])
            dvm_ref[...] = mv_s[...]

    cur = lambda b: (jnp.minimum(b, nb - 1), 0)
    prev = lambda b: (jnp.clip(b - 1, 0, nb - 1), 0)
    first = lambda b: (0, 0)
    kvs = lambda f: pl.BlockSpec((BLOCK, KV_WIDTH), f)
    tab = lambda f: pl.BlockSpec((3, BLOCK, LANES), lambda b: (0,) + f(b)[:1] + (0,))
    kv_out = lambda b: (jnp.maximum(b - 1, 0), 0)
    return _hosted_call(
        body, comm,
        out_shape=[jax.ShapeDtypeStruct((lp, ATTN_WIDTH), F32), jax.ShapeDtypeStruct((lp, KV_WIDTH), F32),
                   jax.ShapeDtypeStruct((lp, KV_WIDTH), F32), jax.ShapeDtypeStruct((BLOCK, KV_WIDTH), F32),
                   jax.ShapeDtypeStruct((BLOCK, KV_WIDTH), F32), jax.ShapeDtypeStruct((N_Q_HEADS, LANES), F32)],
        grid=(nb + 1,),
        in_specs=[pl.BlockSpec(memory_space=pltpu.SMEM), pl.BlockSpec((BLOCK, ATTN_WIDTH), cur),
                  pl.BlockSpec((BLOCK, ATTN_WIDTH), cur),
                  kvs(prev), kvs(cur), kvs(first), kvs(prev), kvs(cur), kvs(first),
                  tab(cur), tab(kv_out), tab(first)],
        out_specs=[pl.BlockSpec((BLOCK, ATTN_WIDTH), cur), kvs(kv_out), kvs(kv_out), kvs(first), kvs(first),
                   pl.BlockSpec((N_Q_HEADS, LANES), first)],
        scratch_shapes=[pltpu.VMEM((BLOCK, ATTN_WIDTH), F32), pltpu.VMEM((3 * BLOCK, KV_WIDTH), F32),
                        pltpu.VMEM((3 * BLOCK, KV_WIDTH), F32), pltpu.VMEM((BLOCK, KV_WIDTH), F32),
                        pltpu.VMEM((BLOCK, KV_WIDTH), F32), pltpu.VMEM((BLOCK, KV_WIDTH), F32),
                        pltpu.VMEM((BLOCK, KV_WIDTH), F32)],
        sem=("arbitrary",), name="attn_bwd", args=(sinks, q, do, k, k, k, v, v, v, tabs, tabs, tabs))


def _cmul(ar, ai, br, bi):
    return ar * br - ai * bi, ar * bi + ai * br


def _cpow(lr, li, n):
    rr = ri = None
    br, bi = lr, li
    while n:
        if n & 1:
            rr, ri = (br, bi) if rr is None else _cmul(rr, ri, br, bi)
        n >>= 1
        if n:
            br, bi = _cmul(br, bi, br, bi)
    return rr, ri


def _shift_rows(x, d, reverse):
    rows = lax.broadcasted_iota(jnp.int32, x.shape, 0)
    if not reverse:
        return jnp.where(rows >= d, pltpu.roll(x, d, 0), 0.0)
    return jnp.where(rows < SUBLANES - d, pltpu.roll(x, SUBLANES - d, 0), 0.0)


def _sublane_powers(mr, mi, reverse):
    rows = lax.broadcasted_iota(jnp.int32, mr.shape, 0)
    e = SUBLANES - 1 - rows if reverse else rows
    pr, pi = jnp.ones_like(mr), jnp.zeros_like(mr)
    br, bi = mr, mi
    for d in (1, 2, 4):
        tr, ti = _cmul(pr, pi, br, bi)
        on = (e & d) != 0
        pr, pi = jnp.where(on, tr, pr), jnp.where(on, ti, pi)
        if d < 4:
            br, bi = _cmul(br, bi, br, bi)
    return pr, pi


def _inclusive_prefix(er, ei, mr, mi, reverse):
    ir, ii, pr, pi = er, ei, mr, mi
    for d in (1, 2, 4):
        tr, ti = _cmul(pr, pi, _shift_rows(ir, d, reverse), _shift_rows(ii, d, reverse))
        ir, ii = ir + tr, ii + ti
        if d < 4:
            pr, pi = _cmul(pr, pi, pr, pi)
    return ir, ii


def _chain_rows(a, t, seg):
    return pl.ds(a * SUBLANES * seg + t, SUBLANES, stride=seg)


def _seg_scan(xr_ref, xi_ref, lam, seg, nchain, reverse, store, init, extra=None):
    nt = len(lam)
    acc0 = () if extra is None else extra[1]

    def step(i, carry):
        hs, acc = carry
        t = seg - 1 - i if reverse else i
        out = []
        for a in range(nchain):
            sl = _chain_rows(a, t, seg)
            for j in range(nt):
                lr, li = lam[j]
                k = 2 * (a * nt + j)
                hr, hi = hs[k], hs[k + 1]
                nr = lr * hr - li * hi + xr_ref[j, sl, :]
                ni = lr * hi + li * hr + xi_ref[j, sl, :]
                if store:
                    xr_ref[j, sl, :] = nr
                    xi_ref[j, sl, :] = ni
                if extra is not None:
                    acc = extra[0](t, a, j, nr, ni, acc)
                out += [nr, ni]
        return tuple(out), acc

    return lax.fori_loop(0, seg, step, (tuple(init), acc0))


def _ssm_scan(xr_ref, xi_ref, lam, seg, nchain, reverse, extra=None):
    nt = len(lam)
    zero = [jnp.zeros((SUBLANES, LANES), F32)] * (2 * nt * nchain)
    ends, _ = _seg_scan(xr_ref, xi_ref, lam, seg, nchain, reverse, False, zero)
    init = [None] * (2 * nt * nchain)
    last = 0 if reverse else SUBLANES - 1
    for j in range(nt):
        mr, mi = _cpow(lam[j][0], lam[j][1], seg)
        m8r, m8i = _cpow(mr, mi, SUBLANES)
        pwr, pwi = _sublane_powers(mr, mi, reverse)
        gr = gi = jnp.zeros((SUBLANES, LANES), F32)
        for a in (reversed(range(nchain)) if reverse else range(nchain)):
            k = 2 * (a * nt + j)
            incr, inci = _inclusive_prefix(ends[k], ends[k + 1], mr, mi, reverse)
            tr, ti = _cmul(pwr, pwi, gr, gi)
            init[k] = _shift_rows(incr, 1, reverse) + tr
            init[k + 1] = _shift_rows(inci, 1, reverse) + ti
            g2r, g2i = _cmul(m8r, m8i, gr, gi)
            gr = g2r + jnp.broadcast_to(incr[last:last + 1, :], gr.shape)
            gi = g2i + jnp.broadcast_to(inci[last:last + 1, :], gi.shape)
    _, acc = _seg_scan(xr_ref, xi_ref, lam, seg, nchain, reverse, True, init, extra)
    return acc


def _diag_mask():
    steps = LANES // SSM_GROUP // GROUPS_PER_TILE
    return (jnp.eye(steps, dtype=F32)[:, None, :, None] * jnp.eye(GROUPS_PER_TILE, dtype=F32)[None, :, None, :])


def _ssm_tables(bb_re, bb_im, c_re, c_im):
    g = bb_re.shape[0]
    nt = g // GROUPS_PER_TILE
    steps = LANES // SSM_GROUP // GROUPS_PER_TILE
    mask = _diag_mask()

    def b_tab(bb):
        x = bb.reshape(nt // steps, steps, GROUPS_PER_TILE, SSM_STATE, SSM_GROUP)
        x = jnp.transpose(x, (0, 1, 4, 2, 3))[:, :, None, None]
        m = jnp.transpose(mask, (0, 2, 3, 1))[None, :, :, :, None, :, None]
        return (x * m).reshape(nt, LANES, TILE_STATES)

    def c_tab(c):
        x = c.reshape(nt // steps, steps, GROUPS_PER_TILE, SSM_GROUP, SSM_STATE)
        x = jnp.transpose(x, (0, 1, 2, 4, 3))[:, :, :, :, None, None]
        m = mask[None, :, :, None, :, :, None]
        return (x * m).reshape(nt, TILE_STATES, LANES)

    return b_tab(bb_re), b_tab(bb_im), c_tab(c_re), c_tab(c_im)


def _ssm_untable_b(db, g):
    nt = g // GROUPS_PER_TILE
    steps = LANES // SSM_GROUP // GROUPS_PER_TILE
    x = db.reshape(nt // steps, steps, GROUPS_PER_TILE, SSM_STATE, steps, GROUPS_PER_TILE, SSM_GROUP)
    m = _diag_mask()[None, :, :, None, :, :, None]
    return jnp.sum(x * m, axis=(4, 5)).reshape(g, SSM_STATE, SSM_GROUP)


def _ssm_untable_c(dc, g):
    nt = g // GROUPS_PER_TILE
    steps = LANES // SSM_GROUP // GROUPS_PER_TILE
    x = dc.reshape(nt // steps, steps, steps, GROUPS_PER_TILE, SSM_GROUP, GROUPS_PER_TILE, SSM_STATE)
    m = jnp.transpose(_diag_mask(), (0, 2, 3, 1))[None, :, :, :, None, :, None]
    out = jnp.sum(x * m, axis=(2, 3))
    return jnp.transpose(out, (0, 1, 3, 2, 4)).reshape(g, SSM_GROUP, SSM_STATE)


def _lam_tiles(lam_ref):
    out = []
    for j in range(TILE_STATES // LANES):
        out.append(jnp.broadcast_to(lam_ref[:, j * LANES:(j + 1) * LANES], (SUBLANES, LANES)))
    return out


def _scan_chains(lp):
    for n in (4, 2, 1):
        if lp % (SUBLANES * n) == 0 and (lp // SUBLANES) % 16 == 0:
            return n
    raise ValueError(lp)


def _split_tiles(dst_ref, rows, val):
    for j in range(val.shape[1] // LANES):
        dst_ref[j, rows, :] = val[:, j * LANES:(j + 1) * LANES]


def _cat_tiles(src_ref, rows):
    njt = src_ref.shape[0]
    return jnp.concatenate([src_ref[j, rows, :] for j in range(njt)], axis=1).astype(BF16)


def _ssm_fwd(u, lam_re, lam_im, tb_re, tb_im, tc_re, tc_im, d_skip, comm=None):
    lp, w = u.shape
    nt = tb_re.shape[0]
    nchain = _scan_chains(lp)
    seg = lp // (SUBLANES * nchain)
    chunk = lp // SUBLANES
    njt = TILE_STATES // LANES

    def body(u_ref, lr_ref, li_ref, br_ref, bi_ref, cr_ref, ci_ref, d_ref, y_ref, yg_ref, xr, xi):
        t = pl.program_id(0)
        for s in range(SUBLANES):
            rs = pl.ds(s * chunk, chunk)
            ub = u_ref[rs, :].astype(BF16)
            _split_tiles(xr, rs, _dot(ub, br_ref[...], "nn"))
            _split_tiles(xi, rs, _dot(ub, bi_ref[...], "nn"))
        lrs, lis = _lam_tiles(lr_ref), _lam_tiles(li_ref)
        _ssm_scan(xr, xi, list(zip(lrs, lis)), seg, nchain, False)
        for s in range(SUBLANES):
            rs = pl.ds(s * chunk, chunk)
            y = _dot(_cat_tiles(xr, rs), cr_ref[...], "nn") - _dot(_cat_tiles(xi, rs), ci_ref[...], "nn")

            @pl.when(t % 2 == 0)
            def _():
                y_ref[rs, :] = y + d_ref[...] * u_ref[rs, :]

            @pl.when(t % 2 == 1)
            def _():
                total = y_ref[rs, :] + y
                y_ref[rs, :] = total
                yg_ref[rs, :] = _gelu(total).astype(BF16)

    blk = pl.BlockSpec((lp, LANES), lambda t: (0, t // 2))
    lam_spec = pl.BlockSpec((None, 1, TILE_STATES), lambda t: (t, 0, 0))
    b_spec = pl.BlockSpec((None, LANES, TILE_STATES), lambda t: (t, 0, 0))
    c_spec = pl.BlockSpec((None, TILE_STATES, LANES), lambda t: (t, 0, 0))
    (y, yg), couts = _hosted_call(
        body, comm, out_shape=[jax.ShapeDtypeStruct((lp, w), F32), jax.ShapeDtypeStruct((lp, w), BF16)], grid=(nt,),
        in_specs=[blk, lam_spec, lam_spec, b_spec, b_spec, c_spec, c_spec,
                  pl.BlockSpec((1, LANES), lambda t: (0, t // 2))],
        out_specs=[blk, blk],
        scratch_shapes=[pltpu.VMEM((njt, lp, LANES), F32), pltpu.VMEM((njt, lp, LANES), F32)],
        sem=("arbitrary",), name="ssm_fwd",
        args=(u, lam_re, lam_im, tb_re, tb_im, tc_re, tc_im, d_skip.reshape(1, w)))
    return (y, yg), couts


def _ssm_bwd(u, dy, lam_re, lam_im, tb_re, tb_im, tc_re, tc_im, d_skip, comm=None):
    lp, w = u.shape
    nt = tb_re.shape[0]
    nchain = _scan_chains(lp)
    seg = lp // (SUBLANES * nchain)
    chunk = lp // SUBLANES
    njt = TILE_STATES // LANES
    tbt_re, tbt_im = jnp.swapaxes(tb_re, 1, 2), jnp.swapaxes(tb_im, 1, 2)
    tct_re, tct_im = jnp.swapaxes(tc_re, 1, 2), jnp.swapaxes(tc_im, 1, 2)

    def body(u_ref, dy_ref, lr_ref, li_ref, br_ref, bi_ref, btr_ref, bti_ref, ctr_ref, cti_ref, d_ref,
             du_ref, dlr_ref, dli_ref, dbr_ref, dbi_ref, dcr_ref, dci_ref, dd_ref, hr, hi, ar, ai):
        t = pl.program_id(0)
        lrs, lis = _lam_tiles(lr_ref), _lam_tiles(li_ref)
        for s in range(SUBLANES):
            rs = pl.ds(s * chunk, chunk)
            ub = u_ref[rs, :].astype(BF16)
            dyb = dy_ref[rs, :].astype(BF16)
            _split_tiles(hr, rs, _dot(ub, br_ref[...], "nn"))
            _split_tiles(hi, rs, _dot(ub, bi_ref[...], "nn"))
            _split_tiles(ar, rs, _dot(dyb, ctr_ref[...], "nn"))
            _split_tiles(ai, rs, -_dot(dyb, cti_ref[...], "nn"))
        _ssm_scan(hr, hi, list(zip(lrs, lis)), seg, nchain, False)

        def dlam_step(tt, a, j, a_r, a_i, acc):
            sl = _chain_rows(a, jnp.maximum(tt - 1, 0), seg)
            p_r, p_i = hr[j, sl, :], hi[j, sl, :]
            acc = list(acc)
            acc[2 * j] = acc[2 * j] + jnp.where(tt > 0, a_r * p_r + a_i * p_i, 0.0)
            acc[2 * j + 1] = acc[2 * j + 1] + jnp.where(tt > 0, a_i * p_r - a_r * p_i, 0.0)
            return tuple(acc)

        zero = tuple([jnp.zeros((SUBLANES, LANES), F32)] * (2 * njt))
        conj = [(lr, -li) for lr, li in zip(lrs, lis)]
        acc = list(_ssm_scan(ar, ai, conj, seg, nchain, True, (dlam_step, zero)))
        row0 = lax.broadcasted_iota(jnp.int32, (SUBLANES, LANES), 0) == 0
        for j in range(njt):
            cs = slice(j * LANES, (j + 1) * LANES)
            for a in range(nchain):
                p_r = _shift_rows(hr[j, _chain_rows(a, seg - 1, seg), :], 1, False)
                p_i = _shift_rows(hi[j, _chain_rows(a, seg - 1, seg), :], 1, False)
                if a > 0:
                    before = pl.ds(a * SUBLANES * seg - 1, 1)
                    p_r = jnp.where(row0, jnp.broadcast_to(hr[j, before, :], p_r.shape), p_r)
                    p_i = jnp.where(row0, jnp.broadcast_to(hi[j, before, :], p_i.shape), p_i)
                a_r, a_i = ar[j, _chain_rows(a, 0, seg), :], ai[j, _chain_rows(a, 0, seg), :]
                acc[2 * j] = acc[2 * j] + a_r * p_r + a_i * p_i
                acc[2 * j + 1] = acc[2 * j + 1] + a_i * p_r - a_r * p_i
            dlr_ref[:, cs] = jnp.sum(acc[2 * j], axis=0, keepdims=True)
            dli_ref[:, cs] = jnp.sum(acc[2 * j + 1], axis=0, keepdims=True)

        dd = jnp.zeros((1, LANES), F32)
        for s in range(SUBLANES):
            rs = pl.ds(s * chunk, chunk)
            ub = u_ref[rs, :].astype(BF16)
            dyv = dy_ref[rs, :]
            dyb = dyv.astype(BF16)
            arb, aib = _cat_tiles(ar, rs), _cat_tiles(ai, rs)
            hrb, hib = _cat_tiles(hr, rs), _cat_tiles(hi, rs)
            du = _dot(arb, btr_ref[...], "nn") + _dot(aib, bti_ref[...], "nn")
            upd = [(dbr_ref, _dot(arb, ub, "tn")), (dbi_ref, _dot(aib, ub, "tn")),
                   (dcr_ref, _dot(dyb, hrb, "tn")), (dci_ref, -_dot(dyb, hib, "tn"))]
            for ref, val in upd:
                if s == 0:
                    ref[...] = val
                else:
                    ref[...] += val
            rows = lax.broadcasted_iota(jnp.int32, (chunk, LANES), 0) + s * chunk
            keep = rows >= PAD_FRONT
            dd = dd + jnp.sum(dyv * u_ref[rs, :], axis=0, keepdims=True)

            @pl.when(t % 2 == 0)
            def _():
                du_ref[rs, :] = jnp.where(keep, du + d_ref[...] * dyv, 0.0)

            @pl.when(t % 2 == 1)
            def _():
                du_ref[rs, :] += jnp.where(keep, du, 0.0)

        @pl.when(t % 2 == 0)
        def _():
            dd_ref[...] = dd

    blk = pl.BlockSpec((lp, LANES), lambda t: (0, t // 2))
    vec = pl.BlockSpec((1, LANES), lambda t: (0, t // 2))
    lam_spec = pl.BlockSpec((None, 1, TILE_STATES), lambda t: (t, 0, 0))
    b_spec = pl.BlockSpec((None, LANES, TILE_STATES), lambda t: (t, 0, 0))
    c_spec = pl.BlockSpec((None, TILE_STATES, LANES), lambda t: (t, 0, 0))
    lam_shape = jax.ShapeDtypeStruct((nt, 1, TILE_STATES), F32)
    bt_shape = jax.ShapeDtypeStruct((nt, TILE_STATES, LANES), F32)
    ct_shape = jax.ShapeDtypeStruct((nt, LANES, TILE_STATES), F32)
    st = pltpu.VMEM((njt, lp, LANES), F32)
    return _hosted_call(
        body, comm,
        out_shape=[jax.ShapeDtypeStruct((lp, w), F32), lam_shape, lam_shape, bt_shape, bt_shape, ct_shape, ct_shape,
                   jax.ShapeDtypeStruct((1, w), F32)],
        grid=(nt,),
        in_specs=[blk, blk, lam_spec, lam_spec, b_spec, b_spec, c_spec, c_spec, b_spec, b_spec, vec],
        out_specs=[blk, lam_spec, lam_spec, c_spec, c_spec, b_spec, b_spec, vec],
        scratch_shapes=[st, st, st, st], sem=("arbitrary",), name="ssm_bwd",
        args=(u, dy, lam_re, lam_im, tb_re, tb_im, tbt_re, tbt_im, tct_re, tct_im, d_skip.reshape(1, w)))


def _ssm_params(a_re, a_im, log_dt, b_re, b_im):
    dt = jnp.exp(log_dt)[:, None]
    mag = jnp.exp(a_re * dt)
    lb_re = mag * jnp.cos(a_im * dt)
    lb_im = mag * jnp.sin(a_im * dt)
    den = a_re * a_re + a_im * a_im
    num_re = lb_re - 1.0
    coef_re = (num_re * a_re + lb_im * a_im) / den
    coef_im = (lb_im * a_re - num_re * a_im) / den
    bb_re = coef_re[..., None] * b_re - coef_im[..., None] * b_im
    bb_im = coef_re[..., None] * b_im + coef_im[..., None] * b_re
    return lb_re, lb_im, bb_re, bb_im


def _merge_fwd(o, yg, ga, gs, w3t, comm=None):
    lp, d = ga.shape
    kw = w3t.shape[2]
    tm = _row_tile(lp)

    def epi(accs, in_refs, out_refs):
        attn, vv, gg = accs
        out_refs[0][...] = (_sigmoid(in_refs[5][...]) * attn
                            + _sigmoid(in_refs[6][...]) * (vv * _sigmoid(gg))).astype(BF16)

    wspec = lambda which: pl.BlockSpec((None, d, kw), lambda i: (which, 0, 0))
    rowspec = pl.BlockSpec((tm, d), lambda i: (i, 0))
    aspec = pl.BlockSpec((tm, kw), lambda i: (i, 0))
    res = _matmul(
        "merge_fwd", (lp // tm,), None, [o, yg, w3t, w3t, w3t, ga, gs],
        [aspec, aspec, wspec(0), wspec(1), wspec(2), rowspec, rowspec],
        [(0, 2, "nt", 0), (1, 3, "nt", 1), (1, 4, "nt", 2)], [(tm, d)] * 3, epi,
        [jax.ShapeDtypeStruct((lp, d), BF16)], [rowspec], ("parallel",), comm)
    return (res[0], []) if comm is None else (res[0][0], res[1])


def _out_proj(merged, w_out, h, next_gain, comm=None):
    lp, d = h.shape
    tm = _row_tile(lp)
    shapes, specs, extra_in, extra_specs = _residual_specs(lp, d, tm, next_gain)

    def epi(accs, in_refs, out_refs):
        _residual_outputs(in_refs[2][...] + accs[0], in_refs, out_refs, 3 if extra_in else None)

    rowspec = pl.BlockSpec((tm, d), lambda i: (i, 0))
    res = _matmul(
        "mix_out_proj", (lp // tm,), None, [merged, w_out, h] + extra_in,
        [rowspec, pl.BlockSpec((d, d), lambda i: (0, 0)), rowspec] + extra_specs,
        [(0, 1, "nn", 0)], [(tm, d)], epi, shapes, specs, ("parallel",), comm)
    return (res, []) if comm is None else res


def _merge_bwd(dhb, w_out, o, yg, ga, gs, w3t):
    lp, d = ga.shape
    kw = w3t.shape[2]
    tm = _row_tile(lp)

    def epi(accs, in_refs, out_refs):
        dm, attn, vv, gg = accs
        sa = _sigmoid(in_refs[7][...])
        ss = _sigmoid(in_refs[8][...])
        sg = _sigmoid(gg)
        ssm = vv * sg
        dssm = dm * ss
        out_refs[0][...] = (dm * sa).astype(BF16)
        out_refs[1][...] = (dssm * sg).astype(BF16)
        out_refs[2][...] = (dssm * vv * sg * (1.0 - sg)).astype(BF16)
        out_refs[3][...] = (dm * attn * sa * (1.0 - sa)).astype(BF16)
        out_refs[4][...] = (dm * ssm * ss * (1.0 - ss)).astype(BF16)

    wspec = lambda which: pl.BlockSpec((None, d, kw), lambda i: (which, 0, 0))
    rowspec = pl.BlockSpec((tm, d), lambda i: (i, 0))
    aspec = pl.BlockSpec((tm, kw), lambda i: (i, 0))
    shp = jax.ShapeDtypeStruct((lp, d), BF16)
    return _matmul(
        "merge_bwd", (lp // tm,), None, [dhb, w_out, o, yg, w3t, w3t, w3t, ga, gs],
        [rowspec, pl.BlockSpec((d, d), lambda i: (0, 0)), aspec, aspec, wspec(0), wspec(1), wspec(2), rowspec,
         rowspec],
        [(0, 1, "nt", 0), (2, 4, "nt", 1), (3, 5, "nt", 2), (3, 6, "nt", 3)], [(tm, d)] * 4, epi,
        [shp] * 5, [rowspec] * 5, ("parallel",))


def _branch_bwd(dattn, dv, dg, w3t, y):
    lp, d = dattn.shape
    kw = w3t.shape[2]
    tm = _row_tile(lp)

    def epi(accs, in_refs, out_refs):
        out_refs[0][...] = accs[0].astype(BF16)
        out_refs[1][...] = accs[1] * _gelu_grad(in_refs[6][...])

    wspec = lambda which: pl.BlockSpec((None, d, kw), lambda i: (which, 0, 0))
    rowspec = pl.BlockSpec((tm, d), lambda i: (i, 0))
    aspec = pl.BlockSpec((tm, kw), lambda i: (i, 0))
    return _matmul(
        "branch_bwd", (lp // tm,), None, [dattn, dv, dg, w3t, w3t, w3t, y],
        [rowspec, rowspec, rowspec, wspec(0), wspec(1), wspec(2), aspec],
        [(0, 3, "nn", 0), (1, 4, "nn", 1), (2, 5, "nn", 1)], [(tm, kw)] * 2, epi,
        [jax.ShapeDtypeStruct((lp, kw), BF16), jax.ShapeDtypeStruct((lp, kw), F32)], [aspec, aspec],
        ("parallel",))


def _sibling_half(acc, rows):
    half = rows // 2
    return jnp.where(lax.axis_index("c") == 0, acc[half:rows], acc[:half]).astype(BF16)


def _tn_cols(x, ys, name):
    lp, kx = x.shape
    n = ys[0].shape[1]
    n4 = n // N_CHIPS
    tk = _tn_tiles(lp)
    ny = len(ys)

    def epi(accs, in_refs, out_refs):
        for i, acc in enumerate(accs):
            out_refs[i][...] = acc
            out_refs[ny + i][...] = _sibling_half(acc, kx)

    shp = jax.ShapeDtypeStruct((N_CHIPS, kx, n4), F32)
    shp_half = jax.ShapeDtypeStruct((N_CHIPS, kx // 2, n4), BF16)
    res = _matmul(
        name, (N_CHIPS, lp // tk), 1, [x] + list(ys),
        [pl.BlockSpec((tk, kx), lambda j, k: (k, 0))] + [pl.BlockSpec((tk, n4), lambda j, k: (k, j))] * ny,
        [(0, 1 + i, "tn", i) for i in range(ny)], [(kx, n4)] * ny, epi,
        [shp] * ny + [shp_half] * ny,
        [pl.BlockSpec((None, kx, n4), lambda j, k: (j, 0, 0))] * ny
        + [pl.BlockSpec((None, kx // 2, n4), lambda j, k: (j, 0, 0))] * ny,
        ("arbitrary", "arbitrary"))
    return res[:ny], res[ny:]


def _tn_full(x, y, name, tn_cols=None):
    lp, kx = x.shape
    n = y.shape[1]
    tk = _tn_tiles(lp)
    tn = n if tn_cols is None else tn_cols
    k4 = kx // N_CHIPS

    def epi(accs, in_refs, out_refs):
        for j in range(N_CHIPS):
            slab = accs[0][j * k4:(j + 1) * k4]
            out_refs[0][j] = slab
            out_refs[1][j] = _sibling_half(slab, k4)

    return _matmul(
        name, (n // tn, lp // tk), 1, [x, y],
        [pl.BlockSpec((tk, kx), lambda j, k: (k, 0)), pl.BlockSpec((tk, tn), lambda j, k: (k, j))],
        [(0, 1, "tn", 0)], [(kx, tn)], epi,
        [jax.ShapeDtypeStruct((N_CHIPS, k4, n), F32), jax.ShapeDtypeStruct((N_CHIPS, k4 // 2, n), BF16)],
        [pl.BlockSpec((N_CHIPS, k4, tn), lambda j, k: (0, 0, j)),
         pl.BlockSpec((N_CHIPS, k4 // 2, tn), lambda j, k: (0, 0, j))],
        ("arbitrary", "arbitrary"))


def _in_proj_bwd(dz, w_in, dh, h_in, gain):
    lp, d = h_in.shape
    inw = w_in.shape[0]
    tm = _row_tile(lp)

    def epi(accs, in_refs, out_refs):
        i = pl.program_id(0)
        dx, dgrow = _rms_bwd_math(accs[0], in_refs[3][...], in_refs[4][...])
        dh_new = in_refs[2][...] + dx
        out_refs[0][...] = dh_new
        out_refs[2][...] = dh_new.astype(BF16)

        @pl.when(i == 0)
        def _():
            out_refs[1][...] = jnp.zeros_like(out_refs[1])

        out_refs[1][...] += jnp.sum(dgrow, axis=0, keepdims=True)

    row = pl.BlockSpec((tm, d), lambda i: (i, 0))
    return _matmul(
        "mix_in_proj_bwd", (lp // tm,), None, [dz, w_in, dh, h_in, gain.reshape(1, d)],
        [pl.BlockSpec((tm, inw), lambda i: (i, 0)), pl.BlockSpec((inw, d), lambda i: (0, 0)), row, row,
         pl.BlockSpec((1, d), lambda i: (0, 0))],
        [(0, 1, "nn", 0)], [(tm, d)], epi,
        [jax.ShapeDtypeStruct((lp, d), F32), jax.ShapeDtypeStruct((1, d), F32), jax.ShapeDtypeStruct((lp, d), BF16)],
        [row, pl.BlockSpec((1, d), lambda i: (0, 0)), row], ("arbitrary",))


def _loss_head(h, gain, target):
    lp, d = h.shape
    nb = lp // BLOCK

    def body(h_ref, g_ref, t_ref, dh_ref, dg_ref, loss_ref, dhb_ref):
        i = pl.program_id(0)

        @pl.when(i == 0)
        def _():
            dg_ref[...] = jnp.zeros_like(dg_ref)
            loss_ref[...] = jnp.zeros_like(loss_ref)
            dh_ref[...] = jnp.zeros_like(dh_ref)
            dhb_ref[...] = jnp.zeros_like(dhb_ref)

        @pl.when(i > 0)
        def _():
            x = h_ref[...]
            g = g_ref[...]
            r = lax.rsqrt(jnp.mean(x * x, axis=-1, keepdims=True) + EPS)
            err = x * r * g - t_ref[...]
            loss_ref[...] += jnp.zeros_like(loss_ref) + 0.5 * jnp.sum(jnp.sum(err * err, axis=-1, keepdims=True)) / d
            dx, dgrow = _rms_bwd_math(err * (1.0 / d), x, g)
            dh_ref[...] = dx
            dhb_ref[...] = dx.astype(BF16)
            dg_ref[...] += jnp.sum(dgrow, axis=0, keepdims=True)

    row = pl.BlockSpec((BLOCK, d), lambda i: (i, 0))
    one = pl.BlockSpec((1, d), lambda i: (0, 0))
    return pl.pallas_call(
        body,
        out_shape=[jax.ShapeDtypeStruct((lp, d), F32), jax.ShapeDtypeStruct((1, d), F32),
                   jax.ShapeDtypeStruct((SUBLANES, LANES), F32), jax.ShapeDtypeStruct((lp, d), BF16)],
        grid=(nb,),
        in_specs=[row, one, pl.BlockSpec((BLOCK, d), lambda i: (jnp.maximum(i - 1, 0), 0))],
        out_specs=[row, one, pl.BlockSpec((SUBLANES, LANES), lambda i: (0, 0)), row],
        compiler_params=_cparams(("arbitrary",)), name="loss_head")(h, gain.reshape(1, d), target)


def _adam_math(w, g, m, v):
    m = ADAM_B1 * m + (1.0 - ADAM_B1) * g
    v = ADAM_B2 * v + (1.0 - ADAM_B2) * (g * g)
    m_hat = m / (1.0 - ADAM_B1 ** ADAM_STEP)
    v_hat = v / (1.0 - ADAM_B2 ** ADAM_STEP)
    delta = -ADAM_LR * (m_hat / (jnp.sqrt(v_hat) + ADAM_EPS) + ADAM_WD * w)
    return delta, m, v


def _adamw_layers(w, m, v, mine, other, pos, name):
    depth, r, c = w.shape
    half = r // 2
    tr = _div_tile(half, c * 4)
    nh = half // tr

    def body(*refs):
        pos_ref, w_ref, m_ref, v_ref = refs[:4]
        mine_refs = refs[4:4 + depth]
        other_refs = refs[4 + depth:4 + 2 * depth]
        g_out, d_out, m_out, v_out = refs[4 + 2 * depth:]
        layer, i = pl.program_id(0), pl.program_id(1)
        is_mine = (i // nh) == pos_ref[0]

        def update(g):
            delta, nm, nv = _adam_math(w_ref[...], g, m_ref[...], v_ref[...])
            g_out[...] = g
            d_out[...] = delta
            m_out[...] = nm
            v_out[...] = nv

        for l in range(depth):
            @pl.when((layer == l) & is_mine)
            def _(l=l):
                update(mine_refs[l][...])

            @pl.when((layer == l) & jnp.logical_not(is_mine))
            def _(l=l):
                update(other_refs[l][...])

    stacked = pl.BlockSpec((None, tr, c), lambda l, i, p: (l, i, 0))

    def gspec(layer, is_other):
        def imap(l, i, p):
            first = jnp.where(is_other, 1 - p[0], p[0]) * nh
            here = jnp.clip(i - first, 0, nh - 1)
            return (jnp.where(l == layer, here, jnp.where(l < layer, 0, nh - 1)), 0)
        return pl.BlockSpec((tr, c), imap)

    shp = jax.ShapeDtypeStruct((depth, r, c), F32)
    grid_spec = pltpu.PrefetchScalarGridSpec(
        num_scalar_prefetch=1, grid=(depth, 2 * nh),
        in_specs=[stacked] * 3 + [gspec(l, 0) for l in range(depth)] + [gspec(l, 1) for l in range(depth)],
        out_specs=[stacked] * 4)
    return pl.pallas_call(
        body, out_shape=[shp] * 4, grid_spec=grid_spec,
        compiler_params=_cparams(("arbitrary", "arbitrary")), name=name)(pos, w, m, v, *mine, *other)


def _adamw_whole(w, g, m, v, name):
    def body(w_ref, g_ref, m_ref, v_ref, d_out, m_out, v_out):
        delta, nm, nv = _adam_math(w_ref[...], g_ref[...], m_ref[...], v_ref[...])
        d_out[...] = delta
        m_out[...] = nm
        v_out[...] = nv

    shp = jax.ShapeDtypeStruct(w.shape, F32)
    return pl.pallas_call(body, out_shape=[shp] * 3, compiler_params=_cparams(), name=name)(w, g, m, v)


def _mesh_pos():
    return lax.axis_index("x"), lax.axis_index("y"), lax.axis_index("c")


def _row_half(ref, which, lead):
    half = ref.shape[lead] // 2
    idx = (slice(None),) * lead + (pl.ds(which * half, half), slice(None))
    return ref.at[idx]


def _gather_comm(arrs, tag):
    n = len(arrs)

    def ctx(ins, outs, sems):
        send_sems, recv_sems, local_sems = sems
        x, y, c = _mesh_pos()
        chips = [(1 - x, y), (x, 1 - y), (1 - x, 1 - y)]

        def slot(k, chip, which):
            lead = len(ins[k].shape) - 2
            return _row_half(outs[k].at[2 * chip[0] + chip[1]], which, lead)

        def copy(k, j, src, dst, to):
            return pltpu.make_async_remote_copy(
                src_ref=src, dst_ref=dst, send_sem=send_sems.at[6 * k + j], recv_sem=recv_sems.at[6 * k + j],
                device_id=to, device_id_type=MESH)

        def local(k):
            return pltpu.make_async_copy(ins[k], outs[k].at[2 * x + y], local_sems.at[k])

        def first(k, j):
            lead = len(ins[k].shape) - 2
            return copy(k, j, _row_half(ins[k], c, lead), slot(k, (x, y), c), (*chips[j], c))

        def passed(k, j, which):
            return copy(k, 3 + j, slot(k, chips[j], which), slot(k, chips[j], which), (x, y, 1 - c))

        def landed(k, j):
            return copy(k, j, slot(k, chips[j], c), slot(k, chips[j], c), (x, y, 1 - c))

        return c, local, first, passed, landed

    def start(ins, outs, sems):
        c, local, first, passed, landed = ctx(ins, outs, sems)
        for k in range(n):
            local(k).start()
            for j in range(3):
                first(k, j).start()

    def mid(ins, outs, sems):
        c, local, first, passed, landed = ctx(ins, outs, sems)
        for j in range(3):
            for k in range(n):
                landed(k, j).wait_recv()
                passed(k, j, c).start()

    def finish(ins, outs, sems):
        c, local, first, passed, landed = ctx(ins, outs, sems)
        for j in range(3):
            for k in range(n):
                passed(k, j, 1 - c).wait_recv()
        for k in range(n):
            for j in range(3):
                first(k, j).wait_send()
                passed(k, j, c).wait_send()
            local(k).wait()

    return _Comm(
        tag, arrs, [jax.ShapeDtypeStruct((N_CHIPS,) + a.shape, a.dtype) for a in arrs],
        [pltpu.SemaphoreType.DMA((6 * n,)), pltpu.SemaphoreType.DMA((6 * n,)), pltpu.SemaphoreType.DMA((n,))],
        start, mid, finish)


def _run_comm(comm, name):
    n_in, n_out = len(comm.ins), len(comm.out_shapes)

    def body(*refs):
        ins, outs, sems = refs[:n_in], refs[n_in:n_in + n_out], refs[n_in + n_out:]
        comm.start(ins, outs, sems)
        if comm.mid is not None:
            comm.mid(ins, outs, sems)
        comm.finish(ins, outs, sems)

    return pl.pallas_call(
        body, out_shape=comm.out_shapes, in_specs=[HBM_SPEC] * n_in, out_specs=[HBM_SPEC] * n_out,
        scratch_shapes=comm.sems, name=name)(*comm.ins)


def _all_gather_chips(arrs, name):
    return _run_comm(_gather_comm(arrs, "gather"), name)


GATHER_US_PER_BYTE = 380.0 / 11.65e6
HOST_US = dict(ffn_up=68.0, ffn_down=37.0, in_proj=38.0, attn_fwd=103.0, ssm_fwd=70.0, merge_fwd=30.0,
               out_proj=23.0)
HOST_SLACK_US = 10.0


class _WeightStream:
    def __init__(self, pieces):
        self.keys = [k for k, _ in pieces]
        self.shards = dict(pieces)
        self.next = 0
        self.full = {}
        self.pending = []

    def comm_for(self, host):
        budget = HOST_US[host] + HOST_SLACK_US
        taken, cost = [], 0.0
        while self.next < len(self.keys):
            key = self.keys[self.next]
            c = self.shards[key].size * self.shards[key].dtype.itemsize * GATHER_US_PER_BYTE
            if cost + c > budget and taken:
                break
            taken.append(key)
            cost += c
            self.next += 1
        self.pending = taken
        if not taken:
            return None
        return _gather_comm([self.shards[k] for k in taken], "g_" + "_".join(k[1] for k in taken))

    def deposit(self, gathered):
        for key, arr in zip(self.pending, gathered):
            self.full[key] = arr
        self.pending = []

    def get(self, key):
        if key not in self.full:
            upto = self.keys.index(key) + 1
            keys = self.keys[self.next:upto]
            self.next = upto
            for k, arr in zip(keys, _all_gather_chips([self.shards[k] for k in keys], "gather_now")):
                self.full[k] = arr
        return self.full[key]


def _all_gather_devices(x_shard, name):
    m_per, ncol = x_shard.shape

    def body(x_ref, out_ref, send_sems, recv_sems, local_sem):
        x, y, c = _mesh_pos()
        me, sibling = (x, y, c), (x, y, 1 - c)
        chips = [(1 - x, y), (x, 1 - y), (1 - x, 1 - y)]

        def rows(px, py, pc):
            return out_ref.at[4 * px + 2 * py + pc]

        def copy(k, block, to, src=None):
            return pltpu.make_async_remote_copy(
                src_ref=rows(*block) if src is None else src, dst_ref=rows(*block),
                send_sem=send_sems.at[k], recv_sem=recv_sems.at[k], device_id=to, device_id_type=MESH)

        mine = pltpu.make_async_copy(x_ref, rows(*me), local_sem)
        mine.start()
        first = [copy(0, me, sibling, src=x_ref)]
        first += [copy(1 + j, me, (*chip, c), src=x_ref) for j, chip in enumerate(chips)]
        for cp in first:
            cp.start()
        passed = [copy(4 + j, (*chip, c), sibling) for j, chip in enumerate(chips)]
        for j, chip in enumerate(chips):
            copy(1 + j, (*chip, c), me).wait_recv()
            passed[j].start()
        copy(0, sibling, me).wait_recv()
        for j, chip in enumerate(chips):
            copy(4 + j, (*chip, 1 - c), me).wait_recv()
        for cp in first + passed:
            cp.wait_send()
        mine.wait()

    return pl.pallas_call(
        body, out_shape=jax.ShapeDtypeStruct((8, m_per, ncol), x_shard.dtype),
        in_specs=[pl.BlockSpec(memory_space=pltpu.VMEM)], out_specs=pl.BlockSpec(memory_space=pltpu.VMEM),
        scratch_shapes=[pltpu.SemaphoreType.DMA((7,)), pltpu.SemaphoreType.DMA((7,)), pltpu.SemaphoreType.DMA],
        compiler_params=pltpu.CompilerParams(vmem_limit_bytes=VMEM_LIMIT), name=name)(x_shard)


def _sum_devices(g8, name):
    _, r, c = g8.shape
    tr = _div_tile(r, c * 4 * 8)

    def body(g_ref, o_ref):
        acc = g_ref[0]
        for dev in range(1, 8):
            acc = acc + g_ref[dev]
        o_ref[...] = acc

    return pl.pallas_call(
        body, out_shape=jax.ShapeDtypeStruct((r, c), F32), grid=(r // tr,),
        in_specs=[pl.BlockSpec((8, tr, c), lambda i: (0, i, 0))], out_specs=pl.BlockSpec((tr, c), lambda i: (i, 0)),
        compiler_params=_cparams(("parallel",)), name=name)(g8)


def _chip_partials(arrs, recvs, pos, name):
    n = len(arrs)

    def body(pos_ref, *refs):
        for a_ref, b_ref, o_ref in zip(refs[:n], refs[n:2 * n], refs[2 * n:]):
            o_ref[...] = (a_ref[...] + b_ref[...]).astype(BF16)

    own_specs, recv_specs, shapes = [], [], []
    for arr in arrs:
        nslab, r, c = arr.shape
        own_specs.append(pl.BlockSpec((None, r // 2, c), lambda j, p: (j, p[0], 0)))
        recv_specs.append(pl.BlockSpec((None, r // 2, c), lambda j, p: (j, 0, 0)))
        shapes.append(jax.ShapeDtypeStruct((nslab, r // 2, c), BF16))
    grid_spec = pltpu.PrefetchScalarGridSpec(
        num_scalar_prefetch=1, grid=(N_CHIPS,), in_specs=own_specs + recv_specs, out_specs=recv_specs)
    return pl.pallas_call(
        body, out_shape=shapes, grid_spec=grid_spec,
        compiler_params=_cparams(("parallel",)), name=name)(pos, *arrs, *recvs)


def _chip_exchange_comm(parts, tag):
    n = len(parts)

    def copies(ins, outs, sems):
        send_sems, recv_sems = sems
        x, y, c = _mesh_pos()
        chips = [(1 - x, y), (x, 1 - y), (1 - x, 1 - y)]
        return [pltpu.make_async_remote_copy(
            src_ref=ins[k].at[2 * chip[0] + chip[1]], dst_ref=outs[k].at[j],
            send_sem=send_sems.at[3 * k + j], recv_sem=recv_sems.at[3 * k + j],
            device_id=(*chip, c), device_id_type=MESH) for k in range(n) for j, chip in enumerate(chips)]

    def start(ins, outs, sems):
        for cp in copies(ins, outs, sems):
            cp.start()

    def finish(ins, outs, sems):
        for cp in copies(ins, outs, sems):
            cp.wait()

    return _Comm(
        tag, parts, [jax.ShapeDtypeStruct((3,) + p.shape[1:], p.dtype) for p in parts],
        [pltpu.SemaphoreType.DMA((3 * n,)), pltpu.SemaphoreType.DMA((3 * n,))], start, None, finish)


def _reduce_halves(arrs, recvs, gots, pos, name):
    n = len(arrs)

    def body(pos_ref, *refs):
        for a_ref, b_ref, g_ref, o_ref in zip(refs[:n], refs[n:2 * n], refs[2 * n:3 * n], refs[3 * n:]):
            acc = a_ref[...] + b_ref[...]
            for j in range(3):
                acc = acc + g_ref[j].astype(F32)
            o_ref[...] = acc

    own_specs, recv_specs, got_specs, out_specs, shapes = [], [], [], [], []
    for arr in arrs:
        _, r, c = arr.shape
        own_specs.append(pl.BlockSpec((None, r // 2, c), lambda i, p: (p[1], p[0], 0)))
        recv_specs.append(pl.BlockSpec((None, r // 2, c), lambda i, p: (p[1], 0, 0)))
        got_specs.append(pl.BlockSpec((3, r // 2, c), lambda i, p: (0, 0, 0)))
        out_specs.append(pl.BlockSpec((r // 2, c), lambda i, p: (0, 0)))
        shapes.append(jax.ShapeDtypeStruct((r // 2, c), F32))
    grid_spec = pltpu.PrefetchScalarGridSpec(
        num_scalar_prefetch=1, grid=(1,), in_specs=own_specs + recv_specs + got_specs, out_specs=out_specs)
    return pl.pallas_call(
        body, out_shape=shapes, grid_spec=grid_spec,
        compiler_params=_cparams(("arbitrary",)), name=name)(pos, *arrs, *recvs, *gots)


def _share_halves(halves, name):
    n = len(halves)

    def body(*refs):
        ins, outs = refs[:n], refs[n:2 * n]
        send_sems, recv_sems = refs[2 * n:]
        x, y, c = _mesh_pos()
        cps = []
        for k in range(n):
            cp = pltpu.make_async_remote_copy(
                src_ref=ins[k], dst_ref=outs[k], send_sem=send_sems.at[k], recv_sem=recv_sems.at[k],
                device_id=(x, y, 1 - c), device_id_type=MESH)
            cp.start()
            cps.append(cp)
        for cp in cps:
            cp.wait()

    return pl.pallas_call(
        body, out_shape=[jax.ShapeDtypeStruct(h.shape, h.dtype) for h in halves],
        in_specs=[HBM_SPEC] * n, out_specs=[HBM_SPEC] * n,
        scratch_shapes=[pltpu.SemaphoreType.DMA((n,)), pltpu.SemaphoreType.DMA((n,))], name=name)(*halves)


class _Reduction:
    def __init__(self, arrs, others, pos, tag):
        self.arrs, self.pos, self.tag = arrs, pos, tag
        self.recv = _share_halves(others, "rs_sibling_" + tag)
        self.parts = _chip_partials(arrs, self.recv, pos, "rs_partial_" + tag)
        self.got = None

    def comm(self):
        return _chip_exchange_comm(self.parts, "rs_" + self.tag)

    def end(self):
        if self.got is None:
            self.got = _run_comm(self.comm(), "rs_chips_" + self.tag)
        halves = _reduce_halves(self.arrs, self.recv, self.got, self.pos, "rs_reduce_" + self.tag)
        return halves, _share_halves(halves, "rs_share_" + self.tag)


def _w_in_full(p, l, ws):
    slabs = ws.get((l, "w_in"))
    return slabs.reshape(-1, slabs.shape[2])


def _w3t_full(p, l, ws):
    if "w3t" not in p:
        slabs = ws.get((l, "w3"))
        p["w3t"] = jnp.swapaxes(slabs, 0, 1).reshape(slabs.shape[1], -1, slabs.shape[3])
    return p["w3t"]


def _w_out_full(l, ws):
    slabs = ws.get((l, "w_out"))
    return slabs.reshape(-1, slabs.shape[2])


def _layer_fwd(h, n0, l, p, next_gain, ws, tabs):
    def hosted(host, fn, *args):
        out, got = fn(*args, ws.comm_for(host))
        ws.deposit(got)
        return out

    ffn1_saved = hosted("ffn_up", _ffn_up, n0, ws.get((l, "wg1")), ws.get((l, "wu1")))
    h1, n = hosted("ffn_down", _ffn_down, ffn1_saved[2], ws.get((l, "wd1")), h, p["mix_norm"])
    ssm_w = p["ssm_d"].shape[0]
    q, k, v, u, ga, gs = hosted("in_proj", _in_proj, n, _w_in_full(p, l, ws), tabs, ssm_w)
    o = hosted("attn_fwd", _attn_fwd, q, k, v, p["attn_sinks"])
    y, yg = hosted("ssm_fwd", _ssm_fwd, u, *p["ssm_tabs"], p["ssm_d"])
    merged = hosted("merge_fwd", _merge_fwd, o, yg, ga, gs, _w3t_full(p, l, ws))
    h2, n2 = hosted("out_proj", _out_proj, merged, _w_out_full(l, ws), h1, p["ffn2_norm"])
    ffn2_saved = hosted("ffn_up", _ffn_up, n2, ws.get((l, "wg2")), ws.get((l, "wu2")))
    h3, *n3 = hosted("ffn_down", _ffn_down, ffn2_saved[2], ws.get((l, "wd2")), h2, next_gain)
    saved = dict(h0=h, h1=h1, h2=h2, ffn1=ffn1_saved, ffn2=ffn2_saved, n_mix=n, q=q, k=k, v=v, u=u, ga=ga, gs=gs,
                 o=o, y=y, yg=yg, merged=merged)
    return h3, (n3[0] if n3 else None), saved


def _layer_bwd(dh_pair, l, p, ws, s, tabs, pos):
    g = {}
    (dh2, dhb), g["ffn2_norm"], red_ffn2, _ = _ffn_bwd(
        dh_pair, s["h2"], p["ffn2_norm"], ws.get((l, "wg2")), ws.get((l, "wu2")), ws.get((l, "wd2")), p["f4"],
        s["ffn2"], pos)
    w3, w_out_w = _w3t_full(p, l, ws), _w_out_full(l, ws)
    lp, d = dh2.shape
    d4 = d // N_CHIPS
    dw_out, dw_out_other = _tn_full(s["merged"], dhb, "mix_dw_out")
    dattn, dv, dg, dga, dgs = _merge_bwd(dhb, w_out_w, s["o"], s["yg"], s["ga"], s["gs"], w3)
    (dw_ap,), (dw_ap_other,) = _tn_cols(s["o"], [dattn], "mix_dw_ap")
    (dw_gv, dw_gg), (dw_gv_other, dw_gg_other) = _tn_cols(s["yg"], [dv, dg], "mix_dw_glu")
    do, dy = _branch_bwd(dattn, dv, dg, w3, s["y"])
    (dq, dk, dvv, dkm, dvm, dsink), _ = _attn_bwd(s["q"], s["k"], s["v"], do, p["attn_sinks"], tabs)
    g["attn_sinks"] = dsink[:, 0]
    (du, dlr, dli, dbr, dbi, dcr, dci, dd), _ = _ssm_bwd(s["u"], dy, *p["ssm_tabs"], p["ssm_d"])
    ngrp = p["ssm_d"].shape[0] // SSM_GROUP
    g["ssm_lam"] = (dlr.reshape(ngrp, SSM_STATE), dli.reshape(ngrp, SSM_STATE),
                    _ssm_untable_b(dbr, ngrp), _ssm_untable_b(dbi, ngrp))
    g["ssm_c_re"] = _ssm_untable_c(dcr, ngrp)
    g["ssm_c_im"] = _ssm_untable_c(dci, ngrp)
    g["ssm_d"] = dd[0]
    dk = dk.at[:BLOCK].add(dkm)
    dvv = dvv.at[:BLOCK].add(dvm)
    dz = jnp.concatenate([dq.astype(BF16), dk.astype(BF16), dvv.astype(BF16), du.astype(BF16), dga, dgs], axis=1)
    n = s["n_mix"]
    w_in = _w_in_full(p, l, ws)
    dw_in, dw_in_other = _tn_full(dz, n, "mix_dw_in", d // 2)
    red_mix = _Reduction([dw_in, dw_ap, dw_gv, dw_gg, dw_out],
                         [dw_in_other, dw_ap_other, dw_gv_other, dw_gg_other, dw_out_other], pos, "mix")
    dh1, g["mix_norm"], dh1b = _in_proj_bwd(dz, w_in, dh2, s["h1"], p["mix_norm"])
    dh0_pair, g["ffn1_norm"], red_ffn1, red_mix.got = _ffn_bwd(
        (dh1, dh1b), s["h0"], p["ffn1_norm"], ws.get((l, "wg1")), ws.get((l, "wu1")), ws.get((l, "wd1")), p["f4"],
        s["ffn1"], pos, red_mix.comm())
    return dh0_pair, g, [*red_ffn1, red_mix, *red_ffn2]


BIG = ["ffn1_w_gate", "ffn1_w_up", "ffn1_w_down", "w_in", "w_attn_proj", "w_glu_v", "w_glu_g", "w_out",
       "ffn2_w_gate", "ffn2_w_up", "ffn2_w_down"]
TRANSPOSED = ["ffn1_w_gate", "ffn1_w_up", "w_in", "ffn2_w_gate", "ffn2_w_up"]
SMALL = ["ffn1_norm", "mix_norm", "attn_sinks", "ssm_a_re", "ssm_a_im", "ssm_log_dt", "ssm_b_re", "ssm_b_im",
         "ssm_c_re", "ssm_c_im", "ssm_d", "ffn2_norm", "final_norm"]
WEIGHTS = ["meta_tokens", "ffn1_norm", "ffn1_w_gate", "ffn1_w_up", "ffn1_w_down", "mix_norm", "w_in", "attn_sinks",
           "ssm_a_re", "ssm_a_im", "ssm_log_dt", "ssm_b_re", "ssm_b_im", "ssm_c_re", "ssm_c_im", "ssm_d",
           "w_attn_proj", "w_glu_v", "w_glu_g", "w_out", "ffn2_norm", "ffn2_w_gate", "ffn2_w_up", "ffn2_w_down",
           "final_norm"]


def _small_rows(shape):
    rows = -(-math.prod(shape) // LANES)
    return -(-rows // SUBLANES) * SUBLANES


def _pack_small(tree):
    parts = []
    for k in SMALL + ["meta_tokens"]:
        size, rows = math.prod(tree[k].shape), _small_rows(tree[k].shape)
        if size % LANES == 0:
            part = tree[k].reshape(size // LANES, LANES)
        else:
            part = jnp.pad(tree[k].reshape(1, size), ((0, 0), (0, LANES - size)))
        parts.append(jnp.pad(part, ((0, rows - part.shape[0]), (0, 0))))
    return jnp.concatenate(parts, axis=0)


def _unpack_small(packed, like):
    out, off = {}, 0
    for k in SMALL + ["meta_tokens"]:
        size, rows = math.prod(like[k].shape), _small_rows(like[k].shape)
        if size % LANES == 0:
            out[k] = packed[off:off + size // LANES].reshape(like[k].shape)
        else:
            out[k] = packed[off, :size].reshape(like[k].shape)
        off += rows
    return out


def kernel(x, meta_tokens, ffn1_norm, ffn1_w_gate, ffn1_w_up, ffn1_w_down, mix_norm, w_in, attn_sinks, ssm_a_re, ssm_a_im, ssm_log_dt, ssm_b_re, ssm_b_im, ssm_c_re, ssm_c_im, ssm_d, w_attn_proj, w_glu_v, w_glu_g, w_out, ffn2_norm, ffn2_w_gate, ffn2_w_up, ffn2_w_down, final_norm, loss_target, m_meta_tokens, m_ffn1_norm, m_ffn1_w_gate, m_ffn1_w_up, m_ffn1_w_down, m_mix_norm, m_w_in, m_attn_sinks, m_ssm_a_re, m_ssm_a_im, m_ssm_log_dt, m_ssm_b_re, m_ssm_b_im, m_ssm_c_re, m_ssm_c_im, m_ssm_d, m_w_attn_proj, m_w_glu_v, m_w_glu_g, m_w_out, m_ffn2_norm, m_ffn2_w_gate, m_ffn2_w_up, m_ffn2_w_down, m_final_norm, v_meta_tokens, v_ffn1_norm, v_ffn1_w_gate, v_ffn1_w_up, v_ffn1_w_down, v_mix_norm, v_w_in, v_attn_sinks, v_ssm_a_re, v_ssm_a_im, v_ssm_log_dt, v_ssm_b_re, v_ssm_b_im, v_ssm_c_re, v_ssm_c_im, v_ssm_d, v_w_attn_proj, v_w_glu_v, v_w_glu_g, v_w_out, v_ffn2_norm, v_ffn2_w_gate, v_ffn2_w_up, v_ffn2_w_down, v_final_norm):
    args = dict(locals())
    w = {k: args[k] for k in WEIGHTS}
    m = {k: args["m_" + k] for k in WEIGHTS}
    v = {k: args["v_" + k] for k in WEIGHTS}
    depth = ffn1_norm.shape[0]
    seq, d = x.shape[1], x.shape[2]
    lp = seq + BLOCK
    xi, yi, ci = _mesh_pos()
    pos = jnp.stack([ci, 2 * xi + yi]).astype(jnp.int32)

    tabs = _rope_tables(lp)
    (meta_all,) = _all_gather_chips([meta_tokens], "gather_meta")
    meta_full = jnp.concatenate([meta_all[j] for j in range(N_CHIPS)], axis=1)
    layers, pieces = [], []
    f4 = ffn1_w_gate.shape[2]
    fp = -(-f4 // MXU_DIM) * MXU_DIM

    def ffn_rows(wt):
        return jnp.pad(wt, ((0, fp - f4), (0, 0))).astype(BF16)

    for l in range(depth):
        pieces += [
            ((l, "wg1"), ffn_rows(ffn1_w_gate[l].T)), ((l, "wu1"), ffn_rows(ffn1_w_up[l].T)),
            ((l, "wd1"), ffn_rows(ffn1_w_down[l])), ((l, "w_in"), w_in[l].T.astype(BF16)),
            ((l, "w3"), jnp.stack([w_attn_proj[l].T, w_glu_v[l].T, w_glu_g[l].T]).astype(BF16)),
            ((l, "w_out"), w_out[l].astype(BF16)),
            ((l, "wg2"), ffn_rows(ffn2_w_gate[l].T)), ((l, "wu2"), ffn_rows(ffn2_w_up[l].T)),
            ((l, "wd2"), ffn_rows(ffn2_w_down[l]))]
        lb_re, lb_im, bb_re, bb_im = _ssm_params(ssm_a_re[l], ssm_a_im[l], ssm_log_dt[l], ssm_b_re[l], ssm_b_im[l])
        ngrp = lb_re.shape[0]
        nt = ngrp // GROUPS_PER_TILE
        ssm_tabs = (lb_re.reshape(nt, 1, TILE_STATES), lb_im.reshape(nt, 1, TILE_STATES),
                    *_ssm_tables(bb_re, bb_im, ssm_c_re[l], ssm_c_im[l]))
        layers.append(dict(
            ffn1_norm=ffn1_norm[l], mix_norm=mix_norm[l], ffn2_norm=ffn2_norm[l], attn_sinks=attn_sinks[l],
            ssm_d=ssm_d[l], ssm_tabs=ssm_tabs, f4=f4))
    ws = _WeightStream(pieces)
    ws.get((0, "wu1"))

    h = jnp.concatenate([jnp.zeros((PAD_FRONT, d), F32), meta_full, x[0]], axis=0)
    saved = []
    n0 = _rms_fwd(h, ffn1_norm[0], "rms_fwd_first")
    for l in range(depth):
        next_gain = ffn1_norm[l + 1] if l + 1 < depth else None
        h, n0, s = _layer_fwd(h, n0, l, layers[l], next_gain, ws, tabs)
        saved.append(s)
    dh, g_final, loss_acc, dhb = _loss_head(h, final_norm, loss_target[0])
    dh_pair = (dh, dhb)
    loss = lax.psum(loss_acc[0, 0], ("x", "y", "c"))

    grads, reds = [None] * depth, [None] * depth
    for l in reversed(range(depth)):
        dh_pair, grads[l], reds[l] = _layer_bwd(dh_pair, l, layers[l], ws, saved[l], tabs, pos)
    dh = dh_pair[0]
    grad_x = dh[BLOCK:][None]
    dmeta_local = dh[PAD_FRONT:BLOCK]

    small = {k: [] for k in SMALL}
    for l in range(depth):
        gl = grads[l]
        _, vjp = jax.vjp(_ssm_params, ssm_a_re[l], ssm_a_im[l], ssm_log_dt[l], ssm_b_re[l], ssm_b_im[l])
        da_re, da_im, dlog_dt, db_re, db_im = vjp(gl["ssm_lam"])
        for k, val in (("ffn1_norm", gl["ffn1_norm"][0]), ("mix_norm", gl["mix_norm"][0]),
                       ("attn_sinks", gl["attn_sinks"]), ("ssm_a_re", da_re), ("ssm_a_im", da_im),
                       ("ssm_log_dt", dlog_dt), ("ssm_b_re", db_re), ("ssm_b_im", db_im),
                       ("ssm_c_re", gl["ssm_c_re"]), ("ssm_c_im", gl["ssm_c_im"]), ("ssm_d", gl["ssm_d"]),
                       ("ffn2_norm", gl["ffn2_norm"][0])):
            small[k].append(val)
    small_local = {k: jnp.stack(vals) for k, vals in small.items() if k != "final_norm"}
    small_local["final_norm"] = g_final[0]
    small_local["meta_tokens"] = dmeta_local
    like = dict(small_local)
    g_small = _sum_devices(_all_gather_devices(_pack_small(small_local), "gather_small_grads"), "sum_small_grads")
    g_small_tree = _unpack_small(g_small, like)
    d4 = d // N_CHIPS
    chip = 2 * xi + yi
    g_meta = lax.dynamic_slice_in_dim(g_small_tree["meta_tokens"], chip * d4, d4, axis=1)

    reduced = []
    for l in range(depth):
        mine, other = [], []
        for red in reds[l]:
            halves, sibling_halves = red.end()
            mine += halves
            other += sibling_halves
        reduced.append((mine, other))

    g_out, delta, new_m, new_v = {}, {}, {}, {}
    for i, k in enumerate(BIG):
        flip = (lambda t: jnp.swapaxes(t, 1, 2)) if k in TRANSPOSED else (lambda t: t)
        outs = _adamw_layers(
            flip(w[k]), flip(m[k]), flip(v[k]), [reduced[l][0][i] for l in range(depth)],
            [reduced[l][1][i] for l in range(depth)], pos, "adamw_" + k)
        g_out[k], delta[k], new_m[k], new_v[k] = [flip(t) for t in outs]
    g_small_tree["meta_tokens"] = g_meta
    for k in SMALL + ["meta_tokens"]:
        shape = w[k].shape if w[k].ndim > 1 else (1,) + w[k].shape
        outs = _adamw_whole(w[k].reshape(shape), g_small_tree[k].reshape(shape), m[k].reshape(shape),
                            v[k].reshape(shape), "adamw_" + k)
        g_out[k] = g_small_tree[k]
        delta[k], new_m[k], new_v[k] = [t.reshape(w[k].shape) for t in outs]

    return (loss, grad_x, *[g_out[k] for k in WEIGHTS], *[delta[k] for k in WEIGHTS],
            *[new_m[k] for k in WEIGHTS], *[new_v[k] for k in WEIGHTS])
```

```python
import functools
import math

import jax
import jax.numpy as jnp
from jax import lax
from jax.experimental import pallas as pl
from jax.experimental.pallas import tpu as pltpu

F32 = jnp.float32
BF16 = jnp.bfloat16

N_META = 16
HEAD_DIM = 64
N_Q_HEADS = 8
N_KV_HEADS = 2
Q_PER_KV = N_Q_HEADS // N_KV_HEADS
ATTN_WIDTH = N_Q_HEADS * HEAD_DIM
KV_WIDTH = N_KV_HEADS * HEAD_DIM
BLOCK = 128
PAD_FRONT = BLOCK - N_META
ROPE_THETA = 500000.0
ROT_DIM = HEAD_DIM // 4
SSM_GROUP = 16
SSM_STATE = 64
GROUPS_PER_TILE = 4
TILE_STATES = GROUPS_PER_TILE * SSM_STATE
LANES = 128
SUBLANES = 8
MXU_DIM = 256
EPS = 1e-6
NEG_INF = -1e30
N_CHIPS = 4

ADAM_LR = 0.001
ADAM_B1 = 0.9
ADAM_B2 = 0.999
ADAM_EPS = 1e-08
ADAM_WD = 0.01
ADAM_STEP = 10

VMEM_LIMIT = 56 * 1024 * 1024
MESH = pl.DeviceIdType.MESH


def _cparams(sem=None):
    return pltpu.CompilerParams(dimension_semantics=sem, vmem_limit_bytes=VMEM_LIMIT)


def _row_tile(rows, limit=512):
    best = None
    for t in range(128, limit + 1, 128):
        if rows % t == 0:
            best = t
    assert best is not None, rows
    return best


def _div_tile(rows, row_bytes, max_bytes=1 << 20, mult=8):
    best = None
    for t in range(mult, rows + 1, mult):
        if rows % t == 0 and t * row_bytes <= max_bytes:
            best = t
    if best is None:
        best = rows
    return best


def _dot(a, b, mode):
    if mode == "nn":
        dims = (((1,), (0,)), ((), ()))
    elif mode == "nt":
        dims = (((1,), (1,)), ((), ()))
    else:
        dims = (((0,), (0,)), ((), ()))
    return lax.dot_general(a.astype(BF16), b.astype(BF16), dims, preferred_element_type=F32)


def _sigmoid(x):
    return 1.0 / (1.0 + jnp.exp(-x))


_GELU_C = math.sqrt(2.0 / math.pi)


def _gelu(x):
    return 0.5 * x * (1.0 + jnp.tanh(_GELU_C * (x + 0.044715 * x * x * x)))


def _gelu_grad(x):
    t = jnp.tanh(_GELU_C * (x + 0.044715 * x * x * x))
    return 0.5 * (1.0 + t) + 0.5 * x * (1.0 - t * t) * _GELU_C * (1.0 + 3.0 * 0.044715 * x * x)


class _Comm:
    def __init__(self, tag, ins, out_shapes, sems, start, mid, finish):
        self.tag, self.ins, self.out_shapes, self.sems = tag, list(ins), list(out_shapes), list(sems)
        self.start, self.mid, self.finish = start, mid, finish


HBM_SPEC = pl.BlockSpec(memory_space=pltpu.HBM)
MID_NUM, MID_DEN = 7, 8


def _hosted_call(body, comm, *, out_shape, grid, in_specs, out_specs, scratch_shapes, sem, name, args):
    out_shape, in_specs, out_specs = list(out_shape), list(in_specs), list(out_specs)
    scratch_shapes = list(scratch_shapes)
    if comm is None:
        res = pl.pallas_call(
            body, out_shape=out_shape, grid=grid, in_specs=in_specs, out_specs=out_specs,
            scratch_shapes=scratch_shapes, compiler_params=_cparams(sem), name=name)(*args)
        return list(res), []
    n_in, n_out, n_sc = len(args), len(out_shape), len(scratch_shapes)
    nci, nco = len(comm.ins), len(comm.out_shapes)
    total = math.prod(grid)

    def wrapped(*refs):
        in_refs, cin = refs[:n_in], refs[n_in:n_in + nci]
        o0 = n_in + nci
        out_refs, cout = refs[o0:o0 + n_out], refs[o0 + n_out:o0 + n_out + nco]
        s0 = o0 + n_out + nco
        sc, csem = refs[s0:s0 + n_sc], refs[s0 + n_sc:]
        lin = 0
        for dim, size in enumerate(grid):
            lin = lin * size + pl.program_id(dim)

        @pl.when(lin == 0)
        def _():
            comm.start(cin, cout, csem)

        if comm.mid is not None:
            @pl.when(lin == (total * MID_NUM) // MID_DEN)
            def _():
                comm.mid(cin, cout, csem)

        body(*in_refs, *out_refs, *sc)

        @pl.when(lin == total - 1)
        def _():
            comm.finish(cin, cout, csem)

    res = pl.pallas_call(
        wrapped, out_shape=out_shape + comm.out_shapes, grid=grid,
        in_specs=in_specs + [HBM_SPEC] * nci, out_specs=out_specs + [HBM_SPEC] * nco,
        scratch_shapes=scratch_shapes + comm.sems,
        compiler_params=_cparams(("arbitrary",) * len(grid)), name=name + "_" + comm.tag)(*args, *comm.ins)
    return list(res[:n_out]), list(res[n_out:])


def _matmul(name, grid, k_axis, ins, in_specs, pairs, acc_shapes, epilogue, out_shapes, out_specs, sem, comm=None):
    n_in, n_out, n_acc = len(ins), len(out_shapes), len(acc_shapes)

    def body(*refs):
        in_refs = refs[:n_in]
        out_refs = refs[n_in:n_in + n_out]
        acc_refs = refs[n_in + n_out:]
        if k_axis is None:
            accs = [None] * n_acc
            for ia, ib, mode, iacc in pairs:
                d = _dot(in_refs[ia][...], in_refs[ib][...], mode)
                accs[iacc] = d if accs[iacc] is None else accs[iacc] + d
            epilogue(accs, in_refs, out_refs)
            return
        k = pl.program_id(k_axis)

        @pl.when(k == 0)
        def _():
            for r in acc_refs:
                r[...] = jnp.zeros_like(r)

        for ia, ib, mode, iacc in pairs:
            acc_refs[iacc][...] += _dot(in_refs[ia][...], in_refs[ib][...], mode)

        @pl.when(k == pl.num_programs(k_axis) - 1)
        def _():
            epilogue([r[...] for r in acc_refs], in_refs, out_refs)

    scratch = [] if k_axis is None else [pltpu.VMEM(s, F32) for s in acc_shapes]
    outs, couts = _hosted_call(
        body, comm, out_shape=out_shapes, grid=grid, in_specs=in_specs, out_specs=out_specs,
        scratch_shapes=scratch, sem=sem, name=name, args=ins)
    return outs if comm is None else (outs, couts)


def _rms_math(x, g):
    r = lax.rsqrt(jnp.mean(x * x, axis=-1, keepdims=True) + EPS)
    return (x * r * g).astype(BF16)


def _rms_fwd(h, g, name):
    lp, d = h.shape
    tm = _row_tile(lp)

    def body(h_ref, g_ref, n_ref):
        n_ref[...] = _rms_math(h_ref[...], g_ref[...])

    return pl.pallas_call(
        body, out_shape=jax.ShapeDtypeStruct((lp, d), BF16), grid=(lp // tm,),
        in_specs=[pl.BlockSpec((tm, d), lambda i: (i, 0)), pl.BlockSpec((1, d), lambda i: (0, 0))],
        out_specs=pl.BlockSpec((tm, d), lambda i: (i, 0)),
        compiler_params=_cparams(("parallel",)), name=name)(h, g.reshape(1, d))


def _rms_bwd_math(dn, x, g):
    r = lax.rsqrt(jnp.mean(x * x, axis=-1, keepdims=True) + EPS)
    xh = x * r
    dxh = dn * g
    dx = r * (dxh - xh * jnp.mean(dxh * xh, axis=-1, keepdims=True))
    return dx, dn * xh


def _ffn_up(n, wgt, wut, comm=None):
    lp, d = n.shape
    fp = wgt.shape[1]
    tm = _row_tile(lp)

    def up_body(n_ref, wg_ref, wu_ref, a_ref, b_ref, s_ref):
        x = n_ref[...]
        for jc in range(N_CHIPS):
            cols = slice(jc * fp, (jc + 1) * fp)
            a = _dot(x, wg_ref[jc], "nt")
            b = _dot(x, wu_ref[jc], "nt")
            a_ref[:, cols] = a.astype(BF16)
            b_ref[:, cols] = b.astype(BF16)
            s_ref[:, cols] = (a * _sigmoid(a) * b).astype(BF16)

    ff = N_CHIPS * fp
    act = jax.ShapeDtypeStruct((lp, ff), BF16)
    act_tile = pl.BlockSpec((tm, ff), lambda i: (i, 0))
    w_spec = pl.BlockSpec((N_CHIPS, fp, d), lambda i: (0, 0, 0))
    outs, couts = _hosted_call(
        up_body, comm, out_shape=[act, act, act], grid=(lp // tm,),
        in_specs=[pl.BlockSpec((tm, d), lambda i: (i, 0)), w_spec, w_spec],
        out_specs=[act_tile] * 3, scratch_shapes=[], sem=("parallel",), name="ffn_up", args=(n, wgt, wut))
    return (*outs, n), couts


def _residual_outputs(h_new, in_refs, out_refs, gain_at):
    out_refs[0][...] = h_new
    if gain_at is not None:
        out_refs[1][...] = _rms_math(h_new, in_refs[gain_at][...])


def _residual_specs(lp, d, tm, next_gain):
    row = pl.BlockSpec((tm, d), lambda i: (i, 0))
    shapes, specs = [jax.ShapeDtypeStruct((lp, d), F32)], [row]
    extra_in, extra_specs = [], []
    if next_gain is not None:
        shapes.append(jax.ShapeDtypeStruct((lp, d), BF16))
        specs.append(row)
        extra_in, extra_specs = [next_gain.reshape(1, d)], [pl.BlockSpec((1, d), lambda i: (0, 0))]
    return shapes, specs, extra_in, extra_specs


def _ffn_down(s, wd, h, next_gain, comm=None):
    lp, d = h.shape
    ff = s.shape[1]
    tm = _row_tile(lp)
    shapes, specs, extra_in, extra_specs = _residual_specs(lp, d, tm, next_gain)

    def down_epi(accs, in_refs, out_refs):
        _residual_outputs(in_refs[2][...] + 0.5 * accs[0], in_refs, out_refs, 3 if extra_in else None)

    res = _matmul(
        "ffn_down", (lp // tm,), None, [s, wd.reshape(ff, d), h] + extra_in,
        [pl.BlockSpec((tm, ff), lambda i: (i, 0)), pl.BlockSpec((ff, d), lambda i: (0, 0)),
         pl.BlockSpec((tm, d), lambda i: (i, 0))] + extra_specs,
        [(0, 1, "nn", 0)], [(tm, d)], down_epi, shapes, specs, ("parallel",), comm)
    return (res, []) if comm is None else res


def _tn_tiles(lp):
    return _row_tile(lp, 1408)


def _ffn_bwd(dh_pair, h_in, gain, wgt, wut, wd, f4, saved, pos, comm=None, comm2=None):
    dh, dhb = dh_pair
    a, b, s, n = saved
    lp, d = h_in.shape
    fp = wgt.shape[1]
    ff = N_CHIPS * fp
    tm = _row_tile(lp)
    ni = lp // tm
    tk = _tn_tiles(lp)
    nk = lp // tk

    def ds_body(dh_ref, wd_ref, a_ref, b_ref, da_ref, db_ref):
        x = dh_ref[...]
        for jc in range(N_CHIPS):
            cols = slice(jc * fp, (jc + 1) * fp)
            ds = 0.5 * _dot(x, wd_ref[jc], "nt")
            av = a_ref[:, cols].astype(F32)
            bv = b_ref[:, cols].astype(F32)
            sg = _sigmoid(av)
            da_ref[:, cols] = (ds * bv * sg * (1.0 + av * (1.0 - sg))).astype(BF16)
            db_ref[:, cols] = (ds * av * sg).astype(BF16)

    act = jax.ShapeDtypeStruct((lp, ff), BF16)
    act_tile = pl.BlockSpec((tm, ff), lambda i: (i, 0))
    (da, db), couts = _hosted_call(
        ds_body, comm, out_shape=[act, act], grid=(ni,),
        in_specs=[pl.BlockSpec((tm, d), lambda i: (i, 0)), pl.BlockSpec((N_CHIPS, fp, d), lambda i: (0, 0, 0)),
                  act_tile, act_tile],
        out_specs=[act_tile, act_tile], scratch_shapes=[], sem=("parallel",), name="ffn_bwd_ds",
        args=(dhb, wd, a, b))

    dw_shape = jax.ShapeDtypeStruct((N_CHIPS, f4, d), F32)
    dw_spec = pl.BlockSpec((None, f4, d), lambda j, k: (j, 0, 0))
    in_col = pl.BlockSpec((tk, fp), lambda j, k: (k, j))
    in_row = pl.BlockSpec((tk, d), lambda j, k: (k, 0))

    half_shape = jax.ShapeDtypeStruct((N_CHIPS, f4 // 2, d), BF16)
    half_spec = pl.BlockSpec((None, f4 // 2, d), lambda j, k: (j, 0, 0))

    def dwd_epi(accs, in_refs, out_refs):
        dw = 0.5 * accs[0]
        out_refs[0][...] = dw[:f4]
        out_refs[1][...] = _sibling_half(dw, f4)

    res = _matmul(
        "ffn_dwd", (N_CHIPS, nk), 1, [s, dhb], [in_col, in_row],
        [(0, 1, "tn", 0)], [(fp, d)], dwd_epi, [dw_shape, half_shape], [dw_spec, half_spec],
        ("arbitrary", "arbitrary"), comm2)
    (dwd, dwd_other), couts2 = (res, []) if comm2 is None else res

    def dwgu_epi(accs, in_refs, out_refs):
        for i, acc in enumerate(accs):
            out_refs[i][...] = acc[:f4]
            out_refs[2 + i][...] = _sibling_half(acc, f4)

    red_down = _Reduction([dwd], [dwd_other], pos, "ffn_d")
    (dwg, dwu, dwg_other, dwu_other), red_down.got = _matmul(
        "ffn_dwgu", (N_CHIPS, nk), 1, [n, da, db], [in_row, in_col, in_col],
        [(1, 0, "tn", 0), (2, 0, "tn", 1)], [(fp, d)] * 2, dwgu_epi,
        [dw_shape, dw_shape, half_shape, half_shape], [dw_spec, dw_spec, half_spec, half_spec],
        ("arbitrary", "arbitrary"), red_down.comm())

    def dn_epi(accs, in_refs, out_refs):
        i = pl.program_id(0)
        dx, dgrow = _rms_bwd_math(accs[0], in_refs[5][...], in_refs[6][...])
        dh_new = in_refs[4][...] + dx
        out_refs[0][...] = dh_new
        out_refs[2][...] = dh_new.astype(BF16)

        @pl.when(i == 0)
        def _():
            out_refs[1][...] = jnp.zeros_like(out_refs[1])

        out_refs[1][...] += jnp.sum(dgrow, axis=0, keepdims=True)

    red = _Reduction([dwg, dwu], [dwg_other, dwu_other], pos, "ffn_gu")
    row_spec = pl.BlockSpec((tm, d), lambda i: (i, 0))
    act_spec = pl.BlockSpec((tm, ff), lambda i: (i, 0))
    w_spec = pl.BlockSpec((ff, d), lambda i: (0, 0))
    one_spec = pl.BlockSpec((1, d), lambda i: (0, 0))
    (dh_in, dgain, dh_in_b), red.got = _matmul(
        "ffn_bwd_dn", (ni,), None, [da, wgt.reshape(ff, d), db, wut.reshape(ff, d), dh, h_in, gain.reshape(1, d)],
        [act_spec, w_spec, act_spec, w_spec, row_spec, row_spec, one_spec],
        [(0, 1, "nn", 0), (2, 3, "nn", 0)], [(tm, d)], dn_epi,
        [jax.ShapeDtypeStruct((lp, d), F32), jax.ShapeDtypeStruct((1, d), F32), jax.ShapeDtypeStruct((lp, d), BF16)],
        [row_spec, one_spec, row_spec], ("arbitrary",), red.comm())
    return (dh_in, dh_in_b), dgain, [red, red_down], couts, couts2


def _rope_tables(lp):
    pos = jnp.arange(lp, dtype=F32) - float(PAD_FRONT)
    inv_freq = ROPE_THETA ** (-jnp.arange(0, ROT_DIM, 2, dtype=F32) / ROT_DIM)
    ang = pos[:, None] * inv_freq[None, :]
    cos, sin = jnp.cos(ang), jnp.sin(ang)
    half = ROT_DIM // 2
    ones = jnp.ones((lp, HEAD_DIM - ROT_DIM), F32)
    zeros_h = jnp.zeros((lp, half), F32)
    zeros_r = jnp.zeros((lp, HEAD_DIM - ROT_DIM), F32)
    c = jnp.concatenate([cos, cos, ones], axis=1)
    s1 = jnp.concatenate([-sin, zeros_h, zeros_r], axis=1)
    s2 = jnp.concatenate([zeros_h, sin, zeros_r], axis=1)
    reps = LANES // HEAD_DIM
    return jnp.stack([jnp.tile(c, (1, reps)), jnp.tile(s1, (1, reps)), jnp.tile(s2, (1, reps))])


def _rope(x, c, s1, s2):
    half = ROT_DIM // 2
    outs = []
    for ch in range(x.shape[1] // LANES):
        xc = x[:, ch * LANES:(ch + 1) * LANES]
        outs.append(xc * c + pltpu.roll(xc, LANES - half, 1) * s1 + pltpu.roll(xc, half, 1) * s2)
    return outs[0] if len(outs) == 1 else jnp.concatenate(outs, axis=1)


def _rope_t(dy, c, s1, s2):
    half = ROT_DIM // 2
    outs = []
    for ch in range(dy.shape[1] // LANES):
        dc = dy[:, ch * LANES:(ch + 1) * LANES]
        outs.append(dc * c + pltpu.roll(dc * s1, half, 1) + pltpu.roll(dc * s2, LANES - half, 1))
    return outs[0] if len(outs) == 1 else jnp.concatenate(outs, axis=1)


def _in_proj(n, w_in, tabs, ssm_w, comm=None):
    lp, d = n.shape
    inw = w_in.shape[0]
    tm = _row_tile(lp)
    o1 = ATTN_WIDTH
    o2 = o1 + KV_WIDTH
    o3 = o2 + KV_WIDTH
    o4 = o3 + ssm_w
    o5 = o4 + d

    def epi(accs, in_refs, out_refs):
        z = accs[0]
        c, s1, s2 = in_refs[2][0], in_refs[2][1], in_refs[2][2]
        out_refs[0][...] = _rope(z[:, :o1], c, s1, s2).astype(BF16)
        out_refs[1][...] = _rope(z[:, o1:o2], c, s1, s2).astype(BF16)
        out_refs[2][...] = z[:, o2:o3].astype(BF16)
        out_refs[3][...] = z[:, o3:o4]
        out_refs[4][...] = z[:, o4:o5]
        out_refs[5][...] = z[:, o5:]

    def rs(w, dt):
        return jax.ShapeDtypeStruct((lp, w), dt), pl.BlockSpec((tm, w), lambda i: (i, 0))

    shapes, specs = zip(rs(o1, BF16), rs(KV_WIDTH, BF16), rs(KV_WIDTH, BF16), rs(ssm_w, F32), rs(d, F32), rs(d, F32))
    res = _matmul(
        "mix_in_proj", (lp // tm,), None, [n, w_in, tabs],
        [pl.BlockSpec((tm, d), lambda i: (i, 0)), pl.BlockSpec((inw, d), lambda i: (0, 0)),
         pl.BlockSpec((3, tm, LANES), lambda i: (0, i, 0))],
        [(0, 1, "nt", 0)], [(tm, inw)], epi, list(shapes), list(specs), ("parallel",), comm)
    return (res, []) if comm is None else res


def _attn_mask(b):
    rows = lax.broadcasted_iota(jnp.int32, (BLOCK, 3 * BLOCK), 0)
    cols = lax.broadcasted_iota(jnp.int32, (BLOCK, 3 * BLOCK), 1)
    qpos = b * BLOCK + rows - PAD_FRONT
    kpos = (b - 1) * BLOCK + cols - PAD_FRONT
    dist = qpos - kpos
    band = (cols < 2 * BLOCK) & (kpos >= N_META) & (dist >= 0) & (dist < BLOCK)
    mrow = cols - 2 * BLOCK
    meta = (mrow >= PAD_FRONT) & ((mrow - PAD_FRONT) <= qpos)
    return band | meta


def _attn_probs(qh, kk, mask, sink):
    s = _dot(qh, kk, "nt") * (HEAD_DIM ** -0.5)
    s = jnp.where(mask, s, NEG_INF)
    m = jnp.maximum(jnp.max(s, axis=-1, keepdims=True), sink)
    e = jnp.exp(s - m)
    es = jnp.exp(sink - m)
    z = jnp.sum(e, axis=-1, keepdims=True) + es
    inv = 1.0 / z
    return e * inv, es * inv


def _head(ref_or_val, h):
    return ref_or_val[:, h * HEAD_DIM:(h + 1) * HEAD_DIM]


def _attn_fwd(q, k, v, sinks, comm=None):
    lp = q.shape[0]
    nb = lp // BLOCK

    def body(sink_ref, q_ref, kp_ref, kc_ref, km_ref, vp_ref, vc_ref, vm_ref, o_ref):
        b = pl.program_id(0)
        mask = _attn_mask(b)
        for hk in range(N_KV_HEADS):
            kk = jnp.concatenate([_head(kp_ref, hk), _head(kc_ref, hk), _head(km_ref, hk)], axis=0)
            vv = jnp.concatenate([_head(vp_ref, hk), _head(vc_ref, hk), _head(vm_ref, hk)], axis=0)
            for g in range(Q_PER_KV):
                h = hk * Q_PER_KV + g
                p, _ = _attn_probs(_head(q_ref, h), kk, mask, sink_ref[h])
                o_ref[:, h * HEAD_DIM:(h + 1) * HEAD_DIM] = _dot(p, vv, "nn").astype(BF16)

    cur = lambda b: (b, 0)
    prev = lambda b: (jnp.maximum(b - 1, 0), 0)
    first = lambda b: (0, 0)
    kvs = lambda f: pl.BlockSpec((BLOCK, KV_WIDTH), f)
    (o,), couts = _hosted_call(
        body, comm, out_shape=[jax.ShapeDtypeStruct((lp, ATTN_WIDTH), BF16)], grid=(nb,),
        in_specs=[pl.BlockSpec(memory_space=pltpu.SMEM), pl.BlockSpec((BLOCK, ATTN_WIDTH), cur),
                  kvs(prev), kvs(cur), kvs(first), kvs(prev), kvs(cur), kvs(first)],
        out_specs=[pl.BlockSpec((BLOCK, ATTN_WIDTH), cur)], scratch_shapes=[],
        sem=("parallel",), name="attn_fwd", args=(sinks, q, k, k, k, v, v, v))
    return o, couts


def _attn_bwd(q, k, v, do, sinks, tabs, comm=None):
    lp = q.shape[0]
    nb = lp // BLOCK
    scale = HEAD_DIM ** -0.5

    def body(sink_ref, q_ref, do_ref, kp_ref, kc_ref, km_ref, vp_ref, vc_ref, vm_ref, tq_ref, tk_ref, t0_ref,
             dq_ref, dk_ref, dv_ref, dkm_ref, dvm_ref, dsink_ref,
             dq_s, dkk_s, dvv_s, ck_s, cv_s, mk_s, mv_s):
        b = pl.program_id(0)

        @pl.when(b == 0)
        def _():
            for r in (ck_s, cv_s, mk_s, mv_s, dsink_ref):
                r[...] = jnp.zeros_like(r)

        @pl.when(b < nb)
        def _():
            mask = _attn_mask(b)
            for hk in range(N_KV_HEADS):
                kk = jnp.concatenate([_head(kp_ref, hk), _head(kc_ref, hk), _head(km_ref, hk)], axis=0)
                vv = jnp.concatenate([_head(vp_ref, hk), _head(vc_ref, hk), _head(vm_ref, hk)], axis=0)
                dkk = jnp.zeros((3 * BLOCK, HEAD_DIM), F32)
                dvv = jnp.zeros((3 * BLOCK, HEAD_DIM), F32)
                for g in range(Q_PER_KV):
                    h = hk * Q_PER_KV + g
                    qh = _head(q_ref, h)
                    doh = _head(do_ref, h)
                    p, ps = _attn_probs(qh, kk, mask, sink_ref[h])
                    dp = _dot(doh, vv, "nt")
                    delta = jnp.sum(p * dp, axis=-1, keepdims=True)
                    ds = (p * (dp - delta)).astype(BF16)
                    dsink_ref[h:h + 1, :] += jnp.zeros((1, LANES), F32) - jnp.sum(ps * delta)
                    dq_s[:, h * HEAD_DIM:(h + 1) * HEAD_DIM] = _dot(ds, kk, "nn") * scale
                    dkk = dkk + _dot(ds, qh, "tn") * scale
                    dvv = dvv + _dot(p, doh, "tn")
                dkk_s[:, hk * HEAD_DIM:(hk + 1) * HEAD_DIM] = dkk
                dvv_s[:, hk * HEAD_DIM:(hk + 1) * HEAD_DIM] = dvv
            dq_ref[...] = _rope_t(dq_s[...], tq_ref[0], tq_ref[1], tq_ref[2])
            dk_ref[...] = _rope_t(ck_s[...] + dkk_s[0:BLOCK, :], tk_ref[0], tk_ref[1], tk_ref[2])
            dv_ref[...] = cv_s[...] + dvv_s[0:BLOCK, :]
            ck_s[...] = dkk_s[BLOCK:2 * BLOCK, :]
            cv_s[...] = dvv_s[BLOCK:2 * BLOCK, :]
            mk_s[...] += dkk_s[2 * BLOCK:, :]
            mv_s[...] += dvv_s[2 * BLOCK:, :]

        @pl.when(b == nb)
        def _():
            dk_ref[...] = _rope_t(ck_s[...], tk_ref[0], tk_ref[1], tk_ref[2])
            dv_ref[...] = cv_s[...]
            dkm_ref[...] = _rope_t(mk_s[...], t0_ref[0], t0_ref[1], t0_ref[2])
            dvm_ref[...] = mv_s[...]

    cur = lambda b: (jnp.minimum(b, nb - 1), 0)
    prev = lambda b: (jnp.clip(b - 1, 0, nb - 1), 0)
    first = lambda b: (0, 0)
    kvs = lambda f: pl.BlockSpec((BLOCK, KV_WIDTH), f)
    tab = lambda f: pl.BlockSpec((3, BLOCK, LANES), lambda b: (0,) + f(b)[:1] + (0,))
    kv_out = lambda b: (jnp.maximum(b - 1, 0), 0)
    return _hosted_call(
        body, comm,
        out_shape=[jax.ShapeDtypeStruct((lp, ATTN_WIDTH), F32), jax.ShapeDtypeStruct((lp, KV_WIDTH), F32),
                   jax.ShapeDtypeStruct((lp, KV_WIDTH), F32), jax.ShapeDtypeStruct((BLOCK, KV_WIDTH), F32),
                   jax.ShapeDtypeStruct((BLOCK, KV_WIDTH), F32), jax.ShapeDtypeStruct((N_Q_HEADS, LANES), F32)],
        grid=(nb + 1,),
        in_specs=[pl.BlockSpec(memory_space=pltpu.SMEM), pl.BlockSpec((BLOCK, ATTN_WIDTH), cur),
                  pl.BlockSpec((BLOCK, ATTN_WIDTH), cur),
                  kvs(prev), kvs(cur), kvs(first), kvs(prev), kvs(cur), kvs(first),
                  tab(cur), tab(kv_out), tab(first)],
        out_specs=[pl.BlockSpec((BLOCK, ATTN_WIDTH), cur), kvs(kv_out), kvs(kv_out), kvs(first), kvs(first),
                   pl.BlockSpec((N_Q_HEADS, LANES), first)],
        scratch_shapes=[pltpu.VMEM((BLOCK, ATTN_WIDTH), F32), pltpu.VMEM((3 * BLOCK, KV_WIDTH), F32),
                        pltpu.VMEM((3 * BLOCK, KV_WIDTH), F32), pltpu.VMEM((BLOCK, KV_WIDTH), F32),
                        pltpu.VMEM((BLOCK, KV_WIDTH), F32), pltpu.VMEM((BLOCK, KV_WIDTH), F32),
                        pltpu.VMEM((BLOCK, KV_WIDTH), F32)],
        sem=("arbitrary",), name="attn_bwd", args=(sinks, q, do, k, k, k, v, v, v, tabs, tabs, tabs))


def _cmul(ar, ai, br, bi):
    return ar * br - ai * bi, ar * bi + ai * br


def _cpow(lr, li, n):
    rr = ri = None
    br, bi = lr, li
    while n:
        if n & 1:
            rr, ri = (br, bi) if rr is None else _cmul(rr, ri, br, bi)
        n >>= 1
        if n:
            br, bi = _cmul(br, bi, br, bi)
    return rr, ri


def _shift_rows(x, d, reverse):
    rows = lax.broadcasted_iota(jnp.int32, x.shape, 0)
    if not reverse:
        return jnp.where(rows >= d, pltpu.roll(x, d, 0), 0.0)
    return jnp.where(rows < SUBLANES - d, pltpu.roll(x, SUBLANES - d, 0), 0.0)


def _sublane_powers(mr, mi, reverse):
    rows = lax.broadcasted_iota(jnp.int32, mr.shape, 0)
    e = SUBLANES - 1 - rows if reverse else rows
    pr, pi = jnp.ones_like(mr), jnp.zeros_like(mr)
    br, bi = mr, mi
    for d in (1, 2, 4):
        tr, ti = _cmul(pr, pi, br, bi)
        on = (e & d) != 0
        pr, pi = jnp.where(on, tr, pr), jnp.where(on, ti, pi)
        if d < 4:
            br, bi = _cmul(br, bi, br, bi)
    return pr, pi


def _inclusive_prefix(er, ei, mr, mi, reverse):
    ir, ii, pr, pi = er, ei, mr, mi
    for d in (1, 2, 4):
        tr, ti = _cmul(pr, pi, _shift_rows(ir, d, reverse), _shift_rows(ii, d, reverse))
        ir, ii = ir + tr, ii + ti
        if d < 4:
            pr, pi = _cmul(pr, pi, pr, pi)
    return ir, ii


def _chain_rows(a, t, seg):
    return pl.ds(a * SUBLANES * seg + t, SUBLANES, stride=seg)


def _seg_scan(xr_ref, xi_ref, lam, seg, nchain, reverse, store, init, extra=None):
    nt = len(lam)
    acc0 = () if extra is None else extra[1]

    def step(i, carry):
        hs, acc = carry
        t = seg - 1 - i if reverse else i
        out = []
        for a in range(nchain):
            sl = _chain_rows(a, t, seg)
            for j in range(nt):
                lr, li = lam[j]
                k = 2 * (a * nt + j)
                hr, hi = hs[k], hs[k + 1]
                nr = lr * hr - li * hi + xr_ref[j, sl, :]
                ni = lr * hi + li * hr + xi_ref[j, sl, :]
                if store:
                    xr_ref[j, sl, :] = nr
                    xi_ref[j, sl, :] = ni
                if extra is not None:
                    acc = extra[0](t, a, j, nr, ni, acc)
                out += [nr, ni]
        return tuple(out), acc

    return lax.fori_loop(0, seg, step, (tuple(init), acc0))


def _ssm_scan(xr_ref, xi_ref, lam, seg, nchain, reverse, extra=None):
    nt = len(lam)
    zero = [jnp.zeros((SUBLANES, LANES), F32)] * (2 * nt * nchain)
    ends, _ = _seg_scan(xr_ref, xi_ref, lam, seg, nchain, reverse, False, zero)
    init = [None] * (2 * nt * nchain)
    last = 0 if reverse else SUBLANES - 1
    for j in range(nt):
        mr, mi = _cpow(lam[j][0], lam[j][1], seg)
        m8r, m8i = _cpow(mr, mi, SUBLANES)
        pwr, pwi = _sublane_powers(mr, mi, reverse)
        gr = gi = jnp.zeros((SUBLANES, LANES), F32)
        for a in (reversed(range(nchain)) if reverse else range(nchain)):
            k = 2 * (a * nt + j)
            incr, inci = _inclusive_prefix(ends[k], ends[k + 1], mr, mi, reverse)
            tr, ti = _cmul(pwr, pwi, gr, gi)
            init[k] = _shift_rows(incr, 1, reverse) + tr
            init[k + 1] = _shift_rows(inci, 1, reverse) + ti
            g2r, g2i = _cmul(m8r, m8i, gr, gi)
            gr = g2r + jnp.broadcast_to(incr[last:last + 1, :], gr.shape)
            gi = g2i + jnp.broadcast_to(inci[last:last + 1, :], gi.shape)
    _, acc = _seg_scan(xr_ref, xi_ref, lam, seg, nchain, reverse, True, init, extra)
    return acc


def _diag_mask():
    steps = LANES // SSM_GROUP // GROUPS_PER_TILE
    return (jnp.eye(steps, dtype=F32)[:, None, :, None] * jnp.eye(GROUPS_PER_TILE, dtype=F32)[None, :, None, :])


def _ssm_tables(bb_re, bb_im, c_re, c_im):
    g = bb_re.shape[0]
    nt = g // GROUPS_PER_TILE
    steps = LANES // SSM_GROUP // GROUPS_PER_TILE
    mask = _diag_mask()

    def b_tab(bb):
        x = bb.reshape(nt // steps, steps, GROUPS_PER_TILE, SSM_STATE, SSM_GROUP)
        x = jnp.transpose(x, (0, 1, 4, 2, 3))[:, :, None, None]
        m = jnp.transpose(mask, (0, 2, 3, 1))[None, :, :, :, None, :, None]
        return (x * m).reshape(nt, LANES, TILE_STATES)

    def c_tab(c):
        x = c.reshape(nt // steps, steps, GROUPS_PER_TILE, SSM_GROUP, SSM_STATE)
        x = jnp.transpose(x, (0, 1, 2, 4, 3))[:, :, :, :, None, None]
        m = mask[None, :, :, None, :, :, None]
        return (x * m).reshape(nt, TILE_STATES, LANES)

    return b_tab(bb_re), b_tab(bb_im), c_tab(c_re), c_tab(c_im)


def _ssm_untable_b(db, g):
    nt = g // GROUPS_PER_TILE
    steps = LANES // SSM_GROUP // GROUPS_PER_TILE
    x = db.reshape(nt // steps, steps, GROUPS_PER_TILE, SSM_STATE, steps, GROUPS_PER_TILE, SSM_GROUP)
    m = _diag_mask()[None, :, :, None, :, :, None]
    return jnp.sum(x * m, axis=(4, 5)).reshape(g, SSM_STATE, SSM_GROUP)


def _ssm_untable_c(dc, g):
    nt = g // GROUPS_PER_TILE
    steps = LANES // SSM_GROUP // GROUPS_PER_TILE
    x = dc.reshape(nt // steps, steps, steps, GROUPS_PER_TILE, SSM_GROUP, GROUPS_PER_TILE, SSM_STATE)
    m = jnp.transpose(_diag_mask(), (0, 2, 3, 1))[None, :, :, :, None, :, None]
    out = jnp.sum(x * m, axis=(2, 3))
    return jnp.transpose(out, (0, 1, 3, 2, 4)).reshape(g, SSM_GROUP, SSM_STATE)


def _lam_tiles(lam_ref):
    out = []
    for j in range(TILE_STATES // LANES):
        out.append(jnp.broadcast_to(lam_ref[:, j * LANES:(j + 1) * LANES], (SUBLANES, LANES)))
    return out


def _scan_chains(lp):
    for n in (4, 2, 1):
        if lp % (SUBLANES * n) == 0 and (lp // SUBLANES) % 16 == 0:
            return n
    raise ValueError(lp)


def _split_tiles(dst_ref, rows, val):
    for j in range(val.shape[1] // LANES):
        dst_ref[j, rows, :] = val[:, j * LANES:(j + 1) * LANES]


def _cat_tiles(src_ref, rows):
    njt = src_ref.shape[0]
    return jnp.concatenate([src_ref[j, rows, :] for j in range(njt)], axis=1).astype(BF16)


def _ssm_fwd(u, lam_re, lam_im, tb_re, tb_im, tc_re, tc_im, d_skip, comm=None):
    lp, w = u.shape
    nt = tb_re.shape[0]
    nchain = _scan_chains(lp)
    seg = lp // (SUBLANES * nchain)
    chunk = lp // SUBLANES
    njt = TILE_STATES // LANES

    def body(u_ref, lr_ref, li_ref, br_ref, bi_ref, cr_ref, ci_ref, d_ref, y_ref, yg_ref, xr, xi):
        t = pl.program_id(0)
        for s in range(SUBLANES):
            rs = pl.ds(s * chunk, chunk)
            ub = u_ref[rs, :].astype(BF16)
            _split_tiles(xr, rs, _dot(ub, br_ref[...], "nn"))
            _split_tiles(xi, rs, _dot(ub, bi_ref[...], "nn"))
        lrs, lis = _lam_tiles(lr_ref), _lam_tiles(li_ref)
        _ssm_scan(xr, xi, list(zip(lrs, lis)), seg, nchain, False)
        for s in range(SUBLANES):
            rs = pl.ds(s * chunk, chunk)
            y = _dot(_cat_tiles(xr, rs), cr_ref[...], "nn") - _dot(_cat_tiles(xi, rs), ci_ref[...], "nn")

            @pl.when(t % 2 == 0)
            def _():
                y_ref[rs, :] = y + d_ref[...] * u_ref[rs, :]

            @pl.when(t % 2 == 1)
            def _():
                total = y_ref[rs, :] + y
                y_ref[rs, :] = total
                yg_ref[rs, :] = _gelu(total).astype(BF16)

    blk = pl.BlockSpec((lp, LANES), lambda t: (0, t // 2))
    lam_spec = pl.BlockSpec((None, 1, TILE_STATES), lambda t: (t, 0, 0))
    b_spec = pl.BlockSpec((None, LANES, TILE_STATES), lambda t: (t, 0, 0))
    c_spec = pl.BlockSpec((None, TILE_STATES, LANES), lambda t: (t, 0, 0))
    (y, yg), couts = _hosted_call(
        body, comm, out_shape=[jax.ShapeDtypeStruct((lp, w), F32), jax.ShapeDtypeStruct((lp, w), BF16)], grid=(nt,),
        in_specs=[blk, lam_spec, lam_spec, b_spec, b_spec, c_spec, c_spec,
                  pl.BlockSpec((1, LANES), lambda t: (0, t // 2))],
        out_specs=[blk, blk],
        scratch_shapes=[pltpu.VMEM((njt, lp, LANES), F32), pltpu.VMEM((njt, lp, LANES), F32)],
        sem=("arbitrary",), name="ssm_fwd",
        args=(u, lam_re, lam_im, tb_re, tb_im, tc_re, tc_im, d_skip.reshape(1, w)))
    return (y, yg), couts


def _ssm_bwd(u, dy, lam_re, lam_im, tb_re, tb_im, tc_re, tc_im, d_skip, comm=None):
    lp, w = u.shape
    nt = tb_re.shape[0]
    nchain = _scan_chains(lp)
    seg = lp // (SUBLANES * nchain)
    chunk = lp // SUBLANES
    njt = TILE_STATES // LANES
    tbt_re, tbt_im = jnp.swapaxes(tb_re, 1, 2), jnp.swapaxes(tb_im, 1, 2)
    tct_re, tct_im = jnp.swapaxes(tc_re, 1, 2), jnp.swapaxes(tc_im, 1, 2)

    def body(u_ref, dy_ref, lr_ref, li_ref, br_ref, bi_ref, btr_ref, bti_ref, ctr_ref, cti_ref, d_ref,
             du_ref, dlr_ref, dli_ref, dbr_ref, dbi_ref, dcr_ref, dci_ref, dd_ref, hr, hi, ar, ai):
        t = pl.program_id(0)
        lrs, lis = _lam_tiles(lr_ref), _lam_tiles(li_ref)
        for s in range(SUBLANES):
            rs = pl.ds(s * chunk, chunk)
            ub = u_ref[rs, :].astype(BF16)
            dyb = dy_ref[rs, :].astype(BF16)
            _split_tiles(hr, rs, _dot(ub, br_ref[...], "nn"))
            _split_tiles(hi, rs, _dot(ub, bi_ref[...], "nn"))
            _split_tiles(ar, rs, _dot(dyb, ctr_ref[...], "nn"))
            _split_tiles(ai, rs, -_dot(dyb, cti_ref[...], "nn"))
        _ssm_scan(hr, hi, list(zip(lrs, lis)), seg, nchain, False)

        def dlam_step(tt, a, j, a_r, a_i, acc):
            sl = _chain_rows(a, jnp.maximum(tt - 1, 0), seg)
            p_r, p_i = hr[j, sl, :], hi[j, sl, :]
            acc = list(acc)
            acc[2 * j] = acc[2 * j] + jnp.where(tt > 0, a_r * p_r + a_i * p_i, 0.0)
            acc[2 * j + 1] = acc[2 * j + 1] + jnp.where(tt > 0, a_i * p_r - a_r * p_i, 0.0)
            return tuple(acc)

        zero = tuple([jnp.zeros((SUBLANES, LANES), F32)] * (2 * njt))
        conj = [(lr, -li) for lr, li in zip(lrs, lis)]
        acc = list(_ssm_scan(ar, ai, conj, seg, nchain, True, (dlam_step, zero)))
        row0 = lax.broadcasted_iota(jnp.int32, (SUBLANES, LANES), 0) == 0
        for j in range(njt):
            cs = slice(j * LANES, (j + 1) * LANES)
            for a in range(nchain):
                p_r = _shift_rows(hr[j, _chain_rows(a, seg - 1, seg), :], 1, False)
                p_i = _shift_rows(hi[j, _chain_rows(a, seg - 1, seg), :], 1, False)
                if a > 0:
                    before = pl.ds(a * SUBLANES * seg - 1, 1)
                    p_r = jnp.where(row0, jnp.broadcast_to(hr[j, before, :], p_r.shape), p_r)
                    p_i = jnp.where(row0, jnp.broadcast_to(hi[j, before, :], p_i.shape), p_i)
                a_r, a_i = ar[j, _chain_rows(a, 0, seg), :], ai[j, _chain_rows(a, 0, seg), :]
                acc[2 * j] = acc[2 * j] + a_r * p_r + a_i * p_i
                acc[2 * j + 1] = acc[2 * j + 1] + a_i * p_r - a_r * p_i
            dlr_ref[:, cs] = jnp.sum(acc[2 * j], axis=0, keepdims=True)
            dli_ref[:, cs] = jnp.sum(acc[2 * j + 1], axis=0, keepdims=True)

        dd = jnp.zeros((1, LANES), F32)
        for s in range(SUBLANES):
            rs = pl.ds(s * chunk, chunk)
            ub = u_ref[rs, :].astype(BF16)
            dyv = dy_ref[rs, :]
            dyb = dyv.astype(BF16)
            arb, aib = _cat_tiles(ar, rs), _cat_tiles(ai, rs)
            hrb, hib = _cat_tiles(hr, rs), _cat_tiles(hi, rs)
            du = _dot(arb, btr_ref[...], "nn") + _dot(aib, bti_ref[...], "nn")
            upd = [(dbr_ref, _dot(arb, ub, "tn")), (dbi_ref, _dot(aib, ub, "tn")),
                   (dcr_ref, _dot(dyb, hrb, "tn")), (dci_ref, -_dot(dyb, hib, "tn"))]
            for ref, val in upd:
                if s == 0:
                    ref[...] = val
                else:
                    ref[...] += val
            rows = lax.broadcasted_iota(jnp.int32, (chunk, LANES), 0) + s * chunk
            keep = rows >= PAD_FRONT
            dd = dd + jnp.sum(dyv * u_ref[rs, :], axis=0, keepdims=True)

            @pl.when(t % 2 == 0)
            def _():
                du_ref[rs, :] = jnp.where(keep, du + d_ref[...] * dyv, 0.0)

            @pl.when(t % 2 == 1)
            def _():
                du_ref[rs, :] += jnp.where(keep, du, 0.0)

        @pl.when(t % 2 == 0)
        def _():
            dd_ref[...] = dd

    blk = pl.BlockSpec((lp, LANES), lambda t: (0, t // 2))
    vec = pl.BlockSpec((1, LANES), lambda t: (0, t // 2))
    lam_spec = pl.BlockSpec((None, 1, TILE_STATES), lambda t: (t, 0, 0))
    b_spec = pl.BlockSpec((None, LANES, TILE_STATES), lambda t: (t, 0, 0))
    c_spec = pl.BlockSpec((None, TILE_STATES, LANES), lambda t: (t, 0, 0))
    lam_shape = jax.ShapeDtypeStruct((nt, 1, TILE_STATES), F32)
    bt_shape = jax.ShapeDtypeStruct((nt, TILE_STATES, LANES), F32)
    ct_shape = jax.ShapeDtypeStruct((nt, LANES, TILE_STATES), F32)
    st = pltpu.VMEM((njt, lp, LANES), F32)
    return _hosted_call(
        body, comm,
        out_shape=[jax.ShapeDtypeStruct((lp, w), F32), lam_shape, lam_shape, bt_shape, bt_shape, ct_shape, ct_shape,
                   jax.ShapeDtypeStruct((1, w), F32)],
        grid=(nt,),
        in_specs=[blk, blk, lam_spec, lam_spec, b_spec, b_spec, c_spec, c_spec, b_spec, b_spec, vec],
        out_specs=[blk, lam_spec, lam_spec, c_spec, c_spec, b_spec, b_spec, vec],
        scratch_shapes=[st, st, st, st], sem=("arbitrary",), name="ssm_bwd",
        args=(u, dy, lam_re, lam_im, tb_re, tb_im, tbt_re, tbt_im, tct_re, tct_im, d_skip.reshape(1, w)))


def _ssm_params(a_re, a_im, log_dt, b_re, b_im):
    dt = jnp.exp(log_dt)[:, None]
    mag = jnp.exp(a_re * dt)
    lb_re = mag * jnp.cos(a_im * dt)
    lb_im = mag * jnp.sin(a_im * dt)
    den = a_re * a_re + a_im * a_im
    num_re = lb_re - 1.0
    coef_re = (num_re * a_re + lb_im * a_im) / den
    coef_im = (lb_im * a_re - num_re * a_im) / den
    bb_re = coef_re[..., None] * b_re - coef_im[..., None] * b_im
    bb_im = coef_re[..., None] * b_im + coef_im[..., None] * b_re
    return lb_re, lb_im, bb_re, bb_im


def _merge_fwd(o, yg, ga, gs, w3t, comm=None):
    lp, d = ga.shape
    kw = w3t.shape[2]
    tm = _row_tile(lp)

    def epi(accs, in_refs, out_refs):
        attn, vv, gg = accs
        out_refs[0][...] = (_sigmoid(in_refs[5][...]) * attn
                            + _sigmoid(in_refs[6][...]) * (vv * _sigmoid(gg))).astype(BF16)

    wspec = lambda which: pl.BlockSpec((None, d, kw), lambda i: (which, 0, 0))
    rowspec = pl.BlockSpec((tm, d), lambda i: (i, 0))
    aspec = pl.BlockSpec((tm, kw), lambda i: (i, 0))
    res = _matmul(
        "merge_fwd", (lp // tm,), None, [o, yg, w3t, w3t, w3t, ga, gs],
        [aspec, aspec, wspec(0), wspec(1), wspec(2), rowspec, rowspec],
        [(0, 2, "nt", 0), (1, 3, "nt", 1), (1, 4, "nt", 2)], [(tm, d)] * 3, epi,
        [jax.ShapeDtypeStruct((lp, d), BF16)], [rowspec], ("parallel",), comm)
    return (res[0], []) if comm is None else (res[0][0], res[1])


def _out_proj(merged, w_out, h, next_gain, comm=None):
    lp, d = h.shape
    tm = _row_tile(lp)
    shapes, specs, extra_in, extra_specs = _residual_specs(lp, d, tm, next_gain)

    def epi(accs, in_refs, out_refs):
        _residual_outputs(in_refs[2][...] + accs[0], in_refs, out_refs, 3 if extra_in else None)

    rowspec = pl.BlockSpec((tm, d), lambda i: (i, 0))
    res = _matmul(
        "mix_out_proj", (lp // tm,), None, [merged, w_out, h] + extra_in,
        [rowspec, pl.BlockSpec((d, d), lambda i: (0, 0)), rowspec] + extra_specs,
        [(0, 1, "nn", 0)], [(tm, d)], epi, shapes, specs, ("parallel",), comm)
    return (res, []) if comm is None else res


def _merge_bwd(dhb, w_out, o, yg, ga, gs, w3t):
    lp, d = ga.shape
    kw = w3t.shape[2]
    tm = _row_tile(lp)

    def epi(accs, in_refs, out_refs):
        dm, attn, vv, gg = accs
        sa = _sigmoid(in_refs[7][...])
        ss = _sigmoid(in_refs[8][...])
        sg = _sigmoid(gg)
        ssm = vv * sg
        dssm = dm * ss
        out_refs[0][...] = (dm * sa).astype(BF16)
        out_refs[1][...] = (dssm * sg).astype(BF16)
        out_refs[2][...] = (dssm * vv * sg * (1.0 - sg)).astype(BF16)
        out_refs[3][...] = (dm * attn * sa * (1.0 - sa)).astype(BF16)
        out_refs[4][...] = (dm * ssm * ss * (1.0 - ss)).astype(BF16)

    wspec = lambda which: pl.BlockSpec((None, d, kw), lambda i: (which, 0, 0))
    rowspec = pl.BlockSpec((tm, d), lambda i: (i, 0))
    aspec = pl.BlockSpec((tm, kw), lambda i: (i, 0))
    shp = jax.ShapeDtypeStruct((lp, d), BF16)
    return _matmul(
        "merge_bwd", (lp // tm,), None, [dhb, w_out, o, yg, w3t, w3t, w3t, ga, gs],
        [rowspec, pl.BlockSpec((d, d), lambda i: (0, 0)), aspec, aspec, wspec(0), wspec(1), wspec(2), rowspec,
         rowspec],
        [(0, 1, "nt", 0), (2, 4, "nt", 1), (3, 5, "nt", 2), (3, 6, "nt", 3)], [(tm, d)] * 4, epi,
        [shp] * 5, [rowspec] * 5, ("parallel",))


def _branch_bwd(dattn, dv, dg, w3t, y):
    lp, d = dattn.shape
    kw = w3t.shape[2]
    tm = _row_tile(lp)

    def epi(accs, in_refs, out_refs):
        out_refs[0][...] = accs[0].astype(BF16)
        out_refs[1][...] = accs[1] * _gelu_grad(in_refs[6][...])

    wspec = lambda which: pl.BlockSpec((None, d, kw), lambda i: (which, 0, 0))
    rowspec = pl.BlockSpec((tm, d), lambda i: (i, 0))
    aspec = pl.BlockSpec((tm, kw), lambda i: (i, 0))
    return _matmul(
        "branch_bwd", (lp // tm,), None, [dattn, dv, dg, w3t, w3t, w3t, y],
        [rowspec, rowspec, rowspec, wspec(0), wspec(1), wspec(2), aspec],
        [(0, 3, "nn", 0), (1, 4, "nn", 1), (2, 5, "nn", 1)], [(tm, kw)] * 2, epi,
        [jax.ShapeDtypeStruct((lp, kw), BF16), jax.ShapeDtypeStruct((lp, kw), F32)], [aspec, aspec],
        ("parallel",))


def _sibling_half(acc, rows):
    half = rows // 2
    return jnp.where(lax.axis_index("c") == 0, acc[half:rows], acc[:half]).astype(BF16)


def _tn_cols(x, ys, name):
    lp, kx = x.shape
    n = ys[0].shape[1]
    n4 = n // N_CHIPS
    tk = _tn_tiles(lp)
    ny = len(ys)

    def epi(accs, in_refs, out_refs):
        for i, acc in enumerate(accs):
            out_refs[i][...] = acc
            out_refs[ny + i][...] = _sibling_half(acc, kx)

    shp = jax.ShapeDtypeStruct((N_CHIPS, kx, n4), F32)
    shp_half = jax.ShapeDtypeStruct((N_CHIPS, kx // 2, n4), BF16)
    res = _matmul(
        name, (N_CHIPS, lp // tk), 1, [x] + list(ys),
        [pl.BlockSpec((tk, kx), lambda j, k: (k, 0))] + [pl.BlockSpec((tk, n4), lambda j, k: (k, j))] * ny,
        [(0, 1 + i, "tn", i) for i in range(ny)], [(kx, n4)] * ny, epi,
        [shp] * ny + [shp_half] * ny,
        [pl.BlockSpec((None, kx, n4), lambda j, k: (j, 0, 0))] * ny
        + [pl.BlockSpec((None, kx // 2, n4), lambda j, k: (j, 0, 0))] * ny,
        ("arbitrary", "arbitrary"))
    return res[:ny], res[ny:]


def _tn_full(x, y, name, tn_cols=None):
    lp, kx = x.shape
    n = y.shape[1]
    tk = _tn_tiles(lp)
    tn = n if tn_cols is None else tn_cols
    k4 = kx // N_CHIPS

    def epi(accs, in_refs, out_refs):
        for j in range(N_CHIPS):
            slab = accs[0][j * k4:(j + 1) * k4]
            out_refs[0][j] = slab
            out_refs[1][j] = _sibling_half(slab, k4)

    return _matmul(
        name, (n // tn, lp // tk), 1, [x, y],
        [pl.BlockSpec((tk, kx), lambda j, k: (k, 0)), pl.BlockSpec((tk, tn), lambda j, k: (k, j))],
        [(0, 1, "tn", 0)], [(kx, tn)], epi,
        [jax.ShapeDtypeStruct((N_CHIPS, k4, n), F32), jax.ShapeDtypeStruct((N_CHIPS, k4 // 2, n), BF16)],
        [pl.BlockSpec((N_CHIPS, k4, tn), lambda j, k: (0, 0, j)),
         pl.BlockSpec((N_CHIPS, k4 // 2, tn), lambda j, k: (0, 0, j))],
        ("arbitrary", "arbitrary"))


def _in_proj_bwd(dz, w_in, dh, h_in, gain):
    lp, d = h_in.shape
    inw = w_in.shape[0]
    tm = _row_tile(lp)

    def epi(accs, in_refs, out_refs):
        i = pl.program_id(0)
        dx, dgrow = _rms_bwd_math(accs[0], in_refs[3][...], in_refs[4][...])
        dh_new = in_refs[2][...] + dx
        out_refs[0][...] = dh_new
        out_refs[2][...] = dh_new.astype(BF16)

        @pl.when(i == 0)
        def _():
            out_refs[1][...] = jnp.zeros_like(out_refs[1])

        out_refs[1][...] += jnp.sum(dgrow, axis=0, keepdims=True)

    row = pl.BlockSpec((tm, d), lambda i: (i, 0))
    return _matmul(
        "mix_in_proj_bwd", (lp // tm,), None, [dz, w_in, dh, h_in, gain.reshape(1, d)],
        [pl.BlockSpec((tm, inw), lambda i: (i, 0)), pl.BlockSpec((inw, d), lambda i: (0, 0)), row, row,
         pl.BlockSpec((1, d), lambda i: (0, 0))],
        [(0, 1, "nn", 0)], [(tm, d)], epi,
        [jax.ShapeDtypeStruct((lp, d), F32), jax.ShapeDtypeStruct((1, d), F32), jax.ShapeDtypeStruct((lp, d), BF16)],
        [row, pl.BlockSpec((1, d), lambda i: (0, 0)), row], ("arbitrary",))


def _loss_head(h, gain, target):
    lp, d = h.shape
    nb = lp // BLOCK

    def body(h_ref, g_ref, t_ref, dh_ref, dg_ref, loss_ref, dhb_ref):
        i = pl.program_id(0)

        @pl.when(i == 0)
        def _():
            dg_ref[...] = jnp.zeros_like(dg_ref)
            loss_ref[...] = jnp.zeros_like(loss_ref)
            dh_ref[...] = jnp.zeros_like(dh_ref)
            dhb_ref[...] = jnp.zeros_like(dhb_ref)

        @pl.when(i > 0)
        def _():
            x = h_ref[...]
            g = g_ref[...]
            r = lax.rsqrt(jnp.mean(x * x, axis=-1, keepdims=True) + EPS)
            err = x * r * g - t_ref[...]
            loss_ref[...] += jnp.zeros_like(loss_ref) + 0.5 * jnp.sum(jnp.sum(err * err, axis=-1, keepdims=True)) / d
            dx, dgrow = _rms_bwd_math(err * (1.0 / d), x, g)
            dh_ref[...] = dx
            dhb_ref[...] = dx.astype(BF16)
            dg_ref[...] += jnp.sum(dgrow, axis=0, keepdims=True)

    row = pl.BlockSpec((BLOCK, d), lambda i: (i, 0))
    one = pl.BlockSpec((1, d), lambda i: (0, 0))
    return pl.pallas_call(
        body,
        out_shape=[jax.ShapeDtypeStruct((lp, d), F32), jax.ShapeDtypeStruct((1, d), F32),
                   jax.ShapeDtypeStruct((SUBLANES, LANES), F32), jax.ShapeDtypeStruct((lp, d), BF16)],
        grid=(nb,),
        in_specs=[row, one, pl.BlockSpec((BLOCK, d), lambda i: (jnp.maximum(i - 1, 0), 0))],
        out_specs=[row, one, pl.BlockSpec((SUBLANES, LANES), lambda i: (0, 0)), row],
        compiler_params=_cparams(("arbitrary",)), name="loss_head")(h, gain.reshape(1, d), target)


def _adam_math(w, g, m, v):
    m = ADAM_B1 * m + (1.0 - ADAM_B1) * g
    v = ADAM_B2 * v + (1.0 - ADAM_B2) * (g * g)
    m_hat = m / (1.0 - ADAM_B1 ** ADAM_STEP)
    v_hat = v / (1.0 - ADAM_B2 ** ADAM_STEP)
    delta = -ADAM_LR * (m_hat / (jnp.sqrt(v_hat) + ADAM_EPS) + ADAM_WD * w)
    return delta, m, v


def _adamw_layers(w, m, v, mine, other, pos, name):
    depth, r, c = w.shape
    half = r // 2
    tr = _div_tile(half, c * 4)
    nh = half // tr

    def body(*refs):
        pos_ref, w_ref, m_ref, v_ref = refs[:4]
        mine_refs = refs[4:4 + depth]
        other_refs = refs[4 + depth:4 + 2 * depth]
        g_out, d_out, m_out, v_out = refs[4 + 2 * depth:]
        layer, i = pl.program_id(0), pl.program_id(1)
        is_mine = (i // nh) == pos_ref[0]

        def update(g):
            delta, nm, nv = _adam_math(w_ref[...], g, m_ref[...], v_ref[...])
            g_out[...] = g
            d_out[...] = delta
            m_out[...] = nm
            v_out[...] = nv

        for l in range(depth):
            @pl.when((layer == l) & is_mine)
            def _(l=l):
                update(mine_refs[l][...])

            @pl.when((layer == l) & jnp.logical_not(is_mine))
            def _(l=l):
                update(other_refs[l][...])

    stacked = pl.BlockSpec((None, tr, c), lambda l, i, p: (l, i, 0))

    def gspec(layer, is_other):
        def imap(l, i, p):
            first = jnp.where(is_other, 1 - p[0], p[0]) * nh
            here = jnp.clip(i - first, 0, nh - 1)
            return (jnp.where(l == layer, here, jnp.where(l < layer, 0, nh - 1)), 0)
        return pl.BlockSpec((tr, c), imap)

    shp = jax.ShapeDtypeStruct((depth, r, c), F32)
    grid_spec = pltpu.PrefetchScalarGridSpec(
        num_scalar_prefetch=1, grid=(depth, 2 * nh),
        in_specs=[stacked] * 3 + [gspec(l, 0) for l in range(depth)] + [gspec(l, 1) for l in range(depth)],
        out_specs=[stacked] * 4)
    return pl.pallas_call(
        body, out_shape=[shp] * 4, grid_spec=grid_spec,
        compiler_params=_cparams(("arbitrary", "arbitrary")), name=name)(pos, w, m, v, *mine, *other)


def _adamw_whole(w, g, m, v, name):
    def body(w_ref, g_ref, m_ref, v_ref, d_out, m_out, v_out):
        delta, nm, nv = _adam_math(w_ref[...], g_ref[...], m_ref[...], v_ref[...])
        d_out[...] = delta
        m_out[...] = nm
        v_out[...] = nv

    shp = jax.ShapeDtypeStruct(w.shape, F32)
    return pl.pallas_call(body, out_shape=[shp] * 3, compiler_params=_cparams(), name=name)(w, g, m, v)


def _mesh_pos():
    return lax.axis_index("x"), lax.axis_index("y"), lax.axis_index("c")


def _row_half(ref, which, lead):
    half = ref.shape[lead] // 2
    idx = (slice(None),) * lead + (pl.ds(which * half, half), slice(None))
    return ref.at[idx]


def _gather_comm(arrs, tag):
    n = len(arrs)

    def ctx(ins, outs, sems):
        send_sems, recv_sems, local_sems = sems
        x, y, c = _mesh_pos()
        chips = [(1 - x, y), (x, 1 - y), (1 - x, 1 - y)]

        def slot(k, chip, which):
            lead = len(ins[k].shape) - 2
            return _row_half(outs[k].at[2 * chip[0] + chip[1]], which, lead)

        def copy(k, j, src, dst, to):
            return pltpu.make_async_remote_copy(
                src_ref=src, dst_ref=dst, send_sem=send_sems.at[6 * k + j], recv_sem=recv_sems.at[6 * k + j],
                device_id=to, device_id_type=MESH)

        def local(k):
            return pltpu.make_async_copy(ins[k], outs[k].at[2 * x + y], local_sems.at[k])

        def first(k, j):
            lead = len(ins[k].shape) - 2
            return copy(k, j, _row_half(ins[k], c, lead), slot(k, (x, y), c), (*chips[j], c))

        def passed(k, j, which):
            return copy(k, 3 + j, slot(k, chips[j], which), slot(k, chips[j], which), (x, y, 1 - c))

        def landed(k, j):
            return copy(k, j, slot(k, chips[j], c), slot(k, chips[j], c), (x, y, 1 - c))

        return c, local, first, passed, landed

    def start(ins, outs, sems):
        c, local, first, passed, landed = ctx(ins, outs, sems)
        for k in range(n):
            local(k).start()
            for j in range(3):
                first(k, j).start()

    def mid(ins, outs, sems):
        c, local, first, passed, landed = ctx(ins, outs, sems)
        for j in range(3):
            for k in range(n):
                landed(k, j).wait_recv()
                passed(k, j, c).start()

    def finish(ins, outs, sems):
        c, local, first, passed, landed = ctx(ins, outs, sems)
        for j in range(3):
            for k in range(n):
                passed(k, j, 1 - c).wait_recv()
        for k in range(n):
            for j in range(3):
                first(k, j).wait_send()
                passed(k, j, c).wait_send()
            local(k).wait()

    return _Comm(
        tag, arrs, [jax.ShapeDtypeStruct((N_CHIPS,) + a.shape, a.dtype) for a in arrs],
        [pltpu.SemaphoreType.DMA((6 * n,)), pltpu.SemaphoreType.DMA((6 * n,)), pltpu.SemaphoreType.DMA((n,))],
        start, mid, finish)


def _run_comm(comm, name):
    n_in, n_out = len(comm.ins), len(comm.out_shapes)

    def body(*refs):
        ins, outs, sems = refs[:n_in], refs[n_in:n_in + n_out], refs[n_in + n_out:]
        comm.start(ins, outs, sems)
        if comm.mid is not None:
            comm.mid(ins, outs, sems)
        comm.finish(ins, outs, sems)

    return pl.pallas_call(
        body, out_shape=comm.out_shapes, in_specs=[HBM_SPEC] * n_in, out_specs=[HBM_SPEC] * n_out,
        scratch_shapes=comm.sems, name=name)(*comm.ins)


def _all_gather_chips(arrs, name):
    return _run_comm(_gather_comm(arrs, "gather"), name)


GATHER_US_PER_BYTE = 380.0 / 11.65e6
HOST_US = dict(ffn_up=68.0, ffn_down=37.0, in_proj=38.0, attn_fwd=103.0, ssm_fwd=70.0, merge_fwd=30.0,
               out_proj=23.0)
HOST_SLACK_US = 10.0


class _WeightStream:
    def __init__(self, pieces):
        self.keys = [k for k, _ in pieces]
        self.shards = dict(pieces)
        self.next = 0
        self.full = {}
        self.pending = []

    def comm_for(self, host):
        budget = HOST_US[host] + HOST_SLACK_US
        taken, cost = [], 0.0
        while self.next < len(self.keys):
            key = self.keys[self.next]
            c = self.shards[key].size * self.shards[key].dtype.itemsize * GATHER_US_PER_BYTE
            if cost + c > budget and taken:
                break
            taken.append(key)
            cost += c
            self.next += 1
        self.pending = taken
        if not taken:
            return None
        return _gather_comm([self.shards[k] for k in taken], "g_" + "_".join(k[1] for k in taken))

    def deposit(self, gathered):
        for key, arr in zip(self.pending, gathered):
            self.full[key] = arr
        self.pending = []

    def get(self, key):
        if key not in self.full:
            upto = self.keys.index(key) + 1
            keys = self.keys[self.next:upto]
            self.next = upto
            for k, arr in zip(keys, _all_gather_chips([self.shards[k] for k in keys], "gather_now")):
                self.full[k] = arr
        return self.full[key]


def _all_gather_devices(x_shard, name):
    m_per, ncol = x_shard.shape

    def body(x_ref, out_ref, send_sems, recv_sems, local_sem):
        x, y, c = _mesh_pos()
        me, sibling = (x, y, c), (x, y, 1 - c)
        chips = [(1 - x, y), (x, 1 - y), (1 - x, 1 - y)]

        def rows(px, py, pc):
            return out_ref.at[4 * px + 2 * py + pc]

        def copy(k, block, to, src=None):
            return pltpu.make_async_remote_copy(
                src_ref=rows(*block) if src is None else src, dst_ref=rows(*block),
                send_sem=send_sems.at[k], recv_sem=recv_sems.at[k], device_id=to, device_id_type=MESH)

        mine = pltpu.make_async_copy(x_ref, rows(*me), local_sem)
        mine.start()
        first = [copy(0, me, sibling, src=x_ref)]
        first += [copy(1 + j, me, (*chip, c), src=x_ref) for j, chip in enumerate(chips)]
        for cp in first:
            cp.start()
        passed = [copy(4 + j, (*chip, c), sibling) for j, chip in enumerate(chips)]
        for j, chip in enumerate(chips):
            copy(1 + j, (*chip, c), me).wait_recv()
            passed[j].start()
        copy(0, sibling, me).wait_recv()
        for j, chip in enumerate(chips):
            copy(4 + j, (*chip, 1 - c), me).wait_recv()
        for cp in first + passed:
            cp.wait_send()
        mine.wait()

    return pl.pallas_call(
        body, out_shape=jax.ShapeDtypeStruct((8, m_per, ncol), x_shard.dtype),
        in_specs=[pl.BlockSpec(memory_space=pltpu.VMEM)], out_specs=pl.BlockSpec(memory_space=pltpu.VMEM),
        scratch_shapes=[pltpu.SemaphoreType.DMA((7,)), pltpu.SemaphoreType.DMA((7,)), pltpu.SemaphoreType.DMA],
        compiler_params=pltpu.CompilerParams(vmem_limit_bytes=VMEM_LIMIT), name=name)(x_shard)


def _sum_devices(g8, name):
    _, r, c = g8.shape
    tr = _div_tile(r, c * 4 * 8)

    def body(g_ref, o_ref):
        acc = g_ref[0]
        for dev in range(1, 8):
            acc = acc + g_ref[dev]
        o_ref[...] = acc

    return pl.pallas_call(
        body, out_shape=jax.ShapeDtypeStruct((r, c), F32), grid=(r // tr,),
        in_specs=[pl.BlockSpec((8, tr, c), lambda i: (0, i, 0))], out_specs=pl.BlockSpec((tr, c), lambda i: (i, 0)),
        compiler_params=_cparams(("parallel",)), name=name)(g8)


def _chip_partials(arrs, recvs, pos, name):
    n = len(arrs)

    def body(pos_ref, *refs):
        for a_ref, b_ref, o_ref in zip(refs[:n], refs[n:2 * n], refs[2 * n:]):
            o_ref[...] = (a_ref[...] + b_ref[...]).astype(BF16)

    own_specs, recv_specs, shapes = [], [], []
    for arr in arrs:
        nslab, r, c = arr.shape
        own_specs.append(pl.BlockSpec((None, r // 2, c), lambda j, p: (j, p[0], 0)))
        recv_specs.append(pl.BlockSpec((None, r // 2, c), lambda j, p: (j, 0, 0)))
        shapes.append(jax.ShapeDtypeStruct((nslab, r // 2, c), BF16))
    grid_spec = pltpu.PrefetchScalarGridSpec(
        num_scalar_prefetch=1, grid=(N_CHIPS,), in_specs=own_specs + recv_specs, out_specs=recv_specs)
    return pl.pallas_call(
        body, out_shape=shapes, grid_spec=grid_spec,
        compiler_params=_cparams(("parallel",)), name=name)(pos, *arrs, *recvs)


def _chip_exchange_comm(parts, tag):
    n = len(parts)

    def copies(ins, outs, sems):
        send_sems, recv_sems = sems
        x, y, c = _mesh_pos()
        chips = [(1 - x, y), (x, 1 - y), (1 - x, 1 - y)]
        return [pltpu.make_async_remote_copy(
            src_ref=ins[k].at[2 * chip[0] + chip[1]], dst_ref=outs[k].at[j],
            send_sem=send_sems.at[3 * k + j], recv_sem=recv_sems.at[3 * k + j],
            device_id=(*chip, c), device_id_type=MESH) for k in range(n) for j, chip in enumerate(chips)]

    def start(ins, outs, sems):
        for cp in copies(ins, outs, sems):
            cp.start()

    def finish(ins, outs, sems):
        for cp in copies(ins, outs, sems):
            cp.wait()

    return _Comm(
        tag, parts, [jax.ShapeDtypeStruct((3,) + p.shape[1:], p.dtype) for p in parts],
        [pltpu.SemaphoreType.DMA((3 * n,)), pltpu.SemaphoreType.DMA((3 * n,))], start, None, finish)


def _reduce_halves(arrs, recvs, gots, pos, name):
    n = len(arrs)

    def body(pos_ref, *refs):
        for a_ref, b_ref, g_ref, o_ref in zip(refs[:n], refs[n:2 * n], refs[2 * n:3 * n], refs[3 * n:]):
            acc = a_ref[...] + b_ref[...]
            for j in range(3):
                acc = acc + g_ref[j].astype(F32)
            o_ref[...] = acc

    own_specs, recv_specs, got_specs, out_specs, shapes = [], [], [], [], []
    for arr in arrs:
        _, r, c = arr.shape
        own_specs.append(pl.BlockSpec((None, r // 2, c), lambda i, p: (p[1], p[0], 0)))
        recv_specs.append(pl.BlockSpec((None, r // 2, c), lambda i, p: (p[1], 0, 0)))
        got_specs.append(pl.BlockSpec((3, r // 2, c), lambda i, p: (0, 0, 0)))
        out_specs.append(pl.BlockSpec((r // 2, c), lambda i, p: (0, 0)))
        shapes.append(jax.ShapeDtypeStruct((r // 2, c), F32))
    grid_spec = pltpu.PrefetchScalarGridSpec(
        num_scalar_prefetch=1, grid=(1,), in_specs=own_specs + recv_specs + got_specs, out_specs=out_specs)
    return pl.pallas_call(
        body, out_shape=shapes, grid_spec=grid_spec,
        compiler_params=_cparams(("arbitrary",)), name=name)(pos, *arrs, *recvs, *gots)


def _share_halves(halves, name):
    n = len(halves)

    def body(*refs):
        ins, outs = refs[:n], refs[n:2 * n]
        send_sems, recv_sems = refs[2 * n:]
        x, y, c = _mesh_pos()
        cps = []
        for k in range(n):
            cp = pltpu.make_async_remote_copy(
                src_ref=ins[k], dst_ref=outs[k], send_sem=send_sems.at[k], recv_sem=recv_sems.at[k],
                device_id=(x, y, 1 - c), device_id_type=MESH)
            cp.start()
            cps.append(cp)
        for cp in cps:
            cp.wait()

    return pl.pallas_call(
        body, out_shape=[jax.ShapeDtypeStruct(h.shape, h.dtype) for h in halves],
        in_specs=[HBM_SPEC] * n, out_specs=[HBM_SPEC] * n,
        scratch_shapes=[pltpu.SemaphoreType.DMA((n,)), pltpu.SemaphoreType.DMA((n,))], name=name)(*halves)


class _Reduction:
    def __init__(self, arrs, others, pos, tag):
        self.arrs, self.pos, self.tag = arrs, pos, tag
        self.recv = _share_halves(others, "rs_sibling_" + tag)
        self.parts = _chip_partials(arrs, self.recv, pos, "rs_partial_" + tag)
        self.got = None

    def comm(self):
        return _chip_exchange_comm(self.parts, "rs_" + self.tag)

    def end(self):
        if self.got is None:
            self.got = _run_comm(self.comm(), "rs_chips_" + self.tag)
        halves = _reduce_halves(self.arrs, self.recv, self.got, self.pos, "rs_reduce_" + self.tag)
        return halves, _share_halves(halves, "rs_share_" + self.tag)


def _w_in_full(p, l, ws):
    slabs = ws.get((l, "w_in"))
    return slabs.reshape(-1, slabs.shape[2])


def _w3t_full(p, l, ws):
    if "w3t" not in p:
        slabs = ws.get((l, "w3"))
        p["w3t"] = jnp.swapaxes(slabs, 0, 1).reshape(slabs.shape[1], -1, slabs.shape[3])
    return p["w3t"]


def _w_out_full(l, ws):
    slabs = ws.get((l, "w_out"))
    return slabs.reshape(-1, slabs.shape[2])


def _layer_fwd(h, n0, l, p, next_gain, ws, tabs):
    def hosted(host, fn, *args):
        out, got = fn(*args, ws.comm_for(host))
        ws.deposit(got)
        return out

    ffn1_saved = hosted("ffn_up", _ffn_up, n0, ws.get((l, "wg1")), ws.get((l, "wu1")))
    h1, n = hosted("ffn_down", _ffn_down, ffn1_saved[2], ws.get((l, "wd1")), h, p["mix_norm"])
    ssm_w = p["ssm_d"].shape[0]
    q, k, v, u, ga, gs = hosted("in_proj", _in_proj, n, _w_in_full(p, l, ws), tabs, ssm_w)
    o = hosted("attn_fwd", _attn_fwd, q, k, v, p["attn_sinks"])
    y, yg = hosted("ssm_fwd", _ssm_fwd, u, *p["ssm_tabs"], p["ssm_d"])
    merged = hosted("merge_fwd", _merge_fwd, o, yg, ga, gs, _w3t_full(p, l, ws))
    h2, n2 = hosted("out_proj", _out_proj, merged, _w_out_full(l, ws), h1, p["ffn2_norm"])
    ffn2_saved = hosted("ffn_up", _ffn_up, n2, ws.get((l, "wg2")), ws.get((l, "wu2")))
    h3, *n3 = hosted("ffn_down", _ffn_down, ffn2_saved[2], ws.get((l, "wd2")), h2, next_gain)
    saved = dict(h0=h, h1=h1, h2=h2, ffn1=ffn1_saved, ffn2=ffn2_saved, n_mix=n, q=q, k=k, v=v, u=u, ga=ga, gs=gs,
                 o=o, y=y, yg=yg, merged=merged)
    return h3, (n3[0] if n3 else None), saved


def _layer_bwd(dh_pair, l, p, ws, s, tabs, pos):
    g = {}
    (dh2, dhb), g["ffn2_norm"], red_ffn2, _, _ = _ffn_bwd(
        dh_pair, s["h2"], p["ffn2_norm"], ws.get((l, "wg2")), ws.get((l, "wu2")), ws.get((l, "wd2")), p["f4"],
        s["ffn2"], pos)
    w3, w_out_w = _w3t_full(p, l, ws), _w_out_full(l, ws)
    lp, d = dh2.shape
    d4 = d // N_CHIPS
    dw_out, dw_out_other = _tn_full(s["merged"], dhb, "mix_dw_out")
    dattn, dv, dg, dga, dgs = _merge_bwd(dhb, w_out_w, s["o"], s["yg"], s["ga"], s["gs"], w3)
    (dw_ap,), (dw_ap_other,) = _tn_cols(s["o"], [dattn], "mix_dw_ap")
    (dw_gv, dw_gg), (dw_gv_other, dw_gg_other) = _tn_cols(s["yg"], [dv, dg], "mix_dw_glu")
    do, dy = _branch_bwd(dattn, dv, dg, w3, s["y"])
    (dq, dk, dvv, dkm, dvm, dsink), _ = _attn_bwd(s["q"], s["k"], s["v"], do, p["attn_sinks"], tabs)
    g["attn_sinks"] = dsink[:, 0]
    (du, dlr, dli, dbr, dbi, dcr, dci, dd), _ = _ssm_bwd(s["u"], dy, *p["ssm_tabs"], p["ssm_d"])
    ngrp = p["ssm_d"].shape[0] // SSM_GROUP
    g["ssm_lam"] = (dlr.reshape(ngrp, SSM_STATE), dli.reshape(ngrp, SSM_STATE),
                    _ssm_untable_b(dbr, ngrp), _ssm_untable_b(dbi, ngrp))
    g["ssm_c_re"] = _ssm_untable_c(dcr, ngrp)
    g["ssm_c_im"] = _ssm_untable_c(dci, ngrp)
    g["ssm_d"] = dd[0]
    dk = dk.at[:BLOCK].add(dkm)
    dvv = dvv.at[:BLOCK].add(dvm)
    dz = jnp.concatenate([dq.astype(BF16), dk.astype(BF16), dvv.astype(BF16), du.astype(BF16), dga, dgs], axis=1)
    n = s["n_mix"]
    w_in = _w_in_full(p, l, ws)
    dw_in, dw_in_other = _tn_full(dz, n, "mix_dw_in", d // 2)
    red_in = _Reduction([dw_in], [dw_in_other], pos, "mix_in")
    red_mix = _Reduction([dw_ap, dw_gv, dw_gg, dw_out], [dw_ap_other, dw_gv_other, dw_gg_other, dw_out_other],
                         pos, "mix")
    dh1, g["mix_norm"], dh1b = _in_proj_bwd(dz, w_in, dh2, s["h1"], p["mix_norm"])
    dh0_pair, g["ffn1_norm"], red_ffn1, red_in.got, red_mix.got = _ffn_bwd(
        (dh1, dh1b), s["h0"], p["ffn1_norm"], ws.get((l, "wg1")), ws.get((l, "wu1")), ws.get((l, "wd1")), p["f4"],
        s["ffn1"], pos, red_in.comm(), red_mix.comm())
    return dh0_pair, g, [*red_ffn1, red_in, red_mix, *red_ffn2]


BIG = ["ffn1_w_gate", "ffn1_w_up", "ffn1_w_down", "w_in", "w_attn_proj", "w_glu_v", "w_glu_g", "w_out",
       "ffn2_w_gate", "ffn2_w_up", "ffn2_w_down"]
TRANSPOSED = ["ffn1_w_gate", "ffn1_w_up", "w_in", "ffn2_w_gate", "ffn2_w_up"]
SMALL = ["ffn1_norm", "mix_norm", "attn_sinks", "ssm_a_re", "ssm_a_im", "ssm_log_dt", "ssm_b_re", "ssm_b_im",
         "ssm_c_re", "ssm_c_im", "ssm_d", "ffn2_norm", "final_norm"]
WEIGHTS = ["meta_tokens", "ffn1_norm", "ffn1_w_gate", "ffn1_w_up", "ffn1_w_down", "mix_norm", "w_in", "attn_sinks",
           "ssm_a_re", "ssm_a_im", "ssm_log_dt", "ssm_b_re", "ssm_b_im", "ssm_c_re", "ssm_c_im", "ssm_d",
           "w_attn_proj", "w_glu_v", "w_glu_g", "w_out", "ffn2_norm", "ffn2_w_gate", "ffn2_w_up", "ffn2_w_down",
           "final_norm"]


def _small_rows(shape):
    rows = -(-math.prod(shape) // LANES)
    return -(-rows // SUBLANES) * SUBLANES


def _pack_small(tree):
    parts = []
    for k in SMALL + ["meta_tokens"]:
        size, rows = math.prod(tree[k].shape), _small_rows(tree[k].shape)
        if size % LANES == 0:
            part = tree[k].reshape(size // LANES, LANES)
        else:
            part = jnp.pad(tree[k].reshape(1, size), ((0, 0), (0, LANES - size)))
        parts.append(jnp.pad(part, ((0, rows - part.shape[0]), (0, 0))))
    return jnp.concatenate(parts, axis=0)


def _unpack_small(packed, like):
    out, off = {}, 0
    for k in SMALL + ["meta_tokens"]:
        size, rows = math.prod(like[k].shape), _small_rows(like[k].shape)
        if size % LANES == 0:
            out[k] = packed[off:off + size // LANES].reshape(like[k].shape)
        else:
            out[k] = packed[off, :size].reshape(like[k].shape)
        off += rows
    return out


def kernel(x, meta_tokens, ffn1_norm, ffn1_w_gate, ffn1_w_up, ffn1_w_down, mix_norm, w_in, attn_sinks, ssm_a_re, ssm_a_im, ssm_log_dt, ssm_b_re, ssm_b_im, ssm_c_re, ssm_c_im, ssm_d, w_attn_proj, w_glu_v, w_glu_g, w_out, ffn2_norm, ffn2_w_gate, ffn2_w_up, ffn2_w_down, final_norm, loss_target, m_meta_tokens, m_ffn1_norm, m_ffn1_w_gate, m_ffn1_w_up, m_ffn1_w_down, m_mix_norm, m_w_in, m_attn_sinks, m_ssm_a_re, m_ssm_a_im, m_ssm_log_dt, m_ssm_b_re, m_ssm_b_im, m_ssm_c_re, m_ssm_c_im, m_ssm_d, m_w_attn_proj, m_w_glu_v, m_w_glu_g, m_w_out, m_ffn2_norm, m_ffn2_w_gate, m_ffn2_w_up, m_ffn2_w_down, m_final_norm, v_meta_tokens, v_ffn1_norm, v_ffn1_w_gate, v_ffn1_w_up, v_ffn1_w_down, v_mix_norm, v_w_in, v_attn_sinks, v_ssm_a_re, v_ssm_a_im, v_ssm_log_dt, v_ssm_b_re, v_ssm_b_im, v_ssm_c_re, v_ssm_c_im, v_ssm_d, v_w_attn_proj, v_w_glu_v, v_w_glu_g, v_w_out, v_ffn2_norm, v_ffn2_w_gate, v_ffn2_w_up, v_ffn2_w_down, v_final_norm):
    args = dict(locals())
    w = {k: args[k] for k in WEIGHTS}
    m = {k: args["m_" + k] for k in WEIGHTS}
    v = {k: args["v_" + k] for k in WEIGHTS}
    depth = ffn1_norm.shape[0]
    seq, d = x.shape[1], x.shape[2]
    lp = seq + BLOCK
    xi, yi, ci = _mesh_pos()
    pos = jnp.stack([ci, 2 * xi + yi]).astype(jnp.int32)

    tabs = _rope_tables(lp)
    (meta_all,) = _all_gather_chips([meta_tokens], "gather_meta")
    meta_full = jnp.concatenate([meta_all[j] for j in range(N_CHIPS)], axis=1)
    layers, pieces = [], []
    f4 = ffn1_w_gate.shape[2]
    fp = -(-f4 // MXU_DIM) * MXU_DIM

    def ffn_rows(wt):
        return jnp.pad(wt, ((0, fp - f4), (0, 0))).astype(BF16)

    for l in range(depth):
        pieces += [
            ((l, "wg1"), ffn_rows(ffn1_w_gate[l].T)), ((l, "wu1"), ffn_rows(ffn1_w_up[l].T)),
            ((l, "wd1"), ffn_rows(ffn1_w_down[l])), ((l, "w_in"), w_in[l].T.astype(BF16)),
            ((l, "w3"), jnp.stack([w_attn_proj[l].T, w_glu_v[l].T, w_glu_g[l].T]).astype(BF16)),
            ((l, "w_out"), w_out[l].astype(BF16)),
            ((l, "wg2"), ffn_rows(ffn2_w_gate[l].T)), ((l, "wu2"), ffn_rows(ffn2_w_up[l].T)),
            ((l, "wd2"), ffn_rows(ffn2_w_down[l]))]
        lb_re, lb_im, bb_re, bb_im = _ssm_params(ssm_a_re[l], ssm_a_im[l], ssm_log_dt[l], ssm_b_re[l], ssm_b_im[l])
        ngrp = lb_re.shape[0]
        nt = ngrp // GROUPS_PER_TILE
        ssm_tabs = (lb_re.reshape(nt, 1, TILE_STATES), lb_im.reshape(nt, 1, TILE_STATES),
                    *_ssm_tables(bb_re, bb_im, ssm_c_re[l], ssm_c_im[l]))
        layers.append(dict(
            ffn1_norm=ffn1_norm[l], mix_norm=mix_norm[l], ffn2_norm=ffn2_norm[l], attn_sinks=attn_sinks[l],
            ssm_d=ssm_d[l], ssm_tabs=ssm_tabs, f4=f4))
    ws = _WeightStream(pieces)
    ws.get((0, "wu1"))

    h = jnp.concatenate([jnp.zeros((PAD_FRONT, d), F32), meta_full, x[0]], axis=0)
    saved = []
    n0 = _rms_fwd(h, ffn1_norm[0], "rms_fwd_first")
    for l in range(depth):
        next_gain = ffn1_norm[l + 1] if l + 1 < depth else None
        h, n0, s = _layer_fwd(h, n0, l, layers[l], next_gain, ws, tabs)
        saved.append(s)
    dh, g_final, loss_acc, dhb = _loss_head(h, final_norm, loss_target[0])
    dh_pair = (dh, dhb)
    loss = lax.psum(loss_acc[0, 0], ("x", "y", "c"))

    grads, reds = [None] * depth, [None] * depth
    for l in reversed(range(depth)):
        dh_pair, grads[l], reds[l] = _layer_bwd(dh_pair, l, layers[l], ws, saved[l], tabs, pos)
    dh = dh_pair[0]
    grad_x = dh[BLOCK:][None]
    dmeta_local = dh[PAD_FRONT:BLOCK]

    small = {k: [] for k in SMALL}
    for l in range(depth):
        gl = grads[l]
        _, vjp = jax.vjp(_ssm_params, ssm_a_re[l], ssm_a_im[l], ssm_log_dt[l], ssm_b_re[l], ssm_b_im[l])
        da_re, da_im, dlog_dt, db_re, db_im = vjp(gl["ssm_lam"])
        for k, val in (("ffn1_norm", gl["ffn1_norm"][0]), ("mix_norm", gl["mix_norm"][0]),
                       ("attn_sinks", gl["attn_sinks"]), ("ssm_a_re", da_re), ("ssm_a_im", da_im),
                       ("ssm_log_dt", dlog_dt), ("ssm_b_re", db_re), ("ssm_b_im", db_im),
                       ("ssm_c_re", gl["ssm_c_re"]), ("ssm_c_im", gl["ssm_c_im"]), ("ssm_d", gl["ssm_d"]),
                       ("ffn2_norm", gl["ffn2_norm"][0])):
            small[k].append(val)
    small_local = {k: jnp.stack(vals) for k, vals in small.items() if k != "final_norm"}
    small_local["final_norm"] = g_final[0]
    small_local["meta_tokens"] = dmeta_local
    like = dict(small_local)
    g_small = _sum_devices(_all_gather_devices(_pack_small(small_local), "gather_small_grads"), "sum_small_grads")
    g_small_tree = _unpack_small(g_small, like)
    d4 = d // N_CHIPS
    chip = 2 * xi + yi
    g_meta = lax.dynamic_slice_in_dim(g_small_tree["meta_tokens"], chip * d4, d4, axis=1)

    reduced = []
    for l in range(depth):
        mine, other = [], []
        for red in reds[l]:
            halves, sibling_halves = red.end()
            mine += halves
            other += sibling_halves
        reduced.append((mine, other))

    g_out, delta, new_m, new_v = {}, {}, {}, {}
    for i, k in enumerate(BIG):
        flip = (lambda t: jnp.swapaxes(t, 1, 2)) if k in TRANSPOSED else (lambda t: t)
        outs = _adamw_layers(
            flip(w[k]), flip(m[k]), flip(v[k]), [reduced[l][0][i] for l in range(depth)],
            [reduced[l][1][i] for l in range(depth)], pos, "adamw_" + k)
        g_out[k], delta[k], new_m[k], new_v[k] = [flip(t) for t in outs]
    g_small_tree["meta_tokens"] = g_meta
    for k in SMALL + ["meta_tokens"]:
        shape = w[k].shape if w[k].ndim > 1 else (1,) + w[k].shape
        outs = _adamw_whole(w[k].reshape(shape), g_small_tree[k].reshape(shape), m[k].reshape(shape),
                            v[k].reshape(shape), "adamw_" + k)
        g_out[k] = g_small_tree[k]
        delta[k], new_m[k], new_v[k] = [t.reshape(w[k].shape) for t in outs]

    return (loss, grad_x, *[g_out[k] for k in WEIGHTS], *[delta[k] for k in WEIGHTS],
            *[new_m[k] for k in WEIGHTS], *[new_v[k] for k in WEIGHTS])
```

```python
import functools
import math

import jax
import jax.numpy as jnp
from jax import lax
from jax.experimental import pallas as pl
from jax.experimental.pallas import tpu as pltpu

F32 = jnp.float32
BF16 = jnp.bfloat16

N_META = 16
HEAD_DIM = 64
N_Q_HEADS = 8
N_KV_HEADS = 2
Q_PER_KV = N_Q_HEADS // N_KV_HEADS
ATTN_WIDTH = N_Q_HEADS * HEAD_DIM
KV_WIDTH = N_KV_HEADS * HEAD_DIM
BLOCK = 128
PAD_FRONT = BLOCK - N_META
ROPE_THETA = 500000.0
ROT_DIM = HEAD_DIM // 4
SSM_GROUP = 16
SSM_STATE = 64
GROUPS_PER_TILE = 4
TILE_STATES = GROUPS_PER_TILE * SSM_STATE
LANES = 128
SUBLANES = 8
MXU_DIM = 256
EPS = 1e-6
NEG_INF = -1e30
N_CHIPS = 4

ADAM_LR = 0.001
ADAM_B1 = 0.9
ADAM_B2 = 0.999
ADAM_EPS = 1e-08
ADAM_WD = 0.01
ADAM_STEP = 10

VMEM_LIMIT = 56 * 1024 * 1024
MESH = pl.DeviceIdType.MESH


def _cparams(sem=None):
    return pltpu.CompilerParams(dimension_semantics=sem, vmem_limit_bytes=VMEM_LIMIT)


def _row_tile(rows, limit=512):
    best = None
    for t in range(128, limit + 1, 128):
        if rows % t == 0:
            best = t
    assert best is not None, rows
    return best


def _div_tile(rows, row_bytes, max_bytes=1 << 20, mult=8):
    best = None
    for t in range(mult, rows + 1, mult):
        if rows % t == 0 and t * row_bytes <= max_bytes:
            best = t
    if best is None:
        best = rows
    return best


def _dot(a, b, mode):
    if mode == "nn":
        dims = (((1,), (0,)), ((), ()))
    elif mode == "nt":
        dims = (((1,), (1,)), ((), ()))
    else:
        dims = (((0,), (0,)), ((), ()))
    return lax.dot_general(a.astype(BF16), b.astype(BF16), dims, preferred_element_type=F32)


def _sigmoid(x):
    return 1.0 / (1.0 + jnp.exp(-x))


_GELU_C = math.sqrt(2.0 / math.pi)


def _gelu(x):
    return 0.5 * x * (1.0 + jnp.tanh(_GELU_C * (x + 0.044715 * x * x * x)))


def _gelu_grad(x):
    t = jnp.tanh(_GELU_C * (x + 0.044715 * x * x * x))
    return 0.5 * (1.0 + t) + 0.5 * x * (1.0 - t * t) * _GELU_C * (1.0 + 3.0 * 0.044715 * x * x)


class _Comm:
    def __init__(self, tag, ins, out_shapes, sems, start, mid, finish):
        self.tag, self.ins, self.out_shapes, self.sems = tag, list(ins), list(out_shapes), list(sems)
        self.start, self.mid, self.finish = start, mid, finish


HBM_SPEC = pl.BlockSpec(memory_space=pltpu.HBM)
MID_NUM, MID_DEN = 4, 5


def _hosted_call(body, comm, *, out_shape, grid, in_specs, out_specs, scratch_shapes, sem, name, args):
    out_shape, in_specs, out_specs = list(out_shape), list(in_specs), list(out_specs)
    scratch_shapes = list(scratch_shapes)
    if comm is None:
        res = pl.pallas_call(
            body, out_shape=out_shape, grid=grid, in_specs=in_specs, out_specs=out_specs,
            scratch_shapes=scratch_shapes, compiler_params=_cparams(sem), name=name)(*args)
        return list(res), []
    n_in, n_out, n_sc = len(args), len(out_shape), len(scratch_shapes)
    nci, nco = len(comm.ins), len(comm.out_shapes)
    total = math.prod(grid)

    def wrapped(*refs):
        in_refs, cin = refs[:n_in], refs[n_in:n_in + nci]
        o0 = n_in + nci
        out_refs, cout = refs[o0:o0 + n_out], refs[o0 + n_out:o0 + n_out + nco]
        s0 = o0 + n_out + nco
        sc, csem = refs[s0:s0 + n_sc], refs[s0 + n_sc:]
        lin = 0
        for dim, size in enumerate(grid):
            lin = lin * size + pl.program_id(dim)

        @pl.when(lin == 0)
        def _():
            comm.start(cin, cout, csem)

        if comm.mid is not None:
            @pl.when(lin == (total * MID_NUM) // MID_DEN)
            def _():
                comm.mid(cin, cout, csem)

        body(*in_refs, *out_refs, *sc)

        @pl.when(lin == total - 1)
        def _():
            comm.finish(cin, cout, csem)

    res = pl.pallas_call(
        wrapped, out_shape=out_shape + comm.out_shapes, grid=grid,
        in_specs=in_specs + [HBM_SPEC] * nci, out_specs=out_specs + [HBM_SPEC] * nco,
        scratch_shapes=scratch_shapes + comm.sems,
        compiler_params=_cparams(("arbitrary",) * len(grid)), name=name + "_" + comm.tag)(*args, *comm.ins)
    return list(res[:n_out]), list(res[n_out:])


def _matmul(name, grid, k_axis, ins, in_specs, pairs, acc_shapes, epilogue, out_shapes, out_specs, sem, comm=None):
    n_in, n_out, n_acc = len(ins), len(out_shapes), len(acc_shapes)

    def body(*refs):
        in_refs = refs[:n_in]
        out_refs = refs[n_in:n_in + n_out]
        acc_refs = refs[n_in + n_out:]
        if k_axis is None:
            accs = [None] * n_acc
            for ia, ib, mode, iacc in pairs:
                d = _dot(in_refs[ia][...], in_refs[ib][...], mode)
                accs[iacc] = d if accs[iacc] is None else accs[iacc] + d
            epilogue(accs, in_refs, out_refs)
            return
        k = pl.program_id(k_axis)

        @pl.when(k == 0)
        def _():
            for r in acc_refs:
                r[...] = jnp.zeros_like(r)

        for ia, ib, mode, iacc in pairs:
            acc_refs[iacc][...] += _dot(in_refs[ia][...], in_refs[ib][...], mode)

        @pl.when(k == pl.num_programs(k_axis) - 1)
        def _():
            epilogue([r[...] for r in acc_refs], in_refs, out_refs)

    scratch = [] if k_axis is None else [pltpu.VMEM(s, F32) for s in acc_shapes]
    outs, couts = _hosted_call(
        body, comm, out_shape=out_shapes, grid=grid, in_specs=in_specs, out_specs=out_specs,
        scratch_shapes=scratch, sem=sem, name=name, args=ins)
    return outs if comm is None else (outs, couts)


def _rms_math(x, g):
    r = lax.rsqrt(jnp.mean(x * x, axis=-1, keepdims=True) + EPS)
    return (x * r * g).astype(BF16)


def _rms_fwd(h, g, name):
    lp, d = h.shape
    tm = _row_tile(lp)

    def body(h_ref, g_ref, n_ref):
        n_ref[...] = _rms_math(h_ref[...], g_ref[...])

    return pl.pallas_call(
        body, out_shape=jax.ShapeDtypeStruct((lp, d), BF16), grid=(lp // tm,),
        in_specs=[pl.BlockSpec((tm, d), lambda i: (i, 0)), pl.BlockSpec((1, d), lambda i: (0, 0))],
        out_specs=pl.BlockSpec((tm, d), lambda i: (i, 0)),
        compiler_params=_cparams(("parallel",)), name=name)(h, g.reshape(1, d))


def _rms_bwd_math(dn, x, g):
    r = lax.rsqrt(jnp.mean(x * x, axis=-1, keepdims=True) + EPS)
    xh = x * r
    dxh = dn * g
    dx = r * (dxh - xh * jnp.mean(dxh * xh, axis=-1, keepdims=True))
    return dx, dn * xh


def _ffn_up(n, wgt, wut, comm=None):
    lp, d = n.shape
    fp = wgt.shape[1]
    tm = _row_tile(lp)

    def up_body(n_ref, wg_ref, wu_ref, a_ref, b_ref, s_ref):
        x = n_ref[...]
        for jc in range(N_CHIPS):
            cols = slice(jc * fp, (jc + 1) * fp)
            a = _dot(x, wg_ref[jc], "nt")
            b = _dot(x, wu_ref[jc], "nt")
            a_ref[:, cols] = a.astype(BF16)
            b_ref[:, cols] = b.astype(BF16)
            s_ref[:, cols] = (a * _sigmoid(a) * b).astype(BF16)

    ff = N_CHIPS * fp
    act = jax.ShapeDtypeStruct((lp, ff), BF16)
    act_tile = pl.BlockSpec((tm, ff), lambda i: (i, 0))
    w_spec = pl.BlockSpec((N_CHIPS, fp, d), lambda i: (0, 0, 0))
    outs, couts = _hosted_call(
        up_body, comm, out_shape=[act, act, act], grid=(lp // tm,),
        in_specs=[pl.BlockSpec((tm, d), lambda i: (i, 0)), w_spec, w_spec],
        out_specs=[act_tile] * 3, scratch_shapes=[], sem=("parallel",), name="ffn_up", args=(n, wgt, wut))
    return (*outs, n), couts


def _residual_outputs(h_new, in_refs, out_refs, gain_at):
    out_refs[0][...] = h_new
    if gain_at is not None:
        out_refs[1][...] = _rms_math(h_new, in_refs[gain_at][...])


def _residual_specs(lp, d, tm, next_gain):
    row = pl.BlockSpec((tm, d), lambda i: (i, 0))
    shapes, specs = [jax.ShapeDtypeStruct((lp, d), F32)], [row]
    extra_in, extra_specs = [], []
    if next_gain is not None:
        shapes.append(jax.ShapeDtypeStruct((lp, d), BF16))
        specs.append(row)
        extra_in, extra_specs = [next_gain.reshape(1, d)], [pl.BlockSpec((1, d), lambda i: (0, 0))]
    return shapes, specs, extra_in, extra_specs


def _ffn_down(s, wd, h, next_gain, comm=None):
    lp, d = h.shape
    ff = s.shape[1]
    tm = _row_tile(lp)
    shapes, specs, extra_in, extra_specs = _residual_specs(lp, d, tm, next_gain)

    def down_epi(accs, in_refs, out_refs):
        _residual_outputs(in_refs[2][...] + 0.5 * accs[0], in_refs, out_refs, 3 if extra_in else None)

    res = _matmul(
        "ffn_down", (lp // tm,), None, [s, wd.reshape(ff, d), h] + extra_in,
        [pl.BlockSpec((tm, ff), lambda i: (i, 0)), pl.BlockSpec((ff, d), lambda i: (0, 0)),
         pl.BlockSpec((tm, d), lambda i: (i, 0))] + extra_specs,
        [(0, 1, "nn", 0)], [(tm, d)], down_epi, shapes, specs, ("parallel",), comm)
    return (res, []) if comm is None else res


def _tn_tiles(lp):
    return _row_tile(lp, 1408)


def _ffn_bwd(dh_pair, h_in, gain, wgt, wut, wd, f4, saved, pos, comm=None):
    dh, dhb = dh_pair
    a, b, s, n = saved
    lp, d = h_in.shape
    fp = wgt.shape[1]
    ff = N_CHIPS * fp
    tm = _row_tile(lp)
    ni = lp // tm
    tk = _tn_tiles(lp)
    nk = lp // tk

    def ds_body(dh_ref, wd_ref, a_ref, b_ref, da_ref, db_ref):
        x = dh_ref[...]
        for jc in range(N_CHIPS):
            cols = slice(jc * fp, (jc + 1) * fp)
            ds = 0.5 * _dot(x, wd_ref[jc], "nt")
            av = a_ref[:, cols].astype(F32)
            bv = b_ref[:, cols].astype(F32)
            sg = _sigmoid(av)
            da_ref[:, cols] = (ds * bv * sg * (1.0 + av * (1.0 - sg))).astype(BF16)
            db_ref[:, cols] = (ds * av * sg).astype(BF16)

    act = jax.ShapeDtypeStruct((lp, ff), BF16)
    act_tile = pl.BlockSpec((tm, ff), lambda i: (i, 0))
    (da, db), couts = _hosted_call(
        ds_body, comm, out_shape=[act, act], grid=(ni,),
        in_specs=[pl.BlockSpec((tm, d), lambda i: (i, 0)), pl.BlockSpec((N_CHIPS, fp, d), lambda i: (0, 0, 0)),
                  act_tile, act_tile],
        out_specs=[act_tile, act_tile], scratch_shapes=[], sem=("parallel",), name="ffn_bwd_ds",
        args=(dhb, wd, a, b))

    dw_shape = jax.ShapeDtypeStruct((N_CHIPS, f4, d), F32)
    dw_spec = pl.BlockSpec((None, f4, d), lambda j, k: (j, 0, 0))
    in_col = pl.BlockSpec((tk, fp), lambda j, k: (k, j))
    in_row = pl.BlockSpec((tk, d), lambda j, k: (k, 0))

    half_shape = jax.ShapeDtypeStruct((N_CHIPS, f4 // 2, d), BF16)
    half_spec = pl.BlockSpec((None, f4 // 2, d), lambda j, k: (j, 0, 0))

    def dwd_epi(accs, in_refs, out_refs):
        dw = 0.5 * accs[0]
        out_refs[0][...] = dw[:f4]
        out_refs[1][...] = _sibling_half(dw, f4)

    dwd, dwd_other = _matmul(
        "ffn_dwd", (N_CHIPS, nk), 1, [s, dhb], [in_col, in_row],
        [(0, 1, "tn", 0)], [(fp, d)], dwd_epi, [dw_shape, half_shape], [dw_spec, half_spec],
        ("arbitrary", "arbitrary"))

    def dwgu_epi(accs, in_refs, out_refs):
        for i, acc in enumerate(accs):
            out_refs[i][...] = acc[:f4]
            out_refs[2 + i][...] = _sibling_half(acc, f4)

    red_down = _Reduction([dwd], [dwd_other], pos, "ffn_d")
    (dwg, dwu, dwg_other, dwu_other), red_down.got = _matmul(
        "ffn_dwgu", (N_CHIPS, nk), 1, [n, da, db], [in_row, in_col, in_col],
        [(1, 0, "tn", 0), (2, 0, "tn", 1)], [(fp, d)] * 2, dwgu_epi,
        [dw_shape, dw_shape, half_shape, half_shape], [dw_spec, dw_spec, half_spec, half_spec],
        ("arbitrary", "arbitrary"), red_down.comm())

    def dn_epi(accs, in_refs, out_refs):
        i = pl.program_id(0)
        dx, dgrow = _rms_bwd_math(accs[0], in_refs[5][...], in_refs[6][...])
        dh_new = in_refs[4][...] + dx
        out_refs[0][...] = dh_new
        out_refs[2][...] = dh_new.astype(BF16)

        @pl.when(i == 0)
        def _():
            out_refs[1][...] = jnp.zeros_like(out_refs[1])

        out_refs[1][...] += jnp.sum(dgrow, axis=0, keepdims=True)

    red = _Reduction([dwg, dwu], [dwg_other, dwu_other], pos, "ffn_gu")
    row_spec = pl.BlockSpec((tm, d), lambda i: (i, 0))
    act_spec = pl.BlockSpec((tm, ff), lambda i: (i, 0))
    w_spec = pl.BlockSpec((ff, d), lambda i: (0, 0))
    one_spec = pl.BlockSpec((1, d), lambda i: (0, 0))
    (dh_in, dgain, dh_in_b), red.got = _matmul(
        "ffn_bwd_dn", (ni,), None, [da, wgt.reshape(ff, d), db, wut.reshape(ff, d), dh, h_in, gain.reshape(1, d)],
        [act_spec, w_spec, act_spec, w_spec, row_spec, row_spec, one_spec],
        [(0, 1, "nn", 0), (2, 3, "nn", 0)], [(tm, d)], dn_epi,
        [jax.ShapeDtypeStruct((lp, d), F32), jax.ShapeDtypeStruct((1, d), F32), jax.ShapeDtypeStruct((lp, d), BF16)],
        [row_spec, one_spec, row_spec], ("arbitrary",), red.comm())
    return (dh_in, dh_in_b), dgain, [red, red_down], couts


def _rope_tables(lp):
    pos = jnp.arange(lp, dtype=F32) - float(PAD_FRONT)
    inv_freq = ROPE_THETA ** (-jnp.arange(0, ROT_DIM, 2, dtype=F32) / ROT_DIM)
    ang = pos[:, None] * inv_freq[None, :]
    cos, sin = jnp.cos(ang), jnp.sin(ang)
    half = ROT_DIM // 2
    ones = jnp.ones((lp, HEAD_DIM - ROT_DIM), F32)
    zeros_h = jnp.zeros((lp, half), F32)
    zeros_r = jnp.zeros((lp, HEAD_DIM - ROT_DIM), F32)
    c = jnp.concatenate([cos, cos, ones], axis=1)
    s1 = jnp.concatenate([-sin, zeros_h, zeros_r], axis=1)
    s2 = jnp.concatenate([zeros_h, sin, zeros_r], axis=1)
    reps = LANES // HEAD_DIM
    return jnp.stack([jnp.tile(c, (1, reps)), jnp.tile(s1, (1, reps)), jnp.tile(s2, (1, reps))])


def _rope(x, c, s1, s2):
    half = ROT_DIM // 2
    outs = []
    for ch in range(x.shape[1] // LANES):
        xc = x[:, ch * LANES:(ch + 1) * LANES]
        outs.append(xc * c + pltpu.roll(xc, LANES - half, 1) * s1 + pltpu.roll(xc, half, 1) * s2)
    return outs[0] if len(outs) == 1 else jnp.concatenate(outs, axis=1)


def _rope_t(dy, c, s1, s2):
    half = ROT_DIM // 2
    outs = []
    for ch in range(dy.shape[1] // LANES):
        dc = dy[:, ch * LANES:(ch + 1) * LANES]
        outs.append(dc * c + pltpu.roll(dc * s1, half, 1) + pltpu.roll(dc * s2, LANES - half, 1))
    return outs[0] if len(outs) == 1 else jnp.concatenate(outs, axis=1)


def _in_proj(n, w_in, tabs, ssm_w, comm=None):
    lp, d = n.shape
    inw = w_in.shape[0]
    tm = _row_tile(lp)
    o1 = ATTN_WIDTH
    o2 = o1 + KV_WIDTH
    o3 = o2 + KV_WIDTH
    o4 = o3 + ssm_w
    o5 = o4 + d

    def epi(accs, in_refs, out_refs):
        z = accs[0]
        c, s1, s2 = in_refs[2][0], in_refs[2][1], in_refs[2][2]
        out_refs[0][...] = _rope(z[:, :o1], c, s1, s2).astype(BF16)
        out_refs[1][...] = _rope(z[:, o1:o2], c, s1, s2).astype(BF16)
        out_refs[2][...] = z[:, o2:o3].astype(BF16)
        out_refs[3][...] = z[:, o3:o4]
        out_refs[4][...] = z[:, o4:o5]
        out_refs[5][...] = z[:, o5:]

    def rs(w, dt):
        return jax.ShapeDtypeStruct((lp, w), dt), pl.BlockSpec((tm, w), lambda i: (i, 0))

    shapes, specs = zip(rs(o1, BF16), rs(KV_WIDTH, BF16), rs(KV_WIDTH, BF16), rs(ssm_w, F32), rs(d, F32), rs(d, F32))
    res = _matmul(
        "mix_in_proj", (lp // tm,), None, [n, w_in, tabs],
        [pl.BlockSpec((tm, d), lambda i: (i, 0)), pl.BlockSpec((inw, d), lambda i: (0, 0)),
         pl.BlockSpec((3, tm, LANES), lambda i: (0, i, 0))],
        [(0, 1, "nt", 0)], [(tm, inw)], epi, list(shapes), list(specs), ("parallel",), comm)
    return (res, []) if comm is None else res


def _attn_mask(b):
    rows = lax.broadcasted_iota(jnp.int32, (BLOCK, 3 * BLOCK), 0)
    cols = lax.broadcasted_iota(jnp.int32, (BLOCK, 3 * BLOCK), 1)
    qpos = b * BLOCK + rows - PAD_FRONT
    kpos = (b - 1) * BLOCK + cols - PAD_FRONT
    dist = qpos - kpos
    band = (cols < 2 * BLOCK) & (kpos >= N_META) & (dist >= 0) & (dist < BLOCK)
    mrow = cols - 2 * BLOCK
    meta = (mrow >= PAD_FRONT) & ((mrow - PAD_FRONT) <= qpos)
    return band | meta


def _attn_probs(qh, kk, mask, sink):
    s = _dot(qh, kk, "nt") * (HEAD_DIM ** -0.5)
    s = jnp.where(mask, s, NEG_INF)
    m = jnp.maximum(jnp.max(s, axis=-1, keepdims=True), sink)
    e = jnp.exp(s - m)
    es = jnp.exp(sink - m)
    z = jnp.sum(e, axis=-1, keepdims=True) + es
    inv = 1.0 / z
    return e * inv, es * inv


def _head(ref_or_val, h):
    return ref_or_val[:, h * HEAD_DIM:(h + 1) * HEAD_DIM]


def _attn_fwd(q, k, v, sinks, comm=None):
    lp = q.shape[0]
    nb = lp // BLOCK

    def body(sink_ref, q_ref, kp_ref, kc_ref, km_ref, vp_ref, vc_ref, vm_ref, o_ref):
        b = pl.program_id(0)
        mask = _attn_mask(b)
        for hk in range(N_KV_HEADS):
            kk = jnp.concatenate([_head(kp_ref, hk), _head(kc_ref, hk), _head(km_ref, hk)], axis=0)
            vv = jnp.concatenate([_head(vp_ref, hk), _head(vc_ref, hk), _head(vm_ref, hk)], axis=0)
            for g in range(Q_PER_KV):
                h = hk * Q_PER_KV + g
                p, _ = _attn_probs(_head(q_ref, h), kk, mask, sink_ref[h])
                o_ref[:, h * HEAD_DIM:(h + 1) * HEAD_DIM] = _dot(p, vv, "nn").astype(BF16)

    cur = lambda b: (b, 0)
    prev = lambda b: (jnp.maximum(b - 1, 0), 0)
    first = lambda b: (0, 0)
    kvs = lambda f: pl.BlockSpec((BLOCK, KV_WIDTH), f)
    (o,), couts = _hosted_call(
        body, comm, out_shape=[jax.ShapeDtypeStruct((lp, ATTN_WIDTH), BF16)], grid=(nb,),
        in_specs=[pl.BlockSpec(memory_space=pltpu.SMEM), pl.BlockSpec((BLOCK, ATTN_WIDTH), cur),
                  kvs(prev), kvs(cur), kvs(first), kvs(prev), kvs(cur), kvs(first)],
        out_specs=[pl.BlockSpec((BLOCK, ATTN_WIDTH), cur)], scratch_shapes=[],
        sem=("parallel",), name="attn_fwd", args=(sinks, q, k, k, k, v, v, v))
    return o, couts


def _attn_bwd(q, k, v, do, sinks, tabs, comm=None):
    lp = q.shape[0]
    nb = lp // BLOCK
    scale = HEAD_DIM ** -0.5

    def body(sink_ref, q_ref, do_ref, kp_ref, kc_ref, km_ref, vp_ref, vc_ref, vm_ref, tq_ref, tk_ref, t0_ref,
             dq_ref, dk_ref, dv_ref, dkm_ref, dvm_ref, dsink_ref,
             dq_s, dkk_s, dvv_s, ck_s, cv_s, mk_s, mv_s):
        b = pl.program_id(0)

        @pl.when(b == 0)
        def _():
            for r in (ck_s, cv_s, mk_s, mv_s, dsink_ref):
                r[...] = jnp.zeros_like(r)

        @pl.when(b < nb)
        def _():
            mask = _attn_mask(b)
            for hk in range(N_KV_HEADS):
                kk = jnp.concatenate([_head(kp_ref, hk), _head(kc_ref, hk), _head(km_ref, hk)], axis=0)
                vv = jnp.concatenate([_head(vp_ref, hk), _head(vc_ref, hk), _head(vm_ref, hk)], axis=0)
                dkk = jnp.zeros((3 * BLOCK, HEAD_DIM), F32)
                dvv = jnp.zeros((3 * BLOCK, HEAD_DIM), F32)
                for g in range(Q_PER_KV):
                    h = hk * Q_PER_KV + g
                    qh = _head(q_ref, h)
                    doh = _head(do_ref, h)
                    p, ps = _attn_probs(qh, kk, mask, sink_ref[h])
                    dp = _dot(doh, vv, "nt")
                    delta = jnp.sum(p * dp, axis=-1, keepdims=True)
                    ds = (p * (dp - delta)).astype(BF16)
                    dsink_ref[h:h + 1, :] += jnp.zeros((1, LANES), F32) - jnp.sum(ps * delta)
                    dq_s[:, h * HEAD_DIM:(h + 1) * HEAD_DIM] = _dot(ds, kk, "nn") * scale
                    dkk = dkk + _dot(ds, qh, "tn") * scale
                    dvv = dvv + _dot(p, doh, "tn")
                dkk_s[:, hk * HEAD_DIM:(hk + 1) * HEAD_DIM] = dkk
                dvv_s[:, hk * HEAD_DIM:(hk + 1) * HEAD_DIM] = dvv
            dq_ref[...] = _rope_t(dq_s[...], tq_ref[0], tq_ref[1], tq_ref[2])
            dk_ref[...] = _rope_t(ck_s[...] + dkk_s[0:BLOCK, :], tk_ref[0], tk_ref[1], tk_ref[2])
            dv_ref[...] = cv_s[...] + dvv_s[0:BLOCK, :]
            ck_s[...] = dkk_s[BLOCK:2 * BLOCK, :]
            cv_s[...] = dvv_s[BLOCK:2 * BLOCK, :]
            mk_s[...] += dkk_s[2 * BLOCK:, :]
            mv_s[...] += dvv_s[2 * BLOCK:, :]

        @pl.when(b == nb)
        def _():
            dk_ref[...] = _rope_t(ck_s[...], tk_ref[0], tk_ref[1], tk_ref[2])
            dv_ref[...] = cv_s[...]
            dkm_ref[...] = _rope_t(mk_s[...], t0_ref[0], t0_ref[1], t0_ref[2])
            dvm_ref[...] = mv_s[...]

    cur = lambda b: (jnp.minimum(b, nb - 1), 0)
    prev = lambda b: (jnp.clip(b - 1, 0, nb - 1), 0)
    first = lambda b: (0, 0)
    kvs = lambda f: pl.BlockSpec((BLOCK, KV_WIDTH), f)
    tab = lambda f: pl.BlockSpec((3, BLOCK, LANES), lambda b: (0,) + f(b)[:1] + (0,))
    kv_out = lambda b: (jnp.maximum(b - 1, 0), 0)
    return _hosted_call(
        body, comm,
        out_shape=[jax.ShapeDtypeStruct((lp, ATTN_WIDTH), F32), jax.ShapeDtypeStruct((lp, KV_WIDTH), F32),
                   jax.ShapeDtypeStruct((lp, KV_WIDTH), F32), jax.ShapeDtypeStruct((BLOCK, KV_WIDTH), F32),
                   jax.ShapeDtypeStruct((BLOCK, KV_WIDTH), F32), jax.ShapeDtypeStruct((N_Q_HEADS, LANES), F32)],
        grid=(nb + 1,),
        in_specs=[pl.BlockSpec(memory_space=pltpu.SMEM), pl.BlockSpec((BLOCK, ATTN_WIDTH), cur),
                  pl.BlockSpec((BLOCK, ATTN_WIDTH), cur),
                  kvs(prev), kvs(cur), kvs(first), kvs(prev), kvs(cur), kvs(first),
                  tab(cur), tab(kv_out), tab(first)],
        out_specs=[pl.BlockSpec((BLOCK, ATTN_WIDTH), cur), kvs(kv_out), kvs(kv_out), kvs(first), kvs(first),
                   pl.BlockSpec((N_Q_HEADS, LANES), first)],
        scratch_shapes=[pltpu.VMEM((BLOCK, ATTN_WIDTH), F32), pltpu.VMEM((3 * BLOCK, KV_WIDTH), F32),
                        pltpu.VMEM((3 * BLOCK, KV_WIDTH), F32), pltpu.VMEM((BLOCK, KV_WIDTH), F32),
                        pltpu.VMEM((BLOCK, KV_WIDTH), F32), pltpu.VMEM((BLOCK, KV_WIDTH), F32),
                        pltpu.VMEM((BLOCK, KV_WIDTH), F32)],
        sem=("arbitrary",), name="attn_bwd", args=(sinks, q, do, k, k, k, v, v, v, tabs, tabs, tabs))


def _cmul(ar, ai, br, bi):
    return ar * br - ai * bi, ar * bi + ai * br


def _cpow(lr, li, n):
    rr = ri = None
    br, bi = lr, li
    while n:
        if n & 1:
            rr, ri = (br, bi) if rr is None else _cmul(rr, ri, br, bi)
        n >>= 1
        if n:
            br, bi = _cmul(br, bi, br, bi)
    return rr, ri


def _shift_rows(x, d, reverse):
    rows = lax.broadcasted_iota(jnp.int32, x.shape, 0)
    if not reverse:
        return jnp.where(rows >= d, pltpu.roll(x, d, 0), 0.0)
    return jnp.where(rows < SUBLANES - d, pltpu.roll(x, SUBLANES - d, 0), 0.0)


def _sublane_powers(mr, mi, reverse):
    rows = lax.broadcasted_iota(jnp.int32, mr.shape, 0)
    e = SUBLANES - 1 - rows if reverse else rows
    pr, pi = jnp.ones_like(mr), jnp.zeros_like(mr)
    br, bi = mr, mi
    for d in (1, 2, 4):
        tr, ti = _cmul(pr, pi, br, bi)
        on = (e & d) != 0
        pr, pi = jnp.where(on, tr, pr), jnp.where(on, ti, pi)
        if d < 4:
            br, bi = _cmul(br, bi, br, bi)
    return pr, pi


def _inclusive_prefix(er, ei, mr, mi, reverse):
    ir, ii, pr, pi = er, ei, mr, mi
    for d in (1, 2, 4):
        tr, ti = _cmul(pr, pi, _shift_rows(ir, d, reverse), _shift_rows(ii, d, reverse))
        ir, ii = ir + tr, ii + ti
        if d < 4:
            pr, pi = _cmul(pr, pi, pr, pi)
    return ir, ii


def _chain_rows(a, t, seg):
    return pl.ds(a * SUBLANES * seg + t, SUBLANES, stride=seg)


def _seg_scan(xr_ref, xi_ref, lam, seg, nchain, reverse, store, init, extra=None):
    nt = len(lam)
    acc0 = () if extra is None else extra[1]

    def step(i, carry):
        hs, acc = carry
        t = seg - 1 - i if reverse else i
        out = []
        for a in range(nchain):
            sl = _chain_rows(a, t, seg)
            for j in range(nt):
                lr, li = lam[j]
                k = 2 * (a * nt + j)
                hr, hi = hs[k], hs[k + 1]
                nr = lr * hr - li * hi + xr_ref[j, sl, :]
                ni = lr * hi + li * hr + xi_ref[j, sl, :]
                if store:
                    xr_ref[j, sl, :] = nr
                    xi_ref[j, sl, :] = ni
                if extra is not None:
                    acc = extra[0](t, a, j, nr, ni, acc)
                out += [nr, ni]
        return tuple(out), acc

    return lax.fori_loop(0, seg, step, (tuple(init), acc0))


def _ssm_scan(xr_ref, xi_ref, lam, seg, nchain, reverse, extra=None):
    nt = len(lam)
    zero = [jnp.zeros((SUBLANES, LANES), F32)] * (2 * nt * nchain)
    ends, _ = _seg_scan(xr_ref, xi_ref, lam, seg, nchain, reverse, False, zero)
    init = [None] * (2 * nt * nchain)
    last = 0 if reverse else SUBLANES - 1
    for j in range(nt):
        mr, mi = _cpow(lam[j][0], lam[j][1], seg)
        m8r, m8i = _cpow(mr, mi, SUBLANES)
        pwr, pwi = _sublane_powers(mr, mi, reverse)
        gr = gi = jnp.zeros((SUBLANES, LANES), F32)
        for a in (reversed(range(nchain)) if reverse else range(nchain)):
            k = 2 * (a * nt + j)
            incr, inci = _inclusive_prefix(ends[k], ends[k + 1], mr, mi, reverse)
            tr, ti = _cmul(pwr, pwi, gr, gi)
            init[k] = _shift_rows(incr, 1, reverse) + tr
            init[k + 1] = _shift_rows(inci, 1, reverse) + ti
            g2r, g2i = _cmul(m8r, m8i, gr, gi)
            gr = g2r + jnp.broadcast_to(incr[last:last + 1, :], gr.shape)
            gi = g2i + jnp.broadcast_to(inci[last:last + 1, :], gi.shape)
    _, acc = _seg_scan(xr_ref, xi_ref, lam, seg, nchain, reverse, True, init, extra)
    return acc


def _diag_mask():
    steps = LANES // SSM_GROUP // GROUPS_PER_TILE
    return (jnp.eye(steps, dtype=F32)[:, None, :, None] * jnp.eye(GROUPS_PER_TILE, dtype=F32)[None, :, None, :])


def _ssm_tables(bb_re, bb_im, c_re, c_im):
    g = bb_re.shape[0]
    nt = g // GROUPS_PER_TILE
    steps = LANES // SSM_GROUP // GROUPS_PER_TILE
    mask = _diag_mask()

    def b_tab(bb):
        x = bb.reshape(nt // steps, steps, GROUPS_PER_TILE, SSM_STATE, SSM_GROUP)
        x = jnp.transpose(x, (0, 1, 4, 2, 3))[:, :, None, None]
        m = jnp.transpose(mask, (0, 2, 3, 1))[None, :, :, :, None, :, None]
        return (x * m).reshape(nt, LANES, TILE_STATES)

    def c_tab(c):
        x = c.reshape(nt // steps, steps, GROUPS_PER_TILE, SSM_GROUP, SSM_STATE)
        x = jnp.transpose(x, (0, 1, 2, 4, 3))[:, :, :, :, None, None]
        m = mask[None, :, :, None, :, :, None]
        return (x * m).reshape(nt, TILE_STATES, LANES)

    return b_tab(bb_re), b_tab(bb_im), c_tab(c_re), c_tab(c_im)


def _ssm_untable_b(db, g):
    nt = g // GROUPS_PER_TILE
    steps = LANES // SSM_GROUP // GROUPS_PER_TILE
    x = db.reshape(nt // steps, steps, GROUPS_PER_TILE, SSM_STATE, steps, GROUPS_PER_TILE, SSM_GROUP)
    m = _diag_mask()[None, :, :, None, :, :, None]
    return jnp.sum(x * m, axis=(4, 5)).reshape(g, SSM_STATE, SSM_GROUP)


def _ssm_untable_c(dc, g):
    nt = g // GROUPS_PER_TILE
    steps = LANES // SSM_GROUP // GROUPS_PER_TILE
    x = dc.reshape(nt // steps, steps, steps, GROUPS_PER_TILE, SSM_GROUP, GROUPS_PER_TILE, SSM_STATE)
    m = jnp.transpose(_diag_mask(), (0, 2, 3, 1))[None, :, :, :, None, :, None]
    out = jnp.sum(x * m, axis=(2, 3))
    return jnp.transpose(out, (0, 1, 3, 2, 4)).reshape(g, SSM_GROUP, SSM_STATE)


def _lam_tiles(lam_ref):
    out = []
    for j in range(TILE_STATES // LANES):
        out.append(jnp.broadcast_to(lam_ref[:, j * LANES:(j + 1) * LANES], (SUBLANES, LANES)))
    return out


def _scan_chains(lp):
    for n in (4, 2, 1):
        if lp % (SUBLANES * n) == 0 and (lp // SUBLANES) % 16 == 0:
            return n
    raise ValueError(lp)


def _split_tiles(dst_ref, rows, val):
    for j in range(val.shape[1] // LANES):
        dst_ref[j, rows, :] = val[:, j * LANES:(j + 1) * LANES]


def _cat_tiles(src_ref, rows):
    njt = src_ref.shape[0]
    return jnp.concatenate([src_ref[j, rows, :] for j in range(njt)], axis=1).astype(BF16)


def _ssm_fwd(u, lam_re, lam_im, tb_re, tb_im, tc_re, tc_im, d_skip, comm=None):
    lp, w = u.shape
    nt = tb_re.shape[0]
    nchain = _scan_chains(lp)
    seg = lp // (SUBLANES * nchain)
    chunk = lp // SUBLANES
    njt = TILE_STATES // LANES

    def body(u_ref, lr_ref, li_ref, br_ref, bi_ref, cr_ref, ci_ref, d_ref, y_ref, yg_ref, xr, xi):
        t = pl.program_id(0)
        for s in range(SUBLANES):
            rs = pl.ds(s * chunk, chunk)
            ub = u_ref[rs, :].astype(BF16)
            _split_tiles(xr, rs, _dot(ub, br_ref[...], "nn"))
            _split_tiles(xi, rs, _dot(ub, bi_ref[...], "nn"))
        lrs, lis = _lam_tiles(lr_ref), _lam_tiles(li_ref)
        _ssm_scan(xr, xi, list(zip(lrs, lis)), seg, nchain, False)
        for s in range(SUBLANES):
            rs = pl.ds(s * chunk, chunk)
            y = _dot(_cat_tiles(xr, rs), cr_ref[...], "nn") - _dot(_cat_tiles(xi, rs), ci_ref[...], "nn")

            @pl.when(t % 2 == 0)
            def _():
                y_ref[rs, :] = y + d_ref[...] * u_ref[rs, :]

            @pl.when(t % 2 == 1)
            def _():
                total = y_ref[rs, :] + y
                y_ref[rs, :] = total
                yg_ref[rs, :] = _gelu(total).astype(BF16)

    blk = pl.BlockSpec((lp, LANES), lambda t: (0, t // 2))
    lam_spec = pl.BlockSpec((None, 1, TILE_STATES), lambda t: (t, 0, 0))
    b_spec = pl.BlockSpec((None, LANES, TILE_STATES), lambda t: (t, 0, 0))
    c_spec = pl.BlockSpec((None, TILE_STATES, LANES), lambda t: (t, 0, 0))
    (y, yg), couts = _hosted_call(
        body, comm, out_shape=[jax.ShapeDtypeStruct((lp, w), F32), jax.ShapeDtypeStruct((lp, w), BF16)], grid=(nt,),
        in_specs=[blk, lam_spec, lam_spec, b_spec, b_spec, c_spec, c_spec,
                  pl.BlockSpec((1, LANES), lambda t: (0, t // 2))],
        out_specs=[blk, blk],
        scratch_shapes=[pltpu.VMEM((njt, lp, LANES), F32), pltpu.VMEM((njt, lp, LANES), F32)],
        sem=("arbitrary",), name="ssm_fwd",
        args=(u, lam_re, lam_im, tb_re, tb_im, tc_re, tc_im, d_skip.reshape(1, w)))
    return (y, yg), couts


def _ssm_bwd(u, dy, lam_re, lam_im, tb_re, tb_im, tc_re, tc_im, d_skip, comm=None):
    lp, w = u.shape
    nt = tb_re.shape[0]
    nchain = _scan_chains(lp)
    seg = lp // (SUBLANES * nchain)
    chunk = lp // SUBLANES
    njt = TILE_STATES // LANES
    tbt_re, tbt_im = jnp.swapaxes(tb_re, 1, 2), jnp.swapaxes(tb_im, 1, 2)
    tct_re, tct_im = jnp.swapaxes(tc_re, 1, 2), jnp.swapaxes(tc_im, 1, 2)

    def body(u_ref, dy_ref, lr_ref, li_ref, br_ref, bi_ref, btr_ref, bti_ref, ctr_ref, cti_ref, d_ref,
             du_ref, dlr_ref, dli_ref, dbr_ref, dbi_ref, dcr_ref, dci_ref, dd_ref, hr, hi, ar, ai):
        t = pl.program_id(0)
        lrs, lis = _lam_tiles(lr_ref), _lam_tiles(li_ref)
        for s in range(SUBLANES):
            rs = pl.ds(s * chunk, chunk)
            ub = u_ref[rs, :].astype(BF16)
            dyb = dy_ref[rs, :].astype(BF16)
            _split_tiles(hr, rs, _dot(ub, br_ref[...], "nn"))
            _split_tiles(hi, rs, _dot(ub, bi_ref[...], "nn"))
            _split_tiles(ar, rs, _dot(dyb, ctr_ref[...], "nn"))
            _split_tiles(ai, rs, -_dot(dyb, cti_ref[...], "nn"))
        _ssm_scan(hr, hi, list(zip(lrs, lis)), seg, nchain, False)

        def dlam_step(tt, a, j, a_r, a_i, acc):
            sl = _chain_rows(a, jnp.maximum(tt - 1, 0), seg)
            p_r, p_i = hr[j, sl, :], hi[j, sl, :]
            acc = list(acc)
            acc[2 * j] = acc[2 * j] + jnp.where(tt > 0, a_r * p_r + a_i * p_i, 0.0)
            acc[2 * j + 1] = acc[2 * j + 1] + jnp.where(tt > 0, a_i * p_r - a_r * p_i, 0.0)
            return tuple(acc)

        zero = tuple([jnp.zeros((SUBLANES, LANES), F32)] * (2 * njt))
        conj = [(lr, -li) for lr, li in zip(lrs, lis)]
        acc = list(_ssm_scan(ar, ai, conj, seg, nchain, True, (dlam_step, zero)))
        row0 = lax.broadcasted_iota(jnp.int32, (SUBLANES, LANES), 0) == 0
        for j in range(njt):
            cs = slice(j * LANES, (j + 1) * LANES)
            for a in range(nchain):
                p_r = _shift_rows(hr[j, _chain_rows(a, seg - 1, seg), :], 1, False)
                p_i = _shift_rows(hi[j, _chain_rows(a, seg - 1, seg), :], 1, False)
                if a > 0:
                    before = pl.ds(a * SUBLANES * seg - 1, 1)
                    p_r = jnp.where(row0, jnp.broadcast_to(hr[j, before, :], p_r.shape), p_r)
                    p_i = jnp.where(row0, jnp.broadcast_to(hi[j, before, :], p_i.shape), p_i)
                a_r, a_i = ar[j, _chain_rows(a, 0, seg), :], ai[j, _chain_rows(a, 0, seg), :]
                acc[2 * j] = acc[2 * j] + a_r * p_r + a_i * p_i
                acc[2 * j + 1] = acc[2 * j + 1] + a_i * p_r - a_r * p_i
            dlr_ref[:, cs] = jnp.sum(acc[2 * j], axis=0, keepdims=True)
            dli_ref[:, cs] = jnp.sum(acc[2 * j + 1], axis=0, keepdims=True)

        dd = jnp.zeros((1, LANES), F32)
        for s in range(SUBLANES):
            rs = pl.ds(s * chunk, chunk)
            ub = u_ref[rs, :].astype(BF16)
            dyv = dy_ref[rs, :]
            dyb = dyv.astype(BF16)
            arb, aib = _cat_tiles(ar, rs), _cat_tiles(ai, rs)
            hrb, hib = _cat_tiles(hr, rs), _cat_tiles(hi, rs)
            du = _dot(arb, btr_ref[...], "nn") + _dot(aib, bti_ref[...], "nn")
            upd = [(dbr_ref, _dot(arb, ub, "tn")), (dbi_ref, _dot(aib, ub, "tn")),
                   (dcr_ref, _dot(dyb, hrb, "tn")), (dci_ref, -_dot(dyb, hib, "tn"))]
            for ref, val in upd:
                if s == 0:
                    ref[...] = val
                else:
                    ref[...] += val
            rows = lax.broadcasted_iota(jnp.int32, (chunk, LANES), 0) + s * chunk
            keep = rows >= PAD_FRONT
            dd = dd + jnp.sum(dyv * u_ref[rs, :], axis=0, keepdims=True)

            @pl.when(t % 2 == 0)
            def _():
                du_ref[rs, :] = jnp.where(keep, du + d_ref[...] * dyv, 0.0)

            @pl.when(t % 2 == 1)
            def _():
                du_ref[rs, :] += jnp.where(keep, du, 0.0)

        @pl.when(t % 2 == 0)
        def _():
            dd_ref[...] = dd

    blk = pl.BlockSpec((lp, LANES), lambda t: (0, t // 2))
    vec = pl.BlockSpec((1, LANES), lambda t: (0, t // 2))
    lam_spec = pl.BlockSpec((None, 1, TILE_STATES), lambda t: (t, 0, 0))
    b_spec = pl.BlockSpec((None, LANES, TILE_STATES), lambda t: (t, 0, 0))
    c_spec = pl.BlockSpec((None, TILE_STATES, LANES), lambda t: (t, 0, 0))
    lam_shape = jax.ShapeDtypeStruct((nt, 1, TILE_STATES), F32)
    bt_shape = jax.ShapeDtypeStruct((nt, TILE_STATES, LANES), F32)
    ct_shape = jax.ShapeDtypeStruct((nt, LANES, TILE_STATES), F32)
    st = pltpu.VMEM((njt, lp, LANES), F32)
    return _hosted_call(
        body, comm,
        out_shape=[jax.ShapeDtypeStruct((lp, w), F32), lam_shape, lam_shape, bt_shape, bt_shape, ct_shape, ct_shape,
                   jax.ShapeDtypeStruct((1, w), F32)],
        grid=(nt,),
        in_specs=[blk, blk, lam_spec, lam_spec, b_spec, b_spec, c_spec, c_spec, b_spec, b_spec, vec],
        out_specs=[blk, lam_spec, lam_spec, c_spec, c_spec, b_spec, b_spec, vec],
        scratch_shapes=[st, st, st, st], sem=("arbitrary",), name="ssm_bwd",
        args=(u, dy, lam_re, lam_im, tb_re, tb_im, tbt_re, tbt_im, tct_re, tct_im, d_skip.reshape(1, w)))


def _ssm_params(a_re, a_im, log_dt, b_re, b_im):
    dt = jnp.exp(log_dt)[:, None]
    mag = jnp.exp(a_re * dt)
    lb_re = mag * jnp.cos(a_im * dt)
    lb_im = mag * jnp.sin(a_im * dt)
    den = a_re * a_re + a_im * a_im
    num_re = lb_re - 1.0
    coef_re = (num_re * a_re + lb_im * a_im) / den
    coef_im = (lb_im * a_re - num_re * a_im) / den
    bb_re = coef_re[..., None] * b_re - coef_im[..., None] * b_im
    bb_im = coef_re[..., None] * b_im + coef_im[..., None] * b_re
    return lb_re, lb_im, bb_re, bb_im


def _merge_fwd(o, yg, ga, gs, w3t, comm=None):
    lp, d = ga.shape
    kw = w3t.shape[2]
    tm = _row_tile(lp)

    def epi(accs, in_refs, out_refs):
        attn, vv, gg = accs
        out_refs[0][...] = (_sigmoid(in_refs[5][...]) * attn
                            + _sigmoid(in_refs[6][...]) * (vv * _sigmoid(gg))).astype(BF16)

    wspec = lambda which: pl.BlockSpec((None, d, kw), lambda i: (which, 0, 0))
    rowspec = pl.BlockSpec((tm, d), lambda i: (i, 0))
    aspec = pl.BlockSpec((tm, kw), lambda i: (i, 0))
    res = _matmul(
        "merge_fwd", (lp // tm,), None, [o, yg, w3t, w3t, w3t, ga, gs],
        [aspec, aspec, wspec(0), wspec(1), wspec(2), rowspec, rowspec],
        [(0, 2, "nt", 0), (1, 3, "nt", 1), (1, 4, "nt", 2)], [(tm, d)] * 3, epi,
        [jax.ShapeDtypeStruct((lp, d), BF16)], [rowspec], ("parallel",), comm)
    return (res[0], []) if comm is None else (res[0][0], res[1])


def _out_proj(merged, w_out, h, next_gain, comm=None):
    lp, d = h.shape
    tm = _row_tile(lp)
    shapes, specs, extra_in, extra_specs = _residual_specs(lp, d, tm, next_gain)

    def epi(accs, in_refs, out_refs):
        _residual_outputs(in_refs[2][...] + accs[0], in_refs, out_refs, 3 if extra_in else None)

    rowspec = pl.BlockSpec((tm, d), lambda i: (i, 0))
    res = _matmul(
        "mix_out_proj", (lp // tm,), None, [merged, w_out, h] + extra_in,
        [rowspec, pl.BlockSpec((d, d), lambda i: (0, 0)), rowspec] + extra_specs,
        [(0, 1, "nn", 0)], [(tm, d)], epi, shapes, specs, ("parallel",), comm)
    return (res, []) if comm is None else res


def _merge_bwd(dhb, w_out, o, yg, ga, gs, w3t):
    lp, d = ga.shape
    kw = w3t.shape[2]
    tm = _row_tile(lp)

    def epi(accs, in_refs, out_refs):
        dm, attn, vv, gg = accs
        sa = _sigmoid(in_refs[7][...])
        ss = _sigmoid(in_refs[8][...])
        sg = _sigmoid(gg)
        ssm = vv * sg
        dssm = dm * ss
        out_refs[0][...] = (dm * sa).astype(BF16)
        out_refs[1][...] = (dssm * sg).astype(BF16)
        out_refs[2][...] = (dssm * vv * sg * (1.0 - sg)).astype(BF16)
        out_refs[3][...] = (dm * attn * sa * (1.0 - sa)).astype(BF16)
        out_refs[4][...] = (dm * ssm * ss * (1.0 - ss)).astype(BF16)

    wspec = lambda which: pl.BlockSpec((None, d, kw), lambda i: (which, 0, 0))
    rowspec = pl.BlockSpec((tm, d), lambda i: (i, 0))
    aspec = pl.BlockSpec((tm, kw), lambda i: (i, 0))
    shp = jax.ShapeDtypeStruct((lp, d), BF16)
    return _matmul(
        "merge_bwd", (lp // tm,), None, [dhb, w_out, o, yg, w3t, w3t, w3t, ga, gs],
        [rowspec, pl.BlockSpec((d, d), lambda i: (0, 0)), aspec, aspec, wspec(0), wspec(1), wspec(2), rowspec,
         rowspec],
        [(0, 1, "nt", 0), (2, 4, "nt", 1), (3, 5, "nt", 2), (3, 6, "nt", 3)], [(tm, d)] * 4, epi,
        [shp] * 5, [rowspec] * 5, ("parallel",))


def _branch_bwd(dattn, dv, dg, w3t, y):
    lp, d = dattn.shape
    kw = w3t.shape[2]
    tm = _row_tile(lp)

    def epi(accs, in_refs, out_refs):
        out_refs[0][...] = accs[0].astype(BF16)
        out_refs[1][...] = accs[1] * _gelu_grad(in_refs[6][...])

    wspec = lambda which: pl.BlockSpec((None, d, kw), lambda i: (which, 0, 0))
    rowspec = pl.BlockSpec((tm, d), lambda i: (i, 0))
    aspec = pl.BlockSpec((tm, kw), lambda i: (i, 0))
    return _matmul(
        "branch_bwd", (lp // tm,), None, [dattn, dv, dg, w3t, w3t, w3t, y],
        [rowspec, rowspec, rowspec, wspec(0), wspec(1), wspec(2), aspec],
        [(0, 3, "nn", 0), (1, 4, "nn", 1), (2, 5, "nn", 1)], [(tm, kw)] * 2, epi,
        [jax.ShapeDtypeStruct((lp, kw), BF16), jax.ShapeDtypeStruct((lp, kw), F32)], [aspec, aspec],
        ("parallel",))


def _sibling_half(acc, rows):
    half = rows // 2
    return jnp.where(lax.axis_index("c") == 0, acc[half:rows], acc[:half]).astype(BF16)


def _tn_cols(x, ys, name):
    lp, kx = x.shape
    n = ys[0].shape[1]
    n4 = n // N_CHIPS
    tk = _tn_tiles(lp)
    ny = len(ys)

    def epi(accs, in_refs, out_refs):
        for i, acc in enumerate(accs):
            out_refs[i][...] = acc
            out_refs[ny + i][...] = _sibling_half(acc, kx)

    shp = jax.ShapeDtypeStruct((N_CHIPS, kx, n4), F32)
    shp_half = jax.ShapeDtypeStruct((N_CHIPS, kx // 2, n4), BF16)
    res = _matmul(
        name, (N_CHIPS, lp // tk), 1, [x] + list(ys),
        [pl.BlockSpec((tk, kx), lambda j, k: (k, 0))] + [pl.BlockSpec((tk, n4), lambda j, k: (k, j))] * ny,
        [(0, 1 + i, "tn", i) for i in range(ny)], [(kx, n4)] * ny, epi,
        [shp] * ny + [shp_half] * ny,
        [pl.BlockSpec((None, kx, n4), lambda j, k: (j, 0, 0))] * ny
        + [pl.BlockSpec((None, kx // 2, n4), lambda j, k: (j, 0, 0))] * ny,
        ("arbitrary", "arbitrary"))
    return res[:ny], res[ny:]


def _tn_full(x, y, name, tn_cols=None):
    lp, kx = x.shape
    n = y.shape[1]
    tk = _tn_tiles(lp)
    tn = n if tn_cols is None else tn_cols
    k4 = kx // N_CHIPS

    def epi(accs, in_refs, out_refs):
        for j in range(N_CHIPS):
            slab = accs[0][j * k4:(j + 1) * k4]
            out_refs[0][j] = slab
            out_refs[1][j] = _sibling_half(slab, k4)

    return _matmul(
        name, (n // tn, lp // tk), 1, [x, y],
        [pl.BlockSpec((tk, kx), lambda j, k: (k, 0)), pl.BlockSpec((tk, tn), lambda j, k: (k, j))],
        [(0, 1, "tn", 0)], [(kx, tn)], epi,
        [jax.ShapeDtypeStruct((N_CHIPS, k4, n), F32), jax.ShapeDtypeStruct((N_CHIPS, k4 // 2, n), BF16)],
        [pl.BlockSpec((N_CHIPS, k4, tn), lambda j, k: (0, 0, j)),
         pl.BlockSpec((N_CHIPS, k4 // 2, tn), lambda j, k: (0, 0, j))],
        ("arbitrary", "arbitrary"))


def _in_proj_bwd(dz, w_in, dh, h_in, gain):
    lp, d = h_in.shape
    inw = w_in.shape[0]
    tm = _row_tile(lp)

    def epi(accs, in_refs, out_refs):
        i = pl.program_id(0)
        dx, dgrow = _rms_bwd_math(accs[0], in_refs[3][...], in_refs[4][...])
        dh_new = in_refs[2][...] + dx
        out_refs[0][...] = dh_new
        out_refs[2][...] = dh_new.astype(BF16)

        @pl.when(i == 0)
        def _():
            out_refs[1][...] = jnp.zeros_like(out_refs[1])

        out_refs[1][...] += jnp.sum(dgrow, axis=0, keepdims=True)

    row = pl.BlockSpec((tm, d), lambda i: (i, 0))
    return _matmul(
        "mix_in_proj_bwd", (lp // tm,), None, [dz, w_in, dh, h_in, gain.reshape(1, d)],
        [pl.BlockSpec((tm, inw), lambda i: (i, 0)), pl.BlockSpec((inw, d), lambda i: (0, 0)), row, row,
         pl.BlockSpec((1, d), lambda i: (0, 0))],
        [(0, 1, "nn", 0)], [(tm, d)], epi,
        [jax.ShapeDtypeStruct((lp, d), F32), jax.ShapeDtypeStruct((1, d), F32), jax.ShapeDtypeStruct((lp, d), BF16)],
        [row, pl.BlockSpec((1, d), lambda i: (0, 0)), row], ("arbitrary",))


def _loss_head(h, gain, target):
    lp, d = h.shape
    nb = lp // BLOCK

    def body(h_ref, g_ref, t_ref, dh_ref, dg_ref, loss_ref, dhb_ref):
        i = pl.program_id(0)

        @pl.when(i == 0)
        def _():
            dg_ref[...] = jnp.zeros_like(dg_ref)
            loss_ref[...] = jnp.zeros_like(loss_ref)
            dh_ref[...] = jnp.zeros_like(dh_ref)
            dhb_ref[...] = jnp.zeros_like(dhb_ref)

        @pl.when(i > 0)
        def _():
            x = h_ref[...]
            g = g_ref[...]
            r = lax.rsqrt(jnp.mean(x * x, axis=-1, keepdims=True) + EPS)
            err = x * r * g - t_ref[...]
            loss_ref[...] += jnp.zeros_like(loss_ref) + 0.5 * jnp.sum(jnp.sum(err * err, axis=-1, keepdims=True)) / d
            dx, dgrow = _rms_bwd_math(err * (1.0 / d), x, g)
            dh_ref[...] = dx
            dhb_ref[...] = dx.astype(BF16)
            dg_ref[...] += jnp.sum(dgrow, axis=0, keepdims=True)

    row = pl.BlockSpec((BLOCK, d), lambda i: (i, 0))
    one = pl.BlockSpec((1, d), lambda i: (0, 0))
    return pl.pallas_call(
        body,
        out_shape=[jax.ShapeDtypeStruct((lp, d), F32), jax.ShapeDtypeStruct((1, d), F32),
                   jax.ShapeDtypeStruct((SUBLANES, LANES), F32), jax.ShapeDtypeStruct((lp, d), BF16)],
        grid=(nb,),
        in_specs=[row, one, pl.BlockSpec((BLOCK, d), lambda i: (jnp.maximum(i - 1, 0), 0))],
        out_specs=[row, one, pl.BlockSpec((SUBLANES, LANES), lambda i: (0, 0)), row],
        compiler_params=_cparams(("arbitrary",)), name="loss_head")(h, gain.reshape(1, d), target)


def _adam_math(w, g, m, v):
    m = ADAM_B1 * m + (1.0 - ADAM_B1) * g
    v = ADAM_B2 * v + (1.0 - ADAM_B2) * (g * g)
    m_hat = m / (1.0 - ADAM_B1 ** ADAM_STEP)
    v_hat = v / (1.0 - ADAM_B2 ** ADAM_STEP)
    delta = -ADAM_LR * (m_hat / (jnp.sqrt(v_hat) + ADAM_EPS) + ADAM_WD * w)
    return delta, m, v


def _adamw_layers(w, m, v, mine, other, pos, name):
    depth, r, c = w.shape
    half = r // 2
    tr = _div_tile(half, c * 4)
    nh = half // tr

    def body(*refs):
        pos_ref, w_ref, m_ref, v_ref = refs[:4]
        mine_refs = refs[4:4 + depth]
        other_refs = refs[4 + depth:4 + 2 * depth]
        g_out, d_out, m_out, v_out = refs[4 + 2 * depth:]
        layer, i = pl.program_id(0), pl.program_id(1)
        is_mine = (i // nh) == pos_ref[0]

        def update(g):
            delta, nm, nv = _adam_math(w_ref[...], g, m_ref[...], v_ref[...])
            g_out[...] = g
            d_out[...] = delta
            m_out[...] = nm
            v_out[...] = nv

        for l in range(depth):
            @pl.when((layer == l) & is_mine)
            def _(l=l):
                update(mine_refs[l][...])

            @pl.when((layer == l) & jnp.logical_not(is_mine))
            def _(l=l):
                update(other_refs[l][...])

    stacked = pl.BlockSpec((None, tr, c), lambda l, i, p: (l, i, 0))

    def gspec(layer, is_other):
        def imap(l, i, p):
            first = jnp.where(is_other, 1 - p[0], p[0]) * nh
            here = jnp.clip(i - first, 0, nh - 1)
            return (jnp.where(l == layer, here, jnp.where(l < layer, 0, nh - 1)), 0)
        return pl.BlockSpec((tr, c), imap)

    shp = jax.ShapeDtypeStruct((depth, r, c), F32)
    grid_spec = pltpu.PrefetchScalarGridSpec(
        num_scalar_prefetch=1, grid=(depth, 2 * nh),
        in_specs=[stacked] * 3 + [gspec(l, 0) for l in range(depth)] + [gspec(l, 1) for l in range(depth)],
        out_specs=[stacked] * 4)
    return pl.pallas_call(
        body, out_shape=[shp] * 4, grid_spec=grid_spec,
        compiler_params=_cparams(("arbitrary", "arbitrary")), name=name)(pos, w, m, v, *mine, *other)


def _adamw_whole(w, g, m, v, name):
    def body(w_ref, g_ref, m_ref, v_ref, d_out, m_out, v_out):
        delta, nm, nv = _adam_math(w_ref[...], g_ref[...], m_ref[...], v_ref[...])
        d_out[...] = delta
        m_out[...] = nm
        v_out[...] = nv

    shp = jax.ShapeDtypeStruct(w.shape, F32)
    return pl.pallas_call(body, out_shape=[shp] * 3, compiler_params=_cparams(), name=name)(w, g, m, v)


def _mesh_pos():
    return lax.axis_index("x"), lax.axis_index("y"), lax.axis_index("c")


def _row_half(ref, which, lead):
    half = ref.shape[lead] // 2
    idx = (slice(None),) * lead + (pl.ds(which * half, half), slice(None))
    return ref.at[idx]


def _gather_comm(arrs, tag):
    n = len(arrs)

    def ctx(ins, outs, sems):
        send_sems, recv_sems, local_sems = sems
        x, y, c = _mesh_pos()
        chips = [(1 - x, y), (x, 1 - y), (1 - x, 1 - y)]

        def slot(k, chip, which):
            lead = len(ins[k].shape) - 2
            return _row_half(outs[k].at[2 * chip[0] + chip[1]], which, lead)

        def copy(k, j, src, dst, to):
            return pltpu.make_async_remote_copy(
                src_ref=src, dst_ref=dst, send_sem=send_sems.at[6 * k + j], recv_sem=recv_sems.at[6 * k + j],
                device_id=to, device_id_type=MESH)

        def local(k):
            return pltpu.make_async_copy(ins[k], outs[k].at[2 * x + y], local_sems.at[k])

        def first(k, j):
            lead = len(ins[k].shape) - 2
            return copy(k, j, _row_half(ins[k], c, lead), slot(k, (x, y), c), (*chips[j], c))

        def passed(k, j, which):
            return copy(k, 3 + j, slot(k, chips[j], which), slot(k, chips[j], which), (x, y, 1 - c))

        def landed(k, j):
            return copy(k, j, slot(k, chips[j], c), slot(k, chips[j], c), (x, y, 1 - c))

        return c, local, first, passed, landed

    def start(ins, outs, sems):
        c, local, first, passed, landed = ctx(ins, outs, sems)
        for k in range(n):
            local(k).start()
            for j in range(3):
                first(k, j).start()

    def mid(ins, outs, sems):
        c, local, first, passed, landed = ctx(ins, outs, sems)
        for j in range(3):
            for k in range(n):
                landed(k, j).wait_recv()
                passed(k, j, c).start()

    def finish(ins, outs, sems):
        c, local, first, passed, landed = ctx(ins, outs, sems)
        for j in range(3):
            for k in range(n):
                passed(k, j, 1 - c).wait_recv()
        for k in range(n):
            for j in range(3):
                first(k, j).wait_send()
                passed(k, j, c).wait_send()
            local(k).wait()

    return _Comm(
        tag, arrs, [jax.ShapeDtypeStruct((N_CHIPS,) + a.shape, a.dtype) for a in arrs],
        [pltpu.SemaphoreType.DMA((6 * n,)), pltpu.SemaphoreType.DMA((6 * n,)), pltpu.SemaphoreType.DMA((n,))],
        start, mid, finish)


def _run_comm(comm, name):
    n_in, n_out = len(comm.ins), len(comm.out_shapes)

    def body(*refs):
        ins, outs, sems = refs[:n_in], refs[n_in:n_in + n_out], refs[n_in + n_out:]
        comm.start(ins, outs, sems)
        if comm.mid is not None:
            comm.mid(ins, outs, sems)
        comm.finish(ins, outs, sems)

    return pl.pallas_call(
        body, out_shape=comm.out_shapes, in_specs=[HBM_SPEC] * n_in, out_specs=[HBM_SPEC] * n_out,
        scratch_shapes=comm.sems, name=name)(*comm.ins)


def _all_gather_chips(arrs, name):
    return _run_comm(_gather_comm(arrs, "gather"), name)


GATHER_US_PER_BYTE = 380.0 / 11.65e6
HOST_US = dict(ffn_up=68.0, ffn_down=37.0, in_proj=38.0, attn_fwd=103.0, ssm_fwd=70.0, merge_fwd=30.0,
               out_proj=23.0)
HOST_SLACK_US = 10.0


class _WeightStream:
    def __init__(self, pieces):
        self.keys = [k for k, _ in pieces]
        self.shards = dict(pieces)
        self.next = 0
        self.full = {}
        self.pending = []

    def comm_for(self, host):
        budget = HOST_US[host] + HOST_SLACK_US
        taken, cost = [], 0.0
        while self.next < len(self.keys):
            key = self.keys[self.next]
            c = self.shards[key].size * self.shards[key].dtype.itemsize * GATHER_US_PER_BYTE
            if cost + c > budget and taken:
                break
            taken.append(key)
            cost += c
            self.next += 1
        self.pending = taken
        if not taken:
            return None
        return _gather_comm([self.shards[k] for k in taken], "g_" + "_".join(k[1] for k in taken))

    def deposit(self, gathered):
        for key, arr in zip(self.pending, gathered):
            self.full[key] = arr
        self.pending = []

    def get(self, key):
        if key not in self.full:
            upto = self.keys.index(key) + 1
            keys = self.keys[self.next:upto]
            self.next = upto
            for k, arr in zip(keys, _all_gather_chips([self.shards[k] for k in keys], "gather_now")):
                self.full[k] = arr
        return self.full[key]


def _all_gather_devices(x_shard, name):
    m_per, ncol = x_shard.shape

    def body(x_ref, out_ref, send_sems, recv_sems, local_sem):
        x, y, c = _mesh_pos()
        me, sibling = (x, y, c), (x, y, 1 - c)
        chips = [(1 - x, y), (x, 1 - y), (1 - x, 1 - y)]

        def rows(px, py, pc):
            return out_ref.at[4 * px + 2 * py + pc]

        def copy(k, block, to, src=None):
            return pltpu.make_async_remote_copy(
                src_ref=rows(*block) if src is None else src, dst_ref=rows(*block),
                send_sem=send_sems.at[k], recv_sem=recv_sems.at[k], device_id=to, device_id_type=MESH)

        mine = pltpu.make_async_copy(x_ref, rows(*me), local_sem)
        mine.start()
        first = [copy(0, me, sibling, src=x_ref)]
        first += [copy(1 + j, me, (*chip, c), src=x_ref) for j, chip in enumerate(chips)]
        for cp in first:
            cp.start()
        passed = [copy(4 + j, (*chip, c), sibling) for j, chip in enumerate(chips)]
        for j, chip in enumerate(chips):
            copy(1 + j, (*chip, c), me).wait_recv()
            passed[j].start()
        copy(0, sibling, me).wait_recv()
        for j, chip in enumerate(chips):
            copy(4 + j, (*chip, 1 - c), me).wait_recv()
        for cp in first + passed:
            cp.wait_send()
        mine.wait()

    return pl.pallas_call(
        body, out_shape=jax.ShapeDtypeStruct((8, m_per, ncol), x_shard.dtype),
        in_specs=[pl.BlockSpec(memory_space=pltpu.VMEM)], out_specs=pl.BlockSpec(memory_space=pltpu.VMEM),
        scratch_shapes=[pltpu.SemaphoreType.DMA((7,)), pltpu.SemaphoreType.DMA((7,)), pltpu.SemaphoreType.DMA],
        compiler_params=pltpu.CompilerParams(vmem_limit_bytes=VMEM_LIMIT), name=name)(x_shard)


def _sum_devices(g8, name):
    _, r, c = g8.shape
    tr = _div_tile(r, c * 4 * 8)

    def body(g_ref, o_ref):
        acc = g_ref[0]
        for dev in range(1, 8):
            acc = acc + g_ref[dev]
        o_ref[...] = acc

    return pl.pallas_call(
        body, out_shape=jax.ShapeDtypeStruct((r, c), F32), grid=(r // tr,),
        in_specs=[pl.BlockSpec((8, tr, c), lambda i: (0, i, 0))], out_specs=pl.BlockSpec((tr, c), lambda i: (i, 0)),
        compiler_params=_cparams(("parallel",)), name=name)(g8)


def _chip_partials(arrs, recvs, pos, name):
    n = len(arrs)

    def body(pos_ref, *refs):
        for a_ref, b_ref, o_ref in zip(refs[:n], refs[n:2 * n], refs[2 * n:]):
            o_ref[...] = (a_ref[...] + b_ref[...]).astype(BF16)

    own_specs, recv_specs, shapes = [], [], []
    for arr in arrs:
        nslab, r, c = arr.shape
        own_specs.append(pl.BlockSpec((None, r // 2, c), lambda j, p: (j, p[0], 0)))
        recv_specs.append(pl.BlockSpec((None, r // 2, c), lambda j, p: (j, 0, 0)))
        shapes.append(jax.ShapeDtypeStruct((nslab, r // 2, c), BF16))
    grid_spec = pltpu.PrefetchScalarGridSpec(
        num_scalar_prefetch=1, grid=(N_CHIPS,), in_specs=own_specs + recv_specs, out_specs=recv_specs)
    return pl.pallas_call(
        body, out_shape=shapes, grid_spec=grid_spec,
        compiler_params=_cparams(("parallel",)), name=name)(pos, *arrs, *recvs)


def _chip_exchange_comm(parts, tag):
    n = len(parts)

    def copies(ins, outs, sems):
        send_sems, recv_sems = sems
        x, y, c = _mesh_pos()
        chips = [(1 - x, y), (x, 1 - y), (1 - x, 1 - y)]
        return [pltpu.make_async_remote_copy(
            src_ref=ins[k].at[2 * chip[0] + chip[1]], dst_ref=outs[k].at[j],
            send_sem=send_sems.at[3 * k + j], recv_sem=recv_sems.at[3 * k + j],
            device_id=(*chip, c), device_id_type=MESH) for k in range(n) for j, chip in enumerate(chips)]

    def start(ins, outs, sems):
        for cp in copies(ins, outs, sems):
            cp.start()

    def finish(ins, outs, sems):
        for cp in copies(ins, outs, sems):
            cp.wait()

    return _Comm(
        tag, parts, [jax.ShapeDtypeStruct((3,) + p.shape[1:], p.dtype) for p in parts],
        [pltpu.SemaphoreType.DMA((3 * n,)), pltpu.SemaphoreType.DMA((3 * n,))], start, None, finish)


def _reduce_halves(arrs, recvs, gots, pos, name):
    n = len(arrs)

    def body(pos_ref, *refs):
        for a_ref, b_ref, g_ref, o_ref in zip(refs[:n], refs[n:2 * n], refs[2 * n:3 * n], refs[3 * n:]):
            acc = a_ref[...] + b_ref[...]
            for j in range(3):
                acc = acc + g_ref[j].astype(F32)
            o_ref[...] = acc

    own_specs, recv_specs, got_specs, out_specs, shapes = [], [], [], [], []
    for arr in arrs:
        _, r, c = arr.shape
        own_specs.append(pl.BlockSpec((None, r // 2, c), lambda i, p: (p[1], p[0], 0)))
        recv_specs.append(pl.BlockSpec((None, r // 2, c), lambda i, p: (p[1], 0, 0)))
        got_specs.append(pl.BlockSpec((3, r // 2, c), lambda i, p: (0, 0, 0)))
        out_specs.append(pl.BlockSpec((r // 2, c), lambda i, p: (0, 0)))
        shapes.append(jax.ShapeDtypeStruct((r // 2, c), F32))
    grid_spec = pltpu.PrefetchScalarGridSpec(
        num_scalar_prefetch=1, grid=(1,), in_specs=own_specs + recv_specs + got_specs, out_specs=out_specs)
    return pl.pallas_call(
        body, out_shape=shapes, grid_spec=grid_spec,
        compiler_params=_cparams(("arbitrary",)), name=name)(pos, *arrs, *recvs, *gots)


def _share_halves(halves, name):
    n = len(halves)

    def body(*refs):
        ins, outs = refs[:n], refs[n:2 * n]
        send_sems, recv_sems = refs[2 * n:]
        x, y, c = _mesh_pos()
        cps = []
        for k in range(n):
            cp = pltpu.make_async_remote_copy(
                src_ref=ins[k], dst_ref=outs[k], send_sem=send_sems.at[k], recv_sem=recv_sems.at[k],
                device_id=(x, y, 1 - c), device_id_type=MESH)
            cp.start()
            cps.append(cp)
        for cp in cps:
            cp.wait()

    return pl.pallas_call(
        body, out_shape=[jax.ShapeDtypeStruct(h.shape, h.dtype) for h in halves],
        in_specs=[HBM_SPEC] * n, out_specs=[HBM_SPEC] * n,
        scratch_shapes=[pltpu.SemaphoreType.DMA((n,)), pltpu.SemaphoreType.DMA((n,))], name=name)(*halves)


class _Reduction:
    def __init__(self, arrs, others, pos, tag):
        self.arrs, self.pos, self.tag = arrs, pos, tag
        self.recv = _share_halves(others, "rs_sibling_" + tag)
        self.parts = _chip_partials(arrs, self.recv, pos, "rs_partial_" + tag)
        self.got = None

    def comm(self):
        return _chip_exchange_comm(self.parts, "rs_" + self.tag)

    def end(self):
        if self.got is None:
            self.got = _run_comm(self.comm(), "rs_chips_" + self.tag)
        halves = _reduce_halves(self.arrs, self.recv, self.got, self.pos, "rs_reduce_" + self.tag)
        return halves, _share_halves(halves, "rs_share_" + self.tag)


def _w_in_full(p, l, ws):
    slabs = ws.get((l, "w_in"))
    return slabs.reshape(-1, slabs.shape[2])


def _w3t_full(p, l, ws):
    if "w3t" not in p:
        slabs = ws.get((l, "w3"))
        p["w3t"] = jnp.swapaxes(slabs, 0, 1).reshape(slabs.shape[1], -1, slabs.shape[3])
    return p["w3t"]


def _w_out_full(l, ws):
    slabs = ws.get((l, "w_out"))
    return slabs.reshape(-1, slabs.shape[2])


def _layer_fwd(h, n0, l, p, next_gain, ws, tabs):
    def hosted(host, fn, *args):
        out, got = fn(*args, ws.comm_for(host))
        ws.deposit(got)
        return out

    ffn1_saved = hosted("ffn_up", _ffn_up, n0, ws.get((l, "wg1")), ws.get((l, "wu1")))
    h1, n = hosted("ffn_down", _ffn_down, ffn1_saved[2], ws.get((l, "wd1")), h, p["mix_norm"])
    ssm_w = p["ssm_d"].shape[0]
    q, k, v, u, ga, gs = hosted("in_proj", _in_proj, n, _w_in_full(p, l, ws), tabs, ssm_w)
    o = hosted("attn_fwd", _attn_fwd, q, k, v, p["attn_sinks"])
    y, yg = hosted("ssm_fwd", _ssm_fwd, u, *p["ssm_tabs"], p["ssm_d"])
    merged = hosted("merge_fwd", _merge_fwd, o, yg, ga, gs, _w3t_full(p, l, ws))
    h2, n2 = hosted("out_proj", _out_proj, merged, _w_out_full(l, ws), h1, p["ffn2_norm"])
    ffn2_saved = hosted("ffn_up", _ffn_up, n2, ws.get((l, "wg2")), ws.get((l, "wu2")))
    h3, *n3 = hosted("ffn_down", _ffn_down, ffn2_saved[2], ws.get((l, "wd2")), h2, next_gain)
    saved = dict(h0=h, h1=h1, h2=h2, ffn1=ffn1_saved, ffn2=ffn2_saved, n_mix=n, q=q, k=k, v=v, u=u, ga=ga, gs=gs,
                 o=o, y=y, yg=yg, merged=merged)
    return h3, (n3[0] if n3 else None), saved


def _layer_bwd(dh_pair, l, p, ws, s, tabs, pos):
    g = {}
    (dh2, dhb), g["ffn2_norm"], red_ffn2, _ = _ffn_bwd(
        dh_pair, s["h2"], p["ffn2_norm"], ws.get((l, "wg2")), ws.get((l, "wu2")), ws.get((l, "wd2")), p["f4"],
        s["ffn2"], pos)
    w3, w_out_w = _w3t_full(p, l, ws), _w_out_full(l, ws)
    lp, d = dh2.shape
    d4 = d // N_CHIPS
    dw_out, dw_out_other = _tn_full(s["merged"], dhb, "mix_dw_out")
    dattn, dv, dg, dga, dgs = _merge_bwd(dhb, w_out_w, s["o"], s["yg"], s["ga"], s["gs"], w3)
    (dw_ap,), (dw_ap_other,) = _tn_cols(s["o"], [dattn], "mix_dw_ap")
    (dw_gv, dw_gg), (dw_gv_other, dw_gg_other) = _tn_cols(s["yg"], [dv, dg], "mix_dw_glu")
    do, dy = _branch_bwd(dattn, dv, dg, w3, s["y"])
    (dq, dk, dvv, dkm, dvm, dsink), _ = _attn_bwd(s["q"], s["k"], s["v"], do, p["attn_sinks"], tabs)
    g["attn_sinks"] = dsink[:, 0]
    (du, dlr, dli, dbr, dbi, dcr, dci, dd), _ = _ssm_bwd(s["u"], dy, *p["ssm_tabs"], p["ssm_d"])
    ngrp = p["ssm_d"].shape[0] // SSM_GROUP
    g["ssm_lam"] = (dlr.reshape(ngrp, SSM_STATE), dli.reshape(ngrp, SSM_STATE),
                    _ssm_untable_b(dbr, ngrp), _ssm_untable_b(dbi, ngrp))
    g["ssm_c_re"] = _ssm_untable_c(dcr, ngrp)
    g["ssm_c_im"] = _ssm_untable_c(dci, ngrp)
    g["ssm_d"] = dd[0]
    dk = dk.at[:BLOCK].add(dkm)
    dvv = dvv.at[:BLOCK].add(dvm)
    dz = jnp.concatenate([dq.astype(BF16), dk.astype(BF16), dvv.astype(BF16), du.astype(BF16), dga, dgs], axis=1)
    n = s["n_mix"]
    w_in = _w_in_full(p, l, ws)
    dw_in, dw_in_other = _tn_full(dz, n, "mix_dw_in", d // 2)
    red_mix = _Reduction([dw_in, dw_ap, dw_gv, dw_gg, dw_out],
                         [dw_in_other, dw_ap_other, dw_gv_other, dw_gg_other, dw_out_other], pos, "mix")
    dh1, g["mix_norm"], dh1b = _in_proj_bwd(dz, w_in, dh2, s["h1"], p["mix_norm"])
    dh0_pair, g["ffn1_norm"], red_ffn1, red_mix.got = _ffn_bwd(
        (dh1, dh1b), s["h0"], p["ffn1_norm"], ws.get((l, "wg1")), ws.get((l, "wu1")), ws.get((l, "wd1")), p["f4"],
        s["ffn1"], pos, red_mix.comm())
    return dh0_pair, g, [*red_ffn1, red_mix, *red_ffn2]


BIG = ["ffn1_w_gate", "ffn1_w_up", "ffn1_w_down", "w_in", "w_attn_proj", "w_glu_v", "w_glu_g", "w_out",
       "ffn2_w_gate", "ffn2_w_up", "ffn2_w_down"]
TRANSPOSED = ["ffn1_w_gate", "ffn1_w_up", "w_in", "ffn2_w_gate", "ffn2_w_up"]
SMALL = ["ffn1_norm", "mix_norm", "attn_sinks", "ssm_a_re", "ssm_a_im", "ssm_log_dt", "ssm_b_re", "ssm_b_im",
         "ssm_c_re", "ssm_c_im", "ssm_d", "ffn2_norm", "final_norm"]
WEIGHTS = ["meta_tokens", "ffn1_norm", "ffn1_w_gate", "ffn1_w_up", "ffn1_w_down", "mix_norm", "w_in", "attn_sinks",
           "ssm_a_re", "ssm_a_im", "ssm_log_dt", "ssm_b_re", "ssm_b_im", "ssm_c_re", "ssm_c_im", "ssm_d",
           "w_attn_proj", "w_glu_v", "w_glu_g", "w_out", "ffn2_norm", "ffn2_w_gate", "ffn2_w_up", "ffn2_w_down",
           "final_norm"]


def _small_rows(shape):
    rows = -(-math.prod(shape) // LANES)
    return -(-rows // SUBLANES) * SUBLANES


def _pack_small(tree):
    parts = []
    for k in SMALL + ["meta_tokens"]:
        size, rows = math.prod(tree[k].shape), _small_rows(tree[k].shape)
        if size % LANES == 0:
            part = tree[k].reshape(size // LANES, LANES)
        else:
            part = jnp.pad(tree[k].reshape(1, size), ((0, 0), (0, LANES - size)))
        parts.append(jnp.pad(part, ((0, rows - part.shape[0]), (0, 0))))
    return jnp.concatenate(parts, axis=0)


def _unpack_small(packed, like):
    out, off = {}, 0
    for k in SMALL + ["meta_tokens"]:
        size, rows = math.prod(like[k].shape), _small_rows(like[k].shape)
        if size % LANES == 0:
            out[k] = packed[off:off + size // LANES].reshape(like[k].shape)
        else:
            out[k] = packed[off, :size].reshape(like[k].shape)
        off += rows
    return out


def kernel(x, meta_tokens, ffn1_norm, ffn1_w_gate, ffn1_w_up, ffn1_w_down, mix_norm, w_in, attn_sinks, ssm_a_re, ssm_a_im, ssm_log_dt, ssm_b_re, ssm_b_im, ssm_c_re, ssm_c_im, ssm_d, w_attn_proj, w_glu_v, w_glu_g, w_out, ffn2_norm, ffn2_w_gate, ffn2_w_up, ffn2_w_down, final_norm, loss_target, m_meta_tokens, m_ffn1_norm, m_ffn1_w_gate, m_ffn1_w_up, m_ffn1_w_down, m_mix_norm, m_w_in, m_attn_sinks, m_ssm_a_re, m_ssm_a_im, m_ssm_log_dt, m_ssm_b_re, m_ssm_b_im, m_ssm_c_re, m_ssm_c_im, m_ssm_d, m_w_attn_proj, m_w_glu_v, m_w_glu_g, m_w_out, m_ffn2_norm, m_ffn2_w_gate, m_ffn2_w_up, m_ffn2_w_down, m_final_norm, v_meta_tokens, v_ffn1_norm, v_ffn1_w_gate, v_ffn1_w_up, v_ffn1_w_down, v_mix_norm, v_w_in, v_attn_sinks, v_ssm_a_re, v_ssm_a_im, v_ssm_log_dt, v_ssm_b_re, v_ssm_b_im, v_ssm_c_re, v_ssm_c_im, v_ssm_d, v_w_attn_proj, v_w_glu_v, v_w_glu_g, v_w_out, v_ffn2_norm, v_ffn2_w_gate, v_ffn2_w_up, v_ffn2_w_down, v_final_norm):
    args = dict(locals())
    w = {k: args[k] for k in WEIGHTS}
    m = {k: args["m_" + k] for k in WEIGHTS}
    v = {k: args["v_" + k] for k in WEIGHTS}
    depth = ffn1_norm.shape[0]
    seq, d = x.shape[1], x.shape[2]
    lp = seq + BLOCK
    xi, yi, ci = _mesh_pos()
    pos = jnp.stack([ci, 2 * xi + yi]).astype(jnp.int32)

    tabs = _rope_tables(lp)
    layers, pieces = [], [((0, "meta"), meta_tokens)]
    f4 = ffn1_w_gate.shape[2]
    fp = -(-f4 // MXU_DIM) * MXU_DIM

    def ffn_rows(wt):
        return jnp.pad(wt, ((0, fp - f4), (0, 0))).astype(BF16)

    for l in range(depth):
        pieces += [
            ((l, "wg1"), ffn_rows(ffn1_w_gate[l].T)), ((l, "wu1"), ffn_rows(ffn1_w_up[l].T)),
            ((l, "wd1"), ffn_rows(ffn1_w_down[l])), ((l, "w_in"), w_in[l].T.astype(BF16)),
            ((l, "w3"), jnp.stack([w_attn_proj[l].T, w_glu_v[l].T, w_glu_g[l].T]).astype(BF16)),
            ((l, "w_out"), w_out[l].astype(BF16)),
            ((l, "wg2"), ffn_rows(ffn2_w_gate[l].T)), ((l, "wu2"), ffn_rows(ffn2_w_up[l].T)),
            ((l, "wd2"), ffn_rows(ffn2_w_down[l]))]
        lb_re, lb_im, bb_re, bb_im = _ssm_params(ssm_a_re[l], ssm_a_im[l], ssm_log_dt[l], ssm_b_re[l], ssm_b_im[l])
        ngrp = lb_re.shape[0]
        nt = ngrp // GROUPS_PER_TILE
        ssm_tabs = (lb_re.reshape(nt, 1, TILE_STATES), lb_im.reshape(nt, 1, TILE_STATES),
                    *_ssm_tables(bb_re, bb_im, ssm_c_re[l], ssm_c_im[l]))
        layers.append(dict(
            ffn1_norm=ffn1_norm[l], mix_norm=mix_norm[l], ffn2_norm=ffn2_norm[l], attn_sinks=attn_sinks[l],
            ssm_d=ssm_d[l], ssm_tabs=ssm_tabs, f4=f4))
    ws = _WeightStream(pieces)
    ws.get((0, "wu1"))
    meta_all = ws.get((0, "meta"))
    meta_full = jnp.concatenate([meta_all[j] for j in range(N_CHIPS)], axis=1)

    h = jnp.concatenate([jnp.zeros((PAD_FRONT, d), F32), meta_full, x[0]], axis=0)
    saved = []
    n0 = _rms_fwd(h, ffn1_norm[0], "rms_fwd_first")
    for l in range(depth):
        next_gain = ffn1_norm[l + 1] if l + 1 < depth else None
        h, n0, s = _layer_fwd(h, n0, l, layers[l], next_gain, ws, tabs)
        saved.append(s)
    dh, g_final, loss_acc, dhb = _loss_head(h, final_norm, loss_target[0])
    dh_pair = (dh, dhb)
    loss = lax.psum(loss_acc[0, 0], ("x", "y", "c"))

    grads, reds = [None] * depth, [None] * depth
    for l in reversed(range(depth)):
        dh_pair, grads[l], reds[l] = _layer_bwd(dh_pair, l, layers[l], ws, saved[l], tabs, pos)
    dh = dh_pair[0]
    grad_x = dh[BLOCK:][None]
    dmeta_local = dh[PAD_FRONT:BLOCK]

    small = {k: [] for k in SMALL}
    for l in range(depth):
        gl = grads[l]
        _, vjp = jax.vjp(_ssm_params, ssm_a_re[l], ssm_a_im[l], ssm_log_dt[l], ssm_b_re[l], ssm_b_im[l])
        da_re, da_im, dlog_dt, db_re, db_im = vjp(gl["ssm_lam"])
        for k, val in (("ffn1_norm", gl["ffn1_norm"][0]), ("mix_norm", gl["mix_norm"][0]),
                       ("attn_sinks", gl["attn_sinks"]), ("ssm_a_re", da_re), ("ssm_a_im", da_im),
                       ("ssm_log_dt", dlog_dt), ("ssm_b_re", db_re), ("ssm_b_im", db_im),
                       ("ssm_c_re", gl["ssm_c_re"]), ("ssm_c_im", gl["ssm_c_im"]), ("ssm_d", gl["ssm_d"]),
                       ("ffn2_norm", gl["ffn2_norm"][0])):
            small[k].append(val)
    small_local = {k: jnp.stack(vals) for k, vals in small.items() if k != "final_norm"}
    small_local["final_norm"] = g_final[0]
    small_local["meta_tokens"] = dmeta_local
    like = dict(small_local)
    g_small = _sum_devices(_all_gather_devices(_pack_small(small_local), "gather_small_grads"), "sum_small_grads")
    g_small_tree = _unpack_small(g_small, like)
    d4 = d // N_CHIPS
    chip = 2 * xi + yi
    g_meta = lax.dynamic_slice_in_dim(g_small_tree["meta_tokens"], chip * d4, d4, axis=1)

    reduced = []
    for l in range(depth):
        mine, other = [], []
        for red in reds[l]:
            halves, sibling_halves = red.end()
            mine += halves
            other += sibling_halves
        reduced.append((mine, other))

    g_out, delta, new_m, new_v = {}, {}, {}, {}
    for i, k in enumerate(BIG):
        flip = (lambda t: jnp.swapaxes(t, 1, 2)) if k in TRANSPOSED else (lambda t: t)
        outs = _adamw_layers(
            flip(w[k]), flip(m[k]), flip(v[k]), [reduced[l][0][i] for l in range(depth)],
            [reduced[l][1][i] for l in range(depth)], pos, "adamw_" + k)
        g_out[k], delta[k], new_m[k], new_v[k] = [flip(t) for t in outs]
    g_small_tree["meta_tokens"] = g_meta
    for k in SMALL + ["meta_tokens"]:
        shape = w[k].shape if w[k].ndim > 1 else (1,) + w[k].shape
        outs = _adamw_whole(w[k].reshape(shape), g_small_tree[k].reshape(shape), m[k].reshape(shape),
                            v[k].reshape(shape), "adamw_" + k)
        g_out[k] = g_small_tree[k]
        delta[k], new_m[k], new_v[k] = [t.reshape(w[k].shape) for t in outs]

    return (loss, grad_x, *[g_out[k] for k in WEIGHTS], *[delta[k] for k in WEIGHTS],
            *[new_m[k] for k in WEIGHTS], *[new_v[k] for k in WEIGHTS])
```

```python
import functools
import math

import jax
import jax.numpy as jnp
from jax import lax
from jax.experimental import pallas as pl
from jax.experimental.pallas import tpu as pltpu

F32 = jnp.float32
BF16 = jnp.bfloat16

N_META = 16
HEAD_DIM = 64
N_Q_HEADS = 8
N_KV_HEADS = 2
Q_PER_KV = N_Q_HEADS // N_KV_HEADS
ATTN_WIDTH = N_Q_HEADS * HEAD_DIM
KV_WIDTH = N_KV_HEADS * HEAD_DIM
BLOCK = 128
PAD_FRONT = BLOCK - N_META
ROPE_THETA = 500000.0
ROT_DIM = HEAD_DIM // 4
SSM_GROUP = 16
SSM_STATE = 64
GROUPS_PER_TILE = 4
TILE_STATES = GROUPS_PER_TILE * SSM_STATE
LANES = 128
SUBLANES = 8
MXU_DIM = 256
EPS = 1e-6
NEG_INF = -1e30
N_CHIPS = 4

ADAM_LR = 0.001
ADAM_B1 = 0.9
ADAM_B2 = 0.999
ADAM_EPS = 1e-08
ADAM_WD = 0.01
ADAM_STEP = 10

VMEM_LIMIT = 56 * 1024 * 1024
MESH = pl.DeviceIdType.MESH


def _cparams(sem=None):
    return pltpu.CompilerParams(dimension_semantics=sem, vmem_limit_bytes=VMEM_LIMIT)


def _row_tile(rows, limit=512):
    best = None
    for t in range(128, limit + 1, 128):
        if rows % t == 0:
            best = t
    assert best is not None, rows
    return best


def _div_tile(rows, row_bytes, max_bytes=1 << 20, mult=8):
    best = None
    for t in range(mult, rows + 1, mult):
        if rows % t == 0 and t * row_bytes <= max_bytes:
            best = t
    if best is None:
        best = rows
    return best


def _dot(a, b, mode):
    if mode == "nn":
        dims = (((1,), (0,)), ((), ()))
    elif mode == "nt":
        dims = (((1,), (1,)), ((), ()))
    else:
        dims = (((0,), (0,)), ((), ()))
    return lax.dot_general(a.astype(BF16), b.astype(BF16), dims, preferred_element_type=F32)


def _sigmoid(x):
    return 1.0 / (1.0 + jnp.exp(-x))


_GELU_C = math.sqrt(2.0 / math.pi)


def _gelu(x):
    return 0.5 * x * (1.0 + jnp.tanh(_GELU_C * (x + 0.044715 * x * x * x)))


def _gelu_grad(x):
    t = jnp.tanh(_GELU_C * (x + 0.044715 * x * x * x))
    return 0.5 * (1.0 + t) + 0.5 * x * (1.0 - t * t) * _GELU_C * (1.0 + 3.0 * 0.044715 * x * x)


class _Comm:
    def __init__(self, tag, ins, out_shapes, sems, start, mid, finish):
        self.tag, self.ins, self.out_shapes, self.sems = tag, list(ins), list(out_shapes), list(sems)
        self.start, self.mid, self.finish = start, mid, finish


HBM_SPEC = pl.BlockSpec(memory_space=pltpu.HBM)
MID_NUM, MID_DEN = 4, 5


def _hosted_call(body, comm, *, out_shape, grid, in_specs, out_specs, scratch_shapes, sem, name, args):
    out_shape, in_specs, out_specs = list(out_shape), list(in_specs), list(out_specs)
    scratch_shapes = list(scratch_shapes)
    if comm is None:
        res = pl.pallas_call(
            body, out_shape=out_shape, grid=grid, in_specs=in_specs, out_specs=out_specs,
            scratch_shapes=scratch_shapes, compiler_params=_cparams(sem), name=name)(*args)
        return list(res), []
    n_in, n_out, n_sc = len(args), len(out_shape), len(scratch_shapes)
    nci, nco = len(comm.ins), len(comm.out_shapes)
    total = math.prod(grid)

    def wrapped(*refs):
        in_refs, cin = refs[:n_in], refs[n_in:n_in + nci]
        o0 = n_in + nci
        out_refs, cout = refs[o0:o0 + n_out], refs[o0 + n_out:o0 + n_out + nco]
        s0 = o0 + n_out + nco
        sc, csem = refs[s0:s0 + n_sc], refs[s0 + n_sc:]
        lin = 0
        for dim, size in enumerate(grid):
            lin = lin * size + pl.program_id(dim)

        @pl.when(lin == 0)
        def _():
            comm.start(cin, cout, csem)

        if comm.mid is not None:
            @pl.when(lin == (total * MID_NUM) // MID_DEN)
            def _():
                comm.mid(cin, cout, csem)

        body(*in_refs, *out_refs, *sc)

        @pl.when(lin == total - 1)
        def _():
            comm.finish(cin, cout, csem)

    res = pl.pallas_call(
        wrapped, out_shape=out_shape + comm.out_shapes, grid=grid,
        in_specs=in_specs + [HBM_SPEC] * nci, out_specs=out_specs + [HBM_SPEC] * nco,
        scratch_shapes=scratch_shapes + comm.sems,
        compiler_params=_cparams(("arbitrary",) * len(grid)), name=name + "_" + comm.tag)(*args, *comm.ins)
    return list(res[:n_out]), list(res[n_out:])


def _matmul(name, grid, k_axis, ins, in_specs, pairs, acc_shapes, epilogue, out_shapes, out_specs, sem, comm=None):
    n_in, n_out, n_acc = len(ins), len(out_shapes), len(acc_shapes)

    def body(*refs):
        in_refs = refs[:n_in]
        out_refs = refs[n_in:n_in + n_out]
        acc_refs = refs[n_in + n_out:]
        if k_axis is None:
            accs = [None] * n_acc
            for ia, ib, mode, iacc in pairs:
                d = _dot(in_refs[ia][...], in_refs[ib][...], mode)
                accs[iacc] = d if accs[iacc] is None else accs[iacc] + d
            epilogue(accs, in_refs, out_refs)
            return
        k = pl.program_id(k_axis)

        @pl.when(k == 0)
        def _():
            for r in acc_refs:
                r[...] = jnp.zeros_like(r)

        for ia, ib, mode, iacc in pairs:
            acc_refs[iacc][...] += _dot(in_refs[ia][...], in_refs[ib][...], mode)

        @pl.when(k == pl.num_programs(k_axis) - 1)
        def _():
            epilogue([r[...] for r in acc_refs], in_refs, out_refs)

    scratch = [] if k_axis is None else [pltpu.VMEM(s, F32) for s in acc_shapes]
    outs, couts = _hosted_call(
        body, comm, out_shape=out_shapes, grid=grid, in_specs=in_specs, out_specs=out_specs,
        scratch_shapes=scratch, sem=sem, name=name, args=ins)
    return outs if comm is None else (outs, couts)


def _rms_math(x, g):
    r = lax.rsqrt(jnp.mean(x * x, axis=-1, keepdims=True) + EPS)
    return (x * r * g).astype(BF16)


def _rms_fwd(h, g, name):
    lp, d = h.shape
    tm = _row_tile(lp)

    def body(h_ref, g_ref, n_ref):
        n_ref[...] = _rms_math(h_ref[...], g_ref[...])

    return pl.pallas_call(
        body, out_shape=jax.ShapeDtypeStruct((lp, d), BF16), grid=(lp // tm,),
        in_specs=[pl.BlockSpec((tm, d), lambda i: (i, 0)), pl.BlockSpec((1, d), lambda i: (0, 0))],
        out_specs=pl.BlockSpec((tm, d), lambda i: (i, 0)),
        compiler_params=_cparams(("parallel",)), name=name)(h, g.reshape(1, d))


def _rms_bwd_math(dn, x, g):
    r = lax.rsqrt(jnp.mean(x * x, axis=-1, keepdims=True) + EPS)
    xh = x * r
    dxh = dn * g
    dx = r * (dxh - xh * jnp.mean(dxh * xh, axis=-1, keepdims=True))
    return dx, dn * xh


def _ffn_up(n, wgt, wut, comm=None):
    lp, d = n.shape
    fp = wgt.shape[1]
    tm = _row_tile(lp)

    def up_body(n_ref, wg_ref, wu_ref, a_ref, b_ref, s_ref):
        x = n_ref[...]
        for jc in range(N_CHIPS):
            cols = slice(jc * fp, (jc + 1) * fp)
            a = _dot(x, wg_ref[jc], "nt")
            b = _dot(x, wu_ref[jc], "nt")
            a_ref[:, cols] = a.astype(BF16)
            b_ref[:, cols] = b.astype(BF16)
            s_ref[:, cols] = (a * _sigmoid(a) * b).astype(BF16)

    ff = N_CHIPS * fp
    act = jax.ShapeDtypeStruct((lp, ff), BF16)
    act_tile = pl.BlockSpec((tm, ff), lambda i: (i, 0))
    w_spec = pl.BlockSpec((N_CHIPS, fp, d), lambda i: (0, 0, 0))
    outs, couts = _hosted_call(
        up_body, comm, out_shape=[act, act, act], grid=(lp // tm,),
        in_specs=[pl.BlockSpec((tm, d), lambda i: (i, 0)), w_spec, w_spec],
        out_specs=[act_tile] * 3, scratch_shapes=[], sem=("parallel",), name="ffn_up", args=(n, wgt, wut))
    return (*outs, n), couts


def _residual_outputs(h_new, in_refs, out_refs, gain_at):
    out_refs[0][...] = h_new
    if gain_at is not None:
        out_refs[1][...] = _rms_math(h_new, in_refs[gain_at][...])


def _residual_specs(lp, d, tm, next_gain):
    row = pl.BlockSpec((tm, d), lambda i: (i, 0))
    shapes, specs = [jax.ShapeDtypeStruct((lp, d), F32)], [row]
    extra_in, extra_specs = [], []
    if next_gain is not None:
        shapes.append(jax.ShapeDtypeStruct((lp, d), BF16))
        specs.append(row)
        extra_in, extra_specs = [next_gain.reshape(1, d)], [pl.BlockSpec((1, d), lambda i: (0, 0))]
    return shapes, specs, extra_in, extra_specs


def _ffn_down(s, wd, h, next_gain, comm=None):
    lp, d = h.shape
    ff = s.shape[1]
    tm = _row_tile(lp)
    shapes, specs, extra_in, extra_specs = _residual_specs(lp, d, tm, next_gain)

    def down_epi(accs, in_refs, out_refs):
        _residual_outputs(in_refs[2][...] + 0.5 * accs[0], in_refs, out_refs, 3 if extra_in else None)

    res = _matmul(
        "ffn_down", (lp // tm,), None, [s, wd.reshape(ff, d), h] + extra_in,
        [pl.BlockSpec((tm, ff), lambda i: (i, 0)), pl.BlockSpec((ff, d), lambda i: (0, 0)),
         pl.BlockSpec((tm, d), lambda i: (i, 0))] + extra_specs,
        [(0, 1, "nn", 0)], [(tm, d)], down_epi, shapes, specs, ("parallel",), comm)
    return (res, []) if comm is None else res


def _tn_tiles(lp):
    return _row_tile(lp, 1408)


def _ffn_bwd(dh_pair, h_in, gain, wgt, wut, wd, f4, saved, pos, comm=None):
    dh, dhb = dh_pair
    a, b, s, n = saved
    lp, d = h_in.shape
    fp = wgt.shape[1]
    ff = N_CHIPS * fp
    tm = _row_tile(lp)
    ni = lp // tm
    tk = _tn_tiles(lp)
    nk = lp // tk

    def ds_body(dh_ref, wd_ref, a_ref, b_ref, da_ref, db_ref):
        x = dh_ref[...]
        for jc in range(N_CHIPS):
            cols = slice(jc * fp, (jc + 1) * fp)
            ds = 0.5 * _dot(x, wd_ref[jc], "nt")
            av = a_ref[:, cols].astype(F32)
            bv = b_ref[:, cols].astype(F32)
            sg = _sigmoid(av)
            da_ref[:, cols] = (ds * bv * sg * (1.0 + av * (1.0 - sg))).astype(BF16)
            db_ref[:, cols] = (ds * av * sg).astype(BF16)

    act = jax.ShapeDtypeStruct((lp, ff), BF16)
    act_tile = pl.BlockSpec((tm, ff), lambda i: (i, 0))
    (da, db), couts = _hosted_call(
        ds_body, comm, out_shape=[act, act], grid=(ni,),
        in_specs=[pl.BlockSpec((tm, d), lambda i: (i, 0)), pl.BlockSpec((N_CHIPS, fp, d), lambda i: (0, 0, 0)),
                  act_tile, act_tile],
        out_specs=[act_tile, act_tile], scratch_shapes=[], sem=("parallel",), name="ffn_bwd_ds",
        args=(dhb, wd, a, b))

    dw_shape = jax.ShapeDtypeStruct((N_CHIPS, f4, d), F32)
    dw_spec = pl.BlockSpec((None, f4, d), lambda j, k: (j, 0, 0))
    in_col = pl.BlockSpec((tk, fp), lambda j, k: (k, j))
    in_row = pl.BlockSpec((tk, d), lambda j, k: (k, 0))

    half_shape = jax.ShapeDtypeStruct((N_CHIPS, f4 // 2, d), BF16)
    half_spec = pl.BlockSpec((None, f4 // 2, d), lambda j, k: (j, 0, 0))

    def dwd_epi(accs, in_refs, out_refs):
        dw = 0.5 * accs[0]
        out_refs[0][...] = dw[:f4]
        out_refs[1][...] = _sibling_half(dw, f4)

    dwd, dwd_other = _matmul(
        "ffn_dwd", (N_CHIPS, nk), 1, [s, dhb], [in_col, in_row],
        [(0, 1, "tn", 0)], [(fp, d)], dwd_epi, [dw_shape, half_shape], [dw_spec, half_spec],
        ("arbitrary", "arbitrary"))

    def dwgu_epi(accs, in_refs, out_refs):
        for i, acc in enumerate(accs):
            out_refs[i][...] = acc[:f4]
            out_refs[2 + i][...] = _sibling_half(acc, f4)

    red_down = _Reduction([dwd], [dwd_other], pos, "ffn_d")
    (dwg, dwu, dwg_other, dwu_other), red_down.got = _matmul(
        "ffn_dwgu", (N_CHIPS, nk), 1, [n, da, db], [in_row, in_col, in_col],
        [(1, 0, "tn", 0), (2, 0, "tn", 1)], [(fp, d)] * 2, dwgu_epi,
        [dw_shape, dw_shape, half_shape, half_shape], [dw_spec, dw_spec, half_spec, half_spec],
        ("arbitrary", "arbitrary"), red_down.comm())

    def dn_epi(accs, in_refs, out_refs):
        i = pl.program_id(0)
        dx, dgrow = _rms_bwd_math(accs[0], in_refs[5][...], in_refs[6][...])
        dh_new = in_refs[4][...] + dx
        out_refs[0][...] = dh_new
        out_refs[2][...] = dh_new.astype(BF16)

        @pl.when(i == 0)
        def _():
            out_refs[1][...] = jnp.zeros_like(out_refs[1])

        out_refs[1][...] += jnp.sum(dgrow, axis=0, keepdims=True)

    red = _Reduction([dwg, dwu], [dwg_other, dwu_other], pos, "ffn_gu")
    row_spec = pl.BlockSpec((tm, d), lambda i: (i, 0))
    act_spec = pl.BlockSpec((tm, ff), lambda i: (i, 0))
    w_spec = pl.BlockSpec((ff, d), lambda i: (0, 0))
    one_spec = pl.BlockSpec((1, d), lambda i: (0, 0))
    (dh_in, dgain, dh_in_b), red.got = _matmul(
        "ffn_bwd_dn", (ni,), None, [da, wgt.reshape(ff, d), db, wut.reshape(ff, d), dh, h_in, gain.reshape(1, d)],
        [act_spec, w_spec, act_spec, w_spec, row_spec, row_spec, one_spec],
        [(0, 1, "nn", 0), (2, 3, "nn", 0)], [(tm, d)], dn_epi,
        [jax.ShapeDtypeStruct((lp, d), F32), jax.ShapeDtypeStruct((1, d), F32), jax.ShapeDtypeStruct((lp, d), BF16)],
        [row_spec, one_spec, row_spec], ("arbitrary",), red.comm())
    return (dh_in, dh_in_b), dgain, [red, red_down], couts


def _rope_tables(lp):
    pos = jnp.arange(lp, dtype=F32) - float(PAD_FRONT)
    inv_freq = ROPE_THETA ** (-jnp.arange(0, ROT_DIM, 2, dtype=F32) / ROT_DIM)
    ang = pos[:, None] * inv_freq[None, :]
    cos, sin = jnp.cos(ang), jnp.sin(ang)
    half = ROT_DIM // 2
    ones = jnp.ones((lp, HEAD_DIM - ROT_DIM), F32)
    zeros_h = jnp.zeros((lp, half), F32)
    zeros_r = jnp.zeros((lp, HEAD_DIM - ROT_DIM), F32)
    c = jnp.concatenate([cos, cos, ones], axis=1)
    s1 = jnp.concatenate([-sin, zeros_h, zeros_r], axis=1)
    s2 = jnp.concatenate([zeros_h, sin, zeros_r], axis=1)
    reps = LANES // HEAD_DIM
    return jnp.stack([jnp.tile(c, (1, reps)), jnp.tile(s1, (1, reps)), jnp.tile(s2, (1, reps))])


def _rope(x, c, s1, s2):
    half = ROT_DIM // 2
    outs = []
    for ch in range(x.shape[1] // LANES):
        xc = x[:, ch * LANES:(ch + 1) * LANES]
        outs.append(xc * c + pltpu.roll(xc, LANES - half, 1) * s1 + pltpu.roll(xc, half, 1) * s2)
    return outs[0] if len(outs) == 1 else jnp.concatenate(outs, axis=1)


def _rope_t(dy, c, s1, s2):
    half = ROT_DIM // 2
    outs = []
    for ch in range(dy.shape[1] // LANES):
        dc = dy[:, ch * LANES:(ch + 1) * LANES]
        outs.append(dc * c + pltpu.roll(dc * s1, half, 1) + pltpu.roll(dc * s2, LANES - half, 1))
    return outs[0] if len(outs) == 1 else jnp.concatenate(outs, axis=1)


def _in_proj(n, w_in, tabs, ssm_w, comm=None):
    lp, d = n.shape
    inw = w_in.shape[0]
    tm = _row_tile(lp)
    o1 = ATTN_WIDTH
    o2 = o1 + KV_WIDTH
    o3 = o2 + KV_WIDTH
    o4 = o3 + ssm_w
    o5 = o4 + d

    def epi(accs, in_refs, out_refs):
        z = accs[0]
        c, s1, s2 = in_refs[2][0], in_refs[2][1], in_refs[2][2]
        out_refs[0][...] = _rope(z[:, :o1], c, s1, s2).astype(BF16)
        out_refs[1][...] = _rope(z[:, o1:o2], c, s1, s2).astype(BF16)
        out_refs[2][...] = z[:, o2:o3].astype(BF16)
        out_refs[3][...] = z[:, o3:o4]
        out_refs[4][...] = z[:, o4:o5]
        out_refs[5][...] = z[:, o5:]

    def rs(w, dt):
        return jax.ShapeDtypeStruct((lp, w), dt), pl.BlockSpec((tm, w), lambda i: (i, 0))

    shapes, specs = zip(rs(o1, BF16), rs(KV_WIDTH, BF16), rs(KV_WIDTH, BF16), rs(ssm_w, F32), rs(d, F32), rs(d, F32))
    res = _matmul(
        "mix_in_proj", (lp // tm,), None, [n, w_in, tabs],
        [pl.BlockSpec((tm, d), lambda i: (i, 0)), pl.BlockSpec((inw, d), lambda i: (0, 0)),
         pl.BlockSpec((3, tm, LANES), lambda i: (0, i, 0))],
        [(0, 1, "nt", 0)], [(tm, inw)], epi, list(shapes), list(specs), ("parallel",), comm)
    return (res, []) if comm is None else res


def _attn_mask(b):
    rows = lax.broadcasted_iota(jnp.int32, (BLOCK, 3 * BLOCK), 0)
    cols = lax.broadcasted_iota(jnp.int32, (BLOCK, 3 * BLOCK), 1)
    qpos = b * BLOCK + rows - PAD_FRONT
    kpos = (b - 1) * BLOCK + cols - PAD_FRONT
    dist = qpos - kpos
    band = (cols < 2 * BLOCK) & (kpos >= N_META) & (dist >= 0) & (dist < BLOCK)
    mrow = cols - 2 * BLOCK
    meta = (mrow >= PAD_FRONT) & ((mrow - PAD_FRONT) <= qpos)
    return band | meta


def _attn_probs(qh, kk, mask, sink):
    s = _dot(qh, kk, "nt") * (HEAD_DIM ** -0.5)
    s = jnp.where(mask, s, NEG_INF)
    m = jnp.maximum(jnp.max(s, axis=-1, keepdims=True), sink)
    e = jnp.exp(s - m)
    es = jnp.exp(sink - m)
    z = jnp.sum(e, axis=-1, keepdims=True) + es
    inv = 1.0 / z
    return e * inv, es * inv


def _head(ref_or_val, h):
    return ref_or_val[:, h * HEAD_DIM:(h + 1) * HEAD_DIM]


def _attn_fwd(q, k, v, sinks, comm=None):
    lp = q.shape[0]
    nb = lp // BLOCK

    def body(sink_ref, q_ref, kp_ref, kc_ref, km_ref, vp_ref, vc_ref, vm_ref, o_ref):
        b = pl.program_id(0)
        mask = _attn_mask(b)
        for hk in range(N_KV_HEADS):
            kk = jnp.concatenate([_head(kp_ref, hk), _head(kc_ref, hk), _head(km_ref, hk)], axis=0)
            vv = jnp.concatenate([_head(vp_ref, hk), _head(vc_ref, hk), _head(vm_ref, hk)], axis=0)
            for g in range(Q_PER_KV):
                h = hk * Q_PER_KV + g
                p, _ = _attn_probs(_head(q_ref, h), kk, mask, sink_ref[h])
                o_ref[:, h * HEAD_DIM:(h + 1) * HEAD_DIM] = _dot(p, vv, "nn").astype(BF16)

    cur = lambda b: (b, 0)
    prev = lambda b: (jnp.maximum(b - 1, 0), 0)
    first = lambda b: (0, 0)
    kvs = lambda f: pl.BlockSpec((BLOCK, KV_WIDTH), f)
    (o,), couts = _hosted_call(
        body, comm, out_shape=[jax.ShapeDtypeStruct((lp, ATTN_WIDTH), BF16)], grid=(nb,),
        in_specs=[pl.BlockSpec(memory_space=pltpu.SMEM), pl.BlockSpec((BLOCK, ATTN_WIDTH), cur),
                  kvs(prev), kvs(cur), kvs(first), kvs(prev), kvs(cur), kvs(first)],
        out_specs=[pl.BlockSpec((BLOCK, ATTN_WIDTH), cur)], scratch_shapes=[],
        sem=("parallel",), name="attn_fwd", args=(sinks, q, k, k, k, v, v, v))
    return o, couts


def _attn_bwd(q, k, v, do, sinks, tabs, comm=None):
    lp = q.shape[0]
    nb = lp // BLOCK
    scale = HEAD_DIM ** -0.5

    def body(sink_ref, q_ref, do_ref, kp_ref, kc_ref, km_ref, vp_ref, vc_ref, vm_ref, tq_ref, tk_ref, t0_ref,
             dq_ref, dk_ref, dv_ref, dkm_ref, dvm_ref, dsink_ref,
             dq_s, dkk_s, dvv_s, ck_s, cv_s, mk_s, mv_s):
        b = pl.program_id(0)

        @pl.when(b == 0)
        def _():
            for r in (ck_s, cv_s, mk_s, mv_s, dsink_ref):
                r[...] = jnp.zeros_like(r)

        @pl.when(b < nb)
        def _():
            mask = _attn_mask(b)
            for hk in range(N_KV_HEADS):
                kk = jnp.concatenate([_head(kp_ref, hk), _head(kc_ref, hk), _head(km_ref, hk)], axis=0)
                vv = jnp.concatenate([_head(vp_ref, hk), _head(vc_ref, hk), _head(vm_ref, hk)], axis=0)
                dkk = jnp.zeros((3 * BLOCK, HEAD_DIM), F32)
                dvv = jnp.zeros((3 * BLOCK, HEAD_DIM), F32)
                for g in range(Q_PER_KV):
                    h = hk * Q_PER_KV + g
                    qh = _head(q_ref, h)
                    doh = _head(do_ref, h)
                    p, ps = _attn_probs(qh, kk, mask, sink_ref[h])
                    dp = _dot(doh, vv, "nt")
                    delta = jnp.sum(p * dp, axis=-1, keepdims=True)
                    ds = (p * (dp - delta)).astype(BF16)
                    dsink_ref[h:h + 1, :] += jnp.zeros((1, LANES), F32) - jnp.sum(ps * delta)
                    dq_s[:, h * HEAD_DIM:(h + 1) * HEAD_DIM] = _dot(ds, kk, "nn") * scale
                    dkk = dkk + _dot(ds, qh, "tn") * scale
                    dvv = dvv + _dot(p, doh, "tn")
                dkk_s[:, hk * HEAD_DIM:(hk + 1) * HEAD_DIM] = dkk
                dvv_s[:, hk * HEAD_DIM:(hk + 1) * HEAD_DIM] = dvv
            dq_ref[...] = _rope_t(dq_s[...], tq_ref[0], tq_ref[1], tq_ref[2])
            dk_ref[...] = _rope_t(ck_s[...] + dkk_s[0:BLOCK, :], tk_ref[0], tk_ref[1], tk_ref[2])
            dv_ref[...] = cv_s[...] + dvv_s[0:BLOCK, :]
            ck_s[...] = dkk_s[BLOCK:2 * BLOCK, :]
            cv_s[...] = dvv_s[BLOCK:2 * BLOCK, :]
            mk_s[...] += dkk_s[2 * BLOCK:, :]
            mv_s[...] += dvv_s[2 * BLOCK:, :]

        @pl.when(b == nb)
        def _():
            dk_ref[...] = _rope_t(ck_s[...], tk_ref[0], tk_ref[1], tk_ref[2])
            dv_ref[...] = cv_s[...]
            dkm_ref[...] = _rope_t(mk_s[...], t0_ref[0], t0_ref[1], t0_ref[2])
            dvm_ref[...] = mv_s[...]

    cur = lambda b: (jnp.minimum(b, nb - 1), 0)
    prev = lambda b: (jnp.clip(b - 1, 0, nb - 1), 0)
    first = lambda b: (0, 0)
    kvs = lambda f: pl.BlockSpec((BLOCK, KV_WIDTH), f)
    tab = lambda f: pl.BlockSpec((3, BLOCK, LANES), lambda b: (0,) + f(b)[:1] + (0,))
    kv_out = lambda b: (jnp.maximum(b - 1, 0), 0)
    return _hosted_call(
        body, comm,
        out_shape=[jax.ShapeDtypeStruct((lp, ATTN_WIDTH), F32), jax.ShapeDtypeStruct((lp, KV_WIDTH), F32),
                   jax.ShapeDtypeStruct((lp, KV_WIDTH), F32), jax.ShapeDtypeStruct((BLOCK, KV_WIDTH), F32),
                   jax.ShapeDtypeStruct((BLOCK, KV_WIDTH), F32), jax.ShapeDtypeStruct((N_Q_HEADS, LANES), F32)],
        grid=(nb + 1,),
        in_specs=[pl.BlockSpec(memory_space=pltpu.SMEM), pl.BlockSpec((BLOCK, ATTN_WIDTH), cur),
                  pl.BlockSpec((BLOCK, ATTN_WIDTH), cur),
                  kvs(prev), kvs(cur), kvs(first), kvs(prev), kvs(cur), kvs(first),
                  tab(cur), tab(kv_out), tab(first)],
        out_specs=[pl.BlockSpec((BLOCK, ATTN_WIDTH), cur), kvs(kv_out), kvs(kv_out), kvs(first), kvs(first),
                   pl.BlockSpec((N_Q_HEADS, LANES), first)],
        scratch_shapes=[pltpu.VMEM((BLOCK, ATTN_WIDTH), F32), pltpu.VMEM((3 * BLOCK, KV_WIDTH), F32),
                        pltpu.VMEM((3 * BLOCK, KV_WIDTH), F32), pltpu.VMEM((BLOCK, KV_WIDTH), F32),
                        pltpu.VMEM((BLOCK, KV_WIDTH), F32), pltpu.VMEM((BLOCK, KV_WIDTH), F32),
                        pltpu.VMEM((BLOCK, KV_WIDTH), F32)],
        sem=("arbitrary",), name="attn_bwd", args=(sinks, q, do, k, k, k, v, v, v, tabs, tabs, tabs))


def _cmul(ar, ai, br, bi):
    return ar * br - ai * bi, ar * bi + ai * br


def _cpow(lr, li, n):
    rr = ri = None
    br, bi = lr, li
    while n:
        if n & 1:
            rr, ri = (br, bi) if rr is None else _cmul(rr, ri, br, bi)
        n >>= 1
        if n:
            br, bi = _cmul(br, bi, br, bi)
    return rr, ri


def _shift_rows(x, d, reverse):
    rows = lax.broadcasted_iota(jnp.int32, x.shape, 0)
    if not reverse:
        return jnp.where(rows >= d, pltpu.roll(x, d, 0), 0.0)
    return jnp.where(rows < SUBLANES - d, pltpu.roll(x, SUBLANES - d, 0), 0.0)


def _sublane_powers(mr, mi, reverse):
    rows = lax.broadcasted_iota(jnp.int32, mr.shape, 0)
    e = SUBLANES - 1 - rows if reverse else rows
    pr, pi = jnp.ones_like(mr), jnp.zeros_like(mr)
    br, bi = mr, mi
    for d in (1, 2, 4):
        tr, ti = _cmul(pr, pi, br, bi)
        on = (e & d) != 0
        pr, pi = jnp.where(on, tr, pr), jnp.where(on, ti, pi)
        if d < 4:
            br, bi = _cmul(br, bi, br, bi)
    return pr, pi


def _inclusive_prefix(er, ei, mr, mi, reverse):
    ir, ii, pr, pi = er, ei, mr, mi
    for d in (1, 2, 4):
        tr, ti = _cmul(pr, pi, _shift_rows(ir, d, reverse), _shift_rows(ii, d, reverse))
        ir, ii = ir + tr, ii + ti
        if d < 4:
            pr, pi = _cmul(pr, pi, pr, pi)
    return ir, ii


def _chain_rows(a, t, seg):
    return pl.ds(a * SUBLANES * seg + t, SUBLANES, stride=seg)


def _seg_scan(xr_ref, xi_ref, lam, seg, nchain, reverse, store, init, extra=None):
    nt = len(lam)
    acc0 = () if extra is None else extra[1]

    def step(i, carry):
        hs, acc = carry
        t = seg - 1 - i if reverse else i
        out = []
        for a in range(nchain):
            sl = _chain_rows(a, t, seg)
            for j in range(nt):
                lr, li = lam[j]
                k = 2 * (a * nt + j)
                hr, hi = hs[k], hs[k + 1]
                nr = lr * hr - li * hi + xr_ref[j, sl, :]
                ni = lr * hi + li * hr + xi_ref[j, sl, :]
                if store:
                    xr_ref[j, sl, :] = nr
                    xi_ref[j, sl, :] = ni
                if extra is not None:
                    acc = extra[0](t, a, j, nr, ni, acc)
                out += [nr, ni]
        return tuple(out), acc

    return lax.fori_loop(0, seg, step, (tuple(init), acc0))


def _ssm_scan(xr_ref, xi_ref, lam, seg, nchain, reverse, extra=None):
    nt = len(lam)
    zero = [jnp.zeros((SUBLANES, LANES), F32)] * (2 * nt * nchain)
    ends, _ = _seg_scan(xr_ref, xi_ref, lam, seg, nchain, reverse, False, zero)
    init = [None] * (2 * nt * nchain)
    last = 0 if reverse else SUBLANES - 1
    for j in range(nt):
        mr, mi = _cpow(lam[j][0], lam[j][1], seg)
        m8r, m8i = _cpow(mr, mi, SUBLANES)
        pwr, pwi = _sublane_powers(mr, mi, reverse)
        gr = gi = jnp.zeros((SUBLANES, LANES), F32)
        for a in (reversed(range(nchain)) if reverse else range(nchain)):
            k = 2 * (a * nt + j)
            incr, inci = _inclusive_prefix(ends[k], ends[k + 1], mr, mi, reverse)
            tr, ti = _cmul(pwr, pwi, gr, gi)
            init[k] = _shift_rows(incr, 1, reverse) + tr
            init[k + 1] = _shift_rows(inci, 1, reverse) + ti
            g2r, g2i = _cmul(m8r, m8i, gr, gi)
            gr = g2r + jnp.broadcast_to(incr[last:last + 1, :], gr.shape)
            gi = g2i + jnp.broadcast_to(inci[last:last + 1, :], gi.shape)
    _, acc = _seg_scan(xr_ref, xi_ref, lam, seg, nchain, reverse, True, init, extra)
    return acc


def _diag_mask():
    steps = LANES // SSM_GROUP // GROUPS_PER_TILE
    return (jnp.eye(steps, dtype=F32)[:, None, :, None] * jnp.eye(GROUPS_PER_TILE, dtype=F32)[None, :, None, :])


def _ssm_tables(bb_re, bb_im, c_re, c_im):
    g = bb_re.shape[0]
    nt = g // GROUPS_PER_TILE
    steps = LANES // SSM_GROUP // GROUPS_PER_TILE
    mask = _diag_mask()

    def b_tab(bb):
        x = bb.reshape(nt // steps, steps, GROUPS_PER_TILE, SSM_STATE, SSM_GROUP)
        x = jnp.transpose(x, (0, 1, 4, 2, 3))[:, :, None, None]
        m = jnp.transpose(mask, (0, 2, 3, 1))[None, :, :, :, None, :, None]
        return (x * m).reshape(nt, LANES, TILE_STATES)

    def c_tab(c):
        x = c.reshape(nt // steps, steps, GROUPS_PER_TILE, SSM_GROUP, SSM_STATE)
        x = jnp.transpose(x, (0, 1, 2, 4, 3))[:, :, :, :, None, None]
        m = mask[None, :, :, None, :, :, None]
        return (x * m).reshape(nt, TILE_STATES, LANES)

    return b_tab(bb_re), b_tab(bb_im), c_tab(c_re), c_tab(c_im)


def _ssm_untable_b(db, g):
    nt = g // GROUPS_PER_TILE
    steps = LANES // SSM_GROUP // GROUPS_PER_TILE
    x = db.reshape(nt // steps, steps, GROUPS_PER_TILE, SSM_STATE, steps, GROUPS_PER_TILE, SSM_GROUP)
    m = _diag_mask()[None, :, :, None, :, :, None]
    return jnp.sum(x * m, axis=(4, 5)).reshape(g, SSM_STATE, SSM_GROUP)


def _ssm_untable_c(dc, g):
    nt = g // GROUPS_PER_TILE
    steps = LANES // SSM_GROUP // GROUPS_PER_TILE
    x = dc.reshape(nt // steps, steps, steps, GROUPS_PER_TILE, SSM_GROUP, GROUPS_PER_TILE, SSM_STATE)
    m = jnp.transpose(_diag_mask(), (0, 2, 3, 1))[None, :, :, :, None, :, None]
    out = jnp.sum(x * m, axis=(2, 3))
    return jnp.transpose(out, (0, 1, 3, 2, 4)).reshape(g, SSM_GROUP, SSM_STATE)


def _lam_tiles(lam_ref):
    out = []
    for j in range(TILE_STATES // LANES):
        out.append(jnp.broadcast_to(lam_ref[:, j * LANES:(j + 1) * LANES], (SUBLANES, LANES)))
    return out


def _scan_chains(lp):
    for n in (4, 2, 1):
        if lp % (SUBLANES * n) == 0 and (lp // SUBLANES) % 16 == 0:
            return n
    raise ValueError(lp)


def _split_tiles(dst_ref, rows, val):
    for j in range(val.shape[1] // LANES):
        dst_ref[j, rows, :] = val[:, j * LANES:(j + 1) * LANES]


def _cat_tiles(src_ref, rows):
    njt = src_ref.shape[0]
    return jnp.concatenate([src_ref[j, rows, :] for j in range(njt)], axis=1).astype(BF16)


def _ssm_fwd(u, lam_re, lam_im, tb_re, tb_im, tc_re, tc_im, d_skip, comm=None):
    lp, w = u.shape
    nt = tb_re.shape[0]
    nchain = _scan_chains(lp)
    seg = lp // (SUBLANES * nchain)
    chunk = lp // SUBLANES
    njt = TILE_STATES // LANES

    def body(u_ref, lr_ref, li_ref, br_ref, bi_ref, cr_ref, ci_ref, d_ref, y_ref, yg_ref, xr, xi):
        t = pl.program_id(0)
        for s in range(SUBLANES):
            rs = pl.ds(s * chunk, chunk)
            ub = u_ref[rs, :].astype(BF16)
            _split_tiles(xr, rs, _dot(ub, br_ref[...], "nn"))
            _split_tiles(xi, rs, _dot(ub, bi_ref[...], "nn"))
        lrs, lis = _lam_tiles(lr_ref), _lam_tiles(li_ref)
        _ssm_scan(xr, xi, list(zip(lrs, lis)), seg, nchain, False)
        for s in range(SUBLANES):
            rs = pl.ds(s * chunk, chunk)
            y = _dot(_cat_tiles(xr, rs), cr_ref[...], "nn") - _dot(_cat_tiles(xi, rs), ci_ref[...], "nn")

            @pl.when(t % 2 == 0)
            def _():
                y_ref[rs, :] = y + d_ref[...] * u_ref[rs, :]

            @pl.when(t % 2 == 1)
            def _():
                total = y_ref[rs, :] + y
                y_ref[rs, :] = total
                yg_ref[rs, :] = _gelu(total).astype(BF16)

    blk = pl.BlockSpec((lp, LANES), lambda t: (0, t // 2))
    lam_spec = pl.BlockSpec((None, 1, TILE_STATES), lambda t: (t, 0, 0))
    b_spec = pl.BlockSpec((None, LANES, TILE_STATES), lambda t: (t, 0, 0))
    c_spec = pl.BlockSpec((None, TILE_STATES, LANES), lambda t: (t, 0, 0))
    (y, yg), couts = _hosted_call(
        body, comm, out_shape=[jax.ShapeDtypeStruct((lp, w), F32), jax.ShapeDtypeStruct((lp, w), BF16)], grid=(nt,),
        in_specs=[blk, lam_spec, lam_spec, b_spec, b_spec, c_spec, c_spec,
                  pl.BlockSpec((1, LANES), lambda t: (0, t // 2))],
        out_specs=[blk, blk],
        scratch_shapes=[pltpu.VMEM((njt, lp, LANES), F32), pltpu.VMEM((njt, lp, LANES), F32)],
        sem=("arbitrary",), name="ssm_fwd",
        args=(u, lam_re, lam_im, tb_re, tb_im, tc_re, tc_im, d_skip.reshape(1, w)))
    return (y, yg), couts


def _ssm_bwd(u, dy, lam_re, lam_im, tb_re, tb_im, tc_re, tc_im, d_skip, comm=None):
    lp, w = u.shape
    nt = tb_re.shape[0]
    nchain = _scan_chains(lp)
    seg = lp // (SUBLANES * nchain)
    chunk = lp // SUBLANES
    njt = TILE_STATES // LANES
    tbt_re, tbt_im = jnp.swapaxes(tb_re, 1, 2), jnp.swapaxes(tb_im, 1, 2)
    tct_re, tct_im = jnp.swapaxes(tc_re, 1, 2), jnp.swapaxes(tc_im, 1, 2)

    def body(u_ref, dy_ref, lr_ref, li_ref, br_ref, bi_ref, btr_ref, bti_ref, ctr_ref, cti_ref, d_ref,
             du_ref, dlr_ref, dli_ref, dbr_ref, dbi_ref, dcr_ref, dci_ref, dd_ref, hr, hi, ar, ai):
        t = pl.program_id(0)
        lrs, lis = _lam_tiles(lr_ref), _lam_tiles(li_ref)
        for s in range(SUBLANES):
            rs = pl.ds(s * chunk, chunk)
            ub = u_ref[rs, :].astype(BF16)
            dyb = dy_ref[rs, :].astype(BF16)
            _split_tiles(hr, rs, _dot(ub, br_ref[...], "nn"))
            _split_tiles(hi, rs, _dot(ub, bi_ref[...], "nn"))
            _split_tiles(ar, rs, _dot(dyb, ctr_ref[...], "nn"))
            _split_tiles(ai, rs, -_dot(dyb, cti_ref[...], "nn"))
        _ssm_scan(hr, hi, list(zip(lrs, lis)), seg, nchain, False)

        def dlam_step(tt, a, j, a_r, a_i, acc):
            sl = _chain_rows(a, jnp.maximum(tt - 1, 0), seg)
            p_r, p_i = hr[j, sl, :], hi[j, sl, :]
            acc = list(acc)
            acc[2 * j] = acc[2 * j] + jnp.where(tt > 0, a_r * p_r + a_i * p_i, 0.0)
            acc[2 * j + 1] = acc[2 * j + 1] + jnp.where(tt > 0, a_i * p_r - a_r * p_i, 0.0)
            return tuple(acc)

        zero = tuple([jnp.zeros((SUBLANES, LANES), F32)] * (2 * njt))
        conj = [(lr, -li) for lr, li in zip(lrs, lis)]
        acc = list(_ssm_scan(ar, ai, conj, seg, nchain, True, (dlam_step, zero)))
        row0 = lax.broadcasted_iota(jnp.int32, (SUBLANES, LANES), 0) == 0
        for j in range(njt):
            cs = slice(j * LANES, (j + 1) * LANES)
            for a in range(nchain):
                p_r = _shift_rows(hr[j, _chain_rows(a, seg - 1, seg), :], 1, False)
                p_i = _shift_rows(hi[j, _chain_rows(a, seg - 1, seg), :], 1, False)
                if a > 0:
                    before = pl.ds(a * SUBLANES * seg - 1, 1)
                    p_r = jnp.where(row0, jnp.broadcast_to(hr[j, before, :], p_r.shape), p_r)
                    p_i = jnp.where(row0, jnp.broadcast_to(hi[j, before, :], p_i.shape), p_i)
                a_r, a_i = ar[j, _chain_rows(a, 0, seg), :], ai[j, _chain_rows(a, 0, seg), :]
                acc[2 * j] = acc[2 * j] + a_r * p_r + a_i * p_i
                acc[2 * j + 1] = acc[2 * j + 1] + a_i * p_r - a_r * p_i
            dlr_ref[:, cs] = jnp.sum(acc[2 * j], axis=0, keepdims=True)
            dli_ref[:, cs] = jnp.sum(acc[2 * j + 1], axis=0, keepdims=True)

        dd = jnp.zeros((1, LANES), F32)
        for s in range(SUBLANES):
            rs = pl.ds(s * chunk, chunk)
            ub = u_ref[rs, :].astype(BF16)
            dyv = dy_ref[rs, :]
            dyb = dyv.astype(BF16)
            arb, aib = _cat_tiles(ar, rs), _cat_tiles(ai, rs)
            hrb, hib = _cat_tiles(hr, rs), _cat_tiles(hi, rs)
            du = _dot(arb, btr_ref[...], "nn") + _dot(aib, bti_ref[...], "nn")
            upd = [(dbr_ref, _dot(arb, ub, "tn")), (dbi_ref, _dot(aib, ub, "tn")),
                   (dcr_ref, _dot(dyb, hrb, "tn")), (dci_ref, -_dot(dyb, hib, "tn"))]
            for ref, val in upd:
                if s == 0:
                    ref[...] = val
                else:
                    ref[...] += val
            rows = lax.broadcasted_iota(jnp.int32, (chunk, LANES), 0) + s * chunk
            keep = rows >= PAD_FRONT
            dd = dd + jnp.sum(dyv * u_ref[rs, :], axis=0, keepdims=True)

            @pl.when(t % 2 == 0)
            def _():
                du_ref[rs, :] = jnp.where(keep, du + d_ref[...] * dyv, 0.0)

            @pl.when(t % 2 == 1)
            def _():
                du_ref[rs, :] += jnp.where(keep, du, 0.0)

        @pl.when(t % 2 == 0)
        def _():
            dd_ref[...] = dd

    blk = pl.BlockSpec((lp, LANES), lambda t: (0, t // 2))
    vec = pl.BlockSpec((1, LANES), lambda t: (0, t // 2))
    lam_spec = pl.BlockSpec((None, 1, TILE_STATES), lambda t: (t, 0, 0))
    b_spec = pl.BlockSpec((None, LANES, TILE_STATES), lambda t: (t, 0, 0))
    c_spec = pl.BlockSpec((None, TILE_STATES, LANES), lambda t: (t, 0, 0))
    lam_shape = jax.ShapeDtypeStruct((nt, 1, TILE_STATES), F32)
    bt_shape = jax.ShapeDtypeStruct((nt, TILE_STATES, LANES), F32)
    ct_shape = jax.ShapeDtypeStruct((nt, LANES, TILE_STATES), F32)
    st = pltpu.VMEM((njt, lp, LANES), F32)
    return _hosted_call(
        body, comm,
        out_shape=[jax.ShapeDtypeStruct((lp, w), F32), lam_shape, lam_shape, bt_shape, bt_shape, ct_shape, ct_shape,
                   jax.ShapeDtypeStruct((1, w), F32)],
        grid=(nt,),
        in_specs=[blk, blk, lam_spec, lam_spec, b_spec, b_spec, c_spec, c_spec, b_spec, b_spec, vec],
        out_specs=[blk, lam_spec, lam_spec, c_spec, c_spec, b_spec, b_spec, vec],
        scratch_shapes=[st, st, st, st], sem=("arbitrary",), name="ssm_bwd",
        args=(u, dy, lam_re, lam_im, tb_re, tb_im, tbt_re, tbt_im, tct_re, tct_im, d_skip.reshape(1, w)))


def _ssm_params(a_re, a_im, log_dt, b_re, b_im):
    dt = jnp.exp(log_dt)[:, None]
    mag = jnp.exp(a_re * dt)
    lb_re = mag * jnp.cos(a_im * dt)
    lb_im = mag * jnp.sin(a_im * dt)
    den = a_re * a_re + a_im * a_im
    num_re = lb_re - 1.0
    coef_re = (num_re * a_re + lb_im * a_im) / den
    coef_im = (lb_im * a_re - num_re * a_im) / den
    bb_re = coef_re[..., None] * b_re - coef_im[..., None] * b_im
    bb_im = coef_re[..., None] * b_im + coef_im[..., None] * b_re
    return lb_re, lb_im, bb_re, bb_im


def _merge_fwd(o, yg, ga, gs, w3t, comm=None):
    lp, d = ga.shape
    kw = w3t.shape[2]
    tm = _row_tile(lp)

    def epi(accs, in_refs, out_refs):
        attn, vv, gg = accs
        out_refs[0][...] = (_sigmoid(in_refs[5][...]) * attn
                            + _sigmoid(in_refs[6][...]) * (vv * _sigmoid(gg))).astype(BF16)

    wspec = lambda which: pl.BlockSpec((None, d, kw), lambda i: (which, 0, 0))
    rowspec = pl.BlockSpec((tm, d), lambda i: (i, 0))
    aspec = pl.BlockSpec((tm, kw), lambda i: (i, 0))
    res = _matmul(
        "merge_fwd", (lp // tm,), None, [o, yg, w3t, w3t, w3t, ga, gs],
        [aspec, aspec, wspec(0), wspec(1), wspec(2), rowspec, rowspec],
        [(0, 2, "nt", 0), (1, 3, "nt", 1), (1, 4, "nt", 2)], [(tm, d)] * 3, epi,
        [jax.ShapeDtypeStruct((lp, d), BF16)], [rowspec], ("parallel",), comm)
    return (res[0], []) if comm is None else (res[0][0], res[1])


def _out_proj(merged, w_out, h, next_gain, comm=None):
    lp, d = h.shape
    tm = _row_tile(lp)
    shapes, specs, extra_in, extra_specs = _residual_specs(lp, d, tm, next_gain)

    def epi(accs, in_refs, out_refs):
        _residual_outputs(in_refs[2][...] + accs[0], in_refs, out_refs, 3 if extra_in else None)

    rowspec = pl.BlockSpec((tm, d), lambda i: (i, 0))
    res = _matmul(
        "mix_out_proj", (lp // tm,), None, [merged, w_out, h] + extra_in,
        [rowspec, pl.BlockSpec((d, d), lambda i: (0, 0)), rowspec] + extra_specs,
        [(0, 1, "nn", 0)], [(tm, d)], epi, shapes, specs, ("parallel",), comm)
    return (res, []) if comm is None else res


def _merge_bwd(dhb, w_out, o, yg, ga, gs, w3t):
    lp, d = ga.shape
    kw = w3t.shape[2]
    tm = _row_tile(lp)

    def epi(accs, in_refs, out_refs):
        dm, attn, vv, gg = accs
        sa = _sigmoid(in_refs[7][...])
        ss = _sigmoid(in_refs[8][...])
        sg = _sigmoid(gg)
        ssm = vv * sg
        dssm = dm * ss
        out_refs[0][...] = (dm * sa).astype(BF16)
        out_refs[1][...] = (dssm * sg).astype(BF16)
        out_refs[2][...] = (dssm * vv * sg * (1.0 - sg)).astype(BF16)
        out_refs[3][...] = (dm * attn * sa * (1.0 - sa)).astype(BF16)
        out_refs[4][...] = (dm * ssm * ss * (1.0 - ss)).astype(BF16)

    wspec = lambda which: pl.BlockSpec((None, d, kw), lambda i: (which, 0, 0))
    rowspec = pl.BlockSpec((tm, d), lambda i: (i, 0))
    aspec = pl.BlockSpec((tm, kw), lambda i: (i, 0))
    shp = jax.ShapeDtypeStruct((lp, d), BF16)
    return _matmul(
        "merge_bwd", (lp // tm,), None, [dhb, w_out, o, yg, w3t, w3t, w3t, ga, gs],
        [rowspec, pl.BlockSpec((d, d), lambda i: (0, 0)), aspec, aspec, wspec(0), wspec(1), wspec(2), rowspec,
         rowspec],
        [(0, 1, "nt", 0), (2, 4, "nt", 1), (3, 5, "nt", 2), (3, 6, "nt", 3)], [(tm, d)] * 4, epi,
        [shp] * 5, [rowspec] * 5, ("parallel",))


def _branch_bwd(dattn, dv, dg, w3t, y):
    lp, d = dattn.shape
    kw = w3t.shape[2]
    tm = _row_tile(lp)

    def epi(accs, in_refs, out_refs):
        out_refs[0][...] = accs[0].astype(BF16)
        out_refs[1][...] = accs[1] * _gelu_grad(in_refs[6][...])

    wspec = lambda which: pl.BlockSpec((None, d, kw), lambda i: (which, 0, 0))
    rowspec = pl.BlockSpec((tm, d), lambda i: (i, 0))
    aspec = pl.BlockSpec((tm, kw), lambda i: (i, 0))
    return _matmul(
        "branch_bwd", (lp // tm,), None, [dattn, dv, dg, w3t, w3t, w3t, y],
        [rowspec, rowspec, rowspec, wspec(0), wspec(1), wspec(2), aspec],
        [(0, 3, "nn", 0), (1, 4, "nn", 1), (2, 5, "nn", 1)], [(tm, kw)] * 2, epi,
        [jax.ShapeDtypeStruct((lp, kw), BF16), jax.ShapeDtypeStruct((lp, kw), F32)], [aspec, aspec],
        ("parallel",))


def _sibling_half(acc, rows):
    half = rows // 2
    return jnp.where(lax.axis_index("c") == 0, acc[half:rows], acc[:half]).astype(BF16)


def _tn_cols(x, ys, name):
    lp, kx = x.shape
    n = ys[0].shape[1]
    n4 = n // N_CHIPS
    tk = _tn_tiles(lp)
    ny = len(ys)

    def epi(accs, in_refs, out_refs):
        for i, acc in enumerate(accs):
            out_refs[i][...] = acc
            out_refs[ny + i][...] = _sibling_half(acc, kx)

    shp = jax.ShapeDtypeStruct((N_CHIPS, kx, n4), F32)
    shp_half = jax.ShapeDtypeStruct((N_CHIPS, kx // 2, n4), BF16)
    res = _matmul(
        name, (N_CHIPS, lp // tk), 1, [x] + list(ys),
        [pl.BlockSpec((tk, kx), lambda j, k: (k, 0))] + [pl.BlockSpec((tk, n4), lambda j, k: (k, j))] * ny,
        [(0, 1 + i, "tn", i) for i in range(ny)], [(kx, n4)] * ny, epi,
        [shp] * ny + [shp_half] * ny,
        [pl.BlockSpec((None, kx, n4), lambda j, k: (j, 0, 0))] * ny
        + [pl.BlockSpec((None, kx // 2, n4), lambda j, k: (j, 0, 0))] * ny,
        ("arbitrary", "arbitrary"))
    return res[:ny], res[ny:]


def _tn_full(x, y, name, tn_cols=None):
    lp, kx = x.shape
    n = y.shape[1]
    tk = _tn_tiles(lp)
    tn = n if tn_cols is None else tn_cols
    k4 = kx // N_CHIPS

    def epi(accs, in_refs, out_refs):
        for j in range(N_CHIPS):
            slab = accs[0][j * k4:(j + 1) * k4]
            out_refs[0][j] = slab
            out_refs[1][j] = _sibling_half(slab, k4)

    return _matmul(
        name, (n // tn, lp // tk), 1, [x, y],
        [pl.BlockSpec((tk, kx), lambda j, k: (k, 0)), pl.BlockSpec((tk, tn), lambda j, k: (k, j))],
        [(0, 1, "tn", 0)], [(kx, tn)], epi,
        [jax.ShapeDtypeStruct((N_CHIPS, k4, n), F32), jax.ShapeDtypeStruct((N_CHIPS, k4 // 2, n), BF16)],
        [pl.BlockSpec((N_CHIPS, k4, tn), lambda j, k: (0, 0, j)),
         pl.BlockSpec((N_CHIPS, k4 // 2, tn), lambda j, k: (0, 0, j))],
        ("arbitrary", "arbitrary"))


def _in_proj_bwd(dz, w_in, dh, h_in, gain):
    lp, d = h_in.shape
    inw = w_in.shape[0]
    tm = _row_tile(lp)

    def epi(accs, in_refs, out_refs):
        i = pl.program_id(0)
        dx, dgrow = _rms_bwd_math(accs[0], in_refs[3][...], in_refs[4][...])
        dh_new = in_refs[2][...] + dx
        out_refs[0][...] = dh_new
        out_refs[2][...] = dh_new.astype(BF16)

        @pl.when(i == 0)
        def _():
            out_refs[1][...] = jnp.zeros_like(out_refs[1])

        out_refs[1][...] += jnp.sum(dgrow, axis=0, keepdims=True)

    row = pl.BlockSpec((tm, d), lambda i: (i, 0))
    return _matmul(
        "mix_in_proj_bwd", (lp // tm,), None, [dz, w_in, dh, h_in, gain.reshape(1, d)],
        [pl.BlockSpec((tm, inw), lambda i: (i, 0)), pl.BlockSpec((inw, d), lambda i: (0, 0)), row, row,
         pl.BlockSpec((1, d), lambda i: (0, 0))],
        [(0, 1, "nn", 0)], [(tm, d)], epi,
        [jax.ShapeDtypeStruct((lp, d), F32), jax.ShapeDtypeStruct((1, d), F32), jax.ShapeDtypeStruct((lp, d), BF16)],
        [row, pl.BlockSpec((1, d), lambda i: (0, 0)), row], ("arbitrary",))


def _loss_head(h, gain, target):
    lp, d = h.shape
    nb = lp // BLOCK

    def body(h_ref, g_ref, t_ref, dh_ref, dg_ref, loss_ref, dhb_ref):
        i = pl.program_id(0)

        @pl.when(i == 0)
        def _():
            dg_ref[...] = jnp.zeros_like(dg_ref)
            loss_ref[...] = jnp.zeros_like(loss_ref)
            dh_ref[...] = jnp.zeros_like(dh_ref)
            dhb_ref[...] = jnp.zeros_like(dhb_ref)

        @pl.when(i > 0)
        def _():
            x = h_ref[...]
            g = g_ref[...]
            r = lax.rsqrt(jnp.mean(x * x, axis=-1, keepdims=True) + EPS)
            err = x * r * g - t_ref[...]
            loss_ref[...] += jnp.zeros_like(loss_ref) + 0.5 * jnp.sum(jnp.sum(err * err, axis=-1, keepdims=True)) / d
            dx, dgrow = _rms_bwd_math(err * (1.0 / d), x, g)
            dh_ref[...] = dx
            dhb_ref[...] = dx.astype(BF16)
            dg_ref[...] += jnp.sum(dgrow, axis=0, keepdims=True)

    row = pl.BlockSpec((BLOCK, d), lambda i: (i, 0))
    one = pl.BlockSpec((1, d), lambda i: (0, 0))
    return pl.pallas_call(
        body,
        out_shape=[jax.ShapeDtypeStruct((lp, d), F32), jax.ShapeDtypeStruct((1, d), F32),
                   jax.ShapeDtypeStruct((SUBLANES, LANES), F32), jax.ShapeDtypeStruct((lp, d), BF16)],
        grid=(nb,),
        in_specs=[row, one, pl.BlockSpec((BLOCK, d), lambda i: (jnp.maximum(i - 1, 0), 0))],
        out_specs=[row, one, pl.BlockSpec((SUBLANES, LANES), lambda i: (0, 0)), row],
        compiler_params=_cparams(("arbitrary",)), name="loss_head")(h, gain.reshape(1, d), target)


def _adam_math(w, g, m, v):
    m = ADAM_B1 * m + (1.0 - ADAM_B1) * g
    v = ADAM_B2 * v + (1.0 - ADAM_B2) * (g * g)
    m_hat = m / (1.0 - ADAM_B1 ** ADAM_STEP)
    v_hat = v / (1.0 - ADAM_B2 ** ADAM_STEP)
    delta = -ADAM_LR * (m_hat / (jnp.sqrt(v_hat) + ADAM_EPS) + ADAM_WD * w)
    return delta, m, v


def _adamw_layers(w, m, v, mine, other, pos, name):
    depth, r, c = w.shape
    half = r // 2
    tr = _div_tile(half, c * 4)
    nh = half // tr

    def body(*refs):
        pos_ref, w_ref, m_ref, v_ref = refs[:4]
        mine_refs = refs[4:4 + depth]
        other_refs = refs[4 + depth:4 + 2 * depth]
        g_out, d_out, m_out, v_out = refs[4 + 2 * depth:]
        layer, i = pl.program_id(0), pl.program_id(1)
        is_mine = (i // nh) == pos_ref[0]

        def update(g):
            delta, nm, nv = _adam_math(w_ref[...], g, m_ref[...], v_ref[...])
            g_out[...] = g
            d_out[...] = delta
            m_out[...] = nm
            v_out[...] = nv

        for l in range(depth):
            @pl.when((layer == l) & is_mine)
            def _(l=l):
                update(mine_refs[l][...])

            @pl.when((layer == l) & jnp.logical_not(is_mine))
            def _(l=l):
                update(other_refs[l][...])

    stacked = pl.BlockSpec((None, tr, c), lambda l, i, p: (l, i, 0))

    def gspec(layer, is_other):
        def imap(l, i, p):
            first = jnp.where(is_other, 1 - p[0], p[0]) * nh
            here = jnp.clip(i - first, 0, nh - 1)
            return (jnp.where(l == layer, here, jnp.where(l < layer, 0, nh - 1)), 0)
        return pl.BlockSpec((tr, c), imap)

    shp = jax.ShapeDtypeStruct((depth, r, c), F32)
    grid_spec = pltpu.PrefetchScalarGridSpec(
        num_scalar_prefetch=1, grid=(depth, 2 * nh),
        in_specs=[stacked] * 3 + [gspec(l, 0) for l in range(depth)] + [gspec(l, 1) for l in range(depth)],
        out_specs=[stacked] * 4)
    return pl.pallas_call(
        body, out_shape=[shp] * 4, grid_spec=grid_spec,
        compiler_params=_cparams(("arbitrary", "arbitrary")), name=name)(pos, w, m, v, *mine, *other)


def _adamw_whole(w, g, m, v, name):
    def body(w_ref, g_ref, m_ref, v_ref, d_out, m_out, v_out):
        delta, nm, nv = _adam_math(w_ref[...], g_ref[...], m_ref[...], v_ref[...])
        d_out[...] = delta
        m_out[...] = nm
        v_out[...] = nv

    shp = jax.ShapeDtypeStruct(w.shape, F32)
    return pl.pallas_call(body, out_shape=[shp] * 3, compiler_params=_cparams(), name=name)(w, g, m, v)


def _mesh_pos():
    return lax.axis_index("x"), lax.axis_index("y"), lax.axis_index("c")


def _row_half(ref, which, lead):
    half = ref.shape[lead] // 2
    idx = (slice(None),) * lead + (pl.ds(which * half, half), slice(None))
    return ref.at[idx]


def _gather_comm(arrs, tag):
    n = len(arrs)

    def ctx(ins, outs, sems):
        send_sems, recv_sems, local_sems = sems
        x, y, c = _mesh_pos()
        chips = [(1 - x, y), (x, 1 - y), (1 - x, 1 - y)]

        def slot(k, chip, which):
            lead = len(ins[k].shape) - 2
            return _row_half(outs[k].at[2 * chip[0] + chip[1]], which, lead)

        def copy(k, j, src, dst, to):
            return pltpu.make_async_remote_copy(
                src_ref=src, dst_ref=dst, send_sem=send_sems.at[6 * k + j], recv_sem=recv_sems.at[6 * k + j],
                device_id=to, device_id_type=MESH)

        def local(k):
            return pltpu.make_async_copy(ins[k], outs[k].at[2 * x + y], local_sems.at[k])

        def first(k, j):
            lead = len(ins[k].shape) - 2
            return copy(k, j, _row_half(ins[k], c, lead), slot(k, (x, y), c), (*chips[j], c))

        def passed(k, j, which):
            return copy(k, 3 + j, slot(k, chips[j], which), slot(k, chips[j], which), (x, y, 1 - c))

        def landed(k, j):
            return copy(k, j, slot(k, chips[j], c), slot(k, chips[j], c), (x, y, 1 - c))

        return c, local, first, passed, landed

    def start(ins, outs, sems):
        c, local, first, passed, landed = ctx(ins, outs, sems)
        for k in range(n):
            local(k).start()
            for j in range(3):
                first(k, j).start()

    def mid(ins, outs, sems):
        c, local, first, passed, landed = ctx(ins, outs, sems)
        for j in range(3):
            for k in range(n):
                landed(k, j).wait_recv()
                passed(k, j, c).start()

    def finish(ins, outs, sems):
        c, local, first, passed, landed = ctx(ins, outs, sems)
        for j in range(3):
            for k in range(n):
                passed(k, j, 1 - c).wait_recv()
        for k in range(n):
            for j in range(3):
                first(k, j).wait_send()
                passed(k, j, c).wait_send()
            local(k).wait()

    return _Comm(
        tag, arrs, [jax.ShapeDtypeStruct((N_CHIPS,) + a.shape, a.dtype) for a in arrs],
        [pltpu.SemaphoreType.DMA((6 * n,)), pltpu.SemaphoreType.DMA((6 * n,)), pltpu.SemaphoreType.DMA((n,))],
        start, mid, finish)


def _run_comm(comm, name):
    n_in, n_out = len(comm.ins), len(comm.out_shapes)

    def body(*refs):
        ins, outs, sems = refs[:n_in], refs[n_in:n_in + n_out], refs[n_in + n_out:]
        comm.start(ins, outs, sems)
        if comm.mid is not None:
            comm.mid(ins, outs, sems)
        comm.finish(ins, outs, sems)

    return pl.pallas_call(
        body, out_shape=comm.out_shapes, in_specs=[HBM_SPEC] * n_in, out_specs=[HBM_SPEC] * n_out,
        scratch_shapes=comm.sems, name=name)(*comm.ins)


def _all_gather_chips(arrs, name):
    return _run_comm(_gather_comm(arrs, "gather"), name)


GATHER_US_PER_BYTE = 380.0 / 11.65e6
HOST_US = dict(ffn_up=68.0, ffn_down=37.0, in_proj=38.0, attn_fwd=103.0, ssm_fwd=70.0, merge_fwd=30.0,
               out_proj=23.0)
HOST_SLACK_US = 10.0


class _WeightStream:
    def __init__(self, pieces):
        self.keys = [k for k, _ in pieces]
        self.shards = dict(pieces)
        self.next = 0
        self.full = {}
        self.pending = []

    def comm_for(self, host):
        budget = HOST_US[host] + HOST_SLACK_US
        taken, cost = [], 0.0
        while self.next < len(self.keys):
            key = self.keys[self.next]
            c = self.shards[key].size * self.shards[key].dtype.itemsize * GATHER_US_PER_BYTE
            if cost + c > budget and taken:
                break
            taken.append(key)
            cost += c
            self.next += 1
        self.pending = taken
        if not taken:
            return None
        return _gather_comm([self.shards[k] for k in taken], "g_" + "_".join(k[1] for k in taken))

    def deposit(self, gathered):
        for key, arr in zip(self.pending, gathered):
            self.full[key] = arr
        self.pending = []

    def get(self, key):
        if key not in self.full:
            upto = self.keys.index(key) + 1
            keys = self.keys[self.next:upto]
            self.next = upto
            for k, arr in zip(keys, _all_gather_chips([self.shards[k] for k in keys], "gather_now")):
                self.full[k] = arr
        return self.full[key]


def _all_gather_devices(x_shard, name):
    m_per, ncol = x_shard.shape

    def body(x_ref, out_ref, send_sems, recv_sems, local_sem):
        x, y, c = _mesh_pos()
        me, sibling = (x, y, c), (x, y, 1 - c)
        chips = [(1 - x, y), (x, 1 - y), (1 - x, 1 - y)]

        def rows(px, py, pc):
            return out_ref.at[4 * px + 2 * py + pc]

        def copy(k, block, to, src=None):
            return pltpu.make_async_remote_copy(
                src_ref=rows(*block) if src is None else src, dst_ref=rows(*block),
                send_sem=send_sems.at[k], recv_sem=recv_sems.at[k], device_id=to, device_id_type=MESH)

        mine = pltpu.make_async_copy(x_ref, rows(*me), local_sem)
        mine.start()
        first = [copy(0, me, sibling, src=x_ref)]
        first += [copy(1 + j, me, (*chip, c), src=x_ref) for j, chip in enumerate(chips)]
        for cp in first:
            cp.start()
        passed = [copy(4 + j, (*chip, c), sibling) for j, chip in enumerate(chips)]
        for j, chip in enumerate(chips):
            copy(1 + j, (*chip, c), me).wait_recv()
            passed[j].start()
        copy(0, sibling, me).wait_recv()
        for j, chip in enumerate(chips):
            copy(4 + j, (*chip, 1 - c), me).wait_recv()
        for cp in first + passed:
            cp.wait_send()
        mine.wait()

    return pl.pallas_call(
        body, out_shape=jax.ShapeDtypeStruct((8, m_per, ncol), x_shard.dtype),
        in_specs=[pl.BlockSpec(memory_space=pltpu.VMEM)], out_specs=pl.BlockSpec(memory_space=pltpu.VMEM),
        scratch_shapes=[pltpu.SemaphoreType.DMA((7,)), pltpu.SemaphoreType.DMA((7,)), pltpu.SemaphoreType.DMA],
        compiler_params=pltpu.CompilerParams(vmem_limit_bytes=VMEM_LIMIT), name=name)(x_shard)


def _sum_devices(g8, name):
    _, r, c = g8.shape
    tr = _div_tile(r, c * 4 * 8)

    def body(g_ref, o_ref):
        acc = g_ref[0]
        for dev in range(1, 8):
            acc = acc + g_ref[dev]
        o_ref[...] = acc

    return pl.pallas_call(
        body, out_shape=jax.ShapeDtypeStruct((r, c), F32), grid=(r // tr,),
        in_specs=[pl.BlockSpec((8, tr, c), lambda i: (0, i, 0))], out_specs=pl.BlockSpec((tr, c), lambda i: (i, 0)),
        compiler_params=_cparams(("parallel",)), name=name)(g8)


def _chip_partials(arrs, recvs, pos, name):
    n = len(arrs)

    def body(pos_ref, *refs):
        for a_ref, b_ref, o_ref in zip(refs[:n], refs[n:2 * n], refs[2 * n:]):
            o_ref[...] = (a_ref[...] + b_ref[...]).astype(BF16)

    own_specs, recv_specs, shapes = [], [], []
    for arr in arrs:
        nslab, r, c = arr.shape
        own_specs.append(pl.BlockSpec((None, r // 2, c), lambda j, p: (j, p[0], 0)))
        recv_specs.append(pl.BlockSpec((None, r // 2, c), lambda j, p: (j, 0, 0)))
        shapes.append(jax.ShapeDtypeStruct((nslab, r // 2, c), BF16))
    grid_spec = pltpu.PrefetchScalarGridSpec(
        num_scalar_prefetch=1, grid=(N_CHIPS,), in_specs=own_specs + recv_specs, out_specs=recv_specs)
    return pl.pallas_call(
        body, out_shape=shapes, grid_spec=grid_spec,
        compiler_params=_cparams(("parallel",)), name=name)(pos, *arrs, *recvs)


def _chip_exchange_comm(parts, tag):
    n = len(parts)

    def copies(ins, outs, sems):
        send_sems, recv_sems = sems
        x, y, c = _mesh_pos()
        chips = [(1 - x, y), (x, 1 - y), (1 - x, 1 - y)]
        return [pltpu.make_async_remote_copy(
            src_ref=ins[k].at[2 * chip[0] + chip[1]], dst_ref=outs[k].at[j],
            send_sem=send_sems.at[3 * k + j], recv_sem=recv_sems.at[3 * k + j],
            device_id=(*chip, c), device_id_type=MESH) for k in range(n) for j, chip in enumerate(chips)]

    def start(ins, outs, sems):
        for cp in copies(ins, outs, sems):
            cp.start()

    def finish(ins, outs, sems):
        for cp in copies(ins, outs, sems):
            cp.wait()

    return _Comm(
        tag, parts, [jax.ShapeDtypeStruct((3,) + p.shape[1:], p.dtype) for p in parts],
        [pltpu.SemaphoreType.DMA((3 * n,)), pltpu.SemaphoreType.DMA((3 * n,))], start, None, finish)


def _reduce_halves(arrs, recvs, gots, pos, name):
    n = len(arrs)

    def body(pos_ref, *refs):
        for a_ref, b_ref, g_ref, o_ref in zip(refs[:n], refs[n:2 * n], refs[2 * n:3 * n], refs[3 * n:]):
            acc = a_ref[...] + b_ref[...]
            for j in range(3):
                acc = acc + g_ref[j].astype(F32)
            o_ref[...] = acc

    own_specs, recv_specs, got_specs, out_specs, shapes = [], [], [], [], []
    for arr in arrs:
        _, r, c = arr.shape
        own_specs.append(pl.BlockSpec((None, r // 2, c), lambda i, p: (p[1], p[0], 0)))
        recv_specs.append(pl.BlockSpec((None, r // 2, c), lambda i, p: (p[1], 0, 0)))
        got_specs.append(pl.BlockSpec((3, r // 2, c), lambda i, p: (0, 0, 0)))
        out_specs.append(pl.BlockSpec((r // 2, c), lambda i, p: (0, 0)))
        shapes.append(jax.ShapeDtypeStruct((r // 2, c), F32))
    grid_spec = pltpu.PrefetchScalarGridSpec(
        num_scalar_prefetch=1, grid=(1,), in_specs=own_specs + recv_specs + got_specs, out_specs=out_specs)
    return pl.pallas_call(
        body, out_shape=shapes, grid_spec=grid_spec,
        compiler_params=_cparams(("arbitrary",)), name=name)(pos, *arrs, *recvs, *gots)


def _share_halves(halves, name):
    n = len(halves)

    def body(*refs):
        ins, outs = refs[:n], refs[n:2 * n]
        send_sems, recv_sems = refs[2 * n:]
        x, y, c = _mesh_pos()
        cps = []
        for k in range(n):
            cp = pltpu.make_async_remote_copy(
                src_ref=ins[k], dst_ref=outs[k], send_sem=send_sems.at[k], recv_sem=recv_sems.at[k],
                device_id=(x, y, 1 - c), device_id_type=MESH)
            cp.start()
            cps.append(cp)
        for cp in cps:
            cp.wait()

    return pl.pallas_call(
        body, out_shape=[jax.ShapeDtypeStruct(h.shape, h.dtype) for h in halves],
        in_specs=[HBM_SPEC] * n, out_specs=[HBM_SPEC] * n,
        scratch_shapes=[pltpu.SemaphoreType.DMA((n,)), pltpu.SemaphoreType.DMA((n,))], name=name)(*halves)


class _Reduction:
    def __init__(self, arrs, others, pos, tag):
        self.arrs, self.pos, self.tag = arrs, pos, tag
        self.recv = _share_halves(others, "rs_sibling_" + tag)
        self.parts = _chip_partials(arrs, self.recv, pos, "rs_partial_" + tag)
        self.got = None

    def comm(self):
        return _chip_exchange_comm(self.parts, "rs_" + self.tag)

    def end(self):
        if self.got is None:
            self.got = _run_comm(self.comm(), "rs_chips_" + self.tag)
        halves = _reduce_halves(self.arrs, self.recv, self.got, self.pos, "rs_reduce_" + self.tag)
        return halves, _share_halves(halves, "rs_share_" + self.tag)


def _w_in_full(p, l, ws):
    slabs = ws.get((l, "w_in"))
    return slabs.reshape(-1, slabs.shape[2])


def _w3t_full(p, l, ws):
    if "w3t" not in p:
        slabs = ws.get((l, "w3"))
        p["w3t"] = jnp.swapaxes(slabs, 0, 1).reshape(slabs.shape[1], -1, slabs.shape[3])
    return p["w3t"]


def _w_out_full(l, ws):
    slabs = ws.get((l, "w_out"))
    return slabs.reshape(-1, slabs.shape[2])


def _layer_fwd(h, n0, l, p, next_gain, ws, tabs):
    def hosted(host, fn, *args):
        out, got = fn(*args, ws.comm_for(host))
        ws.deposit(got)
        return out

    ffn1_saved = hosted("ffn_up", _ffn_up, n0, ws.get((l, "wg1")), ws.get((l, "wu1")))
    h1, n = hosted("ffn_down", _ffn_down, ffn1_saved[2], ws.get((l, "wd1")), h, p["mix_norm"])
    ssm_w = p["ssm_d"].shape[0]
    q, k, v, u, ga, gs = hosted("in_proj", _in_proj, n, _w_in_full(p, l, ws), tabs, ssm_w)
    o = hosted("attn_fwd", _attn_fwd, q, k, v, p["attn_sinks"])
    y, yg = hosted("ssm_fwd", _ssm_fwd, u, *p["ssm_tabs"], p["ssm_d"])
    merged = hosted("merge_fwd", _merge_fwd, o, yg, ga, gs, _w3t_full(p, l, ws))
    h2, n2 = hosted("out_proj", _out_proj, merged, _w_out_full(l, ws), h1, p["ffn2_norm"])
    ffn2_saved = hosted("ffn_up", _ffn_up, n2, ws.get((l, "wg2")), ws.get((l, "wu2")))
    h3, *n3 = hosted("ffn_down", _ffn_down, ffn2_saved[2], ws.get((l, "wd2")), h2, next_gain)
    saved = dict(h0=h, h1=h1, h2=h2, ffn1=ffn1_saved, ffn2=ffn2_saved, n_mix=n, q=q, k=k, v=v, u=u, ga=ga, gs=gs,
                 o=o, y=y, yg=yg, merged=merged)
    return h3, (n3[0] if n3 else None), saved


def _layer_bwd(dh_pair, l, p, ws, s, tabs, pos):
    g = {}
    (dh2, dhb), g["ffn2_norm"], red_ffn2, _ = _ffn_bwd(
        dh_pair, s["h2"], p["ffn2_norm"], ws.get((l, "wg2")), ws.get((l, "wu2")), ws.get((l, "wd2")), p["f4"],
        s["ffn2"], pos)
    w3, w_out_w = _w3t_full(p, l, ws), _w_out_full(l, ws)
    lp, d = dh2.shape
    d4 = d // N_CHIPS
    dw_out, dw_out_other = _tn_full(s["merged"], dhb, "mix_dw_out")
    dattn, dv, dg, dga, dgs = _merge_bwd(dhb, w_out_w, s["o"], s["yg"], s["ga"], s["gs"], w3)
    (dw_ap,), (dw_ap_other,) = _tn_cols(s["o"], [dattn], "mix_dw_ap")
    (dw_gv, dw_gg), (dw_gv_other, dw_gg_other) = _tn_cols(s["yg"], [dv, dg], "mix_dw_glu")
    do, dy = _branch_bwd(dattn, dv, dg, w3, s["y"])
    (dq, dk, dvv, dkm, dvm, dsink), _ = _attn_bwd(s["q"], s["k"], s["v"], do, p["attn_sinks"], tabs)
    g["attn_sinks"] = dsink[:, 0]
    (du, dlr, dli, dbr, dbi, dcr, dci, dd), _ = _ssm_bwd(s["u"], dy, *p["ssm_tabs"], p["ssm_d"])
    ngrp = p["ssm_d"].shape[0] // SSM_GROUP
    g["ssm_lam"] = (dlr.reshape(ngrp, SSM_STATE), dli.reshape(ngrp, SSM_STATE),
                    _ssm_untable_b(dbr, ngrp), _ssm_untable_b(dbi, ngrp))
    g["ssm_c_re"] = _ssm_untable_c(dcr, ngrp)
    g["ssm_c_im"] = _ssm_untable_c(dci, ngrp)
    g["ssm_d"] = dd[0]
    dk = dk.at[:BLOCK].add(dkm)
    dvv = dvv.at[:BLOCK].add(dvm)
    dz = jnp.concatenate([dq.astype(BF16), dk.astype(BF16), dvv.astype(BF16), du.astype(BF16), dga, dgs], axis=1)
    n = s["n_mix"]
    w_in = _w_in_full(p, l, ws)
    dw_in, dw_in_other = _tn_full(dz, n, "mix_dw_in", d // 2)
    red_mix = _Reduction([dw_in, dw_ap, dw_gv, dw_gg, dw_out],
                         [dw_in_other, dw_ap_other, dw_gv_other, dw_gg_other, dw_out_other], pos, "mix")
    dh1, g["mix_norm"], dh1b = _in_proj_bwd(dz, w_in, dh2, s["h1"], p["mix_norm"])
    dh0_pair, g["ffn1_norm"], red_ffn1, red_mix.got = _ffn_bwd(
        (dh1, dh1b), s["h0"], p["ffn1_norm"], ws.get((l, "wg1")), ws.get((l, "wu1")), ws.get((l, "wd1")), p["f4"],
        s["ffn1"], pos, red_mix.comm())
    return dh0_pair, g, [*red_ffn1, red_mix, *red_ffn2]


BIG = ["ffn1_w_gate", "ffn1_w_up", "ffn1_w_down", "w_in", "w_attn_proj", "w_glu_v", "w_glu_g", "w_out",
       "ffn2_w_gate", "ffn2_w_up", "ffn2_w_down"]
TRANSPOSED = ["ffn1_w_gate", "ffn1_w_up", "w_in", "ffn2_w_gate", "ffn2_w_up"]
SMALL = ["ffn1_norm", "mix_norm", "attn_sinks", "ssm_a_re", "ssm_a_im", "ssm_log_dt", "ssm_b_re", "ssm_b_im",
         "ssm_c_re", "ssm_c_im", "ssm_d", "ffn2_norm", "final_norm"]
WEIGHTS = ["meta_tokens", "ffn1_norm", "ffn1_w_gate", "ffn1_w_up", "ffn1_w_down", "mix_norm", "w_in", "attn_sinks",
           "ssm_a_re", "ssm_a_im", "ssm_log_dt", "ssm_b_re", "ssm_b_im", "ssm_c_re", "ssm_c_im", "ssm_d",
           "w_attn_proj", "w_glu_v", "w_glu_g", "w_out", "ffn2_norm", "ffn2_w_gate", "ffn2_w_up", "ffn2_w_down",
           "final_norm"]


def _small_rows(shape):
    rows = -(-math.prod(shape) // LANES)
    return -(-rows // SUBLANES) * SUBLANES


def _pack_small(tree):
    parts = []
    for k in SMALL + ["meta_tokens"]:
        size, rows = math.prod(tree[k].shape), _small_rows(tree[k].shape)
        if size % LANES == 0:
            part = tree[k].reshape(size // LANES, LANES)
        else:
            part = jnp.pad(tree[k].reshape(1, size), ((0, 0), (0, LANES - size)))
        parts.append(jnp.pad(part, ((0, rows - part.shape[0]), (0, 0))))
    return jnp.concatenate(parts, axis=0)


def _unpack_small(packed, like):
    out, off = {}, 0
    for k in SMALL + ["meta_tokens"]:
        size, rows = math.prod(like[k].shape), _small_rows(like[k].shape)
        if size % LANES == 0:
            out[k] = packed[off:off + size // LANES].reshape(like[k].shape)
        else:
            out[k] = packed[off, :size].reshape(like[k].shape)
        off += rows
    return out


def kernel(x, meta_tokens, ffn1_norm, ffn1_w_gate, ffn1_w_up, ffn1_w_down, mix_norm, w_in, attn_sinks, ssm_a_re, ssm_a_im, ssm_log_dt, ssm_b_re, ssm_b_im, ssm_c_re, ssm_c_im, ssm_d, w_attn_proj, w_glu_v, w_glu_g, w_out, ffn2_norm, ffn2_w_gate, ffn2_w_up, ffn2_w_down, final_norm, loss_target, m_meta_tokens, m_ffn1_norm, m_ffn1_w_gate, m_ffn1_w_up, m_ffn1_w_down, m_mix_norm, m_w_in, m_attn_sinks, m_ssm_a_re, m_ssm_a_im, m_ssm_log_dt, m_ssm_b_re, m_ssm_b_im, m_ssm_c_re, m_ssm_c_im, m_ssm_d, m_w_attn_proj, m_w_glu_v, m_w_glu_g, m_w_out, m_ffn2_norm, m_ffn2_w_gate, m_ffn2_w_up, m_ffn2_w_down, m_final_norm, v_meta_tokens, v_ffn1_norm, v_ffn1_w_gate, v_ffn1_w_up, v_ffn1_w_down, v_mix_norm, v_w_in, v_attn_sinks, v_ssm_a_re, v_ssm_a_im, v_ssm_log_dt, v_ssm_b_re, v_ssm_b_im, v_ssm_c_re, v_ssm_c_im, v_ssm_d, v_w_attn_proj, v_w_glu_v, v_w_glu_g, v_w_out, v_ffn2_norm, v_ffn2_w_gate, v_ffn2_w_up, v_ffn2_w_down, v_final_norm):
    args = dict(locals())
    w = {k: args[k] for k in WEIGHTS}
    m = {k: args["m_" + k] for k in WEIGHTS}
    v = {k: args["v_" + k] for k in WEIGHTS}
    depth = ffn1_norm.shape[0]
    seq, d = x.shape[1], x.shape[2]
    lp = seq + BLOCK
    xi, yi, ci = _mesh_pos()
    pos = jnp.stack([ci, 2 * xi + yi]).astype(jnp.int32)

    tabs = _rope_tables(lp)
    layers, pieces = [], [((0, "meta"), meta_tokens)]
    f4 = ffn1_w_gate.shape[2]
    fp = -(-f4 // MXU_DIM) * MXU_DIM

    def ffn_rows(wt):
        return jnp.pad(wt, ((0, fp - f4), (0, 0))).astype(BF16)

    for l in range(depth):
        small = [((l, "w3"), jnp.stack([w_attn_proj[l].T, w_glu_v[l].T, w_glu_g[l].T]).astype(BF16)),
                 ((l, "w_out"), w_out[l].astype(BF16))]
        first = [((l, "wg1"), ffn_rows(ffn1_w_gate[l].T)), ((l, "wu1"), ffn_rows(ffn1_w_up[l].T)),
                 ((l, "wd1"), ffn_rows(ffn1_w_down[l])), ((l, "w_in"), w_in[l].T.astype(BF16))]
        pieces += (first + small if l == 0 else small + first) + [
            ((l, "wg2"), ffn_rows(ffn2_w_gate[l].T)), ((l, "wu2"), ffn_rows(ffn2_w_up[l].T)),
            ((l, "wd2"), ffn_rows(ffn2_w_down[l]))]
        lb_re, lb_im, bb_re, bb_im = _ssm_params(ssm_a_re[l], ssm_a_im[l], ssm_log_dt[l], ssm_b_re[l], ssm_b_im[l])
        ngrp = lb_re.shape[0]
        nt = ngrp // GROUPS_PER_TILE
        ssm_tabs = (lb_re.reshape(nt, 1, TILE_STATES), lb_im.reshape(nt, 1, TILE_STATES),
                    *_ssm_tables(bb_re, bb_im, ssm_c_re[l], ssm_c_im[l]))
        layers.append(dict(
            ffn1_norm=ffn1_norm[l], mix_norm=mix_norm[l], ffn2_norm=ffn2_norm[l], attn_sinks=attn_sinks[l],
            ssm_d=ssm_d[l], ssm_tabs=ssm_tabs, f4=f4))
    ws = _WeightStream(pieces)
    ws.get((0, "wu1"))
    meta_all = ws.get((0, "meta"))
    meta_full = jnp.concatenate([meta_all[j] for j in range(N_CHIPS)], axis=1)

    h = jnp.concatenate([jnp.zeros((PAD_FRONT, d), F32), meta_full, x[0]], axis=0)
    saved = []
    n0 = _rms_fwd(h, ffn1_norm[0], "rms_fwd_first")
    for l in range(depth):
        next_gain = ffn1_norm[l + 1] if l + 1 < depth else None
        h, n0, s = _layer_fwd(h, n0, l, layers[l], next_gain, ws, tabs)
        saved.append(s)
    dh, g_final, loss_acc, dhb = _loss_head(h, final_norm, loss_target[0])
    dh_pair = (dh, dhb)
    loss = lax.psum(loss_acc[0, 0], ("x", "y", "c"))

    grads, reds = [None] * depth, [None] * depth
    for l in reversed(range(depth)):
        dh_pair, grads[l], reds[l] = _layer_bwd(dh_pair, l, layers[l], ws, saved[l], tabs, pos)
    dh = dh_pair[0]
    grad_x = dh[BLOCK:][None]
    dmeta_local = dh[PAD_FRONT:BLOCK]

    small = {k: [] for k in SMALL}
    for l in range(depth):
        gl = grads[l]
        _, vjp = jax.vjp(_ssm_params, ssm_a_re[l], ssm_a_im[l], ssm_log_dt[l], ssm_b_re[l], ssm_b_im[l])
        da_re, da_im, dlog_dt, db_re, db_im = vjp(gl["ssm_lam"])
        for k, val in (("ffn1_norm", gl["ffn1_norm"][0]), ("mix_norm", gl["mix_norm"][0]),
                       ("attn_sinks", gl["attn_sinks"]), ("ssm_a_re", da_re), ("ssm_a_im", da_im),
                       ("ssm_log_dt", dlog_dt), ("ssm_b_re", db_re), ("ssm_b_im", db_im),
                       ("ssm_c_re", gl["ssm_c_re"]), ("ssm_c_im", gl["ssm_c_im"]), ("ssm_d", gl["ssm_d"]),
                       ("ffn2_norm", gl["ffn2_norm"][0])):
            small[k].append(val)
    small_local = {k: jnp.stack(vals) for k, vals in small.items() if k != "final_norm"}
    small_local["final_norm"] = g_final[0]
    small_local["meta_tokens"] = dmeta_local
    like = dict(small_local)
    g_small = _sum_devices(_all_gather_devices(_pack_small(small_local), "gather_small_grads"), "sum_small_grads")
    g_small_tree = _unpack_small(g_small, like)
    d4 = d // N_CHIPS
    chip = 2 * xi + yi
    g_meta = lax.dynamic_slice_in_dim(g_small_tree["meta_tokens"], chip * d4, d4, axis=1)

    reduced = []
    for l in range(depth):
        mine, other = [], []
        for red in reds[l]:
            halves, sibling_halves = red.end()
            mine += halves
            other += sibling_halves
        reduced.append((mine, other))

    g_out, delta, new_m, new_v = {}, {}, {}, {}
    for i, k in enumerate(BIG):
        flip = (lambda t: jnp.swapaxes(t, 1, 2)) if k in TRANSPOSED else (lambda t: t)
        outs = _adamw_layers(
            flip(w[k]), flip(m[k]), flip(v[k]), [reduced[l][0][i] for l in range(depth)],
            [reduced[l][1][i] for l in range(depth)], pos, "adamw_" + k)
        g_out[k], delta[k], new_m[k], new_v[k] = [flip(t) for t in outs]
    g_small_tree["meta_tokens"] = g_meta
    for k in SMALL + ["meta_tokens"]:
        shape = w[k].shape if w[k].ndim > 1 else (1,) + w[k].shape
        outs = _adamw_whole(w[k].reshape(shape), g_small_tree[k].reshape(shape), m[k].reshape(shape),
                            v[k].reshape(shape), "adamw_" + k)
        g_out[k] = g_small_tree[k]
        delta[k], new_m[k], new_v[k] = [t.reshape(w[k].shape) for t in outs]

    return (loss, grad_x, *[g_out[k] for k in WEIGHTS], *[delta[k] for k in WEIGHTS],
            *[new_m[k] for k in WEIGHTS], *[new_v[k] for k in WEIGHTS])
```

```python
import functools
import math

import jax
import jax.numpy as jnp
from jax import lax
from jax.experimental import pallas as pl
from jax.experimental.pallas import tpu as pltpu

F32 = jnp.float32
BF16 = jnp.bfloat16

N_META = 16
HEAD_DIM = 64
N_Q_HEADS = 8
N_KV_HEADS = 2
Q_PER_KV = N_Q_HEADS // N_KV_HEADS
ATTN_WIDTH = N_Q_HEADS * HEAD_DIM
KV_WIDTH = N_KV_HEADS * HEAD_DIM
BLOCK = 128
PAD_FRONT = BLOCK - N_META
ROPE_THETA = 500000.0
ROT_DIM = HEAD_DIM // 4
SSM_GROUP = 16
SSM_STATE = 64
GROUPS_PER_TILE = 4
TILE_STATES = GROUPS_PER_TILE * SSM_STATE
LANES = 128
SUBLANES = 8
MXU_DIM = 256
EPS = 1e-6
NEG_INF = -1e30
N_CHIPS = 4

ADAM_LR = 0.001
ADAM_B1 = 0.9
ADAM_B2 = 0.999
ADAM_EPS = 1e-08
ADAM_WD = 0.01
ADAM_STEP = 10

VMEM_LIMIT = 56 * 1024 * 1024
MESH = pl.DeviceIdType.MESH


def _cparams(sem=None):
    return pltpu.CompilerParams(dimension_semantics=sem, vmem_limit_bytes=VMEM_LIMIT)


def _row_tile(rows, limit=512):
    best = None
    for t in range(128, limit + 1, 128):
        if rows % t == 0:
            best = t
    assert best is not None, rows
    return best


def _div_tile(rows, row_bytes, max_bytes=1 << 20, mult=8):
    best = None
    for t in range(mult, rows + 1, mult):
        if rows % t == 0 and t * row_bytes <= max_bytes:
            best = t
    if best is None:
        best = rows
    return best


def _dot(a, b, mode):
    if mode == "nn":
        dims = (((1,), (0,)), ((), ()))
    elif mode == "nt":
        dims = (((1,), (1,)), ((), ()))
    else:
        dims = (((0,), (0,)), ((), ()))
    return lax.dot_general(a.astype(BF16), b.astype(BF16), dims, preferred_element_type=F32)


def _sigmoid(x):
    return 1.0 / (1.0 + jnp.exp(-x))


_GELU_C = math.sqrt(2.0 / math.pi)


def _gelu(x):
    return 0.5 * x * (1.0 + jnp.tanh(_GELU_C * (x + 0.044715 * x * x * x)))


def _gelu_grad(x):
    t = jnp.tanh(_GELU_C * (x + 0.044715 * x * x * x))
    return 0.5 * (1.0 + t) + 0.5 * x * (1.0 - t * t) * _GELU_C * (1.0 + 3.0 * 0.044715 * x * x)


class _Comm:
    def __init__(self, tag, ins, out_shapes, sems, start, mid, finish):
        self.tag, self.ins, self.out_shapes, self.sems = tag, list(ins), list(out_shapes), list(sems)
        self.start, self.mid, self.finish = start, mid, finish


HBM_SPEC = pl.BlockSpec(memory_space=pltpu.HBM)
MID_NUM, MID_DEN = 4, 5


def _hosted_call(body, comm, *, out_shape, grid, in_specs, out_specs, scratch_shapes, sem, name, args):
    out_shape, in_specs, out_specs = list(out_shape), list(in_specs), list(out_specs)
    scratch_shapes = list(scratch_shapes)
    if comm is None:
        res = pl.pallas_call(
            body, out_shape=out_shape, grid=grid, in_specs=in_specs, out_specs=out_specs,
            scratch_shapes=scratch_shapes, compiler_params=_cparams(sem), name=name)(*args)
        return list(res), []
    n_in, n_out, n_sc = len(args), len(out_shape), len(scratch_shapes)
    nci, nco = len(comm.ins), len(comm.out_shapes)
    total = math.prod(grid)

    def wrapped(*refs):
        in_refs, cin = refs[:n_in], refs[n_in:n_in + nci]
        o0 = n_in + nci
        out_refs, cout = refs[o0:o0 + n_out], refs[o0 + n_out:o0 + n_out + nco]
        s0 = o0 + n_out + nco
        sc, csem = refs[s0:s0 + n_sc], refs[s0 + n_sc:]
        lin = 0
        for dim, size in enumerate(grid):
            lin = lin * size + pl.program_id(dim)

        @pl.when(lin == 0)
        def _():
            comm.start(cin, cout, csem)

        if comm.mid is not None:
            @pl.when(lin == (total * MID_NUM) // MID_DEN)
            def _():
                comm.mid(cin, cout, csem)

        body(*in_refs, *out_refs, *sc)

        @pl.when(lin == total - 1)
        def _():
            comm.finish(cin, cout, csem)

    res = pl.pallas_call(
        wrapped, out_shape=out_shape + comm.out_shapes, grid=grid,
        in_specs=in_specs + [HBM_SPEC] * nci, out_specs=out_specs + [HBM_SPEC] * nco,
        scratch_shapes=scratch_shapes + comm.sems,
        compiler_params=_cparams(("arbitrary",) * len(grid)), name=name + "_" + comm.tag)(*args, *comm.ins)
    return list(res[:n_out]), list(res[n_out:])


def _matmul(name, grid, k_axis, ins, in_specs, pairs, acc_shapes, epilogue, out_shapes, out_specs, sem, comm=None):
    n_in, n_out, n_acc = len(ins), len(out_shapes), len(acc_shapes)

    def body(*refs):
        in_refs = refs[:n_in]
        out_refs = refs[n_in:n_in + n_out]
        acc_refs = refs[n_in + n_out:]
        if k_axis is None:
            accs = [None] * n_acc
            for ia, ib, mode, iacc in pairs:
                d = _dot(in_refs[ia][...], in_refs[ib][...], mode)
                accs[iacc] = d if accs[iacc] is None else accs[iacc] + d
            epilogue(accs, in_refs, out_refs)
            return
        k = pl.program_id(k_axis)

        @pl.when(k == 0)
        def _():
            for r in acc_refs:
                r[...] = jnp.zeros_like(r)

        for ia, ib, mode, iacc in pairs:
            acc_refs[iacc][...] += _dot(in_refs[ia][...], in_refs[ib][...], mode)

        @pl.when(k == pl.num_programs(k_axis) - 1)
        def _():
            epilogue([r[...] for r in acc_refs], in_refs, out_refs)

    scratch = [] if k_axis is None else [pltpu.VMEM(s, F32) for s in acc_shapes]
    outs, couts = _hosted_call(
        body, comm, out_shape=out_shapes, grid=grid, in_specs=in_specs, out_specs=out_specs,
        scratch_shapes=scratch, sem=sem, name=name, args=ins)
    return outs if comm is None else (outs, couts)


def _rms_math(x, g):
    r = lax.rsqrt(jnp.mean(x * x, axis=-1, keepdims=True) + EPS)
    return (x * r * g).astype(BF16)


def _rms_fwd(h, g, name):
    lp, d = h.shape
    tm = _row_tile(lp)

    def body(h_ref, g_ref, n_ref):
        n_ref[...] = _rms_math(h_ref[...], g_ref[...])

    return pl.pallas_call(
        body, out_shape=jax.ShapeDtypeStruct((lp, d), BF16), grid=(lp // tm,),
        in_specs=[pl.BlockSpec((tm, d), lambda i: (i, 0)), pl.BlockSpec((1, d), lambda i: (0, 0))],
        out_specs=pl.BlockSpec((tm, d), lambda i: (i, 0)),
        compiler_params=_cparams(("parallel",)), name=name)(h, g.reshape(1, d))


def _rms_bwd_math(dn, x, g):
    r = lax.rsqrt(jnp.mean(x * x, axis=-1, keepdims=True) + EPS)
    xh = x * r
    dxh = dn * g
    dx = r * (dxh - xh * jnp.mean(dxh * xh, axis=-1, keepdims=True))
    return dx, dn * xh


def _ffn_up(n, wgt, wut, comm=None):
    lp, d = n.shape
    fp = wgt.shape[1]
    tm = _row_tile(lp)

    def up_body(n_ref, wg_ref, wu_ref, a_ref, b_ref, s_ref):
        x = n_ref[...]
        for jc in range(N_CHIPS):
            cols = slice(jc * fp, (jc + 1) * fp)
            a = _dot(x, wg_ref[jc], "nt")
            b = _dot(x, wu_ref[jc], "nt")
            a_ref[:, cols] = a.astype(BF16)
            b_ref[:, cols] = b.astype(BF16)
            s_ref[:, cols] = (a * _sigmoid(a) * b).astype(BF16)

    ff = N_CHIPS * fp
    act = jax.ShapeDtypeStruct((lp, ff), BF16)
    act_tile = pl.BlockSpec((tm, ff), lambda i: (i, 0))
    w_spec = pl.BlockSpec((N_CHIPS, fp, d), lambda i: (0, 0, 0))
    outs, couts = _hosted_call(
        up_body, comm, out_shape=[act, act, act], grid=(lp // tm,),
        in_specs=[pl.BlockSpec((tm, d), lambda i: (i, 0)), w_spec, w_spec],
        out_specs=[act_tile] * 3, scratch_shapes=[], sem=("parallel",), name="ffn_up", args=(n, wgt, wut))
    return (*outs, n), couts


def _residual_outputs(h_new, in_refs, out_refs, gain_at):
    out_refs[0][...] = h_new
    if gain_at is not None:
        out_refs[1][...] = _rms_math(h_new, in_refs[gain_at][...])


def _residual_specs(lp, d, tm, next_gain):
    row = pl.BlockSpec((tm, d), lambda i: (i, 0))
    shapes, specs = [jax.ShapeDtypeStruct((lp, d), F32)], [row]
    extra_in, extra_specs = [], []
    if next_gain is not None:
        shapes.append(jax.ShapeDtypeStruct((lp, d), BF16))
        specs.append(row)
        extra_in, extra_specs = [next_gain.reshape(1, d)], [pl.BlockSpec((1, d), lambda i: (0, 0))]
    return shapes, specs, extra_in, extra_specs


def _ffn_down(s, wd, h, next_gain, comm=None):
    lp, d = h.shape
    ff = s.shape[1]
    tm = _row_tile(lp)
    shapes, specs, extra_in, extra_specs = _residual_specs(lp, d, tm, next_gain)

    def down_epi(accs, in_refs, out_refs):
        _residual_outputs(in_refs[2][...] + 0.5 * accs[0], in_refs, out_refs, 3 if extra_in else None)

    res = _matmul(
        "ffn_down", (lp // tm,), None, [s, wd.reshape(ff, d), h] + extra_in,
        [pl.BlockSpec((tm, ff), lambda i: (i, 0)), pl.BlockSpec((ff, d), lambda i: (0, 0)),
         pl.BlockSpec((tm, d), lambda i: (i, 0))] + extra_specs,
        [(0, 1, "nn", 0)], [(tm, d)], down_epi, shapes, specs, ("parallel",), comm)
    return (res, []) if comm is None else res


def _tn_tiles(lp):
    return _row_tile(lp, 1408)


def _ffn_bwd(dh_pair, h_in, gain, wgt, wut, wd, f4, saved, pos, comm=None):
    dh, dhb = dh_pair
    a, b, s, n = saved
    lp, d = h_in.shape
    fp = wgt.shape[1]
    ff = N_CHIPS * fp
    tm = _row_tile(lp)
    ni = lp // tm
    tk = _tn_tiles(lp)
    nk = lp // tk

    def ds_body(dh_ref, wd_ref, a_ref, b_ref, da_ref, db_ref):
        x = dh_ref[...]
        for jc in range(N_CHIPS):
            cols = slice(jc * fp, (jc + 1) * fp)
            ds = 0.5 * _dot(x, wd_ref[jc], "nt")
            av = a_ref[:, cols].astype(F32)
            bv = b_ref[:, cols].astype(F32)
            sg = _sigmoid(av)
            da_ref[:, cols] = (ds * bv * sg * (1.0 + av * (1.0 - sg))).astype(BF16)
            db_ref[:, cols] = (ds * av * sg).astype(BF16)

    act = jax.ShapeDtypeStruct((lp, ff), BF16)
    act_tile = pl.BlockSpec((tm, ff), lambda i: (i, 0))
    (da, db), couts = _hosted_call(
        ds_body, comm, out_shape=[act, act], grid=(ni,),
        in_specs=[pl.BlockSpec((tm, d), lambda i: (i, 0)), pl.BlockSpec((N_CHIPS, fp, d), lambda i: (0, 0, 0)),
                  act_tile, act_tile],
        out_specs=[act_tile, act_tile], scratch_shapes=[], sem=("parallel",), name="ffn_bwd_ds",
        args=(dhb, wd, a, b))

    dw_shape = jax.ShapeDtypeStruct((N_CHIPS, f4, d), F32)
    dw_spec = pl.BlockSpec((None, f4, d), lambda j, k: (j, 0, 0))
    in_col = pl.BlockSpec((tk, fp), lambda j, k: (k, j))
    in_row = pl.BlockSpec((tk, d), lambda j, k: (k, 0))

    half_shape = jax.ShapeDtypeStruct((N_CHIPS, f4 // 2, d), BF16)
    half_spec = pl.BlockSpec((None, f4 // 2, d), lambda j, k: (j, 0, 0))

    def dwd_epi(accs, in_refs, out_refs):
        dw = 0.5 * accs[0]
        out_refs[0][...] = dw[:f4]
        out_refs[1][...] = _sibling_half(dw, f4)

    dwd, dwd_other = _matmul(
        "ffn_dwd", (N_CHIPS, nk), 1, [s, dhb], [in_col, in_row],
        [(0, 1, "tn", 0)], [(fp, d)], dwd_epi, [dw_shape, half_shape], [dw_spec, half_spec],
        ("arbitrary", "arbitrary"))

    def dwgu_epi(accs, in_refs, out_refs):
        for i, acc in enumerate(accs):
            out_refs[i][...] = acc[:f4]
            out_refs[2 + i][...] = _sibling_half(acc, f4)

    red_down = _Reduction([dwd], [dwd_other], pos, "ffn_d")
    (dwg, dwu, dwg_other, dwu_other), red_down.got = _matmul(
        "ffn_dwgu", (N_CHIPS, nk), 1, [n, da, db], [in_row, in_col, in_col],
        [(1, 0, "tn", 0), (2, 0, "tn", 1)], [(fp, d)] * 2, dwgu_epi,
        [dw_shape, dw_shape, half_shape, half_shape], [dw_spec, dw_spec, half_spec, half_spec],
        ("arbitrary", "arbitrary"), red_down.comm())

    def dn_epi(accs, in_refs, out_refs):
        i = pl.program_id(0)
        dx, dgrow = _rms_bwd_math(accs[0], in_refs[5][...], in_refs[6][...])
        dh_new = in_refs[4][...] + dx
        out_refs[0][...] = dh_new
        out_refs[2][...] = dh_new.astype(BF16)

        @pl.when(i == 0)
        def _():
            out_refs[1][...] = jnp.zeros_like(out_refs[1])

        out_refs[1][...] += jnp.sum(dgrow, axis=0, keepdims=True)

    red = _Reduction([dwg, dwu], [dwg_other, dwu_other], pos, "ffn_gu")
    row_spec = pl.BlockSpec((tm, d), lambda i: (i, 0))
    act_spec = pl.BlockSpec((tm, ff), lambda i: (i, 0))
    w_spec = pl.BlockSpec((ff, d), lambda i: (0, 0))
    one_spec = pl.BlockSpec((1, d), lambda i: (0, 0))
    (dh_in, dgain, dh_in_b), red.got = _matmul(
        "ffn_bwd_dn", (ni,), None, [da, wgt.reshape(ff, d), db, wut.reshape(ff, d), dh, h_in, gain.reshape(1, d)],
        [act_spec, w_spec, act_spec, w_spec, row_spec, row_spec, one_spec],
        [(0, 1, "nn", 0), (2, 3, "nn", 0)], [(tm, d)], dn_epi,
        [jax.ShapeDtypeStruct((lp, d), F32), jax.ShapeDtypeStruct((1, d), F32), jax.ShapeDtypeStruct((lp, d), BF16)],
        [row_spec, one_spec, row_spec], ("arbitrary",), red.comm())
    return (dh_in, dh_in_b), dgain, [red, red_down], couts


def _rope_tables(lp):
    pos = jnp.arange(lp, dtype=F32) - float(PAD_FRONT)
    inv_freq = ROPE_THETA ** (-jnp.arange(0, ROT_DIM, 2, dtype=F32) / ROT_DIM)
    ang = pos[:, None] * inv_freq[None, :]
    cos, sin = jnp.cos(ang), jnp.sin(ang)
    half = ROT_DIM // 2
    ones = jnp.ones((lp, HEAD_DIM - ROT_DIM), F32)
    zeros_h = jnp.zeros((lp, half), F32)
    zeros_r = jnp.zeros((lp, HEAD_DIM - ROT_DIM), F32)
    c = jnp.concatenate([cos, cos, ones], axis=1)
    s1 = jnp.concatenate([-sin, zeros_h, zeros_r], axis=1)
    s2 = jnp.concatenate([zeros_h, sin, zeros_r], axis=1)
    reps = LANES // HEAD_DIM
    return jnp.stack([jnp.tile(c, (1, reps)), jnp.tile(s1, (1, reps)), jnp.tile(s2, (1, reps))])


def _rope(x, c, s1, s2):
    half = ROT_DIM // 2
    outs = []
    for ch in range(x.shape[1] // LANES):
        xc = x[:, ch * LANES:(ch + 1) * LANES]
        outs.append(xc * c + pltpu.roll(xc, LANES - half, 1) * s1 + pltpu.roll(xc, half, 1) * s2)
    return outs[0] if len(outs) == 1 else jnp.concatenate(outs, axis=1)


def _rope_t(dy, c, s1, s2):
    half = ROT_DIM // 2
    outs = []
    for ch in range(dy.shape[1] // LANES):
        dc = dy[:, ch * LANES:(ch + 1) * LANES]
        outs.append(dc * c + pltpu.roll(dc * s1, half, 1) + pltpu.roll(dc * s2, LANES - half, 1))
    return outs[0] if len(outs) == 1 else jnp.concatenate(outs, axis=1)


def _in_proj(n, w_in, tabs, ssm_w, comm=None):
    lp, d = n.shape
    inw = w_in.shape[0]
    tm = _row_tile(lp)
    o1 = ATTN_WIDTH
    o2 = o1 + KV_WIDTH
    o3 = o2 + KV_WIDTH
    o4 = o3 + ssm_w
    o5 = o4 + d

    def epi(accs, in_refs, out_refs):
        z = accs[0]
        c, s1, s2 = in_refs[2][0], in_refs[2][1], in_refs[2][2]
        out_refs[0][...] = _rope(z[:, :o1], c, s1, s2).astype(BF16)
        out_refs[1][...] = _rope(z[:, o1:o2], c, s1, s2).astype(BF16)
        out_refs[2][...] = z[:, o2:o3].astype(BF16)
        out_refs[3][...] = z[:, o3:o4]
        out_refs[4][...] = z[:, o4:o5]
        out_refs[5][...] = z[:, o5:]

    def rs(w, dt):
        return jax.ShapeDtypeStruct((lp, w), dt), pl.BlockSpec((tm, w), lambda i: (i, 0))

    shapes, specs = zip(rs(o1, BF16), rs(KV_WIDTH, BF16), rs(KV_WIDTH, BF16), rs(ssm_w, F32), rs(d, F32), rs(d, F32))
    res = _matmul(
        "mix_in_proj", (lp // tm,), None, [n, w_in, tabs],
        [pl.BlockSpec((tm, d), lambda i: (i, 0)), pl.BlockSpec((inw, d), lambda i: (0, 0)),
         pl.BlockSpec((3, tm, LANES), lambda i: (0, i, 0))],
        [(0, 1, "nt", 0)], [(tm, inw)], epi, list(shapes), list(specs), ("parallel",), comm)
    return (res, []) if comm is None else res


def _attn_mask(b):
    rows = lax.broadcasted_iota(jnp.int32, (BLOCK, 3 * BLOCK), 0)
    cols = lax.broadcasted_iota(jnp.int32, (BLOCK, 3 * BLOCK), 1)
    qpos = b * BLOCK + rows - PAD_FRONT
    kpos = (b - 1) * BLOCK + cols - PAD_FRONT
    dist = qpos - kpos
    band = (cols < 2 * BLOCK) & (kpos >= N_META) & (dist >= 0) & (dist < BLOCK)
    mrow = cols - 2 * BLOCK
    meta = (mrow >= PAD_FRONT) & ((mrow - PAD_FRONT) <= qpos)
    return band | meta


def _attn_probs(qh, kk, mask, sink):
    s = _dot(qh, kk, "nt") * (HEAD_DIM ** -0.5)
    s = jnp.where(mask, s, NEG_INF)
    m = jnp.maximum(jnp.max(s, axis=-1, keepdims=True), sink)
    e = jnp.exp(s - m)
    es = jnp.exp(sink - m)
    z = jnp.sum(e, axis=-1, keepdims=True) + es
    inv = 1.0 / z
    return e * inv, es * inv


def _head(ref_or_val, h):
    return ref_or_val[:, h * HEAD_DIM:(h + 1) * HEAD_DIM]


def _attn_fwd(q, k, v, sinks, comm=None):
    lp = q.shape[0]
    nb = lp // BLOCK

    def body(sink_ref, q_ref, kp_ref, kc_ref, km_ref, vp_ref, vc_ref, vm_ref, o_ref):
        b = pl.program_id(0)
        mask = _attn_mask(b)
        for hk in range(N_KV_HEADS):
            kk = jnp.concatenate([_head(kp_ref, hk), _head(kc_ref, hk), _head(km_ref, hk)], axis=0)
            vv = jnp.concatenate([_head(vp_ref, hk), _head(vc_ref, hk), _head(vm_ref, hk)], axis=0)
            for g in range(Q_PER_KV):
                h = hk * Q_PER_KV + g
                p, _ = _attn_probs(_head(q_ref, h), kk, mask, sink_ref[h])
                o_ref[:, h * HEAD_DIM:(h + 1) * HEAD_DIM] = _dot(p, vv, "nn").astype(BF16)

    cur = lambda b: (b, 0)
    prev = lambda b: (jnp.maximum(b - 1, 0), 0)
    first = lambda b: (0, 0)
    kvs = lambda f: pl.BlockSpec((BLOCK, KV_WIDTH), f)
    (o,), couts = _hosted_call(
        body, comm, out_shape=[jax.ShapeDtypeStruct((lp, ATTN_WIDTH), BF16)], grid=(nb,),
        in_specs=[pl.BlockSpec(memory_space=pltpu.SMEM), pl.BlockSpec((BLOCK, ATTN_WIDTH), cur),
                  kvs(prev), kvs(cur), kvs(first), kvs(prev), kvs(cur), kvs(first)],
        out_specs=[pl.BlockSpec((BLOCK, ATTN_WIDTH), cur)], scratch_shapes=[],
        sem=("parallel",), name="attn_fwd", args=(sinks, q, k, k, k, v, v, v))
    return o, couts


def _attn_bwd(q, k, v, do, sinks, tabs, comm=None):
    lp = q.shape[0]
    nb = lp // BLOCK
    scale = HEAD_DIM ** -0.5

    def body(sink_ref, q_ref, do_ref, kp_ref, kc_ref, km_ref, vp_ref, vc_ref, vm_ref, tq_ref, tk_ref, t0_ref,
             dq_ref, dk_ref, dv_ref, dkm_ref, dvm_ref, dsink_ref,
             dq_s, dkk_s, dvv_s, ck_s, cv_s, mk_s, mv_s):
        b = pl.program_id(0)

        @pl.when(b == 0)
        def _():
            for r in (ck_s, cv_s, mk_s, mv_s, dsink_ref):
                r[...] = jnp.zeros_like(r)

        @pl.when(b < nb)
        def _():
            mask = _attn_mask(b)
            for hk in range(N_KV_HEADS):
                kk = jnp.concatenate([_head(kp_ref, hk), _head(kc_ref, hk), _head(km_ref, hk)], axis=0)
                vv = jnp.concatenate([_head(vp_ref, hk), _head(vc_ref, hk), _head(vm_ref, hk)], axis=0)
                dkk = jnp.zeros((3 * BLOCK, HEAD_DIM), F32)
                dvv = jnp.zeros((3 * BLOCK, HEAD_DIM), F32)
                for g in range(Q_PER_KV):
                    h = hk * Q_PER_KV + g
                    qh = _head(q_ref, h)
                    doh = _head(do_ref, h)
                    p, ps = _attn_probs(qh, kk, mask, sink_ref[h])
                    dp = _dot(doh, vv, "nt")
                    delta = jnp.sum(p * dp, axis=-1, keepdims=True)
                    ds = (p * (dp - delta)).astype(BF16)
                    dsink_ref[h:h + 1, :] += jnp.zeros((1, LANES), F32) - jnp.sum(ps * delta)
                    dq_s[:, h * HEAD_DIM:(h + 1) * HEAD_DIM] = _dot(ds, kk, "nn") * scale
                    dkk = dkk + _dot(ds, qh, "tn") * scale
                    dvv = dvv + _dot(p, doh, "tn")
                dkk_s[:, hk * HEAD_DIM:(hk + 1) * HEAD_DIM] = dkk
                dvv_s[:, hk * HEAD_DIM:(hk + 1) * HEAD_DIM] = dvv
            dq_ref[...] = _rope_t(dq_s[...], tq_ref[0], tq_ref[1], tq_ref[2])
            dk_ref[...] = _rope_t(ck_s[...] + dkk_s[0:BLOCK, :], tk_ref[0], tk_ref[1], tk_ref[2])
            dv_ref[...] = cv_s[...] + dvv_s[0:BLOCK, :]
            ck_s[...] = dkk_s[BLOCK:2 * BLOCK, :]
            cv_s[...] = dvv_s[BLOCK:2 * BLOCK, :]
            mk_s[...] += dkk_s[2 * BLOCK:, :]
            mv_s[...] += dvv_s[2 * BLOCK:, :]

        @pl.when(b == nb)
        def _():
            dk_ref[...] = _rope_t(ck_s[...], tk_ref[0], tk_ref[1], tk_ref[2])
            dv_ref[...] = cv_s[...]
            dkm_ref[...] = _rope_t(mk_s[...], t0_ref[0], t0_ref[1], t0_ref[2])
            dvm_ref[...] = mv_s[...]

    cur = lambda b: (jnp.minimum(b, nb - 1), 0)
    prev = lambda b: (jnp.clip(b - 1, 0, nb - 1), 0)
    first = lambda b: (0, 0)
    kvs = lambda f: pl.BlockSpec((BLOCK, KV_WIDTH), f)
    tab = lambda f: pl.BlockSpec((3, BLOCK, LANES), lambda b: (0,) + f(b)[:1] + (0,))
    kv_out = lambda b: (jnp.maximum(b - 1, 0), 0)
    return _hosted_call(
        body, comm,
        out_shape=[jax.ShapeDtypeStruct((lp, ATTN_WIDTH), F32), jax.ShapeDtypeStruct((lp, KV_WIDTH), F32),
                   jax.ShapeDtypeStruct((lp, KV_WIDTH), F32), jax.ShapeDtypeStruct((BLOCK, KV_WIDTH), F32),
                   jax.ShapeDtypeStruct((BLOCK, KV_WIDTH), F32), jax.ShapeDtypeStruct((N_Q_HEADS, LANES), F32)],
        grid=(nb + 1,),
        in_specs=[pl.BlockSpec(memory_space=pltpu.SMEM), pl.BlockSpec((BLOCK, ATTN_WIDTH), cur),
                  pl.BlockSpec((BLOCK, ATTN_WIDTH), cur),
                  kvs(prev), kvs(cur), kvs(first), kvs(prev), kvs(cur), kvs(first),
                  tab(cur), tab(kv_out), tab(first)],
        out_specs=[pl.BlockSpec((BLOCK, ATTN_WIDTH), cur), kvs(kv_out), kvs(kv_out), kvs(first), kvs(first),
                   pl.BlockSpec((N_Q_HEADS, LANES), first)],
        scratch_shapes=[pltpu.VMEM((BLOCK, ATTN_WIDTH), F32), pltpu.VMEM((3 * BLOCK, KV_WIDTH), F32),
                        pltpu.VMEM((3 * BLOCK, KV_WIDTH), F32), pltpu.VMEM((BLOCK, KV_WIDTH), F32),
                        pltpu.VMEM((BLOCK, KV_WIDTH), F32), pltpu.VMEM((BLOCK, KV_WIDTH), F32),
                        pltpu.VMEM((BLOCK, KV_WIDTH), F32)],
        sem=("arbitrary",), name="attn_bwd", args=(sinks, q, do, k, k, k, v, v, v, tabs, tabs, tabs))


def _cmul(ar, ai, br, bi):
    return ar * br - ai * bi, ar * bi + ai * br


def _cpow(lr, li, n):
    rr = ri = None
    br, bi = lr, li
    while n:
        if n & 1:
            rr, ri = (br, bi) if rr is None else _cmul(rr, ri, br, bi)
        n >>= 1
        if n:
            br, bi = _cmul(br, bi, br, bi)
    return rr, ri


def _shift_rows(x, d, reverse):
    rows = lax.broadcasted_iota(jnp.int32, x.shape, 0)
    if not reverse:
        return jnp.where(rows >= d, pltpu.roll(x, d, 0), 0.0)
    return jnp.where(rows < SUBLANES - d, pltpu.roll(x, SUBLANES - d, 0), 0.0)


def _sublane_powers(mr, mi, reverse):
    rows = lax.broadcasted_iota(jnp.int32, mr.shape, 0)
    e = SUBLANES - 1 - rows if reverse else rows
    pr, pi = jnp.ones_like(mr), jnp.zeros_like(mr)
    br, bi = mr, mi
    for d in (1, 2, 4):
        tr, ti = _cmul(pr, pi, br, bi)
        on = (e & d) != 0
        pr, pi = jnp.where(on, tr, pr), jnp.where(on, ti, pi)
        if d < 4:
            br, bi = _cmul(br, bi, br, bi)
    return pr, pi


def _inclusive_prefix(er, ei, mr, mi, reverse):
    ir, ii, pr, pi = er, ei, mr, mi
    for d in (1, 2, 4):
        tr, ti = _cmul(pr, pi, _shift_rows(ir, d, reverse), _shift_rows(ii, d, reverse))
        ir, ii = ir + tr, ii + ti
        if d < 4:
            pr, pi = _cmul(pr, pi, pr, pi)
    return ir, ii


def _chain_rows(a, t, seg):
    return pl.ds(a * SUBLANES * seg + t, SUBLANES, stride=seg)


def _seg_scan(xr_ref, xi_ref, lam, seg, nchain, reverse, store, init, extra=None):
    nt = len(lam)
    acc0 = () if extra is None else extra[1]

    def step(i, carry):
        hs, acc = carry
        t = seg - 1 - i if reverse else i
        out = []
        for a in range(nchain):
            sl = _chain_rows(a, t, seg)
            for j in range(nt):
                lr, li = lam[j]
                k = 2 * (a * nt + j)
                hr, hi = hs[k], hs[k + 1]
                nr = lr * hr - li * hi + xr_ref[j, sl, :]
                ni = lr * hi + li * hr + xi_ref[j, sl, :]
                if store:
                    xr_ref[j, sl, :] = nr
                    xi_ref[j, sl, :] = ni
                if extra is not None:
                    acc = extra[0](t, a, j, nr, ni, acc)
                out += [nr, ni]
        return tuple(out), acc

    return lax.fori_loop(0, seg, step, (tuple(init), acc0))


def _ssm_scan(xr_ref, xi_ref, lam, seg, nchain, reverse, extra=None):
    nt = len(lam)
    zero = [jnp.zeros((SUBLANES, LANES), F32)] * (2 * nt * nchain)
    ends, _ = _seg_scan(xr_ref, xi_ref, lam, seg, nchain, reverse, False, zero)
    init = [None] * (2 * nt * nchain)
    last = 0 if reverse else SUBLANES - 1
    for j in range(nt):
        mr, mi = _cpow(lam[j][0], lam[j][1], seg)
        m8r, m8i = _cpow(mr, mi, SUBLANES)
        pwr, pwi = _sublane_powers(mr, mi, reverse)
        gr = gi = jnp.zeros((SUBLANES, LANES), F32)
        for a in (reversed(range(nchain)) if reverse else range(nchain)):
            k = 2 * (a * nt + j)
            incr, inci = _inclusive_prefix(ends[k], ends[k + 1], mr, mi, reverse)
            tr, ti = _cmul(pwr, pwi, gr, gi)
            init[k] = _shift_rows(incr, 1, reverse) + tr
            init[k + 1] = _shift_rows(inci, 1, reverse) + ti
            g2r, g2i = _cmul(m8r, m8i, gr, gi)
            gr = g2r + jnp.broadcast_to(incr[last:last + 1, :], gr.shape)
            gi = g2i + jnp.broadcast_to(inci[last:last + 1, :], gi.shape)
    _, acc = _seg_scan(xr_ref, xi_ref, lam, seg, nchain, reverse, True, init, extra)
    return acc


def _diag_mask():
    steps = LANES // SSM_GROUP // GROUPS_PER_TILE
    return (jnp.eye(steps, dtype=F32)[:, None, :, None] * jnp.eye(GROUPS_PER_TILE, dtype=F32)[None, :, None, :])


def _ssm_tables(bb_re, bb_im, c_re, c_im):
    g = bb_re.shape[0]
    nt = g // GROUPS_PER_TILE
    steps = LANES // SSM_GROUP // GROUPS_PER_TILE
    mask = _diag_mask()

    def b_tab(bb):
        x = bb.reshape(nt // steps, steps, GROUPS_PER_TILE, SSM_STATE, SSM_GROUP)
        x = jnp.transpose(x, (0, 1, 4, 2, 3))[:, :, None, None]
        m = jnp.transpose(mask, (0, 2, 3, 1))[None, :, :, :, None, :, None]
        return (x * m).reshape(nt, LANES, TILE_STATES)

    def c_tab(c):
        x = c.reshape(nt // steps, steps, GROUPS_PER_TILE, SSM_GROUP, SSM_STATE)
        x = jnp.transpose(x, (0, 1, 2, 4, 3))[:, :, :, :, None, None]
        m = mask[None, :, :, None, :, :, None]
        return (x * m).reshape(nt, TILE_STATES, LANES)

    return b_tab(bb_re), b_tab(bb_im), c_tab(c_re), c_tab(c_im)


def _ssm_untable_b(db, g):
    nt = g // GROUPS_PER_TILE
    steps = LANES // SSM_GROUP // GROUPS_PER_TILE
    x = db.reshape(nt // steps, steps, GROUPS_PER_TILE, SSM_STATE, steps, GROUPS_PER_TILE, SSM_GROUP)
    m = _diag_mask()[None, :, :, None, :, :, None]
    return jnp.sum(x * m, axis=(4, 5)).reshape(g, SSM_STATE, SSM_GROUP)


def _ssm_untable_c(dc, g):
    nt = g // GROUPS_PER_TILE
    steps = LANES // SSM_GROUP // GROUPS_PER_TILE
    x = dc.reshape(nt // steps, steps, steps, GROUPS_PER_TILE, SSM_GROUP, GROUPS_PER_TILE, SSM_STATE)
    m = jnp.transpose(_diag_mask(), (0, 2, 3, 1))[None, :, :, :, None, :, None]
    out = jnp.sum(x * m, axis=(2, 3))
    return jnp.transpose(out, (0, 1, 3, 2, 4)).reshape(g, SSM_GROUP, SSM_STATE)


def _lam_tiles(lam_ref):
    out = []
    for j in range(TILE_STATES // LANES):
        out.append(jnp.broadcast_to(lam_ref[:, j * LANES:(j + 1) * LANES], (SUBLANES, LANES)))
    return out


def _scan_chains(lp):
    for n in (4, 2, 1):
        if lp % (SUBLANES * n) == 0 and (lp // SUBLANES) % 16 == 0:
            return n
    raise ValueError(lp)


def _split_tiles(dst_ref, rows, val):
    for j in range(val.shape[1] // LANES):
        dst_ref[j, rows, :] = val[:, j * LANES:(j + 1) * LANES]


def _cat_tiles(src_ref, rows):
    njt = src_ref.shape[0]
    return jnp.concatenate([src_ref[j, rows, :] for j in range(njt)], axis=1).astype(BF16)


def _ssm_fwd(u, lam_re, lam_im, tb_re, tb_im, tc_re, tc_im, d_skip, comm=None):
    lp, w = u.shape
    nt = tb_re.shape[0]
    nchain = _scan_chains(lp)
    seg = lp // (SUBLANES * nchain)
    chunk = lp // SUBLANES
    njt = TILE_STATES // LANES

    def body(u_ref, lr_ref, li_ref, br_ref, bi_ref, cr_ref, ci_ref, d_ref, y_ref, yg_ref, xr, xi):
        t = pl.program_id(0)
        for s in range(SUBLANES):
            rs = pl.ds(s * chunk, chunk)
            ub = u_ref[rs, :].astype(BF16)
            _split_tiles(xr, rs, _dot(ub, br_ref[...], "nn"))
            _split_tiles(xi, rs, _dot(ub, bi_ref[...], "nn"))
        lrs, lis = _lam_tiles(lr_ref), _lam_tiles(li_ref)
        _ssm_scan(xr, xi, list(zip(lrs, lis)), seg, nchain, False)
        for s in range(SUBLANES):
            rs = pl.ds(s * chunk, chunk)
            y = _dot(_cat_tiles(xr, rs), cr_ref[...], "nn") - _dot(_cat_tiles(xi, rs), ci_ref[...], "nn")

            @pl.when(t % 2 == 0)
            def _():
                y_ref[rs, :] = y + d_ref[...] * u_ref[rs, :]

            @pl.when(t % 2 == 1)
            def _():
                total = y_ref[rs, :] + y
                y_ref[rs, :] = total
                yg_ref[rs, :] = _gelu(total).astype(BF16)

    blk = pl.BlockSpec((lp, LANES), lambda t: (0, t // 2))
    lam_spec = pl.BlockSpec((None, 1, TILE_STATES), lambda t: (t, 0, 0))
    b_spec = pl.BlockSpec((None, LANES, TILE_STATES), lambda t: (t, 0, 0))
    c_spec = pl.BlockSpec((None, TILE_STATES, LANES), lambda t: (t, 0, 0))
    (y, yg), couts = _hosted_call(
        body, comm, out_shape=[jax.ShapeDtypeStruct((lp, w), F32), jax.ShapeDtypeStruct((lp, w), BF16)], grid=(nt,),
        in_specs=[blk, lam_spec, lam_spec, b_spec, b_spec, c_spec, c_spec,
                  pl.BlockSpec((1, LANES), lambda t: (0, t // 2))],
        out_specs=[blk, blk],
        scratch_shapes=[pltpu.VMEM((njt, lp, LANES), F32), pltpu.VMEM((njt, lp, LANES), F32)],
        sem=("arbitrary",), name="ssm_fwd",
        args=(u, lam_re, lam_im, tb_re, tb_im, tc_re, tc_im, d_skip.reshape(1, w)))
    return (y, yg), couts


def _ssm_bwd(u, dy, lam_re, lam_im, tb_re, tb_im, tc_re, tc_im, d_skip, comm=None):
    lp, w = u.shape
    nt = tb_re.shape[0]
    nchain = _scan_chains(lp)
    seg = lp // (SUBLANES * nchain)
    chunk = lp // SUBLANES
    njt = TILE_STATES // LANES
    tbt_re, tbt_im = jnp.swapaxes(tb_re, 1, 2), jnp.swapaxes(tb_im, 1, 2)
    tct_re, tct_im = jnp.swapaxes(tc_re, 1, 2), jnp.swapaxes(tc_im, 1, 2)

    def body(u_ref, dy_ref, lr_ref, li_ref, br_ref, bi_ref, btr_ref, bti_ref, ctr_ref, cti_ref, d_ref,
             du_ref, dlr_ref, dli_ref, dbr_ref, dbi_ref, dcr_ref, dci_ref, dd_ref, hr, hi, ar, ai):
        t = pl.program_id(0)
        lrs, lis = _lam_tiles(lr_ref), _lam_tiles(li_ref)
        for s in range(SUBLANES):
            rs = pl.ds(s * chunk, chunk)
            ub = u_ref[rs, :].astype(BF16)
            dyb = dy_ref[rs, :].astype(BF16)
            _split_tiles(hr, rs, _dot(ub, br_ref[...], "nn"))
            _split_tiles(hi, rs, _dot(ub, bi_ref[...], "nn"))
            _split_tiles(ar, rs, _dot(dyb, ctr_ref[...], "nn"))
            _split_tiles(ai, rs, -_dot(dyb, cti_ref[...], "nn"))
        _ssm_scan(hr, hi, list(zip(lrs, lis)), seg, nchain, False)

        def dlam_step(tt, a, j, a_r, a_i, acc):
            sl = _chain_rows(a, jnp.maximum(tt - 1, 0), seg)
            p_r, p_i = hr[j, sl, :], hi[j, sl, :]
            acc = list(acc)
            acc[2 * j] = acc[2 * j] + jnp.where(tt > 0, a_r * p_r + a_i * p_i, 0.0)
            acc[2 * j + 1] = acc[2 * j + 1] + jnp.where(tt > 0, a_i * p_r - a_r * p_i, 0.0)
            return tuple(acc)

        zero = tuple([jnp.zeros((SUBLANES, LANES), F32)] * (2 * njt))
        conj = [(lr, -li) for lr, li in zip(lrs, lis)]
        acc = list(_ssm_scan(ar, ai, conj, seg, nchain, True, (dlam_step, zero)))
        row0 = lax.broadcasted_iota(jnp.int32, (SUBLANES, LANES), 0) == 0
        for j in range(njt):
            cs = slice(j * LANES, (j + 1) * LANES)
            for a in range(nchain):
                p_r = _shift_rows(hr[j, _chain_rows(a, seg - 1, seg), :], 1, False)
                p_i = _shift_rows(hi[j, _chain_rows(a, seg - 1, seg), :], 1, False)
                if a > 0:
                    before = pl.ds(a * SUBLANES * seg - 1, 1)
                    p_r = jnp.where(row0, jnp.broadcast_to(hr[j, before, :], p_r.shape), p_r)
                    p_i = jnp.where(row0, jnp.broadcast_to(hi[j, before, :], p_i.shape), p_i)
                a_r, a_i = ar[j, _chain_rows(a, 0, seg), :], ai[j, _chain_rows(a, 0, seg), :]
                acc[2 * j] = acc[2 * j] + a_r * p_r + a_i * p_i
                acc[2 * j + 1] = acc[2 * j + 1] + a_i * p_r - a_r * p_i
            dlr_ref[:, cs] = jnp.sum(acc[2 * j], axis=0, keepdims=True)
            dli_ref[:, cs] = jnp.sum(acc[2 * j + 1], axis=0, keepdims=True)

        dd = jnp.zeros((1, LANES), F32)
        for s in range(SUBLANES):
            rs = pl.ds(s * chunk, chunk)
            ub = u_ref[rs, :].astype(BF16)
            dyv = dy_ref[rs, :]
            dyb = dyv.astype(BF16)
            arb, aib = _cat_tiles(ar, rs), _cat_tiles(ai, rs)
            hrb, hib = _cat_tiles(hr, rs), _cat_tiles(hi, rs)
            du = _dot(arb, btr_ref[...], "nn") + _dot(aib, bti_ref[...], "nn")
            upd = [(dbr_ref, _dot(arb, ub, "tn")), (dbi_ref, _dot(aib, ub, "tn")),
                   (dcr_ref, _dot(dyb, hrb, "tn")), (dci_ref, -_dot(dyb, hib, "tn"))]
            for ref, val in upd:
                if s == 0:
                    ref[...] = val
                else:
                    ref[...] += val
            rows = lax.broadcasted_iota(jnp.int32, (chunk, LANES), 0) + s * chunk
            keep = rows >= PAD_FRONT
            dd = dd + jnp.sum(dyv * u_ref[rs, :], axis=0, keepdims=True)

            @pl.when(t % 2 == 0)
            def _():
                du_ref[rs, :] = jnp.where(keep, du + d_ref[...] * dyv, 0.0)

            @pl.when(t % 2 == 1)
            def _():
                du_ref[rs, :] += jnp.where(keep, du, 0.0)

        @pl.when(t % 2 == 0)
        def _():
            dd_ref[...] = dd

    blk = pl.BlockSpec((lp, LANES), lambda t: (0, t // 2))
    vec = pl.BlockSpec((1, LANES), lambda t: (0, t // 2))
    lam_spec = pl.BlockSpec((None, 1, TILE_STATES), lambda t: (t, 0, 0))
    b_spec = pl.BlockSpec((None, LANES, TILE_STATES), lambda t: (t, 0, 0))
    c_spec = pl.BlockSpec((None, TILE_STATES, LANES), lambda t: (t, 0, 0))
    lam_shape = jax.ShapeDtypeStruct((nt, 1, TILE_STATES), F32)
    bt_shape = jax.ShapeDtypeStruct((nt, TILE_STATES, LANES), F32)
    ct_shape = jax.ShapeDtypeStruct((nt, LANES, TILE_STATES), F32)
    st = pltpu.VMEM((njt, lp, LANES), F32)
    return _hosted_call(
        body, comm,
        out_shape=[jax.ShapeDtypeStruct((lp, w), F32), lam_shape, lam_shape, bt_shape, bt_shape, ct_shape, ct_shape,
                   jax.ShapeDtypeStruct((1, w), F32)],
        grid=(nt,),
        in_specs=[blk, blk, lam_spec, lam_spec, b_spec, b_spec, c_spec, c_spec, b_spec, b_spec, vec],
        out_specs=[blk, lam_spec, lam_spec, c_spec, c_spec, b_spec, b_spec, vec],
        scratch_shapes=[st, st, st, st], sem=("arbitrary",), name="ssm_bwd",
        args=(u, dy, lam_re, lam_im, tb_re, tb_im, tbt_re, tbt_im, tct_re, tct_im, d_skip.reshape(1, w)))


def _ssm_params(a_re, a_im, log_dt, b_re, b_im):
    dt = jnp.exp(log_dt)[:, None]
    mag = jnp.exp(a_re * dt)
    lb_re = mag * jnp.cos(a_im * dt)
    lb_im = mag * jnp.sin(a_im * dt)
    den = a_re * a_re + a_im * a_im
    num_re = lb_re - 1.0
    coef_re = (num_re * a_re + lb_im * a_im) / den
    coef_im = (lb_im * a_re - num_re * a_im) / den
    bb_re = coef_re[..., None] * b_re - coef_im[..., None] * b_im
    bb_im = coef_re[..., None] * b_im + coef_im[..., None] * b_re
    return lb_re, lb_im, bb_re, bb_im


def _merge_fwd(o, yg, ga, gs, w3t, comm=None):
    lp, d = ga.shape
    kw = w3t.shape[2]
    tm = _row_tile(lp)

    def epi(accs, in_refs, out_refs):
        attn, vv, gg = accs
        out_refs[0][...] = (_sigmoid(in_refs[5][...]) * attn
                            + _sigmoid(in_refs[6][...]) * (vv * _sigmoid(gg))).astype(BF16)

    wspec = lambda which: pl.BlockSpec((None, d, kw), lambda i: (which, 0, 0))
    rowspec = pl.BlockSpec((tm, d), lambda i: (i, 0))
    aspec = pl.BlockSpec((tm, kw), lambda i: (i, 0))
    res = _matmul(
        "merge_fwd", (lp // tm,), None, [o, yg, w3t, w3t, w3t, ga, gs],
        [aspec, aspec, wspec(0), wspec(1), wspec(2), rowspec, rowspec],
        [(0, 2, "nt", 0), (1, 3, "nt", 1), (1, 4, "nt", 2)], [(tm, d)] * 3, epi,
        [jax.ShapeDtypeStruct((lp, d), BF16)], [rowspec], ("parallel",), comm)
    return (res[0], []) if comm is None else (res[0][0], res[1])


def _out_proj(merged, w_out, h, next_gain, comm=None):
    lp, d = h.shape
    tm = _row_tile(lp)
    shapes, specs, extra_in, extra_specs = _residual_specs(lp, d, tm, next_gain)

    def epi(accs, in_refs, out_refs):
        _residual_outputs(in_refs[2][...] + accs[0], in_refs, out_refs, 3 if extra_in else None)

    rowspec = pl.BlockSpec((tm, d), lambda i: (i, 0))
    res = _matmul(
        "mix_out_proj", (lp // tm,), None, [merged, w_out, h] + extra_in,
        [rowspec, pl.BlockSpec((d, d), lambda i: (0, 0)), rowspec] + extra_specs,
        [(0, 1, "nn", 0)], [(tm, d)], epi, shapes, specs, ("parallel",), comm)
    return (res, []) if comm is None else res


def _merge_bwd(dhb, w_out, o, yg, ga, gs, w3t):
    lp, d = ga.shape
    kw = w3t.shape[2]
    tm = _row_tile(lp)

    def epi(accs, in_refs, out_refs):
        dm, attn, vv, gg = accs
        sa = _sigmoid(in_refs[7][...])
        ss = _sigmoid(in_refs[8][...])
        sg = _sigmoid(gg)
        ssm = vv * sg
        dssm = dm * ss
        out_refs[0][...] = (dm * sa).astype(BF16)
        out_refs[1][...] = (dssm * sg).astype(BF16)
        out_refs[2][...] = (dssm * vv * sg * (1.0 - sg)).astype(BF16)
        out_refs[3][...] = (dm * attn * sa * (1.0 - sa)).astype(BF16)
        out_refs[4][...] = (dm * ssm * ss * (1.0 - ss)).astype(BF16)

    wspec = lambda which: pl.BlockSpec((None, d, kw), lambda i: (which, 0, 0))
    rowspec = pl.BlockSpec((tm, d), lambda i: (i, 0))
    aspec = pl.BlockSpec((tm, kw), lambda i: (i, 0))
    shp = jax.ShapeDtypeStruct((lp, d), BF16)
    return _matmul(
        "merge_bwd", (lp // tm,), None, [dhb, w_out, o, yg, w3t, w3t, w3t, ga, gs],
        [rowspec, pl.BlockSpec((d, d), lambda i: (0, 0)), aspec, aspec, wspec(0), wspec(1), wspec(2), rowspec,
         rowspec],
        [(0, 1, "nt", 0), (2, 4, "nt", 1), (3, 5, "nt", 2), (3, 6, "nt", 3)], [(tm, d)] * 4, epi,
        [shp] * 5, [rowspec] * 5, ("parallel",))


def _branch_bwd(dattn, dv, dg, w3t, y):
    lp, d = dattn.shape
    kw = w3t.shape[2]
    tm = _row_tile(lp)

    def epi(accs, in_refs, out_refs):
        out_refs[0][...] = accs[0].astype(BF16)
        out_refs[1][...] = accs[1] * _gelu_grad(in_refs[6][...])

    wspec = lambda which: pl.BlockSpec((None, d, kw), lambda i: (which, 0, 0))
    rowspec = pl.BlockSpec((tm, d), lambda i: (i, 0))
    aspec = pl.BlockSpec((tm, kw), lambda i: (i, 0))
    return _matmul(
        "branch_bwd", (lp // tm,), None, [dattn, dv, dg, w3t, w3t, w3t, y],
        [rowspec, rowspec, rowspec, wspec(0), wspec(1), wspec(2), aspec],
        [(0, 3, "nn", 0), (1, 4, "nn", 1), (2, 5, "nn", 1)], [(tm, kw)] * 2, epi,
        [jax.ShapeDtypeStruct((lp, kw), BF16), jax.ShapeDtypeStruct((lp, kw), F32)], [aspec, aspec],
        ("parallel",))


def _sibling_half(acc, rows):
    half = rows // 2
    return jnp.where(lax.axis_index("c") == 0, acc[half:rows], acc[:half]).astype(BF16)


def _tn_cols(x, ys, name):
    lp, kx = x.shape
    n = ys[0].shape[1]
    n4 = n // N_CHIPS
    tk = _tn_tiles(lp)
    ny = len(ys)

    def epi(accs, in_refs, out_refs):
        for i, acc in enumerate(accs):
            out_refs[i][...] = acc
            out_refs[ny + i][...] = _sibling_half(acc, kx)

    shp = jax.ShapeDtypeStruct((N_CHIPS, kx, n4), F32)
    shp_half = jax.ShapeDtypeStruct((N_CHIPS, kx // 2, n4), BF16)
    res = _matmul(
        name, (N_CHIPS, lp // tk), 1, [x] + list(ys),
        [pl.BlockSpec((tk, kx), lambda j, k: (k, 0))] + [pl.BlockSpec((tk, n4), lambda j, k: (k, j))] * ny,
        [(0, 1 + i, "tn", i) for i in range(ny)], [(kx, n4)] * ny, epi,
        [shp] * ny + [shp_half] * ny,
        [pl.BlockSpec((None, kx, n4), lambda j, k: (j, 0, 0))] * ny
        + [pl.BlockSpec((None, kx // 2, n4), lambda j, k: (j, 0, 0))] * ny,
        ("arbitrary", "arbitrary"))
    return res[:ny], res[ny:]


def _tn_full(x, y, name, tn_cols=None):
    lp, kx = x.shape
    n = y.shape[1]
    tk = _tn_tiles(lp)
    tn = n if tn_cols is None else tn_cols
    k4 = kx // N_CHIPS

    def epi(accs, in_refs, out_refs):
        for j in range(N_CHIPS):
            slab = accs[0][j * k4:(j + 1) * k4]
            out_refs[0][j] = slab
            out_refs[1][j] = _sibling_half(slab, k4)

    return _matmul(
        name, (n // tn, lp // tk), 1, [x, y],
        [pl.BlockSpec((tk, kx), lambda j, k: (k, 0)), pl.BlockSpec((tk, tn), lambda j, k: (k, j))],
        [(0, 1, "tn", 0)], [(kx, tn)], epi,
        [jax.ShapeDtypeStruct((N_CHIPS, k4, n), F32), jax.ShapeDtypeStruct((N_CHIPS, k4 // 2, n), BF16)],
        [pl.BlockSpec((N_CHIPS, k4, tn), lambda j, k: (0, 0, j)),
         pl.BlockSpec((N_CHIPS, k4 // 2, tn), lambda j, k: (0, 0, j))],
        ("arbitrary", "arbitrary"))


def _in_proj_bwd(dz, w_in, dh, h_in, gain):
    lp, d = h_in.shape
    inw = w_in.shape[0]
    tm = _row_tile(lp)

    def epi(accs, in_refs, out_refs):
        i = pl.program_id(0)
        dx, dgrow = _rms_bwd_math(accs[0], in_refs[3][...], in_refs[4][...])
        dh_new = in_refs[2][...] + dx
        out_refs[0][...] = dh_new
        out_refs[2][...] = dh_new.astype(BF16)

        @pl.when(i == 0)
        def _():
            out_refs[1][...] = jnp.zeros_like(out_refs[1])

        out_refs[1][...] += jnp.sum(dgrow, axis=0, keepdims=True)

    row = pl.BlockSpec((tm, d), lambda i: (i, 0))
    return _matmul(
        "mix_in_proj_bwd", (lp // tm,), None, [dz, w_in, dh, h_in, gain.reshape(1, d)],
        [pl.BlockSpec((tm, inw), lambda i: (i, 0)), pl.BlockSpec((inw, d), lambda i: (0, 0)), row, row,
         pl.BlockSpec((1, d), lambda i: (0, 0))],
        [(0, 1, "nn", 0)], [(tm, d)], epi,
        [jax.ShapeDtypeStruct((lp, d), F32), jax.ShapeDtypeStruct((1, d), F32), jax.ShapeDtypeStruct((lp, d), BF16)],
        [row, pl.BlockSpec((1, d), lambda i: (0, 0)), row], ("arbitrary",))


def _loss_head(h, gain, target):
    lp, d = h.shape
    nb = lp // BLOCK

    def body(h_ref, g_ref, t_ref, dh_ref, dg_ref, loss_ref, dhb_ref):
        i = pl.program_id(0)

        @pl.when(i == 0)
        def _():
            dg_ref[...] = jnp.zeros_like(dg_ref)
            loss_ref[...] = jnp.zeros_like(loss_ref)
            dh_ref[...] = jnp.zeros_like(dh_ref)
            dhb_ref[...] = jnp.zeros_like(dhb_ref)

        @pl.when(i > 0)
        def _():
            x = h_ref[...]
            g = g_ref[...]
            r = lax.rsqrt(jnp.mean(x * x, axis=-1, keepdims=True) + EPS)
            err = x * r * g - t_ref[...]
            loss_ref[...] += jnp.zeros_like(loss_ref) + 0.5 * jnp.sum(jnp.sum(err * err, axis=-1, keepdims=True)) / d
            dx, dgrow = _rms_bwd_math(err * (1.0 / d), x, g)
            dh_ref[...] = dx
            dhb_ref[...] = dx.astype(BF16)
            dg_ref[...] += jnp.sum(dgrow, axis=0, keepdims=True)

    row = pl.BlockSpec((BLOCK, d), lambda i: (i, 0))
    one = pl.BlockSpec((1, d), lambda i: (0, 0))
    return pl.pallas_call(
        body,
        out_shape=[jax.ShapeDtypeStruct((lp, d), F32), jax.ShapeDtypeStruct((1, d), F32),
                   jax.ShapeDtypeStruct((SUBLANES, LANES), F32), jax.ShapeDtypeStruct((lp, d), BF16)],
        grid=(nb,),
        in_specs=[row, one, pl.BlockSpec((BLOCK, d), lambda i: (jnp.maximum(i - 1, 0), 0))],
        out_specs=[row, one, pl.BlockSpec((SUBLANES, LANES), lambda i: (0, 0)), row],
        compiler_params=_cparams(("arbitrary",)), name="loss_head")(h, gain.reshape(1, d), target)


def _adam_math(w, g, m, v):
    m = ADAM_B1 * m + (1.0 - ADAM_B1) * g
    v = ADAM_B2 * v + (1.0 - ADAM_B2) * (g * g)
    m_hat = m / (1.0 - ADAM_B1 ** ADAM_STEP)
    v_hat = v / (1.0 - ADAM_B2 ** ADAM_STEP)
    delta = -ADAM_LR * (m_hat / (jnp.sqrt(v_hat) + ADAM_EPS) + ADAM_WD * w)
    return delta, m, v


def _adamw_layers(w, m, v, mine, other, pos, name):
    depth, r, c = w.shape
    half = r // 2
    tr = _div_tile(half, c * 4)
    nh = half // tr

    def body(*refs):
        pos_ref, w_ref, m_ref, v_ref = refs[:4]
        mine_refs = refs[4:4 + depth]
        other_refs = refs[4 + depth:4 + 2 * depth]
        g_out, d_out, m_out, v_out = refs[4 + 2 * depth:]
        layer, i = pl.program_id(0), pl.program_id(1)
        is_mine = (i // nh) == pos_ref[0]

        def update(g):
            delta, nm, nv = _adam_math(w_ref[...], g, m_ref[...], v_ref[...])
            g_out[...] = g
            d_out[...] = delta
            m_out[...] = nm
            v_out[...] = nv

        for l in range(depth):
            @pl.when((layer == l) & is_mine)
            def _(l=l):
                update(mine_refs[l][...])

            @pl.when((layer == l) & jnp.logical_not(is_mine))
            def _(l=l):
                update(other_refs[l][...])

    stacked = pl.BlockSpec((None, tr, c), lambda l, i, p: (l, i, 0))

    def gspec(layer, is_other):
        def imap(l, i, p):
            first = jnp.where(is_other, 1 - p[0], p[0]) * nh
            here = jnp.clip(i - first, 0, nh - 1)
            return (jnp.where(l == layer, here, jnp.where(l < layer, 0, nh - 1)), 0)
        return pl.BlockSpec((tr, c), imap)

    shp = jax.ShapeDtypeStruct((depth, r, c), F32)
    grid_spec = pltpu.PrefetchScalarGridSpec(
        num_scalar_prefetch=1, grid=(depth, 2 * nh),
        in_specs=[stacked] * 3 + [gspec(l, 0) for l in range(depth)] + [gspec(l, 1) for l in range(depth)],
        out_specs=[stacked] * 4)
    return pl.pallas_call(
        body, out_shape=[shp] * 4, grid_spec=grid_spec,
        compiler_params=_cparams(("arbitrary", "arbitrary")), name=name)(pos, w, m, v, *mine, *other)


def _adamw_whole(w, g, m, v, name):
    def body(w_ref, g_ref, m_ref, v_ref, d_out, m_out, v_out):
        delta, nm, nv = _adam_math(w_ref[...], g_ref[...], m_ref[...], v_ref[...])
        d_out[...] = delta
        m_out[...] = nm
        v_out[...] = nv

    shp = jax.ShapeDtypeStruct(w.shape, F32)
    return pl.pallas_call(body, out_shape=[shp] * 3, compiler_params=_cparams(), name=name)(w, g, m, v)


def _mesh_pos():
    return lax.axis_index("x"), lax.axis_index("y"), lax.axis_index("c")


def _row_half(ref, which, lead):
    half = ref.shape[lead] // 2
    idx = (slice(None),) * lead + (pl.ds(which * half, half), slice(None))
    return ref.at[idx]


def _gather_comm(arrs, tag):
    n = len(arrs)

    def ctx(ins, outs, sems):
        send_sems, recv_sems, local_sems = sems
        x, y, c = _mesh_pos()
        chips = [(1 - x, y), (x, 1 - y), (1 - x, 1 - y)]

        def slot(k, chip, which):
            lead = len(ins[k].shape) - 2
            return _row_half(outs[k].at[2 * chip[0] + chip[1]], which, lead)

        def copy(k, j, src, dst, to):
            return pltpu.make_async_remote_copy(
                src_ref=src, dst_ref=dst, send_sem=send_sems.at[6 * k + j], recv_sem=recv_sems.at[6 * k + j],
                device_id=to, device_id_type=MESH)

        def local(k):
            return pltpu.make_async_copy(ins[k], outs[k].at[2 * x + y], local_sems.at[k])

        def first(k, j):
            lead = len(ins[k].shape) - 2
            return copy(k, j, _row_half(ins[k], c, lead), slot(k, (x, y), c), (*chips[j], c))

        def passed(k, j, which):
            return copy(k, 3 + j, slot(k, chips[j], which), slot(k, chips[j], which), (x, y, 1 - c))

        def landed(k, j):
            return copy(k, j, slot(k, chips[j], c), slot(k, chips[j], c), (x, y, 1 - c))

        return c, local, first, passed, landed

    def start(ins, outs, sems):
        c, local, first, passed, landed = ctx(ins, outs, sems)
        for k in range(n):
            local(k).start()
            for j in range(3):
                first(k, j).start()

    def mid(ins, outs, sems):
        c, local, first, passed, landed = ctx(ins, outs, sems)
        for j in range(3):
            for k in range(n):
                landed(k, j).wait_recv()
                passed(k, j, c).start()

    def finish(ins, outs, sems):
        c, local, first, passed, landed = ctx(ins, outs, sems)
        for j in range(3):
            for k in range(n):
                passed(k, j, 1 - c).wait_recv()
        for k in range(n):
            for j in range(3):
                first(k, j).wait_send()
                passed(k, j, c).wait_send()
            local(k).wait()

    return _Comm(
        tag, arrs, [jax.ShapeDtypeStruct((N_CHIPS,) + a.shape, a.dtype) for a in arrs],
        [pltpu.SemaphoreType.DMA((6 * n,)), pltpu.SemaphoreType.DMA((6 * n,)), pltpu.SemaphoreType.DMA((n,))],
        start, mid, finish)


def _run_comm(comm, name):
    n_in, n_out = len(comm.ins), len(comm.out_shapes)

    def body(*refs):
        ins, outs, sems = refs[:n_in], refs[n_in:n_in + n_out], refs[n_in + n_out:]
        comm.start(ins, outs, sems)
        if comm.mid is not None:
            comm.mid(ins, outs, sems)
        comm.finish(ins, outs, sems)

    return pl.pallas_call(
        body, out_shape=comm.out_shapes, in_specs=[HBM_SPEC] * n_in, out_specs=[HBM_SPEC] * n_out,
        scratch_shapes=comm.sems, name=name)(*comm.ins)


def _all_gather_chips(arrs, name):
    return _run_comm(_gather_comm(arrs, "gather"), name)


GATHER_US_PER_BYTE = 380.0 / 11.65e6
HOST_US = dict(ffn_up=68.0, ffn_down=37.0, in_proj=38.0, attn_fwd=103.0, ssm_fwd=70.0, merge_fwd=30.0,
               out_proj=23.0)
HOST_SLACK_US = 10.0


class _WeightStream:
    def __init__(self, pieces):
        self.keys = [k for k, _ in pieces]
        self.shards = dict(pieces)
        self.next = 0
        self.full = {}
        self.pending = []

    def comm_for(self, host):
        budget = HOST_US[host] + HOST_SLACK_US
        taken, cost = [], 0.0
        while self.next < len(self.keys):
            key = self.keys[self.next]
            c = self.shards[key].size * self.shards[key].dtype.itemsize * GATHER_US_PER_BYTE
            if cost + c > budget and taken:
                break
            taken.append(key)
            cost += c
            self.next += 1
        self.pending = taken
        if not taken:
            return None
        return _gather_comm([self.shards[k] for k in taken], "g_" + "_".join(k[1] for k in taken))

    def deposit(self, gathered):
        for key, arr in zip(self.pending, gathered):
            self.full[key] = arr
        self.pending = []

    def get(self, key):
        if key not in self.full:
            upto = self.keys.index(key) + 1
            keys = self.keys[self.next:upto]
            self.next = upto
            for k, arr in zip(keys, _all_gather_chips([self.shards[k] for k in keys], "gather_now")):
                self.full[k] = arr
        return self.full[key]


def _all_gather_devices(x_shard, name):
    m_per, ncol = x_shard.shape

    def body(x_ref, out_ref, send_sems, recv_sems, local_sem):
        x, y, c = _mesh_pos()
        me, sibling = (x, y, c), (x, y, 1 - c)
        chips = [(1 - x, y), (x, 1 - y), (1 - x, 1 - y)]

        def rows(px, py, pc):
            return out_ref.at[4 * px + 2 * py + pc]

        def copy(k, block, to, src=None):
            return pltpu.make_async_remote_copy(
                src_ref=rows(*block) if src is None else src, dst_ref=rows(*block),
                send_sem=send_sems.at[k], recv_sem=recv_sems.at[k], device_id=to, device_id_type=MESH)

        mine = pltpu.make_async_copy(x_ref, rows(*me), local_sem)
        mine.start()
        first = [copy(0, me, sibling, src=x_ref)]
        first += [copy(1 + j, me, (*chip, c), src=x_ref) for j, chip in enumerate(chips)]
        for cp in first:
            cp.start()
        passed = [copy(4 + j, (*chip, c), sibling) for j, chip in enumerate(chips)]
        for j, chip in enumerate(chips):
            copy(1 + j, (*chip, c), me).wait_recv()
            passed[j].start()
        copy(0, sibling, me).wait_recv()
        for j, chip in enumerate(chips):
            copy(4 + j, (*chip, 1 - c), me).wait_recv()
        for cp in first + passed:
            cp.wait_send()
        mine.wait()

    return pl.pallas_call(
        body, out_shape=jax.ShapeDtypeStruct((8, m_per, ncol), x_shard.dtype),
        in_specs=[pl.BlockSpec(memory_space=pltpu.VMEM)], out_specs=pl.BlockSpec(memory_space=pltpu.VMEM),
        scratch_shapes=[pltpu.SemaphoreType.DMA((7,)), pltpu.SemaphoreType.DMA((7,)), pltpu.SemaphoreType.DMA],
        compiler_params=pltpu.CompilerParams(vmem_limit_bytes=VMEM_LIMIT), name=name)(x_shard)


def _sum_devices(g8, name):
    _, r, c = g8.shape
    tr = _div_tile(r, c * 4 * 8)

    def body(g_ref, o_ref):
        acc = g_ref[0]
        for dev in range(1, 8):
            acc = acc + g_ref[dev]
        o_ref[...] = acc

    return pl.pallas_call(
        body, out_shape=jax.ShapeDtypeStruct((r, c), F32), grid=(r // tr,),
        in_specs=[pl.BlockSpec((8, tr, c), lambda i: (0, i, 0))], out_specs=pl.BlockSpec((tr, c), lambda i: (i, 0)),
        compiler_params=_cparams(("parallel",)), name=name)(g8)


def _chip_partials(arrs, recvs, pos, name):
    n = len(arrs)

    def body(pos_ref, *refs):
        for a_ref, b_ref, o_ref in zip(refs[:n], refs[n:2 * n], refs[2 * n:]):
            o_ref[...] = (a_ref[...] + b_ref[...]).astype(BF16)

    own_specs, recv_specs, shapes = [], [], []
    for arr in arrs:
        nslab, r, c = arr.shape
        own_specs.append(pl.BlockSpec((None, r // 2, c), lambda j, p: (j, p[0], 0)))
        recv_specs.append(pl.BlockSpec((None, r // 2, c), lambda j, p: (j, 0, 0)))
        shapes.append(jax.ShapeDtypeStruct((nslab, r // 2, c), BF16))
    grid_spec = pltpu.PrefetchScalarGridSpec(
        num_scalar_prefetch=1, grid=(N_CHIPS,), in_specs=own_specs + recv_specs, out_specs=recv_specs)
    return pl.pallas_call(
        body, out_shape=shapes, grid_spec=grid_spec,
        compiler_params=_cparams(("parallel",)), name=name)(pos, *arrs, *recvs)


def _chip_exchange_comm(parts, tag):
    n = len(parts)

    def copies(ins, outs, sems):
        send_sems, recv_sems = sems
        x, y, c = _mesh_pos()
        chips = [(1 - x, y), (x, 1 - y), (1 - x, 1 - y)]
        return [pltpu.make_async_remote_copy(
            src_ref=ins[k].at[2 * chip[0] + chip[1]], dst_ref=outs[k].at[j],
            send_sem=send_sems.at[3 * k + j], recv_sem=recv_sems.at[3 * k + j],
            device_id=(*chip, c), device_id_type=MESH) for k in range(n) for j, chip in enumerate(chips)]

    def start(ins, outs, sems):
        for cp in copies(ins, outs, sems):
            cp.start()

    def finish(ins, outs, sems):
        for cp in copies(ins, outs, sems):
            cp.wait()

    return _Comm(
        tag, parts, [jax.ShapeDtypeStruct((3,) + p.shape[1:], p.dtype) for p in parts],
        [pltpu.SemaphoreType.DMA((3 * n,)), pltpu.SemaphoreType.DMA((3 * n,))], start, None, finish)


def _reduce_halves(arrs, recvs, gots, pos, name):
    n = len(arrs)

    def body(pos_ref, *refs):
        for a_ref, b_ref, g_ref, o_ref in zip(refs[:n], refs[n:2 * n], refs[2 * n:3 * n], refs[3 * n:]):
            acc = a_ref[...] + b_ref[...]
            for j in range(3):
                acc = acc + g_ref[j].astype(F32)
            o_ref[...] = acc

    own_specs, recv_specs, got_specs, out_specs, shapes = [], [], [], [], []
    for arr in arrs:
        _, r, c = arr.shape
        own_specs.append(pl.BlockSpec((None, r // 2, c), lambda i, p: (p[1], p[0], 0)))
        recv_specs.append(pl.BlockSpec((None, r // 2, c), lambda i, p: (p[1], 0, 0)))
        got_specs.append(pl.BlockSpec((3, r // 2, c), lambda i, p: (0, 0, 0)))
        out_specs.append(pl.BlockSpec((r // 2, c), lambda i, p: (0, 0)))
        shapes.append(jax.ShapeDtypeStruct((r // 2, c), F32))
    grid_spec = pltpu.PrefetchScalarGridSpec(
        num_scalar_prefetch=1, grid=(1,), in_specs=own_specs + recv_specs + got_specs, out_specs=out_specs)
    return pl.pallas_call(
        body, out_shape=shapes, grid_spec=grid_spec,
        compiler_params=_cparams(("arbitrary",)), name=name)(pos, *arrs, *recvs, *gots)


def _share_halves(halves, name):
    n = len(halves)

    def body(*refs):
        ins, outs = refs[:n], refs[n:2 * n]
        send_sems, recv_sems = refs[2 * n:]
        x, y, c = _mesh_pos()
        cps = []
        for k in range(n):
            cp = pltpu.make_async_remote_copy(
                src_ref=ins[k], dst_ref=outs[k], send_sem=send_sems.at[k], recv_sem=recv_sems.at[k],
                device_id=(x, y, 1 - c), device_id_type=MESH)
            cp.start()
            cps.append(cp)
        for cp in cps:
            cp.wait()

    return pl.pallas_call(
        body, out_shape=[jax.ShapeDtypeStruct(h.shape, h.dtype) for h in halves],
        in_specs=[HBM_SPEC] * n, out_specs=[HBM_SPEC] * n,
        scratch_shapes=[pltpu.SemaphoreType.DMA((n,)), pltpu.SemaphoreType.DMA((n,))], name=name)(*halves)


class _Reduction:
    def __init__(self, arrs, others, pos, tag):
        self.arrs, self.pos, self.tag = arrs, pos, tag
        self.recv = _share_halves(others, "rs_sibling_" + tag)
        self.parts = _chip_partials(arrs, self.recv, pos, "rs_partial_" + tag)
        self.got = None

    def comm(self):
        return _chip_exchange_comm(self.parts, "rs_" + self.tag)

    def end(self):
        if self.got is None:
            self.got = _run_comm(self.comm(), "rs_chips_" + self.tag)
        return _reduce_halves(self.arrs, self.recv, self.got, self.pos, "rs_reduce_" + self.tag)


def _w_in_full(p, l, ws):
    slabs = ws.get((l, "w_in"))
    return slabs.reshape(-1, slabs.shape[2])


def _w3t_full(p, l, ws):
    if "w3t" not in p:
        slabs = ws.get((l, "w3"))
        p["w3t"] = jnp.swapaxes(slabs, 0, 1).reshape(slabs.shape[1], -1, slabs.shape[3])
    return p["w3t"]


def _w_out_full(l, ws):
    slabs = ws.get((l, "w_out"))
    return slabs.reshape(-1, slabs.shape[2])


def _layer_fwd(h, n0, l, p, next_gain, ws, tabs):
    def hosted(host, fn, *args):
        out, got = fn(*args, ws.comm_for(host))
        ws.deposit(got)
        return out

    ffn1_saved = hosted("ffn_up", _ffn_up, n0, ws.get((l, "wg1")), ws.get((l, "wu1")))
    h1, n = hosted("ffn_down", _ffn_down, ffn1_saved[2], ws.get((l, "wd1")), h, p["mix_norm"])
    ssm_w = p["ssm_d"].shape[0]
    q, k, v, u, ga, gs = hosted("in_proj", _in_proj, n, _w_in_full(p, l, ws), tabs, ssm_w)
    o = hosted("attn_fwd", _attn_fwd, q, k, v, p["attn_sinks"])
    y, yg = hosted("ssm_fwd", _ssm_fwd, u, *p["ssm_tabs"], p["ssm_d"])
    merged = hosted("merge_fwd", _merge_fwd, o, yg, ga, gs, _w3t_full(p, l, ws))
    h2, n2 = hosted("out_proj", _out_proj, merged, _w_out_full(l, ws), h1, p["ffn2_norm"])
    ffn2_saved = hosted("ffn_up", _ffn_up, n2, ws.get((l, "wg2")), ws.get((l, "wu2")))
    h3, *n3 = hosted("ffn_down", _ffn_down, ffn2_saved[2], ws.get((l, "wd2")), h2, next_gain)
    saved = dict(h0=h, h1=h1, h2=h2, ffn1=ffn1_saved, ffn2=ffn2_saved, n_mix=n, q=q, k=k, v=v, u=u, ga=ga, gs=gs,
                 o=o, y=y, yg=yg, merged=merged)
    return h3, (n3[0] if n3 else None), saved


def _layer_bwd(dh_pair, l, p, ws, s, tabs, pos):
    g = {}
    (dh2, dhb), g["ffn2_norm"], red_ffn2, _ = _ffn_bwd(
        dh_pair, s["h2"], p["ffn2_norm"], ws.get((l, "wg2")), ws.get((l, "wu2")), ws.get((l, "wd2")), p["f4"],
        s["ffn2"], pos)
    w3, w_out_w = _w3t_full(p, l, ws), _w_out_full(l, ws)
    lp, d = dh2.shape
    d4 = d // N_CHIPS
    dw_out, dw_out_other = _tn_full(s["merged"], dhb, "mix_dw_out")
    dattn, dv, dg, dga, dgs = _merge_bwd(dhb, w_out_w, s["o"], s["yg"], s["ga"], s["gs"], w3)
    (dw_ap,), (dw_ap_other,) = _tn_cols(s["o"], [dattn], "mix_dw_ap")
    (dw_gv, dw_gg), (dw_gv_other, dw_gg_other) = _tn_cols(s["yg"], [dv, dg], "mix_dw_glu")
    do, dy = _branch_bwd(dattn, dv, dg, w3, s["y"])
    (dq, dk, dvv, dkm, dvm, dsink), _ = _attn_bwd(s["q"], s["k"], s["v"], do, p["attn_sinks"], tabs)
    g["attn_sinks"] = dsink[:, 0]
    (du, dlr, dli, dbr, dbi, dcr, dci, dd), _ = _ssm_bwd(s["u"], dy, *p["ssm_tabs"], p["ssm_d"])
    ngrp = p["ssm_d"].shape[0] // SSM_GROUP
    g["ssm_lam"] = (dlr.reshape(ngrp, SSM_STATE), dli.reshape(ngrp, SSM_STATE),
                    _ssm_untable_b(dbr, ngrp), _ssm_untable_b(dbi, ngrp))
    g["ssm_c_re"] = _ssm_untable_c(dcr, ngrp)
    g["ssm_c_im"] = _ssm_untable_c(dci, ngrp)
    g["ssm_d"] = dd[0]
    dk = dk.at[:BLOCK].add(dkm)
    dvv = dvv.at[:BLOCK].add(dvm)
    dz = jnp.concatenate([dq.astype(BF16), dk.astype(BF16), dvv.astype(BF16), du.astype(BF16), dga, dgs], axis=1)
    n = s["n_mix"]
    w_in = _w_in_full(p, l, ws)
    dw_in, dw_in_other = _tn_full(dz, n, "mix_dw_in", d // 2)
    red_mix = _Reduction([dw_in, dw_ap, dw_gv, dw_gg, dw_out],
                         [dw_in_other, dw_ap_other, dw_gv_other, dw_gg_other, dw_out_other], pos, "mix")
    dh1, g["mix_norm"], dh1b = _in_proj_bwd(dz, w_in, dh2, s["h1"], p["mix_norm"])
    dh0_pair, g["ffn1_norm"], red_ffn1, red_mix.got = _ffn_bwd(
        (dh1, dh1b), s["h0"], p["ffn1_norm"], ws.get((l, "wg1")), ws.get((l, "wu1")), ws.get((l, "wd1")), p["f4"],
        s["ffn1"], pos, red_mix.comm())
    return dh0_pair, g, [*red_ffn1, red_mix, *red_ffn2]


BIG = ["ffn1_w_gate", "ffn1_w_up", "ffn1_w_down", "w_in", "w_attn_proj", "w_glu_v", "w_glu_g", "w_out",
       "ffn2_w_gate", "ffn2_w_up", "ffn2_w_down"]
TRANSPOSED = ["ffn1_w_gate", "ffn1_w_up", "w_in", "ffn2_w_gate", "ffn2_w_up"]
SMALL = ["ffn1_norm", "mix_norm", "attn_sinks", "ssm_a_re", "ssm_a_im", "ssm_log_dt", "ssm_b_re", "ssm_b_im",
         "ssm_c_re", "ssm_c_im", "ssm_d", "ffn2_norm", "final_norm"]
WEIGHTS = ["meta_tokens", "ffn1_norm", "ffn1_w_gate", "ffn1_w_up", "ffn1_w_down", "mix_norm", "w_in", "attn_sinks",
           "ssm_a_re", "ssm_a_im", "ssm_log_dt", "ssm_b_re", "ssm_b_im", "ssm_c_re", "ssm_c_im", "ssm_d",
           "w_attn_proj", "w_glu_v", "w_glu_g", "w_out", "ffn2_norm", "ffn2_w_gate", "ffn2_w_up", "ffn2_w_down",
           "final_norm"]


def _small_rows(shape):
    rows = -(-math.prod(shape) // LANES)
    return -(-rows // SUBLANES) * SUBLANES


def _pack_small(tree):
    parts = []
    for k in SMALL + ["meta_tokens"]:
        size, rows = math.prod(tree[k].shape), _small_rows(tree[k].shape)
        if size % LANES == 0:
            part = tree[k].reshape(size // LANES, LANES)
        else:
            part = jnp.pad(tree[k].reshape(1, size), ((0, 0), (0, LANES - size)))
        parts.append(jnp.pad(part, ((0, rows - part.shape[0]), (0, 0))))
    return jnp.concatenate(parts, axis=0)


def _unpack_small(packed, like):
    out, off = {}, 0
    for k in SMALL + ["meta_tokens"]:
        size, rows = math.prod(like[k].shape), _small_rows(like[k].shape)
        if size % LANES == 0:
            out[k] = packed[off:off + size // LANES].reshape(like[k].shape)
        else:
            out[k] = packed[off, :size].reshape(like[k].shape)
        off += rows
    return out


def kernel(x, meta_tokens, ffn1_norm, ffn1_w_gate, ffn1_w_up, ffn1_w_down, mix_norm, w_in, attn_sinks, ssm_a_re, ssm_a_im, ssm_log_dt, ssm_b_re, ssm_b_im, ssm_c_re, ssm_c_im, ssm_d, w_attn_proj, w_glu_v, w_glu_g, w_out, ffn2_norm, ffn2_w_gate, ffn2_w_up, ffn2_w_down, final_norm, loss_target, m_meta_tokens, m_ffn1_norm, m_ffn1_w_gate, m_ffn1_w_up, m_ffn1_w_down, m_mix_norm, m_w_in, m_attn_sinks, m_ssm_a_re, m_ssm_a_im, m_ssm_log_dt, m_ssm_b_re, m_ssm_b_im, m_ssm_c_re, m_ssm_c_im, m_ssm_d, m_w_attn_proj, m_w_glu_v, m_w_glu_g, m_w_out, m_ffn2_norm, m_ffn2_w_gate, m_ffn2_w_up, m_ffn2_w_down, m_final_norm, v_meta_tokens, v_ffn1_norm, v_ffn1_w_gate, v_ffn1_w_up, v_ffn1_w_down, v_mix_norm, v_w_in, v_attn_sinks, v_ssm_a_re, v_ssm_a_im, v_ssm_log_dt, v_ssm_b_re, v_ssm_b_im, v_ssm_c_re, v_ssm_c_im, v_ssm_d, v_w_attn_proj, v_w_glu_v, v_w_glu_g, v_w_out, v_ffn2_norm, v_ffn2_w_gate, v_ffn2_w_up, v_ffn2_w_down, v_final_norm):
    args = dict(locals())
    w = {k: args[k] for k in WEIGHTS}
    m = {k: args["m_" + k] for k in WEIGHTS}
    v = {k: args["v_" + k] for k in WEIGHTS}
    depth = ffn1_norm.shape[0]
    seq, d = x.shape[1], x.shape[2]
    lp = seq + BLOCK
    xi, yi, ci = _mesh_pos()
    pos = jnp.stack([ci, 2 * xi + yi]).astype(jnp.int32)

    tabs = _rope_tables(lp)
    layers, pieces = [], [((0, "meta"), meta_tokens)]
    f4 = ffn1_w_gate.shape[2]
    fp = -(-f4 // MXU_DIM) * MXU_DIM

    def ffn_rows(wt):
        return jnp.pad(wt, ((0, fp - f4), (0, 0))).astype(BF16)

    for l in range(depth):
        small = [((l, "w3"), jnp.stack([w_attn_proj[l].T, w_glu_v[l].T, w_glu_g[l].T]).astype(BF16)),
                 ((l, "w_out"), w_out[l].astype(BF16))]
        first = [((l, "wg1"), ffn_rows(ffn1_w_gate[l].T)), ((l, "wu1"), ffn_rows(ffn1_w_up[l].T)),
                 ((l, "wd1"), ffn_rows(ffn1_w_down[l])), ((l, "w_in"), w_in[l].T.astype(BF16))]
        pieces += (first + small if l == 0 else small + first) + [
            ((l, "wg2"), ffn_rows(ffn2_w_gate[l].T)), ((l, "wu2"), ffn_rows(ffn2_w_up[l].T)),
            ((l, "wd2"), ffn_rows(ffn2_w_down[l]))]
        lb_re, lb_im, bb_re, bb_im = _ssm_params(ssm_a_re[l], ssm_a_im[l], ssm_log_dt[l], ssm_b_re[l], ssm_b_im[l])
        ngrp = lb_re.shape[0]
        nt = ngrp // GROUPS_PER_TILE
        ssm_tabs = (lb_re.reshape(nt, 1, TILE_STATES), lb_im.reshape(nt, 1, TILE_STATES),
                    *_ssm_tables(bb_re, bb_im, ssm_c_re[l], ssm_c_im[l]))
        layers.append(dict(
            ffn1_norm=ffn1_norm[l], mix_norm=mix_norm[l], ffn2_norm=ffn2_norm[l], attn_sinks=attn_sinks[l],
            ssm_d=ssm_d[l], ssm_tabs=ssm_tabs, f4=f4))
    ws = _WeightStream(pieces)
    ws.get((0, "wu1"))
    meta_all = ws.get((0, "meta"))
    meta_full = jnp.concatenate([meta_all[j] for j in range(N_CHIPS)], axis=1)

    h = jnp.concatenate([jnp.zeros((PAD_FRONT, d), F32), meta_full, x[0]], axis=0)
    saved = []
    n0 = _rms_fwd(h, ffn1_norm[0], "rms_fwd_first")
    for l in range(depth):
        next_gain = ffn1_norm[l + 1] if l + 1 < depth else None
        h, n0, s = _layer_fwd(h, n0, l, layers[l], next_gain, ws, tabs)
        saved.append(s)
    dh, g_final, loss_acc, dhb = _loss_head(h, final_norm, loss_target[0])
    dh_pair = (dh, dhb)
    loss = lax.psum(loss_acc[0, 0], ("x", "y", "c"))

    grads, reds = [None] * depth, [None] * depth
    for l in reversed(range(depth)):
        dh_pair, grads[l], reds[l] = _layer_bwd(dh_pair, l, layers[l], ws, saved[l], tabs, pos)
    dh = dh_pair[0]
    grad_x = dh[BLOCK:][None]
    dmeta_local = dh[PAD_FRONT:BLOCK]

    small = {k: [] for k in SMALL}
    for l in range(depth):
        gl = grads[l]
        _, vjp = jax.vjp(_ssm_params, ssm_a_re[l], ssm_a_im[l], ssm_log_dt[l], ssm_b_re[l], ssm_b_im[l])
        da_re, da_im, dlog_dt, db_re, db_im = vjp(gl["ssm_lam"])
        for k, val in (("ffn1_norm", gl["ffn1_norm"][0]), ("mix_norm", gl["mix_norm"][0]),
                       ("attn_sinks", gl["attn_sinks"]), ("ssm_a_re", da_re), ("ssm_a_im", da_im),
                       ("ssm_log_dt", dlog_dt), ("ssm_b_re", db_re), ("ssm_b_im", db_im),
                       ("ssm_c_re", gl["ssm_c_re"]), ("ssm_c_im", gl["ssm_c_im"]), ("ssm_d", gl["ssm_d"]),
                       ("ffn2_norm", gl["ffn2_norm"][0])):
            small[k].append(val)
    small_local = {k: jnp.stack(vals) for k, vals in small.items() if k != "final_norm"}
    small_local["final_norm"] = g_final[0]
    small_local["meta_tokens"] = dmeta_local
    like = dict(small_local)
    g_small = _sum_devices(_all_gather_devices(_pack_small(small_local), "gather_small_grads"), "sum_small_grads")
    g_small_tree = _unpack_small(g_small, like)
    d4 = d // N_CHIPS
    chip = 2 * xi + yi
    g_meta = lax.dynamic_slice_in_dim(g_small_tree["meta_tokens"], chip * d4, d4, axis=1)

    mine = [[half for red in reds[l] for half in red.end()] for l in range(depth)]
    flat = _share_halves([half for layer_halves in mine for half in layer_halves], "rs_share")
    per_layer = len(mine[0])
    reduced = [(mine[l], flat[l * per_layer:(l + 1) * per_layer]) for l in range(depth)]

    g_out, delta, new_m, new_v = {}, {}, {}, {}
    for i, k in enumerate(BIG):
        flip = (lambda t: jnp.swapaxes(t, 1, 2)) if k in TRANSPOSED else (lambda t: t)
        outs = _adamw_layers(
            flip(w[k]), flip(m[k]), flip(v[k]), [reduced[l][0][i] for l in range(depth)],
            [reduced[l][1][i] for l in range(depth)], pos, "adamw_" + k)
        g_out[k], delta[k], new_m[k], new_v[k] = [flip(t) for t in outs]
    g_small_tree["meta_tokens"] = g_meta
    for k in SMALL + ["meta_tokens"]:
        shape = w[k].shape if w[k].ndim > 1 else (1,) + w[k].shape
        outs = _adamw_whole(w[k].reshape(shape), g_small_tree[k].reshape(shape), m[k].reshape(shape),
                            v[k].reshape(shape), "adamw_" + k)
        g_out[k] = g_small_tree[k]
        delta[k], new_m[k], new_v[k] = [t.reshape(w[k].shape) for t in outs]

    return (loss, grad_x, *[g_out[k] for k in WEIGHTS], *[delta[k] for k in WEIGHTS],
            *[new_m[k] for k in WEIGHTS], *[new_v[k] for k in WEIGHTS])
```

```python
import functools
import math

import jax
import jax.numpy as jnp
from jax import lax
from jax.experimental import pallas as pl
from jax.experimental.pallas import tpu as pltpu

F32 = jnp.float32
BF16 = jnp.bfloat16

N_META = 16
HEAD_DIM = 64
N_Q_HEADS = 8
N_KV_HEADS = 2
Q_PER_KV = N_Q_HEADS // N_KV_HEADS
ATTN_WIDTH = N_Q_HEADS * HEAD_DIM
KV_WIDTH = N_KV_HEADS * HEAD_DIM
BLOCK = 128
PAD_FRONT = BLOCK - N_META
ROPE_THETA = 500000.0
ROT_DIM = HEAD_DIM // 4
SSM_GROUP = 16
SSM_STATE = 64
GROUPS_PER_TILE = 4
TILE_STATES = GROUPS_PER_TILE * SSM_STATE
LANES = 128
SUBLANES = 8
MXU_DIM = 256
EPS = 1e-6
NEG_INF = -1e30
N_CHIPS = 4

ADAM_LR = 0.001
ADAM_B1 = 0.9
ADAM_B2 = 0.999
ADAM_EPS = 1e-08
ADAM_WD = 0.01
ADAM_STEP = 10

VMEM_LIMIT = 56 * 1024 * 1024
MESH = pl.DeviceIdType.MESH


def _cparams(sem=None):
    return pltpu.CompilerParams(dimension_semantics=sem, vmem_limit_bytes=VMEM_LIMIT)


def _row_tile(rows, limit=512):
    best = None
    for t in range(128, limit + 1, 128):
        if rows % t == 0:
            best = t
    assert best is not None, rows
    return best


def _div_tile(rows, row_bytes, max_bytes=1 << 20, mult=8):
    best = None
    for t in range(mult, rows + 1, mult):
        if rows % t == 0 and t * row_bytes <= max_bytes:
            best = t
    if best is None:
        best = rows
    return best


def _dot(a, b, mode):
    if mode == "nn":
        dims = (((1,), (0,)), ((), ()))
    elif mode == "nt":
        dims = (((1,), (1,)), ((), ()))
    else:
        dims = (((0,), (0,)), ((), ()))
    return lax.dot_general(a.astype(BF16), b.astype(BF16), dims, preferred_element_type=F32)


def _sigmoid(x):
    return 1.0 / (1.0 + jnp.exp(-x))


_GELU_C = math.sqrt(2.0 / math.pi)


def _gelu(x):
    return 0.5 * x * (1.0 + jnp.tanh(_GELU_C * (x + 0.044715 * x * x * x)))


def _gelu_grad(x):
    t = jnp.tanh(_GELU_C * (x + 0.044715 * x * x * x))
    return 0.5 * (1.0 + t) + 0.5 * x * (1.0 - t * t) * _GELU_C * (1.0 + 3.0 * 0.044715 * x * x)


class _Comm:
    def __init__(self, tag, ins, out_shapes, sems, start, mid, finish):
        self.tag, self.ins, self.out_shapes, self.sems = tag, list(ins), list(out_shapes), list(sems)
        self.start, self.mid, self.finish = start, mid, finish


HBM_SPEC = pl.BlockSpec(memory_space=pltpu.HBM)
MID_NUM, MID_DEN = 4, 5


def _hosted_call(body, comm, *, out_shape, grid, in_specs, out_specs, scratch_shapes, sem, name, args):
    out_shape, in_specs, out_specs = list(out_shape), list(in_specs), list(out_specs)
    scratch_shapes = list(scratch_shapes)
    if comm is None:
        res = pl.pallas_call(
            body, out_shape=out_shape, grid=grid, in_specs=in_specs, out_specs=out_specs,
            scratch_shapes=scratch_shapes, compiler_params=_cparams(sem), name=name)(*args)
        return list(res), []
    n_in, n_out, n_sc = len(args), len(out_shape), len(scratch_shapes)
    nci, nco = len(comm.ins), len(comm.out_shapes)
    total = math.prod(grid)

    def wrapped(*refs):
        in_refs, cin = refs[:n_in], refs[n_in:n_in + nci]
        o0 = n_in + nci
        out_refs, cout = refs[o0:o0 + n_out], refs[o0 + n_out:o0 + n_out + nco]
        s0 = o0 + n_out + nco
        sc, csem = refs[s0:s0 + n_sc], refs[s0 + n_sc:]
        lin = 0
        for dim, size in enumerate(grid):
            lin = lin * size + pl.program_id(dim)

        @pl.when(lin == 0)
        def _():
            comm.start(cin, cout, csem)

        if comm.mid is not None:
            @pl.when(lin == (total * MID_NUM) // MID_DEN)
            def _():
                comm.mid(cin, cout, csem)

        body(*in_refs, *out_refs, *sc)

        @pl.when(lin == total - 1)
        def _():
            comm.finish(cin, cout, csem)

    res = pl.pallas_call(
        wrapped, out_shape=out_shape + comm.out_shapes, grid=grid,
        in_specs=in_specs + [HBM_SPEC] * nci, out_specs=out_specs + [HBM_SPEC] * nco,
        scratch_shapes=scratch_shapes + comm.sems,
        compiler_params=_cparams(("arbitrary",) * len(grid)), name=name + "_" + comm.tag)(*args, *comm.ins)
    return list(res[:n_out]), list(res[n_out:])


def _matmul(name, grid, k_axis, ins, in_specs, pairs, acc_shapes, epilogue, out_shapes, out_specs, sem, comm=None):
    n_in, n_out, n_acc = len(ins), len(out_shapes), len(acc_shapes)

    def body(*refs):
        in_refs = refs[:n_in]
        out_refs = refs[n_in:n_in + n_out]
        acc_refs = refs[n_in + n_out:]
        if k_axis is None:
            accs = [None] * n_acc
            for ia, ib, mode, iacc in pairs:
                d = _dot(in_refs[ia][...], in_refs[ib][...], mode)
                accs[iacc] = d if accs[iacc] is None else accs[iacc] + d
            epilogue(accs, in_refs, out_refs)
            return
        k = pl.program_id(k_axis)

        @pl.when(k == 0)
        def _():
            for r in acc_refs:
                r[...] = jnp.zeros_like(r)

        for ia, ib, mode, iacc in pairs:
            acc_refs[iacc][...] += _dot(in_refs[ia][...], in_refs[ib][...], mode)

        @pl.when(k == pl.num_programs(k_axis) - 1)
        def _():
            epilogue([r[...] for r in acc_refs], in_refs, out_refs)

    scratch = [] if k_axis is None else [pltpu.VMEM(s, F32) for s in acc_shapes]
    outs, couts = _hosted_call(
        body, comm, out_shape=out_shapes, grid=grid, in_specs=in_specs, out_specs=out_specs,
        scratch_shapes=scratch, sem=sem, name=name, args=ins)
    return outs if comm is None else (outs, couts)


def _rms_math(x, g):
    r = lax.rsqrt(jnp.mean(x * x, axis=-1, keepdims=True) + EPS)
    return (x * r * g).astype(BF16)


def _rms_fwd(h, g, name):
    lp, d = h.shape
    tm = _row_tile(lp)

    def body(h_ref, g_ref, n_ref):
        n_ref[...] = _rms_math(h_ref[...], g_ref[...])

    return pl.pallas_call(
        body, out_shape=jax.ShapeDtypeStruct((lp, d), BF16), grid=(lp // tm,),
        in_specs=[pl.BlockSpec((tm, d), lambda i: (i, 0)), pl.BlockSpec((1, d), lambda i: (0, 0))],
        out_specs=pl.BlockSpec((tm, d), lambda i: (i, 0)),
        compiler_params=_cparams(("parallel",)), name=name)(h, g.reshape(1, d))


def _rms_bwd_math(dn, x, g):
    r = lax.rsqrt(jnp.mean(x * x, axis=-1, keepdims=True) + EPS)
    xh = x * r
    dxh = dn * g
    dx = r * (dxh - xh * jnp.mean(dxh * xh, axis=-1, keepdims=True))
    return dx, dn * xh


def _ffn_up(n, wgt, wut, comm=None):
    lp, d = n.shape
    fp = wgt.shape[1]
    tm = _row_tile(lp)

    def up_body(n_ref, wg_ref, wu_ref, a_ref, b_ref, s_ref):
        x = n_ref[...]
        for jc in range(N_CHIPS):
            cols = slice(jc * fp, (jc + 1) * fp)
            a = _dot(x, wg_ref[jc], "nt")
            b = _dot(x, wu_ref[jc], "nt")
            a_ref[:, cols] = a.astype(BF16)
            b_ref[:, cols] = b.astype(BF16)
            s_ref[:, cols] = (a * _sigmoid(a) * b).astype(BF16)

    ff = N_CHIPS * fp
    act = jax.ShapeDtypeStruct((lp, ff), BF16)
    act_tile = pl.BlockSpec((tm, ff), lambda i: (i, 0))
    w_spec = pl.BlockSpec((N_CHIPS, fp, d), lambda i: (0, 0, 0))
    outs, couts = _hosted_call(
        up_body, comm, out_shape=[act, act, act], grid=(lp // tm,),
        in_specs=[pl.BlockSpec((tm, d), lambda i: (i, 0)), w_spec, w_spec],
        out_specs=[act_tile] * 3, scratch_shapes=[], sem=("parallel",), name="ffn_up", args=(n, wgt, wut))
    return (*outs, n), couts


def _residual_outputs(h_new, in_refs, out_refs, gain_at):
    out_refs[0][...] = h_new
    if gain_at is not None:
        out_refs[1][...] = _rms_math(h_new, in_refs[gain_at][...])


def _residual_specs(lp, d, tm, next_gain):
    row = pl.BlockSpec((tm, d), lambda i: (i, 0))
    shapes, specs = [jax.ShapeDtypeStruct((lp, d), F32)], [row]
    extra_in, extra_specs = [], []
    if next_gain is not None:
        shapes.append(jax.ShapeDtypeStruct((lp, d), BF16))
        specs.append(row)
        extra_in, extra_specs = [next_gain.reshape(1, d)], [pl.BlockSpec((1, d), lambda i: (0, 0))]
    return shapes, specs, extra_in, extra_specs


def _ffn_down(s, wd, h, next_gain, comm=None):
    lp, d = h.shape
    ff = s.shape[1]
    tm = _row_tile(lp)
    shapes, specs, extra_in, extra_specs = _residual_specs(lp, d, tm, next_gain)

    def down_epi(accs, in_refs, out_refs):
        _residual_outputs(in_refs[2][...] + 0.5 * accs[0], in_refs, out_refs, 3 if extra_in else None)

    res = _matmul(
        "ffn_down", (lp // tm,), None, [s, wd.reshape(ff, d), h] + extra_in,
        [pl.BlockSpec((tm, ff), lambda i: (i, 0)), pl.BlockSpec((ff, d), lambda i: (0, 0)),
         pl.BlockSpec((tm, d), lambda i: (i, 0))] + extra_specs,
        [(0, 1, "nn", 0)], [(tm, d)], down_epi, shapes, specs, ("parallel",), comm)
    return (res, []) if comm is None else res


def _tn_tiles(lp):
    return _row_tile(lp, 1408)


def _ffn_bwd(dh_pair, h_in, gain, wgt, wut, wd, f4, saved, pos, comm=None):
    dh, dhb = dh_pair
    a, b, s, n = saved
    lp, d = h_in.shape
    fp = wgt.shape[1]
    ff = N_CHIPS * fp
    tm = _row_tile(lp)
    ni = lp // tm
    tk = _tn_tiles(lp)
    nk = lp // tk

    def ds_body(dh_ref, wd_ref, a_ref, b_ref, da_ref, db_ref):
        x = dh_ref[...]
        for jc in range(N_CHIPS):
            cols = slice(jc * fp, (jc + 1) * fp)
            ds = 0.5 * _dot(x, wd_ref[jc], "nt")
            av = a_ref[:, cols].astype(F32)
            bv = b_ref[:, cols].astype(F32)
            sg = _sigmoid(av)
            da_ref[:, cols] = (ds * bv * sg * (1.0 + av * (1.0 - sg))).astype(BF16)
            db_ref[:, cols] = (ds * av * sg).astype(BF16)

    act = jax.ShapeDtypeStruct((lp, ff), BF16)
    act_tile = pl.BlockSpec((tm, ff), lambda i: (i, 0))
    (da, db), couts = _hosted_call(
        ds_body, comm, out_shape=[act, act], grid=(ni,),
        in_specs=[pl.BlockSpec((tm, d), lambda i: (i, 0)), pl.BlockSpec((N_CHIPS, fp, d), lambda i: (0, 0, 0)),
                  act_tile, act_tile],
        out_specs=[act_tile, act_tile], scratch_shapes=[], sem=("parallel",), name="ffn_bwd_ds",
        args=(dhb, wd, a, b))

    dw_shape = jax.ShapeDtypeStruct((N_CHIPS, f4, d), F32)
    dw_spec = pl.BlockSpec((None, f4, d), lambda j, k: (j, 0, 0))
    in_col = pl.BlockSpec((tk, fp), lambda j, k: (k, j))
    in_row = pl.BlockSpec((tk, d), lambda j, k: (k, 0))

    half_shape = jax.ShapeDtypeStruct((N_CHIPS, f4 // 2, d), BF16)
    half_spec = pl.BlockSpec((None, f4 // 2, d), lambda j, k: (j, 0, 0))

    def dwd_epi(accs, in_refs, out_refs):
        dw = 0.5 * accs[0]
        out_refs[0][...] = dw[:f4]
        out_refs[1][...] = _sibling_half(dw, f4)

    dwd, dwd_other = _matmul(
        "ffn_dwd", (N_CHIPS, nk), 1, [s, dhb], [in_col, in_row],
        [(0, 1, "tn", 0)], [(fp, d)], dwd_epi, [dw_shape, half_shape], [dw_spec, half_spec],
        ("arbitrary", "arbitrary"))

    def dwgu_epi(accs, in_refs, out_refs):
        for i, acc in enumerate(accs):
            out_refs[i][...] = acc[:f4]
            out_refs[2 + i][...] = _sibling_half(acc, f4)

    red_down = _Reduction([dwd], [dwd_other], pos, "ffn_d")
    (dwg, dwu, dwg_other, dwu_other), red_down.got = _matmul(
        "ffn_dwgu", (N_CHIPS, nk), 1, [n, da, db], [in_row, in_col, in_col],
        [(1, 0, "tn", 0), (2, 0, "tn", 1)], [(fp, d)] * 2, dwgu_epi,
        [dw_shape, dw_shape, half_shape, half_shape], [dw_spec, dw_spec, half_spec, half_spec],
        ("arbitrary", "arbitrary"), red_down.comm())

    def dn_epi(accs, in_refs, out_refs):
        i = pl.program_id(0)
        dx, dgrow = _rms_bwd_math(accs[0], in_refs[5][...], in_refs[6][...])
        dh_new = in_refs[4][...] + dx
        out_refs[0][...] = dh_new
        out_refs[2][...] = dh_new.astype(BF16)

        @pl.when(i == 0)
        def _():
            out_refs[1][...] = jnp.zeros_like(out_refs[1])

        out_refs[1][...] += jnp.sum(dgrow, axis=0, keepdims=True)

    red = _Reduction([dwg, dwu], [dwg_other, dwu_other], pos, "ffn_gu")
    row_spec = pl.BlockSpec((tm, d), lambda i: (i, 0))
    act_spec = pl.BlockSpec((tm, ff), lambda i: (i, 0))
    w_spec = pl.BlockSpec((ff, d), lambda i: (0, 0))
    one_spec = pl.BlockSpec((1, d), lambda i: (0, 0))
    (dh_in, dgain, dh_in_b), red.got = _matmul(
        "ffn_bwd_dn", (ni,), None, [da, wgt.reshape(ff, d), db, wut.reshape(ff, d), dh, h_in, gain.reshape(1, d)],
        [act_spec, w_spec, act_spec, w_spec, row_spec, row_spec, one_spec],
        [(0, 1, "nn", 0), (2, 3, "nn", 0)], [(tm, d)], dn_epi,
        [jax.ShapeDtypeStruct((lp, d), F32), jax.ShapeDtypeStruct((1, d), F32), jax.ShapeDtypeStruct((lp, d), BF16)],
        [row_spec, one_spec, row_spec], ("arbitrary",), red.comm())
    return (dh_in, dh_in_b), dgain, [red, red_down], couts


def _rope_tables(lp):
    pos = jnp.arange(lp, dtype=F32) - float(PAD_FRONT)
    inv_freq = ROPE_THETA ** (-jnp.arange(0, ROT_DIM, 2, dtype=F32) / ROT_DIM)
    ang = pos[:, None] * inv_freq[None, :]
    cos, sin = jnp.cos(ang), jnp.sin(ang)
    half = ROT_DIM // 2
    ones = jnp.ones((lp, HEAD_DIM - ROT_DIM), F32)
    zeros_h = jnp.zeros((lp, half), F32)
    zeros_r = jnp.zeros((lp, HEAD_DIM - ROT_DIM), F32)
    c = jnp.concatenate([cos, cos, ones], axis=1)
    s1 = jnp.concatenate([-sin, zeros_h, zeros_r], axis=1)
    s2 = jnp.concatenate([zeros_h, sin, zeros_r], axis=1)
    reps = LANES // HEAD_DIM
    return jnp.stack([jnp.tile(c, (1, reps)), jnp.tile(s1, (1, reps)), jnp.tile(s2, (1, reps))])


def _rope(x, c, s1, s2):
    half = ROT_DIM // 2
    outs = []
    for ch in range(x.shape[1] // LANES):
        xc = x[:, ch * LANES:(ch + 1) * LANES]
        outs.append(xc * c + pltpu.roll(xc, LANES - half, 1) * s1 + pltpu.roll(xc, half, 1) * s2)
    return outs[0] if len(outs) == 1 else jnp.concatenate(outs, axis=1)


def _rope_t(dy, c, s1, s2):
    half = ROT_DIM // 2
    outs = []
    for ch in range(dy.shape[1] // LANES):
        dc = dy[:, ch * LANES:(ch + 1) * LANES]
        outs.append(dc * c + pltpu.roll(dc * s1, half, 1) + pltpu.roll(dc * s2, LANES - half, 1))
    return outs[0] if len(outs) == 1 else jnp.concatenate(outs, axis=1)


def _in_proj(n, w_in, tabs, ssm_w, comm=None):
    lp, d = n.shape
    inw = w_in.shape[0]
    tm = _row_tile(lp)
    o1 = ATTN_WIDTH
    o2 = o1 + KV_WIDTH
    o3 = o2 + KV_WIDTH
    o4 = o3 + ssm_w
    o5 = o4 + d

    def epi(accs, in_refs, out_refs):
        z = accs[0]
        c, s1, s2 = in_refs[2][0], in_refs[2][1], in_refs[2][2]
        out_refs[0][...] = _rope(z[:, :o1], c, s1, s2).astype(BF16)
        out_refs[1][...] = _rope(z[:, o1:o2], c, s1, s2).astype(BF16)
        out_refs[2][...] = z[:, o2:o3].astype(BF16)
        out_refs[3][...] = z[:, o3:o4]
        out_refs[4][...] = z[:, o4:o5]
        out_refs[5][...] = z[:, o5:]

    def rs(w, dt):
        return jax.ShapeDtypeStruct((lp, w), dt), pl.BlockSpec((tm, w), lambda i: (i, 0))

    shapes, specs = zip(rs(o1, BF16), rs(KV_WIDTH, BF16), rs(KV_WIDTH, BF16), rs(ssm_w, F32), rs(d, F32), rs(d, F32))
    res = _matmul(
        "mix_in_proj", (lp // tm,), None, [n, w_in, tabs],
        [pl.BlockSpec((tm, d), lambda i: (i, 0)), pl.BlockSpec((inw, d), lambda i: (0, 0)),
         pl.BlockSpec((3, tm, LANES), lambda i: (0, i, 0))],
        [(0, 1, "nt", 0)], [(tm, inw)], epi, list(shapes), list(specs), ("parallel",), comm)
    return (res, []) if comm is None else res


def _attn_mask(b):
    rows = lax.broadcasted_iota(jnp.int32, (BLOCK, 3 * BLOCK), 0)
    cols = lax.broadcasted_iota(jnp.int32, (BLOCK, 3 * BLOCK), 1)
    qpos = b * BLOCK + rows - PAD_FRONT
    kpos = (b - 1) * BLOCK + cols - PAD_FRONT
    dist = qpos - kpos
    band = (cols < 2 * BLOCK) & (kpos >= N_META) & (dist >= 0) & (dist < BLOCK)
    mrow = cols - 2 * BLOCK
    meta = (mrow >= PAD_FRONT) & ((mrow - PAD_FRONT) <= qpos)
    return band | meta


def _attn_probs(qh, kk, mask, sink):
    s = _dot(qh, kk, "nt") * (HEAD_DIM ** -0.5)
    s = jnp.where(mask, s, NEG_INF)
    m = jnp.maximum(jnp.max(s, axis=-1, keepdims=True), sink)
    e = jnp.exp(s - m)
    es = jnp.exp(sink - m)
    z = jnp.sum(e, axis=-1, keepdims=True) + es
    inv = 1.0 / z
    return e * inv, es * inv


def _head(ref_or_val, h):
    return ref_or_val[:, h * HEAD_DIM:(h + 1) * HEAD_DIM]


def _attn_fwd(q, k, v, sinks, comm=None):
    lp = q.shape[0]
    nb = lp // BLOCK

    def body(sink_ref, q_ref, kp_ref, kc_ref, km_ref, vp_ref, vc_ref, vm_ref, o_ref):
        b = pl.program_id(0)
        mask = _attn_mask(b)
        for hk in range(N_KV_HEADS):
            kk = jnp.concatenate([_head(kp_ref, hk), _head(kc_ref, hk), _head(km_ref, hk)], axis=0)
            vv = jnp.concatenate([_head(vp_ref, hk), _head(vc_ref, hk), _head(vm_ref, hk)], axis=0)
            for g in range(Q_PER_KV):
                h = hk * Q_PER_KV + g
                p, _ = _attn_probs(_head(q_ref, h), kk, mask, sink_ref[h])
                o_ref[:, h * HEAD_DIM:(h + 1) * HEAD_DIM] = _dot(p, vv, "nn").astype(BF16)

    cur = lambda b: (b, 0)
    prev = lambda b: (jnp.maximum(b - 1, 0), 0)
    first = lambda b: (0, 0)
    kvs = lambda f: pl.BlockSpec((BLOCK, KV_WIDTH), f)
    (o,), couts = _hosted_call(
        body, comm, out_shape=[jax.ShapeDtypeStruct((lp, ATTN_WIDTH), BF16)], grid=(nb,),
        in_specs=[pl.BlockSpec(memory_space=pltpu.SMEM), pl.BlockSpec((BLOCK, ATTN_WIDTH), cur),
                  kvs(prev), kvs(cur), kvs(first), kvs(prev), kvs(cur), kvs(first)],
        out_specs=[pl.BlockSpec((BLOCK, ATTN_WIDTH), cur)], scratch_shapes=[],
        sem=("parallel",), name="attn_fwd", args=(sinks, q, k, k, k, v, v, v))
    return o, couts


def _attn_bwd(q, k, v, do, sinks, tabs, comm=None):
    lp = q.shape[0]
    nb = lp // BLOCK
    scale = HEAD_DIM ** -0.5

    def body(sink_ref, q_ref, do_ref, kp_ref, kc_ref, km_ref, vp_ref, vc_ref, vm_ref, tq_ref, tk_ref, t0_ref,
             dq_ref, dk_ref, dv_ref, dkm_ref, dvm_ref, dsink_ref,
             dq_s, dkk_s, dvv_s, ck_s, cv_s, mk_s, mv_s):
        b = pl.program_id(0)

        @pl.when(b == 0)
        def _():
            for r in (ck_s, cv_s, mk_s, mv_s, dsink_ref):
                r[...] = jnp.zeros_like(r)

        @pl.when(b < nb)
        def _():
            mask = _attn_mask(b)
            for hk in range(N_KV_HEADS):
                kk = jnp.concatenate([_head(kp_ref, hk), _head(kc_ref, hk), _head(km_ref, hk)], axis=0)
                vv = jnp.concatenate([_head(vp_ref, hk), _head(vc_ref, hk), _head(vm_ref, hk)], axis=0)
                dkk = jnp.zeros((3 * BLOCK, HEAD_DIM), F32)
                dvv = jnp.zeros((3 * BLOCK, HEAD_DIM), F32)
                for g in range(Q_PER_KV):
                    h = hk * Q_PER_KV + g
                    qh = _head(q_ref, h)
                    doh = _head(do_ref, h)
                    p, ps = _attn_probs(qh, kk, mask, sink_ref[h])
                    dp = _dot(doh, vv, "nt")
                    delta = jnp.sum(p * dp, axis=-1, keepdims=True)
                    ds = (p * (dp - delta)).astype(BF16)
                    dsink_ref[h:h + 1, :] += jnp.zeros((1, LANES), F32) - jnp.sum(ps * delta)
                    dq_s[:, h * HEAD_DIM:(h + 1) * HEAD_DIM] = _dot(ds, kk, "nn") * scale
                    dkk = dkk + _dot(ds, qh, "tn") * scale
                    dvv = dvv + _dot(p, doh, "tn")
                dkk_s[:, hk * HEAD_DIM:(hk + 1) * HEAD_DIM] = dkk
                dvv_s[:, hk * HEAD_DIM:(hk + 1) * HEAD_DIM] = dvv
            dq_ref[...] = _rope_t(dq_s[...], tq_ref[0], tq_ref[1], tq_ref[2])
            dk_ref[...] = _rope_t(ck_s[...] + dkk_s[0:BLOCK, :], tk_ref[0], tk_ref[1], tk_ref[2])
            dv_ref[...] = cv_s[...] + dvv_s[0:BLOCK, :]
            ck_s[...] = dkk_s[BLOCK:2 * BLOCK, :]
            cv_s[...] = dvv_s[BLOCK:2 * BLOCK, :]
            mk_s[...] += dkk_s[2 * BLOCK:, :]
            mv_s[...] += dvv_s[2 * BLOCK:, :]

        @pl.when(b == nb)
        def _():
            dk_ref[...] = _rope_t(ck_s[...], tk_ref[0], tk_ref[1], tk_ref[2])
            dv_ref[...] = cv_s[...]
            dkm_ref[...] = _rope_t(mk_s[...], t0_ref[0], t0_ref[1], t0_ref[2])
            dvm_ref[...] = mv_s[...]

    cur = lambda b: (jnp.minimum(b, nb - 1), 0)
    prev = lambda b: (jnp.clip(b - 1, 0, nb - 1), 0)
    first = lambda b: (0, 0)
    kvs = lambda f: pl.BlockSpec((BLOCK, KV_WIDTH), f)
    tab = lambda f: pl.BlockSpec((3, BLOCK, LANES), lambda b: (0,) + f(b)[:1] + (0,))
    kv_out = lambda b: (jnp.maximum(b - 1, 0), 0)
    return _hosted_call(
        body, comm,
        out_shape=[jax.ShapeDtypeStruct((lp, ATTN_WIDTH), F32), jax.ShapeDtypeStruct((lp, KV_WIDTH), F32),
                   jax.ShapeDtypeStruct((lp, KV_WIDTH), F32), jax.ShapeDtypeStruct((BLOCK, KV_WIDTH), F32),
                   jax.ShapeDtypeStruct((BLOCK, KV_WIDTH), F32), jax.ShapeDtypeStruct((N_Q_HEADS, LANES), F32)],
        grid=(nb + 1,),
        in_specs=[pl.BlockSpec(memory_space=pltpu.SMEM), pl.BlockSpec((BLOCK, ATTN_WIDTH), cur),
                  pl.BlockSpec((BLOCK, ATTN_WIDTH), cur),
                  kvs(prev), kvs(cur), kvs(first), kvs(prev), kvs(cur), kvs(first),
                  tab(cur), tab(kv_out), tab(first)],
        out_specs=[pl.BlockSpec((BLOCK, ATTN_WIDTH), cur), kvs(kv_out), kvs(kv_out), kvs(first), kvs(first),
                   pl.BlockSpec((N_Q_HEADS, LANES), first)],
        scratch_shapes=[pltpu.VMEM((BLOCK, ATTN_WIDTH), F32), pltpu.VMEM((3 * BLOCK, KV_WIDTH), F32),
                        pltpu.VMEM((3 * BLOCK, KV_WIDTH), F32), pltpu.VMEM((BLOCK, KV_WIDTH), F32),
                        pltpu.VMEM((BLOCK, KV_WIDTH), F32), pltpu.VMEM((BLOCK, KV_WIDTH), F32),
                        pltpu.VMEM((BLOCK, KV_WIDTH), F32)],
        sem=("arbitrary",), name="attn_bwd", args=(sinks, q, do, k, k, k, v, v, v, tabs, tabs, tabs))


def _cmul(ar, ai, br, bi):
    return ar * br - ai * bi, ar * bi + ai * br


def _cpow(lr, li, n):
    rr = ri = None
    br, bi = lr, li
    while n:
        if n & 1:
            rr, ri = (br, bi) if rr is None else _cmul(rr, ri, br, bi)
        n >>= 1
        if n:
            br, bi = _cmul(br, bi, br, bi)
    return rr, ri


def _shift_rows(x, d, reverse):
    rows = lax.broadcasted_iota(jnp.int32, x.shape, 0)
    if not reverse:
        return jnp.where(rows >= d, pltpu.roll(x, d, 0), 0.0)
    return jnp.where(rows < SUBLANES - d, pltpu.roll(x, SUBLANES - d, 0), 0.0)


def _sublane_powers(mr, mi, reverse):
    rows = lax.broadcasted_iota(jnp.int32, mr.shape, 0)
    e = SUBLANES - 1 - rows if reverse else rows
    pr, pi = jnp.ones_like(mr), jnp.zeros_like(mr)
    br, bi = mr, mi
    for d in (1, 2, 4):
        tr, ti = _cmul(pr, pi, br, bi)
        on = (e & d) != 0
        pr, pi = jnp.where(on, tr, pr), jnp.where(on, ti, pi)
        if d < 4:
            br, bi = _cmul(br, bi, br, bi)
    return pr, pi


def _inclusive_prefix(er, ei, mr, mi, reverse):
    ir, ii, pr, pi = er, ei, mr, mi
    for d in (1, 2, 4):
        tr, ti = _cmul(pr, pi, _shift_rows(ir, d, reverse), _shift_rows(ii, d, reverse))
        ir, ii = ir + tr, ii + ti
        if d < 4:
            pr, pi = _cmul(pr, pi, pr, pi)
    return ir, ii


def _chain_rows(a, t, seg):
    return pl.ds(a * SUBLANES * seg + t, SUBLANES, stride=seg)


def _seg_scan(xr_ref, xi_ref, lam, seg, nchain, reverse, store, init, extra=None):
    nt = len(lam)
    acc0 = () if extra is None else extra[1]

    def step(i, carry):
        hs, acc = carry
        t = seg - 1 - i if reverse else i
        out = []
        for a in range(nchain):
            sl = _chain_rows(a, t, seg)
            for j in range(nt):
                lr, li = lam[j]
                k = 2 * (a * nt + j)
                hr, hi = hs[k], hs[k + 1]
                nr = lr * hr - li * hi + xr_ref[j, sl, :]
                ni = lr * hi + li * hr + xi_ref[j, sl, :]
                if store:
                    xr_ref[j, sl, :] = nr
                    xi_ref[j, sl, :] = ni
                if extra is not None:
                    acc = extra[0](t, a, j, nr, ni, acc)
                out += [nr, ni]
        return tuple(out), acc

    return lax.fori_loop(0, seg, step, (tuple(init), acc0))


def _ssm_scan(xr_ref, xi_ref, lam, seg, nchain, reverse, extra=None):
    nt = len(lam)
    zero = [jnp.zeros((SUBLANES, LANES), F32)] * (2 * nt * nchain)
    ends, _ = _seg_scan(xr_ref, xi_ref, lam, seg, nchain, reverse, False, zero)
    init = [None] * (2 * nt * nchain)
    last = 0 if reverse else SUBLANES - 1
    for j in range(nt):
        mr, mi = _cpow(lam[j][0], lam[j][1], seg)
        m8r, m8i = _cpow(mr, mi, SUBLANES)
        pwr, pwi = _sublane_powers(mr, mi, reverse)
        gr = gi = jnp.zeros((SUBLANES, LANES), F32)
        for a in (reversed(range(nchain)) if reverse else range(nchain)):
            k = 2 * (a * nt + j)
            incr, inci = _inclusive_prefix(ends[k], ends[k + 1], mr, mi, reverse)
            tr, ti = _cmul(pwr, pwi, gr, gi)
            init[k] = _shift_rows(incr, 1, reverse) + tr
            init[k + 1] = _shift_rows(inci, 1, reverse) + ti
            g2r, g2i = _cmul(m8r, m8i, gr, gi)
            gr = g2r + jnp.broadcast_to(incr[last:last + 1, :], gr.shape)
            gi = g2i + jnp.broadcast_to(inci[last:last + 1, :], gi.shape)
    _, acc = _seg_scan(xr_ref, xi_ref, lam, seg, nchain, reverse, True, init, extra)
    return acc


def _diag_mask():
    steps = LANES // SSM_GROUP // GROUPS_PER_TILE
    return (jnp.eye(steps, dtype=F32)[:, None, :, None] * jnp.eye(GROUPS_PER_TILE, dtype=F32)[None, :, None, :])


def _ssm_tables(bb_re, bb_im, c_re, c_im):
    g = bb_re.shape[0]
    nt = g // GROUPS_PER_TILE
    steps = LANES // SSM_GROUP // GROUPS_PER_TILE
    mask = _diag_mask()

    def b_tab(bb):
        x = bb.reshape(nt // steps, steps, GROUPS_PER_TILE, SSM_STATE, SSM_GROUP)
        x = jnp.transpose(x, (0, 1, 4, 2, 3))[:, :, None, None]
        m = jnp.transpose(mask, (0, 2, 3, 1))[None, :, :, :, None, :, None]
        return (x * m).reshape(nt, LANES, TILE_STATES)

    def c_tab(c):
        x = c.reshape(nt // steps, steps, GROUPS_PER_TILE, SSM_GROUP, SSM_STATE)
        x = jnp.transpose(x, (0, 1, 2, 4, 3))[:, :, :, :, None, None]
        m = mask[None, :, :, None, :, :, None]
        return (x * m).reshape(nt, TILE_STATES, LANES)

    return b_tab(bb_re), b_tab(bb_im), c_tab(c_re), c_tab(c_im)


def _ssm_untable_b(db, g):
    nt = g // GROUPS_PER_TILE
    steps = LANES // SSM_GROUP // GROUPS_PER_TILE
    x = db.reshape(nt // steps, steps, GROUPS_PER_TILE, SSM_STATE, steps, GROUPS_PER_TILE, SSM_GROUP)
    m = _diag_mask()[None, :, :, None, :, :, None]
    return jnp.sum(x * m, axis=(4, 5)).reshape(g, SSM_STATE, SSM_GROUP)


def _ssm_untable_c(dc, g):
    nt = g // GROUPS_PER_TILE
    steps = LANES // SSM_GROUP // GROUPS_PER_TILE
    x = dc.reshape(nt // steps, steps, steps, GROUPS_PER_TILE, SSM_GROUP, GROUPS_PER_TILE, SSM_STATE)
    m = jnp.transpose(_diag_mask(), (0, 2, 3, 1))[None, :, :, :, None, :, None]
    out = jnp.sum(x * m, axis=(2, 3))
    return jnp.transpose(out, (0, 1, 3, 2, 4)).reshape(g, SSM_GROUP, SSM_STATE)


def _lam_tiles(lam_ref):
    out = []
    for j in range(TILE_STATES // LANES):
        out.append(jnp.broadcast_to(lam_ref[:, j * LANES:(j + 1) * LANES], (SUBLANES, LANES)))
    return out


def _scan_chains(lp):
    for n in (4, 2, 1):
        if lp % (SUBLANES * n) == 0 and (lp // SUBLANES) % 16 == 0:
            return n
    raise ValueError(lp)


def _split_tiles(dst_ref, rows, val):
    for j in range(val.shape[1] // LANES):
        dst_ref[j, rows, :] = val[:, j * LANES:(j + 1) * LANES]


def _cat_tiles(src_ref, rows):
    njt = src_ref.shape[0]
    return jnp.concatenate([src_ref[j, rows, :] for j in range(njt)], axis=1).astype(BF16)


def _ssm_fwd(u, lam_re, lam_im, tb_re, tb_im, tc_re, tc_im, d_skip, comm=None):
    lp, w = u.shape
    nt = tb_re.shape[0]
    nchain = _scan_chains(lp)
    seg = lp // (SUBLANES * nchain)
    chunk = lp // SUBLANES
    njt = TILE_STATES // LANES

    def body(u_ref, lr_ref, li_ref, br_ref, bi_ref, cr_ref, ci_ref, d_ref, y_ref, yg_ref, xr, xi):
        t = pl.program_id(0)
        for s in range(SUBLANES):
            rs = pl.ds(s * chunk, chunk)
            ub = u_ref[rs, :].astype(BF16)
            _split_tiles(xr, rs, _dot(ub, br_ref[...], "nn"))
            _split_tiles(xi, rs, _dot(ub, bi_ref[...], "nn"))
        lrs, lis = _lam_tiles(lr_ref), _lam_tiles(li_ref)
        _ssm_scan(xr, xi, list(zip(lrs, lis)), seg, nchain, False)
        for s in range(SUBLANES):
            rs = pl.ds(s * chunk, chunk)
            y = _dot(_cat_tiles(xr, rs), cr_ref[...], "nn") - _dot(_cat_tiles(xi, rs), ci_ref[...], "nn")

            @pl.when(t % 2 == 0)
            def _():
                y_ref[rs, :] = y + d_ref[...] * u_ref[rs, :]

            @pl.when(t % 2 == 1)
            def _():
                total = y_ref[rs, :] + y
                y_ref[rs, :] = total
                yg_ref[rs, :] = _gelu(total).astype(BF16)

    blk = pl.BlockSpec((lp, LANES), lambda t: (0, t // 2))
    lam_spec = pl.BlockSpec((None, 1, TILE_STATES), lambda t: (t, 0, 0))
    b_spec = pl.BlockSpec((None, LANES, TILE_STATES), lambda t: (t, 0, 0))
    c_spec = pl.BlockSpec((None, TILE_STATES, LANES), lambda t: (t, 0, 0))
    (y, yg), couts = _hosted_call(
        body, comm, out_shape=[jax.ShapeDtypeStruct((lp, w), F32), jax.ShapeDtypeStruct((lp, w), BF16)], grid=(nt,),
        in_specs=[blk, lam_spec, lam_spec, b_spec, b_spec, c_spec, c_spec,
                  pl.BlockSpec((1, LANES), lambda t: (0, t // 2))],
        out_specs=[blk, blk],
        scratch_shapes=[pltpu.VMEM((njt, lp, LANES), F32), pltpu.VMEM((njt, lp, LANES), F32)],
        sem=("arbitrary",), name="ssm_fwd",
        args=(u, lam_re, lam_im, tb_re, tb_im, tc_re, tc_im, d_skip.reshape(1, w)))
    return (y, yg), couts


def _ssm_bwd(u, dy, lam_re, lam_im, tb_re, tb_im, tc_re, tc_im, d_skip, comm=None):
    lp, w = u.shape
    nt = tb_re.shape[0]
    nchain = _scan_chains(lp)
    seg = lp // (SUBLANES * nchain)
    chunk = lp // SUBLANES
    njt = TILE_STATES // LANES
    tbt_re, tbt_im = jnp.swapaxes(tb_re, 1, 2), jnp.swapaxes(tb_im, 1, 2)
    tct_re, tct_im = jnp.swapaxes(tc_re, 1, 2), jnp.swapaxes(tc_im, 1, 2)

    def body(u_ref, dy_ref, lr_ref, li_ref, br_ref, bi_ref, btr_ref, bti_ref, ctr_ref, cti_ref, d_ref,
             du_ref, dlr_ref, dli_ref, dbr_ref, dbi_ref, dcr_ref, dci_ref, dd_ref, hr, hi, ar, ai):
        t = pl.program_id(0)
        lrs, lis = _lam_tiles(lr_ref), _lam_tiles(li_ref)
        for s in range(SUBLANES):
            rs = pl.ds(s * chunk, chunk)
            ub = u_ref[rs, :].astype(BF16)
            dyb = dy_ref[rs, :].astype(BF16)
            _split_tiles(hr, rs, _dot(ub, br_ref[...], "nn"))
            _split_tiles(hi, rs, _dot(ub, bi_ref[...], "nn"))
            _split_tiles(ar, rs, _dot(dyb, ctr_ref[...], "nn"))
            _split_tiles(ai, rs, -_dot(dyb, cti_ref[...], "nn"))
        _ssm_scan(hr, hi, list(zip(lrs, lis)), seg, nchain, False)

        def dlam_step(tt, a, j, a_r, a_i, acc):
            sl = _chain_rows(a, jnp.maximum(tt - 1, 0), seg)
            p_r, p_i = hr[j, sl, :], hi[j, sl, :]
            acc = list(acc)
            acc[2 * j] = acc[2 * j] + jnp.where(tt > 0, a_r * p_r + a_i * p_i, 0.0)
            acc[2 * j + 1] = acc[2 * j + 1] + jnp.where(tt > 0, a_i * p_r - a_r * p_i, 0.0)
            return tuple(acc)

        zero = tuple([jnp.zeros((SUBLANES, LANES), F32)] * (2 * njt))
        conj = [(lr, -li) for lr, li in zip(lrs, lis)]
        acc = list(_ssm_scan(ar, ai, conj, seg, nchain, True, (dlam_step, zero)))
        row0 = lax.broadcasted_iota(jnp.int32, (SUBLANES, LANES), 0) == 0
        for j in range(njt):
            cs = slice(j * LANES, (j + 1) * LANES)
            for a in range(nchain):
                p_r = _shift_rows(hr[j, _chain_rows(a, seg - 1, seg), :], 1, False)
                p_i = _shift_rows(hi[j, _chain_rows(a, seg - 1, seg), :], 1, False)
                if a > 0:
                    before = pl.ds(a * SUBLANES * seg - 1, 1)
                    p_r = jnp.where(row0, jnp.broadcast_to(hr[j, before, :], p_r.shape), p_r)
                    p_i = jnp.where(row0, jnp.broadcast_to(hi[j, before, :], p_i.shape), p_i)
                a_r, a_i = ar[j, _chain_rows(a, 0, seg), :], ai[j, _chain_rows(a, 0, seg), :]
                acc[2 * j] = acc[2 * j] + a_r * p_r + a_i * p_i
                acc[2 * j + 1] = acc[2 * j + 1] + a_i * p_r - a_r * p_i
            dlr_ref[:, cs] = jnp.sum(acc[2 * j], axis=0, keepdims=True)
            dli_ref[:, cs] = jnp.sum(acc[2 * j + 1], axis=0, keepdims=True)

        dd = jnp.zeros((1, LANES), F32)
        for s in range(SUBLANES):
            rs = pl.ds(s * chunk, chunk)
            ub = u_ref[rs, :].astype(BF16)
            dyv = dy_ref[rs, :]
            dyb = dyv.astype(BF16)
            arb, aib = _cat_tiles(ar, rs), _cat_tiles(ai, rs)
            hrb, hib = _cat_tiles(hr, rs), _cat_tiles(hi, rs)
            du = _dot(arb, btr_ref[...], "nn") + _dot(aib, bti_ref[...], "nn")
            upd = [(dbr_ref, _dot(arb, ub, "tn")), (dbi_ref, _dot(aib, ub, "tn")),
                   (dcr_ref, _dot(dyb, hrb, "tn")), (dci_ref, -_dot(dyb, hib, "tn"))]
            for ref, val in upd:
                if s == 0:
                    ref[...] = val
                else:
                    ref[...] += val
            rows = lax.broadcasted_iota(jnp.int32, (chunk, LANES), 0) + s * chunk
            keep = rows >= PAD_FRONT
            dd = dd + jnp.sum(dyv * u_ref[rs, :], axis=0, keepdims=True)

            @pl.when(t % 2 == 0)
            def _():
                du_ref[rs, :] = jnp.where(keep, du + d_ref[...] * dyv, 0.0)

            @pl.when(t % 2 == 1)
            def _():
                du_ref[rs, :] += jnp.where(keep, du, 0.0)

        @pl.when(t % 2 == 0)
        def _():
            dd_ref[...] = dd

    blk = pl.BlockSpec((lp, LANES), lambda t: (0, t // 2))
    vec = pl.BlockSpec((1, LANES), lambda t: (0, t // 2))
    lam_spec = pl.BlockSpec((None, 1, TILE_STATES), lambda t: (t, 0, 0))
    b_spec = pl.BlockSpec((None, LANES, TILE_STATES), lambda t: (t, 0, 0))
    c_spec = pl.BlockSpec((None, TILE_STATES, LANES), lambda t: (t, 0, 0))
    lam_shape = jax.ShapeDtypeStruct((nt, 1, TILE_STATES), F32)
    bt_shape = jax.ShapeDtypeStruct((nt, TILE_STATES, LANES), F32)
    ct_shape = jax.ShapeDtypeStruct((nt, LANES, TILE_STATES), F32)
    st = pltpu.VMEM((njt, lp, LANES), F32)
    return _hosted_call(
        body, comm,
        out_shape=[jax.ShapeDtypeStruct((lp, w), F32), lam_shape, lam_shape, bt_shape, bt_shape, ct_shape, ct_shape,
                   jax.ShapeDtypeStruct((1, w), F32)],
        grid=(nt,),
        in_specs=[blk, blk, lam_spec, lam_spec, b_spec, b_spec, c_spec, c_spec, b_spec, b_spec, vec],
        out_specs=[blk, lam_spec, lam_spec, c_spec, c_spec, b_spec, b_spec, vec],
        scratch_shapes=[st, st, st, st], sem=("arbitrary",), name="ssm_bwd",
        args=(u, dy, lam_re, lam_im, tb_re, tb_im, tbt_re, tbt_im, tct_re, tct_im, d_skip.reshape(1, w)))


def _ssm_params(a_re, a_im, log_dt, b_re, b_im):
    dt = jnp.exp(log_dt)[:, None]
    mag = jnp.exp(a_re * dt)
    lb_re = mag * jnp.cos(a_im * dt)
    lb_im = mag * jnp.sin(a_im * dt)
    den = a_re * a_re + a_im * a_im
    num_re = lb_re - 1.0
    coef_re = (num_re * a_re + lb_im * a_im) / den
    coef_im = (lb_im * a_re - num_re * a_im) / den
    bb_re = coef_re[..., None] * b_re - coef_im[..., None] * b_im
    bb_im = coef_re[..., None] * b_im + coef_im[..., None] * b_re
    return lb_re, lb_im, bb_re, bb_im


def _merge_fwd(o, yg, ga, gs, w3t, comm=None):
    lp, d = ga.shape
    kw = w3t.shape[2]
    tm = _row_tile(lp)

    def epi(accs, in_refs, out_refs):
        attn, vv, gg = accs
        out_refs[0][...] = (_sigmoid(in_refs[5][...]) * attn
                            + _sigmoid(in_refs[6][...]) * (vv * _sigmoid(gg))).astype(BF16)

    wspec = lambda which: pl.BlockSpec((None, d, kw), lambda i: (which, 0, 0))
    rowspec = pl.BlockSpec((tm, d), lambda i: (i, 0))
    aspec = pl.BlockSpec((tm, kw), lambda i: (i, 0))
    res = _matmul(
        "merge_fwd", (lp // tm,), None, [o, yg, w3t, w3t, w3t, ga, gs],
        [aspec, aspec, wspec(0), wspec(1), wspec(2), rowspec, rowspec],
        [(0, 2, "nt", 0), (1, 3, "nt", 1), (1, 4, "nt", 2)], [(tm, d)] * 3, epi,
        [jax.ShapeDtypeStruct((lp, d), BF16)], [rowspec], ("parallel",), comm)
    return (res[0], []) if comm is None else (res[0][0], res[1])


def _out_proj(merged, w_out, h, next_gain, comm=None):
    lp, d = h.shape
    tm = _row_tile(lp)
    shapes, specs, extra_in, extra_specs = _residual_specs(lp, d, tm, next_gain)

    def epi(accs, in_refs, out_refs):
        _residual_outputs(in_refs[2][...] + accs[0], in_refs, out_refs, 3 if extra_in else None)

    rowspec = pl.BlockSpec((tm, d), lambda i: (i, 0))
    res = _matmul(
        "mix_out_proj", (lp // tm,), None, [merged, w_out, h] + extra_in,
        [rowspec, pl.BlockSpec((d, d), lambda i: (0, 0)), rowspec] + extra_specs,
        [(0, 1, "nn", 0)], [(tm, d)], epi, shapes, specs, ("parallel",), comm)
    return (res, []) if comm is None else res


def _merge_bwd(dhb, w_out, o, yg, ga, gs, w3t):
    lp, d = ga.shape
    kw = w3t.shape[2]
    tm = _row_tile(lp)

    def epi(accs, in_refs, out_refs):
        dm, attn, vv, gg = accs
        sa = _sigmoid(in_refs[7][...])
        ss = _sigmoid(in_refs[8][...])
        sg = _sigmoid(gg)
        ssm = vv * sg
        dssm = dm * ss
        out_refs[0][...] = (dm * sa).astype(BF16)
        out_refs[1][...] = (dssm * sg).astype(BF16)
        out_refs[2][...] = (dssm * vv * sg * (1.0 - sg)).astype(BF16)
        out_refs[3][...] = (dm * attn * sa * (1.0 - sa)).astype(BF16)
        out_refs[4][...] = (dm * ssm * ss * (1.0 - ss)).astype(BF16)

    wspec = lambda which: pl.BlockSpec((None, d, kw), lambda i: (which, 0, 0))
    rowspec = pl.BlockSpec((tm, d), lambda i: (i, 0))
    aspec = pl.BlockSpec((tm, kw), lambda i: (i, 0))
    shp = jax.ShapeDtypeStruct((lp, d), BF16)
    return _matmul(
        "merge_bwd", (lp // tm,), None, [dhb, w_out, o, yg, w3t, w3t, w3t, ga, gs],
        [rowspec, pl.BlockSpec((d, d), lambda i: (0, 0)), aspec, aspec, wspec(0), wspec(1), wspec(2), rowspec,
         rowspec],
        [(0, 1, "nt", 0), (2, 4, "nt", 1), (3, 5, "nt", 2), (3, 6, "nt", 3)], [(tm, d)] * 4, epi,
        [shp] * 5, [rowspec] * 5, ("parallel",))


def _branch_bwd(dattn, dv, dg, w3t, y):
    lp, d = dattn.shape
    kw = w3t.shape[2]
    tm = _row_tile(lp)

    def epi(accs, in_refs, out_refs):
        out_refs[0][...] = accs[0].astype(BF16)
        out_refs[1][...] = accs[1] * _gelu_grad(in_refs[6][...])

    wspec = lambda which: pl.BlockSpec((None, d, kw), lambda i: (which, 0, 0))
    rowspec = pl.BlockSpec((tm, d), lambda i: (i, 0))
    aspec = pl.BlockSpec((tm, kw), lambda i: (i, 0))
    return _matmul(
        "branch_bwd", (lp // tm,), None, [dattn, dv, dg, w3t, w3t, w3t, y],
        [rowspec, rowspec, rowspec, wspec(0), wspec(1), wspec(2), aspec],
        [(0, 3, "nn", 0), (1, 4, "nn", 1), (2, 5, "nn", 1)], [(tm, kw)] * 2, epi,
        [jax.ShapeDtypeStruct((lp, kw), BF16), jax.ShapeDtypeStruct((lp, kw), F32)], [aspec, aspec],
        ("parallel",))


def _sibling_half(acc, rows):
    half = rows // 2
    return jnp.where(lax.axis_index("c") == 0, acc[half:rows], acc[:half]).astype(BF16)


def _tn_cols(x, ys, name):
    lp, kx = x.shape
    n = ys[0].shape[1]
    n4 = n // N_CHIPS
    tk = _tn_tiles(lp)
    ny = len(ys)

    def epi(accs, in_refs, out_refs):
        for i, acc in enumerate(accs):
            out_refs[i][...] = acc
            out_refs[ny + i][...] = _sibling_half(acc, kx)

    shp = jax.ShapeDtypeStruct((N_CHIPS, kx, n4), F32)
    shp_half = jax.ShapeDtypeStruct((N_CHIPS, kx // 2, n4), BF16)
    res = _matmul(
        name, (N_CHIPS, lp // tk), 1, [x] + list(ys),
        [pl.BlockSpec((tk, kx), lambda j, k: (k, 0))] + [pl.BlockSpec((tk, n4), lambda j, k: (k, j))] * ny,
        [(0, 1 + i, "tn", i) for i in range(ny)], [(kx, n4)] * ny, epi,
        [shp] * ny + [shp_half] * ny,
        [pl.BlockSpec((None, kx, n4), lambda j, k: (j, 0, 0))] * ny
        + [pl.BlockSpec((None, kx // 2, n4), lambda j, k: (j, 0, 0))] * ny,
        ("arbitrary", "arbitrary"))
    return res[:ny], res[ny:]


def _tn_full(x, y, name, tn_cols=None):
    lp, kx = x.shape
    n = y.shape[1]
    tk = _tn_tiles(lp)
    tn = n if tn_cols is None else tn_cols
    k4 = kx // N_CHIPS

    def epi(accs, in_refs, out_refs):
        for j in range(N_CHIPS):
            slab = accs[0][j * k4:(j + 1) * k4]
            out_refs[0][j] = slab
            out_refs[1][j] = _sibling_half(slab, k4)

    return _matmul(
        name, (n // tn, lp // tk), 1, [x, y],
        [pl.BlockSpec((tk, kx), lambda j, k: (k, 0)), pl.BlockSpec((tk, tn), lambda j, k: (k, j))],
        [(0, 1, "tn", 0)], [(kx, tn)], epi,
        [jax.ShapeDtypeStruct((N_CHIPS, k4, n), F32), jax.ShapeDtypeStruct((N_CHIPS, k4 // 2, n), BF16)],
        [pl.BlockSpec((N_CHIPS, k4, tn), lambda j, k: (0, 0, j)),
         pl.BlockSpec((N_CHIPS, k4 // 2, tn), lambda j, k: (0, 0, j))],
        ("arbitrary", "arbitrary"))


def _in_proj_bwd(dz, w_in, dh, h_in, gain):
    lp, d = h_in.shape
    inw = w_in.shape[0]
    tm = _row_tile(lp)

    def epi(accs, in_refs, out_refs):
        i = pl.program_id(0)
        dx, dgrow = _rms_bwd_math(accs[0], in_refs[3][...], in_refs[4][...])
        dh_new = in_refs[2][...] + dx
        out_refs[0][...] = dh_new
        out_refs[2][...] = dh_new.astype(BF16)

        @pl.when(i == 0)
        def _():
            out_refs[1][...] = jnp.zeros_like(out_refs[1])

        out_refs[1][...] += jnp.sum(dgrow, axis=0, keepdims=True)

    row = pl.BlockSpec((tm, d), lambda i: (i, 0))
    return _matmul(
        "mix_in_proj_bwd", (lp // tm,), None, [dz, w_in, dh, h_in, gain.reshape(1, d)],
        [pl.BlockSpec((tm, inw), lambda i: (i, 0)), pl.BlockSpec((inw, d), lambda i: (0, 0)), row, row,
         pl.BlockSpec((1, d), lambda i: (0, 0))],
        [(0, 1, "nn", 0)], [(tm, d)], epi,
        [jax.ShapeDtypeStruct((lp, d), F32), jax.ShapeDtypeStruct((1, d), F32), jax.ShapeDtypeStruct((lp, d), BF16)],
        [row, pl.BlockSpec((1, d), lambda i: (0, 0)), row], ("arbitrary",))


def _loss_head(h, gain, target):
    lp, d = h.shape
    nb = lp // BLOCK

    def body(h_ref, g_ref, t_ref, dh_ref, dg_ref, loss_ref, dhb_ref):
        i = pl.program_id(0)

        @pl.when(i == 0)
        def _():
            dg_ref[...] = jnp.zeros_like(dg_ref)
            loss_ref[...] = jnp.zeros_like(loss_ref)
            dh_ref[...] = jnp.zeros_like(dh_ref)
            dhb_ref[...] = jnp.zeros_like(dhb_ref)

        @pl.when(i > 0)
        def _():
            x = h_ref[...]
            g = g_ref[...]
            r = lax.rsqrt(jnp.mean(x * x, axis=-1, keepdims=True) + EPS)
            err = x * r * g - t_ref[...]
            loss_ref[...] += jnp.zeros_like(loss_ref) + 0.5 * jnp.sum(jnp.sum(err * err, axis=-1, keepdims=True)) / d
            dx, dgrow = _rms_bwd_math(err * (1.0 / d), x, g)
            dh_ref[...] = dx
            dhb_ref[...] = dx.astype(BF16)
            dg_ref[...] += jnp.sum(dgrow, axis=0, keepdims=True)

    row = pl.BlockSpec((BLOCK, d), lambda i: (i, 0))
    one = pl.BlockSpec((1, d), lambda i: (0, 0))
    return pl.pallas_call(
        body,
        out_shape=[jax.ShapeDtypeStruct((lp, d), F32), jax.ShapeDtypeStruct((1, d), F32),
                   jax.ShapeDtypeStruct((SUBLANES, LANES), F32), jax.ShapeDtypeStruct((lp, d), BF16)],
        grid=(nb,),
        in_specs=[row, one, pl.BlockSpec((BLOCK, d), lambda i: (jnp.maximum(i - 1, 0), 0))],
        out_specs=[row, one, pl.BlockSpec((SUBLANES, LANES), lambda i: (0, 0)), row],
        compiler_params=_cparams(("arbitrary",)), name="loss_head")(h, gain.reshape(1, d), target)


def _adam_math(w, g, m, v):
    m = ADAM_B1 * m + (1.0 - ADAM_B1) * g
    v = ADAM_B2 * v + (1.0 - ADAM_B2) * (g * g)
    m_hat = m / (1.0 - ADAM_B1 ** ADAM_STEP)
    v_hat = v / (1.0 - ADAM_B2 ** ADAM_STEP)
    delta = -ADAM_LR * (m_hat / (jnp.sqrt(v_hat) + ADAM_EPS) + ADAM_WD * w)
    return delta, m, v


def _adamw_layers(w, m, v, mine, other, pos, name):
    depth, r, c = w.shape
    half = r // 2
    tr = _div_tile(half, c * 4)
    nh = half // tr

    def body(*refs):
        pos_ref, w_ref, m_ref, v_ref = refs[:4]
        mine_refs = refs[4:4 + depth]
        other_refs = refs[4 + depth:4 + 2 * depth]
        g_out, d_out, m_out, v_out = refs[4 + 2 * depth:]
        layer, i = pl.program_id(0), pl.program_id(1)
        is_mine = (i // nh) == pos_ref[0]

        def update(g):
            delta, nm, nv = _adam_math(w_ref[...], g, m_ref[...], v_ref[...])
            g_out[...] = g
            d_out[...] = delta
            m_out[...] = nm
            v_out[...] = nv

        for l in range(depth):
            @pl.when((layer == l) & is_mine)
            def _(l=l):
                update(mine_refs[l][...])

            @pl.when((layer == l) & jnp.logical_not(is_mine))
            def _(l=l):
                update(other_refs[l][...])

    stacked = pl.BlockSpec((None, tr, c), lambda l, i, p: (l, i, 0))

    def gspec(layer, is_other):
        def imap(l, i, p):
            first = jnp.where(is_other, 1 - p[0], p[0]) * nh
            here = jnp.clip(i - first, 0, nh - 1)
            return (jnp.where(l == layer, here, jnp.where(l < layer, 0, nh - 1)), 0)
        return pl.BlockSpec((tr, c), imap)

    shp = jax.ShapeDtypeStruct((depth, r, c), F32)
    grid_spec = pltpu.PrefetchScalarGridSpec(
        num_scalar_prefetch=1, grid=(depth, 2 * nh),
        in_specs=[stacked] * 3 + [gspec(l, 0) for l in range(depth)] + [gspec(l, 1) for l in range(depth)],
        out_specs=[stacked] * 4)
    return pl.pallas_call(
        body, out_shape=[shp] * 4, grid_spec=grid_spec,
        compiler_params=_cparams(("arbitrary", "arbitrary")), name=name)(pos, w, m, v, *mine, *other)


def _adamw_whole(w, g, m, v, name):
    def body(w_ref, g_ref, m_ref, v_ref, d_out, m_out, v_out):
        delta, nm, nv = _adam_math(w_ref[...], g_ref[...], m_ref[...], v_ref[...])
        d_out[...] = delta
        m_out[...] = nm
        v_out[...] = nv

    shp = jax.ShapeDtypeStruct(w.shape, F32)
    return pl.pallas_call(body, out_shape=[shp] * 3, compiler_params=_cparams(), name=name)(w, g, m, v)


def _mesh_pos():
    return lax.axis_index("x"), lax.axis_index("y"), lax.axis_index("c")


def _row_half(ref, which, lead):
    half = ref.shape[lead] // 2
    idx = (slice(None),) * lead + (pl.ds(which * half, half), slice(None))
    return ref.at[idx]


def _gather_comm(arrs, tag):
    n = len(arrs)

    def ctx(ins, outs, sems):
        send_sems, recv_sems, local_sems = sems
        x, y, c = _mesh_pos()
        chips = [(1 - x, y), (x, 1 - y), (1 - x, 1 - y)]

        def slot(k, chip, which):
            lead = len(ins[k].shape) - 2
            return _row_half(outs[k].at[2 * chip[0] + chip[1]], which, lead)

        def copy(k, j, src, dst, to):
            return pltpu.make_async_remote_copy(
                src_ref=src, dst_ref=dst, send_sem=send_sems.at[6 * k + j], recv_sem=recv_sems.at[6 * k + j],
                device_id=to, device_id_type=MESH)

        def local(k):
            return pltpu.make_async_copy(ins[k], outs[k].at[2 * x + y], local_sems.at[k])

        def first(k, j):
            lead = len(ins[k].shape) - 2
            return copy(k, j, _row_half(ins[k], c, lead), slot(k, (x, y), c), (*chips[j], c))

        def passed(k, j, which):
            return copy(k, 3 + j, slot(k, chips[j], which), slot(k, chips[j], which), (x, y, 1 - c))

        def landed(k, j):
            return copy(k, j, slot(k, chips[j], c), slot(k, chips[j], c), (x, y, 1 - c))

        return c, local, first, passed, landed

    def start(ins, outs, sems):
        c, local, first, passed, landed = ctx(ins, outs, sems)
        for k in range(n):
            local(k).start()
            for j in range(3):
                first(k, j).start()

    def mid(ins, outs, sems):
        c, local, first, passed, landed = ctx(ins, outs, sems)
        for j in range(3):
            for k in range(n):
                landed(k, j).wait_recv()
                passed(k, j, c).start()

    def finish(ins, outs, sems):
        c, local, first, passed, landed = ctx(ins, outs, sems)
        for j in range(3):
            for k in range(n):
                passed(k, j, 1 - c).wait_recv()
        for k in range(n):
            for j in range(3):
                first(k, j).wait_send()
                passed(k, j, c).wait_send()
            local(k).wait()

    return _Comm(
        tag, arrs, [jax.ShapeDtypeStruct((N_CHIPS,) + a.shape, a.dtype) for a in arrs],
        [pltpu.SemaphoreType.DMA((6 * n,)), pltpu.SemaphoreType.DMA((6 * n,)), pltpu.SemaphoreType.DMA((n,))],
        start, mid, finish)


def _run_comm(comm, name):
    n_in, n_out = len(comm.ins), len(comm.out_shapes)

    def body(*refs):
        ins, outs, sems = refs[:n_in], refs[n_in:n_in + n_out], refs[n_in + n_out:]
        comm.start(ins, outs, sems)
        if comm.mid is not None:
            comm.mid(ins, outs, sems)
        comm.finish(ins, outs, sems)

    return pl.pallas_call(
        body, out_shape=comm.out_shapes, in_specs=[HBM_SPEC] * n_in, out_specs=[HBM_SPEC] * n_out,
        scratch_shapes=comm.sems, name=name)(*comm.ins)


def _all_gather_chips(arrs, name):
    return _run_comm(_gather_comm(arrs, "gather"), name)


GATHER_US_PER_BYTE = 380.0 / 11.65e6
HOST_US = dict(ffn_up=68.0, ffn_down=37.0, in_proj=38.0, attn_fwd=103.0, ssm_fwd=70.0, merge_fwd=30.0,
               out_proj=23.0)
HOST_SLACK_US = 10.0


class _WeightStream:
    def __init__(self, pieces):
        self.keys = [k for k, _ in pieces]
        self.shards = dict(pieces)
        self.next = 0
        self.full = {}
        self.pending = []

    def comm_for(self, host):
        budget = HOST_US[host] + HOST_SLACK_US
        taken, cost = [], 0.0
        while self.next < len(self.keys):
            key = self.keys[self.next]
            c = self.shards[key].size * self.shards[key].dtype.itemsize * GATHER_US_PER_BYTE
            if cost + c > budget and taken:
                break
            taken.append(key)
            cost += c
            self.next += 1
        self.pending = taken
        if not taken:
            return None
        return _gather_comm([self.shards[k] for k in taken], "g_" + "_".join(k[1] for k in taken))

    def deposit(self, gathered):
        for key, arr in zip(self.pending, gathered):
            self.full[key] = arr
        self.pending = []

    def get(self, key):
        if key not in self.full:
            upto = self.keys.index(key) + 1
            keys = self.keys[self.next:upto]
            self.next = upto
            for k, arr in zip(keys, _all_gather_chips([self.shards[k] for k in keys], "gather_now")):
                self.full[k] = arr
        return self.full[key]


def _all_gather_devices(x_shard, name):
    m_per, ncol = x_shard.shape

    def body(x_ref, out_ref, send_sems, recv_sems, local_sem):
        x, y, c = _mesh_pos()
        me, sibling = (x, y, c), (x, y, 1 - c)
        chips = [(1 - x, y), (x, 1 - y), (1 - x, 1 - y)]

        def rows(px, py, pc):
            return out_ref.at[4 * px + 2 * py + pc]

        def copy(k, block, to, src=None):
            return pltpu.make_async_remote_copy(
                src_ref=rows(*block) if src is None else src, dst_ref=rows(*block),
                send_sem=send_sems.at[k], recv_sem=recv_sems.at[k], device_id=to, device_id_type=MESH)

        mine = pltpu.make_async_copy(x_ref, rows(*me), local_sem)
        mine.start()
        first = [copy(0, me, sibling, src=x_ref)]
        first += [copy(1 + j, me, (*chip, c), src=x_ref) for j, chip in enumerate(chips)]
        for cp in first:
            cp.start()
        passed = [copy(4 + j, (*chip, c), sibling) for j, chip in enumerate(chips)]
        for j, chip in enumerate(chips):
            copy(1 + j, (*chip, c), me).wait_recv()
            passed[j].start()
        copy(0, sibling, me).wait_recv()
        for j, chip in enumerate(chips):
            copy(4 + j, (*chip, 1 - c), me).wait_recv()
        for cp in first + passed:
            cp.wait_send()
        mine.wait()

    return pl.pallas_call(
        body, out_shape=jax.ShapeDtypeStruct((8, m_per, ncol), x_shard.dtype),
        in_specs=[pl.BlockSpec(memory_space=pltpu.VMEM)], out_specs=pl.BlockSpec(memory_space=pltpu.VMEM),
        scratch_shapes=[pltpu.SemaphoreType.DMA((7,)), pltpu.SemaphoreType.DMA((7,)), pltpu.SemaphoreType.DMA],
        compiler_params=pltpu.CompilerParams(vmem_limit_bytes=VMEM_LIMIT), name=name)(x_shard)


def _sum_devices(g8, name):
    _, r, c = g8.shape
    tr = _div_tile(r, c * 4 * 8)

    def body(g_ref, o_ref):
        acc = g_ref[0]
        for dev in range(1, 8):
            acc = acc + g_ref[dev]
        o_ref[...] = acc

    return pl.pallas_call(
        body, out_shape=jax.ShapeDtypeStruct((r, c), F32), grid=(r // tr,),
        in_specs=[pl.BlockSpec((8, tr, c), lambda i: (0, i, 0))], out_specs=pl.BlockSpec((tr, c), lambda i: (i, 0)),
        compiler_params=_cparams(("parallel",)), name=name)(g8)


def _chip_partials(arrs, recvs, pos, name):
    n = len(arrs)

    def body(pos_ref, *refs):
        for a_ref, b_ref, o_ref in zip(refs[:n], refs[n:2 * n], refs[2 * n:]):
            o_ref[...] = (a_ref[...] + b_ref[...]).astype(BF16)

    own_specs, recv_specs, shapes = [], [], []
    for arr in arrs:
        nslab, r, c = arr.shape
        own_specs.append(pl.BlockSpec((None, r // 2, c), lambda j, p: (j, p[0], 0)))
        recv_specs.append(pl.BlockSpec((None, r // 2, c), lambda j, p: (j, 0, 0)))
        shapes.append(jax.ShapeDtypeStruct((nslab, r // 2, c), BF16))
    grid_spec = pltpu.PrefetchScalarGridSpec(
        num_scalar_prefetch=1, grid=(N_CHIPS,), in_specs=own_specs + recv_specs, out_specs=recv_specs)
    return pl.pallas_call(
        body, out_shape=shapes, grid_spec=grid_spec,
        compiler_params=_cparams(("parallel",)), name=name)(pos, *arrs, *recvs)


def _chip_exchange_comm(parts, tag):
    n = len(parts)

    def copies(ins, outs, sems):
        send_sems, recv_sems = sems
        x, y, c = _mesh_pos()
        chips = [(1 - x, y), (x, 1 - y), (1 - x, 1 - y)]
        return [pltpu.make_async_remote_copy(
            src_ref=ins[k].at[2 * chip[0] + chip[1]], dst_ref=outs[k].at[j],
            send_sem=send_sems.at[3 * k + j], recv_sem=recv_sems.at[3 * k + j],
            device_id=(*chip, c), device_id_type=MESH) for k in range(n) for j, chip in enumerate(chips)]

    def start(ins, outs, sems):
        for cp in copies(ins, outs, sems):
            cp.start()

    def finish(ins, outs, sems):
        for cp in copies(ins, outs, sems):
            cp.wait()

    return _Comm(
        tag, parts, [jax.ShapeDtypeStruct((3,) + p.shape[1:], p.dtype) for p in parts],
        [pltpu.SemaphoreType.DMA((3 * n,)), pltpu.SemaphoreType.DMA((3 * n,))], start, None, finish)


def _reduce_halves(arrs, recvs, gots, pos, name):
    n = len(arrs)

    def body(pos_ref, *refs):
        for a_ref, b_ref, g_ref, o_ref in zip(refs[:n], refs[n:2 * n], refs[2 * n:3 * n], refs[3 * n:]):
            acc = a_ref[...] + b_ref[...]
            for j in range(3):
                acc = acc + g_ref[j].astype(F32)
            o_ref[...] = acc

    own_specs, recv_specs, got_specs, out_specs, shapes = [], [], [], [], []
    for arr in arrs:
        _, r, c = arr.shape
        own_specs.append(pl.BlockSpec((None, r // 2, c), lambda i, p: (p[1], p[0], 0)))
        recv_specs.append(pl.BlockSpec((None, r // 2, c), lambda i, p: (p[1], 0, 0)))
        got_specs.append(pl.BlockSpec((3, r // 2, c), lambda i, p: (0, 0, 0)))
        out_specs.append(pl.BlockSpec((r // 2, c), lambda i, p: (0, 0)))
        shapes.append(jax.ShapeDtypeStruct((r // 2, c), F32))
    grid_spec = pltpu.PrefetchScalarGridSpec(
        num_scalar_prefetch=1, grid=(1,), in_specs=own_specs + recv_specs + got_specs, out_specs=out_specs)
    return pl.pallas_call(
        body, out_shape=shapes, grid_spec=grid_spec,
        compiler_params=_cparams(("arbitrary",)), name=name)(pos, *arrs, *recvs, *gots)


def _share_halves(halves, name):
    n = len(halves)

    def body(*refs):
        ins, outs = refs[:n], refs[n:2 * n]
        send_sems, recv_sems = refs[2 * n:]
        x, y, c = _mesh_pos()
        cps = []
        for k in range(n):
            cp = pltpu.make_async_remote_copy(
                src_ref=ins[k], dst_ref=outs[k], send_sem=send_sems.at[k], recv_sem=recv_sems.at[k],
                device_id=(x, y, 1 - c), device_id_type=MESH)
            cp.start()
            cps.append(cp)
        for cp in cps:
            cp.wait()

    return pl.pallas_call(
        body, out_shape=[jax.ShapeDtypeStruct(h.shape, h.dtype) for h in halves],
        in_specs=[HBM_SPEC] * n, out_specs=[HBM_SPEC] * n,
        scratch_shapes=[pltpu.SemaphoreType.DMA((n,)), pltpu.SemaphoreType.DMA((n,))], name=name)(*halves)


class _Reduction:
    def __init__(self, arrs, others, pos, tag):
        self.arrs, self.pos, self.tag = arrs, pos, tag
        self.recv = _share_halves(others, "rs_sibling_" + tag)
        self.parts = _chip_partials(arrs, self.recv, pos, "rs_partial_" + tag)
        self.got = None

    def comm(self):
        return _chip_exchange_comm(self.parts, "rs_" + self.tag)

    def end(self):
        if self.got is None:
            self.got = _run_comm(self.comm(), "rs_chips_" + self.tag)
        return _reduce_halves(self.arrs, self.recv, self.got, self.pos, "rs_reduce_" + self.tag)


def _w_in_full(p, l, ws):
    slabs = ws.get((l, "w_in"))
    return slabs.reshape(-1, slabs.shape[2])


def _w3t_full(p, l, ws):
    if "w3t" not in p:
        slabs = ws.get((l, "w3"))
        p["w3t"] = jnp.swapaxes(slabs, 0, 1).reshape(slabs.shape[1], -1, slabs.shape[3])
    return p["w3t"]


def _w_out_full(l, ws):
    slabs = ws.get((l, "w_out"))
    return slabs.reshape(-1, slabs.shape[2])


def _layer_fwd(h, n0, l, p, next_gain, ws, tabs):
    def hosted(host, fn, *args):
        out, got = fn(*args, ws.comm_for(host))
        ws.deposit(got)
        return out

    ffn1_saved = hosted("ffn_up", _ffn_up, n0, ws.get((l, "wg1")), ws.get((l, "wu1")))
    h1, n = hosted("ffn_down", _ffn_down, ffn1_saved[2], ws.get((l, "wd1")), h, p["mix_norm"])
    ssm_w = p["ssm_d"].shape[0]
    q, k, v, u, ga, gs = hosted("in_proj", _in_proj, n, _w_in_full(p, l, ws), tabs, ssm_w)
    o = hosted("attn_fwd", _attn_fwd, q, k, v, p["attn_sinks"])
    y, yg = hosted("ssm_fwd", _ssm_fwd, u, *p["ssm_tabs"], p["ssm_d"])
    merged = hosted("merge_fwd", _merge_fwd, o, yg, ga, gs, _w3t_full(p, l, ws))
    h2, n2 = hosted("out_proj", _out_proj, merged, _w_out_full(l, ws), h1, p["ffn2_norm"])
    ffn2_saved = hosted("ffn_up", _ffn_up, n2, ws.get((l, "wg2")), ws.get((l, "wu2")))
    h3, *n3 = hosted("ffn_down", _ffn_down, ffn2_saved[2], ws.get((l, "wd2")), h2, next_gain)
    saved = dict(h0=h, h1=h1, h2=h2, ffn1=ffn1_saved, ffn2=ffn2_saved, n_mix=n, q=q, k=k, v=v, u=u, ga=ga, gs=gs,
                 o=o, y=y, yg=yg, merged=merged)
    return h3, (n3[0] if n3 else None), saved


def _layer_bwd(dh_pair, l, p, ws, s, tabs, pos):
    g = {}
    (dh2, dhb), g["ffn2_norm"], red_ffn2, _ = _ffn_bwd(
        dh_pair, s["h2"], p["ffn2_norm"], ws.get((l, "wg2")), ws.get((l, "wu2")), ws.get((l, "wd2")), p["f4"],
        s["ffn2"], pos)
    w3, w_out_w = _w3t_full(p, l, ws), _w_out_full(l, ws)
    lp, d = dh2.shape
    d4 = d // N_CHIPS
    dw_out, dw_out_other = _tn_full(s["merged"], dhb, "mix_dw_out")
    dattn, dv, dg, dga, dgs = _merge_bwd(dhb, w_out_w, s["o"], s["yg"], s["ga"], s["gs"], w3)
    (dw_ap,), (dw_ap_other,) = _tn_cols(s["o"], [dattn], "mix_dw_ap")
    (dw_gv, dw_gg), (dw_gv_other, dw_gg_other) = _tn_cols(s["yg"], [dv, dg], "mix_dw_glu")
    do, dy = _branch_bwd(dattn, dv, dg, w3, s["y"])
    (dq, dk, dvv, dkm, dvm, dsink), _ = _attn_bwd(s["q"], s["k"], s["v"], do, p["attn_sinks"], tabs)
    g["attn_sinks"] = dsink[:, 0]
    (du, dlr, dli, dbr, dbi, dcr, dci, dd), _ = _ssm_bwd(s["u"], dy, *p["ssm_tabs"], p["ssm_d"])
    ngrp = p["ssm_d"].shape[0] // SSM_GROUP
    g["ssm_lam"] = (dlr.reshape(ngrp, SSM_STATE), dli.reshape(ngrp, SSM_STATE),
                    _ssm_untable_b(dbr, ngrp), _ssm_untable_b(dbi, ngrp))
    g["ssm_c_re"] = _ssm_untable_c(dcr, ngrp)
    g["ssm_c_im"] = _ssm_untable_c(dci, ngrp)
    g["ssm_d"] = dd[0]
    dk = dk.at[:BLOCK].add(dkm)
    dvv = dvv.at[:BLOCK].add(dvm)
    dz = jnp.concatenate([dq.astype(BF16), dk.astype(BF16), dvv.astype(BF16), du.astype(BF16), dga, dgs], axis=1)
    n = s["n_mix"]
    w_in = _w_in_full(p, l, ws)
    dw_in, dw_in_other = _tn_full(dz, n, "mix_dw_in", d // 2)
    red_mix = _Reduction([dw_in, dw_ap, dw_gv, dw_gg, dw_out],
                         [dw_in_other, dw_ap_other, dw_gv_other, dw_gg_other, dw_out_other], pos, "mix")
    dh1, g["mix_norm"], dh1b = _in_proj_bwd(dz, w_in, dh2, s["h1"], p["mix_norm"])
    dh0_pair, g["ffn1_norm"], red_ffn1, red_mix.got = _ffn_bwd(
        (dh1, dh1b), s["h0"], p["ffn1_norm"], ws.get((l, "wg1")), ws.get((l, "wu1")), ws.get((l, "wd1")), p["f4"],
        s["ffn1"], pos, red_mix.comm())
    return dh0_pair, g, [*red_ffn1, red_mix, *red_ffn2]


BIG = ["ffn1_w_gate", "ffn1_w_up", "ffn1_w_down", "w_in", "w_attn_proj", "w_glu_v", "w_glu_g", "w_out",
       "ffn2_w_gate", "ffn2_w_up", "ffn2_w_down"]
TRANSPOSED = ["ffn1_w_gate", "ffn1_w_up", "w_in", "ffn2_w_gate", "ffn2_w_up"]
SMALL = ["ffn1_norm", "mix_norm", "attn_sinks", "ssm_a_re", "ssm_a_im", "ssm_log_dt", "ssm_b_re", "ssm_b_im",
         "ssm_c_re", "ssm_c_im", "ssm_d", "ffn2_norm", "final_norm"]
WEIGHTS = ["meta_tokens", "ffn1_norm", "ffn1_w_gate", "ffn1_w_up", "ffn1_w_down", "mix_norm", "w_in", "attn_sinks",
           "ssm_a_re", "ssm_a_im", "ssm_log_dt", "ssm_b_re", "ssm_b_im", "ssm_c_re", "ssm_c_im", "ssm_d",
           "w_attn_proj", "w_glu_v", "w_glu_g", "w_out", "ffn2_norm", "ffn2_w_gate", "ffn2_w_up", "ffn2_w_down",
           "final_norm"]


def _small_rows(shape):
    rows = -(-math.prod(shape) // LANES)
    return -(-rows // SUBLANES) * SUBLANES


def _pack_small(tree):
    parts = []
    for k in SMALL + ["meta_tokens"]:
        size, rows = math.prod(tree[k].shape), _small_rows(tree[k].shape)
        if size % LANES == 0:
            part = tree[k].reshape(size // LANES, LANES)
        else:
            part = jnp.pad(tree[k].reshape(1, size), ((0, 0), (0, LANES - size)))
        parts.append(jnp.pad(part, ((0, rows - part.shape[0]), (0, 0))))
    return jnp.concatenate(parts, axis=0)


def _unpack_small(packed, like):
    out, off = {}, 0
    for k in SMALL + ["meta_tokens"]:
        size, rows = math.prod(like[k].shape), _small_rows(like[k].shape)
        if size % LANES == 0:
            out[k] = packed[off:off + size // LANES].reshape(like[k].shape)
        else:
            out[k] = packed[off, :size].reshape(like[k].shape)
        off += rows
    return out


def kernel(x, meta_tokens, ffn1_norm, ffn1_w_gate, ffn1_w_up, ffn1_w_down, mix_norm, w_in, attn_sinks, ssm_a_re, ssm_a_im, ssm_log_dt, ssm_b_re, ssm_b_im, ssm_c_re, ssm_c_im, ssm_d, w_attn_proj, w_glu_v, w_glu_g, w_out, ffn2_norm, ffn2_w_gate, ffn2_w_up, ffn2_w_down, final_norm, loss_target, m_meta_tokens, m_ffn1_norm, m_ffn1_w_gate, m_ffn1_w_up, m_ffn1_w_down, m_mix_norm, m_w_in, m_attn_sinks, m_ssm_a_re, m_ssm_a_im, m_ssm_log_dt, m_ssm_b_re, m_ssm_b_im, m_ssm_c_re, m_ssm_c_im, m_ssm_d, m_w_attn_proj, m_w_glu_v, m_w_glu_g, m_w_out, m_ffn2_norm, m_ffn2_w_gate, m_ffn2_w_up, m_ffn2_w_down, m_final_norm, v_meta_tokens, v_ffn1_norm, v_ffn1_w_gate, v_ffn1_w_up, v_ffn1_w_down, v_mix_norm, v_w_in, v_attn_sinks, v_ssm_a_re, v_ssm_a_im, v_ssm_log_dt, v_ssm_b_re, v_ssm_b_im, v_ssm_c_re, v_ssm_c_im, v_ssm_d, v_w_attn_proj, v_w_glu_v, v_w_glu_g, v_w_out, v_ffn2_norm, v_ffn2_w_gate, v_ffn2_w_up, v_ffn2_w_down, v_final_norm):
    args = dict(locals())
    w = {k: args[k] for k in WEIGHTS}
    m = {k: args["m_" + k] for k in WEIGHTS}
    v = {k: args["v_" + k] for k in WEIGHTS}
    depth = ffn1_norm.shape[0]
    seq, d = x.shape[1], x.shape[2]
    lp = seq + BLOCK
    xi, yi, ci = _mesh_pos()
    pos = jnp.stack([ci, 2 * xi + yi]).astype(jnp.int32)

    tabs = _rope_tables(lp)
    layers, pieces = [], [((0, "meta"), meta_tokens)]
    f4 = ffn1_w_gate.shape[2]
    fp = -(-f4 // MXU_DIM) * MXU_DIM

    def ffn_rows(wt):
        return jnp.pad(wt, ((0, fp - f4), (0, 0))).astype(BF16)

    for l in range(depth):
        small = [((l, "w3"), jnp.stack([w_attn_proj[l].T, w_glu_v[l].T, w_glu_g[l].T]).astype(BF16)),
                 ((l, "w_out"), w_out[l].astype(BF16))]
        first = [((l, "wg1"), ffn_rows(ffn1_w_gate[l].T)), ((l, "wu1"), ffn_rows(ffn1_w_up[l].T)),
                 ((l, "wd1"), ffn_rows(ffn1_w_down[l])), ((l, "w_in"), w_in[l].T.astype(BF16))]
        pieces += (first + small if l == 0 else small + first) + [
            ((l, "wg2"), ffn_rows(ffn2_w_gate[l].T)), ((l, "wu2"), ffn_rows(ffn2_w_up[l].T)),
            ((l, "wd2"), ffn_rows(ffn2_w_down[l]))]
        lb_re, lb_im, bb_re, bb_im = _ssm_params(ssm_a_re[l], ssm_a_im[l], ssm_log_dt[l], ssm_b_re[l], ssm_b_im[l])
        ngrp = lb_re.shape[0]
        nt = ngrp // GROUPS_PER_TILE
        ssm_tabs = (lb_re.reshape(nt, 1, TILE_STATES), lb_im.reshape(nt, 1, TILE_STATES),
                    *_ssm_tables(bb_re, bb_im, ssm_c_re[l], ssm_c_im[l]))
        layers.append(dict(
            ffn1_norm=ffn1_norm[l], mix_norm=mix_norm[l], ffn2_norm=ffn2_norm[l], attn_sinks=attn_sinks[l],
            ssm_d=ssm_d[l], ssm_tabs=ssm_tabs, f4=f4))
    ws = _WeightStream(pieces)
    ws.get((0, "wu1"))
    meta_all = ws.get((0, "meta"))
    meta_full = jnp.concatenate([meta_all[j] for j in range(N_CHIPS)], axis=1)

    h = jnp.concatenate([jnp.zeros((PAD_FRONT, d), F32), meta_full, x[0]], axis=0)
    saved = []
    n0 = _rms_fwd(h, ffn1_norm[0], "rms_fwd_first")
    for l in range(depth):
        next_gain = ffn1_norm[l + 1] if l + 1 < depth else None
        h, n0, s = _layer_fwd(h, n0, l, layers[l], next_gain, ws, tabs)
        saved.append(s)
    dh, g_final, loss_acc, dhb = _loss_head(h, final_norm, loss_target[0])
    dh_pair = (dh, dhb)
    loss = lax.psum(loss_acc[0, 0], ("x", "y", "c"))

    grads, reds = [None] * depth, [None] * depth
    for l in reversed(range(depth)):
        dh_pair, grads[l], reds[l] = _layer_bwd(dh_pair, l, layers[l], ws, saved[l], tabs, pos)
    dh = dh_pair[0]
    grad_x = dh[BLOCK:][None]
    dmeta_local = dh[PAD_FRONT:BLOCK]

    small = {k: [] for k in SMALL}
    for l in range(depth):
        gl = grads[l]
        _, vjp = jax.vjp(_ssm_params, ssm_a_re[l], ssm_a_im[l], ssm_log_dt[l], ssm_b_re[l], ssm_b_im[l])
        da_re, da_im, dlog_dt, db_re, db_im = vjp(gl["ssm_lam"])
        for k, val in (("ffn1_norm", gl["ffn1_norm"][0]), ("mix_norm", gl["mix_norm"][0]),
                       ("attn_sinks", gl["attn_sinks"]), ("ssm_a_re", da_re), ("ssm_a_im", da_im),
                       ("ssm_log_dt", dlog_dt), ("ssm_b_re", db_re), ("ssm_b_im", db_im),
                       ("ssm_c_re", gl["ssm_c_re"]), ("ssm_c_im", gl["ssm_c_im"]), ("ssm_d", gl["ssm_d"]),
                       ("ffn2_norm", gl["ffn2_norm"][0])):
            small[k].append(val)
    small_local = {k: jnp.stack(vals) for k, vals in small.items() if k != "final_norm"}
    small_local["final_norm"] = g_final[0]
    small_local["meta_tokens"] = dmeta_local
    like = dict(small_local)
    g_small = _sum_devices(_all_gather_devices(_pack_small(small_local), "gather_small_grads"), "sum_small_grads")
    g_small_tree = _unpack_small(g_small, like)
    d4 = d // N_CHIPS
    chip = 2 * xi + yi
    g_meta = lax.dynamic_slice_in_dim(g_small_tree["meta_tokens"], chip * d4, d4, axis=1)

    mine = [[half for red in reds[l] for half in red.end()] for l in range(depth)]
    flat = _share_halves([half for layer_halves in mine for half in layer_halves], "rs_share")
    per_layer = len(mine[0])
    reduced = [(mine[l], flat[l * per_layer:(l + 1) * per_layer]) for l in range(depth)]

    g_out, delta, new_m, new_v = {}, {}, {}, {}
    for i, k in enumerate(BIG):
        flip = (lambda t: jnp.swapaxes(t, 1, 2)) if k in TRANSPOSED else (lambda t: t)
        outs = _adamw_layers(
            flip(w[k]), flip(m[k]), flip(v[k]), [reduced[l][0][i] for l in range(depth)],
            [reduced[l][1][i] for l in range(depth)], pos, "adamw_" + k)
        g_out[k], delta[k], new_m[k], new_v[k] = [flip(t) for t in outs]
    g_small_tree["meta_tokens"] = g_meta
    for k in SMALL + ["meta_tokens"]:
        narrow = w[k].ndim > 2 and w[k].shape[-1] < w[k].shape[-2]
        view = (lambda t: jnp.swapaxes(t, -1, -2)) if narrow else (lambda t: t)
        shape = view(w[k]).shape if w[k].ndim > 1 else (1,) + w[k].shape
        outs = _adamw_whole(view(w[k]).reshape(shape), view(g_small_tree[k]).reshape(shape),
                            view(m[k]).reshape(shape), view(v[k]).reshape(shape), "adamw_" + k)
        g_out[k] = g_small_tree[k]
        delta[k], new_m[k], new_v[k] = [view(t).reshape(w[k].shape) for t in outs]

    return (loss, grad_x, *[g_out[k] for k in WEIGHTS], *[delta[k] for k in WEIGHTS],
            *[new_m[k] for k in WEIGHTS], *[new_v[k] for k in WEIGHTS])
```

```python
import functools
import math

import jax
import jax.numpy as jnp
from jax import lax
from jax.experimental import pallas as pl
from jax.experimental.pallas import tpu as pltpu

F32 = jnp.float32
BF16 = jnp.bfloat16

N_META = 16
HEAD_DIM = 64
N_Q_HEADS = 8
N_KV_HEADS = 2
Q_PER_KV = N_Q_HEADS // N_KV_HEADS
ATTN_WIDTH = N_Q_HEADS * HEAD_DIM
KV_WIDTH = N_KV_HEADS * HEAD_DIM
BLOCK = 128
PAD_FRONT = BLOCK - N_META
ROPE_THETA = 500000.0
ROT_DIM = HEAD_DIM // 4
SSM_GROUP = 16
SSM_STATE = 64
GROUPS_PER_TILE = 4
TILE_STATES = GROUPS_PER_TILE * SSM_STATE
LANES = 128
SUBLANES = 8
MXU_DIM = 256
EPS = 1e-6
NEG_INF = -1e30
N_CHIPS = 4

ADAM_LR = 0.001
ADAM_B1 = 0.9
ADAM_B2 = 0.999
ADAM_EPS = 1e-08
ADAM_WD = 0.01
ADAM_STEP = 10

VMEM_LIMIT = 56 * 1024 * 1024
MESH = pl.DeviceIdType.MESH


def _cparams(sem=None):
    return pltpu.CompilerParams(dimension_semantics=sem, vmem_limit_bytes=VMEM_LIMIT)


def _row_tile(rows, limit=512):
    best = None
    for t in range(128, limit + 1, 128):
        if rows % t == 0:
            best = t
    assert best is not None, rows
    return best


def _div_tile(rows, row_bytes, max_bytes=1 << 20, mult=8):
    best = None
    for t in range(mult, rows + 1, mult):
        if rows % t == 0 and t * row_bytes <= max_bytes:
            best = t
    if best is None:
        best = rows
    return best


def _dot(a, b, mode):
    if mode == "nn":
        dims = (((1,), (0,)), ((), ()))
    elif mode == "nt":
        dims = (((1,), (1,)), ((), ()))
    else:
        dims = (((0,), (0,)), ((), ()))
    return lax.dot_general(a.astype(BF16), b.astype(BF16), dims, preferred_element_type=F32)


def _sigmoid(x):
    return 1.0 / (1.0 + jnp.exp(-x))


_GELU_C = math.sqrt(2.0 / math.pi)


def _gelu(x):
    return 0.5 * x * (1.0 + jnp.tanh(_GELU_C * (x + 0.044715 * x * x * x)))


def _gelu_grad(x):
    t = jnp.tanh(_GELU_C * (x + 0.044715 * x * x * x))
    return 0.5 * (1.0 + t) + 0.5 * x * (1.0 - t * t) * _GELU_C * (1.0 + 3.0 * 0.044715 * x * x)


class _Comm:
    def __init__(self, tag, ins, out_shapes, sems, start, mid, finish):
        self.tag, self.ins, self.out_shapes, self.sems = tag, list(ins), list(out_shapes), list(sems)
        self.start, self.mid, self.finish = start, mid, finish


HBM_SPEC = pl.BlockSpec(memory_space=pltpu.HBM)
MID_NUM, MID_DEN = 4, 5


def _hosted_call(body, comm, *, out_shape, grid, in_specs, out_specs, scratch_shapes, sem, name, args):
    out_shape, in_specs, out_specs = list(out_shape), list(in_specs), list(out_specs)
    scratch_shapes = list(scratch_shapes)
    if comm is None:
        res = pl.pallas_call(
            body, out_shape=out_shape, grid=grid, in_specs=in_specs, out_specs=out_specs,
            scratch_shapes=scratch_shapes, compiler_params=_cparams(sem), name=name)(*args)
        return list(res), []
    n_in, n_out, n_sc = len(args), len(out_shape), len(scratch_shapes)
    nci, nco = len(comm.ins), len(comm.out_shapes)
    total = math.prod(grid)

    def wrapped(*refs):
        in_refs, cin = refs[:n_in], refs[n_in:n_in + nci]
        o0 = n_in + nci
        out_refs, cout = refs[o0:o0 + n_out], refs[o0 + n_out:o0 + n_out + nco]
        s0 = o0 + n_out + nco
        sc, csem = refs[s0:s0 + n_sc], refs[s0 + n_sc:]
        lin = 0
        for dim, size in enumerate(grid):
            lin = lin * size + pl.program_id(dim)

        @pl.when(lin == 0)
        def _():
            comm.start(cin, cout, csem)

        if comm.mid is not None:
            @pl.when(lin == (total * MID_NUM) // MID_DEN)
            def _():
                comm.mid(cin, cout, csem)

        body(*in_refs, *out_refs, *sc)

        @pl.when(lin == total - 1)
        def _():
            comm.finish(cin, cout, csem)

    res = pl.pallas_call(
        wrapped, out_shape=out_shape + comm.out_shapes, grid=grid,
        in_specs=in_specs + [HBM_SPEC] * nci, out_specs=out_specs + [HBM_SPEC] * nco,
        scratch_shapes=scratch_shapes + comm.sems,
        compiler_params=_cparams(("arbitrary",) * len(grid)), name=name + "_" + comm.tag)(*args, *comm.ins)
    return list(res[:n_out]), list(res[n_out:])


def _matmul(name, grid, k_axis, ins, in_specs, pairs, acc_shapes, epilogue, out_shapes, out_specs, sem, comm=None):
    n_in, n_out, n_acc = len(ins), len(out_shapes), len(acc_shapes)

    def body(*refs):
        in_refs = refs[:n_in]
        out_refs = refs[n_in:n_in + n_out]
        acc_refs = refs[n_in + n_out:]
        if k_axis is None:
            accs = [None] * n_acc
            for ia, ib, mode, iacc in pairs:
                d = _dot(in_refs[ia][...], in_refs[ib][...], mode)
                accs[iacc] = d if accs[iacc] is None else accs[iacc] + d
            epilogue(accs, in_refs, out_refs)
            return
        k = pl.program_id(k_axis)

        @pl.when(k == 0)
        def _():
            for r in acc_refs:
                r[...] = jnp.zeros_like(r)

        for ia, ib, mode, iacc in pairs:
            acc_refs[iacc][...] += _dot(in_refs[ia][...], in_refs[ib][...], mode)

        @pl.when(k == pl.num_programs(k_axis) - 1)
        def _():
            epilogue([r[...] for r in acc_refs], in_refs, out_refs)

    scratch = [] if k_axis is None else [pltpu.VMEM(s, F32) for s in acc_shapes]
    outs, couts = _hosted_call(
        body, comm, out_shape=out_shapes, grid=grid, in_specs=in_specs, out_specs=out_specs,
        scratch_shapes=scratch, sem=sem, name=name, args=ins)
    return outs if comm is None else (outs, couts)


def _rms_math(x, g):
    r = lax.rsqrt(jnp.mean(x * x, axis=-1, keepdims=True) + EPS)
    return (x * r * g).astype(BF16)


def _rms_fwd(h, g, name):
    lp, d = h.shape
    tm = _row_tile(lp)

    def body(h_ref, g_ref, n_ref):
        n_ref[...] = _rms_math(h_ref[...], g_ref[...])

    return pl.pallas_call(
        body, out_shape=jax.ShapeDtypeStruct((lp, d), BF16), grid=(lp // tm,),
        in_specs=[pl.BlockSpec((tm, d), lambda i: (i, 0)), pl.BlockSpec((1, d), lambda i: (0, 0))],
        out_specs=pl.BlockSpec((tm, d), lambda i: (i, 0)),
        compiler_params=_cparams(("parallel",)), name=name)(h, g.reshape(1, d))


def _rms_bwd_math(dn, x, g):
    r = lax.rsqrt(jnp.mean(x * x, axis=-1, keepdims=True) + EPS)
    xh = x * r
    dxh = dn * g
    dx = r * (dxh - xh * jnp.mean(dxh * xh, axis=-1, keepdims=True))
    return dx, dn * xh


def _ffn_up(n, wgt, wut, comm=None):
    lp, d = n.shape
    fp = wgt.shape[1]
    tm = _row_tile(lp)

    def up_body(n_ref, wg_ref, wu_ref, a_ref, b_ref, s_ref):
        x = n_ref[...]
        for jc in range(N_CHIPS):
            cols = slice(jc * fp, (jc + 1) * fp)
            a = _dot(x, wg_ref[jc], "nt")
            b = _dot(x, wu_ref[jc], "nt")
            a_ref[:, cols] = a.astype(BF16)
            b_ref[:, cols] = b.astype(BF16)
            s_ref[:, cols] = (a * _sigmoid(a) * b).astype(BF16)

    ff = N_CHIPS * fp
    act = jax.ShapeDtypeStruct((lp, ff), BF16)
    act_tile = pl.BlockSpec((tm, ff), lambda i: (i, 0))
    w_spec = pl.BlockSpec((N_CHIPS, fp, d), lambda i: (0, 0, 0))
    outs, couts = _hosted_call(
        up_body, comm, out_shape=[act, act, act], grid=(lp // tm,),
        in_specs=[pl.BlockSpec((tm, d), lambda i: (i, 0)), w_spec, w_spec],
        out_specs=[act_tile] * 3, scratch_shapes=[], sem=("parallel",), name="ffn_up", args=(n, wgt, wut))
    return (*outs, n), couts


def _residual_outputs(h_new, in_refs, out_refs, gain_at):
    out_refs[0][...] = h_new
    if gain_at is not None:
        out_refs[1][...] = _rms_math(h_new, in_refs[gain_at][...])


def _residual_specs(lp, d, tm, next_gain):
    row = pl.BlockSpec((tm, d), lambda i: (i, 0))
    shapes, specs = [jax.ShapeDtypeStruct((lp, d), F32)], [row]
    extra_in, extra_specs = [], []
    if next_gain is not None:
        shapes.append(jax.ShapeDtypeStruct((lp, d), BF16))
        specs.append(row)
        extra_in, extra_specs = [next_gain.reshape(1, d)], [pl.BlockSpec((1, d), lambda i: (0, 0))]
    return shapes, specs, extra_in, extra_specs


def _ffn_down(s, wd, h, next_gain, comm=None):
    lp, d = h.shape
    ff = s.shape[1]
    tm = _row_tile(lp)
    shapes, specs, extra_in, extra_specs = _residual_specs(lp, d, tm, next_gain)

    def down_epi(accs, in_refs, out_refs):
        _residual_outputs(in_refs[2][...] + 0.5 * accs[0], in_refs, out_refs, 3 if extra_in else None)

    res = _matmul(
        "ffn_down", (lp // tm,), None, [s, wd.reshape(ff, d), h] + extra_in,
        [pl.BlockSpec((tm, ff), lambda i: (i, 0)), pl.BlockSpec((ff, d), lambda i: (0, 0)),
         pl.BlockSpec((tm, d), lambda i: (i, 0))] + extra_specs,
        [(0, 1, "nn", 0)], [(tm, d)], down_epi, shapes, specs, ("parallel",), comm)
    return (res, []) if comm is None else res


def _tn_tiles(lp):
    return _row_tile(lp, 1408)


def _ffn_bwd(dh_pair, h_in, gain, wgt, wut, wd, f4, saved, pos, comm=None, comm2=None):
    dh, dhb = dh_pair
    a, b, s, n = saved
    lp, d = h_in.shape
    fp = wgt.shape[1]
    ff = N_CHIPS * fp
    tm = _row_tile(lp)
    ni = lp // tm
    tk = _tn_tiles(lp)
    nk = lp // tk

    def ds_body(dh_ref, wd_ref, a_ref, b_ref, da_ref, db_ref):
        x = dh_ref[...]
        for jc in range(N_CHIPS):
            cols = slice(jc * fp, (jc + 1) * fp)
            ds = 0.5 * _dot(x, wd_ref[jc], "nt")
            av = a_ref[:, cols].astype(F32)
            bv = b_ref[:, cols].astype(F32)
            sg = _sigmoid(av)
            da_ref[:, cols] = (ds * bv * sg * (1.0 + av * (1.0 - sg))).astype(BF16)
            db_ref[:, cols] = (ds * av * sg).astype(BF16)

    act = jax.ShapeDtypeStruct((lp, ff), BF16)
    act_tile = pl.BlockSpec((tm, ff), lambda i: (i, 0))
    (da, db), couts = _hosted_call(
        ds_body, comm, out_shape=[act, act], grid=(ni,),
        in_specs=[pl.BlockSpec((tm, d), lambda i: (i, 0)), pl.BlockSpec((N_CHIPS, fp, d), lambda i: (0, 0, 0)),
                  act_tile, act_tile],
        out_specs=[act_tile, act_tile], scratch_shapes=[], sem=("parallel",), name="ffn_bwd_ds",
        args=(dhb, wd, a, b))

    dw_shape = jax.ShapeDtypeStruct((N_CHIPS, f4, d), F32)
    dw_spec = pl.BlockSpec((None, f4, d), lambda j, k: (j, 0, 0))
    in_col = pl.BlockSpec((tk, fp), lambda j, k: (k, j))
    in_row = pl.BlockSpec((tk, d), lambda j, k: (k, 0))

    half_shape = jax.ShapeDtypeStruct((N_CHIPS, f4 // 2, d), BF16)
    half_spec = pl.BlockSpec((None, f4 // 2, d), lambda j, k: (j, 0, 0))

    def dwd_epi(accs, in_refs, out_refs):
        dw = 0.5 * accs[0]
        out_refs[0][...] = dw[:f4]
        out_refs[1][...] = _sibling_half(dw, f4)

    res = _matmul(
        "ffn_dwd", (N_CHIPS, nk), 1, [s, dhb], [in_col, in_row],
        [(0, 1, "tn", 0)], [(fp, d)], dwd_epi, [dw_shape, half_shape], [dw_spec, half_spec],
        ("arbitrary", "arbitrary"), comm2)
    (dwd, dwd_other), couts2 = (res, []) if comm2 is None else res

    def dwgu_epi(accs, in_refs, out_refs):
        for i, acc in enumerate(accs):
            out_refs[i][...] = acc[:f4]
            out_refs[2 + i][...] = _sibling_half(acc, f4)

    red_down = _Reduction([dwd], [dwd_other], pos, "ffn_d")
    (dwg, dwu, dwg_other, dwu_other), red_down.got = _matmul(
        "ffn_dwgu", (N_CHIPS, nk), 1, [n, da, db], [in_row, in_col, in_col],
        [(1, 0, "tn", 0), (2, 0, "tn", 1)], [(fp, d)] * 2, dwgu_epi,
        [dw_shape, dw_shape, half_shape, half_shape], [dw_spec, dw_spec, half_spec, half_spec],
        ("arbitrary", "arbitrary"), red_down.comm())

    def dn_epi(accs, in_refs, out_refs):
        i = pl.program_id(0)
        dx, dgrow = _rms_bwd_math(accs[0], in_refs[5][...], in_refs[6][...])
        dh_new = in_refs[4][...] + dx
        out_refs[0][...] = dh_new
        out_refs[2][...] = dh_new.astype(BF16)

        @pl.when(i == 0)
        def _():
            out_refs[1][...] = jnp.zeros_like(out_refs[1])

        out_refs[1][...] += jnp.sum(dgrow, axis=0, keepdims=True)

    red = _Reduction([dwg, dwu], [dwg_other, dwu_other], pos, "ffn_gu")
    row_spec = pl.BlockSpec((tm, d), lambda i: (i, 0))
    act_spec = pl.BlockSpec((tm, ff), lambda i: (i, 0))
    w_spec = pl.BlockSpec((ff, d), lambda i: (0, 0))
    one_spec = pl.BlockSpec((1, d), lambda i: (0, 0))
    (dh_in, dgain, dh_in_b), red.got = _matmul(
        "ffn_bwd_dn", (ni,), None, [da, wgt.reshape(ff, d), db, wut.reshape(ff, d), dh, h_in, gain.reshape(1, d)],
        [act_spec, w_spec, act_spec, w_spec, row_spec, row_spec, one_spec],
        [(0, 1, "nn", 0), (2, 3, "nn", 0)], [(tm, d)], dn_epi,
        [jax.ShapeDtypeStruct((lp, d), F32), jax.ShapeDtypeStruct((1, d), F32), jax.ShapeDtypeStruct((lp, d), BF16)],
        [row_spec, one_spec, row_spec], ("arbitrary",), red.comm())
    return (dh_in, dh_in_b), dgain, [red, red_down], couts, couts2


def _rope_tables(lp):
    pos = jnp.arange(lp, dtype=F32) - float(PAD_FRONT)
    inv_freq = ROPE_THETA ** (-jnp.arange(0, ROT_DIM, 2, dtype=F32) / ROT_DIM)
    ang = pos[:, None] * inv_freq[None, :]
    cos, sin = jnp.cos(ang), jnp.sin(ang)
    half = ROT_DIM // 2
    ones = jnp.ones((lp, HEAD_DIM - ROT_DIM), F32)
    zeros_h = jnp.zeros((lp, half), F32)
    zeros_r = jnp.zeros((lp, HEAD_DIM - ROT_DIM), F32)
    c = jnp.concatenate([cos, cos, ones], axis=1)
    s1 = jnp.concatenate([-sin, zeros_h, zeros_r], axis=1)
    s2 = jnp.concatenate([zeros_h, sin, zeros_r], axis=1)
    reps = LANES // HEAD_DIM
    return jnp.stack([jnp.tile(c, (1, reps)), jnp.tile(s1, (1, reps)), jnp.tile(s2, (1, reps))])


def _rope(x, c, s1, s2):
    half = ROT_DIM // 2
    outs = []
    for ch in range(x.shape[1] // LANES):
        xc = x[:, ch * LANES:(ch + 1) * LANES]
        outs.append(xc * c + pltpu.roll(xc, LANES - half, 1) * s1 + pltpu.roll(xc, half, 1) * s2)
    return outs[0] if len(outs) == 1 else jnp.concatenate(outs, axis=1)


def _rope_t(dy, c, s1, s2):
    half = ROT_DIM // 2
    outs = []
    for ch in range(dy.shape[1] // LANES):
        dc = dy[:, ch * LANES:(ch + 1) * LANES]
        outs.append(dc * c + pltpu.roll(dc * s1, half, 1) + pltpu.roll(dc * s2, LANES - half, 1))
    return outs[0] if len(outs) == 1 else jnp.concatenate(outs, axis=1)


def _in_proj(n, w_in, tabs, ssm_w, comm=None):
    lp, d = n.shape
    inw = w_in.shape[0]
    tm = _row_tile(lp)
    o1 = ATTN_WIDTH
    o2 = o1 + KV_WIDTH
    o3 = o2 + KV_WIDTH
    o4 = o3 + ssm_w
    o5 = o4 + d

    def epi(accs, in_refs, out_refs):
        z = accs[0]
        c, s1, s2 = in_refs[2][0], in_refs[2][1], in_refs[2][2]
        out_refs[0][...] = _rope(z[:, :o1], c, s1, s2).astype(BF16)
        out_refs[1][...] = _rope(z[:, o1:o2], c, s1, s2).astype(BF16)
        out_refs[2][...] = z[:, o2:o3].astype(BF16)
        out_refs[3][...] = z[:, o3:o4]
        out_refs[4][...] = z[:, o4:o5]
        out_refs[5][...] = z[:, o5:]

    def rs(w, dt):
        return jax.ShapeDtypeStruct((lp, w), dt), pl.BlockSpec((tm, w), lambda i: (i, 0))

    shapes, specs = zip(rs(o1, BF16), rs(KV_WIDTH, BF16), rs(KV_WIDTH, BF16), rs(ssm_w, F32), rs(d, F32), rs(d, F32))
    res = _matmul(
        "mix_in_proj", (lp // tm,), None, [n, w_in, tabs],
        [pl.BlockSpec((tm, d), lambda i: (i, 0)), pl.BlockSpec((inw, d), lambda i: (0, 0)),
         pl.BlockSpec((3, tm, LANES), lambda i: (0, i, 0))],
        [(0, 1, "nt", 0)], [(tm, inw)], epi, list(shapes), list(specs), ("parallel",), comm)
    return (res, []) if comm is None else res


def _attn_mask(b):
    rows = lax.broadcasted_iota(jnp.int32, (BLOCK, 3 * BLOCK), 0)
    cols = lax.broadcasted_iota(jnp.int32, (BLOCK, 3 * BLOCK), 1)
    qpos = b * BLOCK + rows - PAD_FRONT
    kpos = (b - 1) * BLOCK + cols - PAD_FRONT
    dist = qpos - kpos
    band = (cols < 2 * BLOCK) & (kpos >= N_META) & (dist >= 0) & (dist < BLOCK)
    mrow = cols - 2 * BLOCK
    meta = (mrow >= PAD_FRONT) & ((mrow - PAD_FRONT) <= qpos)
    return band | meta


def _attn_probs(qh, kk, mask, sink):
    s = _dot(qh, kk, "nt") * (HEAD_DIM ** -0.5)
    s = jnp.where(mask, s, NEG_INF)
    m = jnp.maximum(jnp.max(s, axis=-1, keepdims=True), sink)
    e = jnp.exp(s - m)
    es = jnp.exp(sink - m)
    z = jnp.sum(e, axis=-1, keepdims=True) + es
    inv = 1.0 / z
    return e * inv, es * inv


def _head(ref_or_val, h):
    return ref_or_val[:, h * HEAD_DIM:(h + 1) * HEAD_DIM]


def _attn_fwd(q, k, v, sinks, comm=None):
    lp = q.shape[0]
    nb = lp // BLOCK

    def body(sink_ref, q_ref, kp_ref, kc_ref, km_ref, vp_ref, vc_ref, vm_ref, o_ref):
        b = pl.program_id(0)
        mask = _attn_mask(b)
        for hk in range(N_KV_HEADS):
            kk = jnp.concatenate([_head(kp_ref, hk), _head(kc_ref, hk), _head(km_ref, hk)], axis=0)
            vv = jnp.concatenate([_head(vp_ref, hk), _head(vc_ref, hk), _head(vm_ref, hk)], axis=0)
            for g in range(Q_PER_KV):
                h = hk * Q_PER_KV + g
                p, _ = _attn_probs(_head(q_ref, h), kk, mask, sink_ref[h])
                o_ref[:, h * HEAD_DIM:(h + 1) * HEAD_DIM] = _dot(p, vv, "nn").astype(BF16)

    cur = lambda b: (b, 0)
    prev = lambda b: (jnp.maximum(b - 1, 0), 0)
    first = lambda b: (0, 0)
    kvs = lambda f: pl.BlockSpec((BLOCK, KV_WIDTH), f)
    (o,), couts = _hosted_call(
        body, comm, out_shape=[jax.ShapeDtypeStruct((lp, ATTN_WIDTH), BF16)], grid=(nb,),
        in_specs=[pl.BlockSpec(memory_space=pltpu.SMEM), pl.BlockSpec((BLOCK, ATTN_WIDTH), cur),
                  kvs(prev), kvs(cur), kvs(first), kvs(prev), kvs(cur), kvs(first)],
        out_specs=[pl.BlockSpec((BLOCK, ATTN_WIDTH), cur)], scratch_shapes=[],
        sem=("parallel",), name="attn_fwd", args=(sinks, q, k, k, k, v, v, v))
    return o, couts


def _attn_bwd(q, k, v, do, sinks, tabs, comm=None):
    lp = q.shape[0]
    nb = lp // BLOCK
    scale = HEAD_DIM ** -0.5

    def body(sink_ref, q_ref, do_ref, kp_ref, kc_ref, km_ref, vp_ref, vc_ref, vm_ref, tq_ref, tk_ref, t0_ref,
             dq_ref, dk_ref, dv_ref, dkm_ref, dvm_ref, dsink_ref,
             dq_s, dkk_s, dvv_s, ck_s, cv_s, mk_s, mv_s):
        b = pl.program_id(0)

        @pl.when(b == 0)
        def _():
            for r in (ck_s, cv_s, mk_s, mv_s, dsink_ref):
                r[...] = jnp.zeros_like(r)

        @pl.when(b < nb)
        def _():
            mask = _attn_mask(b)
            for hk in range(N_KV_HEADS):
                kk = jnp.concatenate([_head(kp_ref, hk), _head(kc_ref, hk), _head(km_ref, hk)], axis=0)
                vv = jnp.concatenate([_head(vp_ref, hk), _head(vc_ref, hk), _head(vm_ref, hk)], axis=0)
                dkk = jnp.zeros((3 * BLOCK, HEAD_DIM), F32)
                dvv = jnp.zeros((3 * BLOCK, HEAD_DIM), F32)
                for g in range(Q_PER_KV):
                    h = hk * Q_PER_KV + g
                    qh = _head(q_ref, h)
                    doh = _head(do_ref, h)
                    p, ps = _attn_probs(qh, kk, mask, sink_ref[h])
                    dp = _dot(doh, vv, "nt")
                    delta = jnp.sum(p * dp, axis=-1, keepdims=True)
                    ds = (p * (dp - delta)).astype(BF16)
                    dsink_ref[h:h + 1, :] += jnp.zeros((1, LANES), F32) - jnp.sum(ps * delta)
                    dq_s[:, h * HEAD_DIM:(h + 1) * HEAD_DIM] = _dot(ds, kk, "nn") * scale
                    dkk = dkk + _dot(ds, qh, "tn") * scale
                    dvv = dvv + _dot(p, doh, "tn")
                dkk_s[:, hk * HEAD_DIM:(hk + 1) * HEAD_DIM] = dkk
                dvv_s[:, hk * HEAD_DIM:(hk + 1) * HEAD_DIM] = dvv
            dq_ref[...] = _rope_t(dq_s[...], tq_ref[0], tq_ref[1], tq_ref[2])
            dk_ref[...] = _rope_t(ck_s[...] + dkk_s[0:BLOCK, :], tk_ref[0], tk_ref[1], tk_ref[2])
            dv_ref[...] = cv_s[...] + dvv_s[0:BLOCK, :]
            ck_s[...] = dkk_s[BLOCK:2 * BLOCK, :]
            cv_s[...] = dvv_s[BLOCK:2 * BLOCK, :]
            mk_s[...] += dkk_s[2 * BLOCK:, :]
            mv_s[...] += dvv_s[2 * BLOCK:, :]

        @pl.when(b == nb)
        def _():
            dk_ref[...] = _rope_t(ck_s[...], tk_ref[0], tk_ref[1], tk_ref[2])
            dv_ref[...] = cv_s[...]
            dkm_ref[...] = _rope_t(mk_s[...], t0_ref[0], t0_ref[1], t0_ref[2])
            dvm_ref[...] = mv_s[...]

    cur = lambda b: (jnp.minimum(b, nb - 1), 0)
    prev = lambda b: (jnp.clip(b - 1, 0, nb - 1), 0)
    first = lambda b: (0, 0)
    kvs = lambda f: pl.BlockSpec((BLOCK, KV_WIDTH), f)
    tab = lambda f: pl.BlockSpec((3, BLOCK, LANES), lambda b: (0,) + f(b)[:1] + (0,))
    kv_out = lambda b: (jnp.maximum(b - 1, 0), 0)
    return _hosted_call(
        body, comm,
        out_shape=[jax.ShapeDtypeStruct((lp, ATTN_WIDTH), F32), jax.ShapeDtypeStruct((lp, KV_WIDTH), F32),
                   jax.ShapeDtypeStruct((lp, KV_WIDTH), F32), jax.ShapeDtypeStruct((BLOCK, KV_WIDTH), F32),
                   jax.ShapeDtypeStruct((BLOCK, KV_WIDTH), F32), jax.ShapeDtypeStruct((N_Q_HEADS, LANES), F32)],
        grid=(nb + 1,),
        in_specs=[pl.BlockSpec(memory_space=pltpu.SMEM), pl.BlockSpec((BLOCK, ATTN_WIDTH), cur),
                  pl.BlockSpec((BLOCK, ATTN_WIDTH), cur),
                  kvs(prev), kvs(cur), kvs(first), kvs(prev), kvs(cur), kvs(first),
                  tab(cur), tab(kv_out), tab(first)],
        out_specs=[pl.BlockSpec((BLOCK, ATTN_WIDTH), cur), kvs(kv_out), kvs(kv_out), kvs(first), kvs(first),
                   pl.BlockSpec((N_Q_HEADS, LANES), first)],
        scratch_shapes=[pltpu.VMEM((BLOCK, ATTN_WIDTH), F32), pltpu.VMEM((3 * BLOCK, KV_WIDTH), F32),
                        pltpu.VMEM((3 * BLOCK, KV_WIDTH), F32), pltpu.VMEM((BLOCK, KV_WIDTH), F32),
                        pltpu.VMEM((BLOCK, KV_WIDTH), F32), pltpu.VMEM((BLOCK, KV_WIDTH), F32),
                        pltpu.VMEM((BLOCK, KV_WIDTH), F32)],
        sem=("arbitrary",), name="attn_bwd", args=(sinks, q, do, k, k, k, v, v, v, tabs, tabs, tabs))


def _cmul(ar, ai, br, bi):
    return ar * br - ai * bi, ar * bi + ai * br


def _cpow(lr, li, n):
    rr = ri = None
    br, bi = lr, li
    while n:
        if n & 1:
            rr, ri = (br, bi) if rr is None else _cmul(rr, ri, br, bi)
        n >>= 1
        if n:
            br, bi = _cmul(br, bi, br, bi)
    return rr, ri


def _shift_rows(x, d, reverse):
    rows = lax.broadcasted_iota(jnp.int32, x.shape, 0)
    if not reverse:
        return jnp.where(rows >= d, pltpu.roll(x, d, 0), 0.0)
    return jnp.where(rows < SUBLANES - d, pltpu.roll(x, SUBLANES - d, 0), 0.0)


def _sublane_powers(mr, mi, reverse):
    rows = lax.broadcasted_iota(jnp.int32, mr.shape, 0)
    e = SUBLANES - 1 - rows if reverse else rows
    pr, pi = jnp.ones_like(mr), jnp.zeros_like(mr)
    br, bi = mr, mi
    for d in (1, 2, 4):
        tr, ti = _cmul(pr, pi, br, bi)
        on = (e & d) != 0
        pr, pi = jnp.where(on, tr, pr), jnp.where(on, ti, pi)
        if d < 4:
            br, bi = _cmul(br, bi, br, bi)
    return pr, pi


def _inclusive_prefix(er, ei, mr, mi, reverse):
    ir, ii, pr, pi = er, ei, mr, mi
    for d in (1, 2, 4):
        tr, ti = _cmul(pr, pi, _shift_rows(ir, d, reverse), _shift_rows(ii, d, reverse))
        ir, ii = ir + tr, ii + ti
        if d < 4:
            pr, pi = _cmul(pr, pi, pr, pi)
    return ir, ii


def _chain_rows(a, t, seg):
    return pl.ds(a * SUBLANES * seg + t, SUBLANES, stride=seg)


def _seg_scan(xr_ref, xi_ref, lam, seg, nchain, reverse, store, init, extra=None):
    nt = len(lam)
    acc0 = () if extra is None else extra[1]

    def step(i, carry):
        hs, acc = carry
        t = seg - 1 - i if reverse else i
        out = []
        for a in range(nchain):
            sl = _chain_rows(a, t, seg)
            for j in range(nt):
                lr, li = lam[j]
                k = 2 * (a * nt + j)
                hr, hi = hs[k], hs[k + 1]
                nr = lr * hr - li * hi + xr_ref[j, sl, :]
                ni = lr * hi + li * hr + xi_ref[j, sl, :]
                if store:
                    xr_ref[j, sl, :] = nr
                    xi_ref[j, sl, :] = ni
                if extra is not None:
                    acc = extra[0](t, a, j, nr, ni, acc)
                out += [nr, ni]
        return tuple(out), acc

    return lax.fori_loop(0, seg, step, (tuple(init), acc0))


def _ssm_scan(xr_ref, xi_ref, lam, seg, nchain, reverse, extra=None):
    nt = len(lam)
    zero = [jnp.zeros((SUBLANES, LANES), F32)] * (2 * nt * nchain)
    ends, _ = _seg_scan(xr_ref, xi_ref, lam, seg, nchain, reverse, False, zero)
    init = [None] * (2 * nt * nchain)
    last = 0 if reverse else SUBLANES - 1
    for j in range(nt):
        mr, mi = _cpow(lam[j][0], lam[j][1], seg)
        m8r, m8i = _cpow(mr, mi, SUBLANES)
        pwr, pwi = _sublane_powers(mr, mi, reverse)
        gr = gi = jnp.zeros((SUBLANES, LANES), F32)
        for a in (reversed(range(nchain)) if reverse else range(nchain)):
            k = 2 * (a * nt + j)
            incr, inci = _inclusive_prefix(ends[k], ends[k + 1], mr, mi, reverse)
            tr, ti = _cmul(pwr, pwi, gr, gi)
            init[k] = _shift_rows(incr, 1, reverse) + tr
            init[k + 1] = _shift_rows(inci, 1, reverse) + ti
            g2r, g2i = _cmul(m8r, m8i, gr, gi)
            gr = g2r + jnp.broadcast_to(incr[last:last + 1, :], gr.shape)
            gi = g2i + jnp.broadcast_to(inci[last:last + 1, :], gi.shape)
    _, acc = _seg_scan(xr_ref, xi_ref, lam, seg, nchain, reverse, True, init, extra)
    return acc


def _diag_mask():
    steps = LANES // SSM_GROUP // GROUPS_PER_TILE
    return (jnp.eye(steps, dtype=F32)[:, None, :, None] * jnp.eye(GROUPS_PER_TILE, dtype=F32)[None, :, None, :])


def _ssm_tables(bb_re, bb_im, c_re, c_im):
    g = bb_re.shape[0]
    nt = g // GROUPS_PER_TILE
    steps = LANES // SSM_GROUP // GROUPS_PER_TILE
    mask = _diag_mask()

    def b_tab(bb):
        x = bb.reshape(nt // steps, steps, GROUPS_PER_TILE, SSM_STATE, SSM_GROUP)
        x = jnp.transpose(x, (0, 1, 4, 2, 3))[:, :, None, None]
        m = jnp.transpose(mask, (0, 2, 3, 1))[None, :, :, :, None, :, None]
        return (x * m).reshape(nt, LANES, TILE_STATES)

    def c_tab(c):
        x = c.reshape(nt // steps, steps, GROUPS_PER_TILE, SSM_GROUP, SSM_STATE)
        x = jnp.transpose(x, (0, 1, 2, 4, 3))[:, :, :, :, None, None]
        m = mask[None, :, :, None, :, :, None]
        return (x * m).reshape(nt, TILE_STATES, LANES)

    return b_tab(bb_re), b_tab(bb_im), c_tab(c_re), c_tab(c_im)


def _ssm_untable_b(db, g):
    nt = g // GROUPS_PER_TILE
    steps = LANES // SSM_GROUP // GROUPS_PER_TILE
    x = db.reshape(nt // steps, steps, GROUPS_PER_TILE, SSM_STATE, steps, GROUPS_PER_TILE, SSM_GROUP)
    m = _diag_mask()[None, :, :, None, :, :, None]
    return jnp.sum(x * m, axis=(4, 5)).reshape(g, SSM_STATE, SSM_GROUP)


def _ssm_untable_c(dc, g):
    nt = g // GROUPS_PER_TILE
    steps = LANES // SSM_GROUP // GROUPS_PER_TILE
    x = dc.reshape(nt // steps, steps, steps, GROUPS_PER_TILE, SSM_GROUP, GROUPS_PER_TILE, SSM_STATE)
    m = jnp.transpose(_diag_mask(), (0, 2, 3, 1))[None, :, :, :, None, :, None]
    out = jnp.sum(x * m, axis=(2, 3))
    return jnp.transpose(out, (0, 1, 3, 2, 4)).reshape(g, SSM_GROUP, SSM_STATE)


def _lam_tiles(lam_ref):
    out = []
    for j in range(TILE_STATES // LANES):
        out.append(jnp.broadcast_to(lam_ref[:, j * LANES:(j + 1) * LANES], (SUBLANES, LANES)))
    return out


def _scan_chains(lp):
    for n in (4, 2, 1):
        if lp % (SUBLANES * n) == 0 and (lp // SUBLANES) % 16 == 0:
            return n
    raise ValueError(lp)


def _split_tiles(dst_ref, rows, val):
    for j in range(val.shape[1] // LANES):
        dst_ref[j, rows, :] = val[:, j * LANES:(j + 1) * LANES]


def _cat_tiles(src_ref, rows):
    njt = src_ref.shape[0]
    return jnp.concatenate([src_ref[j, rows, :] for j in range(njt)], axis=1).astype(BF16)


def _ssm_fwd(u, lam_re, lam_im, tb_re, tb_im, tc_re, tc_im, d_skip, comm=None):
    lp, w = u.shape
    nt = tb_re.shape[0]
    nchain = _scan_chains(lp)
    seg = lp // (SUBLANES * nchain)
    chunk = lp // SUBLANES
    njt = TILE_STATES // LANES

    def body(u_ref, lr_ref, li_ref, br_ref, bi_ref, cr_ref, ci_ref, d_ref, y_ref, yg_ref, xr, xi):
        t = pl.program_id(0)
        for s in range(SUBLANES):
            rs = pl.ds(s * chunk, chunk)
            ub = u_ref[rs, :].astype(BF16)
            _split_tiles(xr, rs, _dot(ub, br_ref[...], "nn"))
            _split_tiles(xi, rs, _dot(ub, bi_ref[...], "nn"))
        lrs, lis = _lam_tiles(lr_ref), _lam_tiles(li_ref)
        _ssm_scan(xr, xi, list(zip(lrs, lis)), seg, nchain, False)
        for s in range(SUBLANES):
            rs = pl.ds(s * chunk, chunk)
            y = _dot(_cat_tiles(xr, rs), cr_ref[...], "nn") - _dot(_cat_tiles(xi, rs), ci_ref[...], "nn")

            @pl.when(t % 2 == 0)
            def _():
                y_ref[rs, :] = y + d_ref[...] * u_ref[rs, :]

            @pl.when(t % 2 == 1)
            def _():
                total = y_ref[rs, :] + y
                y_ref[rs, :] = total
                yg_ref[rs, :] = _gelu(total).astype(BF16)

    blk = pl.BlockSpec((lp, LANES), lambda t: (0, t // 2))
    lam_spec = pl.BlockSpec((None, 1, TILE_STATES), lambda t: (t, 0, 0))
    b_spec = pl.BlockSpec((None, LANES, TILE_STATES), lambda t: (t, 0, 0))
    c_spec = pl.BlockSpec((None, TILE_STATES, LANES), lambda t: (t, 0, 0))
    (y, yg), couts = _hosted_call(
        body, comm, out_shape=[jax.ShapeDtypeStruct((lp, w), F32), jax.ShapeDtypeStruct((lp, w), BF16)], grid=(nt,),
        in_specs=[blk, lam_spec, lam_spec, b_spec, b_spec, c_spec, c_spec,
                  pl.BlockSpec((1, LANES), lambda t: (0, t // 2))],
        out_specs=[blk, blk],
        scratch_shapes=[pltpu.VMEM((njt, lp, LANES), F32), pltpu.VMEM((njt, lp, LANES), F32)],
        sem=("arbitrary",), name="ssm_fwd",
        args=(u, lam_re, lam_im, tb_re, tb_im, tc_re, tc_im, d_skip.reshape(1, w)))
    return (y, yg), couts


def _ssm_bwd(u, dy, lam_re, lam_im, tb_re, tb_im, tc_re, tc_im, d_skip, comm=None):
    lp, w = u.shape
    nt = tb_re.shape[0]
    nchain = _scan_chains(lp)
    seg = lp // (SUBLANES * nchain)
    chunk = lp // SUBLANES
    njt = TILE_STATES // LANES
    tbt_re, tbt_im = jnp.swapaxes(tb_re, 1, 2), jnp.swapaxes(tb_im, 1, 2)
    tct_re, tct_im = jnp.swapaxes(tc_re, 1, 2), jnp.swapaxes(tc_im, 1, 2)

    def body(u_ref, dy_ref, lr_ref, li_ref, br_ref, bi_ref, btr_ref, bti_ref, ctr_ref, cti_ref, d_ref,
             du_ref, dlr_ref, dli_ref, dbr_ref, dbi_ref, dcr_ref, dci_ref, dd_ref, hr, hi, ar, ai):
        t = pl.program_id(0)
        lrs, lis = _lam_tiles(lr_ref), _lam_tiles(li_ref)
        for s in range(SUBLANES):
            rs = pl.ds(s * chunk, chunk)
            ub = u_ref[rs, :].astype(BF16)
            dyb = dy_ref[rs, :].astype(BF16)
            _split_tiles(hr, rs, _dot(ub, br_ref[...], "nn"))
            _split_tiles(hi, rs, _dot(ub, bi_ref[...], "nn"))
            _split_tiles(ar, rs, _dot(dyb, ctr_ref[...], "nn"))
            _split_tiles(ai, rs, -_dot(dyb, cti_ref[...], "nn"))
        _ssm_scan(hr, hi, list(zip(lrs, lis)), seg, nchain, False)

        def dlam_step(tt, a, j, a_r, a_i, acc):
            sl = _chain_rows(a, jnp.maximum(tt - 1, 0), seg)
            p_r, p_i = hr[j, sl, :], hi[j, sl, :]
            acc = list(acc)
            acc[2 * j] = acc[2 * j] + jnp.where(tt > 0, a_r * p_r + a_i * p_i, 0.0)
            acc[2 * j + 1] = acc[2 * j + 1] + jnp.where(tt > 0, a_i * p_r - a_r * p_i, 0.0)
            return tuple(acc)

        zero = tuple([jnp.zeros((SUBLANES, LANES), F32)] * (2 * njt))
        conj = [(lr, -li) for lr, li in zip(lrs, lis)]
        acc = list(_ssm_scan(ar, ai, conj, seg, nchain, True, (dlam_step, zero)))
        row0 = lax.broadcasted_iota(jnp.int32, (SUBLANES, LANES), 0) == 0
        for j in range(njt):
            cs = slice(j * LANES, (j + 1) * LANES)
            for a in range(nchain):
                p_r = _shift_rows(hr[j, _chain_rows(a, seg - 1, seg), :], 1, False)
                p_i = _shift_rows(hi[j, _chain_rows(a, seg - 1, seg), :], 1, False)
                if a > 0:
                    before = pl.ds(a * SUBLANES * seg - 1, 1)
                    p_r = jnp.where(row0, jnp.broadcast_to(hr[j, before, :], p_r.shape), p_r)
                    p_i = jnp.where(row0, jnp.broadcast_to(hi[j, before, :], p_i.shape), p_i)
                a_r, a_i = ar[j, _chain_rows(a, 0, seg), :], ai[j, _chain_rows(a, 0, seg), :]
                acc[2 * j] = acc[2 * j] + a_r * p_r + a_i * p_i
                acc[2 * j + 1] = acc[2 * j + 1] + a_i * p_r - a_r * p_i
            dlr_ref[:, cs] = jnp.sum(acc[2 * j], axis=0, keepdims=True)
            dli_ref[:, cs] = jnp.sum(acc[2 * j + 1], axis=0, keepdims=True)

        dd = jnp.zeros((1, LANES), F32)
        for s in range(SUBLANES):
            rs = pl.ds(s * chunk, chunk)
            ub = u_ref[rs, :].astype(BF16)
            dyv = dy_ref[rs, :]
            dyb = dyv.astype(BF16)
            arb, aib = _cat_tiles(ar, rs), _cat_tiles(ai, rs)
            hrb, hib = _cat_tiles(hr, rs), _cat_tiles(hi, rs)
            du = _dot(arb, btr_ref[...], "nn") + _dot(aib, bti_ref[...], "nn")
            upd = [(dbr_ref, _dot(arb, ub, "tn")), (dbi_ref, _dot(aib, ub, "tn")),
                   (dcr_ref, _dot(dyb, hrb, "tn")), (dci_ref, -_dot(dyb, hib, "tn"))]
            for ref, val in upd:
                if s == 0:
                    ref[...] = val
                else:
                    ref[...] += val
            rows = lax.broadcasted_iota(jnp.int32, (chunk, LANES), 0) + s * chunk
            keep = rows >= PAD_FRONT
            dd = dd + jnp.sum(dyv * u_ref[rs, :], axis=0, keepdims=True)

            @pl.when(t % 2 == 0)
            def _():
                du_ref[rs, :] = jnp.where(keep, du + d_ref[...] * dyv, 0.0)

            @pl.when(t % 2 == 1)
            def _():
                du_ref[rs, :] += jnp.where(keep, du, 0.0)

        @pl.when(t % 2 == 0)
        def _():
            dd_ref[...] = dd

    blk = pl.BlockSpec((lp, LANES), lambda t: (0, t // 2))
    vec = pl.BlockSpec((1, LANES), lambda t: (0, t // 2))
    lam_spec = pl.BlockSpec((None, 1, TILE_STATES), lambda t: (t, 0, 0))
    b_spec = pl.BlockSpec((None, LANES, TILE_STATES), lambda t: (t, 0, 0))
    c_spec = pl.BlockSpec((None, TILE_STATES, LANES), lambda t: (t, 0, 0))
    lam_shape = jax.ShapeDtypeStruct((nt, 1, TILE_STATES), F32)
    bt_shape = jax.ShapeDtypeStruct((nt, TILE_STATES, LANES), F32)
    ct_shape = jax.ShapeDtypeStruct((nt, LANES, TILE_STATES), F32)
    st = pltpu.VMEM((njt, lp, LANES), F32)
    return _hosted_call(
        body, comm,
        out_shape=[jax.ShapeDtypeStruct((lp, w), F32), lam_shape, lam_shape, bt_shape, bt_shape, ct_shape, ct_shape,
                   jax.ShapeDtypeStruct((1, w), F32)],
        grid=(nt,),
        in_specs=[blk, blk, lam_spec, lam_spec, b_spec, b_spec, c_spec, c_spec, b_spec, b_spec, vec],
        out_specs=[blk, lam_spec, lam_spec, c_spec, c_spec, b_spec, b_spec, vec],
        scratch_shapes=[st, st, st, st], sem=("arbitrary",), name="ssm_bwd",
        args=(u, dy, lam_re, lam_im, tb_re, tb_im, tbt_re, tbt_im, tct_re, tct_im, d_skip.reshape(1, w)))


def _ssm_params(a_re, a_im, log_dt, b_re, b_im):
    dt = jnp.exp(log_dt)[:, None]
    mag = jnp.exp(a_re * dt)
    lb_re = mag * jnp.cos(a_im * dt)
    lb_im = mag * jnp.sin(a_im * dt)
    den = a_re * a_re + a_im * a_im
    num_re = lb_re - 1.0
    coef_re = (num_re * a_re + lb_im * a_im) / den
    coef_im = (lb_im * a_re - num_re * a_im) / den
    bb_re = coef_re[..., None] * b_re - coef_im[..., None] * b_im
    bb_im = coef_re[..., None] * b_im + coef_im[..., None] * b_re
    return lb_re, lb_im, bb_re, bb_im


def _merge_fwd(o, yg, ga, gs, w3t, comm=None):
    lp, d = ga.shape
    kw = w3t.shape[2]
    tm = _row_tile(lp)

    def epi(accs, in_refs, out_refs):
        attn, vv, gg = accs
        out_refs[0][...] = (_sigmoid(in_refs[5][...]) * attn
                            + _sigmoid(in_refs[6][...]) * (vv * _sigmoid(gg))).astype(BF16)

    wspec = lambda which: pl.BlockSpec((None, d, kw), lambda i: (which, 0, 0))
    rowspec = pl.BlockSpec((tm, d), lambda i: (i, 0))
    aspec = pl.BlockSpec((tm, kw), lambda i: (i, 0))
    res = _matmul(
        "merge_fwd", (lp // tm,), None, [o, yg, w3t, w3t, w3t, ga, gs],
        [aspec, aspec, wspec(0), wspec(1), wspec(2), rowspec, rowspec],
        [(0, 2, "nt", 0), (1, 3, "nt", 1), (1, 4, "nt", 2)], [(tm, d)] * 3, epi,
        [jax.ShapeDtypeStruct((lp, d), BF16)], [rowspec], ("parallel",), comm)
    return (res[0], []) if comm is None else (res[0][0], res[1])


def _out_proj(merged, w_out, h, next_gain, comm=None):
    lp, d = h.shape
    tm = _row_tile(lp)
    shapes, specs, extra_in, extra_specs = _residual_specs(lp, d, tm, next_gain)

    def epi(accs, in_refs, out_refs):
        _residual_outputs(in_refs[2][...] + accs[0], in_refs, out_refs, 3 if extra_in else None)

    rowspec = pl.BlockSpec((tm, d), lambda i: (i, 0))
    res = _matmul(
        "mix_out_proj", (lp // tm,), None, [merged, w_out, h] + extra_in,
        [rowspec, pl.BlockSpec((d, d), lambda i: (0, 0)), rowspec] + extra_specs,
        [(0, 1, "nn", 0)], [(tm, d)], epi, shapes, specs, ("parallel",), comm)
    return (res, []) if comm is None else res


def _merge_bwd(dhb, w_out, o, yg, ga, gs, w3t):
    lp, d = ga.shape
    kw = w3t.shape[2]
    tm = _row_tile(lp)

    def epi(accs, in_refs, out_refs):
        dm, attn, vv, gg = accs
        sa = _sigmoid(in_refs[7][...])
        ss = _sigmoid(in_refs[8][...])
        sg = _sigmoid(gg)
        ssm = vv * sg
        dssm = dm * ss
        out_refs[0][...] = (dm * sa).astype(BF16)
        out_refs[1][...] = (dssm * sg).astype(BF16)
        out_refs[2][...] = (dssm * vv * sg * (1.0 - sg)).astype(BF16)
        out_refs[3][...] = (dm * attn * sa * (1.0 - sa)).astype(BF16)
        out_refs[4][...] = (dm * ssm * ss * (1.0 - ss)).astype(BF16)

    wspec = lambda which: pl.BlockSpec((None, d, kw), lambda i: (which, 0, 0))
    rowspec = pl.BlockSpec((tm, d), lambda i: (i, 0))
    aspec = pl.BlockSpec((tm, kw), lambda i: (i, 0))
    shp = jax.ShapeDtypeStruct((lp, d), BF16)
    return _matmul(
        "merge_bwd", (lp // tm,), None, [dhb, w_out, o, yg, w3t, w3t, w3t, ga, gs],
        [rowspec, pl.BlockSpec((d, d), lambda i: (0, 0)), aspec, aspec, wspec(0), wspec(1), wspec(2), rowspec,
         rowspec],
        [(0, 1, "nt", 0), (2, 4, "nt", 1), (3, 5, "nt", 2), (3, 6, "nt", 3)], [(tm, d)] * 4, epi,
        [shp] * 5, [rowspec] * 5, ("parallel",))


def _branch_bwd(dattn, dv, dg, w3t, y):
    lp, d = dattn.shape
    kw = w3t.shape[2]
    tm = _row_tile(lp)

    def epi(accs, in_refs, out_refs):
        out_refs[0][...] = accs[0].astype(BF16)
        out_refs[1][...] = accs[1] * _gelu_grad(in_refs[6][...])

    wspec = lambda which: pl.BlockSpec((None, d, kw), lambda i: (which, 0, 0))
    rowspec = pl.BlockSpec((tm, d), lambda i: (i, 0))
    aspec = pl.BlockSpec((tm, kw), lambda i: (i, 0))
    return _matmul(
        "branch_bwd", (lp // tm,), None, [dattn, dv, dg, w3t, w3t, w3t, y],
        [rowspec, rowspec, rowspec, wspec(0), wspec(1), wspec(2), aspec],
        [(0, 3, "nn", 0), (1, 4, "nn", 1), (2, 5, "nn", 1)], [(tm, kw)] * 2, epi,
        [jax.ShapeDtypeStruct((lp, kw), BF16), jax.ShapeDtypeStruct((lp, kw), F32)], [aspec, aspec],
        ("parallel",))


def _sibling_half(acc, rows):
    half = rows // 2
    return jnp.where(lax.axis_index("c") == 0, acc[half:rows], acc[:half]).astype(BF16)


def _tn_cols(x, ys, name):
    lp, kx = x.shape
    n = ys[0].shape[1]
    n4 = n // N_CHIPS
    tk = _tn_tiles(lp)
    ny = len(ys)

    def epi(accs, in_refs, out_refs):
        for i, acc in enumerate(accs):
            out_refs[i][...] = acc
            out_refs[ny + i][...] = _sibling_half(acc, kx)

    shp = jax.ShapeDtypeStruct((N_CHIPS, kx, n4), F32)
    shp_half = jax.ShapeDtypeStruct((N_CHIPS, kx // 2, n4), BF16)
    res = _matmul(
        name, (N_CHIPS, lp // tk), 1, [x] + list(ys),
        [pl.BlockSpec((tk, kx), lambda j, k: (k, 0))] + [pl.BlockSpec((tk, n4), lambda j, k: (k, j))] * ny,
        [(0, 1 + i, "tn", i) for i in range(ny)], [(kx, n4)] * ny, epi,
        [shp] * ny + [shp_half] * ny,
        [pl.BlockSpec((None, kx, n4), lambda j, k: (j, 0, 0))] * ny
        + [pl.BlockSpec((None, kx // 2, n4), lambda j, k: (j, 0, 0))] * ny,
        ("arbitrary", "arbitrary"))
    return res[:ny], res[ny:]


def _tn_full(x, y, name, tn_cols=None):
    lp, kx = x.shape
    n = y.shape[1]
    tk = _tn_tiles(lp)
    tn = n if tn_cols is None else tn_cols
    k4 = kx // N_CHIPS

    def epi(accs, in_refs, out_refs):
        for j in range(N_CHIPS):
            slab = accs[0][j * k4:(j + 1) * k4]
            out_refs[0][j] = slab
            out_refs[1][j] = _sibling_half(slab, k4)

    return _matmul(
        name, (n // tn, lp // tk), 1, [x, y],
        [pl.BlockSpec((tk, kx), lambda j, k: (k, 0)), pl.BlockSpec((tk, tn), lambda j, k: (k, j))],
        [(0, 1, "tn", 0)], [(kx, tn)], epi,
        [jax.ShapeDtypeStruct((N_CHIPS, k4, n), F32), jax.ShapeDtypeStruct((N_CHIPS, k4 // 2, n), BF16)],
        [pl.BlockSpec((N_CHIPS, k4, tn), lambda j, k: (0, 0, j)),
         pl.BlockSpec((N_CHIPS, k4 // 2, tn), lambda j, k: (0, 0, j))],
        ("arbitrary", "arbitrary"))


def _in_proj_bwd(dz, w_in, dh, h_in, gain):
    lp, d = h_in.shape
    inw = w_in.shape[0]
    tm = _row_tile(lp)

    def epi(accs, in_refs, out_refs):
        i = pl.program_id(0)
        dx, dgrow = _rms_bwd_math(accs[0], in_refs[3][...], in_refs[4][...])
        dh_new = in_refs[2][...] + dx
        out_refs[0][...] = dh_new
        out_refs[2][...] = dh_new.astype(BF16)

        @pl.when(i == 0)
        def _():
            out_refs[1][...] = jnp.zeros_like(out_refs[1])

        out_refs[1][...] += jnp.sum(dgrow, axis=0, keepdims=True)

    row = pl.BlockSpec((tm, d), lambda i: (i, 0))
    return _matmul(
        "mix_in_proj_bwd", (lp // tm,), None, [dz, w_in, dh, h_in, gain.reshape(1, d)],
        [pl.BlockSpec((tm, inw), lambda i: (i, 0)), pl.BlockSpec((inw, d), lambda i: (0, 0)), row, row,
         pl.BlockSpec((1, d), lambda i: (0, 0))],
        [(0, 1, "nn", 0)], [(tm, d)], epi,
        [jax.ShapeDtypeStruct((lp, d), F32), jax.ShapeDtypeStruct((1, d), F32), jax.ShapeDtypeStruct((lp, d), BF16)],
        [row, pl.BlockSpec((1, d), lambda i: (0, 0)), row], ("arbitrary",))


def _loss_head(h, gain, target):
    lp, d = h.shape
    nb = lp // BLOCK

    def body(h_ref, g_ref, t_ref, dh_ref, dg_ref, loss_ref, dhb_ref):
        i = pl.program_id(0)

        @pl.when(i == 0)
        def _():
            dg_ref[...] = jnp.zeros_like(dg_ref)
            loss_ref[...] = jnp.zeros_like(loss_ref)
            dh_ref[...] = jnp.zeros_like(dh_ref)
            dhb_ref[...] = jnp.zeros_like(dhb_ref)

        @pl.when(i > 0)
        def _():
            x = h_ref[...]
            g = g_ref[...]
            r = lax.rsqrt(jnp.mean(x * x, axis=-1, keepdims=True) + EPS)
            err = x * r * g - t_ref[...]
            loss_ref[...] += jnp.zeros_like(loss_ref) + 0.5 * jnp.sum(jnp.sum(err * err, axis=-1, keepdims=True)) / d
            dx, dgrow = _rms_bwd_math(err * (1.0 / d), x, g)
            dh_ref[...] = dx
            dhb_ref[...] = dx.astype(BF16)
            dg_ref[...] += jnp.sum(dgrow, axis=0, keepdims=True)

    row = pl.BlockSpec((BLOCK, d), lambda i: (i, 0))
    one = pl.BlockSpec((1, d), lambda i: (0, 0))
    return pl.pallas_call(
        body,
        out_shape=[jax.ShapeDtypeStruct((lp, d), F32), jax.ShapeDtypeStruct((1, d), F32),
                   jax.ShapeDtypeStruct((SUBLANES, LANES), F32), jax.ShapeDtypeStruct((lp, d), BF16)],
        grid=(nb,),
        in_specs=[row, one, pl.BlockSpec((BLOCK, d), lambda i: (jnp.maximum(i - 1, 0), 0))],
        out_specs=[row, one, pl.BlockSpec((SUBLANES, LANES), lambda i: (0, 0)), row],
        compiler_params=_cparams(("arbitrary",)), name="loss_head")(h, gain.reshape(1, d), target)


def _adam_math(w, g, m, v):
    m = ADAM_B1 * m + (1.0 - ADAM_B1) * g
    v = ADAM_B2 * v + (1.0 - ADAM_B2) * (g * g)
    m_hat = m / (1.0 - ADAM_B1 ** ADAM_STEP)
    v_hat = v / (1.0 - ADAM_B2 ** ADAM_STEP)
    delta = -ADAM_LR * (m_hat / (jnp.sqrt(v_hat) + ADAM_EPS) + ADAM_WD * w)
    return delta, m, v


def _adamw_layers(w, m, v, mine, other, pos, name):
    depth, r, c = w.shape
    half = r // 2
    tr = _div_tile(half, c * 4)
    nh = half // tr

    def body(*refs):
        pos_ref, w_ref, m_ref, v_ref = refs[:4]
        mine_refs = refs[4:4 + depth]
        other_refs = refs[4 + depth:4 + 2 * depth]
        g_out, d_out, m_out, v_out = refs[4 + 2 * depth:]
        layer, i = pl.program_id(0), pl.program_id(1)
        is_mine = (i // nh) == pos_ref[0]

        def update(g):
            delta, nm, nv = _adam_math(w_ref[...], g, m_ref[...], v_ref[...])
            g_out[...] = g
            d_out[...] = delta
            m_out[...] = nm
            v_out[...] = nv

        for l in range(depth):
            @pl.when((layer == l) & is_mine)
            def _(l=l):
                update(mine_refs[l][...])

            @pl.when((layer == l) & jnp.logical_not(is_mine))
            def _(l=l):
                update(other_refs[l][...])

    stacked = pl.BlockSpec((None, tr, c), lambda l, i, p: (l, i, 0))

    def gspec(layer, is_other):
        def imap(l, i, p):
            first = jnp.where(is_other, 1 - p[0], p[0]) * nh
            here = jnp.clip(i - first, 0, nh - 1)
            return (jnp.where(l == layer, here, jnp.where(l < layer, 0, nh - 1)), 0)
        return pl.BlockSpec((tr, c), imap)

    shp = jax.ShapeDtypeStruct((depth, r, c), F32)
    grid_spec = pltpu.PrefetchScalarGridSpec(
        num_scalar_prefetch=1, grid=(depth, 2 * nh),
        in_specs=[stacked] * 3 + [gspec(l, 0) for l in range(depth)] + [gspec(l, 1) for l in range(depth)],
        out_specs=[stacked] * 4)
    return pl.pallas_call(
        body, out_shape=[shp] * 4, grid_spec=grid_spec,
        compiler_params=_cparams(("arbitrary", "arbitrary")), name=name)(pos, w, m, v, *mine, *other)


def _adamw_whole(w, g, m, v, name):
    def body(w_ref, g_ref, m_ref, v_ref, d_out, m_out, v_out):
        delta, nm, nv = _adam_math(w_ref[...], g_ref[...], m_ref[...], v_ref[...])
        d_out[...] = delta
        m_out[...] = nm
        v_out[...] = nv

    shp = jax.ShapeDtypeStruct(w.shape, F32)
    return pl.pallas_call(body, out_shape=[shp] * 3, compiler_params=_cparams(), name=name)(w, g, m, v)


def _mesh_pos():
    return lax.axis_index("x"), lax.axis_index("y"), lax.axis_index("c")


def _row_half(ref, which, lead):
    half = ref.shape[lead] // 2
    idx = (slice(None),) * lead + (pl.ds(which * half, half), slice(None))
    return ref.at[idx]


def _gather_comm(arrs, tag):
    n = len(arrs)

    def ctx(ins, outs, sems):
        send_sems, recv_sems, local_sems = sems
        x, y, c = _mesh_pos()
        chips = [(1 - x, y), (x, 1 - y), (1 - x, 1 - y)]

        def slot(k, chip, which):
            lead = len(ins[k].shape) - 2
            return _row_half(outs[k].at[2 * chip[0] + chip[1]], which, lead)

        def copy(k, j, src, dst, to):
            return pltpu.make_async_remote_copy(
                src_ref=src, dst_ref=dst, send_sem=send_sems.at[6 * k + j], recv_sem=recv_sems.at[6 * k + j],
                device_id=to, device_id_type=MESH)

        def local(k):
            return pltpu.make_async_copy(ins[k], outs[k].at[2 * x + y], local_sems.at[k])

        def first(k, j):
            lead = len(ins[k].shape) - 2
            return copy(k, j, _row_half(ins[k], c, lead), slot(k, (x, y), c), (*chips[j], c))

        def passed(k, j, which):
            return copy(k, 3 + j, slot(k, chips[j], which), slot(k, chips[j], which), (x, y, 1 - c))

        def landed(k, j):
            return copy(k, j, slot(k, chips[j], c), slot(k, chips[j], c), (x, y, 1 - c))

        return c, local, first, passed, landed

    def start(ins, outs, sems):
        c, local, first, passed, landed = ctx(ins, outs, sems)
        for k in range(n):
            local(k).start()
            for j in range(3):
                first(k, j).start()

    def mid(ins, outs, sems):
        c, local, first, passed, landed = ctx(ins, outs, sems)
        for j in range(3):
            for k in range(n):
                landed(k, j).wait_recv()
                passed(k, j, c).start()

    def finish(ins, outs, sems):
        c, local, first, passed, landed = ctx(ins, outs, sems)
        for j in range(3):
            for k in range(n):
                passed(k, j, 1 - c).wait_recv()
        for k in range(n):
            for j in range(3):
                first(k, j).wait_send()
                passed(k, j, c).wait_send()
            local(k).wait()

    return _Comm(
        tag, arrs, [jax.ShapeDtypeStruct((N_CHIPS,) + a.shape, a.dtype) for a in arrs],
        [pltpu.SemaphoreType.DMA((6 * n,)), pltpu.SemaphoreType.DMA((6 * n,)), pltpu.SemaphoreType.DMA((n,))],
        start, mid, finish)


def _run_comm(comm, name):
    n_in, n_out = len(comm.ins), len(comm.out_shapes)

    def body(*refs):
        ins, outs, sems = refs[:n_in], refs[n_in:n_in + n_out], refs[n_in + n_out:]
        comm.start(ins, outs, sems)
        if comm.mid is not None:
            comm.mid(ins, outs, sems)
        comm.finish(ins, outs, sems)

    return pl.pallas_call(
        body, out_shape=comm.out_shapes, in_specs=[HBM_SPEC] * n_in, out_specs=[HBM_SPEC] * n_out,
        scratch_shapes=comm.sems, name=name)(*comm.ins)


def _all_gather_chips(arrs, name):
    return _run_comm(_gather_comm(arrs, "gather"), name)


GATHER_US_PER_BYTE = 380.0 / 11.65e6
HOST_US = dict(ffn_up=68.0, ffn_down=37.0, in_proj=38.0, attn_fwd=103.0, ssm_fwd=70.0, merge_fwd=30.0,
               out_proj=23.0)
HOST_SLACK_US = 10.0


class _WeightStream:
    def __init__(self, pieces):
        self.keys = [k for k, _ in pieces]
        self.shards = dict(pieces)
        self.next = 0
        self.full = {}
        self.pending = []

    def comm_for(self, host):
        budget = HOST_US[host] + HOST_SLACK_US
        taken, cost = [], 0.0
        while self.next < len(self.keys):
            key = self.keys[self.next]
            c = self.shards[key].size * self.shards[key].dtype.itemsize * GATHER_US_PER_BYTE
            if cost + c > budget and taken:
                break
            taken.append(key)
            cost += c
            self.next += 1
        self.pending = taken
        if not taken:
            return None
        return _gather_comm([self.shards[k] for k in taken], "g_" + "_".join(k[1] for k in taken))

    def deposit(self, gathered):
        for key, arr in zip(self.pending, gathered):
            self.full[key] = arr
        self.pending = []

    def get(self, key):
        if key not in self.full:
            upto = self.keys.index(key) + 1
            keys = self.keys[self.next:upto]
            self.next = upto
            for k, arr in zip(keys, _all_gather_chips([self.shards[k] for k in keys], "gather_now")):
                self.full[k] = arr
        return self.full[key]


def _all_gather_devices(x_shard, name):
    m_per, ncol = x_shard.shape

    def body(x_ref, out_ref, send_sems, recv_sems, local_sem):
        x, y, c = _mesh_pos()
        me, sibling = (x, y, c), (x, y, 1 - c)
        chips = [(1 - x, y), (x, 1 - y), (1 - x, 1 - y)]

        def rows(px, py, pc):
            return out_ref.at[4 * px + 2 * py + pc]

        def copy(k, block, to, src=None):
            return pltpu.make_async_remote_copy(
                src_ref=rows(*block) if src is None else src, dst_ref=rows(*block),
                send_sem=send_sems.at[k], recv_sem=recv_sems.at[k], device_id=to, device_id_type=MESH)

        mine = pltpu.make_async_copy(x_ref, rows(*me), local_sem)
        mine.start()
        first = [copy(0, me, sibling, src=x_ref)]
        first += [copy(1 + j, me, (*chip, c), src=x_ref) for j, chip in enumerate(chips)]
        for cp in first:
            cp.start()
        passed = [copy(4 + j, (*chip, c), sibling) for j, chip in enumerate(chips)]
        for j, chip in enumerate(chips):
            copy(1 + j, (*chip, c), me).wait_recv()
            passed[j].start()
        copy(0, sibling, me).wait_recv()
        for j, chip in enumerate(chips):
            copy(4 + j, (*chip, 1 - c), me).wait_recv()
        for cp in first + passed:
            cp.wait_send()
        mine.wait()

    return pl.pallas_call(
        body, out_shape=jax.ShapeDtypeStruct((8, m_per, ncol), x_shard.dtype),
        in_specs=[pl.BlockSpec(memory_space=pltpu.VMEM)], out_specs=pl.BlockSpec(memory_space=pltpu.VMEM),
        scratch_shapes=[pltpu.SemaphoreType.DMA((7,)), pltpu.SemaphoreType.DMA((7,)), pltpu.SemaphoreType.DMA],
        compiler_params=pltpu.CompilerParams(vmem_limit_bytes=VMEM_LIMIT), name=name)(x_shard)


def _device_gather_comm(x_shard, tag):
    def ctx(ins, outs, sems):
        (x_ref,), (out_ref,) = ins, outs
        send_sems, recv_sems, local_sems = sems
        x, y, c = _mesh_pos()
        me, sibling = (x, y, c), (x, y, 1 - c)
        chips = [(1 - x, y), (x, 1 - y), (1 - x, 1 - y)]

        def rows(px, py, pc):
            return out_ref.at[4 * px + 2 * py + pc]

        def copy(k, block, to, src=None):
            return pltpu.make_async_remote_copy(
                src_ref=rows(*block) if src is None else src, dst_ref=rows(*block),
                send_sem=send_sems.at[k], recv_sem=recv_sems.at[k], device_id=to, device_id_type=MESH)

        mine = pltpu.make_async_copy(x_ref, rows(*me), local_sems.at[0])
        first = [copy(0, me, sibling, src=x_ref)] + [copy(1 + j, me, (*chip, c), src=x_ref)
                                                     for j, chip in enumerate(chips)]
        passed = [copy(4 + j, (*chip, c), sibling) for j, chip in enumerate(chips)]
        landed = [copy(1 + j, (*chip, c), me) for j, chip in enumerate(chips)]
        last = [copy(0, sibling, me)] + [copy(4 + j, (*chip, 1 - c), me) for j, chip in enumerate(chips)]
        return mine, first, passed, landed, last

    def start(ins, outs, sems):
        mine, first, _, _, _ = ctx(ins, outs, sems)
        mine.start()
        for cp in first:
            cp.start()

    def mid(ins, outs, sems):
        _, _, passed, landed, _ = ctx(ins, outs, sems)
        for cp, fwd in zip(landed, passed):
            cp.wait_recv()
            fwd.start()

    def finish(ins, outs, sems):
        mine, first, passed, _, last = ctx(ins, outs, sems)
        for cp in last:
            cp.wait_recv()
        for cp in first + passed:
            cp.wait_send()
        mine.wait()

    return _Comm(
        tag, [x_shard], [jax.ShapeDtypeStruct((8,) + x_shard.shape, x_shard.dtype)],
        [pltpu.SemaphoreType.DMA((7,)), pltpu.SemaphoreType.DMA((7,)), pltpu.SemaphoreType.DMA((1,))],
        start, mid, finish)


def _sum_devices(g8, name):
    _, r, c = g8.shape
    tr = _div_tile(r, c * 4 * 8)

    def body(g_ref, o_ref):
        acc = g_ref[0]
        for dev in range(1, 8):
            acc = acc + g_ref[dev]
        o_ref[...] = acc

    return pl.pallas_call(
        body, out_shape=jax.ShapeDtypeStruct((r, c), F32), grid=(r // tr,),
        in_specs=[pl.BlockSpec((8, tr, c), lambda i: (0, i, 0))], out_specs=pl.BlockSpec((tr, c), lambda i: (i, 0)),
        compiler_params=_cparams(("parallel",)), name=name)(g8)


def _chip_partials(arrs, recvs, pos, name):
    n = len(arrs)

    def body(pos_ref, *refs):
        for a_ref, b_ref, o_ref in zip(refs[:n], refs[n:2 * n], refs[2 * n:]):
            o_ref[...] = (a_ref[...] + b_ref[...]).astype(BF16)

    own_specs, recv_specs, shapes = [], [], []
    for arr in arrs:
        nslab, r, c = arr.shape
        own_specs.append(pl.BlockSpec((None, r // 2, c), lambda j, p: (j, p[0], 0)))
        recv_specs.append(pl.BlockSpec((None, r // 2, c), lambda j, p: (j, 0, 0)))
        shapes.append(jax.ShapeDtypeStruct((nslab, r // 2, c), BF16))
    grid_spec = pltpu.PrefetchScalarGridSpec(
        num_scalar_prefetch=1, grid=(N_CHIPS,), in_specs=own_specs + recv_specs, out_specs=recv_specs)
    return pl.pallas_call(
        body, out_shape=shapes, grid_spec=grid_spec,
        compiler_params=_cparams(("parallel",)), name=name)(pos, *arrs, *recvs)


def _chip_exchange_comm(parts, tag):
    n = len(parts)

    def copies(ins, outs, sems):
        send_sems, recv_sems = sems
        x, y, c = _mesh_pos()
        chips = [(1 - x, y), (x, 1 - y), (1 - x, 1 - y)]
        return [pltpu.make_async_remote_copy(
            src_ref=ins[k].at[2 * chip[0] + chip[1]], dst_ref=outs[k].at[j],
            send_sem=send_sems.at[3 * k + j], recv_sem=recv_sems.at[3 * k + j],
            device_id=(*chip, c), device_id_type=MESH) for k in range(n) for j, chip in enumerate(chips)]

    def start(ins, outs, sems):
        for cp in copies(ins, outs, sems):
            cp.start()

    def finish(ins, outs, sems):
        for cp in copies(ins, outs, sems):
            cp.wait()

    return _Comm(
        tag, parts, [jax.ShapeDtypeStruct((3,) + p.shape[1:], p.dtype) for p in parts],
        [pltpu.SemaphoreType.DMA((3 * n,)), pltpu.SemaphoreType.DMA((3 * n,))], start, None, finish)


def _reduce_halves(arrs, recvs, gots, pos, name):
    n = len(arrs)

    def body(pos_ref, *refs):
        for a_ref, b_ref, g_ref, o_ref in zip(refs[:n], refs[n:2 * n], refs[2 * n:3 * n], refs[3 * n:]):
            acc = a_ref[...] + b_ref[...]
            for j in range(3):
                acc = acc + g_ref[j].astype(F32)
            o_ref[...] = acc

    own_specs, recv_specs, got_specs, out_specs, shapes = [], [], [], [], []
    for arr in arrs:
        _, r, c = arr.shape
        own_specs.append(pl.BlockSpec((None, r // 2, c), lambda i, p: (p[1], p[0], 0)))
        recv_specs.append(pl.BlockSpec((None, r // 2, c), lambda i, p: (p[1], 0, 0)))
        got_specs.append(pl.BlockSpec((3, r // 2, c), lambda i, p: (0, 0, 0)))
        out_specs.append(pl.BlockSpec((r // 2, c), lambda i, p: (0, 0)))
        shapes.append(jax.ShapeDtypeStruct((r // 2, c), F32))
    grid_spec = pltpu.PrefetchScalarGridSpec(
        num_scalar_prefetch=1, grid=(1,), in_specs=own_specs + recv_specs + got_specs, out_specs=out_specs)
    return pl.pallas_call(
        body, out_shape=shapes, grid_spec=grid_spec,
        compiler_params=_cparams(("arbitrary",)), name=name)(pos, *arrs, *recvs, *gots)


def _share_halves(halves, name):
    n = len(halves)

    def body(*refs):
        ins, outs = refs[:n], refs[n:2 * n]
        send_sems, recv_sems = refs[2 * n:]
        x, y, c = _mesh_pos()
        cps = []
        for k in range(n):
            cp = pltpu.make_async_remote_copy(
                src_ref=ins[k], dst_ref=outs[k], send_sem=send_sems.at[k], recv_sem=recv_sems.at[k],
                device_id=(x, y, 1 - c), device_id_type=MESH)
            cp.start()
            cps.append(cp)
        for cp in cps:
            cp.wait()

    return pl.pallas_call(
        body, out_shape=[jax.ShapeDtypeStruct(h.shape, h.dtype) for h in halves],
        in_specs=[HBM_SPEC] * n, out_specs=[HBM_SPEC] * n,
        scratch_shapes=[pltpu.SemaphoreType.DMA((n,)), pltpu.SemaphoreType.DMA((n,))], name=name)(*halves)


class _Reduction:
    def __init__(self, arrs, others, pos, tag):
        self.arrs, self.pos, self.tag = arrs, pos, tag
        self.recv = _share_halves(others, "rs_sibling_" + tag)
        self.parts = _chip_partials(arrs, self.recv, pos, "rs_partial_" + tag)
        self.got = None

    def comm(self):
        return _chip_exchange_comm(self.parts, "rs_" + self.tag)

    def end(self):
        if self.got is None:
            self.got = _run_comm(self.comm(), "rs_chips_" + self.tag)
        return _reduce_halves(self.arrs, self.recv, self.got, self.pos, "rs_reduce_" + self.tag)


def _w_in_full(p, l, ws):
    slabs = ws.get((l, "w_in"))
    return slabs.reshape(-1, slabs.shape[2])


def _w3t_full(p, l, ws):
    if "w3t" not in p:
        slabs = ws.get((l, "w3"))
        p["w3t"] = jnp.swapaxes(slabs, 0, 1).reshape(slabs.shape[1], -1, slabs.shape[3])
    return p["w3t"]


def _w_out_full(l, ws):
    slabs = ws.get((l, "w_out"))
    return slabs.reshape(-1, slabs.shape[2])


def _layer_fwd(h, n0, l, p, next_gain, ws, tabs):
    def hosted(host, fn, *args):
        out, got = fn(*args, ws.comm_for(host))
        ws.deposit(got)
        return out

    ffn1_saved = hosted("ffn_up", _ffn_up, n0, ws.get((l, "wg1")), ws.get((l, "wu1")))
    h1, n = hosted("ffn_down", _ffn_down, ffn1_saved[2], ws.get((l, "wd1")), h, p["mix_norm"])
    ssm_w = p["ssm_d"].shape[0]
    q, k, v, u, ga, gs = hosted("in_proj", _in_proj, n, _w_in_full(p, l, ws), tabs, ssm_w)
    o = hosted("attn_fwd", _attn_fwd, q, k, v, p["attn_sinks"])
    y, yg = hosted("ssm_fwd", _ssm_fwd, u, *p["ssm_tabs"], p["ssm_d"])
    merged = hosted("merge_fwd", _merge_fwd, o, yg, ga, gs, _w3t_full(p, l, ws))
    h2, n2 = hosted("out_proj", _out_proj, merged, _w_out_full(l, ws), h1, p["ffn2_norm"])
    ffn2_saved = hosted("ffn_up", _ffn_up, n2, ws.get((l, "wg2")), ws.get((l, "wu2")))
    h3, *n3 = hosted("ffn_down", _ffn_down, ffn2_saved[2], ws.get((l, "wd2")), h2, next_gain)
    saved = dict(h0=h, h1=h1, h2=h2, ffn1=ffn1_saved, ffn2=ffn2_saved, n_mix=n, q=q, k=k, v=v, u=u, ga=ga, gs=gs,
                 o=o, y=y, yg=yg, merged=merged)
    return h3, (n3[0] if n3 else None), saved


def _layer_bwd(dh_pair, l, p, ws, s, tabs, pos, early_comm=None):
    g = {}
    (dh2, dhb), g["ffn2_norm"], red_ffn2, _, _ = _ffn_bwd(
        dh_pair, s["h2"], p["ffn2_norm"], ws.get((l, "wg2")), ws.get((l, "wu2")), ws.get((l, "wd2")), p["f4"],
        s["ffn2"], pos)
    w3, w_out_w = _w3t_full(p, l, ws), _w_out_full(l, ws)
    lp, d = dh2.shape
    d4 = d // N_CHIPS
    dw_out, dw_out_other = _tn_full(s["merged"], dhb, "mix_dw_out")
    dattn, dv, dg, dga, dgs = _merge_bwd(dhb, w_out_w, s["o"], s["yg"], s["ga"], s["gs"], w3)
    (dw_ap,), (dw_ap_other,) = _tn_cols(s["o"], [dattn], "mix_dw_ap")
    (dw_gv, dw_gg), (dw_gv_other, dw_gg_other) = _tn_cols(s["yg"], [dv, dg], "mix_dw_glu")
    do, dy = _branch_bwd(dattn, dv, dg, w3, s["y"])
    (dq, dk, dvv, dkm, dvm, dsink), _ = _attn_bwd(s["q"], s["k"], s["v"], do, p["attn_sinks"], tabs)
    g["attn_sinks"] = dsink[:, 0]
    (du, dlr, dli, dbr, dbi, dcr, dci, dd), _ = _ssm_bwd(s["u"], dy, *p["ssm_tabs"], p["ssm_d"])
    ngrp = p["ssm_d"].shape[0] // SSM_GROUP
    g["ssm_lam"] = (dlr.reshape(ngrp, SSM_STATE), dli.reshape(ngrp, SSM_STATE),
                    _ssm_untable_b(dbr, ngrp), _ssm_untable_b(dbi, ngrp))
    g["ssm_c_re"] = _ssm_untable_c(dcr, ngrp)
    g["ssm_c_im"] = _ssm_untable_c(dci, ngrp)
    g["ssm_d"] = dd[0]
    dk = dk.at[:BLOCK].add(dkm)
    dvv = dvv.at[:BLOCK].add(dvm)
    dz = jnp.concatenate([dq.astype(BF16), dk.astype(BF16), dvv.astype(BF16), du.astype(BF16), dga, dgs], axis=1)
    n = s["n_mix"]
    w_in = _w_in_full(p, l, ws)
    dw_in, dw_in_other = _tn_full(dz, n, "mix_dw_in", d // 2)
    red_mix = _Reduction([dw_in, dw_ap, dw_gv, dw_gg, dw_out],
                         [dw_in_other, dw_ap_other, dw_gv_other, dw_gg_other, dw_out_other], pos, "mix")
    dh1, g["mix_norm"], dh1b = _in_proj_bwd(dz, w_in, dh2, s["h1"], p["mix_norm"])
    comm2 = None if early_comm is None else early_comm(g)
    dh0_pair, g["ffn1_norm"], red_ffn1, red_mix.got, early_got = _ffn_bwd(
        (dh1, dh1b), s["h0"], p["ffn1_norm"], ws.get((l, "wg1")), ws.get((l, "wu1")), ws.get((l, "wd1")), p["f4"],
        s["ffn1"], pos, red_mix.comm(), comm2)
    return dh0_pair, g, [*red_ffn1, red_mix, *red_ffn2], early_got


BIG = ["ffn1_w_gate", "ffn1_w_up", "ffn1_w_down", "w_in", "w_attn_proj", "w_glu_v", "w_glu_g", "w_out",
       "ffn2_w_gate", "ffn2_w_up", "ffn2_w_down"]
TRANSPOSED = ["ffn1_w_gate", "ffn1_w_up", "w_in", "ffn2_w_gate", "ffn2_w_up"]
SMALL = ["ffn1_norm", "mix_norm", "attn_sinks", "ssm_a_re", "ssm_a_im", "ssm_log_dt", "ssm_b_re", "ssm_b_im",
         "ssm_c_re", "ssm_c_im", "ssm_d", "ffn2_norm", "final_norm"]
WEIGHTS = ["meta_tokens", "ffn1_norm", "ffn1_w_gate", "ffn1_w_up", "ffn1_w_down", "mix_norm", "w_in", "attn_sinks",
           "ssm_a_re", "ssm_a_im", "ssm_log_dt", "ssm_b_re", "ssm_b_im", "ssm_c_re", "ssm_c_im", "ssm_d",
           "w_attn_proj", "w_glu_v", "w_glu_g", "w_out", "ffn2_norm", "ffn2_w_gate", "ffn2_w_up", "ffn2_w_down",
           "final_norm"]


def _small_rows(shape):
    rows = -(-math.prod(shape) // LANES)
    return -(-rows // SUBLANES) * SUBLANES


def _pack_small(tree, names):
    parts = []
    for k in names:
        size, rows = math.prod(tree[k].shape), _small_rows(tree[k].shape)
        if size % LANES == 0:
            part = tree[k].reshape(size // LANES, LANES)
        else:
            part = jnp.pad(tree[k].reshape(1, size), ((0, 0), (0, LANES - size)))
        parts.append(jnp.pad(part, ((0, rows - part.shape[0]), (0, 0))))
    return jnp.concatenate(parts, axis=0)


def _unpack_small(packed, like, names):
    out, off = {}, 0
    for k in names:
        size, rows = math.prod(like[k].shape), _small_rows(like[k].shape)
        if size % LANES == 0:
            out[k] = packed[off:off + size // LANES].reshape(like[k].shape)
        else:
            out[k] = packed[off, :size].reshape(like[k].shape)
        off += rows
    return out


def kernel(x, meta_tokens, ffn1_norm, ffn1_w_gate, ffn1_w_up, ffn1_w_down, mix_norm, w_in, attn_sinks, ssm_a_re, ssm_a_im, ssm_log_dt, ssm_b_re, ssm_b_im, ssm_c_re, ssm_c_im, ssm_d, w_attn_proj, w_glu_v, w_glu_g, w_out, ffn2_norm, ffn2_w_gate, ffn2_w_up, ffn2_w_down, final_norm, loss_target, m_meta_tokens, m_ffn1_norm, m_ffn1_w_gate, m_ffn1_w_up, m_ffn1_w_down, m_mix_norm, m_w_in, m_attn_sinks, m_ssm_a_re, m_ssm_a_im, m_ssm_log_dt, m_ssm_b_re, m_ssm_b_im, m_ssm_c_re, m_ssm_c_im, m_ssm_d, m_w_attn_proj, m_w_glu_v, m_w_glu_g, m_w_out, m_ffn2_norm, m_ffn2_w_gate, m_ffn2_w_up, m_ffn2_w_down, m_final_norm, v_meta_tokens, v_ffn1_norm, v_ffn1_w_gate, v_ffn1_w_up, v_ffn1_w_down, v_mix_norm, v_w_in, v_attn_sinks, v_ssm_a_re, v_ssm_a_im, v_ssm_log_dt, v_ssm_b_re, v_ssm_b_im, v_ssm_c_re, v_ssm_c_im, v_ssm_d, v_w_attn_proj, v_w_glu_v, v_w_glu_g, v_w_out, v_ffn2_norm, v_ffn2_w_gate, v_ffn2_w_up, v_ffn2_w_down, v_final_norm):
    args = dict(locals())
    w = {k: args[k] for k in WEIGHTS}
    m = {k: args["m_" + k] for k in WEIGHTS}
    v = {k: args["v_" + k] for k in WEIGHTS}
    depth = ffn1_norm.shape[0]
    seq, d = x.shape[1], x.shape[2]
    lp = seq + BLOCK
    xi, yi, ci = _mesh_pos()
    pos = jnp.stack([ci, 2 * xi + yi]).astype(jnp.int32)

    tabs = _rope_tables(lp)
    layers, pieces = [], [((0, "meta"), meta_tokens)]
    f4 = ffn1_w_gate.shape[2]
    fp = -(-f4 // MXU_DIM) * MXU_DIM

    def ffn_rows(wt):
        return jnp.pad(wt, ((0, fp - f4), (0, 0))).astype(BF16)

    for l in range(depth):
        small = [((l, "w3"), jnp.stack([w_attn_proj[l].T, w_glu_v[l].T, w_glu_g[l].T]).astype(BF16)),
                 ((l, "w_out"), w_out[l].astype(BF16))]
        first = [((l, "wg1"), ffn_rows(ffn1_w_gate[l].T)), ((l, "wu1"), ffn_rows(ffn1_w_up[l].T)),
                 ((l, "wd1"), ffn_rows(ffn1_w_down[l])), ((l, "w_in"), w_in[l].T.astype(BF16))]
        pieces += (first + small if l == 0 else small + first) + [
            ((l, "wg2"), ffn_rows(ffn2_w_gate[l].T)), ((l, "wu2"), ffn_rows(ffn2_w_up[l].T)),
            ((l, "wd2"), ffn_rows(ffn2_w_down[l]))]
        lb_re, lb_im, bb_re, bb_im = _ssm_params(ssm_a_re[l], ssm_a_im[l], ssm_log_dt[l], ssm_b_re[l], ssm_b_im[l])
        ngrp = lb_re.shape[0]
        nt = ngrp // GROUPS_PER_TILE
        ssm_tabs = (lb_re.reshape(nt, 1, TILE_STATES), lb_im.reshape(nt, 1, TILE_STATES),
                    *_ssm_tables(bb_re, bb_im, ssm_c_re[l], ssm_c_im[l]))
        layers.append(dict(
            ffn1_norm=ffn1_norm[l], mix_norm=mix_norm[l], ffn2_norm=ffn2_norm[l], attn_sinks=attn_sinks[l],
            ssm_d=ssm_d[l], ssm_tabs=ssm_tabs, f4=f4))
    ws = _WeightStream(pieces)
    ws.get((0, "wu1"))
    meta_all = ws.get((0, "meta"))
    meta_full = jnp.concatenate([meta_all[j] for j in range(N_CHIPS)], axis=1)

    h = jnp.concatenate([jnp.zeros((PAD_FRONT, d), F32), meta_full, x[0]], axis=0)
    saved = []
    n0 = _rms_fwd(h, ffn1_norm[0], "rms_fwd_first")
    for l in range(depth):
        next_gain = ffn1_norm[l + 1] if l + 1 < depth else None
        h, n0, s = _layer_fwd(h, n0, l, layers[l], next_gain, ws, tabs)
        saved.append(s)
    dh, g_final, loss_acc, dhb = _loss_head(h, final_norm, loss_target[0])
    dh_pair = (dh, dhb)
    loss = lax.psum(loss_acc[0, 0], ("x", "y", "c"))

    grads, reds = [None] * depth, [None] * depth

    def layer_small(gl, l):
        _, vjp = jax.vjp(_ssm_params, ssm_a_re[l], ssm_a_im[l], ssm_log_dt[l], ssm_b_re[l], ssm_b_im[l])
        da_re, da_im, dlog_dt, db_re, db_im = vjp(gl["ssm_lam"])
        first = gl["ffn1_norm"][0] if "ffn1_norm" in gl else jnp.zeros((d,), F32)
        return dict(ffn1_norm=first, mix_norm=gl["mix_norm"][0], attn_sinks=gl["attn_sinks"], ssm_a_re=da_re,
                    ssm_a_im=da_im, ssm_log_dt=dlog_dt, ssm_b_re=db_re, ssm_b_im=db_im, ssm_c_re=gl["ssm_c_re"],
                    ssm_c_im=gl["ssm_c_im"], ssm_d=gl["ssm_d"], ffn2_norm=gl["ffn2_norm"][0])

    class early:
        got, like = None, None

    def early_comm(g0):
        per = [layer_small(g0, 0)] + [layer_small(grads[l], l) for l in range(1, depth)]
        tree = {k: jnp.stack([lay[k] for lay in per]) for k in SMALL if k != "final_norm"}
        tree["final_norm"] = g_final[0]
        early.like = tree
        return _device_gather_comm(_pack_small(tree, SMALL), "small_grads")

    for l in reversed(range(depth)):
        dh_pair, grads[l], reds[l], got = _layer_bwd(
            dh_pair, l, layers[l], ws, saved[l], tabs, pos, early_comm if l == 0 else None)
        if l == 0:
            early.got = got
    dh = dh_pair[0]
    grad_x = dh[BLOCK:][None]
    dmeta_local = dh[PAD_FRONT:BLOCK]

    g_small_tree = _unpack_small(_sum_devices(early.got[0], "sum_small_grads"), early.like, SMALL)
    late_names = ["ffn1_norm", "meta_tokens"]
    late = dict(ffn1_norm=grads[0]["ffn1_norm"], meta_tokens=dmeta_local)
    g_late = _sum_devices(_all_gather_devices(_pack_small(late, late_names), "gather_late_grads"), "sum_late_grads")
    g_late = _unpack_small(g_late, late, late_names)
    g_small_tree["ffn1_norm"] = g_small_tree["ffn1_norm"].at[0].set(g_late["ffn1_norm"][0])
    d4 = d // N_CHIPS
    chip = 2 * xi + yi
    g_meta = lax.dynamic_slice_in_dim(g_late["meta_tokens"], chip * d4, d4, axis=1)

    mine = [[half for red in reds[l] for half in red.end()] for l in range(depth)]
    flat = _share_halves([half for layer_halves in mine for half in layer_halves], "rs_share")
    per_layer = len(mine[0])
    reduced = [(mine[l], flat[l * per_layer:(l + 1) * per_layer]) for l in range(depth)]

    g_out, delta, new_m, new_v = {}, {}, {}, {}
    for i, k in enumerate(BIG):
        flip = (lambda t: jnp.swapaxes(t, 1, 2)) if k in TRANSPOSED else (lambda t: t)
        outs = _adamw_layers(
            flip(w[k]), flip(m[k]), flip(v[k]), [reduced[l][0][i] for l in range(depth)],
            [reduced[l][1][i] for l in range(depth)], pos, "adamw_" + k)
        g_out[k], delta[k], new_m[k], new_v[k] = [flip(t) for t in outs]
    g_small_tree["meta_tokens"] = g_meta
    for k in SMALL + ["meta_tokens"]:
        narrow = w[k].ndim > 2 and w[k].shape[-1] < w[k].shape[-2]
        view = (lambda t: jnp.swapaxes(t, -1, -2)) if narrow else (lambda t: t)
        shape = view(w[k]).shape if w[k].ndim > 1 else (1,) + w[k].shape
        outs = _adamw_whole(view(w[k]).reshape(shape), view(g_small_tree[k]).reshape(shape),
                            view(m[k]).reshape(shape), view(v[k]).reshape(shape), "adamw_" + k)
        g_out[k] = g_small_tree[k]
        delta[k], new_m[k], new_v[k] = [view(t).reshape(w[k].shape) for t in outs]

    return (loss, grad_x, *[g_out[k] for k in WEIGHTS], *[delta[k] for k in WEIGHTS],
            *[new_m[k] for k in WEIGHTS], *[new_v[k] for k in WEIGHTS])
```

```python
import functools
import math

import jax
import jax.numpy as jnp
from jax import lax
from jax.experimental import pallas as pl
from jax.experimental.pallas import tpu as pltpu

F32 = jnp.float32
BF16 = jnp.bfloat16

N_META = 16
HEAD_DIM = 64
N_Q_HEADS = 8
N_KV_HEADS = 2
Q_PER_KV = N_Q_HEADS // N_KV_HEADS
ATTN_WIDTH = N_Q_HEADS * HEAD_DIM
KV_WIDTH = N_KV_HEADS * HEAD_DIM
BLOCK = 128
PAD_FRONT = BLOCK - N_META
ROPE_THETA = 500000.0
ROT_DIM = HEAD_DIM // 4
SSM_GROUP = 16
SSM_STATE = 64
GROUPS_PER_TILE = 4
TILE_STATES = GROUPS_PER_TILE * SSM_STATE
LANES = 128
SUBLANES = 8
MXU_DIM = 256
PACK_ROWS = 256
EPS = 1e-6
NEG_INF = -1e30
N_CHIPS = 4

ADAM_LR = 0.001
ADAM_B1 = 0.9
ADAM_B2 = 0.999
ADAM_EPS = 1e-08
ADAM_WD = 0.01
ADAM_STEP = 10

VMEM_LIMIT = 56 * 1024 * 1024
MESH = pl.DeviceIdType.MESH


def _cparams(sem=None):
    return pltpu.CompilerParams(dimension_semantics=sem, vmem_limit_bytes=VMEM_LIMIT)


def _row_tile(rows, limit=512):
    best = None
    for t in range(128, limit + 1, 128):
        if rows % t == 0:
            best = t
    assert best is not None, rows
    return best


def _div_tile(rows, row_bytes, max_bytes=1 << 20, mult=8):
    best = None
    for t in range(mult, rows + 1, mult):
        if rows % t == 0 and t * row_bytes <= max_bytes:
            best = t
    if best is None:
        best = rows
    return best


def _dot(a, b, mode):
    if mode == "nn":
        dims = (((1,), (0,)), ((), ()))
    elif mode == "nt":
        dims = (((1,), (1,)), ((), ()))
    else:
        dims = (((0,), (0,)), ((), ()))
    return lax.dot_general(a.astype(BF16), b.astype(BF16), dims, preferred_element_type=F32)


def _sigmoid(x):
    return 1.0 / (1.0 + jnp.exp(-x))


_GELU_C = math.sqrt(2.0 / math.pi)


def _gelu(x):
    return 0.5 * x * (1.0 + jnp.tanh(_GELU_C * (x + 0.044715 * x * x * x)))


def _gelu_grad(x):
    t = jnp.tanh(_GELU_C * (x + 0.044715 * x * x * x))
    return 0.5 * (1.0 + t) + 0.5 * x * (1.0 - t * t) * _GELU_C * (1.0 + 3.0 * 0.044715 * x * x)


class _Comm:
    def __init__(self, tag, ins, out_shapes, sems, start, mid, finish):
        self.tag, self.ins, self.out_shapes, self.sems = tag, list(ins), list(out_shapes), list(sems)
        self.start, self.mid, self.finish = start, mid, finish


HBM_SPEC = pl.BlockSpec(memory_space=pltpu.HBM)
MID_NUM, MID_DEN = 4, 5


def _hosted_call(body, comm, *, out_shape, grid, in_specs, out_specs, scratch_shapes, sem, name, args):
    out_shape, in_specs, out_specs = list(out_shape), list(in_specs), list(out_specs)
    scratch_shapes = list(scratch_shapes)
    if comm is None:
        res = pl.pallas_call(
            body, out_shape=out_shape, grid=grid, in_specs=in_specs, out_specs=out_specs,
            scratch_shapes=scratch_shapes, compiler_params=_cparams(sem), name=name)(*args)
        return list(res), []
    n_in, n_out, n_sc = len(args), len(out_shape), len(scratch_shapes)
    nci, nco = len(comm.ins), len(comm.out_shapes)
    total = math.prod(grid)

    def wrapped(*refs):
        in_refs, cin = refs[:n_in], refs[n_in:n_in + nci]
        o0 = n_in + nci
        out_refs, cout = refs[o0:o0 + n_out], refs[o0 + n_out:o0 + n_out + nco]
        s0 = o0 + n_out + nco
        sc, csem = refs[s0:s0 + n_sc], refs[s0 + n_sc:]
        lin = 0
        for dim, size in enumerate(grid):
            lin = lin * size + pl.program_id(dim)

        @pl.when(lin == 0)
        def _():
            comm.start(cin, cout, csem)

        if comm.mid is not None:
            @pl.when(lin == (total * MID_NUM) // MID_DEN)
            def _():
                comm.mid(cin, cout, csem)

        body(*in_refs, *out_refs, *sc)

        @pl.when(lin == total - 1)
        def _():
            comm.finish(cin, cout, csem)

    res = pl.pallas_call(
        wrapped, out_shape=out_shape + comm.out_shapes, grid=grid,
        in_specs=in_specs + [HBM_SPEC] * nci, out_specs=out_specs + [HBM_SPEC] * nco,
        scratch_shapes=scratch_shapes + comm.sems,
        compiler_params=_cparams(("arbitrary",) * len(grid)), name=name + "_" + comm.tag)(*args, *comm.ins)
    return list(res[:n_out]), list(res[n_out:])


def _matmul(name, grid, k_axis, ins, in_specs, pairs, acc_shapes, epilogue, out_shapes, out_specs, sem, comm=None):
    n_in, n_out, n_acc = len(ins), len(out_shapes), len(acc_shapes)

    def body(*refs):
        in_refs = refs[:n_in]
        out_refs = refs[n_in:n_in + n_out]
        acc_refs = refs[n_in + n_out:]
        if k_axis is None:
            accs = [None] * n_acc
            for ia, ib, mode, iacc in pairs:
                d = _dot(in_refs[ia][...], in_refs[ib][...], mode)
                accs[iacc] = d if accs[iacc] is None else accs[iacc] + d
            epilogue(accs, in_refs, out_refs)
            return
        k = pl.program_id(k_axis)

        @pl.when(k == 0)
        def _():
            for r in acc_refs:
                r[...] = jnp.zeros_like(r)

        for ia, ib, mode, iacc in pairs:
            acc_refs[iacc][...] += _dot(in_refs[ia][...], in_refs[ib][...], mode)

        @pl.when(k == pl.num_programs(k_axis) - 1)
        def _():
            epilogue([r[...] for r in acc_refs], in_refs, out_refs)

    scratch = [] if k_axis is None else [pltpu.VMEM(s, F32) for s in acc_shapes]
    outs, couts = _hosted_call(
        body, comm, out_shape=out_shapes, grid=grid, in_specs=in_specs, out_specs=out_specs,
        scratch_shapes=scratch, sem=sem, name=name, args=ins)
    return outs if comm is None else (outs, couts)


def _rms_math(x, g):
    r = lax.rsqrt(jnp.mean(x * x, axis=-1, keepdims=True) + EPS)
    return (x * r * g).astype(BF16)


def _rms_fwd(h, g, name):
    lp, d = h.shape
    tm = _row_tile(lp)

    def body(h_ref, g_ref, n_ref):
        n_ref[...] = _rms_math(h_ref[...], g_ref[...])

    return pl.pallas_call(
        body, out_shape=jax.ShapeDtypeStruct((lp, d), BF16), grid=(lp // tm,),
        in_specs=[pl.BlockSpec((tm, d), lambda i: (i, 0)), pl.BlockSpec((1, d), lambda i: (0, 0))],
        out_specs=pl.BlockSpec((tm, d), lambda i: (i, 0)),
        compiler_params=_cparams(("parallel",)), name=name)(h, g.reshape(1, d))


def _rms_bwd_math(dn, x, g):
    r = lax.rsqrt(jnp.mean(x * x, axis=-1, keepdims=True) + EPS)
    xh = x * r
    dxh = dn * g
    dx = r * (dxh - xh * jnp.mean(dxh * xh, axis=-1, keepdims=True))
    return dx, dn * xh


def _ffn_up(n, wgt, wut, comm=None):
    lp, d = n.shape
    fp = wgt.shape[1]
    tm = _row_tile(lp)

    def up_body(n_ref, wg_ref, wu_ref, a_ref, b_ref, s_ref):
        x = n_ref[...]
        for jc in range(N_CHIPS):
            cols = slice(jc * fp, (jc + 1) * fp)
            a = _dot(x, wg_ref[jc], "nt")
            b = _dot(x, wu_ref[jc], "nt")
            a_ref[:, cols] = a.astype(BF16)
            b_ref[:, cols] = b.astype(BF16)
            s_ref[:, cols] = (a * _sigmoid(a) * b).astype(BF16)

    ff = N_CHIPS * fp
    act = jax.ShapeDtypeStruct((lp, ff), BF16)
    act_tile = pl.BlockSpec((tm, ff), lambda i: (i, 0))
    w_spec = pl.BlockSpec((N_CHIPS, fp, d), lambda i: (0, 0, 0))
    outs, couts = _hosted_call(
        up_body, comm, out_shape=[act, act, act], grid=(lp // tm,),
        in_specs=[pl.BlockSpec((tm, d), lambda i: (i, 0)), w_spec, w_spec],
        out_specs=[act_tile] * 3, scratch_shapes=[], sem=("parallel",), name="ffn_up", args=(n, wgt, wut))
    return (*outs, n), couts


def _residual_outputs(h_new, in_refs, out_refs, gain_at):
    out_refs[0][...] = h_new
    if gain_at is not None:
        out_refs[1][...] = _rms_math(h_new, in_refs[gain_at][...])


def _residual_specs(lp, d, tm, next_gain):
    row = pl.BlockSpec((tm, d), lambda i: (i, 0))
    shapes, specs = [jax.ShapeDtypeStruct((lp, d), F32)], [row]
    extra_in, extra_specs = [], []
    if next_gain is not None:
        shapes.append(jax.ShapeDtypeStruct((lp, d), BF16))
        specs.append(row)
        extra_in, extra_specs = [next_gain.reshape(1, d)], [pl.BlockSpec((1, d), lambda i: (0, 0))]
    return shapes, specs, extra_in, extra_specs


def _ffn_down(s, wd, h, next_gain, comm=None):
    lp, d = h.shape
    ff = s.shape[1]
    tm = _row_tile(lp)
    shapes, specs, extra_in, extra_specs = _residual_specs(lp, d, tm, next_gain)

    def down_epi(accs, in_refs, out_refs):
        _residual_outputs(in_refs[2][...] + 0.5 * accs[0], in_refs, out_refs, 3 if extra_in else None)

    res = _matmul(
        "ffn_down", (lp // tm,), None, [s, wd.reshape(ff, d), h] + extra_in,
        [pl.BlockSpec((tm, ff), lambda i: (i, 0)), pl.BlockSpec((ff, d), lambda i: (0, 0)),
         pl.BlockSpec((tm, d), lambda i: (i, 0))] + extra_specs,
        [(0, 1, "nn", 0)], [(tm, d)], down_epi, shapes, specs, ("parallel",), comm)
    return (res, []) if comm is None else res


def _tn_tiles(lp):
    return _row_tile(lp, 1408)


def _ffn_bwd(dh_pair, h_in, gain, wgt, wut, wd, f4, saved, pos, comm=None, comm2=None):
    dh, dhb = dh_pair
    a, b, s, n = saved
    lp, d = h_in.shape
    fp = wgt.shape[1]
    ff = N_CHIPS * fp
    tm = _row_tile(lp)
    ni = lp // tm
    tk = _tn_tiles(lp)
    nk = lp // tk

    def ds_body(dh_ref, wd_ref, a_ref, b_ref, da_ref, db_ref):
        x = dh_ref[...]
        for jc in range(N_CHIPS):
            cols = slice(jc * fp, (jc + 1) * fp)
            ds = 0.5 * _dot(x, wd_ref[jc], "nt")
            av = a_ref[:, cols].astype(F32)
            bv = b_ref[:, cols].astype(F32)
            sg = _sigmoid(av)
            da_ref[:, cols] = (ds * bv * sg * (1.0 + av * (1.0 - sg))).astype(BF16)
            db_ref[:, cols] = (ds * av * sg).astype(BF16)

    act = jax.ShapeDtypeStruct((lp, ff), BF16)
    act_tile = pl.BlockSpec((tm, ff), lambda i: (i, 0))
    (da, db), couts = _hosted_call(
        ds_body, comm, out_shape=[act, act], grid=(ni,),
        in_specs=[pl.BlockSpec((tm, d), lambda i: (i, 0)), pl.BlockSpec((N_CHIPS, fp, d), lambda i: (0, 0, 0)),
                  act_tile, act_tile],
        out_specs=[act_tile, act_tile], scratch_shapes=[], sem=("parallel",), name="ffn_bwd_ds",
        args=(dhb, wd, a, b))

    dw_shape = jax.ShapeDtypeStruct((N_CHIPS, f4, d), F32)
    dw_spec = pl.BlockSpec((None, f4, d), lambda j, k: (j, 0, 0))
    in_col = pl.BlockSpec((tk, fp), lambda j, k: (k, j))
    in_row = pl.BlockSpec((tk, d), lambda j, k: (k, 0))

    half_shape = jax.ShapeDtypeStruct((N_CHIPS, f4 // 2, d), BF16)
    half_spec = pl.BlockSpec((None, f4 // 2, d), lambda j, k: (j, 0, 0))

    def dwd_epi(accs, in_refs, out_refs):
        dw = 0.5 * accs[0]
        out_refs[0][...] = dw[:f4]
        out_refs[1][...] = _sibling_half(dw, f4)

    res = _matmul(
        "ffn_dwd", (N_CHIPS, nk), 1, [s, dhb], [in_col, in_row],
        [(0, 1, "tn", 0)], [(fp, d)], dwd_epi, [dw_shape, half_shape], [dw_spec, half_spec],
        ("arbitrary", "arbitrary"), comm2)
    (dwd, dwd_other), couts2 = (res, []) if comm2 is None else res

    def dwgu_epi(accs, in_refs, out_refs):
        for i, acc in enumerate(accs):
            out_refs[i][...] = acc[:f4]
            out_refs[2 + i][...] = _sibling_half(acc, f4)

    red_down = _Reduction([dwd], [dwd_other], pos, "ffn_d")
    (dwg, dwu, dwg_other, dwu_other), red_down.got = _matmul(
        "ffn_dwgu", (N_CHIPS, nk), 1, [n, da, db], [in_row, in_col, in_col],
        [(1, 0, "tn", 0), (2, 0, "tn", 1)], [(fp, d)] * 2, dwgu_epi,
        [dw_shape, dw_shape, half_shape, half_shape], [dw_spec, dw_spec, half_spec, half_spec],
        ("arbitrary", "arbitrary"), red_down.comm())

    def dn_epi(accs, in_refs, out_refs):
        i = pl.program_id(0)
        dx, dgrow = _rms_bwd_math(accs[0], in_refs[5][...], in_refs[6][...])
        dh_new = in_refs[4][...] + dx
        out_refs[0][...] = dh_new
        out_refs[2][...] = dh_new.astype(BF16)

        @pl.when(i == 0)
        def _():
            out_refs[1][...] = jnp.zeros_like(out_refs[1])

        out_refs[1][...] += jnp.sum(dgrow, axis=0, keepdims=True)

    red = _Reduction([dwg, dwu], [dwg_other, dwu_other], pos, "ffn_gu")
    row_spec = pl.BlockSpec((tm, d), lambda i: (i, 0))
    act_spec = pl.BlockSpec((tm, ff), lambda i: (i, 0))
    w_spec = pl.BlockSpec((ff, d), lambda i: (0, 0))
    one_spec = pl.BlockSpec((1, d), lambda i: (0, 0))
    (dh_in, dgain, dh_in_b), red.got = _matmul(
        "ffn_bwd_dn", (ni,), None, [da, wgt.reshape(ff, d), db, wut.reshape(ff, d), dh, h_in, gain.reshape(1, d)],
        [act_spec, w_spec, act_spec, w_spec, row_spec, row_spec, one_spec],
        [(0, 1, "nn", 0), (2, 3, "nn", 0)], [(tm, d)], dn_epi,
        [jax.ShapeDtypeStruct((lp, d), F32), jax.ShapeDtypeStruct((1, d), F32), jax.ShapeDtypeStruct((lp, d), BF16)],
        [row_spec, one_spec, row_spec], ("arbitrary",), red.comm())
    return (dh_in, dh_in_b), dgain, [red, red_down], couts, couts2


def _rope_tables(lp):
    pos = jnp.arange(lp, dtype=F32) - float(PAD_FRONT)
    inv_freq = ROPE_THETA ** (-jnp.arange(0, ROT_DIM, 2, dtype=F32) / ROT_DIM)
    ang = pos[:, None] * inv_freq[None, :]
    cos, sin = jnp.cos(ang), jnp.sin(ang)
    half = ROT_DIM // 2
    ones = jnp.ones((lp, HEAD_DIM - ROT_DIM), F32)
    zeros_h = jnp.zeros((lp, half), F32)
    zeros_r = jnp.zeros((lp, HEAD_DIM - ROT_DIM), F32)
    c = jnp.concatenate([cos, cos, ones], axis=1)
    s1 = jnp.concatenate([-sin, zeros_h, zeros_r], axis=1)
    s2 = jnp.concatenate([zeros_h, sin, zeros_r], axis=1)
    reps = LANES // HEAD_DIM
    return jnp.stack([jnp.tile(c, (1, reps)), jnp.tile(s1, (1, reps)), jnp.tile(s2, (1, reps))])


def _rope(x, c, s1, s2):
    half = ROT_DIM // 2
    outs = []
    for ch in range(x.shape[1] // LANES):
        xc = x[:, ch * LANES:(ch + 1) * LANES]
        outs.append(xc * c + pltpu.roll(xc, LANES - half, 1) * s1 + pltpu.roll(xc, half, 1) * s2)
    return outs[0] if len(outs) == 1 else jnp.concatenate(outs, axis=1)


def _rope_t(dy, c, s1, s2):
    half = ROT_DIM // 2
    outs = []
    for ch in range(dy.shape[1] // LANES):
        dc = dy[:, ch * LANES:(ch + 1) * LANES]
        outs.append(dc * c + pltpu.roll(dc * s1, half, 1) + pltpu.roll(dc * s2, LANES - half, 1))
    return outs[0] if len(outs) == 1 else jnp.concatenate(outs, axis=1)


def _in_proj(n, w_in, tabs, ssm_w, comm=None):
    lp, d = n.shape
    inw = w_in.shape[0]
    tm = _row_tile(lp)
    o1 = ATTN_WIDTH
    o2 = o1 + KV_WIDTH
    o3 = o2 + KV_WIDTH
    o4 = o3 + ssm_w
    o5 = o4 + d

    def epi(accs, in_refs, out_refs):
        z = accs[0]
        c, s1, s2 = in_refs[2][0], in_refs[2][1], in_refs[2][2]
        out_refs[0][...] = _rope(z[:, :o1], c, s1, s2).astype(BF16)
        out_refs[1][...] = _rope(z[:, o1:o2], c, s1, s2).astype(BF16)
        out_refs[2][...] = z[:, o2:o3].astype(BF16)
        out_refs[3][...] = z[:, o3:o4]
        out_refs[4][...] = z[:, o4:o5]
        out_refs[5][...] = z[:, o5:]

    def rs(w, dt):
        return jax.ShapeDtypeStruct((lp, w), dt), pl.BlockSpec((tm, w), lambda i: (i, 0))

    shapes, specs = zip(rs(o1, BF16), rs(KV_WIDTH, BF16), rs(KV_WIDTH, BF16), rs(ssm_w, F32), rs(d, F32), rs(d, F32))
    res = _matmul(
        "mix_in_proj", (lp // tm,), None, [n, w_in, tabs],
        [pl.BlockSpec((tm, d), lambda i: (i, 0)), pl.BlockSpec((inw, d), lambda i: (0, 0)),
         pl.BlockSpec((3, tm, LANES), lambda i: (0, i, 0))],
        [(0, 1, "nt", 0)], [(tm, inw)], epi, list(shapes), list(specs), ("parallel",), comm)
    return (res, []) if comm is None else res


def _attn_mask(b):
    rows = lax.broadcasted_iota(jnp.int32, (BLOCK, 3 * BLOCK), 0)
    cols = lax.broadcasted_iota(jnp.int32, (BLOCK, 3 * BLOCK), 1)
    qpos = b * BLOCK + rows - PAD_FRONT
    kpos = (b - 1) * BLOCK + cols - PAD_FRONT
    dist = qpos - kpos
    band = (cols < 2 * BLOCK) & (kpos >= N_META) & (dist >= 0) & (dist < BLOCK)
    mrow = cols - 2 * BLOCK
    meta = (mrow >= PAD_FRONT) & ((mrow - PAD_FRONT) <= qpos)
    return band | meta


def _attn_probs(qh, kk, mask, sink):
    s = _dot(qh, kk, "nt") * (HEAD_DIM ** -0.5)
    s = jnp.where(mask, s, NEG_INF)
    m = jnp.maximum(jnp.max(s, axis=-1, keepdims=True), sink)
    e = jnp.exp(s - m)
    es = jnp.exp(sink - m)
    z = jnp.sum(e, axis=-1, keepdims=True) + es
    inv = 1.0 / z
    return e * inv, es * inv


def _head(ref_or_val, h):
    return ref_or_val[:, h * HEAD_DIM:(h + 1) * HEAD_DIM]


def _attn_fwd(q, k, v, sinks, comm=None):
    lp = q.shape[0]
    nb = lp // BLOCK

    def body(sink_ref, q_ref, kp_ref, kc_ref, km_ref, vp_ref, vc_ref, vm_ref, o_ref):
        b = pl.program_id(0)
        mask = _attn_mask(b)
        for hk in range(N_KV_HEADS):
            kk = jnp.concatenate([_head(kp_ref, hk), _head(kc_ref, hk), _head(km_ref, hk)], axis=0)
            vv = jnp.concatenate([_head(vp_ref, hk), _head(vc_ref, hk), _head(vm_ref, hk)], axis=0)
            for g in range(Q_PER_KV):
                h = hk * Q_PER_KV + g
                p, _ = _attn_probs(_head(q_ref, h), kk, mask, sink_ref[h])
                o_ref[:, h * HEAD_DIM:(h + 1) * HEAD_DIM] = _dot(p, vv, "nn").astype(BF16)

    cur = lambda b: (b, 0)
    prev = lambda b: (jnp.maximum(b - 1, 0), 0)
    first = lambda b: (0, 0)
    kvs = lambda f: pl.BlockSpec((BLOCK, KV_WIDTH), f)
    (o,), couts = _hosted_call(
        body, comm, out_shape=[jax.ShapeDtypeStruct((lp, ATTN_WIDTH), BF16)], grid=(nb,),
        in_specs=[pl.BlockSpec(memory_space=pltpu.SMEM), pl.BlockSpec((BLOCK, ATTN_WIDTH), cur),
                  kvs(prev), kvs(cur), kvs(first), kvs(prev), kvs(cur), kvs(first)],
        out_specs=[pl.BlockSpec((BLOCK, ATTN_WIDTH), cur)], scratch_shapes=[],
        sem=("parallel",), name="attn_fwd", args=(sinks, q, k, k, k, v, v, v))
    return o, couts


def _attn_bwd(q, k, v, do, sinks, tabs, comm=None):
    lp = q.shape[0]
    nb = lp // BLOCK
    scale = HEAD_DIM ** -0.5

    def body(sink_ref, q_ref, do_ref, kp_ref, kc_ref, km_ref, vp_ref, vc_ref, vm_ref, tq_ref, tk_ref, t0_ref,
             dq_ref, dk_ref, dv_ref, dkm_ref, dvm_ref, dsink_ref,
             dq_s, dkk_s, dvv_s, ck_s, cv_s, mk_s, mv_s):
        b = pl.program_id(0)

        @pl.when(b == 0)
        def _():
            for r in (ck_s, cv_s, mk_s, mv_s, dsink_ref):
                r[...] = jnp.zeros_like(r)

        @pl.when(b < nb)
        def _():
            mask = _attn_mask(b)
            for hk in range(N_KV_HEADS):
                kk = jnp.concatenate([_head(kp_ref, hk), _head(kc_ref, hk), _head(km_ref, hk)], axis=0)
                vv = jnp.concatenate([_head(vp_ref, hk), _head(vc_ref, hk), _head(vm_ref, hk)], axis=0)
                dkk = jnp.zeros((3 * BLOCK, HEAD_DIM), F32)
                dvv = jnp.zeros((3 * BLOCK, HEAD_DIM), F32)
                for g in range(Q_PER_KV):
                    h = hk * Q_PER_KV + g
                    qh = _head(q_ref, h)
                    doh = _head(do_ref, h)
                    p, ps = _attn_probs(qh, kk, mask, sink_ref[h])
                    dp = _dot(doh, vv, "nt")
                    delta = jnp.sum(p * dp, axis=-1, keepdims=True)
                    ds = (p * (dp - delta)).astype(BF16)
                    dsink_ref[h:h + 1, :] += jnp.zeros((1, LANES), F32) - jnp.sum(ps * delta)
                    dq_s[:, h * HEAD_DIM:(h + 1) * HEAD_DIM] = _dot(ds, kk, "nn") * scale
                    dkk = dkk + _dot(ds, qh, "tn") * scale
                    dvv = dvv + _dot(p, doh, "tn")
                dkk_s[:, hk * HEAD_DIM:(hk + 1) * HEAD_DIM] = dkk
                dvv_s[:, hk * HEAD_DIM:(hk + 1) * HEAD_DIM] = dvv
            dq_ref[...] = _rope_t(dq_s[...], tq_ref[0], tq_ref[1], tq_ref[2])
            dk_ref[...] = _rope_t(ck_s[...] + dkk_s[0:BLOCK, :], tk_ref[0], tk_ref[1], tk_ref[2])
            dv_ref[...] = cv_s[...] + dvv_s[0:BLOCK, :]
            ck_s[...] = dkk_s[BLOCK:2 * BLOCK, :]
            cv_s[...] = dvv_s[BLOCK:2 * BLOCK, :]
            mk_s[...] += dkk_s[2 * BLOCK:, :]
            mv_s[...] += dvv_s[2 * BLOCK:, :]

        @pl.when(b == nb)
        def _():
            dk_ref[...] = _rope_t(ck_s[...], tk_ref[0], tk_ref[1], tk_ref[2])
            dv_ref[...] = cv_s[...]
            dkm_ref[...] = _rope_t(mk_s[...], t0_ref[0], t0_ref[1], t0_ref[2])
            dvm_ref[...] = mv_s[...]

    cur = lambda b: (jnp.minimum(b, nb - 1), 0)
    prev = lambda b: (jnp.clip(b - 1, 0, nb - 1), 0)
    first = lambda b: (0, 0)
    kvs = lambda f: pl.BlockSpec((BLOCK, KV_WIDTH), f)
    tab = lambda f: pl.BlockSpec((3, BLOCK, LANES), lambda b: (0,) + f(b)[:1] + (0,))
    kv_out = lambda b: (jnp.maximum(b - 1, 0), 0)
    return _hosted_call(
        body, comm,
        out_shape=[jax.ShapeDtypeStruct((lp, ATTN_WIDTH), F32), jax.ShapeDtypeStruct((lp, KV_WIDTH), F32),
                   jax.ShapeDtypeStruct((lp, KV_WIDTH), F32), jax.ShapeDtypeStruct((BLOCK, KV_WIDTH), F32),
                   jax.ShapeDtypeStruct((BLOCK, KV_WIDTH), F32), jax.ShapeDtypeStruct((N_Q_HEADS, LANES), F32)],
        grid=(nb + 1,),
        in_specs=[pl.BlockSpec(memory_space=pltpu.SMEM), pl.BlockSpec((BLOCK, ATTN_WIDTH), cur),
                  pl.BlockSpec((BLOCK, ATTN_WIDTH), cur),
                  kvs(prev), kvs(cur), kvs(first), kvs(prev), kvs(cur), kvs(first),
                  tab(cur), tab(kv_out), tab(first)],
        out_specs=[pl.BlockSpec((BLOCK, ATTN_WIDTH), cur), kvs(kv_out), kvs(kv_out), kvs(first), kvs(first),
                   pl.BlockSpec((N_Q_HEADS, LANES), first)],
        scratch_shapes=[pltpu.VMEM((BLOCK, ATTN_WIDTH), F32), pltpu.VMEM((3 * BLOCK, KV_WIDTH), F32),
                        pltpu.VMEM((3 * BLOCK, KV_WIDTH), F32), pltpu.VMEM((BLOCK, KV_WIDTH), F32),
                        pltpu.VMEM((BLOCK, KV_WIDTH), F32), pltpu.VMEM((BLOCK, KV_WIDTH), F32),
                        pltpu.VMEM((BLOCK, KV_WIDTH), F32)],
        sem=("arbitrary",), name="attn_bwd", args=(sinks, q, do, k, k, k, v, v, v, tabs, tabs, tabs))


def _cmul(ar, ai, br, bi):
    return ar * br - ai * bi, ar * bi + ai * br


def _cpow(lr, li, n):
    rr = ri = None
    br, bi = lr, li
    while n:
        if n & 1:
            rr, ri = (br, bi) if rr is None else _cmul(rr, ri, br, bi)
        n >>= 1
        if n:
            br, bi = _cmul(br, bi, br, bi)
    return rr, ri


def _shift_rows(x, d, reverse):
    rows = lax.broadcasted_iota(jnp.int32, x.shape, 0)
    if not reverse:
        return jnp.where(rows >= d, pltpu.roll(x, d, 0), 0.0)
    return jnp.where(rows < SUBLANES - d, pltpu.roll(x, SUBLANES - d, 0), 0.0)


def _sublane_powers(mr, mi, reverse):
    rows = lax.broadcasted_iota(jnp.int32, mr.shape, 0)
    e = SUBLANES - 1 - rows if reverse else rows
    pr, pi = jnp.ones_like(mr), jnp.zeros_like(mr)
    br, bi = mr, mi
    for d in (1, 2, 4):
        tr, ti = _cmul(pr, pi, br, bi)
        on = (e & d) != 0
        pr, pi = jnp.where(on, tr, pr), jnp.where(on, ti, pi)
        if d < 4:
            br, bi = _cmul(br, bi, br, bi)
    return pr, pi


def _inclusive_prefix(er, ei, mr, mi, reverse):
    ir, ii, pr, pi = er, ei, mr, mi
    for d in (1, 2, 4):
        tr, ti = _cmul(pr, pi, _shift_rows(ir, d, reverse), _shift_rows(ii, d, reverse))
        ir, ii = ir + tr, ii + ti
        if d < 4:
            pr, pi = _cmul(pr, pi, pr, pi)
    return ir, ii


def _chain_rows(a, t, seg):
    return pl.ds(a * SUBLANES * seg + t, SUBLANES, stride=seg)


def _seg_scan(xr_ref, xi_ref, lam, seg, nchain, reverse, store, init, extra=None):
    nt = len(lam)
    acc0 = () if extra is None else extra[1]

    def step(i, carry):
        hs, acc = carry
        t = seg - 1 - i if reverse else i
        out = []
        for a in range(nchain):
            sl = _chain_rows(a, t, seg)
            for j in range(nt):
                lr, li = lam[j]
                k = 2 * (a * nt + j)
                hr, hi = hs[k], hs[k + 1]
                nr = lr * hr - li * hi + xr_ref[j, sl, :]
                ni = lr * hi + li * hr + xi_ref[j, sl, :]
                if store:
                    xr_ref[j, sl, :] = nr
                    xi_ref[j, sl, :] = ni
                if extra is not None:
                    acc = extra[0](t, a, j, nr, ni, acc)
                out += [nr, ni]
        return tuple(out), acc

    return lax.fori_loop(0, seg, step, (tuple(init), acc0))


def _ssm_scan(xr_ref, xi_ref, lam, seg, nchain, reverse, extra=None):
    nt = len(lam)
    zero = [jnp.zeros((SUBLANES, LANES), F32)] * (2 * nt * nchain)
    ends, _ = _seg_scan(xr_ref, xi_ref, lam, seg, nchain, reverse, False, zero)
    init = [None] * (2 * nt * nchain)
    last = 0 if reverse else SUBLANES - 1
    for j in range(nt):
        mr, mi = _cpow(lam[j][0], lam[j][1], seg)
        m8r, m8i = _cpow(mr, mi, SUBLANES)
        pwr, pwi = _sublane_powers(mr, mi, reverse)
        gr = gi = jnp.zeros((SUBLANES, LANES), F32)
        for a in (reversed(range(nchain)) if reverse else range(nchain)):
            k = 2 * (a * nt + j)
            incr, inci = _inclusive_prefix(ends[k], ends[k + 1], mr, mi, reverse)
            tr, ti = _cmul(pwr, pwi, gr, gi)
            init[k] = _shift_rows(incr, 1, reverse) + tr
            init[k + 1] = _shift_rows(inci, 1, reverse) + ti
            g2r, g2i = _cmul(m8r, m8i, gr, gi)
            gr = g2r + jnp.broadcast_to(incr[last:last + 1, :], gr.shape)
            gi = g2i + jnp.broadcast_to(inci[last:last + 1, :], gi.shape)
    _, acc = _seg_scan(xr_ref, xi_ref, lam, seg, nchain, reverse, True, init, extra)
    return acc


def _diag_mask():
    steps = LANES // SSM_GROUP // GROUPS_PER_TILE
    return (jnp.eye(steps, dtype=F32)[:, None, :, None] * jnp.eye(GROUPS_PER_TILE, dtype=F32)[None, :, None, :])


def _ssm_tables(bb_re, bb_im, c_re, c_im):
    g = bb_re.shape[0]
    nt = g // GROUPS_PER_TILE
    steps = LANES // SSM_GROUP // GROUPS_PER_TILE
    mask = _diag_mask()

    def b_tab(bb):
        x = bb.reshape(nt // steps, steps, GROUPS_PER_TILE, SSM_STATE, SSM_GROUP)
        x = jnp.transpose(x, (0, 1, 4, 2, 3))[:, :, None, None]
        m = jnp.transpose(mask, (0, 2, 3, 1))[None, :, :, :, None, :, None]
        return (x * m).reshape(nt, LANES, TILE_STATES)

    def c_tab(c):
        x = c.reshape(nt // steps, steps, GROUPS_PER_TILE, SSM_GROUP, SSM_STATE)
        x = jnp.transpose(x, (0, 1, 2, 4, 3))[:, :, :, :, None, None]
        m = mask[None, :, :, None, :, :, None]
        return (x * m).reshape(nt, TILE_STATES, LANES)

    return b_tab(bb_re), b_tab(bb_im), c_tab(c_re), c_tab(c_im)


def _ssm_untable_b(db, g):
    nt = g // GROUPS_PER_TILE
    steps = LANES // SSM_GROUP // GROUPS_PER_TILE
    x = db.reshape(nt // steps, steps, GROUPS_PER_TILE, SSM_STATE, steps, GROUPS_PER_TILE, SSM_GROUP)
    m = _diag_mask()[None, :, :, None, :, :, None]
    return jnp.sum(x * m, axis=(4, 5)).reshape(g, SSM_STATE, SSM_GROUP)


def _ssm_untable_c(dc, g):
    nt = g // GROUPS_PER_TILE
    steps = LANES // SSM_GROUP // GROUPS_PER_TILE
    x = dc.reshape(nt // steps, steps, steps, GROUPS_PER_TILE, SSM_GROUP, GROUPS_PER_TILE, SSM_STATE)
    m = jnp.transpose(_diag_mask(), (0, 2, 3, 1))[None, :, :, :, None, :, None]
    out = jnp.sum(x * m, axis=(2, 3))
    return jnp.transpose(out, (0, 1, 3, 2, 4)).reshape(g, SSM_GROUP, SSM_STATE)


def _lam_tiles(lam_ref):
    out = []
    for j in range(TILE_STATES // LANES):
        out.append(jnp.broadcast_to(lam_ref[:, j * LANES:(j + 1) * LANES], (SUBLANES, LANES)))
    return out


def _scan_chains(lp):
    for n in (4, 2, 1):
        if lp % (SUBLANES * n) == 0 and (lp // SUBLANES) % 16 == 0:
            return n
    raise ValueError(lp)


def _split_tiles(dst_ref, rows, val):
    for j in range(val.shape[1] // LANES):
        dst_ref[j, rows, :] = val[:, j * LANES:(j + 1) * LANES]


def _cat_tiles(src_ref, rows):
    njt = src_ref.shape[0]
    return jnp.concatenate([src_ref[j, rows, :] for j in range(njt)], axis=1).astype(BF16)


def _ssm_fwd(u, lam_re, lam_im, tb_re, tb_im, tc_re, tc_im, d_skip, comm=None):
    lp, w = u.shape
    nt = tb_re.shape[0]
    nchain = _scan_chains(lp)
    seg = lp // (SUBLANES * nchain)
    chunk = lp // SUBLANES
    njt = TILE_STATES // LANES

    def body(u_ref, lr_ref, li_ref, br_ref, bi_ref, cr_ref, ci_ref, d_ref, y_ref, yg_ref, xr, xi):
        t = pl.program_id(0)
        for s in range(SUBLANES):
            rs = pl.ds(s * chunk, chunk)
            ub = u_ref[rs, :].astype(BF16)
            _split_tiles(xr, rs, _dot(ub, br_ref[...], "nn"))
            _split_tiles(xi, rs, _dot(ub, bi_ref[...], "nn"))
        lrs, lis = _lam_tiles(lr_ref), _lam_tiles(li_ref)
        _ssm_scan(xr, xi, list(zip(lrs, lis)), seg, nchain, False)
        for s in range(SUBLANES):
            rs = pl.ds(s * chunk, chunk)
            y = _dot(_cat_tiles(xr, rs), cr_ref[...], "nn") - _dot(_cat_tiles(xi, rs), ci_ref[...], "nn")

            @pl.when(t % 2 == 0)
            def _():
                y_ref[rs, :] = y + d_ref[...] * u_ref[rs, :]

            @pl.when(t % 2 == 1)
            def _():
                total = y_ref[rs, :] + y
                y_ref[rs, :] = total
                yg_ref[rs, :] = _gelu(total).astype(BF16)

    blk = pl.BlockSpec((lp, LANES), lambda t: (0, t // 2))
    lam_spec = pl.BlockSpec((None, 1, TILE_STATES), lambda t: (t, 0, 0))
    b_spec = pl.BlockSpec((None, LANES, TILE_STATES), lambda t: (t, 0, 0))
    c_spec = pl.BlockSpec((None, TILE_STATES, LANES), lambda t: (t, 0, 0))
    (y, yg), couts = _hosted_call(
        body, comm, out_shape=[jax.ShapeDtypeStruct((lp, w), F32), jax.ShapeDtypeStruct((lp, w), BF16)], grid=(nt,),
        in_specs=[blk, lam_spec, lam_spec, b_spec, b_spec, c_spec, c_spec,
                  pl.BlockSpec((1, LANES), lambda t: (0, t // 2))],
        out_specs=[blk, blk],
        scratch_shapes=[pltpu.VMEM((njt, lp, LANES), F32), pltpu.VMEM((njt, lp, LANES), F32)],
        sem=("arbitrary",), name="ssm_fwd",
        args=(u, lam_re, lam_im, tb_re, tb_im, tc_re, tc_im, d_skip.reshape(1, w)))
    return (y, yg), couts


def _ssm_bwd(u, dy, lam_re, lam_im, tb_re, tb_im, tc_re, tc_im, d_skip, comm=None):
    lp, w = u.shape
    nt = tb_re.shape[0]
    nchain = _scan_chains(lp)
    seg = lp // (SUBLANES * nchain)
    chunk = lp // SUBLANES
    njt = TILE_STATES // LANES
    tbt_re, tbt_im = jnp.swapaxes(tb_re, 1, 2), jnp.swapaxes(tb_im, 1, 2)
    tct_re, tct_im = jnp.swapaxes(tc_re, 1, 2), jnp.swapaxes(tc_im, 1, 2)

    def body(u_ref, dy_ref, lr_ref, li_ref, br_ref, bi_ref, btr_ref, bti_ref, ctr_ref, cti_ref, d_ref,
             du_ref, dlr_ref, dli_ref, dbr_ref, dbi_ref, dcr_ref, dci_ref, dd_ref, hr, hi, ar, ai):
        t = pl.program_id(0)
        lrs, lis = _lam_tiles(lr_ref), _lam_tiles(li_ref)
        for s in range(SUBLANES):
            rs = pl.ds(s * chunk, chunk)
            ub = u_ref[rs, :].astype(BF16)
            dyb = dy_ref[rs, :].astype(BF16)
            _split_tiles(hr, rs, _dot(ub, br_ref[...], "nn"))
            _split_tiles(hi, rs, _dot(ub, bi_ref[...], "nn"))
            _split_tiles(ar, rs, _dot(dyb, ctr_ref[...], "nn"))
            _split_tiles(ai, rs, -_dot(dyb, cti_ref[...], "nn"))
        _ssm_scan(hr, hi, list(zip(lrs, lis)), seg, nchain, False)

        def dlam_step(tt, a, j, a_r, a_i, acc):
            sl = _chain_rows(a, jnp.maximum(tt - 1, 0), seg)
            p_r, p_i = hr[j, sl, :], hi[j, sl, :]
            acc = list(acc)
            acc[2 * j] = acc[2 * j] + jnp.where(tt > 0, a_r * p_r + a_i * p_i, 0.0)
            acc[2 * j + 1] = acc[2 * j + 1] + jnp.where(tt > 0, a_i * p_r - a_r * p_i, 0.0)
            return tuple(acc)

        zero = tuple([jnp.zeros((SUBLANES, LANES), F32)] * (2 * njt))
        conj = [(lr, -li) for lr, li in zip(lrs, lis)]
        acc = list(_ssm_scan(ar, ai, conj, seg, nchain, True, (dlam_step, zero)))
        row0 = lax.broadcasted_iota(jnp.int32, (SUBLANES, LANES), 0) == 0
        for j in range(njt):
            cs = slice(j * LANES, (j + 1) * LANES)
            for a in range(nchain):
                p_r = _shift_rows(hr[j, _chain_rows(a, seg - 1, seg), :], 1, False)
                p_i = _shift_rows(hi[j, _chain_rows(a, seg - 1, seg), :], 1, False)
                if a > 0:
                    before = pl.ds(a * SUBLANES * seg - 1, 1)
                    p_r = jnp.where(row0, jnp.broadcast_to(hr[j, before, :], p_r.shape), p_r)
                    p_i = jnp.where(row0, jnp.broadcast_to(hi[j, before, :], p_i.shape), p_i)
                a_r, a_i = ar[j, _chain_rows(a, 0, seg), :], ai[j, _chain_rows(a, 0, seg), :]
                acc[2 * j] = acc[2 * j] + a_r * p_r + a_i * p_i
                acc[2 * j + 1] = acc[2 * j + 1] + a_i * p_r - a_r * p_i
            dlr_ref[:, cs] = jnp.sum(acc[2 * j], axis=0, keepdims=True)
            dli_ref[:, cs] = jnp.sum(acc[2 * j + 1], axis=0, keepdims=True)

        dd = jnp.zeros((1, LANES), F32)
        for s in range(SUBLANES):
            rs = pl.ds(s * chunk, chunk)
            ub = u_ref[rs, :].astype(BF16)
            dyv = dy_ref[rs, :]
            dyb = dyv.astype(BF16)
            arb, aib = _cat_tiles(ar, rs), _cat_tiles(ai, rs)
            hrb, hib = _cat_tiles(hr, rs), _cat_tiles(hi, rs)
            du = _dot(arb, btr_ref[...], "nn") + _dot(aib, bti_ref[...], "nn")
            upd = [(dbr_ref, _dot(arb, ub, "tn")), (dbi_ref, _dot(aib, ub, "tn")),
                   (dcr_ref, _dot(dyb, hrb, "tn")), (dci_ref, -_dot(dyb, hib, "tn"))]
            for ref, val in upd:
                if s == 0:
                    ref[...] = val
                else:
                    ref[...] += val
            rows = lax.broadcasted_iota(jnp.int32, (chunk, LANES), 0) + s * chunk
            keep = rows >= PAD_FRONT
            dd = dd + jnp.sum(dyv * u_ref[rs, :], axis=0, keepdims=True)

            @pl.when(t % 2 == 0)
            def _():
                du_ref[rs, :] = jnp.where(keep, du + d_ref[...] * dyv, 0.0)

            @pl.when(t % 2 == 1)
            def _():
                du_ref[rs, :] += jnp.where(keep, du, 0.0)

        @pl.when(t % 2 == 0)
        def _():
            dd_ref[...] = dd

    blk = pl.BlockSpec((lp, LANES), lambda t: (0, t // 2))
    vec = pl.BlockSpec((1, LANES), lambda t: (0, t // 2))
    lam_spec = pl.BlockSpec((None, 1, TILE_STATES), lambda t: (t, 0, 0))
    b_spec = pl.BlockSpec((None, LANES, TILE_STATES), lambda t: (t, 0, 0))
    c_spec = pl.BlockSpec((None, TILE_STATES, LANES), lambda t: (t, 0, 0))
    lam_shape = jax.ShapeDtypeStruct((nt, 1, TILE_STATES), F32)
    bt_shape = jax.ShapeDtypeStruct((nt, TILE_STATES, LANES), F32)
    ct_shape = jax.ShapeDtypeStruct((nt, LANES, TILE_STATES), F32)
    st = pltpu.VMEM((njt, lp, LANES), F32)
    return _hosted_call(
        body, comm,
        out_shape=[jax.ShapeDtypeStruct((lp, w), F32), lam_shape, lam_shape, bt_shape, bt_shape, ct_shape, ct_shape,
                   jax.ShapeDtypeStruct((1, w), F32)],
        grid=(nt,),
        in_specs=[blk, blk, lam_spec, lam_spec, b_spec, b_spec, c_spec, c_spec, b_spec, b_spec, vec],
        out_specs=[blk, lam_spec, lam_spec, c_spec, c_spec, b_spec, b_spec, vec],
        scratch_shapes=[st, st, st, st], sem=("arbitrary",), name="ssm_bwd",
        args=(u, dy, lam_re, lam_im, tb_re, tb_im, tbt_re, tbt_im, tct_re, tct_im, d_skip.reshape(1, w)))


def _ssm_params(a_re, a_im, log_dt, b_re, b_im):
    dt = jnp.exp(log_dt)[:, None]
    mag = jnp.exp(a_re * dt)
    lb_re = mag * jnp.cos(a_im * dt)
    lb_im = mag * jnp.sin(a_im * dt)
    den = a_re * a_re + a_im * a_im
    num_re = lb_re - 1.0
    coef_re = (num_re * a_re + lb_im * a_im) / den
    coef_im = (lb_im * a_re - num_re * a_im) / den
    bb_re = coef_re[..., None] * b_re - coef_im[..., None] * b_im
    bb_im = coef_re[..., None] * b_im + coef_im[..., None] * b_re
    return lb_re, lb_im, bb_re, bb_im


def _merge_fwd(o, yg, ga, gs, w3t, comm=None):
    lp, d = ga.shape
    kw = w3t.shape[2]
    tm = _row_tile(lp)

    def epi(accs, in_refs, out_refs):
        attn, vv, gg = accs
        out_refs[0][...] = (_sigmoid(in_refs[5][...]) * attn
                            + _sigmoid(in_refs[6][...]) * (vv * _sigmoid(gg))).astype(BF16)

    wspec = lambda which: pl.BlockSpec((None, d, kw), lambda i: (which, 0, 0))
    rowspec = pl.BlockSpec((tm, d), lambda i: (i, 0))
    aspec = pl.BlockSpec((tm, kw), lambda i: (i, 0))
    res = _matmul(
        "merge_fwd", (lp // tm,), None, [o, yg, w3t, w3t, w3t, ga, gs],
        [aspec, aspec, wspec(0), wspec(1), wspec(2), rowspec, rowspec],
        [(0, 2, "nt", 0), (1, 3, "nt", 1), (1, 4, "nt", 2)], [(tm, d)] * 3, epi,
        [jax.ShapeDtypeStruct((lp, d), BF16)], [rowspec], ("parallel",), comm)
    return (res[0], []) if comm is None else (res[0][0], res[1])


def _out_proj(merged, w_out, h, next_gain, comm=None):
    lp, d = h.shape
    tm = _row_tile(lp)
    shapes, specs, extra_in, extra_specs = _residual_specs(lp, d, tm, next_gain)

    def epi(accs, in_refs, out_refs):
        _residual_outputs(in_refs[2][...] + accs[0], in_refs, out_refs, 3 if extra_in else None)

    rowspec = pl.BlockSpec((tm, d), lambda i: (i, 0))
    res = _matmul(
        "mix_out_proj", (lp // tm,), None, [merged, w_out, h] + extra_in,
        [rowspec, pl.BlockSpec((d, d), lambda i: (0, 0)), rowspec] + extra_specs,
        [(0, 1, "nn", 0)], [(tm, d)], epi, shapes, specs, ("parallel",), comm)
    return (res, []) if comm is None else res


def _merge_bwd(dhb, w_out, o, yg, ga, gs, w3t):
    lp, d = ga.shape
    kw = w3t.shape[2]
    tm = _row_tile(lp)

    def epi(accs, in_refs, out_refs):
        dm, attn, vv, gg = accs
        sa = _sigmoid(in_refs[7][...])
        ss = _sigmoid(in_refs[8][...])
        sg = _sigmoid(gg)
        ssm = vv * sg
        dssm = dm * ss
        out_refs[0][...] = (dm * sa).astype(BF16)
        out_refs[1][...] = (dssm * sg).astype(BF16)
        out_refs[2][...] = (dssm * vv * sg * (1.0 - sg)).astype(BF16)
        out_refs[3][...] = (dm * attn * sa * (1.0 - sa)).astype(BF16)
        out_refs[4][...] = (dm * ssm * ss * (1.0 - ss)).astype(BF16)

    wspec = lambda which: pl.BlockSpec((None, d, kw), lambda i: (which, 0, 0))
    rowspec = pl.BlockSpec((tm, d), lambda i: (i, 0))
    aspec = pl.BlockSpec((tm, kw), lambda i: (i, 0))
    shp = jax.ShapeDtypeStruct((lp, d), BF16)
    return _matmul(
        "merge_bwd", (lp // tm,), None, [dhb, w_out, o, yg, w3t, w3t, w3t, ga, gs],
        [rowspec, pl.BlockSpec((d, d), lambda i: (0, 0)), aspec, aspec, wspec(0), wspec(1), wspec(2), rowspec,
         rowspec],
        [(0, 1, "nt", 0), (2, 4, "nt", 1), (3, 5, "nt", 2), (3, 6, "nt", 3)], [(tm, d)] * 4, epi,
        [shp] * 5, [rowspec] * 5, ("parallel",))


def _branch_bwd(dattn, dv, dg, w3t, y):
    lp, d = dattn.shape
    kw = w3t.shape[2]
    tm = _row_tile(lp)

    def epi(accs, in_refs, out_refs):
        out_refs[0][...] = accs[0].astype(BF16)
        out_refs[1][...] = accs[1] * _gelu_grad(in_refs[6][...])

    wspec = lambda which: pl.BlockSpec((None, d, kw), lambda i: (which, 0, 0))
    rowspec = pl.BlockSpec((tm, d), lambda i: (i, 0))
    aspec = pl.BlockSpec((tm, kw), lambda i: (i, 0))
    return _matmul(
        "branch_bwd", (lp // tm,), None, [dattn, dv, dg, w3t, w3t, w3t, y],
        [rowspec, rowspec, rowspec, wspec(0), wspec(1), wspec(2), aspec],
        [(0, 3, "nn", 0), (1, 4, "nn", 1), (2, 5, "nn", 1)], [(tm, kw)] * 2, epi,
        [jax.ShapeDtypeStruct((lp, kw), BF16), jax.ShapeDtypeStruct((lp, kw), F32)], [aspec, aspec],
        ("parallel",))


def _sibling_half(acc, rows):
    half = rows // 2
    return jnp.where(lax.axis_index("c") == 0, acc[half:rows], acc[:half]).astype(BF16)


def _tn_cols(x, ys, name):
    lp, kx = x.shape
    n = ys[0].shape[1]
    n4 = n // N_CHIPS
    tk = _tn_tiles(lp)
    ny = len(ys)

    def epi(accs, in_refs, out_refs):
        for i, acc in enumerate(accs):
            out_refs[i][...] = acc
            out_refs[ny + i][...] = _sibling_half(acc, kx)

    shp = jax.ShapeDtypeStruct((N_CHIPS, kx, n4), F32)
    shp_half = jax.ShapeDtypeStruct((N_CHIPS, kx // 2, n4), BF16)
    res = _matmul(
        name, (N_CHIPS, lp // tk), 1, [x] + list(ys),
        [pl.BlockSpec((tk, kx), lambda j, k: (k, 0))] + [pl.BlockSpec((tk, n4), lambda j, k: (k, j))] * ny,
        [(0, 1 + i, "tn", i) for i in range(ny)], [(kx, n4)] * ny, epi,
        [shp] * ny + [shp_half] * ny,
        [pl.BlockSpec((None, kx, n4), lambda j, k: (j, 0, 0))] * ny
        + [pl.BlockSpec((None, kx // 2, n4), lambda j, k: (j, 0, 0))] * ny,
        ("arbitrary", "arbitrary"))
    return res[:ny], res[ny:]


def _tn_full(x, y, name, tn_cols=None):
    lp, kx = x.shape
    n = y.shape[1]
    tk = _tn_tiles(lp)
    tn = n if tn_cols is None else tn_cols
    k4 = kx // N_CHIPS

    def epi(accs, in_refs, out_refs):
        for j in range(N_CHIPS):
            slab = accs[0][j * k4:(j + 1) * k4]
            out_refs[0][j] = slab
            out_refs[1][j] = _sibling_half(slab, k4)

    return _matmul(
        name, (n // tn, lp // tk), 1, [x, y],
        [pl.BlockSpec((tk, kx), lambda j, k: (k, 0)), pl.BlockSpec((tk, tn), lambda j, k: (k, j))],
        [(0, 1, "tn", 0)], [(kx, tn)], epi,
        [jax.ShapeDtypeStruct((N_CHIPS, k4, n), F32), jax.ShapeDtypeStruct((N_CHIPS, k4 // 2, n), BF16)],
        [pl.BlockSpec((N_CHIPS, k4, tn), lambda j, k: (0, 0, j)),
         pl.BlockSpec((N_CHIPS, k4 // 2, tn), lambda j, k: (0, 0, j))],
        ("arbitrary", "arbitrary"))


def _in_proj_bwd(dz, w_in, dh, h_in, gain):
    lp, d = h_in.shape
    inw = w_in.shape[0]
    tm = _row_tile(lp)

    def epi(accs, in_refs, out_refs):
        i = pl.program_id(0)
        dx, dgrow = _rms_bwd_math(accs[0], in_refs[3][...], in_refs[4][...])
        dh_new = in_refs[2][...] + dx
        out_refs[0][...] = dh_new
        out_refs[2][...] = dh_new.astype(BF16)

        @pl.when(i == 0)
        def _():
            out_refs[1][...] = jnp.zeros_like(out_refs[1])

        out_refs[1][...] += jnp.sum(dgrow, axis=0, keepdims=True)

    row = pl.BlockSpec((tm, d), lambda i: (i, 0))
    return _matmul(
        "mix_in_proj_bwd", (lp // tm,), None, [dz, w_in, dh, h_in, gain.reshape(1, d)],
        [pl.BlockSpec((tm, inw), lambda i: (i, 0)), pl.BlockSpec((inw, d), lambda i: (0, 0)), row, row,
         pl.BlockSpec((1, d), lambda i: (0, 0))],
        [(0, 1, "nn", 0)], [(tm, d)], epi,
        [jax.ShapeDtypeStruct((lp, d), F32), jax.ShapeDtypeStruct((1, d), F32), jax.ShapeDtypeStruct((lp, d), BF16)],
        [row, pl.BlockSpec((1, d), lambda i: (0, 0)), row], ("arbitrary",))


def _loss_head(h, gain, target):
    lp, d = h.shape
    nb = lp // BLOCK

    def body(h_ref, g_ref, t_ref, dh_ref, dg_ref, loss_ref, dhb_ref):
        i = pl.program_id(0)

        @pl.when(i == 0)
        def _():
            dg_ref[...] = jnp.zeros_like(dg_ref)
            loss_ref[...] = jnp.zeros_like(loss_ref)
            dh_ref[...] = jnp.zeros_like(dh_ref)
            dhb_ref[...] = jnp.zeros_like(dhb_ref)

        @pl.when(i > 0)
        def _():
            x = h_ref[...]
            g = g_ref[...]
            r = lax.rsqrt(jnp.mean(x * x, axis=-1, keepdims=True) + EPS)
            err = x * r * g - t_ref[...]
            loss_ref[...] += jnp.zeros_like(loss_ref) + 0.5 * jnp.sum(jnp.sum(err * err, axis=-1, keepdims=True)) / d
            dx, dgrow = _rms_bwd_math(err * (1.0 / d), x, g)
            dh_ref[...] = dx
            dhb_ref[...] = dx.astype(BF16)
            dg_ref[...] += jnp.sum(dgrow, axis=0, keepdims=True)

    row = pl.BlockSpec((BLOCK, d), lambda i: (i, 0))
    one = pl.BlockSpec((1, d), lambda i: (0, 0))
    return pl.pallas_call(
        body,
        out_shape=[jax.ShapeDtypeStruct((lp, d), F32), jax.ShapeDtypeStruct((1, d), F32),
                   jax.ShapeDtypeStruct((SUBLANES, LANES), F32), jax.ShapeDtypeStruct((lp, d), BF16)],
        grid=(nb,),
        in_specs=[row, one, pl.BlockSpec((BLOCK, d), lambda i: (jnp.maximum(i - 1, 0), 0))],
        out_specs=[row, one, pl.BlockSpec((SUBLANES, LANES), lambda i: (0, 0)), row],
        compiler_params=_cparams(("arbitrary",)), name="loss_head")(h, gain.reshape(1, d), target)


def _adam_math(w, g, m, v):
    m = ADAM_B1 * m + (1.0 - ADAM_B1) * g
    v = ADAM_B2 * v + (1.0 - ADAM_B2) * (g * g)
    m_hat = m / (1.0 - ADAM_B1 ** ADAM_STEP)
    v_hat = v / (1.0 - ADAM_B2 ** ADAM_STEP)
    delta = -ADAM_LR * (m_hat / (jnp.sqrt(v_hat) + ADAM_EPS) + ADAM_WD * w)
    return delta, m, v


def _adamw_layers(w, m, v, mine, other, pos, name):
    depth, r, c = w.shape
    half = r // 2
    tr = _div_tile(half, c * 4)
    nh = half // tr

    def body(*refs):
        pos_ref, w_ref, m_ref, v_ref = refs[:4]
        mine_refs = refs[4:4 + depth]
        other_refs = refs[4 + depth:4 + 2 * depth]
        g_out, d_out, m_out, v_out = refs[4 + 2 * depth:]
        layer, i = pl.program_id(0), pl.program_id(1)
        is_mine = (i // nh) == pos_ref[0]

        def update(g):
            delta, nm, nv = _adam_math(w_ref[...], g, m_ref[...], v_ref[...])
            g_out[...] = g
            d_out[...] = delta
            m_out[...] = nm
            v_out[...] = nv

        for l in range(depth):
            @pl.when((layer == l) & is_mine)
            def _(l=l):
                update(mine_refs[l][...])

            @pl.when((layer == l) & jnp.logical_not(is_mine))
            def _(l=l):
                update(other_refs[l][...])

    stacked = pl.BlockSpec((None, tr, c), lambda l, i, p: (l, i, 0))

    def gspec(layer, is_other):
        def imap(l, i, p):
            first = jnp.where(is_other, 1 - p[0], p[0]) * nh
            here = jnp.clip(i - first, 0, nh - 1)
            return (jnp.where(l == layer, here, jnp.where(l < layer, 0, nh - 1)), 0)
        return pl.BlockSpec((tr, c), imap)

    shp = jax.ShapeDtypeStruct((depth, r, c), F32)
    grid_spec = pltpu.PrefetchScalarGridSpec(
        num_scalar_prefetch=1, grid=(depth, 2 * nh),
        in_specs=[stacked] * 3 + [gspec(l, 0) for l in range(depth)] + [gspec(l, 1) for l in range(depth)],
        out_specs=[stacked] * 4)
    return pl.pallas_call(
        body, out_shape=[shp] * 4, grid_spec=grid_spec,
        compiler_params=_cparams(("arbitrary", "arbitrary")), name=name)(pos, w, m, v, *mine, *other)


def _adamw_whole(w, g, m, v, name):
    def body(w_ref, g_ref, m_ref, v_ref, d_out, m_out, v_out):
        delta, nm, nv = _adam_math(w_ref[...], g_ref[...], m_ref[...], v_ref[...])
        d_out[...] = delta
        m_out[...] = nm
        v_out[...] = nv

    shp = jax.ShapeDtypeStruct(w.shape, F32)
    return pl.pallas_call(body, out_shape=[shp] * 3, compiler_params=_cparams(), name=name)(w, g, m, v)


def _mesh_pos():
    return lax.axis_index("x"), lax.axis_index("y"), lax.axis_index("c")


def _row_half(ref, which, lead):
    half = ref.shape[lead] // 2
    idx = (slice(None),) * lead + (pl.ds(which * half, half), slice(None))
    return ref.at[idx]


def _gather_comm(arrs, tag):
    n = len(arrs)

    def ctx(ins, outs, sems):
        send_sems, recv_sems, local_sems = sems
        x, y, c = _mesh_pos()
        chips = [(1 - x, y), (x, 1 - y), (1 - x, 1 - y)]

        def slot(k, chip, which):
            lead = len(ins[k].shape) - 2
            return _row_half(outs[k].at[2 * chip[0] + chip[1]], which, lead)

        def copy(k, j, src, dst, to):
            return pltpu.make_async_remote_copy(
                src_ref=src, dst_ref=dst, send_sem=send_sems.at[6 * k + j], recv_sem=recv_sems.at[6 * k + j],
                device_id=to, device_id_type=MESH)

        def local(k):
            return pltpu.make_async_copy(ins[k], outs[k].at[2 * x + y], local_sems.at[k])

        def first(k, j):
            lead = len(ins[k].shape) - 2
            return copy(k, j, _row_half(ins[k], c, lead), slot(k, (x, y), c), (*chips[j], c))

        def passed(k, j, which):
            return copy(k, 3 + j, slot(k, chips[j], which), slot(k, chips[j], which), (x, y, 1 - c))

        def landed(k, j):
            return copy(k, j, slot(k, chips[j], c), slot(k, chips[j], c), (x, y, 1 - c))

        return c, local, first, passed, landed

    def start(ins, outs, sems):
        c, local, first, passed, landed = ctx(ins, outs, sems)
        for k in range(n):
            local(k).start()
            for j in range(3):
                first(k, j).start()

    def mid(ins, outs, sems):
        c, local, first, passed, landed = ctx(ins, outs, sems)
        for j in range(3):
            for k in range(n):
                landed(k, j).wait_recv()
                passed(k, j, c).start()

    def finish(ins, outs, sems):
        c, local, first, passed, landed = ctx(ins, outs, sems)
        for j in range(3):
            for k in range(n):
                passed(k, j, 1 - c).wait_recv()
        for k in range(n):
            for j in range(3):
                first(k, j).wait_send()
                passed(k, j, c).wait_send()
            local(k).wait()

    return _Comm(
        tag, arrs, [jax.ShapeDtypeStruct((N_CHIPS,) + a.shape, a.dtype) for a in arrs],
        [pltpu.SemaphoreType.DMA((6 * n,)), pltpu.SemaphoreType.DMA((6 * n,)), pltpu.SemaphoreType.DMA((n,))],
        start, mid, finish)


def _run_comm(comm, name):
    n_in, n_out = len(comm.ins), len(comm.out_shapes)

    def body(*refs):
        ins, outs, sems = refs[:n_in], refs[n_in:n_in + n_out], refs[n_in + n_out:]
        comm.start(ins, outs, sems)
        if comm.mid is not None:
            comm.mid(ins, outs, sems)
        comm.finish(ins, outs, sems)

    return pl.pallas_call(
        body, out_shape=comm.out_shapes, in_specs=[HBM_SPEC] * n_in, out_specs=[HBM_SPEC] * n_out,
        scratch_shapes=comm.sems, name=name)(*comm.ins)


def _all_gather_chips(arrs, name):
    return _run_comm(_gather_comm(arrs, "gather"), name)


GATHER_US_PER_BYTE = 380.0 / 11.65e6
HOST_US = dict(ffn_up=68.0, ffn_down=37.0, in_proj=38.0, attn_fwd=103.0, ssm_fwd=70.0, merge_fwd=30.0,
               out_proj=23.0)
HOST_SLACK_US = 10.0


class _WeightStream:
    def __init__(self, pieces):
        self.keys = [k for k, _ in pieces]
        self.shards = dict(pieces)
        self.next = 0
        self.full = {}
        self.pending = []

    def comm_for(self, host):
        budget = HOST_US[host] + HOST_SLACK_US
        taken, cost = [], 0.0
        while self.next < len(self.keys):
            key = self.keys[self.next]
            c = self.shards[key].size * self.shards[key].dtype.itemsize * GATHER_US_PER_BYTE
            if cost + c > budget and taken:
                break
            taken.append(key)
            cost += c
            self.next += 1
        self.pending = taken
        if not taken:
            return None
        return _gather_comm([self.shards[k] for k in taken], "g_" + "_".join(k[1] for k in taken))

    def deposit(self, gathered):
        for key, arr in zip(self.pending, gathered):
            self.full[key] = arr
        self.pending = []

    def get(self, key):
        if key not in self.full:
            upto = self.keys.index(key) + 1
            keys = self.keys[self.next:upto]
            self.next = upto
            for k, arr in zip(keys, _all_gather_chips([self.shards[k] for k in keys], "gather_now")):
                self.full[k] = arr
        return self.full[key]


def _all_gather_devices(x_shard, name):
    m_per, ncol = x_shard.shape

    def body(x_ref, out_ref, send_sems, recv_sems, local_sem):
        x, y, c = _mesh_pos()
        me, sibling = (x, y, c), (x, y, 1 - c)
        chips = [(1 - x, y), (x, 1 - y), (1 - x, 1 - y)]

        def rows(px, py, pc):
            return out_ref.at[4 * px + 2 * py + pc]

        def copy(k, block, to, src=None):
            return pltpu.make_async_remote_copy(
                src_ref=rows(*block) if src is None else src, dst_ref=rows(*block),
                send_sem=send_sems.at[k], recv_sem=recv_sems.at[k], device_id=to, device_id_type=MESH)

        mine = pltpu.make_async_copy(x_ref, rows(*me), local_sem)
        mine.start()
        first = [copy(0, me, sibling, src=x_ref)]
        first += [copy(1 + j, me, (*chip, c), src=x_ref) for j, chip in enumerate(chips)]
        for cp in first:
            cp.start()
        passed = [copy(4 + j, (*chip, c), sibling) for j, chip in enumerate(chips)]
        for j, chip in enumerate(chips):
            copy(1 + j, (*chip, c), me).wait_recv()
            passed[j].start()
        copy(0, sibling, me).wait_recv()
        for j, chip in enumerate(chips):
            copy(4 + j, (*chip, 1 - c), me).wait_recv()
        for cp in first + passed:
            cp.wait_send()
        mine.wait()

    return pl.pallas_call(
        body, out_shape=jax.ShapeDtypeStruct((8, m_per, ncol), x_shard.dtype),
        in_specs=[pl.BlockSpec(memory_space=pltpu.VMEM)], out_specs=pl.BlockSpec(memory_space=pltpu.VMEM),
        scratch_shapes=[pltpu.SemaphoreType.DMA((7,)), pltpu.SemaphoreType.DMA((7,)), pltpu.SemaphoreType.DMA],
        compiler_params=pltpu.CompilerParams(vmem_limit_bytes=VMEM_LIMIT), name=name)(x_shard)


def _device_gather_comm(x_shard, tag):
    def ctx(ins, outs, sems):
        (x_ref,), (out_ref,) = ins, outs
        send_sems, recv_sems, local_sems = sems
        x, y, c = _mesh_pos()
        me, sibling = (x, y, c), (x, y, 1 - c)
        chips = [(1 - x, y), (x, 1 - y), (1 - x, 1 - y)]

        def rows(px, py, pc):
            return out_ref.at[4 * px + 2 * py + pc]

        def copy(k, block, to, src=None):
            return pltpu.make_async_remote_copy(
                src_ref=rows(*block) if src is None else src, dst_ref=rows(*block),
                send_sem=send_sems.at[k], recv_sem=recv_sems.at[k], device_id=to, device_id_type=MESH)

        mine = pltpu.make_async_copy(x_ref, rows(*me), local_sems.at[0])
        first = [copy(0, me, sibling, src=x_ref)] + [copy(1 + j, me, (*chip, c), src=x_ref)
                                                     for j, chip in enumerate(chips)]
        passed = [copy(4 + j, (*chip, c), sibling) for j, chip in enumerate(chips)]
        landed = [copy(1 + j, (*chip, c), me) for j, chip in enumerate(chips)]
        last = [copy(0, sibling, me)] + [copy(4 + j, (*chip, 1 - c), me) for j, chip in enumerate(chips)]
        return mine, first, passed, landed, last

    def start(ins, outs, sems):
        mine, first, _, _, _ = ctx(ins, outs, sems)
        mine.start()
        for cp in first:
            cp.start()

    def mid(ins, outs, sems):
        _, _, passed, landed, _ = ctx(ins, outs, sems)
        for cp, fwd in zip(landed, passed):
            cp.wait_recv()
            fwd.start()

    def finish(ins, outs, sems):
        mine, first, passed, _, last = ctx(ins, outs, sems)
        for cp in last:
            cp.wait_recv()
        for cp in first + passed:
            cp.wait_send()
        mine.wait()

    return _Comm(
        tag, [x_shard], [jax.ShapeDtypeStruct((8,) + x_shard.shape, x_shard.dtype)],
        [pltpu.SemaphoreType.DMA((7,)), pltpu.SemaphoreType.DMA((7,)), pltpu.SemaphoreType.DMA((1,))],
        start, mid, finish)


def _sum_devices(g8, name):
    _, r, c = g8.shape
    tr = _div_tile(r, c * 4 * 8)

    def body(g_ref, o_ref):
        acc = g_ref[0]
        for dev in range(1, 8):
            acc = acc + g_ref[dev]
        o_ref[...] = acc

    return pl.pallas_call(
        body, out_shape=jax.ShapeDtypeStruct((r, c), F32), grid=(r // tr,),
        in_specs=[pl.BlockSpec((8, tr, c), lambda i: (0, i, 0))], out_specs=pl.BlockSpec((tr, c), lambda i: (i, 0)),
        compiler_params=_cparams(("parallel",)), name=name)(g8)


def _chip_partials(arrs, recvs, pos, name):
    n = len(arrs)

    def body(pos_ref, *refs):
        for a_ref, b_ref, o_ref in zip(refs[:n], refs[n:2 * n], refs[2 * n:]):
            o_ref[...] = (a_ref[...] + b_ref[...]).astype(BF16)

    own_specs, recv_specs, shapes = [], [], []
    for arr in arrs:
        nslab, r, c = arr.shape
        own_specs.append(pl.BlockSpec((None, r // 2, c), lambda j, p: (j, p[0], 0)))
        recv_specs.append(pl.BlockSpec((None, r // 2, c), lambda j, p: (j, 0, 0)))
        shapes.append(jax.ShapeDtypeStruct((nslab, r // 2, c), BF16))
    grid_spec = pltpu.PrefetchScalarGridSpec(
        num_scalar_prefetch=1, grid=(N_CHIPS,), in_specs=own_specs + recv_specs, out_specs=recv_specs)
    return pl.pallas_call(
        body, out_shape=shapes, grid_spec=grid_spec,
        compiler_params=_cparams(("parallel",)), name=name)(pos, *arrs, *recvs)


def _chip_exchange_comm(parts, tag):
    n = len(parts)

    def copies(ins, outs, sems):
        send_sems, recv_sems = sems
        x, y, c = _mesh_pos()
        chips = [(1 - x, y), (x, 1 - y), (1 - x, 1 - y)]
        return [pltpu.make_async_remote_copy(
            src_ref=ins[k].at[2 * chip[0] + chip[1]], dst_ref=outs[k].at[j],
            send_sem=send_sems.at[3 * k + j], recv_sem=recv_sems.at[3 * k + j],
            device_id=(*chip, c), device_id_type=MESH) for k in range(n) for j, chip in enumerate(chips)]

    def start(ins, outs, sems):
        for cp in copies(ins, outs, sems):
            cp.start()

    def finish(ins, outs, sems):
        for cp in copies(ins, outs, sems):
            cp.wait()

    return _Comm(
        tag, parts, [jax.ShapeDtypeStruct((3,) + p.shape[1:], p.dtype) for p in parts],
        [pltpu.SemaphoreType.DMA((3 * n,)), pltpu.SemaphoreType.DMA((3 * n,))], start, None, finish)


def _reduce_halves(arrs, recvs, gots, pos, name):
    n = len(arrs)

    def body(pos_ref, *refs):
        for a_ref, b_ref, g_ref, o_ref in zip(refs[:n], refs[n:2 * n], refs[2 * n:3 * n], refs[3 * n:]):
            acc = a_ref[...] + b_ref[...]
            for j in range(3):
                acc = acc + g_ref[j].astype(F32)
            o_ref[...] = acc

    own_specs, recv_specs, got_specs, out_specs, shapes = [], [], [], [], []
    for arr in arrs:
        _, r, c = arr.shape
        own_specs.append(pl.BlockSpec((None, r // 2, c), lambda i, p: (p[1], p[0], 0)))
        recv_specs.append(pl.BlockSpec((None, r // 2, c), lambda i, p: (p[1], 0, 0)))
        got_specs.append(pl.BlockSpec((3, r // 2, c), lambda i, p: (0, 0, 0)))
        out_specs.append(pl.BlockSpec((r // 2, c), lambda i, p: (0, 0)))
        shapes.append(jax.ShapeDtypeStruct((r // 2, c), F32))
    grid_spec = pltpu.PrefetchScalarGridSpec(
        num_scalar_prefetch=1, grid=(1,), in_specs=own_specs + recv_specs + got_specs, out_specs=out_specs)
    return pl.pallas_call(
        body, out_shape=shapes, grid_spec=grid_spec,
        compiler_params=_cparams(("arbitrary",)), name=name)(pos, *arrs, *recvs, *gots)


def _share_halves(halves, name):
    n = len(halves)

    def body(*refs):
        ins, outs = refs[:n], refs[n:2 * n]
        send_sems, recv_sems = refs[2 * n:]
        x, y, c = _mesh_pos()
        cps = []
        for k in range(n):
            cp = pltpu.make_async_remote_copy(
                src_ref=ins[k], dst_ref=outs[k], send_sem=send_sems.at[k], recv_sem=recv_sems.at[k],
                device_id=(x, y, 1 - c), device_id_type=MESH)
            cp.start()
            cps.append(cp)
        for cp in cps:
            cp.wait()

    return pl.pallas_call(
        body, out_shape=[jax.ShapeDtypeStruct(h.shape, h.dtype) for h in halves],
        in_specs=[HBM_SPEC] * n, out_specs=[HBM_SPEC] * n,
        scratch_shapes=[pltpu.SemaphoreType.DMA((n,)), pltpu.SemaphoreType.DMA((n,))], name=name)(*halves)


class _Reduction:
    def __init__(self, arrs, others, pos, tag):
        self.arrs, self.pos, self.tag = arrs, pos, tag
        self.recv = _share_halves(others, "rs_sibling_" + tag)
        self.parts = _chip_partials(arrs, self.recv, pos, "rs_partial_" + tag)
        self.got = None

    def comm(self):
        return _chip_exchange_comm(self.parts, "rs_" + self.tag)

    def end(self):
        if self.got is None:
            self.got = _run_comm(self.comm(), "rs_chips_" + self.tag)
        return _reduce_halves(self.arrs, self.recv, self.got, self.pos, "rs_reduce_" + self.tag)


def _w_in_full(p, l, ws):
    slabs = ws.get((l, "w_in"))
    return slabs.reshape(-1, slabs.shape[2])


def _w3t_full(p, l, ws):
    if "w3t" not in p:
        slabs = ws.get((l, "w3"))
        p["w3t"] = jnp.swapaxes(slabs, 0, 1).reshape(slabs.shape[1], -1, slabs.shape[3])
    return p["w3t"]


def _w_out_full(l, ws):
    slabs = ws.get((l, "w_out"))
    return slabs.reshape(-1, slabs.shape[2])


def _layer_fwd(h, n0, l, p, next_gain, ws, tabs):
    def hosted(host, fn, *args):
        out, got = fn(*args, ws.comm_for(host))
        ws.deposit(got)
        return out

    ffn1_saved = hosted("ffn_up", _ffn_up, n0, ws.get((l, "wg1")), ws.get((l, "wu1")))
    h1, n = hosted("ffn_down", _ffn_down, ffn1_saved[2], ws.get((l, "wd1")), h, p["mix_norm"])
    ssm_w = p["ssm_d"].shape[0]
    q, k, v, u, ga, gs = hosted("in_proj", _in_proj, n, _w_in_full(p, l, ws), tabs, ssm_w)
    o = hosted("attn_fwd", _attn_fwd, q, k, v, p["attn_sinks"])
    y, yg = hosted("ssm_fwd", _ssm_fwd, u, *p["ssm_tabs"], p["ssm_d"])
    merged = hosted("merge_fwd", _merge_fwd, o, yg, ga, gs, _w3t_full(p, l, ws))
    h2, n2 = hosted("out_proj", _out_proj, merged, _w_out_full(l, ws), h1, p["ffn2_norm"])
    ffn2_saved = hosted("ffn_up", _ffn_up, n2, ws.get((l, "wg2")), ws.get((l, "wu2")))
    h3, *n3 = hosted("ffn_down", _ffn_down, ffn2_saved[2], ws.get((l, "wd2")), h2, next_gain)
    saved = dict(h0=h, h1=h1, h2=h2, ffn1=ffn1_saved, ffn2=ffn2_saved, n_mix=n, q=q, k=k, v=v, u=u, ga=ga, gs=gs,
                 o=o, y=y, yg=yg, merged=merged)
    return h3, (n3[0] if n3 else None), saved


def _layer_bwd(dh_pair, l, p, ws, s, tabs, pos, early_comm=None):
    g = {}
    (dh2, dhb), g["ffn2_norm"], red_ffn2, _, _ = _ffn_bwd(
        dh_pair, s["h2"], p["ffn2_norm"], ws.get((l, "wg2")), ws.get((l, "wu2")), ws.get((l, "wd2")), p["f4"],
        s["ffn2"], pos)
    w3, w_out_w = _w3t_full(p, l, ws), _w_out_full(l, ws)
    lp, d = dh2.shape
    d4 = d // N_CHIPS
    dw_out, dw_out_other = _tn_full(s["merged"], dhb, "mix_dw_out")
    dattn, dv, dg, dga, dgs = _merge_bwd(dhb, w_out_w, s["o"], s["yg"], s["ga"], s["gs"], w3)
    (dw_ap,), (dw_ap_other,) = _tn_cols(s["o"], [dattn], "mix_dw_ap")
    (dw_gv, dw_gg), (dw_gv_other, dw_gg_other) = _tn_cols(s["yg"], [dv, dg], "mix_dw_glu")
    do, dy = _branch_bwd(dattn, dv, dg, w3, s["y"])
    (dq, dk, dvv, dkm, dvm, dsink), _ = _attn_bwd(s["q"], s["k"], s["v"], do, p["attn_sinks"], tabs)
    g["attn_sinks"] = dsink[:, 0]
    (du, dlr, dli, dbr, dbi, dcr, dci, dd), _ = _ssm_bwd(s["u"], dy, *p["ssm_tabs"], p["ssm_d"])
    ngrp = p["ssm_d"].shape[0] // SSM_GROUP
    g["ssm_lam"] = (dlr.reshape(ngrp, SSM_STATE), dli.reshape(ngrp, SSM_STATE),
                    _ssm_untable_b(dbr, ngrp), _ssm_untable_b(dbi, ngrp))
    g["ssm_c_re"] = _ssm_untable_c(dcr, ngrp)
    g["ssm_c_im"] = _ssm_untable_c(dci, ngrp)
    g["ssm_d"] = dd[0]
    dk = dk.at[:BLOCK].add(dkm)
    dvv = dvv.at[:BLOCK].add(dvm)
    dz = jnp.concatenate([dq.astype(BF16), dk.astype(BF16), dvv.astype(BF16), du.astype(BF16), dga, dgs], axis=1)
    n = s["n_mix"]
    w_in = _w_in_full(p, l, ws)
    dw_in, dw_in_other = _tn_full(dz, n, "mix_dw_in", d // 2)
    red_mix = _Reduction([dw_in, dw_ap, dw_gv, dw_gg, dw_out],
                         [dw_in_other, dw_ap_other, dw_gv_other, dw_gg_other, dw_out_other], pos, "mix")
    dh1, g["mix_norm"], dh1b = _in_proj_bwd(dz, w_in, dh2, s["h1"], p["mix_norm"])
    comm2 = None if early_comm is None else early_comm(g)
    dh0_pair, g["ffn1_norm"], red_ffn1, red_mix.got, early_got = _ffn_bwd(
        (dh1, dh1b), s["h0"], p["ffn1_norm"], ws.get((l, "wg1")), ws.get((l, "wu1")), ws.get((l, "wd1")), p["f4"],
        s["ffn1"], pos, red_mix.comm(), comm2)
    return dh0_pair, g, [*red_ffn1, red_mix, *red_ffn2], early_got


BIG = ["ffn1_w_gate", "ffn1_w_up", "ffn1_w_down", "w_in", "w_attn_proj", "w_glu_v", "w_glu_g", "w_out",
       "ffn2_w_gate", "ffn2_w_up", "ffn2_w_down"]
TRANSPOSED = ["ffn1_w_gate", "ffn1_w_up", "w_in", "ffn2_w_gate", "ffn2_w_up"]
SMALL = ["ffn1_norm", "mix_norm", "attn_sinks", "ssm_a_re", "ssm_a_im", "ssm_log_dt", "ssm_b_re", "ssm_b_im",
         "ssm_c_re", "ssm_c_im", "ssm_d", "ffn2_norm", "final_norm"]
WEIGHTS = ["meta_tokens", "ffn1_norm", "ffn1_w_gate", "ffn1_w_up", "ffn1_w_down", "mix_norm", "w_in", "attn_sinks",
           "ssm_a_re", "ssm_a_im", "ssm_log_dt", "ssm_b_re", "ssm_b_im", "ssm_c_re", "ssm_c_im", "ssm_d",
           "w_attn_proj", "w_glu_v", "w_glu_g", "w_out", "ffn2_norm", "ffn2_w_gate", "ffn2_w_up", "ffn2_w_down",
           "final_norm"]


def _small_rows(shape):
    rows = -(-math.prod(shape) // LANES)
    return -(-rows // SUBLANES) * SUBLANES


def _pack_small(tree, names):
    parts = []
    for k in names:
        size, rows = math.prod(tree[k].shape), _small_rows(tree[k].shape)
        if size % LANES == 0:
            part = tree[k].reshape(size // LANES, LANES)
        else:
            part = jnp.pad(tree[k].reshape(1, size), ((0, 0), (0, LANES - size)))
        parts.append(jnp.pad(part, ((0, rows - part.shape[0]), (0, 0))))
    total = sum(part.shape[0] for part in parts)
    if total > PACK_ROWS:
        parts.append(jnp.zeros((-total % PACK_ROWS, LANES), F32))
    return jnp.concatenate(parts, axis=0)


def _unpack_small(packed, like, names):
    out, off = {}, 0
    for k in names:
        size, rows = math.prod(like[k].shape), _small_rows(like[k].shape)
        if size % LANES == 0:
            out[k] = packed[off:off + size // LANES].reshape(like[k].shape)
        else:
            out[k] = packed[off, :size].reshape(like[k].shape)
        off += rows
    return out


def kernel(x, meta_tokens, ffn1_norm, ffn1_w_gate, ffn1_w_up, ffn1_w_down, mix_norm, w_in, attn_sinks, ssm_a_re, ssm_a_im, ssm_log_dt, ssm_b_re, ssm_b_im, ssm_c_re, ssm_c_im, ssm_d, w_attn_proj, w_glu_v, w_glu_g, w_out, ffn2_norm, ffn2_w_gate, ffn2_w_up, ffn2_w_down, final_norm, loss_target, m_meta_tokens, m_ffn1_norm, m_ffn1_w_gate, m_ffn1_w_up, m_ffn1_w_down, m_mix_norm, m_w_in, m_attn_sinks, m_ssm_a_re, m_ssm_a_im, m_ssm_log_dt, m_ssm_b_re, m_ssm_b_im, m_ssm_c_re, m_ssm_c_im, m_ssm_d, m_w_attn_proj, m_w_glu_v, m_w_glu_g, m_w_out, m_ffn2_norm, m_ffn2_w_gate, m_ffn2_w_up, m_ffn2_w_down, m_final_norm, v_meta_tokens, v_ffn1_norm, v_ffn1_w_gate, v_ffn1_w_up, v_ffn1_w_down, v_mix_norm, v_w_in, v_attn_sinks, v_ssm_a_re, v_ssm_a_im, v_ssm_log_dt, v_ssm_b_re, v_ssm_b_im, v_ssm_c_re, v_ssm_c_im, v_ssm_d, v_w_attn_proj, v_w_glu_v, v_w_glu_g, v_w_out, v_ffn2_norm, v_ffn2_w_gate, v_ffn2_w_up, v_ffn2_w_down, v_final_norm):
    args = dict(locals())
    w = {k: args[k] for k in WEIGHTS}
    m = {k: args["m_" + k] for k in WEIGHTS}
    v = {k: args["v_" + k] for k in WEIGHTS}
    depth = ffn1_norm.shape[0]
    seq, d = x.shape[1], x.shape[2]
    lp = seq + BLOCK
    xi, yi, ci = _mesh_pos()
    pos = jnp.stack([ci, 2 * xi + yi]).astype(jnp.int32)

    tabs = _rope_tables(lp)
    layers, pieces = [], [((0, "meta"), meta_tokens)]
    f4 = ffn1_w_gate.shape[2]
    fp = -(-f4 // MXU_DIM) * MXU_DIM

    def ffn_rows(wt):
        return jnp.pad(wt, ((0, fp - f4), (0, 0))).astype(BF16)

    for l in range(depth):
        small = [((l, "w3"), jnp.stack([w_attn_proj[l].T, w_glu_v[l].T, w_glu_g[l].T]).astype(BF16)),
                 ((l, "w_out"), w_out[l].astype(BF16))]
        first = [((l, "wg1"), ffn_rows(ffn1_w_gate[l].T)), ((l, "wu1"), ffn_rows(ffn1_w_up[l].T)),
                 ((l, "wd1"), ffn_rows(ffn1_w_down[l])), ((l, "w_in"), w_in[l].T.astype(BF16))]
        pieces += (first + small if l == 0 else small + first) + [
            ((l, "wg2"), ffn_rows(ffn2_w_gate[l].T)), ((l, "wu2"), ffn_rows(ffn2_w_up[l].T)),
            ((l, "wd2"), ffn_rows(ffn2_w_down[l]))]
        lb_re, lb_im, bb_re, bb_im = _ssm_params(ssm_a_re[l], ssm_a_im[l], ssm_log_dt[l], ssm_b_re[l], ssm_b_im[l])
        ngrp = lb_re.shape[0]
        nt = ngrp // GROUPS_PER_TILE
        ssm_tabs = (lb_re.reshape(nt, 1, TILE_STATES), lb_im.reshape(nt, 1, TILE_STATES),
                    *_ssm_tables(bb_re, bb_im, ssm_c_re[l], ssm_c_im[l]))
        layers.append(dict(
            ffn1_norm=ffn1_norm[l], mix_norm=mix_norm[l], ffn2_norm=ffn2_norm[l], attn_sinks=attn_sinks[l],
            ssm_d=ssm_d[l], ssm_tabs=ssm_tabs, f4=f4))
    ws = _WeightStream(pieces)
    ws.get((0, "wu1"))
    meta_all = ws.get((0, "meta"))
    meta_full = jnp.concatenate([meta_all[j] for j in range(N_CHIPS)], axis=1)

    h = jnp.concatenate([jnp.zeros((PAD_FRONT, d), F32), meta_full, x[0]], axis=0)
    saved = []
    n0 = _rms_fwd(h, ffn1_norm[0], "rms_fwd_first")
    for l in range(depth):
        next_gain = ffn1_norm[l + 1] if l + 1 < depth else None
        h, n0, s = _layer_fwd(h, n0, l, layers[l], next_gain, ws, tabs)
        saved.append(s)
    dh, g_final, loss_acc, dhb = _loss_head(h, final_norm, loss_target[0])
    dh_pair = (dh, dhb)
    loss = lax.psum(loss_acc[0, 0], ("x", "y", "c"))

    grads, reds = [None] * depth, [None] * depth

    def layer_small(gl, l):
        _, vjp = jax.vjp(_ssm_params, ssm_a_re[l], ssm_a_im[l], ssm_log_dt[l], ssm_b_re[l], ssm_b_im[l])
        da_re, da_im, dlog_dt, db_re, db_im = vjp(gl["ssm_lam"])
        first = gl["ffn1_norm"][0] if "ffn1_norm" in gl else jnp.zeros((d,), F32)
        return dict(ffn1_norm=first, mix_norm=gl["mix_norm"][0], attn_sinks=gl["attn_sinks"], ssm_a_re=da_re,
                    ssm_a_im=da_im, ssm_log_dt=dlog_dt, ssm_b_re=db_re, ssm_b_im=db_im, ssm_c_re=gl["ssm_c_re"],
                    ssm_c_im=gl["ssm_c_im"], ssm_d=gl["ssm_d"], ffn2_norm=gl["ffn2_norm"][0])

    class early:
        got, like = None, None

    def early_comm(g0):
        per = [layer_small(g0, 0)] + [layer_small(grads[l], l) for l in range(1, depth)]
        tree = {k: jnp.stack([lay[k] for lay in per]) for k in SMALL if k != "final_norm"}
        tree["final_norm"] = g_final[0]
        early.like = tree
        return _device_gather_comm(_pack_small(tree, SMALL), "small_grads")

    for l in reversed(range(depth)):
        dh_pair, grads[l], reds[l], got = _layer_bwd(
            dh_pair, l, layers[l], ws, saved[l], tabs, pos, early_comm if l == 0 else None)
        if l == 0:
            early.got = got
    dh = dh_pair[0]
    grad_x = dh[BLOCK:][None]
    dmeta_local = dh[PAD_FRONT:BLOCK]

    g_small_tree = _unpack_small(_sum_devices(early.got[0], "sum_small_grads"), early.like, SMALL)
    late_names = ["ffn1_norm", "meta_tokens"]
    late = dict(ffn1_norm=grads[0]["ffn1_norm"], meta_tokens=dmeta_local)
    g_late = _sum_devices(_all_gather_devices(_pack_small(late, late_names), "gather_late_grads"), "sum_late_grads")
    g_late = _unpack_small(g_late, late, late_names)
    g_small_tree["ffn1_norm"] = g_small_tree["ffn1_norm"].at[0].set(g_late["ffn1_norm"][0])
    d4 = d // N_CHIPS
    chip = 2 * xi + yi
    g_meta = lax.dynamic_slice_in_dim(g_late["meta_tokens"], chip * d4, d4, axis=1)

    mine = [[half for red in reds[l] for half in red.end()] for l in range(depth)]
    flat = _share_halves([half for layer_halves in mine for half in layer_halves], "rs_share")
    per_layer = len(mine[0])
    reduced = [(mine[l], flat[l * per_layer:(l + 1) * per_layer]) for l in range(depth)]

    g_out, delta, new_m, new_v = {}, {}, {}, {}
    for i, k in enumerate(BIG):
        flip = (lambda t: jnp.swapaxes(t, 1, 2)) if k in TRANSPOSED else (lambda t: t)
        outs = _adamw_layers(
            flip(w[k]), flip(m[k]), flip(v[k]), [reduced[l][0][i] for l in range(depth)],
            [reduced[l][1][i] for l in range(depth)], pos, "adamw_" + k)
        g_out[k], delta[k], new_m[k], new_v[k] = [flip(t) for t in outs]
    g_small_tree["meta_tokens"] = g_meta
    for k in SMALL + ["meta_tokens"]:
        narrow = w[k].ndim > 2 and w[k].shape[-1] < w[k].shape[-2]
        view = (lambda t: jnp.swapaxes(t, -1, -2)) if narrow else (lambda t: t)
        shape = view(w[k]).shape if w[k].ndim > 1 else (1,) + w[k].shape
        outs = _adamw_whole(view(w[k]).reshape(shape), view(g_small_tree[k]).reshape(shape),
                            view(m[k]).reshape(shape), view(v[k]).reshape(shape), "adamw_" + k)
        g_out[k] = g_small_tree[k]
        delta[k], new_m[k], new_v[k] = [view(t).reshape(w[k].shape) for t in outs]

    return (loss, grad_x, *[g_out[k] for k in WEIGHTS], *[delta[k] for k in WEIGHTS],
            *[new_m[k] for k in WEIGHTS], *[new_v[k] for k in WEIGHTS])
```

```python
import functools
import math

import jax
import jax.numpy as jnp
from jax import lax
from jax.experimental import pallas as pl
from jax.experimental.pallas import tpu as pltpu

F32 = jnp.float32
BF16 = jnp.bfloat16

N_META = 16
HEAD_DIM = 64
N_Q_HEADS = 8
N_KV_HEADS = 2
Q_PER_KV = N_Q_HEADS // N_KV_HEADS
ATTN_WIDTH = N_Q_HEADS * HEAD_DIM
KV_WIDTH = N_KV_HEADS * HEAD_DIM
BLOCK = 128
PAD_FRONT = BLOCK - N_META
ROPE_THETA = 500000.0
ROT_DIM = HEAD_DIM // 4
SSM_GROUP = 16
SSM_STATE = 64
GROUPS_PER_TILE = 4
TILE_STATES = GROUPS_PER_TILE * SSM_STATE
LANES = 128
SUBLANES = 8
MXU_DIM = 256
PACK_ROWS = 256
EPS = 1e-6
NEG_INF = -1e30
N_CHIPS = 4

ADAM_LR = 0.001
ADAM_B1 = 0.9
ADAM_B2 = 0.999
ADAM_EPS = 1e-08
ADAM_WD = 0.01
ADAM_STEP = 10

VMEM_LIMIT = 56 * 1024 * 1024
MESH = pl.DeviceIdType.MESH


def _cparams(sem=None):
    return pltpu.CompilerParams(dimension_semantics=sem, vmem_limit_bytes=VMEM_LIMIT)


def _row_tile(rows, limit=512):
    best = None
    for t in range(128, limit + 1, 128):
        if rows % t == 0:
            best = t
    assert best is not None, rows
    return best


def _div_tile(rows, row_bytes, max_bytes=1 << 20, mult=8):
    best = None
    for t in range(mult, rows + 1, mult):
        if rows % t == 0 and t * row_bytes <= max_bytes:
            best = t
    if best is None:
        best = rows
    return best


def _dot(a, b, mode):
    if mode == "nn":
        dims = (((1,), (0,)), ((), ()))
    elif mode == "nt":
        dims = (((1,), (1,)), ((), ()))
    else:
        dims = (((0,), (0,)), ((), ()))
    return lax.dot_general(a.astype(BF16), b.astype(BF16), dims, preferred_element_type=F32)


def _sigmoid(x):
    return 1.0 / (1.0 + jnp.exp(-x))


_GELU_C = math.sqrt(2.0 / math.pi)


def _gelu(x):
    return 0.5 * x * (1.0 + jnp.tanh(_GELU_C * (x + 0.044715 * x * x * x)))


def _gelu_grad(x):
    t = jnp.tanh(_GELU_C * (x + 0.044715 * x * x * x))
    return 0.5 * (1.0 + t) + 0.5 * x * (1.0 - t * t) * _GELU_C * (1.0 + 3.0 * 0.044715 * x * x)


class _Comm:
    def __init__(self, tag, ins, out_shapes, sems, start, mid, finish):
        self.tag, self.ins, self.out_shapes, self.sems = tag, list(ins), list(out_shapes), list(sems)
        self.start, self.mid, self.finish = start, mid, finish


HBM_SPEC = pl.BlockSpec(memory_space=pltpu.HBM)
MID_NUM, MID_DEN = 4, 5


def _hosted_call(body, comm, *, out_shape, grid, in_specs, out_specs, scratch_shapes, sem, name, args):
    out_shape, in_specs, out_specs = list(out_shape), list(in_specs), list(out_specs)
    scratch_shapes = list(scratch_shapes)
    if comm is None:
        res = pl.pallas_call(
            body, out_shape=out_shape, grid=grid, in_specs=in_specs, out_specs=out_specs,
            scratch_shapes=scratch_shapes, compiler_params=_cparams(sem), name=name)(*args)
        return list(res), []
    n_in, n_out, n_sc = len(args), len(out_shape), len(scratch_shapes)
    nci, nco = len(comm.ins), len(comm.out_shapes)
    total = math.prod(grid)

    def wrapped(*refs):
        in_refs, cin = refs[:n_in], refs[n_in:n_in + nci]
        o0 = n_in + nci
        out_refs, cout = refs[o0:o0 + n_out], refs[o0 + n_out:o0 + n_out + nco]
        s0 = o0 + n_out + nco
        sc, csem = refs[s0:s0 + n_sc], refs[s0 + n_sc:]
        lin = 0
        for dim, size in enumerate(grid):
            lin = lin * size + pl.program_id(dim)

        @pl.when(lin == 0)
        def _():
            comm.start(cin, cout, csem)

        if comm.mid is not None:
            @pl.when(lin == (total * MID_NUM) // MID_DEN)
            def _():
                comm.mid(cin, cout, csem)

        body(*in_refs, *out_refs, *sc)

        @pl.when(lin == total - 1)
        def _():
            comm.finish(cin, cout, csem)

    res = pl.pallas_call(
        wrapped, out_shape=out_shape + comm.out_shapes, grid=grid,
        in_specs=in_specs + [HBM_SPEC] * nci, out_specs=out_specs + [HBM_SPEC] * nco,
        scratch_shapes=scratch_shapes + comm.sems,
        compiler_params=_cparams(("arbitrary",) * len(grid)), name=name + "_" + comm.tag)(*args, *comm.ins)
    return list(res[:n_out]), list(res[n_out:])


def _matmul(name, grid, k_axis, ins, in_specs, pairs, acc_shapes, epilogue, out_shapes, out_specs, sem, comm=None):
    n_in, n_out, n_acc = len(ins), len(out_shapes), len(acc_shapes)

    def body(*refs):
        in_refs = refs[:n_in]
        out_refs = refs[n_in:n_in + n_out]
        acc_refs = refs[n_in + n_out:]
        if k_axis is None:
            accs = [None] * n_acc
            for ia, ib, mode, iacc in pairs:
                d = _dot(in_refs[ia][...], in_refs[ib][...], mode)
                accs[iacc] = d if accs[iacc] is None else accs[iacc] + d
            epilogue(accs, in_refs, out_refs)
            return
        k = pl.program_id(k_axis)

        @pl.when(k == 0)
        def _():
            for r in acc_refs:
                r[...] = jnp.zeros_like(r)

        for ia, ib, mode, iacc in pairs:
            acc_refs[iacc][...] += _dot(in_refs[ia][...], in_refs[ib][...], mode)

        @pl.when(k == pl.num_programs(k_axis) - 1)
        def _():
            epilogue([r[...] for r in acc_refs], in_refs, out_refs)

    scratch = [] if k_axis is None else [pltpu.VMEM(s, F32) for s in acc_shapes]
    outs, couts = _hosted_call(
        body, comm, out_shape=out_shapes, grid=grid, in_specs=in_specs, out_specs=out_specs,
        scratch_shapes=scratch, sem=sem, name=name, args=ins)
    return outs if comm is None else (outs, couts)


def _rms_math(x, g):
    r = lax.rsqrt(jnp.mean(x * x, axis=-1, keepdims=True) + EPS)
    return (x * r * g).astype(BF16)


def _rms_fwd(h, g, name):
    lp, d = h.shape
    tm = _row_tile(lp)

    def body(h_ref, g_ref, n_ref):
        n_ref[...] = _rms_math(h_ref[...], g_ref[...])

    return pl.pallas_call(
        body, out_shape=jax.ShapeDtypeStruct((lp, d), BF16), grid=(lp // tm,),
        in_specs=[pl.BlockSpec((tm, d), lambda i: (i, 0)), pl.BlockSpec((1, d), lambda i: (0, 0))],
        out_specs=pl.BlockSpec((tm, d), lambda i: (i, 0)),
        compiler_params=_cparams(("parallel",)), name=name)(h, g.reshape(1, d))


def _rms_bwd_math(dn, x, g):
    r = lax.rsqrt(jnp.mean(x * x, axis=-1, keepdims=True) + EPS)
    xh = x * r
    dxh = dn * g
    dx = r * (dxh - xh * jnp.mean(dxh * xh, axis=-1, keepdims=True))
    return dx, dn * xh


def _ffn_up(n, wgt, wut, comm=None):
    lp, d = n.shape
    fp = wgt.shape[1]
    tm = _row_tile(lp)

    def up_body(n_ref, wg_ref, wu_ref, a_ref, b_ref, s_ref):
        x = n_ref[...]
        for jc in range(N_CHIPS):
            cols = slice(jc * fp, (jc + 1) * fp)
            a = _dot(x, wg_ref[jc], "nt")
            b = _dot(x, wu_ref[jc], "nt")
            a_ref[:, cols] = a.astype(BF16)
            b_ref[:, cols] = b.astype(BF16)
            s_ref[:, cols] = (a * _sigmoid(a) * b).astype(BF16)

    ff = N_CHIPS * fp
    act = jax.ShapeDtypeStruct((lp, ff), BF16)
    act_tile = pl.BlockSpec((tm, ff), lambda i: (i, 0))
    w_spec = pl.BlockSpec((N_CHIPS, fp, d), lambda i: (0, 0, 0))
    outs, couts = _hosted_call(
        up_body, comm, out_shape=[act, act, act], grid=(lp // tm,),
        in_specs=[pl.BlockSpec((tm, d), lambda i: (i, 0)), w_spec, w_spec],
        out_specs=[act_tile] * 3, scratch_shapes=[], sem=("parallel",), name="ffn_up", args=(n, wgt, wut))
    return (*outs, n), couts


def _residual_outputs(h_new, in_refs, out_refs, gain_at):
    out_refs[0][...] = h_new
    if gain_at is not None:
        out_refs[1][...] = _rms_math(h_new, in_refs[gain_at][...])


def _residual_specs(lp, d, tm, next_gain):
    row = pl.BlockSpec((tm, d), lambda i: (i, 0))
    shapes, specs = [jax.ShapeDtypeStruct((lp, d), F32)], [row]
    extra_in, extra_specs = [], []
    if next_gain is not None:
        shapes.append(jax.ShapeDtypeStruct((lp, d), BF16))
        specs.append(row)
        extra_in, extra_specs = [next_gain.reshape(1, d)], [pl.BlockSpec((1, d), lambda i: (0, 0))]
    return shapes, specs, extra_in, extra_specs


def _ffn_down(s, wd, h, next_gain, comm=None):
    lp, d = h.shape
    ff = s.shape[1]
    tm = _row_tile(lp)
    shapes, specs, extra_in, extra_specs = _residual_specs(lp, d, tm, next_gain)

    def down_epi(accs, in_refs, out_refs):
        _residual_outputs(in_refs[2][...] + 0.5 * accs[0], in_refs, out_refs, 3 if extra_in else None)

    res = _matmul(
        "ffn_down", (lp // tm,), None, [s, wd.reshape(ff, d), h] + extra_in,
        [pl.BlockSpec((tm, ff), lambda i: (i, 0)), pl.BlockSpec((ff, d), lambda i: (0, 0)),
         pl.BlockSpec((tm, d), lambda i: (i, 0))] + extra_specs,
        [(0, 1, "nn", 0)], [(tm, d)], down_epi, shapes, specs, ("parallel",), comm)
    return (res, []) if comm is None else res


def _tn_tiles(lp):
    return _row_tile(lp, 1408)


def _ffn_bwd(dh_pair, h_in, gain, wgt, wut, wd, f4, saved, pos, comm=None, comm2=None):
    dh, dhb = dh_pair
    a, b, s, n = saved
    lp, d = h_in.shape
    fp = wgt.shape[1]
    ff = N_CHIPS * fp
    tm = _row_tile(lp)
    ni = lp // tm
    tk = _tn_tiles(lp)
    nk = lp // tk

    def ds_body(dh_ref, wd_ref, a_ref, b_ref, da_ref, db_ref):
        x = dh_ref[...]
        for jc in range(N_CHIPS):
            cols = slice(jc * fp, (jc + 1) * fp)
            ds = 0.5 * _dot(x, wd_ref[jc], "nt")
            av = a_ref[:, cols].astype(F32)
            bv = b_ref[:, cols].astype(F32)
            sg = _sigmoid(av)
            da_ref[:, cols] = (ds * bv * sg * (1.0 + av * (1.0 - sg))).astype(BF16)
            db_ref[:, cols] = (ds * av * sg).astype(BF16)

    act = jax.ShapeDtypeStruct((lp, ff), BF16)
    act_tile = pl.BlockSpec((tm, ff), lambda i: (i, 0))
    (da, db), couts = _hosted_call(
        ds_body, comm, out_shape=[act, act], grid=(ni,),
        in_specs=[pl.BlockSpec((tm, d), lambda i: (i, 0)), pl.BlockSpec((N_CHIPS, fp, d), lambda i: (0, 0, 0)),
                  act_tile, act_tile],
        out_specs=[act_tile, act_tile], scratch_shapes=[], sem=("parallel",), name="ffn_bwd_ds",
        args=(dhb, wd, a, b))

    dw_shape = jax.ShapeDtypeStruct((N_CHIPS, f4, d), F32)
    dw_spec = pl.BlockSpec((None, f4, d), lambda j, k: (j, 0, 0))
    in_col = pl.BlockSpec((tk, fp), lambda j, k: (k, j))
    in_row = pl.BlockSpec((tk, d), lambda j, k: (k, 0))

    half_shape = jax.ShapeDtypeStruct((N_CHIPS, f4 // 2, d), BF16)
    half_spec = pl.BlockSpec((None, f4 // 2, d), lambda j, k: (j, 0, 0))

    def dwd_epi(accs, in_refs, out_refs):
        dw = 0.5 * accs[0]
        out_refs[0][...] = dw[:f4]
        out_refs[1][...] = _sibling_half(dw, f4)

    res = _matmul(
        "ffn_dwd", (N_CHIPS, nk), 1, [s, dhb], [in_col, in_row],
        [(0, 1, "tn", 0)], [(fp, d)], dwd_epi, [dw_shape, half_shape], [dw_spec, half_spec],
        ("arbitrary", "arbitrary"), comm2)
    (dwd, dwd_other), couts2 = (res, []) if comm2 is None else res

    def dwgu_epi(accs, in_refs, out_refs):
        for i, acc in enumerate(accs):
            out_refs[i][...] = acc[:f4]
            out_refs[2 + i][...] = _sibling_half(acc, f4)

    red_down = _Reduction([dwd], [dwd_other], pos, "ffn_d")
    (dwg, dwu, dwg_other, dwu_other), red_down.got = _matmul(
        "ffn_dwgu", (N_CHIPS, nk), 1, [n, da, db], [in_row, in_col, in_col],
        [(1, 0, "tn", 0), (2, 0, "tn", 1)], [(fp, d)] * 2, dwgu_epi,
        [dw_shape, dw_shape, half_shape, half_shape], [dw_spec, dw_spec, half_spec, half_spec],
        ("arbitrary", "arbitrary"), red_down.comm())

    def dn_epi(accs, in_refs, out_refs):
        i = pl.program_id(0)
        dx, dgrow = _rms_bwd_math(accs[0], in_refs[5][...], in_refs[6][...])
        dh_new = in_refs[4][...] + dx
        out_refs[0][...] = dh_new
        out_refs[2][...] = dh_new.astype(BF16)

        @pl.when(i == 0)
        def _():
            out_refs[1][...] = jnp.zeros_like(out_refs[1])

        out_refs[1][...] += jnp.sum(dgrow, axis=0, keepdims=True)

    red = _Reduction([dwg, dwu], [dwg_other, dwu_other], pos, "ffn_gu")
    row_spec = pl.BlockSpec((tm, d), lambda i: (i, 0))
    act_spec = pl.BlockSpec((tm, ff), lambda i: (i, 0))
    w_spec = pl.BlockSpec((ff, d), lambda i: (0, 0))
    one_spec = pl.BlockSpec((1, d), lambda i: (0, 0))
    (dh_in, dgain, dh_in_b), red.got = _matmul(
        "ffn_bwd_dn", (ni,), None, [da, wgt.reshape(ff, d), db, wut.reshape(ff, d), dh, h_in, gain.reshape(1, d)],
        [act_spec, w_spec, act_spec, w_spec, row_spec, row_spec, one_spec],
        [(0, 1, "nn", 0), (2, 3, "nn", 0)], [(tm, d)], dn_epi,
        [jax.ShapeDtypeStruct((lp, d), F32), jax.ShapeDtypeStruct((1, d), F32), jax.ShapeDtypeStruct((lp, d), BF16)],
        [row_spec, one_spec, row_spec], ("arbitrary",), red.comm())
    return (dh_in, dh_in_b), dgain, [red, red_down], couts, couts2


def _rope_tables(lp):
    pos = jnp.arange(lp, dtype=F32) - float(PAD_FRONT)
    inv_freq = ROPE_THETA ** (-jnp.arange(0, ROT_DIM, 2, dtype=F32) / ROT_DIM)
    ang = pos[:, None] * inv_freq[None, :]
    cos, sin = jnp.cos(ang), jnp.sin(ang)
    half = ROT_DIM // 2
    ones = jnp.ones((lp, HEAD_DIM - ROT_DIM), F32)
    zeros_h = jnp.zeros((lp, half), F32)
    zeros_r = jnp.zeros((lp, HEAD_DIM - ROT_DIM), F32)
    c = jnp.concatenate([cos, cos, ones], axis=1)
    s1 = jnp.concatenate([-sin, zeros_h, zeros_r], axis=1)
    s2 = jnp.concatenate([zeros_h, sin, zeros_r], axis=1)
    reps = LANES // HEAD_DIM
    return jnp.stack([jnp.tile(c, (1, reps)), jnp.tile(s1, (1, reps)), jnp.tile(s2, (1, reps))])


def _rope(x, c, s1, s2):
    half = ROT_DIM // 2
    outs = []
    for ch in range(x.shape[1] // LANES):
        xc = x[:, ch * LANES:(ch + 1) * LANES]
        outs.append(xc * c + pltpu.roll(xc, LANES - half, 1) * s1 + pltpu.roll(xc, half, 1) * s2)
    return outs[0] if len(outs) == 1 else jnp.concatenate(outs, axis=1)


def _rope_t(dy, c, s1, s2):
    half = ROT_DIM // 2
    outs = []
    for ch in range(dy.shape[1] // LANES):
        dc = dy[:, ch * LANES:(ch + 1) * LANES]
        outs.append(dc * c + pltpu.roll(dc * s1, half, 1) + pltpu.roll(dc * s2, LANES - half, 1))
    return outs[0] if len(outs) == 1 else jnp.concatenate(outs, axis=1)


def _in_proj(n, w_in, tabs, ssm_w, comm=None):
    lp, d = n.shape
    inw = w_in.shape[0]
    tm = _row_tile(lp)
    o1 = ATTN_WIDTH
    o2 = o1 + KV_WIDTH
    o3 = o2 + KV_WIDTH
    o4 = o3 + ssm_w
    o5 = o4 + d

    def epi(accs, in_refs, out_refs):
        z = accs[0]
        c, s1, s2 = in_refs[2][0], in_refs[2][1], in_refs[2][2]
        out_refs[0][...] = _rope(z[:, :o1], c, s1, s2).astype(BF16)
        out_refs[1][...] = _rope(z[:, o1:o2], c, s1, s2).astype(BF16)
        out_refs[2][...] = z[:, o2:o3].astype(BF16)
        out_refs[3][...] = z[:, o3:o4]
        out_refs[4][...] = z[:, o4:o5]
        out_refs[5][...] = z[:, o5:]

    def rs(w, dt):
        return jax.ShapeDtypeStruct((lp, w), dt), pl.BlockSpec((tm, w), lambda i: (i, 0))

    shapes, specs = zip(rs(o1, BF16), rs(KV_WIDTH, BF16), rs(KV_WIDTH, BF16), rs(ssm_w, F32), rs(d, F32), rs(d, F32))
    res = _matmul(
        "mix_in_proj", (lp // tm,), None, [n, w_in, tabs],
        [pl.BlockSpec((tm, d), lambda i: (i, 0)), pl.BlockSpec((inw, d), lambda i: (0, 0)),
         pl.BlockSpec((3, tm, LANES), lambda i: (0, i, 0))],
        [(0, 1, "nt", 0)], [(tm, inw)], epi, list(shapes), list(specs), ("parallel",), comm)
    return (res, []) if comm is None else res


def _attn_mask(b):
    rows = lax.broadcasted_iota(jnp.int32, (BLOCK, 3 * BLOCK), 0)
    cols = lax.broadcasted_iota(jnp.int32, (BLOCK, 3 * BLOCK), 1)
    qpos = b * BLOCK + rows - PAD_FRONT
    kpos = (b - 1) * BLOCK + cols - PAD_FRONT
    dist = qpos - kpos
    band = (cols < 2 * BLOCK) & (kpos >= N_META) & (dist >= 0) & (dist < BLOCK)
    mrow = cols - 2 * BLOCK
    meta = (mrow >= PAD_FRONT) & ((mrow - PAD_FRONT) <= qpos)
    return band | meta


def _attn_probs(qh, kk, mask, sink):
    s = _dot(qh, kk, "nt") * (HEAD_DIM ** -0.5)
    s = jnp.where(mask, s, NEG_INF)
    m = jnp.maximum(jnp.max(s, axis=-1, keepdims=True), sink)
    e = jnp.exp(s - m)
    es = jnp.exp(sink - m)
    z = jnp.sum(e, axis=-1, keepdims=True) + es
    inv = 1.0 / z
    return e * inv, es * inv


def _head(ref_or_val, h):
    return ref_or_val[:, h * HEAD_DIM:(h + 1) * HEAD_DIM]


def _attn_fwd(q, k, v, sinks, comm=None):
    lp = q.shape[0]
    nb = lp // BLOCK

    def body(sink_ref, q_ref, kp_ref, kc_ref, km_ref, vp_ref, vc_ref, vm_ref, o_ref):
        b = pl.program_id(0)
        mask = _attn_mask(b)
        for hk in range(N_KV_HEADS):
            kk = jnp.concatenate([_head(kp_ref, hk), _head(kc_ref, hk), _head(km_ref, hk)], axis=0)
            vv = jnp.concatenate([_head(vp_ref, hk), _head(vc_ref, hk), _head(vm_ref, hk)], axis=0)
            for g in range(Q_PER_KV):
                h = hk * Q_PER_KV + g
                p, _ = _attn_probs(_head(q_ref, h), kk, mask, sink_ref[h])
                o_ref[:, h * HEAD_DIM:(h + 1) * HEAD_DIM] = _dot(p, vv, "nn").astype(BF16)

    cur = lambda b: (b, 0)
    prev = lambda b: (jnp.maximum(b - 1, 0), 0)
    first = lambda b: (0, 0)
    kvs = lambda f: pl.BlockSpec((BLOCK, KV_WIDTH), f)
    (o,), couts = _hosted_call(
        body, comm, out_shape=[jax.ShapeDtypeStruct((lp, ATTN_WIDTH), BF16)], grid=(nb,),
        in_specs=[pl.BlockSpec(memory_space=pltpu.SMEM), pl.BlockSpec((BLOCK, ATTN_WIDTH), cur),
                  kvs(prev), kvs(cur), kvs(first), kvs(prev), kvs(cur), kvs(first)],
        out_specs=[pl.BlockSpec((BLOCK, ATTN_WIDTH), cur)], scratch_shapes=[],
        sem=("parallel",), name="attn_fwd", args=(sinks, q, k, k, k, v, v, v))
    return o, couts


def _attn_bwd(q, k, v, do, sinks, tabs, comm=None):
    lp = q.shape[0]
    nb = lp // BLOCK
    scale = HEAD_DIM ** -0.5

    def body(sink_ref, q_ref, do_ref, kp_ref, kc_ref, km_ref, vp_ref, vc_ref, vm_ref, tq_ref, tk_ref, t0_ref,
             dq_ref, dk_ref, dv_ref, dkm_ref, dvm_ref, dsink_ref,
             dq_s, dkk_s, dvv_s, ck_s, cv_s, mk_s, mv_s):
        b = pl.program_id(0)

        @pl.when(b == 0)
        def _():
            for r in (ck_s, cv_s, mk_s, mv_s, dsink_ref):
                r[...] = jnp.zeros_like(r)

        @pl.when(b < nb)
        def _():
            mask = _attn_mask(b)
            for hk in range(N_KV_HEADS):
                kk = jnp.concatenate([_head(kp_ref, hk), _head(kc_ref, hk), _head(km_ref, hk)], axis=0)
                vv = jnp.concatenate([_head(vp_ref, hk), _head(vc_ref, hk), _head(vm_ref, hk)], axis=0)
                dkk = jnp.zeros((3 * BLOCK, HEAD_DIM), F32)
                dvv = jnp.zeros((3 * BLOCK, HEAD_DIM), F32)
                for g in range(Q_PER_KV):
                    h = hk * Q_PER_KV + g
                    qh = _head(q_ref, h)
                    doh = _head(do_ref, h)
                    p, ps = _attn_probs(qh, kk, mask, sink_ref[h])
                    dp = _dot(doh, vv, "nt")
                    delta = jnp.sum(p * dp, axis=-1, keepdims=True)
                    ds = (p * (dp - delta)).astype(BF16)
                    dsink_ref[h:h + 1, :] += jnp.zeros((1, LANES), F32) - jnp.sum(ps * delta)
                    dq_s[:, h * HEAD_DIM:(h + 1) * HEAD_DIM] = _dot(ds, kk, "nn") * scale
                    dkk = dkk + _dot(ds, qh, "tn") * scale
                    dvv = dvv + _dot(p, doh, "tn")
                dkk_s[:, hk * HEAD_DIM:(hk + 1) * HEAD_DIM] = dkk
                dvv_s[:, hk * HEAD_DIM:(hk + 1) * HEAD_DIM] = dvv
            dq_ref[...] = _rope_t(dq_s[...], tq_ref[0], tq_ref[1], tq_ref[2])
            dk_ref[...] = _rope_t(ck_s[...] + dkk_s[0:BLOCK, :], tk_ref[0], tk_ref[1], tk_ref[2])
            dv_ref[...] = cv_s[...] + dvv_s[0:BLOCK, :]
            ck_s[...] = dkk_s[BLOCK:2 * BLOCK, :]
            cv_s[...] = dvv_s[BLOCK:2 * BLOCK, :]
            mk_s[...] += dkk_s[2 * BLOCK:, :]
            mv_s[...] += dvv_s[2 * BLOCK:, :]

        @pl.when(b == nb)
        def _():
            dk_ref[...] = _rope_t(ck_s[...], tk_ref[0], tk_ref[1], tk_ref[2])
            dv_ref[...] = cv_s[...]
            dkm_ref[...] = _rope_t(mk_s[...], t0_ref[0], t0_ref[1], t0_ref[2])
            dvm_ref[...] = mv_s[...]

    cur = lambda b: (jnp.minimum(b, nb - 1), 0)
    prev = lambda b: (jnp.clip(b - 1, 0, nb - 1), 0)
    first = lambda b: (0, 0)
    kvs = lambda f: pl.BlockSpec((BLOCK, KV_WIDTH), f)
    tab = lambda f: pl.BlockSpec((3, BLOCK, LANES), lambda b: (0,) + f(b)[:1] + (0,))
    kv_out = lambda b: (jnp.maximum(b - 1, 0), 0)
    return _hosted_call(
        body, comm,
        out_shape=[jax.ShapeDtypeStruct((lp, ATTN_WIDTH), F32), jax.ShapeDtypeStruct((lp, KV_WIDTH), F32),
                   jax.ShapeDtypeStruct((lp, KV_WIDTH), F32), jax.ShapeDtypeStruct((BLOCK, KV_WIDTH), F32),
                   jax.ShapeDtypeStruct((BLOCK, KV_WIDTH), F32), jax.ShapeDtypeStruct((N_Q_HEADS, LANES), F32)],
        grid=(nb + 1,),
        in_specs=[pl.BlockSpec(memory_space=pltpu.SMEM), pl.BlockSpec((BLOCK, ATTN_WIDTH), cur),
                  pl.BlockSpec((BLOCK, ATTN_WIDTH), cur),
                  kvs(prev), kvs(cur), kvs(first), kvs(prev), kvs(cur), kvs(first),
                  tab(cur), tab(kv_out), tab(first)],
        out_specs=[pl.BlockSpec((BLOCK, ATTN_WIDTH), cur), kvs(kv_out), kvs(kv_out), kvs(first), kvs(first),
                   pl.BlockSpec((N_Q_HEADS, LANES), first)],
        scratch_shapes=[pltpu.VMEM((BLOCK, ATTN_WIDTH), F32), pltpu.VMEM((3 * BLOCK, KV_WIDTH), F32),
                        pltpu.VMEM((3 * BLOCK, KV_WIDTH), F32), pltpu.VMEM((BLOCK, KV_WIDTH), F32),
                        pltpu.VMEM((BLOCK, KV_WIDTH), F32), pltpu.VMEM((BLOCK, KV_WIDTH), F32),
                        pltpu.VMEM((BLOCK, KV_WIDTH), F32)],
        sem=("arbitrary",), name="attn_bwd", args=(sinks, q, do, k, k, k, v, v, v, tabs, tabs, tabs))


def _cmul(ar, ai, br, bi):
    return ar * br - ai * bi, ar * bi + ai * br


def _cpow(lr, li, n):
    rr = ri = None
    br, bi = lr, li
    while n:
        if n & 1:
            rr, ri = (br, bi) if rr is None else _cmul(rr, ri, br, bi)
        n >>= 1
        if n:
            br, bi = _cmul(br, bi, br, bi)
    return rr, ri


def _shift_rows(x, d, reverse):
    rows = lax.broadcasted_iota(jnp.int32, x.shape, 0)
    if not reverse:
        return jnp.where(rows >= d, pltpu.roll(x, d, 0), 0.0)
    return jnp.where(rows < SUBLANES - d, pltpu.roll(x, SUBLANES - d, 0), 0.0)


def _sublane_powers(mr, mi, reverse):
    rows = lax.broadcasted_iota(jnp.int32, mr.shape, 0)
    e = SUBLANES - 1 - rows if reverse else rows
    pr, pi = jnp.ones_like(mr), jnp.zeros_like(mr)
    br, bi = mr, mi
    for d in (1, 2, 4):
        tr, ti = _cmul(pr, pi, br, bi)
        on = (e & d) != 0
        pr, pi = jnp.where(on, tr, pr), jnp.where(on, ti, pi)
        if d < 4:
            br, bi = _cmul(br, bi, br, bi)
    return pr, pi


def _inclusive_prefix(er, ei, mr, mi, reverse):
    ir, ii, pr, pi = er, ei, mr, mi
    for d in (1, 2, 4):
        tr, ti = _cmul(pr, pi, _shift_rows(ir, d, reverse), _shift_rows(ii, d, reverse))
        ir, ii = ir + tr, ii + ti
        if d < 4:
            pr, pi = _cmul(pr, pi, pr, pi)
    return ir, ii


def _chain_rows(a, t, seg):
    return pl.ds(a * SUBLANES * seg + t, SUBLANES, stride=seg)


def _seg_scan(xr_ref, xi_ref, lam, seg, nchain, reverse, store, init, extra=None):
    nt = len(lam)
    acc0 = () if extra is None else extra[1]

    def step(i, carry):
        hs, acc = carry
        t = seg - 1 - i if reverse else i
        out = []
        for a in range(nchain):
            sl = _chain_rows(a, t, seg)
            for j in range(nt):
                lr, li = lam[j]
                k = 2 * (a * nt + j)
                hr, hi = hs[k], hs[k + 1]
                nr = lr * hr - li * hi + xr_ref[j, sl, :]
                ni = lr * hi + li * hr + xi_ref[j, sl, :]
                if store:
                    xr_ref[j, sl, :] = nr
                    xi_ref[j, sl, :] = ni
                if extra is not None:
                    acc = extra[0](t, a, j, nr, ni, acc)
                out += [nr, ni]
        return tuple(out), acc

    return lax.fori_loop(0, seg, step, (tuple(init), acc0))


def _ssm_scan(xr_ref, xi_ref, lam, seg, nchain, reverse, extra=None):
    nt = len(lam)
    zero = [jnp.zeros((SUBLANES, LANES), F32)] * (2 * nt * nchain)
    ends, _ = _seg_scan(xr_ref, xi_ref, lam, seg, nchain, reverse, False, zero)
    init = [None] * (2 * nt * nchain)
    last = 0 if reverse else SUBLANES - 1
    for j in range(nt):
        mr, mi = _cpow(lam[j][0], lam[j][1], seg)
        m8r, m8i = _cpow(mr, mi, SUBLANES)
        pwr, pwi = _sublane_powers(mr, mi, reverse)
        gr = gi = jnp.zeros((SUBLANES, LANES), F32)
        for a in (reversed(range(nchain)) if reverse else range(nchain)):
            k = 2 * (a * nt + j)
            incr, inci = _inclusive_prefix(ends[k], ends[k + 1], mr, mi, reverse)
            tr, ti = _cmul(pwr, pwi, gr, gi)
            init[k] = _shift_rows(incr, 1, reverse) + tr
            init[k + 1] = _shift_rows(inci, 1, reverse) + ti
            g2r, g2i = _cmul(m8r, m8i, gr, gi)
            gr = g2r + jnp.broadcast_to(incr[last:last + 1, :], gr.shape)
            gi = g2i + jnp.broadcast_to(inci[last:last + 1, :], gi.shape)
    _, acc = _seg_scan(xr_ref, xi_ref, lam, seg, nchain, reverse, True, init, extra)
    return acc


def _diag_mask():
    steps = LANES // SSM_GROUP // GROUPS_PER_TILE
    return (jnp.eye(steps, dtype=F32)[:, None, :, None] * jnp.eye(GROUPS_PER_TILE, dtype=F32)[None, :, None, :])


def _ssm_tables(bb_re, bb_im, c_re, c_im):
    g = bb_re.shape[0]
    nt = g // GROUPS_PER_TILE
    steps = LANES // SSM_GROUP // GROUPS_PER_TILE
    mask = _diag_mask()

    def b_tab(bb):
        x = bb.reshape(nt // steps, steps, GROUPS_PER_TILE, SSM_STATE, SSM_GROUP)
        x = jnp.transpose(x, (0, 1, 4, 2, 3))[:, :, None, None]
        m = jnp.transpose(mask, (0, 2, 3, 1))[None, :, :, :, None, :, None]
        return (x * m).reshape(nt, LANES, TILE_STATES)

    def c_tab(c):
        x = c.reshape(nt // steps, steps, GROUPS_PER_TILE, SSM_GROUP, SSM_STATE)
        x = jnp.transpose(x, (0, 1, 2, 4, 3))[:, :, :, :, None, None]
        m = mask[None, :, :, None, :, :, None]
        return (x * m).reshape(nt, TILE_STATES, LANES)

    return b_tab(bb_re), b_tab(bb_im), c_tab(c_re), c_tab(c_im)


def _ssm_untable_b(db, g):
    nt = g // GROUPS_PER_TILE
    steps = LANES // SSM_GROUP // GROUPS_PER_TILE
    x = db.reshape(nt // steps, steps, GROUPS_PER_TILE, SSM_STATE, steps, GROUPS_PER_TILE, SSM_GROUP)
    m = _diag_mask()[None, :, :, None, :, :, None]
    return jnp.sum(x * m, axis=(4, 5)).reshape(g, SSM_STATE, SSM_GROUP)


def _ssm_untable_c(dc, g):
    nt = g // GROUPS_PER_TILE
    steps = LANES // SSM_GROUP // GROUPS_PER_TILE
    x = dc.reshape(nt // steps, steps, steps, GROUPS_PER_TILE, SSM_GROUP, GROUPS_PER_TILE, SSM_STATE)
    m = jnp.transpose(_diag_mask(), (0, 2, 3, 1))[None, :, :, :, None, :, None]
    out = jnp.sum(x * m, axis=(2, 3))
    return jnp.transpose(out, (0, 1, 3, 2, 4)).reshape(g, SSM_GROUP, SSM_STATE)


def _lam_tiles(lam_ref):
    out = []
    for j in range(TILE_STATES // LANES):
        out.append(jnp.broadcast_to(lam_ref[:, j * LANES:(j + 1) * LANES], (SUBLANES, LANES)))
    return out


def _scan_chains(lp):
    for n in (4, 2, 1):
        if lp % (SUBLANES * n) == 0 and (lp // SUBLANES) % 16 == 0:
            return n
    raise ValueError(lp)


def _split_tiles(dst_ref, rows, val):
    for j in range(val.shape[1] // LANES):
        dst_ref[j, rows, :] = val[:, j * LANES:(j + 1) * LANES]


def _cat_tiles(src_ref, rows):
    njt = src_ref.shape[0]
    return jnp.concatenate([src_ref[j, rows, :] for j in range(njt)], axis=1).astype(BF16)


def _ssm_fwd(u, lam_re, lam_im, tb_re, tb_im, tc_re, tc_im, d_skip, comm=None):
    lp, w = u.shape
    nt = tb_re.shape[0]
    nchain = _scan_chains(lp)
    seg = lp // (SUBLANES * nchain)
    chunk = lp // SUBLANES
    njt = TILE_STATES // LANES

    def body(u_ref, lr_ref, li_ref, br_ref, bi_ref, cr_ref, ci_ref, d_ref, y_ref, yg_ref, xr, xi):
        t = pl.program_id(0)
        for s in range(SUBLANES):
            rs = pl.ds(s * chunk, chunk)
            ub = u_ref[rs, :].astype(BF16)
            _split_tiles(xr, rs, _dot(ub, br_ref[...], "nn"))
            _split_tiles(xi, rs, _dot(ub, bi_ref[...], "nn"))
        lrs, lis = _lam_tiles(lr_ref), _lam_tiles(li_ref)
        _ssm_scan(xr, xi, list(zip(lrs, lis)), seg, nchain, False)
        for s in range(SUBLANES):
            rs = pl.ds(s * chunk, chunk)
            y = _dot(_cat_tiles(xr, rs), cr_ref[...], "nn") - _dot(_cat_tiles(xi, rs), ci_ref[...], "nn")

            @pl.when(t % 2 == 0)
            def _():
                y_ref[rs, :] = y + d_ref[...] * u_ref[rs, :]

            @pl.when(t % 2 == 1)
            def _():
                total = y_ref[rs, :] + y
                y_ref[rs, :] = total
                yg_ref[rs, :] = _gelu(total).astype(BF16)

    blk = pl.BlockSpec((lp, LANES), lambda t: (0, t // 2))
    lam_spec = pl.BlockSpec((None, 1, TILE_STATES), lambda t: (t, 0, 0))
    b_spec = pl.BlockSpec((None, LANES, TILE_STATES), lambda t: (t, 0, 0))
    c_spec = pl.BlockSpec((None, TILE_STATES, LANES), lambda t: (t, 0, 0))
    (y, yg), couts = _hosted_call(
        body, comm, out_shape=[jax.ShapeDtypeStruct((lp, w), F32), jax.ShapeDtypeStruct((lp, w), BF16)], grid=(nt,),
        in_specs=[blk, lam_spec, lam_spec, b_spec, b_spec, c_spec, c_spec,
                  pl.BlockSpec((1, LANES), lambda t: (0, t // 2))],
        out_specs=[blk, blk],
        scratch_shapes=[pltpu.VMEM((njt, lp, LANES), F32), pltpu.VMEM((njt, lp, LANES), F32)],
        sem=("arbitrary",), name="ssm_fwd",
        args=(u, lam_re, lam_im, tb_re, tb_im, tc_re, tc_im, d_skip.reshape(1, w)))
    return (y, yg), couts


def _ssm_bwd(u, dy, lam_re, lam_im, tb_re, tb_im, tc_re, tc_im, d_skip, comm=None):
    lp, w = u.shape
    nt = tb_re.shape[0]
    nchain = _scan_chains(lp)
    seg = lp // (SUBLANES * nchain)
    chunk = lp // SUBLANES
    njt = TILE_STATES // LANES
    tbt_re, tbt_im = jnp.swapaxes(tb_re, 1, 2), jnp.swapaxes(tb_im, 1, 2)
    tct_re, tct_im = jnp.swapaxes(tc_re, 1, 2), jnp.swapaxes(tc_im, 1, 2)

    def body(u_ref, dy_ref, lr_ref, li_ref, br_ref, bi_ref, btr_ref, bti_ref, ctr_ref, cti_ref, d_ref,
             du_ref, dlr_ref, dli_ref, dbr_ref, dbi_ref, dcr_ref, dci_ref, dd_ref, hr, hi, ar, ai):
        t = pl.program_id(0)
        lrs, lis = _lam_tiles(lr_ref), _lam_tiles(li_ref)
        for s in range(SUBLANES):
            rs = pl.ds(s * chunk, chunk)
            ub = u_ref[rs, :].astype(BF16)
            dyb = dy_ref[rs, :].astype(BF16)
            _split_tiles(hr, rs, _dot(ub, br_ref[...], "nn"))
            _split_tiles(hi, rs, _dot(ub, bi_ref[...], "nn"))
            _split_tiles(ar, rs, _dot(dyb, ctr_ref[...], "nn"))
            _split_tiles(ai, rs, -_dot(dyb, cti_ref[...], "nn"))
        _ssm_scan(hr, hi, list(zip(lrs, lis)), seg, nchain, False)

        def dlam_step(tt, a, j, a_r, a_i, acc):
            sl = _chain_rows(a, jnp.maximum(tt - 1, 0), seg)
            p_r, p_i = hr[j, sl, :], hi[j, sl, :]
            acc = list(acc)
            acc[2 * j] = acc[2 * j] + jnp.where(tt > 0, a_r * p_r + a_i * p_i, 0.0)
            acc[2 * j + 1] = acc[2 * j + 1] + jnp.where(tt > 0, a_i * p_r - a_r * p_i, 0.0)
            return tuple(acc)

        zero = tuple([jnp.zeros((SUBLANES, LANES), F32)] * (2 * njt))
        conj = [(lr, -li) for lr, li in zip(lrs, lis)]
        acc = list(_ssm_scan(ar, ai, conj, seg, nchain, True, (dlam_step, zero)))
        row0 = lax.broadcasted_iota(jnp.int32, (SUBLANES, LANES), 0) == 0
        for j in range(njt):
            cs = slice(j * LANES, (j + 1) * LANES)
            for a in range(nchain):
                p_r = _shift_rows(hr[j, _chain_rows(a, seg - 1, seg), :], 1, False)
                p_i = _shift_rows(hi[j, _chain_rows(a, seg - 1, seg), :], 1, False)
                if a > 0:
                    before = pl.ds(a * SUBLANES * seg - 1, 1)
                    p_r = jnp.where(row0, jnp.broadcast_to(hr[j, before, :], p_r.shape), p_r)
                    p_i = jnp.where(row0, jnp.broadcast_to(hi[j, before, :], p_i.shape), p_i)
                a_r, a_i = ar[j, _chain_rows(a, 0, seg), :], ai[j, _chain_rows(a, 0, seg), :]
                acc[2 * j] = acc[2 * j] + a_r * p_r + a_i * p_i
                acc[2 * j + 1] = acc[2 * j + 1] + a_i * p_r - a_r * p_i
            dlr_ref[:, cs] = jnp.sum(acc[2 * j], axis=0, keepdims=True)
            dli_ref[:, cs] = jnp.sum(acc[2 * j + 1], axis=0, keepdims=True)

        dd = jnp.zeros((1, LANES), F32)
        for s in range(SUBLANES):
            rs = pl.ds(s * chunk, chunk)
            ub = u_ref[rs, :].astype(BF16)
            dyv = dy_ref[rs, :]
            dyb = dyv.astype(BF16)
            arb, aib = _cat_tiles(ar, rs), _cat_tiles(ai, rs)
            hrb, hib = _cat_tiles(hr, rs), _cat_tiles(hi, rs)
            du = _dot(arb, btr_ref[...], "nn") + _dot(aib, bti_ref[...], "nn")
            upd = [(dbr_ref, _dot(arb, ub, "tn")), (dbi_ref, _dot(aib, ub, "tn")),
                   (dcr_ref, _dot(dyb, hrb, "tn")), (dci_ref, -_dot(dyb, hib, "tn"))]
            for ref, val in upd:
                if s == 0:
                    ref[...] = val
                else:
                    ref[...] += val
            rows = lax.broadcasted_iota(jnp.int32, (chunk, LANES), 0) + s * chunk
            keep = rows >= PAD_FRONT
            dd = dd + jnp.sum(dyv * u_ref[rs, :], axis=0, keepdims=True)

            @pl.when(t % 2 == 0)
            def _():
                du_ref[rs, :] = jnp.where(keep, du + d_ref[...] * dyv, 0.0)

            @pl.when(t % 2 == 1)
            def _():
                du_ref[rs, :] += jnp.where(keep, du, 0.0)

        @pl.when(t % 2 == 0)
        def _():
            dd_ref[...] = dd

    blk = pl.BlockSpec((lp, LANES), lambda t: (0, t // 2))
    vec = pl.BlockSpec((1, LANES), lambda t: (0, t // 2))
    lam_spec = pl.BlockSpec((None, 1, TILE_STATES), lambda t: (t, 0, 0))
    b_spec = pl.BlockSpec((None, LANES, TILE_STATES), lambda t: (t, 0, 0))
    c_spec = pl.BlockSpec((None, TILE_STATES, LANES), lambda t: (t, 0, 0))
    lam_shape = jax.ShapeDtypeStruct((nt, 1, TILE_STATES), F32)
    bt_shape = jax.ShapeDtypeStruct((nt, TILE_STATES, LANES), F32)
    ct_shape = jax.ShapeDtypeStruct((nt, LANES, TILE_STATES), F32)
    st = pltpu.VMEM((njt, lp, LANES), F32)
    return _hosted_call(
        body, comm,
        out_shape=[jax.ShapeDtypeStruct((lp, w), F32), lam_shape, lam_shape, bt_shape, bt_shape, ct_shape, ct_shape,
                   jax.ShapeDtypeStruct((1, w), F32)],
        grid=(nt,),
        in_specs=[blk, blk, lam_spec, lam_spec, b_spec, b_spec, c_spec, c_spec, b_spec, b_spec, vec],
        out_specs=[blk, lam_spec, lam_spec, c_spec, c_spec, b_spec, b_spec, vec],
        scratch_shapes=[st, st, st, st], sem=("arbitrary",), name="ssm_bwd",
        args=(u, dy, lam_re, lam_im, tb_re, tb_im, tbt_re, tbt_im, tct_re, tct_im, d_skip.reshape(1, w)))


def _ssm_params(a_re, a_im, log_dt, b_re, b_im):
    dt = jnp.exp(log_dt)[:, None]
    mag = jnp.exp(a_re * dt)
    lb_re = mag * jnp.cos(a_im * dt)
    lb_im = mag * jnp.sin(a_im * dt)
    den = a_re * a_re + a_im * a_im
    num_re = lb_re - 1.0
    coef_re = (num_re * a_re + lb_im * a_im) / den
    coef_im = (lb_im * a_re - num_re * a_im) / den
    bb_re = coef_re[..., None] * b_re - coef_im[..., None] * b_im
    bb_im = coef_re[..., None] * b_im + coef_im[..., None] * b_re
    return lb_re, lb_im, bb_re, bb_im


def _merge_fwd(o, yg, ga, gs, w3t, comm=None):
    lp, d = ga.shape
    kw = w3t.shape[2]
    tm = _row_tile(lp)

    def epi(accs, in_refs, out_refs):
        attn, vv, gg = accs
        out_refs[0][...] = (_sigmoid(in_refs[5][...]) * attn
                            + _sigmoid(in_refs[6][...]) * (vv * _sigmoid(gg))).astype(BF16)

    wspec = lambda which: pl.BlockSpec((None, d, kw), lambda i: (which, 0, 0))
    rowspec = pl.BlockSpec((tm, d), lambda i: (i, 0))
    aspec = pl.BlockSpec((tm, kw), lambda i: (i, 0))
    res = _matmul(
        "merge_fwd", (lp // tm,), None, [o, yg, w3t, w3t, w3t, ga, gs],
        [aspec, aspec, wspec(0), wspec(1), wspec(2), rowspec, rowspec],
        [(0, 2, "nt", 0), (1, 3, "nt", 1), (1, 4, "nt", 2)], [(tm, d)] * 3, epi,
        [jax.ShapeDtypeStruct((lp, d), BF16)], [rowspec], ("parallel",), comm)
    return (res[0], []) if comm is None else (res[0][0], res[1])


def _out_proj(merged, w_out, h, next_gain, comm=None):
    lp, d = h.shape
    tm = _row_tile(lp)
    shapes, specs, extra_in, extra_specs = _residual_specs(lp, d, tm, next_gain)

    def epi(accs, in_refs, out_refs):
        _residual_outputs(in_refs[2][...] + accs[0], in_refs, out_refs, 3 if extra_in else None)

    rowspec = pl.BlockSpec((tm, d), lambda i: (i, 0))
    res = _matmul(
        "mix_out_proj", (lp // tm,), None, [merged, w_out, h] + extra_in,
        [rowspec, pl.BlockSpec((d, d), lambda i: (0, 0)), rowspec] + extra_specs,
        [(0, 1, "nn", 0)], [(tm, d)], epi, shapes, specs, ("parallel",), comm)
    return (res, []) if comm is None else res


def _merge_bwd(dhb, w_out, o, yg, ga, gs, w3t):
    lp, d = ga.shape
    kw = w3t.shape[2]
    tm = _row_tile(lp)

    def epi(accs, in_refs, out_refs):
        dm, attn, vv, gg = accs
        sa = _sigmoid(in_refs[7][...])
        ss = _sigmoid(in_refs[8][...])
        sg = _sigmoid(gg)
        ssm = vv * sg
        dssm = dm * ss
        out_refs[0][...] = (dm * sa).astype(BF16)
        out_refs[1][...] = (dssm * sg).astype(BF16)
        out_refs[2][...] = (dssm * vv * sg * (1.0 - sg)).astype(BF16)
        out_refs[3][...] = (dm * attn * sa * (1.0 - sa)).astype(BF16)
        out_refs[4][...] = (dm * ssm * ss * (1.0 - ss)).astype(BF16)

    wspec = lambda which: pl.BlockSpec((None, d, kw), lambda i: (which, 0, 0))
    rowspec = pl.BlockSpec((tm, d), lambda i: (i, 0))
    aspec = pl.BlockSpec((tm, kw), lambda i: (i, 0))
    shp = jax.ShapeDtypeStruct((lp, d), BF16)
    return _matmul(
        "merge_bwd", (lp // tm,), None, [dhb, w_out, o, yg, w3t, w3t, w3t, ga, gs],
        [rowspec, pl.BlockSpec((d, d), lambda i: (0, 0)), aspec, aspec, wspec(0), wspec(1), wspec(2), rowspec,
         rowspec],
        [(0, 1, "nt", 0), (2, 4, "nt", 1), (3, 5, "nt", 2), (3, 6, "nt", 3)], [(tm, d)] * 4, epi,
        [shp] * 5, [rowspec] * 5, ("parallel",))


def _branch_bwd(dattn, dv, dg, w3t, y):
    lp, d = dattn.shape
    kw = w3t.shape[2]
    tm = _row_tile(lp)

    def epi(accs, in_refs, out_refs):
        out_refs[0][...] = accs[0].astype(BF16)
        out_refs[1][...] = accs[1] * _gelu_grad(in_refs[6][...])

    wspec = lambda which: pl.BlockSpec((None, d, kw), lambda i: (which, 0, 0))
    rowspec = pl.BlockSpec((tm, d), lambda i: (i, 0))
    aspec = pl.BlockSpec((tm, kw), lambda i: (i, 0))
    return _matmul(
        "branch_bwd", (lp // tm,), None, [dattn, dv, dg, w3t, w3t, w3t, y],
        [rowspec, rowspec, rowspec, wspec(0), wspec(1), wspec(2), aspec],
        [(0, 3, "nn", 0), (1, 4, "nn", 1), (2, 5, "nn", 1)], [(tm, kw)] * 2, epi,
        [jax.ShapeDtypeStruct((lp, kw), BF16), jax.ShapeDtypeStruct((lp, kw), F32)], [aspec, aspec],
        ("parallel",))


def _sibling_half(acc, rows):
    half = rows // 2
    return jnp.where(lax.axis_index("c") == 0, acc[half:rows], acc[:half]).astype(BF16)


def _tn_cols(x, ys, name):
    lp, kx = x.shape
    n = ys[0].shape[1]
    n4 = n // N_CHIPS
    tk = _tn_tiles(lp)
    ny = len(ys)

    def epi(accs, in_refs, out_refs):
        for i, acc in enumerate(accs):
            out_refs[i][...] = acc
            out_refs[ny + i][...] = _sibling_half(acc, kx)

    shp = jax.ShapeDtypeStruct((N_CHIPS, kx, n4), F32)
    shp_half = jax.ShapeDtypeStruct((N_CHIPS, kx // 2, n4), BF16)
    res = _matmul(
        name, (N_CHIPS, lp // tk), 1, [x] + list(ys),
        [pl.BlockSpec((tk, kx), lambda j, k: (k, 0))] + [pl.BlockSpec((tk, n4), lambda j, k: (k, j))] * ny,
        [(0, 1 + i, "tn", i) for i in range(ny)], [(kx, n4)] * ny, epi,
        [shp] * ny + [shp_half] * ny,
        [pl.BlockSpec((None, kx, n4), lambda j, k: (j, 0, 0))] * ny
        + [pl.BlockSpec((None, kx // 2, n4), lambda j, k: (j, 0, 0))] * ny,
        ("arbitrary", "arbitrary"))
    return res[:ny], res[ny:]


def _tn_full(x, y, name, tn_cols=None):
    lp, kx = x.shape
    n = y.shape[1]
    tk = _tn_tiles(lp)
    tn = n if tn_cols is None else tn_cols
    k4 = kx // N_CHIPS

    def epi(accs, in_refs, out_refs):
        for j in range(N_CHIPS):
            slab = accs[0][j * k4:(j + 1) * k4]
            out_refs[0][j] = slab
            out_refs[1][j] = _sibling_half(slab, k4)

    return _matmul(
        name, (n // tn, lp // tk), 1, [x, y],
        [pl.BlockSpec((tk, kx), lambda j, k: (k, 0)), pl.BlockSpec((tk, tn), lambda j, k: (k, j))],
        [(0, 1, "tn", 0)], [(kx, tn)], epi,
        [jax.ShapeDtypeStruct((N_CHIPS, k4, n), F32), jax.ShapeDtypeStruct((N_CHIPS, k4 // 2, n), BF16)],
        [pl.BlockSpec((N_CHIPS, k4, tn), lambda j, k: (0, 0, j)),
         pl.BlockSpec((N_CHIPS, k4 // 2, tn), lambda j, k: (0, 0, j))],
        ("arbitrary", "arbitrary"))


def _in_proj_bwd(dz, w_in, dh, h_in, gain):
    lp, d = h_in.shape
    inw = w_in.shape[0]
    tm = _row_tile(lp)

    def epi(accs, in_refs, out_refs):
        i = pl.program_id(0)
        dx, dgrow = _rms_bwd_math(accs[0], in_refs[3][...], in_refs[4][...])
        dh_new = in_refs[2][...] + dx
        out_refs[0][...] = dh_new
        out_refs[2][...] = dh_new.astype(BF16)

        @pl.when(i == 0)
        def _():
            out_refs[1][...] = jnp.zeros_like(out_refs[1])

        out_refs[1][...] += jnp.sum(dgrow, axis=0, keepdims=True)

    row = pl.BlockSpec((tm, d), lambda i: (i, 0))
    return _matmul(
        "mix_in_proj_bwd", (lp // tm,), None, [dz, w_in, dh, h_in, gain.reshape(1, d)],
        [pl.BlockSpec((tm, inw), lambda i: (i, 0)), pl.BlockSpec((inw, d), lambda i: (0, 0)), row, row,
         pl.BlockSpec((1, d), lambda i: (0, 0))],
        [(0, 1, "nn", 0)], [(tm, d)], epi,
        [jax.ShapeDtypeStruct((lp, d), F32), jax.ShapeDtypeStruct((1, d), F32), jax.ShapeDtypeStruct((lp, d), BF16)],
        [row, pl.BlockSpec((1, d), lambda i: (0, 0)), row], ("arbitrary",))


def _loss_head(h, gain, target):
    lp, d = h.shape
    nb = lp // BLOCK

    def body(h_ref, g_ref, t_ref, dh_ref, dg_ref, loss_ref, dhb_ref):
        i = pl.program_id(0)

        @pl.when(i == 0)
        def _():
            dg_ref[...] = jnp.zeros_like(dg_ref)
            loss_ref[...] = jnp.zeros_like(loss_ref)
            dh_ref[...] = jnp.zeros_like(dh_ref)
            dhb_ref[...] = jnp.zeros_like(dhb_ref)

        @pl.when(i > 0)
        def _():
            x = h_ref[...]
            g = g_ref[...]
            r = lax.rsqrt(jnp.mean(x * x, axis=-1, keepdims=True) + EPS)
            err = x * r * g - t_ref[...]
            loss_ref[...] += jnp.zeros_like(loss_ref) + 0.5 * jnp.sum(jnp.sum(err * err, axis=-1, keepdims=True)) / d
            dx, dgrow = _rms_bwd_math(err * (1.0 / d), x, g)
            dh_ref[...] = dx
            dhb_ref[...] = dx.astype(BF16)
            dg_ref[...] += jnp.sum(dgrow, axis=0, keepdims=True)

    row = pl.BlockSpec((BLOCK, d), lambda i: (i, 0))
    one = pl.BlockSpec((1, d), lambda i: (0, 0))
    return pl.pallas_call(
        body,
        out_shape=[jax.ShapeDtypeStruct((lp, d), F32), jax.ShapeDtypeStruct((1, d), F32),
                   jax.ShapeDtypeStruct((SUBLANES, LANES), F32), jax.ShapeDtypeStruct((lp, d), BF16)],
        grid=(nb,),
        in_specs=[row, one, pl.BlockSpec((BLOCK, d), lambda i: (jnp.maximum(i - 1, 0), 0))],
        out_specs=[row, one, pl.BlockSpec((SUBLANES, LANES), lambda i: (0, 0)), row],
        compiler_params=_cparams(("arbitrary",)), name="loss_head")(h, gain.reshape(1, d), target)


def _adam_math(w, g, m, v):
    m = ADAM_B1 * m + (1.0 - ADAM_B1) * g
    v = ADAM_B2 * v + (1.0 - ADAM_B2) * (g * g)
    m_hat = m / (1.0 - ADAM_B1 ** ADAM_STEP)
    v_hat = v / (1.0 - ADAM_B2 ** ADAM_STEP)
    delta = -ADAM_LR * (m_hat / (jnp.sqrt(v_hat) + ADAM_EPS) + ADAM_WD * w)
    return delta, m, v


def _adamw_layers(w, m, v, mine, other, pos, name):
    depth, r, c = w.shape
    half = r // 2
    tr = _div_tile(half, c * 4)
    nh = half // tr

    def body(*refs):
        pos_ref, w_ref, m_ref, v_ref = refs[:4]
        mine_refs = refs[4:4 + depth]
        other_refs = refs[4 + depth:4 + 2 * depth]
        g_out, d_out, m_out, v_out = refs[4 + 2 * depth:]
        layer, i = pl.program_id(0), pl.program_id(1)
        is_mine = (i // nh) == pos_ref[0]

        def update(g):
            delta, nm, nv = _adam_math(w_ref[...], g, m_ref[...], v_ref[...])
            g_out[...] = g
            d_out[...] = delta
            m_out[...] = nm
            v_out[...] = nv

        for l in range(depth):
            @pl.when((layer == l) & is_mine)
            def _(l=l):
                update(mine_refs[l][...])

            @pl.when((layer == l) & jnp.logical_not(is_mine))
            def _(l=l):
                update(other_refs[l][...])

    stacked = pl.BlockSpec((None, tr, c), lambda l, i, p: (l, i, 0))

    def gspec(layer, is_other):
        def imap(l, i, p):
            first = jnp.where(is_other, 1 - p[0], p[0]) * nh
            here = jnp.clip(i - first, 0, nh - 1)
            return (jnp.where(l == layer, here, jnp.where(l < layer, 0, nh - 1)), 0)
        return pl.BlockSpec((tr, c), imap)

    shp = jax.ShapeDtypeStruct((depth, r, c), F32)
    grid_spec = pltpu.PrefetchScalarGridSpec(
        num_scalar_prefetch=1, grid=(depth, 2 * nh),
        in_specs=[stacked] * 3 + [gspec(l, 0) for l in range(depth)] + [gspec(l, 1) for l in range(depth)],
        out_specs=[stacked] * 4)
    return pl.pallas_call(
        body, out_shape=[shp] * 4, grid_spec=grid_spec,
        compiler_params=_cparams(("arbitrary", "arbitrary")), name=name)(pos, w, m, v, *mine, *other)


def _adamw_whole(w, g, m, v, name):
    def body(w_ref, g_ref, m_ref, v_ref, d_out, m_out, v_out):
        delta, nm, nv = _adam_math(w_ref[...], g_ref[...], m_ref[...], v_ref[...])
        d_out[...] = delta
        m_out[...] = nm
        v_out[...] = nv

    shp = jax.ShapeDtypeStruct(w.shape, F32)
    return pl.pallas_call(body, out_shape=[shp] * 3, compiler_params=_cparams(), name=name)(w, g, m, v)


def _mesh_pos():
    return lax.axis_index("x"), lax.axis_index("y"), lax.axis_index("c")


def _row_half(ref, which, lead):
    half = ref.shape[lead] // 2
    idx = (slice(None),) * lead + (pl.ds(which * half, half), slice(None))
    return ref.at[idx]


def _gather_comm(arrs, tag):
    n = len(arrs)

    def ctx(ins, outs, sems):
        send_sems, recv_sems, local_sems = sems
        x, y, c = _mesh_pos()
        chips = [(1 - x, y), (x, 1 - y), (1 - x, 1 - y)]

        def slot(k, chip, which):
            lead = len(ins[k].shape) - 2
            return _row_half(outs[k].at[2 * chip[0] + chip[1]], which, lead)

        def copy(k, j, src, dst, to):
            return pltpu.make_async_remote_copy(
                src_ref=src, dst_ref=dst, send_sem=send_sems.at[6 * k + j], recv_sem=recv_sems.at[6 * k + j],
                device_id=to, device_id_type=MESH)

        def local(k):
            return pltpu.make_async_copy(ins[k], outs[k].at[2 * x + y], local_sems.at[k])

        def first(k, j):
            lead = len(ins[k].shape) - 2
            return copy(k, j, _row_half(ins[k], c, lead), slot(k, (x, y), c), (*chips[j], c))

        def passed(k, j, which):
            return copy(k, 3 + j, slot(k, chips[j], which), slot(k, chips[j], which), (x, y, 1 - c))

        def landed(k, j):
            return copy(k, j, slot(k, chips[j], c), slot(k, chips[j], c), (x, y, 1 - c))

        return c, local, first, passed, landed

    def start(ins, outs, sems):
        c, local, first, passed, landed = ctx(ins, outs, sems)
        for k in range(n):
            local(k).start()
            for j in range(3):
                first(k, j).start()

    def mid(ins, outs, sems):
        c, local, first, passed, landed = ctx(ins, outs, sems)
        for j in range(3):
            for k in range(n):
                landed(k, j).wait_recv()
                passed(k, j, c).start()

    def finish(ins, outs, sems):
        c, local, first, passed, landed = ctx(ins, outs, sems)
        for j in range(3):
            for k in range(n):
                passed(k, j, 1 - c).wait_recv()
        for k in range(n):
            for j in range(3):
                first(k, j).wait_send()
                passed(k, j, c).wait_send()
            local(k).wait()

    return _Comm(
        tag, arrs, [jax.ShapeDtypeStruct((N_CHIPS,) + a.shape, a.dtype) for a in arrs],
        [pltpu.SemaphoreType.DMA((6 * n,)), pltpu.SemaphoreType.DMA((6 * n,)), pltpu.SemaphoreType.DMA((n,))],
        start, mid, finish)


def _run_comm(comm, name):
    n_in, n_out = len(comm.ins), len(comm.out_shapes)

    def body(*refs):
        ins, outs, sems = refs[:n_in], refs[n_in:n_in + n_out], refs[n_in + n_out:]
        comm.start(ins, outs, sems)
        if comm.mid is not None:
            comm.mid(ins, outs, sems)
        comm.finish(ins, outs, sems)

    return pl.pallas_call(
        body, out_shape=comm.out_shapes, in_specs=[HBM_SPEC] * n_in, out_specs=[HBM_SPEC] * n_out,
        scratch_shapes=comm.sems, name=name)(*comm.ins)


def _all_gather_chips(arrs, name):
    return _run_comm(_gather_comm(arrs, "gather"), name)


GATHER_US_PER_BYTE = 380.0 / 11.65e6
HOST_US = dict(ffn_up=68.0, ffn_down=37.0, in_proj=38.0, attn_fwd=103.0, ssm_fwd=70.0, merge_fwd=30.0,
               out_proj=23.0)
HOST_SLACK_US = 10.0
OVERSHOOT = 1.3


class _WeightStream:
    def __init__(self, pieces):
        self.keys = [k for k, _ in pieces]
        self.shards = dict(pieces)
        self.next = 0
        self.full = {}
        self.pending = []

    def comm_for(self, host):
        budget = HOST_US[host] + HOST_SLACK_US
        taken, cost = [], 0.0
        while self.next < len(self.keys):
            key = self.keys[self.next]
            c = self.shards[key].size * self.shards[key].dtype.itemsize * GATHER_US_PER_BYTE
            if cost + c > budget and (taken or c > OVERSHOOT * budget):
                break
            taken.append(key)
            cost += c
            self.next += 1
        self.pending = taken
        if not taken:
            return None
        return _gather_comm([self.shards[k] for k in taken], "g_" + "_".join(k[1] for k in taken))

    def deposit(self, gathered):
        for key, arr in zip(self.pending, gathered):
            self.full[key] = arr
        self.pending = []

    def get(self, key):
        if key not in self.full:
            upto = self.keys.index(key) + 1
            keys = self.keys[self.next:upto]
            self.next = upto
            for k, arr in zip(keys, _all_gather_chips([self.shards[k] for k in keys], "gather_now")):
                self.full[k] = arr
        return self.full[key]


def _all_gather_devices(x_shard, name):
    m_per, ncol = x_shard.shape

    def body(x_ref, out_ref, send_sems, recv_sems, local_sem):
        x, y, c = _mesh_pos()
        me, sibling = (x, y, c), (x, y, 1 - c)
        chips = [(1 - x, y), (x, 1 - y), (1 - x, 1 - y)]

        def rows(px, py, pc):
            return out_ref.at[4 * px + 2 * py + pc]

        def copy(k, block, to, src=None):
            return pltpu.make_async_remote_copy(
                src_ref=rows(*block) if src is None else src, dst_ref=rows(*block),
                send_sem=send_sems.at[k], recv_sem=recv_sems.at[k], device_id=to, device_id_type=MESH)

        mine = pltpu.make_async_copy(x_ref, rows(*me), local_sem)
        mine.start()
        first = [copy(0, me, sibling, src=x_ref)]
        first += [copy(1 + j, me, (*chip, c), src=x_ref) for j, chip in enumerate(chips)]
        for cp in first:
            cp.start()
        passed = [copy(4 + j, (*chip, c), sibling) for j, chip in enumerate(chips)]
        for j, chip in enumerate(chips):
            copy(1 + j, (*chip, c), me).wait_recv()
            passed[j].start()
        copy(0, sibling, me).wait_recv()
        for j, chip in enumerate(chips):
            copy(4 + j, (*chip, 1 - c), me).wait_recv()
        for cp in first + passed:
            cp.wait_send()
        mine.wait()

    return pl.pallas_call(
        body, out_shape=jax.ShapeDtypeStruct((8, m_per, ncol), x_shard.dtype),
        in_specs=[pl.BlockSpec(memory_space=pltpu.VMEM)], out_specs=pl.BlockSpec(memory_space=pltpu.VMEM),
        scratch_shapes=[pltpu.SemaphoreType.DMA((7,)), pltpu.SemaphoreType.DMA((7,)), pltpu.SemaphoreType.DMA],
        compiler_params=pltpu.CompilerParams(vmem_limit_bytes=VMEM_LIMIT), name=name)(x_shard)


def _device_gather_comm(x_shard, tag):
    def ctx(ins, outs, sems):
        (x_ref,), (out_ref,) = ins, outs
        send_sems, recv_sems, local_sems = sems
        x, y, c = _mesh_pos()
        me, sibling = (x, y, c), (x, y, 1 - c)
        chips = [(1 - x, y), (x, 1 - y), (1 - x, 1 - y)]

        def rows(px, py, pc):
            return out_ref.at[4 * px + 2 * py + pc]

        def copy(k, block, to, src=None):
            return pltpu.make_async_remote_copy(
                src_ref=rows(*block) if src is None else src, dst_ref=rows(*block),
                send_sem=send_sems.at[k], recv_sem=recv_sems.at[k], device_id=to, device_id_type=MESH)

        mine = pltpu.make_async_copy(x_ref, rows(*me), local_sems.at[0])
        first = [copy(0, me, sibling, src=x_ref)] + [copy(1 + j, me, (*chip, c), src=x_ref)
                                                     for j, chip in enumerate(chips)]
        passed = [copy(4 + j, (*chip, c), sibling) for j, chip in enumerate(chips)]
        landed = [copy(1 + j, (*chip, c), me) for j, chip in enumerate(chips)]
        last = [copy(0, sibling, me)] + [copy(4 + j, (*chip, 1 - c), me) for j, chip in enumerate(chips)]
        return mine, first, passed, landed, last

    def start(ins, outs, sems):
        mine, first, _, _, _ = ctx(ins, outs, sems)
        mine.start()
        for cp in first:
            cp.start()

    def mid(ins, outs, sems):
        _, _, passed, landed, _ = ctx(ins, outs, sems)
        for cp, fwd in zip(landed, passed):
            cp.wait_recv()
            fwd.start()

    def finish(ins, outs, sems):
        mine, first, passed, _, last = ctx(ins, outs, sems)
        for cp in last:
            cp.wait_recv()
        for cp in first + passed:
            cp.wait_send()
        mine.wait()

    return _Comm(
        tag, [x_shard], [jax.ShapeDtypeStruct((8,) + x_shard.shape, x_shard.dtype)],
        [pltpu.SemaphoreType.DMA((7,)), pltpu.SemaphoreType.DMA((7,)), pltpu.SemaphoreType.DMA((1,))],
        start, mid, finish)


def _sum_devices(g8, name):
    _, r, c = g8.shape
    tr = _div_tile(r, c * 4 * 8)

    def body(g_ref, o_ref):
        acc = g_ref[0]
        for dev in range(1, 8):
            acc = acc + g_ref[dev]
        o_ref[...] = acc

    return pl.pallas_call(
        body, out_shape=jax.ShapeDtypeStruct((r, c), F32), grid=(r // tr,),
        in_specs=[pl.BlockSpec((8, tr, c), lambda i: (0, i, 0))], out_specs=pl.BlockSpec((tr, c), lambda i: (i, 0)),
        compiler_params=_cparams(("parallel",)), name=name)(g8)


def _chip_partials(arrs, recvs, pos, name):
    n = len(arrs)

    def body(pos_ref, *refs):
        for a_ref, b_ref, o_ref in zip(refs[:n], refs[n:2 * n], refs[2 * n:]):
            o_ref[...] = (a_ref[...] + b_ref[...]).astype(BF16)

    own_specs, recv_specs, shapes = [], [], []
    for arr in arrs:
        nslab, r, c = arr.shape
        own_specs.append(pl.BlockSpec((None, r // 2, c), lambda j, p: (j, p[0], 0)))
        recv_specs.append(pl.BlockSpec((None, r // 2, c), lambda j, p: (j, 0, 0)))
        shapes.append(jax.ShapeDtypeStruct((nslab, r // 2, c), BF16))
    grid_spec = pltpu.PrefetchScalarGridSpec(
        num_scalar_prefetch=1, grid=(N_CHIPS,), in_specs=own_specs + recv_specs, out_specs=recv_specs)
    return pl.pallas_call(
        body, out_shape=shapes, grid_spec=grid_spec,
        compiler_params=_cparams(("parallel",)), name=name)(pos, *arrs, *recvs)


def _chip_exchange_comm(parts, tag):
    n = len(parts)

    def copies(ins, outs, sems):
        send_sems, recv_sems = sems
        x, y, c = _mesh_pos()
        chips = [(1 - x, y), (x, 1 - y), (1 - x, 1 - y)]
        return [pltpu.make_async_remote_copy(
            src_ref=ins[k].at[2 * chip[0] + chip[1]], dst_ref=outs[k].at[j],
            send_sem=send_sems.at[3 * k + j], recv_sem=recv_sems.at[3 * k + j],
            device_id=(*chip, c), device_id_type=MESH) for k in range(n) for j, chip in enumerate(chips)]

    def start(ins, outs, sems):
        for cp in copies(ins, outs, sems):
            cp.start()

    def finish(ins, outs, sems):
        for cp in copies(ins, outs, sems):
            cp.wait()

    return _Comm(
        tag, parts, [jax.ShapeDtypeStruct((3,) + p.shape[1:], p.dtype) for p in parts],
        [pltpu.SemaphoreType.DMA((3 * n,)), pltpu.SemaphoreType.DMA((3 * n,))], start, None, finish)


def _reduce_halves(arrs, recvs, gots, pos, name):
    n = len(arrs)

    def body(pos_ref, *refs):
        for a_ref, b_ref, g_ref, o_ref in zip(refs[:n], refs[n:2 * n], refs[2 * n:3 * n], refs[3 * n:]):
            acc = a_ref[...] + b_ref[...]
            for j in range(3):
                acc = acc + g_ref[j].astype(F32)
            o_ref[...] = acc

    own_specs, recv_specs, got_specs, out_specs, shapes = [], [], [], [], []
    for arr in arrs:
        _, r, c = arr.shape
        own_specs.append(pl.BlockSpec((None, r // 2, c), lambda i, p: (p[1], p[0], 0)))
        recv_specs.append(pl.BlockSpec((None, r // 2, c), lambda i, p: (p[1], 0, 0)))
        got_specs.append(pl.BlockSpec((3, r // 2, c), lambda i, p: (0, 0, 0)))
        out_specs.append(pl.BlockSpec((r // 2, c), lambda i, p: (0, 0)))
        shapes.append(jax.ShapeDtypeStruct((r // 2, c), F32))
    grid_spec = pltpu.PrefetchScalarGridSpec(
        num_scalar_prefetch=1, grid=(1,), in_specs=own_specs + recv_specs + got_specs, out_specs=out_specs)
    return pl.pallas_call(
        body, out_shape=shapes, grid_spec=grid_spec,
        compiler_params=_cparams(("arbitrary",)), name=name)(pos, *arrs, *recvs, *gots)


def _share_halves(halves, name):
    n = len(halves)

    def body(*refs):
        ins, outs = refs[:n], refs[n:2 * n]
        send_sems, recv_sems = refs[2 * n:]
        x, y, c = _mesh_pos()
        cps = []
        for k in range(n):
            cp = pltpu.make_async_remote_copy(
                src_ref=ins[k], dst_ref=outs[k], send_sem=send_sems.at[k], recv_sem=recv_sems.at[k],
                device_id=(x, y, 1 - c), device_id_type=MESH)
            cp.start()
            cps.append(cp)
        for cp in cps:
            cp.wait()

    return pl.pallas_call(
        body, out_shape=[jax.ShapeDtypeStruct(h.shape, h.dtype) for h in halves],
        in_specs=[HBM_SPEC] * n, out_specs=[HBM_SPEC] * n,
        scratch_shapes=[pltpu.SemaphoreType.DMA((n,)), pltpu.SemaphoreType.DMA((n,))], name=name)(*halves)


class _Reduction:
    def __init__(self, arrs, others, pos, tag):
        self.arrs, self.pos, self.tag = arrs, pos, tag
        self.recv = _share_halves(others, "rs_sibling_" + tag)
        self.parts = _chip_partials(arrs, self.recv, pos, "rs_partial_" + tag)
        self.got = None

    def comm(self):
        return _chip_exchange_comm(self.parts, "rs_" + self.tag)

    def end(self):
        if self.got is None:
            self.got = _run_comm(self.comm(), "rs_chips_" + self.tag)
        return _reduce_halves(self.arrs, self.recv, self.got, self.pos, "rs_reduce_" + self.tag)


def _w_in_full(p, l, ws):
    slabs = ws.get((l, "w_in"))
    return slabs.reshape(-1, slabs.shape[2])


def _w3t_full(p, l, ws):
    if "w3t" not in p:
        slabs = ws.get((l, "w3"))
        p["w3t"] = jnp.swapaxes(slabs, 0, 1).reshape(slabs.shape[1], -1, slabs.shape[3])
    return p["w3t"]


def _w_out_full(l, ws):
    slabs = ws.get((l, "w_out"))
    return slabs.reshape(-1, slabs.shape[2])


def _layer_fwd(h, n0, l, p, next_gain, ws, tabs):
    def hosted(host, fn, *args):
        out, got = fn(*args, ws.comm_for(host))
        ws.deposit(got)
        return out

    ffn1_saved = hosted("ffn_up", _ffn_up, n0, ws.get((l, "wg1")), ws.get((l, "wu1")))
    h1, n = hosted("ffn_down", _ffn_down, ffn1_saved[2], ws.get((l, "wd1")), h, p["mix_norm"])
    ssm_w = p["ssm_d"].shape[0]
    q, k, v, u, ga, gs = hosted("in_proj", _in_proj, n, _w_in_full(p, l, ws), tabs, ssm_w)
    o = hosted("attn_fwd", _attn_fwd, q, k, v, p["attn_sinks"])
    y, yg = hosted("ssm_fwd", _ssm_fwd, u, *p["ssm_tabs"], p["ssm_d"])
    merged = hosted("merge_fwd", _merge_fwd, o, yg, ga, gs, _w3t_full(p, l, ws))
    h2, n2 = hosted("out_proj", _out_proj, merged, _w_out_full(l, ws), h1, p["ffn2_norm"])
    ffn2_saved = hosted("ffn_up", _ffn_up, n2, ws.get((l, "wg2")), ws.get((l, "wu2")))
    h3, *n3 = hosted("ffn_down", _ffn_down, ffn2_saved[2], ws.get((l, "wd2")), h2, next_gain)
    saved = dict(h0=h, h1=h1, h2=h2, ffn1=ffn1_saved, ffn2=ffn2_saved, n_mix=n, q=q, k=k, v=v, u=u, ga=ga, gs=gs,
                 o=o, y=y, yg=yg, merged=merged)
    return h3, (n3[0] if n3 else None), saved


def _layer_bwd(dh_pair, l, p, ws, s, tabs, pos, early_comm=None):
    g = {}
    (dh2, dhb), g["ffn2_norm"], red_ffn2, _, _ = _ffn_bwd(
        dh_pair, s["h2"], p["ffn2_norm"], ws.get((l, "wg2")), ws.get((l, "wu2")), ws.get((l, "wd2")), p["f4"],
        s["ffn2"], pos)
    w3, w_out_w = _w3t_full(p, l, ws), _w_out_full(l, ws)
    lp, d = dh2.shape
    d4 = d // N_CHIPS
    dw_out, dw_out_other = _tn_full(s["merged"], dhb, "mix_dw_out")
    dattn, dv, dg, dga, dgs = _merge_bwd(dhb, w_out_w, s["o"], s["yg"], s["ga"], s["gs"], w3)
    (dw_ap,), (dw_ap_other,) = _tn_cols(s["o"], [dattn], "mix_dw_ap")
    (dw_gv, dw_gg), (dw_gv_other, dw_gg_other) = _tn_cols(s["yg"], [dv, dg], "mix_dw_glu")
    do, dy = _branch_bwd(dattn, dv, dg, w3, s["y"])
    (dq, dk, dvv, dkm, dvm, dsink), _ = _attn_bwd(s["q"], s["k"], s["v"], do, p["attn_sinks"], tabs)
    g["attn_sinks"] = dsink[:, 0]
    (du, dlr, dli, dbr, dbi, dcr, dci, dd), _ = _ssm_bwd(s["u"], dy, *p["ssm_tabs"], p["ssm_d"])
    ngrp = p["ssm_d"].shape[0] // SSM_GROUP
    g["ssm_lam"] = (dlr.reshape(ngrp, SSM_STATE), dli.reshape(ngrp, SSM_STATE),
                    _ssm_untable_b(dbr, ngrp), _ssm_untable_b(dbi, ngrp))
    g["ssm_c_re"] = _ssm_untable_c(dcr, ngrp)
    g["ssm_c_im"] = _ssm_untable_c(dci, ngrp)
    g["ssm_d"] = dd[0]
    dk = dk.at[:BLOCK].add(dkm)
    dvv = dvv.at[:BLOCK].add(dvm)
    dz = jnp.concatenate([dq.astype(BF16), dk.astype(BF16), dvv.astype(BF16), du.astype(BF16), dga, dgs], axis=1)
    n = s["n_mix"]
    w_in = _w_in_full(p, l, ws)
    dw_in, dw_in_other = _tn_full(dz, n, "mix_dw_in", d // 2)
    red_mix = _Reduction([dw_in, dw_ap, dw_gv, dw_gg, dw_out],
                         [dw_in_other, dw_ap_other, dw_gv_other, dw_gg_other, dw_out_other], pos, "mix")
    dh1, g["mix_norm"], dh1b = _in_proj_bwd(dz, w_in, dh2, s["h1"], p["mix_norm"])
    comm2 = None if early_comm is None else early_comm(g)
    dh0_pair, g["ffn1_norm"], red_ffn1, red_mix.got, early_got = _ffn_bwd(
        (dh1, dh1b), s["h0"], p["ffn1_norm"], ws.get((l, "wg1")), ws.get((l, "wu1")), ws.get((l, "wd1")), p["f4"],
        s["ffn1"], pos, red_mix.comm(), comm2)
    return dh0_pair, g, [*red_ffn1, red_mix, *red_ffn2], early_got


BIG = ["ffn1_w_gate", "ffn1_w_up", "ffn1_w_down", "w_in", "w_attn_proj", "w_glu_v", "w_glu_g", "w_out",
       "ffn2_w_gate", "ffn2_w_up", "ffn2_w_down"]
TRANSPOSED = ["ffn1_w_gate", "ffn1_w_up", "w_in", "ffn2_w_gate", "ffn2_w_up"]
SMALL = ["ffn1_norm", "mix_norm", "attn_sinks", "ssm_a_re", "ssm_a_im", "ssm_log_dt", "ssm_b_re", "ssm_b_im",
         "ssm_c_re", "ssm_c_im", "ssm_d", "ffn2_norm", "final_norm"]
WEIGHTS = ["meta_tokens", "ffn1_norm", "ffn1_w_gate", "ffn1_w_up", "ffn1_w_down", "mix_norm", "w_in", "attn_sinks",
           "ssm_a_re", "ssm_a_im", "ssm_log_dt", "ssm_b_re", "ssm_b_im", "ssm_c_re", "ssm_c_im", "ssm_d",
           "w_attn_proj", "w_glu_v", "w_glu_g", "w_out", "ffn2_norm", "ffn2_w_gate", "ffn2_w_up", "ffn2_w_down",
           "final_norm"]


def _small_rows(shape):
    rows = -(-math.prod(shape) // LANES)
    return -(-rows // SUBLANES) * SUBLANES


def _pack_small(tree, names):
    parts = []
    for k in names:
        size, rows = math.prod(tree[k].shape), _small_rows(tree[k].shape)
        if size % LANES == 0:
            part = tree[k].reshape(size // LANES, LANES)
        else:
            part = jnp.pad(tree[k].reshape(1, size), ((0, 0), (0, LANES - size)))
        parts.append(jnp.pad(part, ((0, rows - part.shape[0]), (0, 0))))
    total = sum(part.shape[0] for part in parts)
    if total > PACK_ROWS:
        parts.append(jnp.zeros((-total % PACK_ROWS, LANES), F32))
    return jnp.concatenate(parts, axis=0)


def _unpack_small(packed, like, names):
    out, off = {}, 0
    for k in names:
        size, rows = math.prod(like[k].shape), _small_rows(like[k].shape)
        if size % LANES == 0:
            out[k] = packed[off:off + size // LANES].reshape(like[k].shape)
        else:
            out[k] = packed[off, :size].reshape(like[k].shape)
        off += rows
    return out


def kernel(x, meta_tokens, ffn1_norm, ffn1_w_gate, ffn1_w_up, ffn1_w_down, mix_norm, w_in, attn_sinks, ssm_a_re, ssm_a_im, ssm_log_dt, ssm_b_re, ssm_b_im, ssm_c_re, ssm_c_im, ssm_d, w_attn_proj, w_glu_v, w_glu_g, w_out, ffn2_norm, ffn2_w_gate, ffn2_w_up, ffn2_w_down, final_norm, loss_target, m_meta_tokens, m_ffn1_norm, m_ffn1_w_gate, m_ffn1_w_up, m_ffn1_w_down, m_mix_norm, m_w_in, m_attn_sinks, m_ssm_a_re, m_ssm_a_im, m_ssm_log_dt, m_ssm_b_re, m_ssm_b_im, m_ssm_c_re, m_ssm_c_im, m_ssm_d, m_w_attn_proj, m_w_glu_v, m_w_glu_g, m_w_out, m_ffn2_norm, m_ffn2_w_gate, m_ffn2_w_up, m_ffn2_w_down, m_final_norm, v_meta_tokens, v_ffn1_norm, v_ffn1_w_gate, v_ffn1_w_up, v_ffn1_w_down, v_mix_norm, v_w_in, v_attn_sinks, v_ssm_a_re, v_ssm_a_im, v_ssm_log_dt, v_ssm_b_re, v_ssm_b_im, v_ssm_c_re, v_ssm_c_im, v_ssm_d, v_w_attn_proj, v_w_glu_v, v_w_glu_g, v_w_out, v_ffn2_norm, v_ffn2_w_gate, v_ffn2_w_up, v_ffn2_w_down, v_final_norm):
    args = dict(locals())
    w = {k: args[k] for k in WEIGHTS}
    m = {k: args["m_" + k] for k in WEIGHTS}
    v = {k: args["v_" + k] for k in WEIGHTS}
    depth = ffn1_norm.shape[0]
    seq, d = x.shape[1], x.shape[2]
    lp = seq + BLOCK
    xi, yi, ci = _mesh_pos()
    pos = jnp.stack([ci, 2 * xi + yi]).astype(jnp.int32)

    tabs = _rope_tables(lp)
    layers, pieces = [], [((0, "meta"), meta_tokens)]
    f4 = ffn1_w_gate.shape[2]
    fp = -(-f4 // MXU_DIM) * MXU_DIM

    def ffn_rows(wt):
        return jnp.pad(wt, ((0, fp - f4), (0, 0))).astype(BF16)

    for l in range(depth):
        small = [((l, "w3"), jnp.stack([w_attn_proj[l].T, w_glu_v[l].T, w_glu_g[l].T]).astype(BF16)),
                 ((l, "w_out"), w_out[l].astype(BF16))]
        first = [((l, "wg1"), ffn_rows(ffn1_w_gate[l].T)), ((l, "wu1"), ffn_rows(ffn1_w_up[l].T)),
                 ((l, "wd1"), ffn_rows(ffn1_w_down[l])), ((l, "w_in"), w_in[l].T.astype(BF16))]
        pieces += (first + small if l == 0 else small + first) + [
            ((l, "wg2"), ffn_rows(ffn2_w_gate[l].T)), ((l, "wu2"), ffn_rows(ffn2_w_up[l].T)),
            ((l, "wd2"), ffn_rows(ffn2_w_down[l]))]
        lb_re, lb_im, bb_re, bb_im = _ssm_params(ssm_a_re[l], ssm_a_im[l], ssm_log_dt[l], ssm_b_re[l], ssm_b_im[l])
        ngrp = lb_re.shape[0]
        nt = ngrp // GROUPS_PER_TILE
        ssm_tabs = (lb_re.reshape(nt, 1, TILE_STATES), lb_im.reshape(nt, 1, TILE_STATES),
                    *_ssm_tables(bb_re, bb_im, ssm_c_re[l], ssm_c_im[l]))
        layers.append(dict(
            ffn1_norm=ffn1_norm[l], mix_norm=mix_norm[l], ffn2_norm=ffn2_norm[l], attn_sinks=attn_sinks[l],
            ssm_d=ssm_d[l], ssm_tabs=ssm_tabs, f4=f4))
    ws = _WeightStream(pieces)
    ws.get((0, "wu1"))
    meta_all = ws.get((0, "meta"))
    meta_full = jnp.concatenate([meta_all[j] for j in range(N_CHIPS)], axis=1)

    h = jnp.concatenate([jnp.zeros((PAD_FRONT, d), F32), meta_full, x[0]], axis=0)
    saved = []
    n0 = _rms_fwd(h, ffn1_norm[0], "rms_fwd_first")
    for l in range(depth):
        next_gain = ffn1_norm[l + 1] if l + 1 < depth else None
        h, n0, s = _layer_fwd(h, n0, l, layers[l], next_gain, ws, tabs)
        saved.append(s)
    dh, g_final, loss_acc, dhb = _loss_head(h, final_norm, loss_target[0])
    dh_pair = (dh, dhb)
    loss = lax.psum(loss_acc[0, 0], ("x", "y", "c"))

    grads, reds = [None] * depth, [None] * depth

    def layer_small(gl, l):
        _, vjp = jax.vjp(_ssm_params, ssm_a_re[l], ssm_a_im[l], ssm_log_dt[l], ssm_b_re[l], ssm_b_im[l])
        da_re, da_im, dlog_dt, db_re, db_im = vjp(gl["ssm_lam"])
        first = gl["ffn1_norm"][0] if "ffn1_norm" in gl else jnp.zeros((d,), F32)
        return dict(ffn1_norm=first, mix_norm=gl["mix_norm"][0], attn_sinks=gl["attn_sinks"], ssm_a_re=da_re,
                    ssm_a_im=da_im, ssm_log_dt=dlog_dt, ssm_b_re=db_re, ssm_b_im=db_im, ssm_c_re=gl["ssm_c_re"],
                    ssm_c_im=gl["ssm_c_im"], ssm_d=gl["ssm_d"], ffn2_norm=gl["ffn2_norm"][0])

    class early:
        got, like = None, None

    def early_comm(g0):
        per = [layer_small(g0, 0)] + [layer_small(grads[l], l) for l in range(1, depth)]
        tree = {k: jnp.stack([lay[k] for lay in per]) for k in SMALL if k != "final_norm"}
        tree["final_norm"] = g_final[0]
        early.like = tree
        return _device_gather_comm(_pack_small(tree, SMALL), "small_grads")

    for l in reversed(range(depth)):
        dh_pair, grads[l], reds[l], got = _layer_bwd(
            dh_pair, l, layers[l], ws, saved[l], tabs, pos, early_comm if l == 0 else None)
        if l == 0:
            early.got = got
    dh = dh_pair[0]
    grad_x = dh[BLOCK:][None]
    dmeta_local = dh[PAD_FRONT:BLOCK]

    g_small_tree = _unpack_small(_sum_devices(early.got[0], "sum_small_grads"), early.like, SMALL)
    late_names = ["ffn1_norm", "meta_tokens"]
    late = dict(ffn1_norm=grads[0]["ffn1_norm"], meta_tokens=dmeta_local)
    g_late = _sum_devices(_all_gather_devices(_pack_small(late, late_names), "gather_late_grads"), "sum_late_grads")
    g_late = _unpack_small(g_late, late, late_names)
    g_small_tree["ffn1_norm"] = g_small_tree["ffn1_norm"].at[0].set(g_late["ffn1_norm"][0])
    d4 = d // N_CHIPS
    chip = 2 * xi + yi
    g_meta = lax.dynamic_slice_in_dim(g_late["meta_tokens"], chip * d4, d4, axis=1)

    mine = [[half for red in reds[l] for half in red.end()] for l in range(depth)]
    flat = _share_halves([half for layer_halves in mine for half in layer_halves], "rs_share")
    per_layer = len(mine[0])
    reduced = [(mine[l], flat[l * per_layer:(l + 1) * per_layer]) for l in range(depth)]

    g_out, delta, new_m, new_v = {}, {}, {}, {}
    for i, k in enumerate(BIG):
        flip = (lambda t: jnp.swapaxes(t, 1, 2)) if k in TRANSPOSED else (lambda t: t)
        outs = _adamw_layers(
            flip(w[k]), flip(m[k]), flip(v[k]), [reduced[l][0][i] for l in range(depth)],
            [reduced[l][1][i] for l in range(depth)], pos, "adamw_" + k)
        g_out[k], delta[k], new_m[k], new_v[k] = [flip(t) for t in outs]
    g_small_tree["meta_tokens"] = g_meta
    for k in SMALL + ["meta_tokens"]:
        narrow = w[k].ndim > 2 and w[k].shape[-1] < w[k].shape[-2]
        view = (lambda t: jnp.swapaxes(t, -1, -2)) if narrow else (lambda t: t)
        shape = view(w[k]).shape if w[k].ndim > 1 else (1,) + w[k].shape
        outs = _adamw_whole(view(w[k]).reshape(shape), view(g_small_tree[k]).reshape(shape),
                            view(m[k]).reshape(shape), view(v[k]).reshape(shape), "adamw_" + k)
        g_out[k] = g_small_tree[k]
        delta[k], new_m[k], new_v[k] = [view(t).reshape(w[k].shape) for t in outs]

    return (loss, grad_x, *[g_out[k] for k in WEIGHTS], *[delta[k] for k in WEIGHTS],
            *[new_m[k] for k in WEIGHTS], *[new_v[k] for k in WEIGHTS])
```

```python
import functools
import math

import jax
import jax.numpy as jnp
from jax import lax
from jax.experimental import pallas as pl
from jax.experimental.pallas import tpu as pltpu

F32 = jnp.float32
BF16 = jnp.bfloat16

N_META = 16
HEAD_DIM = 64
N_Q_HEADS = 8
N_KV_HEADS = 2
Q_PER_KV = N_Q_HEADS // N_KV_HEADS
ATTN_WIDTH = N_Q_HEADS * HEAD_DIM
KV_WIDTH = N_KV_HEADS * HEAD_DIM
BLOCK = 128
PAD_FRONT = BLOCK - N_META
ROPE_THETA = 500000.0
ROT_DIM = HEAD_DIM // 4
SSM_GROUP = 16
SSM_STATE = 64
GROUPS_PER_TILE = 4
TILE_STATES = GROUPS_PER_TILE * SSM_STATE
LANES = 128
SUBLANES = 8
MXU_DIM = 256
PACK_ROWS = 256
EPS = 1e-6
NEG_INF = -1e30
N_CHIPS = 4

ADAM_LR = 0.001
ADAM_B1 = 0.9
ADAM_B2 = 0.999
ADAM_EPS = 1e-08
ADAM_WD = 0.01
ADAM_STEP = 10

VMEM_LIMIT = 56 * 1024 * 1024
MESH = pl.DeviceIdType.MESH


def _cparams(sem=None):
    return pltpu.CompilerParams(dimension_semantics=sem, vmem_limit_bytes=VMEM_LIMIT)


def _row_tile(rows, limit=512):
    best = None
    for t in range(128, limit + 1, 128):
        if rows % t == 0:
            best = t
    assert best is not None, rows
    return best


def _div_tile(rows, row_bytes, max_bytes=1 << 20, mult=8):
    best = None
    for t in range(mult, rows + 1, mult):
        if rows % t == 0 and t * row_bytes <= max_bytes:
            best = t
    if best is None:
        best = rows
    return best


def _dot(a, b, mode):
    if mode == "nn":
        dims = (((1,), (0,)), ((), ()))
    elif mode == "nt":
        dims = (((1,), (1,)), ((), ()))
    else:
        dims = (((0,), (0,)), ((), ()))
    return lax.dot_general(a.astype(BF16), b.astype(BF16), dims, preferred_element_type=F32)


def _sigmoid(x):
    return 1.0 / (1.0 + jnp.exp(-x))


_GELU_C = math.sqrt(2.0 / math.pi)


def _gelu(x):
    return 0.5 * x * (1.0 + jnp.tanh(_GELU_C * (x + 0.044715 * x * x * x)))


def _gelu_grad(x):
    t = jnp.tanh(_GELU_C * (x + 0.044715 * x * x * x))
    return 0.5 * (1.0 + t) + 0.5 * x * (1.0 - t * t) * _GELU_C * (1.0 + 3.0 * 0.044715 * x * x)


class _Comm:
    def __init__(self, tag, ins, out_shapes, sems, start, mid, finish):
        self.tag, self.ins, self.out_shapes, self.sems = tag, list(ins), list(out_shapes), list(sems)
        self.start, self.mid, self.finish = start, mid, finish


HBM_SPEC = pl.BlockSpec(memory_space=pltpu.HBM)
MID_NUM, MID_DEN = 4, 5


def _hosted_call(body, comm, *, out_shape, grid, in_specs, out_specs, scratch_shapes, sem, name, args):
    out_shape, in_specs, out_specs = list(out_shape), list(in_specs), list(out_specs)
    scratch_shapes = list(scratch_shapes)
    if comm is None:
        res = pl.pallas_call(
            body, out_shape=out_shape, grid=grid, in_specs=in_specs, out_specs=out_specs,
            scratch_shapes=scratch_shapes, compiler_params=_cparams(sem), name=name)(*args)
        return list(res), []
    n_in, n_out, n_sc = len(args), len(out_shape), len(scratch_shapes)
    nci, nco = len(comm.ins), len(comm.out_shapes)
    total = math.prod(grid)

    def wrapped(*refs):
        in_refs, cin = refs[:n_in], refs[n_in:n_in + nci]
        o0 = n_in + nci
        out_refs, cout = refs[o0:o0 + n_out], refs[o0 + n_out:o0 + n_out + nco]
        s0 = o0 + n_out + nco
        sc, csem = refs[s0:s0 + n_sc], refs[s0 + n_sc:]
        lin = 0
        for dim, size in enumerate(grid):
            lin = lin * size + pl.program_id(dim)

        @pl.when(lin == 0)
        def _():
            comm.start(cin, cout, csem)

        if comm.mid is not None:
            @pl.when(lin == (total * MID_NUM) // MID_DEN)
            def _():
                comm.mid(cin, cout, csem)

        body(*in_refs, *out_refs, *sc)

        @pl.when(lin == total - 1)
        def _():
            comm.finish(cin, cout, csem)

    res = pl.pallas_call(
        wrapped, out_shape=out_shape + comm.out_shapes, grid=grid,
        in_specs=in_specs + [HBM_SPEC] * nci, out_specs=out_specs + [HBM_SPEC] * nco,
        scratch_shapes=scratch_shapes + comm.sems,
        compiler_params=_cparams(("arbitrary",) * len(grid)), name=name + "_" + comm.tag)(*args, *comm.ins)
    return list(res[:n_out]), list(res[n_out:])


def _matmul(name, grid, k_axis, ins, in_specs, pairs, acc_shapes, epilogue, out_shapes, out_specs, sem, comm=None):
    n_in, n_out, n_acc = len(ins), len(out_shapes), len(acc_shapes)

    def body(*refs):
        in_refs = refs[:n_in]
        out_refs = refs[n_in:n_in + n_out]
        acc_refs = refs[n_in + n_out:]
        if k_axis is None:
            accs = [None] * n_acc
            for ia, ib, mode, iacc in pairs:
                d = _dot(in_refs[ia][...], in_refs[ib][...], mode)
                accs[iacc] = d if accs[iacc] is None else accs[iacc] + d
            epilogue(accs, in_refs, out_refs)
            return
        k = pl.program_id(k_axis)

        @pl.when(k == 0)
        def _():
            for r in acc_refs:
                r[...] = jnp.zeros_like(r)

        for ia, ib, mode, iacc in pairs:
            acc_refs[iacc][...] += _dot(in_refs[ia][...], in_refs[ib][...], mode)

        @pl.when(k == pl.num_programs(k_axis) - 1)
        def _():
            epilogue([r[...] for r in acc_refs], in_refs, out_refs)

    scratch = [] if k_axis is None else [pltpu.VMEM(s, F32) for s in acc_shapes]
    outs, couts = _hosted_call(
        body, comm, out_shape=out_shapes, grid=grid, in_specs=in_specs, out_specs=out_specs,
        scratch_shapes=scratch, sem=sem, name=name, args=ins)
    return outs if comm is None else (outs, couts)


def _rms_math(x, g):
    r = lax.rsqrt(jnp.mean(x * x, axis=-1, keepdims=True) + EPS)
    return (x * r * g).astype(BF16)


def _rms_fwd(h, g, name):
    lp, d = h.shape
    tm = _row_tile(lp)

    def body(h_ref, g_ref, n_ref):
        n_ref[...] = _rms_math(h_ref[...], g_ref[...])

    return pl.pallas_call(
        body, out_shape=jax.ShapeDtypeStruct((lp, d), BF16), grid=(lp // tm,),
        in_specs=[pl.BlockSpec((tm, d), lambda i: (i, 0)), pl.BlockSpec((1, d), lambda i: (0, 0))],
        out_specs=pl.BlockSpec((tm, d), lambda i: (i, 0)),
        compiler_params=_cparams(("parallel",)), name=name)(h, g.reshape(1, d))


def _rms_bwd_math(dn, x, g):
    r = lax.rsqrt(jnp.mean(x * x, axis=-1, keepdims=True) + EPS)
    xh = x * r
    dxh = dn * g
    dx = r * (dxh - xh * jnp.mean(dxh * xh, axis=-1, keepdims=True))
    return dx, dn * xh


def _ffn_up(n, wgt, wut, comm=None):
    lp, d = n.shape
    fp = wgt.shape[1]
    tm = _row_tile(lp)

    def up_body(n_ref, wg_ref, wu_ref, a_ref, b_ref, s_ref):
        x = n_ref[...]
        for jc in range(N_CHIPS):
            cols = slice(jc * fp, (jc + 1) * fp)
            a = _dot(x, wg_ref[jc], "nt")
            b = _dot(x, wu_ref[jc], "nt")
            a_ref[:, cols] = a.astype(BF16)
            b_ref[:, cols] = b.astype(BF16)
            s_ref[:, cols] = (a * _sigmoid(a) * b).astype(BF16)

    ff = N_CHIPS * fp
    act = jax.ShapeDtypeStruct((lp, ff), BF16)
    act_tile = pl.BlockSpec((tm, ff), lambda i: (i, 0))
    w_spec = pl.BlockSpec((N_CHIPS, fp, d), lambda i: (0, 0, 0))
    outs, couts = _hosted_call(
        up_body, comm, out_shape=[act, act, act], grid=(lp // tm,),
        in_specs=[pl.BlockSpec((tm, d), lambda i: (i, 0)), w_spec, w_spec],
        out_specs=[act_tile] * 3, scratch_shapes=[], sem=("parallel",), name="ffn_up", args=(n, wgt, wut))
    return (*outs, n), couts


def _residual_outputs(h_new, in_refs, out_refs, gain_at):
    out_refs[0][...] = h_new
    if gain_at is not None:
        out_refs[1][...] = _rms_math(h_new, in_refs[gain_at][...])


def _residual_specs(lp, d, tm, next_gain):
    row = pl.BlockSpec((tm, d), lambda i: (i, 0))
    shapes, specs = [jax.ShapeDtypeStruct((lp, d), F32)], [row]
    extra_in, extra_specs = [], []
    if next_gain is not None:
        shapes.append(jax.ShapeDtypeStruct((lp, d), BF16))
        specs.append(row)
        extra_in, extra_specs = [next_gain.reshape(1, d)], [pl.BlockSpec((1, d), lambda i: (0, 0))]
    return shapes, specs, extra_in, extra_specs


def _ffn_down(s, wd, h, next_gain, comm=None):
    lp, d = h.shape
    ff = s.shape[1]
    tm = _row_tile(lp)
    shapes, specs, extra_in, extra_specs = _residual_specs(lp, d, tm, next_gain)

    def down_epi(accs, in_refs, out_refs):
        _residual_outputs(in_refs[2][...] + 0.5 * accs[0], in_refs, out_refs, 3 if extra_in else None)

    res = _matmul(
        "ffn_down", (lp // tm,), None, [s, wd.reshape(ff, d), h] + extra_in,
        [pl.BlockSpec((tm, ff), lambda i: (i, 0)), pl.BlockSpec((ff, d), lambda i: (0, 0)),
         pl.BlockSpec((tm, d), lambda i: (i, 0))] + extra_specs,
        [(0, 1, "nn", 0)], [(tm, d)], down_epi, shapes, specs, ("parallel",), comm)
    return (res, []) if comm is None else res


def _tn_tiles(lp):
    return _row_tile(lp, 1408)


def _ffn_bwd(dh_pair, h_in, gain, wgt, wut, wd, f4, saved, pos, comm=None, comm2=None):
    dh, dhb = dh_pair
    a, b, s, n = saved
    lp, d = h_in.shape
    fp = wgt.shape[1]
    ff = N_CHIPS * fp
    tm = _row_tile(lp)
    ni = lp // tm
    tk = _tn_tiles(lp)
    nk = lp // tk

    def ds_body(dh_ref, wd_ref, a_ref, b_ref, da_ref, db_ref):
        x = dh_ref[...]
        for jc in range(N_CHIPS):
            cols = slice(jc * fp, (jc + 1) * fp)
            ds = 0.5 * _dot(x, wd_ref[jc], "nt")
            av = a_ref[:, cols].astype(F32)
            bv = b_ref[:, cols].astype(F32)
            sg = _sigmoid(av)
            da_ref[:, cols] = (ds * bv * sg * (1.0 + av * (1.0 - sg))).astype(BF16)
            db_ref[:, cols] = (ds * av * sg).astype(BF16)

    act = jax.ShapeDtypeStruct((lp, ff), BF16)
    act_tile = pl.BlockSpec((tm, ff), lambda i: (i, 0))
    (da, db), couts = _hosted_call(
        ds_body, comm, out_shape=[act, act], grid=(ni,),
        in_specs=[pl.BlockSpec((tm, d), lambda i: (i, 0)), pl.BlockSpec((N_CHIPS, fp, d), lambda i: (0, 0, 0)),
                  act_tile, act_tile],
        out_specs=[act_tile, act_tile], scratch_shapes=[], sem=("parallel",), name="ffn_bwd_ds",
        args=(dhb, wd, a, b))

    dw_shape = jax.ShapeDtypeStruct((N_CHIPS, f4, d), F32)
    dw_spec = pl.BlockSpec((None, f4, d), lambda j, k: (j, 0, 0))
    in_col = pl.BlockSpec((tk, fp), lambda j, k: (k, j))
    in_row = pl.BlockSpec((tk, d), lambda j, k: (k, 0))

    half_shape = jax.ShapeDtypeStruct((N_CHIPS, f4 // 2, d), BF16)
    half_spec = pl.BlockSpec((None, f4 // 2, d), lambda j, k: (j, 0, 0))

    def dwd_epi(accs, in_refs, out_refs):
        dw = 0.5 * accs[0]
        out_refs[0][...] = dw[:f4]
        out_refs[1][...] = _sibling_half(dw, f4)

    res = _matmul(
        "ffn_dwd", (N_CHIPS, nk), 1, [s, dhb], [in_col, in_row],
        [(0, 1, "tn", 0)], [(fp, d)], dwd_epi, [dw_shape, half_shape], [dw_spec, half_spec],
        ("arbitrary", "arbitrary"), comm2)
    (dwd, dwd_other), couts2 = (res, []) if comm2 is None else res

    def dwgu_epi(accs, in_refs, out_refs):
        for i, acc in enumerate(accs):
            out_refs[i][...] = acc[:f4]
            out_refs[2 + i][...] = _sibling_half(acc, f4)

    red_down = _Reduction([dwd], [dwd_other], pos, "ffn_d")
    (dwg, dwu, dwg_other, dwu_other), red_down.got = _matmul(
        "ffn_dwgu", (N_CHIPS, nk), 1, [n, da, db], [in_row, in_col, in_col],
        [(1, 0, "tn", 0), (2, 0, "tn", 1)], [(fp, d)] * 2, dwgu_epi,
        [dw_shape, dw_shape, half_shape, half_shape], [dw_spec, dw_spec, half_spec, half_spec],
        ("arbitrary", "arbitrary"), red_down.comm())

    def dn_epi(accs, in_refs, out_refs):
        i = pl.program_id(0)
        dx, dgrow = _rms_bwd_math(accs[0], in_refs[5][...], in_refs[6][...])
        dh_new = in_refs[4][...] + dx
        out_refs[0][...] = dh_new
        out_refs[2][...] = dh_new.astype(BF16)

        @pl.when(i == 0)
        def _():
            out_refs[1][...] = jnp.zeros_like(out_refs[1])

        out_refs[1][...] += jnp.sum(dgrow, axis=0, keepdims=True)

    red = _Reduction([dwg, dwu], [dwg_other, dwu_other], pos, "ffn_gu")
    row_spec = pl.BlockSpec((tm, d), lambda i: (i, 0))
    act_spec = pl.BlockSpec((tm, ff), lambda i: (i, 0))
    w_spec = pl.BlockSpec((ff, d), lambda i: (0, 0))
    one_spec = pl.BlockSpec((1, d), lambda i: (0, 0))
    (dh_in, dgain, dh_in_b), red.got = _matmul(
        "ffn_bwd_dn", (ni,), None, [da, wgt.reshape(ff, d), db, wut.reshape(ff, d), dh, h_in, gain.reshape(1, d)],
        [act_spec, w_spec, act_spec, w_spec, row_spec, row_spec, one_spec],
        [(0, 1, "nn", 0), (2, 3, "nn", 0)], [(tm, d)], dn_epi,
        [jax.ShapeDtypeStruct((lp, d), F32), jax.ShapeDtypeStruct((1, d), F32), jax.ShapeDtypeStruct((lp, d), BF16)],
        [row_spec, one_spec, row_spec], ("arbitrary",), red.comm())
    return (dh_in, dh_in_b), dgain, [red, red_down], couts, couts2


def _rope_tables(lp):
    pos = jnp.arange(lp, dtype=F32) - float(PAD_FRONT)
    inv_freq = ROPE_THETA ** (-jnp.arange(0, ROT_DIM, 2, dtype=F32) / ROT_DIM)
    ang = pos[:, None] * inv_freq[None, :]
    cos, sin = jnp.cos(ang), jnp.sin(ang)
    half = ROT_DIM // 2
    ones = jnp.ones((lp, HEAD_DIM - ROT_DIM), F32)
    zeros_h = jnp.zeros((lp, half), F32)
    zeros_r = jnp.zeros((lp, HEAD_DIM - ROT_DIM), F32)
    c = jnp.concatenate([cos, cos, ones], axis=1)
    s1 = jnp.concatenate([-sin, zeros_h, zeros_r], axis=1)
    s2 = jnp.concatenate([zeros_h, sin, zeros_r], axis=1)
    reps = LANES // HEAD_DIM
    return jnp.stack([jnp.tile(c, (1, reps)), jnp.tile(s1, (1, reps)), jnp.tile(s2, (1, reps))])


def _rope(x, c, s1, s2):
    half = ROT_DIM // 2
    outs = []
    for ch in range(x.shape[1] // LANES):
        xc = x[:, ch * LANES:(ch + 1) * LANES]
        outs.append(xc * c + pltpu.roll(xc, LANES - half, 1) * s1 + pltpu.roll(xc, half, 1) * s2)
    return outs[0] if len(outs) == 1 else jnp.concatenate(outs, axis=1)


def _rope_t(dy, c, s1, s2):
    half = ROT_DIM // 2
    outs = []
    for ch in range(dy.shape[1] // LANES):
        dc = dy[:, ch * LANES:(ch + 1) * LANES]
        outs.append(dc * c + pltpu.roll(dc * s1, half, 1) + pltpu.roll(dc * s2, LANES - half, 1))
    return outs[0] if len(outs) == 1 else jnp.concatenate(outs, axis=1)


def _in_proj(n, w_in, tabs, ssm_w, comm=None):
    lp, d = n.shape
    inw = w_in.shape[0]
    tm = _row_tile(lp)
    o1 = ATTN_WIDTH
    o2 = o1 + KV_WIDTH
    o3 = o2 + KV_WIDTH
    o4 = o3 + ssm_w
    o5 = o4 + d

    def epi(accs, in_refs, out_refs):
        z = accs[0]
        c, s1, s2 = in_refs[2][0], in_refs[2][1], in_refs[2][2]
        out_refs[0][...] = _rope(z[:, :o1], c, s1, s2).astype(BF16)
        out_refs[1][...] = _rope(z[:, o1:o2], c, s1, s2).astype(BF16)
        out_refs[2][...] = z[:, o2:o3].astype(BF16)
        out_refs[3][...] = z[:, o3:o4]
        out_refs[4][...] = z[:, o4:o5]
        out_refs[5][...] = z[:, o5:]

    def rs(w, dt):
        return jax.ShapeDtypeStruct((lp, w), dt), pl.BlockSpec((tm, w), lambda i: (i, 0))

    shapes, specs = zip(rs(o1, BF16), rs(KV_WIDTH, BF16), rs(KV_WIDTH, BF16), rs(ssm_w, F32), rs(d, F32), rs(d, F32))
    res = _matmul(
        "mix_in_proj", (lp // tm,), None, [n, w_in, tabs],
        [pl.BlockSpec((tm, d), lambda i: (i, 0)), pl.BlockSpec((inw, d), lambda i: (0, 0)),
         pl.BlockSpec((3, tm, LANES), lambda i: (0, i, 0))],
        [(0, 1, "nt", 0)], [(tm, inw)], epi, list(shapes), list(specs), ("parallel",), comm)
    return (res, []) if comm is None else res


def _attn_mask(b):
    rows = lax.broadcasted_iota(jnp.int32, (BLOCK, 3 * BLOCK), 0)
    cols = lax.broadcasted_iota(jnp.int32, (BLOCK, 3 * BLOCK), 1)
    qpos = b * BLOCK + rows - PAD_FRONT
    kpos = (b - 1) * BLOCK + cols - PAD_FRONT
    dist = qpos - kpos
    band = (cols < 2 * BLOCK) & (kpos >= N_META) & (dist >= 0) & (dist < BLOCK)
    mrow = cols - 2 * BLOCK
    meta = (mrow >= PAD_FRONT) & ((mrow - PAD_FRONT) <= qpos)
    return band | meta


def _attn_probs(qh, kk, mask, sink):
    s = _dot(qh, kk, "nt") * (HEAD_DIM ** -0.5)
    s = jnp.where(mask, s, NEG_INF)
    m = jnp.maximum(jnp.max(s, axis=-1, keepdims=True), sink)
    e = jnp.exp(s - m)
    es = jnp.exp(sink - m)
    z = jnp.sum(e, axis=-1, keepdims=True) + es
    inv = 1.0 / z
    return e * inv, es * inv


def _head(ref_or_val, h):
    return ref_or_val[:, h * HEAD_DIM:(h + 1) * HEAD_DIM]


def _attn_fwd(q, k, v, sinks, comm=None):
    lp = q.shape[0]
    nb = lp // BLOCK

    def body(sink_ref, q_ref, kp_ref, kc_ref, km_ref, vp_ref, vc_ref, vm_ref, o_ref):
        b = pl.program_id(0)
        mask = _attn_mask(b)
        for hk in range(N_KV_HEADS):
            kk = jnp.concatenate([_head(kp_ref, hk), _head(kc_ref, hk), _head(km_ref, hk)], axis=0)
            vv = jnp.concatenate([_head(vp_ref, hk), _head(vc_ref, hk), _head(vm_ref, hk)], axis=0)
            for g in range(Q_PER_KV):
                h = hk * Q_PER_KV + g
                for rows in (slice(0, BLOCK // 2), slice(BLOCK // 2, BLOCK)):
                    p, _ = _attn_probs(q_ref[rows, h * HEAD_DIM:(h + 1) * HEAD_DIM], kk, mask[rows], sink_ref[h])
                    o_ref[rows, h * HEAD_DIM:(h + 1) * HEAD_DIM] = _dot(p, vv, "nn").astype(BF16)

    cur = lambda b: (b, 0)
    prev = lambda b: (jnp.maximum(b - 1, 0), 0)
    first = lambda b: (0, 0)
    kvs = lambda f: pl.BlockSpec((BLOCK, KV_WIDTH), f)
    (o,), couts = _hosted_call(
        body, comm, out_shape=[jax.ShapeDtypeStruct((lp, ATTN_WIDTH), BF16)], grid=(nb,),
        in_specs=[pl.BlockSpec(memory_space=pltpu.SMEM), pl.BlockSpec((BLOCK, ATTN_WIDTH), cur),
                  kvs(prev), kvs(cur), kvs(first), kvs(prev), kvs(cur), kvs(first)],
        out_specs=[pl.BlockSpec((BLOCK, ATTN_WIDTH), cur)], scratch_shapes=[],
        sem=("parallel",), name="attn_fwd", args=(sinks, q, k, k, k, v, v, v))
    return o, couts


def _attn_bwd(q, k, v, do, sinks, tabs, comm=None):
    lp = q.shape[0]
    nb = lp // BLOCK
    scale = HEAD_DIM ** -0.5

    def body(sink_ref, q_ref, do_ref, kp_ref, kc_ref, km_ref, vp_ref, vc_ref, vm_ref, tq_ref, tk_ref, t0_ref,
             dq_ref, dk_ref, dv_ref, dkm_ref, dvm_ref, dsink_ref,
             dq_s, dkk_s, dvv_s, ck_s, cv_s, mk_s, mv_s):
        b = pl.program_id(0)

        @pl.when(b == 0)
        def _():
            for r in (ck_s, cv_s, mk_s, mv_s, dsink_ref):
                r[...] = jnp.zeros_like(r)

        @pl.when(b < nb)
        def _():
            mask = _attn_mask(b)
            for hk in range(N_KV_HEADS):
                kk = jnp.concatenate([_head(kp_ref, hk), _head(kc_ref, hk), _head(km_ref, hk)], axis=0)
                vv = jnp.concatenate([_head(vp_ref, hk), _head(vc_ref, hk), _head(vm_ref, hk)], axis=0)
                dkk = jnp.zeros((3 * BLOCK, HEAD_DIM), F32)
                dvv = jnp.zeros((3 * BLOCK, HEAD_DIM), F32)
                for g in range(Q_PER_KV):
                    h = hk * Q_PER_KV + g
                    qh = _head(q_ref, h)
                    doh = _head(do_ref, h)
                    p, ps = _attn_probs(qh, kk, mask, sink_ref[h])
                    dp = _dot(doh, vv, "nt")
                    delta = jnp.sum(p * dp, axis=-1, keepdims=True)
                    ds = (p * (dp - delta)).astype(BF16)
                    dsink_ref[h:h + 1, :] += jnp.zeros((1, LANES), F32) - jnp.sum(ps * delta)
                    dq_s[:, h * HEAD_DIM:(h + 1) * HEAD_DIM] = _dot(ds, kk, "nn") * scale
                    dkk = dkk + _dot(ds, qh, "tn") * scale
                    dvv = dvv + _dot(p, doh, "tn")
                dkk_s[:, hk * HEAD_DIM:(hk + 1) * HEAD_DIM] = dkk
                dvv_s[:, hk * HEAD_DIM:(hk + 1) * HEAD_DIM] = dvv
            dq_ref[...] = _rope_t(dq_s[...], tq_ref[0], tq_ref[1], tq_ref[2])
            dk_ref[...] = _rope_t(ck_s[...] + dkk_s[0:BLOCK, :], tk_ref[0], tk_ref[1], tk_ref[2])
            dv_ref[...] = cv_s[...] + dvv_s[0:BLOCK, :]
            ck_s[...] = dkk_s[BLOCK:2 * BLOCK, :]
            cv_s[...] = dvv_s[BLOCK:2 * BLOCK, :]
            mk_s[...] += dkk_s[2 * BLOCK:, :]
            mv_s[...] += dvv_s[2 * BLOCK:, :]

        @pl.when(b == nb)
        def _():
            dk_ref[...] = _rope_t(ck_s[...], tk_ref[0], tk_ref[1], tk_ref[2])
            dv_ref[...] = cv_s[...]
            dkm_ref[...] = _rope_t(mk_s[...], t0_ref[0], t0_ref[1], t0_ref[2])
            dvm_ref[...] = mv_s[...]

    cur = lambda b: (jnp.minimum(b, nb - 1), 0)
    prev = lambda b: (jnp.clip(b - 1, 0, nb - 1), 0)
    first = lambda b: (0, 0)
    kvs = lambda f: pl.BlockSpec((BLOCK, KV_WIDTH), f)
    tab = lambda f: pl.BlockSpec((3, BLOCK, LANES), lambda b: (0,) + f(b)[:1] + (0,))
    kv_out = lambda b: (jnp.maximum(b - 1, 0), 0)
    return _hosted_call(
        body, comm,
        out_shape=[jax.ShapeDtypeStruct((lp, ATTN_WIDTH), F32), jax.ShapeDtypeStruct((lp, KV_WIDTH), F32),
                   jax.ShapeDtypeStruct((lp, KV_WIDTH), F32), jax.ShapeDtypeStruct((BLOCK, KV_WIDTH), F32),
                   jax.ShapeDtypeStruct((BLOCK, KV_WIDTH), F32), jax.ShapeDtypeStruct((N_Q_HEADS, LANES), F32)],
        grid=(nb + 1,),
        in_specs=[pl.BlockSpec(memory_space=pltpu.SMEM), pl.BlockSpec((BLOCK, ATTN_WIDTH), cur),
                  pl.BlockSpec((BLOCK, ATTN_WIDTH), cur),
                  kvs(prev), kvs(cur), kvs(first), kvs(prev), kvs(cur), kvs(first),
                  tab(cur), tab(kv_out), tab(first)],
        out_specs=[pl.BlockSpec((BLOCK, ATTN_WIDTH), cur), kvs(kv_out), kvs(kv_out), kvs(first), kvs(first),
                   pl.BlockSpec((N_Q_HEADS, LANES), first)],
        scratch_shapes=[pltpu.VMEM((BLOCK, ATTN_WIDTH), F32), pltpu.VMEM((3 * BLOCK, KV_WIDTH), F32),
                        pltpu.VMEM((3 * BLOCK, KV_WIDTH), F32), pltpu.VMEM((BLOCK, KV_WIDTH), F32),
                        pltpu.VMEM((BLOCK, KV_WIDTH), F32), pltpu.VMEM((BLOCK, KV_WIDTH), F32),
                        pltpu.VMEM((BLOCK, KV_WIDTH), F32)],
        sem=("arbitrary",), name="attn_bwd", args=(sinks, q, do, k, k, k, v, v, v, tabs, tabs, tabs))


def _cmul(ar, ai, br, bi):
    return ar * br - ai * bi, ar * bi + ai * br


def _cpow(lr, li, n):
    rr = ri = None
    br, bi = lr, li
    while n:
        if n & 1:
            rr, ri = (br, bi) if rr is None else _cmul(rr, ri, br, bi)
        n >>= 1
        if n:
            br, bi = _cmul(br, bi, br, bi)
    return rr, ri


def _shift_rows(x, d, reverse):
    rows = lax.broadcasted_iota(jnp.int32, x.shape, 0)
    if not reverse:
        return jnp.where(rows >= d, pltpu.roll(x, d, 0), 0.0)
    return jnp.where(rows < SUBLANES - d, pltpu.roll(x, SUBLANES - d, 0), 0.0)


def _sublane_powers(mr, mi, reverse):
    rows = lax.broadcasted_iota(jnp.int32, mr.shape, 0)
    e = SUBLANES - 1 - rows if reverse else rows
    pr, pi = jnp.ones_like(mr), jnp.zeros_like(mr)
    br, bi = mr, mi
    for d in (1, 2, 4):
        tr, ti = _cmul(pr, pi, br, bi)
        on = (e & d) != 0
        pr, pi = jnp.where(on, tr, pr), jnp.where(on, ti, pi)
        if d < 4:
            br, bi = _cmul(br, bi, br, bi)
    return pr, pi


def _inclusive_prefix(er, ei, mr, mi, reverse):
    ir, ii, pr, pi = er, ei, mr, mi
    for d in (1, 2, 4):
        tr, ti = _cmul(pr, pi, _shift_rows(ir, d, reverse), _shift_rows(ii, d, reverse))
        ir, ii = ir + tr, ii + ti
        if d < 4:
            pr, pi = _cmul(pr, pi, pr, pi)
    return ir, ii


def _chain_rows(a, t, seg):
    return pl.ds(a * SUBLANES * seg + t, SUBLANES, stride=seg)


def _seg_scan(xr_ref, xi_ref, lam, seg, nchain, reverse, store, init, extra=None):
    nt = len(lam)
    acc0 = () if extra is None else extra[1]

    def step(i, carry):
        hs, acc = carry
        t = seg - 1 - i if reverse else i
        out = []
        for a in range(nchain):
            sl = _chain_rows(a, t, seg)
            for j in range(nt):
                lr, li = lam[j]
                k = 2 * (a * nt + j)
                hr, hi = hs[k], hs[k + 1]
                nr = lr * hr - li * hi + xr_ref[j, sl, :]
                ni = lr * hi + li * hr + xi_ref[j, sl, :]
                if store:
                    xr_ref[j, sl, :] = nr
                    xi_ref[j, sl, :] = ni
                if extra is not None:
                    acc = extra[0](t, a, j, nr, ni, acc)
                out += [nr, ni]
        return tuple(out), acc

    return lax.fori_loop(0, seg, step, (tuple(init), acc0))


def _ssm_scan(xr_ref, xi_ref, lam, seg, nchain, reverse, extra=None):
    nt = len(lam)
    zero = [jnp.zeros((SUBLANES, LANES), F32)] * (2 * nt * nchain)
    ends, _ = _seg_scan(xr_ref, xi_ref, lam, seg, nchain, reverse, False, zero)
    init = [None] * (2 * nt * nchain)
    last = 0 if reverse else SUBLANES - 1
    for j in range(nt):
        mr, mi = _cpow(lam[j][0], lam[j][1], seg)
        m8r, m8i = _cpow(mr, mi, SUBLANES)
        pwr, pwi = _sublane_powers(mr, mi, reverse)
        gr = gi = jnp.zeros((SUBLANES, LANES), F32)
        for a in (reversed(range(nchain)) if reverse else range(nchain)):
            k = 2 * (a * nt + j)
            incr, inci = _inclusive_prefix(ends[k], ends[k + 1], mr, mi, reverse)
            tr, ti = _cmul(pwr, pwi, gr, gi)
            init[k] = _shift_rows(incr, 1, reverse) + tr
            init[k + 1] = _shift_rows(inci, 1, reverse) + ti
            g2r, g2i = _cmul(m8r, m8i, gr, gi)
            gr = g2r + jnp.broadcast_to(incr[last:last + 1, :], gr.shape)
            gi = g2i + jnp.broadcast_to(inci[last:last + 1, :], gi.shape)
    _, acc = _seg_scan(xr_ref, xi_ref, lam, seg, nchain, reverse, True, init, extra)
    return acc


def _diag_mask():
    steps = LANES // SSM_GROUP // GROUPS_PER_TILE
    return (jnp.eye(steps, dtype=F32)[:, None, :, None] * jnp.eye(GROUPS_PER_TILE, dtype=F32)[None, :, None, :])


def _ssm_tables(bb_re, bb_im, c_re, c_im):
    g = bb_re.shape[0]
    nt = g // GROUPS_PER_TILE
    steps = LANES // SSM_GROUP // GROUPS_PER_TILE
    mask = _diag_mask()

    def b_tab(bb):
        x = bb.reshape(nt // steps, steps, GROUPS_PER_TILE, SSM_STATE, SSM_GROUP)
        x = jnp.transpose(x, (0, 1, 4, 2, 3))[:, :, None, None]
        m = jnp.transpose(mask, (0, 2, 3, 1))[None, :, :, :, None, :, None]
        return (x * m).reshape(nt, LANES, TILE_STATES)

    def c_tab(c):
        x = c.reshape(nt // steps, steps, GROUPS_PER_TILE, SSM_GROUP, SSM_STATE)
        x = jnp.transpose(x, (0, 1, 2, 4, 3))[:, :, :, :, None, None]
        m = mask[None, :, :, None, :, :, None]
        return (x * m).reshape(nt, TILE_STATES, LANES)

    return b_tab(bb_re), b_tab(bb_im), c_tab(c_re), c_tab(c_im)


def _ssm_untable_b(db, g):
    nt = g // GROUPS_PER_TILE
    steps = LANES // SSM_GROUP // GROUPS_PER_TILE
    x = db.reshape(nt // steps, steps, GROUPS_PER_TILE, SSM_STATE, steps, GROUPS_PER_TILE, SSM_GROUP)
    m = _diag_mask()[None, :, :, None, :, :, None]
    return jnp.sum(x * m, axis=(4, 5)).reshape(g, SSM_STATE, SSM_GROUP)


def _ssm_untable_c(dc, g):
    nt = g // GROUPS_PER_TILE
    steps = LANES // SSM_GROUP // GROUPS_PER_TILE
    x = dc.reshape(nt // steps, steps, steps, GROUPS_PER_TILE, SSM_GROUP, GROUPS_PER_TILE, SSM_STATE)
    m = jnp.transpose(_diag_mask(), (0, 2, 3, 1))[None, :, :, :, None, :, None]
    out = jnp.sum(x * m, axis=(2, 3))
    return jnp.transpose(out, (0, 1, 3, 2, 4)).reshape(g, SSM_GROUP, SSM_STATE)


def _lam_tiles(lam_ref):
    out = []
    for j in range(TILE_STATES // LANES):
        out.append(jnp.broadcast_to(lam_ref[:, j * LANES:(j + 1) * LANES], (SUBLANES, LANES)))
    return out


def _scan_chains(lp):
    for n in (4, 2, 1):
        if lp % (SUBLANES * n) == 0 and (lp // SUBLANES) % 16 == 0:
            return n
    raise ValueError(lp)


def _split_tiles(dst_ref, rows, val):
    for j in range(val.shape[1] // LANES):
        dst_ref[j, rows, :] = val[:, j * LANES:(j + 1) * LANES]


def _cat_tiles(src_ref, rows):
    njt = src_ref.shape[0]
    return jnp.concatenate([src_ref[j, rows, :] for j in range(njt)], axis=1).astype(BF16)


def _ssm_fwd(u, lam_re, lam_im, tb_re, tb_im, tc_re, tc_im, d_skip, comm=None):
    lp, w = u.shape
    nt = tb_re.shape[0]
    nchain = _scan_chains(lp)
    seg = lp // (SUBLANES * nchain)
    chunk = lp // SUBLANES
    njt = TILE_STATES // LANES

    def body(u_ref, lr_ref, li_ref, br_ref, bi_ref, cr_ref, ci_ref, d_ref, y_ref, yg_ref, xr, xi):
        t = pl.program_id(0)
        for s in range(SUBLANES):
            rs = pl.ds(s * chunk, chunk)
            ub = u_ref[rs, :].astype(BF16)
            _split_tiles(xr, rs, _dot(ub, br_ref[...], "nn"))
            _split_tiles(xi, rs, _dot(ub, bi_ref[...], "nn"))
        lrs, lis = _lam_tiles(lr_ref), _lam_tiles(li_ref)
        _ssm_scan(xr, xi, list(zip(lrs, lis)), seg, nchain, False)
        for s in range(SUBLANES):
            rs = pl.ds(s * chunk, chunk)
            y = _dot(_cat_tiles(xr, rs), cr_ref[...], "nn") - _dot(_cat_tiles(xi, rs), ci_ref[...], "nn")

            @pl.when(t % 2 == 0)
            def _():
                y_ref[rs, :] = y + d_ref[...] * u_ref[rs, :]

            @pl.when(t % 2 == 1)
            def _():
                total = y_ref[rs, :] + y
                y_ref[rs, :] = total
                yg_ref[rs, :] = _gelu(total).astype(BF16)

    blk = pl.BlockSpec((lp, LANES), lambda t: (0, t // 2))
    lam_spec = pl.BlockSpec((None, 1, TILE_STATES), lambda t: (t, 0, 0))
    b_spec = pl.BlockSpec((None, LANES, TILE_STATES), lambda t: (t, 0, 0))
    c_spec = pl.BlockSpec((None, TILE_STATES, LANES), lambda t: (t, 0, 0))
    (y, yg), couts = _hosted_call(
        body, comm, out_shape=[jax.ShapeDtypeStruct((lp, w), F32), jax.ShapeDtypeStruct((lp, w), BF16)], grid=(nt,),
        in_specs=[blk, lam_spec, lam_spec, b_spec, b_spec, c_spec, c_spec,
                  pl.BlockSpec((1, LANES), lambda t: (0, t // 2))],
        out_specs=[blk, blk],
        scratch_shapes=[pltpu.VMEM((njt, lp, LANES), F32), pltpu.VMEM((njt, lp, LANES), F32)],
        sem=("arbitrary",), name="ssm_fwd",
        args=(u, lam_re, lam_im, tb_re, tb_im, tc_re, tc_im, d_skip.reshape(1, w)))
    return (y, yg), couts


def _ssm_bwd(u, dy, lam_re, lam_im, tb_re, tb_im, tc_re, tc_im, d_skip, comm=None):
    lp, w = u.shape
    nt = tb_re.shape[0]
    nchain = _scan_chains(lp)
    seg = lp // (SUBLANES * nchain)
    chunk = lp // SUBLANES
    njt = TILE_STATES // LANES
    tbt_re, tbt_im = jnp.swapaxes(tb_re, 1, 2), jnp.swapaxes(tb_im, 1, 2)
    tct_re, tct_im = jnp.swapaxes(tc_re, 1, 2), jnp.swapaxes(tc_im, 1, 2)

    def body(u_ref, dy_ref, lr_ref, li_ref, br_ref, bi_ref, btr_ref, bti_ref, ctr_ref, cti_ref, d_ref,
             du_ref, dlr_ref, dli_ref, dbr_ref, dbi_ref, dcr_ref, dci_ref, dd_ref, hr, hi, ar, ai):
        t = pl.program_id(0)
        lrs, lis = _lam_tiles(lr_ref), _lam_tiles(li_ref)
        for s in range(SUBLANES):
            rs = pl.ds(s * chunk, chunk)
            ub = u_ref[rs, :].astype(BF16)
            dyb = dy_ref[rs, :].astype(BF16)
            _split_tiles(hr, rs, _dot(ub, br_ref[...], "nn"))
            _split_tiles(hi, rs, _dot(ub, bi_ref[...], "nn"))
            _split_tiles(ar, rs, _dot(dyb, ctr_ref[...], "nn"))
            _split_tiles(ai, rs, -_dot(dyb, cti_ref[...], "nn"))
        _ssm_scan(hr, hi, list(zip(lrs, lis)), seg, nchain, False)

        def dlam_step(tt, a, j, a_r, a_i, acc):
            sl = _chain_rows(a, jnp.maximum(tt - 1, 0), seg)
            p_r, p_i = hr[j, sl, :], hi[j, sl, :]
            acc = list(acc)
            acc[2 * j] = acc[2 * j] + jnp.where(tt > 0, a_r * p_r + a_i * p_i, 0.0)
            acc[2 * j + 1] = acc[2 * j + 1] + jnp.where(tt > 0, a_i * p_r - a_r * p_i, 0.0)
            return tuple(acc)

        zero = tuple([jnp.zeros((SUBLANES, LANES), F32)] * (2 * njt))
        conj = [(lr, -li) for lr, li in zip(lrs, lis)]
        acc = list(_ssm_scan(ar, ai, conj, seg, nchain, True, (dlam_step, zero)))
        row0 = lax.broadcasted_iota(jnp.int32, (SUBLANES, LANES), 0) == 0
        for j in range(njt):
            cs = slice(j * LANES, (j + 1) * LANES)
            for a in range(nchain):
                p_r = _shift_rows(hr[j, _chain_rows(a, seg - 1, seg), :], 1, False)
                p_i = _shift_rows(hi[j, _chain_rows(a, seg - 1, seg), :], 1, False)
                if a > 0:
                    before = pl.ds(a * SUBLANES * seg - 1, 1)
                    p_r = jnp.where(row0, jnp.broadcast_to(hr[j, before, :], p_r.shape), p_r)
                    p_i = jnp.where(row0, jnp.broadcast_to(hi[j, before, :], p_i.shape), p_i)
                a_r, a_i = ar[j, _chain_rows(a, 0, seg), :], ai[j, _chain_rows(a, 0, seg), :]
                acc[2 * j] = acc[2 * j] + a_r * p_r + a_i * p_i
                acc[2 * j + 1] = acc[2 * j + 1] + a_i * p_r - a_r * p_i
            dlr_ref[:, cs] = jnp.sum(acc[2 * j], axis=0, keepdims=True)
            dli_ref[:, cs] = jnp.sum(acc[2 * j + 1], axis=0, keepdims=True)

        dd = jnp.zeros((1, LANES), F32)
        for s in range(SUBLANES):
            rs = pl.ds(s * chunk, chunk)
            ub = u_ref[rs, :].astype(BF16)
            dyv = dy_ref[rs, :]
            dyb = dyv.astype(BF16)
            arb, aib = _cat_tiles(ar, rs), _cat_tiles(ai, rs)
            hrb, hib = _cat_tiles(hr, rs), _cat_tiles(hi, rs)
            du = _dot(arb, btr_ref[...], "nn") + _dot(aib, bti_ref[...], "nn")
            upd = [(dbr_ref, _dot(arb, ub, "tn")), (dbi_ref, _dot(aib, ub, "tn")),
                   (dcr_ref, _dot(dyb, hrb, "tn")), (dci_ref, -_dot(dyb, hib, "tn"))]
            for ref, val in upd:
                if s == 0:
                    ref[...] = val
                else:
                    ref[...] += val
            rows = lax.broadcasted_iota(jnp.int32, (chunk, LANES), 0) + s * chunk
            keep = rows >= PAD_FRONT
            dd = dd + jnp.sum(dyv * u_ref[rs, :], axis=0, keepdims=True)

            @pl.when(t % 2 == 0)
            def _():
                du_ref[rs, :] = jnp.where(keep, du + d_ref[...] * dyv, 0.0)

            @pl.when(t % 2 == 1)
            def _():
                du_ref[rs, :] += jnp.where(keep, du, 0.0)

        @pl.when(t % 2 == 0)
        def _():
            dd_ref[...] = dd

    blk = pl.BlockSpec((lp, LANES), lambda t: (0, t // 2))
    vec = pl.BlockSpec((1, LANES), lambda t: (0, t // 2))
    lam_spec = pl.BlockSpec((None, 1, TILE_STATES), lambda t: (t, 0, 0))
    b_spec = pl.BlockSpec((None, LANES, TILE_STATES), lambda t: (t, 0, 0))
    c_spec = pl.BlockSpec((None, TILE_STATES, LANES), lambda t: (t, 0, 0))
    lam_shape = jax.ShapeDtypeStruct((nt, 1, TILE_STATES), F32)
    bt_shape = jax.ShapeDtypeStruct((nt, TILE_STATES, LANES), F32)
    ct_shape = jax.ShapeDtypeStruct((nt, LANES, TILE_STATES), F32)
    st = pltpu.VMEM((njt, lp, LANES), F32)
    return _hosted_call(
        body, comm,
        out_shape=[jax.ShapeDtypeStruct((lp, w), F32), lam_shape, lam_shape, bt_shape, bt_shape, ct_shape, ct_shape,
                   jax.ShapeDtypeStruct((1, w), F32)],
        grid=(nt,),
        in_specs=[blk, blk, lam_spec, lam_spec, b_spec, b_spec, c_spec, c_spec, b_spec, b_spec, vec],
        out_specs=[blk, lam_spec, lam_spec, c_spec, c_spec, b_spec, b_spec, vec],
        scratch_shapes=[st, st, st, st], sem=("arbitrary",), name="ssm_bwd",
        args=(u, dy, lam_re, lam_im, tb_re, tb_im, tbt_re, tbt_im, tct_re, tct_im, d_skip.reshape(1, w)))


def _ssm_params(a_re, a_im, log_dt, b_re, b_im):
    dt = jnp.exp(log_dt)[:, None]
    mag = jnp.exp(a_re * dt)
    lb_re = mag * jnp.cos(a_im * dt)
    lb_im = mag * jnp.sin(a_im * dt)
    den = a_re * a_re + a_im * a_im
    num_re = lb_re - 1.0
    coef_re = (num_re * a_re + lb_im * a_im) / den
    coef_im = (lb_im * a_re - num_re * a_im) / den
    bb_re = coef_re[..., None] * b_re - coef_im[..., None] * b_im
    bb_im = coef_re[..., None] * b_im + coef_im[..., None] * b_re
    return lb_re, lb_im, bb_re, bb_im


def _merge_fwd(o, yg, ga, gs, w3t, comm=None):
    lp, d = ga.shape
    kw = w3t.shape[2]
    tm = _row_tile(lp)

    def epi(accs, in_refs, out_refs):
        attn, vv, gg = accs
        out_refs[0][...] = (_sigmoid(in_refs[5][...]) * attn
                            + _sigmoid(in_refs[6][...]) * (vv * _sigmoid(gg))).astype(BF16)

    wspec = lambda which: pl.BlockSpec((None, d, kw), lambda i: (which, 0, 0))
    rowspec = pl.BlockSpec((tm, d), lambda i: (i, 0))
    aspec = pl.BlockSpec((tm, kw), lambda i: (i, 0))
    res = _matmul(
        "merge_fwd", (lp // tm,), None, [o, yg, w3t, w3t, w3t, ga, gs],
        [aspec, aspec, wspec(0), wspec(1), wspec(2), rowspec, rowspec],
        [(0, 2, "nt", 0), (1, 3, "nt", 1), (1, 4, "nt", 2)], [(tm, d)] * 3, epi,
        [jax.ShapeDtypeStruct((lp, d), BF16)], [rowspec], ("parallel",), comm)
    return (res[0], []) if comm is None else (res[0][0], res[1])


def _out_proj(merged, w_out, h, next_gain, comm=None):
    lp, d = h.shape
    tm = _row_tile(lp)
    shapes, specs, extra_in, extra_specs = _residual_specs(lp, d, tm, next_gain)

    def epi(accs, in_refs, out_refs):
        _residual_outputs(in_refs[2][...] + accs[0], in_refs, out_refs, 3 if extra_in else None)

    rowspec = pl.BlockSpec((tm, d), lambda i: (i, 0))
    res = _matmul(
        "mix_out_proj", (lp // tm,), None, [merged, w_out, h] + extra_in,
        [rowspec, pl.BlockSpec((d, d), lambda i: (0, 0)), rowspec] + extra_specs,
        [(0, 1, "nn", 0)], [(tm, d)], epi, shapes, specs, ("parallel",), comm)
    return (res, []) if comm is None else res


def _merge_bwd(dhb, w_out, o, yg, ga, gs, w3t):
    lp, d = ga.shape
    kw = w3t.shape[2]
    tm = _row_tile(lp)

    def epi(accs, in_refs, out_refs):
        dm, attn, vv, gg = accs
        sa = _sigmoid(in_refs[7][...])
        ss = _sigmoid(in_refs[8][...])
        sg = _sigmoid(gg)
        ssm = vv * sg
        dssm = dm * ss
        out_refs[0][...] = (dm * sa).astype(BF16)
        out_refs[1][...] = (dssm * sg).astype(BF16)
        out_refs[2][...] = (dssm * vv * sg * (1.0 - sg)).astype(BF16)
        out_refs[3][...] = (dm * attn * sa * (1.0 - sa)).astype(BF16)
        out_refs[4][...] = (dm * ssm * ss * (1.0 - ss)).astype(BF16)

    wspec = lambda which: pl.BlockSpec((None, d, kw), lambda i: (which, 0, 0))
    rowspec = pl.BlockSpec((tm, d), lambda i: (i, 0))
    aspec = pl.BlockSpec((tm, kw), lambda i: (i, 0))
    shp = jax.ShapeDtypeStruct((lp, d), BF16)
    return _matmul(
        "merge_bwd", (lp // tm,), None, [dhb, w_out, o, yg, w3t, w3t, w3t, ga, gs],
        [rowspec, pl.BlockSpec((d, d), lambda i: (0, 0)), aspec, aspec, wspec(0), wspec(1), wspec(2), rowspec,
         rowspec],
        [(0, 1, "nt", 0), (2, 4, "nt", 1), (3, 5, "nt", 2), (3, 6, "nt", 3)], [(tm, d)] * 4, epi,
        [shp] * 5, [rowspec] * 5, ("parallel",))


def _branch_bwd(dattn, dv, dg, w3t, y):
    lp, d = dattn.shape
    kw = w3t.shape[2]
    tm = _row_tile(lp)

    def epi(accs, in_refs, out_refs):
        out_refs[0][...] = accs[0].astype(BF16)
        out_refs[1][...] = accs[1] * _gelu_grad(in_refs[6][...])

    wspec = lambda which: pl.BlockSpec((None, d, kw), lambda i: (which, 0, 0))
    rowspec = pl.BlockSpec((tm, d), lambda i: (i, 0))
    aspec = pl.BlockSpec((tm, kw), lambda i: (i, 0))
    return _matmul(
        "branch_bwd", (lp // tm,), None, [dattn, dv, dg, w3t, w3t, w3t, y],
        [rowspec, rowspec, rowspec, wspec(0), wspec(1), wspec(2), aspec],
        [(0, 3, "nn", 0), (1, 4, "nn", 1), (2, 5, "nn", 1)], [(tm, kw)] * 2, epi,
        [jax.ShapeDtypeStruct((lp, kw), BF16), jax.ShapeDtypeStruct((lp, kw), F32)], [aspec, aspec],
        ("parallel",))


def _sibling_half(acc, rows):
    half = rows // 2
    return jnp.where(lax.axis_index("c") == 0, acc[half:rows], acc[:half]).astype(BF16)


def _tn_cols(x, ys, name):
    lp, kx = x.shape
    n = ys[0].shape[1]
    n4 = n // N_CHIPS
    tk = _tn_tiles(lp)
    ny = len(ys)

    def epi(accs, in_refs, out_refs):
        for i, acc in enumerate(accs):
            out_refs[i][...] = acc
            out_refs[ny + i][...] = _sibling_half(acc, kx)

    shp = jax.ShapeDtypeStruct((N_CHIPS, kx, n4), F32)
    shp_half = jax.ShapeDtypeStruct((N_CHIPS, kx // 2, n4), BF16)
    res = _matmul(
        name, (N_CHIPS, lp // tk), 1, [x] + list(ys),
        [pl.BlockSpec((tk, kx), lambda j, k: (k, 0))] + [pl.BlockSpec((tk, n4), lambda j, k: (k, j))] * ny,
        [(0, 1 + i, "tn", i) for i in range(ny)], [(kx, n4)] * ny, epi,
        [shp] * ny + [shp_half] * ny,
        [pl.BlockSpec((None, kx, n4), lambda j, k: (j, 0, 0))] * ny
        + [pl.BlockSpec((None, kx // 2, n4), lambda j, k: (j, 0, 0))] * ny,
        ("arbitrary", "arbitrary"))
    return res[:ny], res[ny:]


def _tn_full(x, y, name, tn_cols=None):
    lp, kx = x.shape
    n = y.shape[1]
    tk = _tn_tiles(lp)
    tn = n if tn_cols is None else tn_cols
    k4 = kx // N_CHIPS

    def epi(accs, in_refs, out_refs):
        for j in range(N_CHIPS):
            slab = accs[0][j * k4:(j + 1) * k4]
            out_refs[0][j] = slab
            out_refs[1][j] = _sibling_half(slab, k4)

    return _matmul(
        name, (n // tn, lp // tk), 1, [x, y],
        [pl.BlockSpec((tk, kx), lambda j, k: (k, 0)), pl.BlockSpec((tk, tn), lambda j, k: (k, j))],
        [(0, 1, "tn", 0)], [(kx, tn)], epi,
        [jax.ShapeDtypeStruct((N_CHIPS, k4, n), F32), jax.ShapeDtypeStruct((N_CHIPS, k4 // 2, n), BF16)],
        [pl.BlockSpec((N_CHIPS, k4, tn), lambda j, k: (0, 0, j)),
         pl.BlockSpec((N_CHIPS, k4 // 2, tn), lambda j, k: (0, 0, j))],
        ("arbitrary", "arbitrary"))


def _in_proj_bwd(dz, w_in, dh, h_in, gain):
    lp, d = h_in.shape
    inw = w_in.shape[0]
    tm = _row_tile(lp)

    def epi(accs, in_refs, out_refs):
        i = pl.program_id(0)
        dx, dgrow = _rms_bwd_math(accs[0], in_refs[3][...], in_refs[4][...])
        dh_new = in_refs[2][...] + dx
        out_refs[0][...] = dh_new
        out_refs[2][...] = dh_new.astype(BF16)

        @pl.when(i == 0)
        def _():
            out_refs[1][...] = jnp.zeros_like(out_refs[1])

        out_refs[1][...] += jnp.sum(dgrow, axis=0, keepdims=True)

    row = pl.BlockSpec((tm, d), lambda i: (i, 0))
    return _matmul(
        "mix_in_proj_bwd", (lp // tm,), None, [dz, w_in, dh, h_in, gain.reshape(1, d)],
        [pl.BlockSpec((tm, inw), lambda i: (i, 0)), pl.BlockSpec((inw, d), lambda i: (0, 0)), row, row,
         pl.BlockSpec((1, d), lambda i: (0, 0))],
        [(0, 1, "nn", 0)], [(tm, d)], epi,
        [jax.ShapeDtypeStruct((lp, d), F32), jax.ShapeDtypeStruct((1, d), F32), jax.ShapeDtypeStruct((lp, d), BF16)],
        [row, pl.BlockSpec((1, d), lambda i: (0, 0)), row], ("arbitrary",))


def _loss_head(h, gain, target):
    lp, d = h.shape
    nb = lp // BLOCK

    def body(h_ref, g_ref, t_ref, dh_ref, dg_ref, loss_ref, dhb_ref):
        i = pl.program_id(0)

        @pl.when(i == 0)
        def _():
            dg_ref[...] = jnp.zeros_like(dg_ref)
            loss_ref[...] = jnp.zeros_like(loss_ref)
            dh_ref[...] = jnp.zeros_like(dh_ref)
            dhb_ref[...] = jnp.zeros_like(dhb_ref)

        @pl.when(i > 0)
        def _():
            x = h_ref[...]
            g = g_ref[...]
            r = lax.rsqrt(jnp.mean(x * x, axis=-1, keepdims=True) + EPS)
            err = x * r * g - t_ref[...]
            loss_ref[...] += jnp.zeros_like(loss_ref) + 0.5 * jnp.sum(jnp.sum(err * err, axis=-1, keepdims=True)) / d
            dx, dgrow = _rms_bwd_math(err * (1.0 / d), x, g)
            dh_ref[...] = dx
            dhb_ref[...] = dx.astype(BF16)
            dg_ref[...] += jnp.sum(dgrow, axis=0, keepdims=True)

    row = pl.BlockSpec((BLOCK, d), lambda i: (i, 0))
    one = pl.BlockSpec((1, d), lambda i: (0, 0))
    return pl.pallas_call(
        body,
        out_shape=[jax.ShapeDtypeStruct((lp, d), F32), jax.ShapeDtypeStruct((1, d), F32),
                   jax.ShapeDtypeStruct((SUBLANES, LANES), F32), jax.ShapeDtypeStruct((lp, d), BF16)],
        grid=(nb,),
        in_specs=[row, one, pl.BlockSpec((BLOCK, d), lambda i: (jnp.maximum(i - 1, 0), 0))],
        out_specs=[row, one, pl.BlockSpec((SUBLANES, LANES), lambda i: (0, 0)), row],
        compiler_params=_cparams(("arbitrary",)), name="loss_head")(h, gain.reshape(1, d), target)


def _adam_math(w, g, m, v):
    m = ADAM_B1 * m + (1.0 - ADAM_B1) * g
    v = ADAM_B2 * v + (1.0 - ADAM_B2) * (g * g)
    m_hat = m / (1.0 - ADAM_B1 ** ADAM_STEP)
    v_hat = v / (1.0 - ADAM_B2 ** ADAM_STEP)
    delta = -ADAM_LR * (m_hat / (jnp.sqrt(v_hat) + ADAM_EPS) + ADAM_WD * w)
    return delta, m, v


def _adamw_layers(w, m, v, mine, other, pos, name):
    depth, r, c = w.shape
    half = r // 2
    tr = _div_tile(half, c * 4)
    nh = half // tr

    def body(*refs):
        pos_ref, w_ref, m_ref, v_ref = refs[:4]
        mine_refs = refs[4:4 + depth]
        other_refs = refs[4 + depth:4 + 2 * depth]
        g_out, d_out, m_out, v_out = refs[4 + 2 * depth:]
        layer, i = pl.program_id(0), pl.program_id(1)
        is_mine = (i // nh) == pos_ref[0]

        def update(g):
            delta, nm, nv = _adam_math(w_ref[...], g, m_ref[...], v_ref[...])
            g_out[...] = g
            d_out[...] = delta
            m_out[...] = nm
            v_out[...] = nv

        for l in range(depth):
            @pl.when((layer == l) & is_mine)
            def _(l=l):
                update(mine_refs[l][...])

            @pl.when((layer == l) & jnp.logical_not(is_mine))
            def _(l=l):
                update(other_refs[l][...])

    stacked = pl.BlockSpec((None, tr, c), lambda l, i, p: (l, i, 0))

    def gspec(layer, is_other):
        def imap(l, i, p):
            first = jnp.where(is_other, 1 - p[0], p[0]) * nh
            here = jnp.clip(i - first, 0, nh - 1)
            return (jnp.where(l == layer, here, jnp.where(l < layer, 0, nh - 1)), 0)
        return pl.BlockSpec((tr, c), imap)

    shp = jax.ShapeDtypeStruct((depth, r, c), F32)
    grid_spec = pltpu.PrefetchScalarGridSpec(
        num_scalar_prefetch=1, grid=(depth, 2 * nh),
        in_specs=[stacked] * 3 + [gspec(l, 0) for l in range(depth)] + [gspec(l, 1) for l in range(depth)],
        out_specs=[stacked] * 4)
    return pl.pallas_call(
        body, out_shape=[shp] * 4, grid_spec=grid_spec,
        compiler_params=_cparams(("arbitrary", "arbitrary")), name=name)(pos, w, m, v, *mine, *other)


def _adamw_whole(w, g, m, v, name):
    def body(w_ref, g_ref, m_ref, v_ref, d_out, m_out, v_out):
        delta, nm, nv = _adam_math(w_ref[...], g_ref[...], m_ref[...], v_ref[...])
        d_out[...] = delta
        m_out[...] = nm
        v_out[...] = nv

    shp = jax.ShapeDtypeStruct(w.shape, F32)
    return pl.pallas_call(body, out_shape=[shp] * 3, compiler_params=_cparams(), name=name)(w, g, m, v)


def _mesh_pos():
    return lax.axis_index("x"), lax.axis_index("y"), lax.axis_index("c")


def _row_half(ref, which, lead):
    half = ref.shape[lead] // 2
    idx = (slice(None),) * lead + (pl.ds(which * half, half), slice(None))
    return ref.at[idx]


def _gather_comm(arrs, tag):
    n = len(arrs)

    def ctx(ins, outs, sems):
        send_sems, recv_sems, local_sems = sems
        x, y, c = _mesh_pos()
        chips = [(1 - x, y), (x, 1 - y), (1 - x, 1 - y)]

        def slot(k, chip, which):
            lead = len(ins[k].shape) - 2
            return _row_half(outs[k].at[2 * chip[0] + chip[1]], which, lead)

        def copy(k, j, src, dst, to):
            return pltpu.make_async_remote_copy(
                src_ref=src, dst_ref=dst, send_sem=send_sems.at[6 * k + j], recv_sem=recv_sems.at[6 * k + j],
                device_id=to, device_id_type=MESH)

        def local(k):
            return pltpu.make_async_copy(ins[k], outs[k].at[2 * x + y], local_sems.at[k])

        def first(k, j):
            lead = len(ins[k].shape) - 2
            return copy(k, j, _row_half(ins[k], c, lead), slot(k, (x, y), c), (*chips[j], c))

        def passed(k, j, which):
            return copy(k, 3 + j, slot(k, chips[j], which), slot(k, chips[j], which), (x, y, 1 - c))

        def landed(k, j):
            return copy(k, j, slot(k, chips[j], c), slot(k, chips[j], c), (x, y, 1 - c))

        return c, local, first, passed, landed

    def start(ins, outs, sems):
        c, local, first, passed, landed = ctx(ins, outs, sems)
        for k in range(n):
            local(k).start()
            for j in range(3):
                first(k, j).start()

    def mid(ins, outs, sems):
        c, local, first, passed, landed = ctx(ins, outs, sems)
        for j in range(3):
            for k in range(n):
                landed(k, j).wait_recv()
                passed(k, j, c).start()

    def finish(ins, outs, sems):
        c, local, first, passed, landed = ctx(ins, outs, sems)
        for j in range(3):
            for k in range(n):
                passed(k, j, 1 - c).wait_recv()
        for k in range(n):
            for j in range(3):
                first(k, j).wait_send()
                passed(k, j, c).wait_send()
            local(k).wait()

    return _Comm(
        tag, arrs, [jax.ShapeDtypeStruct((N_CHIPS,) + a.shape, a.dtype) for a in arrs],
        [pltpu.SemaphoreType.DMA((6 * n,)), pltpu.SemaphoreType.DMA((6 * n,)), pltpu.SemaphoreType.DMA((n,))],
        start, mid, finish)


def _run_comm(comm, name):
    n_in, n_out = len(comm.ins), len(comm.out_shapes)

    def body(*refs):
        ins, outs, sems = refs[:n_in], refs[n_in:n_in + n_out], refs[n_in + n_out:]
        comm.start(ins, outs, sems)
        if comm.mid is not None:
            comm.mid(ins, outs, sems)
        comm.finish(ins, outs, sems)

    return pl.pallas_call(
        body, out_shape=comm.out_shapes, in_specs=[HBM_SPEC] * n_in, out_specs=[HBM_SPEC] * n_out,
        scratch_shapes=comm.sems, name=name)(*comm.ins)


def _all_gather_chips(arrs, name):
    return _run_comm(_gather_comm(arrs, "gather"), name)


GATHER_US_PER_BYTE = 380.0 / 11.65e6
HOST_US = dict(ffn_up=68.0, ffn_down=37.0, in_proj=38.0, attn_fwd=103.0, ssm_fwd=70.0, merge_fwd=30.0,
               out_proj=23.0)
HOST_SLACK_US = 10.0
OVERSHOOT = 1.3


class _WeightStream:
    def __init__(self, pieces):
        self.keys = [k for k, _ in pieces]
        self.shards = dict(pieces)
        self.next = 0
        self.full = {}
        self.pending = []

    def comm_for(self, host):
        budget = HOST_US[host] + HOST_SLACK_US
        taken, cost = [], 0.0
        while self.next < len(self.keys):
            key = self.keys[self.next]
            c = self.shards[key].size * self.shards[key].dtype.itemsize * GATHER_US_PER_BYTE
            if cost + c > budget and (taken or c > OVERSHOOT * budget):
                break
            taken.append(key)
            cost += c
            self.next += 1
        self.pending = taken
        if not taken:
            return None
        return _gather_comm([self.shards[k] for k in taken], "g_" + "_".join(k[1] for k in taken))

    def deposit(self, gathered):
        for key, arr in zip(self.pending, gathered):
            self.full[key] = arr
        self.pending = []

    def get(self, key):
        if key not in self.full:
            upto = self.keys.index(key) + 1
            keys = self.keys[self.next:upto]
            self.next = upto
            for k, arr in zip(keys, _all_gather_chips([self.shards[k] for k in keys], "gather_now")):
                self.full[k] = arr
        return self.full[key]


def _all_gather_devices(x_shard, name):
    m_per, ncol = x_shard.shape

    def body(x_ref, out_ref, send_sems, recv_sems, local_sem):
        x, y, c = _mesh_pos()
        me, sibling = (x, y, c), (x, y, 1 - c)
        chips = [(1 - x, y), (x, 1 - y), (1 - x, 1 - y)]

        def rows(px, py, pc):
            return out_ref.at[4 * px + 2 * py + pc]

        def copy(k, block, to, src=None):
            return pltpu.make_async_remote_copy(
                src_ref=rows(*block) if src is None else src, dst_ref=rows(*block),
                send_sem=send_sems.at[k], recv_sem=recv_sems.at[k], device_id=to, device_id_type=MESH)

        mine = pltpu.make_async_copy(x_ref, rows(*me), local_sem)
        mine.start()
        first = [copy(0, me, sibling, src=x_ref)]
        first += [copy(1 + j, me, (*chip, c), src=x_ref) for j, chip in enumerate(chips)]
        for cp in first:
            cp.start()
        passed = [copy(4 + j, (*chip, c), sibling) for j, chip in enumerate(chips)]
        for j, chip in enumerate(chips):
            copy(1 + j, (*chip, c), me).wait_recv()
            passed[j].start()
        copy(0, sibling, me).wait_recv()
        for j, chip in enumerate(chips):
            copy(4 + j, (*chip, 1 - c), me).wait_recv()
        for cp in first + passed:
            cp.wait_send()
        mine.wait()

    return pl.pallas_call(
        body, out_shape=jax.ShapeDtypeStruct((8, m_per, ncol), x_shard.dtype),
        in_specs=[pl.BlockSpec(memory_space=pltpu.VMEM)], out_specs=pl.BlockSpec(memory_space=pltpu.VMEM),
        scratch_shapes=[pltpu.SemaphoreType.DMA((7,)), pltpu.SemaphoreType.DMA((7,)), pltpu.SemaphoreType.DMA],
        compiler_params=pltpu.CompilerParams(vmem_limit_bytes=VMEM_LIMIT), name=name)(x_shard)


def _device_gather_comm(x_shard, tag):
    def ctx(ins, outs, sems):
        (x_ref,), (out_ref,) = ins, outs
        send_sems, recv_sems, local_sems = sems
        x, y, c = _mesh_pos()
        me, sibling = (x, y, c), (x, y, 1 - c)
        chips = [(1 - x, y), (x, 1 - y), (1 - x, 1 - y)]

        def rows(px, py, pc):
            return out_ref.at[4 * px + 2 * py + pc]

        def copy(k, block, to, src=None):
            return pltpu.make_async_remote_copy(
                src_ref=rows(*block) if src is None else src, dst_ref=rows(*block),
                send_sem=send_sems.at[k], recv_sem=recv_sems.at[k], device_id=to, device_id_type=MESH)

        mine = pltpu.make_async_copy(x_ref, rows(*me), local_sems.at[0])
        first = [copy(0, me, sibling, src=x_ref)] + [copy(1 + j, me, (*chip, c), src=x_ref)
                                                     for j, chip in enumerate(chips)]
        passed = [copy(4 + j, (*chip, c), sibling) for j, chip in enumerate(chips)]
        landed = [copy(1 + j, (*chip, c), me) for j, chip in enumerate(chips)]
        last = [copy(0, sibling, me)] + [copy(4 + j, (*chip, 1 - c), me) for j, chip in enumerate(chips)]
        return mine, first, passed, landed, last

    def start(ins, outs, sems):
        mine, first, _, _, _ = ctx(ins, outs, sems)
        mine.start()
        for cp in first:
            cp.start()

    def mid(ins, outs, sems):
        _, _, passed, landed, _ = ctx(ins, outs, sems)
        for cp, fwd in zip(landed, passed):
            cp.wait_recv()
            fwd.start()

    def finish(ins, outs, sems):
        mine, first, passed, _, last = ctx(ins, outs, sems)
        for cp in last:
            cp.wait_recv()
        for cp in first + passed:
            cp.wait_send()
        mine.wait()

    return _Comm(
        tag, [x_shard], [jax.ShapeDtypeStruct((8,) + x_shard.shape, x_shard.dtype)],
        [pltpu.SemaphoreType.DMA((7,)), pltpu.SemaphoreType.DMA((7,)), pltpu.SemaphoreType.DMA((1,))],
        start, mid, finish)


def _sum_devices(g8, name):
    _, r, c = g8.shape
    tr = _div_tile(r, c * 4 * 8)

    def body(g_ref, o_ref):
        acc = g_ref[0]
        for dev in range(1, 8):
            acc = acc + g_ref[dev]
        o_ref[...] = acc

    return pl.pallas_call(
        body, out_shape=jax.ShapeDtypeStruct((r, c), F32), grid=(r // tr,),
        in_specs=[pl.BlockSpec((8, tr, c), lambda i: (0, i, 0))], out_specs=pl.BlockSpec((tr, c), lambda i: (i, 0)),
        compiler_params=_cparams(("parallel",)), name=name)(g8)


def _chip_partials(arrs, recvs, pos, name):
    n = len(arrs)

    def body(pos_ref, *refs):
        for a_ref, b_ref, o_ref in zip(refs[:n], refs[n:2 * n], refs[2 * n:]):
            o_ref[...] = (a_ref[...] + b_ref[...]).astype(BF16)

    own_specs, recv_specs, shapes = [], [], []
    for arr in arrs:
        nslab, r, c = arr.shape
        own_specs.append(pl.BlockSpec((None, r // 2, c), lambda j, p: (j, p[0], 0)))
        recv_specs.append(pl.BlockSpec((None, r // 2, c), lambda j, p: (j, 0, 0)))
        shapes.append(jax.ShapeDtypeStruct((nslab, r // 2, c), BF16))
    grid_spec = pltpu.PrefetchScalarGridSpec(
        num_scalar_prefetch=1, grid=(N_CHIPS,), in_specs=own_specs + recv_specs, out_specs=recv_specs)
    return pl.pallas_call(
        body, out_shape=shapes, grid_spec=grid_spec,
        compiler_params=_cparams(("parallel",)), name=name)(pos, *arrs, *recvs)


def _chip_exchange_comm(parts, tag):
    n = len(parts)

    def copies(ins, outs, sems):
        send_sems, recv_sems = sems
        x, y, c = _mesh_pos()
        chips = [(1 - x, y), (x, 1 - y), (1 - x, 1 - y)]
        return [pltpu.make_async_remote_copy(
            src_ref=ins[k].at[2 * chip[0] + chip[1]], dst_ref=outs[k].at[j],
            send_sem=send_sems.at[3 * k + j], recv_sem=recv_sems.at[3 * k + j],
            device_id=(*chip, c), device_id_type=MESH) for k in range(n) for j, chip in enumerate(chips)]

    def start(ins, outs, sems):
        for cp in copies(ins, outs, sems):
            cp.start()

    def finish(ins, outs, sems):
        for cp in copies(ins, outs, sems):
            cp.wait()

    return _Comm(
        tag, parts, [jax.ShapeDtypeStruct((3,) + p.shape[1:], p.dtype) for p in parts],
        [pltpu.SemaphoreType.DMA((3 * n,)), pltpu.SemaphoreType.DMA((3 * n,))], start, None, finish)


def _reduce_halves(arrs, recvs, gots, pos, name):
    n = len(arrs)

    def body(pos_ref, *refs):
        for a_ref, b_ref, g_ref, o_ref in zip(refs[:n], refs[n:2 * n], refs[2 * n:3 * n], refs[3 * n:]):
            acc = a_ref[...] + b_ref[...]
            for j in range(3):
                acc = acc + g_ref[j].astype(F32)
            o_ref[...] = acc

    own_specs, recv_specs, got_specs, out_specs, shapes = [], [], [], [], []
    for arr in arrs:
        _, r, c = arr.shape
        own_specs.append(pl.BlockSpec((None, r // 2, c), lambda i, p: (p[1], p[0], 0)))
        recv_specs.append(pl.BlockSpec((None, r // 2, c), lambda i, p: (p[1], 0, 0)))
        got_specs.append(pl.BlockSpec((3, r // 2, c), lambda i, p: (0, 0, 0)))
        out_specs.append(pl.BlockSpec((r // 2, c), lambda i, p: (0, 0)))
        shapes.append(jax.ShapeDtypeStruct((r // 2, c), F32))
    grid_spec = pltpu.PrefetchScalarGridSpec(
        num_scalar_prefetch=1, grid=(1,), in_specs=own_specs + recv_specs + got_specs, out_specs=out_specs)
    return pl.pallas_call(
        body, out_shape=shapes, grid_spec=grid_spec,
        compiler_params=_cparams(("arbitrary",)), name=name)(pos, *arrs, *recvs, *gots)


def _share_halves(halves, name):
    n = len(halves)

    def body(*refs):
        ins, outs = refs[:n], refs[n:2 * n]
        send_sems, recv_sems = refs[2 * n:]
        x, y, c = _mesh_pos()
        cps = []
        for k in range(n):
            cp = pltpu.make_async_remote_copy(
                src_ref=ins[k], dst_ref=outs[k], send_sem=send_sems.at[k], recv_sem=recv_sems.at[k],
                device_id=(x, y, 1 - c), device_id_type=MESH)
            cp.start()
            cps.append(cp)
        for cp in cps:
            cp.wait()

    return pl.pallas_call(
        body, out_shape=[jax.ShapeDtypeStruct(h.shape, h.dtype) for h in halves],
        in_specs=[HBM_SPEC] * n, out_specs=[HBM_SPEC] * n,
        scratch_shapes=[pltpu.SemaphoreType.DMA((n,)), pltpu.SemaphoreType.DMA((n,))], name=name)(*halves)


class _Reduction:
    def __init__(self, arrs, others, pos, tag):
        self.arrs, self.pos, self.tag = arrs, pos, tag
        self.recv = _share_halves(others, "rs_sibling_" + tag)
        self.parts = _chip_partials(arrs, self.recv, pos, "rs_partial_" + tag)
        self.got = None

    def comm(self):
        return _chip_exchange_comm(self.parts, "rs_" + self.tag)

    def end(self):
        if self.got is None:
            self.got = _run_comm(self.comm(), "rs_chips_" + self.tag)
        return _reduce_halves(self.arrs, self.recv, self.got, self.pos, "rs_reduce_" + self.tag)


def _w_in_full(p, l, ws):
    slabs = ws.get((l, "w_in"))
    return slabs.reshape(-1, slabs.shape[2])


def _w3t_full(p, l, ws):
    if "w3t" not in p:
        slabs = ws.get((l, "w3"))
        p["w3t"] = jnp.swapaxes(slabs, 0, 1).reshape(slabs.shape[1], -1, slabs.shape[3])
    return p["w3t"]


def _w_out_full(l, ws):
    slabs = ws.get((l, "w_out"))
    return slabs.reshape(-1, slabs.shape[2])


def _layer_fwd(h, n0, l, p, next_gain, ws, tabs):
    def hosted(host, fn, *args):
        out, got = fn(*args, ws.comm_for(host))
        ws.deposit(got)
        return out

    ffn1_saved = hosted("ffn_up", _ffn_up, n0, ws.get((l, "wg1")), ws.get((l, "wu1")))
    h1, n = hosted("ffn_down", _ffn_down, ffn1_saved[2], ws.get((l, "wd1")), h, p["mix_norm"])
    ssm_w = p["ssm_d"].shape[0]
    q, k, v, u, ga, gs = hosted("in_proj", _in_proj, n, _w_in_full(p, l, ws), tabs, ssm_w)
    o = hosted("attn_fwd", _attn_fwd, q, k, v, p["attn_sinks"])
    y, yg = hosted("ssm_fwd", _ssm_fwd, u, *p["ssm_tabs"], p["ssm_d"])
    merged = hosted("merge_fwd", _merge_fwd, o, yg, ga, gs, _w3t_full(p, l, ws))
    h2, n2 = hosted("out_proj", _out_proj, merged, _w_out_full(l, ws), h1, p["ffn2_norm"])
    ffn2_saved = hosted("ffn_up", _ffn_up, n2, ws.get((l, "wg2")), ws.get((l, "wu2")))
    h3, *n3 = hosted("ffn_down", _ffn_down, ffn2_saved[2], ws.get((l, "wd2")), h2, next_gain)
    saved = dict(h0=h, h1=h1, h2=h2, ffn1=ffn1_saved, ffn2=ffn2_saved, n_mix=n, q=q, k=k, v=v, u=u, ga=ga, gs=gs,
                 o=o, y=y, yg=yg, merged=merged)
    return h3, (n3[0] if n3 else None), saved


def _layer_bwd(dh_pair, l, p, ws, s, tabs, pos, early_comm=None):
    g = {}
    (dh2, dhb), g["ffn2_norm"], red_ffn2, _, _ = _ffn_bwd(
        dh_pair, s["h2"], p["ffn2_norm"], ws.get((l, "wg2")), ws.get((l, "wu2")), ws.get((l, "wd2")), p["f4"],
        s["ffn2"], pos)
    w3, w_out_w = _w3t_full(p, l, ws), _w_out_full(l, ws)
    lp, d = dh2.shape
    d4 = d // N_CHIPS
    dw_out, dw_out_other = _tn_full(s["merged"], dhb, "mix_dw_out")
    dattn, dv, dg, dga, dgs = _merge_bwd(dhb, w_out_w, s["o"], s["yg"], s["ga"], s["gs"], w3)
    (dw_ap,), (dw_ap_other,) = _tn_cols(s["o"], [dattn], "mix_dw_ap")
    (dw_gv, dw_gg), (dw_gv_other, dw_gg_other) = _tn_cols(s["yg"], [dv, dg], "mix_dw_glu")
    do, dy = _branch_bwd(dattn, dv, dg, w3, s["y"])
    (dq, dk, dvv, dkm, dvm, dsink), _ = _attn_bwd(s["q"], s["k"], s["v"], do, p["attn_sinks"], tabs)
    g["attn_sinks"] = dsink[:, 0]
    (du, dlr, dli, dbr, dbi, dcr, dci, dd), _ = _ssm_bwd(s["u"], dy, *p["ssm_tabs"], p["ssm_d"])
    ngrp = p["ssm_d"].shape[0] // SSM_GROUP
    g["ssm_lam"] = (dlr.reshape(ngrp, SSM_STATE), dli.reshape(ngrp, SSM_STATE),
                    _ssm_untable_b(dbr, ngrp), _ssm_untable_b(dbi, ngrp))
    g["ssm_c_re"] = _ssm_untable_c(dcr, ngrp)
    g["ssm_c_im"] = _ssm_untable_c(dci, ngrp)
    g["ssm_d"] = dd[0]
    dk = dk.at[:BLOCK].add(dkm)
    dvv = dvv.at[:BLOCK].add(dvm)
    dz = jnp.concatenate([dq.astype(BF16), dk.astype(BF16), dvv.astype(BF16), du.astype(BF16), dga, dgs], axis=1)
    n = s["n_mix"]
    w_in = _w_in_full(p, l, ws)
    dw_in, dw_in_other = _tn_full(dz, n, "mix_dw_in", d // 2)
    red_mix = _Reduction([dw_in, dw_ap, dw_gv, dw_gg, dw_out],
                         [dw_in_other, dw_ap_other, dw_gv_other, dw_gg_other, dw_out_other], pos, "mix")
    dh1, g["mix_norm"], dh1b = _in_proj_bwd(dz, w_in, dh2, s["h1"], p["mix_norm"])
    comm2 = None if early_comm is None else early_comm(g)
    dh0_pair, g["ffn1_norm"], red_ffn1, red_mix.got, early_got = _ffn_bwd(
        (dh1, dh1b), s["h0"], p["ffn1_norm"], ws.get((l, "wg1")), ws.get((l, "wu1")), ws.get((l, "wd1")), p["f4"],
        s["ffn1"], pos, red_mix.comm(), comm2)
    return dh0_pair, g, [*red_ffn1, red_mix, *red_ffn2], early_got


BIG = ["ffn1_w_gate", "ffn1_w_up", "ffn1_w_down", "w_in", "w_attn_proj", "w_glu_v", "w_glu_g", "w_out",
       "ffn2_w_gate", "ffn2_w_up", "ffn2_w_down"]
TRANSPOSED = ["ffn1_w_gate", "ffn1_w_up", "w_in", "ffn2_w_gate", "ffn2_w_up"]
SMALL = ["ffn1_norm", "mix_norm", "attn_sinks", "ssm_a_re", "ssm_a_im", "ssm_log_dt", "ssm_b_re", "ssm_b_im",
         "ssm_c_re", "ssm_c_im", "ssm_d", "ffn2_norm", "final_norm"]
WEIGHTS = ["meta_tokens", "ffn1_norm", "ffn1_w_gate", "ffn1_w_up", "ffn1_w_down", "mix_norm", "w_in", "attn_sinks",
           "ssm_a_re", "ssm_a_im", "ssm_log_dt", "ssm_b_re", "ssm_b_im", "ssm_c_re", "ssm_c_im", "ssm_d",
           "w_attn_proj", "w_glu_v", "w_glu_g", "w_out", "ffn2_norm", "ffn2_w_gate", "ffn2_w_up", "ffn2_w_down",
           "final_norm"]


def _small_rows(shape):
    rows = -(-math.prod(shape) // LANES)
    return -(-rows // SUBLANES) * SUBLANES


def _pack_small(tree, names):
    parts = []
    for k in names:
        size, rows = math.prod(tree[k].shape), _small_rows(tree[k].shape)
        if size % LANES == 0:
            part = tree[k].reshape(size // LANES, LANES)
        else:
            part = jnp.pad(tree[k].reshape(1, size), ((0, 0), (0, LANES - size)))
        parts.append(jnp.pad(part, ((0, rows - part.shape[0]), (0, 0))))
    total = sum(part.shape[0] for part in parts)
    if total > PACK_ROWS:
        parts.append(jnp.zeros((-total % PACK_ROWS, LANES), F32))
    return jnp.concatenate(parts, axis=0)


def _unpack_small(packed, like, names):
    out, off = {}, 0
    for k in names:
        size, rows = math.prod(like[k].shape), _small_rows(like[k].shape)
        if size % LANES == 0:
            out[k] = packed[off:off + size // LANES].reshape(like[k].shape)
        else:
            out[k] = packed[off, :size].reshape(like[k].shape)
        off += rows
    return out


def kernel(x, meta_tokens, ffn1_norm, ffn1_w_gate, ffn1_w_up, ffn1_w_down, mix_norm, w_in, attn_sinks, ssm_a_re, ssm_a_im, ssm_log_dt, ssm_b_re, ssm_b_im, ssm_c_re, ssm_c_im, ssm_d, w_attn_proj, w_glu_v, w_glu_g, w_out, ffn2_norm, ffn2_w_gate, ffn2_w_up, ffn2_w_down, final_norm, loss_target, m_meta_tokens, m_ffn1_norm, m_ffn1_w_gate, m_ffn1_w_up, m_ffn1_w_down, m_mix_norm, m_w_in, m_attn_sinks, m_ssm_a_re, m_ssm_a_im, m_ssm_log_dt, m_ssm_b_re, m_ssm_b_im, m_ssm_c_re, m_ssm_c_im, m_ssm_d, m_w_attn_proj, m_w_glu_v, m_w_glu_g, m_w_out, m_ffn2_norm, m_ffn2_w_gate, m_ffn2_w_up, m_ffn2_w_down, m_final_norm, v_meta_tokens, v_ffn1_norm, v_ffn1_w_gate, v_ffn1_w_up, v_ffn1_w_down, v_mix_norm, v_w_in, v_attn_sinks, v_ssm_a_re, v_ssm_a_im, v_ssm_log_dt, v_ssm_b_re, v_ssm_b_im, v_ssm_c_re, v_ssm_c_im, v_ssm_d, v_w_attn_proj, v_w_glu_v, v_w_glu_g, v_w_out, v_ffn2_norm, v_ffn2_w_gate, v_ffn2_w_up, v_ffn2_w_down, v_final_norm):
    args = dict(locals())
    w = {k: args[k] for k in WEIGHTS}
    m = {k: args["m_" + k] for k in WEIGHTS}
    v = {k: args["v_" + k] for k in WEIGHTS}
    depth = ffn1_norm.shape[0]
    seq, d = x.shape[1], x.shape[2]
    lp = seq + BLOCK
    xi, yi, ci = _mesh_pos()
    pos = jnp.stack([ci, 2 * xi + yi]).astype(jnp.int32)

    tabs = _rope_tables(lp)
    layers, pieces = [], [((0, "meta"), meta_tokens)]
    f4 = ffn1_w_gate.shape[2]
    fp = -(-f4 // MXU_DIM) * MXU_DIM

    def ffn_rows(wt):
        return jnp.pad(wt, ((0, fp - f4), (0, 0))).astype(BF16)

    for l in range(depth):
        small = [((l, "w3"), jnp.stack([w_attn_proj[l].T, w_glu_v[l].T, w_glu_g[l].T]).astype(BF16)),
                 ((l, "w_out"), w_out[l].astype(BF16))]
        first = [((l, "wg1"), ffn_rows(ffn1_w_gate[l].T)), ((l, "wu1"), ffn_rows(ffn1_w_up[l].T)),
                 ((l, "wd1"), ffn_rows(ffn1_w_down[l])), ((l, "w_in"), w_in[l].T.astype(BF16))]
        pieces += (first + small if l == 0 else small + first) + [
            ((l, "wg2"), ffn_rows(ffn2_w_gate[l].T)), ((l, "wu2"), ffn_rows(ffn2_w_up[l].T)),
            ((l, "wd2"), ffn_rows(ffn2_w_down[l]))]
        lb_re, lb_im, bb_re, bb_im = _ssm_params(ssm_a_re[l], ssm_a_im[l], ssm_log_dt[l], ssm_b_re[l], ssm_b_im[l])
        ngrp = lb_re.shape[0]
        nt = ngrp // GROUPS_PER_TILE
        ssm_tabs = (lb_re.reshape(nt, 1, TILE_STATES), lb_im.reshape(nt, 1, TILE_STATES),
                    *_ssm_tables(bb_re, bb_im, ssm_c_re[l], ssm_c_im[l]))
        layers.append(dict(
            ffn1_norm=ffn1_norm[l], mix_norm=mix_norm[l], ffn2_norm=ffn2_norm[l], attn_sinks=attn_sinks[l],
            ssm_d=ssm_d[l], ssm_tabs=ssm_tabs, f4=f4))
    ws = _WeightStream(pieces)
    ws.get((0, "wu1"))
    meta_all = ws.get((0, "meta"))
    meta_full = jnp.concatenate([meta_all[j] for j in range(N_CHIPS)], axis=1)

    h = jnp.concatenate([jnp.zeros((PAD_FRONT, d), F32), meta_full, x[0]], axis=0)
    saved = []
    n0 = _rms_fwd(h, ffn1_norm[0], "rms_fwd_first")
    for l in range(depth):
        next_gain = ffn1_norm[l + 1] if l + 1 < depth else None
        h, n0, s = _layer_fwd(h, n0, l, layers[l], next_gain, ws, tabs)
        saved.append(s)
    dh, g_final, loss_acc, dhb = _loss_head(h, final_norm, loss_target[0])
    dh_pair = (dh, dhb)
    loss = lax.psum(loss_acc[0, 0], ("x", "y", "c"))

    grads, reds = [None] * depth, [None] * depth

    def layer_small(gl, l):
        _, vjp = jax.vjp(_ssm_params, ssm_a_re[l], ssm_a_im[l], ssm_log_dt[l], ssm_b_re[l], ssm_b_im[l])
        da_re, da_im, dlog_dt, db_re, db_im = vjp(gl["ssm_lam"])
        first = gl["ffn1_norm"][0] if "ffn1_norm" in gl else jnp.zeros((d,), F32)
        return dict(ffn1_norm=first, mix_norm=gl["mix_norm"][0], attn_sinks=gl["attn_sinks"], ssm_a_re=da_re,
                    ssm_a_im=da_im, ssm_log_dt=dlog_dt, ssm_b_re=db_re, ssm_b_im=db_im, ssm_c_re=gl["ssm_c_re"],
                    ssm_c_im=gl["ssm_c_im"], ssm_d=gl["ssm_d"], ffn2_norm=gl["ffn2_norm"][0])

    class early:
        got, like = None, None

    def early_comm(g0):
        per = [layer_small(g0, 0)] + [layer_small(grads[l], l) for l in range(1, depth)]
        tree = {k: jnp.stack([lay[k] for lay in per]) for k in SMALL if k != "final_norm"}
        tree["final_norm"] = g_final[0]
        early.like = tree
        return _device_gather_comm(_pack_small(tree, SMALL), "small_grads")

    for l in reversed(range(depth)):
        dh_pair, grads[l], reds[l], got = _layer_bwd(
            dh_pair, l, layers[l], ws, saved[l], tabs, pos, early_comm if l == 0 else None)
        if l == 0:
            early.got = got
    dh = dh_pair[0]
    grad_x = dh[BLOCK:][None]
    dmeta_local = dh[PAD_FRONT:BLOCK]

    g_small_tree = _unpack_small(_sum_devices(early.got[0], "sum_small_grads"), early.like, SMALL)
    late_names = ["ffn1_norm", "meta_tokens"]
    late = dict(ffn1_norm=grads[0]["ffn1_norm"], meta_tokens=dmeta_local)
    g_late = _sum_devices(_all_gather_devices(_pack_small(late, late_names), "gather_late_grads"), "sum_late_grads")
    g_late = _unpack_small(g_late, late, late_names)
    g_small_tree["ffn1_norm"] = g_small_tree["ffn1_norm"].at[0].set(g_late["ffn1_norm"][0])
    d4 = d // N_CHIPS
    chip = 2 * xi + yi
    g_meta = lax.dynamic_slice_in_dim(g_late["meta_tokens"], chip * d4, d4, axis=1)

    mine = [[half for red in reds[l] for half in red.end()] for l in range(depth)]
    flat = _share_halves([half for layer_halves in mine for half in layer_halves], "rs_share")
    per_layer = len(mine[0])
    reduced = [(mine[l], flat[l * per_layer:(l + 1) * per_layer]) for l in range(depth)]

    g_out, delta, new_m, new_v = {}, {}, {}, {}
    for i, k in enumerate(BIG):
        flip = (lambda t: jnp.swapaxes(t, 1, 2)) if k in TRANSPOSED else (lambda t: t)
        outs = _adamw_layers(
            flip(w[k]), flip(m[k]), flip(v[k]), [reduced[l][0][i] for l in range(depth)],
            [reduced[l][1][i] for l in range(depth)], pos, "adamw_" + k)
        g_out[k], delta[k], new_m[k], new_v[k] = [flip(t) for t in outs]
    g_small_tree["meta_tokens"] = g_meta
    for k in SMALL + ["meta_tokens"]:
        narrow = w[k].ndim > 2 and w[k].shape[-1] < w[k].shape[-2]
        view = (lambda t: jnp.swapaxes(t, -1, -2)) if narrow else (lambda t: t)
        shape = view(w[k]).shape if w[k].ndim > 1 else (1,) + w[k].shape
        outs = _adamw_whole(view(w[k]).reshape(shape), view(g_small_tree[k]).reshape(shape),
                            view(m[k]).reshape(shape), view(v[k]).reshape(shape), "adamw_" + k)
        g_out[k] = g_small_tree[k]
        delta[k], new_m[k], new_v[k] = [view(t).reshape(w[k].shape) for t in outs]

    return (loss, grad_x, *[g_out[k] for k in WEIGHTS], *[delta[k] for k in WEIGHTS],
            *[new_m[k] for k in WEIGHTS], *[new_v[k] for k in WEIGHTS])
```
